```python
import jax
import jax.numpy as jnp
from jax import lax
import numpy as np

D_MODEL = 1024
BATCH = 8
SEQ = 4096
DEPTH = 1

MEM_LEN = 256
SSM_GROUP = 16
SSM_WIDTH = 768
SSM_GROUPS = SSM_WIDTH // SSM_GROUP
SSM_STATE = 64
SSM_DT_MIN = 0.001
SSM_DT_MAX = 0.1
ATT_HEAD_DIM = 64
ATT_HEADS_PER_GROUP = 4
DILATION_PATTERN = ((128, 1), (512, 4), (2048, 16))
ATT_GROUPS = len(DILATION_PATTERN)
ATT_HEADS = ATT_GROUPS * ATT_HEADS_PER_GROUP
ATT_WIDTH = ATT_HEADS * ATT_HEAD_DIM
ATT_MERGED = ATT_HEADS_PER_GROUP * ATT_HEAD_DIM
ATT_SCALE = ATT_HEAD_DIM ** -0.5
ROT_DIM = ATT_HEAD_DIM // 4
ROPE_THETA = 500000.0
XATT_HEADS = 4
XATT_HEAD_DIM = D_MODEL // XATT_HEADS
XATT_SCALE = XATT_HEAD_DIM ** -0.5
D_FF = 4 * D_MODEL
DEEPNORM_ALPHA = (2 * DEPTH) ** 0.25
DEEPNORM_BETA = (8 * DEPTH) ** -0.25
LN_EPS = 1e-5
NEG_INF = -1e30
OFF_U = 0
OFF_Q = OFF_U + SSM_WIDTH
OFF_K = OFF_Q + ATT_WIDTH
OFF_V = OFF_K + ATT_WIDTH
OFF_GS = OFF_V + ATT_WIDTH
OFF_GA = OFF_GS + D_MODEL
IN_COLS = OFF_GA + D_MODEL

kernel_name = 'hybrid_s5_dilated_attn_block'


def layer_norm(x, g, b):
    xf = x.astype(jnp.float32)
    mu = jnp.mean(xf, axis=-1, keepdims=True)
    var = jnp.mean(jnp.square(xf - mu), axis=-1, keepdims=True)
    y = (xf - mu) * lax.rsqrt(var + LN_EPS) * g.astype(jnp.float32) + b.astype(jnp.float32)
    return y.astype(x.dtype)


def rope_partial(t, cos, sin):
    half = ROT_DIM // 2
    rot = t[..., :ROT_DIM].astype(jnp.float32)
    x1, x2 = rot[..., :half], rot[..., half:]
    c = cos[:, :, None, :]
    s = sin[:, :, None, :]
    rot = jnp.concatenate([x1 * c - x2 * s, x2 * c + x1 * s], axis=-1).astype(t.dtype)
    return jnp.concatenate([rot, t[..., ROT_DIM:]], axis=-1)


def s5_ssm(u, log_dt, a_re, a_im, b_re, b_im, c_re, c_im, d):
    f32 = jnp.float32
    bsz, s, _ = u.shape
    uf = u.astype(f32)
    ug = uf.reshape(bsz, s, SSM_GROUPS, SSM_GROUP)
    a_re = a_re.astype(f32)
    a_im = a_im.astype(f32)
    dt = jnp.exp(log_dt.astype(f32))[:, None]
    mag = jnp.exp(a_re * dt)
    ab_re = mag * jnp.cos(a_im * dt)
    ab_im = mag * jnp.sin(a_im * dt)
    den = jnp.square(a_re) + jnp.square(a_im)
    nr = ab_re - 1.0
    f_re = (nr * a_re + ab_im * a_im) / den
    f_im = (ab_im * a_re - nr * a_im) / den
    b_re = b_re.astype(f32)
    b_im = b_im.astype(f32)
    bb_re = f_re[..., None] * b_re - f_im[..., None] * b_im
    bb_im = f_re[..., None] * b_im + f_im[..., None] * b_re
    w_re = jnp.einsum('bsgc,gnc->bsgn', ug, bb_re)
    w_im = jnp.einsum('bsgc,gnc->bsgn', ug, bb_im)
    ar = jnp.broadcast_to(ab_re, w_re.shape)
    ai = jnp.broadcast_to(ab_im, w_im.shape)

    def combine(e1, e2):
        a1r, a1i, b1r, b1i = e1
        a2r, a2i, b2r, b2i = e2
        return (a2r * a1r - a2i * a1i,
                a2r * a1i + a2i * a1r,
                a2r * b1r - a2i * b1i + b2r,
                a2r * b1i + a2i * b1r + b2i)

    _, _, h_re, h_im = lax.associative_scan(combine, (ar, ai, w_re, w_im), axis=1)
    y = (jnp.einsum('bsgn,gcn->bsgc', h_re, c_re.astype(f32))
         - jnp.einsum('bsgn,gcn->bsgc', h_im, c_im.astype(f32)))
    y = y.reshape(bsz, s, SSM_WIDTH) + d.astype(f32) * uf
    return y.astype(u.dtype)


def dilated_window_attention(q, k, v, window, dilation):
    bsz, s, h, dh = q.shape
    span = window // dilation
    blk = span
    unit = blk * dilation
    length = -(-s // unit) * unit
    n_blk = length // unit
    pad = length - s

    def arrange(t):
        t = jnp.pad(t, ((0, 0), (0, pad), (0, 0), (0, 0)))
        t = t.reshape(bsz, length // dilation, dilation, h, dh)
        t = t.transpose(0, 2, 1, 3, 4)
        return t.reshape(bsz, dilation, n_blk, blk, h, dh)

    def with_prev(t):
        prev = jnp.pad(t, ((0, 0), (0, 0), (1, 0), (0, 0), (0, 0), (0, 0)))[:, :, :-1]
        return jnp.concatenate([prev, t], axis=3)

    qb = arrange(q)
    kw = with_prev(arrange(k))
    vw = with_prev(arrange(v))
    scores = jnp.einsum('brnqhd,brnkhd->brnhqk', qb, kw).astype(jnp.float32) * ATT_SCALE
    qi = jnp.arange(blk)[:, None]
    ki = jnp.arange(2 * blk)[None, :]
    steps = qi + blk - ki
    band = (steps >= 0) & (steps <= span)
    has_prev = (jnp.arange(n_blk) > 0)[:, None, None]
    valid = band[None] & (has_prev | (ki >= blk)[None])
    scores = jnp.where(valid[None, None, :, None], scores, NEG_INF)
    m = jnp.max(scores, axis=-1, keepdims=True)
    p = jnp.exp(scores - m)
    den = jnp.sum(p, axis=-1, keepdims=True)
    lse = (m + jnp.log(den))[..., 0]
    out = jnp.einsum('brnhqk,brnkhd->brnhqd', p, vw.astype(jnp.float32)) / den
    out = out.transpose(0, 1, 2, 4, 3, 5).reshape(bsz, dilation, length // dilation, h, dh)
    out = out.transpose(0, 2, 1, 3, 4).reshape(bsz, length, h, dh)[:, :s]
    lse = lse.transpose(0, 1, 2, 4, 3).reshape(bsz, dilation, length // dilation, h)
    lse = lse.transpose(0, 2, 1, 3).reshape(bsz, length, h)[:, :s]
    return out, lse


def memory_cross_attention(h, mem, w_xq, w_xkv, w_xo):
    bsz, s, _ = h.shape
    q = (h @ w_xq).reshape(bsz, s, XATT_HEADS, XATT_HEAD_DIM)
    kv = mem @ w_xkv
    k = kv[..., :D_MODEL].reshape(bsz, -1, XATT_HEADS, XATT_HEAD_DIM)
    v = kv[..., D_MODEL:].reshape(bsz, -1, XATT_HEADS, XATT_HEAD_DIM)
    scores = jnp.einsum('bshd,bmhd->bhsm', q, k).astype(jnp.float32) * XATT_SCALE
    p = jax.nn.softmax(scores, axis=-1)
    o = jnp.einsum('bhsm,bmhd->bshd', p, v.astype(jnp.float32)).astype(h.dtype)
    return o.reshape(bsz, s, D_MODEL) @ w_xo


def _fwd_setup_inputs(seed: int = 0) -> dict:
    key = jax.random.key(seed)
    ks = jax.random.split(key, 40)
    f32 = jnp.float32
    L, D, G, N, C = DEPTH, D_MODEL, SSM_GROUPS, SSM_STATE, SSM_GROUP

    def nrm(k, shape, scale):
        return jax.random.normal(k, shape, f32) * scale

    def gain(k, shape):
        return 1.0 + nrm(k, shape, 0.05)

    n_idx = jnp.arange(N, dtype=f32)
    inp = {
        'x': nrm(ks[0], (BATCH, SEQ, D), 1.0),
        'mem': nrm(ks[1], (BATCH, MEM_LEN, D), 1.0),
        'positions': jnp.broadcast_to(jnp.arange(SEQ, dtype=jnp.int32)[None, :], (BATCH, SEQ)),
        'ln_in_g': gain(ks[2], (D,)),
        'ln_in_b': nrm(ks[3], (D,), 0.02),
        'w_in': nrm(ks[4], (L, D, IN_COLS), D ** -0.5),
        'b_in': nrm(ks[5], (L, IN_COLS), 0.02),
        'ssm_log_dt': jax.random.uniform(ks[6], (L, G), f32, np.log(SSM_DT_MIN), np.log(SSM_DT_MAX)),
        'ssm_a_re': -0.5 + nrm(ks[7], (L, G, N), 0.01),
        'ssm_a_im': jnp.pi * n_idx + nrm(ks[8], (L, G, N), 0.01),
        'ssm_b_re': nrm(ks[9], (L, G, N, C), (0.5 / C) ** 0.5),
        'ssm_b_im': nrm(ks[10], (L, G, N, C), (0.5 / C) ** 0.5),
        'ssm_c_re': nrm(ks[11], (L, G, C, N), (0.5 / N) ** 0.5),
        'ssm_c_im': nrm(ks[12], (L, G, C, N), (0.5 / N) ** 0.5),
        'ssm_d': nrm(ks[13], (L, SSM_WIDTH), 1.0),
        'w_glu': nrm(ks[14], (L, SSM_WIDTH, 2 * D), SSM_WIDTH ** -0.5),
        'b_glu': nrm(ks[15], (L, 2 * D), 0.02),
        'w_att_up': nrm(ks[16], (L, ATT_MERGED, D), ATT_MERGED ** -0.5),
        'w_mix_out': nrm(ks[17], (L, D, D), DEEPNORM_BETA * D ** -0.5),
        'b_mix_out': nrm(ks[18], (L, D), 0.02),
        'ln1_g': gain(ks[19], (L, D)),
        'ln1_b': nrm(ks[20], (L, D), 0.02),
        'w_xq': nrm(ks[21], (L, D, D), D ** -0.5),
        'w_xkv': nrm(ks[22], (L, D, 2 * D), D ** -0.5),
        'w_xo': nrm(ks[23], (L, D, D), DEEPNORM_BETA * D ** -0.5),
        'ln2_g': gain(ks[24], (L, D)),
        'ln2_b': nrm(ks[25], (L, D), 0.02),
        'w_ff1': nrm(ks[26], (L, D, D_FF), D ** -0.5),
        'b_ff1': nrm(ks[27], (L, D_FF), 0.02),
        'w_ff2': nrm(ks[28], (L, D_FF, D), DEEPNORM_BETA * D_FF ** -0.5),
        'b_ff2': nrm(ks[29], (L, D), 0.02),
        'ln3_g': gain(ks[30], (L, D)),
        'ln3_b': nrm(ks[31], (L, D), 0.02),
    }
    return inp


def _fwd_reference(x, mem, positions, ln_in_g, ln_in_b, w_in, b_in, ssm_log_dt, ssm_a_re, ssm_a_im,
              ssm_b_re, ssm_b_im, ssm_c_re, ssm_c_im, ssm_d, w_glu, b_glu, w_att_up, w_mix_out,
              b_mix_out, ln1_g, ln1_b, w_xq, w_xkv, w_xo, ln2_g, ln2_b, w_ff1, b_ff1, w_ff2, b_ff2,
              ln3_g, ln3_b):
    bsz, s, _ = x.shape
    inv_freq = ROPE_THETA ** (-jnp.arange(0, ROT_DIM, 2, dtype=jnp.float32) / ROT_DIM)
    ang = positions.astype(jnp.float32)[..., None] * inv_freq
    cos, sin = jnp.cos(ang), jnp.sin(ang)

    h = layer_norm(x, ln_in_g, ln_in_b)
    for l in range(DEPTH):
        proj = h @ w_in[l] + b_in[l]
        u = proj[..., OFF_U:OFF_U + SSM_WIDTH]
        q = proj[..., OFF_Q:OFF_Q + ATT_WIDTH].reshape(bsz, s, ATT_HEADS, ATT_HEAD_DIM)
        k = proj[..., OFF_K:OFF_K + ATT_WIDTH].reshape(bsz, s, ATT_HEADS, ATT_HEAD_DIM)
        v = proj[..., OFF_V:OFF_V + ATT_WIDTH].reshape(bsz, s, ATT_HEADS, ATT_HEAD_DIM)
        g_ssm = proj[..., OFF_GS:OFF_GS + D_MODEL]
        g_att = proj[..., OFF_GA:OFF_GA + D_MODEL]

        y = s5_ssm(u, ssm_log_dt[l], ssm_a_re[l], ssm_a_im[l], ssm_b_re[l], ssm_b_im[l],
                   ssm_c_re[l], ssm_c_im[l], ssm_d[l])
        z = jax.nn.gelu(y) @ w_glu[l] + b_glu[l]
        b_ssm = z[..., :D_MODEL] * jax.nn.sigmoid(z[..., D_MODEL:])

        q = rope_partial(q, cos, sin)
        k = rope_partial(k, cos, sin)
        outs, lses = [], []
        for gi, (win, dil) in enumerate(DILATION_PATTERN):
            sl = slice(gi * ATT_HEADS_PER_GROUP, (gi + 1) * ATT_HEADS_PER_GROUP)
            o_g, lse_g = dilated_window_attention(q[:, :, sl], k[:, :, sl], v[:, :, sl], win, dil)
            outs.append(o_g)
            lses.append(lse_g)
        wts = jax.nn.softmax(jnp.stack(lses, axis=0), axis=0)
        att = jnp.einsum('gbsh,gbshd->bshd', wts, jnp.stack(outs, axis=0)).astype(h.dtype)
        b_att = att.reshape(bsz, s, ATT_MERGED) @ w_att_up[l]

        mixed = jax.nn.sigmoid(g_ssm) * b_ssm + jax.nn.sigmoid(g_att) * b_att
        h = layer_norm(DEEPNORM_ALPHA * h + (mixed @ w_mix_out[l] + b_mix_out[l]), ln1_g[l], ln1_b[l])

        xo = memory_cross_attention(h, mem, w_xq[l], w_xkv[l], w_xo[l])
        h = layer_norm(DEEPNORM_ALPHA * h + xo, ln2_g[l], ln2_b[l])

        ff = jnp.square(jax.nn.relu(h @ w_ff1[l] + b_ff1[l])) @ w_ff2[l] + b_ff2[l]
        h = layer_norm(DEEPNORM_ALPHA * h + ff, ln3_g[l], ln3_b[l])
    return h


import jax as _jax
import jax.numpy as _jnp

TWIN_FORMAT = 'train_step'
FWD_PARAMS = ['x', 'mem', 'positions', 'ln_in_g', 'ln_in_b', 'w_in', 'b_in', 'ssm_log_dt', 'ssm_a_re', 'ssm_a_im', 'ssm_b_re', 'ssm_b_im', 'ssm_c_re', 'ssm_c_im', 'ssm_d', 'w_glu', 'b_glu', 'w_att_up', 'w_mix_out', 'b_mix_out', 'ln1_g', 'ln1_b', 'w_xq', 'w_xkv', 'w_xo', 'ln2_g', 'ln2_b', 'w_ff1', 'b_ff1', 'w_ff2', 'b_ff2', 'ln3_g', 'ln3_b']
TWIN_WEIGHTS = ['ln_in_g', 'ln_in_b', 'w_in', 'b_in', 'ssm_log_dt', 'ssm_a_re', 'ssm_a_im', 'ssm_b_re', 'ssm_b_im', 'ssm_c_re', 'ssm_c_im', 'ssm_d', 'w_glu', 'b_glu', 'w_att_up', 'w_mix_out', 'b_mix_out', 'ln1_g', 'ln1_b', 'w_xq', 'w_xkv', 'w_xo', 'ln2_g', 'ln2_b', 'w_ff1', 'b_ff1', 'w_ff2', 'b_ff2', 'ln3_g', 'ln3_b']
TWIN_DIFF_INPUT = 'x'
TWIN_INPUTS = ['x', 'mem', 'positions', 'ln_in_g', 'ln_in_b', 'w_in', 'b_in', 'ssm_log_dt', 'ssm_a_re', 'ssm_a_im', 'ssm_b_re', 'ssm_b_im', 'ssm_c_re', 'ssm_c_im', 'ssm_d', 'w_glu', 'b_glu', 'w_att_up', 'w_mix_out', 'b_mix_out', 'ln1_g', 'ln1_b', 'w_xq', 'w_xkv', 'w_xo', 'ln2_g', 'ln2_b', 'w_ff1', 'b_ff1', 'w_ff2', 'b_ff2', 'ln3_g', 'ln3_b', 'loss_target', 'm_ln_in_g', 'm_ln_in_b', 'm_w_in', 'm_b_in', 'm_ssm_log_dt', 'm_ssm_a_re', 'm_ssm_a_im', 'm_ssm_b_re', 'm_ssm_b_im', 'm_ssm_c_re', 'm_ssm_c_im', 'm_ssm_d', 'm_w_glu', 'm_b_glu', 'm_w_att_up', 'm_w_mix_out', 'm_b_mix_out', 'm_ln1_g', 'm_ln1_b', 'm_w_xq', 'm_w_xkv', 'm_w_xo', 'm_ln2_g', 'm_ln2_b', 'm_w_ff1', 'm_b_ff1', 'm_w_ff2', 'm_b_ff2', 'm_ln3_g', 'm_ln3_b', 'v_ln_in_g', 'v_ln_in_b', 'v_w_in', 'v_b_in', 'v_ssm_log_dt', 'v_ssm_a_re', 'v_ssm_a_im', 'v_ssm_b_re', 'v_ssm_b_im', 'v_ssm_c_re', 'v_ssm_c_im', 'v_ssm_d', 'v_w_glu', 'v_b_glu', 'v_w_att_up', 'v_w_mix_out', 'v_b_mix_out', 'v_ln1_g', 'v_ln1_b', 'v_w_xq', 'v_w_xkv', 'v_w_xo', 'v_ln2_g', 'v_ln2_b', 'v_w_ff1', 'v_b_ff1', 'v_w_ff2', 'v_b_ff2', 'v_ln3_g', 'v_ln3_b']
TWIN_OUTPUTS = ['loss', 'grad_x', 'grad_ln_in_g', 'grad_ln_in_b', 'grad_w_in', 'grad_b_in', 'grad_ssm_log_dt', 'grad_ssm_a_re', 'grad_ssm_a_im', 'grad_ssm_b_re', 'grad_ssm_b_im', 'grad_ssm_c_re', 'grad_ssm_c_im', 'grad_ssm_d', 'grad_w_glu', 'grad_b_glu', 'grad_w_att_up', 'grad_w_mix_out', 'grad_b_mix_out', 'grad_ln1_g', 'grad_ln1_b', 'grad_w_xq', 'grad_w_xkv', 'grad_w_xo', 'grad_ln2_g', 'grad_ln2_b', 'grad_w_ff1', 'grad_b_ff1', 'grad_w_ff2', 'grad_b_ff2', 'grad_ln3_g', 'grad_ln3_b', 'delta_ln_in_g', 'delta_ln_in_b', 'delta_w_in', 'delta_b_in', 'delta_ssm_log_dt', 'delta_ssm_a_re', 'delta_ssm_a_im', 'delta_ssm_b_re', 'delta_ssm_b_im', 'delta_ssm_c_re', 'delta_ssm_c_im', 'delta_ssm_d', 'delta_w_glu', 'delta_b_glu', 'delta_w_att_up', 'delta_w_mix_out', 'delta_b_mix_out', 'delta_ln1_g', 'delta_ln1_b', 'delta_w_xq', 'delta_w_xkv', 'delta_w_xo', 'delta_ln2_g', 'delta_ln2_b', 'delta_w_ff1', 'delta_b_ff1', 'delta_w_ff2', 'delta_b_ff2', 'delta_ln3_g', 'delta_ln3_b', 'new_m_ln_in_g', 'new_m_ln_in_b', 'new_m_w_in', 'new_m_b_in', 'new_m_ssm_log_dt', 'new_m_ssm_a_re', 'new_m_ssm_a_im', 'new_m_ssm_b_re', 'new_m_ssm_b_im', 'new_m_ssm_c_re', 'new_m_ssm_c_im', 'new_m_ssm_d', 'new_m_w_glu', 'new_m_b_glu', 'new_m_w_att_up', 'new_m_w_mix_out', 'new_m_b_mix_out', 'new_m_ln1_g', 'new_m_ln1_b', 'new_m_w_xq', 'new_m_w_xkv', 'new_m_w_xo', 'new_m_ln2_g', 'new_m_ln2_b', 'new_m_w_ff1', 'new_m_b_ff1', 'new_m_w_ff2', 'new_m_b_ff2', 'new_m_ln3_g', 'new_m_ln3_b', 'new_v_ln_in_g', 'new_v_ln_in_b', 'new_v_w_in', 'new_v_b_in', 'new_v_ssm_log_dt', 'new_v_ssm_a_re', 'new_v_ssm_a_im', 'new_v_ssm_b_re', 'new_v_ssm_b_im', 'new_v_ssm_c_re', 'new_v_ssm_c_im', 'new_v_ssm_d', 'new_v_w_glu', 'new_v_b_glu', 'new_v_w_att_up', 'new_v_w_mix_out', 'new_v_b_mix_out', 'new_v_ln1_g', 'new_v_ln1_b', 'new_v_w_xq', 'new_v_w_xkv', 'new_v_w_xo', 'new_v_ln2_g', 'new_v_ln2_b', 'new_v_w_ff1', 'new_v_b_ff1', 'new_v_w_ff2', 'new_v_b_ff2', 'new_v_ln3_g', 'new_v_ln3_b']
TWIN_LEAF_KINDS = {'loss': 'loss', 'grad_x': 'grad_x', 'grad_ln_in_g': 'grad_w', 'grad_ln_in_b': 'grad_w', 'grad_w_in': 'grad_w', 'grad_b_in': 'grad_w', 'grad_ssm_log_dt': 'grad_w', 'grad_ssm_a_re': 'grad_w', 'grad_ssm_a_im': 'grad_w', 'grad_ssm_b_re': 'grad_w', 'grad_ssm_b_im': 'grad_w', 'grad_ssm_c_re': 'grad_w', 'grad_ssm_c_im': 'grad_w', 'grad_ssm_d': 'grad_w', 'grad_w_glu': 'grad_w', 'grad_b_glu': 'grad_w', 'grad_w_att_up': 'grad_w', 'grad_w_mix_out': 'grad_w', 'grad_b_mix_out': 'grad_w', 'grad_ln1_g': 'grad_w', 'grad_ln1_b': 'grad_w', 'grad_w_xq': 'grad_w', 'grad_w_xkv': 'grad_w', 'grad_w_xo': 'grad_w', 'grad_ln2_g': 'grad_w', 'grad_ln2_b': 'grad_w', 'grad_w_ff1': 'grad_w', 'grad_b_ff1': 'grad_w', 'grad_w_ff2': 'grad_w', 'grad_b_ff2': 'grad_w', 'grad_ln3_g': 'grad_w', 'grad_ln3_b': 'grad_w', 'delta_ln_in_g': 'delta_w', 'delta_ln_in_b': 'delta_w', 'delta_w_in': 'delta_w', 'delta_b_in': 'delta_w', 'delta_ssm_log_dt': 'delta_w', 'delta_ssm_a_re': 'delta_w', 'delta_ssm_a_im': 'delta_w', 'delta_ssm_b_re': 'delta_w', 'delta_ssm_b_im': 'delta_w', 'delta_ssm_c_re': 'delta_w', 'delta_ssm_c_im': 'delta_w', 'delta_ssm_d': 'delta_w', 'delta_w_glu': 'delta_w', 'delta_b_glu': 'delta_w', 'delta_w_att_up': 'delta_w', 'delta_w_mix_out': 'delta_w', 'delta_b_mix_out': 'delta_w', 'delta_ln1_g': 'delta_w', 'delta_ln1_b': 'delta_w', 'delta_w_xq': 'delta_w', 'delta_w_xkv': 'delta_w', 'delta_w_xo': 'delta_w', 'delta_ln2_g': 'delta_w', 'delta_ln2_b': 'delta_w', 'delta_w_ff1': 'delta_w', 'delta_b_ff1': 'delta_w', 'delta_w_ff2': 'delta_w', 'delta_b_ff2': 'delta_w', 'delta_ln3_g': 'delta_w', 'delta_ln3_b': 'delta_w', 'new_m_ln_in_g': 'new_m', 'new_m_ln_in_b': 'new_m', 'new_m_w_in': 'new_m', 'new_m_b_in': 'new_m', 'new_m_ssm_log_dt': 'new_m', 'new_m_ssm_a_re': 'new_m', 'new_m_ssm_a_im': 'new_m', 'new_m_ssm_b_re': 'new_m', 'new_m_ssm_b_im': 'new_m', 'new_m_ssm_c_re': 'new_m', 'new_m_ssm_c_im': 'new_m', 'new_m_ssm_d': 'new_m', 'new_m_w_glu': 'new_m', 'new_m_b_glu': 'new_m', 'new_m_w_att_up': 'new_m', 'new_m_w_mix_out': 'new_m', 'new_m_b_mix_out': 'new_m', 'new_m_ln1_g': 'new_m', 'new_m_ln1_b': 'new_m', 'new_m_w_xq': 'new_m', 'new_m_w_xkv': 'new_m', 'new_m_w_xo': 'new_m', 'new_m_ln2_g': 'new_m', 'new_m_ln2_b': 'new_m', 'new_m_w_ff1': 'new_m', 'new_m_b_ff1': 'new_m', 'new_m_w_ff2': 'new_m', 'new_m_b_ff2': 'new_m', 'new_m_ln3_g': 'new_m', 'new_m_ln3_b': 'new_m', 'new_v_ln_in_g': 'new_v', 'new_v_ln_in_b': 'new_v', 'new_v_w_in': 'new_v', 'new_v_b_in': 'new_v', 'new_v_ssm_log_dt': 'new_v', 'new_v_ssm_a_re': 'new_v', 'new_v_ssm_a_im': 'new_v', 'new_v_ssm_b_re': 'new_v', 'new_v_ssm_b_im': 'new_v', 'new_v_ssm_c_re': 'new_v', 'new_v_ssm_c_im': 'new_v', 'new_v_ssm_d': 'new_v', 'new_v_w_glu': 'new_v', 'new_v_b_glu': 'new_v', 'new_v_w_att_up': 'new_v', 'new_v_w_mix_out': 'new_v', 'new_v_b_mix_out': 'new_v', 'new_v_ln1_g': 'new_v', 'new_v_ln1_b': 'new_v', 'new_v_w_xq': 'new_v', 'new_v_w_xkv': 'new_v', 'new_v_w_xo': 'new_v', 'new_v_ln2_g': 'new_v', 'new_v_ln2_b': 'new_v', 'new_v_w_ff1': 'new_v', 'new_v_b_ff1': 'new_v', 'new_v_w_ff2': 'new_v', 'new_v_b_ff2': 'new_v', 'new_v_ln3_g': 'new_v', 'new_v_ln3_b': 'new_v'}


def _forward(args):
    return _fwd_reference(*[args[k] for k in FWD_PARAMS])


def _output_shape():
    def fwd():
        inp = _fwd_setup_inputs(0)
        return _fwd_reference(*[inp[k] for k in FWD_PARAMS])
    out = _jax.eval_shape(fwd)
    return out.shape, out.dtype

N_MICROBATCH = 1
ADAM_LR = 0.001
ADAM_B1 = 0.9
ADAM_B2 = 0.999
ADAM_EPS = 1e-08
ADAM_WD = 0.01
ADAM_STEP = 10
PER_EXAMPLE_BATCH_AXIS = {'x': 0, 'mem': 0, 'positions': 0, 'loss_target': 0}
SHARED_INPUTS = []
_WEIGHT_DTYPES = {'ln_in_g': _jnp.float32, 'ln_in_b': _jnp.float32, 'w_in': _jnp.float32, 'b_in': _jnp.float32, 'ssm_log_dt': _jnp.float32, 'ssm_a_re': _jnp.float32, 'ssm_a_im': _jnp.float32, 'ssm_b_re': _jnp.float32, 'ssm_b_im': _jnp.float32, 'ssm_c_re': _jnp.float32, 'ssm_c_im': _jnp.float32, 'ssm_d': _jnp.float32, 'w_glu': _jnp.float32, 'b_glu': _jnp.float32, 'w_att_up': _jnp.float32, 'w_mix_out': _jnp.float32, 'b_mix_out': _jnp.float32, 'ln1_g': _jnp.float32, 'ln1_b': _jnp.float32, 'w_xq': _jnp.float32, 'w_xkv': _jnp.float32, 'w_xo': _jnp.float32, 'ln2_g': _jnp.float32, 'ln2_b': _jnp.float32, 'w_ff1': _jnp.float32, 'b_ff1': _jnp.float32, 'w_ff2': _jnp.float32, 'b_ff2': _jnp.float32, 'ln3_g': _jnp.float32, 'ln3_b': _jnp.float32}
MOMENT_SCALE = {'ln_in_g': 2.240923e+00, 'ln_in_b': 8.129474e-01, 'w_in': 1.157994e-02, 'b_in': 5.965593e-02, 'ssm_log_dt': 7.660778e-01, 'ssm_a_re': 1.072408e-03, 'ssm_a_im': 1.094183e-03, 'ssm_b_re': 7.070231e-04, 'ssm_b_im': 7.008867e-04, 'ssm_c_re': 1.415206e-03, 'ssm_c_im': 1.440687e-03, 'ssm_d': 4.345203e-02, 'w_glu': 2.382699e-02, 'b_glu': 7.138452e-02, 'w_att_up': 9.604869e-03, 'w_mix_out': 5.426625e-02, 'b_mix_out': 6.713395e-01, 'ln1_g': 2.266855e+00, 'ln1_b': 8.030343e-01, 'w_xq': 9.727851e-03, 'w_xkv': 1.076926e-02, 'w_xo': 1.982527e-02, 'ln2_g': 2.281509e+00, 'ln2_b': 8.100767e-01, 'w_ff1': 5.513646e-02, 'b_ff1': 1.447788e-01, 'w_ff2': 3.108715e-01, 'b_ff2': 6.348598e-01, 'ln3_g': 3.232624e+01, 'ln3_b': 7.262465e+00}


def _to_microbatches(a, axis):
    t = _jnp.moveaxis(a, axis, 0)
    t = t.reshape((N_MICROBATCH, t.shape[0] // N_MICROBATCH) + t.shape[1:])
    return _jnp.moveaxis(t, 1, axis + 1)


def setup_inputs(seed: int = 0) -> dict:
    inp = _fwd_setup_inputs(seed)
    key = _jax.random.fold_in(_jax.random.key(seed), 7919)
    shape, _ = _output_shape()
    out = dict(inp)
    out["loss_target"] = _jax.random.normal(_jax.random.fold_in(key, 0), shape, _jnp.float32)
    for i, name in enumerate(TWIN_WEIGHTS):
        w = inp[name].astype(_jnp.float32)
        if MOMENT_SCALE is None:
            s = _jnp.sqrt(_jnp.mean(_jnp.square(w)) + 1e-30)
        else:
            s = MOMENT_SCALE[name]
        km, kv = _jax.random.split(_jax.random.fold_in(key, i + 1))
        out[name] = w
        out["m_" + name] = s * _jax.random.normal(km, w.shape, _jnp.float32)
        out["v_" + name] = (s * s) * _jax.random.uniform(kv, w.shape, _jnp.float32, 0.5, 1.5)
    if N_MICROBATCH > 1:
        for name, axis in PER_EXAMPLE_BATCH_AXIS.items():
            out[name] = _to_microbatches(out[name], axis)
    return {'x': out['x'], 'mem': out['mem'], 'positions': out['positions'], 'ln_in_g': out['ln_in_g'], 'ln_in_b': out['ln_in_b'], 'w_in': out['w_in'], 'b_in': out['b_in'], 'ssm_log_dt': out['ssm_log_dt'], 'ssm_a_re': out['ssm_a_re'], 'ssm_a_im': out['ssm_a_im'], 'ssm_b_re': out['ssm_b_re'], 'ssm_b_im': out['ssm_b_im'], 'ssm_c_re': out['ssm_c_re'], 'ssm_c_im': out['ssm_c_im'], 'ssm_d': out['ssm_d'], 'w_glu': out['w_glu'], 'b_glu': out['b_glu'], 'w_att_up': out['w_att_up'], 'w_mix_out': out['w_mix_out'], 'b_mix_out': out['b_mix_out'], 'ln1_g': out['ln1_g'], 'ln1_b': out['ln1_b'], 'w_xq': out['w_xq'], 'w_xkv': out['w_xkv'], 'w_xo': out['w_xo'], 'ln2_g': out['ln2_g'], 'ln2_b': out['ln2_b'], 'w_ff1': out['w_ff1'], 'b_ff1': out['b_ff1'], 'w_ff2': out['w_ff2'], 'b_ff2': out['b_ff2'], 'ln3_g': out['ln3_g'], 'ln3_b': out['ln3_b'], 'loss_target': out['loss_target'], 'm_ln_in_g': out['m_ln_in_g'], 'm_ln_in_b': out['m_ln_in_b'], 'm_w_in': out['m_w_in'], 'm_b_in': out['m_b_in'], 'm_ssm_log_dt': out['m_ssm_log_dt'], 'm_ssm_a_re': out['m_ssm_a_re'], 'm_ssm_a_im': out['m_ssm_a_im'], 'm_ssm_b_re': out['m_ssm_b_re'], 'm_ssm_b_im': out['m_ssm_b_im'], 'm_ssm_c_re': out['m_ssm_c_re'], 'm_ssm_c_im': out['m_ssm_c_im'], 'm_ssm_d': out['m_ssm_d'], 'm_w_glu': out['m_w_glu'], 'm_b_glu': out['m_b_glu'], 'm_w_att_up': out['m_w_att_up'], 'm_w_mix_out': out['m_w_mix_out'], 'm_b_mix_out': out['m_b_mix_out'], 'm_ln1_g': out['m_ln1_g'], 'm_ln1_b': out['m_ln1_b'], 'm_w_xq': out['m_w_xq'], 'm_w_xkv': out['m_w_xkv'], 'm_w_xo': out['m_w_xo'], 'm_ln2_g': out['m_ln2_g'], 'm_ln2_b': out['m_ln2_b'], 'm_w_ff1': out['m_w_ff1'], 'm_b_ff1': out['m_b_ff1'], 'm_w_ff2': out['m_w_ff2'], 'm_b_ff2': out['m_b_ff2'], 'm_ln3_g': out['m_ln3_g'], 'm_ln3_b': out['m_ln3_b'], 'v_ln_in_g': out['v_ln_in_g'], 'v_ln_in_b': out['v_ln_in_b'], 'v_w_in': out['v_w_in'], 'v_b_in': out['v_b_in'], 'v_ssm_log_dt': out['v_ssm_log_dt'], 'v_ssm_a_re': out['v_ssm_a_re'], 'v_ssm_a_im': out['v_ssm_a_im'], 'v_ssm_b_re': out['v_ssm_b_re'], 'v_ssm_b_im': out['v_ssm_b_im'], 'v_ssm_c_re': out['v_ssm_c_re'], 'v_ssm_c_im': out['v_ssm_c_im'], 'v_ssm_d': out['v_ssm_d'], 'v_w_glu': out['v_w_glu'], 'v_b_glu': out['v_b_glu'], 'v_w_att_up': out['v_w_att_up'], 'v_w_mix_out': out['v_w_mix_out'], 'v_b_mix_out': out['v_b_mix_out'], 'v_ln1_g': out['v_ln1_g'], 'v_ln1_b': out['v_ln1_b'], 'v_w_xq': out['v_w_xq'], 'v_w_xkv': out['v_w_xkv'], 'v_w_xo': out['v_w_xo'], 'v_ln2_g': out['v_ln2_g'], 'v_ln2_b': out['v_ln2_b'], 'v_w_ff1': out['v_w_ff1'], 'v_b_ff1': out['v_b_ff1'], 'v_w_ff2': out['v_w_ff2'], 'v_b_ff2': out['v_b_ff2'], 'v_ln3_g': out['v_ln3_g'], 'v_ln3_b': out['v_ln3_b']}


def _loss(weights, diff, rest, loss_target):
    with _jax.named_scope("forward"):
        args = {**rest, TWIN_DIFF_INPUT: diff, **{k: w.astype(_WEIGHT_DTYPES[k]) for k, w in weights.items()}}
        y = _forward(args)
    with _jax.named_scope("loss_head"):
        err = _jnp.square(y.astype(_jnp.float32) - loss_target)
        return 0.5 * _jnp.sum(_jnp.mean(err, axis=-1)) if err.ndim else 0.5 * err


def _adamw(w, g, m, v):
    m = ADAM_B1 * m + (1.0 - ADAM_B1) * g
    v = ADAM_B2 * v + (1.0 - ADAM_B2) * _jnp.square(g)
    m_hat = m / (1.0 - ADAM_B1 ** ADAM_STEP)
    v_hat = v / (1.0 - ADAM_B2 ** ADAM_STEP)
    delta = -ADAM_LR * (m_hat / (_jnp.sqrt(v_hat) + ADAM_EPS) + ADAM_WD * w)
    return delta, m, v


def reference(x, mem, positions, ln_in_g, ln_in_b, w_in, b_in, ssm_log_dt, ssm_a_re, ssm_a_im, ssm_b_re, ssm_b_im, ssm_c_re, ssm_c_im, ssm_d, w_glu, b_glu, w_att_up, w_mix_out, b_mix_out, ln1_g, ln1_b, w_xq, w_xkv, w_xo, ln2_g, ln2_b, w_ff1, b_ff1, w_ff2, b_ff2, ln3_g, ln3_b, loss_target, m_ln_in_g, m_ln_in_b, m_w_in, m_b_in, m_ssm_log_dt, m_ssm_a_re, m_ssm_a_im, m_ssm_b_re, m_ssm_b_im, m_ssm_c_re, m_ssm_c_im, m_ssm_d, m_w_glu, m_b_glu, m_w_att_up, m_w_mix_out, m_b_mix_out, m_ln1_g, m_ln1_b, m_w_xq, m_w_xkv, m_w_xo, m_ln2_g, m_ln2_b, m_w_ff1, m_b_ff1, m_w_ff2, m_b_ff2, m_ln3_g, m_ln3_b, v_ln_in_g, v_ln_in_b, v_w_in, v_b_in, v_ssm_log_dt, v_ssm_a_re, v_ssm_a_im, v_ssm_b_re, v_ssm_b_im, v_ssm_c_re, v_ssm_c_im, v_ssm_d, v_w_glu, v_b_glu, v_w_att_up, v_w_mix_out, v_b_mix_out, v_ln1_g, v_ln1_b, v_w_xq, v_w_xkv, v_w_xo, v_ln2_g, v_ln2_b, v_w_ff1, v_b_ff1, v_w_ff2, v_b_ff2, v_ln3_g, v_ln3_b):
    given = dict(x=x, mem=mem, positions=positions, ln_in_g=ln_in_g, ln_in_b=ln_in_b, w_in=w_in, b_in=b_in, ssm_log_dt=ssm_log_dt, ssm_a_re=ssm_a_re, ssm_a_im=ssm_a_im, ssm_b_re=ssm_b_re, ssm_b_im=ssm_b_im, ssm_c_re=ssm_c_re, ssm_c_im=ssm_c_im, ssm_d=ssm_d, w_glu=w_glu, b_glu=b_glu, w_att_up=w_att_up, w_mix_out=w_mix_out, b_mix_out=b_mix_out, ln1_g=ln1_g, ln1_b=ln1_b, w_xq=w_xq, w_xkv=w_xkv, w_xo=w_xo, ln2_g=ln2_g, ln2_b=ln2_b, w_ff1=w_ff1, b_ff1=b_ff1, w_ff2=w_ff2, b_ff2=b_ff2, ln3_g=ln3_g, ln3_b=ln3_b, loss_target=loss_target, m_ln_in_g=m_ln_in_g, m_ln_in_b=m_ln_in_b, m_w_in=m_w_in, m_b_in=m_b_in, m_ssm_log_dt=m_ssm_log_dt, m_ssm_a_re=m_ssm_a_re, m_ssm_a_im=m_ssm_a_im, m_ssm_b_re=m_ssm_b_re, m_ssm_b_im=m_ssm_b_im, m_ssm_c_re=m_ssm_c_re, m_ssm_c_im=m_ssm_c_im, m_ssm_d=m_ssm_d, m_w_glu=m_w_glu, m_b_glu=m_b_glu, m_w_att_up=m_w_att_up, m_w_mix_out=m_w_mix_out, m_b_mix_out=m_b_mix_out, m_ln1_g=m_ln1_g, m_ln1_b=m_ln1_b, m_w_xq=m_w_xq, m_w_xkv=m_w_xkv, m_w_xo=m_w_xo, m_ln2_g=m_ln2_g, m_ln2_b=m_ln2_b, m_w_ff1=m_w_ff1, m_b_ff1=m_b_ff1, m_w_ff2=m_w_ff2, m_b_ff2=m_b_ff2, m_ln3_g=m_ln3_g, m_ln3_b=m_ln3_b, v_ln_in_g=v_ln_in_g, v_ln_in_b=v_ln_in_b, v_w_in=v_w_in, v_b_in=v_b_in, v_ssm_log_dt=v_ssm_log_dt, v_ssm_a_re=v_ssm_a_re, v_ssm_a_im=v_ssm_a_im, v_ssm_b_re=v_ssm_b_re, v_ssm_b_im=v_ssm_b_im, v_ssm_c_re=v_ssm_c_re, v_ssm_c_im=v_ssm_c_im, v_ssm_d=v_ssm_d, v_w_glu=v_w_glu, v_b_glu=v_b_glu, v_w_att_up=v_w_att_up, v_w_mix_out=v_w_mix_out, v_b_mix_out=v_b_mix_out, v_ln1_g=v_ln1_g, v_ln1_b=v_ln1_b, v_w_xq=v_w_xq, v_w_xkv=v_w_xkv, v_w_xo=v_w_xo, v_ln2_g=v_ln2_g, v_ln2_b=v_ln2_b, v_w_ff1=v_w_ff1, v_b_ff1=v_b_ff1, v_w_ff2=v_w_ff2, v_b_ff2=v_b_ff2, v_ln3_g=v_ln3_g, v_ln3_b=v_ln3_b)
    weights = {n: given[n] for n in TWIN_WEIGHTS}
    shared = {n: given[n] for n in SHARED_INPUTS}
    per_example = {n: given[n] for n in ['x', 'mem', 'positions']}
    grad_fn = _jax.value_and_grad(_loss, argnums=(0, 1))

    def one_microbatch(ex, loss_target):
        ex = dict(ex)
        diff = ex.pop(TWIN_DIFF_INPUT)
        return grad_fn(weights, diff, {**shared, **ex}, loss_target)

    if N_MICROBATCH == 1:
        loss, (grad_w, grad_x) = one_microbatch(per_example, given["loss_target"])
    else:
        def body(carry, xs):
            loss_sum, grad_sum = carry
            l_k, (gw_k, gx_k) = one_microbatch(xs[0], xs[1])
            with _jax.named_scope("update"):
                return (loss_sum + l_k, _jax.tree.map(_jnp.add, grad_sum, gw_k)), gx_k

        init = (_jnp.zeros((), _jnp.float32), _jax.tree.map(_jnp.zeros_like, weights))
        (loss, grad_w), grad_x = _jax.lax.scan(body, init, (per_example, given["loss_target"]))
    with _jax.named_scope("update"):
        delta_w, new_m, new_v = {}, {}, {}
        for n in TWIN_WEIGHTS:
            delta_w[n], new_m[n], new_v[n] = _adamw(weights[n], grad_w[n], given["m_" + n], given["v_" + n])
    return (loss, grad_x, *[grad_w[n] for n in TWIN_WEIGHTS], *[delta_w[n] for n in TWIN_WEIGHTS],
            *[new_m[n] for n in TWIN_WEIGHTS], *[new_v[n] for n in TWIN_WEIGHTS])
```

```python
import functools

import numpy as np
import jax
import jax.numpy as jnp
from jax import lax
from jax.experimental import pallas as pl
from jax.experimental.pallas import tpu as pltpu

F32 = jnp.float32
MXU_DTYPE = jnp.bfloat16
VMEM_LIMIT_BYTES = 48 * 1024 * 1024
LANES = 128

N_DEV = 8
D_MODEL = 1024
SSM_GROUP = 16
SSM_WIDTH = 768
SSM_GROUPS = SSM_WIDTH // SSM_GROUP
SSM_STATE = 64
SSM_CH = SSM_GROUPS * SSM_STATE
SSM_TILES = SSM_WIDTH // LANES
GROUPS_PER_TILE = LANES // SSM_GROUP
STATE_VREG_ROWS = SSM_CH // LANES
ATT_HEAD_DIM = 64
ATT_HEADS_PER_GROUP = 4
ATT_MERGED = ATT_HEADS_PER_GROUP * ATT_HEAD_DIM
DILATIONS = (1, 4, 16)
ATT_BLK = 128
ATT_SCALE = ATT_HEAD_DIM ** -0.5
ROT_DIM = ATT_HEAD_DIM // 4
ROPE_THETA = 500000.0
XATT_HEADS = 4
XATT_HEAD_DIM = D_MODEL // XATT_HEADS
XATT_SCALE = XATT_HEAD_DIM ** -0.5
DEEPNORM_ALPHA = 2.0 ** 0.25
LN_EPS = 1e-5
NEG_INF = -1e30
OFF_Q_BLK, OFF_K_BLK, OFF_V_BLK = 3, 6, 9
OFF_GS_BLK, OFF_GA_BLK = 3, 4

ADAM_LR = 0.001
ADAM_B1 = 0.9
ADAM_B2 = 0.999
ADAM_EPS = 1e-08
ADAM_WD = 0.01
ADAM_STEP = 10

BIG = ("w_in", "w_glu", "w_att_up", "w_mix_out", "w_xq", "w_xkv", "w_xo", "w_ff1", "w_ff2")
BIG_COL_SHARDED = ("w_in", "w_glu", "w_att_up", "w_xkv", "w_ff1")
WEIGHTS = ("ln_in_g", "ln_in_b", "w_in", "b_in", "ssm_log_dt", "ssm_a_re", "ssm_a_im", "ssm_b_re", "ssm_b_im",
           "ssm_c_re", "ssm_c_im", "ssm_d", "w_glu", "b_glu", "w_att_up", "w_mix_out", "b_mix_out", "ln1_g", "ln1_b",
           "w_xq", "w_xkv", "w_xo", "ln2_g", "ln2_b", "w_ff1", "b_ff1", "w_ff2", "b_ff2", "ln3_g", "ln3_b")
SMALL = tuple(n for n in WEIGHTS if n not in BIG)
PACK_COLS = 1024
PACK_ROW_ALIGN = 256


def _params(*sem):
    return pltpu.CompilerParams(dimension_semantics=sem, vmem_limit_bytes=VMEM_LIMIT_BYTES)


def _dot(a, b, ca, cb):
    return lax.dot_general(a.astype(MXU_DTYPE), b.astype(MXU_DTYPE), (((ca,), (cb,)), ((), ())),
                           preferred_element_type=F32)


def _fit(dim, pref):
    if dim <= pref:
        return dim
    best = max(t for t in range(LANES, pref + 1, LANES) if dim % t == 0)
    return best


def _mm(a, b, *, name, ta=False, tb=False, bias=None, out_dtype=F32, tm=512, tn=1024, tk=1024):
    m, k = (a.shape[1], a.shape[0]) if ta else a.shape
    n = b.shape[0] if tb else b.shape[1]
    tm, tn, tk = _fit(m, tm), _fit(n, tn), _fit(k, tk)
    nk = k // tk
    a_spec = pl.BlockSpec((tk, tm), lambda i, j, kk: (kk, i)) if ta else pl.BlockSpec((tm, tk), lambda i, j, kk: (i, kk))
    b_spec = pl.BlockSpec((tn, tk), lambda i, j, kk: (j, kk)) if tb else pl.BlockSpec((tk, tn), lambda i, j, kk: (kk, j))
    in_specs, args = [a_spec, b_spec], [a, b]
    if bias is not None:
        in_specs.append(pl.BlockSpec((1, tn), lambda i, j, kk: (0, j)))
        args.append(bias)

    def body(*refs):
        a_ref, b_ref = refs[0], refs[1]
        o_ref, acc_ref = refs[-2], refs[-1]
        kk = pl.program_id(2)

        @pl.when(kk == 0)
        def _():
            acc_ref[...] = jnp.zeros_like(acc_ref)

        acc_ref[...] += _dot(a_ref[...], b_ref[...], 0 if ta else 1, 1 if tb else 0)

        @pl.when(kk == nk - 1)
        def _():
            r = acc_ref[...]
            if bias is not None:
                r = r + refs[2][...]
            o_ref[...] = r.astype(o_ref.dtype)

    return pl.pallas_call(
        body, name=name, grid=(m // tm, n // tn, nk),
        in_specs=in_specs, out_specs=pl.BlockSpec((tm, tn), lambda i, j, kk: (i, j)),
        out_shape=jax.ShapeDtypeStruct((m, n), out_dtype),
        scratch_shapes=[pltpu.VMEM((tm, tn), F32)],
        compiler_params=_params("parallel", "parallel", "arbitrary"),
    )(*args)


def _rowcall(fn, rows, fulls, row_outs, acc_outs=(), *, n_rows, tm, name):
    n_r, n_f, n_o, n_a = len(rows), len(fulls), len(row_outs), len(acc_outs)
    assert n_rows % tm == 0, (name, n_rows, tm)

    def body(*refs):
        res = fn(*[r[...] for r in refs[:n_r + n_f]])
        res = tuple(res) if isinstance(res, (tuple, list)) else (res,)
        o_refs = refs[n_r + n_f:n_r + n_f + n_o]
        a_refs = refs[n_r + n_f + n_o:]
        for o_ref, val in zip(o_refs, res[:n_o]):
            o_ref[...] = val.astype(o_ref.dtype)
        if n_a:
            @pl.when(pl.program_id(0) == 0)
            def _():
                for a_ref in a_refs:
                    a_ref[...] = jnp.zeros_like(a_ref)

            for a_ref, val in zip(a_refs, res[n_o:]):
                a_ref[...] += val

    in_specs = [pl.BlockSpec((tm, w), functools.partial(lambda i, cb: (i, cb), cb=cb)) for _, w, cb in rows]
    in_specs += [pl.BlockSpec(f.shape, functools.partial(lambda i, nd: (0,) * nd, nd=f.ndim)) for f in fulls]
    out_specs = [pl.BlockSpec((tm, w), lambda i: (i, 0)) for w, _ in row_outs]
    out_specs += [pl.BlockSpec((1, w), lambda i: (0, 0)) for w in acc_outs]
    out_shape = [jax.ShapeDtypeStruct((n_rows, w), dt) for w, dt in row_outs]
    out_shape += [jax.ShapeDtypeStruct((1, w), F32) for w in acc_outs]
    return pl.pallas_call(
        body, name=name, grid=(n_rows // tm,), in_specs=in_specs, out_specs=out_specs, out_shape=out_shape,
        compiler_params=_params("arbitrary" if n_a else "parallel"),
    )(*[r[0] for r in rows], *fulls)


def _colsum(v):
    return jnp.sum(v, axis=0, keepdims=True)


def _ln_fwd(a, r, g, b, *, alpha, name):
    n_rows, d = a.shape

    def fn(*t):
        xin = t[0] if alpha == 1.0 else alpha * t[0]
        if r is not None:
            xin = xin + t[1]
        gv, bv = t[-2], t[-1]
        mu = jnp.mean(xin, axis=-1, keepdims=True)
        xc = xin - mu
        var = jnp.mean(xc * xc, axis=-1, keepdims=True)
        rstd = lax.rsqrt(var + LN_EPS)
        xh = xc * rstd
        return xh * gv + bv, xh, rstd

    rows = [(a, d, 0)] + ([(r, d, 0)] if r is not None else [])
    return _rowcall(fn, rows, [g, b], [(d, F32), (d, F32), (1, F32)], n_rows=n_rows, tm=256, name=name)


def _ln_bwd(dya, dyb, xh, rstd, g, *, alpha, name):
    n_rows, d = xh.shape

    def fn(*t):
        if dya is not None:
            dy = alpha * t[0] + t[1]
            xhv, rs, gv = t[2], t[3], t[4]
        else:
            dy, xhv, rs, gv = t[0], t[1], t[2], t[3]
        dyg = dy * gv
        m1 = jnp.mean(dyg, axis=-1, keepdims=True)
        m2 = jnp.mean(dyg * xhv, axis=-1, keepdims=True)
        dx = rs * (dyg - m1 - xhv * m2)
        return dx, _colsum(dy * xhv), _colsum(dy), _colsum(dx)

    rows = ([(dya, d, 0)] if dya is not None else []) + [(dyb, d, 0), (xh, d, 0), (rstd, 1, 0)]
    return _rowcall(fn, rows, [g], [(d, F32)], [d, d, d], n_rows=n_rows, tm=256, name=name)


def _loss_head(y, target, *, name):
    n_rows, d = y.shape

    def fn(yv, tv):
        diff = yv - tv
        part = jnp.sum(jnp.sum(diff * diff, axis=1, keepdims=True), axis=0, keepdims=True) * (0.5 / d)
        return diff * (1.0 / d), jnp.broadcast_to(part, (1, LANES))

    return _rowcall(fn, [(y, d, 0), (target, d, 0)], [], [(d, F32)], [LANES], n_rows=n_rows, tm=256, name=name)


def _rope_lane_constants():
    lane = np.arange(ATT_MERGED)
    in_head = lane % ATT_HEAD_DIM
    sign = np.where(in_head < ROT_DIM // 2, -1.0, np.where(in_head < ROT_DIM, 1.0, 0.0)).astype(np.float32)
    inv_freq = ROPE_THETA ** (-jnp.arange(0, ROT_DIM, 2, dtype=F32) / ROT_DIM)
    return inv_freq[lane % (ROT_DIM // 2)].reshape(1, ATT_MERGED), jnp.asarray(sign).reshape(1, ATT_MERGED)


def _rope_tables(pos_col, *, name):
    inv_lane, sign = _rope_lane_constants()

    def fn(pos, inv, sg):
        ang = pos.astype(F32) * inv
        return jnp.where(sg != 0.0, jnp.cos(ang), 1.0), sg * jnp.sin(ang)

    return _rowcall(fn, [(pos_col, 1, 0)], [inv_lane, sign], [(ATT_MERGED, F32), (ATT_MERGED, F32)],
                    n_rows=pos_col.shape[0], tm=512, name=name)


def _rot_partner(t):
    lane = lax.broadcasted_iota(jnp.int32, t.shape, 1)
    width = t.shape[1]
    return jnp.where((lane & (ROT_DIM // 2)) == 0, pltpu.roll(t, width - ROT_DIM // 2, 1), pltpu.roll(t, ROT_DIM // 2, 1))


def _rope(t, cos_t, sin_t):
    return t * cos_t + _rot_partner(t) * sin_t


def _rope_transpose(dt, cos_t, sin_t):
    return dt * cos_t + _rot_partner(dt * sin_t)


def _qkv_split(proj, cos_t, sin_t, *, name):
    n_rows = proj.shape[0]
    n_g = len(DILATIONS)

    def fn(*t):
        c, s = t[3 * n_g], t[3 * n_g + 1]
        out = [_rope(t[g], c, s) for g in range(n_g)]
        out += [_rope(t[n_g + g], c, s) for g in range(n_g)]
        out += [t[2 * n_g + g] for g in range(n_g)]
        return out

    rows = [(proj, ATT_MERGED, off + g) for off in (OFF_Q_BLK, OFF_K_BLK, OFF_V_BLK) for g in range(n_g)]
    rows += [(cos_t, ATT_MERGED, 0), (sin_t, ATT_MERGED, 0)]
    outs = _rowcall(fn, rows, [], [(ATT_MERGED, MXU_DTYPE)] * (3 * n_g), n_rows=n_rows, tm=512, name=name)
    return outs[:n_g], outs[n_g:2 * n_g], outs[2 * n_g:]


def _mix(gs, ga, z1, z2, b_att):
    return jax.nn.sigmoid(gs) * (z1 * jax.nn.sigmoid(z2)) + jax.nn.sigmoid(ga) * b_att


def _mix_rows(proj, z, b_att):
    return [(proj, D_MODEL, OFF_GS_BLK), (proj, D_MODEL, OFF_GA_BLK), (z, D_MODEL, 0), (z, D_MODEL, 1), (b_att, D_MODEL, 0)]


def _mix_fwd(proj, z, b_att, *, name):
    return _rowcall(_mix, _mix_rows(proj, z, b_att), [], [(D_MODEL, MXU_DTYPE)],
                    n_rows=proj.shape[0], tm=256, name=name)[0]


def _mix_bwd(dmixed, proj, z, b_att, *, name):
    def fn(dm, gs, ga, z1, z2, ba):
        _, vjp = jax.vjp(_mix, gs, ga, z1, z2, ba)
        dgs, dga, dz1, dz2, dba = vjp(dm)
        return dgs, dga, dz1, dz2, dba, _colsum(dgs), _colsum(dga), _colsum(dz1), _colsum(dz2)

    rows = [(dmixed, D_MODEL, 0)] + _mix_rows(proj, z, b_att)
    return _rowcall(fn, rows, [], [(D_MODEL, MXU_DTYPE)] * 5, [D_MODEL] * 4, n_rows=proj.shape[0], tm=256, name=name)


def _relu2_fwd(pre, *, name):
    n_rows, w = pre.shape
    return _rowcall(lambda p: jnp.square(jnp.maximum(p, 0.0)), [(pre, w, 0)], [], [(w, MXU_DTYPE)],
                    n_rows=n_rows, tm=256, name=name)[0]


def _relu2_bwd(da, pre, *, name):
    n_rows, w = pre.shape

    def fn(dav, p):
        dp = dav * (2.0 * jnp.maximum(p, 0.0))
        return dp, _colsum(dp)

    return _rowcall(fn, [(da, w, 0), (pre, w, 0)], [], [(w, MXU_DTYPE)], [w], n_rows=n_rows, tm=256, name=name)


def _gelu_bwd(dgy, y, proj, *, name):
    def fn(dg, yv, u):
        _, vjp = jax.vjp(jax.nn.gelu, yv)
        dy = vjp(dg)[0]
        return dy, _colsum(dy * u)

    return _rowcall(fn, [(dgy, SSM_WIDTH, 0), (y, SSM_WIDTH, 0), (proj, SSM_WIDTH, 0)], [], [(SSM_WIDTH, F32)],
                    [SSM_WIDTH], n_rows=y.shape[0], tm=512, name=name)


def _dilated_view(t, dil):
    return t.reshape(t.shape[0] // dil, dil * ATT_MERGED)


def _head(h):
    return slice(h * ATT_HEAD_DIM, (h + 1) * ATT_HEAD_DIM)


def _band_mask(first_key):
    qi = lax.broadcasted_iota(jnp.int32, (ATT_BLK, 2 * ATT_BLK), 0)
    ki = lax.broadcasted_iota(jnp.int32, (ATT_BLK, 2 * ATT_BLK), 1)
    steps = qi + ATT_BLK - ki
    return (steps >= 0) & (steps <= ATT_BLK) & (ki >= first_key)


def _dil_fwd(q, k, v, dil, *, name):
    n_rows = q.shape[0]
    n_blk = n_rows // dil // ATT_BLK
    cur = pl.BlockSpec((ATT_BLK, ATT_MERGED), lambda r, n: (n, r))
    prev = pl.BlockSpec((ATT_BLK, ATT_MERGED), lambda r, n: (jnp.maximum(n - 1, 0), r))

    def body(q_ref, kp_ref, kc_ref, vp_ref, vc_ref, o_ref, l_ref):
        valid = _band_mask(jnp.where(pl.program_id(1) > 0, 0, ATT_BLK))
        for h in range(ATT_HEADS_PER_GROUP):
            sl = _head(h)
            keys = jnp.concatenate([kp_ref[:, sl], kc_ref[:, sl]], axis=0)
            vals = jnp.concatenate([vp_ref[:, sl], vc_ref[:, sl]], axis=0)
            s = jnp.where(valid, _dot(q_ref[:, sl], keys, 1, 1) * ATT_SCALE, NEG_INF)
            m = jnp.max(s, axis=-1, keepdims=True)
            p = jnp.exp(s - m)
            den = jnp.sum(p, axis=-1, keepdims=True)
            o_ref[:, sl] = _dot(p, vals, 1, 0) / den
            l_ref[:, sl] = jnp.broadcast_to(m + jnp.log(den), (ATT_BLK, ATT_HEAD_DIM))

    shape = jax.ShapeDtypeStruct((n_rows // dil, dil * ATT_MERGED), F32)
    o, lse = pl.pallas_call(
        body, name=name, grid=(dil, n_blk), in_specs=[cur, prev, cur, prev, cur], out_specs=[cur, cur],
        out_shape=[shape, shape], compiler_params=_params("parallel", "parallel"),
    )(_dilated_view(q, dil), _dilated_view(k, dil), _dilated_view(k, dil), _dilated_view(v, dil), _dilated_view(v, dil))
    return o.reshape(n_rows, ATT_MERGED), lse.reshape(n_rows, ATT_MERGED)


def _att_merge(outs, lses, *, name):
    n_g = len(outs)

    def fn(*t):
        o, l = t[:n_g], t[n_g:]
        m = functools.reduce(jnp.maximum, l)
        e = [jnp.exp(li - m) for li in l]
        z = functools.reduce(jnp.add, e)
        att = functools.reduce(jnp.add, [(ei / z) * oi for ei, oi in zip(e, o)])
        return att, m + jnp.log(z)

    rows = [(t, ATT_MERGED, 0) for t in (*outs, *lses)]
    return _rowcall(fn, rows, [], [(ATT_MERGED, F32), (ATT_MERGED, F32)], n_rows=outs[0].shape[0], tm=512, name=name)


def _dil_bwd(q, k, v, datt, att, lse, cos_t, sin_t, dil, *, name):
    n_rows = q.shape[0]
    n_blk = n_rows // dil // ATT_BLK
    cur = pl.BlockSpec((ATT_BLK, ATT_MERGED), lambda r, n: (n, r))
    prev = pl.BlockSpec((ATT_BLK, ATT_MERGED), lambda r, n: (jnp.maximum(n - 1, 0), r))
    nxt = pl.BlockSpec((ATT_BLK, ATT_MERGED), lambda r, n: (jnp.minimum(n + 1, n_blk - 1), r))
    acc = pl.BlockSpec((1, ATT_MERGED), lambda r, n: (0, 0))

    def body(qc_ref, qn_ref, kp_ref, kc_ref, vp_ref, vc_ref, dc_ref, dn_ref, ac_ref, an_ref, lc_ref, ln_ref,
             cos_ref, sin_ref, dq_ref, dk_ref, dv_ref, sq_ref, sk_ref, sv_ref, dq_s, dk_s, dv_s):
        n = pl.program_id(1)

        @pl.when((pl.program_id(0) == 0) & (n == 0))
        def _():
            for s_ref in (sq_ref, sk_ref, sv_ref):
                s_ref[...] = jnp.zeros_like(s_ref)

        valid = _band_mask(jnp.where(n > 0, 0, ATT_BLK))
        qi = lax.broadcasted_iota(jnp.int32, (ATT_BLK, ATT_BLK), 0)
        ki = lax.broadcasted_iota(jnp.int32, (ATT_BLK, ATT_BLK), 1)
        valid_next = (ki - qi) >= jnp.where(n < n_blk - 1, 0, ATT_BLK)
        for h in range(ATT_HEADS_PER_GROUP):
            sl = _head(h)
            lane0 = slice(h * ATT_HEAD_DIM, h * ATT_HEAD_DIM + 1)
            qc, kc, vc = qc_ref[:, sl], kc_ref[:, sl], vc_ref[:, sl]
            keys = jnp.concatenate([kp_ref[:, sl], kc], axis=0)
            vals = jnp.concatenate([vp_ref[:, sl], vc], axis=0)
            dc = dc_ref[:, sl]
            delta = jnp.sum(dc * ac_ref[:, sl], axis=-1, keepdims=True)
            p = jnp.where(valid, jnp.exp(_dot(qc, keys, 1, 1) * ATT_SCALE - lc_ref[:, lane0]), 0.0)
            ds = p * (_dot(dc, vals, 1, 1) - delta) * ATT_SCALE
            dq_s[:, sl] = _dot(ds, keys, 1, 0)
            qn, dn = qn_ref[:, sl], dn_ref[:, sl]
            delta_n = jnp.sum(dn * an_ref[:, sl], axis=-1, keepdims=True)
            p_n = jnp.where(valid_next, jnp.exp(_dot(qn, kc, 1, 1) * ATT_SCALE - ln_ref[:, lane0]), 0.0)
            ds_n = p_n * (_dot(dn, vc, 1, 1) - delta_n) * ATT_SCALE
            dv_s[:, sl] = _dot(p[:, ATT_BLK:], dc, 0, 0) + _dot(p_n, dn, 0, 0)
            dk_s[:, sl] = _dot(ds[:, ATT_BLK:], qc, 0, 0) + _dot(ds_n, qn, 0, 0)
        cos_v, sin_v = cos_ref[...], sin_ref[...]
        dq = _rope_transpose(dq_s[...], cos_v, sin_v)
        dk = _rope_transpose(dk_s[...], cos_v, sin_v)
        dv = dv_s[...]
        dq_ref[...] = dq.astype(dq_ref.dtype)
        dk_ref[...] = dk.astype(dk_ref.dtype)
        dv_ref[...] = dv.astype(dv_ref.dtype)
        sq_ref[...] += _colsum(dq)
        sk_ref[...] += _colsum(dk)
        sv_ref[...] += _colsum(dv)

    view = functools.partial(_dilated_view, dil=dil)
    shape = jax.ShapeDtypeStruct((n_rows // dil, dil * ATT_MERGED), MXU_DTYPE)
    sums = jax.ShapeDtypeStruct((1, ATT_MERGED), F32)
    dq, dk, dv, sq, sk, sv = pl.pallas_call(
        body, name=name, grid=(dil, n_blk),
        in_specs=[cur, nxt, prev, cur, prev, cur, cur, nxt, cur, nxt, cur, nxt, cur, cur],
        out_specs=[cur, cur, cur, acc, acc, acc], out_shape=[shape, shape, shape, sums, sums, sums],
        scratch_shapes=[pltpu.VMEM((ATT_BLK, ATT_MERGED), F32)] * 3,
        compiler_params=_params("arbitrary", "arbitrary"),
    )(view(q), view(q), view(k), view(k), view(v), view(v), view(datt), view(datt), view(att), view(att),
      view(lse), view(lse), view(cos_t), view(sin_t))
    return [t.reshape(n_rows, ATT_MERGED) for t in (dq, dk, dv)], [sq, sk, sv]


def _xhead(h):
    return slice(h * XATT_HEAD_DIM, (h + 1) * XATT_HEAD_DIM)


def _xatt_probs(qh, kh):
    s = _dot(qh, kh, 1, 1) * XATT_SCALE
    e = jnp.exp(s - jnp.max(s, axis=-1, keepdims=True))
    return e / jnp.sum(e, axis=-1, keepdims=True)


def _xatt_fwd(q, kv, *, name, tm=512):
    n_rows = q.shape[0]
    n_mem = kv.shape[0]

    def body(q_ref, kv_ref, o_ref):
        for h in range(XATT_HEADS):
            sl = _xhead(h)
            p = _xatt_probs(q_ref[:, sl], kv_ref[:, sl])
            o_ref[:, sl] = _dot(p, kv_ref[:, D_MODEL + h * XATT_HEAD_DIM:D_MODEL + (h + 1) * XATT_HEAD_DIM], 1, 0
                                ).astype(o_ref.dtype)

    row = pl.BlockSpec((tm, D_MODEL), lambda i: (i, 0))
    return pl.pallas_call(
        body, name=name, grid=(n_rows // tm,),
        in_specs=[row, pl.BlockSpec((n_mem, 2 * D_MODEL), lambda i: (0, 0))], out_specs=row,
        out_shape=jax.ShapeDtypeStruct((n_rows, D_MODEL), MXU_DTYPE), compiler_params=_params("parallel"),
    )(q, kv)


def _xatt_bwd(q, kv, do, *, name, tm=512):
    n_rows = q.shape[0]
    n_mem = kv.shape[0]

    def body(q_ref, kv_ref, do_ref, dq_ref, dkv_ref):
        @pl.when(pl.program_id(0) == 0)
        def _():
            dkv_ref[...] = jnp.zeros_like(dkv_ref)

        for h in range(XATT_HEADS):
            sl = _xhead(h)
            vsl = slice(D_MODEL + h * XATT_HEAD_DIM, D_MODEL + (h + 1) * XATT_HEAD_DIM)
            qh, kh, doh = q_ref[:, sl], kv_ref[:, sl], do_ref[:, sl]
            p = _xatt_probs(qh, kh)
            dp = _dot(doh, kv_ref[:, vsl], 1, 1)
            ds = p * (dp - jnp.sum(dp * p, axis=-1, keepdims=True)) * XATT_SCALE
            dq_ref[:, sl] = _dot(ds, kh, 1, 0).astype(dq_ref.dtype)
            dkv_ref[:, sl] += _dot(ds, qh, 0, 0)
            dkv_ref[:, vsl] += _dot(p, doh, 0, 0)

    row = pl.BlockSpec((tm, D_MODEL), lambda i: (i, 0))
    full = pl.BlockSpec((n_mem, 2 * D_MODEL), lambda i: (0, 0))
    return pl.pallas_call(
        body, name=name, grid=(n_rows // tm,), in_specs=[row, full, row], out_specs=[row, full],
        out_shape=[jax.ShapeDtypeStruct((n_rows, D_MODEL), MXU_DTYPE), jax.ShapeDtypeStruct((n_mem, 2 * D_MODEL), F32)],
        compiler_params=_params("arbitrary"),
    )(q, kv, do)


def _disc(logdt, a_re, a_im, b_re, b_im):
    dt = jnp.exp(logdt)
    mag = jnp.exp(a_re * dt)
    ab_re = mag * jnp.cos(a_im * dt)
    ab_im = mag * jnp.sin(a_im * dt)
    den = jnp.square(a_re) + jnp.square(a_im)
    nr = ab_re - 1.0
    f_re = (nr * a_re + ab_im * a_im) / den
    f_im = (ab_im * a_re - nr * a_im) / den
    bb_re = f_re[None] * b_re - f_im[None] * b_im
    bb_im = f_re[None] * b_im + f_im[None] * b_re
    return ab_re, ab_im, bb_re, bb_im


def _disc_transpose(logdt, a_re, a_im, b_re, b_im, g_ab_re, g_ab_im, g_bb_re, g_bb_im):
    dt = jnp.exp(logdt)
    mag = jnp.exp(a_re * dt)
    th = a_im * dt
    cs, sn = jnp.cos(th), jnp.sin(th)
    ab_re, ab_im = mag * cs, mag * sn
    den = jnp.square(a_re) + jnp.square(a_im)
    nr = ab_re - 1.0
    f_re = (nr * a_re + ab_im * a_im) / den
    f_im = (ab_im * a_re - nr * a_im) / den
    d_f_re = jnp.sum(g_bb_re * b_re + g_bb_im * b_im, axis=0)
    d_f_im = jnp.sum(g_bb_im * b_re - g_bb_re * b_im, axis=0)
    d_b_re = g_bb_re * f_re[None] + g_bb_im * f_im[None]
    d_b_im = g_bb_im * f_re[None] - g_bb_re * f_im[None]
    d_n_re, d_n_im = d_f_re / den, d_f_im / den
    d_den = -(d_f_re * f_re + d_f_im * f_im) / den
    d_ab_re = g_ab_re + d_n_re * a_re - d_n_im * a_im
    d_ab_im = g_ab_im + d_n_re * a_im + d_n_im * a_re
    d_a_re = d_n_re * nr + d_n_im * ab_im + 2.0 * d_den * a_re
    d_a_im = d_n_re * ab_im - d_n_im * nr + 2.0 * d_den * a_im
    d_mag = d_ab_re * cs + d_ab_im * sn
    d_th = mag * (d_ab_im * cs - d_ab_re * sn)
    d_a_re = d_a_re + d_mag * mag * dt
    d_a_im = d_a_im + d_th * dt
    d_dt = jnp.sum(d_mag * mag * a_re + d_th * a_im, axis=-1, keepdims=True)
    return d_dt * dt, d_a_re, d_a_im, d_b_re, d_b_im


def _whole(fn, args, out_shapes, *, name):
    n_in = len(args)

    def body(*refs):
        res = fn(*[r[...] for r in refs[:n_in]])
        for o_ref, val in zip(refs[n_in:], res):
            o_ref[...] = val

    return pl.pallas_call(body, name=name, out_shape=[jax.ShapeDtypeStruct(s, F32) for s in out_shapes],
                          compiler_params=pltpu.CompilerParams(vmem_limit_bytes=VMEM_LIMIT_BYTES))(*args)


def _tiles_cn(t):
    t = t.reshape(SSM_TILES, GROUPS_PER_TILE, SSM_GROUP, SSM_STATE)
    eye = jnp.eye(GROUPS_PER_TILE, dtype=t.dtype)
    return (t[:, :, :, None, :] * eye[None, :, None, :, None]).reshape(SSM_TILES, LANES, GROUPS_PER_TILE * SSM_STATE)


def _tiles_nc(t):
    t = t.reshape(SSM_TILES, GROUPS_PER_TILE, SSM_STATE, SSM_GROUP)
    eye = jnp.eye(GROUPS_PER_TILE, dtype=t.dtype)
    return (t[:, :, :, None, :] * eye[None, :, None, :, None]).reshape(SSM_TILES, GROUPS_PER_TILE * SSM_STATE, LANES)


def _untile_cn(t):
    t = t.reshape(SSM_TILES, GROUPS_PER_TILE, SSM_GROUP, GROUPS_PER_TILE, SSM_STATE)
    eye = jnp.eye(GROUPS_PER_TILE, dtype=t.dtype)
    return jnp.sum(t * eye[None, :, None, :, None], axis=3).reshape(SSM_GROUPS, SSM_GROUP, SSM_STATE)


def _bd_expand(xsrc, t1, t2, *, name, tm=512):
    n_rows = xsrc.shape[0]
    wide = GROUPS_PER_TILE * SSM_STATE

    def body(x_ref, t1_ref, t2_ref, o1_ref, o2_ref):
        xv = x_ref[...]
        o1_ref[...] = _dot(xv, t1_ref[0], 1, 0)
        o2_ref[...] = _dot(xv, t2_ref[0], 1, 0)

    tile = pl.BlockSpec((1, LANES, wide), lambda j, i: (j, 0, 0))
    out = pl.BlockSpec((tm, wide), lambda j, i: (i, j))
    shape = jax.ShapeDtypeStruct((n_rows, SSM_CH), F32)
    return pl.pallas_call(
        body, name=name, grid=(SSM_TILES, n_rows // tm),
        in_specs=[pl.BlockSpec((tm, LANES), lambda j, i: (i, j)), tile, tile], out_specs=[out, out],
        out_shape=[shape, shape], compiler_params=_params("parallel", "parallel"),
    )(xsrc, t1, t2)


def _bd_contract(s1, s2, t1, t2, gain, xsrc, *, forward, name, tm=512):
    n_rows = s1.shape[0]
    wide = GROUPS_PER_TILE * SSM_STATE

    def body(s1_ref, s2_ref, t1_ref, t2_ref, g_ref, x_ref, o1_ref, o2_ref):
        r = _dot(s1_ref[...], t1_ref[0], 1, 0) + _dot(s2_ref[...], t2_ref[0], 1, 0) + g_ref[...] * x_ref[...]
        if forward:
            o1_ref[...] = r
            o2_ref[...] = jax.nn.gelu(r).astype(o2_ref.dtype)
        else:
            o1_ref[...] = r.astype(o1_ref.dtype)

            @pl.when(pl.program_id(1) == 0)
            def _():
                o2_ref[...] = jnp.zeros_like(o2_ref)

            o2_ref[...] += _colsum(r)

    state = pl.BlockSpec((tm, wide), lambda j, i: (i, j))
    tile = pl.BlockSpec((1, wide, LANES), lambda j, i: (j, 0, 0))
    chan = pl.BlockSpec((tm, LANES), lambda j, i: (i, j))
    vec = pl.BlockSpec((1, LANES), lambda j, i: (0, j))
    if forward:
        out_specs = [chan, chan]
        out_shape = [jax.ShapeDtypeStruct((n_rows, SSM_WIDTH), F32), jax.ShapeDtypeStruct((n_rows, SSM_WIDTH), MXU_DTYPE)]
    else:
        out_specs = [chan, vec]
        out_shape = [jax.ShapeDtypeStruct((n_rows, SSM_WIDTH), MXU_DTYPE), jax.ShapeDtypeStruct((1, SSM_WIDTH), F32)]
    return pl.pallas_call(
        body, name=name, grid=(SSM_TILES, n_rows // tm), in_specs=[state, state, tile, tile, vec, chan],
        out_specs=out_specs, out_shape=out_shape, compiler_params=_params("parallel", "arbitrary"),
    )(s1, s2, t1, t2, gain, xsrc)


def _bd_outer(xsrc, s1, s2, *, name, tk=512):
    n_rows = s1.shape[0]
    wide = GROUPS_PER_TILE * SSM_STATE

    def body(x_ref, s1_ref, s2_ref, o1_ref, o2_ref):
        @pl.when(pl.program_id(1) == 0)
        def _():
            o1_ref[...] = jnp.zeros_like(o1_ref)
            o2_ref[...] = jnp.zeros_like(o2_ref)

        xv = x_ref[...]
        o1_ref[0] += _dot(xv, s1_ref[...], 0, 0)
        o2_ref[0] += _dot(xv, s2_ref[...], 0, 0)

    state = pl.BlockSpec((tk, wide), lambda j, kk: (kk, j))
    tile = pl.BlockSpec((1, LANES, wide), lambda j, kk: (j, 0, 0))
    shape = jax.ShapeDtypeStruct((SSM_TILES, LANES, wide), F32)
    return pl.pallas_call(
        body, name=name, grid=(SSM_TILES, n_rows // tk),
        in_specs=[pl.BlockSpec((tk, LANES), lambda j, kk: (kk, j)), state, state], out_specs=[tile, tile],
        out_shape=[shape, shape], compiler_params=_params("parallel", "arbitrary"),
    )(xsrc, s1, s2)


def _scan_rows(t):
    return pl.ds(pl.multiple_of(t * STATE_VREG_ROWS, 8), STATE_VREG_ROWS)


def _scan_fwd(w_re, w_im, a_re, a_im, *, name, tc=128):
    n_rows = w_re.shape[0]

    def body(wr_ref, wi_ref, ar_ref, ai_ref, hr_ref, hi_ref, sr_ref, si_ref):
        @pl.when(pl.program_id(0) == 0)
        def _():
            sr_ref[...] = jnp.zeros_like(sr_ref)
            si_ref[...] = jnp.zeros_like(si_ref)

        ar, ai = ar_ref[...], ai_ref[...]

        def step(t, carry):
            hr, hi = carry
            rows = _scan_rows(t)
            nr = ar * hr - ai * hi + wr_ref[rows, :]
            ni = ar * hi + ai * hr + wi_ref[rows, :]
            hr_ref[rows, :] = nr
            hi_ref[rows, :] = ni
            return nr, ni

        hr, hi = lax.fori_loop(0, tc, step, (sr_ref[...], si_ref[...]), unroll=4)
        sr_ref[...] = hr
        si_ref[...] = hi

    blk = pl.BlockSpec((tc * STATE_VREG_ROWS, LANES), lambda i: (i, 0))
    coef = pl.BlockSpec((STATE_VREG_ROWS, LANES), lambda i: (0, 0))
    shape = jax.ShapeDtypeStruct((n_rows * STATE_VREG_ROWS, LANES), F32)
    h_re, h_im = pl.pallas_call(
        body, name=name, grid=(n_rows // tc,), in_specs=[blk, blk, coef, coef], out_specs=[blk, blk],
        out_shape=[shape, shape], scratch_shapes=[pltpu.VMEM((STATE_VREG_ROWS, LANES), F32)] * 2,
        compiler_params=_params("arbitrary"),
    )(w_re.reshape(-1, LANES), w_im.reshape(-1, LANES), a_re, a_im)
    return h_re.reshape(n_rows, SSM_CH), h_im.reshape(n_rows, SSM_CH)


def _scan_bwd(dh_re, dh_im, h_re, h_im, a_re, a_im, *, name, tc=128):
    n_rows = dh_re.shape[0]
    n_chunk = n_rows // tc

    def body(gr_ref, gi_ref, hr_ref, hi_ref, ar_ref, ai_ref, lr_ref, li_ref, dar_ref, dai_ref, s_ref):
        @pl.when(pl.program_id(0) == 0)
        def _():
            s_ref[...] = jnp.zeros_like(s_ref)

        ar, ai = ar_ref[...], ai_ref[...]

        def step(kk, carry):
            lr, li, dar, dai = carry
            rows = _scan_rows(tc - 1 - kk)
            hr, hi = hr_ref[rows, :], hi_ref[rows, :]
            dar = dar + lr * hr + li * hi
            dai = dai + li * hr - lr * hi
            nlr = gr_ref[rows, :] + ar * lr + ai * li
            nli = gi_ref[rows, :] + ar * li - ai * lr
            lr_ref[rows, :] = nlr
            li_ref[rows, :] = nli
            return nlr, nli, dar, dai

        lr, li, dar, dai = lax.fori_loop(0, tc, step, (s_ref[0], s_ref[1], s_ref[2], s_ref[3]), unroll=4)
        s_ref[0], s_ref[1], s_ref[2], s_ref[3] = lr, li, dar, dai
        dar_ref[...] = dar
        dai_ref[...] = dai

    blk = pl.BlockSpec((tc * STATE_VREG_ROWS, LANES), lambda i: (n_chunk - 1 - i, 0))
    coef = pl.BlockSpec((STATE_VREG_ROWS, LANES), lambda i: (0, 0))
    shape = jax.ShapeDtypeStruct((n_rows * STATE_VREG_ROWS, LANES), F32)
    cshape = jax.ShapeDtypeStruct((STATE_VREG_ROWS, LANES), F32)
    lam_re, lam_im, da_re, da_im = pl.pallas_call(
        body, name=name, grid=(n_chunk,), in_specs=[blk, blk, blk, blk, coef, coef], out_specs=[blk, blk, coef, coef],
        out_shape=[shape, shape, cshape, cshape], scratch_shapes=[pltpu.VMEM((4, STATE_VREG_ROWS, LANES), F32)],
        compiler_params=_params("arbitrary"),
    )(dh_re.reshape(-1, LANES), dh_im.reshape(-1, LANES), h_re.reshape(-1, LANES), h_im.reshape(-1, LANES), a_re, a_im)
    return lam_re.reshape(n_rows, SSM_CH), lam_im.reshape(n_rows, SSM_CH), da_re, da_im


def _local_grads(x, mem, pos_col, target, wts, sm):
    row = lambda v: v.reshape(1, -1)
    b_re_t = sm["ssm_b_re"].transpose(2, 0, 1)
    b_im_t = sm["ssm_b_im"].transpose(2, 0, 1)
    logdt = sm["ssm_log_dt"].reshape(SSM_GROUPS, 1)
    c_re, c_im = sm["ssm_c_re"], sm["ssm_c_im"]
    grp = (SSM_GROUPS, SSM_STATE)
    chn = (SSM_GROUP, SSM_GROUPS, SSM_STATE)

    cos_t, sin_t = _rope_tables(pos_col, name="rope_tables")
    h0, xh0, rs0 = _ln_fwd(x, None, sm["ln_in_g"], sm["ln_in_b"], alpha=1.0, name="ln_in_fwd")
    proj = _mm(h0, wts["w_in"], bias=sm["b_in"], name="in_proj")

    disc_in = (logdt, sm["ssm_a_re"], sm["ssm_a_im"], b_re_t, b_im_t)
    ab_re, ab_im, bb_re_t, bb_im_t = _whole(_disc, disc_in, [grp, grp, chn, chn], name="ssm_disc")
    a_re_rows, a_im_rows = ab_re.reshape(STATE_VREG_ROWS, LANES), ab_im.reshape(STATE_VREG_ROWS, LANES)
    w_re, w_im = _bd_expand(proj, _tiles_cn(bb_re_t.transpose(1, 0, 2)), _tiles_cn(bb_im_t.transpose(1, 0, 2)),
                            name="ssm_in")
    h_re, h_im = _scan_fwd(w_re, w_im, a_re_rows, a_im_rows, name="ssm_scan")
    y, gy = _bd_contract(h_re, h_im, _tiles_nc(c_re.transpose(0, 2, 1)), _tiles_nc(-c_im.transpose(0, 2, 1)),
                         sm["ssm_d"], proj, forward=True, name="ssm_out")
    z = _mm(gy, wts["w_glu"], bias=sm["b_glu"], name="glu_proj")

    q, k, v = _qkv_split(proj, cos_t, sin_t, name="qkv_split")
    outs, lses = [], []
    for g, dil in enumerate(DILATIONS):
        o_g, l_g = _dil_fwd(q[g], k[g], v[g], dil, name=f"dil_att_fwd_{dil}")
        outs.append(o_g)
        lses.append(l_g)
    att, lse = _att_merge(outs, lses, name="att_merge")
    b_att = _mm(att, wts["w_att_up"], name="att_up")

    mixed = _mix_fwd(proj, z, b_att, name="gate_mix")
    mix_out = _mm(mixed, wts["w_mix_out"], bias=sm["b_mix_out"], name="mix_out")
    h1, xh1, rs1 = _ln_fwd(h0, mix_out, sm["ln1_g"], sm["ln1_b"], alpha=DEEPNORM_ALPHA, name="ln1_fwd")

    xq = _mm(h1, wts["w_xq"], out_dtype=MXU_DTYPE, name="xatt_q")
    kv = _mm(mem, wts["w_xkv"], out_dtype=MXU_DTYPE, name="xatt_kv")
    xo_in = _xatt_fwd(xq, kv, name="xatt_fwd")
    xo = _mm(xo_in, wts["w_xo"], name="xatt_o")
    h2, xh2, rs2 = _ln_fwd(h1, xo, sm["ln2_g"], sm["ln2_b"], alpha=DEEPNORM_ALPHA, name="ln2_fwd")

    pre = _mm(h2, wts["w_ff1"], bias=sm["b_ff1"], name="ff1")
    act = _relu2_fwd(pre, name="relu2")
    ff = _mm(act, wts["w_ff2"], bias=sm["b_ff2"], name="ff2")
    h3, xh3, rs3 = _ln_fwd(h2, ff, sm["ln3_g"], sm["ln3_b"], alpha=DEEPNORM_ALPHA, name="ln3_fwd")
    dh3, loss_row = _loss_head(h3, target, name="loss_head")

    gw, gs = {}, {}
    dr3, gs["ln3_g"], gs["ln3_b"], gs["b_ff2"] = _ln_bwd(None, dh3, xh3, rs3, sm["ln3_g"], alpha=1.0, name="ln3_bwd")
    gw["w_ff2"] = _mm(act, dr3, ta=True, name="ff2_dw")
    dact = _mm(dr3, wts["w_ff2"], tb=True, name="ff2_dx")
    dpre, gs["b_ff1"] = _relu2_bwd(dact, pre, name="relu2_bwd")
    gw["w_ff1"] = _mm(h2, dpre, ta=True, name="ff1_dw")
    dh2 = _mm(dpre, wts["w_ff1"], tb=True, name="ff1_dx")

    dr2, gs["ln2_g"], gs["ln2_b"], _ = _ln_bwd(dr3, dh2, xh2, rs2, sm["ln2_g"], alpha=DEEPNORM_ALPHA, name="ln2_bwd")
    gw["w_xo"] = _mm(xo_in, dr2, ta=True, name="xatt_o_dw")
    dxo_in = _mm(dr2, wts["w_xo"], tb=True, out_dtype=MXU_DTYPE, name="xatt_o_dx")
    dxq, dkv = _xatt_bwd(xq, kv, dxo_in, name="xatt_bwd")
    gw["w_xq"] = _mm(h1, dxq, ta=True, name="xatt_q_dw")
    gw["w_xkv"] = _mm(mem, dkv, ta=True, name="xatt_kv_dw")
    dh1 = _mm(dxq, wts["w_xq"], tb=True, name="xatt_q_dx")

    dr1, gs["ln1_g"], gs["ln1_b"], gs["b_mix_out"] = _ln_bwd(dr2, dh1, xh1, rs1, sm["ln1_g"], alpha=DEEPNORM_ALPHA,
                                                             name="ln1_bwd")
    gw["w_mix_out"] = _mm(mixed, dr1, ta=True, name="mix_out_dw")
    dmixed = _mm(dr1, wts["w_mix_out"], tb=True, name="mix_out_dx")
    dgs, dga, dz1, dz2, db_att, s_gs, s_ga, s_z1, s_z2 = _mix_bwd(dmixed, proj, z, b_att, name="gate_mix_bwd")
    dz = jnp.concatenate([dz1, dz2], axis=1)
    gs["b_glu"] = jnp.concatenate([s_z1, s_z2], axis=1)

    gw["w_att_up"] = _mm(att, db_att, ta=True, name="att_up_dw")
    datt = _mm(db_att, wts["w_att_up"], tb=True, name="att_up_dx")
    dqkv, sqkv = [], []
    for g, dil in enumerate(DILATIONS):
        d_g, s_g = _dil_bwd(q[g], k[g], v[g], datt, att, lse, cos_t, sin_t, dil, name=f"dil_att_bwd_{dil}")
        dqkv.append(d_g)
        sqkv.append(s_g)

    gw["w_glu"] = _mm(gy, dz, ta=True, name="glu_dw")
    dgy = _mm(dz, wts["w_glu"], tb=True, name="glu_dx")
    dy, gs["ssm_d"] = _gelu_bwd(dgy, y, proj, name="gelu_bwd")
    dh_re, dh_im = _bd_expand(dy, _tiles_cn(c_re), _tiles_cn(-c_im), name="ssm_out_dh")
    lam_re, lam_im, da_re, da_im = _scan_bwd(dh_re, dh_im, h_re, h_im, a_re_rows, a_im_rows, name="ssm_scan_bwd")
    dc_re_t, dc_im_t = _bd_outer(dy, h_re, h_im, name="ssm_out_dc")
    gs["ssm_c_re"], gs["ssm_c_im"] = _untile_cn(dc_re_t), -_untile_cn(dc_im_t)
    dbb_re_t, dbb_im_t = _bd_outer(proj, lam_re, lam_im, name="ssm_in_db")
    du, s_u = _bd_contract(lam_re, lam_im, _tiles_nc(bb_re_t.transpose(1, 2, 0)), _tiles_nc(bb_im_t.transpose(1, 2, 0)),
                           sm["ssm_d"], dy, forward=False, name="ssm_in_du")
    disc_ct = (da_re.reshape(grp), da_im.reshape(grp), _untile_cn(dbb_re_t).transpose(1, 0, 2),
               _untile_cn(dbb_im_t).transpose(1, 0, 2))
    d_logdt, gs["ssm_a_re"], gs["ssm_a_im"], d_b_re_t, d_b_im_t = _whole(
        _disc_transpose, disc_in + disc_ct, [(SSM_GROUPS, 1), grp, grp, chn, chn], name="ssm_disc_bwd")
    gs["ssm_log_dt"] = d_logdt
    gs["ssm_b_re"], gs["ssm_b_im"] = d_b_re_t.transpose(1, 2, 0), d_b_im_t.transpose(1, 2, 0)

    dproj = jnp.concatenate([du] + [dqkv[g][i] for i in range(3) for g in range(len(DILATIONS))] + [dgs, dga], axis=1)
    gs["b_in"] = jnp.concatenate([s_u] + [sqkv[g][i] for i in range(3) for g in range(len(DILATIONS))] + [s_gs, s_ga],
                                 axis=1)
    gw["w_in"] = _mm(h0, dproj, ta=True, name="in_proj_dw")
    dh0 = _mm(dproj, wts["w_in"], tb=True, name="in_proj_dx")
    grad_x, gs["ln_in_g"], gs["ln_in_b"], _ = _ln_bwd(dr1, dh0, xh0, rs0, sm["ln_in_g"], alpha=DEEPNORM_ALPHA,
                                                      name="ln_in_bwd")
    return loss_row, grad_x, gw, gs


def _exchange(src, *, scatter, name):
    blk = src.shape[1:] if scatter else src.shape

    def body(src_ref, out_ref, send_sems, recv_sems, local_sem):
        x, y, c = lax.axis_index("x"), lax.axis_index("y"), lax.axis_index("c")
        me = 4 * x + 2 * y + c

        def slot(dev):
            return src_ref.at[dev] if scatter else src_ref

        local = pltpu.make_async_copy(slot(me), out_ref.at[me], local_sem)
        local.start()
        copies = []
        for kk in range(1, N_DEV):
            px = (x + (kk >> 2)) % 2
            py = (y + ((kk >> 1) & 1)) % 2
            pc = (c + (kk & 1)) % 2
            peer = 4 * px + 2 * py + pc
            cp = pltpu.make_async_remote_copy(
                src_ref=slot(peer), dst_ref=out_ref.at[me], send_sem=send_sems.at[kk - 1], recv_sem=recv_sems.at[kk - 1],
                device_id=(px, py, pc), device_id_type=pl.DeviceIdType.MESH)
            cp.start()
            copies.append(cp)
        for cp in copies:
            cp.wait()
        local.wait()

    return pl.pallas_call(
        body, name=name, out_shape=jax.ShapeDtypeStruct((N_DEV,) + tuple(blk), src.dtype),
        in_specs=[pl.BlockSpec(memory_space=pl.ANY)], out_specs=pl.BlockSpec(memory_space=pl.ANY),
        scratch_shapes=[pltpu.SemaphoreType.DMA((N_DEV - 1,)), pltpu.SemaphoreType.DMA((N_DEV - 1,)),
                        pltpu.SemaphoreType.DMA],
    )(src)


def _reduce_adamw(gstack, w, m, v, *, name, tr=128):
    n_rows, cols = w.shape
    tr = min(tr, n_rows)
    assert n_rows % tr == 0, (name, n_rows, tr)

    def body(g_ref, w_ref, m_ref, v_ref, go_ref, d_ref, mo_ref, vo_ref):
        g = g_ref[0]
        for dev in range(1, N_DEV):
            g = g + g_ref[dev]
        m_new = ADAM_B1 * m_ref[...] + (1.0 - ADAM_B1) * g
        v_new = ADAM_B2 * v_ref[...] + (1.0 - ADAM_B2) * jnp.square(g)
        m_hat = m_new / (1.0 - ADAM_B1 ** ADAM_STEP)
        v_hat = v_new / (1.0 - ADAM_B2 ** ADAM_STEP)
        go_ref[...] = g
        d_ref[...] = -ADAM_LR * (m_hat / (jnp.sqrt(v_hat) + ADAM_EPS) + ADAM_WD * w_ref[...])
        mo_ref[...] = m_new
        vo_ref[...] = v_new

    flat = pl.BlockSpec((tr, cols), lambda i: (i, 0))
    shape = jax.ShapeDtypeStruct((n_rows, cols), F32)
    return pl.pallas_call(
        body, name=name, grid=(n_rows // tr,),
        in_specs=[pl.BlockSpec((N_DEV, tr, cols), lambda i: (0, i, 0)), flat, flat, flat],
        out_specs=[flat] * 4, out_shape=[shape] * 4, compiler_params=_params("parallel"),
    )(gstack, w, m, v)


def _pack(parts, dtype):
    flat = jnp.concatenate([p.reshape(-1).astype(dtype) for p in parts])
    unit = PACK_COLS * PACK_ROW_ALIGN
    total = -(-flat.shape[0] // unit) * unit
    return jnp.pad(flat, (0, total - flat.shape[0])).reshape(-1, PACK_COLS)


def _unpack(packed, shapes):
    flat = packed.reshape(-1)
    out, off = [], 0
    for s in shapes:
        size = int(np.prod(s))
        out.append(flat[off:off + size].reshape(s))
        off += size
    return out


def _gather_big(local):
    shapes = [local[n].shape for n in BIG]
    gathered = _exchange(_pack([local[n] for n in BIG], MXU_DTYPE), scatter=False, name="gather_weights")
    per_dev = [_unpack(gathered[dev], shapes) for dev in range(N_DEV)]
    full = {}
    for i, n in enumerate(BIG):
        full[n] = jnp.concatenate([per_dev[dev][i] for dev in range(N_DEV)], axis=1 if n in BIG_COL_SHARDED else 0)
    return full


def _scatter_big(gw, local_shapes):
    slots = []
    for dev in range(N_DEV):
        parts = []
        for n in BIG:
            r, c = local_shapes[n]
            parts.append(gw[n][:, dev * c:(dev + 1) * c] if n in BIG_COL_SHARDED else gw[n][dev * r:(dev + 1) * r, :])
        slots.append(_pack(parts, F32))
    return jnp.stack(slots)


def kernel(x, mem, positions, ln_in_g, ln_in_b, w_in, b_in, ssm_log_dt, ssm_a_re, ssm_a_im, ssm_b_re, ssm_b_im, ssm_c_re, ssm_c_im, ssm_d, w_glu, b_glu, w_att_up, w_mix_out, b_mix_out, ln1_g, ln1_b, w_xq, w_xkv, w_xo, ln2_g, ln2_b, w_ff1, b_ff1, w_ff2, b_ff2, ln3_g, ln3_b, loss_target, m_ln_in_g, m_ln_in_b, m_w_in, m_b_in, m_ssm_log_dt, m_ssm_a_re, m_ssm_a_im, m_ssm_b_re, m_ssm_b_im, m_ssm_c_re, m_ssm_c_im, m_ssm_d, m_w_glu, m_b_glu, m_w_att_up, m_w_mix_out, m_b_mix_out, m_ln1_g, m_ln1_b, m_w_xq, m_w_xkv, m_w_xo, m_ln2_g, m_ln2_b, m_w_ff1, m_b_ff1, m_w_ff2, m_b_ff2, m_ln3_g, m_ln3_b, v_ln_in_g, v_ln_in_b, v_w_in, v_b_in, v_ssm_log_dt, v_ssm_a_re, v_ssm_a_im, v_ssm_b_re, v_ssm_b_im, v_ssm_c_re, v_ssm_c_im, v_ssm_d, v_w_glu, v_b_glu, v_w_att_up, v_w_mix_out, v_b_mix_out, v_ln1_g, v_ln1_b, v_w_xq, v_w_xkv, v_w_xo, v_ln2_g, v_ln2_b, v_w_ff1, v_b_ff1, v_w_ff2, v_b_ff2, v_ln3_g, v_ln3_b):
    given = dict(locals())
    w_arg = {n: given[n] for n in WEIGHTS}
    m_arg = {n: given["m_" + n] for n in WEIGHTS}
    v_arg = {n: given["v_" + n] for n in WEIGHTS}

    local_big = {n: w_arg[n][0] for n in BIG}
    local_shapes = {n: local_big[n].shape for n in BIG}
    wts = _gather_big(local_big)

    sm = {}
    for n in SMALL:
        t = w_arg[n]
        if n.startswith("ssm_") and n not in ("ssm_d", "ssm_log_dt"):
            sm[n] = t[0]
        else:
            sm[n] = t.reshape(1, -1)

    loss_row, grad_x, gw, gs = _local_grads(x[0], mem[0], positions.reshape(-1, 1), loss_target[0], wts, sm)
    loss = lax.psum(loss_row[0, 0], ("x", "y", "c"))

    big_stack = _exchange(_scatter_big(gw, local_shapes), scatter=True, name="scatter_grads")
    small_stack = _exchange(_pack([gs[n] for n in SMALL], F32), scatter=False, name="gather_small_grads")

    def update(names, stack, name):
        shapes = [w_arg[n].shape for n in names]
        res = _reduce_adamw(stack, *[_pack([d[n] for n in names], F32) for d in (w_arg, m_arg, v_arg)], name=name)
        return [dict(zip(names, _unpack(r, shapes))) for r in res]

    big_res = update(BIG, big_stack, "adamw_big")
    small_res = update(SMALL, small_stack, "adamw_small")
    merged = [{**b, **s} for b, s in zip(big_res, small_res)]
    out = [loss, grad_x[None]]
    for res in merged:
        out += [res[n] for n in WEIGHTS]
    return tuple(out)
```

```python
import functools

import numpy as np
import jax
import jax.numpy as jnp
from jax import lax
from jax.experimental import pallas as pl
from jax.experimental.pallas import tpu as pltpu

F32 = jnp.float32
MXU_DTYPE = jnp.bfloat16
WIRE_DTYPE = jnp.bfloat16
VMEM_LIMIT_BYTES = 48 * 1024 * 1024
LANES = 128

N_DEV = 8
D_MODEL = 1024
SSM_GROUP = 16
SSM_WIDTH = 768
SSM_GROUPS = SSM_WIDTH // SSM_GROUP
SSM_STATE = 64
SSM_CH = SSM_GROUPS * SSM_STATE
SSM_TILES = SSM_WIDTH // LANES
GROUPS_PER_TILE = LANES // SSM_GROUP
STATE_VREG_ROWS = SSM_CH // LANES
ATT_HEAD_DIM = 64
ATT_HEADS_PER_GROUP = 4
ATT_MERGED = ATT_HEADS_PER_GROUP * ATT_HEAD_DIM
DILATIONS = (1, 4, 16)
ATT_BLK = 128
ATT_SCALE = ATT_HEAD_DIM ** -0.5
ROT_DIM = ATT_HEAD_DIM // 4
ROPE_THETA = 500000.0
XATT_HEADS = 4
XATT_HEAD_DIM = D_MODEL // XATT_HEADS
XATT_SCALE = XATT_HEAD_DIM ** -0.5
DEEPNORM_ALPHA = 2.0 ** 0.25
LN_EPS = 1e-5
NEG_INF = -1e30
OFF_Q_BLK, OFF_K_BLK, OFF_V_BLK = 3, 6, 9
OFF_GS_BLK, OFF_GA_BLK = 3, 4

ADAM_LR = 0.001
ADAM_B1 = 0.9
ADAM_B2 = 0.999
ADAM_EPS = 1e-08
ADAM_WD = 0.01
ADAM_STEP = 10

BIG = ("w_in", "w_glu", "w_att_up", "w_mix_out", "w_xq", "w_xkv", "w_xo", "w_ff1", "w_ff2")
BIG_COL_SHARDED = ("w_in", "w_glu", "w_att_up", "w_xkv", "w_ff1")
WEIGHTS = ("ln_in_g", "ln_in_b", "w_in", "b_in", "ssm_log_dt", "ssm_a_re", "ssm_a_im", "ssm_b_re", "ssm_b_im",
           "ssm_c_re", "ssm_c_im", "ssm_d", "w_glu", "b_glu", "w_att_up", "w_mix_out", "b_mix_out", "ln1_g", "ln1_b",
           "w_xq", "w_xkv", "w_xo", "ln2_g", "ln2_b", "w_ff1", "b_ff1", "w_ff2", "b_ff2", "ln3_g", "ln3_b")
SMALL = tuple(n for n in WEIGHTS if n not in BIG)
PACK_COLS = 1024
PACK_ROW_ALIGN = 256


def _params(*sem):
    return pltpu.CompilerParams(dimension_semantics=sem, vmem_limit_bytes=VMEM_LIMIT_BYTES)


def _dot(a, b, ca, cb):
    return lax.dot_general(a.astype(MXU_DTYPE), b.astype(MXU_DTYPE), (((ca,), (cb,)), ((), ())),
                           preferred_element_type=F32)


def _fit(dim, pref):
    if dim <= pref:
        return dim
    best = max(t for t in range(LANES, pref + 1, LANES) if dim % t == 0)
    return best


def _mm(a, b, *, name, ta=False, tb=False, bias=None, out_dtype=F32, b_shards=False, out_shards=False,
        tm=512, tn=1024, tk=1024):
    m, k = (a.shape[1], a.shape[0]) if ta else a.shape
    if b_shards:
        n_sh, rows, n_loc = b.shape
        if tb:
            n, tn, tk = rows, _fit(rows, tn), n_loc
            assert k == n_sh * n_loc, (name, k, b.shape)
            b_spec = pl.BlockSpec((1, tn, tk), lambda i, j, kk: (kk, j, 0))
        else:
            n, tn, tk = n_sh * n_loc, n_loc, _fit(k, tk)
            b_spec = pl.BlockSpec((1, tk, tn), lambda i, j, kk: (j, kk, 0))
    else:
        n = b.shape[0] if tb else b.shape[1]
        tn = n // N_DEV if out_shards else _fit(n, tn)
        tk = _fit(k, tk)
        b_spec = pl.BlockSpec((tn, tk), lambda i, j, kk: (j, kk)) if tb else pl.BlockSpec((tk, tn), lambda i, j, kk: (kk, j))
    tm = _fit(m, tm)
    nk = k // tk
    a_spec = pl.BlockSpec((tk, tm), lambda i, j, kk: (kk, i)) if ta else pl.BlockSpec((tm, tk), lambda i, j, kk: (i, kk))
    in_specs, args = [a_spec, b_spec], [a, b]
    if bias is not None:
        in_specs.append(pl.BlockSpec((1, tn), lambda i, j, kk: (0, j)))
        args.append(bias)
    if out_shards:
        assert n == N_DEV * tn, (name, n, tn)
        out_spec = pl.BlockSpec((1, tm, tn), lambda i, j, kk: (j, i, 0))
        out_shape = jax.ShapeDtypeStruct((N_DEV, m, tn), out_dtype)
    else:
        out_spec = pl.BlockSpec((tm, tn), lambda i, j, kk: (i, j))
        out_shape = jax.ShapeDtypeStruct((m, n), out_dtype)

    def body(*refs):
        a_ref, b_ref = refs[0], refs[1]
        o_ref = refs[3] if bias is not None else refs[2]

        def product():
            return _dot(a_ref[...], b_ref[0] if b_shards else b_ref[...], 0 if ta else 1, 1 if tb else 0)

        def finish(r):
            if bias is not None:
                r = r + refs[2][...]
            r = r.astype(o_ref.dtype)
            if out_shards:
                o_ref[0] = r
            else:
                o_ref[...] = r

        if nk == 1:
            finish(product())
            return
        acc_ref = refs[-1]
        kk = pl.program_id(2)

        @pl.when(kk == 0)
        def _():
            acc_ref[...] = jnp.zeros_like(acc_ref)

        acc_ref[...] += product()

        @pl.when(kk == nk - 1)
        def _():
            finish(acc_ref[...])

    return pl.pallas_call(
        body, name=name, grid=(m // tm, n // tn, nk), in_specs=in_specs, out_specs=out_spec, out_shape=out_shape,
        scratch_shapes=[pltpu.VMEM((tm, tn), F32)] if nk > 1 else [],
        compiler_params=_params("parallel", "parallel", "arbitrary"),
    )(*args)


def _rowcall(fn, rows, fulls, row_outs, acc_outs=(), *, n_rows, tm, name):
    n_r, n_f, n_o, n_a = len(rows), len(fulls), len(row_outs), len(acc_outs)
    assert n_rows % tm == 0, (name, n_rows, tm)

    def body(*refs):
        res = fn(*[r[...] for r in refs[:n_r + n_f]])
        res = tuple(res) if isinstance(res, (tuple, list)) else (res,)
        o_refs = refs[n_r + n_f:n_r + n_f + n_o]
        a_refs = refs[n_r + n_f + n_o:]
        for o_ref, val in zip(o_refs, res[:n_o]):
            o_ref[...] = val.astype(o_ref.dtype)
        if n_a:
            @pl.when(pl.program_id(0) == 0)
            def _():
                for a_ref in a_refs:
                    a_ref[...] = jnp.zeros_like(a_ref)

            for a_ref, val in zip(a_refs, res[n_o:]):
                a_ref[...] += val

    in_specs = [pl.BlockSpec((tm, w), functools.partial(lambda i, cb: (i, cb), cb=cb)) for _, w, cb in rows]
    in_specs += [pl.BlockSpec(f.shape, functools.partial(lambda i, nd: (0,) * nd, nd=f.ndim)) for f in fulls]
    out_specs = [pl.BlockSpec((tm, w), lambda i: (i, 0)) for w, _ in row_outs]
    out_specs += [pl.BlockSpec((1, w), lambda i: (0, 0)) for w in acc_outs]
    out_shape = [jax.ShapeDtypeStruct((n_rows, w), dt) for w, dt in row_outs]
    out_shape += [jax.ShapeDtypeStruct((1, w), F32) for w in acc_outs]
    return pl.pallas_call(
        body, name=name, grid=(n_rows // tm,), in_specs=in_specs, out_specs=out_specs, out_shape=out_shape,
        compiler_params=_params("arbitrary" if n_a else "parallel"),
    )(*[r[0] for r in rows], *fulls)


def _colsum(v):
    return jnp.sum(v, axis=0, keepdims=True)


def _ln_fwd(a, r, g, b, *, alpha, name):
    n_rows, d = a.shape

    def fn(*t):
        xin = t[0] if alpha == 1.0 else alpha * t[0]
        if r is not None:
            xin = xin + t[1]
        gv, bv = t[-2], t[-1]
        mu = jnp.mean(xin, axis=-1, keepdims=True)
        xc = xin - mu
        var = jnp.mean(xc * xc, axis=-1, keepdims=True)
        rstd = lax.rsqrt(var + LN_EPS)
        xh = xc * rstd
        return xh * gv + bv, xh, rstd

    rows = [(a, d, 0)] + ([(r, d, 0)] if r is not None else [])
    return _rowcall(fn, rows, [g, b], [(d, F32), (d, F32), (1, F32)], n_rows=n_rows, tm=256, name=name)


def _ln_bwd(dya, dyb, xh, rstd, g, *, alpha, name):
    n_rows, d = xh.shape

    def fn(*t):
        if dya is not None:
            dy = alpha * t[0] + t[1]
            xhv, rs, gv = t[2], t[3], t[4]
        else:
            dy, xhv, rs, gv = t[0], t[1], t[2], t[3]
        dyg = dy * gv
        m1 = jnp.mean(dyg, axis=-1, keepdims=True)
        m2 = jnp.mean(dyg * xhv, axis=-1, keepdims=True)
        dx = rs * (dyg - m1 - xhv * m2)
        return dx, _colsum(dy * xhv), _colsum(dy), _colsum(dx)

    rows = ([(dya, d, 0)] if dya is not None else []) + [(dyb, d, 0), (xh, d, 0), (rstd, 1, 0)]
    return _rowcall(fn, rows, [g], [(d, F32)], [d, d, d], n_rows=n_rows, tm=256, name=name)


def _loss_head(y, target, *, name):
    n_rows, d = y.shape

    def fn(yv, tv):
        diff = yv - tv
        part = jnp.sum(jnp.sum(diff * diff, axis=1, keepdims=True), axis=0, keepdims=True) * (0.5 / d)
        return diff * (1.0 / d), jnp.broadcast_to(part, (1, LANES))

    return _rowcall(fn, [(y, d, 0), (target, d, 0)], [], [(d, F32)], [LANES], n_rows=n_rows, tm=256, name=name)


def _rope_lane_constants():
    lane = np.arange(ATT_MERGED)
    in_head = lane % ATT_HEAD_DIM
    sign = np.where(in_head < ROT_DIM // 2, -1.0, np.where(in_head < ROT_DIM, 1.0, 0.0)).astype(np.float32)
    inv_freq = ROPE_THETA ** (-jnp.arange(0, ROT_DIM, 2, dtype=F32) / ROT_DIM)
    return inv_freq[lane % (ROT_DIM // 2)].reshape(1, ATT_MERGED), jnp.asarray(sign).reshape(1, ATT_MERGED)


def _rope_tables(pos_col, *, name):
    inv_lane, sign = _rope_lane_constants()

    def fn(pos, inv, sg):
        ang = pos.astype(F32) * inv
        return jnp.where(sg != 0.0, jnp.cos(ang), 1.0), sg * jnp.sin(ang)

    return _rowcall(fn, [(pos_col, 1, 0)], [inv_lane, sign], [(ATT_MERGED, F32), (ATT_MERGED, F32)],
                    n_rows=pos_col.shape[0], tm=512, name=name)


def _rot_partner(t):
    lane = lax.broadcasted_iota(jnp.int32, t.shape, 1)
    width = t.shape[1]
    return jnp.where((lane & (ROT_DIM // 2)) == 0, pltpu.roll(t, width - ROT_DIM // 2, 1), pltpu.roll(t, ROT_DIM // 2, 1))


def _rope(t, cos_t, sin_t):
    return t * cos_t + _rot_partner(t) * sin_t


def _rope_transpose(dt, cos_t, sin_t):
    return dt * cos_t + _rot_partner(dt * sin_t)


def _qkv_split(proj, cos_t, sin_t, *, name):
    n_rows = proj.shape[0]
    n_g = len(DILATIONS)

    def fn(*t):
        c, s = t[3 * n_g], t[3 * n_g + 1]
        out = [_rope(t[g], c, s) for g in range(n_g)]
        out += [_rope(t[n_g + g], c, s) for g in range(n_g)]
        out += [t[2 * n_g + g] for g in range(n_g)]
        return out

    rows = [(proj, ATT_MERGED, off + g) for off in (OFF_Q_BLK, OFF_K_BLK, OFF_V_BLK) for g in range(n_g)]
    rows += [(cos_t, ATT_MERGED, 0), (sin_t, ATT_MERGED, 0)]
    outs = _rowcall(fn, rows, [], [(ATT_MERGED, MXU_DTYPE)] * (3 * n_g), n_rows=n_rows, tm=512, name=name)
    return outs[:n_g], outs[n_g:2 * n_g], outs[2 * n_g:]


def _mix(gs, ga, z1, z2, b_att):
    return jax.nn.sigmoid(gs) * (z1 * jax.nn.sigmoid(z2)) + jax.nn.sigmoid(ga) * b_att


def _mix_rows(proj, z, b_att):
    return [(proj, D_MODEL, OFF_GS_BLK), (proj, D_MODEL, OFF_GA_BLK), (z, D_MODEL, 0), (z, D_MODEL, 1), (b_att, D_MODEL, 0)]


def _mix_fwd(proj, z, b_att, *, name):
    return _rowcall(_mix, _mix_rows(proj, z, b_att), [], [(D_MODEL, MXU_DTYPE)],
                    n_rows=proj.shape[0], tm=256, name=name)[0]


def _mix_bwd(dmixed, proj, z, b_att, *, name):
    def fn(dm, gs, ga, z1, z2, ba):
        _, vjp = jax.vjp(_mix, gs, ga, z1, z2, ba)
        dgs, dga, dz1, dz2, dba = vjp(dm)
        dz = jnp.concatenate([dz1, dz2], axis=1)
        return dgs, dga, dz, dba, _colsum(dgs), _colsum(dga), _colsum(dz)

    rows = [(dmixed, D_MODEL, 0)] + _mix_rows(proj, z, b_att)
    widths = [D_MODEL, D_MODEL, 2 * D_MODEL, D_MODEL]
    return _rowcall(fn, rows, [], [(w, MXU_DTYPE) for w in widths], widths[:3], n_rows=proj.shape[0], tm=256, name=name)


def _relu2_fwd(pre, *, name):
    n_rows, w = pre.shape
    return _rowcall(lambda p: jnp.square(jnp.maximum(p, 0.0)), [(pre, w, 0)], [], [(w, MXU_DTYPE)],
                    n_rows=n_rows, tm=256, name=name)[0]


def _relu2_bwd(da, pre, *, name):
    n_rows, w = pre.shape

    def fn(dav, p):
        dp = dav * (2.0 * jnp.maximum(p, 0.0))
        return dp, _colsum(dp)

    return _rowcall(fn, [(da, w, 0), (pre, w, 0)], [], [(w, MXU_DTYPE)], [w], n_rows=n_rows, tm=256, name=name)


def _gelu_bwd(dgy, y, proj, *, name):
    def fn(dg, yv, u):
        _, vjp = jax.vjp(jax.nn.gelu, yv)
        dy = vjp(dg)[0]
        return dy, _colsum(dy * u)

    return _rowcall(fn, [(dgy, SSM_WIDTH, 0), (y, SSM_WIDTH, 0), (proj, SSM_WIDTH, 0)], [], [(SSM_WIDTH, F32)],
                    [SSM_WIDTH], n_rows=y.shape[0], tm=512, name=name)


def _dilated_view(t, dil):
    return t.reshape(t.shape[0] // dil, dil * ATT_MERGED)


def _head(h):
    return slice(h * ATT_HEAD_DIM, (h + 1) * ATT_HEAD_DIM)


def _band_mask(first_key):
    qi = lax.broadcasted_iota(jnp.int32, (ATT_BLK, 2 * ATT_BLK), 0)
    ki = lax.broadcasted_iota(jnp.int32, (ATT_BLK, 2 * ATT_BLK), 1)
    steps = qi + ATT_BLK - ki
    return (steps >= 0) & (steps <= ATT_BLK) & (ki >= first_key)


def _dil_fwd(q, k, v, dil, *, name):
    n_rows = q.shape[0]
    n_blk = n_rows // dil // ATT_BLK
    cur = pl.BlockSpec((ATT_BLK, ATT_MERGED), lambda r, n: (n, r))
    prev = pl.BlockSpec((ATT_BLK, ATT_MERGED), lambda r, n: (jnp.maximum(n - 1, 0), r))

    def body(q_ref, kp_ref, kc_ref, vp_ref, vc_ref, o_ref, l_ref):
        valid = _band_mask(jnp.where(pl.program_id(1) > 0, 0, ATT_BLK))
        for h in range(ATT_HEADS_PER_GROUP):
            sl = _head(h)
            keys = jnp.concatenate([kp_ref[:, sl], kc_ref[:, sl]], axis=0)
            vals = jnp.concatenate([vp_ref[:, sl], vc_ref[:, sl]], axis=0)
            s = jnp.where(valid, _dot(q_ref[:, sl], keys, 1, 1) * ATT_SCALE, NEG_INF)
            m = jnp.max(s, axis=-1, keepdims=True)
            p = jnp.exp(s - m)
            den = jnp.sum(p, axis=-1, keepdims=True)
            o_ref[:, sl] = _dot(p, vals, 1, 0) / den
            l_ref[:, sl] = jnp.broadcast_to(m + jnp.log(den), (ATT_BLK, ATT_HEAD_DIM))

    shape = jax.ShapeDtypeStruct((n_rows // dil, dil * ATT_MERGED), F32)
    o, lse = pl.pallas_call(
        body, name=name, grid=(dil, n_blk), in_specs=[cur, prev, cur, prev, cur], out_specs=[cur, cur],
        out_shape=[shape, shape], compiler_params=_params("parallel", "parallel"),
    )(_dilated_view(q, dil), _dilated_view(k, dil), _dilated_view(k, dil), _dilated_view(v, dil), _dilated_view(v, dil))
    return o.reshape(n_rows, ATT_MERGED), lse.reshape(n_rows, ATT_MERGED)


def _att_merge(outs, lses, *, name):
    n_g = len(outs)

    def fn(*t):
        o, l = t[:n_g], t[n_g:]
        m = functools.reduce(jnp.maximum, l)
        e = [jnp.exp(li - m) for li in l]
        z = functools.reduce(jnp.add, e)
        att = functools.reduce(jnp.add, [(ei / z) * oi for ei, oi in zip(e, o)])
        return att, m + jnp.log(z)

    rows = [(t, ATT_MERGED, 0) for t in (*outs, *lses)]
    return _rowcall(fn, rows, [], [(ATT_MERGED, F32), (ATT_MERGED, F32)], n_rows=outs[0].shape[0], tm=512, name=name)


def _dil_bwd(q, k, v, datt, att, lse, cos_t, sin_t, dil, *, name):
    n_rows = q.shape[0]
    n_blk = n_rows // dil // ATT_BLK
    cur = pl.BlockSpec((ATT_BLK, ATT_MERGED), lambda r, n: (n, r))
    prev = pl.BlockSpec((ATT_BLK, ATT_MERGED), lambda r, n: (jnp.maximum(n - 1, 0), r))
    nxt = pl.BlockSpec((ATT_BLK, ATT_MERGED), lambda r, n: (jnp.minimum(n + 1, n_blk - 1), r))
    acc = pl.BlockSpec((1, ATT_MERGED), lambda r, n: (0, 0))

    def body(qc_ref, qn_ref, kp_ref, kc_ref, vp_ref, vc_ref, dc_ref, dn_ref, ac_ref, an_ref, lc_ref, ln_ref,
             cos_ref, sin_ref, dq_ref, dk_ref, dv_ref, sq_ref, sk_ref, sv_ref, dq_s, dk_s, dv_s):
        n = pl.program_id(1)

        @pl.when((pl.program_id(0) == 0) & (n == 0))
        def _():
            for s_ref in (sq_ref, sk_ref, sv_ref):
                s_ref[...] = jnp.zeros_like(s_ref)

        valid = _band_mask(jnp.where(n > 0, 0, ATT_BLK))
        qi = lax.broadcasted_iota(jnp.int32, (ATT_BLK, ATT_BLK), 0)
        ki = lax.broadcasted_iota(jnp.int32, (ATT_BLK, ATT_BLK), 1)
        valid_next = (ki - qi) >= jnp.where(n < n_blk - 1, 0, ATT_BLK)
        for h in range(ATT_HEADS_PER_GROUP):
            sl = _head(h)
            lane0 = slice(h * ATT_HEAD_DIM, h * ATT_HEAD_DIM + 1)
            qc, kc, vc = qc_ref[:, sl], kc_ref[:, sl], vc_ref[:, sl]
            keys = jnp.concatenate([kp_ref[:, sl], kc], axis=0)
            vals = jnp.concatenate([vp_ref[:, sl], vc], axis=0)
            dc = dc_ref[:, sl]
            delta = jnp.sum(dc * ac_ref[:, sl], axis=-1, keepdims=True)
            p = jnp.where(valid, jnp.exp(_dot(qc, keys, 1, 1) * ATT_SCALE - lc_ref[:, lane0]), 0.0)
            ds = p * (_dot(dc, vals, 1, 1) - delta) * ATT_SCALE
            dq_s[:, sl] = _dot(ds, keys, 1, 0)
            qn, dn = qn_ref[:, sl], dn_ref[:, sl]
            delta_n = jnp.sum(dn * an_ref[:, sl], axis=-1, keepdims=True)
            p_n = jnp.where(valid_next, jnp.exp(_dot(qn, kc, 1, 1) * ATT_SCALE - ln_ref[:, lane0]), 0.0)
            ds_n = p_n * (_dot(dn, vc, 1, 1) - delta_n) * ATT_SCALE
            dv_s[:, sl] = _dot(p[:, ATT_BLK:], dc, 0, 0) + _dot(p_n, dn, 0, 0)
            dk_s[:, sl] = _dot(ds[:, ATT_BLK:], qc, 0, 0) + _dot(ds_n, qn, 0, 0)
        cos_v, sin_v = cos_ref[...], sin_ref[...]
        dq = _rope_transpose(dq_s[...], cos_v, sin_v)
        dk = _rope_transpose(dk_s[...], cos_v, sin_v)
        dv = dv_s[...]
        dq_ref[...] = dq.astype(dq_ref.dtype)
        dk_ref[...] = dk.astype(dk_ref.dtype)
        dv_ref[...] = dv.astype(dv_ref.dtype)
        sq_ref[...] += _colsum(dq)
        sk_ref[...] += _colsum(dk)
        sv_ref[...] += _colsum(dv)

    view = functools.partial(_dilated_view, dil=dil)
    shape = jax.ShapeDtypeStruct((n_rows // dil, dil * ATT_MERGED), MXU_DTYPE)
    sums = jax.ShapeDtypeStruct((1, ATT_MERGED), F32)
    dq, dk, dv, sq, sk, sv = pl.pallas_call(
        body, name=name, grid=(dil, n_blk),
        in_specs=[cur, nxt, prev, cur, prev, cur, cur, nxt, cur, nxt, cur, nxt, cur, cur],
        out_specs=[cur, cur, cur, acc, acc, acc], out_shape=[shape, shape, shape, sums, sums, sums],
        scratch_shapes=[pltpu.VMEM((ATT_BLK, ATT_MERGED), F32)] * 3,
        compiler_params=_params("arbitrary", "arbitrary"),
    )(view(q), view(q), view(k), view(k), view(v), view(v), view(datt), view(datt), view(att), view(att),
      view(lse), view(lse), view(cos_t), view(sin_t))
    return [t.reshape(n_rows, ATT_MERGED) for t in (dq, dk, dv)], [sq, sk, sv]


def _xhead(h):
    return slice(h * XATT_HEAD_DIM, (h + 1) * XATT_HEAD_DIM)


def _xatt_probs(qh, kh):
    s = _dot(qh, kh, 1, 1) * XATT_SCALE
    e = jnp.exp(s - jnp.max(s, axis=-1, keepdims=True))
    return e / jnp.sum(e, axis=-1, keepdims=True)


def _xatt_fwd(q, kv, *, name, tm=512):
    n_rows = q.shape[0]
    n_mem = kv.shape[0]

    def body(q_ref, kv_ref, o_ref):
        for h in range(XATT_HEADS):
            sl = _xhead(h)
            p = _xatt_probs(q_ref[:, sl], kv_ref[:, sl])
            o_ref[:, sl] = _dot(p, kv_ref[:, D_MODEL + h * XATT_HEAD_DIM:D_MODEL + (h + 1) * XATT_HEAD_DIM], 1, 0
                                ).astype(o_ref.dtype)

    row = pl.BlockSpec((tm, D_MODEL), lambda i: (i, 0))
    return pl.pallas_call(
        body, name=name, grid=(n_rows // tm,),
        in_specs=[row, pl.BlockSpec((n_mem, 2 * D_MODEL), lambda i: (0, 0))], out_specs=row,
        out_shape=jax.ShapeDtypeStruct((n_rows, D_MODEL), MXU_DTYPE), compiler_params=_params("parallel"),
    )(q, kv)


def _xatt_bwd(q, kv, do, *, name, tm=512):
    n_rows = q.shape[0]
    n_mem = kv.shape[0]

    def body(q_ref, kv_ref, do_ref, dq_ref, dkv_ref):
        @pl.when(pl.program_id(0) == 0)
        def _():
            dkv_ref[...] = jnp.zeros_like(dkv_ref)

        for h in range(XATT_HEADS):
            sl = _xhead(h)
            vsl = slice(D_MODEL + h * XATT_HEAD_DIM, D_MODEL + (h + 1) * XATT_HEAD_DIM)
            qh, kh, doh = q_ref[:, sl], kv_ref[:, sl], do_ref[:, sl]
            p = _xatt_probs(qh, kh)
            dp = _dot(doh, kv_ref[:, vsl], 1, 1)
            ds = p * (dp - jnp.sum(dp * p, axis=-1, keepdims=True)) * XATT_SCALE
            dq_ref[:, sl] = _dot(ds, kh, 1, 0).astype(dq_ref.dtype)
            dkv_ref[:, sl] += _dot(ds, qh, 0, 0)
            dkv_ref[:, vsl] += _dot(p, doh, 0, 0)

    row = pl.BlockSpec((tm, D_MODEL), lambda i: (i, 0))
    full = pl.BlockSpec((n_mem, 2 * D_MODEL), lambda i: (0, 0))
    return pl.pallas_call(
        body, name=name, grid=(n_rows // tm,), in_specs=[row, full, row], out_specs=[row, full],
        out_shape=[jax.ShapeDtypeStruct((n_rows, D_MODEL), MXU_DTYPE), jax.ShapeDtypeStruct((n_mem, 2 * D_MODEL), F32)],
        compiler_params=_params("arbitrary"),
    )(q, kv, do)


def _disc(logdt, a_re, a_im, b_re, b_im):
    dt = jnp.exp(logdt)
    mag = jnp.exp(a_re * dt)
    ab_re = mag * jnp.cos(a_im * dt)
    ab_im = mag * jnp.sin(a_im * dt)
    den = jnp.square(a_re) + jnp.square(a_im)
    nr = ab_re - 1.0
    f_re = (nr * a_re + ab_im * a_im) / den
    f_im = (ab_im * a_re - nr * a_im) / den
    bb_re = f_re[None] * b_re - f_im[None] * b_im
    bb_im = f_re[None] * b_im + f_im[None] * b_re
    return ab_re, ab_im, bb_re, bb_im


def _disc_transpose(logdt, a_re, a_im, b_re, b_im, g_ab_re, g_ab_im, g_bb_re, g_bb_im):
    dt = jnp.exp(logdt)
    mag = jnp.exp(a_re * dt)
    th = a_im * dt
    cs, sn = jnp.cos(th), jnp.sin(th)
    ab_re, ab_im = mag * cs, mag * sn
    den = jnp.square(a_re) + jnp.square(a_im)
    nr = ab_re - 1.0
    f_re = (nr * a_re + ab_im * a_im) / den
    f_im = (ab_im * a_re - nr * a_im) / den
    d_f_re = jnp.sum(g_bb_re * b_re + g_bb_im * b_im, axis=0)
    d_f_im = jnp.sum(g_bb_im * b_re - g_bb_re * b_im, axis=0)
    d_b_re = g_bb_re * f_re[None] + g_bb_im * f_im[None]
    d_b_im = g_bb_im * f_re[None] - g_bb_re * f_im[None]
    d_n_re, d_n_im = d_f_re / den, d_f_im / den
    d_den = -(d_f_re * f_re + d_f_im * f_im) / den
    d_ab_re = g_ab_re + d_n_re * a_re - d_n_im * a_im
    d_ab_im = g_ab_im + d_n_re * a_im + d_n_im * a_re
    d_a_re = d_n_re * nr + d_n_im * ab_im + 2.0 * d_den * a_re
    d_a_im = d_n_re * ab_im - d_n_im * nr + 2.0 * d_den * a_im
    d_mag = d_ab_re * cs + d_ab_im * sn
    d_th = mag * (d_ab_im * cs - d_ab_re * sn)
    d_a_re = d_a_re + d_mag * mag * dt
    d_a_im = d_a_im + d_th * dt
    d_dt = jnp.sum(d_mag * mag * a_re + d_th * a_im, axis=-1, keepdims=True)
    return d_dt * dt, d_a_re, d_a_im, d_b_re, d_b_im


def _whole(fn, args, out_shapes, *, name):
    n_in = len(args)

    def body(*refs):
        res = fn(*[r[...] for r in refs[:n_in]])
        for o_ref, val in zip(refs[n_in:], res):
            o_ref[...] = val

    return pl.pallas_call(body, name=name, out_shape=[jax.ShapeDtypeStruct(s, F32) for s in out_shapes],
                          compiler_params=pltpu.CompilerParams(vmem_limit_bytes=VMEM_LIMIT_BYTES))(*args)


def _tiles_cn(t):
    t = t.reshape(SSM_TILES, GROUPS_PER_TILE, SSM_GROUP, SSM_STATE)
    eye = jnp.eye(GROUPS_PER_TILE, dtype=t.dtype)
    return (t[:, :, :, None, :] * eye[None, :, None, :, None]).reshape(SSM_TILES, LANES, GROUPS_PER_TILE * SSM_STATE)


def _tiles_nc(t):
    t = t.reshape(SSM_TILES, GROUPS_PER_TILE, SSM_STATE, SSM_GROUP)
    eye = jnp.eye(GROUPS_PER_TILE, dtype=t.dtype)
    return (t[:, :, :, None, :] * eye[None, :, None, :, None]).reshape(SSM_TILES, GROUPS_PER_TILE * SSM_STATE, LANES)


def _untile_cn(t):
    t = t.reshape(SSM_TILES, GROUPS_PER_TILE, SSM_GROUP, GROUPS_PER_TILE, SSM_STATE)
    eye = jnp.eye(GROUPS_PER_TILE, dtype=t.dtype)
    return jnp.sum(t * eye[None, :, None, :, None], axis=3).reshape(SSM_GROUPS, SSM_GROUP, SSM_STATE)


def _bd_expand(xsrc, t1, t2, *, name, tm=512):
    n_rows = xsrc.shape[0]
    wide = GROUPS_PER_TILE * SSM_STATE

    def body(x_ref, t1_ref, t2_ref, o1_ref, o2_ref):
        xv = x_ref[...]
        o1_ref[...] = _dot(xv, t1_ref[0], 1, 0)
        o2_ref[...] = _dot(xv, t2_ref[0], 1, 0)

    tile = pl.BlockSpec((1, LANES, wide), lambda j, i: (j, 0, 0))
    out = pl.BlockSpec((tm, wide), lambda j, i: (i, j))
    shape = jax.ShapeDtypeStruct((n_rows, SSM_CH), F32)
    return pl.pallas_call(
        body, name=name, grid=(SSM_TILES, n_rows // tm),
        in_specs=[pl.BlockSpec((tm, LANES), lambda j, i: (i, j)), tile, tile], out_specs=[out, out],
        out_shape=[shape, shape], compiler_params=_params("parallel", "parallel"),
    )(xsrc, t1, t2)


def _bd_contract(s1, s2, t1, t2, gain, xsrc, *, forward, name, tm=512):
    n_rows = s1.shape[0]
    wide = GROUPS_PER_TILE * SSM_STATE

    def body(s1_ref, s2_ref, t1_ref, t2_ref, g_ref, x_ref, o1_ref, o2_ref):
        r = _dot(s1_ref[...], t1_ref[0], 1, 0) + _dot(s2_ref[...], t2_ref[0], 1, 0) + g_ref[...] * x_ref[...]
        if forward:
            o1_ref[...] = r
            o2_ref[...] = jax.nn.gelu(r).astype(o2_ref.dtype)
        else:
            o1_ref[...] = r.astype(o1_ref.dtype)

            @pl.when(pl.program_id(1) == 0)
            def _():
                o2_ref[...] = jnp.zeros_like(o2_ref)

            o2_ref[...] += _colsum(r)

    state = pl.BlockSpec((tm, wide), lambda j, i: (i, j))
    tile = pl.BlockSpec((1, wide, LANES), lambda j, i: (j, 0, 0))
    chan = pl.BlockSpec((tm, LANES), lambda j, i: (i, j))
    vec = pl.BlockSpec((1, LANES), lambda j, i: (0, j))
    if forward:
        out_specs = [chan, chan]
        out_shape = [jax.ShapeDtypeStruct((n_rows, SSM_WIDTH), F32), jax.ShapeDtypeStruct((n_rows, SSM_WIDTH), MXU_DTYPE)]
    else:
        out_specs = [chan, vec]
        out_shape = [jax.ShapeDtypeStruct((n_rows, SSM_WIDTH), MXU_DTYPE), jax.ShapeDtypeStruct((1, SSM_WIDTH), F32)]
    return pl.pallas_call(
        body, name=name, grid=(SSM_TILES, n_rows // tm), in_specs=[state, state, tile, tile, vec, chan],
        out_specs=out_specs, out_shape=out_shape, compiler_params=_params("parallel", "arbitrary"),
    )(s1, s2, t1, t2, gain, xsrc)


def _bd_outer(xsrc, s1, s2, *, name, tk=512):
    n_rows = s1.shape[0]
    wide = GROUPS_PER_TILE * SSM_STATE

    def body(x_ref, s1_ref, s2_ref, o1_ref, o2_ref):
        @pl.when(pl.program_id(1) == 0)
        def _():
            o1_ref[...] = jnp.zeros_like(o1_ref)
            o2_ref[...] = jnp.zeros_like(o2_ref)

        xv = x_ref[...]
        o1_ref[0] += _dot(xv, s1_ref[...], 0, 0)
        o2_ref[0] += _dot(xv, s2_ref[...], 0, 0)

    state = pl.BlockSpec((tk, wide), lambda j, kk: (kk, j))
    tile = pl.BlockSpec((1, LANES, wide), lambda j, kk: (j, 0, 0))
    shape = jax.ShapeDtypeStruct((SSM_TILES, LANES, wide), F32)
    return pl.pallas_call(
        body, name=name, grid=(SSM_TILES, n_rows // tk),
        in_specs=[pl.BlockSpec((tk, LANES), lambda j, kk: (kk, j)), state, state], out_specs=[tile, tile],
        out_shape=[shape, shape], compiler_params=_params("parallel", "arbitrary"),
    )(xsrc, s1, s2)


def _scan_rows(t):
    return pl.ds(pl.multiple_of(t * STATE_VREG_ROWS, 8), STATE_VREG_ROWS)


def _scan_fwd(w_re, w_im, a_re, a_im, *, name, tc=128):
    n_rows = w_re.shape[0]

    def body(wr_ref, wi_ref, ar_ref, ai_ref, hr_ref, hi_ref, sr_ref, si_ref):
        @pl.when(pl.program_id(0) == 0)
        def _():
            sr_ref[...] = jnp.zeros_like(sr_ref)
            si_ref[...] = jnp.zeros_like(si_ref)

        ar, ai = ar_ref[...], ai_ref[...]

        def step(t, carry):
            hr, hi = carry
            rows = _scan_rows(t)
            nr = ar * hr - ai * hi + wr_ref[rows, :]
            ni = ar * hi + ai * hr + wi_ref[rows, :]
            hr_ref[rows, :] = nr
            hi_ref[rows, :] = ni
            return nr, ni

        hr, hi = lax.fori_loop(0, tc, step, (sr_ref[...], si_ref[...]), unroll=4)
        sr_ref[...] = hr
        si_ref[...] = hi

    blk = pl.BlockSpec((tc * STATE_VREG_ROWS, LANES), lambda i: (i, 0))
    coef = pl.BlockSpec((STATE_VREG_ROWS, LANES), lambda i: (0, 0))
    shape = jax.ShapeDtypeStruct((n_rows * STATE_VREG_ROWS, LANES), F32)
    h_re, h_im = pl.pallas_call(
        body, name=name, grid=(n_rows // tc,), in_specs=[blk, blk, coef, coef], out_specs=[blk, blk],
        out_shape=[shape, shape], scratch_shapes=[pltpu.VMEM((STATE_VREG_ROWS, LANES), F32)] * 2,
        compiler_params=_params("arbitrary"),
    )(w_re.reshape(-1, LANES), w_im.reshape(-1, LANES), a_re, a_im)
    return h_re.reshape(n_rows, SSM_CH), h_im.reshape(n_rows, SSM_CH)


def _scan_bwd(dh_re, dh_im, h_re, h_im, a_re, a_im, *, name, tc=128):
    n_rows = dh_re.shape[0]
    n_chunk = n_rows // tc

    def body(gr_ref, gi_ref, hr_ref, hi_ref, ar_ref, ai_ref, lr_ref, li_ref, dar_ref, dai_ref, s_ref):
        @pl.when(pl.program_id(0) == 0)
        def _():
            s_ref[...] = jnp.zeros_like(s_ref)

        ar, ai = ar_ref[...], ai_ref[...]

        def step(kk, carry):
            lr, li, dar, dai = carry
            rows = _scan_rows(tc - 1 - kk)
            hr, hi = hr_ref[rows, :], hi_ref[rows, :]
            dar = dar + lr * hr + li * hi
            dai = dai + li * hr - lr * hi
            nlr = gr_ref[rows, :] + ar * lr + ai * li
            nli = gi_ref[rows, :] + ar * li - ai * lr
            lr_ref[rows, :] = nlr
            li_ref[rows, :] = nli
            return nlr, nli, dar, dai

        lr, li, dar, dai = lax.fori_loop(0, tc, step, (s_ref[0], s_ref[1], s_ref[2], s_ref[3]), unroll=4)
        s_ref[0], s_ref[1], s_ref[2], s_ref[3] = lr, li, dar, dai
        dar_ref[...] = dar
        dai_ref[...] = dai

    blk = pl.BlockSpec((tc * STATE_VREG_ROWS, LANES), lambda i: (n_chunk - 1 - i, 0))
    coef = pl.BlockSpec((STATE_VREG_ROWS, LANES), lambda i: (0, 0))
    shape = jax.ShapeDtypeStruct((n_rows * STATE_VREG_ROWS, LANES), F32)
    cshape = jax.ShapeDtypeStruct((STATE_VREG_ROWS, LANES), F32)
    lam_re, lam_im, da_re, da_im = pl.pallas_call(
        body, name=name, grid=(n_chunk,), in_specs=[blk, blk, blk, blk, coef, coef], out_specs=[blk, blk, coef, coef],
        out_shape=[shape, shape, cshape, cshape], scratch_shapes=[pltpu.VMEM((4, STATE_VREG_ROWS, LANES), F32)],
        compiler_params=_params("arbitrary"),
    )(dh_re.reshape(-1, LANES), dh_im.reshape(-1, LANES), h_re.reshape(-1, LANES), h_im.reshape(-1, LANES), a_re, a_im)
    return lam_re.reshape(n_rows, SSM_CH), lam_im.reshape(n_rows, SSM_CH), da_re, da_im


def _local_grads(x, mem, pos_col, target, wts, sm):
    row = lambda v: v.reshape(1, -1)
    b_re_t = sm["ssm_b_re"].transpose(2, 0, 1)
    b_im_t = sm["ssm_b_im"].transpose(2, 0, 1)
    logdt = sm["ssm_log_dt"].reshape(SSM_GROUPS, 1)
    c_re, c_im = sm["ssm_c_re"], sm["ssm_c_im"]
    grp = (SSM_GROUPS, SSM_STATE)
    chn = (SSM_GROUP, SSM_GROUPS, SSM_STATE)

    cos_t, sin_t = _rope_tables(pos_col, name="rope_tables")
    h0, xh0, rs0 = _ln_fwd(x, None, sm["ln_in_g"], sm["ln_in_b"], alpha=1.0, name="ln_in_fwd")
    proj = _mm(h0, wts["w_in"], bias=sm["b_in"], b_shards=True, name="in_proj")

    disc_in = (logdt, sm["ssm_a_re"], sm["ssm_a_im"], b_re_t, b_im_t)
    ab_re, ab_im, bb_re_t, bb_im_t = _whole(_disc, disc_in, [grp, grp, chn, chn], name="ssm_disc")
    a_re_rows, a_im_rows = ab_re.reshape(STATE_VREG_ROWS, LANES), ab_im.reshape(STATE_VREG_ROWS, LANES)
    w_re, w_im = _bd_expand(proj, _tiles_cn(bb_re_t.transpose(1, 0, 2)), _tiles_cn(bb_im_t.transpose(1, 0, 2)),
                            name="ssm_in")
    h_re, h_im = _scan_fwd(w_re, w_im, a_re_rows, a_im_rows, name="ssm_scan")
    y, gy = _bd_contract(h_re, h_im, _tiles_nc(c_re.transpose(0, 2, 1)), _tiles_nc(-c_im.transpose(0, 2, 1)),
                         sm["ssm_d"], proj, forward=True, name="ssm_out")
    z = _mm(gy, wts["w_glu"], bias=sm["b_glu"], b_shards=True, name="glu_proj")

    q, k, v = _qkv_split(proj, cos_t, sin_t, name="qkv_split")
    outs, lses = [], []
    for g, dil in enumerate(DILATIONS):
        o_g, l_g = _dil_fwd(q[g], k[g], v[g], dil, name=f"dil_att_fwd_{dil}")
        outs.append(o_g)
        lses.append(l_g)
    att, lse = _att_merge(outs, lses, name="att_merge")
    b_att = _mm(att, wts["w_att_up"], b_shards=True, name="att_up")

    mixed = _mix_fwd(proj, z, b_att, name="gate_mix")
    mix_out = _mm(mixed, wts["w_mix_out"], bias=sm["b_mix_out"], name="mix_out")
    h1, xh1, rs1 = _ln_fwd(h0, mix_out, sm["ln1_g"], sm["ln1_b"], alpha=DEEPNORM_ALPHA, name="ln1_fwd")

    xq = _mm(h1, wts["w_xq"], out_dtype=MXU_DTYPE, name="xatt_q")
    kv = _mm(mem, wts["w_xkv"], out_dtype=MXU_DTYPE, b_shards=True, name="xatt_kv")
    xo_in = _xatt_fwd(xq, kv, name="xatt_fwd")
    xo = _mm(xo_in, wts["w_xo"], name="xatt_o")
    h2, xh2, rs2 = _ln_fwd(h1, xo, sm["ln2_g"], sm["ln2_b"], alpha=DEEPNORM_ALPHA, name="ln2_fwd")

    pre = _mm(h2, wts["w_ff1"], bias=sm["b_ff1"], b_shards=True, name="ff1")
    act = _relu2_fwd(pre, name="relu2")
    ff = _mm(act, wts["w_ff2"], bias=sm["b_ff2"], name="ff2")
    h3, xh3, rs3 = _ln_fwd(h2, ff, sm["ln3_g"], sm["ln3_b"], alpha=DEEPNORM_ALPHA, name="ln3_fwd")
    dh3, loss_row = _loss_head(h3, target, name="loss_head")

    gw, gs = {}, {}
    dr3, gs["ln3_g"], gs["ln3_b"], gs["b_ff2"] = _ln_bwd(None, dh3, xh3, rs3, sm["ln3_g"], alpha=1.0, name="ln3_bwd")
    wgrad = functools.partial(_mm, ta=True, out_dtype=WIRE_DTYPE)
    gw["w_ff2"] = wgrad(act, dr3, name="ff2_dw")
    dact = _mm(dr3, wts["w_ff2"], tb=True, name="ff2_dx")
    dpre, gs["b_ff1"] = _relu2_bwd(dact, pre, name="relu2_bwd")
    gw["w_ff1"] = wgrad(h2, dpre, out_shards=True, name="ff1_dw")
    dh2 = _mm(dpre, wts["w_ff1"], tb=True, b_shards=True, name="ff1_dx")

    dr2, gs["ln2_g"], gs["ln2_b"], _ = _ln_bwd(dr3, dh2, xh2, rs2, sm["ln2_g"], alpha=DEEPNORM_ALPHA, name="ln2_bwd")
    gw["w_xo"] = wgrad(xo_in, dr2, name="xatt_o_dw")
    dxo_in = _mm(dr2, wts["w_xo"], tb=True, out_dtype=MXU_DTYPE, name="xatt_o_dx")
    dxq, dkv = _xatt_bwd(xq, kv, dxo_in, name="xatt_bwd")
    gw["w_xq"] = wgrad(h1, dxq, name="xatt_q_dw")
    gw["w_xkv"] = wgrad(mem, dkv, out_shards=True, name="xatt_kv_dw")
    dh1 = _mm(dxq, wts["w_xq"], tb=True, name="xatt_q_dx")

    dr1, gs["ln1_g"], gs["ln1_b"], gs["b_mix_out"] = _ln_bwd(dr2, dh1, xh1, rs1, sm["ln1_g"], alpha=DEEPNORM_ALPHA,
                                                             name="ln1_bwd")
    gw["w_mix_out"] = wgrad(mixed, dr1, name="mix_out_dw")
    dmixed = _mm(dr1, wts["w_mix_out"], tb=True, name="mix_out_dx")
    dgs, dga, dz, db_att, s_gs, s_ga, gs["b_glu"] = _mix_bwd(dmixed, proj, z, b_att, name="gate_mix_bwd")

    gw["w_att_up"] = wgrad(att, db_att, out_shards=True, name="att_up_dw")
    datt = _mm(db_att, wts["w_att_up"], tb=True, b_shards=True, name="att_up_dx")
    dqkv, sqkv = [], []
    for g, dil in enumerate(DILATIONS):
        d_g, s_g = _dil_bwd(q[g], k[g], v[g], datt, att, lse, cos_t, sin_t, dil, name=f"dil_att_bwd_{dil}")
        dqkv.append(d_g)
        sqkv.append(s_g)

    gw["w_glu"] = wgrad(gy, dz, out_shards=True, name="glu_dw")
    dgy = _mm(dz, wts["w_glu"], tb=True, b_shards=True, name="glu_dx")
    dy, gs["ssm_d"] = _gelu_bwd(dgy, y, proj, name="gelu_bwd")
    dh_re, dh_im = _bd_expand(dy, _tiles_cn(c_re), _tiles_cn(-c_im), name="ssm_out_dh")
    lam_re, lam_im, da_re, da_im = _scan_bwd(dh_re, dh_im, h_re, h_im, a_re_rows, a_im_rows, name="ssm_scan_bwd")
    dc_re_t, dc_im_t = _bd_outer(dy, h_re, h_im, name="ssm_out_dc")
    gs["ssm_c_re"], gs["ssm_c_im"] = _untile_cn(dc_re_t), -_untile_cn(dc_im_t)
    dbb_re_t, dbb_im_t = _bd_outer(proj, lam_re, lam_im, name="ssm_in_db")
    du, s_u = _bd_contract(lam_re, lam_im, _tiles_nc(bb_re_t.transpose(1, 2, 0)), _tiles_nc(bb_im_t.transpose(1, 2, 0)),
                           sm["ssm_d"], dy, forward=False, name="ssm_in_du")
    disc_ct = (da_re.reshape(grp), da_im.reshape(grp), _untile_cn(dbb_re_t).transpose(1, 0, 2),
               _untile_cn(dbb_im_t).transpose(1, 0, 2))
    d_logdt, gs["ssm_a_re"], gs["ssm_a_im"], d_b_re_t, d_b_im_t = _whole(
        _disc_transpose, disc_in + disc_ct, [(SSM_GROUPS, 1), grp, grp, chn, chn], name="ssm_disc_bwd")
    gs["ssm_log_dt"] = d_logdt
    gs["ssm_b_re"], gs["ssm_b_im"] = d_b_re_t.transpose(1, 2, 0), d_b_im_t.transpose(1, 2, 0)

    dproj = jnp.concatenate([du] + [dqkv[g][i] for i in range(3) for g in range(len(DILATIONS))] + [dgs, dga], axis=1)
    gs["b_in"] = jnp.concatenate([s_u] + [sqkv[g][i] for i in range(3) for g in range(len(DILATIONS))] + [s_gs, s_ga],
                                 axis=1)
    gw["w_in"] = wgrad(h0, dproj, out_shards=True, name="in_proj_dw")
    dh0 = _mm(dproj, wts["w_in"], tb=True, b_shards=True, name="in_proj_dx")
    grad_x, gs["ln_in_g"], gs["ln_in_b"], _ = _ln_bwd(dr1, dh0, xh0, rs0, sm["ln_in_g"], alpha=DEEPNORM_ALPHA,
                                                      name="ln_in_bwd")
    return loss_row, grad_x, gw, gs


def _exchange(srcs, *, scatter, name):
    n_arr = len(srcs)
    n_peer = N_DEV - 1

    def body(*refs):
        src_refs, out_refs = refs[:n_arr], refs[n_arr:2 * n_arr]
        send_sems, recv_sems, local_sems = refs[2 * n_arr:]
        x, y, c = lax.axis_index("x"), lax.axis_index("y"), lax.axis_index("c")
        me = 4 * x + 2 * y + c
        copies = []
        for a, (src_ref, out_ref) in enumerate(zip(src_refs, out_refs)):
            local = pltpu.make_async_copy(src_ref.at[me] if scatter else src_ref, out_ref.at[me], local_sems.at[a])
            local.start()
            copies.append(local)
            for kk in range(1, N_DEV):
                px = (x + (kk >> 2)) % 2
                py = (y + ((kk >> 1) & 1)) % 2
                pc = (c + (kk & 1)) % 2
                sem = a * n_peer + kk - 1
                cp = pltpu.make_async_remote_copy(
                    src_ref=src_ref.at[4 * px + 2 * py + pc] if scatter else src_ref, dst_ref=out_ref.at[me],
                    send_sem=send_sems.at[sem], recv_sem=recv_sems.at[sem],
                    device_id=(px, py, pc), device_id_type=pl.DeviceIdType.MESH)
                cp.start()
                copies.append(cp)
        for cp in copies:
            cp.wait()

    out_shape = [jax.ShapeDtypeStruct((N_DEV,) + tuple(s.shape[1:] if scatter else s.shape), s.dtype) for s in srcs]
    return pl.pallas_call(
        body, name=name, out_shape=out_shape,
        in_specs=[pl.BlockSpec(memory_space=pl.ANY)] * n_arr, out_specs=[pl.BlockSpec(memory_space=pl.ANY)] * n_arr,
        scratch_shapes=[pltpu.SemaphoreType.DMA((n_arr * n_peer,)), pltpu.SemaphoreType.DMA((n_arr * n_peer,)),
                        pltpu.SemaphoreType.DMA((n_arr,))],
    )(*srcs)


def _reduce_adamw(gstack, w, m, v, *, name, tr=128):
    n_rows, cols = w.shape
    tr = min(tr, n_rows)
    assert n_rows % tr == 0, (name, n_rows, tr)

    def body(g_ref, w_ref, m_ref, v_ref, go_ref, d_ref, mo_ref, vo_ref):
        g = g_ref[0].astype(F32)
        for dev in range(1, N_DEV):
            g = g + g_ref[dev].astype(F32)
        m_new = ADAM_B1 * m_ref[...] + (1.0 - ADAM_B1) * g
        v_new = ADAM_B2 * v_ref[...] + (1.0 - ADAM_B2) * jnp.square(g)
        m_hat = m_new / (1.0 - ADAM_B1 ** ADAM_STEP)
        v_hat = v_new / (1.0 - ADAM_B2 ** ADAM_STEP)
        go_ref[...] = g
        d_ref[...] = -ADAM_LR * (m_hat / (jnp.sqrt(v_hat) + ADAM_EPS) + ADAM_WD * w_ref[...])
        mo_ref[...] = m_new
        vo_ref[...] = v_new

    flat = pl.BlockSpec((tr, cols), lambda i: (i, 0))
    shape = jax.ShapeDtypeStruct((n_rows, cols), F32)
    return pl.pallas_call(
        body, name=name, grid=(n_rows // tr,),
        in_specs=[pl.BlockSpec((N_DEV, tr, cols), lambda i: (0, i, 0)), flat, flat, flat],
        out_specs=[flat] * 4, out_shape=[shape] * 4, compiler_params=_params("parallel"),
    )(gstack, w, m, v)


def _pack(parts, dtype):
    flat = jnp.concatenate([p.reshape(-1).astype(dtype) for p in parts])
    unit = PACK_COLS * PACK_ROW_ALIGN
    total = -(-flat.shape[0] // unit) * unit
    return jnp.pad(flat, (0, total - flat.shape[0])).reshape(-1, PACK_COLS)


def _unpack(packed, shapes):
    flat = packed.reshape(-1)
    out, off = [], 0
    for s in shapes:
        size = int(np.prod(s))
        out.append(flat[off:off + size].reshape(s))
        off += size
    return out


def kernel(x, mem, positions, ln_in_g, ln_in_b, w_in, b_in, ssm_log_dt, ssm_a_re, ssm_a_im, ssm_b_re, ssm_b_im, ssm_c_re, ssm_c_im, ssm_d, w_glu, b_glu, w_att_up, w_mix_out, b_mix_out, ln1_g, ln1_b, w_xq, w_xkv, w_xo, ln2_g, ln2_b, w_ff1, b_ff1, w_ff2, b_ff2, ln3_g, ln3_b, loss_target, m_ln_in_g, m_ln_in_b, m_w_in, m_b_in, m_ssm_log_dt, m_ssm_a_re, m_ssm_a_im, m_ssm_b_re, m_ssm_b_im, m_ssm_c_re, m_ssm_c_im, m_ssm_d, m_w_glu, m_b_glu, m_w_att_up, m_w_mix_out, m_b_mix_out, m_ln1_g, m_ln1_b, m_w_xq, m_w_xkv, m_w_xo, m_ln2_g, m_ln2_b, m_w_ff1, m_b_ff1, m_w_ff2, m_b_ff2, m_ln3_g, m_ln3_b, v_ln_in_g, v_ln_in_b, v_w_in, v_b_in, v_ssm_log_dt, v_ssm_a_re, v_ssm_a_im, v_ssm_b_re, v_ssm_b_im, v_ssm_c_re, v_ssm_c_im, v_ssm_d, v_w_glu, v_b_glu, v_w_att_up, v_w_mix_out, v_b_mix_out, v_ln1_g, v_ln1_b, v_w_xq, v_w_xkv, v_w_xo, v_ln2_g, v_ln2_b, v_w_ff1, v_b_ff1, v_w_ff2, v_b_ff2, v_ln3_g, v_ln3_b):
    given = dict(locals())
    w_arg = {n: given[n] for n in WEIGHTS}
    m_arg = {n: given["m_" + n] for n in WEIGHTS}
    v_arg = {n: given["v_" + n] for n in WEIGHTS}

    gathered = _exchange([w_arg[n][0].astype(MXU_DTYPE) for n in BIG], scatter=False, name="gather_weights")
    wts = {n: g if n in BIG_COL_SHARDED else g.reshape(-1, g.shape[-1]) for n, g in zip(BIG, gathered)}

    sm = {}
    for n in SMALL:
        t = w_arg[n]
        if n.startswith("ssm_") and n not in ("ssm_d", "ssm_log_dt"):
            sm[n] = t[0]
        else:
            sm[n] = t.reshape(1, -1)

    loss_row, grad_x, gw, gs = _local_grads(x[0], mem[0], positions.reshape(-1, 1), loss_target[0], wts, sm)
    loss = lax.psum(loss_row[0, 0], ("x", "y", "c"))

    slots = [gw[n] if n in BIG_COL_SHARDED else gw[n].reshape(N_DEV, -1, gw[n].shape[-1]) for n in BIG]
    big_stacks = _exchange(slots, scatter=True, name="scatter_grads")
    small_stack = _exchange([_pack([gs[n] for n in SMALL], F32)], scatter=False, name="gather_small_grads")[0]

    results = [{}, {}, {}, {}]
    for n, stack in zip(BIG, big_stacks):
        res = _reduce_adamw(stack, w_arg[n][0], m_arg[n][0], v_arg[n][0], name="adamw_" + n)
        for d, r in zip(results, res):
            d[n] = r[None]
    small_shapes = [w_arg[n].shape for n in SMALL]
    res = _reduce_adamw(small_stack, *[_pack([d[n] for n in SMALL], F32) for d in (w_arg, m_arg, v_arg)],
                        name="adamw_small")
    for d, r in zip(results, res):
        d.update(zip(SMALL, _unpack(r, small_shapes)))
    out = [loss, grad_x[None]]
    for d in results:
        out += [d[n] for n in WEIGHTS]
    return tuple(out)
```

```python
import functools

import numpy as np
import jax
import jax.numpy as jnp
from jax import lax
from jax.experimental import pallas as pl
from jax.experimental.pallas import tpu as pltpu

F32 = jnp.float32
MXU_DTYPE = jnp.bfloat16
WIRE_DTYPE = jnp.bfloat16
VMEM_LIMIT_BYTES = 48 * 1024 * 1024
LANES = 128

N_DEV = 8
D_MODEL = 1024
SSM_GROUP = 16
SSM_WIDTH = 768
SSM_GROUPS = SSM_WIDTH // SSM_GROUP
SSM_STATE = 64
SSM_CH = SSM_GROUPS * SSM_STATE
SSM_TILES = SSM_WIDTH // LANES
GROUPS_PER_TILE = LANES // SSM_GROUP
STATE_VREG_ROWS = SSM_CH // LANES
ATT_HEAD_DIM = 64
ATT_HEADS_PER_GROUP = 4
ATT_MERGED = ATT_HEADS_PER_GROUP * ATT_HEAD_DIM
DILATIONS = (1, 4, 16)
ATT_BLK = 128
ATT_SCALE = ATT_HEAD_DIM ** -0.5
ROT_DIM = ATT_HEAD_DIM // 4
ROPE_THETA = 500000.0
XATT_HEADS = 4
XATT_HEAD_DIM = D_MODEL // XATT_HEADS
XATT_SCALE = XATT_HEAD_DIM ** -0.5
DEEPNORM_ALPHA = 2.0 ** 0.25
LN_EPS = 1e-5
NEG_INF = -1e30
OFF_Q_BLK, OFF_K_BLK, OFF_V_BLK = 3, 6, 9
OFF_GS_BLK, OFF_GA_BLK = 3, 4

ADAM_LR = 0.001
ADAM_B1 = 0.9
ADAM_B2 = 0.999
ADAM_EPS = 1e-08
ADAM_WD = 0.01
ADAM_STEP = 10

BIG = ("w_in", "w_glu", "w_att_up", "w_mix_out", "w_xq", "w_xkv", "w_xo", "w_ff1", "w_ff2")
BIG_COL_SHARDED = ("w_in", "w_glu", "w_att_up", "w_xkv", "w_ff1")
WEIGHTS = ("ln_in_g", "ln_in_b", "w_in", "b_in", "ssm_log_dt", "ssm_a_re", "ssm_a_im", "ssm_b_re", "ssm_b_im",
           "ssm_c_re", "ssm_c_im", "ssm_d", "w_glu", "b_glu", "w_att_up", "w_mix_out", "b_mix_out", "ln1_g", "ln1_b",
           "w_xq", "w_xkv", "w_xo", "ln2_g", "ln2_b", "w_ff1", "b_ff1", "w_ff2", "b_ff2", "ln3_g", "ln3_b")
SMALL = tuple(n for n in WEIGHTS if n not in BIG)
PACK_COLS = 1024
PACK_ROW_ALIGN = 256


def _params(*sem):
    return pltpu.CompilerParams(dimension_semantics=sem, vmem_limit_bytes=VMEM_LIMIT_BYTES)


def _dot(a, b, ca, cb):
    return lax.dot_general(a.astype(MXU_DTYPE), b.astype(MXU_DTYPE), (((ca,), (cb,)), ((), ())),
                           preferred_element_type=F32)


def _fit(dim, pref):
    if dim <= pref:
        return dim
    best = max(t for t in range(LANES, pref + 1, LANES) if dim % t == 0)
    return best


def _mm(a, b, *, name, ta=False, tb=False, bias=None, out_dtype=F32, b_shards=False, out_shards=False,
        tm=512, tn=1024, tk=1024):
    m, k = (a.shape[1], a.shape[0]) if ta else a.shape
    if b_shards:
        n_sh, rows, n_loc = b.shape
        if tb:
            n, tn, tk = rows, _fit(rows, tn), n_loc
            assert k == n_sh * n_loc, (name, k, b.shape)
            b_spec = pl.BlockSpec((1, tn, tk), lambda i, j, kk: (kk, j, 0))
        else:
            n, tn, tk = n_sh * n_loc, n_loc, _fit(k, tk)
            b_spec = pl.BlockSpec((1, tk, tn), lambda i, j, kk: (j, kk, 0))
    else:
        n = b.shape[0] if tb else b.shape[1]
        tn = n // N_DEV if out_shards else _fit(n, tn)
        tk = _fit(k, tk)
        b_spec = pl.BlockSpec((tn, tk), lambda i, j, kk: (j, kk)) if tb else pl.BlockSpec((tk, tn), lambda i, j, kk: (kk, j))
    tm = _fit(m, tm)
    nk = k // tk
    a_spec = pl.BlockSpec((tk, tm), lambda i, j, kk: (kk, i)) if ta else pl.BlockSpec((tm, tk), lambda i, j, kk: (i, kk))
    in_specs, args = [a_spec, b_spec], [a, b]
    if bias is not None:
        in_specs.append(pl.BlockSpec((1, tn), lambda i, j, kk: (0, j)))
        args.append(bias)
    if out_shards:
        assert n == N_DEV * tn, (name, n, tn)
        out_spec = pl.BlockSpec((1, tm, tn), lambda i, j, kk: (j, i, 0))
        out_shape = jax.ShapeDtypeStruct((N_DEV, m, tn), out_dtype)
    else:
        out_spec = pl.BlockSpec((tm, tn), lambda i, j, kk: (i, j))
        out_shape = jax.ShapeDtypeStruct((m, n), out_dtype)

    def body(*refs):
        a_ref, b_ref = refs[0], refs[1]
        o_ref = refs[3] if bias is not None else refs[2]

        def product():
            return _dot(a_ref[...], b_ref[0] if b_shards else b_ref[...], 0 if ta else 1, 1 if tb else 0)

        def finish(r):
            if bias is not None:
                r = r + refs[2][...]
            r = r.astype(o_ref.dtype)
            if out_shards:
                o_ref[0] = r
            else:
                o_ref[...] = r

        if nk == 1:
            finish(product())
            return
        acc_ref = refs[-1]
        kk = pl.program_id(2)

        @pl.when(kk == 0)
        def _():
            acc_ref[...] = jnp.zeros_like(acc_ref)

        acc_ref[...] += product()

        @pl.when(kk == nk - 1)
        def _():
            finish(acc_ref[...])

    return pl.pallas_call(
        body, name=name, grid=(m // tm, n // tn, nk), in_specs=in_specs, out_specs=out_spec, out_shape=out_shape,
        scratch_shapes=[pltpu.VMEM((tm, tn), F32)] if nk > 1 else [],
        compiler_params=_params("parallel", "parallel", "arbitrary"),
    )(*args)


def _rowcall(fn, rows, fulls, row_outs, acc_outs=(), *, n_rows, tm, name):
    n_r, n_f, n_o, n_a = len(rows), len(fulls), len(row_outs), len(acc_outs)
    assert n_rows % tm == 0, (name, n_rows, tm)

    def body(*refs):
        res = fn(*[r[...] for r in refs[:n_r + n_f]])
        res = tuple(res) if isinstance(res, (tuple, list)) else (res,)
        o_refs = refs[n_r + n_f:n_r + n_f + n_o]
        a_refs = refs[n_r + n_f + n_o:]
        for o_ref, val in zip(o_refs, res[:n_o]):
            o_ref[...] = val.astype(o_ref.dtype)
        if n_a:
            @pl.when(pl.program_id(0) == 0)
            def _():
                for a_ref in a_refs:
                    a_ref[...] = jnp.zeros_like(a_ref)

            for a_ref, val in zip(a_refs, res[n_o:]):
                a_ref[...] += val

    in_specs = [pl.BlockSpec((tm, w), functools.partial(lambda i, cb: (i, cb), cb=cb)) for _, w, cb in rows]
    in_specs += [pl.BlockSpec(f.shape, functools.partial(lambda i, nd: (0,) * nd, nd=f.ndim)) for f in fulls]
    out_specs = [pl.BlockSpec((tm, w), lambda i: (i, 0)) for w, _ in row_outs]
    out_specs += [pl.BlockSpec((1, w), lambda i: (0, 0)) for w in acc_outs]
    out_shape = [jax.ShapeDtypeStruct((n_rows, w), dt) for w, dt in row_outs]
    out_shape += [jax.ShapeDtypeStruct((1, w), F32) for w in acc_outs]
    return pl.pallas_call(
        body, name=name, grid=(n_rows // tm,), in_specs=in_specs, out_specs=out_specs, out_shape=out_shape,
        compiler_params=_params("arbitrary" if n_a else "parallel"),
    )(*[r[0] for r in rows], *fulls)


def _colsum(v):
    return jnp.sum(v, axis=0, keepdims=True)


def _ln_fwd(a, r, g, b, *, alpha, name):
    n_rows, d = a.shape

    def fn(*t):
        xin = t[0] if alpha == 1.0 else alpha * t[0]
        if r is not None:
            xin = xin + t[1]
        gv, bv = t[-2], t[-1]
        mu = jnp.mean(xin, axis=-1, keepdims=True)
        xc = xin - mu
        var = jnp.mean(xc * xc, axis=-1, keepdims=True)
        rstd = lax.rsqrt(var + LN_EPS)
        xh = xc * rstd
        return xh * gv + bv, xh, rstd

    rows = [(a, d, 0)] + ([(r, d, 0)] if r is not None else [])
    return _rowcall(fn, rows, [g, b], [(d, F32), (d, F32), (1, F32)], n_rows=n_rows, tm=256, name=name)


def _ln_bwd(dya, dyb, xh, rstd, g, *, alpha, name):
    n_rows, d = xh.shape

    def fn(*t):
        if dya is not None:
            dy = alpha * t[0] + t[1]
            xhv, rs, gv = t[2], t[3], t[4]
        else:
            dy, xhv, rs, gv = t[0], t[1], t[2], t[3]
        dyg = dy * gv
        m1 = jnp.mean(dyg, axis=-1, keepdims=True)
        m2 = jnp.mean(dyg * xhv, axis=-1, keepdims=True)
        dx = rs * (dyg - m1 - xhv * m2)
        return dx, _colsum(dy * xhv), _colsum(dy), _colsum(dx)

    rows = ([(dya, d, 0)] if dya is not None else []) + [(dyb, d, 0), (xh, d, 0), (rstd, 1, 0)]
    return _rowcall(fn, rows, [g], [(d, F32)], [d, d, d], n_rows=n_rows, tm=256, name=name)


def _loss_head(y, target, *, name):
    n_rows, d = y.shape

    def fn(yv, tv):
        diff = yv - tv
        part = jnp.sum(jnp.sum(diff * diff, axis=1, keepdims=True), axis=0, keepdims=True) * (0.5 / d)
        return diff * (1.0 / d), jnp.broadcast_to(part, (1, LANES))

    return _rowcall(fn, [(y, d, 0), (target, d, 0)], [], [(d, F32)], [LANES], n_rows=n_rows, tm=256, name=name)


def _rope_lane_constants():
    lane = np.arange(ATT_MERGED)
    in_head = lane % ATT_HEAD_DIM
    sign = np.where(in_head < ROT_DIM // 2, -1.0, np.where(in_head < ROT_DIM, 1.0, 0.0)).astype(np.float32)
    inv_freq = ROPE_THETA ** (-jnp.arange(0, ROT_DIM, 2, dtype=F32) / ROT_DIM)
    return inv_freq[lane % (ROT_DIM // 2)].reshape(1, ATT_MERGED), jnp.asarray(sign).reshape(1, ATT_MERGED)


def _rope_tables(pos_col, *, name):
    inv_lane, sign = _rope_lane_constants()

    def fn(pos, inv, sg):
        ang = pos.astype(F32) * inv
        return jnp.where(sg != 0.0, jnp.cos(ang), 1.0), sg * jnp.sin(ang)

    return _rowcall(fn, [(pos_col, 1, 0)], [inv_lane, sign], [(ATT_MERGED, F32), (ATT_MERGED, F32)],
                    n_rows=pos_col.shape[0], tm=512, name=name)


def _rot_partner(t):
    lane = lax.broadcasted_iota(jnp.int32, t.shape, 1)
    width = t.shape[1]
    return jnp.where((lane & (ROT_DIM // 2)) == 0, pltpu.roll(t, width - ROT_DIM // 2, 1), pltpu.roll(t, ROT_DIM // 2, 1))


def _rope(t, cos_t, sin_t):
    return t * cos_t + _rot_partner(t) * sin_t


def _rope_transpose(dt, cos_t, sin_t):
    return dt * cos_t + _rot_partner(dt * sin_t)


def _qkv_split(proj, cos_t, sin_t, *, name):
    n_rows = proj.shape[0]
    n_g = len(DILATIONS)

    def fn(*t):
        c, s = t[3 * n_g], t[3 * n_g + 1]
        out = [_rope(t[g], c, s) for g in range(n_g)]
        out += [_rope(t[n_g + g], c, s) for g in range(n_g)]
        out += [t[2 * n_g + g] for g in range(n_g)]
        return out

    rows = [(proj, ATT_MERGED, off + g) for off in (OFF_Q_BLK, OFF_K_BLK, OFF_V_BLK) for g in range(n_g)]
    rows += [(cos_t, ATT_MERGED, 0), (sin_t, ATT_MERGED, 0)]
    outs = _rowcall(fn, rows, [], [(ATT_MERGED, MXU_DTYPE)] * (3 * n_g), n_rows=n_rows, tm=512, name=name)
    return outs[:n_g], outs[n_g:2 * n_g], outs[2 * n_g:]


def _mix(gs, ga, z1, z2, b_att):
    return jax.nn.sigmoid(gs) * (z1 * jax.nn.sigmoid(z2)) + jax.nn.sigmoid(ga) * b_att


def _mix_rows(proj, z, b_att):
    return [(proj, D_MODEL, OFF_GS_BLK), (proj, D_MODEL, OFF_GA_BLK), (z, D_MODEL, 0), (z, D_MODEL, 1), (b_att, D_MODEL, 0)]


def _mix_fwd(proj, z, b_att, *, name):
    return _rowcall(_mix, _mix_rows(proj, z, b_att), [], [(D_MODEL, MXU_DTYPE)],
                    n_rows=proj.shape[0], tm=256, name=name)[0]


def _mix_bwd(dmixed, proj, z, b_att, *, name):
    def fn(dm, gs, ga, z1, z2, ba):
        _, vjp = jax.vjp(_mix, gs, ga, z1, z2, ba)
        dgs, dga, dz1, dz2, dba = vjp(dm)
        dz = jnp.concatenate([dz1, dz2], axis=1)
        return dgs, dga, dz, dba, _colsum(dgs), _colsum(dga), _colsum(dz)

    rows = [(dmixed, D_MODEL, 0)] + _mix_rows(proj, z, b_att)
    widths = [D_MODEL, D_MODEL, 2 * D_MODEL, D_MODEL]
    return _rowcall(fn, rows, [], [(w, MXU_DTYPE) for w in widths], widths[:3], n_rows=proj.shape[0], tm=256, name=name)


def _relu2_fwd(pre, *, name):
    n_rows, w = pre.shape
    return _rowcall(lambda p: jnp.square(jnp.maximum(p, 0.0)), [(pre, w, 0)], [], [(w, MXU_DTYPE)],
                    n_rows=n_rows, tm=256, name=name)[0]


def _relu2_bwd(da, pre, *, name):
    n_rows, w = pre.shape

    def fn(dav, p):
        dp = dav * (2.0 * jnp.maximum(p, 0.0))
        return dp, _colsum(dp)

    return _rowcall(fn, [(da, w, 0), (pre, w, 0)], [], [(w, MXU_DTYPE)], [w], n_rows=n_rows, tm=256, name=name)


def _gelu_bwd(dgy, y, proj, *, name):
    def fn(dg, yv, u):
        _, vjp = jax.vjp(jax.nn.gelu, yv)
        dy = vjp(dg)[0]
        return dy, _colsum(dy * u)

    return _rowcall(fn, [(dgy, SSM_WIDTH, 0), (y, SSM_WIDTH, 0), (proj, SSM_WIDTH, 0)], [], [(SSM_WIDTH, F32)],
                    [SSM_WIDTH], n_rows=y.shape[0], tm=512, name=name)


def _dilated_view(t, dil):
    return t.reshape(t.shape[0] // dil, dil * ATT_MERGED)


def _head(h):
    return slice(h * ATT_HEAD_DIM, (h + 1) * ATT_HEAD_DIM)


def _band_mask(first_key):
    qi = lax.broadcasted_iota(jnp.int32, (ATT_BLK, 2 * ATT_BLK), 0)
    ki = lax.broadcasted_iota(jnp.int32, (ATT_BLK, 2 * ATT_BLK), 1)
    steps = qi + ATT_BLK - ki
    return (steps >= 0) & (steps <= ATT_BLK) & (ki >= first_key)


def _dil_fwd(q, k, v, dil, *, name):
    n_rows = q.shape[0]
    n_blk = n_rows // dil // ATT_BLK
    cur = pl.BlockSpec((ATT_BLK, ATT_MERGED), lambda r, n: (n, r))
    prev = pl.BlockSpec((ATT_BLK, ATT_MERGED), lambda r, n: (jnp.maximum(n - 1, 0), r))

    def body(q_ref, kp_ref, kc_ref, vp_ref, vc_ref, o_ref, l_ref):
        valid = _band_mask(jnp.where(pl.program_id(1) > 0, 0, ATT_BLK))
        for h in range(ATT_HEADS_PER_GROUP):
            sl = _head(h)
            keys = jnp.concatenate([kp_ref[:, sl], kc_ref[:, sl]], axis=0)
            vals = jnp.concatenate([vp_ref[:, sl], vc_ref[:, sl]], axis=0)
            s = jnp.where(valid, _dot(q_ref[:, sl], keys, 1, 1) * ATT_SCALE, NEG_INF)
            m = jnp.max(s, axis=-1, keepdims=True)
            p = jnp.exp(s - m)
            den = jnp.sum(p, axis=-1, keepdims=True)
            o_ref[:, sl] = _dot(p, vals, 1, 0) / den
            l_ref[:, sl] = jnp.broadcast_to(m + jnp.log(den), (ATT_BLK, ATT_HEAD_DIM))

    shape = jax.ShapeDtypeStruct((n_rows // dil, dil * ATT_MERGED), F32)
    o, lse = pl.pallas_call(
        body, name=name, grid=(dil, n_blk), in_specs=[cur, prev, cur, prev, cur], out_specs=[cur, cur],
        out_shape=[shape, shape], compiler_params=_params("parallel", "parallel"),
    )(_dilated_view(q, dil), _dilated_view(k, dil), _dilated_view(k, dil), _dilated_view(v, dil), _dilated_view(v, dil))
    return o.reshape(n_rows, ATT_MERGED), lse.reshape(n_rows, ATT_MERGED)


def _att_merge(outs, lses, *, name):
    n_g = len(outs)

    def fn(*t):
        o, l = t[:n_g], t[n_g:]
        m = functools.reduce(jnp.maximum, l)
        e = [jnp.exp(li - m) for li in l]
        z = functools.reduce(jnp.add, e)
        att = functools.reduce(jnp.add, [(ei / z) * oi for ei, oi in zip(e, o)])
        return att, m + jnp.log(z)

    rows = [(t, ATT_MERGED, 0) for t in (*outs, *lses)]
    return _rowcall(fn, rows, [], [(ATT_MERGED, F32), (ATT_MERGED, F32)], n_rows=outs[0].shape[0], tm=512, name=name)


def _dil_bwd(q, k, v, datt, att, lse, cos_t, sin_t, dil, *, name):
    n_rows = q.shape[0]
    n_blk = n_rows // dil // ATT_BLK
    cur = pl.BlockSpec((ATT_BLK, ATT_MERGED), lambda r, n: (n, r))
    prev = pl.BlockSpec((ATT_BLK, ATT_MERGED), lambda r, n: (jnp.maximum(n - 1, 0), r))
    nxt = pl.BlockSpec((ATT_BLK, ATT_MERGED), lambda r, n: (jnp.minimum(n + 1, n_blk - 1), r))
    acc = pl.BlockSpec((1, ATT_MERGED), lambda r, n: (0, 0))

    def body(qc_ref, qn_ref, kp_ref, kc_ref, vp_ref, vc_ref, dc_ref, dn_ref, ac_ref, an_ref, lc_ref, ln_ref,
             cos_ref, sin_ref, dq_ref, dk_ref, dv_ref, sq_ref, sk_ref, sv_ref, dq_s, dk_s, dv_s):
        n = pl.program_id(1)

        @pl.when((pl.program_id(0) == 0) & (n == 0))
        def _():
            for s_ref in (sq_ref, sk_ref, sv_ref):
                s_ref[...] = jnp.zeros_like(s_ref)

        valid = _band_mask(jnp.where(n > 0, 0, ATT_BLK))
        qi = lax.broadcasted_iota(jnp.int32, (ATT_BLK, ATT_BLK), 0)
        ki = lax.broadcasted_iota(jnp.int32, (ATT_BLK, ATT_BLK), 1)
        valid_next = (ki - qi) >= jnp.where(n < n_blk - 1, 0, ATT_BLK)
        for h in range(ATT_HEADS_PER_GROUP):
            sl = _head(h)
            lane0 = slice(h * ATT_HEAD_DIM, h * ATT_HEAD_DIM + 1)
            qc, kc, vc = qc_ref[:, sl], kc_ref[:, sl], vc_ref[:, sl]
            keys = jnp.concatenate([kp_ref[:, sl], kc], axis=0)
            vals = jnp.concatenate([vp_ref[:, sl], vc], axis=0)
            dc = dc_ref[:, sl]
            delta = jnp.sum(dc * ac_ref[:, sl], axis=-1, keepdims=True)
            p = jnp.where(valid, jnp.exp(_dot(qc, keys, 1, 1) * ATT_SCALE - lc_ref[:, lane0]), 0.0)
            ds = p * (_dot(dc, vals, 1, 1) - delta) * ATT_SCALE
            dq_s[:, sl] = _dot(ds, keys, 1, 0)
            qn, dn = qn_ref[:, sl], dn_ref[:, sl]
            delta_n = jnp.sum(dn * an_ref[:, sl], axis=-1, keepdims=True)
            p_n = jnp.where(valid_next, jnp.exp(_dot(qn, kc, 1, 1) * ATT_SCALE - ln_ref[:, lane0]), 0.0)
            ds_n = p_n * (_dot(dn, vc, 1, 1) - delta_n) * ATT_SCALE
            dv_s[:, sl] = _dot(p[:, ATT_BLK:], dc, 0, 0) + _dot(p_n, dn, 0, 0)
            dk_s[:, sl] = _dot(ds[:, ATT_BLK:], qc, 0, 0) + _dot(ds_n, qn, 0, 0)
        cos_v, sin_v = cos_ref[...], sin_ref[...]
        dq = _rope_transpose(dq_s[...], cos_v, sin_v)
        dk = _rope_transpose(dk_s[...], cos_v, sin_v)
        dv = dv_s[...]
        dq_ref[...] = dq.astype(dq_ref.dtype)
        dk_ref[...] = dk.astype(dk_ref.dtype)
        dv_ref[...] = dv.astype(dv_ref.dtype)
        sq_ref[...] += _colsum(dq)
        sk_ref[...] += _colsum(dk)
        sv_ref[...] += _colsum(dv)

    view = functools.partial(_dilated_view, dil=dil)
    shape = jax.ShapeDtypeStruct((n_rows // dil, dil * ATT_MERGED), MXU_DTYPE)
    sums = jax.ShapeDtypeStruct((1, ATT_MERGED), F32)
    dq, dk, dv, sq, sk, sv = pl.pallas_call(
        body, name=name, grid=(dil, n_blk),
        in_specs=[cur, nxt, prev, cur, prev, cur, cur, nxt, cur, nxt, cur, nxt, cur, cur],
        out_specs=[cur, cur, cur, acc, acc, acc], out_shape=[shape, shape, shape, sums, sums, sums],
        scratch_shapes=[pltpu.VMEM((ATT_BLK, ATT_MERGED), F32)] * 3,
        compiler_params=_params("arbitrary", "arbitrary"),
    )(view(q), view(q), view(k), view(k), view(v), view(v), view(datt), view(datt), view(att), view(att),
      view(lse), view(lse), view(cos_t), view(sin_t))
    return [t.reshape(n_rows, ATT_MERGED) for t in (dq, dk, dv)], [sq, sk, sv]


def _xhead(h):
    return slice(h * XATT_HEAD_DIM, (h + 1) * XATT_HEAD_DIM)


def _xatt_probs(qh, kh):
    s = _dot(qh, kh, 1, 1) * XATT_SCALE
    e = jnp.exp(s - jnp.max(s, axis=-1, keepdims=True))
    return e / jnp.sum(e, axis=-1, keepdims=True)


def _xatt_fwd(q, kv, *, name, tm=512):
    n_rows = q.shape[0]
    n_mem = kv.shape[0]

    def body(q_ref, kv_ref, o_ref):
        for h in range(XATT_HEADS):
            sl = _xhead(h)
            p = _xatt_probs(q_ref[:, sl], kv_ref[:, sl])
            o_ref[:, sl] = _dot(p, kv_ref[:, D_MODEL + h * XATT_HEAD_DIM:D_MODEL + (h + 1) * XATT_HEAD_DIM], 1, 0
                                ).astype(o_ref.dtype)

    row = pl.BlockSpec((tm, D_MODEL), lambda i: (i, 0))
    return pl.pallas_call(
        body, name=name, grid=(n_rows // tm,),
        in_specs=[row, pl.BlockSpec((n_mem, 2 * D_MODEL), lambda i: (0, 0))], out_specs=row,
        out_shape=jax.ShapeDtypeStruct((n_rows, D_MODEL), MXU_DTYPE), compiler_params=_params("parallel"),
    )(q, kv)


def _xatt_bwd(q, kv, do, *, name, tm=512):
    n_rows = q.shape[0]
    n_mem = kv.shape[0]

    def body(q_ref, kv_ref, do_ref, dq_ref, dkv_ref):
        @pl.when(pl.program_id(0) == 0)
        def _():
            dkv_ref[...] = jnp.zeros_like(dkv_ref)

        for h in range(XATT_HEADS):
            sl = _xhead(h)
            vsl = slice(D_MODEL + h * XATT_HEAD_DIM, D_MODEL + (h + 1) * XATT_HEAD_DIM)
            qh, kh, doh = q_ref[:, sl], kv_ref[:, sl], do_ref[:, sl]
            p = _xatt_probs(qh, kh)
            dp = _dot(doh, kv_ref[:, vsl], 1, 1)
            ds = p * (dp - jnp.sum(dp * p, axis=-1, keepdims=True)) * XATT_SCALE
            dq_ref[:, sl] = _dot(ds, kh, 1, 0).astype(dq_ref.dtype)
            dkv_ref[:, sl] += _dot(ds, qh, 0, 0)
            dkv_ref[:, vsl] += _dot(p, doh, 0, 0)

    row = pl.BlockSpec((tm, D_MODEL), lambda i: (i, 0))
    full = pl.BlockSpec((n_mem, 2 * D_MODEL), lambda i: (0, 0))
    return pl.pallas_call(
        body, name=name, grid=(n_rows // tm,), in_specs=[row, full, row], out_specs=[row, full],
        out_shape=[jax.ShapeDtypeStruct((n_rows, D_MODEL), MXU_DTYPE), jax.ShapeDtypeStruct((n_mem, 2 * D_MODEL), F32)],
        compiler_params=_params("arbitrary"),
    )(q, kv, do)


def _disc(logdt, a_re, a_im, b_re, b_im):
    dt = jnp.exp(logdt)
    mag = jnp.exp(a_re * dt)
    ab_re = mag * jnp.cos(a_im * dt)
    ab_im = mag * jnp.sin(a_im * dt)
    den = jnp.square(a_re) + jnp.square(a_im)
    nr = ab_re - 1.0
    f_re = (nr * a_re + ab_im * a_im) / den
    f_im = (ab_im * a_re - nr * a_im) / den
    bb_re = f_re[None] * b_re - f_im[None] * b_im
    bb_im = f_re[None] * b_im + f_im[None] * b_re
    return ab_re, ab_im, bb_re, bb_im


def _disc_transpose(logdt, a_re, a_im, b_re, b_im, g_ab_re, g_ab_im, g_bb_re, g_bb_im):
    dt = jnp.exp(logdt)
    mag = jnp.exp(a_re * dt)
    th = a_im * dt
    cs, sn = jnp.cos(th), jnp.sin(th)
    ab_re, ab_im = mag * cs, mag * sn
    den = jnp.square(a_re) + jnp.square(a_im)
    nr = ab_re - 1.0
    f_re = (nr * a_re + ab_im * a_im) / den
    f_im = (ab_im * a_re - nr * a_im) / den
    d_f_re = jnp.sum(g_bb_re * b_re + g_bb_im * b_im, axis=0)
    d_f_im = jnp.sum(g_bb_im * b_re - g_bb_re * b_im, axis=0)
    d_b_re = g_bb_re * f_re[None] + g_bb_im * f_im[None]
    d_b_im = g_bb_im * f_re[None] - g_bb_re * f_im[None]
    d_n_re, d_n_im = d_f_re / den, d_f_im / den
    d_den = -(d_f_re * f_re + d_f_im * f_im) / den
    d_ab_re = g_ab_re + d_n_re * a_re - d_n_im * a_im
    d_ab_im = g_ab_im + d_n_re * a_im + d_n_im * a_re
    d_a_re = d_n_re * nr + d_n_im * ab_im + 2.0 * d_den * a_re
    d_a_im = d_n_re * ab_im - d_n_im * nr + 2.0 * d_den * a_im
    d_mag = d_ab_re * cs + d_ab_im * sn
    d_th = mag * (d_ab_im * cs - d_ab_re * sn)
    d_a_re = d_a_re + d_mag * mag * dt
    d_a_im = d_a_im + d_th * dt
    d_dt = jnp.sum(d_mag * mag * a_re + d_th * a_im, axis=-1, keepdims=True)
    return d_dt * dt, d_a_re, d_a_im, d_b_re, d_b_im


def _whole(fn, args, out_shapes, *, name):
    n_in = len(args)

    def body(*refs):
        res = fn(*[r[...] for r in refs[:n_in]])
        for o_ref, val in zip(refs[n_in:], res):
            o_ref[...] = val

    return pl.pallas_call(body, name=name, out_shape=[jax.ShapeDtypeStruct(s, F32) for s in out_shapes],
                          compiler_params=pltpu.CompilerParams(vmem_limit_bytes=VMEM_LIMIT_BYTES))(*args)


def _tiles_cn(t):
    t = t.reshape(SSM_TILES, GROUPS_PER_TILE, SSM_GROUP, SSM_STATE)
    eye = jnp.eye(GROUPS_PER_TILE, dtype=t.dtype)
    return (t[:, :, :, None, :] * eye[None, :, None, :, None]).reshape(SSM_TILES, LANES, GROUPS_PER_TILE * SSM_STATE)


def _tiles_nc(t):
    t = t.reshape(SSM_TILES, GROUPS_PER_TILE, SSM_STATE, SSM_GROUP)
    eye = jnp.eye(GROUPS_PER_TILE, dtype=t.dtype)
    return (t[:, :, :, None, :] * eye[None, :, None, :, None]).reshape(SSM_TILES, GROUPS_PER_TILE * SSM_STATE, LANES)


def _untile_cn(t):
    t = t.reshape(SSM_TILES, GROUPS_PER_TILE, SSM_GROUP, GROUPS_PER_TILE, SSM_STATE)
    eye = jnp.eye(GROUPS_PER_TILE, dtype=t.dtype)
    return jnp.sum(t * eye[None, :, None, :, None], axis=3).reshape(SSM_GROUPS, SSM_GROUP, SSM_STATE)


SSM_WIDE = GROUPS_PER_TILE * SSM_STATE
LANE_GROUPS_PER_TILE = SSM_WIDE // LANES


def _chan(j):
    return slice(j * LANES, (j + 1) * LANES)


def _time_major_rows(j, q, tc):
    return pl.ds(j * LANE_GROUPS_PER_TILE + q, tc, stride=STATE_VREG_ROWS)


def _to_time_major(x, t_re_ref, t_im_ref, dst_re, dst_im, tc):
    for j in range(SSM_TILES):
        xj = x[:, _chan(j)]
        for t_ref, dst in ((t_re_ref, dst_re), (t_im_ref, dst_im)):
            r = _dot(xj, t_ref[j], 1, 0)
            for q in range(LANE_GROUPS_PER_TILE):
                dst[_time_major_rows(j, q, tc), :] = r[:, q * LANES:(q + 1) * LANES]


def _from_time_major(src, j, tc):
    return jnp.concatenate([src[_time_major_rows(j, q, tc), :] for q in range(LANE_GROUPS_PER_TILE)], axis=1)


def _scan_chunk(w_re, w_im, h_re, h_im, a_re, a_im, start, tc):
    def step(t, carry):
        hr, hi = carry
        rows = _scan_rows(t)
        nr = a_re * hr - a_im * hi + w_re[rows, :]
        ni = a_re * hi + a_im * hr + w_im[rows, :]
        h_re[rows, :] = nr
        h_im[rows, :] = ni
        return nr, ni

    return lax.fori_loop(0, tc, step, start, unroll=8)


SSM_CHUNK = 128


def _ssm_fwd(proj, tb_re, tb_im, tc_re, tc_im, a_re, a_im, gain, *, name, tc=SSM_CHUNK):
    n_rows = proj.shape[0]
    n_chunk = n_rows // tc

    def body(u_ref, tbr_ref, tbi_ref, tcr_ref, tci_ref, ar_ref, ai_ref, g_ref, y_ref, gy_ref, sbr_ref, sbi_ref,
             wr, wi, hr, hi, state):
        @pl.when(pl.program_id(0) == 0)
        def _():
            state[...] = jnp.zeros_like(state)

        sbr_ref[0] = state[0]
        sbi_ref[0] = state[1]
        u = u_ref[...]
        _to_time_major(u, tbr_ref, tbi_ref, wr, wi, tc)
        state[0], state[1] = _scan_chunk(wr, wi, hr, hi, ar_ref[...], ai_ref[...], (state[0], state[1]), tc)
        for j in range(SSM_TILES):
            yj = (_dot(_from_time_major(hr, j, tc), tcr_ref[j], 1, 0) + _dot(_from_time_major(hi, j, tc), tci_ref[j], 1, 0)
                  + g_ref[:, _chan(j)] * u[:, _chan(j)])
            y_ref[:, _chan(j)] = yj
            gy_ref[:, _chan(j)] = jax.nn.gelu(yj).astype(gy_ref.dtype)

    rows = pl.BlockSpec((tc, SSM_WIDTH), lambda i: (i, 0))
    in_tile = pl.BlockSpec((SSM_TILES, LANES, SSM_WIDE), lambda i: (0, 0, 0))
    out_tile = pl.BlockSpec((SSM_TILES, SSM_WIDE, LANES), lambda i: (0, 0, 0))
    coef = pl.BlockSpec((STATE_VREG_ROWS, LANES), lambda i: (0, 0))
    bound = pl.BlockSpec((1, STATE_VREG_ROWS, LANES), lambda i: (i, 0, 0))
    bshape = jax.ShapeDtypeStruct((n_chunk, STATE_VREG_ROWS, LANES), F32)
    tm_scratch = pltpu.VMEM((tc * STATE_VREG_ROWS, LANES), F32)
    return pl.pallas_call(
        body, name=name, grid=(n_chunk,),
        in_specs=[rows, in_tile, in_tile, out_tile, out_tile, coef, coef, pl.BlockSpec((1, SSM_WIDTH), lambda i: (0, 0))],
        out_specs=[rows, rows, bound, bound],
        out_shape=[jax.ShapeDtypeStruct((n_rows, SSM_WIDTH), F32), jax.ShapeDtypeStruct((n_rows, SSM_WIDTH), MXU_DTYPE),
                   bshape, bshape],
        scratch_shapes=[tm_scratch] * 4 + [pltpu.VMEM((2, STATE_VREG_ROWS, LANES), F32)],
        compiler_params=_params("arbitrary"),
    )(proj, tb_re, tb_im, tc_re, tc_im, a_re, a_im, gain)


def _ssm_bwd(proj, dy, sb_re, sb_im, tb_re, tb_im, td_re, td_im, tu_re, tu_im, a_re, a_im, gain, *, name, tc=SSM_CHUNK):
    n_rows = proj.shape[0]
    n_chunk = n_rows // tc

    def body(u_ref, dy_ref, sbr_ref, sbi_ref, tbr_ref, tbi_ref, tdr_ref, tdi_ref, tur_ref, tui_ref, ar_ref, ai_ref, g_ref,
             du_ref, su_ref, dcr_ref, dci_ref, dbr_ref, dbi_ref, dar_ref, dai_ref, wr, wi, hr, hi, lr, li, carry):
        @pl.when(pl.program_id(0) == 0)
        def _():
            carry[...] = jnp.zeros_like(carry)
            for acc_ref in (su_ref, dcr_ref, dci_ref, dbr_ref, dbi_ref):
                acc_ref[...] = jnp.zeros_like(acc_ref)

        a_r, a_i = ar_ref[...], ai_ref[...]
        u, dyv = u_ref[...], dy_ref[...]
        _to_time_major(u, tbr_ref, tbi_ref, wr, wi, tc)
        _scan_chunk(wr, wi, hr, hi, a_r, a_i, (sbr_ref[0], sbi_ref[0]), tc)
        _to_time_major(dyv, tdr_ref, tdi_ref, wr, wi, tc)

        def step(kk, c):
            lam_r, lam_i, dar, dai = c
            rows = _scan_rows(tc - 1 - kk)
            h_r, h_i = hr[rows, :], hi[rows, :]
            dar = dar + lam_r * h_r + lam_i * h_i
            dai = dai + lam_i * h_r - lam_r * h_i
            new_r = wr[rows, :] + a_r * lam_r + a_i * lam_i
            new_i = wi[rows, :] + a_r * lam_i - a_i * lam_r
            lr[rows, :] = new_r
            li[rows, :] = new_i
            return new_r, new_i, dar, dai

        carry[0], carry[1], carry[2], carry[3] = lax.fori_loop(0, tc, step, (carry[0], carry[1], carry[2], carry[3]),
                                                              unroll=8)
        dar_ref[...] = carry[2]
        dai_ref[...] = carry[3]
        for j in range(SSM_TILES):
            cj = _chan(j)
            lam_r, lam_i = _from_time_major(lr, j, tc), _from_time_major(li, j, tc)
            dcr_ref[j] += _dot(dyv[:, cj], _from_time_major(hr, j, tc), 0, 0)
            dci_ref[j] += _dot(dyv[:, cj], _from_time_major(hi, j, tc), 0, 0)
            dbr_ref[j] += _dot(u[:, cj], lam_r, 0, 0)
            dbi_ref[j] += _dot(u[:, cj], lam_i, 0, 0)
            duj = _dot(lam_r, tur_ref[j], 1, 0) + _dot(lam_i, tui_ref[j], 1, 0) + g_ref[:, cj] * dyv[:, cj]
            du_ref[:, cj] = duj.astype(du_ref.dtype)
            su_ref[:, cj] += _colsum(duj)

    back = lambda i: (n_chunk - 1 - i, 0)
    rows = pl.BlockSpec((tc, SSM_WIDTH), back)
    in_tile = pl.BlockSpec((SSM_TILES, LANES, SSM_WIDE), lambda i: (0, 0, 0))
    out_tile = pl.BlockSpec((SSM_TILES, SSM_WIDE, LANES), lambda i: (0, 0, 0))
    coef = pl.BlockSpec((STATE_VREG_ROWS, LANES), lambda i: (0, 0))
    bound = pl.BlockSpec((1, STATE_VREG_ROWS, LANES), lambda i: (n_chunk - 1 - i, 0, 0))
    vec = pl.BlockSpec((1, SSM_WIDTH), lambda i: (0, 0))
    tshape = jax.ShapeDtypeStruct((SSM_TILES, LANES, SSM_WIDE), F32)
    cshape = jax.ShapeDtypeStruct((STATE_VREG_ROWS, LANES), F32)
    tm_scratch = pltpu.VMEM((tc * STATE_VREG_ROWS, LANES), F32)
    return pl.pallas_call(
        body, name=name, grid=(n_chunk,),
        in_specs=[rows, rows, bound, bound, in_tile, in_tile, in_tile, in_tile, out_tile, out_tile, coef, coef, vec],
        out_specs=[rows, vec, in_tile, in_tile, in_tile, in_tile, coef, coef],
        out_shape=[jax.ShapeDtypeStruct((n_rows, SSM_WIDTH), MXU_DTYPE), jax.ShapeDtypeStruct((1, SSM_WIDTH), F32),
                   tshape, tshape, tshape, tshape, cshape, cshape],
        scratch_shapes=[tm_scratch] * 6 + [pltpu.VMEM((4, STATE_VREG_ROWS, LANES), F32)],
        compiler_params=_params("arbitrary"),
    )(proj, dy, sb_re, sb_im, tb_re, tb_im, td_re, td_im, tu_re, tu_im, a_re, a_im, gain)


def _scan_rows(t):
    return pl.ds(pl.multiple_of(t * STATE_VREG_ROWS, 8), STATE_VREG_ROWS)


def _local_grads(x, mem, pos_col, target, wts, sm):
    row = lambda v: v.reshape(1, -1)
    b_re_t = sm["ssm_b_re"].transpose(2, 0, 1)
    b_im_t = sm["ssm_b_im"].transpose(2, 0, 1)
    logdt = sm["ssm_log_dt"].reshape(SSM_GROUPS, 1)
    c_re, c_im = sm["ssm_c_re"], sm["ssm_c_im"]
    grp = (SSM_GROUPS, SSM_STATE)
    chn = (SSM_GROUP, SSM_GROUPS, SSM_STATE)

    cos_t, sin_t = _rope_tables(pos_col, name="rope_tables")
    h0, xh0, rs0 = _ln_fwd(x, None, sm["ln_in_g"], sm["ln_in_b"], alpha=1.0, name="ln_in_fwd")
    proj = _mm(h0, wts["w_in"], bias=sm["b_in"], b_shards=True, name="in_proj")

    disc_in = (logdt, sm["ssm_a_re"], sm["ssm_a_im"], b_re_t, b_im_t)
    ab_re, ab_im, bb_re_t, bb_im_t = _whole(_disc, disc_in, [grp, grp, chn, chn], name="ssm_disc")
    a_re_rows, a_im_rows = ab_re.reshape(STATE_VREG_ROWS, LANES), ab_im.reshape(STATE_VREG_ROWS, LANES)
    mxu = lambda t: t.astype(MXU_DTYPE)
    tb_re, tb_im = mxu(_tiles_cn(bb_re_t.transpose(1, 0, 2))), mxu(_tiles_cn(bb_im_t.transpose(1, 0, 2)))
    y, gy, sb_re, sb_im = _ssm_fwd(proj, tb_re, tb_im, mxu(_tiles_nc(c_re.transpose(0, 2, 1))),
                                   mxu(_tiles_nc(-c_im.transpose(0, 2, 1))), a_re_rows, a_im_rows, sm["ssm_d"],
                                   name="ssm_fwd")
    z = _mm(gy, wts["w_glu"], bias=sm["b_glu"], b_shards=True, name="glu_proj")

    q, k, v = _qkv_split(proj, cos_t, sin_t, name="qkv_split")
    outs, lses = [], []
    for g, dil in enumerate(DILATIONS):
        o_g, l_g = _dil_fwd(q[g], k[g], v[g], dil, name=f"dil_att_fwd_{dil}")
        outs.append(o_g)
        lses.append(l_g)
    att, lse = _att_merge(outs, lses, name="att_merge")
    b_att = _mm(att, wts["w_att_up"], b_shards=True, name="att_up")

    mixed = _mix_fwd(proj, z, b_att, name="gate_mix")
    mix_out = _mm(mixed, wts["w_mix_out"], bias=sm["b_mix_out"], name="mix_out")
    h1, xh1, rs1 = _ln_fwd(h0, mix_out, sm["ln1_g"], sm["ln1_b"], alpha=DEEPNORM_ALPHA, name="ln1_fwd")

    xq = _mm(h1, wts["w_xq"], out_dtype=MXU_DTYPE, name="xatt_q")
    kv = _mm(mem, wts["w_xkv"], out_dtype=MXU_DTYPE, b_shards=True, name="xatt_kv")
    xo_in = _xatt_fwd(xq, kv, name="xatt_fwd")
    xo = _mm(xo_in, wts["w_xo"], name="xatt_o")
    h2, xh2, rs2 = _ln_fwd(h1, xo, sm["ln2_g"], sm["ln2_b"], alpha=DEEPNORM_ALPHA, name="ln2_fwd")

    pre = _mm(h2, wts["w_ff1"], bias=sm["b_ff1"], b_shards=True, name="ff1")
    act = _relu2_fwd(pre, name="relu2")
    ff = _mm(act, wts["w_ff2"], bias=sm["b_ff2"], name="ff2")
    h3, xh3, rs3 = _ln_fwd(h2, ff, sm["ln3_g"], sm["ln3_b"], alpha=DEEPNORM_ALPHA, name="ln3_fwd")
    dh3, loss_row = _loss_head(h3, target, name="loss_head")

    gw, gs = {}, {}
    dr3, gs["ln3_g"], gs["ln3_b"], gs["b_ff2"] = _ln_bwd(None, dh3, xh3, rs3, sm["ln3_g"], alpha=1.0, name="ln3_bwd")
    wgrad = functools.partial(_mm, ta=True, out_dtype=WIRE_DTYPE)
    gw["w_ff2"] = wgrad(act, dr3, name="ff2_dw")
    dact = _mm(dr3, wts["w_ff2"], tb=True, name="ff2_dx")
    dpre, gs["b_ff1"] = _relu2_bwd(dact, pre, name="relu2_bwd")
    gw["w_ff1"] = wgrad(h2, dpre, out_shards=True, name="ff1_dw")
    dh2 = _mm(dpre, wts["w_ff1"], tb=True, b_shards=True, name="ff1_dx")

    dr2, gs["ln2_g"], gs["ln2_b"], _ = _ln_bwd(dr3, dh2, xh2, rs2, sm["ln2_g"], alpha=DEEPNORM_ALPHA, name="ln2_bwd")
    gw["w_xo"] = wgrad(xo_in, dr2, name="xatt_o_dw")
    dxo_in = _mm(dr2, wts["w_xo"], tb=True, out_dtype=MXU_DTYPE, name="xatt_o_dx")
    dxq, dkv = _xatt_bwd(xq, kv, dxo_in, name="xatt_bwd")
    gw["w_xq"] = wgrad(h1, dxq, name="xatt_q_dw")
    gw["w_xkv"] = wgrad(mem, dkv, out_shards=True, name="xatt_kv_dw")
    dh1 = _mm(dxq, wts["w_xq"], tb=True, name="xatt_q_dx")

    dr1, gs["ln1_g"], gs["ln1_b"], gs["b_mix_out"] = _ln_bwd(dr2, dh1, xh1, rs1, sm["ln1_g"], alpha=DEEPNORM_ALPHA,
                                                             name="ln1_bwd")
    gw["w_mix_out"] = wgrad(mixed, dr1, name="mix_out_dw")
    dmixed = _mm(dr1, wts["w_mix_out"], tb=True, name="mix_out_dx")
    dgs, dga, dz, db_att, s_gs, s_ga, gs["b_glu"] = _mix_bwd(dmixed, proj, z, b_att, name="gate_mix_bwd")

    gw["w_att_up"] = wgrad(att, db_att, out_shards=True, name="att_up_dw")
    datt = _mm(db_att, wts["w_att_up"], tb=True, b_shards=True, name="att_up_dx")
    dqkv, sqkv = [], []
    for g, dil in enumerate(DILATIONS):
        d_g, s_g = _dil_bwd(q[g], k[g], v[g], datt, att, lse, cos_t, sin_t, dil, name=f"dil_att_bwd_{dil}")
        dqkv.append(d_g)
        sqkv.append(s_g)

    gw["w_glu"] = wgrad(gy, dz, out_shards=True, name="glu_dw")
    dgy = _mm(dz, wts["w_glu"], tb=True, b_shards=True, name="glu_dx")
    dy, gs["ssm_d"] = _gelu_bwd(dgy, y, proj, name="gelu_bwd")
    du, s_u, dc_re_t, dc_im_t, dbb_re_t, dbb_im_t, da_re, da_im = _ssm_bwd(
        proj, dy, sb_re, sb_im, tb_re, tb_im, mxu(_tiles_cn(c_re)), mxu(_tiles_cn(-c_im)),
        mxu(_tiles_nc(bb_re_t.transpose(1, 2, 0))), mxu(_tiles_nc(bb_im_t.transpose(1, 2, 0))),
        a_re_rows, a_im_rows, sm["ssm_d"], name="ssm_bwd")
    gs["ssm_c_re"], gs["ssm_c_im"] = _untile_cn(dc_re_t), -_untile_cn(dc_im_t)
    disc_ct = (da_re.reshape(grp), da_im.reshape(grp), _untile_cn(dbb_re_t).transpose(1, 0, 2),
               _untile_cn(dbb_im_t).transpose(1, 0, 2))
    d_logdt, gs["ssm_a_re"], gs["ssm_a_im"], d_b_re_t, d_b_im_t = _whole(
        _disc_transpose, disc_in + disc_ct, [(SSM_GROUPS, 1), grp, grp, chn, chn], name="ssm_disc_bwd")
    gs["ssm_log_dt"] = d_logdt
    gs["ssm_b_re"], gs["ssm_b_im"] = d_b_re_t.transpose(1, 2, 0), d_b_im_t.transpose(1, 2, 0)

    dproj = jnp.concatenate([du] + [dqkv[g][i] for i in range(3) for g in range(len(DILATIONS))] + [dgs, dga], axis=1)
    gs["b_in"] = jnp.concatenate([s_u] + [sqkv[g][i] for i in range(3) for g in range(len(DILATIONS))] + [s_gs, s_ga],
                                 axis=1)
    gw["w_in"] = wgrad(h0, dproj, out_shards=True, name="in_proj_dw")
    dh0 = _mm(dproj, wts["w_in"], tb=True, b_shards=True, name="in_proj_dx")
    grad_x, gs["ln_in_g"], gs["ln_in_b"], _ = _ln_bwd(dr1, dh0, xh0, rs0, sm["ln_in_g"], alpha=DEEPNORM_ALPHA,
                                                      name="ln_in_bwd")
    return loss_row, grad_x, gw, gs


def _exchange(srcs, *, scatter, name):
    n_arr = len(srcs)
    n_peer = N_DEV - 1

    def body(*refs):
        src_refs, out_refs = refs[:n_arr], refs[n_arr:2 * n_arr]
        send_sems, recv_sems, local_sems = refs[2 * n_arr:]
        x, y, c = lax.axis_index("x"), lax.axis_index("y"), lax.axis_index("c")
        me = 4 * x + 2 * y + c
        copies = []
        for a, (src_ref, out_ref) in enumerate(zip(src_refs, out_refs)):
            local = pltpu.make_async_copy(src_ref.at[me] if scatter else src_ref, out_ref.at[me], local_sems.at[a])
            local.start()
            copies.append(local)
            for kk in range(1, N_DEV):
                px = (x + (kk >> 2)) % 2
                py = (y + ((kk >> 1) & 1)) % 2
                pc = (c + (kk & 1)) % 2
                sem = a * n_peer + kk - 1
                cp = pltpu.make_async_remote_copy(
                    src_ref=src_ref.at[4 * px + 2 * py + pc] if scatter else src_ref, dst_ref=out_ref.at[me],
                    send_sem=send_sems.at[sem], recv_sem=recv_sems.at[sem],
                    device_id=(px, py, pc), device_id_type=pl.DeviceIdType.MESH)
                cp.start()
                copies.append(cp)
        for cp in copies:
            cp.wait()

    out_shape = [jax.ShapeDtypeStruct((N_DEV,) + tuple(s.shape[1:] if scatter else s.shape), s.dtype) for s in srcs]
    return pl.pallas_call(
        body, name=name, out_shape=out_shape,
        in_specs=[pl.BlockSpec(memory_space=pl.ANY)] * n_arr, out_specs=[pl.BlockSpec(memory_space=pl.ANY)] * n_arr,
        scratch_shapes=[pltpu.SemaphoreType.DMA((n_arr * n_peer,)), pltpu.SemaphoreType.DMA((n_arr * n_peer,)),
                        pltpu.SemaphoreType.DMA((n_arr,))],
    )(*srcs)


def _reduce_adamw(gstack, w, m, v, *, name, tr=128):
    n_rows, cols = w.shape
    tr = min(tr, n_rows)
    assert n_rows % tr == 0, (name, n_rows, tr)

    def body(g_ref, w_ref, m_ref, v_ref, go_ref, d_ref, mo_ref, vo_ref):
        g = g_ref[0].astype(F32)
        for dev in range(1, N_DEV):
            g = g + g_ref[dev].astype(F32)
        m_new = ADAM_B1 * m_ref[...] + (1.0 - ADAM_B1) * g
        v_new = ADAM_B2 * v_ref[...] + (1.0 - ADAM_B2) * jnp.square(g)
        m_hat = m_new / (1.0 - ADAM_B1 ** ADAM_STEP)
        v_hat = v_new / (1.0 - ADAM_B2 ** ADAM_STEP)
        go_ref[...] = g
        d_ref[...] = -ADAM_LR * (m_hat / (jnp.sqrt(v_hat) + ADAM_EPS) + ADAM_WD * w_ref[...])
        mo_ref[...] = m_new
        vo_ref[...] = v_new

    flat = pl.BlockSpec((tr, cols), lambda i: (i, 0))
    shape = jax.ShapeDtypeStruct((n_rows, cols), F32)
    return pl.pallas_call(
        body, name=name, grid=(n_rows // tr,),
        in_specs=[pl.BlockSpec((N_DEV, tr, cols), lambda i: (0, i, 0)), flat, flat, flat],
        out_specs=[flat] * 4, out_shape=[shape] * 4, compiler_params=_params("parallel"),
    )(gstack, w, m, v)


def _pack(parts, dtype):
    flat = jnp.concatenate([p.reshape(-1).astype(dtype) for p in parts])
    unit = PACK_COLS * PACK_ROW_ALIGN
    total = -(-flat.shape[0] // unit) * unit
    return jnp.pad(flat, (0, total - flat.shape[0])).reshape(-1, PACK_COLS)


def _unpack(packed, shapes):
    flat = packed.reshape(-1)
    out, off = [], 0
    for s in shapes:
        size = int(np.prod(s))
        out.append(flat[off:off + size].reshape(s))
        off += size
    return out


def kernel(x, mem, positions, ln_in_g, ln_in_b, w_in, b_in, ssm_log_dt, ssm_a_re, ssm_a_im, ssm_b_re, ssm_b_im, ssm_c_re, ssm_c_im, ssm_d, w_glu, b_glu, w_att_up, w_mix_out, b_mix_out, ln1_g, ln1_b, w_xq, w_xkv, w_xo, ln2_g, ln2_b, w_ff1, b_ff1, w_ff2, b_ff2, ln3_g, ln3_b, loss_target, m_ln_in_g, m_ln_in_b, m_w_in, m_b_in, m_ssm_log_dt, m_ssm_a_re, m_ssm_a_im, m_ssm_b_re, m_ssm_b_im, m_ssm_c_re, m_ssm_c_im, m_ssm_d, m_w_glu, m_b_glu, m_w_att_up, m_w_mix_out, m_b_mix_out, m_ln1_g, m_ln1_b, m_w_xq, m_w_xkv, m_w_xo, m_ln2_g, m_ln2_b, m_w_ff1, m_b_ff1, m_w_ff2, m_b_ff2, m_ln3_g, m_ln3_b, v_ln_in_g, v_ln_in_b, v_w_in, v_b_in, v_ssm_log_dt, v_ssm_a_re, v_ssm_a_im, v_ssm_b_re, v_ssm_b_im, v_ssm_c_re, v_ssm_c_im, v_ssm_d, v_w_glu, v_b_glu, v_w_att_up, v_w_mix_out, v_b_mix_out, v_ln1_g, v_ln1_b, v_w_xq, v_w_xkv, v_w_xo, v_ln2_g, v_ln2_b, v_w_ff1, v_b_ff1, v_w_ff2, v_b_ff2, v_ln3_g, v_ln3_b):
    given = dict(locals())
    w_arg = {n: given[n] for n in WEIGHTS}
    m_arg = {n: given["m_" + n] for n in WEIGHTS}
    v_arg = {n: given["v_" + n] for n in WEIGHTS}

    gathered = _exchange([w_arg[n][0].astype(MXU_DTYPE) for n in BIG], scatter=False, name="gather_weights")
    wts = {n: g if n in BIG_COL_SHARDED else g.reshape(-1, g.shape[-1]) for n, g in zip(BIG, gathered)}

    sm = {}
    for n in SMALL:
        t = w_arg[n]
        if n.startswith("ssm_") and n not in ("ssm_d", "ssm_log_dt"):
            sm[n] = t[0]
        else:
            sm[n] = t.reshape(1, -1)

    loss_row, grad_x, gw, gs = _local_grads(x[0], mem[0], positions.reshape(-1, 1), loss_target[0], wts, sm)
    loss = lax.psum(loss_row[0, 0], ("x", "y", "c"))

    slots = [gw[n] if n in BIG_COL_SHARDED else gw[n].reshape(N_DEV, -1, gw[n].shape[-1]) for n in BIG]
    big_stacks = _exchange(slots, scatter=True, name="scatter_grads")
    small_stack = _exchange([_pack([gs[n] for n in SMALL], F32)], scatter=False, name="gather_small_grads")[0]

    results = [{}, {}, {}, {}]
    for n, stack in zip(BIG, big_stacks):
        res = _reduce_adamw(stack, w_arg[n][0], m_arg[n][0], v_arg[n][0], name="adamw_" + n)
        for d, r in zip(results, res):
            d[n] = r[None]
    small_shapes = [w_arg[n].shape for n in SMALL]
    res = _reduce_adamw(small_stack, *[_pack([d[n] for n in SMALL], F32) for d in (w_arg, m_arg, v_arg)],
                        name="adamw_small")
    for d, r in zip(results, res):
        d.update(zip(SMALL, _unpack(r, small_shapes)))
    out = [loss, grad_x[None]]
    for d in results:
        out += [d[n] for n in WEIGHTS]
    return tuple(out)
```

```python
import functools

import numpy as np
import jax
import jax.numpy as jnp
from jax import lax
from jax.experimental import pallas as pl
from jax.experimental.pallas import tpu as pltpu

F32 = jnp.float32
MXU_DTYPE = jnp.bfloat16
WIRE_DTYPE = jnp.bfloat16
VMEM_LIMIT_BYTES = 48 * 1024 * 1024
LANES = 128

N_DEV = 8
D_MODEL = 1024
SSM_GROUP = 16
SSM_WIDTH = 768
SSM_GROUPS = SSM_WIDTH // SSM_GROUP
SSM_STATE = 64
SSM_CH = SSM_GROUPS * SSM_STATE
SSM_TILES = SSM_WIDTH // LANES
GROUPS_PER_TILE = LANES // SSM_GROUP
STATE_VREG_ROWS = SSM_CH // LANES
ATT_HEAD_DIM = 64
ATT_HEADS_PER_GROUP = 4
ATT_MERGED = ATT_HEADS_PER_GROUP * ATT_HEAD_DIM
DILATIONS = (1, 4, 16)
ATT_BLK = 128
ATT_SCALE = ATT_HEAD_DIM ** -0.5
ROT_DIM = ATT_HEAD_DIM // 4
ROPE_THETA = 500000.0
XATT_HEADS = 4
XATT_HEAD_DIM = D_MODEL // XATT_HEADS
XATT_SCALE = XATT_HEAD_DIM ** -0.5
DEEPNORM_ALPHA = 2.0 ** 0.25
LN_EPS = 1e-5
NEG_INF = -1e30
OFF_Q_BLK, OFF_K_BLK, OFF_V_BLK = 3, 6, 9
OFF_GS_BLK, OFF_GA_BLK = 3, 4

ADAM_LR = 0.001
ADAM_B1 = 0.9
ADAM_B2 = 0.999
ADAM_EPS = 1e-08
ADAM_WD = 0.01
ADAM_STEP = 10

BIG = ("w_in", "w_glu", "w_att_up", "w_mix_out", "w_xq", "w_xkv", "w_xo", "w_ff1", "w_ff2")
BIG_COL_SHARDED = ("w_in", "w_glu", "w_att_up", "w_xkv", "w_ff1")
WEIGHTS = ("ln_in_g", "ln_in_b", "w_in", "b_in", "ssm_log_dt", "ssm_a_re", "ssm_a_im", "ssm_b_re", "ssm_b_im",
           "ssm_c_re", "ssm_c_im", "ssm_d", "w_glu", "b_glu", "w_att_up", "w_mix_out", "b_mix_out", "ln1_g", "ln1_b",
           "w_xq", "w_xkv", "w_xo", "ln2_g", "ln2_b", "w_ff1", "b_ff1", "w_ff2", "b_ff2", "ln3_g", "ln3_b")
SMALL = tuple(n for n in WEIGHTS if n not in BIG)
PACK_COLS = 1024
PACK_ROW_ALIGN = 256


def _params(*sem):
    return pltpu.CompilerParams(dimension_semantics=sem, vmem_limit_bytes=VMEM_LIMIT_BYTES)


def _dot(a, b, ca, cb):
    return lax.dot_general(a.astype(MXU_DTYPE), b.astype(MXU_DTYPE), (((ca,), (cb,)), ((), ())),
                           preferred_element_type=F32)


def _fit(dim, pref):
    if dim <= pref:
        return dim
    best = max(t for t in range(LANES, pref + 1, LANES) if dim % t == 0)
    return best


def _mm(a, b, *, name, ta=False, tb=False, bias=None, out_dtype=F32, b_shards=False, out_shards=False, after=None,
        tm=512, tn=1024, tk=1024):
    m, k = (a.shape[1], a.shape[0]) if ta else a.shape
    if b_shards:
        n_sh, rows, n_loc = b.shape
        if tb:
            n, tn, tk = rows, _fit(rows, tn), n_loc
            assert k == n_sh * n_loc, (name, k, b.shape)
            b_spec = pl.BlockSpec((1, tn, tk), lambda i, j, kk: (kk, j, 0))
        else:
            n, tn, tk = n_sh * n_loc, n_loc, _fit(k, tk)
            b_spec = pl.BlockSpec((1, tk, tn), lambda i, j, kk: (j, kk, 0))
    else:
        n = b.shape[0] if tb else b.shape[1]
        tn = n // N_DEV if out_shards else _fit(n, tn)
        tk = _fit(k, tk)
        b_spec = pl.BlockSpec((tn, tk), lambda i, j, kk: (j, kk)) if tb else pl.BlockSpec((tk, tn), lambda i, j, kk: (kk, j))
    tm = _fit(m, tm)
    nk = k // tk
    a_spec = pl.BlockSpec((tk, tm), lambda i, j, kk: (kk, i)) if ta else pl.BlockSpec((tm, tk), lambda i, j, kk: (i, kk))
    in_specs, args = [a_spec, b_spec], [a, b]
    if bias is not None:
        in_specs.append(pl.BlockSpec((1, tn), lambda i, j, kk: (0, j)))
        args.append(bias)
    if out_shards:
        assert n == N_DEV * tn, (name, n, tn)
        out_spec = pl.BlockSpec((1, tm, tn), lambda i, j, kk: (j, i, 0))
        out_shape = jax.ShapeDtypeStruct((N_DEV, m, tn), out_dtype)
    else:
        out_spec = pl.BlockSpec((tm, tn), lambda i, j, kk: (i, j))
        out_shape = jax.ShapeDtypeStruct((m, n), out_dtype)

    if after is not None:
        in_specs.append(pl.BlockSpec(memory_space=pl.ANY))
        args.append(after)
    n_in = len(args)

    def body(*refs):
        a_ref, b_ref = refs[0], refs[1]
        o_ref = refs[n_in]

        def product():
            return _dot(a_ref[...], b_ref[0] if b_shards else b_ref[...], 0 if ta else 1, 1 if tb else 0)

        def finish(r):
            if bias is not None:
                r = r + refs[2][...]
            r = r.astype(o_ref.dtype)
            if out_shards:
                o_ref[0] = r
            else:
                o_ref[...] = r

        if nk == 1:
            finish(product())
            return
        acc_ref = refs[-1]
        kk = pl.program_id(2)

        @pl.when(kk == 0)
        def _():
            acc_ref[...] = jnp.zeros_like(acc_ref)

        acc_ref[...] += product()

        @pl.when(kk == nk - 1)
        def _():
            finish(acc_ref[...])

    return pl.pallas_call(
        body, name=name, grid=(m // tm, n // tn, nk), in_specs=in_specs, out_specs=out_spec, out_shape=out_shape,
        scratch_shapes=[pltpu.VMEM((tm, tn), F32)] if nk > 1 else [],
        compiler_params=_params("parallel", "parallel", "arbitrary"),
    )(*args)


def _rowcall(fn, rows, fulls, row_outs, acc_outs=(), *, n_rows, tm, name, after=None):
    n_r, n_f, n_o, n_a = len(rows), len(fulls), len(row_outs), len(acc_outs)
    n_in = n_r + n_f + (after is not None)
    assert n_rows % tm == 0, (name, n_rows, tm)

    def body(*refs):
        res = fn(*[r[...] for r in refs[:n_r + n_f]])
        res = tuple(res) if isinstance(res, (tuple, list)) else (res,)
        o_refs = refs[n_in:n_in + n_o]
        a_refs = refs[n_in + n_o:]
        for o_ref, val in zip(o_refs, res[:n_o]):
            o_ref[...] = val.astype(o_ref.dtype)
        if n_a:
            @pl.when(pl.program_id(0) == 0)
            def _():
                for a_ref in a_refs:
                    a_ref[...] = jnp.zeros_like(a_ref)

            for a_ref, val in zip(a_refs, res[n_o:]):
                a_ref[...] += val

    in_specs = [pl.BlockSpec((tm, w), functools.partial(lambda i, cb: (i, cb), cb=cb)) for _, w, cb in rows]
    in_specs += [pl.BlockSpec(f.shape, functools.partial(lambda i, nd: (0,) * nd, nd=f.ndim)) for f in fulls]
    in_specs += [pl.BlockSpec(memory_space=pl.ANY)] * (after is not None)
    out_specs = [pl.BlockSpec((tm, w), lambda i: (i, 0)) for w, _ in row_outs]
    out_specs += [pl.BlockSpec((1, w), lambda i: (0, 0)) for w in acc_outs]
    out_shape = [jax.ShapeDtypeStruct((n_rows, w), dt) for w, dt in row_outs]
    out_shape += [jax.ShapeDtypeStruct((1, w), F32) for w in acc_outs]
    return pl.pallas_call(
        body, name=name, grid=(n_rows // tm,), in_specs=in_specs, out_specs=out_specs, out_shape=out_shape,
        compiler_params=_params("arbitrary" if n_a else "parallel"),
    )(*[r[0] for r in rows], *fulls, *([after] if after is not None else []))


def _colsum(v):
    return jnp.sum(v, axis=0, keepdims=True)


def _ln_fwd(a, r, g, b, *, alpha, name):
    n_rows, d = a.shape

    def fn(*t):
        xin = t[0] if alpha == 1.0 else alpha * t[0]
        if r is not None:
            xin = xin + t[1]
        gv, bv = t[-2], t[-1]
        mu = jnp.mean(xin, axis=-1, keepdims=True)
        xc = xin - mu
        var = jnp.mean(xc * xc, axis=-1, keepdims=True)
        rstd = lax.rsqrt(var + LN_EPS)
        xh = xc * rstd
        return xh * gv + bv, xh, rstd

    rows = [(a, d, 0)] + ([(r, d, 0)] if r is not None else [])
    return _rowcall(fn, rows, [g, b], [(d, F32), (d, F32), (1, F32)], n_rows=n_rows, tm=256, name=name)


def _ln_bwd(dya, dyb, xh, rstd, g, *, alpha, name):
    n_rows, d = xh.shape

    def fn(*t):
        if dya is not None:
            dy = alpha * t[0] + t[1]
            xhv, rs, gv = t[2], t[3], t[4]
        else:
            dy, xhv, rs, gv = t[0], t[1], t[2], t[3]
        dyg = dy * gv
        m1 = jnp.mean(dyg, axis=-1, keepdims=True)
        m2 = jnp.mean(dyg * xhv, axis=-1, keepdims=True)
        dx = rs * (dyg - m1 - xhv * m2)
        return dx, _colsum(dy * xhv), _colsum(dy), _colsum(dx)

    rows = ([(dya, d, 0)] if dya is not None else []) + [(dyb, d, 0), (xh, d, 0), (rstd, 1, 0)]
    return _rowcall(fn, rows, [g], [(d, F32)], [d, d, d], n_rows=n_rows, tm=256, name=name)


def _loss_head(y, target, *, name):
    n_rows, d = y.shape

    def fn(yv, tv):
        diff = yv - tv
        part = jnp.sum(jnp.sum(diff * diff, axis=1, keepdims=True), axis=0, keepdims=True) * (0.5 / d)
        return diff * (1.0 / d), jnp.broadcast_to(part, (1, LANES))

    return _rowcall(fn, [(y, d, 0), (target, d, 0)], [], [(d, F32)], [LANES], n_rows=n_rows, tm=256, name=name)


def _rope_lane_constants():
    lane = np.arange(ATT_MERGED)
    in_head = lane % ATT_HEAD_DIM
    sign = np.where(in_head < ROT_DIM // 2, -1.0, np.where(in_head < ROT_DIM, 1.0, 0.0)).astype(np.float32)
    inv_freq = ROPE_THETA ** (-jnp.arange(0, ROT_DIM, 2, dtype=F32) / ROT_DIM)
    return inv_freq[lane % (ROT_DIM // 2)].reshape(1, ATT_MERGED), jnp.asarray(sign).reshape(1, ATT_MERGED)


def _rope_tables(pos_col, *, name, after=None):
    inv_lane, sign = _rope_lane_constants()

    def fn(pos, inv, sg):
        ang = pos.astype(F32) * inv
        return jnp.where(sg != 0.0, jnp.cos(ang), 1.0), sg * jnp.sin(ang)

    return _rowcall(fn, [(pos_col, 1, 0)], [inv_lane, sign], [(ATT_MERGED, F32), (ATT_MERGED, F32)],
                    n_rows=pos_col.shape[0], tm=512, name=name, after=after)


def _rot_partner(t):
    lane = lax.broadcasted_iota(jnp.int32, t.shape, 1)
    width = t.shape[1]
    return jnp.where((lane & (ROT_DIM // 2)) == 0, pltpu.roll(t, width - ROT_DIM // 2, 1), pltpu.roll(t, ROT_DIM // 2, 1))


def _rope(t, cos_t, sin_t):
    return t * cos_t + _rot_partner(t) * sin_t


def _rope_transpose(dt, cos_t, sin_t):
    return dt * cos_t + _rot_partner(dt * sin_t)


def _qkv_split(proj, cos_t, sin_t, *, name):
    n_rows = proj.shape[0]
    n_g = len(DILATIONS)

    def fn(*t):
        c, s = t[3 * n_g], t[3 * n_g + 1]
        out = [_rope(t[g], c, s) for g in range(n_g)]
        out += [_rope(t[n_g + g], c, s) for g in range(n_g)]
        out += [t[2 * n_g + g] for g in range(n_g)]
        return out

    rows = [(proj, ATT_MERGED, off + g) for off in (OFF_Q_BLK, OFF_K_BLK, OFF_V_BLK) for g in range(n_g)]
    rows += [(cos_t, ATT_MERGED, 0), (sin_t, ATT_MERGED, 0)]
    outs = _rowcall(fn, rows, [], [(ATT_MERGED, MXU_DTYPE)] * (3 * n_g), n_rows=n_rows, tm=512, name=name)
    return outs[:n_g], outs[n_g:2 * n_g], outs[2 * n_g:]


def _mix(gs, ga, z1, z2, b_att):
    return jax.nn.sigmoid(gs) * (z1 * jax.nn.sigmoid(z2)) + jax.nn.sigmoid(ga) * b_att


def _mix_rows(proj, z, b_att):
    return [(proj, D_MODEL, OFF_GS_BLK), (proj, D_MODEL, OFF_GA_BLK), (z, D_MODEL, 0), (z, D_MODEL, 1), (b_att, D_MODEL, 0)]


def _mix_fwd(proj, z, b_att, *, name):
    return _rowcall(_mix, _mix_rows(proj, z, b_att), [], [(D_MODEL, MXU_DTYPE)],
                    n_rows=proj.shape[0], tm=256, name=name)[0]


def _mix_bwd(dmixed, proj, z, b_att, *, name):
    def fn(dm, gs, ga, z1, z2, ba):
        _, vjp = jax.vjp(_mix, gs, ga, z1, z2, ba)
        dgs, dga, dz1, dz2, dba = vjp(dm)
        dz = jnp.concatenate([dz1, dz2], axis=1)
        return dgs, dga, dz, dba, _colsum(dgs), _colsum(dga), _colsum(dz)

    rows = [(dmixed, D_MODEL, 0)] + _mix_rows(proj, z, b_att)
    widths = [D_MODEL, D_MODEL, 2 * D_MODEL, D_MODEL]
    return _rowcall(fn, rows, [], [(w, MXU_DTYPE) for w in widths], widths[:3], n_rows=proj.shape[0], tm=256, name=name)


def _relu2_fwd(pre, *, name):
    n_rows, w = pre.shape
    return _rowcall(lambda p: jnp.square(jnp.maximum(p, 0.0)), [(pre, w, 0)], [], [(w, MXU_DTYPE)],
                    n_rows=n_rows, tm=256, name=name)[0]


def _relu2_bwd(da, pre, *, name):
    n_rows, w = pre.shape

    def fn(dav, p):
        dp = dav * (2.0 * jnp.maximum(p, 0.0))
        return dp, _colsum(dp)

    return _rowcall(fn, [(da, w, 0), (pre, w, 0)], [], [(w, MXU_DTYPE)], [w], n_rows=n_rows, tm=256, name=name)


def _gelu_bwd(dgy, y, proj, *, name):
    def fn(dg, yv, u):
        _, vjp = jax.vjp(jax.nn.gelu, yv)
        dy = vjp(dg)[0]
        return dy, _colsum(dy * u)

    return _rowcall(fn, [(dgy, SSM_WIDTH, 0), (y, SSM_WIDTH, 0), (proj, SSM_WIDTH, 0)], [], [(SSM_WIDTH, F32)],
                    [SSM_WIDTH], n_rows=y.shape[0], tm=512, name=name)


def _dilated_view(t, dil):
    return t.reshape(t.shape[0] // dil, dil * ATT_MERGED)


def _head(h):
    return slice(h * ATT_HEAD_DIM, (h + 1) * ATT_HEAD_DIM)


def _band_mask(first_key):
    qi = lax.broadcasted_iota(jnp.int32, (ATT_BLK, 2 * ATT_BLK), 0)
    ki = lax.broadcasted_iota(jnp.int32, (ATT_BLK, 2 * ATT_BLK), 1)
    steps = qi + ATT_BLK - ki
    return (steps >= 0) & (steps <= ATT_BLK) & (ki >= first_key)


def _dil_fwd(q, k, v, dil, *, name):
    n_rows = q.shape[0]
    n_blk = n_rows // dil // ATT_BLK
    cur = pl.BlockSpec((ATT_BLK, ATT_MERGED), lambda r, n: (n, r))
    prev = pl.BlockSpec((ATT_BLK, ATT_MERGED), lambda r, n: (jnp.maximum(n - 1, 0), r))

    def body(q_ref, kp_ref, kc_ref, vp_ref, vc_ref, o_ref, l_ref):
        valid = _band_mask(jnp.where(pl.program_id(1) > 0, 0, ATT_BLK))
        for h in range(ATT_HEADS_PER_GROUP):
            sl = _head(h)
            keys = jnp.concatenate([kp_ref[:, sl], kc_ref[:, sl]], axis=0)
            vals = jnp.concatenate([vp_ref[:, sl], vc_ref[:, sl]], axis=0)
            s = jnp.where(valid, _dot(q_ref[:, sl], keys, 1, 1) * ATT_SCALE, NEG_INF)
            m = jnp.max(s, axis=-1, keepdims=True)
            p = jnp.exp(s - m)
            den = jnp.sum(p, axis=-1, keepdims=True)
            o_ref[:, sl] = _dot(p, vals, 1, 0) / den
            l_ref[:, sl] = jnp.broadcast_to(m + jnp.log(den), (ATT_BLK, ATT_HEAD_DIM))

    shape = jax.ShapeDtypeStruct((n_rows // dil, dil * ATT_MERGED), F32)
    o, lse = pl.pallas_call(
        body, name=name, grid=(dil, n_blk), in_specs=[cur, prev, cur, prev, cur], out_specs=[cur, cur],
        out_shape=[shape, shape], compiler_params=_params("parallel", "parallel"),
    )(_dilated_view(q, dil), _dilated_view(k, dil), _dilated_view(k, dil), _dilated_view(v, dil), _dilated_view(v, dil))
    return o.reshape(n_rows, ATT_MERGED), lse.reshape(n_rows, ATT_MERGED)


def _att_merge(outs, lses, *, name):
    n_g = len(outs)

    def fn(*t):
        o, l = t[:n_g], t[n_g:]
        m = functools.reduce(jnp.maximum, l)
        e = [jnp.exp(li - m) for li in l]
        z = functools.reduce(jnp.add, e)
        att = functools.reduce(jnp.add, [(ei / z) * oi for ei, oi in zip(e, o)])
        return att, m + jnp.log(z)

    rows = [(t, ATT_MERGED, 0) for t in (*outs, *lses)]
    return _rowcall(fn, rows, [], [(ATT_MERGED, F32), (ATT_MERGED, F32)], n_rows=outs[0].shape[0], tm=512, name=name)


def _dil_bwd(q, k, v, datt, att, lse, cos_t, sin_t, dil, *, name):
    n_rows = q.shape[0]
    n_blk = n_rows // dil // ATT_BLK
    cur = pl.BlockSpec((ATT_BLK, ATT_MERGED), lambda r, n: (n, r))
    prev = pl.BlockSpec((ATT_BLK, ATT_MERGED), lambda r, n: (jnp.maximum(n - 1, 0), r))
    nxt = pl.BlockSpec((ATT_BLK, ATT_MERGED), lambda r, n: (jnp.minimum(n + 1, n_blk - 1), r))
    acc = pl.BlockSpec((1, ATT_MERGED), lambda r, n: (0, 0))

    def body(qc_ref, qn_ref, kp_ref, kc_ref, vp_ref, vc_ref, dc_ref, dn_ref, ac_ref, an_ref, lc_ref, ln_ref,
             cos_ref, sin_ref, dq_ref, dk_ref, dv_ref, sq_ref, sk_ref, sv_ref, dq_s, dk_s, dv_s):
        n = pl.program_id(1)

        @pl.when((pl.program_id(0) == 0) & (n == 0))
        def _():
            for s_ref in (sq_ref, sk_ref, sv_ref):
                s_ref[...] = jnp.zeros_like(s_ref)

        valid = _band_mask(jnp.where(n > 0, 0, ATT_BLK))
        qi = lax.broadcasted_iota(jnp.int32, (ATT_BLK, ATT_BLK), 0)
        ki = lax.broadcasted_iota(jnp.int32, (ATT_BLK, ATT_BLK), 1)
        valid_next = (ki - qi) >= jnp.where(n < n_blk - 1, 0, ATT_BLK)
        for h in range(ATT_HEADS_PER_GROUP):
            sl = _head(h)
            lane0 = slice(h * ATT_HEAD_DIM, h * ATT_HEAD_DIM + 1)
            qc, kc, vc = qc_ref[:, sl], kc_ref[:, sl], vc_ref[:, sl]
            keys = jnp.concatenate([kp_ref[:, sl], kc], axis=0)
            vals = jnp.concatenate([vp_ref[:, sl], vc], axis=0)
            dc = dc_ref[:, sl]
            delta = jnp.sum(dc * ac_ref[:, sl], axis=-1, keepdims=True)
            p = jnp.where(valid, jnp.exp(_dot(qc, keys, 1, 1) * ATT_SCALE - lc_ref[:, lane0]), 0.0)
            ds = p * (_dot(dc, vals, 1, 1) - delta) * ATT_SCALE
            dq_s[:, sl] = _dot(ds, keys, 1, 0)
            qn, dn = qn_ref[:, sl], dn_ref[:, sl]
            delta_n = jnp.sum(dn * an_ref[:, sl], axis=-1, keepdims=True)
            p_n = jnp.where(valid_next, jnp.exp(_dot(qn, kc, 1, 1) * ATT_SCALE - ln_ref[:, lane0]), 0.0)
            ds_n = p_n * (_dot(dn, vc, 1, 1) - delta_n) * ATT_SCALE
            dv_s[:, sl] = _dot(p[:, ATT_BLK:], dc, 0, 0) + _dot(p_n, dn, 0, 0)
            dk_s[:, sl] = _dot(ds[:, ATT_BLK:], qc, 0, 0) + _dot(ds_n, qn, 0, 0)
        cos_v, sin_v = cos_ref[...], sin_ref[...]
        dq = _rope_transpose(dq_s[...], cos_v, sin_v)
        dk = _rope_transpose(dk_s[...], cos_v, sin_v)
        dv = dv_s[...]
        dq_ref[...] = dq.astype(dq_ref.dtype)
        dk_ref[...] = dk.astype(dk_ref.dtype)
        dv_ref[...] = dv.astype(dv_ref.dtype)
        sq_ref[...] += _colsum(dq)
        sk_ref[...] += _colsum(dk)
        sv_ref[...] += _colsum(dv)

    view = functools.partial(_dilated_view, dil=dil)
    shape = jax.ShapeDtypeStruct((n_rows // dil, dil * ATT_MERGED), MXU_DTYPE)
    sums = jax.ShapeDtypeStruct((1, ATT_MERGED), F32)
    dq, dk, dv, sq, sk, sv = pl.pallas_call(
        body, name=name, grid=(dil, n_blk),
        in_specs=[cur, nxt, prev, cur, prev, cur, cur, nxt, cur, nxt, cur, nxt, cur, cur],
        out_specs=[cur, cur, cur, acc, acc, acc], out_shape=[shape, shape, shape, sums, sums, sums],
        scratch_shapes=[pltpu.VMEM((ATT_BLK, ATT_MERGED), F32)] * 3,
        compiler_params=_params("arbitrary", "arbitrary"),
    )(view(q), view(q), view(k), view(k), view(v), view(v), view(datt), view(datt), view(att), view(att),
      view(lse), view(lse), view(cos_t), view(sin_t))
    return [t.reshape(n_rows, ATT_MERGED) for t in (dq, dk, dv)], [sq, sk, sv]


def _xhead(h):
    return slice(h * XATT_HEAD_DIM, (h + 1) * XATT_HEAD_DIM)


def _xatt_probs(qh, kh):
    s = _dot(qh, kh, 1, 1) * XATT_SCALE
    e = jnp.exp(s - jnp.max(s, axis=-1, keepdims=True))
    return e / jnp.sum(e, axis=-1, keepdims=True)


def _xatt_fwd(q, kv, *, name, tm=512):
    n_rows = q.shape[0]
    n_mem = kv.shape[0]

    def body(q_ref, kv_ref, o_ref):
        for h in range(XATT_HEADS):
            sl = _xhead(h)
            p = _xatt_probs(q_ref[:, sl], kv_ref[:, sl])
            o_ref[:, sl] = _dot(p, kv_ref[:, D_MODEL + h * XATT_HEAD_DIM:D_MODEL + (h + 1) * XATT_HEAD_DIM], 1, 0
                                ).astype(o_ref.dtype)

    row = pl.BlockSpec((tm, D_MODEL), lambda i: (i, 0))
    return pl.pallas_call(
        body, name=name, grid=(n_rows // tm,),
        in_specs=[row, pl.BlockSpec((n_mem, 2 * D_MODEL), lambda i: (0, 0))], out_specs=row,
        out_shape=jax.ShapeDtypeStruct((n_rows, D_MODEL), MXU_DTYPE), compiler_params=_params("parallel"),
    )(q, kv)


def _xatt_bwd(q, kv, do, *, name, tm=512):
    n_rows = q.shape[0]
    n_mem = kv.shape[0]

    def body(q_ref, kv_ref, do_ref, dq_ref, dkv_ref):
        @pl.when(pl.program_id(0) == 0)
        def _():
            dkv_ref[...] = jnp.zeros_like(dkv_ref)

        for h in range(XATT_HEADS):
            sl = _xhead(h)
            vsl = slice(D_MODEL + h * XATT_HEAD_DIM, D_MODEL + (h + 1) * XATT_HEAD_DIM)
            qh, kh, doh = q_ref[:, sl], kv_ref[:, sl], do_ref[:, sl]
            p = _xatt_probs(qh, kh)
            dp = _dot(doh, kv_ref[:, vsl], 1, 1)
            ds = p * (dp - jnp.sum(dp * p, axis=-1, keepdims=True)) * XATT_SCALE
            dq_ref[:, sl] = _dot(ds, kh, 1, 0).astype(dq_ref.dtype)
            dkv_ref[:, sl] += _dot(ds, qh, 0, 0)
            dkv_ref[:, vsl] += _dot(p, doh, 0, 0)

    row = pl.BlockSpec((tm, D_MODEL), lambda i: (i, 0))
    full = pl.BlockSpec((n_mem, 2 * D_MODEL), lambda i: (0, 0))
    return pl.pallas_call(
        body, name=name, grid=(n_rows // tm,), in_specs=[row, full, row], out_specs=[row, full],
        out_shape=[jax.ShapeDtypeStruct((n_rows, D_MODEL), MXU_DTYPE), jax.ShapeDtypeStruct((n_mem, 2 * D_MODEL), F32)],
        compiler_params=_params("arbitrary"),
    )(q, kv, do)


def _disc(logdt, a_re, a_im, b_re, b_im):
    dt = jnp.exp(logdt)
    mag = jnp.exp(a_re * dt)
    ab_re = mag * jnp.cos(a_im * dt)
    ab_im = mag * jnp.sin(a_im * dt)
    den = jnp.square(a_re) + jnp.square(a_im)
    nr = ab_re - 1.0
    f_re = (nr * a_re + ab_im * a_im) / den
    f_im = (ab_im * a_re - nr * a_im) / den
    bb_re = f_re[None] * b_re - f_im[None] * b_im
    bb_im = f_re[None] * b_im + f_im[None] * b_re
    return ab_re, ab_im, bb_re, bb_im


def _disc_transpose(logdt, a_re, a_im, b_re, b_im, g_ab_re, g_ab_im, g_bb_re, g_bb_im):
    dt = jnp.exp(logdt)
    mag = jnp.exp(a_re * dt)
    th = a_im * dt
    cs, sn = jnp.cos(th), jnp.sin(th)
    ab_re, ab_im = mag * cs, mag * sn
    den = jnp.square(a_re) + jnp.square(a_im)
    nr = ab_re - 1.0
    f_re = (nr * a_re + ab_im * a_im) / den
    f_im = (ab_im * a_re - nr * a_im) / den
    d_f_re = jnp.sum(g_bb_re * b_re + g_bb_im * b_im, axis=0)
    d_f_im = jnp.sum(g_bb_im * b_re - g_bb_re * b_im, axis=0)
    d_b_re = g_bb_re * f_re[None] + g_bb_im * f_im[None]
    d_b_im = g_bb_im * f_re[None] - g_bb_re * f_im[None]
    d_n_re, d_n_im = d_f_re / den, d_f_im / den
    d_den = -(d_f_re * f_re + d_f_im * f_im) / den
    d_ab_re = g_ab_re + d_n_re * a_re - d_n_im * a_im
    d_ab_im = g_ab_im + d_n_re * a_im + d_n_im * a_re
    d_a_re = d_n_re * nr + d_n_im * ab_im + 2.0 * d_den * a_re
    d_a_im = d_n_re * ab_im - d_n_im * nr + 2.0 * d_den * a_im
    d_mag = d_ab_re * cs + d_ab_im * sn
    d_th = mag * (d_ab_im * cs - d_ab_re * sn)
    d_a_re = d_a_re + d_mag * mag * dt
    d_a_im = d_a_im + d_th * dt
    d_dt = jnp.sum(d_mag * mag * a_re + d_th * a_im, axis=-1, keepdims=True)
    return d_dt * dt, d_a_re, d_a_im, d_b_re, d_b_im


def _whole(fn, args, out_shapes, *, name):
    n_in = len(args)

    def body(*refs):
        res = fn(*[r[...] for r in refs[:n_in]])
        for o_ref, val in zip(refs[n_in:], res):
            o_ref[...] = val

    return pl.pallas_call(body, name=name, out_shape=[jax.ShapeDtypeStruct(s, F32) for s in out_shapes],
                          compiler_params=pltpu.CompilerParams(vmem_limit_bytes=VMEM_LIMIT_BYTES))(*args)


def _tiles_cn(t):
    t = t.reshape(SSM_TILES, GROUPS_PER_TILE, SSM_GROUP, SSM_STATE)
    eye = jnp.eye(GROUPS_PER_TILE, dtype=t.dtype)
    return (t[:, :, :, None, :] * eye[None, :, None, :, None]).reshape(SSM_TILES, LANES, GROUPS_PER_TILE * SSM_STATE)


def _tiles_nc(t):
    t = t.reshape(SSM_TILES, GROUPS_PER_TILE, SSM_STATE, SSM_GROUP)
    eye = jnp.eye(GROUPS_PER_TILE, dtype=t.dtype)
    return (t[:, :, :, None, :] * eye[None, :, None, :, None]).reshape(SSM_TILES, GROUPS_PER_TILE * SSM_STATE, LANES)


def _untile_cn(t):
    t = t.reshape(SSM_TILES, GROUPS_PER_TILE, SSM_GROUP, GROUPS_PER_TILE, SSM_STATE)
    eye = jnp.eye(GROUPS_PER_TILE, dtype=t.dtype)
    return jnp.sum(t * eye[None, :, None, :, None], axis=3).reshape(SSM_GROUPS, SSM_GROUP, SSM_STATE)


SSM_WIDE = GROUPS_PER_TILE * SSM_STATE
LANE_GROUPS_PER_TILE = SSM_WIDE // LANES


def _chan(j):
    return slice(j * LANES, (j + 1) * LANES)


def _time_major_rows(j, q, tc):
    return pl.ds(j * LANE_GROUPS_PER_TILE + q, tc, stride=STATE_VREG_ROWS)


def _to_time_major(x, t_re_ref, t_im_ref, dst_re, dst_im, tc):
    for j in range(SSM_TILES):
        xj = x[:, _chan(j)]
        for t_ref, dst in ((t_re_ref, dst_re), (t_im_ref, dst_im)):
            r = _dot(xj, t_ref[j], 1, 0)
            for q in range(LANE_GROUPS_PER_TILE):
                dst[_time_major_rows(j, q, tc), :] = r[:, q * LANES:(q + 1) * LANES]


def _from_time_major(src, j, tc):
    return jnp.concatenate([src[_time_major_rows(j, q, tc), :] for q in range(LANE_GROUPS_PER_TILE)], axis=1)


def _scan_chunk(w_re, w_im, h_re, h_im, a_re, a_im, start, tc):
    def step(t, carry):
        hr, hi = carry
        rows = _scan_rows(t)
        nr = a_re * hr - a_im * hi + w_re[rows, :]
        ni = a_re * hi + a_im * hr + w_im[rows, :]
        h_re[rows, :] = nr
        h_im[rows, :] = ni
        return nr, ni

    return lax.fori_loop(0, tc, step, start, unroll=8)


SSM_CHUNK = 128


def _ssm_fwd(proj, tb_re, tb_im, tc_re, tc_im, a_re, a_im, gain, *, name, tc=SSM_CHUNK):
    n_rows = proj.shape[0]
    n_chunk = n_rows // tc

    def body(u_ref, tbr_ref, tbi_ref, tcr_ref, tci_ref, ar_ref, ai_ref, g_ref, y_ref, gy_ref, sbr_ref, sbi_ref,
             wr, wi, hr, hi, state):
        @pl.when(pl.program_id(0) == 0)
        def _():
            state[...] = jnp.zeros_like(state)

        sbr_ref[0] = state[0]
        sbi_ref[0] = state[1]
        u = u_ref[...]
        _to_time_major(u, tbr_ref, tbi_ref, wr, wi, tc)
        state[0], state[1] = _scan_chunk(wr, wi, hr, hi, ar_ref[...], ai_ref[...], (state[0], state[1]), tc)
        for j in range(SSM_TILES):
            yj = (_dot(_from_time_major(hr, j, tc), tcr_ref[j], 1, 0) + _dot(_from_time_major(hi, j, tc), tci_ref[j], 1, 0)
                  + g_ref[:, _chan(j)] * u[:, _chan(j)])
            y_ref[:, _chan(j)] = yj
            gy_ref[:, _chan(j)] = jax.nn.gelu(yj).astype(gy_ref.dtype)

    rows = pl.BlockSpec((tc, SSM_WIDTH), lambda i: (i, 0))
    in_tile = pl.BlockSpec((SSM_TILES, LANES, SSM_WIDE), lambda i: (0, 0, 0))
    out_tile = pl.BlockSpec((SSM_TILES, SSM_WIDE, LANES), lambda i: (0, 0, 0))
    coef = pl.BlockSpec((STATE_VREG_ROWS, LANES), lambda i: (0, 0))
    bound = pl.BlockSpec((1, STATE_VREG_ROWS, LANES), lambda i: (i, 0, 0))
    bshape = jax.ShapeDtypeStruct((n_chunk, STATE_VREG_ROWS, LANES), F32)
    tm_scratch = pltpu.VMEM((tc * STATE_VREG_ROWS, LANES), F32)
    return pl.pallas_call(
        body, name=name, grid=(n_chunk,),
        in_specs=[rows, in_tile, in_tile, out_tile, out_tile, coef, coef, pl.BlockSpec((1, SSM_WIDTH), lambda i: (0, 0))],
        out_specs=[rows, rows, bound, bound],
        out_shape=[jax.ShapeDtypeStruct((n_rows, SSM_WIDTH), F32), jax.ShapeDtypeStruct((n_rows, SSM_WIDTH), MXU_DTYPE),
                   bshape, bshape],
        scratch_shapes=[tm_scratch] * 4 + [pltpu.VMEM((2, STATE_VREG_ROWS, LANES), F32)],
        compiler_params=_params("arbitrary"),
    )(proj, tb_re, tb_im, tc_re, tc_im, a_re, a_im, gain)


def _ssm_bwd(proj, dy, sb_re, sb_im, tb_re, tb_im, td_re, td_im, tu_re, tu_im, a_re, a_im, gain, *, name, tc=SSM_CHUNK):
    n_rows = proj.shape[0]
    n_chunk = n_rows // tc

    def body(u_ref, dy_ref, sbr_ref, sbi_ref, tbr_ref, tbi_ref, tdr_ref, tdi_ref, tur_ref, tui_ref, ar_ref, ai_ref, g_ref,
             du_ref, su_ref, dcr_ref, dci_ref, dbr_ref, dbi_ref, dar_ref, dai_ref, wr, wi, hr, hi, lr, li, carry):
        @pl.when(pl.program_id(0) == 0)
        def _():
            carry[...] = jnp.zeros_like(carry)
            for acc_ref in (su_ref, dcr_ref, dci_ref, dbr_ref, dbi_ref):
                acc_ref[...] = jnp.zeros_like(acc_ref)

        a_r, a_i = ar_ref[...], ai_ref[...]
        u, dyv = u_ref[...], dy_ref[...]
        _to_time_major(u, tbr_ref, tbi_ref, wr, wi, tc)
        _scan_chunk(wr, wi, hr, hi, a_r, a_i, (sbr_ref[0], sbi_ref[0]), tc)
        _to_time_major(dyv, tdr_ref, tdi_ref, wr, wi, tc)

        def step(kk, c):
            lam_r, lam_i, dar, dai = c
            rows = _scan_rows(tc - 1 - kk)
            h_r, h_i = hr[rows, :], hi[rows, :]
            dar = dar + lam_r * h_r + lam_i * h_i
            dai = dai + lam_i * h_r - lam_r * h_i
            new_r = wr[rows, :] + a_r * lam_r + a_i * lam_i
            new_i = wi[rows, :] + a_r * lam_i - a_i * lam_r
            lr[rows, :] = new_r
            li[rows, :] = new_i
            return new_r, new_i, dar, dai

        carry[0], carry[1], carry[2], carry[3] = lax.fori_loop(0, tc, step, (carry[0], carry[1], carry[2], carry[3]),
                                                              unroll=8)
        dar_ref[...] = carry[2]
        dai_ref[...] = carry[3]
        for j in range(SSM_TILES):
            cj = _chan(j)
            lam_r, lam_i = _from_time_major(lr, j, tc), _from_time_major(li, j, tc)
            dcr_ref[j] += _dot(dyv[:, cj], _from_time_major(hr, j, tc), 0, 0)
            dci_ref[j] += _dot(dyv[:, cj], _from_time_major(hi, j, tc), 0, 0)
            dbr_ref[j] += _dot(u[:, cj], lam_r, 0, 0)
            dbi_ref[j] += _dot(u[:, cj], lam_i, 0, 0)
            duj = _dot(lam_r, tur_ref[j], 1, 0) + _dot(lam_i, tui_ref[j], 1, 0) + g_ref[:, cj] * dyv[:, cj]
            du_ref[:, cj] = duj.astype(du_ref.dtype)
            su_ref[:, cj] += _colsum(duj)

    back = lambda i: (n_chunk - 1 - i, 0)
    rows = pl.BlockSpec((tc, SSM_WIDTH), back)
    in_tile = pl.BlockSpec((SSM_TILES, LANES, SSM_WIDE), lambda i: (0, 0, 0))
    out_tile = pl.BlockSpec((SSM_TILES, SSM_WIDE, LANES), lambda i: (0, 0, 0))
    coef = pl.BlockSpec((STATE_VREG_ROWS, LANES), lambda i: (0, 0))
    bound = pl.BlockSpec((1, STATE_VREG_ROWS, LANES), lambda i: (n_chunk - 1 - i, 0, 0))
    vec = pl.BlockSpec((1, SSM_WIDTH), lambda i: (0, 0))
    tshape = jax.ShapeDtypeStruct((SSM_TILES, LANES, SSM_WIDE), F32)
    cshape = jax.ShapeDtypeStruct((STATE_VREG_ROWS, LANES), F32)
    tm_scratch = pltpu.VMEM((tc * STATE_VREG_ROWS, LANES), F32)
    return pl.pallas_call(
        body, name=name, grid=(n_chunk,),
        in_specs=[rows, rows, bound, bound, in_tile, in_tile, in_tile, in_tile, out_tile, out_tile, coef, coef, vec],
        out_specs=[rows, vec, in_tile, in_tile, in_tile, in_tile, coef, coef],
        out_shape=[jax.ShapeDtypeStruct((n_rows, SSM_WIDTH), MXU_DTYPE), jax.ShapeDtypeStruct((1, SSM_WIDTH), F32),
                   tshape, tshape, tshape, tshape, cshape, cshape],
        scratch_shapes=[tm_scratch] * 6 + [pltpu.VMEM((4, STATE_VREG_ROWS, LANES), F32)],
        compiler_params=_params("arbitrary"),
    )(proj, dy, sb_re, sb_im, tb_re, tb_im, td_re, td_im, tu_re, tu_im, a_re, a_im, gain)


def _scan_rows(t):
    return pl.ds(pl.multiple_of(t * STATE_VREG_ROWS, 8), STATE_VREG_ROWS)


GATHER_GROUPS = (("w_in",), ("w_glu", "w_att_up", "w_mix_out"), ("w_xq", "w_xkv", "w_xo", "w_ff1", "w_ff2"))
SCATTER_GROUPS = (("w_ff2", "w_ff1"), ("w_xo", "w_xq", "w_xkv", "w_mix_out"), ("w_att_up", "w_glu", "w_in"))


def _local_grads(x, mem, pos_col, target, sm, fetch, send, start_token):
    b_re_t = sm["ssm_b_re"].transpose(2, 0, 1)
    b_im_t = sm["ssm_b_im"].transpose(2, 0, 1)
    logdt = sm["ssm_log_dt"].reshape(SSM_GROUPS, 1)
    c_re, c_im = sm["ssm_c_re"], sm["ssm_c_im"]
    grp = (SSM_GROUPS, SSM_STATE)
    chn = (SSM_GROUP, SSM_GROUPS, SSM_STATE)

    wts = {}
    cos_t, sin_t = _rope_tables(pos_col, after=start_token, name="rope_tables")
    h0, xh0, rs0 = _ln_fwd(x, None, sm["ln_in_g"], sm["ln_in_b"], alpha=1.0, name="ln_in_fwd")
    disc_in = (logdt, sm["ssm_a_re"], sm["ssm_a_im"], b_re_t, b_im_t)
    ab_re, ab_im, bb_re_t, bb_im_t = _whole(_disc, disc_in, [grp, grp, chn, chn], name="ssm_disc")
    a_re_rows, a_im_rows = ab_re.reshape(STATE_VREG_ROWS, LANES), ab_im.reshape(STATE_VREG_ROWS, LANES)
    wts.update(fetch(0, h0))
    proj = _mm(h0, wts["w_in"], bias=sm["b_in"], b_shards=True, name="in_proj")

    mxu = lambda t: t.astype(MXU_DTYPE)
    tb_re, tb_im = mxu(_tiles_cn(bb_re_t.transpose(1, 0, 2))), mxu(_tiles_cn(bb_im_t.transpose(1, 0, 2)))
    y, gy, sb_re, sb_im = _ssm_fwd(proj, tb_re, tb_im, mxu(_tiles_nc(c_re.transpose(0, 2, 1))),
                                   mxu(_tiles_nc(-c_im.transpose(0, 2, 1))), a_re_rows, a_im_rows, sm["ssm_d"],
                                   name="ssm_fwd")

    q, k, v = _qkv_split(proj, cos_t, sin_t, name="qkv_split")
    outs, lses = [], []
    for g, dil in enumerate(DILATIONS):
        o_g, l_g = _dil_fwd(q[g], k[g], v[g], dil, name=f"dil_att_fwd_{dil}")
        outs.append(o_g)
        lses.append(l_g)
    att, lse = _att_merge(outs, lses, name="att_merge")
    wts.update(fetch(1, att))
    z = _mm(gy, wts["w_glu"], bias=sm["b_glu"], b_shards=True, name="glu_proj")
    b_att = _mm(att, wts["w_att_up"], b_shards=True, name="att_up")

    mixed = _mix_fwd(proj, z, b_att, name="gate_mix")
    mix_out = _mm(mixed, wts["w_mix_out"], bias=sm["b_mix_out"], name="mix_out")
    h1, xh1, rs1 = _ln_fwd(h0, mix_out, sm["ln1_g"], sm["ln1_b"], alpha=DEEPNORM_ALPHA, name="ln1_fwd")

    wts.update(fetch(2, h1))
    xq = _mm(h1, wts["w_xq"], out_dtype=MXU_DTYPE, name="xatt_q")
    kv = _mm(mem, wts["w_xkv"], out_dtype=MXU_DTYPE, b_shards=True, name="xatt_kv")
    xo_in = _xatt_fwd(xq, kv, name="xatt_fwd")
    xo = _mm(xo_in, wts["w_xo"], name="xatt_o")
    h2, xh2, rs2 = _ln_fwd(h1, xo, sm["ln2_g"], sm["ln2_b"], alpha=DEEPNORM_ALPHA, name="ln2_fwd")

    pre = _mm(h2, wts["w_ff1"], bias=sm["b_ff1"], b_shards=True, name="ff1")
    act = _relu2_fwd(pre, name="relu2")
    ff = _mm(act, wts["w_ff2"], bias=sm["b_ff2"], name="ff2")
    h3, xh3, rs3 = _ln_fwd(h2, ff, sm["ln3_g"], sm["ln3_b"], alpha=DEEPNORM_ALPHA, name="ln3_fwd")
    dh3, loss_row = _loss_head(h3, target, name="loss_head")

    gw, gs = {}, {}
    dr3, gs["ln3_g"], gs["ln3_b"], gs["b_ff2"] = _ln_bwd(None, dh3, xh3, rs3, sm["ln3_g"], alpha=1.0, name="ln3_bwd")
    wgrad = functools.partial(_mm, ta=True, out_dtype=WIRE_DTYPE)
    gw["w_ff2"] = wgrad(act, dr3, name="ff2_dw")
    dact = _mm(dr3, wts["w_ff2"], tb=True, name="ff2_dx")
    dpre, gs["b_ff1"] = _relu2_bwd(dact, pre, name="relu2_bwd")
    gw["w_ff1"] = wgrad(h2, dpre, out_shards=True, name="ff1_dw")
    sent = send(0, gw)
    dh2 = _mm(dpre, wts["w_ff1"], tb=True, b_shards=True, after=sent, name="ff1_dx")

    dr2, gs["ln2_g"], gs["ln2_b"], _ = _ln_bwd(dr3, dh2, xh2, rs2, sm["ln2_g"], alpha=DEEPNORM_ALPHA, name="ln2_bwd")
    gw["w_xo"] = wgrad(xo_in, dr2, name="xatt_o_dw")
    dxo_in = _mm(dr2, wts["w_xo"], tb=True, out_dtype=MXU_DTYPE, name="xatt_o_dx")
    dxq, dkv = _xatt_bwd(xq, kv, dxo_in, name="xatt_bwd")
    gw["w_xq"] = wgrad(h1, dxq, name="xatt_q_dw")
    gw["w_xkv"] = wgrad(mem, dkv, out_shards=True, name="xatt_kv_dw")
    dh1 = _mm(dxq, wts["w_xq"], tb=True, name="xatt_q_dx")

    dr1, gs["ln1_g"], gs["ln1_b"], gs["b_mix_out"] = _ln_bwd(dr2, dh1, xh1, rs1, sm["ln1_g"], alpha=DEEPNORM_ALPHA,
                                                             name="ln1_bwd")
    gw["w_mix_out"] = wgrad(mixed, dr1, name="mix_out_dw")
    sent = send(1, gw)
    dmixed = _mm(dr1, wts["w_mix_out"], tb=True, after=sent, name="mix_out_dx")
    dgs, dga, dz, db_att, s_gs, s_ga, gs["b_glu"] = _mix_bwd(dmixed, proj, z, b_att, name="gate_mix_bwd")

    gw["w_att_up"] = wgrad(att, db_att, out_shards=True, name="att_up_dw")
    datt = _mm(db_att, wts["w_att_up"], tb=True, b_shards=True, name="att_up_dx")
    dqkv, sqkv = [], []
    for g, dil in enumerate(DILATIONS):
        d_g, s_g = _dil_bwd(q[g], k[g], v[g], datt, att, lse, cos_t, sin_t, dil, name=f"dil_att_bwd_{dil}")
        dqkv.append(d_g)
        sqkv.append(s_g)

    gw["w_glu"] = wgrad(gy, dz, out_shards=True, name="glu_dw")
    dgy = _mm(dz, wts["w_glu"], tb=True, b_shards=True, name="glu_dx")
    dy, gs["ssm_d"] = _gelu_bwd(dgy, y, proj, name="gelu_bwd")
    du, s_u, dc_re_t, dc_im_t, dbb_re_t, dbb_im_t, da_re, da_im = _ssm_bwd(
        proj, dy, sb_re, sb_im, tb_re, tb_im, mxu(_tiles_cn(c_re)), mxu(_tiles_cn(-c_im)),
        mxu(_tiles_nc(bb_re_t.transpose(1, 2, 0))), mxu(_tiles_nc(bb_im_t.transpose(1, 2, 0))),
        a_re_rows, a_im_rows, sm["ssm_d"], name="ssm_bwd")
    gs["ssm_c_re"], gs["ssm_c_im"] = _untile_cn(dc_re_t), -_untile_cn(dc_im_t)
    disc_ct = (da_re.reshape(grp), da_im.reshape(grp), _untile_cn(dbb_re_t).transpose(1, 0, 2),
               _untile_cn(dbb_im_t).transpose(1, 0, 2))
    d_logdt, gs["ssm_a_re"], gs["ssm_a_im"], d_b_re_t, d_b_im_t = _whole(
        _disc_transpose, disc_in + disc_ct, [(SSM_GROUPS, 1), grp, grp, chn, chn], name="ssm_disc_bwd")
    gs["ssm_log_dt"] = d_logdt
    gs["ssm_b_re"], gs["ssm_b_im"] = d_b_re_t.transpose(1, 2, 0), d_b_im_t.transpose(1, 2, 0)

    dproj = jnp.concatenate([du] + [dqkv[g][i] for i in range(3) for g in range(len(DILATIONS))] + [dgs, dga], axis=1)
    gs["b_in"] = jnp.concatenate([s_u] + [sqkv[g][i] for i in range(3) for g in range(len(DILATIONS))] + [s_gs, s_ga],
                                 axis=1)
    gw["w_in"] = wgrad(h0, dproj, out_shards=True, name="in_proj_dw")
    sent = send(2, gw)
    dh0 = _mm(dproj, wts["w_in"], tb=True, b_shards=True, after=sent, name="in_proj_dx")
    grad_x, gs["ln_in_g"], gs["ln_in_b"], _ = _ln_bwd(dr1, dh0, xh0, rs0, sm["ln_in_g"], alpha=DEEPNORM_ALPHA,
                                                      name="ln_in_bwd")
    return loss_row, grad_x, gs


N_PEER = N_DEV - 1
_IN_HBM = pl.BlockSpec(memory_space=pltpu.HBM)
_IN_SEMAPHORE = pl.BlockSpec(memory_space=pltpu.SEMAPHORE)


def _device_index():
    return 4 * lax.axis_index("x") + 2 * lax.axis_index("y") + lax.axis_index("c")


def _exchange_copies(src_refs, land_refs, send_sems, recv_sems, scatter):
    x, y, c = lax.axis_index("x"), lax.axis_index("y"), lax.axis_index("c")
    me = 4 * x + 2 * y + c
    pairs = []
    for a, (src_ref, land_ref) in enumerate(zip(src_refs, land_refs)):
        for kk in range(1, N_DEV):
            px = (x + (kk >> 2)) % 2
            py = (y + ((kk >> 1) & 1)) % 2
            pc = (c + (kk & 1)) % 2
            peer = 4 * px + 2 * py + pc
            sem = a * N_PEER + kk - 1
            src = src_ref.at[peer] if scatter else src_ref

            def copy(dst, src=src, sem=sem, px=px, py=py, pc=pc):
                return pltpu.make_async_remote_copy(
                    src_ref=src, dst_ref=dst, send_sem=send_sems.at[sem], recv_sem=recv_sems.at[sem],
                    device_id=(px, py, pc), device_id_type=pl.DeviceIdType.MESH)

            pairs.append((copy(land_ref.at[me]), copy(land_ref.at[peer])))
    return pairs


def _exchange_start(srcs, *, scatter, name, after=None):
    n_arr = len(srcs)
    lands = [lax.empty((N_DEV,) + tuple(s.shape[1:] if scatter else s.shape), s.dtype) for s in srcs]
    n_in = 2 * n_arr + (after is not None)

    def body(*refs):
        send_sems, recv_sems = refs[n_in], refs[n_in + 1]
        for sent, _ in _exchange_copies(refs[:n_arr], refs[n_arr:2 * n_arr], send_sems, recv_sems, scatter):
            sent.start()
        refs[-1][...] = jnp.zeros_like(refs[-1])

    through = [pltpu.HBM(t.shape, t.dtype) for t in (*srcs, *lands)]
    res = pl.pallas_call(
        body, name=name,
        out_shape=(pltpu.SemaphoreType.DMA((n_arr * N_PEER,)), pltpu.SemaphoreType.DMA((n_arr * N_PEER,)), *through,
                   jax.ShapeDtypeStruct((8, LANES), F32)),
        in_specs=[_IN_HBM] * (2 * n_arr) + [pl.BlockSpec(memory_space=pl.ANY)] * (after is not None),
        out_specs=(_IN_SEMAPHORE, _IN_SEMAPHORE, *[_IN_HBM] * (2 * n_arr), pl.BlockSpec(memory_space=pltpu.VMEM)),
        input_output_aliases={i: 2 + i for i in range(2 * n_arr)},
        compiler_params=pltpu.CompilerParams(has_side_effects=pltpu.SideEffectType.DATAFLOW_SIDE_EFFECTING),
    )(*[pltpu.with_memory_space_constraint(t, pltpu.HBM) for t in (*srcs, *lands)],
      *([after] if after is not None else []))
    return (res[0], res[1], res[2:2 + n_arr], res[2 + n_arr:2 + 2 * n_arr], scatter), res[-1]


def _exchange_wait(handle, *, after, name):
    send_sems, recv_sems, srcs, lands, scatter = handle
    n_arr = len(srcs)

    def body(*refs):
        for sent, received in _exchange_copies(refs[:n_arr], refs[n_arr:2 * n_arr], refs[2 * n_arr], refs[2 * n_arr + 1],
                                               scatter):
            sent.wait_send()
            received.wait_recv()

    res = pl.pallas_call(
        body, name=name, out_shape=tuple(pltpu.HBM(t.shape, t.dtype) for t in (*srcs, *lands)),
        in_specs=[_IN_HBM] * (2 * n_arr) + [_IN_SEMAPHORE, _IN_SEMAPHORE, pl.BlockSpec(memory_space=pl.ANY)],
        out_specs=tuple([_IN_HBM] * (2 * n_arr)), input_output_aliases={i: i for i in range(2 * n_arr)},
        compiler_params=pltpu.CompilerParams(has_side_effects=pltpu.SideEffectType.DATAFLOW_SIDE_EFFECTING),
    )(*srcs, *lands, send_sems, recv_sems, after)
    return res[n_arr:]


def _with_own_slot(land, own):
    return lax.dynamic_update_slice_in_dim(land, own[None], _device_index(), axis=0)


def _reduce_adamw(gstack, w, m, v, *, name, tr=128):
    n_rows, cols = w.shape
    tr = min(tr, n_rows)
    assert n_rows % tr == 0, (name, n_rows, tr)

    def body(g_ref, w_ref, m_ref, v_ref, go_ref, d_ref, mo_ref, vo_ref):
        g = g_ref[0].astype(F32)
        for dev in range(1, N_DEV):
            g = g + g_ref[dev].astype(F32)
        m_new = ADAM_B1 * m_ref[...] + (1.0 - ADAM_B1) * g
        v_new = ADAM_B2 * v_ref[...] + (1.0 - ADAM_B2) * jnp.square(g)
        m_hat = m_new / (1.0 - ADAM_B1 ** ADAM_STEP)
        v_hat = v_new / (1.0 - ADAM_B2 ** ADAM_STEP)
        go_ref[...] = g
        d_ref[...] = -ADAM_LR * (m_hat / (jnp.sqrt(v_hat) + ADAM_EPS) + ADAM_WD * w_ref[...])
        mo_ref[...] = m_new
        vo_ref[...] = v_new

    flat = pl.BlockSpec((tr, cols), lambda i: (i, 0))
    shape = jax.ShapeDtypeStruct((n_rows, cols), F32)
    return pl.pallas_call(
        body, name=name, grid=(n_rows // tr,),
        in_specs=[pl.BlockSpec((N_DEV, tr, cols), lambda i: (0, i, 0)), flat, flat, flat],
        out_specs=[flat] * 4, out_shape=[shape] * 4, compiler_params=_params("parallel"),
    )(gstack, w, m, v)


def _pack(parts, dtype):
    flat = jnp.concatenate([p.reshape(-1).astype(dtype) for p in parts])
    unit = PACK_COLS * PACK_ROW_ALIGN
    total = -(-flat.shape[0] // unit) * unit
    return jnp.pad(flat, (0, total - flat.shape[0])).reshape(-1, PACK_COLS)


def _unpack(packed, shapes):
    flat = packed.reshape(-1)
    out, off = [], 0
    for s in shapes:
        size = int(np.prod(s))
        out.append(flat[off:off + size].reshape(s))
        off += size
    return out


def kernel(x, mem, positions, ln_in_g, ln_in_b, w_in, b_in, ssm_log_dt, ssm_a_re, ssm_a_im, ssm_b_re, ssm_b_im, ssm_c_re, ssm_c_im, ssm_d, w_glu, b_glu, w_att_up, w_mix_out, b_mix_out, ln1_g, ln1_b, w_xq, w_xkv, w_xo, ln2_g, ln2_b, w_ff1, b_ff1, w_ff2, b_ff2, ln3_g, ln3_b, loss_target, m_ln_in_g, m_ln_in_b, m_w_in, m_b_in, m_ssm_log_dt, m_ssm_a_re, m_ssm_a_im, m_ssm_b_re, m_ssm_b_im, m_ssm_c_re, m_ssm_c_im, m_ssm_d, m_w_glu, m_b_glu, m_w_att_up, m_w_mix_out, m_b_mix_out, m_ln1_g, m_ln1_b, m_w_xq, m_w_xkv, m_w_xo, m_ln2_g, m_ln2_b, m_w_ff1, m_b_ff1, m_w_ff2, m_b_ff2, m_ln3_g, m_ln3_b, v_ln_in_g, v_ln_in_b, v_w_in, v_b_in, v_ssm_log_dt, v_ssm_a_re, v_ssm_a_im, v_ssm_b_re, v_ssm_b_im, v_ssm_c_re, v_ssm_c_im, v_ssm_d, v_w_glu, v_b_glu, v_w_att_up, v_w_mix_out, v_b_mix_out, v_ln1_g, v_ln1_b, v_w_xq, v_w_xkv, v_w_xo, v_ln2_g, v_ln2_b, v_w_ff1, v_b_ff1, v_w_ff2, v_b_ff2, v_ln3_g, v_ln3_b):
    given = dict(locals())
    w_arg = {n: given[n] for n in WEIGHTS}
    m_arg = {n: given["m_" + n] for n in WEIGHTS}
    v_arg = {n: given["v_" + n] for n in WEIGHTS}

    shards = {n: w_arg[n][0].astype(MXU_DTYPE) for n in BIG}
    gathers, token = [], None
    for i, names in enumerate(GATHER_GROUPS):
        handle, token = _exchange_start([shards[n] for n in names], scatter=False, after=token, name=f"gather_start_{i}")
        gathers.append(handle)

    def fetch(i, after):
        lands = _exchange_wait(gathers[i], after=after, name=f"gather_wait_{i}")
        full = {n: _with_own_slot(land, shards[n]) for n, land in zip(GATHER_GROUPS[i], lands)}
        return {n: t if n in BIG_COL_SHARDED else t.reshape(-1, t.shape[-1]) for n, t in full.items()}

    scatters = {}

    def send(i, gw):
        slots = [gw[n] if n in BIG_COL_SHARDED else gw[n].reshape(N_DEV, -1, gw[n].shape[-1]) for n in SCATTER_GROUPS[i]]
        handle, sent = _exchange_start(slots, scatter=True, name=f"scatter_start_{i}")
        scatters[i] = (handle, slots)
        return sent

    sm = {}
    for n in SMALL:
        t = w_arg[n]
        if n.startswith("ssm_") and n not in ("ssm_d", "ssm_log_dt"):
            sm[n] = t[0]
        else:
            sm[n] = t.reshape(1, -1)

    loss_row, grad_x, gs = _local_grads(x[0], mem[0], positions.reshape(-1, 1), loss_target[0], sm, fetch, send, token)
    loss = lax.psum(loss_row[0, 0], ("x", "y", "c"))
    small = _pack([gs[n] for n in SMALL], F32)
    small_handle, _ = _exchange_start([small], scatter=False, name="small_start")

    results = [{}, {}, {}, {}]
    done = grad_x
    for i, names in enumerate(SCATTER_GROUPS):
        handle, slots = scatters[i]
        lands = _exchange_wait(handle, after=done, name=f"scatter_wait_{i}")
        for n, land, slot in zip(names, lands, slots):
            own = lax.dynamic_index_in_dim(slot, _device_index(), axis=0, keepdims=False)
            res = _reduce_adamw(_with_own_slot(land, own), w_arg[n][0], m_arg[n][0], v_arg[n][0], name="adamw_" + n)
            done = res[0]
            for d, r in zip(results, res):
                d[n] = r[None]
    small_stack = _with_own_slot(_exchange_wait(small_handle, after=done, name="small_wait")[0], small)
    small_shapes = [w_arg[n].shape for n in SMALL]
    res = _reduce_adamw(small_stack, *[_pack([d[n] for n in SMALL], F32) for d in (w_arg, m_arg, v_arg)],
                        name="adamw_small")
    for d, r in zip(results, res):
        d.update(zip(SMALL, _unpack(r, small_shapes)))
    out = [loss, grad_x[None]]
    for d in results:
        out += [d[n] for n in WEIGHTS]
    return tuple(out)
```

```python
import functools

import numpy as np
import jax
import jax.numpy as jnp
from jax import lax
from jax.experimental import pallas as pl
from jax.experimental.pallas import tpu as pltpu

F32 = jnp.float32
MXU_DTYPE = jnp.bfloat16
WIRE_DTYPE = jnp.bfloat16
VMEM_LIMIT_BYTES = 48 * 1024 * 1024
LANES = 128

N_DEV = 8
D_MODEL = 1024
SSM_GROUP = 16
SSM_WIDTH = 768
SSM_GROUPS = SSM_WIDTH // SSM_GROUP
SSM_STATE = 64
SSM_CH = SSM_GROUPS * SSM_STATE
SSM_TILES = SSM_WIDTH // LANES
GROUPS_PER_TILE = LANES // SSM_GROUP
STATE_VREG_ROWS = SSM_CH // LANES
ATT_HEAD_DIM = 64
ATT_HEADS_PER_GROUP = 4
ATT_MERGED = ATT_HEADS_PER_GROUP * ATT_HEAD_DIM
DILATIONS = (1, 4, 16)
ATT_BLK = 128
ATT_SCALE = ATT_HEAD_DIM ** -0.5
ROT_DIM = ATT_HEAD_DIM // 4
ROPE_THETA = 500000.0
XATT_HEADS = 4
XATT_HEAD_DIM = D_MODEL // XATT_HEADS
XATT_SCALE = XATT_HEAD_DIM ** -0.5
DEEPNORM_ALPHA = 2.0 ** 0.25
LN_EPS = 1e-5
NEG_INF = -1e30
OFF_Q_BLK, OFF_K_BLK, OFF_V_BLK = 3, 6, 9
OFF_GS_BLK, OFF_GA_BLK = 3, 4

ADAM_LR = 0.001
ADAM_B1 = 0.9
ADAM_B2 = 0.999
ADAM_EPS = 1e-08
ADAM_WD = 0.01
ADAM_STEP = 10

BIG = ("w_in", "w_glu", "w_att_up", "w_mix_out", "w_xq", "w_xkv", "w_xo", "w_ff1", "w_ff2")
BIG_COL_SHARDED = ("w_in", "w_glu", "w_att_up", "w_xkv", "w_ff1")
WEIGHTS = ("ln_in_g", "ln_in_b", "w_in", "b_in", "ssm_log_dt", "ssm_a_re", "ssm_a_im", "ssm_b_re", "ssm_b_im",
           "ssm_c_re", "ssm_c_im", "ssm_d", "w_glu", "b_glu", "w_att_up", "w_mix_out", "b_mix_out", "ln1_g", "ln1_b",
           "w_xq", "w_xkv", "w_xo", "ln2_g", "ln2_b", "w_ff1", "b_ff1", "w_ff2", "b_ff2", "ln3_g", "ln3_b")
SMALL = tuple(n for n in WEIGHTS if n not in BIG)
PACK_COLS = 1024
PACK_ROW_ALIGN = 256


def _params(*sem):
    return pltpu.CompilerParams(dimension_semantics=sem, vmem_limit_bytes=VMEM_LIMIT_BYTES)


def _dot(a, b, ca, cb):
    return lax.dot_general(a.astype(MXU_DTYPE), b.astype(MXU_DTYPE), (((ca,), (cb,)), ((), ())),
                           preferred_element_type=F32)


def _fit(dim, pref):
    if dim <= pref:
        return dim
    best = max(t for t in range(LANES, pref + 1, LANES) if dim % t == 0)
    return best


def _mm(a, b, *, name, ta=False, tb=False, bias=None, out_dtype=F32, b_shards=False, out_shards=False, after=None,
        also=None, gate=None, colsum=False, tm=1024, tn=1024, tk=1024):
    m, k = (a.shape[1], a.shape[0]) if ta else a.shape
    order = (lambda f: (lambda j, i, kk: f(i, j, kk))) if colsum else (lambda f: f)
    spec = lambda shape, f: pl.BlockSpec(shape, order(f))
    if b_shards:
        n_sh, rows, n_loc = b.shape
        if tb:
            n, tn, tk = rows, _fit(rows, tn), n_loc
            assert k == n_sh * n_loc, (name, k, b.shape)
            b_spec = spec((1, tn, tk), lambda i, j, kk: (kk, j, 0))
        else:
            n, tn, tk = n_sh * n_loc, n_loc, _fit(k, tk)
            b_spec = spec((1, tk, tn), lambda i, j, kk: (j, kk, 0))
    else:
        n = b.shape[0] if tb else b.shape[1]
        tn = n // N_DEV if out_shards else _fit(n, tn)
        tk = _fit(k, tk)
        b_spec = spec((tn, tk), lambda i, j, kk: (j, kk)) if tb else spec((tk, tn), lambda i, j, kk: (kk, j))
    tm = _fit(m, tm)
    nk = k // tk
    a_spec = spec((tk, tm), lambda i, j, kk: (kk, i)) if ta else spec((tm, tk), lambda i, j, kk: (i, kk))
    tile = spec((tm, tn), lambda i, j, kk: (i, j))
    in_specs, args = [a_spec, b_spec], [a, b]
    if bias is not None:
        in_specs.append(spec((1, tn), lambda i, j, kk: (0, j)))
        args.append(bias)
    if gate is not None:
        in_specs.append(tile)
        args.append(gate[0])
    if after is not None:
        in_specs.append(pl.BlockSpec(memory_space=pl.ANY))
        args.append(after)
    n_in = len(args)
    if out_shards:
        assert n == N_DEV * tn, (name, n, tn)
        out_specs = [spec((1, tm, tn), lambda i, j, kk: (j, i, 0))]
        out_shape = [jax.ShapeDtypeStruct((N_DEV, m, tn), out_dtype)]
    else:
        out_specs = [tile]
        out_shape = [jax.ShapeDtypeStruct((m, n), out_dtype)]
    if also is not None:
        out_specs.append(tile)
        out_shape.append(jax.ShapeDtypeStruct((m, n), also[1]))
    if colsum:
        out_specs.append(spec((1, tn), lambda i, j, kk: (0, j)))
        out_shape.append(jax.ShapeDtypeStruct((1, n), F32))

    def body(*refs):
        a_ref, b_ref = refs[0], refs[1]
        o_ref = refs[n_in]

        def product():
            return _dot(a_ref[...], b_ref[0] if b_shards else b_ref[...], 0 if ta else 1, 1 if tb else 0)

        def finish(r):
            if bias is not None:
                r = r + refs[2][...]
            if gate is not None:
                r = r * gate[1](refs[2 + (bias is not None)][...])
            if out_shards:
                o_ref[0] = r.astype(o_ref.dtype)
            else:
                o_ref[...] = r.astype(o_ref.dtype)
            if also is not None:
                refs[n_in + 1][...] = also[0](r).astype(also[1])
            if colsum:
                s_ref = refs[n_in + 1 + (also is not None)]

                @pl.when(pl.program_id(1) == 0)
                def _():
                    s_ref[...] = jnp.zeros_like(s_ref)

                s_ref[...] += _colsum(r)

        if nk == 1:
            finish(product())
            return
        acc_ref = refs[-1]
        kk = pl.program_id(2)

        @pl.when(kk == 0)
        def _():
            acc_ref[...] = jnp.zeros_like(acc_ref)

        acc_ref[...] += product()

        @pl.when(kk == nk - 1)
        def _():
            finish(acc_ref[...])

    grid = (n // tn, m // tm, nk) if colsum else (m // tm, n // tn, nk)
    res = pl.pallas_call(
        body, name=name, grid=grid, in_specs=in_specs, out_specs=out_specs, out_shape=out_shape,
        scratch_shapes=[pltpu.VMEM((tm, tn), F32)] if nk > 1 else [],
        compiler_params=_params("parallel", "arbitrary" if colsum else "parallel", "arbitrary"),
    )(*args)
    return res[0] if len(res) == 1 else res


def _rowcall(fn, rows, fulls, row_outs, acc_outs=(), *, n_rows, tm, name, after=None):
    n_r, n_f, n_o, n_a = len(rows), len(fulls), len(row_outs), len(acc_outs)
    n_in = n_r + n_f + (after is not None)
    assert n_rows % tm == 0, (name, n_rows, tm)

    def body(*refs):
        res = fn(*[r[...] for r in refs[:n_r + n_f]])
        res = tuple(res) if isinstance(res, (tuple, list)) else (res,)
        o_refs = refs[n_in:n_in + n_o]
        a_refs = refs[n_in + n_o:]
        for o_ref, val in zip(o_refs, res[:n_o]):
            o_ref[...] = val.astype(o_ref.dtype)
        if n_a:
            @pl.when(pl.program_id(0) == 0)
            def _():
                for a_ref in a_refs:
                    a_ref[...] = jnp.zeros_like(a_ref)

            for a_ref, val in zip(a_refs, res[n_o:]):
                a_ref[...] += val

    in_specs = [pl.BlockSpec((tm, w), functools.partial(lambda i, cb: (i, cb), cb=cb)) for _, w, cb in rows]
    in_specs += [pl.BlockSpec(f.shape, functools.partial(lambda i, nd: (0,) * nd, nd=f.ndim)) for f in fulls]
    in_specs += [pl.BlockSpec(memory_space=pl.ANY)] * (after is not None)
    out_specs = [pl.BlockSpec((tm, w), lambda i: (i, 0)) for w, _ in row_outs]
    out_specs += [pl.BlockSpec((1, w), lambda i: (0, 0)) for w in acc_outs]
    out_shape = [jax.ShapeDtypeStruct((n_rows, w), dt) for w, dt in row_outs]
    out_shape += [jax.ShapeDtypeStruct((1, w), F32) for w in acc_outs]
    return pl.pallas_call(
        body, name=name, grid=(n_rows // tm,), in_specs=in_specs, out_specs=out_specs, out_shape=out_shape,
        compiler_params=_params("arbitrary" if n_a else "parallel"),
    )(*[r[0] for r in rows], *fulls, *([after] if after is not None else []))


def _colsum(v):
    return jnp.sum(v, axis=0, keepdims=True)


def _ln_fwd(a, r, g, b, *, alpha, name):
    n_rows, d = a.shape

    def fn(*t):
        xin = t[0] if alpha == 1.0 else alpha * t[0]
        if r is not None:
            xin = xin + t[1]
        gv, bv = t[-2], t[-1]
        mu = jnp.mean(xin, axis=-1, keepdims=True)
        xc = xin - mu
        var = jnp.mean(xc * xc, axis=-1, keepdims=True)
        rstd = lax.rsqrt(var + LN_EPS)
        xh = xc * rstd
        return xh * gv + bv, xh, rstd

    rows = [(a, d, 0)] + ([(r, d, 0)] if r is not None else [])
    return _rowcall(fn, rows, [g, b], [(d, F32), (d, F32), (1, F32)], n_rows=n_rows, tm=256, name=name)


def _ln_bwd(dya, dyb, xh, rstd, g, *, alpha, name):
    n_rows, d = xh.shape

    def fn(*t):
        if dya is not None:
            dy = alpha * t[0] + t[1]
            xhv, rs, gv = t[2], t[3], t[4]
        else:
            dy, xhv, rs, gv = t[0], t[1], t[2], t[3]
        dyg = dy * gv
        m1 = jnp.mean(dyg, axis=-1, keepdims=True)
        m2 = jnp.mean(dyg * xhv, axis=-1, keepdims=True)
        dx = rs * (dyg - m1 - xhv * m2)
        return dx, _colsum(dy * xhv), _colsum(dy), _colsum(dx)

    rows = ([(dya, d, 0)] if dya is not None else []) + [(dyb, d, 0), (xh, d, 0), (rstd, 1, 0)]
    return _rowcall(fn, rows, [g], [(d, F32)], [d, d, d], n_rows=n_rows, tm=256, name=name)


def _loss_head(y, target, *, name):
    n_rows, d = y.shape

    def fn(yv, tv):
        diff = yv - tv
        part = jnp.sum(jnp.sum(diff * diff, axis=1, keepdims=True), axis=0, keepdims=True) * (0.5 / d)
        return diff * (1.0 / d), jnp.broadcast_to(part, (1, LANES))

    return _rowcall(fn, [(y, d, 0), (target, d, 0)], [], [(d, F32)], [LANES], n_rows=n_rows, tm=256, name=name)


def _rope_lane_constants():
    lane = np.arange(ATT_MERGED)
    in_head = lane % ATT_HEAD_DIM
    sign = np.where(in_head < ROT_DIM // 2, -1.0, np.where(in_head < ROT_DIM, 1.0, 0.0)).astype(np.float32)
    inv_freq = ROPE_THETA ** (-jnp.arange(0, ROT_DIM, 2, dtype=F32) / ROT_DIM)
    return inv_freq[lane % (ROT_DIM // 2)].reshape(1, ATT_MERGED), jnp.asarray(sign).reshape(1, ATT_MERGED)


def _rope_tables(pos_col, *, name, after=None):
    inv_lane, sign = _rope_lane_constants()

    def fn(pos, inv, sg):
        ang = pos.astype(F32) * inv
        return jnp.where(sg != 0.0, jnp.cos(ang), 1.0), sg * jnp.sin(ang)

    return _rowcall(fn, [(pos_col, 1, 0)], [inv_lane, sign], [(ATT_MERGED, F32), (ATT_MERGED, F32)],
                    n_rows=pos_col.shape[0], tm=512, name=name, after=after)


def _rot_partner(t):
    lane = lax.broadcasted_iota(jnp.int32, t.shape, 1)
    width = t.shape[1]
    return jnp.where((lane & (ROT_DIM // 2)) == 0, pltpu.roll(t, width - ROT_DIM // 2, 1), pltpu.roll(t, ROT_DIM // 2, 1))


def _rope(t, cos_t, sin_t):
    return t * cos_t + _rot_partner(t) * sin_t


def _rope_transpose(dt, cos_t, sin_t):
    return dt * cos_t + _rot_partner(dt * sin_t)


def _qkv_split(proj, cos_t, sin_t, *, name):
    n_rows = proj.shape[0]
    n_g = len(DILATIONS)

    def fn(*t):
        c, s = t[3 * n_g], t[3 * n_g + 1]
        out = [_rope(t[g], c, s) for g in range(n_g)]
        out += [_rope(t[n_g + g], c, s) for g in range(n_g)]
        out += [t[2 * n_g + g] for g in range(n_g)]
        return out

    rows = [(proj, ATT_MERGED, off + g) for off in (OFF_Q_BLK, OFF_K_BLK, OFF_V_BLK) for g in range(n_g)]
    rows += [(cos_t, ATT_MERGED, 0), (sin_t, ATT_MERGED, 0)]
    outs = _rowcall(fn, rows, [], [(ATT_MERGED, MXU_DTYPE)] * (3 * n_g), n_rows=n_rows, tm=512, name=name)
    return outs[:n_g], outs[n_g:2 * n_g], outs[2 * n_g:]


def _mix(gs, ga, z1, z2, b_att):
    return jax.nn.sigmoid(gs) * (z1 * jax.nn.sigmoid(z2)) + jax.nn.sigmoid(ga) * b_att


def _mix_rows(proj, z, b_att):
    return [(proj, D_MODEL, OFF_GS_BLK), (proj, D_MODEL, OFF_GA_BLK), (z, D_MODEL, 0), (z, D_MODEL, 1), (b_att, D_MODEL, 0)]


def _mix_fwd(proj, z, b_att, *, name):
    return _rowcall(_mix, _mix_rows(proj, z, b_att), [], [(D_MODEL, MXU_DTYPE)],
                    n_rows=proj.shape[0], tm=256, name=name)[0]


def _mix_bwd(dmixed, proj, z, b_att, *, name):
    def fn(dm, gs, ga, z1, z2, ba):
        _, vjp = jax.vjp(_mix, gs, ga, z1, z2, ba)
        dgs, dga, dz1, dz2, dba = vjp(dm)
        dz = jnp.concatenate([dz1, dz2], axis=1)
        return dgs, dga, dz, dba, _colsum(dgs), _colsum(dga), _colsum(dz)

    rows = [(dmixed, D_MODEL, 0)] + _mix_rows(proj, z, b_att)
    widths = [D_MODEL, D_MODEL, 2 * D_MODEL, D_MODEL]
    return _rowcall(fn, rows, [], [(w, MXU_DTYPE) for w in widths], widths[:3], n_rows=proj.shape[0], tm=256, name=name)


def _gelu_bwd(dgy, y, proj, *, name):
    def fn(dg, yv, u):
        _, vjp = jax.vjp(jax.nn.gelu, yv)
        dy = vjp(dg)[0]
        return dy, _colsum(dy * u)

    return _rowcall(fn, [(dgy, SSM_WIDTH, 0), (y, SSM_WIDTH, 0), (proj, SSM_WIDTH, 0)], [], [(SSM_WIDTH, F32)],
                    [SSM_WIDTH], n_rows=y.shape[0], tm=512, name=name)


def _dilated_view(t, dil):
    return t.reshape(t.shape[0] // dil, dil * ATT_MERGED)


def _head(h):
    return slice(h * ATT_HEAD_DIM, (h + 1) * ATT_HEAD_DIM)


def _band_mask(first_key):
    qi = lax.broadcasted_iota(jnp.int32, (ATT_BLK, 2 * ATT_BLK), 0)
    ki = lax.broadcasted_iota(jnp.int32, (ATT_BLK, 2 * ATT_BLK), 1)
    steps = qi + ATT_BLK - ki
    return (steps >= 0) & (steps <= ATT_BLK) & (ki >= first_key)


def _dil_fwd(q, k, v, dil, *, name):
    n_rows = q.shape[0]
    n_blk = n_rows // dil // ATT_BLK
    cur = pl.BlockSpec((ATT_BLK, ATT_MERGED), lambda r, n: (n, r))
    prev = pl.BlockSpec((ATT_BLK, ATT_MERGED), lambda r, n: (jnp.maximum(n - 1, 0), r))

    def body(q_ref, kp_ref, kc_ref, vp_ref, vc_ref, o_ref, l_ref):
        valid = _band_mask(jnp.where(pl.program_id(1) > 0, 0, ATT_BLK))
        for h in range(ATT_HEADS_PER_GROUP):
            sl = _head(h)
            keys = jnp.concatenate([kp_ref[:, sl], kc_ref[:, sl]], axis=0)
            vals = jnp.concatenate([vp_ref[:, sl], vc_ref[:, sl]], axis=0)
            s = jnp.where(valid, _dot(q_ref[:, sl], keys, 1, 1) * ATT_SCALE, NEG_INF)
            m = jnp.max(s, axis=-1, keepdims=True)
            p = jnp.exp(s - m)
            den = jnp.sum(p, axis=-1, keepdims=True)
            o_ref[:, sl] = _dot(p, vals, 1, 0) / den
            l_ref[:, sl] = jnp.broadcast_to(m + jnp.log(den), (ATT_BLK, ATT_HEAD_DIM))

    shape = jax.ShapeDtypeStruct((n_rows // dil, dil * ATT_MERGED), F32)
    o, lse = pl.pallas_call(
        body, name=name, grid=(dil, n_blk), in_specs=[cur, prev, cur, prev, cur], out_specs=[cur, cur],
        out_shape=[shape, shape], compiler_params=_params("parallel", "parallel"),
    )(_dilated_view(q, dil), _dilated_view(k, dil), _dilated_view(k, dil), _dilated_view(v, dil), _dilated_view(v, dil))
    return o.reshape(n_rows, ATT_MERGED), lse.reshape(n_rows, ATT_MERGED)


def _att_merge(outs, lses, *, name):
    n_g = len(outs)

    def fn(*t):
        o, l = t[:n_g], t[n_g:]
        m = functools.reduce(jnp.maximum, l)
        e = [jnp.exp(li - m) for li in l]
        z = functools.reduce(jnp.add, e)
        att = functools.reduce(jnp.add, [(ei / z) * oi for ei, oi in zip(e, o)])
        return att, m + jnp.log(z)

    rows = [(t, ATT_MERGED, 0) for t in (*outs, *lses)]
    return _rowcall(fn, rows, [], [(ATT_MERGED, F32), (ATT_MERGED, F32)], n_rows=outs[0].shape[0], tm=512, name=name)


def _dil_bwd(q, k, v, datt, att, lse, cos_t, sin_t, dil, *, name):
    n_rows = q.shape[0]
    n_blk = n_rows // dil // ATT_BLK
    cur = pl.BlockSpec((ATT_BLK, ATT_MERGED), lambda r, n: (n, r))
    prev = pl.BlockSpec((ATT_BLK, ATT_MERGED), lambda r, n: (jnp.maximum(n - 1, 0), r))
    nxt = pl.BlockSpec((ATT_BLK, ATT_MERGED), lambda r, n: (jnp.minimum(n + 1, n_blk - 1), r))
    acc = pl.BlockSpec((1, ATT_MERGED), lambda r, n: (0, 0))

    def body(qc_ref, qn_ref, kp_ref, kc_ref, vp_ref, vc_ref, dc_ref, dn_ref, ac_ref, an_ref, lc_ref, ln_ref,
             cos_ref, sin_ref, dq_ref, dk_ref, dv_ref, sq_ref, sk_ref, sv_ref, dq_s, dk_s, dv_s):
        n = pl.program_id(1)

        @pl.when((pl.program_id(0) == 0) & (n == 0))
        def _():
            for s_ref in (sq_ref, sk_ref, sv_ref):
                s_ref[...] = jnp.zeros_like(s_ref)

        valid = _band_mask(jnp.where(n > 0, 0, ATT_BLK))
        qi = lax.broadcasted_iota(jnp.int32, (ATT_BLK, ATT_BLK), 0)
        ki = lax.broadcasted_iota(jnp.int32, (ATT_BLK, ATT_BLK), 1)
        valid_next = (ki - qi) >= jnp.where(n < n_blk - 1, 0, ATT_BLK)
        for h in range(ATT_HEADS_PER_GROUP):
            sl = _head(h)
            lane0 = slice(h * ATT_HEAD_DIM, h * ATT_HEAD_DIM + 1)
            qc, kc, vc = qc_ref[:, sl], kc_ref[:, sl], vc_ref[:, sl]
            keys = jnp.concatenate([kp_ref[:, sl], kc], axis=0)
            vals = jnp.concatenate([vp_ref[:, sl], vc], axis=0)
            dc = dc_ref[:, sl]
            delta = jnp.sum(dc * ac_ref[:, sl], axis=-1, keepdims=True)
            p = jnp.where(valid, jnp.exp(_dot(qc, keys, 1, 1) * ATT_SCALE - lc_ref[:, lane0]), 0.0)
            ds = p * (_dot(dc, vals, 1, 1) - delta) * ATT_SCALE
            dq_s[:, sl] = _dot(ds, keys, 1, 0)
            qn, dn = qn_ref[:, sl], dn_ref[:, sl]
            delta_n = jnp.sum(dn * an_ref[:, sl], axis=-1, keepdims=True)
            p_n = jnp.where(valid_next, jnp.exp(_dot(qn, kc, 1, 1) * ATT_SCALE - ln_ref[:, lane0]), 0.0)
            ds_n = p_n * (_dot(dn, vc, 1, 1) - delta_n) * ATT_SCALE
            dv_s[:, sl] = _dot(p[:, ATT_BLK:], dc, 0, 0) + _dot(p_n, dn, 0, 0)
            dk_s[:, sl] = _dot(ds[:, ATT_BLK:], qc, 0, 0) + _dot(ds_n, qn, 0, 0)
        cos_v, sin_v = cos_ref[...], sin_ref[...]
        dq = _rope_transpose(dq_s[...], cos_v, sin_v)
        dk = _rope_transpose(dk_s[...], cos_v, sin_v)
        dv = dv_s[...]
        dq_ref[...] = dq.astype(dq_ref.dtype)
        dk_ref[...] = dk.astype(dk_ref.dtype)
        dv_ref[...] = dv.astype(dv_ref.dtype)
        sq_ref[...] += _colsum(dq)
        sk_ref[...] += _colsum(dk)
        sv_ref[...] += _colsum(dv)

    view = functools.partial(_dilated_view, dil=dil)
    shape = jax.ShapeDtypeStruct((n_rows // dil, dil * ATT_MERGED), MXU_DTYPE)
    sums = jax.ShapeDtypeStruct((1, ATT_MERGED), F32)
    dq, dk, dv, sq, sk, sv = pl.pallas_call(
        body, name=name, grid=(dil, n_blk),
        in_specs=[cur, nxt, prev, cur, prev, cur, cur, nxt, cur, nxt, cur, nxt, cur, cur],
        out_specs=[cur, cur, cur, acc, acc, acc], out_shape=[shape, shape, shape, sums, sums, sums],
        scratch_shapes=[pltpu.VMEM((ATT_BLK, ATT_MERGED), F32)] * 3,
        compiler_params=_params("arbitrary", "arbitrary"),
    )(view(q), view(q), view(k), view(k), view(v), view(v), view(datt), view(datt), view(att), view(att),
      view(lse), view(lse), view(cos_t), view(sin_t))
    return [t.reshape(n_rows, ATT_MERGED) for t in (dq, dk, dv)], [sq, sk, sv]


def _xhead(h):
    return slice(h * XATT_HEAD_DIM, (h + 1) * XATT_HEAD_DIM)


def _xatt_probs(qh, kh):
    s = _dot(qh, kh, 1, 1) * XATT_SCALE
    e = jnp.exp(s - jnp.max(s, axis=-1, keepdims=True))
    return e / jnp.sum(e, axis=-1, keepdims=True)


def _xatt_fwd(q, kv, *, name, tm=512):
    n_rows = q.shape[0]
    n_mem = kv.shape[0]

    def body(q_ref, kv_ref, o_ref):
        for h in range(XATT_HEADS):
            sl = _xhead(h)
            p = _xatt_probs(q_ref[:, sl], kv_ref[:, sl])
            o_ref[:, sl] = _dot(p, kv_ref[:, D_MODEL + h * XATT_HEAD_DIM:D_MODEL + (h + 1) * XATT_HEAD_DIM], 1, 0
                                ).astype(o_ref.dtype)

    row = pl.BlockSpec((tm, D_MODEL), lambda i: (i, 0))
    return pl.pallas_call(
        body, name=name, grid=(n_rows // tm,),
        in_specs=[row, pl.BlockSpec((n_mem, 2 * D_MODEL), lambda i: (0, 0))], out_specs=row,
        out_shape=jax.ShapeDtypeStruct((n_rows, D_MODEL), MXU_DTYPE), compiler_params=_params("parallel"),
    )(q, kv)


def _xatt_bwd(q, kv, do, *, name, tm=512):
    n_rows = q.shape[0]
    n_mem = kv.shape[0]

    def body(q_ref, kv_ref, do_ref, dq_ref, dkv_ref):
        @pl.when(pl.program_id(0) == 0)
        def _():
            dkv_ref[...] = jnp.zeros_like(dkv_ref)

        for h in range(XATT_HEADS):
            sl = _xhead(h)
            vsl = slice(D_MODEL + h * XATT_HEAD_DIM, D_MODEL + (h + 1) * XATT_HEAD_DIM)
            qh, kh, doh = q_ref[:, sl], kv_ref[:, sl], do_ref[:, sl]
            p = _xatt_probs(qh, kh)
            dp = _dot(doh, kv_ref[:, vsl], 1, 1)
            ds = p * (dp - jnp.sum(dp * p, axis=-1, keepdims=True)) * XATT_SCALE
            dq_ref[:, sl] = _dot(ds, kh, 1, 0).astype(dq_ref.dtype)
            dkv_ref[:, sl] += _dot(ds, qh, 0, 0)
            dkv_ref[:, vsl] += _dot(p, doh, 0, 0)

    row = pl.BlockSpec((tm, D_MODEL), lambda i: (i, 0))
    full = pl.BlockSpec((n_mem, 2 * D_MODEL), lambda i: (0, 0))
    return pl.pallas_call(
        body, name=name, grid=(n_rows // tm,), in_specs=[row, full, row], out_specs=[row, full],
        out_shape=[jax.ShapeDtypeStruct((n_rows, D_MODEL), MXU_DTYPE), jax.ShapeDtypeStruct((n_mem, 2 * D_MODEL), F32)],
        compiler_params=_params("arbitrary"),
    )(q, kv, do)


def _disc(logdt, a_re, a_im, b_re, b_im):
    dt = jnp.exp(logdt)
    mag = jnp.exp(a_re * dt)
    ab_re = mag * jnp.cos(a_im * dt)
    ab_im = mag * jnp.sin(a_im * dt)
    den = jnp.square(a_re) + jnp.square(a_im)
    nr = ab_re - 1.0
    f_re = (nr * a_re + ab_im * a_im) / den
    f_im = (ab_im * a_re - nr * a_im) / den
    bb_re = f_re[None] * b_re - f_im[None] * b_im
    bb_im = f_re[None] * b_im + f_im[None] * b_re
    return ab_re, ab_im, bb_re, bb_im


def _disc_transpose(logdt, a_re, a_im, b_re, b_im, g_ab_re, g_ab_im, g_bb_re, g_bb_im):
    dt = jnp.exp(logdt)
    mag = jnp.exp(a_re * dt)
    th = a_im * dt
    cs, sn = jnp.cos(th), jnp.sin(th)
    ab_re, ab_im = mag * cs, mag * sn
    den = jnp.square(a_re) + jnp.square(a_im)
    nr = ab_re - 1.0
    f_re = (nr * a_re + ab_im * a_im) / den
    f_im = (ab_im * a_re - nr * a_im) / den
    d_f_re = jnp.sum(g_bb_re * b_re + g_bb_im * b_im, axis=0)
    d_f_im = jnp.sum(g_bb_im * b_re - g_bb_re * b_im, axis=0)
    d_b_re = g_bb_re * f_re[None] + g_bb_im * f_im[None]
    d_b_im = g_bb_im * f_re[None] - g_bb_re * f_im[None]
    d_n_re, d_n_im = d_f_re / den, d_f_im / den
    d_den = -(d_f_re * f_re + d_f_im * f_im) / den
    d_ab_re = g_ab_re + d_n_re * a_re - d_n_im * a_im
    d_ab_im = g_ab_im + d_n_re * a_im + d_n_im * a_re
    d_a_re = d_n_re * nr + d_n_im * ab_im + 2.0 * d_den * a_re
    d_a_im = d_n_re * ab_im - d_n_im * nr + 2.0 * d_den * a_im
    d_mag = d_ab_re * cs + d_ab_im * sn
    d_th = mag * (d_ab_im * cs - d_ab_re * sn)
    d_a_re = d_a_re + d_mag * mag * dt
    d_a_im = d_a_im + d_th * dt
    d_dt = jnp.sum(d_mag * mag * a_re + d_th * a_im, axis=-1, keepdims=True)
    return d_dt * dt, d_a_re, d_a_im, d_b_re, d_b_im


def _whole(fn, args, out_shapes, *, name):
    n_in = len(args)

    def body(*refs):
        res = fn(*[r[...] for r in refs[:n_in]])
        for o_ref, val in zip(refs[n_in:], res):
            o_ref[...] = val

    return pl.pallas_call(body, name=name, out_shape=[jax.ShapeDtypeStruct(s, F32) for s in out_shapes],
                          compiler_params=pltpu.CompilerParams(vmem_limit_bytes=VMEM_LIMIT_BYTES))(*args)


def _tiles_cn(t):
    t = t.reshape(SSM_TILES, GROUPS_PER_TILE, SSM_GROUP, SSM_STATE)
    eye = jnp.eye(GROUPS_PER_TILE, dtype=t.dtype)
    return (t[:, :, :, None, :] * eye[None, :, None, :, None]).reshape(SSM_TILES, LANES, GROUPS_PER_TILE * SSM_STATE)


def _tiles_nc(t):
    t = t.reshape(SSM_TILES, GROUPS_PER_TILE, SSM_STATE, SSM_GROUP)
    eye = jnp.eye(GROUPS_PER_TILE, dtype=t.dtype)
    return (t[:, :, :, None, :] * eye[None, :, None, :, None]).reshape(SSM_TILES, GROUPS_PER_TILE * SSM_STATE, LANES)


def _untile_cn(t):
    t = t.reshape(SSM_TILES, GROUPS_PER_TILE, SSM_GROUP, GROUPS_PER_TILE, SSM_STATE)
    eye = jnp.eye(GROUPS_PER_TILE, dtype=t.dtype)
    return jnp.sum(t * eye[None, :, None, :, None], axis=3).reshape(SSM_GROUPS, SSM_GROUP, SSM_STATE)


SSM_WIDE = GROUPS_PER_TILE * SSM_STATE
LANE_GROUPS_PER_TILE = SSM_WIDE // LANES


def _chan(j):
    return slice(j * LANES, (j + 1) * LANES)


def _time_major_rows(j, q, tc):
    return pl.ds(j * LANE_GROUPS_PER_TILE + q, tc, stride=STATE_VREG_ROWS)


def _to_time_major(x, t_re_ref, t_im_ref, dst_re, dst_im, tc):
    for j in range(SSM_TILES):
        xj = x[:, _chan(j)]
        for t_ref, dst in ((t_re_ref, dst_re), (t_im_ref, dst_im)):
            r = _dot(xj, t_ref[j], 1, 0)
            for q in range(LANE_GROUPS_PER_TILE):
                dst[_time_major_rows(j, q, tc), :] = r[:, q * LANES:(q + 1) * LANES]


def _from_time_major(src, j, tc):
    return jnp.concatenate([src[_time_major_rows(j, q, tc), :] for q in range(LANE_GROUPS_PER_TILE)], axis=1)


def _scan_chunk(w_re, w_im, h_re, h_im, a_re, a_im, start, tc):
    def step(t, carry):
        hr, hi = carry
        rows = _scan_rows(t)
        nr = a_re * hr - a_im * hi + w_re[rows, :]
        ni = a_re * hi + a_im * hr + w_im[rows, :]
        h_re[rows, :] = nr
        h_im[rows, :] = ni
        return nr, ni

    return lax.fori_loop(0, tc, step, start, unroll=8)


SSM_CHUNK = 128


def _ssm_fwd(proj, tb_re, tb_im, tc_re, tc_im, a_re, a_im, gain, *, name, tc=SSM_CHUNK):
    n_rows = proj.shape[0]
    n_chunk = n_rows // tc

    def body(u_ref, tbr_ref, tbi_ref, tcr_ref, tci_ref, ar_ref, ai_ref, g_ref, y_ref, gy_ref, sbr_ref, sbi_ref,
             wr, wi, hr, hi, state):
        @pl.when(pl.program_id(0) == 0)
        def _():
            state[...] = jnp.zeros_like(state)

        sbr_ref[0] = state[0]
        sbi_ref[0] = state[1]
        u = u_ref[...]
        _to_time_major(u, tbr_ref, tbi_ref, wr, wi, tc)
        state[0], state[1] = _scan_chunk(wr, wi, hr, hi, ar_ref[...], ai_ref[...], (state[0], state[1]), tc)
        for j in range(SSM_TILES):
            yj = (_dot(_from_time_major(hr, j, tc), tcr_ref[j], 1, 0) + _dot(_from_time_major(hi, j, tc), tci_ref[j], 1, 0)
                  + g_ref[:, _chan(j)] * u[:, _chan(j)])
            y_ref[:, _chan(j)] = yj
            gy_ref[:, _chan(j)] = jax.nn.gelu(yj).astype(gy_ref.dtype)

    rows = pl.BlockSpec((tc, SSM_WIDTH), lambda i: (i, 0))
    in_tile = pl.BlockSpec((SSM_TILES, LANES, SSM_WIDE), lambda i: (0, 0, 0))
    out_tile = pl.BlockSpec((SSM_TILES, SSM_WIDE, LANES), lambda i: (0, 0, 0))
    coef = pl.BlockSpec((STATE_VREG_ROWS, LANES), lambda i: (0, 0))
    bound = pl.BlockSpec((1, STATE_VREG_ROWS, LANES), lambda i: (i, 0, 0))
    bshape = jax.ShapeDtypeStruct((n_chunk, STATE_VREG_ROWS, LANES), F32)
    tm_scratch = pltpu.VMEM((tc * STATE_VREG_ROWS, LANES), F32)
    return pl.pallas_call(
        body, name=name, grid=(n_chunk,),
        in_specs=[rows, in_tile, in_tile, out_tile, out_tile, coef, coef, pl.BlockSpec((1, SSM_WIDTH), lambda i: (0, 0))],
        out_specs=[rows, rows, bound, bound],
        out_shape=[jax.ShapeDtypeStruct((n_rows, SSM_WIDTH), F32), jax.ShapeDtypeStruct((n_rows, SSM_WIDTH), MXU_DTYPE),
                   bshape, bshape],
        scratch_shapes=[tm_scratch] * 4 + [pltpu.VMEM((2, STATE_VREG_ROWS, LANES), F32)],
        compiler_params=_params("arbitrary"),
    )(proj, tb_re, tb_im, tc_re, tc_im, a_re, a_im, gain)


def _ssm_bwd(proj, dy, sb_re, sb_im, tb_re, tb_im, td_re, td_im, tu_re, tu_im, a_re, a_im, gain, *, name, tc=SSM_CHUNK):
    n_rows = proj.shape[0]
    n_chunk = n_rows // tc

    def body(u_ref, dy_ref, sbr_ref, sbi_ref, tbr_ref, tbi_ref, tdr_ref, tdi_ref, tur_ref, tui_ref, ar_ref, ai_ref, g_ref,
             du_ref, su_ref, dcr_ref, dci_ref, dbr_ref, dbi_ref, dar_ref, dai_ref, wr, wi, hr, hi, lr, li, carry):
        @pl.when(pl.program_id(0) == 0)
        def _():
            carry[...] = jnp.zeros_like(carry)
            for acc_ref in (su_ref, dcr_ref, dci_ref, dbr_ref, dbi_ref):
                acc_ref[...] = jnp.zeros_like(acc_ref)

        a_r, a_i = ar_ref[...], ai_ref[...]
        u, dyv = u_ref[...], dy_ref[...]
        _to_time_major(u, tbr_ref, tbi_ref, wr, wi, tc)
        _scan_chunk(wr, wi, hr, hi, a_r, a_i, (sbr_ref[0], sbi_ref[0]), tc)
        _to_time_major(dyv, tdr_ref, tdi_ref, wr, wi, tc)

        def step(kk, c):
            lam_r, lam_i, dar, dai = c
            rows = _scan_rows(tc - 1 - kk)
            h_r, h_i = hr[rows, :], hi[rows, :]
            dar = dar + lam_r * h_r + lam_i * h_i
            dai = dai + lam_i * h_r - lam_r * h_i
            new_r = wr[rows, :] + a_r * lam_r + a_i * lam_i
            new_i = wi[rows, :] + a_r * lam_i - a_i * lam_r
            lr[rows, :] = new_r
            li[rows, :] = new_i
            return new_r, new_i, dar, dai

        carry[0], carry[1], carry[2], carry[3] = lax.fori_loop(0, tc, step, (carry[0], carry[1], carry[2], carry[3]),
                                                              unroll=8)
        dar_ref[...] = carry[2]
        dai_ref[...] = carry[3]
        for j in range(SSM_TILES):
            cj = _chan(j)
            lam_r, lam_i = _from_time_major(lr, j, tc), _from_time_major(li, j, tc)
            dcr_ref[j] += _dot(dyv[:, cj], _from_time_major(hr, j, tc), 0, 0)
            dci_ref[j] += _dot(dyv[:, cj], _from_time_major(hi, j, tc), 0, 0)
            dbr_ref[j] += _dot(u[:, cj], lam_r, 0, 0)
            dbi_ref[j] += _dot(u[:, cj], lam_i, 0, 0)
            duj = _dot(lam_r, tur_ref[j], 1, 0) + _dot(lam_i, tui_ref[j], 1, 0) + g_ref[:, cj] * dyv[:, cj]
            du_ref[:, cj] = duj.astype(du_ref.dtype)
            su_ref[:, cj] += _colsum(duj)

    back = lambda i: (n_chunk - 1 - i, 0)
    rows = pl.BlockSpec((tc, SSM_WIDTH), back)
    in_tile = pl.BlockSpec((SSM_TILES, LANES, SSM_WIDE), lambda i: (0, 0, 0))
    out_tile = pl.BlockSpec((SSM_TILES, SSM_WIDE, LANES), lambda i: (0, 0, 0))
    coef = pl.BlockSpec((STATE_VREG_ROWS, LANES), lambda i: (0, 0))
    bound = pl.BlockSpec((1, STATE_VREG_ROWS, LANES), lambda i: (n_chunk - 1 - i, 0, 0))
    vec = pl.BlockSpec((1, SSM_WIDTH), lambda i: (0, 0))
    tshape = jax.ShapeDtypeStruct((SSM_TILES, LANES, SSM_WIDE), F32)
    cshape = jax.ShapeDtypeStruct((STATE_VREG_ROWS, LANES), F32)
    tm_scratch = pltpu.VMEM((tc * STATE_VREG_ROWS, LANES), F32)
    return pl.pallas_call(
        body, name=name, grid=(n_chunk,),
        in_specs=[rows, rows, bound, bound, in_tile, in_tile, in_tile, in_tile, out_tile, out_tile, coef, coef, vec],
        out_specs=[rows, vec, in_tile, in_tile, in_tile, in_tile, coef, coef],
        out_shape=[jax.ShapeDtypeStruct((n_rows, SSM_WIDTH), MXU_DTYPE), jax.ShapeDtypeStruct((1, SSM_WIDTH), F32),
                   tshape, tshape, tshape, tshape, cshape, cshape],
        scratch_shapes=[tm_scratch] * 6 + [pltpu.VMEM((4, STATE_VREG_ROWS, LANES), F32)],
        compiler_params=_params("arbitrary"),
    )(proj, dy, sb_re, sb_im, tb_re, tb_im, td_re, td_im, tu_re, tu_im, a_re, a_im, gain)


def _scan_rows(t):
    return pl.ds(pl.multiple_of(t * STATE_VREG_ROWS, 8), STATE_VREG_ROWS)


GATHER_GROUPS = (("w_in",), ("w_glu", "w_att_up", "w_mix_out"), ("w_xq", "w_xkv", "w_xo", "w_ff1", "w_ff2"))
SCATTER_GROUPS = (("w_ff2", "w_ff1"), ("w_xo", "w_xq", "w_xkv", "w_mix_out"), ("w_att_up", "w_glu", "w_in"))


def _local_grads(x, mem, pos_col, target, sm, fetch, send, start_token):
    b_re_t = sm["ssm_b_re"].transpose(2, 0, 1)
    b_im_t = sm["ssm_b_im"].transpose(2, 0, 1)
    logdt = sm["ssm_log_dt"].reshape(SSM_GROUPS, 1)
    c_re, c_im = sm["ssm_c_re"], sm["ssm_c_im"]
    grp = (SSM_GROUPS, SSM_STATE)
    chn = (SSM_GROUP, SSM_GROUPS, SSM_STATE)

    wts = {}
    cos_t, sin_t = _rope_tables(pos_col, after=start_token, name="rope_tables")
    h0, xh0, rs0 = _ln_fwd(x, None, sm["ln_in_g"], sm["ln_in_b"], alpha=1.0, name="ln_in_fwd")
    disc_in = (logdt, sm["ssm_a_re"], sm["ssm_a_im"], b_re_t, b_im_t)
    ab_re, ab_im, bb_re_t, bb_im_t = _whole(_disc, disc_in, [grp, grp, chn, chn], name="ssm_disc")
    a_re_rows, a_im_rows = ab_re.reshape(STATE_VREG_ROWS, LANES), ab_im.reshape(STATE_VREG_ROWS, LANES)
    wts.update(fetch(0, h0))
    proj = _mm(h0, wts["w_in"], bias=sm["b_in"], b_shards=True, name="in_proj")

    mxu = lambda t: t.astype(MXU_DTYPE)
    tb_re, tb_im = mxu(_tiles_cn(bb_re_t.transpose(1, 0, 2))), mxu(_tiles_cn(bb_im_t.transpose(1, 0, 2)))
    y, gy, sb_re, sb_im = _ssm_fwd(proj, tb_re, tb_im, mxu(_tiles_nc(c_re.transpose(0, 2, 1))),
                                   mxu(_tiles_nc(-c_im.transpose(0, 2, 1))), a_re_rows, a_im_rows, sm["ssm_d"],
                                   name="ssm_fwd")

    q, k, v = _qkv_split(proj, cos_t, sin_t, name="qkv_split")
    outs, lses = [], []
    for g, dil in enumerate(DILATIONS):
        o_g, l_g = _dil_fwd(q[g], k[g], v[g], dil, name=f"dil_att_fwd_{dil}")
        outs.append(o_g)
        lses.append(l_g)
    att, lse = _att_merge(outs, lses, name="att_merge")
    wts.update(fetch(1, att))
    z = _mm(gy, wts["w_glu"], bias=sm["b_glu"], b_shards=True, name="glu_proj")
    b_att = _mm(att, wts["w_att_up"], b_shards=True, name="att_up")

    mixed = _mix_fwd(proj, z, b_att, name="gate_mix")
    mix_out = _mm(mixed, wts["w_mix_out"], bias=sm["b_mix_out"], name="mix_out")
    h1, xh1, rs1 = _ln_fwd(h0, mix_out, sm["ln1_g"], sm["ln1_b"], alpha=DEEPNORM_ALPHA, name="ln1_fwd")

    wts.update(fetch(2, h1))
    xq = _mm(h1, wts["w_xq"], out_dtype=MXU_DTYPE, name="xatt_q")
    kv = _mm(mem, wts["w_xkv"], out_dtype=MXU_DTYPE, b_shards=True, name="xatt_kv")
    xo_in = _xatt_fwd(xq, kv, name="xatt_fwd")
    xo = _mm(xo_in, wts["w_xo"], name="xatt_o")
    h2, xh2, rs2 = _ln_fwd(h1, xo, sm["ln2_g"], sm["ln2_b"], alpha=DEEPNORM_ALPHA, name="ln2_fwd")

    pre, act = _mm(h2, wts["w_ff1"], bias=sm["b_ff1"], b_shards=True, name="ff1",
                   also=(lambda r: jnp.square(jnp.maximum(r, 0.0)), MXU_DTYPE))
    ff = _mm(act, wts["w_ff2"], bias=sm["b_ff2"], name="ff2")
    h3, xh3, rs3 = _ln_fwd(h2, ff, sm["ln3_g"], sm["ln3_b"], alpha=DEEPNORM_ALPHA, name="ln3_fwd")
    dh3, loss_row = _loss_head(h3, target, name="loss_head")

    gw, gs = {}, {}
    dr3, gs["ln3_g"], gs["ln3_b"], gs["b_ff2"] = _ln_bwd(None, dh3, xh3, rs3, sm["ln3_g"], alpha=1.0, name="ln3_bwd")
    wgrad = functools.partial(_mm, ta=True, out_dtype=WIRE_DTYPE)
    gw["w_ff2"] = wgrad(act, dr3, name="ff2_dw")
    dpre, gs["b_ff1"] = _mm(dr3, wts["w_ff2"], tb=True, out_dtype=MXU_DTYPE, colsum=True, name="ff2_dx",
                            gate=(pre, lambda p: 2.0 * jnp.maximum(p, 0.0)))
    gw["w_ff1"] = wgrad(h2, dpre, out_shards=True, name="ff1_dw")
    sent = send(0, gw)
    dh2 = _mm(dpre, wts["w_ff1"], tb=True, b_shards=True, after=sent, name="ff1_dx")

    dr2, gs["ln2_g"], gs["ln2_b"], _ = _ln_bwd(dr3, dh2, xh2, rs2, sm["ln2_g"], alpha=DEEPNORM_ALPHA, name="ln2_bwd")
    gw["w_xo"] = wgrad(xo_in, dr2, name="xatt_o_dw")
    dxo_in = _mm(dr2, wts["w_xo"], tb=True, out_dtype=MXU_DTYPE, name="xatt_o_dx")
    dxq, dkv = _xatt_bwd(xq, kv, dxo_in, name="xatt_bwd")
    gw["w_xq"] = wgrad(h1, dxq, name="xatt_q_dw")
    gw["w_xkv"] = wgrad(mem, dkv, out_shards=True, name="xatt_kv_dw")
    dh1 = _mm(dxq, wts["w_xq"], tb=True, name="xatt_q_dx")

    dr1, gs["ln1_g"], gs["ln1_b"], gs["b_mix_out"] = _ln_bwd(dr2, dh1, xh1, rs1, sm["ln1_g"], alpha=DEEPNORM_ALPHA,
                                                             name="ln1_bwd")
    gw["w_mix_out"] = wgrad(mixed, dr1, name="mix_out_dw")
    sent = send(1, gw)
    dmixed = _mm(dr1, wts["w_mix_out"], tb=True, after=sent, name="mix_out_dx")
    dgs, dga, dz, db_att, s_gs, s_ga, gs["b_glu"] = _mix_bwd(dmixed, proj, z, b_att, name="gate_mix_bwd")

    gw["w_att_up"] = wgrad(att, db_att, out_shards=True, name="att_up_dw")
    datt = _mm(db_att, wts["w_att_up"], tb=True, b_shards=True, name="att_up_dx")
    dqkv, sqkv = [], []
    for g, dil in enumerate(DILATIONS):
        d_g, s_g = _dil_bwd(q[g], k[g], v[g], datt, att, lse, cos_t, sin_t, dil, name=f"dil_att_bwd_{dil}")
        dqkv.append(d_g)
        sqkv.append(s_g)

    gw["w_glu"] = wgrad(gy, dz, out_shards=True, name="glu_dw")
    dgy = _mm(dz, wts["w_glu"], tb=True, b_shards=True, name="glu_dx")
    dy, gs["ssm_d"] = _gelu_bwd(dgy, y, proj, name="gelu_bwd")
    du, s_u, dc_re_t, dc_im_t, dbb_re_t, dbb_im_t, da_re, da_im = _ssm_bwd(
        proj, dy, sb_re, sb_im, tb_re, tb_im, mxu(_tiles_cn(c_re)), mxu(_tiles_cn(-c_im)),
        mxu(_tiles_nc(bb_re_t.transpose(1, 2, 0))), mxu(_tiles_nc(bb_im_t.transpose(1, 2, 0))),
        a_re_rows, a_im_rows, sm["ssm_d"], name="ssm_bwd")
    gs["ssm_c_re"], gs["ssm_c_im"] = _untile_cn(dc_re_t), -_untile_cn(dc_im_t)
    disc_ct = (da_re.reshape(grp), da_im.reshape(grp), _untile_cn(dbb_re_t).transpose(1, 0, 2),
               _untile_cn(dbb_im_t).transpose(1, 0, 2))
    d_logdt, gs["ssm_a_re"], gs["ssm_a_im"], d_b_re_t, d_b_im_t = _whole(
        _disc_transpose, disc_in + disc_ct, [(SSM_GROUPS, 1), grp, grp, chn, chn], name="ssm_disc_bwd")
    gs["ssm_log_dt"] = d_logdt
    gs["ssm_b_re"], gs["ssm_b_im"] = d_b_re_t.transpose(1, 2, 0), d_b_im_t.transpose(1, 2, 0)

    dproj = jnp.concatenate([du] + [dqkv[g][i] for i in range(3) for g in range(len(DILATIONS))] + [dgs, dga], axis=1)
    gs["b_in"] = jnp.concatenate([s_u] + [sqkv[g][i] for i in range(3) for g in range(len(DILATIONS))] + [s_gs, s_ga],
                                 axis=1)
    gw["w_in"] = wgrad(h0, dproj, out_shards=True, name="in_proj_dw")
    sent = send(2, gw)
    dh0 = _mm(dproj, wts["w_in"], tb=True, b_shards=True, after=sent, name="in_proj_dx")
    grad_x, gs["ln_in_g"], gs["ln_in_b"], _ = _ln_bwd(dr1, dh0, xh0, rs0, sm["ln_in_g"], alpha=DEEPNORM_ALPHA,
                                                      name="ln_in_bwd")
    return loss_row, grad_x, gs


N_PEER = N_DEV - 1
_IN_HBM = pl.BlockSpec(memory_space=pltpu.HBM)
_IN_SEMAPHORE = pl.BlockSpec(memory_space=pltpu.SEMAPHORE)


def _device_index():
    return 4 * lax.axis_index("x") + 2 * lax.axis_index("y") + lax.axis_index("c")


def _exchange_copies(src_refs, land_refs, send_sems, recv_sems, scatter):
    x, y, c = lax.axis_index("x"), lax.axis_index("y"), lax.axis_index("c")
    me = 4 * x + 2 * y + c
    pairs = []
    for a, (src_ref, land_ref) in enumerate(zip(src_refs, land_refs)):
        for kk in range(1, N_DEV):
            px = (x + (kk >> 2)) % 2
            py = (y + ((kk >> 1) & 1)) % 2
            pc = (c + (kk & 1)) % 2
            peer = 4 * px + 2 * py + pc
            sem = a * N_PEER + kk - 1
            src = src_ref.at[peer] if scatter else src_ref

            def copy(dst, src=src, sem=sem, px=px, py=py, pc=pc):
                return pltpu.make_async_remote_copy(
                    src_ref=src, dst_ref=dst, send_sem=send_sems.at[sem], recv_sem=recv_sems.at[sem],
                    device_id=(px, py, pc), device_id_type=pl.DeviceIdType.MESH)

            pairs.append((copy(land_ref.at[me]), copy(land_ref.at[peer])))
    return pairs


def _exchange_start(srcs, *, scatter, name, after=None):
    n_arr = len(srcs)
    lands = [lax.empty((N_DEV,) + tuple(s.shape[1:] if scatter else s.shape), s.dtype) for s in srcs]
    n_in = 2 * n_arr + (after is not None)

    def body(*refs):
        send_sems, recv_sems = refs[n_in], refs[n_in + 1]
        for sent, _ in _exchange_copies(refs[:n_arr], refs[n_arr:2 * n_arr], send_sems, recv_sems, scatter):
            sent.start()
        refs[-1][...] = jnp.zeros_like(refs[-1])

    through = [pltpu.HBM(t.shape, t.dtype) for t in (*srcs, *lands)]
    res = pl.pallas_call(
        body, name=name,
        out_shape=(pltpu.SemaphoreType.DMA((n_arr * N_PEER,)), pltpu.SemaphoreType.DMA((n_arr * N_PEER,)), *through,
                   jax.ShapeDtypeStruct((8, LANES), F32)),
        in_specs=[_IN_HBM] * (2 * n_arr) + [pl.BlockSpec(memory_space=pl.ANY)] * (after is not None),
        out_specs=(_IN_SEMAPHORE, _IN_SEMAPHORE, *[_IN_HBM] * (2 * n_arr), pl.BlockSpec(memory_space=pltpu.VMEM)),
        input_output_aliases={i: 2 + i for i in range(2 * n_arr)},
        compiler_params=pltpu.CompilerParams(has_side_effects=pltpu.SideEffectType.DATAFLOW_SIDE_EFFECTING),
    )(*[pltpu.with_memory_space_constraint(t, pltpu.HBM) for t in (*srcs, *lands)],
      *([after] if after is not None else []))
    return (res[0], res[1], res[2:2 + n_arr], res[2 + n_arr:2 + 2 * n_arr], scatter), res[-1]


def _exchange_wait(handle, *, after, name):
    send_sems, recv_sems, srcs, lands, scatter = handle
    n_arr = len(srcs)

    def body(*refs):
        for sent, received in _exchange_copies(refs[:n_arr], refs[n_arr:2 * n_arr], refs[2 * n_arr], refs[2 * n_arr + 1],
                                               scatter):
            sent.wait_send()
            received.wait_recv()

    res = pl.pallas_call(
        body, name=name, out_shape=tuple(pltpu.HBM(t.shape, t.dtype) for t in (*srcs, *lands)),
        in_specs=[_IN_HBM] * (2 * n_arr) + [_IN_SEMAPHORE, _IN_SEMAPHORE, pl.BlockSpec(memory_space=pl.ANY)],
        out_specs=tuple([_IN_HBM] * (2 * n_arr)), input_output_aliases={i: i for i in range(2 * n_arr)},
        compiler_params=pltpu.CompilerParams(has_side_effects=pltpu.SideEffectType.DATAFLOW_SIDE_EFFECTING),
    )(*srcs, *lands, send_sems, recv_sems, after)
    return res[n_arr:]


def _with_own_slot(land, own):
    return lax.dynamic_update_slice_in_dim(land, own[None], _device_index(), axis=0)


def _reduce_adamw(gstack, w, m, v, *, name, tr=128):
    n_rows, cols = w.shape
    tr = min(tr, n_rows)
    assert n_rows % tr == 0, (name, n_rows, tr)

    def body(g_ref, w_ref, m_ref, v_ref, go_ref, d_ref, mo_ref, vo_ref):
        g = g_ref[0].astype(F32)
        for dev in range(1, N_DEV):
            g = g + g_ref[dev].astype(F32)
        m_new = ADAM_B1 * m_ref[...] + (1.0 - ADAM_B1) * g
        v_new = ADAM_B2 * v_ref[...] + (1.0 - ADAM_B2) * jnp.square(g)
        m_hat = m_new / (1.0 - ADAM_B1 ** ADAM_STEP)
        v_hat = v_new / (1.0 - ADAM_B2 ** ADAM_STEP)
        go_ref[...] = g
        d_ref[...] = -ADAM_LR * (m_hat / (jnp.sqrt(v_hat) + ADAM_EPS) + ADAM_WD * w_ref[...])
        mo_ref[...] = m_new
        vo_ref[...] = v_new

    flat = pl.BlockSpec((tr, cols), lambda i: (i, 0))
    shape = jax.ShapeDtypeStruct((n_rows, cols), F32)
    return pl.pallas_call(
        body, name=name, grid=(n_rows // tr,),
        in_specs=[pl.BlockSpec((N_DEV, tr, cols), lambda i: (0, i, 0)), flat, flat, flat],
        out_specs=[flat] * 4, out_shape=[shape] * 4, compiler_params=_params("parallel"),
    )(gstack, w, m, v)


def _pack(parts, dtype):
    flat = jnp.concatenate([p.reshape(-1).astype(dtype) for p in parts])
    unit = PACK_COLS * PACK_ROW_ALIGN
    total = -(-flat.shape[0] // unit) * unit
    return jnp.pad(flat, (0, total - flat.shape[0])).reshape(-1, PACK_COLS)


def _unpack(packed, shapes):
    flat = packed.reshape(-1)
    out, off = [], 0
    for s in shapes:
        size = int(np.prod(s))
        out.append(flat[off:off + size].reshape(s))
        off += size
    return out


def kernel(x, mem, positions, ln_in_g, ln_in_b, w_in, b_in, ssm_log_dt, ssm_a_re, ssm_a_im, ssm_b_re, ssm_b_im, ssm_c_re, ssm_c_im, ssm_d, w_glu, b_glu, w_att_up, w_mix_out, b_mix_out, ln1_g, ln1_b, w_xq, w_xkv, w_xo, ln2_g, ln2_b, w_ff1, b_ff1, w_ff2, b_ff2, ln3_g, ln3_b, loss_target, m_ln_in_g, m_ln_in_b, m_w_in, m_b_in, m_ssm_log_dt, m_ssm_a_re, m_ssm_a_im, m_ssm_b_re, m_ssm_b_im, m_ssm_c_re, m_ssm_c_im, m_ssm_d, m_w_glu, m_b_glu, m_w_att_up, m_w_mix_out, m_b_mix_out, m_ln1_g, m_ln1_b, m_w_xq, m_w_xkv, m_w_xo, m_ln2_g, m_ln2_b, m_w_ff1, m_b_ff1, m_w_ff2, m_b_ff2, m_ln3_g, m_ln3_b, v_ln_in_g, v_ln_in_b, v_w_in, v_b_in, v_ssm_log_dt, v_ssm_a_re, v_ssm_a_im, v_ssm_b_re, v_ssm_b_im, v_ssm_c_re, v_ssm_c_im, v_ssm_d, v_w_glu, v_b_glu, v_w_att_up, v_w_mix_out, v_b_mix_out, v_ln1_g, v_ln1_b, v_w_xq, v_w_xkv, v_w_xo, v_ln2_g, v_ln2_b, v_w_ff1, v_b_ff1, v_w_ff2, v_b_ff2, v_ln3_g, v_ln3_b):
    given = dict(locals())
    w_arg = {n: given[n] for n in WEIGHTS}
    m_arg = {n: given["m_" + n] for n in WEIGHTS}
    v_arg = {n: given["v_" + n] for n in WEIGHTS}

    shards = {n: w_arg[n][0].astype(MXU_DTYPE) for n in BIG}
    gathers, token = [], None
    for i, names in enumerate(GATHER_GROUPS):
        handle, token = _exchange_start([shards[n] for n in names], scatter=False, after=token, name=f"gather_start_{i}")
        gathers.append(handle)

    def fetch(i, after):
        lands = _exchange_wait(gathers[i], after=after, name=f"gather_wait_{i}")
        full = {n: _with_own_slot(land, shards[n]) for n, land in zip(GATHER_GROUPS[i], lands)}
        return {n: t if n in BIG_COL_SHARDED else t.reshape(-1, t.shape[-1]) for n, t in full.items()}

    scatters = {}

    def send(i, gw):
        slots = [gw[n] if n in BIG_COL_SHARDED else gw[n].reshape(N_DEV, -1, gw[n].shape[-1]) for n in SCATTER_GROUPS[i]]
        handle, sent = _exchange_start(slots, scatter=True, name=f"scatter_start_{i}")
        scatters[i] = (handle, slots)
        return sent

    sm = {}
    for n in SMALL:
        t = w_arg[n]
        if n.startswith("ssm_") and n not in ("ssm_d", "ssm_log_dt"):
            sm[n] = t[0]
        else:
            sm[n] = t.reshape(1, -1)

    loss_row, grad_x, gs = _local_grads(x[0], mem[0], positions.reshape(-1, 1), loss_target[0], sm, fetch, send, token)
    loss = lax.psum(loss_row[0, 0], ("x", "y", "c"))
    small = _pack([gs[n] for n in SMALL], WIRE_DTYPE)
    small_handle, _ = _exchange_start([small], scatter=False, name="small_start")

    results = [{}, {}, {}, {}]
    done = grad_x
    for i, names in enumerate(SCATTER_GROUPS):
        handle, slots = scatters[i]
        lands = _exchange_wait(handle, after=done, name=f"scatter_wait_{i}")
        for n, land, slot in zip(names, lands, slots):
            own = lax.dynamic_index_in_dim(slot, _device_index(), axis=0, keepdims=False)
            res = _reduce_adamw(_with_own_slot(land, own), w_arg[n][0], m_arg[n][0], v_arg[n][0], name="adamw_" + n)
            done = res[0]
            for d, r in zip(results, res):
                d[n] = r[None]
    small_stack = _with_own_slot(_exchange_wait(small_handle, after=done, name="small_wait")[0], small)
    small_shapes = [w_arg[n].shape for n in SMALL]
    res = _reduce_adamw(small_stack, *[_pack([d[n] for n in SMALL], F32) for d in (w_arg, m_arg, v_arg)],
                        name="adamw_small")
    for d, r in zip(results, res):
        d.update(zip(SMALL, _unpack(r, small_shapes)))
    out = [loss, grad_x[None]]
    for d in results:
        out += [d[n] for n in WEIGHTS]
    return tuple(out)
```

```python
import functools

import numpy as np
import jax
import jax.numpy as jnp
from jax import lax
from jax.experimental import pallas as pl
from jax.experimental.pallas import tpu as pltpu

F32 = jnp.float32
MXU_DTYPE = jnp.bfloat16
WIRE_DTYPE = jnp.bfloat16
VMEM_LIMIT_BYTES = 48 * 1024 * 1024
LANES = 128

N_DEV = 8
D_MODEL = 1024
SSM_GROUP = 16
SSM_WIDTH = 768
SSM_GROUPS = SSM_WIDTH // SSM_GROUP
SSM_STATE = 64
SSM_CH = SSM_GROUPS * SSM_STATE
SSM_TILES = SSM_WIDTH // LANES
GROUPS_PER_TILE = LANES // SSM_GROUP
STATE_VREG_ROWS = SSM_CH // LANES
ATT_HEAD_DIM = 64
ATT_HEADS_PER_GROUP = 4
ATT_MERGED = ATT_HEADS_PER_GROUP * ATT_HEAD_DIM
DILATIONS = (1, 4, 16)
ATT_BLK = 128
ATT_SCALE = ATT_HEAD_DIM ** -0.5
ROT_DIM = ATT_HEAD_DIM // 4
ROPE_THETA = 500000.0
XATT_HEADS = 4
XATT_HEAD_DIM = D_MODEL // XATT_HEADS
XATT_SCALE = XATT_HEAD_DIM ** -0.5
DEEPNORM_ALPHA = 2.0 ** 0.25
LN_EPS = 1e-5
NEG_INF = -1e30
OFF_Q_BLK, OFF_K_BLK, OFF_V_BLK = 3, 6, 9
OFF_GS_BLK, OFF_GA_BLK = 3, 4

ADAM_LR = 0.001
ADAM_B1 = 0.9
ADAM_B2 = 0.999
ADAM_EPS = 1e-08
ADAM_WD = 0.01
ADAM_STEP = 10

BIG = ("w_in", "w_glu", "w_att_up", "w_mix_out", "w_xq", "w_xkv", "w_xo", "w_ff1", "w_ff2")
BIG_COL_SHARDED = ("w_in", "w_glu", "w_att_up", "w_xkv", "w_ff1")
WEIGHTS = ("ln_in_g", "ln_in_b", "w_in", "b_in", "ssm_log_dt", "ssm_a_re", "ssm_a_im", "ssm_b_re", "ssm_b_im",
           "ssm_c_re", "ssm_c_im", "ssm_d", "w_glu", "b_glu", "w_att_up", "w_mix_out", "b_mix_out", "ln1_g", "ln1_b",
           "w_xq", "w_xkv", "w_xo", "ln2_g", "ln2_b", "w_ff1", "b_ff1", "w_ff2", "b_ff2", "ln3_g", "ln3_b")
SMALL = tuple(n for n in WEIGHTS if n not in BIG)
PACK_COLS = 1024
PACK_ROW_ALIGN = 256


def _params(*sem):
    return pltpu.CompilerParams(dimension_semantics=sem, vmem_limit_bytes=VMEM_LIMIT_BYTES)


def _dot(a, b, ca, cb):
    return lax.dot_general(a.astype(MXU_DTYPE), b.astype(MXU_DTYPE), (((ca,), (cb,)), ((), ())),
                           preferred_element_type=F32)


def _fit(dim, pref):
    if dim <= pref:
        return dim
    best = max(t for t in range(LANES, pref + 1, LANES) if dim % t == 0)
    return best


def _mm(a, b, *, name, ta=False, tb=False, bias=None, out_dtype=F32, b_shards=False, out_shards=False, after=None,
        also=None, gate=None, colsum=False, tm=1024, tn=1024, tk=1024):
    m, k = (a.shape[1], a.shape[0]) if ta else a.shape
    order = (lambda f: (lambda j, i, kk: f(i, j, kk))) if colsum else (lambda f: f)
    spec = lambda shape, f: pl.BlockSpec(shape, order(f))
    if b_shards:
        n_sh, rows, n_loc = b.shape
        if tb:
            n, tn, tk = rows, _fit(rows, tn), n_loc
            assert k == n_sh * n_loc, (name, k, b.shape)
            b_spec = spec((1, tn, tk), lambda i, j, kk: (kk, j, 0))
        else:
            n, tn, tk = n_sh * n_loc, n_loc, _fit(k, tk)
            b_spec = spec((1, tk, tn), lambda i, j, kk: (j, kk, 0))
    else:
        n = b.shape[0] if tb else b.shape[1]
        tn = n // N_DEV if out_shards else _fit(n, tn)
        tk = _fit(k, tk)
        b_spec = spec((tn, tk), lambda i, j, kk: (j, kk)) if tb else spec((tk, tn), lambda i, j, kk: (kk, j))
    tm = _fit(m, tm)
    nk = k // tk
    a_spec = spec((tk, tm), lambda i, j, kk: (kk, i)) if ta else spec((tm, tk), lambda i, j, kk: (i, kk))
    tile = spec((tm, tn), lambda i, j, kk: (i, j))
    in_specs, args = [a_spec, b_spec], [a, b]
    if bias is not None:
        in_specs.append(spec((1, tn), lambda i, j, kk: (0, j)))
        args.append(bias)
    if gate is not None:
        in_specs.append(tile)
        args.append(gate[0])
    if after is not None:
        in_specs.append(pl.BlockSpec(memory_space=pl.ANY))
        args.append(after)
    n_in = len(args)
    if out_shards:
        assert n == N_DEV * tn, (name, n, tn)
        out_specs = [spec((1, tm, tn), lambda i, j, kk: (j, i, 0))]
        out_shape = [jax.ShapeDtypeStruct((N_DEV, m, tn), out_dtype)]
    else:
        out_specs = [tile]
        out_shape = [jax.ShapeDtypeStruct((m, n), out_dtype)]
    if also is not None:
        out_specs.append(tile)
        out_shape.append(jax.ShapeDtypeStruct((m, n), also[1]))
    if colsum:
        out_specs.append(spec((1, tn), lambda i, j, kk: (0, j)))
        out_shape.append(jax.ShapeDtypeStruct((1, n), F32))

    def body(*refs):
        a_ref, b_ref = refs[0], refs[1]
        o_ref = refs[n_in]

        def product():
            return _dot(a_ref[...], b_ref[0] if b_shards else b_ref[...], 0 if ta else 1, 1 if tb else 0)

        def finish(r):
            if bias is not None:
                r = r + refs[2][...]
            if gate is not None:
                r = r * gate[1](refs[2 + (bias is not None)][...])
            if out_shards:
                o_ref[0] = r.astype(o_ref.dtype)
            else:
                o_ref[...] = r.astype(o_ref.dtype)
            if also is not None:
                refs[n_in + 1][...] = also[0](r).astype(also[1])
            if colsum:
                s_ref = refs[n_in + 1 + (also is not None)]

                @pl.when(pl.program_id(1) == 0)
                def _():
                    s_ref[...] = jnp.zeros_like(s_ref)

                s_ref[...] += _colsum(r)

        if nk == 1:
            finish(product())
            return
        acc_ref = refs[-1]
        kk = pl.program_id(2)

        @pl.when(kk == 0)
        def _():
            acc_ref[...] = jnp.zeros_like(acc_ref)

        acc_ref[...] += product()

        @pl.when(kk == nk - 1)
        def _():
            finish(acc_ref[...])

    grid = (n // tn, m // tm, nk) if colsum else (m // tm, n // tn, nk)
    res = pl.pallas_call(
        body, name=name, grid=grid, in_specs=in_specs, out_specs=out_specs, out_shape=out_shape,
        scratch_shapes=[pltpu.VMEM((tm, tn), F32)] if nk > 1 else [],
        compiler_params=_params("parallel", "arbitrary" if colsum else "parallel", "arbitrary"),
    )(*args)
    return res[0] if len(res) == 1 else res


def _rowcall(fn, rows, fulls, row_outs, acc_outs=(), *, n_rows, tm, name, after=None):
    n_r, n_f, n_o, n_a = len(rows), len(fulls), len(row_outs), len(acc_outs)
    n_in = n_r + n_f + (after is not None)
    assert n_rows % tm == 0, (name, n_rows, tm)

    def body(*refs):
        res = fn(*[r[...] for r in refs[:n_r + n_f]])
        res = tuple(res) if isinstance(res, (tuple, list)) else (res,)
        o_refs = refs[n_in:n_in + n_o]
        a_refs = refs[n_in + n_o:]
        for o_ref, val in zip(o_refs, res[:n_o]):
            o_ref[...] = val.astype(o_ref.dtype)
        if n_a:
            @pl.when(pl.program_id(0) == 0)
            def _():
                for a_ref in a_refs:
                    a_ref[...] = jnp.zeros_like(a_ref)

            for a_ref, val in zip(a_refs, res[n_o:]):
                a_ref[...] += val

    in_specs = [pl.BlockSpec((tm, w), functools.partial(lambda i, cb: (i, cb), cb=cb)) for _, w, cb in rows]
    in_specs += [pl.BlockSpec(f.shape, functools.partial(lambda i, nd: (0,) * nd, nd=f.ndim)) for f in fulls]
    in_specs += [pl.BlockSpec(memory_space=pl.ANY)] * (after is not None)
    out_specs = [pl.BlockSpec((tm, w), lambda i: (i, 0)) for w, _ in row_outs]
    out_specs += [pl.BlockSpec((1, w), lambda i: (0, 0)) for w in acc_outs]
    out_shape = [jax.ShapeDtypeStruct((n_rows, w), dt) for w, dt in row_outs]
    out_shape += [jax.ShapeDtypeStruct((1, w), F32) for w in acc_outs]
    return pl.pallas_call(
        body, name=name, grid=(n_rows // tm,), in_specs=in_specs, out_specs=out_specs, out_shape=out_shape,
        compiler_params=_params("arbitrary" if n_a else "parallel"),
    )(*[r[0] for r in rows], *fulls, *([after] if after is not None else []))


def _colsum(v):
    return jnp.sum(v, axis=0, keepdims=True)


def _ln_fwd(a, r, g, b, *, alpha, name):
    n_rows, d = a.shape

    def fn(*t):
        xin = t[0] if alpha == 1.0 else alpha * t[0]
        if r is not None:
            xin = xin + t[1]
        gv, bv = t[-2], t[-1]
        mu = jnp.mean(xin, axis=-1, keepdims=True)
        xc = xin - mu
        var = jnp.mean(xc * xc, axis=-1, keepdims=True)
        rstd = lax.rsqrt(var + LN_EPS)
        xh = xc * rstd
        return xh * gv + bv, xh, rstd

    rows = [(a, d, 0)] + ([(r, d, 0)] if r is not None else [])
    return _rowcall(fn, rows, [g, b], [(d, F32), (d, F32), (1, F32)], n_rows=n_rows, tm=256, name=name)


def _ln_bwd(dya, dyb, xh, rstd, g, *, alpha, name):
    n_rows, d = xh.shape

    def fn(*t):
        if dya is not None:
            dy = alpha * t[0] + t[1]
            xhv, rs, gv = t[2], t[3], t[4]
        else:
            dy, xhv, rs, gv = t[0], t[1], t[2], t[3]
        dyg = dy * gv
        m1 = jnp.mean(dyg, axis=-1, keepdims=True)
        m2 = jnp.mean(dyg * xhv, axis=-1, keepdims=True)
        dx = rs * (dyg - m1 - xhv * m2)
        return dx, _colsum(dy * xhv), _colsum(dy), _colsum(dx)

    rows = ([(dya, d, 0)] if dya is not None else []) + [(dyb, d, 0), (xh, d, 0), (rstd, 1, 0)]
    return _rowcall(fn, rows, [g], [(d, F32)], [d, d, d], n_rows=n_rows, tm=256, name=name)


def _loss_head(y, target, *, name):
    n_rows, d = y.shape

    def fn(yv, tv):
        diff = yv - tv
        part = jnp.sum(jnp.sum(diff * diff, axis=1, keepdims=True), axis=0, keepdims=True) * (0.5 / d)
        return diff * (1.0 / d), jnp.broadcast_to(part, (1, LANES))

    return _rowcall(fn, [(y, d, 0), (target, d, 0)], [], [(d, F32)], [LANES], n_rows=n_rows, tm=256, name=name)


def _rope_lane_constants():
    lane = np.arange(ATT_MERGED)
    in_head = lane % ATT_HEAD_DIM
    sign = np.where(in_head < ROT_DIM // 2, -1.0, np.where(in_head < ROT_DIM, 1.0, 0.0)).astype(np.float32)
    inv_freq = ROPE_THETA ** (-jnp.arange(0, ROT_DIM, 2, dtype=F32) / ROT_DIM)
    return inv_freq[lane % (ROT_DIM // 2)].reshape(1, ATT_MERGED), jnp.asarray(sign).reshape(1, ATT_MERGED)


def _rope_tables(pos_col, *, name, after=None):
    inv_lane, sign = _rope_lane_constants()

    def fn(pos, inv, sg):
        ang = pos.astype(F32) * inv
        return jnp.where(sg != 0.0, jnp.cos(ang), 1.0), sg * jnp.sin(ang)

    return _rowcall(fn, [(pos_col, 1, 0)], [inv_lane, sign], [(ATT_MERGED, F32), (ATT_MERGED, F32)],
                    n_rows=pos_col.shape[0], tm=512, name=name, after=after)


def _rot_partner(t):
    lane = lax.broadcasted_iota(jnp.int32, t.shape, 1)
    width = t.shape[1]
    return jnp.where((lane & (ROT_DIM // 2)) == 0, pltpu.roll(t, width - ROT_DIM // 2, 1), pltpu.roll(t, ROT_DIM // 2, 1))


def _rope(t, cos_t, sin_t):
    return t * cos_t + _rot_partner(t) * sin_t


def _rope_transpose(dt, cos_t, sin_t):
    return dt * cos_t + _rot_partner(dt * sin_t)


def _qkv_split(proj, cos_t, sin_t, *, name):
    n_rows = proj.shape[0]
    n_g = len(DILATIONS)

    def fn(*t):
        c, s = t[3 * n_g], t[3 * n_g + 1]
        out = [_rope(t[g], c, s) for g in range(n_g)]
        out += [_rope(t[n_g + g], c, s) for g in range(n_g)]
        out += [t[2 * n_g + g] for g in range(n_g)]
        return out

    rows = [(proj, ATT_MERGED, off + g) for off in (OFF_Q_BLK, OFF_K_BLK, OFF_V_BLK) for g in range(n_g)]
    rows += [(cos_t, ATT_MERGED, 0), (sin_t, ATT_MERGED, 0)]
    outs = _rowcall(fn, rows, [], [(ATT_MERGED, MXU_DTYPE)] * (3 * n_g), n_rows=n_rows, tm=512, name=name)
    return outs[:n_g], outs[n_g:2 * n_g], outs[2 * n_g:]


def _mix(gs, ga, z1, z2, b_att):
    return jax.nn.sigmoid(gs) * (z1 * jax.nn.sigmoid(z2)) + jax.nn.sigmoid(ga) * b_att


def _mix_rows(proj, z, b_att):
    return [(proj, D_MODEL, OFF_GS_BLK), (proj, D_MODEL, OFF_GA_BLK), (z, D_MODEL, 0), (z, D_MODEL, 1), (b_att, D_MODEL, 0)]


def _mix_fwd(proj, z, b_att, *, name):
    return _rowcall(_mix, _mix_rows(proj, z, b_att), [], [(D_MODEL, MXU_DTYPE)],
                    n_rows=proj.shape[0], tm=256, name=name)[0]


def _mix_bwd(dmixed, proj, z, b_att, *, name):
    def fn(dm, gs, ga, z1, z2, ba):
        _, vjp = jax.vjp(_mix, gs, ga, z1, z2, ba)
        dgs, dga, dz1, dz2, dba = vjp(dm)
        dz = jnp.concatenate([dz1, dz2], axis=1)
        return dgs, dga, dz, dba, _colsum(dgs), _colsum(dga), _colsum(dz)

    rows = [(dmixed, D_MODEL, 0)] + _mix_rows(proj, z, b_att)
    widths = [D_MODEL, D_MODEL, 2 * D_MODEL, D_MODEL]
    return _rowcall(fn, rows, [], [(w, MXU_DTYPE) for w in widths], widths[:3], n_rows=proj.shape[0], tm=256, name=name)


def _gelu_bwd(dgy, y, proj, *, name):
    def fn(dg, yv, u):
        _, vjp = jax.vjp(jax.nn.gelu, yv)
        dy = vjp(dg)[0]
        return dy, _colsum(dy * u)

    return _rowcall(fn, [(dgy, SSM_WIDTH, 0), (y, SSM_WIDTH, 0), (proj, SSM_WIDTH, 0)], [], [(SSM_WIDTH, F32)],
                    [SSM_WIDTH], n_rows=y.shape[0], tm=512, name=name)


def _dilated_view(t, dil):
    return t.reshape(t.shape[0] // dil, dil * ATT_MERGED)


def _head(h):
    return slice(h * ATT_HEAD_DIM, (h + 1) * ATT_HEAD_DIM)


def _band_mask(first_key):
    qi = lax.broadcasted_iota(jnp.int32, (ATT_BLK, 2 * ATT_BLK), 0)
    ki = lax.broadcasted_iota(jnp.int32, (ATT_BLK, 2 * ATT_BLK), 1)
    steps = qi + ATT_BLK - ki
    return (steps >= 0) & (steps <= ATT_BLK) & (ki >= first_key)


def _dil_fwd(q, k, v, dil, *, name):
    n_rows = q.shape[0]
    n_blk = n_rows // dil // ATT_BLK
    cur = pl.BlockSpec((ATT_BLK, ATT_MERGED), lambda r, n: (n, r))
    prev = pl.BlockSpec((ATT_BLK, ATT_MERGED), lambda r, n: (jnp.maximum(n - 1, 0), r))

    def body(q_ref, kp_ref, kc_ref, vp_ref, vc_ref, o_ref, l_ref):
        valid = _band_mask(jnp.where(pl.program_id(1) > 0, 0, ATT_BLK))
        for h in range(ATT_HEADS_PER_GROUP):
            sl = _head(h)
            keys = jnp.concatenate([kp_ref[:, sl], kc_ref[:, sl]], axis=0)
            vals = jnp.concatenate([vp_ref[:, sl], vc_ref[:, sl]], axis=0)
            s = jnp.where(valid, _dot(q_ref[:, sl], keys, 1, 1) * ATT_SCALE, NEG_INF)
            m = jnp.max(s, axis=-1, keepdims=True)
            p = jnp.exp(s - m)
            den = jnp.sum(p, axis=-1, keepdims=True)
            o_ref[:, sl] = _dot(p, vals, 1, 0) / den
            l_ref[:, sl] = jnp.broadcast_to(m + jnp.log(den), (ATT_BLK, ATT_HEAD_DIM))

    shape = jax.ShapeDtypeStruct((n_rows // dil, dil * ATT_MERGED), F32)
    o, lse = pl.pallas_call(
        body, name=name, grid=(dil, n_blk), in_specs=[cur, prev, cur, prev, cur], out_specs=[cur, cur],
        out_shape=[shape, shape], compiler_params=_params("parallel", "parallel"),
    )(_dilated_view(q, dil), _dilated_view(k, dil), _dilated_view(k, dil), _dilated_view(v, dil), _dilated_view(v, dil))
    return o.reshape(n_rows, ATT_MERGED), lse.reshape(n_rows, ATT_MERGED)


def _att_merge(outs, lses, *, name):
    n_g = len(outs)

    def fn(*t):
        o, l = t[:n_g], t[n_g:]
        m = functools.reduce(jnp.maximum, l)
        e = [jnp.exp(li - m) for li in l]
        z = functools.reduce(jnp.add, e)
        att = functools.reduce(jnp.add, [(ei / z) * oi for ei, oi in zip(e, o)])
        return att, m + jnp.log(z)

    rows = [(t, ATT_MERGED, 0) for t in (*outs, *lses)]
    return _rowcall(fn, rows, [], [(ATT_MERGED, F32), (ATT_MERGED, F32)], n_rows=outs[0].shape[0], tm=512, name=name)


def _dil_bwd(q, k, v, datt, att, lse, cos_t, sin_t, dil, *, name):
    n_rows = q.shape[0]
    n_blk = n_rows // dil // ATT_BLK
    cur = pl.BlockSpec((ATT_BLK, ATT_MERGED), lambda r, n: (n, r))
    prev = pl.BlockSpec((ATT_BLK, ATT_MERGED), lambda r, n: (jnp.maximum(n - 1, 0), r))
    nxt = pl.BlockSpec((ATT_BLK, ATT_MERGED), lambda r, n: (jnp.minimum(n + 1, n_blk - 1), r))
    acc = pl.BlockSpec((1, ATT_MERGED), lambda r, n: (0, 0))

    def body(qc_ref, qn_ref, kp_ref, kc_ref, vp_ref, vc_ref, dc_ref, dn_ref, ac_ref, an_ref, lc_ref, ln_ref,
             cos_ref, sin_ref, dq_ref, dk_ref, dv_ref, sq_ref, sk_ref, sv_ref, dq_s, dk_s, dv_s):
        n = pl.program_id(1)

        @pl.when((pl.program_id(0) == 0) & (n == 0))
        def _():
            for s_ref in (sq_ref, sk_ref, sv_ref):
                s_ref[...] = jnp.zeros_like(s_ref)

        valid = _band_mask(jnp.where(n > 0, 0, ATT_BLK))
        qi = lax.broadcasted_iota(jnp.int32, (ATT_BLK, ATT_BLK), 0)
        ki = lax.broadcasted_iota(jnp.int32, (ATT_BLK, ATT_BLK), 1)
        valid_next = (ki - qi) >= jnp.where(n < n_blk - 1, 0, ATT_BLK)
        for h in range(ATT_HEADS_PER_GROUP):
            sl = _head(h)
            lane0 = slice(h * ATT_HEAD_DIM, h * ATT_HEAD_DIM + 1)
            qc, kc, vc = qc_ref[:, sl], kc_ref[:, sl], vc_ref[:, sl]
            keys = jnp.concatenate([kp_ref[:, sl], kc], axis=0)
            vals = jnp.concatenate([vp_ref[:, sl], vc], axis=0)
            dc = dc_ref[:, sl]
            delta = jnp.sum(dc * ac_ref[:, sl], axis=-1, keepdims=True)
            p = jnp.where(valid, jnp.exp(_dot(qc, keys, 1, 1) * ATT_SCALE - lc_ref[:, lane0]), 0.0)
            ds = p * (_dot(dc, vals, 1, 1) - delta) * ATT_SCALE
            dq_s[:, sl] = _dot(ds, keys, 1, 0)
            qn, dn = qn_ref[:, sl], dn_ref[:, sl]
            delta_n = jnp.sum(dn * an_ref[:, sl], axis=-1, keepdims=True)
            p_n = jnp.where(valid_next, jnp.exp(_dot(qn, kc, 1, 1) * ATT_SCALE - ln_ref[:, lane0]), 0.0)
            ds_n = p_n * (_dot(dn, vc, 1, 1) - delta_n) * ATT_SCALE
            dv_s[:, sl] = _dot(p[:, ATT_BLK:], dc, 0, 0) + _dot(p_n, dn, 0, 0)
            dk_s[:, sl] = _dot(ds[:, ATT_BLK:], qc, 0, 0) + _dot(ds_n, qn, 0, 0)
        cos_v, sin_v = cos_ref[...], sin_ref[...]
        dq = _rope_transpose(dq_s[...], cos_v, sin_v)
        dk = _rope_transpose(dk_s[...], cos_v, sin_v)
        dv = dv_s[...]
        dq_ref[...] = dq.astype(dq_ref.dtype)
        dk_ref[...] = dk.astype(dk_ref.dtype)
        dv_ref[...] = dv.astype(dv_ref.dtype)
        sq_ref[...] += _colsum(dq)
        sk_ref[...] += _colsum(dk)
        sv_ref[...] += _colsum(dv)

    view = functools.partial(_dilated_view, dil=dil)
    shape = jax.ShapeDtypeStruct((n_rows // dil, dil * ATT_MERGED), MXU_DTYPE)
    sums = jax.ShapeDtypeStruct((1, ATT_MERGED), F32)
    dq, dk, dv, sq, sk, sv = pl.pallas_call(
        body, name=name, grid=(dil, n_blk),
        in_specs=[cur, nxt, prev, cur, prev, cur, cur, nxt, cur, nxt, cur, nxt, cur, cur],
        out_specs=[cur, cur, cur, acc, acc, acc], out_shape=[shape, shape, shape, sums, sums, sums],
        scratch_shapes=[pltpu.VMEM((ATT_BLK, ATT_MERGED), F32)] * 3,
        compiler_params=_params("arbitrary", "arbitrary"),
    )(view(q), view(q), view(k), view(k), view(v), view(v), view(datt), view(datt), view(att), view(att),
      view(lse), view(lse), view(cos_t), view(sin_t))
    return [t.reshape(n_rows, ATT_MERGED) for t in (dq, dk, dv)], [sq, sk, sv]


def _xhead(h):
    return slice(h * XATT_HEAD_DIM, (h + 1) * XATT_HEAD_DIM)


def _xatt_probs(qh, kh):
    s = _dot(qh, kh, 1, 1) * XATT_SCALE
    e = jnp.exp(s - jnp.max(s, axis=-1, keepdims=True))
    return e / jnp.sum(e, axis=-1, keepdims=True)


def _xatt_fwd(q, kv, *, name, tm=512):
    n_rows = q.shape[0]
    n_mem = kv.shape[0]

    def body(q_ref, kv_ref, o_ref):
        for h in range(XATT_HEADS):
            sl = _xhead(h)
            p = _xatt_probs(q_ref[:, sl], kv_ref[:, sl])
            o_ref[:, sl] = _dot(p, kv_ref[:, D_MODEL + h * XATT_HEAD_DIM:D_MODEL + (h + 1) * XATT_HEAD_DIM], 1, 0
                                ).astype(o_ref.dtype)

    row = pl.BlockSpec((tm, D_MODEL), lambda i: (i, 0))
    return pl.pallas_call(
        body, name=name, grid=(n_rows // tm,),
        in_specs=[row, pl.BlockSpec((n_mem, 2 * D_MODEL), lambda i: (0, 0))], out_specs=row,
        out_shape=jax.ShapeDtypeStruct((n_rows, D_MODEL), MXU_DTYPE), compiler_params=_params("parallel"),
    )(q, kv)


def _xatt_bwd(q, kv, do, *, name, tm=512):
    n_rows = q.shape[0]
    n_mem = kv.shape[0]

    def body(q_ref, kv_ref, do_ref, dq_ref, dkv_ref):
        @pl.when(pl.program_id(0) == 0)
        def _():
            dkv_ref[...] = jnp.zeros_like(dkv_ref)

        for h in range(XATT_HEADS):
            sl = _xhead(h)
            vsl = slice(D_MODEL + h * XATT_HEAD_DIM, D_MODEL + (h + 1) * XATT_HEAD_DIM)
            qh, kh, doh = q_ref[:, sl], kv_ref[:, sl], do_ref[:, sl]
            p = _xatt_probs(qh, kh)
            dp = _dot(doh, kv_ref[:, vsl], 1, 1)
            ds = p * (dp - jnp.sum(dp * p, axis=-1, keepdims=True)) * XATT_SCALE
            dq_ref[:, sl] = _dot(ds, kh, 1, 0).astype(dq_ref.dtype)
            dkv_ref[:, sl] += _dot(ds, qh, 0, 0)
            dkv_ref[:, vsl] += _dot(p, doh, 0, 0)

    row = pl.BlockSpec((tm, D_MODEL), lambda i: (i, 0))
    full = pl.BlockSpec((n_mem, 2 * D_MODEL), lambda i: (0, 0))
    return pl.pallas_call(
        body, name=name, grid=(n_rows // tm,), in_specs=[row, full, row], out_specs=[row, full],
        out_shape=[jax.ShapeDtypeStruct((n_rows, D_MODEL), MXU_DTYPE), jax.ShapeDtypeStruct((n_mem, 2 * D_MODEL), F32)],
        compiler_params=_params("arbitrary"),
    )(q, kv, do)


def _disc(logdt, a_re, a_im, b_re, b_im):
    dt = jnp.exp(logdt)
    mag = jnp.exp(a_re * dt)
    ab_re = mag * jnp.cos(a_im * dt)
    ab_im = mag * jnp.sin(a_im * dt)
    den = jnp.square(a_re) + jnp.square(a_im)
    nr = ab_re - 1.0
    f_re = (nr * a_re + ab_im * a_im) / den
    f_im = (ab_im * a_re - nr * a_im) / den
    bb_re = f_re[None] * b_re - f_im[None] * b_im
    bb_im = f_re[None] * b_im + f_im[None] * b_re
    return ab_re, ab_im, bb_re, bb_im


def _disc_transpose(logdt, a_re, a_im, b_re, b_im, g_ab_re, g_ab_im, g_bb_re, g_bb_im):
    dt = jnp.exp(logdt)
    mag = jnp.exp(a_re * dt)
    th = a_im * dt
    cs, sn = jnp.cos(th), jnp.sin(th)
    ab_re, ab_im = mag * cs, mag * sn
    den = jnp.square(a_re) + jnp.square(a_im)
    nr = ab_re - 1.0
    f_re = (nr * a_re + ab_im * a_im) / den
    f_im = (ab_im * a_re - nr * a_im) / den
    d_f_re = jnp.sum(g_bb_re * b_re + g_bb_im * b_im, axis=0)
    d_f_im = jnp.sum(g_bb_im * b_re - g_bb_re * b_im, axis=0)
    d_b_re = g_bb_re * f_re[None] + g_bb_im * f_im[None]
    d_b_im = g_bb_im * f_re[None] - g_bb_re * f_im[None]
    d_n_re, d_n_im = d_f_re / den, d_f_im / den
    d_den = -(d_f_re * f_re + d_f_im * f_im) / den
    d_ab_re = g_ab_re + d_n_re * a_re - d_n_im * a_im
    d_ab_im = g_ab_im + d_n_re * a_im + d_n_im * a_re
    d_a_re = d_n_re * nr + d_n_im * ab_im + 2.0 * d_den * a_re
    d_a_im = d_n_re * ab_im - d_n_im * nr + 2.0 * d_den * a_im
    d_mag = d_ab_re * cs + d_ab_im * sn
    d_th = mag * (d_ab_im * cs - d_ab_re * sn)
    d_a_re = d_a_re + d_mag * mag * dt
    d_a_im = d_a_im + d_th * dt
    d_dt = jnp.sum(d_mag * mag * a_re + d_th * a_im, axis=-1, keepdims=True)
    return d_dt * dt, d_a_re, d_a_im, d_b_re, d_b_im


def _whole(fn, args, out_shapes, *, name):
    n_in = len(args)

    def body(*refs):
        res = fn(*[r[...] for r in refs[:n_in]])
        for o_ref, val in zip(refs[n_in:], res):
            o_ref[...] = val

    return pl.pallas_call(body, name=name, out_shape=[jax.ShapeDtypeStruct(s, F32) for s in out_shapes],
                          compiler_params=pltpu.CompilerParams(vmem_limit_bytes=VMEM_LIMIT_BYTES))(*args)


def _tiles_cn(t):
    t = t.reshape(SSM_TILES, GROUPS_PER_TILE, SSM_GROUP, SSM_STATE)
    eye = jnp.eye(GROUPS_PER_TILE, dtype=t.dtype)
    return (t[:, :, :, None, :] * eye[None, :, None, :, None]).reshape(SSM_TILES, LANES, GROUPS_PER_TILE * SSM_STATE)


def _tiles_nc(t):
    t = t.reshape(SSM_TILES, GROUPS_PER_TILE, SSM_STATE, SSM_GROUP)
    eye = jnp.eye(GROUPS_PER_TILE, dtype=t.dtype)
    return (t[:, :, :, None, :] * eye[None, :, None, :, None]).reshape(SSM_TILES, GROUPS_PER_TILE * SSM_STATE, LANES)


def _untile_cn(t):
    t = t.reshape(SSM_TILES, GROUPS_PER_TILE, SSM_GROUP, GROUPS_PER_TILE, SSM_STATE)
    eye = jnp.eye(GROUPS_PER_TILE, dtype=t.dtype)
    return jnp.sum(t * eye[None, :, None, :, None], axis=3).reshape(SSM_GROUPS, SSM_GROUP, SSM_STATE)


SSM_WIDE = GROUPS_PER_TILE * SSM_STATE
LANE_GROUPS_PER_TILE = SSM_WIDE // LANES


def _chan(j):
    return slice(j * LANES, (j + 1) * LANES)


def _time_major_rows(j, q, tc):
    return pl.ds(j * LANE_GROUPS_PER_TILE + q, tc, stride=STATE_VREG_ROWS)


def _to_time_major(x, t_re_ref, t_im_ref, dst_re, dst_im, tc):
    for j in range(SSM_TILES):
        xj = x[:, _chan(j)]
        for t_ref, dst in ((t_re_ref, dst_re), (t_im_ref, dst_im)):
            r = _dot(xj, t_ref[j], 1, 0)
            for q in range(LANE_GROUPS_PER_TILE):
                dst[_time_major_rows(j, q, tc), :] = r[:, q * LANES:(q + 1) * LANES]


def _from_time_major(src, j, tc):
    return jnp.concatenate([src[_time_major_rows(j, q, tc), :] for q in range(LANE_GROUPS_PER_TILE)], axis=1)


def _scan_chunk(w_re, w_im, h_re, h_im, a_re, a_im, start, tc):
    def step(t, carry):
        hr, hi = carry
        rows = _scan_rows(t)
        nr = a_re * hr - a_im * hi + w_re[rows, :]
        ni = a_re * hi + a_im * hr + w_im[rows, :]
        h_re[rows, :] = nr
        h_im[rows, :] = ni
        return nr, ni

    return lax.fori_loop(0, tc, step, start, unroll=8)


SSM_CHUNK = 256


def _ssm_fwd(proj, tb_re, tb_im, tc_re, tc_im, a_re, a_im, gain, *, name, tc=SSM_CHUNK):
    n_rows = proj.shape[0]
    n_chunk = n_rows // tc

    def body(u_ref, tbr_ref, tbi_ref, tcr_ref, tci_ref, ar_ref, ai_ref, g_ref, y_ref, gy_ref, sbr_ref, sbi_ref,
             wr, wi, hr, hi, state):
        @pl.when(pl.program_id(0) == 0)
        def _():
            state[...] = jnp.zeros_like(state)

        sbr_ref[0] = state[0]
        sbi_ref[0] = state[1]
        u = u_ref[...]
        _to_time_major(u, tbr_ref, tbi_ref, wr, wi, tc)
        state[0], state[1] = _scan_chunk(wr, wi, hr, hi, ar_ref[...], ai_ref[...], (state[0], state[1]), tc)
        for j in range(SSM_TILES):
            yj = (_dot(_from_time_major(hr, j, tc), tcr_ref[j], 1, 0) + _dot(_from_time_major(hi, j, tc), tci_ref[j], 1, 0)
                  + g_ref[:, _chan(j)] * u[:, _chan(j)])
            y_ref[:, _chan(j)] = yj
            gy_ref[:, _chan(j)] = jax.nn.gelu(yj).astype(gy_ref.dtype)

    rows = pl.BlockSpec((tc, SSM_WIDTH), lambda i: (i, 0))
    in_tile = pl.BlockSpec((SSM_TILES, LANES, SSM_WIDE), lambda i: (0, 0, 0))
    out_tile = pl.BlockSpec((SSM_TILES, SSM_WIDE, LANES), lambda i: (0, 0, 0))
    coef = pl.BlockSpec((STATE_VREG_ROWS, LANES), lambda i: (0, 0))
    bound = pl.BlockSpec((1, STATE_VREG_ROWS, LANES), lambda i: (i, 0, 0))
    bshape = jax.ShapeDtypeStruct((n_chunk, STATE_VREG_ROWS, LANES), F32)
    tm_scratch = pltpu.VMEM((tc * STATE_VREG_ROWS, LANES), F32)
    return pl.pallas_call(
        body, name=name, grid=(n_chunk,),
        in_specs=[rows, in_tile, in_tile, out_tile, out_tile, coef, coef, pl.BlockSpec((1, SSM_WIDTH), lambda i: (0, 0))],
        out_specs=[rows, rows, bound, bound],
        out_shape=[jax.ShapeDtypeStruct((n_rows, SSM_WIDTH), F32), jax.ShapeDtypeStruct((n_rows, SSM_WIDTH), MXU_DTYPE),
                   bshape, bshape],
        scratch_shapes=[tm_scratch] * 4 + [pltpu.VMEM((2, STATE_VREG_ROWS, LANES), F32)],
        compiler_params=_params("arbitrary"),
    )(proj, tb_re, tb_im, tc_re, tc_im, a_re, a_im, gain)


def _ssm_bwd(proj, dy, sb_re, sb_im, tb_re, tb_im, td_re, td_im, tu_re, tu_im, a_re, a_im, gain, *, name, tc=SSM_CHUNK):
    n_rows = proj.shape[0]
    n_chunk = n_rows // tc

    def body(u_ref, dy_ref, sbr_ref, sbi_ref, tbr_ref, tbi_ref, tdr_ref, tdi_ref, tur_ref, tui_ref, ar_ref, ai_ref, g_ref,
             du_ref, su_ref, dcr_ref, dci_ref, dbr_ref, dbi_ref, dar_ref, dai_ref, wr, wi, hr, hi, carry):
        @pl.when(pl.program_id(0) == 0)
        def _():
            carry[...] = jnp.zeros_like(carry)
            for acc_ref in (su_ref, dcr_ref, dci_ref, dbr_ref, dbi_ref):
                acc_ref[...] = jnp.zeros_like(acc_ref)

        a_r, a_i = ar_ref[...], ai_ref[...]
        u, dyv = u_ref[...], dy_ref[...]
        _to_time_major(u, tbr_ref, tbi_ref, wr, wi, tc)
        _scan_chunk(wr, wi, hr, hi, a_r, a_i, (sbr_ref[0], sbi_ref[0]), tc)
        _to_time_major(dyv, tdr_ref, tdi_ref, wr, wi, tc)

        def step(kk, c):
            lam_r, lam_i, dar, dai = c
            rows = _scan_rows(tc - 1 - kk)
            h_r, h_i = hr[rows, :], hi[rows, :]
            dar = dar + lam_r * h_r + lam_i * h_i
            dai = dai + lam_i * h_r - lam_r * h_i
            new_r = wr[rows, :] + a_r * lam_r + a_i * lam_i
            new_i = wi[rows, :] + a_r * lam_i - a_i * lam_r
            wr[rows, :] = new_r
            wi[rows, :] = new_i
            return new_r, new_i, dar, dai

        carry[0], carry[1], carry[2], carry[3] = lax.fori_loop(0, tc, step, (carry[0], carry[1], carry[2], carry[3]),
                                                              unroll=8)
        dar_ref[...] = carry[2]
        dai_ref[...] = carry[3]
        for j in range(SSM_TILES):
            cj = _chan(j)
            lam_r, lam_i = _from_time_major(wr, j, tc), _from_time_major(wi, j, tc)
            dcr_ref[j] += _dot(dyv[:, cj], _from_time_major(hr, j, tc), 0, 0)
            dci_ref[j] += _dot(dyv[:, cj], _from_time_major(hi, j, tc), 0, 0)
            dbr_ref[j] += _dot(u[:, cj], lam_r, 0, 0)
            dbi_ref[j] += _dot(u[:, cj], lam_i, 0, 0)
            duj = _dot(lam_r, tur_ref[j], 1, 0) + _dot(lam_i, tui_ref[j], 1, 0) + g_ref[:, cj] * dyv[:, cj]
            du_ref[:, cj] = duj.astype(du_ref.dtype)
            su_ref[:, cj] += _colsum(duj)

    back = lambda i: (n_chunk - 1 - i, 0)
    rows = pl.BlockSpec((tc, SSM_WIDTH), back)
    in_tile = pl.BlockSpec((SSM_TILES, LANES, SSM_WIDE), lambda i: (0, 0, 0))
    out_tile = pl.BlockSpec((SSM_TILES, SSM_WIDE, LANES), lambda i: (0, 0, 0))
    coef = pl.BlockSpec((STATE_VREG_ROWS, LANES), lambda i: (0, 0))
    bound = pl.BlockSpec((1, STATE_VREG_ROWS, LANES), lambda i: (n_chunk - 1 - i, 0, 0))
    vec = pl.BlockSpec((1, SSM_WIDTH), lambda i: (0, 0))
    tshape = jax.ShapeDtypeStruct((SSM_TILES, LANES, SSM_WIDE), F32)
    cshape = jax.ShapeDtypeStruct((STATE_VREG_ROWS, LANES), F32)
    tm_scratch = pltpu.VMEM((tc * STATE_VREG_ROWS, LANES), F32)
    return pl.pallas_call(
        body, name=name, grid=(n_chunk,),
        in_specs=[rows, rows, bound, bound, in_tile, in_tile, in_tile, in_tile, out_tile, out_tile, coef, coef, vec],
        out_specs=[rows, vec, in_tile, in_tile, in_tile, in_tile, coef, coef],
        out_shape=[jax.ShapeDtypeStruct((n_rows, SSM_WIDTH), MXU_DTYPE), jax.ShapeDtypeStruct((1, SSM_WIDTH), F32),
                   tshape, tshape, tshape, tshape, cshape, cshape],
        scratch_shapes=[tm_scratch] * 4 + [pltpu.VMEM((4, STATE_VREG_ROWS, LANES), F32)],
        compiler_params=_params("arbitrary"),
    )(proj, dy, sb_re, sb_im, tb_re, tb_im, td_re, td_im, tu_re, tu_im, a_re, a_im, gain)


def _scan_rows(t):
    return pl.ds(pl.multiple_of(t * STATE_VREG_ROWS, 8), STATE_VREG_ROWS)


GATHER_GROUPS = (("w_in",), ("w_glu", "w_att_up", "w_mix_out"), ("w_xq", "w_xkv", "w_xo", "w_ff1", "w_ff2"))
SCATTER_GROUPS = (("w_ff2", "w_ff1"), ("w_xo", "w_xq", "w_xkv", "w_mix_out"), ("w_att_up", "w_glu", "w_in"))


def _local_grads(x, mem, pos_col, target, sm, fetch, send, start_token):
    b_re_t = sm["ssm_b_re"].transpose(2, 0, 1)
    b_im_t = sm["ssm_b_im"].transpose(2, 0, 1)
    logdt = sm["ssm_log_dt"].reshape(SSM_GROUPS, 1)
    c_re, c_im = sm["ssm_c_re"], sm["ssm_c_im"]
    grp = (SSM_GROUPS, SSM_STATE)
    chn = (SSM_GROUP, SSM_GROUPS, SSM_STATE)

    wts = {}
    cos_t, sin_t = _rope_tables(pos_col, after=start_token, name="rope_tables")
    h0, xh0, rs0 = _ln_fwd(x, None, sm["ln_in_g"], sm["ln_in_b"], alpha=1.0, name="ln_in_fwd")
    disc_in = (logdt, sm["ssm_a_re"], sm["ssm_a_im"], b_re_t, b_im_t)
    ab_re, ab_im, bb_re_t, bb_im_t = _whole(_disc, disc_in, [grp, grp, chn, chn], name="ssm_disc")
    a_re_rows, a_im_rows = ab_re.reshape(STATE_VREG_ROWS, LANES), ab_im.reshape(STATE_VREG_ROWS, LANES)
    wts.update(fetch(0, h0))
    proj = _mm(h0, wts["w_in"], bias=sm["b_in"], b_shards=True, name="in_proj")

    mxu = lambda t: t.astype(MXU_DTYPE)
    tb_re, tb_im = mxu(_tiles_cn(bb_re_t.transpose(1, 0, 2))), mxu(_tiles_cn(bb_im_t.transpose(1, 0, 2)))
    y, gy, sb_re, sb_im = _ssm_fwd(proj, tb_re, tb_im, mxu(_tiles_nc(c_re.transpose(0, 2, 1))),
                                   mxu(_tiles_nc(-c_im.transpose(0, 2, 1))), a_re_rows, a_im_rows, sm["ssm_d"],
                                   name="ssm_fwd")

    q, k, v = _qkv_split(proj, cos_t, sin_t, name="qkv_split")
    outs, lses = [], []
    for g, dil in enumerate(DILATIONS):
        o_g, l_g = _dil_fwd(q[g], k[g], v[g], dil, name=f"dil_att_fwd_{dil}")
        outs.append(o_g)
        lses.append(l_g)
    att, lse = _att_merge(outs, lses, name="att_merge")
    wts.update(fetch(1, att))
    z = _mm(gy, wts["w_glu"], bias=sm["b_glu"], b_shards=True, name="glu_proj")
    b_att = _mm(att, wts["w_att_up"], b_shards=True, name="att_up")

    mixed = _mix_fwd(proj, z, b_att, name="gate_mix")
    mix_out = _mm(mixed, wts["w_mix_out"], bias=sm["b_mix_out"], name="mix_out")
    h1, xh1, rs1 = _ln_fwd(h0, mix_out, sm["ln1_g"], sm["ln1_b"], alpha=DEEPNORM_ALPHA, name="ln1_fwd")

    wts.update(fetch(2, h1))
    xq = _mm(h1, wts["w_xq"], out_dtype=MXU_DTYPE, name="xatt_q")
    kv = _mm(mem, wts["w_xkv"], out_dtype=MXU_DTYPE, b_shards=True, name="xatt_kv")
    xo_in = _xatt_fwd(xq, kv, name="xatt_fwd")
    xo = _mm(xo_in, wts["w_xo"], name="xatt_o")
    h2, xh2, rs2 = _ln_fwd(h1, xo, sm["ln2_g"], sm["ln2_b"], alpha=DEEPNORM_ALPHA, name="ln2_fwd")

    pre, act = _mm(h2, wts["w_ff1"], bias=sm["b_ff1"], b_shards=True, name="ff1",
                   also=(lambda r: jnp.square(jnp.maximum(r, 0.0)), MXU_DTYPE))
    ff = _mm(act, wts["w_ff2"], bias=sm["b_ff2"], name="ff2")
    h3, xh3, rs3 = _ln_fwd(h2, ff, sm["ln3_g"], sm["ln3_b"], alpha=DEEPNORM_ALPHA, name="ln3_fwd")
    dh3, loss_row = _loss_head(h3, target, name="loss_head")

    gw, gs = {}, {}
    dr3, gs["ln3_g"], gs["ln3_b"], gs["b_ff2"] = _ln_bwd(None, dh3, xh3, rs3, sm["ln3_g"], alpha=1.0, name="ln3_bwd")
    wgrad = functools.partial(_mm, ta=True, out_dtype=WIRE_DTYPE)
    gw["w_ff2"] = wgrad(act, dr3, name="ff2_dw")
    dpre, gs["b_ff1"] = _mm(dr3, wts["w_ff2"], tb=True, out_dtype=MXU_DTYPE, colsum=True, name="ff2_dx",
                            gate=(pre, lambda p: 2.0 * jnp.maximum(p, 0.0)))
    gw["w_ff1"] = wgrad(h2, dpre, out_shards=True, name="ff1_dw")
    sent = send(0, gw)
    dh2 = _mm(dpre, wts["w_ff1"], tb=True, b_shards=True, after=sent, name="ff1_dx")

    dr2, gs["ln2_g"], gs["ln2_b"], _ = _ln_bwd(dr3, dh2, xh2, rs2, sm["ln2_g"], alpha=DEEPNORM_ALPHA, name="ln2_bwd")
    gw["w_xo"] = wgrad(xo_in, dr2, name="xatt_o_dw")
    dxo_in = _mm(dr2, wts["w_xo"], tb=True, out_dtype=MXU_DTYPE, name="xatt_o_dx")
    dxq, dkv = _xatt_bwd(xq, kv, dxo_in, name="xatt_bwd")
    gw["w_xq"] = wgrad(h1, dxq, name="xatt_q_dw")
    gw["w_xkv"] = wgrad(mem, dkv, out_shards=True, name="xatt_kv_dw")
    dh1 = _mm(dxq, wts["w_xq"], tb=True, name="xatt_q_dx")

    dr1, gs["ln1_g"], gs["ln1_b"], gs["b_mix_out"] = _ln_bwd(dr2, dh1, xh1, rs1, sm["ln1_g"], alpha=DEEPNORM_ALPHA,
                                                             name="ln1_bwd")
    gw["w_mix_out"] = wgrad(mixed, dr1, name="mix_out_dw")
    sent = send(1, gw)
    dmixed = _mm(dr1, wts["w_mix_out"], tb=True, after=sent, name="mix_out_dx")
    dgs, dga, dz, db_att, s_gs, s_ga, gs["b_glu"] = _mix_bwd(dmixed, proj, z, b_att, name="gate_mix_bwd")

    gw["w_att_up"] = wgrad(att, db_att, out_shards=True, name="att_up_dw")
    datt = _mm(db_att, wts["w_att_up"], tb=True, b_shards=True, name="att_up_dx")
    dqkv, sqkv = [], []
    for g, dil in enumerate(DILATIONS):
        d_g, s_g = _dil_bwd(q[g], k[g], v[g], datt, att, lse, cos_t, sin_t, dil, name=f"dil_att_bwd_{dil}")
        dqkv.append(d_g)
        sqkv.append(s_g)

    gw["w_glu"] = wgrad(gy, dz, out_shards=True, name="glu_dw")
    dgy = _mm(dz, wts["w_glu"], tb=True, b_shards=True, name="glu_dx")
    dy, gs["ssm_d"] = _gelu_bwd(dgy, y, proj, name="gelu_bwd")
    du, s_u, dc_re_t, dc_im_t, dbb_re_t, dbb_im_t, da_re, da_im = _ssm_bwd(
        proj, dy, sb_re, sb_im, tb_re, tb_im, mxu(_tiles_cn(c_re)), mxu(_tiles_cn(-c_im)),
        mxu(_tiles_nc(bb_re_t.transpose(1, 2, 0))), mxu(_tiles_nc(bb_im_t.transpose(1, 2, 0))),
        a_re_rows, a_im_rows, sm["ssm_d"], name="ssm_bwd")
    gs["ssm_c_re"], gs["ssm_c_im"] = _untile_cn(dc_re_t), -_untile_cn(dc_im_t)
    disc_ct = (da_re.reshape(grp), da_im.reshape(grp), _untile_cn(dbb_re_t).transpose(1, 0, 2),
               _untile_cn(dbb_im_t).transpose(1, 0, 2))
    d_logdt, gs["ssm_a_re"], gs["ssm_a_im"], d_b_re_t, d_b_im_t = _whole(
        _disc_transpose, disc_in + disc_ct, [(SSM_GROUPS, 1), grp, grp, chn, chn], name="ssm_disc_bwd")
    gs["ssm_log_dt"] = d_logdt
    gs["ssm_b_re"], gs["ssm_b_im"] = d_b_re_t.transpose(1, 2, 0), d_b_im_t.transpose(1, 2, 0)

    dproj = jnp.concatenate([du] + [dqkv[g][i] for i in range(3) for g in range(len(DILATIONS))] + [dgs, dga], axis=1)
    gs["b_in"] = jnp.concatenate([s_u] + [sqkv[g][i] for i in range(3) for g in range(len(DILATIONS))] + [s_gs, s_ga],
                                 axis=1)
    gw["w_in"] = wgrad(h0, dproj, out_shards=True, name="in_proj_dw")
    sent = send(2, gw)
    dh0 = _mm(dproj, wts["w_in"], tb=True, b_shards=True, after=sent, name="in_proj_dx")
    grad_x, gs["ln_in_g"], gs["ln_in_b"], _ = _ln_bwd(dr1, dh0, xh0, rs0, sm["ln_in_g"], alpha=DEEPNORM_ALPHA,
                                                      name="ln_in_bwd")
    return loss_row, grad_x, gs


N_PEER = N_DEV - 1
_IN_HBM = pl.BlockSpec(memory_space=pltpu.HBM)
_IN_SEMAPHORE = pl.BlockSpec(memory_space=pltpu.SEMAPHORE)


def _device_index():
    return 4 * lax.axis_index("x") + 2 * lax.axis_index("y") + lax.axis_index("c")


def _exchange_copies(src_refs, land_refs, send_sems, recv_sems, scatter):
    x, y, c = lax.axis_index("x"), lax.axis_index("y"), lax.axis_index("c")
    me = 4 * x + 2 * y + c
    pairs = []
    for a, (src_ref, land_ref) in enumerate(zip(src_refs, land_refs)):
        for kk in range(1, N_DEV):
            px = (x + (kk >> 2)) % 2
            py = (y + ((kk >> 1) & 1)) % 2
            pc = (c + (kk & 1)) % 2
            peer = 4 * px + 2 * py + pc
            sem = a * N_PEER + kk - 1
            src = src_ref.at[peer] if scatter else src_ref

            def copy(dst, src=src, sem=sem, px=px, py=py, pc=pc):
                return pltpu.make_async_remote_copy(
                    src_ref=src, dst_ref=dst, send_sem=send_sems.at[sem], recv_sem=recv_sems.at[sem],
                    device_id=(px, py, pc), device_id_type=pl.DeviceIdType.MESH)

            pairs.append((functools.partial(copy, land_ref.at[me]), functools.partial(copy, land_ref.at[peer])))
    return pairs


def _exchange_start(srcs, *, scatter, name, after=None):
    n_arr = len(srcs)
    lands = [lax.empty((N_DEV,) + tuple(s.shape[1:] if scatter else s.shape), s.dtype) for s in srcs]
    n_in = 2 * n_arr + (after is not None)

    def body(*refs):
        send_sems, recv_sems = refs[n_in], refs[n_in + 1]
        for sent, _ in _exchange_copies(refs[:n_arr], refs[n_arr:2 * n_arr], send_sems, recv_sems, scatter):
            sent().start()
        refs[-1][...] = jnp.zeros_like(refs[-1])

    through = [pltpu.HBM(t.shape, t.dtype) for t in (*srcs, *lands)]
    res = pl.pallas_call(
        body, name=name,
        out_shape=(pltpu.SemaphoreType.DMA((n_arr * N_PEER,)), pltpu.SemaphoreType.DMA((n_arr * N_PEER,)), *through,
                   jax.ShapeDtypeStruct((8, LANES), F32)),
        in_specs=[_IN_HBM] * (2 * n_arr) + [pl.BlockSpec(memory_space=pl.ANY)] * (after is not None),
        out_specs=(_IN_SEMAPHORE, _IN_SEMAPHORE, *[_IN_HBM] * (2 * n_arr), pl.BlockSpec(memory_space=pltpu.VMEM)),
        input_output_aliases={i: 2 + i for i in range(2 * n_arr)},
        compiler_params=pltpu.CompilerParams(has_side_effects=pltpu.SideEffectType.DATAFLOW_SIDE_EFFECTING),
    )(*[pltpu.with_memory_space_constraint(t, pltpu.HBM) for t in (*srcs, *lands)],
      *([after] if after is not None else []))
    return (res[0], res[1], res[2:2 + n_arr], res[2 + n_arr:2 + 2 * n_arr], scatter), res[-1]


def _exchange_wait(handle, *, after, name):
    send_sems, recv_sems, srcs, lands, scatter = handle
    n_arr = len(srcs)

    def body(*refs):
        for sent, received in _exchange_copies(refs[:n_arr], refs[n_arr:2 * n_arr], refs[2 * n_arr], refs[2 * n_arr + 1],
                                               scatter):
            sent().wait_send()
            received().wait_recv()

    res = pl.pallas_call(
        body, name=name, out_shape=tuple(pltpu.HBM(t.shape, t.dtype) for t in (*srcs, *lands)),
        in_specs=[_IN_HBM] * (2 * n_arr) + [_IN_SEMAPHORE, _IN_SEMAPHORE, pl.BlockSpec(memory_space=pl.ANY)],
        out_specs=tuple([_IN_HBM] * (2 * n_arr)), input_output_aliases={i: i for i in range(2 * n_arr)},
        compiler_params=pltpu.CompilerParams(has_side_effects=pltpu.SideEffectType.DATAFLOW_SIDE_EFFECTING),
    )(*srcs, *lands, send_sems, recv_sems, after)
    return res[n_arr:]


def _with_own_slot(land, own):
    return lax.dynamic_update_slice_in_dim(land, own[None], _device_index(), axis=0)


def _reduce_adamw(gstack, w, m, v, *, name, tr=128):
    n_rows, cols = w.shape
    tr = min(tr, n_rows)
    assert n_rows % tr == 0, (name, n_rows, tr)

    def body(g_ref, w_ref, m_ref, v_ref, go_ref, d_ref, mo_ref, vo_ref):
        g = g_ref[0].astype(F32)
        for dev in range(1, N_DEV):
            g = g + g_ref[dev].astype(F32)
        m_new = ADAM_B1 * m_ref[...] + (1.0 - ADAM_B1) * g
        v_new = ADAM_B2 * v_ref[...] + (1.0 - ADAM_B2) * jnp.square(g)
        m_hat = m_new / (1.0 - ADAM_B1 ** ADAM_STEP)
        v_hat = v_new / (1.0 - ADAM_B2 ** ADAM_STEP)
        go_ref[...] = g
        d_ref[...] = -ADAM_LR * (m_hat / (jnp.sqrt(v_hat) + ADAM_EPS) + ADAM_WD * w_ref[...])
        mo_ref[...] = m_new
        vo_ref[...] = v_new

    flat = pl.BlockSpec((tr, cols), lambda i: (i, 0))
    shape = jax.ShapeDtypeStruct((n_rows, cols), F32)
    return pl.pallas_call(
        body, name=name, grid=(n_rows // tr,),
        in_specs=[pl.BlockSpec((N_DEV, tr, cols), lambda i: (0, i, 0)), flat, flat, flat],
        out_specs=[flat] * 4, out_shape=[shape] * 4, compiler_params=_params("parallel"),
    )(gstack, w, m, v)


def _pack(parts, dtype):
    flat = jnp.concatenate([p.reshape(-1).astype(dtype) for p in parts])
    unit = PACK_COLS * PACK_ROW_ALIGN
    total = -(-flat.shape[0] // unit) * unit
    return jnp.pad(flat, (0, total - flat.shape[0])).reshape(-1, PACK_COLS)


def _unpack(packed, shapes):
    flat = packed.reshape(-1)
    out, off = [], 0
    for s in shapes:
        size = int(np.prod(s))
        out.append(flat[off:off + size].reshape(s))
        off += size
    return out


def kernel(x, mem, positions, ln_in_g, ln_in_b, w_in, b_in, ssm_log_dt, ssm_a_re, ssm_a_im, ssm_b_re, ssm_b_im, ssm_c_re, ssm_c_im, ssm_d, w_glu, b_glu, w_att_up, w_mix_out, b_mix_out, ln1_g, ln1_b, w_xq, w_xkv, w_xo, ln2_g, ln2_b, w_ff1, b_ff1, w_ff2, b_ff2, ln3_g, ln3_b, loss_target, m_ln_in_g, m_ln_in_b, m_w_in, m_b_in, m_ssm_log_dt, m_ssm_a_re, m_ssm_a_im, m_ssm_b_re, m_ssm_b_im, m_ssm_c_re, m_ssm_c_im, m_ssm_d, m_w_glu, m_b_glu, m_w_att_up, m_w_mix_out, m_b_mix_out, m_ln1_g, m_ln1_b, m_w_xq, m_w_xkv, m_w_xo, m_ln2_g, m_ln2_b, m_w_ff1, m_b_ff1, m_w_ff2, m_b_ff2, m_ln3_g, m_ln3_b, v_ln_in_g, v_ln_in_b, v_w_in, v_b_in, v_ssm_log_dt, v_ssm_a_re, v_ssm_a_im, v_ssm_b_re, v_ssm_b_im, v_ssm_c_re, v_ssm_c_im, v_ssm_d, v_w_glu, v_b_glu, v_w_att_up, v_w_mix_out, v_b_mix_out, v_ln1_g, v_ln1_b, v_w_xq, v_w_xkv, v_w_xo, v_ln2_g, v_ln2_b, v_w_ff1, v_b_ff1, v_w_ff2, v_b_ff2, v_ln3_g, v_ln3_b):
    given = dict(locals())
    w_arg = {n: given[n] for n in WEIGHTS}
    m_arg = {n: given["m_" + n] for n in WEIGHTS}
    v_arg = {n: given["v_" + n] for n in WEIGHTS}

    shards = {n: w_arg[n][0].astype(MXU_DTYPE) for n in BIG}
    gathers, token = [], None
    for i, names in enumerate(GATHER_GROUPS):
        handle, token = _exchange_start([shards[n] for n in names], scatter=False, after=token, name=f"gather_start_{i}")
        gathers.append(handle)

    def fetch(i, after):
        lands = _exchange_wait(gathers[i], after=after, name=f"gather_wait_{i}")
        full = {n: _with_own_slot(land, shards[n]) for n, land in zip(GATHER_GROUPS[i], lands)}
        return {n: t if n in BIG_COL_SHARDED else t.reshape(-1, t.shape[-1]) for n, t in full.items()}

    scatters = {}

    def send(i, gw):
        slots = [gw[n] if n in BIG_COL_SHARDED else gw[n].reshape(N_DEV, -1, gw[n].shape[-1]) for n in SCATTER_GROUPS[i]]
        handle, sent = _exchange_start(slots, scatter=True, name=f"scatter_start_{i}")
        scatters[i] = (handle, slots)
        return sent

    sm = {}
    for n in SMALL:
        t = w_arg[n]
        if n.startswith("ssm_") and n not in ("ssm_d", "ssm_log_dt"):
            sm[n] = t[0]
        else:
            sm[n] = t.reshape(1, -1)

    loss_row, grad_x, gs = _local_grads(x[0], mem[0], positions.reshape(-1, 1), loss_target[0], sm, fetch, send, token)
    loss = lax.psum(loss_row[0, 0], ("x", "y", "c"))
    small = _pack([gs[n] for n in SMALL], WIRE_DTYPE)
    small_handle, _ = _exchange_start([small], scatter=False, name="small_start")

    results = [{}, {}, {}, {}]
    done = grad_x
    for i, names in enumerate(SCATTER_GROUPS):
        handle, slots = scatters[i]
        lands = _exchange_wait(handle, after=done, name=f"scatter_wait_{i}")
        for n, land, slot in zip(names, lands, slots):
            own = lax.dynamic_index_in_dim(slot, _device_index(), axis=0, keepdims=False)
            res = _reduce_adamw(_with_own_slot(land, own), w_arg[n][0], m_arg[n][0], v_arg[n][0], name="adamw_" + n)
            done = res[0]
            for d, r in zip(results, res):
                d[n] = r[None]
    small_stack = _with_own_slot(_exchange_wait(small_handle, after=done, name="small_wait")[0], small)
    small_shapes = [w_arg[n].shape for n in SMALL]
    res = _reduce_adamw(small_stack, *[_pack([d[n] for n in SMALL], F32) for d in (w_arg, m_arg, v_arg)],
                        name="adamw_small")
    for d, r in zip(results, res):
        d.update(zip(SMALL, _unpack(r, small_shapes)))
    out = [loss, grad_x[None]]
    for d in results:
        out += [d[n] for n in WEIGHTS]
    return tuple(out)
```

```python
import functools

import numpy as np
import jax
import jax.numpy as jnp
from jax import lax
from jax.experimental import pallas as pl
from jax.experimental.pallas import tpu as pltpu

F32 = jnp.float32
MXU_DTYPE = jnp.bfloat16
WIRE_DTYPE = jnp.bfloat16
VMEM_LIMIT_BYTES = 48 * 1024 * 1024
LANES = 128

N_DEV = 8
D_MODEL = 1024
SSM_GROUP = 16
SSM_WIDTH = 768
SSM_GROUPS = SSM_WIDTH // SSM_GROUP
SSM_STATE = 64
SSM_CH = SSM_GROUPS * SSM_STATE
SSM_TILES = SSM_WIDTH // LANES
GROUPS_PER_TILE = LANES // SSM_GROUP
STATE_VREG_ROWS = SSM_CH // LANES
ATT_HEAD_DIM = 64
ATT_HEADS_PER_GROUP = 4
ATT_MERGED = ATT_HEADS_PER_GROUP * ATT_HEAD_DIM
LANE_HALVES = ATT_MERGED // LANES
DILATIONS = (1, 4, 16)
ATT_BLK = 128
ATT_SCALE = ATT_HEAD_DIM ** -0.5
ROT_DIM = ATT_HEAD_DIM // 4
ROPE_THETA = 500000.0
XATT_HEADS = 4
XATT_HEAD_DIM = D_MODEL // XATT_HEADS
XATT_SCALE = XATT_HEAD_DIM ** -0.5
DEEPNORM_ALPHA = 2.0 ** 0.25
LN_EPS = 1e-5
NEG_INF = -1e30
OFF_Q_BLK, OFF_K_BLK, OFF_V_BLK = 3, 6, 9
OFF_GS_BLK, OFF_GA_BLK = 3, 4

ADAM_LR = 0.001
ADAM_B1 = 0.9
ADAM_B2 = 0.999
ADAM_EPS = 1e-08
ADAM_WD = 0.01
ADAM_STEP = 10

BIG = ("w_in", "w_glu", "w_att_up", "w_mix_out", "w_xq", "w_xkv", "w_xo", "w_ff1", "w_ff2")
BIG_COL_SHARDED = ("w_in", "w_glu", "w_att_up", "w_xkv", "w_ff1")
WEIGHTS = ("ln_in_g", "ln_in_b", "w_in", "b_in", "ssm_log_dt", "ssm_a_re", "ssm_a_im", "ssm_b_re", "ssm_b_im",
           "ssm_c_re", "ssm_c_im", "ssm_d", "w_glu", "b_glu", "w_att_up", "w_mix_out", "b_mix_out", "ln1_g", "ln1_b",
           "w_xq", "w_xkv", "w_xo", "ln2_g", "ln2_b", "w_ff1", "b_ff1", "w_ff2", "b_ff2", "ln3_g", "ln3_b")
SMALL = tuple(n for n in WEIGHTS if n not in BIG)
PACK_COLS = 1024
PACK_ROW_ALIGN = 256


def _params(*sem):
    return pltpu.CompilerParams(dimension_semantics=sem, vmem_limit_bytes=VMEM_LIMIT_BYTES)


def _dot(a, b, ca, cb):
    return lax.dot_general(a.astype(MXU_DTYPE), b.astype(MXU_DTYPE), (((ca,), (cb,)), ((), ())),
                           preferred_element_type=F32)


def _fit(dim, pref):
    if dim <= pref:
        return dim
    best = max(t for t in range(LANES, pref + 1, LANES) if dim % t == 0)
    return best


def _mm(a, b, *, name, ta=False, tb=False, bias=None, out_dtype=F32, b_shards=False, out_shards=False, after=None,
        also=None, gate=None, colsum=False, tm=1024, tn=1024, tk=1024):
    m, k = (a.shape[1], a.shape[0]) if ta else a.shape
    order = (lambda f: (lambda j, i, kk: f(i, j, kk))) if colsum else (lambda f: f)
    spec = lambda shape, f: pl.BlockSpec(shape, order(f))
    if b_shards:
        n_sh, rows, n_loc = b.shape
        if tb:
            n, tn, tk = rows, _fit(rows, tn), n_loc
            assert k == n_sh * n_loc, (name, k, b.shape)
            b_spec = spec((1, tn, tk), lambda i, j, kk: (kk, j, 0))
        else:
            n, tn, tk = n_sh * n_loc, n_loc, _fit(k, tk)
            b_spec = spec((1, tk, tn), lambda i, j, kk: (j, kk, 0))
    else:
        n = b.shape[0] if tb else b.shape[1]
        tn = n // N_DEV if out_shards else _fit(n, tn)
        tk = _fit(k, tk)
        b_spec = spec((tn, tk), lambda i, j, kk: (j, kk)) if tb else spec((tk, tn), lambda i, j, kk: (kk, j))
    tm = _fit(m, tm)
    nk = k // tk
    a_spec = spec((tk, tm), lambda i, j, kk: (kk, i)) if ta else spec((tm, tk), lambda i, j, kk: (i, kk))
    tile = spec((tm, tn), lambda i, j, kk: (i, j))
    in_specs, args = [a_spec, b_spec], [a, b]
    if bias is not None:
        in_specs.append(spec((1, tn), lambda i, j, kk: (0, j)))
        args.append(bias)
    if gate is not None:
        in_specs.append(tile)
        args.append(gate[0])
    if after is not None:
        in_specs.append(pl.BlockSpec(memory_space=pl.ANY))
        args.append(after)
    n_in = len(args)
    if out_shards:
        assert n == N_DEV * tn, (name, n, tn)
        out_specs = [spec((1, tm, tn), lambda i, j, kk: (j, i, 0))]
        out_shape = [jax.ShapeDtypeStruct((N_DEV, m, tn), out_dtype)]
    else:
        out_specs = [tile]
        out_shape = [jax.ShapeDtypeStruct((m, n), out_dtype)]
    if also is not None:
        out_specs.append(tile)
        out_shape.append(jax.ShapeDtypeStruct((m, n), also[1]))
    if colsum:
        out_specs.append(spec((1, tn), lambda i, j, kk: (0, j)))
        out_shape.append(jax.ShapeDtypeStruct((1, n), F32))

    def body(*refs):
        a_ref, b_ref = refs[0], refs[1]
        o_ref = refs[n_in]

        def product():
            return _dot(a_ref[...], b_ref[0] if b_shards else b_ref[...], 0 if ta else 1, 1 if tb else 0)

        def finish(r):
            if bias is not None:
                r = r + refs[2][...]
            if gate is not None:
                r = r * gate[1](refs[2 + (bias is not None)][...])
            if out_shards:
                o_ref[0] = r.astype(o_ref.dtype)
            else:
                o_ref[...] = r.astype(o_ref.dtype)
            if also is not None:
                refs[n_in + 1][...] = also[0](r).astype(also[1])
            if colsum:
                s_ref = refs[n_in + 1 + (also is not None)]

                @pl.when(pl.program_id(1) == 0)
                def _():
                    s_ref[...] = jnp.zeros_like(s_ref)

                s_ref[...] += _colsum(r)

        if nk == 1:
            finish(product())
            return
        acc_ref = refs[-1]
        kk = pl.program_id(2)

        @pl.when(kk == 0)
        def _():
            acc_ref[...] = jnp.zeros_like(acc_ref)

        acc_ref[...] += product()

        @pl.when(kk == nk - 1)
        def _():
            finish(acc_ref[...])

    grid = (n // tn, m // tm, nk) if colsum else (m // tm, n // tn, nk)
    res = pl.pallas_call(
        body, name=name, grid=grid, in_specs=in_specs, out_specs=out_specs, out_shape=out_shape,
        scratch_shapes=[pltpu.VMEM((tm, tn), F32)] if nk > 1 else [],
        compiler_params=_params("parallel", "arbitrary" if colsum else "parallel", "arbitrary"),
    )(*args)
    return res[0] if len(res) == 1 else res


def _rowcall(fn, rows, fulls, row_outs, acc_outs=(), *, n_rows, tm, name, after=None):
    n_r, n_f, n_o, n_a = len(rows), len(fulls), len(row_outs), len(acc_outs)
    n_in = n_r + n_f + (after is not None)
    assert n_rows % tm == 0, (name, n_rows, tm)

    def body(*refs):
        res = fn(*[r[...] for r in refs[:n_r + n_f]])
        res = tuple(res) if isinstance(res, (tuple, list)) else (res,)
        o_refs = refs[n_in:n_in + n_o]
        a_refs = refs[n_in + n_o:]
        for o_ref, val in zip(o_refs, res[:n_o]):
            o_ref[...] = val.astype(o_ref.dtype)
        if n_a:
            @pl.when(pl.program_id(0) == 0)
            def _():
                for a_ref in a_refs:
                    a_ref[...] = jnp.zeros_like(a_ref)

            for a_ref, val in zip(a_refs, res[n_o:]):
                a_ref[...] += val

    in_specs = [pl.BlockSpec((tm, w), functools.partial(lambda i, cb: (i, cb), cb=cb)) for _, w, cb in rows]
    in_specs += [pl.BlockSpec(f.shape, functools.partial(lambda i, nd: (0,) * nd, nd=f.ndim)) for f in fulls]
    in_specs += [pl.BlockSpec(memory_space=pl.ANY)] * (after is not None)
    out_specs = [pl.BlockSpec((tm, w), lambda i: (i, 0)) for w, _ in row_outs]
    out_specs += [pl.BlockSpec((1, w), lambda i: (0, 0)) for w in acc_outs]
    out_shape = [jax.ShapeDtypeStruct((n_rows, w), dt) for w, dt in row_outs]
    out_shape += [jax.ShapeDtypeStruct((1, w), F32) for w in acc_outs]
    return pl.pallas_call(
        body, name=name, grid=(n_rows // tm,), in_specs=in_specs, out_specs=out_specs, out_shape=out_shape,
        compiler_params=_params("arbitrary" if n_a else "parallel"),
    )(*[r[0] for r in rows], *fulls, *([after] if after is not None else []))


def _colsum(v):
    return jnp.sum(v, axis=0, keepdims=True)


def _ln_fwd(a, r, g, b, *, alpha, name):
    n_rows, d = a.shape

    def fn(*t):
        xin = t[0] if alpha == 1.0 else alpha * t[0]
        if r is not None:
            xin = xin + t[1]
        gv, bv = t[-2], t[-1]
        mu = jnp.mean(xin, axis=-1, keepdims=True)
        xc = xin - mu
        var = jnp.mean(xc * xc, axis=-1, keepdims=True)
        rstd = lax.rsqrt(var + LN_EPS)
        xh = xc * rstd
        return xh * gv + bv, xh, rstd

    rows = [(a, d, 0)] + ([(r, d, 0)] if r is not None else [])
    return _rowcall(fn, rows, [g, b], [(d, F32), (d, F32), (1, F32)], n_rows=n_rows, tm=256, name=name)


def _ln_bwd(dya, dyb, xh, rstd, g, *, alpha, name):
    n_rows, d = xh.shape

    def fn(*t):
        if dya is not None:
            dy = alpha * t[0] + t[1]
            xhv, rs, gv = t[2], t[3], t[4]
        else:
            dy, xhv, rs, gv = t[0], t[1], t[2], t[3]
        dyg = dy * gv
        m1 = jnp.mean(dyg, axis=-1, keepdims=True)
        m2 = jnp.mean(dyg * xhv, axis=-1, keepdims=True)
        dx = rs * (dyg - m1 - xhv * m2)
        return dx, _colsum(dy * xhv), _colsum(dy), _colsum(dx)

    rows = ([(dya, d, 0)] if dya is not None else []) + [(dyb, d, 0), (xh, d, 0), (rstd, 1, 0)]
    return _rowcall(fn, rows, [g], [(d, F32)], [d, d, d], n_rows=n_rows, tm=256, name=name)


def _loss_head(y, target, *, name):
    n_rows, d = y.shape

    def fn(yv, tv):
        diff = yv - tv
        part = jnp.sum(jnp.sum(diff * diff, axis=1, keepdims=True), axis=0, keepdims=True) * (0.5 / d)
        return diff * (1.0 / d), jnp.broadcast_to(part, (1, LANES))

    return _rowcall(fn, [(y, d, 0), (target, d, 0)], [], [(d, F32)], [LANES], n_rows=n_rows, tm=256, name=name)


def _rope_lane_constants():
    lane = np.arange(ATT_MERGED)
    in_head = lane % ATT_HEAD_DIM
    sign = np.where(in_head < ROT_DIM // 2, -1.0, np.where(in_head < ROT_DIM, 1.0, 0.0)).astype(np.float32)
    inv_freq = ROPE_THETA ** (-jnp.arange(0, ROT_DIM, 2, dtype=F32) / ROT_DIM)
    return inv_freq[lane % (ROT_DIM // 2)].reshape(1, ATT_MERGED), jnp.asarray(sign).reshape(1, ATT_MERGED)


def _rope_tables(pos_col, *, name, after=None):
    inv_lane, sign = _rope_lane_constants()

    def fn(pos, inv, sg):
        ang = pos.astype(F32) * inv
        return jnp.where(sg != 0.0, jnp.cos(ang), 1.0), sg * jnp.sin(ang)

    return _rowcall(fn, [(pos_col, 1, 0)], [inv_lane, sign], [(ATT_MERGED, F32), (ATT_MERGED, F32)],
                    n_rows=pos_col.shape[0], tm=512, name=name, after=after)


def _rot_partner(t):
    lane = lax.broadcasted_iota(jnp.int32, t.shape, 1)
    width = t.shape[1]
    return jnp.where((lane & (ROT_DIM // 2)) == 0, pltpu.roll(t, width - ROT_DIM // 2, 1), pltpu.roll(t, ROT_DIM // 2, 1))


def _rope(t, cos_t, sin_t):
    return t * cos_t + _rot_partner(t) * sin_t


def _rope_transpose(dt, cos_t, sin_t):
    return dt * cos_t + _rot_partner(dt * sin_t)


def _strided_rows(r, count, stride):
    return pl.ds(r, count) if stride == 1 else pl.ds(r, count, stride=stride)


def _qkv_split(proj, cos_t, sin_t, *, name, tm=512):
    n_rows = proj.shape[0]
    n_g = len(DILATIONS)

    def body(*refs):
        n_src = LANE_HALVES * 3 * n_g
        src, tables, dst = refs[:n_src], refs[n_src:n_src + 2 * LANE_HALVES], refs[n_src + 2 * LANE_HALVES:]
        for kind in range(3):
            for g, dil in enumerate(DILATIONS):
                for half in range(LANE_HALVES):
                    x_ref, o_ref = src[(kind * n_g + g) * LANE_HALVES + half], dst[kind * n_g + g]
                    cos_ref, sin_ref = tables[half], tables[LANE_HALVES + half]
                    for r in range(dil):
                        rows = _strided_rows(r, tm // dil, dil)
                        t = x_ref[rows, :]
                        if kind < 2:
                            t = _rope(t, cos_ref[rows, :], sin_ref[rows, :])
                        lo = r * ATT_MERGED + half * LANES
                        o_ref[:, lo:lo + LANES] = t.astype(o_ref.dtype)

    half_spec = lambda cb: pl.BlockSpec((tm, LANES), functools.partial(lambda i, cb: (i, cb), cb=cb))
    in_specs = [half_spec((off + g) * LANE_HALVES + half)
                for off in (OFF_Q_BLK, OFF_K_BLK, OFF_V_BLK) for g in range(n_g) for half in range(LANE_HALVES)]
    in_specs += [half_spec(half) for _ in range(2) for half in range(LANE_HALVES)]
    out_specs = [pl.BlockSpec((tm // dil, dil * ATT_MERGED), lambda i: (i, 0)) for _ in range(3) for dil in DILATIONS]
    out_shape = [jax.ShapeDtypeStruct((n_rows // dil, dil * ATT_MERGED), MXU_DTYPE) for _ in range(3) for dil in DILATIONS]
    outs = pl.pallas_call(
        body, name=name, grid=(n_rows // tm,), in_specs=in_specs, out_specs=out_specs, out_shape=out_shape,
        compiler_params=_params("parallel"),
    )(*[proj] * (LANE_HALVES * 3 * n_g), *[cos_t] * LANE_HALVES, *[sin_t] * LANE_HALVES)
    return outs[:n_g], outs[n_g:2 * n_g], outs[2 * n_g:]


def _mix(gs, ga, z1, z2, b_att):
    return jax.nn.sigmoid(gs) * (z1 * jax.nn.sigmoid(z2)) + jax.nn.sigmoid(ga) * b_att


def _mix_rows(proj, z, b_att):
    return [(proj, D_MODEL, OFF_GS_BLK), (proj, D_MODEL, OFF_GA_BLK), (z, D_MODEL, 0), (z, D_MODEL, 1), (b_att, D_MODEL, 0)]


def _mix_fwd(proj, z, b_att, *, name):
    return _rowcall(_mix, _mix_rows(proj, z, b_att), [], [(D_MODEL, MXU_DTYPE)],
                    n_rows=proj.shape[0], tm=256, name=name)[0]


def _mix_bwd(dmixed, proj, z, b_att, *, name):
    def fn(dm, gs, ga, z1, z2, ba):
        _, vjp = jax.vjp(_mix, gs, ga, z1, z2, ba)
        dgs, dga, dz1, dz2, dba = vjp(dm)
        dz = jnp.concatenate([dz1, dz2], axis=1)
        return dgs, dga, dz, dba, _colsum(dgs), _colsum(dga), _colsum(dz)

    rows = [(dmixed, D_MODEL, 0)] + _mix_rows(proj, z, b_att)
    widths = [D_MODEL, D_MODEL, 2 * D_MODEL, D_MODEL]
    return _rowcall(fn, rows, [], [(w, MXU_DTYPE) for w in widths], widths[:3], n_rows=proj.shape[0], tm=256, name=name)


def _gelu_bwd(dgy, y, proj, *, name):
    def fn(dg, yv, u):
        _, vjp = jax.vjp(jax.nn.gelu, yv)
        dy = vjp(dg)[0]
        return dy, _colsum(dy * u)

    return _rowcall(fn, [(dgy, SSM_WIDTH, 0), (y, SSM_WIDTH, 0), (proj, SSM_WIDTH, 0)], [], [(SSM_WIDTH, F32)],
                    [SSM_WIDTH], n_rows=y.shape[0], tm=512, name=name)


HEAD_ROWS = ATT_HEADS_PER_GROUP * ATT_BLK


def _head_masks(rows):
    head = lax.broadcasted_iota(jnp.int32, (rows, ATT_MERGED), 1) >> (ATT_HEAD_DIM.bit_length() - 1)
    return [head == h for h in range(ATT_HEADS_PER_GROUP)]


def _stack_heads(t, masks):
    return jnp.concatenate([jnp.where(m, t, jnp.zeros_like(t)) for m in masks], axis=0)


def _unstack_heads(t4, masks):
    blocks = [t4[h * ATT_BLK:(h + 1) * ATT_BLK] for h in range(ATT_HEADS_PER_GROUP)]
    return jnp.where(masks[0], blocks[0], jnp.where(masks[1], blocks[1], jnp.where(masks[2], blocks[2], blocks[3])))


def _head_column(stats, first):
    return jnp.concatenate([stats[:, first + h:first + h + 1] for h in range(ATT_HEADS_PER_GROUP)], axis=0)


def _band_mask(first_key):
    qi = lax.broadcasted_iota(jnp.int32, (HEAD_ROWS, 2 * ATT_BLK), 0) & (ATT_BLK - 1)
    ki = lax.broadcasted_iota(jnp.int32, (HEAD_ROWS, 2 * ATT_BLK), 1)
    steps = qi + ATT_BLK - ki
    return (steps >= 0) & (steps <= ATT_BLK) & (ki >= first_key)


def _dil_fwd(q, k, v, dil, *, name):
    n_blk = q.shape[0] // ATT_BLK
    cur = pl.BlockSpec((ATT_BLK, ATT_MERGED), lambda r, n: (n, r))
    prev = pl.BlockSpec((ATT_BLK, ATT_MERGED), lambda r, n: (jnp.maximum(n - 1, 0), r))

    def body(q_ref, kp_ref, kc_ref, vp_ref, vc_ref, o_ref, l_ref):
        masks = _head_masks(ATT_BLK)
        valid = _band_mask(jnp.where(pl.program_id(1) > 0, 0, ATT_BLK))
        keys = jnp.concatenate([kp_ref[...], kc_ref[...]], axis=0)
        vals = jnp.concatenate([vp_ref[...], vc_ref[...]], axis=0)
        s = jnp.where(valid, _dot(_stack_heads(q_ref[...], masks), keys, 1, 1) * ATT_SCALE, NEG_INF)
        m = jnp.max(s, axis=-1, keepdims=True)
        p = jnp.exp(s - m)
        den = jnp.sum(p, axis=-1, keepdims=True)
        o_ref[...] = _unstack_heads(_dot(p, vals, 1, 0) / den, masks)
        l_ref[...] = _unstack_heads(jnp.broadcast_to(m + jnp.log(den), (HEAD_ROWS, ATT_MERGED)), masks)

    shape = jax.ShapeDtypeStruct(q.shape, F32)
    return pl.pallas_call(
        body, name=name, grid=(dil, n_blk), in_specs=[cur, prev, cur, prev, cur], out_specs=[cur, cur],
        out_shape=[shape, shape], compiler_params=_params("parallel", "parallel"),
    )(q, k, k, v, v)


def _att_merge(outs, lses, *, name, tm=512):
    n_g = len(outs)
    n_rows = outs[0].shape[0] * DILATIONS[0]

    def body(*refs):
        src, (att_ref, lse_ref), tmp = refs[:2 * n_g], refs[2 * n_g:2 * n_g + 2], refs[2 * n_g + 2:]
        vals = []
        for idx, src_ref in enumerate(src):
            dil = DILATIONS[idx % n_g]
            if dil == 1:
                vals.append(src_ref[...])
                continue
            for r in range(dil):
                for half in range(LANE_HALVES):
                    lo = r * ATT_MERGED + half * LANES
                    tmp[LANE_HALVES * idx + half][_strided_rows(r, tm // dil, dil), :] = src_ref[:, lo:lo + LANES]
            vals.append(jnp.concatenate([tmp[LANE_HALVES * idx + half][...] for half in range(LANE_HALVES)], axis=1))
        o, l = vals[:n_g], vals[n_g:]
        m = functools.reduce(jnp.maximum, l)
        e = [jnp.exp(li - m) for li in l]
        z = functools.reduce(jnp.add, e)
        att_ref[...] = functools.reduce(jnp.add, [(ei / z) * oi for ei, oi in zip(e, o)])
        lse_ref[...] = m + jnp.log(z)

    in_specs = [pl.BlockSpec((tm // dil, dil * ATT_MERGED), lambda i: (i, 0)) for _ in range(2) for dil in DILATIONS]
    row = pl.BlockSpec((tm, ATT_MERGED), lambda i: (i, 0))
    shape = jax.ShapeDtypeStruct((n_rows, ATT_MERGED), F32)
    return pl.pallas_call(
        body, name=name, grid=(n_rows // tm,), in_specs=in_specs, out_specs=[row, row], out_shape=[shape, shape],
        scratch_shapes=[pltpu.VMEM((tm, LANES), F32)] * (LANE_HALVES * 2 * n_g), compiler_params=_params("parallel"),
    )(*outs, *lses)


def _att_stats(datt, att, lse, *, name):
    n_rows = datt.shape[0]

    def fn(d, a, l):
        prod = d * a
        lane = lax.broadcasted_iota(jnp.int32, (d.shape[0], LANES), 1)
        out = jnp.zeros((d.shape[0], LANES), F32)
        for h in range(ATT_HEADS_PER_GROUP):
            lo = h * ATT_HEAD_DIM
            out = jnp.where(lane == h, l[:, lo:lo + 1], out)
            delta = jnp.sum(prod[:, lo:lo + ATT_HEAD_DIM], axis=-1, keepdims=True)
            out = jnp.where(lane == ATT_HEADS_PER_GROUP + h, delta, out)
        return out

    rows = [(t, ATT_MERGED, 0) for t in (datt, att, lse)]
    return _rowcall(fn, rows, [], [(LANES, F32)], n_rows=n_rows, tm=512, name=name)[0]


def _dil_bwd(q, k, v, datt, stats, dil, *, name):
    n_rows = datt.shape[0]
    n_blk = n_rows // dil // ATT_BLK
    span = ATT_BLK * dil
    cur = pl.BlockSpec((ATT_BLK, ATT_MERGED), lambda n, r: (n, r))
    prev = pl.BlockSpec((ATT_BLK, ATT_MERGED), lambda n, r: (jnp.maximum(n - 1, 0), r))
    nxt = pl.BlockSpec((ATT_BLK, ATT_MERGED), lambda n, r: (jnp.minimum(n + 1, n_blk - 1), r))
    seq = lambda half, ahead: pl.BlockSpec((span, LANES), lambda n, r: (jnp.minimum(n + ahead, n_blk - 1), half))

    def body(qc_ref, qn_ref, kp_ref, kc_ref, vp_ref, vc_ref, dc0_ref, dc1_ref, dn0_ref, dn1_ref, sc_ref, sn_ref,
             dq0_ref, dq1_ref, dk0_ref, dk1_ref, dv0_ref, dv1_ref):
        n = pl.program_id(0)
        rows = slice(None) if dil == 1 else _strided_rows(pl.program_id(1), ATT_BLK, dil)

        def read(ref0, ref1):
            return jnp.concatenate([ref0[rows, :], ref1[rows, :]], axis=1)

        def write(ref0, ref1, val):
            ref0[rows, :] = val[:, :LANES]
            ref1[rows, :] = val[:, LANES:]

        masks = _head_masks(ATT_BLK)
        valid = _band_mask(jnp.where(n > 0, 0, ATT_BLK))
        qi = lax.broadcasted_iota(jnp.int32, (HEAD_ROWS, ATT_BLK), 0) & (ATT_BLK - 1)
        ki = lax.broadcasted_iota(jnp.int32, (HEAD_ROWS, ATT_BLK), 1)
        valid_next = (ki - qi) >= jnp.where(n < n_blk - 1, 0, ATT_BLK)

        kc, vc = kc_ref[...], vc_ref[...]
        keys = jnp.concatenate([kp_ref[...], kc], axis=0)
        vals = jnp.concatenate([vp_ref[...], vc], axis=0)
        q4 = _stack_heads(qc_ref[...], masks)
        d4 = _stack_heads(read(dc0_ref, dc1_ref).astype(MXU_DTYPE), masks)
        st = sc_ref[rows, :]
        p = jnp.where(valid, jnp.exp(_dot(q4, keys, 1, 1) * ATT_SCALE - _head_column(st, 0)), 0.0)
        ds = p * (_dot(d4, vals, 1, 1) - _head_column(st, ATT_HEADS_PER_GROUP)) * ATT_SCALE
        write(dq0_ref, dq1_ref, _unstack_heads(_dot(ds, keys, 1, 0), masks))

        q4n = _stack_heads(qn_ref[...], masks)
        d4n = _stack_heads(read(dn0_ref, dn1_ref).astype(MXU_DTYPE), masks)
        stn = sn_ref[rows, :]
        p_n = jnp.where(valid_next, jnp.exp(_dot(q4n, kc, 1, 1) * ATT_SCALE - _head_column(stn, 0)), 0.0)
        ds_n = p_n * (_dot(d4n, vc, 1, 1) - _head_column(stn, ATT_HEADS_PER_GROUP)) * ATT_SCALE
        write(dv0_ref, dv1_ref, _dot(p[:, ATT_BLK:], d4, 0, 0) + _dot(p_n, d4n, 0, 0))
        write(dk0_ref, dk1_ref, _dot(ds[:, ATT_BLK:], q4, 0, 0) + _dot(ds_n, q4n, 0, 0))

    shape = jax.ShapeDtypeStruct((n_rows, LANES), F32)
    out = seq(0, 0)
    res = pl.pallas_call(
        body, name=name, grid=(n_blk, dil),
        in_specs=[cur, nxt, prev, cur, prev, cur, seq(0, 0), seq(1, 0), seq(0, 1), seq(1, 1), seq(0, 0), seq(0, 1)],
        out_specs=[out] * 6, out_shape=[shape] * 6, compiler_params=_params("parallel", "arbitrary"),
    )(q, q, k, k, v, v, datt, datt, datt, datt, stats, stats)
    return [(res[2 * i], res[2 * i + 1]) for i in range(3)]


def _dproj_assemble(du, dqkv, dgs, dga, cos_t, sin_t, *, name):
    n_g = len(DILATIONS)

    def fn(*t):
        n_half = LANE_HALVES * 3 * n_g
        du_t, halves, (dgs_t, dga_t, c, s) = t[0], t[1:1 + n_half], t[1 + n_half:]
        parts = [jnp.concatenate(halves[LANE_HALVES * i:LANE_HALVES * (i + 1)], axis=1) for i in range(3 * n_g)]
        for i in range(2 * n_g):
            parts[i] = _rope_transpose(parts[i], c, s)
        cast = [p.astype(MXU_DTYPE) for p in parts]
        return [jnp.concatenate([du_t] + cast + [dgs_t, dga_t], axis=1)] + [_colsum(p) for p in parts]

    rows = [(du, SSM_WIDTH, 0)]
    rows += [(half, LANES, 0) for i in range(3) for g in range(n_g) for half in dqkv[g][i]]
    rows += [(dgs, D_MODEL, 0), (dga, D_MODEL, 0), (cos_t, ATT_MERGED, 0), (sin_t, ATT_MERGED, 0)]
    width = SSM_WIDTH + 3 * n_g * ATT_MERGED + 2 * D_MODEL
    res = _rowcall(fn, rows, [], [(width, MXU_DTYPE)], [ATT_MERGED] * (3 * n_g), n_rows=du.shape[0], tm=256, name=name)
    return res[0], res[1:]


def _xhead(h):
    return slice(h * XATT_HEAD_DIM, (h + 1) * XATT_HEAD_DIM)


def _xatt_probs(qh, kh):
    s = _dot(qh, kh, 1, 1) * XATT_SCALE
    e = jnp.exp(s - jnp.max(s, axis=-1, keepdims=True))
    return e / jnp.sum(e, axis=-1, keepdims=True)


def _xatt_fwd(q, kv, *, name, tm=512):
    n_rows = q.shape[0]
    n_mem = kv.shape[0]

    def body(q_ref, kv_ref, o_ref):
        for h in range(XATT_HEADS):
            sl = _xhead(h)
            p = _xatt_probs(q_ref[:, sl], kv_ref[:, sl])
            o_ref[:, sl] = _dot(p, kv_ref[:, D_MODEL + h * XATT_HEAD_DIM:D_MODEL + (h + 1) * XATT_HEAD_DIM], 1, 0
                                ).astype(o_ref.dtype)

    row = pl.BlockSpec((tm, D_MODEL), lambda i: (i, 0))
    return pl.pallas_call(
        body, name=name, grid=(n_rows // tm,),
        in_specs=[row, pl.BlockSpec((n_mem, 2 * D_MODEL), lambda i: (0, 0))], out_specs=row,
        out_shape=jax.ShapeDtypeStruct((n_rows, D_MODEL), MXU_DTYPE), compiler_params=_params("parallel"),
    )(q, kv)


def _xatt_bwd(q, kv, do, *, name, tm=512):
    n_rows = q.shape[0]
    n_mem = kv.shape[0]

    def body(q_ref, kv_ref, do_ref, dq_ref, dkv_ref):
        @pl.when(pl.program_id(0) == 0)
        def _():
            dkv_ref[...] = jnp.zeros_like(dkv_ref)

        for h in range(XATT_HEADS):
            sl = _xhead(h)
            vsl = slice(D_MODEL + h * XATT_HEAD_DIM, D_MODEL + (h + 1) * XATT_HEAD_DIM)
            qh, kh, doh = q_ref[:, sl], kv_ref[:, sl], do_ref[:, sl]
            p = _xatt_probs(qh, kh)
            dp = _dot(doh, kv_ref[:, vsl], 1, 1)
            ds = p * (dp - jnp.sum(dp * p, axis=-1, keepdims=True)) * XATT_SCALE
            dq_ref[:, sl] = _dot(ds, kh, 1, 0).astype(dq_ref.dtype)
            dkv_ref[:, sl] += _dot(ds, qh, 0, 0)
            dkv_ref[:, vsl] += _dot(p, doh, 0, 0)

    row = pl.BlockSpec((tm, D_MODEL), lambda i: (i, 0))
    full = pl.BlockSpec((n_mem, 2 * D_MODEL), lambda i: (0, 0))
    return pl.pallas_call(
        body, name=name, grid=(n_rows // tm,), in_specs=[row, full, row], out_specs=[row, full],
        out_shape=[jax.ShapeDtypeStruct((n_rows, D_MODEL), MXU_DTYPE), jax.ShapeDtypeStruct((n_mem, 2 * D_MODEL), F32)],
        compiler_params=_params("arbitrary"),
    )(q, kv, do)


def _disc(logdt, a_re, a_im, b_re, b_im):
    dt = jnp.exp(logdt)
    mag = jnp.exp(a_re * dt)
    ab_re = mag * jnp.cos(a_im * dt)
    ab_im = mag * jnp.sin(a_im * dt)
    den = jnp.square(a_re) + jnp.square(a_im)
    nr = ab_re - 1.0
    f_re = (nr * a_re + ab_im * a_im) / den
    f_im = (ab_im * a_re - nr * a_im) / den
    bb_re = f_re[None] * b_re - f_im[None] * b_im
    bb_im = f_re[None] * b_im + f_im[None] * b_re
    return ab_re, ab_im, bb_re, bb_im


def _disc_transpose(logdt, a_re, a_im, b_re, b_im, g_ab_re, g_ab_im, g_bb_re, g_bb_im):
    dt = jnp.exp(logdt)
    mag = jnp.exp(a_re * dt)
    th = a_im * dt
    cs, sn = jnp.cos(th), jnp.sin(th)
    ab_re, ab_im = mag * cs, mag * sn
    den = jnp.square(a_re) + jnp.square(a_im)
    nr = ab_re - 1.0
    f_re = (nr * a_re + ab_im * a_im) / den
    f_im = (ab_im * a_re - nr * a_im) / den
    d_f_re = jnp.sum(g_bb_re * b_re + g_bb_im * b_im, axis=0)
    d_f_im = jnp.sum(g_bb_im * b_re - g_bb_re * b_im, axis=0)
    d_b_re = g_bb_re * f_re[None] + g_bb_im * f_im[None]
    d_b_im = g_bb_im * f_re[None] - g_bb_re * f_im[None]
    d_n_re, d_n_im = d_f_re / den, d_f_im / den
    d_den = -(d_f_re * f_re + d_f_im * f_im) / den
    d_ab_re = g_ab_re + d_n_re * a_re - d_n_im * a_im
    d_ab_im = g_ab_im + d_n_re * a_im + d_n_im * a_re
    d_a_re = d_n_re * nr + d_n_im * ab_im + 2.0 * d_den * a_re
    d_a_im = d_n_re * ab_im - d_n_im * nr + 2.0 * d_den * a_im
    d_mag = d_ab_re * cs + d_ab_im * sn
    d_th = mag * (d_ab_im * cs - d_ab_re * sn)
    d_a_re = d_a_re + d_mag * mag * dt
    d_a_im = d_a_im + d_th * dt
    d_dt = jnp.sum(d_mag * mag * a_re + d_th * a_im, axis=-1, keepdims=True)
    return d_dt * dt, d_a_re, d_a_im, d_b_re, d_b_im


def _whole(fn, args, out_shapes, *, name):
    n_in = len(args)

    def body(*refs):
        res = fn(*[r[...] for r in refs[:n_in]])
        for o_ref, val in zip(refs[n_in:], res):
            o_ref[...] = val

    return pl.pallas_call(body, name=name, out_shape=[jax.ShapeDtypeStruct(s, F32) for s in out_shapes],
                          compiler_params=pltpu.CompilerParams(vmem_limit_bytes=VMEM_LIMIT_BYTES))(*args)


def _tiles_cn(t):
    t = t.reshape(SSM_TILES, GROUPS_PER_TILE, SSM_GROUP, SSM_STATE)
    eye = jnp.eye(GROUPS_PER_TILE, dtype=t.dtype)
    return (t[:, :, :, None, :] * eye[None, :, None, :, None]).reshape(SSM_TILES, LANES, GROUPS_PER_TILE * SSM_STATE)


def _tiles_nc(t):
    t = t.reshape(SSM_TILES, GROUPS_PER_TILE, SSM_STATE, SSM_GROUP)
    eye = jnp.eye(GROUPS_PER_TILE, dtype=t.dtype)
    return (t[:, :, :, None, :] * eye[None, :, None, :, None]).reshape(SSM_TILES, GROUPS_PER_TILE * SSM_STATE, LANES)


def _untile_cn(t):
    t = t.reshape(SSM_TILES, GROUPS_PER_TILE, SSM_GROUP, GROUPS_PER_TILE, SSM_STATE)
    eye = jnp.eye(GROUPS_PER_TILE, dtype=t.dtype)
    return jnp.sum(t * eye[None, :, None, :, None], axis=3).reshape(SSM_GROUPS, SSM_GROUP, SSM_STATE)


SSM_WIDE = GROUPS_PER_TILE * SSM_STATE
LANE_GROUPS_PER_TILE = SSM_WIDE // LANES


def _chan(j):
    return slice(j * LANES, (j + 1) * LANES)


def _time_major_rows(j, q, tc):
    return pl.ds(j * LANE_GROUPS_PER_TILE + q, tc, stride=STATE_VREG_ROWS)


def _to_time_major(x, t_re_ref, t_im_ref, dst_re, dst_im, tc):
    for j in range(SSM_TILES):
        xj = x[:, _chan(j)]
        for t_ref, dst in ((t_re_ref, dst_re), (t_im_ref, dst_im)):
            r = _dot(xj, t_ref[j], 1, 0)
            for q in range(LANE_GROUPS_PER_TILE):
                dst[_time_major_rows(j, q, tc), :] = r[:, q * LANES:(q + 1) * LANES]


def _from_time_major(src, j, tc):
    return jnp.concatenate([src[_time_major_rows(j, q, tc), :] for q in range(LANE_GROUPS_PER_TILE)], axis=1)


def _scan_chunk(w_re, w_im, h_re, h_im, a_re, a_im, start, tc):
    def step(t, carry):
        hr, hi = carry
        rows = _scan_rows(t)
        nr = a_re * hr - a_im * hi + w_re[rows, :]
        ni = a_re * hi + a_im * hr + w_im[rows, :]
        h_re[rows, :] = nr
        h_im[rows, :] = ni
        return nr, ni

    return lax.fori_loop(0, tc, step, start, unroll=8)


SSM_CHUNK = 256


def _ssm_fwd(proj, tb_re, tb_im, tc_re, tc_im, a_re, a_im, gain, *, name, tc=SSM_CHUNK):
    n_rows = proj.shape[0]
    n_chunk = n_rows // tc

    def body(u_ref, tbr_ref, tbi_ref, tcr_ref, tci_ref, ar_ref, ai_ref, g_ref, y_ref, gy_ref, sbr_ref, sbi_ref,
             wr, wi, hr, hi, state):
        @pl.when(pl.program_id(0) == 0)
        def _():
            state[...] = jnp.zeros_like(state)

        sbr_ref[0] = state[0]
        sbi_ref[0] = state[1]
        u = u_ref[...]
        _to_time_major(u, tbr_ref, tbi_ref, wr, wi, tc)
        state[0], state[1] = _scan_chunk(wr, wi, hr, hi, ar_ref[...], ai_ref[...], (state[0], state[1]), tc)
        for j in range(SSM_TILES):
            yj = (_dot(_from_time_major(hr, j, tc), tcr_ref[j], 1, 0) + _dot(_from_time_major(hi, j, tc), tci_ref[j], 1, 0)
                  + g_ref[:, _chan(j)] * u[:, _chan(j)])
            y_ref[:, _chan(j)] = yj
            gy_ref[:, _chan(j)] = jax.nn.gelu(yj).astype(gy_ref.dtype)

    rows = pl.BlockSpec((tc, SSM_WIDTH), lambda i: (i, 0))
    in_tile = pl.BlockSpec((SSM_TILES, LANES, SSM_WIDE), lambda i: (0, 0, 0))
    out_tile = pl.BlockSpec((SSM_TILES, SSM_WIDE, LANES), lambda i: (0, 0, 0))
    coef = pl.BlockSpec((STATE_VREG_ROWS, LANES), lambda i: (0, 0))
    bound = pl.BlockSpec((1, STATE_VREG_ROWS, LANES), lambda i: (i, 0, 0))
    bshape = jax.ShapeDtypeStruct((n_chunk, STATE_VREG_ROWS, LANES), F32)
    tm_scratch = pltpu.VMEM((tc * STATE_VREG_ROWS, LANES), F32)
    return pl.pallas_call(
        body, name=name, grid=(n_chunk,),
        in_specs=[rows, in_tile, in_tile, out_tile, out_tile, coef, coef, pl.BlockSpec((1, SSM_WIDTH), lambda i: (0, 0))],
        out_specs=[rows, rows, bound, bound],
        out_shape=[jax.ShapeDtypeStruct((n_rows, SSM_WIDTH), F32), jax.ShapeDtypeStruct((n_rows, SSM_WIDTH), MXU_DTYPE),
                   bshape, bshape],
        scratch_shapes=[tm_scratch] * 4 + [pltpu.VMEM((2, STATE_VREG_ROWS, LANES), F32)],
        compiler_params=_params("arbitrary"),
    )(proj, tb_re, tb_im, tc_re, tc_im, a_re, a_im, gain)


def _ssm_bwd(proj, dy, sb_re, sb_im, tb_re, tb_im, td_re, td_im, tu_re, tu_im, a_re, a_im, gain, *, name, tc=SSM_CHUNK):
    n_rows = proj.shape[0]
    n_chunk = n_rows // tc

    def body(u_ref, dy_ref, sbr_ref, sbi_ref, tbr_ref, tbi_ref, tdr_ref, tdi_ref, tur_ref, tui_ref, ar_ref, ai_ref, g_ref,
             du_ref, su_ref, dcr_ref, dci_ref, dbr_ref, dbi_ref, dar_ref, dai_ref, wr, wi, hr, hi, carry):
        @pl.when(pl.program_id(0) == 0)
        def _():
            carry[...] = jnp.zeros_like(carry)
            for acc_ref in (su_ref, dcr_ref, dci_ref, dbr_ref, dbi_ref):
                acc_ref[...] = jnp.zeros_like(acc_ref)

        a_r, a_i = ar_ref[...], ai_ref[...]
        u, dyv = u_ref[...], dy_ref[...]
        _to_time_major(u, tbr_ref, tbi_ref, wr, wi, tc)
        _scan_chunk(wr, wi, hr, hi, a_r, a_i, (sbr_ref[0], sbi_ref[0]), tc)
        _to_time_major(dyv, tdr_ref, tdi_ref, wr, wi, tc)

        def step(kk, c):
            lam_r, lam_i, dar, dai = c
            rows = _scan_rows(tc - 1 - kk)
            h_r, h_i = hr[rows, :], hi[rows, :]
            dar = dar + lam_r * h_r + lam_i * h_i
            dai = dai + lam_i * h_r - lam_r * h_i
            new_r = wr[rows, :] + a_r * lam_r + a_i * lam_i
            new_i = wi[rows, :] + a_r * lam_i - a_i * lam_r
            wr[rows, :] = new_r
            wi[rows, :] = new_i
            return new_r, new_i, dar, dai

        carry[0], carry[1], carry[2], carry[3] = lax.fori_loop(0, tc, step, (carry[0], carry[1], carry[2], carry[3]),
                                                              unroll=8)
        dar_ref[...] = carry[2]
        dai_ref[...] = carry[3]
        for j in range(SSM_TILES):
            cj = _chan(j)
            lam_r, lam_i = _from_time_major(wr, j, tc), _from_time_major(wi, j, tc)
            dcr_ref[j] += _dot(dyv[:, cj], _from_time_major(hr, j, tc), 0, 0)
            dci_ref[j] += _dot(dyv[:, cj], _from_time_major(hi, j, tc), 0, 0)
            dbr_ref[j] += _dot(u[:, cj], lam_r, 0, 0)
            dbi_ref[j] += _dot(u[:, cj], lam_i, 0, 0)
            duj = _dot(lam_r, tur_ref[j], 1, 0) + _dot(lam_i, tui_ref[j], 1, 0) + g_ref[:, cj] * dyv[:, cj]
            du_ref[:, cj] = duj.astype(du_ref.dtype)
            su_ref[:, cj] += _colsum(duj)

    back = lambda i: (n_chunk - 1 - i, 0)
    rows = pl.BlockSpec((tc, SSM_WIDTH), back)
    in_tile = pl.BlockSpec((SSM_TILES, LANES, SSM_WIDE), lambda i: (0, 0, 0))
    out_tile = pl.BlockSpec((SSM_TILES, SSM_WIDE, LANES), lambda i: (0, 0, 0))
    coef = pl.BlockSpec((STATE_VREG_ROWS, LANES), lambda i: (0, 0))
    bound = pl.BlockSpec((1, STATE_VREG_ROWS, LANES), lambda i: (n_chunk - 1 - i, 0, 0))
    vec = pl.BlockSpec((1, SSM_WIDTH), lambda i: (0, 0))
    tshape = jax.ShapeDtypeStruct((SSM_TILES, LANES, SSM_WIDE), F32)
    cshape = jax.ShapeDtypeStruct((STATE_VREG_ROWS, LANES), F32)
    tm_scratch = pltpu.VMEM((tc * STATE_VREG_ROWS, LANES), F32)
    return pl.pallas_call(
        body, name=name, grid=(n_chunk,),
        in_specs=[rows, rows, bound, bound, in_tile, in_tile, in_tile, in_tile, out_tile, out_tile, coef, coef, vec],
        out_specs=[rows, vec, in_tile, in_tile, in_tile, in_tile, coef, coef],
        out_shape=[jax.ShapeDtypeStruct((n_rows, SSM_WIDTH), MXU_DTYPE), jax.ShapeDtypeStruct((1, SSM_WIDTH), F32),
                   tshape, tshape, tshape, tshape, cshape, cshape],
        scratch_shapes=[tm_scratch] * 4 + [pltpu.VMEM((4, STATE_VREG_ROWS, LANES), F32)],
        compiler_params=_params("arbitrary"),
    )(proj, dy, sb_re, sb_im, tb_re, tb_im, td_re, td_im, tu_re, tu_im, a_re, a_im, gain)


def _scan_rows(t):
    return pl.ds(pl.multiple_of(t * STATE_VREG_ROWS, 8), STATE_VREG_ROWS)


GATHER_GROUPS = (("w_in",), ("w_glu", "w_att_up", "w_mix_out"), ("w_xq", "w_xkv", "w_xo", "w_ff1", "w_ff2"))
SCATTER_GROUPS = (("w_ff2", "w_ff1"), ("w_xo", "w_xq", "w_xkv", "w_mix_out"), ("w_att_up", "w_glu", "w_in"))


def _local_grads(x, mem, pos_col, target, sm, fetch, send, start_token):
    b_re_t = sm["ssm_b_re"].transpose(2, 0, 1)
    b_im_t = sm["ssm_b_im"].transpose(2, 0, 1)
    logdt = sm["ssm_log_dt"].reshape(SSM_GROUPS, 1)
    c_re, c_im = sm["ssm_c_re"], sm["ssm_c_im"]
    grp = (SSM_GROUPS, SSM_STATE)
    chn = (SSM_GROUP, SSM_GROUPS, SSM_STATE)

    wts = {}
    cos_t, sin_t = _rope_tables(pos_col, after=start_token, name="rope_tables")
    h0, xh0, rs0 = _ln_fwd(x, None, sm["ln_in_g"], sm["ln_in_b"], alpha=1.0, name="ln_in_fwd")
    disc_in = (logdt, sm["ssm_a_re"], sm["ssm_a_im"], b_re_t, b_im_t)
    ab_re, ab_im, bb_re_t, bb_im_t = _whole(_disc, disc_in, [grp, grp, chn, chn], name="ssm_disc")
    a_re_rows, a_im_rows = ab_re.reshape(STATE_VREG_ROWS, LANES), ab_im.reshape(STATE_VREG_ROWS, LANES)
    wts.update(fetch(0, h0))
    proj = _mm(h0, wts["w_in"], bias=sm["b_in"], b_shards=True, name="in_proj")

    mxu = lambda t: t.astype(MXU_DTYPE)
    tb_re, tb_im = mxu(_tiles_cn(bb_re_t.transpose(1, 0, 2))), mxu(_tiles_cn(bb_im_t.transpose(1, 0, 2)))
    y, gy, sb_re, sb_im = _ssm_fwd(proj, tb_re, tb_im, mxu(_tiles_nc(c_re.transpose(0, 2, 1))),
                                   mxu(_tiles_nc(-c_im.transpose(0, 2, 1))), a_re_rows, a_im_rows, sm["ssm_d"],
                                   name="ssm_fwd")

    q, k, v = _qkv_split(proj, cos_t, sin_t, name="qkv_split")
    outs, lses = [], []
    for g, dil in enumerate(DILATIONS):
        o_g, l_g = _dil_fwd(q[g], k[g], v[g], dil, name=f"dil_att_fwd_{dil}")
        outs.append(o_g)
        lses.append(l_g)
    att, lse = _att_merge(outs, lses, name="att_merge")
    wts.update(fetch(1, att))
    z = _mm(gy, wts["w_glu"], bias=sm["b_glu"], b_shards=True, name="glu_proj")
    b_att = _mm(att, wts["w_att_up"], b_shards=True, name="att_up")

    mixed = _mix_fwd(proj, z, b_att, name="gate_mix")
    mix_out = _mm(mixed, wts["w_mix_out"], bias=sm["b_mix_out"], name="mix_out")
    h1, xh1, rs1 = _ln_fwd(h0, mix_out, sm["ln1_g"], sm["ln1_b"], alpha=DEEPNORM_ALPHA, name="ln1_fwd")

    wts.update(fetch(2, h1))
    xq = _mm(h1, wts["w_xq"], out_dtype=MXU_DTYPE, name="xatt_q")
    kv = _mm(mem, wts["w_xkv"], out_dtype=MXU_DTYPE, b_shards=True, name="xatt_kv")
    xo_in = _xatt_fwd(xq, kv, name="xatt_fwd")
    xo = _mm(xo_in, wts["w_xo"], name="xatt_o")
    h2, xh2, rs2 = _ln_fwd(h1, xo, sm["ln2_g"], sm["ln2_b"], alpha=DEEPNORM_ALPHA, name="ln2_fwd")

    pre, act = _mm(h2, wts["w_ff1"], bias=sm["b_ff1"], b_shards=True, name="ff1",
                   also=(lambda r: jnp.square(jnp.maximum(r, 0.0)), MXU_DTYPE))
    ff = _mm(act, wts["w_ff2"], bias=sm["b_ff2"], name="ff2")
    h3, xh3, rs3 = _ln_fwd(h2, ff, sm["ln3_g"], sm["ln3_b"], alpha=DEEPNORM_ALPHA, name="ln3_fwd")
    dh3, loss_row = _loss_head(h3, target, name="loss_head")

    gw, gs = {}, {}
    dr3, gs["ln3_g"], gs["ln3_b"], gs["b_ff2"] = _ln_bwd(None, dh3, xh3, rs3, sm["ln3_g"], alpha=1.0, name="ln3_bwd")
    wgrad = functools.partial(_mm, ta=True, out_dtype=WIRE_DTYPE)
    gw["w_ff2"] = wgrad(act, dr3, name="ff2_dw")
    dpre, gs["b_ff1"] = _mm(dr3, wts["w_ff2"], tb=True, out_dtype=MXU_DTYPE, colsum=True, name="ff2_dx",
                            gate=(pre, lambda p: 2.0 * jnp.maximum(p, 0.0)))
    gw["w_ff1"] = wgrad(h2, dpre, out_shards=True, name="ff1_dw")
    sent = send(0, gw)
    dh2 = _mm(dpre, wts["w_ff1"], tb=True, b_shards=True, after=sent, name="ff1_dx")

    dr2, gs["ln2_g"], gs["ln2_b"], _ = _ln_bwd(dr3, dh2, xh2, rs2, sm["ln2_g"], alpha=DEEPNORM_ALPHA, name="ln2_bwd")
    gw["w_xo"] = wgrad(xo_in, dr2, name="xatt_o_dw")
    dxo_in = _mm(dr2, wts["w_xo"], tb=True, out_dtype=MXU_DTYPE, name="xatt_o_dx")
    dxq, dkv = _xatt_bwd(xq, kv, dxo_in, name="xatt_bwd")
    gw["w_xq"] = wgrad(h1, dxq, name="xatt_q_dw")
    gw["w_xkv"] = wgrad(mem, dkv, out_shards=True, name="xatt_kv_dw")
    dh1 = _mm(dxq, wts["w_xq"], tb=True, name="xatt_q_dx")

    dr1, gs["ln1_g"], gs["ln1_b"], gs["b_mix_out"] = _ln_bwd(dr2, dh1, xh1, rs1, sm["ln1_g"], alpha=DEEPNORM_ALPHA,
                                                             name="ln1_bwd")
    gw["w_mix_out"] = wgrad(mixed, dr1, name="mix_out_dw")
    sent = send(1, gw)
    dmixed = _mm(dr1, wts["w_mix_out"], tb=True, after=sent, name="mix_out_dx")
    dgs, dga, dz, db_att, s_gs, s_ga, gs["b_glu"] = _mix_bwd(dmixed, proj, z, b_att, name="gate_mix_bwd")

    gw["w_att_up"] = wgrad(att, db_att, out_shards=True, name="att_up_dw")
    datt = _mm(db_att, wts["w_att_up"], tb=True, b_shards=True, name="att_up_dx")
    stats = _att_stats(datt, att, lse, name="att_stats")
    dqkv = [_dil_bwd(q[g], k[g], v[g], datt, stats, dil, name=f"dil_att_bwd_{dil}") for g, dil in enumerate(DILATIONS)]

    gw["w_glu"] = wgrad(gy, dz, out_shards=True, name="glu_dw")
    dgy = _mm(dz, wts["w_glu"], tb=True, b_shards=True, name="glu_dx")
    dy, gs["ssm_d"] = _gelu_bwd(dgy, y, proj, name="gelu_bwd")
    du, s_u, dc_re_t, dc_im_t, dbb_re_t, dbb_im_t, da_re, da_im = _ssm_bwd(
        proj, dy, sb_re, sb_im, tb_re, tb_im, mxu(_tiles_cn(c_re)), mxu(_tiles_cn(-c_im)),
        mxu(_tiles_nc(bb_re_t.transpose(1, 2, 0))), mxu(_tiles_nc(bb_im_t.transpose(1, 2, 0))),
        a_re_rows, a_im_rows, sm["ssm_d"], name="ssm_bwd")
    gs["ssm_c_re"], gs["ssm_c_im"] = _untile_cn(dc_re_t), -_untile_cn(dc_im_t)
    disc_ct = (da_re.reshape(grp), da_im.reshape(grp), _untile_cn(dbb_re_t).transpose(1, 0, 2),
               _untile_cn(dbb_im_t).transpose(1, 0, 2))
    d_logdt, gs["ssm_a_re"], gs["ssm_a_im"], d_b_re_t, d_b_im_t = _whole(
        _disc_transpose, disc_in + disc_ct, [(SSM_GROUPS, 1), grp, grp, chn, chn], name="ssm_disc_bwd")
    gs["ssm_log_dt"] = d_logdt
    gs["ssm_b_re"], gs["ssm_b_im"] = d_b_re_t.transpose(1, 2, 0), d_b_im_t.transpose(1, 2, 0)

    dproj, s_qkv = _dproj_assemble(du, dqkv, dgs, dga, cos_t, sin_t, name="dproj_assemble")
    gs["b_in"] = jnp.concatenate([s_u, *s_qkv, s_gs, s_ga], axis=1)
    gw["w_in"] = wgrad(h0, dproj, out_shards=True, name="in_proj_dw")
    sent = send(2, gw)
    dh0 = _mm(dproj, wts["w_in"], tb=True, b_shards=True, after=sent, name="in_proj_dx")
    grad_x, gs["ln_in_g"], gs["ln_in_b"], _ = _ln_bwd(dr1, dh0, xh0, rs0, sm["ln_in_g"], alpha=DEEPNORM_ALPHA,
                                                      name="ln_in_bwd")
    return loss_row, grad_x, gs


N_PEER = N_DEV - 1
_IN_HBM = pl.BlockSpec(memory_space=pltpu.HBM)
_IN_SEMAPHORE = pl.BlockSpec(memory_space=pltpu.SEMAPHORE)


def _device_index():
    return 4 * lax.axis_index("x") + 2 * lax.axis_index("y") + lax.axis_index("c")


def _exchange_copies(src_refs, land_refs, send_sems, recv_sems, scatter):
    x, y, c = lax.axis_index("x"), lax.axis_index("y"), lax.axis_index("c")
    me = 4 * x + 2 * y + c
    pairs = []
    for a, (src_ref, land_ref) in enumerate(zip(src_refs, land_refs)):
        for kk in range(1, N_DEV):
            px = (x + (kk >> 2)) % 2
            py = (y + ((kk >> 1) & 1)) % 2
            pc = (c + (kk & 1)) % 2
            peer = 4 * px + 2 * py + pc
            sem = a * N_PEER + kk - 1
            src = src_ref.at[peer] if scatter else src_ref

            def copy(dst, src=src, sem=sem, px=px, py=py, pc=pc):
                return pltpu.make_async_remote_copy(
                    src_ref=src, dst_ref=dst, send_sem=send_sems.at[sem], recv_sem=recv_sems.at[sem],
                    device_id=(px, py, pc), device_id_type=pl.DeviceIdType.MESH)

            pairs.append((functools.partial(copy, land_ref.at[me]), functools.partial(copy, land_ref.at[peer])))
    return pairs


def _exchange_start(srcs, *, scatter, name, after=None):
    n_arr = len(srcs)
    lands = [lax.empty((N_DEV,) + tuple(s.shape[1:] if scatter else s.shape), s.dtype) for s in srcs]
    n_in = 2 * n_arr + (after is not None)

    def body(*refs):
        send_sems, recv_sems = refs[n_in], refs[n_in + 1]
        for sent, _ in _exchange_copies(refs[:n_arr], refs[n_arr:2 * n_arr], send_sems, recv_sems, scatter):
            sent().start()
        refs[-1][...] = jnp.zeros_like(refs[-1])

    through = [pltpu.HBM(t.shape, t.dtype) for t in (*srcs, *lands)]
    res = pl.pallas_call(
        body, name=name,
        out_shape=(pltpu.SemaphoreType.DMA((n_arr * N_PEER,)), pltpu.SemaphoreType.DMA((n_arr * N_PEER,)), *through,
                   jax.ShapeDtypeStruct((8, LANES), F32)),
        in_specs=[_IN_HBM] * (2 * n_arr) + [pl.BlockSpec(memory_space=pl.ANY)] * (after is not None),
        out_specs=(_IN_SEMAPHORE, _IN_SEMAPHORE, *[_IN_HBM] * (2 * n_arr), pl.BlockSpec(memory_space=pltpu.VMEM)),
        input_output_aliases={i: 2 + i for i in range(2 * n_arr)},
        compiler_params=pltpu.CompilerParams(has_side_effects=pltpu.SideEffectType.DATAFLOW_SIDE_EFFECTING),
    )(*[pltpu.with_memory_space_constraint(t, pltpu.HBM) for t in (*srcs, *lands)],
      *([after] if after is not None else []))
    return (res[0], res[1], res[2:2 + n_arr], res[2 + n_arr:2 + 2 * n_arr], scatter), res[-1]


def _exchange_wait(handle, *, after, name):
    send_sems, recv_sems, srcs, lands, scatter = handle
    n_arr = len(srcs)

    def body(*refs):
        for sent, received in _exchange_copies(refs[:n_arr], refs[n_arr:2 * n_arr], refs[2 * n_arr], refs[2 * n_arr + 1],
                                               scatter):
            sent().wait_send()
            received().wait_recv()

    res = pl.pallas_call(
        body, name=name, out_shape=tuple(pltpu.HBM(t.shape, t.dtype) for t in (*srcs, *lands)),
        in_specs=[_IN_HBM] * (2 * n_arr) + [_IN_SEMAPHORE, _IN_SEMAPHORE, pl.BlockSpec(memory_space=pl.ANY)],
        out_specs=tuple([_IN_HBM] * (2 * n_arr)), input_output_aliases={i: i for i in range(2 * n_arr)},
        compiler_params=pltpu.CompilerParams(has_side_effects=pltpu.SideEffectType.DATAFLOW_SIDE_EFFECTING),
    )(*srcs, *lands, send_sems, recv_sems, after)
    return res[n_arr:]


def _with_own_slot(land, own):
    return lax.dynamic_update_slice_in_dim(land, own[None], _device_index(), axis=0)


def _reduce_adamw(gstack, w, m, v, *, name, tr=128):
    n_rows, cols = w.shape
    tr = min(tr, n_rows)
    assert n_rows % tr == 0, (name, n_rows, tr)

    def body(g_ref, w_ref, m_ref, v_ref, go_ref, d_ref, mo_ref, vo_ref):
        g = g_ref[0].astype(F32)
        for dev in range(1, N_DEV):
            g = g + g_ref[dev].astype(F32)
        m_new = ADAM_B1 * m_ref[...] + (1.0 - ADAM_B1) * g
        v_new = ADAM_B2 * v_ref[...] + (1.0 - ADAM_B2) * jnp.square(g)
        m_hat = m_new / (1.0 - ADAM_B1 ** ADAM_STEP)
        v_hat = v_new / (1.0 - ADAM_B2 ** ADAM_STEP)
        go_ref[...] = g
        d_ref[...] = -ADAM_LR * (m_hat / (jnp.sqrt(v_hat) + ADAM_EPS) + ADAM_WD * w_ref[...])
        mo_ref[...] = m_new
        vo_ref[...] = v_new

    flat = pl.BlockSpec((tr, cols), lambda i: (i, 0))
    shape = jax.ShapeDtypeStruct((n_rows, cols), F32)
    return pl.pallas_call(
        body, name=name, grid=(n_rows // tr,),
        in_specs=[pl.BlockSpec((N_DEV, tr, cols), lambda i: (0, i, 0)), flat, flat, flat],
        out_specs=[flat] * 4, out_shape=[shape] * 4, compiler_params=_params("parallel"),
    )(gstack, w, m, v)


def _pack(parts, dtype):
    flat = jnp.concatenate([p.reshape(-1).astype(dtype) for p in parts])
    unit = PACK_COLS * PACK_ROW_ALIGN
    total = -(-flat.shape[0] // unit) * unit
    return jnp.pad(flat, (0, total - flat.shape[0])).reshape(-1, PACK_COLS)


def _unpack(packed, shapes):
    flat = packed.reshape(-1)
    out, off = [], 0
    for s in shapes:
        size = int(np.prod(s))
        out.append(flat[off:off + size].reshape(s))
        off += size
    return out


def kernel(x, mem, positions, ln_in_g, ln_in_b, w_in, b_in, ssm_log_dt, ssm_a_re, ssm_a_im, ssm_b_re, ssm_b_im, ssm_c_re, ssm_c_im, ssm_d, w_glu, b_glu, w_att_up, w_mix_out, b_mix_out, ln1_g, ln1_b, w_xq, w_xkv, w_xo, ln2_g, ln2_b, w_ff1, b_ff1, w_ff2, b_ff2, ln3_g, ln3_b, loss_target, m_ln_in_g, m_ln_in_b, m_w_in, m_b_in, m_ssm_log_dt, m_ssm_a_re, m_ssm_a_im, m_ssm_b_re, m_ssm_b_im, m_ssm_c_re, m_ssm_c_im, m_ssm_d, m_w_glu, m_b_glu, m_w_att_up, m_w_mix_out, m_b_mix_out, m_ln1_g, m_ln1_b, m_w_xq, m_w_xkv, m_w_xo, m_ln2_g, m_ln2_b, m_w_ff1, m_b_ff1, m_w_ff2, m_b_ff2, m_ln3_g, m_ln3_b, v_ln_in_g, v_ln_in_b, v_w_in, v_b_in, v_ssm_log_dt, v_ssm_a_re, v_ssm_a_im, v_ssm_b_re, v_ssm_b_im, v_ssm_c_re, v_ssm_c_im, v_ssm_d, v_w_glu, v_b_glu, v_w_att_up, v_w_mix_out, v_b_mix_out, v_ln1_g, v_ln1_b, v_w_xq, v_w_xkv, v_w_xo, v_ln2_g, v_ln2_b, v_w_ff1, v_b_ff1, v_w_ff2, v_b_ff2, v_ln3_g, v_ln3_b):
    given = dict(locals())
    w_arg = {n: given[n] for n in WEIGHTS}
    m_arg = {n: given["m_" + n] for n in WEIGHTS}
    v_arg = {n: given["v_" + n] for n in WEIGHTS}

    shards = {n: w_arg[n][0].astype(MXU_DTYPE) for n in BIG}
    gathers, token = [], None
    for i, names in enumerate(GATHER_GROUPS):
        handle, token = _exchange_start([shards[n] for n in names], scatter=False, after=token, name=f"gather_start_{i}")
        gathers.append(handle)

    def fetch(i, after):
        lands = _exchange_wait(gathers[i], after=after, name=f"gather_wait_{i}")
        full = {n: _with_own_slot(land, shards[n]) for n, land in zip(GATHER_GROUPS[i], lands)}
        return {n: t if n in BIG_COL_SHARDED else t.reshape(-1, t.shape[-1]) for n, t in full.items()}

    scatters = {}

    def send(i, gw):
        slots = [gw[n] if n in BIG_COL_SHARDED else gw[n].reshape(N_DEV, -1, gw[n].shape[-1]) for n in SCATTER_GROUPS[i]]
        handle, sent = _exchange_start(slots, scatter=True, name=f"scatter_start_{i}")
        scatters[i] = (handle, slots)
        return sent

    sm = {}
    for n in SMALL:
        t = w_arg[n]
        if n.startswith("ssm_") and n not in ("ssm_d", "ssm_log_dt"):
            sm[n] = t[0]
        else:
            sm[n] = t.reshape(1, -1)

    loss_row, grad_x, gs = _local_grads(x[0], mem[0], positions.reshape(-1, 1), loss_target[0], sm, fetch, send, token)
    loss = lax.psum(loss_row[0, 0], ("x", "y", "c"))
    small = _pack([gs[n] for n in SMALL], WIRE_DTYPE)
    small_handle, _ = _exchange_start([small], scatter=False, name="small_start")

    results = [{}, {}, {}, {}]
    done = grad_x
    for i, names in enumerate(SCATTER_GROUPS):
        handle, slots = scatters[i]
        lands = _exchange_wait(handle, after=done, name=f"scatter_wait_{i}")
        for n, land, slot in zip(names, lands, slots):
            own = lax.dynamic_index_in_dim(slot, _device_index(), axis=0, keepdims=False)
            res = _reduce_adamw(_with_own_slot(land, own), w_arg[n][0], m_arg[n][0], v_arg[n][0], name="adamw_" + n)
            done = res[0]
            for d, r in zip(results, res):
                d[n] = r[None]
    small_stack = _with_own_slot(_exchange_wait(small_handle, after=done, name="small_wait")[0], small)
    small_shapes = [w_arg[n].shape for n in SMALL]
    res = _reduce_adamw(small_stack, *[_pack([d[n] for n in SMALL], F32) for d in (w_arg, m_arg, v_arg)],
                        name="adamw_small")
    for d, r in zip(results, res):
        d.update(zip(SMALL, _unpack(r, small_shapes)))
    out = [loss, grad_x[None]]
    for d in results:
        out += [d[n] for n in WEIGHTS]
    return tuple(out)
```

```python
import functools

import numpy as np
import jax
import jax.numpy as jnp
from jax import lax
from jax.experimental import pallas as pl
from jax.experimental.pallas import tpu as pltpu

F32 = jnp.float32
MXU_DTYPE = jnp.bfloat16
WIRE_DTYPE = jnp.bfloat16
VMEM_LIMIT_BYTES = 48 * 1024 * 1024
LANES = 128

N_DEV = 8
D_MODEL = 1024
SSM_GROUP = 16
SSM_WIDTH = 768
SSM_GROUPS = SSM_WIDTH // SSM_GROUP
SSM_STATE = 64
SSM_CH = SSM_GROUPS * SSM_STATE
SSM_TILES = SSM_WIDTH // LANES
GROUPS_PER_TILE = LANES // SSM_GROUP
STATE_VREG_ROWS = SSM_CH // LANES
ATT_HEAD_DIM = 64
ATT_HEADS_PER_GROUP = 4
ATT_MERGED = ATT_HEADS_PER_GROUP * ATT_HEAD_DIM
LANE_HALVES = ATT_MERGED // LANES
DILATIONS = (1, 4, 16)
ATT_BLK = 128
ATT_SCALE = ATT_HEAD_DIM ** -0.5
ROT_DIM = ATT_HEAD_DIM // 4
ROPE_THETA = 500000.0
XATT_HEADS = 4
XATT_HEAD_DIM = D_MODEL // XATT_HEADS
XATT_SCALE = XATT_HEAD_DIM ** -0.5
DEEPNORM_ALPHA = 2.0 ** 0.25
LN_EPS = 1e-5
NEG_INF = -1e30
OFF_Q_BLK, OFF_K_BLK, OFF_V_BLK = 3, 6, 9
OFF_GS_BLK, OFF_GA_BLK = 3, 4

ADAM_LR = 0.001
ADAM_B1 = 0.9
ADAM_B2 = 0.999
ADAM_EPS = 1e-08
ADAM_WD = 0.01
ADAM_STEP = 10

BIG = ("w_in", "w_glu", "w_att_up", "w_mix_out", "w_xq", "w_xkv", "w_xo", "w_ff1", "w_ff2")
BIG_COL_SHARDED = ("w_in", "w_glu", "w_att_up", "w_xkv", "w_ff1")
WEIGHTS = ("ln_in_g", "ln_in_b", "w_in", "b_in", "ssm_log_dt", "ssm_a_re", "ssm_a_im", "ssm_b_re", "ssm_b_im",
           "ssm_c_re", "ssm_c_im", "ssm_d", "w_glu", "b_glu", "w_att_up", "w_mix_out", "b_mix_out", "ln1_g", "ln1_b",
           "w_xq", "w_xkv", "w_xo", "ln2_g", "ln2_b", "w_ff1", "b_ff1", "w_ff2", "b_ff2", "ln3_g", "ln3_b")
SMALL = tuple(n for n in WEIGHTS if n not in BIG)
PACK_COLS = 1024
PACK_ROW_ALIGN = 256


def _params(*sem):
    return pltpu.CompilerParams(dimension_semantics=sem, vmem_limit_bytes=VMEM_LIMIT_BYTES)


def _dot(a, b, ca, cb):
    return lax.dot_general(a.astype(MXU_DTYPE), b.astype(MXU_DTYPE), (((ca,), (cb,)), ((), ())),
                           preferred_element_type=F32)


def _fit(dim, pref):
    if dim <= pref:
        return dim
    best = max(t for t in range(LANES, pref + 1, LANES) if dim % t == 0)
    return best


def _mm(a, b, *, name, ta=False, tb=False, bias=None, out_dtype=F32, b_shards=False, out_shards=False, after=None,
        also=None, gate=None, colsum=False, tm=1024, tn=1024, tk=1024):
    m, k = (a.shape[1], a.shape[0]) if ta else a.shape
    order = (lambda f: (lambda j, i, kk: f(i, j, kk))) if colsum else (lambda f: f)
    spec = lambda shape, f: pl.BlockSpec(shape, order(f))
    if b_shards:
        n_sh, rows, n_loc = b.shape
        if tb:
            n, tn, tk = rows, _fit(rows, tn), n_loc
            assert k == n_sh * n_loc, (name, k, b.shape)
            b_spec = spec((1, tn, tk), lambda i, j, kk: (kk, j, 0))
        else:
            n, tn, tk = n_sh * n_loc, n_loc, _fit(k, tk)
            b_spec = spec((1, tk, tn), lambda i, j, kk: (j, kk, 0))
    else:
        n = b.shape[0] if tb else b.shape[1]
        tn = n // N_DEV if out_shards else _fit(n, tn)
        tk = _fit(k, tk)
        b_spec = spec((tn, tk), lambda i, j, kk: (j, kk)) if tb else spec((tk, tn), lambda i, j, kk: (kk, j))
    tm = _fit(m, tm)
    nk = k // tk
    a_spec = spec((tk, tm), lambda i, j, kk: (kk, i)) if ta else spec((tm, tk), lambda i, j, kk: (i, kk))
    tile = spec((tm, tn), lambda i, j, kk: (i, j))
    in_specs, args = [a_spec, b_spec], [a, b]
    if bias is not None:
        in_specs.append(spec((1, tn), lambda i, j, kk: (0, j)))
        args.append(bias)
    if gate is not None:
        in_specs.append(tile)
        args.append(gate[0])
    if after is not None:
        in_specs.append(pl.BlockSpec(memory_space=pl.ANY))
        args.append(after)
    n_in = len(args)
    if out_shards:
        assert n == N_DEV * tn, (name, n, tn)
        out_specs = [spec((1, tm, tn), lambda i, j, kk: (j, i, 0))]
        out_shape = [jax.ShapeDtypeStruct((N_DEV, m, tn), out_dtype)]
    else:
        out_specs = [tile]
        out_shape = [jax.ShapeDtypeStruct((m, n), out_dtype)]
    if also is not None:
        out_specs.append(tile)
        out_shape.append(jax.ShapeDtypeStruct((m, n), also[1]))
    if colsum:
        out_specs.append(spec((1, tn), lambda i, j, kk: (0, j)))
        out_shape.append(jax.ShapeDtypeStruct((1, n), F32))

    def body(*refs):
        a_ref, b_ref = refs[0], refs[1]
        o_ref = refs[n_in]

        def product():
            return _dot(a_ref[...], b_ref[0] if b_shards else b_ref[...], 0 if ta else 1, 1 if tb else 0)

        def finish(r):
            if bias is not None:
                r = r + refs[2][...]
            if gate is not None:
                r = r * gate[1](refs[2 + (bias is not None)][...])
            if out_shards:
                o_ref[0] = r.astype(o_ref.dtype)
            else:
                o_ref[...] = r.astype(o_ref.dtype)
            if also is not None:
                refs[n_in + 1][...] = also[0](r).astype(also[1])
            if colsum:
                s_ref = refs[n_in + 1 + (also is not None)]

                @pl.when(pl.program_id(1) == 0)
                def _():
                    s_ref[...] = jnp.zeros_like(s_ref)

                s_ref[...] += _colsum(r)

        if nk == 1:
            finish(product())
            return
        acc_ref = refs[-1]
        kk = pl.program_id(2)

        @pl.when(kk == 0)
        def _():
            acc_ref[...] = jnp.zeros_like(acc_ref)

        acc_ref[...] += product()

        @pl.when(kk == nk - 1)
        def _():
            finish(acc_ref[...])

    grid = (n // tn, m // tm, nk) if colsum else (m // tm, n // tn, nk)
    res = pl.pallas_call(
        body, name=name, grid=grid, in_specs=in_specs, out_specs=out_specs, out_shape=out_shape,
        scratch_shapes=[pltpu.VMEM((tm, tn), F32)] if nk > 1 else [],
        compiler_params=_params("parallel", "arbitrary" if colsum else "parallel", "arbitrary"),
    )(*args)
    return res[0] if len(res) == 1 else res


def _rowcall(fn, rows, fulls, row_outs, acc_outs=(), *, n_rows, tm, name, after=None):
    n_r, n_f, n_o, n_a = len(rows), len(fulls), len(row_outs), len(acc_outs)
    n_in = n_r + n_f + (after is not None)
    assert n_rows % tm == 0, (name, n_rows, tm)

    def body(*refs):
        res = fn(*[r[...] for r in refs[:n_r + n_f]])
        res = tuple(res) if isinstance(res, (tuple, list)) else (res,)
        o_refs = refs[n_in:n_in + n_o]
        a_refs = refs[n_in + n_o:]
        for o_ref, val in zip(o_refs, res[:n_o]):
            o_ref[...] = val.astype(o_ref.dtype)
        if n_a:
            @pl.when(pl.program_id(0) == 0)
            def _():
                for a_ref in a_refs:
                    a_ref[...] = jnp.zeros_like(a_ref)

            for a_ref, val in zip(a_refs, res[n_o:]):
                a_ref[...] += val

    in_specs = [pl.BlockSpec((tm, w), functools.partial(lambda i, cb: (i, cb), cb=cb)) for _, w, cb in rows]
    in_specs += [pl.BlockSpec(f.shape, functools.partial(lambda i, nd: (0,) * nd, nd=f.ndim)) for f in fulls]
    in_specs += [pl.BlockSpec(memory_space=pl.ANY)] * (after is not None)
    out_specs = [pl.BlockSpec((tm, w), lambda i: (i, 0)) for w, _ in row_outs]
    out_specs += [pl.BlockSpec((1, w), lambda i: (0, 0)) for w in acc_outs]
    out_shape = [jax.ShapeDtypeStruct((n_rows, w), dt) for w, dt in row_outs]
    out_shape += [jax.ShapeDtypeStruct((1, w), F32) for w in acc_outs]
    return pl.pallas_call(
        body, name=name, grid=(n_rows // tm,), in_specs=in_specs, out_specs=out_specs, out_shape=out_shape,
        compiler_params=_params("arbitrary" if n_a else "parallel"),
    )(*[r[0] for r in rows], *fulls, *([after] if after is not None else []))


def _colsum(v):
    return jnp.sum(v, axis=0, keepdims=True)


def _ln_fwd(a, r, g, b, *, alpha, name):
    n_rows, d = a.shape

    def fn(*t):
        xin = t[0] if alpha == 1.0 else alpha * t[0]
        if r is not None:
            xin = xin + t[1]
        gv, bv = t[-2], t[-1]
        mu = jnp.mean(xin, axis=-1, keepdims=True)
        xc = xin - mu
        var = jnp.mean(xc * xc, axis=-1, keepdims=True)
        rstd = lax.rsqrt(var + LN_EPS)
        xh = xc * rstd
        y = xh * gv + bv
        return y, xh, rstd, y

    rows = [(a, d, 0)] + ([(r, d, 0)] if r is not None else [])
    return _rowcall(fn, rows, [g, b], [(d, F32), (d, F32), (1, F32), (d, MXU_DTYPE)], n_rows=n_rows, tm=256, name=name)


def _ln_bwd(dya, dyb, xh, rstd, g, *, alpha, name):
    n_rows, d = xh.shape

    def fn(*t):
        if dya is not None:
            dy = alpha * t[0] + t[1]
            xhv, rs, gv = t[2], t[3], t[4]
        else:
            dy, xhv, rs, gv = t[0], t[1], t[2], t[3]
        dyg = dy * gv
        m1 = jnp.mean(dyg, axis=-1, keepdims=True)
        m2 = jnp.mean(dyg * xhv, axis=-1, keepdims=True)
        dx = rs * (dyg - m1 - xhv * m2)
        return dx, dx, _colsum(dy * xhv), _colsum(dy), _colsum(dx)

    rows = ([(dya, d, 0)] if dya is not None else []) + [(dyb, d, 0), (xh, d, 0), (rstd, 1, 0)]
    return _rowcall(fn, rows, [g], [(d, F32), (d, MXU_DTYPE)], [d, d, d], n_rows=n_rows, tm=256, name=name)


def _loss_head(y, target, *, name):
    n_rows, d = y.shape

    def fn(yv, tv):
        diff = yv - tv
        part = jnp.sum(jnp.sum(diff * diff, axis=1, keepdims=True), axis=0, keepdims=True) * (0.5 / d)
        return diff * (1.0 / d), jnp.broadcast_to(part, (1, LANES))

    return _rowcall(fn, [(y, d, 0), (target, d, 0)], [], [(d, F32)], [LANES], n_rows=n_rows, tm=256, name=name)


def _rope_lane_constants():
    lane = np.arange(ATT_MERGED)
    in_head = lane % ATT_HEAD_DIM
    sign = np.where(in_head < ROT_DIM // 2, -1.0, np.where(in_head < ROT_DIM, 1.0, 0.0)).astype(np.float32)
    inv_freq = ROPE_THETA ** (-jnp.arange(0, ROT_DIM, 2, dtype=F32) / ROT_DIM)
    return inv_freq[lane % (ROT_DIM // 2)].reshape(1, ATT_MERGED), jnp.asarray(sign).reshape(1, ATT_MERGED)


def _rope_tables(pos_col, *, name, after=None):
    inv_lane, sign = _rope_lane_constants()

    def fn(pos, inv, sg):
        ang = pos.astype(F32) * inv
        return jnp.where(sg != 0.0, jnp.cos(ang), 1.0), sg * jnp.sin(ang)

    return _rowcall(fn, [(pos_col, 1, 0)], [inv_lane, sign], [(ATT_MERGED, F32), (ATT_MERGED, F32)],
                    n_rows=pos_col.shape[0], tm=512, name=name, after=after)


def _rot_partner(t):
    lane = lax.broadcasted_iota(jnp.int32, t.shape, 1)
    width = t.shape[1]
    return jnp.where((lane & (ROT_DIM // 2)) == 0, pltpu.roll(t, width - ROT_DIM // 2, 1), pltpu.roll(t, ROT_DIM // 2, 1))


def _rope(t, cos_t, sin_t):
    return t * cos_t + _rot_partner(t) * sin_t


def _rope_transpose(dt, cos_t, sin_t):
    return dt * cos_t + _rot_partner(dt * sin_t)


def _strided_rows(r, count, stride):
    return pl.ds(r, count) if stride == 1 else pl.ds(r, count, stride=stride)


def _qkv_split(proj, cos_t, sin_t, *, name, tm=512):
    n_rows = proj.shape[0]
    n_g = len(DILATIONS)

    def body(*refs):
        n_src = LANE_HALVES * 3 * n_g
        src, tables, dst = refs[:n_src], refs[n_src:n_src + 2 * LANE_HALVES], refs[n_src + 2 * LANE_HALVES:]
        for kind in range(3):
            for g, dil in enumerate(DILATIONS):
                for half in range(LANE_HALVES):
                    x_ref, o_ref = src[(kind * n_g + g) * LANE_HALVES + half], dst[kind * n_g + g]
                    cos_ref, sin_ref = tables[half], tables[LANE_HALVES + half]
                    for r in range(dil):
                        rows = _strided_rows(r, tm // dil, dil)
                        t = x_ref[rows, :]
                        if kind < 2:
                            t = _rope(t, cos_ref[rows, :], sin_ref[rows, :])
                        lo = r * ATT_MERGED + half * LANES
                        o_ref[:, lo:lo + LANES] = t.astype(o_ref.dtype)

    half_spec = lambda cb: pl.BlockSpec((tm, LANES), functools.partial(lambda i, cb: (i, cb), cb=cb))
    in_specs = [half_spec((off + g) * LANE_HALVES + half)
                for off in (OFF_Q_BLK, OFF_K_BLK, OFF_V_BLK) for g in range(n_g) for half in range(LANE_HALVES)]
    in_specs += [half_spec(half) for _ in range(2) for half in range(LANE_HALVES)]
    out_specs = [pl.BlockSpec((tm // dil, dil * ATT_MERGED), lambda i: (i, 0)) for _ in range(3) for dil in DILATIONS]
    out_shape = [jax.ShapeDtypeStruct((n_rows // dil, dil * ATT_MERGED), MXU_DTYPE) for _ in range(3) for dil in DILATIONS]
    outs = pl.pallas_call(
        body, name=name, grid=(n_rows // tm,), in_specs=in_specs, out_specs=out_specs, out_shape=out_shape,
        compiler_params=_params("parallel"),
    )(*[proj] * (LANE_HALVES * 3 * n_g), *[cos_t] * LANE_HALVES, *[sin_t] * LANE_HALVES)
    return outs[:n_g], outs[n_g:2 * n_g], outs[2 * n_g:]


def _mix(gs, ga, z1, z2, b_att):
    return jax.nn.sigmoid(gs) * (z1 * jax.nn.sigmoid(z2)) + jax.nn.sigmoid(ga) * b_att


def _mix_rows(proj, z, b_att):
    return [(proj, D_MODEL, OFF_GS_BLK), (proj, D_MODEL, OFF_GA_BLK), (z, D_MODEL, 0), (z, D_MODEL, 1), (b_att, D_MODEL, 0)]


def _mix_fwd(proj, z, b_att, *, name):
    return _rowcall(_mix, _mix_rows(proj, z, b_att), [], [(D_MODEL, MXU_DTYPE)],
                    n_rows=proj.shape[0], tm=256, name=name)[0]


def _mix_bwd(dmixed, proj, z, b_att, *, name):
    def fn(dm, gs, ga, z1, z2, ba):
        _, vjp = jax.vjp(_mix, gs, ga, z1, z2, ba)
        dgs, dga, dz1, dz2, dba = vjp(dm)
        dz = jnp.concatenate([dz1, dz2], axis=1)
        return dgs, dga, dz, dba, _colsum(dgs), _colsum(dga), _colsum(dz)

    rows = [(dmixed, D_MODEL, 0)] + _mix_rows(proj, z, b_att)
    widths = [D_MODEL, D_MODEL, 2 * D_MODEL, D_MODEL]
    return _rowcall(fn, rows, [], [(w, MXU_DTYPE) for w in widths], widths[:3], n_rows=proj.shape[0], tm=256, name=name)


def _gelu_bwd(dgy, y, proj, *, name):
    def fn(dg, yv, u):
        _, vjp = jax.vjp(jax.nn.gelu, yv)
        dy = vjp(dg)[0]
        return dy, _colsum(dy * u)

    return _rowcall(fn, [(dgy, SSM_WIDTH, 0), (y, SSM_WIDTH, 0), (proj, SSM_WIDTH, 0)], [], [(SSM_WIDTH, F32)],
                    [SSM_WIDTH], n_rows=y.shape[0], tm=512, name=name)


HEAD_ROWS = ATT_HEADS_PER_GROUP * ATT_BLK


def _head_masks(rows):
    head = lax.broadcasted_iota(jnp.int32, (rows, ATT_MERGED), 1) >> (ATT_HEAD_DIM.bit_length() - 1)
    return [head == h for h in range(ATT_HEADS_PER_GROUP)]


def _stack_heads(t, masks):
    return jnp.concatenate([jnp.where(m, t, jnp.zeros_like(t)) for m in masks], axis=0)


def _unstack_heads(t4, masks):
    blocks = [t4[h * ATT_BLK:(h + 1) * ATT_BLK] for h in range(ATT_HEADS_PER_GROUP)]
    return jnp.where(masks[0], blocks[0], jnp.where(masks[1], blocks[1], jnp.where(masks[2], blocks[2], blocks[3])))


def _head_column(stats, first):
    return jnp.concatenate([stats[:, first + h:first + h + 1] for h in range(ATT_HEADS_PER_GROUP)], axis=0)


def _band_mask(first_key):
    qi = lax.broadcasted_iota(jnp.int32, (HEAD_ROWS, 2 * ATT_BLK), 0) & (ATT_BLK - 1)
    ki = lax.broadcasted_iota(jnp.int32, (HEAD_ROWS, 2 * ATT_BLK), 1)
    steps = qi + ATT_BLK - ki
    return (steps >= 0) & (steps <= ATT_BLK) & (ki >= first_key)


def _dil_fwd(q, k, v, dil, *, name):
    n_blk = q.shape[0] // ATT_BLK
    cur = pl.BlockSpec((ATT_BLK, ATT_MERGED), lambda r, n: (n, r))
    prev = pl.BlockSpec((ATT_BLK, ATT_MERGED), lambda r, n: (jnp.maximum(n - 1, 0), r))

    def body(q_ref, kp_ref, kc_ref, vp_ref, vc_ref, o_ref, l_ref):
        masks = _head_masks(ATT_BLK)
        valid = _band_mask(jnp.where(pl.program_id(1) > 0, 0, ATT_BLK))
        keys = jnp.concatenate([kp_ref[...], kc_ref[...]], axis=0)
        vals = jnp.concatenate([vp_ref[...], vc_ref[...]], axis=0)
        s = jnp.where(valid, _dot(_stack_heads(q_ref[...], masks), keys, 1, 1) * ATT_SCALE, NEG_INF)
        m = jnp.max(s, axis=-1, keepdims=True)
        p = jnp.exp(s - m)
        den = jnp.sum(p, axis=-1, keepdims=True)
        o_ref[...] = _unstack_heads(_dot(p, vals, 1, 0) / den, masks)
        l_ref[...] = _unstack_heads(jnp.broadcast_to(m + jnp.log(den), (HEAD_ROWS, ATT_MERGED)), masks)

    shape = jax.ShapeDtypeStruct(q.shape, F32)
    return pl.pallas_call(
        body, name=name, grid=(dil, n_blk), in_specs=[cur, prev, cur, prev, cur], out_specs=[cur, cur],
        out_shape=[shape, shape], compiler_params=_params("parallel", "parallel"),
    )(q, k, k, v, v)


def _att_merge(outs, lses, *, name, tm=512):
    n_g = len(outs)
    n_rows = outs[0].shape[0] * DILATIONS[0]

    def body(*refs):
        src, (att_ref, lse_ref), tmp = refs[:2 * n_g], refs[2 * n_g:2 * n_g + 2], refs[2 * n_g + 2:]
        vals = []
        for idx, src_ref in enumerate(src):
            dil = DILATIONS[idx % n_g]
            if dil == 1:
                vals.append(src_ref[...])
                continue
            for r in range(dil):
                for half in range(LANE_HALVES):
                    lo = r * ATT_MERGED + half * LANES
                    tmp[LANE_HALVES * idx + half][_strided_rows(r, tm // dil, dil), :] = src_ref[:, lo:lo + LANES]
            vals.append(jnp.concatenate([tmp[LANE_HALVES * idx + half][...] for half in range(LANE_HALVES)], axis=1))
        o, l = vals[:n_g], vals[n_g:]
        m = functools.reduce(jnp.maximum, l)
        e = [jnp.exp(li - m) for li in l]
        z = functools.reduce(jnp.add, e)
        att_ref[...] = functools.reduce(jnp.add, [(ei / z) * oi for ei, oi in zip(e, o)])
        lse_ref[...] = m + jnp.log(z)

    in_specs = [pl.BlockSpec((tm // dil, dil * ATT_MERGED), lambda i: (i, 0)) for _ in range(2) for dil in DILATIONS]
    row = pl.BlockSpec((tm, ATT_MERGED), lambda i: (i, 0))
    shape = jax.ShapeDtypeStruct((n_rows, ATT_MERGED), F32)
    return pl.pallas_call(
        body, name=name, grid=(n_rows // tm,), in_specs=in_specs, out_specs=[row, row], out_shape=[shape, shape],
        scratch_shapes=[pltpu.VMEM((tm, LANES), F32)] * (LANE_HALVES * 2 * n_g), compiler_params=_params("parallel"),
    )(*outs, *lses)


def _att_stats(datt, att, lse, *, name):
    n_rows = datt.shape[0]

    def fn(d, a, l):
        prod = d * a
        lane = lax.broadcasted_iota(jnp.int32, (d.shape[0], LANES), 1)
        out = jnp.zeros((d.shape[0], LANES), F32)
        for h in range(ATT_HEADS_PER_GROUP):
            lo = h * ATT_HEAD_DIM
            out = jnp.where(lane == h, l[:, lo:lo + 1], out)
            delta = jnp.sum(prod[:, lo:lo + ATT_HEAD_DIM], axis=-1, keepdims=True)
            out = jnp.where(lane == ATT_HEADS_PER_GROUP + h, delta, out)
        return out

    rows = [(t, ATT_MERGED, 0) for t in (datt, att, lse)]
    return _rowcall(fn, rows, [], [(LANES, F32)], n_rows=n_rows, tm=512, name=name)[0]


def _dil_bwd(q, k, v, datt, stats, dil, *, name):
    n_rows = datt.shape[0]
    n_blk = n_rows // dil // ATT_BLK
    span = ATT_BLK * dil
    cur = pl.BlockSpec((ATT_BLK, ATT_MERGED), lambda n, r: (n, r))
    prev = pl.BlockSpec((ATT_BLK, ATT_MERGED), lambda n, r: (jnp.maximum(n - 1, 0), r))
    nxt = pl.BlockSpec((ATT_BLK, ATT_MERGED), lambda n, r: (jnp.minimum(n + 1, n_blk - 1), r))
    seq = lambda half, ahead: pl.BlockSpec((span, LANES), lambda n, r: (jnp.minimum(n + ahead, n_blk - 1), half))

    def body(qc_ref, qn_ref, kp_ref, kc_ref, vp_ref, vc_ref, dc0_ref, dc1_ref, dn0_ref, dn1_ref, sc_ref, sn_ref,
             dq0_ref, dq1_ref, dk0_ref, dk1_ref, dv0_ref, dv1_ref):
        n = pl.program_id(0)
        rows = slice(None) if dil == 1 else _strided_rows(pl.program_id(1), ATT_BLK, dil)

        def read(ref0, ref1):
            return jnp.concatenate([ref0[rows, :], ref1[rows, :]], axis=1)

        def write(ref0, ref1, val):
            ref0[rows, :] = val[:, :LANES]
            ref1[rows, :] = val[:, LANES:]

        masks = _head_masks(ATT_BLK)
        valid = _band_mask(jnp.where(n > 0, 0, ATT_BLK))
        qi = lax.broadcasted_iota(jnp.int32, (HEAD_ROWS, ATT_BLK), 0) & (ATT_BLK - 1)
        ki = lax.broadcasted_iota(jnp.int32, (HEAD_ROWS, ATT_BLK), 1)
        valid_next = (ki - qi) >= jnp.where(n < n_blk - 1, 0, ATT_BLK)

        kc, vc = kc_ref[...], vc_ref[...]
        keys = jnp.concatenate([kp_ref[...], kc], axis=0)
        vals = jnp.concatenate([vp_ref[...], vc], axis=0)
        q4 = _stack_heads(qc_ref[...], masks)
        d4 = _stack_heads(read(dc0_ref, dc1_ref).astype(MXU_DTYPE), masks)
        st = sc_ref[rows, :]
        p = jnp.where(valid, jnp.exp(_dot(q4, keys, 1, 1) * ATT_SCALE - _head_column(st, 0)), 0.0)
        ds = p * (_dot(d4, vals, 1, 1) - _head_column(st, ATT_HEADS_PER_GROUP)) * ATT_SCALE
        write(dq0_ref, dq1_ref, _unstack_heads(_dot(ds, keys, 1, 0), masks))

        q4n = _stack_heads(qn_ref[...], masks)
        d4n = _stack_heads(read(dn0_ref, dn1_ref).astype(MXU_DTYPE), masks)
        stn = sn_ref[rows, :]
        p_n = jnp.where(valid_next, jnp.exp(_dot(q4n, kc, 1, 1) * ATT_SCALE - _head_column(stn, 0)), 0.0)
        ds_n = p_n * (_dot(d4n, vc, 1, 1) - _head_column(stn, ATT_HEADS_PER_GROUP)) * ATT_SCALE
        write(dv0_ref, dv1_ref, _dot(p[:, ATT_BLK:], d4, 0, 0) + _dot(p_n, d4n, 0, 0))
        write(dk0_ref, dk1_ref, _dot(ds[:, ATT_BLK:], q4, 0, 0) + _dot(ds_n, q4n, 0, 0))

    shape = jax.ShapeDtypeStruct((n_rows, LANES), F32)
    out = seq(0, 0)
    res = pl.pallas_call(
        body, name=name, grid=(n_blk, dil),
        in_specs=[cur, nxt, prev, cur, prev, cur, seq(0, 0), seq(1, 0), seq(0, 1), seq(1, 1), seq(0, 0), seq(0, 1)],
        out_specs=[out] * 6, out_shape=[shape] * 6, compiler_params=_params("parallel", "arbitrary"),
    )(q, q, k, k, v, v, datt, datt, datt, datt, stats, stats)
    return [(res[2 * i], res[2 * i + 1]) for i in range(3)]


def _dproj_assemble(du, dqkv, dgs, dga, cos_t, sin_t, *, name):
    n_g = len(DILATIONS)

    def fn(*t):
        n_half = LANE_HALVES * 3 * n_g
        du_t, halves, (dgs_t, dga_t, c, s) = t[0], t[1:1 + n_half], t[1 + n_half:]
        parts = [jnp.concatenate(halves[LANE_HALVES * i:LANE_HALVES * (i + 1)], axis=1) for i in range(3 * n_g)]
        for i in range(2 * n_g):
            parts[i] = _rope_transpose(parts[i], c, s)
        cast = [p.astype(MXU_DTYPE) for p in parts]
        return [jnp.concatenate([du_t] + cast + [dgs_t, dga_t], axis=1)] + [_colsum(p) for p in parts]

    rows = [(du, SSM_WIDTH, 0)]
    rows += [(half, LANES, 0) for i in range(3) for g in range(n_g) for half in dqkv[g][i]]
    rows += [(dgs, D_MODEL, 0), (dga, D_MODEL, 0), (cos_t, ATT_MERGED, 0), (sin_t, ATT_MERGED, 0)]
    width = SSM_WIDTH + 3 * n_g * ATT_MERGED + 2 * D_MODEL
    res = _rowcall(fn, rows, [], [(width, MXU_DTYPE)], [ATT_MERGED] * (3 * n_g), n_rows=du.shape[0], tm=256, name=name)
    return res[0], res[1:]


def _xhead(h):
    return slice(h * XATT_HEAD_DIM, (h + 1) * XATT_HEAD_DIM)


def _xatt_probs(qh, kh):
    s = _dot(qh, kh, 1, 1) * XATT_SCALE
    e = jnp.exp(s - jnp.max(s, axis=-1, keepdims=True))
    return e / jnp.sum(e, axis=-1, keepdims=True)


def _xatt_fwd(q, kv, *, name, tm=512):
    n_rows = q.shape[0]
    n_mem = kv.shape[0]

    def body(q_ref, kv_ref, o_ref):
        for h in range(XATT_HEADS):
            sl = _xhead(h)
            p = _xatt_probs(q_ref[:, sl], kv_ref[:, sl])
            o_ref[:, sl] = _dot(p, kv_ref[:, D_MODEL + h * XATT_HEAD_DIM:D_MODEL + (h + 1) * XATT_HEAD_DIM], 1, 0
                                ).astype(o_ref.dtype)

    row = pl.BlockSpec((tm, D_MODEL), lambda i: (i, 0))
    return pl.pallas_call(
        body, name=name, grid=(n_rows // tm,),
        in_specs=[row, pl.BlockSpec((n_mem, 2 * D_MODEL), lambda i: (0, 0))], out_specs=row,
        out_shape=jax.ShapeDtypeStruct((n_rows, D_MODEL), MXU_DTYPE), compiler_params=_params("parallel"),
    )(q, kv)


def _xatt_bwd(q, kv, do, *, name, tm=512):
    n_rows = q.shape[0]
    n_mem = kv.shape[0]

    def body(q_ref, kv_ref, do_ref, dq_ref, dkv_ref):
        @pl.when(pl.program_id(0) == 0)
        def _():
            dkv_ref[...] = jnp.zeros_like(dkv_ref)

        for h in range(XATT_HEADS):
            sl = _xhead(h)
            vsl = slice(D_MODEL + h * XATT_HEAD_DIM, D_MODEL + (h + 1) * XATT_HEAD_DIM)
            qh, kh, doh = q_ref[:, sl], kv_ref[:, sl], do_ref[:, sl]
            p = _xatt_probs(qh, kh)
            dp = _dot(doh, kv_ref[:, vsl], 1, 1)
            ds = p * (dp - jnp.sum(dp * p, axis=-1, keepdims=True)) * XATT_SCALE
            dq_ref[:, sl] = _dot(ds, kh, 1, 0).astype(dq_ref.dtype)
            dkv_ref[:, sl] += _dot(ds, qh, 0, 0)
            dkv_ref[:, vsl] += _dot(p, doh, 0, 0)

    row = pl.BlockSpec((tm, D_MODEL), lambda i: (i, 0))
    full = pl.BlockSpec((n_mem, 2 * D_MODEL), lambda i: (0, 0))
    return pl.pallas_call(
        body, name=name, grid=(n_rows // tm,), in_specs=[row, full, row], out_specs=[row, full],
        out_shape=[jax.ShapeDtypeStruct((n_rows, D_MODEL), MXU_DTYPE), jax.ShapeDtypeStruct((n_mem, 2 * D_MODEL), F32)],
        compiler_params=_params("arbitrary"),
    )(q, kv, do)


def _disc(logdt, a_re, a_im, b_re, b_im):
    dt = jnp.exp(logdt)
    mag = jnp.exp(a_re * dt)
    ab_re = mag * jnp.cos(a_im * dt)
    ab_im = mag * jnp.sin(a_im * dt)
    den = jnp.square(a_re) + jnp.square(a_im)
    nr = ab_re - 1.0
    f_re = (nr * a_re + ab_im * a_im) / den
    f_im = (ab_im * a_re - nr * a_im) / den
    bb_re = f_re[None] * b_re - f_im[None] * b_im
    bb_im = f_re[None] * b_im + f_im[None] * b_re
    return ab_re, ab_im, bb_re, bb_im


def _disc_transpose(logdt, a_re, a_im, b_re, b_im, g_ab_re, g_ab_im, g_bb_re, g_bb_im):
    dt = jnp.exp(logdt)
    mag = jnp.exp(a_re * dt)
    th = a_im * dt
    cs, sn = jnp.cos(th), jnp.sin(th)
    ab_re, ab_im = mag * cs, mag * sn
    den = jnp.square(a_re) + jnp.square(a_im)
    nr = ab_re - 1.0
    f_re = (nr * a_re + ab_im * a_im) / den
    f_im = (ab_im * a_re - nr * a_im) / den
    d_f_re = jnp.sum(g_bb_re * b_re + g_bb_im * b_im, axis=0)
    d_f_im = jnp.sum(g_bb_im * b_re - g_bb_re * b_im, axis=0)
    d_b_re = g_bb_re * f_re[None] + g_bb_im * f_im[None]
    d_b_im = g_bb_im * f_re[None] - g_bb_re * f_im[None]
    d_n_re, d_n_im = d_f_re / den, d_f_im / den
    d_den = -(d_f_re * f_re + d_f_im * f_im) / den
    d_ab_re = g_ab_re + d_n_re * a_re - d_n_im * a_im
    d_ab_im = g_ab_im + d_n_re * a_im + d_n_im * a_re
    d_a_re = d_n_re * nr + d_n_im * ab_im + 2.0 * d_den * a_re
    d_a_im = d_n_re * ab_im - d_n_im * nr + 2.0 * d_den * a_im
    d_mag = d_ab_re * cs + d_ab_im * sn
    d_th = mag * (d_ab_im * cs - d_ab_re * sn)
    d_a_re = d_a_re + d_mag * mag * dt
    d_a_im = d_a_im + d_th * dt
    d_dt = jnp.sum(d_mag * mag * a_re + d_th * a_im, axis=-1, keepdims=True)
    return d_dt * dt, d_a_re, d_a_im, d_b_re, d_b_im


def _whole(fn, args, out_shapes, *, name):
    n_in = len(args)

    def body(*refs):
        res = fn(*[r[...] for r in refs[:n_in]])
        for o_ref, val in zip(refs[n_in:], res):
            o_ref[...] = val

    return pl.pallas_call(body, name=name, out_shape=[jax.ShapeDtypeStruct(s, F32) for s in out_shapes],
                          compiler_params=pltpu.CompilerParams(vmem_limit_bytes=VMEM_LIMIT_BYTES))(*args)


def _tiles_cn(t):
    t = t.reshape(-1, SSM_TILES, GROUPS_PER_TILE, SSM_GROUP, SSM_STATE)
    eye = jnp.eye(GROUPS_PER_TILE, dtype=t.dtype)
    return (t[:, :, :, :, None, :] * eye[:, None, :, None]).reshape(-1, SSM_TILES, LANES, GROUPS_PER_TILE * SSM_STATE)


def _tiles_nc(t):
    t = t.reshape(-1, SSM_TILES, GROUPS_PER_TILE, SSM_STATE, SSM_GROUP)
    eye = jnp.eye(GROUPS_PER_TILE, dtype=t.dtype)
    return (t[:, :, :, :, None, :] * eye[:, None, :, None]).reshape(-1, SSM_TILES, GROUPS_PER_TILE * SSM_STATE, LANES)


def _untile_cn(t):
    t = t.reshape(SSM_TILES, GROUPS_PER_TILE, SSM_GROUP, GROUPS_PER_TILE, SSM_STATE)
    eye = jnp.eye(GROUPS_PER_TILE, dtype=t.dtype)
    return jnp.sum(t * eye[None, :, None, :, None], axis=3).reshape(SSM_GROUPS, SSM_GROUP, SSM_STATE)


SSM_WIDE = GROUPS_PER_TILE * SSM_STATE
LANE_GROUPS_PER_TILE = SSM_WIDE // LANES


def _chan(j):
    return slice(j * LANES, (j + 1) * LANES)


def _time_major_rows(j, q, tc):
    return pl.ds(j * LANE_GROUPS_PER_TILE + q, tc, stride=STATE_VREG_ROWS)


def _to_time_major(x, t_re_ref, t_im_ref, dst_re, dst_im, tc):
    for j in range(SSM_TILES):
        xj = x[:, _chan(j)]
        for t_ref, dst in ((t_re_ref, dst_re), (t_im_ref, dst_im)):
            r = _dot(xj, t_ref[j], 1, 0)
            for q in range(LANE_GROUPS_PER_TILE):
                dst[_time_major_rows(j, q, tc), :] = r[:, q * LANES:(q + 1) * LANES]


def _from_time_major(src, j, tc):
    return jnp.concatenate([src[_time_major_rows(j, q, tc), :] for q in range(LANE_GROUPS_PER_TILE)], axis=1)


def _scan_chunk(w_re, w_im, h_re, h_im, a_re, a_im, start, tc):
    def step(t, carry):
        hr, hi = carry
        rows = _scan_rows(t)
        nr = a_re * hr - a_im * hi + w_re[rows, :]
        ni = a_re * hi + a_im * hr + w_im[rows, :]
        h_re[rows, :] = nr
        h_im[rows, :] = ni
        return nr, ni

    return lax.fori_loop(0, tc, step, start, unroll=8)


SSM_CHUNK = 256


def _tile_spec(stack, k):
    return pl.BlockSpec((pl.Squeezed(),) + tuple(stack.shape[1:]), lambda i: (k, 0, 0, 0))


def _ssm_fwd(proj, tiles_cn, tiles_nc, a_re, a_im, gain, *, name, tc=SSM_CHUNK):
    n_rows = proj.shape[0]
    n_chunk = n_rows // tc

    def body(u_ref, tbr_ref, tbi_ref, tcr_ref, tci_ref, ar_ref, ai_ref, g_ref, y_ref, gy_ref, sbr_ref, sbi_ref,
             wr, wi, hr, hi, state):
        @pl.when(pl.program_id(0) == 0)
        def _():
            state[...] = jnp.zeros_like(state)

        sbr_ref[0] = state[0]
        sbi_ref[0] = state[1]
        u = u_ref[...]
        _to_time_major(u, tbr_ref, tbi_ref, wr, wi, tc)
        state[0], state[1] = _scan_chunk(wr, wi, hr, hi, ar_ref[...], ai_ref[...], (state[0], state[1]), tc)
        for j in range(SSM_TILES):
            yj = (_dot(_from_time_major(hr, j, tc), tcr_ref[j], 1, 0) + _dot(_from_time_major(hi, j, tc), tci_ref[j], 1, 0)
                  + g_ref[:, _chan(j)] * u[:, _chan(j)])
            y_ref[:, _chan(j)] = yj
            gy_ref[:, _chan(j)] = jax.nn.gelu(yj).astype(gy_ref.dtype)

    rows = pl.BlockSpec((tc, SSM_WIDTH), lambda i: (i, 0))
    coef = pl.BlockSpec((STATE_VREG_ROWS, LANES), lambda i: (0, 0))
    bound = pl.BlockSpec((1, STATE_VREG_ROWS, LANES), lambda i: (i, 0, 0))
    bshape = jax.ShapeDtypeStruct((n_chunk, STATE_VREG_ROWS, LANES), F32)
    tm_scratch = pltpu.VMEM((tc * STATE_VREG_ROWS, LANES), F32)
    return pl.pallas_call(
        body, name=name, grid=(n_chunk,),
        in_specs=[rows, _tile_spec(tiles_cn, 0), _tile_spec(tiles_cn, 1), _tile_spec(tiles_nc, 0), _tile_spec(tiles_nc, 1),
                  coef, coef, pl.BlockSpec((1, SSM_WIDTH), lambda i: (0, 0))],
        out_specs=[rows, rows, bound, bound],
        out_shape=[jax.ShapeDtypeStruct((n_rows, SSM_WIDTH), F32), jax.ShapeDtypeStruct((n_rows, SSM_WIDTH), MXU_DTYPE),
                   bshape, bshape],
        scratch_shapes=[tm_scratch] * 4 + [pltpu.VMEM((2, STATE_VREG_ROWS, LANES), F32)],
        compiler_params=_params("arbitrary"),
    )(proj, tiles_cn, tiles_cn, tiles_nc, tiles_nc, a_re, a_im, gain)


def _ssm_bwd(proj, dy, sb_re, sb_im, tiles_cn, tiles_nc, a_re, a_im, gain, *, name, tc=SSM_CHUNK):
    n_rows = proj.shape[0]
    n_chunk = n_rows // tc

    def body(u_ref, dy_ref, sbr_ref, sbi_ref, tbr_ref, tbi_ref, tdr_ref, tdi_ref, tur_ref, tui_ref, ar_ref, ai_ref, g_ref,
             du_ref, su_ref, dcr_ref, dci_ref, dbr_ref, dbi_ref, dar_ref, dai_ref, wr, wi, hr, hi, carry):
        @pl.when(pl.program_id(0) == 0)
        def _():
            carry[...] = jnp.zeros_like(carry)
            for acc_ref in (su_ref, dcr_ref, dci_ref, dbr_ref, dbi_ref):
                acc_ref[...] = jnp.zeros_like(acc_ref)

        a_r, a_i = ar_ref[...], ai_ref[...]
        u, dyv = u_ref[...], dy_ref[...]
        _to_time_major(u, tbr_ref, tbi_ref, wr, wi, tc)
        _scan_chunk(wr, wi, hr, hi, a_r, a_i, (sbr_ref[0], sbi_ref[0]), tc)
        _to_time_major(dyv, tdr_ref, tdi_ref, wr, wi, tc)

        def step(kk, c):
            lam_r, lam_i, dar, dai = c
            rows = _scan_rows(tc - 1 - kk)
            h_r, h_i = hr[rows, :], hi[rows, :]
            dar = dar + lam_r * h_r + lam_i * h_i
            dai = dai + lam_i * h_r - lam_r * h_i
            new_r = wr[rows, :] + a_r * lam_r + a_i * lam_i
            new_i = wi[rows, :] + a_r * lam_i - a_i * lam_r
            wr[rows, :] = new_r
            wi[rows, :] = new_i
            return new_r, new_i, dar, dai

        carry[0], carry[1], carry[2], carry[3] = lax.fori_loop(0, tc, step, (carry[0], carry[1], carry[2], carry[3]),
                                                              unroll=8)
        dar_ref[...] = carry[2]
        dai_ref[...] = carry[3]
        for j in range(SSM_TILES):
            cj = _chan(j)
            lam_r, lam_i = _from_time_major(wr, j, tc), _from_time_major(wi, j, tc)
            dcr_ref[j] += _dot(dyv[:, cj], _from_time_major(hr, j, tc), 0, 0)
            dci_ref[j] += _dot(dyv[:, cj], _from_time_major(hi, j, tc), 0, 0)
            dbr_ref[j] += _dot(u[:, cj], lam_r, 0, 0)
            dbi_ref[j] += _dot(u[:, cj], lam_i, 0, 0)
            duj = _dot(lam_r, tur_ref[j], 1, 0) + _dot(lam_i, tui_ref[j], 1, 0) + g_ref[:, cj] * dyv[:, cj]
            du_ref[:, cj] = duj.astype(du_ref.dtype)
            su_ref[:, cj] += _colsum(duj)

    back = lambda i: (n_chunk - 1 - i, 0)
    rows = pl.BlockSpec((tc, SSM_WIDTH), back)
    in_tile = pl.BlockSpec((SSM_TILES, LANES, SSM_WIDE), lambda i: (0, 0, 0))
    coef = pl.BlockSpec((STATE_VREG_ROWS, LANES), lambda i: (0, 0))
    bound = pl.BlockSpec((1, STATE_VREG_ROWS, LANES), lambda i: (n_chunk - 1 - i, 0, 0))
    vec = pl.BlockSpec((1, SSM_WIDTH), lambda i: (0, 0))
    tshape = jax.ShapeDtypeStruct((SSM_TILES, LANES, SSM_WIDE), F32)
    cshape = jax.ShapeDtypeStruct((STATE_VREG_ROWS, LANES), F32)
    tm_scratch = pltpu.VMEM((tc * STATE_VREG_ROWS, LANES), F32)
    return pl.pallas_call(
        body, name=name, grid=(n_chunk,),
        in_specs=[rows, rows, bound, bound, *[_tile_spec(tiles_cn, k) for k in range(4)], _tile_spec(tiles_nc, 2),
                  _tile_spec(tiles_nc, 3), coef, coef, vec],
        out_specs=[rows, vec, in_tile, in_tile, in_tile, in_tile, coef, coef],
        out_shape=[jax.ShapeDtypeStruct((n_rows, SSM_WIDTH), MXU_DTYPE), jax.ShapeDtypeStruct((1, SSM_WIDTH), F32),
                   tshape, tshape, tshape, tshape, cshape, cshape],
        scratch_shapes=[tm_scratch] * 4 + [pltpu.VMEM((4, STATE_VREG_ROWS, LANES), F32)],
        compiler_params=_params("arbitrary"),
    )(proj, dy, sb_re, sb_im, *[tiles_cn] * 4, tiles_nc, tiles_nc, a_re, a_im, gain)


def _scan_rows(t):
    return pl.ds(pl.multiple_of(t * STATE_VREG_ROWS, 8), STATE_VREG_ROWS)


GATHER_GROUPS = (("w_in",), ("w_glu", "w_att_up", "w_mix_out"), ("w_xq", "w_xkv", "w_xo", "w_ff1", "w_ff2"))
SCATTER_GROUPS = (("w_ff2", "w_ff1"), ("w_xo", "w_xq", "w_xkv", "w_mix_out"), ("w_att_up", "w_glu", "w_in"))


def _local_grads(x, mem, pos_col, target, sm, fetch, send, start_token):
    b_re_t = sm["ssm_b_re"].transpose(2, 0, 1)
    b_im_t = sm["ssm_b_im"].transpose(2, 0, 1)
    logdt = sm["ssm_log_dt"].reshape(SSM_GROUPS, 1)
    c_re, c_im = sm["ssm_c_re"], sm["ssm_c_im"]
    grp = (SSM_GROUPS, SSM_STATE)
    chn = (SSM_GROUP, SSM_GROUPS, SSM_STATE)

    wts = {}
    cos_t, sin_t = _rope_tables(pos_col, after=start_token, name="rope_tables")
    h0, xh0, rs0, h0m = _ln_fwd(x, None, sm["ln_in_g"], sm["ln_in_b"], alpha=1.0, name="ln_in_fwd")
    disc_in = (logdt, sm["ssm_a_re"], sm["ssm_a_im"], b_re_t, b_im_t)
    ab_re, ab_im, bb_re_t, bb_im_t = _whole(_disc, disc_in, [grp, grp, chn, chn], name="ssm_disc")
    a_re_rows, a_im_rows = ab_re.reshape(STATE_VREG_ROWS, LANES), ab_im.reshape(STATE_VREG_ROWS, LANES)
    tiles_cn = _tiles_cn(jnp.stack([bb_re_t.transpose(1, 0, 2), bb_im_t.transpose(1, 0, 2), c_re, -c_im])
                         ).astype(MXU_DTYPE)
    tiles_nc = _tiles_nc(jnp.stack([c_re.transpose(0, 2, 1), -c_im.transpose(0, 2, 1), bb_re_t.transpose(1, 2, 0),
                                    bb_im_t.transpose(1, 2, 0)])).astype(MXU_DTYPE)
    wts.update(fetch(0, [h0m, tiles_cn, tiles_nc]))
    proj = _mm(h0m, wts["w_in"], bias=sm["b_in"], b_shards=True, name="in_proj")

    y, gy, sb_re, sb_im = _ssm_fwd(proj, tiles_cn, tiles_nc, a_re_rows, a_im_rows, sm["ssm_d"], name="ssm_fwd")

    q, k, v = _qkv_split(proj, cos_t, sin_t, name="qkv_split")
    outs, lses = [], []
    for g, dil in enumerate(DILATIONS):
        o_g, l_g = _dil_fwd(q[g], k[g], v[g], dil, name=f"dil_att_fwd_{dil}")
        outs.append(o_g)
        lses.append(l_g)
    att, lse = _att_merge(outs, lses, name="att_merge")
    wts.update(fetch(1, [att]))
    z = _mm(gy, wts["w_glu"], bias=sm["b_glu"], b_shards=True, name="glu_proj")
    b_att = _mm(att, wts["w_att_up"], b_shards=True, name="att_up")

    mixed = _mix_fwd(proj, z, b_att, name="gate_mix")
    mix_out = _mm(mixed, wts["w_mix_out"], bias=sm["b_mix_out"], name="mix_out")
    h1, xh1, rs1, h1m = _ln_fwd(h0, mix_out, sm["ln1_g"], sm["ln1_b"], alpha=DEEPNORM_ALPHA, name="ln1_fwd")

    wts.update(fetch(2, [h1m]))
    xq = _mm(h1m, wts["w_xq"], out_dtype=MXU_DTYPE, name="xatt_q")
    kv = _mm(mem, wts["w_xkv"], out_dtype=MXU_DTYPE, b_shards=True, name="xatt_kv")
    xo_in = _xatt_fwd(xq, kv, name="xatt_fwd")
    xo = _mm(xo_in, wts["w_xo"], name="xatt_o")
    h2, xh2, rs2, h2m = _ln_fwd(h1, xo, sm["ln2_g"], sm["ln2_b"], alpha=DEEPNORM_ALPHA, name="ln2_fwd")

    pre, act = _mm(h2m, wts["w_ff1"], bias=sm["b_ff1"], b_shards=True, name="ff1",
                   also=(lambda r: jnp.square(jnp.maximum(r, 0.0)), MXU_DTYPE))
    ff = _mm(act, wts["w_ff2"], bias=sm["b_ff2"], name="ff2")
    h3, xh3, rs3, _ = _ln_fwd(h2, ff, sm["ln3_g"], sm["ln3_b"], alpha=DEEPNORM_ALPHA, name="ln3_fwd")
    dh3, loss_row = _loss_head(h3, target, name="loss_head")

    gw, gs = {}, {}
    dr3, dr3m, gs["ln3_g"], gs["ln3_b"], gs["b_ff2"] = _ln_bwd(None, dh3, xh3, rs3, sm["ln3_g"], alpha=1.0,
                                                               name="ln3_bwd")
    wgrad = functools.partial(_mm, ta=True, out_dtype=WIRE_DTYPE)
    gw["w_ff2"] = wgrad(act, dr3m, name="ff2_dw")
    dpre, gs["b_ff1"] = _mm(dr3m, wts["w_ff2"], tb=True, out_dtype=MXU_DTYPE, colsum=True, name="ff2_dx",
                            gate=(pre, lambda p: 2.0 * jnp.maximum(p, 0.0)))
    gw["w_ff1"] = wgrad(h2m, dpre, out_shards=True, name="ff1_dw")
    sent = send(0, gw)
    dh2 = _mm(dpre, wts["w_ff1"], tb=True, b_shards=True, after=sent, name="ff1_dx")

    dr2, dr2m, gs["ln2_g"], gs["ln2_b"], _ = _ln_bwd(dr3, dh2, xh2, rs2, sm["ln2_g"], alpha=DEEPNORM_ALPHA,
                                                     name="ln2_bwd")
    gw["w_xo"] = wgrad(xo_in, dr2m, name="xatt_o_dw")
    dxo_in = _mm(dr2m, wts["w_xo"], tb=True, out_dtype=MXU_DTYPE, name="xatt_o_dx")
    dxq, dkv = _xatt_bwd(xq, kv, dxo_in, name="xatt_bwd")
    gw["w_xq"] = wgrad(h1m, dxq, name="xatt_q_dw")
    gw["w_xkv"] = wgrad(mem, dkv, out_shards=True, name="xatt_kv_dw")
    dh1 = _mm(dxq, wts["w_xq"], tb=True, name="xatt_q_dx")

    dr1, dr1m, gs["ln1_g"], gs["ln1_b"], gs["b_mix_out"] = _ln_bwd(dr2, dh1, xh1, rs1, sm["ln1_g"],
                                                                   alpha=DEEPNORM_ALPHA, name="ln1_bwd")
    gw["w_mix_out"] = wgrad(mixed, dr1m, name="mix_out_dw")
    sent = send(1, gw)
    dmixed = _mm(dr1m, wts["w_mix_out"], tb=True, after=sent, name="mix_out_dx")
    dgs, dga, dz, db_att, s_gs, s_ga, gs["b_glu"] = _mix_bwd(dmixed, proj, z, b_att, name="gate_mix_bwd")

    gw["w_att_up"] = wgrad(att, db_att, out_shards=True, name="att_up_dw")
    datt = _mm(db_att, wts["w_att_up"], tb=True, b_shards=True, name="att_up_dx")
    stats = _att_stats(datt, att, lse, name="att_stats")
    dqkv = [_dil_bwd(q[g], k[g], v[g], datt, stats, dil, name=f"dil_att_bwd_{dil}") for g, dil in enumerate(DILATIONS)]

    gw["w_glu"] = wgrad(gy, dz, out_shards=True, name="glu_dw")
    dgy = _mm(dz, wts["w_glu"], tb=True, b_shards=True, name="glu_dx")
    dy, gs["ssm_d"] = _gelu_bwd(dgy, y, proj, name="gelu_bwd")
    du, s_u, dc_re_t, dc_im_t, dbb_re_t, dbb_im_t, da_re, da_im = _ssm_bwd(
        proj, dy, sb_re, sb_im, tiles_cn, tiles_nc, a_re_rows, a_im_rows, sm["ssm_d"], name="ssm_bwd")
    gs["ssm_c_re"], gs["ssm_c_im"] = _untile_cn(dc_re_t), -_untile_cn(dc_im_t)
    disc_ct = (da_re.reshape(grp), da_im.reshape(grp), _untile_cn(dbb_re_t).transpose(1, 0, 2),
               _untile_cn(dbb_im_t).transpose(1, 0, 2))
    d_logdt, gs["ssm_a_re"], gs["ssm_a_im"], d_b_re_t, d_b_im_t = _whole(
        _disc_transpose, disc_in + disc_ct, [(SSM_GROUPS, 1), grp, grp, chn, chn], name="ssm_disc_bwd")
    gs["ssm_log_dt"] = d_logdt
    gs["ssm_b_re"], gs["ssm_b_im"] = d_b_re_t.transpose(1, 2, 0), d_b_im_t.transpose(1, 2, 0)

    dproj, s_qkv = _dproj_assemble(du, dqkv, dgs, dga, cos_t, sin_t, name="dproj_assemble")
    gs["b_in"] = jnp.concatenate([s_u, *s_qkv, s_gs, s_ga], axis=1)
    gw["w_in"] = wgrad(h0m, dproj, out_shards=True, name="in_proj_dw")
    sent = send(2, gw)
    dh0 = _mm(dproj, wts["w_in"], tb=True, b_shards=True, after=sent, name="in_proj_dx")
    grad_x, _, gs["ln_in_g"], gs["ln_in_b"], _ = _ln_bwd(dr1, dh0, xh0, rs0, sm["ln_in_g"], alpha=DEEPNORM_ALPHA,
                                                         name="ln_in_bwd")
    return loss_row, grad_x, gs


N_PEER = N_DEV - 1
_IN_HBM = pl.BlockSpec(memory_space=pltpu.HBM)
_IN_SEMAPHORE = pl.BlockSpec(memory_space=pltpu.SEMAPHORE)


def _device_index():
    return 4 * lax.axis_index("x") + 2 * lax.axis_index("y") + lax.axis_index("c")


def _exchange_copies(src_refs, land_refs, send_sems, recv_sems, scatter):
    x, y, c = lax.axis_index("x"), lax.axis_index("y"), lax.axis_index("c")
    me = 4 * x + 2 * y + c
    pairs = []
    for a, (src_ref, land_ref) in enumerate(zip(src_refs, land_refs)):
        for kk in range(1, N_DEV):
            px = (x + (kk >> 2)) % 2
            py = (y + ((kk >> 1) & 1)) % 2
            pc = (c + (kk & 1)) % 2
            peer = 4 * px + 2 * py + pc
            sem = a * N_PEER + kk - 1
            src = src_ref.at[peer] if scatter else src_ref

            def copy(dst, src=src, sem=sem, px=px, py=py, pc=pc):
                return pltpu.make_async_remote_copy(
                    src_ref=src, dst_ref=dst, send_sem=send_sems.at[sem], recv_sem=recv_sems.at[sem],
                    device_id=(px, py, pc), device_id_type=pl.DeviceIdType.MESH)

            pairs.append((functools.partial(copy, land_ref.at[me]), functools.partial(copy, land_ref.at[peer])))
    return pairs


def _exchange_start(srcs, *, scatter, name, after=None):
    n_arr = len(srcs)
    lands = [lax.empty((N_DEV,) + tuple(s.shape[1:] if scatter else s.shape), s.dtype) for s in srcs]
    n_in = 2 * n_arr + (after is not None)

    def body(*refs):
        send_sems, recv_sems = refs[n_in], refs[n_in + 1]
        for sent, _ in _exchange_copies(refs[:n_arr], refs[n_arr:2 * n_arr], send_sems, recv_sems, scatter):
            sent().start()
        refs[-1][...] = jnp.zeros_like(refs[-1])

    through = [pltpu.HBM(t.shape, t.dtype) for t in (*srcs, *lands)]
    res = pl.pallas_call(
        body, name=name,
        out_shape=(pltpu.SemaphoreType.DMA((n_arr * N_PEER,)), pltpu.SemaphoreType.DMA((n_arr * N_PEER,)), *through,
                   jax.ShapeDtypeStruct((8, LANES), F32)),
        in_specs=[_IN_HBM] * (2 * n_arr) + [pl.BlockSpec(memory_space=pl.ANY)] * (after is not None),
        out_specs=(_IN_SEMAPHORE, _IN_SEMAPHORE, *[_IN_HBM] * (2 * n_arr), pl.BlockSpec(memory_space=pltpu.VMEM)),
        input_output_aliases={i: 2 + i for i in range(2 * n_arr)},
        compiler_params=pltpu.CompilerParams(has_side_effects=pltpu.SideEffectType.DATAFLOW_SIDE_EFFECTING),
    )(*[pltpu.with_memory_space_constraint(t, pltpu.HBM) for t in (*srcs, *lands)],
      *([after] if after is not None else []))
    return (res[0], res[1], res[2:2 + n_arr], res[2 + n_arr:2 + 2 * n_arr], scatter), res[-1]


def _exchange_wait(handle, *, after, name):
    send_sems, recv_sems, srcs, lands, scatter = handle
    n_arr = len(srcs)
    after = list(after)

    def body(*refs):
        for sent, received in _exchange_copies(refs[:n_arr], refs[n_arr:2 * n_arr], refs[2 * n_arr], refs[2 * n_arr + 1],
                                               scatter):
            sent().wait_send()
            received().wait_recv()

    res = pl.pallas_call(
        body, name=name, out_shape=tuple(pltpu.HBM(t.shape, t.dtype) for t in (*srcs, *lands)),
        in_specs=[_IN_HBM] * (2 * n_arr) + [_IN_SEMAPHORE, _IN_SEMAPHORE] + [pl.BlockSpec(memory_space=pl.ANY)] * len(after),
        out_specs=tuple([_IN_HBM] * (2 * n_arr)), input_output_aliases={i: i for i in range(2 * n_arr)},
        compiler_params=pltpu.CompilerParams(has_side_effects=pltpu.SideEffectType.DATAFLOW_SIDE_EFFECTING),
    )(*srcs, *lands, send_sems, recv_sems, *after)
    return res[n_arr:]


def _with_own_slot(land, own):
    return lax.dynamic_update_slice_in_dim(land, own[None], _device_index(), axis=0)


def _reduce_adamw(gstack, w, m, v, *, name, tr=128):
    n_rows, cols = w.shape
    tr = min(tr, n_rows)
    assert n_rows % tr == 0, (name, n_rows, tr)

    def body(g_ref, w_ref, m_ref, v_ref, go_ref, d_ref, mo_ref, vo_ref):
        g = g_ref[0].astype(F32)
        for dev in range(1, N_DEV):
            g = g + g_ref[dev].astype(F32)
        m_new = ADAM_B1 * m_ref[...] + (1.0 - ADAM_B1) * g
        v_new = ADAM_B2 * v_ref[...] + (1.0 - ADAM_B2) * jnp.square(g)
        m_hat = m_new / (1.0 - ADAM_B1 ** ADAM_STEP)
        v_hat = v_new / (1.0 - ADAM_B2 ** ADAM_STEP)
        go_ref[...] = g
        d_ref[...] = -ADAM_LR * (m_hat / (jnp.sqrt(v_hat) + ADAM_EPS) + ADAM_WD * w_ref[...])
        mo_ref[...] = m_new
        vo_ref[...] = v_new

    flat = pl.BlockSpec((tr, cols), lambda i: (i, 0))
    shape = jax.ShapeDtypeStruct((n_rows, cols), F32)
    return pl.pallas_call(
        body, name=name, grid=(n_rows // tr,),
        in_specs=[pl.BlockSpec((N_DEV, tr, cols), lambda i: (0, i, 0)), flat, flat, flat],
        out_specs=[flat] * 4, out_shape=[shape] * 4, compiler_params=_params("parallel"),
    )(gstack, w, m, v)


def _pack(parts, dtype):
    flat = jnp.concatenate([p.reshape(-1).astype(dtype) for p in parts])
    unit = PACK_COLS * PACK_ROW_ALIGN
    total = -(-flat.shape[0] // unit) * unit
    return jnp.pad(flat, (0, total - flat.shape[0])).reshape(-1, PACK_COLS)


def _unpack(packed, shapes):
    flat = packed.reshape(-1)
    out, off = [], 0
    for s in shapes:
        size = int(np.prod(s))
        out.append(flat[off:off + size].reshape(s))
        off += size
    return out


def kernel(x, mem, positions, ln_in_g, ln_in_b, w_in, b_in, ssm_log_dt, ssm_a_re, ssm_a_im, ssm_b_re, ssm_b_im, ssm_c_re, ssm_c_im, ssm_d, w_glu, b_glu, w_att_up, w_mix_out, b_mix_out, ln1_g, ln1_b, w_xq, w_xkv, w_xo, ln2_g, ln2_b, w_ff1, b_ff1, w_ff2, b_ff2, ln3_g, ln3_b, loss_target, m_ln_in_g, m_ln_in_b, m_w_in, m_b_in, m_ssm_log_dt, m_ssm_a_re, m_ssm_a_im, m_ssm_b_re, m_ssm_b_im, m_ssm_c_re, m_ssm_c_im, m_ssm_d, m_w_glu, m_b_glu, m_w_att_up, m_w_mix_out, m_b_mix_out, m_ln1_g, m_ln1_b, m_w_xq, m_w_xkv, m_w_xo, m_ln2_g, m_ln2_b, m_w_ff1, m_b_ff1, m_w_ff2, m_b_ff2, m_ln3_g, m_ln3_b, v_ln_in_g, v_ln_in_b, v_w_in, v_b_in, v_ssm_log_dt, v_ssm_a_re, v_ssm_a_im, v_ssm_b_re, v_ssm_b_im, v_ssm_c_re, v_ssm_c_im, v_ssm_d, v_w_glu, v_b_glu, v_w_att_up, v_w_mix_out, v_b_mix_out, v_ln1_g, v_ln1_b, v_w_xq, v_w_xkv, v_w_xo, v_ln2_g, v_ln2_b, v_w_ff1, v_b_ff1, v_w_ff2, v_b_ff2, v_ln3_g, v_ln3_b):
    given = dict(locals())
    w_arg = {n: given[n] for n in WEIGHTS}
    m_arg = {n: given["m_" + n] for n in WEIGHTS}
    v_arg = {n: given["v_" + n] for n in WEIGHTS}

    shards = {n: w_arg[n][0].astype(MXU_DTYPE) for n in BIG}
    gathers, token = [], None
    for i, names in enumerate(GATHER_GROUPS):
        handle, token = _exchange_start([shards[n] for n in names], scatter=False, after=token, name=f"gather_start_{i}")
        gathers.append(handle)

    small_wmv = [_pack([d[n] for n in SMALL], F32) for d in (w_arg, m_arg, v_arg)]

    def fetch(i, after):
        lands = _exchange_wait(gathers[i], after=after + (small_wmv if i == 0 else []), name=f"gather_wait_{i}")
        full = {n: _with_own_slot(land, shards[n]) for n, land in zip(GATHER_GROUPS[i], lands)}
        return {n: t if n in BIG_COL_SHARDED else t.reshape(-1, t.shape[-1]) for n, t in full.items()}

    scatters = {}

    def send(i, gw):
        slots = [gw[n] if n in BIG_COL_SHARDED else gw[n].reshape(N_DEV, -1, gw[n].shape[-1]) for n in SCATTER_GROUPS[i]]
        handle, sent = _exchange_start(slots, scatter=True, name=f"scatter_start_{i}")
        scatters[i] = (handle, slots)
        return sent

    sm = {}
    for n in SMALL:
        t = w_arg[n]
        if n.startswith("ssm_") and n not in ("ssm_d", "ssm_log_dt"):
            sm[n] = t[0]
        else:
            sm[n] = t.reshape(1, -1)

    loss_row, grad_x, gs = _local_grads(x[0], mem[0], positions.reshape(-1, 1), loss_target[0], sm, fetch, send, token)
    loss = lax.psum(loss_row[0, 0], ("x", "y", "c"))
    small = _pack([gs[n] for n in SMALL], WIRE_DTYPE)
    small_handle, _ = _exchange_start([small], scatter=False, name="small_start")

    results = [{}, {}, {}, {}]
    done = grad_x
    for i, names in enumerate(SCATTER_GROUPS):
        handle, slots = scatters[i]
        lands = _exchange_wait(handle, after=[done], name=f"scatter_wait_{i}")
        for n, land, slot in zip(names, lands, slots):
            own = lax.dynamic_index_in_dim(slot, _device_index(), axis=0, keepdims=False)
            res = _reduce_adamw(_with_own_slot(land, own), w_arg[n][0], m_arg[n][0], v_arg[n][0], name="adamw_" + n)
            done = res[0]
            for d, r in zip(results, res):
                d[n] = r[None]
    small_stack = _with_own_slot(_exchange_wait(small_handle, after=[done], name="small_wait")[0], small)
    small_shapes = [w_arg[n].shape for n in SMALL]
    res = _reduce_adamw(small_stack, *small_wmv, name="adamw_small")
    for d, r in zip(results, res):
        d.update(zip(SMALL, _unpack(r, small_shapes)))
    out = [loss, grad_x[None]]
    for d in results:
        out += [d[n] for n in WEIGHTS]
    return tuple(out)
```

```python
import functools

import numpy as np
import jax
import jax.numpy as jnp
from jax import lax
from jax.experimental import pallas as pl
from jax.experimental.pallas import tpu as pltpu

F32 = jnp.float32
MXU_DTYPE = jnp.bfloat16
WIRE_DTYPE = jnp.bfloat16
VMEM_LIMIT_BYTES = 48 * 1024 * 1024
LANES = 128

N_DEV = 8
D_MODEL = 1024
SSM_GROUP = 16
SSM_WIDTH = 768
SSM_GROUPS = SSM_WIDTH // SSM_GROUP
SSM_STATE = 64
SSM_CH = SSM_GROUPS * SSM_STATE
SSM_TILES = SSM_WIDTH // LANES
GROUPS_PER_TILE = LANES // SSM_GROUP
STATE_VREG_ROWS = SSM_CH // LANES
ATT_HEAD_DIM = 64
ATT_HEADS_PER_GROUP = 4
ATT_MERGED = ATT_HEADS_PER_GROUP * ATT_HEAD_DIM
LANE_HALVES = ATT_MERGED // LANES
DILATIONS = (1, 4, 16)
ATT_BLK = 128
ATT_SCALE = ATT_HEAD_DIM ** -0.5
ROT_DIM = ATT_HEAD_DIM // 4
ROPE_THETA = 500000.0
XATT_HEADS = 4
XATT_HEAD_DIM = D_MODEL // XATT_HEADS
XATT_SCALE = XATT_HEAD_DIM ** -0.5
DEEPNORM_ALPHA = 2.0 ** 0.25
LN_EPS = 1e-5
NEG_INF = -1e30
OFF_Q_BLK, OFF_K_BLK, OFF_V_BLK = 3, 6, 9
OFF_GS_BLK, OFF_GA_BLK = 3, 4

ADAM_LR = 0.001
ADAM_B1 = 0.9
ADAM_B2 = 0.999
ADAM_EPS = 1e-08
ADAM_WD = 0.01
ADAM_STEP = 10

BIG = ("w_in", "w_glu", "w_att_up", "w_mix_out", "w_xq", "w_xkv", "w_xo", "w_ff1", "w_ff2")
BIG_COL_SHARDED = ("w_in", "w_glu", "w_att_up", "w_xkv", "w_ff1")
WEIGHTS = ("ln_in_g", "ln_in_b", "w_in", "b_in", "ssm_log_dt", "ssm_a_re", "ssm_a_im", "ssm_b_re", "ssm_b_im",
           "ssm_c_re", "ssm_c_im", "ssm_d", "w_glu", "b_glu", "w_att_up", "w_mix_out", "b_mix_out", "ln1_g", "ln1_b",
           "w_xq", "w_xkv", "w_xo", "ln2_g", "ln2_b", "w_ff1", "b_ff1", "w_ff2", "b_ff2", "ln3_g", "ln3_b")
SMALL = tuple(n for n in WEIGHTS if n not in BIG)
PACK_COLS = 1024
PACK_ROW_ALIGN = 256


def _params(*sem):
    return pltpu.CompilerParams(dimension_semantics=sem, vmem_limit_bytes=VMEM_LIMIT_BYTES)


def _dot(a, b, ca, cb):
    return lax.dot_general(a.astype(MXU_DTYPE), b.astype(MXU_DTYPE), (((ca,), (cb,)), ((), ())),
                           preferred_element_type=F32)


def _fit(dim, pref):
    if dim <= pref:
        return dim
    best = max(t for t in range(LANES, pref + 1, LANES) if dim % t == 0)
    return best


def _mm(a, b, *, name, ta=False, tb=False, bias=None, out_dtype=F32, b_shards=False, out_shards=False, after=None,
        also=None, gate=None, colsum=False, tm=1024, tn=1024, tk=1024):
    m, k = (a.shape[1], a.shape[0]) if ta else a.shape
    order = (lambda f: (lambda j, i, kk: f(i, j, kk))) if colsum else (lambda f: f)
    spec = lambda shape, f: pl.BlockSpec(shape, order(f))
    if b_shards:
        n_sh, rows, n_loc = b.shape
        if tb:
            n, tn, tk = rows, _fit(rows, tn), n_loc
            assert k == n_sh * n_loc, (name, k, b.shape)
            b_spec = spec((1, tn, tk), lambda i, j, kk: (kk, j, 0))
        else:
            n, tn, tk = n_sh * n_loc, n_loc, _fit(k, tk)
            b_spec = spec((1, tk, tn), lambda i, j, kk: (j, kk, 0))
    else:
        n = b.shape[0] if tb else b.shape[1]
        tn = n // N_DEV if out_shards else _fit(n, tn)
        tk = _fit(k, tk)
        b_spec = spec((tn, tk), lambda i, j, kk: (j, kk)) if tb else spec((tk, tn), lambda i, j, kk: (kk, j))
    tm = _fit(m, tm)
    nk = k // tk
    a_spec = spec((tk, tm), lambda i, j, kk: (kk, i)) if ta else spec((tm, tk), lambda i, j, kk: (i, kk))
    tile = spec((tm, tn), lambda i, j, kk: (i, j))
    in_specs, args = [a_spec, b_spec], [a, b]
    if bias is not None:
        in_specs.append(spec((1, tn), lambda i, j, kk: (0, j)))
        args.append(bias)
    if gate is not None:
        in_specs.append(tile)
        args.append(gate[0])
    if after is not None:
        in_specs.append(pl.BlockSpec(memory_space=pl.ANY))
        args.append(after)
    n_in = len(args)
    if out_shards:
        assert n == N_DEV * tn, (name, n, tn)
        out_specs = [spec((1, tm, tn), lambda i, j, kk: (j, i, 0))]
        out_shape = [jax.ShapeDtypeStruct((N_DEV, m, tn), out_dtype)]
    else:
        out_specs = [tile]
        out_shape = [jax.ShapeDtypeStruct((m, n), out_dtype)]
    if also is not None:
        out_specs.append(tile)
        out_shape.append(jax.ShapeDtypeStruct((m, n), also[1]))
    if colsum:
        out_specs.append(spec((1, tn), lambda i, j, kk: (0, j)))
        out_shape.append(jax.ShapeDtypeStruct((1, n), F32))

    def body(*refs):
        a_ref, b_ref = refs[0], refs[1]
        o_ref = refs[n_in]

        def product():
            return _dot(a_ref[...], b_ref[0] if b_shards else b_ref[...], 0 if ta else 1, 1 if tb else 0)

        def finish(r):
            if bias is not None:
                r = r + refs[2][...]
            if gate is not None:
                r = r * gate[1](refs[2 + (bias is not None)][...])
            if out_shards:
                o_ref[0] = r.astype(o_ref.dtype)
            else:
                o_ref[...] = r.astype(o_ref.dtype)
            if also is not None:
                refs[n_in + 1][...] = also[0](r).astype(also[1])
            if colsum:
                s_ref = refs[n_in + 1 + (also is not None)]

                @pl.when(pl.program_id(1) == 0)
                def _():
                    s_ref[...] = jnp.zeros_like(s_ref)

                s_ref[...] += _colsum(r)

        if nk == 1:
            finish(product())
            return
        acc_ref = refs[-1]
        kk = pl.program_id(2)

        @pl.when(kk == 0)
        def _():
            acc_ref[...] = jnp.zeros_like(acc_ref)

        acc_ref[...] += product()

        @pl.when(kk == nk - 1)
        def _():
            finish(acc_ref[...])

    grid = (n // tn, m // tm, nk) if colsum else (m // tm, n // tn, nk)
    res = pl.pallas_call(
        body, name=name, grid=grid, in_specs=in_specs, out_specs=out_specs, out_shape=out_shape,
        scratch_shapes=[pltpu.VMEM((tm, tn), F32)] if nk > 1 else [],
        compiler_params=_params("parallel", "arbitrary" if colsum else "parallel", "arbitrary"),
    )(*args)
    return res[0] if len(res) == 1 else res


def _rowcall(fn, rows, fulls, row_outs, acc_outs=(), *, n_rows, tm, name, after=None):
    n_r, n_f, n_o, n_a = len(rows), len(fulls), len(row_outs), len(acc_outs)
    n_in = n_r + n_f + (after is not None)
    assert n_rows % tm == 0, (name, n_rows, tm)

    def body(*refs):
        res = fn(*[r[...] for r in refs[:n_r + n_f]])
        res = tuple(res) if isinstance(res, (tuple, list)) else (res,)
        o_refs = refs[n_in:n_in + n_o]
        a_refs = refs[n_in + n_o:]
        for o_ref, val in zip(o_refs, res[:n_o]):
            o_ref[...] = val.astype(o_ref.dtype)
        if n_a:
            @pl.when(pl.program_id(0) == 0)
            def _():
                for a_ref in a_refs:
                    a_ref[...] = jnp.zeros_like(a_ref)

            for a_ref, val in zip(a_refs, res[n_o:]):
                a_ref[...] += val

    in_specs = [pl.BlockSpec((tm, w), functools.partial(lambda i, cb: (i, cb), cb=cb)) for _, w, cb in rows]
    in_specs += [pl.BlockSpec(f.shape, functools.partial(lambda i, nd: (0,) * nd, nd=f.ndim)) for f in fulls]
    in_specs += [pl.BlockSpec(memory_space=pl.ANY)] * (after is not None)
    out_specs = [pl.BlockSpec((tm, w), lambda i: (i, 0)) for w, _ in row_outs]
    out_specs += [pl.BlockSpec((1, w), lambda i: (0, 0)) for w in acc_outs]
    out_shape = [jax.ShapeDtypeStruct((n_rows, w), dt) for w, dt in row_outs]
    out_shape += [jax.ShapeDtypeStruct((1, w), F32) for w in acc_outs]
    return pl.pallas_call(
        body, name=name, grid=(n_rows // tm,), in_specs=in_specs, out_specs=out_specs, out_shape=out_shape,
        compiler_params=_params("arbitrary" if n_a else "parallel"),
    )(*[r[0] for r in rows], *fulls, *([after] if after is not None else []))


def _colsum(v):
    return jnp.sum(v, axis=0, keepdims=True)


def _ln_fwd(a, r, g, b, *, alpha, name):
    n_rows, d = a.shape

    def fn(*t):
        xin = t[0] if alpha == 1.0 else alpha * t[0]
        if r is not None:
            xin = xin + t[1]
        gv, bv = t[-2], t[-1]
        mu = jnp.mean(xin, axis=-1, keepdims=True)
        xc = xin - mu
        var = jnp.mean(xc * xc, axis=-1, keepdims=True)
        rstd = lax.rsqrt(var + LN_EPS)
        xh = xc * rstd
        y = xh * gv + bv
        return y, xh, rstd, y

    rows = [(a, d, 0)] + ([(r, d, 0)] if r is not None else [])
    return _rowcall(fn, rows, [g, b], [(d, F32), (d, F32), (1, F32), (d, MXU_DTYPE)], n_rows=n_rows, tm=256, name=name)


def _ln_bwd(dya, dyb, xh, rstd, g, *, alpha, name, operand=True):
    n_rows, d = xh.shape

    def fn(da, db, xhv, rs, gv):
        dy = alpha * da + db
        dyg = dy * gv
        m1 = jnp.mean(dyg, axis=-1, keepdims=True)
        m2 = jnp.mean(dyg * xhv, axis=-1, keepdims=True)
        dx = rs * (dyg - m1 - xhv * m2)
        return (dx,) + ((dx,) if operand else ()) + (_colsum(dy * xhv), _colsum(dy), _colsum(dx))

    rows = [(dya, d, 0), (dyb, d, 0), (xh, d, 0), (rstd, 1, 0)]
    return _rowcall(fn, rows, [g], [(d, F32)] + [(d, MXU_DTYPE)] * operand, [d, d, d], n_rows=n_rows, tm=256, name=name)


def _ln_loss_bwd(a, r, target, g, b, *, alpha, name):
    n_rows, d = a.shape

    def fn(av, rv, tv, gv, bv):
        xin = alpha * av + rv
        mu = jnp.mean(xin, axis=-1, keepdims=True)
        xc = xin - mu
        var = jnp.mean(xc * xc, axis=-1, keepdims=True)
        rs = lax.rsqrt(var + LN_EPS)
        xh = xc * rs
        diff = xh * gv + bv - tv
        part = jnp.sum(jnp.sum(diff * diff, axis=1, keepdims=True), axis=0, keepdims=True) * (0.5 / d)
        dy = diff * (1.0 / d)
        dyg = dy * gv
        m1 = jnp.mean(dyg, axis=-1, keepdims=True)
        m2 = jnp.mean(dyg * xh, axis=-1, keepdims=True)
        dx = rs * (dyg - m1 - xh * m2)
        return dx, dx, _colsum(dy * xh), _colsum(dy), _colsum(dx), jnp.broadcast_to(part, (1, LANES))

    return _rowcall(fn, [(a, d, 0), (r, d, 0), (target, d, 0)], [g, b], [(d, F32), (d, MXU_DTYPE)], [d, d, d, LANES],
                    n_rows=n_rows, tm=256, name=name)


def _rope_lane_constants():
    lane = np.arange(ATT_MERGED)
    in_head = lane % ATT_HEAD_DIM
    sign = np.where(in_head < ROT_DIM // 2, -1.0, np.where(in_head < ROT_DIM, 1.0, 0.0)).astype(np.float32)
    inv_freq = ROPE_THETA ** (-jnp.arange(0, ROT_DIM, 2, dtype=F32) / ROT_DIM)
    return inv_freq[lane % (ROT_DIM // 2)].reshape(1, ATT_MERGED), jnp.asarray(sign).reshape(1, ATT_MERGED)


def _rope_tables(pos_col, *, name, after=None):
    inv_lane, sign = _rope_lane_constants()

    def fn(pos, inv, sg):
        ang = pos.astype(F32) * inv
        return jnp.where(sg != 0.0, jnp.cos(ang), 1.0), sg * jnp.sin(ang)

    return _rowcall(fn, [(pos_col, 1, 0)], [inv_lane, sign], [(ATT_MERGED, F32), (ATT_MERGED, F32)],
                    n_rows=pos_col.shape[0], tm=512, name=name, after=after)


def _rot_partner(t):
    lane = lax.broadcasted_iota(jnp.int32, t.shape, 1)
    width = t.shape[1]
    return jnp.where((lane & (ROT_DIM // 2)) == 0, pltpu.roll(t, width - ROT_DIM // 2, 1), pltpu.roll(t, ROT_DIM // 2, 1))


def _rope(t, cos_t, sin_t):
    return t * cos_t + _rot_partner(t) * sin_t


def _rope_transpose(dt, cos_t, sin_t):
    return dt * cos_t + _rot_partner(dt * sin_t)


def _strided_rows(r, count, stride):
    return pl.ds(r, count) if stride == 1 else pl.ds(r, count, stride=stride)


def _qkv_split(proj, cos_t, sin_t, *, name, tm=512):
    n_rows = proj.shape[0]
    n_g = len(DILATIONS)

    def body(*refs):
        n_src = LANE_HALVES * 3 * n_g
        src, tables, dst = refs[:n_src], refs[n_src:n_src + 2 * LANE_HALVES], refs[n_src + 2 * LANE_HALVES:]
        for kind in range(3):
            for g, dil in enumerate(DILATIONS):
                for half in range(LANE_HALVES):
                    x_ref, o_ref = src[(kind * n_g + g) * LANE_HALVES + half], dst[kind * n_g + g]
                    cos_ref, sin_ref = tables[half], tables[LANE_HALVES + half]
                    for r in range(dil):
                        rows = _strided_rows(r, tm // dil, dil)
                        t = x_ref[rows, :]
                        if kind < 2:
                            t = _rope(t, cos_ref[rows, :], sin_ref[rows, :])
                        lo = r * ATT_MERGED + half * LANES
                        o_ref[:, lo:lo + LANES] = t.astype(o_ref.dtype)

    half_spec = lambda cb: pl.BlockSpec((tm, LANES), functools.partial(lambda i, cb: (i, cb), cb=cb))
    in_specs = [half_spec((off + g) * LANE_HALVES + half)
                for off in (OFF_Q_BLK, OFF_K_BLK, OFF_V_BLK) for g in range(n_g) for half in range(LANE_HALVES)]
    in_specs += [half_spec(half) for _ in range(2) for half in range(LANE_HALVES)]
    out_specs = [pl.BlockSpec((tm // dil, dil * ATT_MERGED), lambda i: (i, 0)) for _ in range(3) for dil in DILATIONS]
    out_shape = [jax.ShapeDtypeStruct((n_rows // dil, dil * ATT_MERGED), MXU_DTYPE) for _ in range(3) for dil in DILATIONS]
    outs = pl.pallas_call(
        body, name=name, grid=(n_rows // tm,), in_specs=in_specs, out_specs=out_specs, out_shape=out_shape,
        compiler_params=_params("parallel"),
    )(*[proj] * (LANE_HALVES * 3 * n_g), *[cos_t] * LANE_HALVES, *[sin_t] * LANE_HALVES)
    return outs[:n_g], outs[n_g:2 * n_g], outs[2 * n_g:]


def _mix(gs, ga, z1, z2, b_att):
    return jax.nn.sigmoid(gs) * (z1 * jax.nn.sigmoid(z2)) + jax.nn.sigmoid(ga) * b_att


def _mix_rows(proj, z, b_att):
    return [(proj, D_MODEL, OFF_GS_BLK), (proj, D_MODEL, OFF_GA_BLK), (z, D_MODEL, 0), (z, D_MODEL, 1), (b_att, D_MODEL, 0)]


def _mix_fwd(proj, z, b_att, *, name):
    return _rowcall(_mix, _mix_rows(proj, z, b_att), [], [(D_MODEL, MXU_DTYPE)],
                    n_rows=proj.shape[0], tm=256, name=name)[0]


def _mix_bwd(dmixed, proj, z, b_att, *, name):
    def fn(dm, gs, ga, z1, z2, ba):
        _, vjp = jax.vjp(_mix, gs, ga, z1, z2, ba)
        dgs, dga, dz1, dz2, dba = vjp(dm)
        dz = jnp.concatenate([dz1, dz2], axis=1)
        return dgs, dga, dz, dba, _colsum(dgs), _colsum(dga), _colsum(dz)

    rows = [(dmixed, D_MODEL, 0)] + _mix_rows(proj, z, b_att)
    widths = [D_MODEL, D_MODEL, 2 * D_MODEL, D_MODEL]
    return _rowcall(fn, rows, [], [(w, MXU_DTYPE) for w in widths], widths[:3], n_rows=proj.shape[0], tm=256, name=name)


def _gelu_bwd(dgy, y, proj, *, name):
    def fn(dg, yv, u):
        _, vjp = jax.vjp(jax.nn.gelu, yv)
        dy = vjp(dg)[0]
        return dy, _colsum(dy * u)

    return _rowcall(fn, [(dgy, SSM_WIDTH, 0), (y, SSM_WIDTH, 0), (proj, SSM_WIDTH, 0)], [], [(SSM_WIDTH, F32)],
                    [SSM_WIDTH], n_rows=y.shape[0], tm=512, name=name)


HEAD_ROWS = ATT_HEADS_PER_GROUP * ATT_BLK


def _head_masks(rows):
    head = lax.broadcasted_iota(jnp.int32, (rows, ATT_MERGED), 1) >> (ATT_HEAD_DIM.bit_length() - 1)
    return [head == h for h in range(ATT_HEADS_PER_GROUP)]


def _stack_heads(t, masks):
    return jnp.concatenate([jnp.where(m, t, jnp.zeros_like(t)) for m in masks], axis=0)


def _unstack_heads(t4, masks):
    blocks = [t4[h * ATT_BLK:(h + 1) * ATT_BLK] for h in range(ATT_HEADS_PER_GROUP)]
    return jnp.where(masks[0], blocks[0], jnp.where(masks[1], blocks[1], jnp.where(masks[2], blocks[2], blocks[3])))


def _head_column(stats, first):
    return jnp.concatenate([stats[:, first + h:first + h + 1] for h in range(ATT_HEADS_PER_GROUP)], axis=0)


def _band_mask(first_key):
    qi = lax.broadcasted_iota(jnp.int32, (HEAD_ROWS, 2 * ATT_BLK), 0) & (ATT_BLK - 1)
    ki = lax.broadcasted_iota(jnp.int32, (HEAD_ROWS, 2 * ATT_BLK), 1)
    steps = qi + ATT_BLK - ki
    return (steps >= 0) & (steps <= ATT_BLK) & (ki >= first_key)


def _dil_fwd(q, k, v, dil, *, name):
    n_blk = q.shape[0] // ATT_BLK
    cur = pl.BlockSpec((ATT_BLK, ATT_MERGED), lambda r, n: (n, r))
    prev = pl.BlockSpec((ATT_BLK, ATT_MERGED), lambda r, n: (jnp.maximum(n - 1, 0), r))

    def body(q_ref, kp_ref, kc_ref, vp_ref, vc_ref, o_ref, l_ref):
        masks = _head_masks(ATT_BLK)
        valid = _band_mask(jnp.where(pl.program_id(1) > 0, 0, ATT_BLK))
        keys = jnp.concatenate([kp_ref[...], kc_ref[...]], axis=0)
        vals = jnp.concatenate([vp_ref[...], vc_ref[...]], axis=0)
        s = jnp.where(valid, _dot(_stack_heads(q_ref[...], masks), keys, 1, 1) * ATT_SCALE, NEG_INF)
        m = jnp.max(s, axis=-1, keepdims=True)
        p = jnp.exp(s - m)
        den = jnp.sum(p, axis=-1, keepdims=True)
        o_ref[...] = _unstack_heads(_dot(p, vals, 1, 0) / den, masks)
        l_ref[...] = _unstack_heads(jnp.broadcast_to(m + jnp.log(den), (HEAD_ROWS, ATT_MERGED)), masks)

    shape = jax.ShapeDtypeStruct(q.shape, F32)
    return pl.pallas_call(
        body, name=name, grid=(dil, n_blk), in_specs=[cur, prev, cur, prev, cur], out_specs=[cur, cur],
        out_shape=[shape, shape], compiler_params=_params("parallel", "parallel"),
    )(q, k, k, v, v)


def _att_merge(outs, lses, *, name, tm=512):
    n_g = len(outs)
    n_rows = outs[0].shape[0] * DILATIONS[0]

    def body(*refs):
        src, (att_ref, lse_ref), tmp = refs[:2 * n_g], refs[2 * n_g:2 * n_g + 2], refs[2 * n_g + 2:]
        vals = []
        for idx, src_ref in enumerate(src):
            dil = DILATIONS[idx % n_g]
            if dil == 1:
                vals.append(src_ref[...])
                continue
            for r in range(dil):
                for half in range(LANE_HALVES):
                    lo = r * ATT_MERGED + half * LANES
                    tmp[LANE_HALVES * idx + half][_strided_rows(r, tm // dil, dil), :] = src_ref[:, lo:lo + LANES]
            vals.append(jnp.concatenate([tmp[LANE_HALVES * idx + half][...] for half in range(LANE_HALVES)], axis=1))
        o, l = vals[:n_g], vals[n_g:]
        m = functools.reduce(jnp.maximum, l)
        e = [jnp.exp(li - m) for li in l]
        z = functools.reduce(jnp.add, e)
        att_ref[...] = functools.reduce(jnp.add, [(ei / z) * oi for ei, oi in zip(e, o)])
        lse_ref[...] = m + jnp.log(z)

    in_specs = [pl.BlockSpec((tm // dil, dil * ATT_MERGED), lambda i: (i, 0)) for _ in range(2) for dil in DILATIONS]
    row = pl.BlockSpec((tm, ATT_MERGED), lambda i: (i, 0))
    shape = jax.ShapeDtypeStruct((n_rows, ATT_MERGED), F32)
    return pl.pallas_call(
        body, name=name, grid=(n_rows // tm,), in_specs=in_specs, out_specs=[row, row], out_shape=[shape, shape],
        scratch_shapes=[pltpu.VMEM((tm, LANES), F32)] * (LANE_HALVES * 2 * n_g), compiler_params=_params("parallel"),
    )(*outs, *lses)


def _att_stats(datt, att, lse, *, name):
    n_rows = datt.shape[0]

    def fn(d, a, l):
        prod = d * a
        lane = lax.broadcasted_iota(jnp.int32, (d.shape[0], LANES), 1)
        out = jnp.zeros((d.shape[0], LANES), F32)
        for h in range(ATT_HEADS_PER_GROUP):
            lo = h * ATT_HEAD_DIM
            out = jnp.where(lane == h, l[:, lo:lo + 1], out)
            delta = jnp.sum(prod[:, lo:lo + ATT_HEAD_DIM], axis=-1, keepdims=True)
            out = jnp.where(lane == ATT_HEADS_PER_GROUP + h, delta, out)
        return out

    rows = [(t, ATT_MERGED, 0) for t in (datt, att, lse)]
    return _rowcall(fn, rows, [], [(LANES, F32)], n_rows=n_rows, tm=512, name=name)[0]


def _dil_bwd(q, k, v, datt, stats, dil, *, name):
    n_rows = datt.shape[0]
    n_blk = n_rows // dil // ATT_BLK
    span = ATT_BLK * dil
    cur = pl.BlockSpec((ATT_BLK, ATT_MERGED), lambda n, r: (n, r))
    prev = pl.BlockSpec((ATT_BLK, ATT_MERGED), lambda n, r: (jnp.maximum(n - 1, 0), r))
    nxt = pl.BlockSpec((ATT_BLK, ATT_MERGED), lambda n, r: (jnp.minimum(n + 1, n_blk - 1), r))
    seq = lambda half, ahead: pl.BlockSpec((span, LANES), lambda n, r: (jnp.minimum(n + ahead, n_blk - 1), half))

    def body(qc_ref, qn_ref, kp_ref, kc_ref, vp_ref, vc_ref, dc0_ref, dc1_ref, dn0_ref, dn1_ref, sc_ref, sn_ref,
             dq0_ref, dq1_ref, dk0_ref, dk1_ref, dv0_ref, dv1_ref):
        n = pl.program_id(0)
        rows = slice(None) if dil == 1 else _strided_rows(pl.program_id(1), ATT_BLK, dil)

        def read(ref0, ref1):
            return jnp.concatenate([ref0[rows, :], ref1[rows, :]], axis=1)

        def write(ref0, ref1, val):
            ref0[rows, :] = val[:, :LANES]
            ref1[rows, :] = val[:, LANES:]

        masks = _head_masks(ATT_BLK)
        valid = _band_mask(jnp.where(n > 0, 0, ATT_BLK))
        qi = lax.broadcasted_iota(jnp.int32, (HEAD_ROWS, ATT_BLK), 0) & (ATT_BLK - 1)
        ki = lax.broadcasted_iota(jnp.int32, (HEAD_ROWS, ATT_BLK), 1)
        valid_next = (ki - qi) >= jnp.where(n < n_blk - 1, 0, ATT_BLK)

        kc, vc = kc_ref[...], vc_ref[...]
        keys = jnp.concatenate([kp_ref[...], kc], axis=0)
        vals = jnp.concatenate([vp_ref[...], vc], axis=0)
        q4 = _stack_heads(qc_ref[...], masks)
        d4 = _stack_heads(read(dc0_ref, dc1_ref).astype(MXU_DTYPE), masks)
        st = sc_ref[rows, :]
        p = jnp.where(valid, jnp.exp(_dot(q4, keys, 1, 1) * ATT_SCALE - _head_column(st, 0)), 0.0)
        ds = p * (_dot(d4, vals, 1, 1) - _head_column(st, ATT_HEADS_PER_GROUP)) * ATT_SCALE
        write(dq0_ref, dq1_ref, _unstack_heads(_dot(ds, keys, 1, 0), masks))

        q4n = _stack_heads(qn_ref[...], masks)
        d4n = _stack_heads(read(dn0_ref, dn1_ref).astype(MXU_DTYPE), masks)
        stn = sn_ref[rows, :]
        p_n = jnp.where(valid_next, jnp.exp(_dot(q4n, kc, 1, 1) * ATT_SCALE - _head_column(stn, 0)), 0.0)
        ds_n = p_n * (_dot(d4n, vc, 1, 1) - _head_column(stn, ATT_HEADS_PER_GROUP)) * ATT_SCALE
        write(dv0_ref, dv1_ref, _dot(p[:, ATT_BLK:], d4, 0, 0) + _dot(p_n, d4n, 0, 0))
        write(dk0_ref, dk1_ref, _dot(ds[:, ATT_BLK:], q4, 0, 0) + _dot(ds_n, q4n, 0, 0))

    shape = jax.ShapeDtypeStruct((n_rows, LANES), F32)
    out = seq(0, 0)
    res = pl.pallas_call(
        body, name=name, grid=(n_blk, dil),
        in_specs=[cur, nxt, prev, cur, prev, cur, seq(0, 0), seq(1, 0), seq(0, 1), seq(1, 1), seq(0, 0), seq(0, 1)],
        out_specs=[out] * 6, out_shape=[shape] * 6, compiler_params=_params("parallel", "arbitrary"),
    )(q, q, k, k, v, v, datt, datt, datt, datt, stats, stats)
    return [(res[2 * i], res[2 * i + 1]) for i in range(3)]


def _dproj_assemble(du, dqkv, dgs, dga, cos_t, sin_t, *, name):
    n_g = len(DILATIONS)

    def fn(*t):
        n_half = LANE_HALVES * 3 * n_g
        du_t, halves, (dgs_t, dga_t, c, s) = t[0], t[1:1 + n_half], t[1 + n_half:]
        parts = [jnp.concatenate(halves[LANE_HALVES * i:LANE_HALVES * (i + 1)], axis=1) for i in range(3 * n_g)]
        for i in range(2 * n_g):
            parts[i] = _rope_transpose(parts[i], c, s)
        cast = [p.astype(MXU_DTYPE) for p in parts]
        return [jnp.concatenate([du_t] + cast + [dgs_t, dga_t], axis=1)] + [_colsum(p) for p in parts]

    rows = [(du, SSM_WIDTH, 0)]
    rows += [(half, LANES, 0) for i in range(3) for g in range(n_g) for half in dqkv[g][i]]
    rows += [(dgs, D_MODEL, 0), (dga, D_MODEL, 0), (cos_t, ATT_MERGED, 0), (sin_t, ATT_MERGED, 0)]
    width = SSM_WIDTH + 3 * n_g * ATT_MERGED + 2 * D_MODEL
    res = _rowcall(fn, rows, [], [(width, MXU_DTYPE)], [ATT_MERGED] * (3 * n_g), n_rows=du.shape[0], tm=256, name=name)
    return res[0], res[1:]


def _xhead(h):
    return slice(h * XATT_HEAD_DIM, (h + 1) * XATT_HEAD_DIM)


def _xatt_probs(qh, kh):
    s = _dot(qh, kh, 1, 1) * XATT_SCALE
    e = jnp.exp(s - jnp.max(s, axis=-1, keepdims=True))
    return e / jnp.sum(e, axis=-1, keepdims=True)


def _xatt_fwd(q, kv, *, name, tm=512):
    n_rows = q.shape[0]
    n_mem = kv.shape[0]

    def body(q_ref, kv_ref, o_ref):
        for h in range(XATT_HEADS):
            sl = _xhead(h)
            p = _xatt_probs(q_ref[:, sl], kv_ref[:, sl])
            o_ref[:, sl] = _dot(p, kv_ref[:, D_MODEL + h * XATT_HEAD_DIM:D_MODEL + (h + 1) * XATT_HEAD_DIM], 1, 0
                                ).astype(o_ref.dtype)

    row = pl.BlockSpec((tm, D_MODEL), lambda i: (i, 0))
    return pl.pallas_call(
        body, name=name, grid=(n_rows // tm,),
        in_specs=[row, pl.BlockSpec((n_mem, 2 * D_MODEL), lambda i: (0, 0))], out_specs=row,
        out_shape=jax.ShapeDtypeStruct((n_rows, D_MODEL), MXU_DTYPE), compiler_params=_params("parallel"),
    )(q, kv)


def _xatt_bwd(q, kv, do, *, name, tm=512):
    n_rows = q.shape[0]
    n_mem = kv.shape[0]

    def body(q_ref, kv_ref, do_ref, dq_ref, dkv_ref):
        @pl.when(pl.program_id(0) == 0)
        def _():
            dkv_ref[...] = jnp.zeros_like(dkv_ref)

        for h in range(XATT_HEADS):
            sl = _xhead(h)
            vsl = slice(D_MODEL + h * XATT_HEAD_DIM, D_MODEL + (h + 1) * XATT_HEAD_DIM)
            qh, kh, doh = q_ref[:, sl], kv_ref[:, sl], do_ref[:, sl]
            p = _xatt_probs(qh, kh)
            dp = _dot(doh, kv_ref[:, vsl], 1, 1)
            ds = p * (dp - jnp.sum(dp * p, axis=-1, keepdims=True)) * XATT_SCALE
            dq_ref[:, sl] = _dot(ds, kh, 1, 0).astype(dq_ref.dtype)
            dkv_ref[:, sl] += _dot(ds, qh, 0, 0)
            dkv_ref[:, vsl] += _dot(p, doh, 0, 0)

    row = pl.BlockSpec((tm, D_MODEL), lambda i: (i, 0))
    full = pl.BlockSpec((n_mem, 2 * D_MODEL), lambda i: (0, 0))
    return pl.pallas_call(
        body, name=name, grid=(n_rows // tm,), in_specs=[row, full, row], out_specs=[row, full],
        out_shape=[jax.ShapeDtypeStruct((n_rows, D_MODEL), MXU_DTYPE), jax.ShapeDtypeStruct((n_mem, 2 * D_MODEL), F32)],
        compiler_params=_params("arbitrary"),
    )(q, kv, do)


def _disc(logdt, a_re, a_im, b_re, b_im):
    dt = jnp.exp(logdt)
    mag = jnp.exp(a_re * dt)
    ab_re = mag * jnp.cos(a_im * dt)
    ab_im = mag * jnp.sin(a_im * dt)
    den = jnp.square(a_re) + jnp.square(a_im)
    nr = ab_re - 1.0
    f_re = (nr * a_re + ab_im * a_im) / den
    f_im = (ab_im * a_re - nr * a_im) / den
    bb_re = f_re[None] * b_re - f_im[None] * b_im
    bb_im = f_re[None] * b_im + f_im[None] * b_re
    return ab_re, ab_im, bb_re, bb_im


def _disc_transpose(logdt, a_re, a_im, b_re, b_im, g_ab_re, g_ab_im, g_bb_re, g_bb_im):
    dt = jnp.exp(logdt)
    mag = jnp.exp(a_re * dt)
    th = a_im * dt
    cs, sn = jnp.cos(th), jnp.sin(th)
    ab_re, ab_im = mag * cs, mag * sn
    den = jnp.square(a_re) + jnp.square(a_im)
    nr = ab_re - 1.0
    f_re = (nr * a_re + ab_im * a_im) / den
    f_im = (ab_im * a_re - nr * a_im) / den
    d_f_re = jnp.sum(g_bb_re * b_re + g_bb_im * b_im, axis=0)
    d_f_im = jnp.sum(g_bb_im * b_re - g_bb_re * b_im, axis=0)
    d_b_re = g_bb_re * f_re[None] + g_bb_im * f_im[None]
    d_b_im = g_bb_im * f_re[None] - g_bb_re * f_im[None]
    d_n_re, d_n_im = d_f_re / den, d_f_im / den
    d_den = -(d_f_re * f_re + d_f_im * f_im) / den
    d_ab_re = g_ab_re + d_n_re * a_re - d_n_im * a_im
    d_ab_im = g_ab_im + d_n_re * a_im + d_n_im * a_re
    d_a_re = d_n_re * nr + d_n_im * ab_im + 2.0 * d_den * a_re
    d_a_im = d_n_re * ab_im - d_n_im * nr + 2.0 * d_den * a_im
    d_mag = d_ab_re * cs + d_ab_im * sn
    d_th = mag * (d_ab_im * cs - d_ab_re * sn)
    d_a_re = d_a_re + d_mag * mag * dt
    d_a_im = d_a_im + d_th * dt
    d_dt = jnp.sum(d_mag * mag * a_re + d_th * a_im, axis=-1, keepdims=True)
    return d_dt * dt, d_a_re, d_a_im, d_b_re, d_b_im


def _whole(fn, args, out_shapes, *, name):
    n_in = len(args)

    def body(*refs):
        res = fn(*[r[...] for r in refs[:n_in]])
        for o_ref, val in zip(refs[n_in:], res):
            o_ref[...] = val

    return pl.pallas_call(body, name=name, out_shape=[jax.ShapeDtypeStruct(s, F32) for s in out_shapes],
                          compiler_params=pltpu.CompilerParams(vmem_limit_bytes=VMEM_LIMIT_BYTES))(*args)


def _tiles_cn(t):
    t = t.reshape(-1, SSM_TILES, GROUPS_PER_TILE, SSM_GROUP, SSM_STATE)
    eye = jnp.eye(GROUPS_PER_TILE, dtype=t.dtype)
    return (t[:, :, :, :, None, :] * eye[:, None, :, None]).reshape(-1, SSM_TILES, LANES, GROUPS_PER_TILE * SSM_STATE)


def _tiles_nc(t):
    t = t.reshape(-1, SSM_TILES, GROUPS_PER_TILE, SSM_STATE, SSM_GROUP)
    eye = jnp.eye(GROUPS_PER_TILE, dtype=t.dtype)
    return (t[:, :, :, :, None, :] * eye[:, None, :, None]).reshape(-1, SSM_TILES, GROUPS_PER_TILE * SSM_STATE, LANES)


def _untile_cn(t):
    t = t.reshape(SSM_TILES, GROUPS_PER_TILE, SSM_GROUP, GROUPS_PER_TILE, SSM_STATE)
    eye = jnp.eye(GROUPS_PER_TILE, dtype=t.dtype)
    return jnp.sum(t * eye[None, :, None, :, None], axis=3).reshape(SSM_GROUPS, SSM_GROUP, SSM_STATE)


SSM_WIDE = GROUPS_PER_TILE * SSM_STATE
LANE_GROUPS_PER_TILE = SSM_WIDE // LANES


def _chan(j):
    return slice(j * LANES, (j + 1) * LANES)


def _time_major_rows(j, q, tc):
    return pl.ds(j * LANE_GROUPS_PER_TILE + q, tc, stride=STATE_VREG_ROWS)


def _to_time_major(x, t_re_ref, t_im_ref, dst_re, dst_im, tc):
    for j in range(SSM_TILES):
        xj = x[:, _chan(j)]
        for t_ref, dst in ((t_re_ref, dst_re), (t_im_ref, dst_im)):
            r = _dot(xj, t_ref[j], 1, 0)
            for q in range(LANE_GROUPS_PER_TILE):
                dst[_time_major_rows(j, q, tc), :] = r[:, q * LANES:(q + 1) * LANES]


def _from_time_major(src, j, tc):
    return jnp.concatenate([src[_time_major_rows(j, q, tc), :] for q in range(LANE_GROUPS_PER_TILE)], axis=1)


def _scan_chunk(w_re, w_im, h_re, h_im, a_re, a_im, start, tc):
    def step(t, carry):
        hr, hi = carry
        rows = _scan_rows(t)
        nr = a_re * hr - a_im * hi + w_re[rows, :]
        ni = a_re * hi + a_im * hr + w_im[rows, :]
        h_re[rows, :] = nr
        h_im[rows, :] = ni
        return nr, ni

    return lax.fori_loop(0, tc, step, start, unroll=8)


SSM_CHUNK = 256


def _tile_spec(stack, k):
    return pl.BlockSpec((pl.Squeezed(),) + tuple(stack.shape[1:]), lambda i: (k, 0, 0, 0))


def _ssm_fwd(proj, tiles_cn, tiles_nc, a_re, a_im, gain, *, name, tc=SSM_CHUNK):
    n_rows = proj.shape[0]
    n_chunk = n_rows // tc

    def body(u_ref, tbr_ref, tbi_ref, tcr_ref, tci_ref, ar_ref, ai_ref, g_ref, y_ref, gy_ref, hr, hi, wr, wi, state):
        @pl.when(pl.program_id(0) == 0)
        def _():
            state[...] = jnp.zeros_like(state)

        u = u_ref[...]
        _to_time_major(u, tbr_ref, tbi_ref, wr, wi, tc)
        state[0], state[1] = _scan_chunk(wr, wi, hr, hi, ar_ref[...], ai_ref[...], (state[0], state[1]), tc)
        for j in range(SSM_TILES):
            yj = (_dot(_from_time_major(hr, j, tc), tcr_ref[j], 1, 0) + _dot(_from_time_major(hi, j, tc), tci_ref[j], 1, 0)
                  + g_ref[:, _chan(j)] * u[:, _chan(j)])
            y_ref[:, _chan(j)] = yj
            gy_ref[:, _chan(j)] = jax.nn.gelu(yj).astype(gy_ref.dtype)

    rows = pl.BlockSpec((tc, SSM_WIDTH), lambda i: (i, 0))
    coef = pl.BlockSpec((STATE_VREG_ROWS, LANES), lambda i: (0, 0))
    states = pl.BlockSpec((tc * STATE_VREG_ROWS, LANES), lambda i: (i, 0))
    sshape = jax.ShapeDtypeStruct((n_rows * STATE_VREG_ROWS, LANES), F32)
    return pl.pallas_call(
        body, name=name, grid=(n_chunk,),
        in_specs=[rows, _tile_spec(tiles_cn, 0), _tile_spec(tiles_cn, 1), _tile_spec(tiles_nc, 0), _tile_spec(tiles_nc, 1),
                  coef, coef, pl.BlockSpec((1, SSM_WIDTH), lambda i: (0, 0))],
        out_specs=[rows, rows, states, states],
        out_shape=[jax.ShapeDtypeStruct((n_rows, SSM_WIDTH), F32), jax.ShapeDtypeStruct((n_rows, SSM_WIDTH), MXU_DTYPE),
                   sshape, sshape],
        scratch_shapes=[pltpu.VMEM((tc * STATE_VREG_ROWS, LANES), F32)] * 2 + [pltpu.VMEM((2, STATE_VREG_ROWS, LANES), F32)],
        compiler_params=_params("arbitrary"),
    )(proj, tiles_cn, tiles_cn, tiles_nc, tiles_nc, a_re, a_im, gain)


def _ssm_bwd(proj, dy, h_re, h_im, tiles_cn, tiles_nc, a_re, a_im, gain, *, name, tc=SSM_CHUNK):
    n_rows = proj.shape[0]
    n_chunk = n_rows // tc

    def body(u_ref, dy_ref, hr, hi, tdr_ref, tdi_ref, tur_ref, tui_ref, ar_ref, ai_ref, g_ref,
             du_ref, su_ref, dcr_ref, dci_ref, dbr_ref, dbi_ref, dar_ref, dai_ref, wr, wi, carry):
        @pl.when(pl.program_id(0) == 0)
        def _():
            carry[...] = jnp.zeros_like(carry)
            for acc_ref in (su_ref, dcr_ref, dci_ref, dbr_ref, dbi_ref):
                acc_ref[...] = jnp.zeros_like(acc_ref)

        a_r, a_i = ar_ref[...], ai_ref[...]
        u, dyv = u_ref[...], dy_ref[...]
        _to_time_major(dyv, tdr_ref, tdi_ref, wr, wi, tc)

        def step(kk, c):
            lam_r, lam_i, dar, dai = c
            rows = _scan_rows(tc - 1 - kk)
            h_r, h_i = hr[rows, :], hi[rows, :]
            dar = dar + lam_r * h_r + lam_i * h_i
            dai = dai + lam_i * h_r - lam_r * h_i
            new_r = wr[rows, :] + a_r * lam_r + a_i * lam_i
            new_i = wi[rows, :] + a_r * lam_i - a_i * lam_r
            wr[rows, :] = new_r
            wi[rows, :] = new_i
            return new_r, new_i, dar, dai

        carry[0], carry[1], carry[2], carry[3] = lax.fori_loop(0, tc, step, (carry[0], carry[1], carry[2], carry[3]),
                                                              unroll=8)
        dar_ref[...] = carry[2]
        dai_ref[...] = carry[3]
        for j in range(SSM_TILES):
            cj = _chan(j)
            lam_r, lam_i = _from_time_major(wr, j, tc), _from_time_major(wi, j, tc)
            dcr_ref[j] += _dot(dyv[:, cj], _from_time_major(hr, j, tc), 0, 0)
            dci_ref[j] += _dot(dyv[:, cj], _from_time_major(hi, j, tc), 0, 0)
            dbr_ref[j] += _dot(u[:, cj], lam_r, 0, 0)
            dbi_ref[j] += _dot(u[:, cj], lam_i, 0, 0)
            duj = _dot(lam_r, tur_ref[j], 1, 0) + _dot(lam_i, tui_ref[j], 1, 0) + g_ref[:, cj] * dyv[:, cj]
            du_ref[:, cj] = duj.astype(du_ref.dtype)
            su_ref[:, cj] += _colsum(duj)

    back = lambda i: (n_chunk - 1 - i, 0)
    rows = pl.BlockSpec((tc, SSM_WIDTH), back)
    in_tile = pl.BlockSpec((SSM_TILES, LANES, SSM_WIDE), lambda i: (0, 0, 0))
    coef = pl.BlockSpec((STATE_VREG_ROWS, LANES), lambda i: (0, 0))
    states = pl.BlockSpec((tc * STATE_VREG_ROWS, LANES), back)
    vec = pl.BlockSpec((1, SSM_WIDTH), lambda i: (0, 0))
    tshape = jax.ShapeDtypeStruct((SSM_TILES, LANES, SSM_WIDE), F32)
    cshape = jax.ShapeDtypeStruct((STATE_VREG_ROWS, LANES), F32)
    return pl.pallas_call(
        body, name=name, grid=(n_chunk,),
        in_specs=[rows, rows, states, states, _tile_spec(tiles_cn, 2), _tile_spec(tiles_cn, 3), _tile_spec(tiles_nc, 2),
                  _tile_spec(tiles_nc, 3), coef, coef, vec],
        out_specs=[rows, vec, in_tile, in_tile, in_tile, in_tile, coef, coef],
        out_shape=[jax.ShapeDtypeStruct((n_rows, SSM_WIDTH), MXU_DTYPE), jax.ShapeDtypeStruct((1, SSM_WIDTH), F32),
                   tshape, tshape, tshape, tshape, cshape, cshape],
        scratch_shapes=[pltpu.VMEM((tc * STATE_VREG_ROWS, LANES), F32)] * 2 + [pltpu.VMEM((4, STATE_VREG_ROWS, LANES), F32)],
        compiler_params=_params("arbitrary"),
    )(proj, dy, h_re, h_im, tiles_cn, tiles_cn, tiles_nc, tiles_nc, a_re, a_im, gain)


def _scan_rows(t):
    return pl.ds(pl.multiple_of(t * STATE_VREG_ROWS, 8), STATE_VREG_ROWS)


GATHER_GROUPS = (("w_in",), ("w_glu", "w_att_up", "w_mix_out"), ("w_xq", "w_xkv", "w_xo", "w_ff1", "w_ff2"))
SCATTER_GROUPS = (("w_ff2", "w_ff1"), ("w_xo", "w_xq", "w_xkv", "w_mix_out"), ("w_att_up", "w_glu", "w_in"))


def _local_grads(x, mem, pos_col, target, sm, fetch, send, start_token):
    b_re_t = sm["ssm_b_re"].transpose(2, 0, 1)
    b_im_t = sm["ssm_b_im"].transpose(2, 0, 1)
    logdt = sm["ssm_log_dt"].reshape(SSM_GROUPS, 1)
    c_re, c_im = sm["ssm_c_re"], sm["ssm_c_im"]
    grp = (SSM_GROUPS, SSM_STATE)
    chn = (SSM_GROUP, SSM_GROUPS, SSM_STATE)

    wts = {}
    cos_t, sin_t = _rope_tables(pos_col, after=start_token, name="rope_tables")
    h0, xh0, rs0, h0m = _ln_fwd(x, None, sm["ln_in_g"], sm["ln_in_b"], alpha=1.0, name="ln_in_fwd")
    disc_in = (logdt, sm["ssm_a_re"], sm["ssm_a_im"], b_re_t, b_im_t)
    ab_re, ab_im, bb_re_t, bb_im_t = _whole(_disc, disc_in, [grp, grp, chn, chn], name="ssm_disc")
    a_re_rows, a_im_rows = ab_re.reshape(STATE_VREG_ROWS, LANES), ab_im.reshape(STATE_VREG_ROWS, LANES)
    tiles_cn = _tiles_cn(jnp.stack([bb_re_t.transpose(1, 0, 2), bb_im_t.transpose(1, 0, 2), c_re, -c_im])
                         ).astype(MXU_DTYPE)
    tiles_nc = _tiles_nc(jnp.stack([c_re.transpose(0, 2, 1), -c_im.transpose(0, 2, 1), bb_re_t.transpose(1, 2, 0),
                                    bb_im_t.transpose(1, 2, 0)])).astype(MXU_DTYPE)
    wts.update(fetch(0, [h0m, tiles_cn, tiles_nc]))
    proj = _mm(h0m, wts["w_in"], bias=sm["b_in"], b_shards=True, name="in_proj")

    y, gy, h_re, h_im = _ssm_fwd(proj, tiles_cn, tiles_nc, a_re_rows, a_im_rows, sm["ssm_d"], name="ssm_fwd")

    q, k, v = _qkv_split(proj, cos_t, sin_t, name="qkv_split")
    outs, lses = [], []
    for g, dil in enumerate(DILATIONS):
        o_g, l_g = _dil_fwd(q[g], k[g], v[g], dil, name=f"dil_att_fwd_{dil}")
        outs.append(o_g)
        lses.append(l_g)
    att, lse = _att_merge(outs, lses, name="att_merge")
    wts.update(fetch(1, [att]))
    z = _mm(gy, wts["w_glu"], bias=sm["b_glu"], b_shards=True, name="glu_proj")
    b_att = _mm(att, wts["w_att_up"], b_shards=True, name="att_up")

    mixed = _mix_fwd(proj, z, b_att, name="gate_mix")
    mix_out = _mm(mixed, wts["w_mix_out"], bias=sm["b_mix_out"], name="mix_out")
    h1, xh1, rs1, h1m = _ln_fwd(h0, mix_out, sm["ln1_g"], sm["ln1_b"], alpha=DEEPNORM_ALPHA, name="ln1_fwd")

    wts.update(fetch(2, [h1m]))
    xq = _mm(h1m, wts["w_xq"], out_dtype=MXU_DTYPE, name="xatt_q")
    kv = _mm(mem, wts["w_xkv"], out_dtype=MXU_DTYPE, b_shards=True, name="xatt_kv")
    xo_in = _xatt_fwd(xq, kv, name="xatt_fwd")
    xo = _mm(xo_in, wts["w_xo"], name="xatt_o")
    h2, xh2, rs2, h2m = _ln_fwd(h1, xo, sm["ln2_g"], sm["ln2_b"], alpha=DEEPNORM_ALPHA, name="ln2_fwd")

    pre, act = _mm(h2m, wts["w_ff1"], bias=sm["b_ff1"], b_shards=True, name="ff1",
                   also=(lambda r: jnp.square(jnp.maximum(r, 0.0)), MXU_DTYPE))
    ff = _mm(act, wts["w_ff2"], bias=sm["b_ff2"], name="ff2")

    gw, gs = {}, {}
    dr3, dr3m, gs["ln3_g"], gs["ln3_b"], gs["b_ff2"], loss_row = _ln_loss_bwd(
        h2, ff, target, sm["ln3_g"], sm["ln3_b"], alpha=DEEPNORM_ALPHA, name="ln3_loss")
    wgrad = functools.partial(_mm, ta=True, out_dtype=WIRE_DTYPE)
    gw["w_ff2"] = wgrad(act, dr3m, name="ff2_dw")
    dpre, gs["b_ff1"] = _mm(dr3m, wts["w_ff2"], tb=True, out_dtype=MXU_DTYPE, colsum=True, name="ff2_dx",
                            gate=(pre, lambda p: 2.0 * jnp.maximum(p, 0.0)))
    gw["w_ff1"] = wgrad(h2m, dpre, out_shards=True, name="ff1_dw")
    sent = send(0, gw)
    dh2 = _mm(dpre, wts["w_ff1"], tb=True, b_shards=True, after=sent, name="ff1_dx")

    dr2, dr2m, gs["ln2_g"], gs["ln2_b"], _ = _ln_bwd(dr3, dh2, xh2, rs2, sm["ln2_g"], alpha=DEEPNORM_ALPHA,
                                                     name="ln2_bwd")
    gw["w_xo"] = wgrad(xo_in, dr2m, name="xatt_o_dw")
    dxo_in = _mm(dr2m, wts["w_xo"], tb=True, out_dtype=MXU_DTYPE, name="xatt_o_dx")
    dxq, dkv = _xatt_bwd(xq, kv, dxo_in, name="xatt_bwd")
    gw["w_xq"] = wgrad(h1m, dxq, name="xatt_q_dw")
    gw["w_xkv"] = wgrad(mem, dkv, out_shards=True, name="xatt_kv_dw")
    dh1 = _mm(dxq, wts["w_xq"], tb=True, name="xatt_q_dx")

    dr1, dr1m, gs["ln1_g"], gs["ln1_b"], gs["b_mix_out"] = _ln_bwd(dr2, dh1, xh1, rs1, sm["ln1_g"],
                                                                   alpha=DEEPNORM_ALPHA, name="ln1_bwd")
    gw["w_mix_out"] = wgrad(mixed, dr1m, name="mix_out_dw")
    sent = send(1, gw)
    dmixed = _mm(dr1m, wts["w_mix_out"], tb=True, after=sent, name="mix_out_dx")
    dgs, dga, dz, db_att, s_gs, s_ga, gs["b_glu"] = _mix_bwd(dmixed, proj, z, b_att, name="gate_mix_bwd")

    gw["w_att_up"] = wgrad(att, db_att, out_shards=True, name="att_up_dw")
    datt = _mm(db_att, wts["w_att_up"], tb=True, b_shards=True, name="att_up_dx")
    stats = _att_stats(datt, att, lse, name="att_stats")
    dqkv = [_dil_bwd(q[g], k[g], v[g], datt, stats, dil, name=f"dil_att_bwd_{dil}") for g, dil in enumerate(DILATIONS)]

    gw["w_glu"] = wgrad(gy, dz, out_shards=True, name="glu_dw")
    dgy = _mm(dz, wts["w_glu"], tb=True, b_shards=True, name="glu_dx")
    dy, gs["ssm_d"] = _gelu_bwd(dgy, y, proj, name="gelu_bwd")
    du, s_u, dc_re_t, dc_im_t, dbb_re_t, dbb_im_t, da_re, da_im = _ssm_bwd(
        proj, dy, h_re, h_im, tiles_cn, tiles_nc, a_re_rows, a_im_rows, sm["ssm_d"], name="ssm_bwd")
    gs["ssm_c_re"], gs["ssm_c_im"] = _untile_cn(dc_re_t), -_untile_cn(dc_im_t)
    disc_ct = (da_re.reshape(grp), da_im.reshape(grp), _untile_cn(dbb_re_t).transpose(1, 0, 2),
               _untile_cn(dbb_im_t).transpose(1, 0, 2))
    d_logdt, gs["ssm_a_re"], gs["ssm_a_im"], d_b_re_t, d_b_im_t = _whole(
        _disc_transpose, disc_in + disc_ct, [(SSM_GROUPS, 1), grp, grp, chn, chn], name="ssm_disc_bwd")
    gs["ssm_log_dt"] = d_logdt
    gs["ssm_b_re"], gs["ssm_b_im"] = d_b_re_t.transpose(1, 2, 0), d_b_im_t.transpose(1, 2, 0)

    dproj, s_qkv = _dproj_assemble(du, dqkv, dgs, dga, cos_t, sin_t, name="dproj_assemble")
    gs["b_in"] = jnp.concatenate([s_u, *s_qkv, s_gs, s_ga], axis=1)
    gw["w_in"] = wgrad(h0m, dproj, out_shards=True, name="in_proj_dw")
    sent = send(2, gw)
    dh0 = _mm(dproj, wts["w_in"], tb=True, b_shards=True, after=sent, name="in_proj_dx")
    grad_x, gs["ln_in_g"], gs["ln_in_b"], _ = _ln_bwd(dr1, dh0, xh0, rs0, sm["ln_in_g"], alpha=DEEPNORM_ALPHA,
                                                      operand=False, name="ln_in_bwd")
    return loss_row, grad_x, gs


N_PEER = N_DEV - 1
_IN_HBM = pl.BlockSpec(memory_space=pltpu.HBM)
_IN_SEMAPHORE = pl.BlockSpec(memory_space=pltpu.SEMAPHORE)


def _device_index():
    return 4 * lax.axis_index("x") + 2 * lax.axis_index("y") + lax.axis_index("c")


def _exchange_copies(src_refs, land_refs, send_sems, recv_sems, scatter):
    x, y, c = lax.axis_index("x"), lax.axis_index("y"), lax.axis_index("c")
    me = 4 * x + 2 * y + c
    pairs = []
    for a, (src_ref, land_ref) in enumerate(zip(src_refs, land_refs)):
        for kk in range(1, N_DEV):
            px = (x + (kk >> 2)) % 2
            py = (y + ((kk >> 1) & 1)) % 2
            pc = (c + (kk & 1)) % 2
            peer = 4 * px + 2 * py + pc
            sem = a * N_PEER + kk - 1
            src = src_ref.at[peer] if scatter else src_ref

            def copy(dst, src=src, sem=sem, px=px, py=py, pc=pc):
                return pltpu.make_async_remote_copy(
                    src_ref=src, dst_ref=dst, send_sem=send_sems.at[sem], recv_sem=recv_sems.at[sem],
                    device_id=(px, py, pc), device_id_type=pl.DeviceIdType.MESH)

            pairs.append((functools.partial(copy, land_ref.at[me]), functools.partial(copy, land_ref.at[peer])))
    return pairs


def _exchange_start(srcs, *, scatter, name, after=None):
    n_arr = len(srcs)
    lands = [lax.empty((N_DEV,) + tuple(s.shape[1:] if scatter else s.shape), s.dtype) for s in srcs]
    n_in = 2 * n_arr + (after is not None)

    def body(*refs):
        send_sems, recv_sems = refs[n_in], refs[n_in + 1]
        for sent, _ in _exchange_copies(refs[:n_arr], refs[n_arr:2 * n_arr], send_sems, recv_sems, scatter):
            sent().start()
        refs[-1][...] = jnp.zeros_like(refs[-1])

    through = [pltpu.HBM(t.shape, t.dtype) for t in (*srcs, *lands)]
    res = pl.pallas_call(
        body, name=name,
        out_shape=(pltpu.SemaphoreType.DMA((n_arr * N_PEER,)), pltpu.SemaphoreType.DMA((n_arr * N_PEER,)), *through,
                   jax.ShapeDtypeStruct((8, LANES), F32)),
        in_specs=[_IN_HBM] * (2 * n_arr) + [pl.BlockSpec(memory_space=pl.ANY)] * (after is not None),
        out_specs=(_IN_SEMAPHORE, _IN_SEMAPHORE, *[_IN_HBM] * (2 * n_arr), pl.BlockSpec(memory_space=pltpu.VMEM)),
        input_output_aliases={i: 2 + i for i in range(2 * n_arr)},
        compiler_params=pltpu.CompilerParams(has_side_effects=pltpu.SideEffectType.DATAFLOW_SIDE_EFFECTING),
    )(*[pltpu.with_memory_space_constraint(t, pltpu.HBM) for t in (*srcs, *lands)],
      *([after] if after is not None else []))
    return (res[0], res[1], res[2:2 + n_arr], res[2 + n_arr:2 + 2 * n_arr], scatter), res[-1]


def _exchange_wait(handle, *, after, name):
    send_sems, recv_sems, srcs, lands, scatter = handle
    n_arr = len(srcs)
    after = list(after)

    def body(*refs):
        for sent, received in _exchange_copies(refs[:n_arr], refs[n_arr:2 * n_arr], refs[2 * n_arr], refs[2 * n_arr + 1],
                                               scatter):
            sent().wait_send()
            received().wait_recv()

    res = pl.pallas_call(
        body, name=name, out_shape=tuple(pltpu.HBM(t.shape, t.dtype) for t in (*srcs, *lands)),
        in_specs=[_IN_HBM] * (2 * n_arr) + [_IN_SEMAPHORE, _IN_SEMAPHORE] + [pl.BlockSpec(memory_space=pl.ANY)] * len(after),
        out_specs=tuple([_IN_HBM] * (2 * n_arr)), input_output_aliases={i: i for i in range(2 * n_arr)},
        compiler_params=pltpu.CompilerParams(has_side_effects=pltpu.SideEffectType.DATAFLOW_SIDE_EFFECTING),
    )(*srcs, *lands, send_sems, recv_sems, *after)
    return res[n_arr:]


def _with_own_slot(land, own):
    return lax.dynamic_update_slice_in_dim(land, own[None], _device_index(), axis=0)


def _reduce_adamw(gstack, w, m, v, *, name, tr=128):
    n_rows, cols = w.shape
    tr = min(tr, n_rows)
    assert n_rows % tr == 0, (name, n_rows, tr)

    def body(g_ref, w_ref, m_ref, v_ref, go_ref, d_ref, mo_ref, vo_ref):
        g = g_ref[0].astype(F32)
        for dev in range(1, N_DEV):
            g = g + g_ref[dev].astype(F32)
        m_new = ADAM_B1 * m_ref[...] + (1.0 - ADAM_B1) * g
        v_new = ADAM_B2 * v_ref[...] + (1.0 - ADAM_B2) * jnp.square(g)
        m_hat = m_new / (1.0 - ADAM_B1 ** ADAM_STEP)
        v_hat = v_new / (1.0 - ADAM_B2 ** ADAM_STEP)
        go_ref[...] = g
        d_ref[...] = -ADAM_LR * (m_hat / (jnp.sqrt(v_hat) + ADAM_EPS) + ADAM_WD * w_ref[...])
        mo_ref[...] = m_new
        vo_ref[...] = v_new

    flat = pl.BlockSpec((tr, cols), lambda i: (i, 0))
    shape = jax.ShapeDtypeStruct((n_rows, cols), F32)
    return pl.pallas_call(
        body, name=name, grid=(n_rows // tr,),
        in_specs=[pl.BlockSpec((N_DEV, tr, cols), lambda i: (0, i, 0)), flat, flat, flat],
        out_specs=[flat] * 4, out_shape=[shape] * 4, compiler_params=_params("parallel"),
    )(gstack, w, m, v)


def _pack(parts, dtype):
    flat = jnp.concatenate([p.reshape(-1).astype(dtype) for p in parts])
    unit = PACK_COLS * PACK_ROW_ALIGN
    total = -(-flat.shape[0] // unit) * unit
    return jnp.pad(flat, (0, total - flat.shape[0])).reshape(-1, PACK_COLS)


def _unpack(packed, shapes):
    flat = packed.reshape(-1)
    out, off = [], 0
    for s in shapes:
        size = int(np.prod(s))
        out.append(flat[off:off + size].reshape(s))
        off += size
    return out


def kernel(x, mem, positions, ln_in_g, ln_in_b, w_in, b_in, ssm_log_dt, ssm_a_re, ssm_a_im, ssm_b_re, ssm_b_im, ssm_c_re, ssm_c_im, ssm_d, w_glu, b_glu, w_att_up, w_mix_out, b_mix_out, ln1_g, ln1_b, w_xq, w_xkv, w_xo, ln2_g, ln2_b, w_ff1, b_ff1, w_ff2, b_ff2, ln3_g, ln3_b, loss_target, m_ln_in_g, m_ln_in_b, m_w_in, m_b_in, m_ssm_log_dt, m_ssm_a_re, m_ssm_a_im, m_ssm_b_re, m_ssm_b_im, m_ssm_c_re, m_ssm_c_im, m_ssm_d, m_w_glu, m_b_glu, m_w_att_up, m_w_mix_out, m_b_mix_out, m_ln1_g, m_ln1_b, m_w_xq, m_w_xkv, m_w_xo, m_ln2_g, m_ln2_b, m_w_ff1, m_b_ff1, m_w_ff2, m_b_ff2, m_ln3_g, m_ln3_b, v_ln_in_g, v_ln_in_b, v_w_in, v_b_in, v_ssm_log_dt, v_ssm_a_re, v_ssm_a_im, v_ssm_b_re, v_ssm_b_im, v_ssm_c_re, v_ssm_c_im, v_ssm_d, v_w_glu, v_b_glu, v_w_att_up, v_w_mix_out, v_b_mix_out, v_ln1_g, v_ln1_b, v_w_xq, v_w_xkv, v_w_xo, v_ln2_g, v_ln2_b, v_w_ff1, v_b_ff1, v_w_ff2, v_b_ff2, v_ln3_g, v_ln3_b):
    given = dict(locals())
    w_arg = {n: given[n] for n in WEIGHTS}
    m_arg = {n: given["m_" + n] for n in WEIGHTS}
    v_arg = {n: given["v_" + n] for n in WEIGHTS}

    shards = {n: w_arg[n][0].astype(MXU_DTYPE) for n in BIG}
    gathers, token = [], None
    for i, names in enumerate(GATHER_GROUPS):
        handle, token = _exchange_start([shards[n] for n in names], scatter=False, after=token, name=f"gather_start_{i}")
        gathers.append(handle)

    small_wmv = [_pack([d[n] for n in SMALL], F32) for d in (w_arg, m_arg, v_arg)]

    def fetch(i, after):
        lands = _exchange_wait(gathers[i], after=after + (small_wmv if i == 0 else []), name=f"gather_wait_{i}")
        full = {n: _with_own_slot(land, shards[n]) for n, land in zip(GATHER_GROUPS[i], lands)}
        return {n: t if n in BIG_COL_SHARDED else t.reshape(-1, t.shape[-1]) for n, t in full.items()}

    scatters = {}

    def send(i, gw):
        slots = [gw[n] if n in BIG_COL_SHARDED else gw[n].reshape(N_DEV, -1, gw[n].shape[-1]) for n in SCATTER_GROUPS[i]]
        handle, sent = _exchange_start(slots, scatter=True, name=f"scatter_start_{i}")
        scatters[i] = (handle, slots)
        return sent

    sm = {}
    for n in SMALL:
        t = w_arg[n]
        if n.startswith("ssm_") and n not in ("ssm_d", "ssm_log_dt"):
            sm[n] = t[0]
        else:
            sm[n] = t.reshape(1, -1)

    loss_row, grad_x, gs = _local_grads(x[0], mem[0], positions.reshape(-1, 1), loss_target[0], sm, fetch, send, token)
    loss = lax.psum(loss_row[0, 0], ("x", "y", "c"))
    small = _pack([gs[n] for n in SMALL], WIRE_DTYPE)
    small_handle, _ = _exchange_start([small], scatter=False, name="small_start")

    results = [{}, {}, {}, {}]
    done = grad_x
    for i, names in enumerate(SCATTER_GROUPS):
        handle, slots = scatters[i]
        lands = _exchange_wait(handle, after=[done], name=f"scatter_wait_{i}")
        for n, land, slot in zip(names, lands, slots):
            own = lax.dynamic_index_in_dim(slot, _device_index(), axis=0, keepdims=False)
            res = _reduce_adamw(_with_own_slot(land, own), w_arg[n][0], m_arg[n][0], v_arg[n][0], name="adamw_" + n)
            done = res[0]
            for d, r in zip(results, res):
                d[n] = r[None]
    small_stack = _with_own_slot(_exchange_wait(small_handle, after=[done], name="small_wait")[0], small)
    small_shapes = [w_arg[n].shape for n in SMALL]
    res = _reduce_adamw(small_stack, *small_wmv, name="adamw_small")
    for d, r in zip(results, res):
        d.update(zip(SMALL, _unpack(r, small_shapes)))
    out = [loss, grad_x[None]]
    for d in results:
        out += [d[n] for n in WEIGHTS]
    return tuple(out)
```

```python
import functools

import numpy as np
import jax
import jax.numpy as jnp
from jax import lax
from jax.experimental import pallas as pl
from jax.experimental.pallas import tpu as pltpu

F32 = jnp.float32
MXU_DTYPE = jnp.bfloat16
WIRE_DTYPE = jnp.bfloat16
VMEM_LIMIT_BYTES = 48 * 1024 * 1024
LANES = 128

N_DEV = 8
D_MODEL = 1024
SSM_GROUP = 16
SSM_WIDTH = 768
SSM_GROUPS = SSM_WIDTH // SSM_GROUP
SSM_STATE = 64
SSM_CH = SSM_GROUPS * SSM_STATE
SSM_TILES = SSM_WIDTH // LANES
GROUPS_PER_TILE = LANES // SSM_GROUP
STATE_VREG_ROWS = SSM_CH // LANES
ATT_HEAD_DIM = 64
ATT_HEADS_PER_GROUP = 4
ATT_MERGED = ATT_HEADS_PER_GROUP * ATT_HEAD_DIM
LANE_HALVES = ATT_MERGED // LANES
DILATIONS = (1, 4, 16)
ATT_BLK = 128
ATT_SCALE = ATT_HEAD_DIM ** -0.5
ROT_DIM = ATT_HEAD_DIM // 4
ROPE_THETA = 500000.0
XATT_HEADS = 4
XATT_HEAD_DIM = D_MODEL // XATT_HEADS
XATT_SCALE = XATT_HEAD_DIM ** -0.5
DEEPNORM_ALPHA = 2.0 ** 0.25
LN_EPS = 1e-5
NEG_INF = -1e30
OFF_Q_BLK, OFF_K_BLK, OFF_V_BLK = 3, 6, 9
OFF_GS_BLK, OFF_GA_BLK = 3, 4

ADAM_LR = 0.001
ADAM_B1 = 0.9
ADAM_B2 = 0.999
ADAM_EPS = 1e-08
ADAM_WD = 0.01
ADAM_STEP = 10

BIG = ("w_in", "w_glu", "w_att_up", "w_mix_out", "w_xq", "w_xkv", "w_xo", "w_ff1", "w_ff2")
BIG_COL_SHARDED = ("w_in", "w_glu", "w_att_up", "w_xkv", "w_ff1")
WEIGHTS = ("ln_in_g", "ln_in_b", "w_in", "b_in", "ssm_log_dt", "ssm_a_re", "ssm_a_im", "ssm_b_re", "ssm_b_im",
           "ssm_c_re", "ssm_c_im", "ssm_d", "w_glu", "b_glu", "w_att_up", "w_mix_out", "b_mix_out", "ln1_g", "ln1_b",
           "w_xq", "w_xkv", "w_xo", "ln2_g", "ln2_b", "w_ff1", "b_ff1", "w_ff2", "b_ff2", "ln3_g", "ln3_b")
SMALL = tuple(n for n in WEIGHTS if n not in BIG)
PACK_COLS = 1024
PACK_ROW_ALIGN = 256


def _params(*sem):
    return pltpu.CompilerParams(dimension_semantics=sem, vmem_limit_bytes=VMEM_LIMIT_BYTES)


def _dot(a, b, ca, cb):
    return lax.dot_general(a.astype(MXU_DTYPE), b.astype(MXU_DTYPE), (((ca,), (cb,)), ((), ())),
                           preferred_element_type=F32)


def _fit(dim, pref):
    if dim <= pref:
        return dim
    best = max(t for t in range(LANES, pref + 1, LANES) if dim % t == 0)
    return best


def _mm(a, b, *, name, ta=False, tb=False, bias=None, out_dtype=F32, b_shards=False, out_shards=False, after=None,
        also=None, gate=None, colsum=False, tm=2048, tn=1024, tk=1024):
    m, k = (a.shape[1], a.shape[0]) if ta else a.shape
    order = (lambda f: (lambda j, i, kk: f(i, j, kk))) if colsum else (lambda f: f)
    spec = lambda shape, f: pl.BlockSpec(shape, order(f))
    if b_shards:
        n_sh, rows, n_loc = b.shape
        if tb:
            n, tn, tk = rows, _fit(rows, tn), n_loc
            assert k == n_sh * n_loc, (name, k, b.shape)
            b_spec = spec((1, tn, tk), lambda i, j, kk: (kk, j, 0))
        else:
            n, tn, tk = n_sh * n_loc, n_loc, _fit(k, tk)
            b_spec = spec((1, tk, tn), lambda i, j, kk: (j, kk, 0))
    else:
        n = b.shape[0] if tb else b.shape[1]
        tn = n // N_DEV if out_shards else _fit(n, tn)
        tk = _fit(k, tk)
        b_spec = spec((tn, tk), lambda i, j, kk: (j, kk)) if tb else spec((tk, tn), lambda i, j, kk: (kk, j))
    tm = _fit(m, tm)
    nk = k // tk
    a_spec = spec((tk, tm), lambda i, j, kk: (kk, i)) if ta else spec((tm, tk), lambda i, j, kk: (i, kk))
    tile = spec((tm, tn), lambda i, j, kk: (i, j))
    in_specs, args = [a_spec, b_spec], [a, b]
    if bias is not None:
        in_specs.append(spec((1, tn), lambda i, j, kk: (0, j)))
        args.append(bias)
    if gate is not None:
        in_specs.append(tile)
        args.append(gate[0])
    if after is not None:
        in_specs.append(pl.BlockSpec(memory_space=pl.ANY))
        args.append(after)
    n_in = len(args)
    if out_shards:
        assert n == N_DEV * tn, (name, n, tn)
        out_specs = [spec((1, tm, tn), lambda i, j, kk: (j, i, 0))]
        out_shape = [jax.ShapeDtypeStruct((N_DEV, m, tn), out_dtype)]
    else:
        out_specs = [tile]
        out_shape = [jax.ShapeDtypeStruct((m, n), out_dtype)]
    if also is not None:
        out_specs.append(tile)
        out_shape.append(jax.ShapeDtypeStruct((m, n), also[1]))
    if colsum:
        out_specs.append(spec((1, tn), lambda i, j, kk: (0, j)))
        out_shape.append(jax.ShapeDtypeStruct((1, n), F32))

    def body(*refs):
        a_ref, b_ref = refs[0], refs[1]
        o_ref = refs[n_in]

        def product():
            return _dot(a_ref[...], b_ref[0] if b_shards else b_ref[...], 0 if ta else 1, 1 if tb else 0)

        def finish(r):
            if bias is not None:
                r = r + refs[2][...]
            if gate is not None:
                r = r * gate[1](refs[2 + (bias is not None)][...])
            if out_shards:
                o_ref[0] = r.astype(o_ref.dtype)
            else:
                o_ref[...] = r.astype(o_ref.dtype)
            if also is not None:
                refs[n_in + 1][...] = also[0](r).astype(also[1])
            if colsum:
                s_ref = refs[n_in + 1 + (also is not None)]

                @pl.when(pl.program_id(1) == 0)
                def _():
                    s_ref[...] = jnp.zeros_like(s_ref)

                s_ref[...] += _colsum(r)

        if nk == 1:
            finish(product())
            return
        acc_ref = refs[-1]
        kk = pl.program_id(2)

        @pl.when(kk == 0)
        def _():
            acc_ref[...] = jnp.zeros_like(acc_ref)

        acc_ref[...] += product()

        @pl.when(kk == nk - 1)
        def _():
            finish(acc_ref[...])

    grid = (n // tn, m // tm, nk) if colsum else (m // tm, n // tn, nk)
    res = pl.pallas_call(
        body, name=name, grid=grid, in_specs=in_specs, out_specs=out_specs, out_shape=out_shape,
        scratch_shapes=[pltpu.VMEM((tm, tn), F32)] if nk > 1 else [],
        compiler_params=_params("parallel", "arbitrary" if colsum else "parallel", "arbitrary"),
    )(*args)
    return res[0] if len(res) == 1 else res


def _rowcall(fn, rows, fulls, row_outs, acc_outs=(), *, n_rows, tm, name, after=None):
    n_r, n_f, n_o, n_a = len(rows), len(fulls), len(row_outs), len(acc_outs)
    n_in = n_r + n_f + (after is not None)
    assert n_rows % tm == 0, (name, n_rows, tm)

    def body(*refs):
        res = fn(*[r[...] for r in refs[:n_r + n_f]])
        res = tuple(res) if isinstance(res, (tuple, list)) else (res,)
        o_refs = refs[n_in:n_in + n_o]
        a_refs = refs[n_in + n_o:]
        for o_ref, val in zip(o_refs, res[:n_o]):
            o_ref[...] = val.astype(o_ref.dtype)
        if n_a:
            @pl.when(pl.program_id(0) == 0)
            def _():
                for a_ref in a_refs:
                    a_ref[...] = jnp.zeros_like(a_ref)

            for a_ref, val in zip(a_refs, res[n_o:]):
                a_ref[...] += val

    in_specs = [pl.BlockSpec((tm, w), functools.partial(lambda i, cb: (i, cb), cb=cb)) for _, w, cb in rows]
    in_specs += [pl.BlockSpec(f.shape, functools.partial(lambda i, nd: (0,) * nd, nd=f.ndim)) for f in fulls]
    in_specs += [pl.BlockSpec(memory_space=pl.ANY)] * (after is not None)
    out_specs = [pl.BlockSpec((tm, w), lambda i: (i, 0)) for w, _ in row_outs]
    out_specs += [pl.BlockSpec((1, w), lambda i: (0, 0)) for w in acc_outs]
    out_shape = [jax.ShapeDtypeStruct((n_rows, w), dt) for w, dt in row_outs]
    out_shape += [jax.ShapeDtypeStruct((1, w), F32) for w in acc_outs]
    return pl.pallas_call(
        body, name=name, grid=(n_rows // tm,), in_specs=in_specs, out_specs=out_specs, out_shape=out_shape,
        compiler_params=_params("arbitrary" if n_a else "parallel"),
    )(*[r[0] for r in rows], *fulls, *([after] if after is not None else []))


def _colsum(v):
    return jnp.sum(v, axis=0, keepdims=True)


def _ln_fwd(a, r, g, b, *, alpha, name):
    n_rows, d = a.shape

    def fn(*t):
        xin = t[0] if alpha == 1.0 else alpha * t[0]
        if r is not None:
            xin = xin + t[1]
        gv, bv = t[-2], t[-1]
        mu = jnp.mean(xin, axis=-1, keepdims=True)
        xc = xin - mu
        var = jnp.mean(xc * xc, axis=-1, keepdims=True)
        rstd = lax.rsqrt(var + LN_EPS)
        xh = xc * rstd
        y = xh * gv + bv
        return y, xh, rstd, y

    rows = [(a, d, 0)] + ([(r, d, 0)] if r is not None else [])
    return _rowcall(fn, rows, [g, b], [(d, F32), (d, F32), (1, F32), (d, MXU_DTYPE)], n_rows=n_rows, tm=256, name=name)


def _ln_bwd(dya, dyb, xh, rstd, g, *, alpha, name, operand=True):
    n_rows, d = xh.shape

    def fn(da, db, xhv, rs, gv):
        dy = alpha * da + db
        dyg = dy * gv
        m1 = jnp.mean(dyg, axis=-1, keepdims=True)
        m2 = jnp.mean(dyg * xhv, axis=-1, keepdims=True)
        dx = rs * (dyg - m1 - xhv * m2)
        return (dx,) + ((dx,) if operand else ()) + (_colsum(dy * xhv), _colsum(dy), _colsum(dx))

    rows = [(dya, d, 0), (dyb, d, 0), (xh, d, 0), (rstd, 1, 0)]
    return _rowcall(fn, rows, [g], [(d, F32)] + [(d, MXU_DTYPE)] * operand, [d, d, d], n_rows=n_rows, tm=256, name=name)


def _ln_loss_bwd(a, r, target, g, b, *, alpha, name):
    n_rows, d = a.shape

    def fn(av, rv, tv, gv, bv):
        xin = alpha * av + rv
        mu = jnp.mean(xin, axis=-1, keepdims=True)
        xc = xin - mu
        var = jnp.mean(xc * xc, axis=-1, keepdims=True)
        rs = lax.rsqrt(var + LN_EPS)
        xh = xc * rs
        diff = xh * gv + bv - tv
        part = jnp.sum(jnp.sum(diff * diff, axis=1, keepdims=True), axis=0, keepdims=True) * (0.5 / d)
        dy = diff * (1.0 / d)
        dyg = dy * gv
        m1 = jnp.mean(dyg, axis=-1, keepdims=True)
        m2 = jnp.mean(dyg * xh, axis=-1, keepdims=True)
        dx = rs * (dyg - m1 - xh * m2)
        return dx, dx, _colsum(dy * xh), _colsum(dy), _colsum(dx), jnp.broadcast_to(part, (1, LANES))

    return _rowcall(fn, [(a, d, 0), (r, d, 0), (target, d, 0)], [g, b], [(d, F32), (d, MXU_DTYPE)], [d, d, d, LANES],
                    n_rows=n_rows, tm=256, name=name)


def _rope_lane_constants():
    lane = np.arange(ATT_MERGED)
    in_head = lane % ATT_HEAD_DIM
    sign = np.where(in_head < ROT_DIM // 2, -1.0, np.where(in_head < ROT_DIM, 1.0, 0.0)).astype(np.float32)
    inv_freq = ROPE_THETA ** (-jnp.arange(0, ROT_DIM, 2, dtype=F32) / ROT_DIM)
    return inv_freq[lane % (ROT_DIM // 2)].reshape(1, ATT_MERGED), jnp.asarray(sign).reshape(1, ATT_MERGED)


def _rope_tables(pos_col, *, name, after=None):
    inv_lane, sign = _rope_lane_constants()

    def fn(pos, inv, sg):
        ang = pos.astype(F32) * inv
        return jnp.where(sg != 0.0, jnp.cos(ang), 1.0), sg * jnp.sin(ang)

    return _rowcall(fn, [(pos_col, 1, 0)], [inv_lane, sign], [(ATT_MERGED, F32), (ATT_MERGED, F32)],
                    n_rows=pos_col.shape[0], tm=512, name=name, after=after)


def _rot_partner(t):
    lane = lax.broadcasted_iota(jnp.int32, t.shape, 1)
    width = t.shape[1]
    return jnp.where((lane & (ROT_DIM // 2)) == 0, pltpu.roll(t, width - ROT_DIM // 2, 1), pltpu.roll(t, ROT_DIM // 2, 1))


def _rope(t, cos_t, sin_t):
    return t * cos_t + _rot_partner(t) * sin_t


def _rope_transpose(dt, cos_t, sin_t):
    return dt * cos_t + _rot_partner(dt * sin_t)


def _strided_rows(r, count, stride):
    return pl.ds(r, count) if stride == 1 else pl.ds(r, count, stride=stride)


def _qkv_split(proj, cos_t, sin_t, *, name, tm=512):
    n_rows = proj.shape[0]
    n_g = len(DILATIONS)

    def body(*refs):
        n_src = LANE_HALVES * 3 * n_g
        src, tables, dst = refs[:n_src], refs[n_src:n_src + 2 * LANE_HALVES], refs[n_src + 2 * LANE_HALVES:]
        for kind in range(3):
            for g, dil in enumerate(DILATIONS):
                for half in range(LANE_HALVES):
                    x_ref, o_ref = src[(kind * n_g + g) * LANE_HALVES + half], dst[kind * n_g + g]
                    cos_ref, sin_ref = tables[half], tables[LANE_HALVES + half]
                    for r in range(dil):
                        rows = _strided_rows(r, tm // dil, dil)
                        t = x_ref[rows, :]
                        if kind < 2:
                            t = _rope(t, cos_ref[rows, :], sin_ref[rows, :])
                        lo = r * ATT_MERGED + half * LANES
                        o_ref[:, lo:lo + LANES] = t.astype(o_ref.dtype)

    half_spec = lambda cb: pl.BlockSpec((tm, LANES), functools.partial(lambda i, cb: (i, cb), cb=cb))
    in_specs = [half_spec((off + g) * LANE_HALVES + half)
                for off in (OFF_Q_BLK, OFF_K_BLK, OFF_V_BLK) for g in range(n_g) for half in range(LANE_HALVES)]
    in_specs += [half_spec(half) for _ in range(2) for half in range(LANE_HALVES)]
    out_specs = [pl.BlockSpec((tm // dil, dil * ATT_MERGED), lambda i: (i, 0)) for _ in range(3) for dil in DILATIONS]
    out_shape = [jax.ShapeDtypeStruct((n_rows // dil, dil * ATT_MERGED), MXU_DTYPE) for _ in range(3) for dil in DILATIONS]
    outs = pl.pallas_call(
        body, name=name, grid=(n_rows // tm,), in_specs=in_specs, out_specs=out_specs, out_shape=out_shape,
        compiler_params=_params("parallel"),
    )(*[proj] * (LANE_HALVES * 3 * n_g), *[cos_t] * LANE_HALVES, *[sin_t] * LANE_HALVES)
    return outs[:n_g], outs[n_g:2 * n_g], outs[2 * n_g:]


def _mix(gs, ga, z1, z2, b_att):
    return jax.nn.sigmoid(gs) * (z1 * jax.nn.sigmoid(z2)) + jax.nn.sigmoid(ga) * b_att


def _mix_rows(proj, z, b_att):
    return [(proj, D_MODEL, OFF_GS_BLK), (proj, D_MODEL, OFF_GA_BLK), (z, D_MODEL, 0), (z, D_MODEL, 1), (b_att, D_MODEL, 0)]


def _mix_fwd(proj, z, b_att, *, name):
    return _rowcall(_mix, _mix_rows(proj, z, b_att), [], [(D_MODEL, MXU_DTYPE)],
                    n_rows=proj.shape[0], tm=256, name=name)[0]


def _mix_bwd(dmixed, proj, z, b_att, *, name):
    def fn(dm, gs, ga, z1, z2, ba):
        _, vjp = jax.vjp(_mix, gs, ga, z1, z2, ba)
        dgs, dga, dz1, dz2, dba = vjp(dm)
        dz = jnp.concatenate([dz1, dz2], axis=1)
        return dgs, dga, dz, dba, _colsum(dgs), _colsum(dga), _colsum(dz)

    rows = [(dmixed, D_MODEL, 0)] + _mix_rows(proj, z, b_att)
    widths = [D_MODEL, D_MODEL, 2 * D_MODEL, D_MODEL]
    return _rowcall(fn, rows, [], [(w, MXU_DTYPE) for w in widths], widths[:3], n_rows=proj.shape[0], tm=256, name=name)


def _gelu_bwd(dgy, y, proj, *, name):
    def fn(dg, yv, u):
        _, vjp = jax.vjp(jax.nn.gelu, yv)
        dy = vjp(dg)[0]
        return dy, _colsum(dy * u)

    return _rowcall(fn, [(dgy, SSM_WIDTH, 0), (y, SSM_WIDTH, 0), (proj, SSM_WIDTH, 0)], [], [(SSM_WIDTH, F32)],
                    [SSM_WIDTH], n_rows=y.shape[0], tm=512, name=name)


HEAD_ROWS = ATT_HEADS_PER_GROUP * ATT_BLK


def _head_masks(rows):
    head = lax.broadcasted_iota(jnp.int32, (rows, ATT_MERGED), 1) >> (ATT_HEAD_DIM.bit_length() - 1)
    return [head == h for h in range(ATT_HEADS_PER_GROUP)]


def _stack_heads(t, masks):
    return jnp.concatenate([jnp.where(m, t, jnp.zeros_like(t)) for m in masks], axis=0)


def _unstack_heads(t4, masks):
    blocks = [t4[h * ATT_BLK:(h + 1) * ATT_BLK] for h in range(ATT_HEADS_PER_GROUP)]
    return jnp.where(masks[0], blocks[0], jnp.where(masks[1], blocks[1], jnp.where(masks[2], blocks[2], blocks[3])))


def _head_column(stats, first):
    return jnp.concatenate([stats[:, first + h:first + h + 1] for h in range(ATT_HEADS_PER_GROUP)], axis=0)


def _band_mask(first_key):
    qi = lax.broadcasted_iota(jnp.int32, (HEAD_ROWS, 2 * ATT_BLK), 0) & (ATT_BLK - 1)
    ki = lax.broadcasted_iota(jnp.int32, (HEAD_ROWS, 2 * ATT_BLK), 1)
    steps = qi + ATT_BLK - ki
    return (steps >= 0) & (steps <= ATT_BLK) & (ki >= first_key)


def _dil_fwd(q, k, v, dil, *, name):
    n_blk = q.shape[0] // ATT_BLK
    cur = pl.BlockSpec((ATT_BLK, ATT_MERGED), lambda r, n: (n, r))
    prev = pl.BlockSpec((ATT_BLK, ATT_MERGED), lambda r, n: (jnp.maximum(n - 1, 0), r))

    def body(q_ref, kp_ref, kc_ref, vp_ref, vc_ref, o_ref, l_ref):
        masks = _head_masks(ATT_BLK)
        valid = _band_mask(jnp.where(pl.program_id(1) > 0, 0, ATT_BLK))
        keys = jnp.concatenate([kp_ref[...], kc_ref[...]], axis=0)
        vals = jnp.concatenate([vp_ref[...], vc_ref[...]], axis=0)
        s = jnp.where(valid, _dot(_stack_heads(q_ref[...], masks), keys, 1, 1) * ATT_SCALE, NEG_INF)
        m = jnp.max(s, axis=-1, keepdims=True)
        p = jnp.exp(s - m)
        den = jnp.sum(p, axis=-1, keepdims=True)
        o_ref[...] = _unstack_heads(_dot(p, vals, 1, 0) / den, masks)
        l_ref[...] = _unstack_heads(jnp.broadcast_to(m + jnp.log(den), (HEAD_ROWS, ATT_MERGED)), masks)

    shape = jax.ShapeDtypeStruct(q.shape, F32)
    return pl.pallas_call(
        body, name=name, grid=(dil, n_blk), in_specs=[cur, prev, cur, prev, cur], out_specs=[cur, cur],
        out_shape=[shape, shape], compiler_params=_params("parallel", "parallel"),
    )(q, k, k, v, v)


def _att_merge(outs, lses, *, name, tm=512):
    n_g = len(outs)
    n_rows = outs[0].shape[0] * DILATIONS[0]

    def body(*refs):
        src, (att_ref, lse_ref), tmp = refs[:2 * n_g], refs[2 * n_g:2 * n_g + 2], refs[2 * n_g + 2:]
        vals = []
        for idx, src_ref in enumerate(src):
            dil = DILATIONS[idx % n_g]
            if dil == 1:
                vals.append(src_ref[...])
                continue
            for r in range(dil):
                for half in range(LANE_HALVES):
                    lo = r * ATT_MERGED + half * LANES
                    tmp[LANE_HALVES * idx + half][_strided_rows(r, tm // dil, dil), :] = src_ref[:, lo:lo + LANES]
            vals.append(jnp.concatenate([tmp[LANE_HALVES * idx + half][...] for half in range(LANE_HALVES)], axis=1))
        o, l = vals[:n_g], vals[n_g:]
        m = functools.reduce(jnp.maximum, l)
        e = [jnp.exp(li - m) for li in l]
        z = functools.reduce(jnp.add, e)
        att_ref[...] = functools.reduce(jnp.add, [(ei / z) * oi for ei, oi in zip(e, o)])
        lse_ref[...] = m + jnp.log(z)

    in_specs = [pl.BlockSpec((tm // dil, dil * ATT_MERGED), lambda i: (i, 0)) for _ in range(2) for dil in DILATIONS]
    row = pl.BlockSpec((tm, ATT_MERGED), lambda i: (i, 0))
    shape = jax.ShapeDtypeStruct((n_rows, ATT_MERGED), F32)
    return pl.pallas_call(
        body, name=name, grid=(n_rows // tm,), in_specs=in_specs, out_specs=[row, row], out_shape=[shape, shape],
        scratch_shapes=[pltpu.VMEM((tm, LANES), F32)] * (LANE_HALVES * 2 * n_g), compiler_params=_params("parallel"),
    )(*outs, *lses)


def _att_stats(datt, att, lse, *, name):
    n_rows = datt.shape[0]

    def fn(d, a, l):
        prod = d * a
        lane = lax.broadcasted_iota(jnp.int32, (d.shape[0], LANES), 1)
        out = jnp.zeros((d.shape[0], LANES), F32)
        for h in range(ATT_HEADS_PER_GROUP):
            lo = h * ATT_HEAD_DIM
            out = jnp.where(lane == h, l[:, lo:lo + 1], out)
            delta = jnp.sum(prod[:, lo:lo + ATT_HEAD_DIM], axis=-1, keepdims=True)
            out = jnp.where(lane == ATT_HEADS_PER_GROUP + h, delta, out)
        return out

    rows = [(t, ATT_MERGED, 0) for t in (datt, att, lse)]
    return _rowcall(fn, rows, [], [(LANES, F32)], n_rows=n_rows, tm=512, name=name)[0]


def _dil_bwd(q, k, v, datt, stats, dil, *, name):
    n_rows = datt.shape[0]
    n_blk = n_rows // dil // ATT_BLK
    span = ATT_BLK * dil
    cur = pl.BlockSpec((ATT_BLK, ATT_MERGED), lambda n, r: (n, r))
    prev = pl.BlockSpec((ATT_BLK, ATT_MERGED), lambda n, r: (jnp.maximum(n - 1, 0), r))
    nxt = pl.BlockSpec((ATT_BLK, ATT_MERGED), lambda n, r: (jnp.minimum(n + 1, n_blk - 1), r))
    seq = lambda half, ahead: pl.BlockSpec((span, LANES), lambda n, r: (jnp.minimum(n + ahead, n_blk - 1), half))

    def body(qc_ref, qn_ref, kp_ref, kc_ref, vp_ref, vc_ref, dc0_ref, dc1_ref, dn0_ref, dn1_ref, sc_ref, sn_ref,
             dq0_ref, dq1_ref, dk0_ref, dk1_ref, dv0_ref, dv1_ref):
        n = pl.program_id(0)
        rows = slice(None) if dil == 1 else _strided_rows(pl.program_id(1), ATT_BLK, dil)

        def read(ref0, ref1):
            return jnp.concatenate([ref0[rows, :], ref1[rows, :]], axis=1)

        def write(ref0, ref1, val):
            ref0[rows, :] = val[:, :LANES]
            ref1[rows, :] = val[:, LANES:]

        masks = _head_masks(ATT_BLK)
        valid = _band_mask(jnp.where(n > 0, 0, ATT_BLK))
        qi = lax.broadcasted_iota(jnp.int32, (HEAD_ROWS, ATT_BLK), 0) & (ATT_BLK - 1)
        ki = lax.broadcasted_iota(jnp.int32, (HEAD_ROWS, ATT_BLK), 1)
        valid_next = (ki - qi) >= jnp.where(n < n_blk - 1, 0, ATT_BLK)

        kc, vc = kc_ref[...], vc_ref[...]
        keys = jnp.concatenate([kp_ref[...], kc], axis=0)
        vals = jnp.concatenate([vp_ref[...], vc], axis=0)
        q4 = _stack_heads(qc_ref[...], masks)
        d4 = _stack_heads(read(dc0_ref, dc1_ref).astype(MXU_DTYPE), masks)
        st = sc_ref[rows, :]
        p = jnp.where(valid, jnp.exp(_dot(q4, keys, 1, 1) * ATT_SCALE - _head_column(st, 0)), 0.0)
        ds = p * (_dot(d4, vals, 1, 1) - _head_column(st, ATT_HEADS_PER_GROUP)) * ATT_SCALE
        write(dq0_ref, dq1_ref, _unstack_heads(_dot(ds, keys, 1, 0), masks))

        q4n = _stack_heads(qn_ref[...], masks)
        d4n = _stack_heads(read(dn0_ref, dn1_ref).astype(MXU_DTYPE), masks)
        stn = sn_ref[rows, :]
        p_n = jnp.where(valid_next, jnp.exp(_dot(q4n, kc, 1, 1) * ATT_SCALE - _head_column(stn, 0)), 0.0)
        ds_n = p_n * (_dot(d4n, vc, 1, 1) - _head_column(stn, ATT_HEADS_PER_GROUP)) * ATT_SCALE
        write(dv0_ref, dv1_ref, _dot(p[:, ATT_BLK:], d4, 0, 0) + _dot(p_n, d4n, 0, 0))
        write(dk0_ref, dk1_ref, _dot(ds[:, ATT_BLK:], q4, 0, 0) + _dot(ds_n, q4n, 0, 0))

    shape = jax.ShapeDtypeStruct((n_rows, LANES), F32)
    out = seq(0, 0)
    res = pl.pallas_call(
        body, name=name, grid=(n_blk, dil),
        in_specs=[cur, nxt, prev, cur, prev, cur, seq(0, 0), seq(1, 0), seq(0, 1), seq(1, 1), seq(0, 0), seq(0, 1)],
        out_specs=[out] * 6, out_shape=[shape] * 6, compiler_params=_params("parallel", "arbitrary"),
    )(q, q, k, k, v, v, datt, datt, datt, datt, stats, stats)
    return [(res[2 * i], res[2 * i + 1]) for i in range(3)]


def _dproj_assemble(du, dqkv, dgs, dga, cos_t, sin_t, *, name):
    n_g = len(DILATIONS)

    def fn(*t):
        n_half = LANE_HALVES * 3 * n_g
        du_t, halves, (dgs_t, dga_t, c, s) = t[0], t[1:1 + n_half], t[1 + n_half:]
        parts = [jnp.concatenate(halves[LANE_HALVES * i:LANE_HALVES * (i + 1)], axis=1) for i in range(3 * n_g)]
        for i in range(2 * n_g):
            parts[i] = _rope_transpose(parts[i], c, s)
        cast = [p.astype(MXU_DTYPE) for p in parts]
        return [jnp.concatenate([du_t] + cast + [dgs_t, dga_t], axis=1)] + [_colsum(p) for p in parts]

    rows = [(du, SSM_WIDTH, 0)]
    rows += [(half, LANES, 0) for i in range(3) for g in range(n_g) for half in dqkv[g][i]]
    rows += [(dgs, D_MODEL, 0), (dga, D_MODEL, 0), (cos_t, ATT_MERGED, 0), (sin_t, ATT_MERGED, 0)]
    width = SSM_WIDTH + 3 * n_g * ATT_MERGED + 2 * D_MODEL
    res = _rowcall(fn, rows, [], [(width, MXU_DTYPE)], [ATT_MERGED] * (3 * n_g), n_rows=du.shape[0], tm=256, name=name)
    return res[0], res[1:]


def _xhead(h):
    return slice(h * XATT_HEAD_DIM, (h + 1) * XATT_HEAD_DIM)


def _xatt_probs(qh, kh):
    s = _dot(qh, kh, 1, 1) * XATT_SCALE
    e = jnp.exp(s - jnp.max(s, axis=-1, keepdims=True))
    return e / jnp.sum(e, axis=-1, keepdims=True)


def _xatt_fwd(q, kv, *, name, tm=512):
    n_rows = q.shape[0]
    n_mem = kv.shape[0]

    def body(q_ref, kv_ref, o_ref):
        for h in range(XATT_HEADS):
            sl = _xhead(h)
            p = _xatt_probs(q_ref[:, sl], kv_ref[:, sl])
            o_ref[:, sl] = _dot(p, kv_ref[:, D_MODEL + h * XATT_HEAD_DIM:D_MODEL + (h + 1) * XATT_HEAD_DIM], 1, 0
                                ).astype(o_ref.dtype)

    row = pl.BlockSpec((tm, D_MODEL), lambda i: (i, 0))
    return pl.pallas_call(
        body, name=name, grid=(n_rows // tm,),
        in_specs=[row, pl.BlockSpec((n_mem, 2 * D_MODEL), lambda i: (0, 0))], out_specs=row,
        out_shape=jax.ShapeDtypeStruct((n_rows, D_MODEL), MXU_DTYPE), compiler_params=_params("parallel"),
    )(q, kv)


def _xatt_bwd(q, kv, do, *, name, tm=512):
    n_rows = q.shape[0]
    n_mem = kv.shape[0]

    def body(q_ref, kv_ref, do_ref, dq_ref, dkv_ref):
        @pl.when(pl.program_id(0) == 0)
        def _():
            dkv_ref[...] = jnp.zeros_like(dkv_ref)

        for h in range(XATT_HEADS):
            sl = _xhead(h)
            vsl = slice(D_MODEL + h * XATT_HEAD_DIM, D_MODEL + (h + 1) * XATT_HEAD_DIM)
            qh, kh, doh = q_ref[:, sl], kv_ref[:, sl], do_ref[:, sl]
            p = _xatt_probs(qh, kh)
            dp = _dot(doh, kv_ref[:, vsl], 1, 1)
            ds = p * (dp - jnp.sum(dp * p, axis=-1, keepdims=True)) * XATT_SCALE
            dq_ref[:, sl] = _dot(ds, kh, 1, 0).astype(dq_ref.dtype)
            dkv_ref[:, sl] += _dot(ds, qh, 0, 0)
            dkv_ref[:, vsl] += _dot(p, doh, 0, 0)

    row = pl.BlockSpec((tm, D_MODEL), lambda i: (i, 0))
    full = pl.BlockSpec((n_mem, 2 * D_MODEL), lambda i: (0, 0))
    return pl.pallas_call(
        body, name=name, grid=(n_rows // tm,), in_specs=[row, full, row], out_specs=[row, full],
        out_shape=[jax.ShapeDtypeStruct((n_rows, D_MODEL), MXU_DTYPE), jax.ShapeDtypeStruct((n_mem, 2 * D_MODEL), F32)],
        compiler_params=_params("arbitrary"),
    )(q, kv, do)


def _disc(logdt, a_re, a_im, b_re, b_im):
    dt = jnp.exp(logdt)
    mag = jnp.exp(a_re * dt)
    ab_re = mag * jnp.cos(a_im * dt)
    ab_im = mag * jnp.sin(a_im * dt)
    den = jnp.square(a_re) + jnp.square(a_im)
    nr = ab_re - 1.0
    f_re = (nr * a_re + ab_im * a_im) / den
    f_im = (ab_im * a_re - nr * a_im) / den
    bb_re = f_re[None] * b_re - f_im[None] * b_im
    bb_im = f_re[None] * b_im + f_im[None] * b_re
    return ab_re, ab_im, bb_re, bb_im


def _disc_transpose(logdt, a_re, a_im, b_re, b_im, g_ab_re, g_ab_im, g_bb_re, g_bb_im):
    dt = jnp.exp(logdt)
    mag = jnp.exp(a_re * dt)
    th = a_im * dt
    cs, sn = jnp.cos(th), jnp.sin(th)
    ab_re, ab_im = mag * cs, mag * sn
    den = jnp.square(a_re) + jnp.square(a_im)
    nr = ab_re - 1.0
    f_re = (nr * a_re + ab_im * a_im) / den
    f_im = (ab_im * a_re - nr * a_im) / den
    d_f_re = jnp.sum(g_bb_re * b_re + g_bb_im * b_im, axis=0)
    d_f_im = jnp.sum(g_bb_im * b_re - g_bb_re * b_im, axis=0)
    d_b_re = g_bb_re * f_re[None] + g_bb_im * f_im[None]
    d_b_im = g_bb_im * f_re[None] - g_bb_re * f_im[None]
    d_n_re, d_n_im = d_f_re / den, d_f_im / den
    d_den = -(d_f_re * f_re + d_f_im * f_im) / den
    d_ab_re = g_ab_re + d_n_re * a_re - d_n_im * a_im
    d_ab_im = g_ab_im + d_n_re * a_im + d_n_im * a_re
    d_a_re = d_n_re * nr + d_n_im * ab_im + 2.0 * d_den * a_re
    d_a_im = d_n_re * ab_im - d_n_im * nr + 2.0 * d_den * a_im
    d_mag = d_ab_re * cs + d_ab_im * sn
    d_th = mag * (d_ab_im * cs - d_ab_re * sn)
    d_a_re = d_a_re + d_mag * mag * dt
    d_a_im = d_a_im + d_th * dt
    d_dt = jnp.sum(d_mag * mag * a_re + d_th * a_im, axis=-1, keepdims=True)
    return d_dt * dt, d_a_re, d_a_im, d_b_re, d_b_im


def _whole(fn, args, out_shapes, *, name):
    n_in = len(args)

    def body(*refs):
        res = fn(*[r[...] for r in refs[:n_in]])
        for o_ref, val in zip(refs[n_in:], res):
            o_ref[...] = val

    return pl.pallas_call(body, name=name, out_shape=[jax.ShapeDtypeStruct(s, F32) for s in out_shapes],
                          compiler_params=pltpu.CompilerParams(vmem_limit_bytes=VMEM_LIMIT_BYTES))(*args)


def _tiles_cn(t):
    t = t.reshape(-1, SSM_TILES, GROUPS_PER_TILE, SSM_GROUP, SSM_STATE)
    eye = jnp.eye(GROUPS_PER_TILE, dtype=t.dtype)
    return (t[:, :, :, :, None, :] * eye[:, None, :, None]).reshape(-1, SSM_TILES, LANES, GROUPS_PER_TILE * SSM_STATE)


def _tiles_nc(t):
    t = t.reshape(-1, SSM_TILES, GROUPS_PER_TILE, SSM_STATE, SSM_GROUP)
    eye = jnp.eye(GROUPS_PER_TILE, dtype=t.dtype)
    return (t[:, :, :, :, None, :] * eye[:, None, :, None]).reshape(-1, SSM_TILES, GROUPS_PER_TILE * SSM_STATE, LANES)


def _untile_cn(t):
    t = t.reshape(SSM_TILES, GROUPS_PER_TILE, SSM_GROUP, GROUPS_PER_TILE, SSM_STATE)
    eye = jnp.eye(GROUPS_PER_TILE, dtype=t.dtype)
    return jnp.sum(t * eye[None, :, None, :, None], axis=3).reshape(SSM_GROUPS, SSM_GROUP, SSM_STATE)


SSM_WIDE = GROUPS_PER_TILE * SSM_STATE
LANE_GROUPS_PER_TILE = SSM_WIDE // LANES


def _chan(j):
    return slice(j * LANES, (j + 1) * LANES)


def _time_major_rows(j, q, tc):
    return pl.ds(j * LANE_GROUPS_PER_TILE + q, tc, stride=STATE_VREG_ROWS)


def _to_time_major(x, t_re_ref, t_im_ref, dst_re, dst_im, tc):
    for j in range(SSM_TILES):
        xj = x[:, _chan(j)]
        for t_ref, dst in ((t_re_ref, dst_re), (t_im_ref, dst_im)):
            r = _dot(xj, t_ref[j], 1, 0)
            for q in range(LANE_GROUPS_PER_TILE):
                dst[_time_major_rows(j, q, tc), :] = r[:, q * LANES:(q + 1) * LANES]


def _from_time_major(src, j, tc):
    return jnp.concatenate([src[_time_major_rows(j, q, tc), :] for q in range(LANE_GROUPS_PER_TILE)], axis=1)


def _scan_chunk(w_re, w_im, h_re, h_im, a_re, a_im, start, tc):
    def step(t, carry):
        hr, hi = carry
        rows = _scan_rows(t)
        nr = a_re * hr - a_im * hi + w_re[rows, :]
        ni = a_re * hi + a_im * hr + w_im[rows, :]
        h_re[rows, :] = nr
        h_im[rows, :] = ni
        return nr, ni

    return lax.fori_loop(0, tc, step, start, unroll=8)


SSM_CHUNK = 256


def _tile_spec(stack, k):
    return pl.BlockSpec((pl.Squeezed(),) + tuple(stack.shape[1:]), lambda i: (k, 0, 0, 0))


def _ssm_fwd(proj, tiles_cn, tiles_nc, a_re, a_im, gain, *, name, tc=SSM_CHUNK):
    n_rows = proj.shape[0]
    n_chunk = n_rows // tc

    def body(u_ref, tbr_ref, tbi_ref, tcr_ref, tci_ref, ar_ref, ai_ref, g_ref, y_ref, gy_ref, hr, hi, wr, wi, state):
        @pl.when(pl.program_id(0) == 0)
        def _():
            state[...] = jnp.zeros_like(state)

        u = u_ref[...]
        _to_time_major(u, tbr_ref, tbi_ref, wr, wi, tc)
        state[0], state[1] = _scan_chunk(wr, wi, hr, hi, ar_ref[...], ai_ref[...], (state[0], state[1]), tc)
        for j in range(SSM_TILES):
            yj = (_dot(_from_time_major(hr, j, tc), tcr_ref[j], 1, 0) + _dot(_from_time_major(hi, j, tc), tci_ref[j], 1, 0)
                  + g_ref[:, _chan(j)] * u[:, _chan(j)])
            y_ref[:, _chan(j)] = yj
            gy_ref[:, _chan(j)] = jax.nn.gelu(yj).astype(gy_ref.dtype)

    rows = pl.BlockSpec((tc, SSM_WIDTH), lambda i: (i, 0))
    coef = pl.BlockSpec((STATE_VREG_ROWS, LANES), lambda i: (0, 0))
    states = pl.BlockSpec((tc * STATE_VREG_ROWS, LANES), lambda i: (i, 0))
    sshape = jax.ShapeDtypeStruct((n_rows * STATE_VREG_ROWS, LANES), F32)
    return pl.pallas_call(
        body, name=name, grid=(n_chunk,),
        in_specs=[rows, _tile_spec(tiles_cn, 0), _tile_spec(tiles_cn, 1), _tile_spec(tiles_nc, 0), _tile_spec(tiles_nc, 1),
                  coef, coef, pl.BlockSpec((1, SSM_WIDTH), lambda i: (0, 0))],
        out_specs=[rows, rows, states, states],
        out_shape=[jax.ShapeDtypeStruct((n_rows, SSM_WIDTH), F32), jax.ShapeDtypeStruct((n_rows, SSM_WIDTH), MXU_DTYPE),
                   sshape, sshape],
        scratch_shapes=[pltpu.VMEM((tc * STATE_VREG_ROWS, LANES), F32)] * 2 + [pltpu.VMEM((2, STATE_VREG_ROWS, LANES), F32)],
        compiler_params=_params("arbitrary"),
    )(proj, tiles_cn, tiles_cn, tiles_nc, tiles_nc, a_re, a_im, gain)


def _ssm_bwd(proj, dy, h_re, h_im, tiles_cn, tiles_nc, a_re, a_im, gain, *, name, tc=SSM_CHUNK):
    n_rows = proj.shape[0]
    n_chunk = n_rows // tc

    def body(u_ref, dy_ref, hr, hi, tdr_ref, tdi_ref, tur_ref, tui_ref, ar_ref, ai_ref, g_ref,
             du_ref, su_ref, dcr_ref, dci_ref, dbr_ref, dbi_ref, dar_ref, dai_ref, wr, wi, carry):
        @pl.when(pl.program_id(0) == 0)
        def _():
            carry[...] = jnp.zeros_like(carry)
            for acc_ref in (su_ref, dcr_ref, dci_ref, dbr_ref, dbi_ref):
                acc_ref[...] = jnp.zeros_like(acc_ref)

        a_r, a_i = ar_ref[...], ai_ref[...]
        u, dyv = u_ref[...], dy_ref[...]
        _to_time_major(dyv, tdr_ref, tdi_ref, wr, wi, tc)

        def step(kk, c):
            lam_r, lam_i, dar, dai = c
            rows = _scan_rows(tc - 1 - kk)
            h_r, h_i = hr[rows, :], hi[rows, :]
            dar = dar + lam_r * h_r + lam_i * h_i
            dai = dai + lam_i * h_r - lam_r * h_i
            new_r = wr[rows, :] + a_r * lam_r + a_i * lam_i
            new_i = wi[rows, :] + a_r * lam_i - a_i * lam_r
            wr[rows, :] = new_r
            wi[rows, :] = new_i
            return new_r, new_i, dar, dai

        carry[0], carry[1], carry[2], carry[3] = lax.fori_loop(0, tc, step, (carry[0], carry[1], carry[2], carry[3]),
                                                              unroll=8)
        dar_ref[...] = carry[2]
        dai_ref[...] = carry[3]
        for j in range(SSM_TILES):
            cj = _chan(j)
            lam_r, lam_i = _from_time_major(wr, j, tc), _from_time_major(wi, j, tc)
            dcr_ref[j] += _dot(dyv[:, cj], _from_time_major(hr, j, tc), 0, 0)
            dci_ref[j] += _dot(dyv[:, cj], _from_time_major(hi, j, tc), 0, 0)
            dbr_ref[j] += _dot(u[:, cj], lam_r, 0, 0)
            dbi_ref[j] += _dot(u[:, cj], lam_i, 0, 0)
            duj = _dot(lam_r, tur_ref[j], 1, 0) + _dot(lam_i, tui_ref[j], 1, 0) + g_ref[:, cj] * dyv[:, cj]
            du_ref[:, cj] = duj.astype(du_ref.dtype)
            su_ref[:, cj] += _colsum(duj)

    back = lambda i: (n_chunk - 1 - i, 0)
    rows = pl.BlockSpec((tc, SSM_WIDTH), back)
    in_tile = pl.BlockSpec((SSM_TILES, LANES, SSM_WIDE), lambda i: (0, 0, 0))
    coef = pl.BlockSpec((STATE_VREG_ROWS, LANES), lambda i: (0, 0))
    states = pl.BlockSpec((tc * STATE_VREG_ROWS, LANES), back)
    vec = pl.BlockSpec((1, SSM_WIDTH), lambda i: (0, 0))
    tshape = jax.ShapeDtypeStruct((SSM_TILES, LANES, SSM_WIDE), F32)
    cshape = jax.ShapeDtypeStruct((STATE_VREG_ROWS, LANES), F32)
    return pl.pallas_call(
        body, name=name, grid=(n_chunk,),
        in_specs=[rows, rows, states, states, _tile_spec(tiles_cn, 2), _tile_spec(tiles_cn, 3), _tile_spec(tiles_nc, 2),
                  _tile_spec(tiles_nc, 3), coef, coef, vec],
        out_specs=[rows, vec, in_tile, in_tile, in_tile, in_tile, coef, coef],
        out_shape=[jax.ShapeDtypeStruct((n_rows, SSM_WIDTH), MXU_DTYPE), jax.ShapeDtypeStruct((1, SSM_WIDTH), F32),
                   tshape, tshape, tshape, tshape, cshape, cshape],
        scratch_shapes=[pltpu.VMEM((tc * STATE_VREG_ROWS, LANES), F32)] * 2 + [pltpu.VMEM((4, STATE_VREG_ROWS, LANES), F32)],
        compiler_params=_params("arbitrary"),
    )(proj, dy, h_re, h_im, tiles_cn, tiles_cn, tiles_nc, tiles_nc, a_re, a_im, gain)


def _scan_rows(t):
    return pl.ds(pl.multiple_of(t * STATE_VREG_ROWS, 8), STATE_VREG_ROWS)


GATHER_GROUPS = (("w_in",), ("w_glu", "w_att_up", "w_mix_out"), ("w_xq", "w_xkv", "w_xo", "w_ff1", "w_ff2"))
SCATTER_GROUPS = (("w_ff2", "w_ff1"), ("w_xo", "w_xq", "w_xkv", "w_mix_out"), ("w_att_up", "w_glu", "w_in"))


def _local_grads(x, mem, pos_col, target, sm, fetch, send, start_token):
    b_re_t = sm["ssm_b_re"].transpose(2, 0, 1)
    b_im_t = sm["ssm_b_im"].transpose(2, 0, 1)
    logdt = sm["ssm_log_dt"].reshape(SSM_GROUPS, 1)
    c_re, c_im = sm["ssm_c_re"], sm["ssm_c_im"]
    grp = (SSM_GROUPS, SSM_STATE)
    chn = (SSM_GROUP, SSM_GROUPS, SSM_STATE)

    wts = {}
    cos_t, sin_t = _rope_tables(pos_col, after=start_token, name="rope_tables")
    h0, xh0, rs0, h0m = _ln_fwd(x, None, sm["ln_in_g"], sm["ln_in_b"], alpha=1.0, name="ln_in_fwd")
    disc_in = (logdt, sm["ssm_a_re"], sm["ssm_a_im"], b_re_t, b_im_t)
    ab_re, ab_im, bb_re_t, bb_im_t = _whole(_disc, disc_in, [grp, grp, chn, chn], name="ssm_disc")
    a_re_rows, a_im_rows = ab_re.reshape(STATE_VREG_ROWS, LANES), ab_im.reshape(STATE_VREG_ROWS, LANES)
    tiles_cn = _tiles_cn(jnp.stack([bb_re_t.transpose(1, 0, 2), bb_im_t.transpose(1, 0, 2), c_re, -c_im])
                         ).astype(MXU_DTYPE)
    tiles_nc = _tiles_nc(jnp.stack([c_re.transpose(0, 2, 1), -c_im.transpose(0, 2, 1), bb_re_t.transpose(1, 2, 0),
                                    bb_im_t.transpose(1, 2, 0)])).astype(MXU_DTYPE)
    wts.update(fetch(0, [h0m, tiles_cn, tiles_nc]))
    proj = _mm(h0m, wts["w_in"], bias=sm["b_in"], b_shards=True, name="in_proj")

    y, gy, h_re, h_im = _ssm_fwd(proj, tiles_cn, tiles_nc, a_re_rows, a_im_rows, sm["ssm_d"], name="ssm_fwd")

    q, k, v = _qkv_split(proj, cos_t, sin_t, name="qkv_split")
    outs, lses = [], []
    for g, dil in enumerate(DILATIONS):
        o_g, l_g = _dil_fwd(q[g], k[g], v[g], dil, name=f"dil_att_fwd_{dil}")
        outs.append(o_g)
        lses.append(l_g)
    att, lse = _att_merge(outs, lses, name="att_merge")
    wts.update(fetch(1, [att]))
    z = _mm(gy, wts["w_glu"], bias=sm["b_glu"], b_shards=True, name="glu_proj")
    b_att = _mm(att, wts["w_att_up"], b_shards=True, name="att_up")

    mixed = _mix_fwd(proj, z, b_att, name="gate_mix")
    mix_out = _mm(mixed, wts["w_mix_out"], bias=sm["b_mix_out"], name="mix_out")
    h1, xh1, rs1, h1m = _ln_fwd(h0, mix_out, sm["ln1_g"], sm["ln1_b"], alpha=DEEPNORM_ALPHA, name="ln1_fwd")

    wts.update(fetch(2, [h1m]))
    xq = _mm(h1m, wts["w_xq"], out_dtype=MXU_DTYPE, name="xatt_q")
    kv = _mm(mem, wts["w_xkv"], out_dtype=MXU_DTYPE, b_shards=True, name="xatt_kv")
    xo_in = _xatt_fwd(xq, kv, name="xatt_fwd")
    xo = _mm(xo_in, wts["w_xo"], name="xatt_o")
    h2, xh2, rs2, h2m = _ln_fwd(h1, xo, sm["ln2_g"], sm["ln2_b"], alpha=DEEPNORM_ALPHA, name="ln2_fwd")

    pre, act = _mm(h2m, wts["w_ff1"], bias=sm["b_ff1"], b_shards=True, name="ff1",
                   also=(lambda r: jnp.square(jnp.maximum(r, 0.0)), MXU_DTYPE))
    ff = _mm(act, wts["w_ff2"], bias=sm["b_ff2"], name="ff2")

    gw, gs = {}, {}
    dr3, dr3m, gs["ln3_g"], gs["ln3_b"], gs["b_ff2"], loss_row = _ln_loss_bwd(
        h2, ff, target, sm["ln3_g"], sm["ln3_b"], alpha=DEEPNORM_ALPHA, name="ln3_loss")
    wgrad = functools.partial(_mm, ta=True, out_dtype=WIRE_DTYPE, tk=2048)
    gw["w_ff2"] = wgrad(act, dr3m, tk=1024, name="ff2_dw")
    dpre, gs["b_ff1"] = _mm(dr3m, wts["w_ff2"], tb=True, out_dtype=MXU_DTYPE, colsum=True, name="ff2_dx",
                            gate=(pre, lambda p: 2.0 * jnp.maximum(p, 0.0)))
    gw["w_ff1"] = wgrad(h2m, dpre, out_shards=True, name="ff1_dw")
    sent = send(0, gw)
    dh2 = _mm(dpre, wts["w_ff1"], tb=True, b_shards=True, after=sent, name="ff1_dx")

    dr2, dr2m, gs["ln2_g"], gs["ln2_b"], _ = _ln_bwd(dr3, dh2, xh2, rs2, sm["ln2_g"], alpha=DEEPNORM_ALPHA,
                                                     name="ln2_bwd")
    gw["w_xo"] = wgrad(xo_in, dr2m, name="xatt_o_dw")
    dxo_in = _mm(dr2m, wts["w_xo"], tb=True, out_dtype=MXU_DTYPE, name="xatt_o_dx")
    dxq, dkv = _xatt_bwd(xq, kv, dxo_in, name="xatt_bwd")
    gw["w_xq"] = wgrad(h1m, dxq, name="xatt_q_dw")
    gw["w_xkv"] = wgrad(mem, dkv, out_shards=True, name="xatt_kv_dw")
    dh1 = _mm(dxq, wts["w_xq"], tb=True, name="xatt_q_dx")

    dr1, dr1m, gs["ln1_g"], gs["ln1_b"], gs["b_mix_out"] = _ln_bwd(dr2, dh1, xh1, rs1, sm["ln1_g"],
                                                                   alpha=DEEPNORM_ALPHA, name="ln1_bwd")
    gw["w_mix_out"] = wgrad(mixed, dr1m, name="mix_out_dw")
    sent = send(1, gw)
    dmixed = _mm(dr1m, wts["w_mix_out"], tb=True, after=sent, name="mix_out_dx")
    dgs, dga, dz, db_att, s_gs, s_ga, gs["b_glu"] = _mix_bwd(dmixed, proj, z, b_att, name="gate_mix_bwd")

    gw["w_att_up"] = wgrad(att, db_att, out_shards=True, name="att_up_dw")
    datt = _mm(db_att, wts["w_att_up"], tb=True, b_shards=True, name="att_up_dx")
    stats = _att_stats(datt, att, lse, name="att_stats")
    dqkv = [_dil_bwd(q[g], k[g], v[g], datt, stats, dil, name=f"dil_att_bwd_{dil}") for g, dil in enumerate(DILATIONS)]

    gw["w_glu"] = wgrad(gy, dz, out_shards=True, name="glu_dw")
    dgy = _mm(dz, wts["w_glu"], tb=True, b_shards=True, name="glu_dx")
    dy, gs["ssm_d"] = _gelu_bwd(dgy, y, proj, name="gelu_bwd")
    du, s_u, dc_re_t, dc_im_t, dbb_re_t, dbb_im_t, da_re, da_im = _ssm_bwd(
        proj, dy, h_re, h_im, tiles_cn, tiles_nc, a_re_rows, a_im_rows, sm["ssm_d"], name="ssm_bwd")
    gs["ssm_c_re"], gs["ssm_c_im"] = _untile_cn(dc_re_t), -_untile_cn(dc_im_t)
    disc_ct = (da_re.reshape(grp), da_im.reshape(grp), _untile_cn(dbb_re_t).transpose(1, 0, 2),
               _untile_cn(dbb_im_t).transpose(1, 0, 2))
    d_logdt, gs["ssm_a_re"], gs["ssm_a_im"], d_b_re_t, d_b_im_t = _whole(
        _disc_transpose, disc_in + disc_ct, [(SSM_GROUPS, 1), grp, grp, chn, chn], name="ssm_disc_bwd")
    gs["ssm_log_dt"] = d_logdt
    gs["ssm_b_re"], gs["ssm_b_im"] = d_b_re_t.transpose(1, 2, 0), d_b_im_t.transpose(1, 2, 0)

    dproj, s_qkv = _dproj_assemble(du, dqkv, dgs, dga, cos_t, sin_t, name="dproj_assemble")
    gs["b_in"] = jnp.concatenate([s_u, *s_qkv, s_gs, s_ga], axis=1)
    gw["w_in"] = wgrad(h0m, dproj, out_shards=True, name="in_proj_dw")
    sent = send(2, gw)
    dh0 = _mm(dproj, wts["w_in"], tb=True, b_shards=True, after=sent, name="in_proj_dx")
    grad_x, gs["ln_in_g"], gs["ln_in_b"], _ = _ln_bwd(dr1, dh0, xh0, rs0, sm["ln_in_g"], alpha=DEEPNORM_ALPHA,
                                                      operand=False, name="ln_in_bwd")
    return loss_row, grad_x, gs


N_PEER = N_DEV - 1
_IN_HBM = pl.BlockSpec(memory_space=pltpu.HBM)
_IN_SEMAPHORE = pl.BlockSpec(memory_space=pltpu.SEMAPHORE)


def _device_index():
    return 4 * lax.axis_index("x") + 2 * lax.axis_index("y") + lax.axis_index("c")


def _exchange_copies(src_refs, land_refs, send_sems, recv_sems, scatter):
    x, y, c = lax.axis_index("x"), lax.axis_index("y"), lax.axis_index("c")
    me = 4 * x + 2 * y + c
    pairs = []
    for a, (src_ref, land_ref) in enumerate(zip(src_refs, land_refs)):
        for kk in range(1, N_DEV):
            px = (x + (kk >> 2)) % 2
            py = (y + ((kk >> 1) & 1)) % 2
            pc = (c + (kk & 1)) % 2
            peer = 4 * px + 2 * py + pc
            sem = a * N_PEER + kk - 1
            src = src_ref.at[peer] if scatter else src_ref

            def copy(dst, src=src, sem=sem, px=px, py=py, pc=pc):
                return pltpu.make_async_remote_copy(
                    src_ref=src, dst_ref=dst, send_sem=send_sems.at[sem], recv_sem=recv_sems.at[sem],
                    device_id=(px, py, pc), device_id_type=pl.DeviceIdType.MESH)

            pairs.append((functools.partial(copy, land_ref.at[me]), functools.partial(copy, land_ref.at[peer])))
    return pairs


def _exchange_start(srcs, *, scatter, name, after=None):
    n_arr = len(srcs)
    lands = [lax.empty((N_DEV,) + tuple(s.shape[1:] if scatter else s.shape), s.dtype) for s in srcs]
    n_in = 2 * n_arr + (after is not None)

    def body(*refs):
        send_sems, recv_sems = refs[n_in], refs[n_in + 1]
        for sent, _ in _exchange_copies(refs[:n_arr], refs[n_arr:2 * n_arr], send_sems, recv_sems, scatter):
            sent().start()
        refs[-1][...] = jnp.zeros_like(refs[-1])

    through = [pltpu.HBM(t.shape, t.dtype) for t in (*srcs, *lands)]
    res = pl.pallas_call(
        body, name=name,
        out_shape=(pltpu.SemaphoreType.DMA((n_arr * N_PEER,)), pltpu.SemaphoreType.DMA((n_arr * N_PEER,)), *through,
                   jax.ShapeDtypeStruct((8, LANES), F32)),
        in_specs=[_IN_HBM] * (2 * n_arr) + [pl.BlockSpec(memory_space=pl.ANY)] * (after is not None),
        out_specs=(_IN_SEMAPHORE, _IN_SEMAPHORE, *[_IN_HBM] * (2 * n_arr), pl.BlockSpec(memory_space=pltpu.VMEM)),
        input_output_aliases={i: 2 + i for i in range(2 * n_arr)},
        compiler_params=pltpu.CompilerParams(has_side_effects=pltpu.SideEffectType.DATAFLOW_SIDE_EFFECTING),
    )(*[pltpu.with_memory_space_constraint(t, pltpu.HBM) for t in (*srcs, *lands)],
      *([after] if after is not None else []))
    return (res[0], res[1], res[2:2 + n_arr], res[2 + n_arr:2 + 2 * n_arr], scatter), res[-1]


def _exchange_wait(handle, *, after, name):
    send_sems, recv_sems, srcs, lands, scatter = handle
    n_arr = len(srcs)
    after = list(after)

    def body(*refs):
        for sent, received in _exchange_copies(refs[:n_arr], refs[n_arr:2 * n_arr], refs[2 * n_arr], refs[2 * n_arr + 1],
                                               scatter):
            sent().wait_send()
            received().wait_recv()

    res = pl.pallas_call(
        body, name=name, out_shape=tuple(pltpu.HBM(t.shape, t.dtype) for t in (*srcs, *lands)),
        in_specs=[_IN_HBM] * (2 * n_arr) + [_IN_SEMAPHORE, _IN_SEMAPHORE] + [pl.BlockSpec(memory_space=pl.ANY)] * len(after),
        out_specs=tuple([_IN_HBM] * (2 * n_arr)), input_output_aliases={i: i for i in range(2 * n_arr)},
        compiler_params=pltpu.CompilerParams(has_side_effects=pltpu.SideEffectType.DATAFLOW_SIDE_EFFECTING),
    )(*srcs, *lands, send_sems, recv_sems, *after)
    return res[n_arr:]


def _with_own_slot(land, own):
    return lax.dynamic_update_slice_in_dim(land, own[None], _device_index(), axis=0)


def _reduce_adamw(gstack, w, m, v, *, name, tr=128):
    n_rows, cols = w.shape
    tr = min(tr, n_rows)
    assert n_rows % tr == 0, (name, n_rows, tr)

    def body(g_ref, w_ref, m_ref, v_ref, go_ref, d_ref, mo_ref, vo_ref):
        g = g_ref[0].astype(F32)
        for dev in range(1, N_DEV):
            g = g + g_ref[dev].astype(F32)
        m_new = ADAM_B1 * m_ref[...] + (1.0 - ADAM_B1) * g
        v_new = ADAM_B2 * v_ref[...] + (1.0 - ADAM_B2) * jnp.square(g)
        m_hat = m_new / (1.0 - ADAM_B1 ** ADAM_STEP)
        v_hat = v_new / (1.0 - ADAM_B2 ** ADAM_STEP)
        go_ref[...] = g
        d_ref[...] = -ADAM_LR * (m_hat / (jnp.sqrt(v_hat) + ADAM_EPS) + ADAM_WD * w_ref[...])
        mo_ref[...] = m_new
        vo_ref[...] = v_new

    flat = pl.BlockSpec((tr, cols), lambda i: (i, 0))
    shape = jax.ShapeDtypeStruct((n_rows, cols), F32)
    return pl.pallas_call(
        body, name=name, grid=(n_rows // tr,),
        in_specs=[pl.BlockSpec((N_DEV, tr, cols), lambda i: (0, i, 0)), flat, flat, flat],
        out_specs=[flat] * 4, out_shape=[shape] * 4, compiler_params=_params("parallel"),
    )(gstack, w, m, v)


def _pack(parts, dtype):
    flat = jnp.concatenate([p.reshape(-1).astype(dtype) for p in parts])
    unit = PACK_COLS * PACK_ROW_ALIGN
    total = -(-flat.shape[0] // unit) * unit
    return jnp.pad(flat, (0, total - flat.shape[0])).reshape(-1, PACK_COLS)


def _unpack(packed, shapes):
    flat = packed.reshape(-1)
    out, off = [], 0
    for s in shapes:
        size = int(np.prod(s))
        out.append(flat[off:off + size].reshape(s))
        off += size
    return out


def kernel(x, mem, positions, ln_in_g, ln_in_b, w_in, b_in, ssm_log_dt, ssm_a_re, ssm_a_im, ssm_b_re, ssm_b_im, ssm_c_re, ssm_c_im, ssm_d, w_glu, b_glu, w_att_up, w_mix_out, b_mix_out, ln1_g, ln1_b, w_xq, w_xkv, w_xo, ln2_g, ln2_b, w_ff1, b_ff1, w_ff2, b_ff2, ln3_g, ln3_b, loss_target, m_ln_in_g, m_ln_in_b, m_w_in, m_b_in, m_ssm_log_dt, m_ssm_a_re, m_ssm_a_im, m_ssm_b_re, m_ssm_b_im, m_ssm_c_re, m_ssm_c_im, m_ssm_d, m_w_glu, m_b_glu, m_w_att_up, m_w_mix_out, m_b_mix_out, m_ln1_g, m_ln1_b, m_w_xq, m_w_xkv, m_w_xo, m_ln2_g, m_ln2_b, m_w_ff1, m_b_ff1, m_w_ff2, m_b_ff2, m_ln3_g, m_ln3_b, v_ln_in_g, v_ln_in_b, v_w_in, v_b_in, v_ssm_log_dt, v_ssm_a_re, v_ssm_a_im, v_ssm_b_re, v_ssm_b_im, v_ssm_c_re, v_ssm_c_im, v_ssm_d, v_w_glu, v_b_glu, v_w_att_up, v_w_mix_out, v_b_mix_out, v_ln1_g, v_ln1_b, v_w_xq, v_w_xkv, v_w_xo, v_ln2_g, v_ln2_b, v_w_ff1, v_b_ff1, v_w_ff2, v_b_ff2, v_ln3_g, v_ln3_b):
    given = dict(locals())
    w_arg = {n: given[n] for n in WEIGHTS}
    m_arg = {n: given["m_" + n] for n in WEIGHTS}
    v_arg = {n: given["v_" + n] for n in WEIGHTS}

    shards = {n: w_arg[n][0].astype(MXU_DTYPE) for n in BIG}
    gathers, token = [], None
    for i, names in enumerate(GATHER_GROUPS):
        handle, token = _exchange_start([shards[n] for n in names], scatter=False, after=token, name=f"gather_start_{i}")
        gathers.append(handle)

    small_wmv = [_pack([d[n] for n in SMALL], F32) for d in (w_arg, m_arg, v_arg)]

    def fetch(i, after):
        lands = _exchange_wait(gathers[i], after=after + (small_wmv if i == 0 else []), name=f"gather_wait_{i}")
        full = {n: _with_own_slot(land, shards[n]) for n, land in zip(GATHER_GROUPS[i], lands)}
        return {n: t if n in BIG_COL_SHARDED else t.reshape(-1, t.shape[-1]) for n, t in full.items()}

    scatters = {}

    def send(i, gw):
        slots = [gw[n] if n in BIG_COL_SHARDED else gw[n].reshape(N_DEV, -1, gw[n].shape[-1]) for n in SCATTER_GROUPS[i]]
        handle, sent = _exchange_start(slots, scatter=True, name=f"scatter_start_{i}")
        scatters[i] = (handle, slots)
        return sent

    sm = {}
    for n in SMALL:
        t = w_arg[n]
        if n.startswith("ssm_") and n not in ("ssm_d", "ssm_log_dt"):
            sm[n] = t[0]
        else:
            sm[n] = t.reshape(1, -1)

    loss_row, grad_x, gs = _local_grads(x[0], mem[0], positions.reshape(-1, 1), loss_target[0], sm, fetch, send, token)
    loss = lax.psum(loss_row[0, 0], ("x", "y", "c"))
    small = _pack([gs[n] for n in SMALL], WIRE_DTYPE)
    small_handle, _ = _exchange_start([small], scatter=False, name="small_start")

    results = [{}, {}, {}, {}]
    done = grad_x
    for i, names in enumerate(SCATTER_GROUPS):
        handle, slots = scatters[i]
        lands = _exchange_wait(handle, after=[done], name=f"scatter_wait_{i}")
        for n, land, slot in zip(names, lands, slots):
            own = lax.dynamic_index_in_dim(slot, _device_index(), axis=0, keepdims=False)
            res = _reduce_adamw(_with_own_slot(land, own), w_arg[n][0], m_arg[n][0], v_arg[n][0], name="adamw_" + n)
            done = res[0]
            for d, r in zip(results, res):
                d[n] = r[None]
    small_stack = _with_own_slot(_exchange_wait(small_handle, after=[done], name="small_wait")[0], small)
    small_shapes = [w_arg[n].shape for n in SMALL]
    res = _reduce_adamw(small_stack, *small_wmv, name="adamw_small")
    for d, r in zip(results, res):
        d.update(zip(SMALL, _unpack(r, small_shapes)))
    out = [loss, grad_x[None]]
    for d in results:
        out += [d[n] for n in WEIGHTS]
    return tuple(out)
```

```python
import functools

import numpy as np
import jax
import jax.numpy as jnp
from jax import lax
from jax.experimental import pallas as pl
from jax.experimental.pallas import tpu as pltpu

F32 = jnp.float32
MXU_DTYPE = jnp.bfloat16
WIRE_DTYPE = jnp.bfloat16
VMEM_LIMIT_BYTES = 48 * 1024 * 1024
LANES = 128

N_DEV = 8
D_MODEL = 1024
SSM_GROUP = 16
SSM_WIDTH = 768
SSM_GROUPS = SSM_WIDTH // SSM_GROUP
SSM_STATE = 64
SSM_CH = SSM_GROUPS * SSM_STATE
SSM_TILES = SSM_WIDTH // LANES
GROUPS_PER_TILE = LANES // SSM_GROUP
STATE_VREG_ROWS = SSM_CH // LANES
ATT_HEAD_DIM = 64
ATT_HEADS_PER_GROUP = 4
ATT_MERGED = ATT_HEADS_PER_GROUP * ATT_HEAD_DIM
LANE_HALVES = ATT_MERGED // LANES
DILATIONS = (1, 4, 16)
ATT_BLK = 128
ATT_SCALE = ATT_HEAD_DIM ** -0.5
ROT_DIM = ATT_HEAD_DIM // 4
ROPE_THETA = 500000.0
XATT_HEADS = 4
XATT_HEAD_DIM = D_MODEL // XATT_HEADS
XATT_SCALE = XATT_HEAD_DIM ** -0.5
DEEPNORM_ALPHA = 2.0 ** 0.25
LN_EPS = 1e-5
NEG_INF = -1e30
OFF_Q_BLK, OFF_K_BLK, OFF_V_BLK = 3, 6, 9
OFF_GS_BLK, OFF_GA_BLK = 3, 4

ADAM_LR = 0.001
ADAM_B1 = 0.9
ADAM_B2 = 0.999
ADAM_EPS = 1e-08
ADAM_WD = 0.01
ADAM_STEP = 10

BIG = ("w_in", "w_glu", "w_att_up", "w_mix_out", "w_xq", "w_xkv", "w_xo", "w_ff1", "w_ff2")
BIG_COL_SHARDED = ("w_in", "w_glu", "w_att_up", "w_xkv", "w_ff1")
WEIGHTS = ("ln_in_g", "ln_in_b", "w_in", "b_in", "ssm_log_dt", "ssm_a_re", "ssm_a_im", "ssm_b_re", "ssm_b_im",
           "ssm_c_re", "ssm_c_im", "ssm_d", "w_glu", "b_glu", "w_att_up", "w_mix_out", "b_mix_out", "ln1_g", "ln1_b",
           "w_xq", "w_xkv", "w_xo", "ln2_g", "ln2_b", "w_ff1", "b_ff1", "w_ff2", "b_ff2", "ln3_g", "ln3_b")
SMALL = tuple(n for n in WEIGHTS if n not in BIG)
PACK_COLS = 1024
PACK_ROW_ALIGN = 256


def _params(*sem):
    return pltpu.CompilerParams(dimension_semantics=sem, vmem_limit_bytes=VMEM_LIMIT_BYTES)


def _dot(a, b, ca, cb):
    return lax.dot_general(a.astype(MXU_DTYPE), b.astype(MXU_DTYPE), (((ca,), (cb,)), ((), ())),
                           preferred_element_type=F32)


def _fit(dim, pref):
    if dim <= pref:
        return dim
    best = max(t for t in range(LANES, pref + 1, LANES) if dim % t == 0)
    return best


def _mm(a, b, *, name, ta=False, tb=False, bias=None, out_dtype=F32, b_shards=False, out_shards=False, after=None,
        also=None, gate=None, colsum=False, tm=2048, tn=1024, tk=1024):
    m, k = (a.shape[1], a.shape[0]) if ta else a.shape
    order = (lambda f: (lambda j, i, kk: f(i, j, kk))) if colsum else (lambda f: f)
    spec = lambda shape, f: pl.BlockSpec(shape, order(f))
    if b_shards:
        n_sh, rows, n_loc = b.shape
        if tb:
            n, tn, tk = rows, _fit(rows, tn), n_loc
            assert k == n_sh * n_loc, (name, k, b.shape)
            b_spec = spec((1, tn, tk), lambda i, j, kk: (kk, j, 0))
        else:
            n, tn, tk = n_sh * n_loc, n_loc, _fit(k, tk)
            b_spec = spec((1, tk, tn), lambda i, j, kk: (j, kk, 0))
    else:
        n = b.shape[0] if tb else b.shape[1]
        tn = n // N_DEV if out_shards else _fit(n, tn)
        tk = _fit(k, tk)
        b_spec = spec((tn, tk), lambda i, j, kk: (j, kk)) if tb else spec((tk, tn), lambda i, j, kk: (kk, j))
    tm = _fit(m, tm)
    nk = k // tk
    a_spec = spec((tk, tm), lambda i, j, kk: (kk, i)) if ta else spec((tm, tk), lambda i, j, kk: (i, kk))
    tile = spec((tm, tn), lambda i, j, kk: (i, j))
    in_specs, args = [a_spec, b_spec], [a, b]
    if bias is not None:
        in_specs.append(spec((1, tn), lambda i, j, kk: (0, j)))
        args.append(bias)
    if gate is not None:
        in_specs.append(tile)
        args.append(gate[0])
    if after is not None:
        in_specs.append(pl.BlockSpec(memory_space=pl.ANY))
        args.append(after)
    n_in = len(args)
    if out_shards:
        assert n == N_DEV * tn, (name, n, tn)
        out_specs = [spec((1, tm, tn), lambda i, j, kk: (j, i, 0))]
        out_shape = [jax.ShapeDtypeStruct((N_DEV, m, tn), out_dtype)]
    else:
        out_specs = [tile]
        out_shape = [jax.ShapeDtypeStruct((m, n), out_dtype)]
    if also is not None:
        out_specs.append(tile)
        out_shape.append(jax.ShapeDtypeStruct((m, n), also[1]))
    if colsum:
        out_specs.append(spec((1, tn), lambda i, j, kk: (0, j)))
        out_shape.append(jax.ShapeDtypeStruct((1, n), F32))

    def body(*refs):
        a_ref, b_ref = refs[0], refs[1]
        o_ref = refs[n_in]

        def product():
            return _dot(a_ref[...], b_ref[0] if b_shards else b_ref[...], 0 if ta else 1, 1 if tb else 0)

        def finish(r):
            if bias is not None:
                r = r + refs[2][...]
            if gate is not None:
                r = r * gate[1](refs[2 + (bias is not None)][...])
            if out_shards:
                o_ref[0] = r.astype(o_ref.dtype)
            else:
                o_ref[...] = r.astype(o_ref.dtype)
            if also is not None:
                refs[n_in + 1][...] = also[0](r).astype(also[1])
            if colsum:
                s_ref = refs[n_in + 1 + (also is not None)]

                @pl.when(pl.program_id(1) == 0)
                def _():
                    s_ref[...] = jnp.zeros_like(s_ref)

                s_ref[...] += _colsum(r)

        if nk == 1:
            finish(product())
            return
        acc_ref = refs[-1]
        kk = pl.program_id(2)

        @pl.when(kk == 0)
        def _():
            acc_ref[...] = jnp.zeros_like(acc_ref)

        acc_ref[...] += product()

        @pl.when(kk == nk - 1)
        def _():
            finish(acc_ref[...])

    grid = (n // tn, m // tm, nk) if colsum else (m // tm, n // tn, nk)
    res = pl.pallas_call(
        body, name=name, grid=grid, in_specs=in_specs, out_specs=out_specs, out_shape=out_shape,
        scratch_shapes=[pltpu.VMEM((tm, tn), F32)] if nk > 1 else [],
        compiler_params=_params("parallel", "arbitrary" if colsum else "parallel", "arbitrary"),
    )(*args)
    return res[0] if len(res) == 1 else res


def _rowcall(fn, rows, fulls, row_outs, acc_outs=(), *, n_rows, tm, name, after=None):
    n_r, n_f, n_o, n_a = len(rows), len(fulls), len(row_outs), len(acc_outs)
    n_in = n_r + n_f + (after is not None)
    assert n_rows % tm == 0, (name, n_rows, tm)

    def body(*refs):
        res = fn(*[r[...] for r in refs[:n_r + n_f]])
        res = tuple(res) if isinstance(res, (tuple, list)) else (res,)
        o_refs = refs[n_in:n_in + n_o]
        a_refs = refs[n_in + n_o:]
        for o_ref, val in zip(o_refs, res[:n_o]):
            o_ref[...] = val.astype(o_ref.dtype)
        if n_a:
            @pl.when(pl.program_id(0) == 0)
            def _():
                for a_ref in a_refs:
                    a_ref[...] = jnp.zeros_like(a_ref)

            for a_ref, val in zip(a_refs, res[n_o:]):
                a_ref[...] += val

    in_specs = [pl.BlockSpec((tm, w), functools.partial(lambda i, cb: (i, cb), cb=cb)) for _, w, cb in rows]
    in_specs += [pl.BlockSpec(f.shape, functools.partial(lambda i, nd: (0,) * nd, nd=f.ndim)) for f in fulls]
    in_specs += [pl.BlockSpec(memory_space=pl.ANY)] * (after is not None)
    out_specs = [pl.BlockSpec((tm, w), lambda i: (i, 0)) for w, _ in row_outs]
    out_specs += [pl.BlockSpec((1, w), lambda i: (0, 0)) for w in acc_outs]
    out_shape = [jax.ShapeDtypeStruct((n_rows, w), dt) for w, dt in row_outs]
    out_shape += [jax.ShapeDtypeStruct((1, w), F32) for w in acc_outs]
    return pl.pallas_call(
        body, name=name, grid=(n_rows // tm,), in_specs=in_specs, out_specs=out_specs, out_shape=out_shape,
        compiler_params=_params("arbitrary" if n_a else "parallel"),
    )(*[r[0] for r in rows], *fulls, *([after] if after is not None else []))


def _colsum(v):
    return jnp.sum(v, axis=0, keepdims=True)


def _ln_fwd(a, r, g, b, *, alpha, name):
    n_rows, d = a.shape

    def fn(*t):
        xin = t[0] if alpha == 1.0 else alpha * t[0]
        if r is not None:
            xin = xin + t[1]
        gv, bv = t[-2], t[-1]
        mu = jnp.mean(xin, axis=-1, keepdims=True)
        xc = xin - mu
        var = jnp.mean(xc * xc, axis=-1, keepdims=True)
        rstd = lax.rsqrt(var + LN_EPS)
        xh = xc * rstd
        y = xh * gv + bv
        return y, xh, rstd, y

    rows = [(a, d, 0)] + ([(r, d, 0)] if r is not None else [])
    return _rowcall(fn, rows, [g, b], [(d, F32), (d, F32), (1, F32), (d, MXU_DTYPE)], n_rows=n_rows, tm=256, name=name)


def _ln_bwd(dya, dyb, xh, rstd, g, *, alpha, name, operand=True):
    n_rows, d = xh.shape

    def fn(da, db, xhv, rs, gv):
        dy = alpha * da + db
        dyg = dy * gv
        m1 = jnp.mean(dyg, axis=-1, keepdims=True)
        m2 = jnp.mean(dyg * xhv, axis=-1, keepdims=True)
        dx = rs * (dyg - m1 - xhv * m2)
        return (dx,) + ((dx,) if operand else ()) + (_colsum(dy * xhv), _colsum(dy), _colsum(dx))

    rows = [(dya, d, 0), (dyb, d, 0), (xh, d, 0), (rstd, 1, 0)]
    return _rowcall(fn, rows, [g], [(d, F32)] + [(d, MXU_DTYPE)] * operand, [d, d, d], n_rows=n_rows, tm=256, name=name)


def _ln_loss_bwd(a, r, target, g, b, *, alpha, name):
    n_rows, d = a.shape

    def fn(av, rv, tv, gv, bv):
        xin = alpha * av + rv
        mu = jnp.mean(xin, axis=-1, keepdims=True)
        xc = xin - mu
        var = jnp.mean(xc * xc, axis=-1, keepdims=True)
        rs = lax.rsqrt(var + LN_EPS)
        xh = xc * rs
        diff = xh * gv + bv - tv
        part = jnp.sum(jnp.sum(diff * diff, axis=1, keepdims=True), axis=0, keepdims=True) * (0.5 / d)
        dy = diff * (1.0 / d)
        dyg = dy * gv
        m1 = jnp.mean(dyg, axis=-1, keepdims=True)
        m2 = jnp.mean(dyg * xh, axis=-1, keepdims=True)
        dx = rs * (dyg - m1 - xh * m2)
        return dx, dx, _colsum(dy * xh), _colsum(dy), _colsum(dx), jnp.broadcast_to(part, (1, LANES))

    return _rowcall(fn, [(a, d, 0), (r, d, 0), (target, d, 0)], [g, b], [(d, F32), (d, MXU_DTYPE)], [d, d, d, LANES],
                    n_rows=n_rows, tm=256, name=name)


def _rope_lane_constants():
    lane = np.arange(ATT_MERGED)
    in_head = lane % ATT_HEAD_DIM
    sign = np.where(in_head < ROT_DIM // 2, -1.0, np.where(in_head < ROT_DIM, 1.0, 0.0)).astype(np.float32)
    inv_freq = ROPE_THETA ** (-jnp.arange(0, ROT_DIM, 2, dtype=F32) / ROT_DIM)
    return inv_freq[lane % (ROT_DIM // 2)].reshape(1, ATT_MERGED), jnp.asarray(sign).reshape(1, ATT_MERGED)


def _rope_tables(pos_col, *, name, after=None):
    inv_lane, sign = _rope_lane_constants()

    def fn(pos, inv, sg):
        ang = pos.astype(F32) * inv
        return jnp.where(sg != 0.0, jnp.cos(ang), 1.0), sg * jnp.sin(ang)

    return _rowcall(fn, [(pos_col, 1, 0)], [inv_lane, sign], [(ATT_MERGED, F32), (ATT_MERGED, F32)],
                    n_rows=pos_col.shape[0], tm=512, name=name, after=after)


def _rot_partner(t):
    lane = lax.broadcasted_iota(jnp.int32, t.shape, 1)
    width = t.shape[1]
    return jnp.where((lane & (ROT_DIM // 2)) == 0, pltpu.roll(t, width - ROT_DIM // 2, 1), pltpu.roll(t, ROT_DIM // 2, 1))


def _rope(t, cos_t, sin_t):
    return t * cos_t + _rot_partner(t) * sin_t


def _rope_transpose(dt, cos_t, sin_t):
    return dt * cos_t + _rot_partner(dt * sin_t)


def _strided_rows(r, count, stride):
    return pl.ds(r, count) if stride == 1 else pl.ds(r, count, stride=stride)


def _qkv_split(proj, cos_t, sin_t, *, name, tm=512):
    n_rows = proj.shape[0]
    n_g = len(DILATIONS)

    def body(*refs):
        n_src = LANE_HALVES * 3 * n_g
        src, tables, dst = refs[:n_src], refs[n_src:n_src + 2 * LANE_HALVES], refs[n_src + 2 * LANE_HALVES:]
        for kind in range(3):
            for g, dil in enumerate(DILATIONS):
                for half in range(LANE_HALVES):
                    x_ref, o_ref = src[(kind * n_g + g) * LANE_HALVES + half], dst[kind * n_g + g]
                    cos_ref, sin_ref = tables[half], tables[LANE_HALVES + half]
                    for r in range(dil):
                        rows = _strided_rows(r, tm // dil, dil)
                        t = x_ref[rows, :]
                        if kind < 2:
                            t = _rope(t, cos_ref[rows, :], sin_ref[rows, :])
                        lo = r * ATT_MERGED + half * LANES
                        o_ref[:, lo:lo + LANES] = t.astype(o_ref.dtype)

    half_spec = lambda cb: pl.BlockSpec((tm, LANES), functools.partial(lambda i, cb: (i, cb), cb=cb))
    in_specs = [half_spec((off + g) * LANE_HALVES + half)
                for off in (OFF_Q_BLK, OFF_K_BLK, OFF_V_BLK) for g in range(n_g) for half in range(LANE_HALVES)]
    in_specs += [half_spec(half) for _ in range(2) for half in range(LANE_HALVES)]
    out_specs = [pl.BlockSpec((tm // dil, dil * ATT_MERGED), lambda i: (i, 0)) for _ in range(3) for dil in DILATIONS]
    out_shape = [jax.ShapeDtypeStruct((n_rows // dil, dil * ATT_MERGED), MXU_DTYPE) for _ in range(3) for dil in DILATIONS]
    outs = pl.pallas_call(
        body, name=name, grid=(n_rows // tm,), in_specs=in_specs, out_specs=out_specs, out_shape=out_shape,
        compiler_params=_params("parallel"),
    )(*[proj] * (LANE_HALVES * 3 * n_g), *[cos_t] * LANE_HALVES, *[sin_t] * LANE_HALVES)
    return outs[:n_g], outs[n_g:2 * n_g], outs[2 * n_g:]


def _mix(gs, ga, z1, z2, b_att):
    return jax.nn.sigmoid(gs) * (z1 * jax.nn.sigmoid(z2)) + jax.nn.sigmoid(ga) * b_att


def _mix_rows(proj, z, b_att):
    return [(proj, D_MODEL, OFF_GS_BLK), (proj, D_MODEL, OFF_GA_BLK), (z, D_MODEL, 0), (z, D_MODEL, 1), (b_att, D_MODEL, 0)]


def _mix_fwd(proj, z, b_att, *, name):
    return _rowcall(_mix, _mix_rows(proj, z, b_att), [], [(D_MODEL, MXU_DTYPE)],
                    n_rows=proj.shape[0], tm=256, name=name)[0]


def _mix_bwd(dmixed, proj, z, b_att, *, name):
    def fn(dm, gs, ga, z1, z2, ba):
        _, vjp = jax.vjp(_mix, gs, ga, z1, z2, ba)
        dgs, dga, dz1, dz2, dba = vjp(dm)
        dz = jnp.concatenate([dz1, dz2], axis=1)
        return dgs, dga, dz, dba, _colsum(dgs), _colsum(dga), _colsum(dz)

    rows = [(dmixed, D_MODEL, 0)] + _mix_rows(proj, z, b_att)
    widths = [D_MODEL, D_MODEL, 2 * D_MODEL, D_MODEL]
    return _rowcall(fn, rows, [], [(w, MXU_DTYPE) for w in widths], widths[:3], n_rows=proj.shape[0], tm=256, name=name)


def _gelu_bwd(dgy, y, proj, *, name):
    def fn(dg, yv, u):
        _, vjp = jax.vjp(jax.nn.gelu, yv)
        dy = vjp(dg)[0]
        return dy, _colsum(dy * u)

    return _rowcall(fn, [(dgy, SSM_WIDTH, 0), (y, SSM_WIDTH, 0), (proj, SSM_WIDTH, 0)], [], [(SSM_WIDTH, F32)],
                    [SSM_WIDTH], n_rows=y.shape[0], tm=512, name=name)


HEAD_ROWS = ATT_HEADS_PER_GROUP * ATT_BLK


def _head_masks(rows):
    head = lax.broadcasted_iota(jnp.int32, (rows, ATT_MERGED), 1) >> (ATT_HEAD_DIM.bit_length() - 1)
    return [head == h for h in range(ATT_HEADS_PER_GROUP)]


def _stack_heads(t, masks):
    return jnp.concatenate([jnp.where(m, t, jnp.zeros_like(t)) for m in masks], axis=0)


def _unstack_heads(t4, masks):
    blocks = [t4[h * ATT_BLK:(h + 1) * ATT_BLK] for h in range(ATT_HEADS_PER_GROUP)]
    return jnp.where(masks[0], blocks[0], jnp.where(masks[1], blocks[1], jnp.where(masks[2], blocks[2], blocks[3])))


def _head_column(stats, first):
    return jnp.concatenate([stats[:, first + h:first + h + 1] for h in range(ATT_HEADS_PER_GROUP)], axis=0)


def _band_mask(first_key):
    qi = lax.broadcasted_iota(jnp.int32, (HEAD_ROWS, 2 * ATT_BLK), 0) & (ATT_BLK - 1)
    ki = lax.broadcasted_iota(jnp.int32, (HEAD_ROWS, 2 * ATT_BLK), 1)
    steps = qi + ATT_BLK - ki
    return (steps >= 0) & (steps <= ATT_BLK) & (ki >= first_key)


def _dil_fwd(q, k, v, dil, *, name):
    n_blk = q.shape[0] // ATT_BLK
    cur = pl.BlockSpec((ATT_BLK, ATT_MERGED), lambda r, n: (n, r))
    prev = pl.BlockSpec((ATT_BLK, ATT_MERGED), lambda r, n: (jnp.maximum(n - 1, 0), r))

    def body(q_ref, kp_ref, kc_ref, vp_ref, vc_ref, o_ref, l_ref):
        masks = _head_masks(ATT_BLK)
        valid = _band_mask(jnp.where(pl.program_id(1) > 0, 0, ATT_BLK))
        keys = jnp.concatenate([kp_ref[...], kc_ref[...]], axis=0)
        vals = jnp.concatenate([vp_ref[...], vc_ref[...]], axis=0)
        s = jnp.where(valid, _dot(_stack_heads(q_ref[...], masks), keys, 1, 1) * ATT_SCALE, NEG_INF)
        m = jnp.max(s, axis=-1, keepdims=True)
        p = jnp.exp(s - m)
        den = jnp.sum(p, axis=-1, keepdims=True)
        o_ref[...] = _unstack_heads(_dot(p, vals, 1, 0) / den, masks)
        l_ref[...] = _unstack_heads(jnp.broadcast_to(m + jnp.log(den), (HEAD_ROWS, ATT_MERGED)), masks)

    shape = jax.ShapeDtypeStruct(q.shape, F32)
    return pl.pallas_call(
        body, name=name, grid=(dil, n_blk), in_specs=[cur, prev, cur, prev, cur], out_specs=[cur, cur],
        out_shape=[shape, shape], compiler_params=_params("parallel", "parallel"),
    )(q, k, k, v, v)


def _att_merge(outs, lses, *, name, tm=512):
    n_g = len(outs)
    n_rows = outs[0].shape[0] * DILATIONS[0]

    def body(*refs):
        src, (att_ref, lse_ref), tmp = refs[:2 * n_g], refs[2 * n_g:2 * n_g + 2], refs[2 * n_g + 2:]
        vals = []
        for idx, src_ref in enumerate(src):
            dil = DILATIONS[idx % n_g]
            if dil == 1:
                vals.append(src_ref[...])
                continue
            for r in range(dil):
                for half in range(LANE_HALVES):
                    lo = r * ATT_MERGED + half * LANES
                    tmp[LANE_HALVES * idx + half][_strided_rows(r, tm // dil, dil), :] = src_ref[:, lo:lo + LANES]
            vals.append(jnp.concatenate([tmp[LANE_HALVES * idx + half][...] for half in range(LANE_HALVES)], axis=1))
        o, l = vals[:n_g], vals[n_g:]
        m = functools.reduce(jnp.maximum, l)
        e = [jnp.exp(li - m) for li in l]
        z = functools.reduce(jnp.add, e)
        att_ref[...] = functools.reduce(jnp.add, [(ei / z) * oi for ei, oi in zip(e, o)])
        lse_ref[...] = m + jnp.log(z)

    in_specs = [pl.BlockSpec((tm // dil, dil * ATT_MERGED), lambda i: (i, 0)) for _ in range(2) for dil in DILATIONS]
    row = pl.BlockSpec((tm, ATT_MERGED), lambda i: (i, 0))
    shape = jax.ShapeDtypeStruct((n_rows, ATT_MERGED), F32)
    return pl.pallas_call(
        body, name=name, grid=(n_rows // tm,), in_specs=in_specs, out_specs=[row, row], out_shape=[shape, shape],
        scratch_shapes=[pltpu.VMEM((tm, LANES), F32)] * (LANE_HALVES * 2 * n_g), compiler_params=_params("parallel"),
    )(*outs, *lses)


def _att_stats(datt, att, lse, *, name):
    n_rows = datt.shape[0]

    def fn(d, a, l):
        prod = d * a
        lane = lax.broadcasted_iota(jnp.int32, (d.shape[0], LANES), 1)
        out = jnp.zeros((d.shape[0], LANES), F32)
        for h in range(ATT_HEADS_PER_GROUP):
            lo = h * ATT_HEAD_DIM
            out = jnp.where(lane == h, l[:, lo:lo + 1], out)
            delta = jnp.sum(prod[:, lo:lo + ATT_HEAD_DIM], axis=-1, keepdims=True)
            out = jnp.where(lane == ATT_HEADS_PER_GROUP + h, delta, out)
        return out

    rows = [(t, ATT_MERGED, 0) for t in (datt, att, lse)]
    return _rowcall(fn, rows, [], [(LANES, F32)], n_rows=n_rows, tm=512, name=name)[0]


def _dil_bwd(q, k, v, datt, stats, dil, *, name):
    n_rows = datt.shape[0]
    n_blk = n_rows // dil // ATT_BLK
    span = ATT_BLK * dil
    cur = pl.BlockSpec((ATT_BLK, ATT_MERGED), lambda n, r: (n, r))
    prev = pl.BlockSpec((ATT_BLK, ATT_MERGED), lambda n, r: (jnp.maximum(n - 1, 0), r))
    nxt = pl.BlockSpec((ATT_BLK, ATT_MERGED), lambda n, r: (jnp.minimum(n + 1, n_blk - 1), r))
    seq = lambda half, ahead: pl.BlockSpec((span, LANES), lambda n, r: (jnp.minimum(n + ahead, n_blk - 1), half))

    def body(qc_ref, qn_ref, kp_ref, kc_ref, vp_ref, vc_ref, dc0_ref, dc1_ref, dn0_ref, dn1_ref, sc_ref, sn_ref,
             dq0_ref, dq1_ref, dk0_ref, dk1_ref, dv0_ref, dv1_ref):
        n = pl.program_id(0)
        rows = slice(None) if dil == 1 else _strided_rows(pl.program_id(1), ATT_BLK, dil)

        def read(ref0, ref1):
            return jnp.concatenate([ref0[rows, :], ref1[rows, :]], axis=1)

        def write(ref0, ref1, val):
            ref0[rows, :] = val[:, :LANES]
            ref1[rows, :] = val[:, LANES:]

        masks = _head_masks(ATT_BLK)
        valid = _band_mask(jnp.where(n > 0, 0, ATT_BLK))
        qi = lax.broadcasted_iota(jnp.int32, (HEAD_ROWS, ATT_BLK), 0) & (ATT_BLK - 1)
        ki = lax.broadcasted_iota(jnp.int32, (HEAD_ROWS, ATT_BLK), 1)
        valid_next = (ki - qi) >= jnp.where(n < n_blk - 1, 0, ATT_BLK)

        kc, vc = kc_ref[...], vc_ref[...]
        keys = jnp.concatenate([kp_ref[...], kc], axis=0)
        vals = jnp.concatenate([vp_ref[...], vc], axis=0)
        q4 = _stack_heads(qc_ref[...], masks)
        d4 = _stack_heads(read(dc0_ref, dc1_ref).astype(MXU_DTYPE), masks)
        st = sc_ref[rows, :]
        p = jnp.where(valid, jnp.exp(_dot(q4, keys, 1, 1) * ATT_SCALE - _head_column(st, 0)), 0.0)
        ds = p * (_dot(d4, vals, 1, 1) - _head_column(st, ATT_HEADS_PER_GROUP)) * ATT_SCALE
        write(dq0_ref, dq1_ref, _unstack_heads(_dot(ds, keys, 1, 0), masks))

        q4n = _stack_heads(qn_ref[...], masks)
        d4n = _stack_heads(read(dn0_ref, dn1_ref).astype(MXU_DTYPE), masks)
        stn = sn_ref[rows, :]
        p_n = jnp.where(valid_next, jnp.exp(_dot(q4n, kc, 1, 1) * ATT_SCALE - _head_column(stn, 0)), 0.0)
        ds_n = p_n * (_dot(d4n, vc, 1, 1) - _head_column(stn, ATT_HEADS_PER_GROUP)) * ATT_SCALE
        write(dv0_ref, dv1_ref, _dot(p[:, ATT_BLK:], d4, 0, 0) + _dot(p_n, d4n, 0, 0))
        write(dk0_ref, dk1_ref, _dot(ds[:, ATT_BLK:], q4, 0, 0) + _dot(ds_n, q4n, 0, 0))

    shape = jax.ShapeDtypeStruct((n_rows, LANES), F32)
    out = seq(0, 0)
    res = pl.pallas_call(
        body, name=name, grid=(n_blk, dil),
        in_specs=[cur, nxt, prev, cur, prev, cur, seq(0, 0), seq(1, 0), seq(0, 1), seq(1, 1), seq(0, 0), seq(0, 1)],
        out_specs=[out] * 6, out_shape=[shape] * 6, compiler_params=_params("parallel", "arbitrary"),
    )(q, q, k, k, v, v, datt, datt, datt, datt, stats, stats)
    return [(res[2 * i], res[2 * i + 1]) for i in range(3)]


def _dproj_assemble(du, dqkv, dgs, dga, cos_t, sin_t, *, name):
    n_g = len(DILATIONS)

    def fn(*t):
        n_half = LANE_HALVES * 3 * n_g
        du_t, halves, (dgs_t, dga_t, c, s) = t[0], t[1:1 + n_half], t[1 + n_half:]
        parts = [jnp.concatenate(halves[LANE_HALVES * i:LANE_HALVES * (i + 1)], axis=1) for i in range(3 * n_g)]
        for i in range(2 * n_g):
            parts[i] = _rope_transpose(parts[i], c, s)
        cast = [p.astype(MXU_DTYPE) for p in parts]
        return [jnp.concatenate([du_t] + cast + [dgs_t, dga_t], axis=1)] + [_colsum(p) for p in parts]

    rows = [(du, SSM_WIDTH, 0)]
    rows += [(half, LANES, 0) for i in range(3) for g in range(n_g) for half in dqkv[g][i]]
    rows += [(dgs, D_MODEL, 0), (dga, D_MODEL, 0), (cos_t, ATT_MERGED, 0), (sin_t, ATT_MERGED, 0)]
    width = SSM_WIDTH + 3 * n_g * ATT_MERGED + 2 * D_MODEL
    res = _rowcall(fn, rows, [], [(width, MXU_DTYPE)], [ATT_MERGED] * (3 * n_g), n_rows=du.shape[0], tm=256, name=name)
    return res[0], res[1:]


def _xhead(h):
    return slice(h * XATT_HEAD_DIM, (h + 1) * XATT_HEAD_DIM)


def _xatt_probs(qh, kh):
    s = _dot(qh, kh, 1, 1) * XATT_SCALE
    e = jnp.exp(s - jnp.max(s, axis=-1, keepdims=True))
    return e / jnp.sum(e, axis=-1, keepdims=True)


def _xatt_fwd(q, kv, *, name, tm=512):
    n_rows = q.shape[0]
    n_mem = kv.shape[0]

    def body(q_ref, kv_ref, o_ref):
        for h in range(XATT_HEADS):
            sl = _xhead(h)
            p = _xatt_probs(q_ref[:, sl], kv_ref[:, sl])
            o_ref[:, sl] = _dot(p, kv_ref[:, D_MODEL + h * XATT_HEAD_DIM:D_MODEL + (h + 1) * XATT_HEAD_DIM], 1, 0
                                ).astype(o_ref.dtype)

    row = pl.BlockSpec((tm, D_MODEL), lambda i: (i, 0))
    return pl.pallas_call(
        body, name=name, grid=(n_rows // tm,),
        in_specs=[row, pl.BlockSpec((n_mem, 2 * D_MODEL), lambda i: (0, 0))], out_specs=row,
        out_shape=jax.ShapeDtypeStruct((n_rows, D_MODEL), MXU_DTYPE), compiler_params=_params("parallel"),
    )(q, kv)


def _xatt_bwd(q, kv, do, *, name, tm=512):
    n_rows = q.shape[0]
    n_mem = kv.shape[0]

    def body(q_ref, kv_ref, do_ref, dq_ref, dkv_ref):
        @pl.when(pl.program_id(0) == 0)
        def _():
            dkv_ref[...] = jnp.zeros_like(dkv_ref)

        for h in range(XATT_HEADS):
            sl = _xhead(h)
            vsl = slice(D_MODEL + h * XATT_HEAD_DIM, D_MODEL + (h + 1) * XATT_HEAD_DIM)
            qh, kh, doh = q_ref[:, sl], kv_ref[:, sl], do_ref[:, sl]
            p = _xatt_probs(qh, kh)
            dp = _dot(doh, kv_ref[:, vsl], 1, 1)
            ds = p * (dp - jnp.sum(dp * p, axis=-1, keepdims=True)) * XATT_SCALE
            dq_ref[:, sl] = _dot(ds, kh, 1, 0).astype(dq_ref.dtype)
            dkv_ref[:, sl] += _dot(ds, qh, 0, 0)
            dkv_ref[:, vsl] += _dot(p, doh, 0, 0)

    row = pl.BlockSpec((tm, D_MODEL), lambda i: (i, 0))
    full = pl.BlockSpec((n_mem, 2 * D_MODEL), lambda i: (0, 0))
    return pl.pallas_call(
        body, name=name, grid=(n_rows // tm,), in_specs=[row, full, row], out_specs=[row, full],
        out_shape=[jax.ShapeDtypeStruct((n_rows, D_MODEL), MXU_DTYPE), jax.ShapeDtypeStruct((n_mem, 2 * D_MODEL), F32)],
        compiler_params=_params("arbitrary"),
    )(q, kv, do)


def _disc(logdt, a_re, a_im, b_re, b_im):
    dt = jnp.exp(logdt)
    mag = jnp.exp(a_re * dt)
    ab_re = mag * jnp.cos(a_im * dt)
    ab_im = mag * jnp.sin(a_im * dt)
    den = jnp.square(a_re) + jnp.square(a_im)
    nr = ab_re - 1.0
    f_re = (nr * a_re + ab_im * a_im) / den
    f_im = (ab_im * a_re - nr * a_im) / den
    bb_re = f_re[None] * b_re - f_im[None] * b_im
    bb_im = f_re[None] * b_im + f_im[None] * b_re
    return ab_re, ab_im, bb_re, bb_im


def _disc_transpose(logdt, a_re, a_im, b_re, b_im, g_ab_re, g_ab_im, g_bb_re, g_bb_im):
    dt = jnp.exp(logdt)
    mag = jnp.exp(a_re * dt)
    th = a_im * dt
    cs, sn = jnp.cos(th), jnp.sin(th)
    ab_re, ab_im = mag * cs, mag * sn
    den = jnp.square(a_re) + jnp.square(a_im)
    nr = ab_re - 1.0
    f_re = (nr * a_re + ab_im * a_im) / den
    f_im = (ab_im * a_re - nr * a_im) / den
    d_f_re = jnp.sum(g_bb_re * b_re + g_bb_im * b_im, axis=0)
    d_f_im = jnp.sum(g_bb_im * b_re - g_bb_re * b_im, axis=0)
    d_b_re = g_bb_re * f_re[None] + g_bb_im * f_im[None]
    d_b_im = g_bb_im * f_re[None] - g_bb_re * f_im[None]
    d_n_re, d_n_im = d_f_re / den, d_f_im / den
    d_den = -(d_f_re * f_re + d_f_im * f_im) / den
    d_ab_re = g_ab_re + d_n_re * a_re - d_n_im * a_im
    d_ab_im = g_ab_im + d_n_re * a_im + d_n_im * a_re
    d_a_re = d_n_re * nr + d_n_im * ab_im + 2.0 * d_den * a_re
    d_a_im = d_n_re * ab_im - d_n_im * nr + 2.0 * d_den * a_im
    d_mag = d_ab_re * cs + d_ab_im * sn
    d_th = mag * (d_ab_im * cs - d_ab_re * sn)
    d_a_re = d_a_re + d_mag * mag * dt
    d_a_im = d_a_im + d_th * dt
    d_dt = jnp.sum(d_mag * mag * a_re + d_th * a_im, axis=-1, keepdims=True)
    return d_dt * dt, d_a_re, d_a_im, d_b_re, d_b_im


def _whole(fn, args, out_shapes, *, name):
    n_in = len(args)

    def body(*refs):
        res = fn(*[r[...] for r in refs[:n_in]])
        for o_ref, val in zip(refs[n_in:], res):
            o_ref[...] = val

    return pl.pallas_call(body, name=name, out_shape=[jax.ShapeDtypeStruct(s, F32) for s in out_shapes],
                          compiler_params=pltpu.CompilerParams(vmem_limit_bytes=VMEM_LIMIT_BYTES))(*args)


def _tiles_cn(t):
    t = t.reshape(-1, SSM_TILES, GROUPS_PER_TILE, SSM_GROUP, SSM_STATE)
    eye = jnp.eye(GROUPS_PER_TILE, dtype=t.dtype)
    return (t[:, :, :, :, None, :] * eye[:, None, :, None]).reshape(-1, SSM_TILES, LANES, GROUPS_PER_TILE * SSM_STATE)


def _tiles_nc(t):
    t = t.reshape(-1, SSM_TILES, GROUPS_PER_TILE, SSM_STATE, SSM_GROUP)
    eye = jnp.eye(GROUPS_PER_TILE, dtype=t.dtype)
    return (t[:, :, :, :, None, :] * eye[:, None, :, None]).reshape(-1, SSM_TILES, GROUPS_PER_TILE * SSM_STATE, LANES)


def _untile_cn(t):
    t = t.reshape(SSM_TILES, GROUPS_PER_TILE, SSM_GROUP, GROUPS_PER_TILE, SSM_STATE)
    eye = jnp.eye(GROUPS_PER_TILE, dtype=t.dtype)
    return jnp.sum(t * eye[None, :, None, :, None], axis=3).reshape(SSM_GROUPS, SSM_GROUP, SSM_STATE)


SSM_WIDE = GROUPS_PER_TILE * SSM_STATE
LANE_GROUPS_PER_TILE = SSM_WIDE // LANES


def _chan(j):
    return slice(j * LANES, (j + 1) * LANES)


def _time_major_rows(j, q, tc):
    return pl.ds(j * LANE_GROUPS_PER_TILE + q, tc, stride=STATE_VREG_ROWS)


def _to_time_major(x, t_re_ref, t_im_ref, dst_re, dst_im, tc):
    for j in range(SSM_TILES):
        xj = x[:, _chan(j)]
        for t_ref, dst in ((t_re_ref, dst_re), (t_im_ref, dst_im)):
            r = _dot(xj, t_ref[j], 1, 0)
            for q in range(LANE_GROUPS_PER_TILE):
                dst[_time_major_rows(j, q, tc), :] = r[:, q * LANES:(q + 1) * LANES]


def _from_time_major(src, j, tc):
    return jnp.concatenate([src[_time_major_rows(j, q, tc), :] for q in range(LANE_GROUPS_PER_TILE)], axis=1)


def _scan_chunk(w_re, w_im, h_re, h_im, a_re, a_im, start, tc):
    def step(t, carry):
        hr, hi = carry
        rows = _scan_rows(t)
        nr = a_re * hr - a_im * hi + w_re[rows, :]
        ni = a_re * hi + a_im * hr + w_im[rows, :]
        h_re[rows, :] = nr
        h_im[rows, :] = ni
        return nr, ni

    return lax.fori_loop(0, tc, step, start, unroll=8)


SSM_CHUNK = 256


def _tile_spec(stack, k):
    return pl.BlockSpec((pl.Squeezed(),) + tuple(stack.shape[1:]), lambda i: (k, 0, 0, 0))


def _ssm_fwd(proj, tiles_cn, tiles_nc, a_re, a_im, gain, *, name, tc=SSM_CHUNK):
    n_rows = proj.shape[0]
    n_chunk = n_rows // tc

    def body(u_ref, tbr_ref, tbi_ref, tcr_ref, tci_ref, ar_ref, ai_ref, g_ref, y_ref, gy_ref, hr, hi, wr, wi, state):
        @pl.when(pl.program_id(0) == 0)
        def _():
            state[...] = jnp.zeros_like(state)

        u = u_ref[...]
        _to_time_major(u, tbr_ref, tbi_ref, wr, wi, tc)
        state[0], state[1] = _scan_chunk(wr, wi, hr, hi, ar_ref[...], ai_ref[...], (state[0], state[1]), tc)
        for j in range(SSM_TILES):
            yj = (_dot(_from_time_major(hr, j, tc), tcr_ref[j], 1, 0) + _dot(_from_time_major(hi, j, tc), tci_ref[j], 1, 0)
                  + g_ref[:, _chan(j)] * u[:, _chan(j)])
            y_ref[:, _chan(j)] = yj
            gy_ref[:, _chan(j)] = jax.nn.gelu(yj).astype(gy_ref.dtype)

    rows = pl.BlockSpec((tc, SSM_WIDTH), lambda i: (i, 0))
    coef = pl.BlockSpec((STATE_VREG_ROWS, LANES), lambda i: (0, 0))
    states = pl.BlockSpec((tc * STATE_VREG_ROWS, LANES), lambda i: (i, 0))
    sshape = jax.ShapeDtypeStruct((n_rows * STATE_VREG_ROWS, LANES), F32)
    return pl.pallas_call(
        body, name=name, grid=(n_chunk,),
        in_specs=[rows, _tile_spec(tiles_cn, 0), _tile_spec(tiles_cn, 1), _tile_spec(tiles_nc, 0), _tile_spec(tiles_nc, 1),
                  coef, coef, pl.BlockSpec((1, SSM_WIDTH), lambda i: (0, 0))],
        out_specs=[rows, rows, states, states],
        out_shape=[jax.ShapeDtypeStruct((n_rows, SSM_WIDTH), F32), jax.ShapeDtypeStruct((n_rows, SSM_WIDTH), MXU_DTYPE),
                   sshape, sshape],
        scratch_shapes=[pltpu.VMEM((tc * STATE_VREG_ROWS, LANES), F32)] * 2 + [pltpu.VMEM((2, STATE_VREG_ROWS, LANES), F32)],
        compiler_params=_params("arbitrary"),
    )(proj, tiles_cn, tiles_cn, tiles_nc, tiles_nc, a_re, a_im, gain)


def _ssm_bwd(proj, dy, h_re, h_im, tiles_cn, tiles_nc, a_re, a_im, gain, *, name, tc=SSM_CHUNK):
    n_rows = proj.shape[0]
    n_chunk = n_rows // tc

    def body(u_ref, dy_ref, hr, hi, tdr_ref, tdi_ref, tur_ref, tui_ref, ar_ref, ai_ref, g_ref,
             du_ref, su_ref, dcr_ref, dci_ref, dbr_ref, dbi_ref, dar_ref, dai_ref, wr, wi, carry):
        @pl.when(pl.program_id(0) == 0)
        def _():
            carry[...] = jnp.zeros_like(carry)
            for acc_ref in (su_ref, dcr_ref, dci_ref, dbr_ref, dbi_ref):
                acc_ref[...] = jnp.zeros_like(acc_ref)

        a_r, a_i = ar_ref[...], ai_ref[...]
        u, dyv = u_ref[...], dy_ref[...]
        _to_time_major(dyv, tdr_ref, tdi_ref, wr, wi, tc)

        def step(kk, c):
            lam_r, lam_i, dar, dai = c
            rows = _scan_rows(tc - 1 - kk)
            h_r, h_i = hr[rows, :], hi[rows, :]
            dar = dar + lam_r * h_r + lam_i * h_i
            dai = dai + lam_i * h_r - lam_r * h_i
            new_r = wr[rows, :] + a_r * lam_r + a_i * lam_i
            new_i = wi[rows, :] + a_r * lam_i - a_i * lam_r
            wr[rows, :] = new_r
            wi[rows, :] = new_i
            return new_r, new_i, dar, dai

        carry[0], carry[1], carry[2], carry[3] = lax.fori_loop(0, tc, step, (carry[0], carry[1], carry[2], carry[3]),
                                                              unroll=8)
        dar_ref[...] = carry[2]
        dai_ref[...] = carry[3]
        for j in range(SSM_TILES):
            cj = _chan(j)
            lam_r, lam_i = _from_time_major(wr, j, tc), _from_time_major(wi, j, tc)
            dcr_ref[j] += _dot(dyv[:, cj], _from_time_major(hr, j, tc), 0, 0)
            dci_ref[j] += _dot(dyv[:, cj], _from_time_major(hi, j, tc), 0, 0)
            dbr_ref[j] += _dot(u[:, cj], lam_r, 0, 0)
            dbi_ref[j] += _dot(u[:, cj], lam_i, 0, 0)
            duj = _dot(lam_r, tur_ref[j], 1, 0) + _dot(lam_i, tui_ref[j], 1, 0) + g_ref[:, cj] * dyv[:, cj]
            du_ref[:, cj] = duj.astype(du_ref.dtype)
            su_ref[:, cj] += _colsum(duj)

    back = lambda i: (n_chunk - 1 - i, 0)
    rows = pl.BlockSpec((tc, SSM_WIDTH), back)
    in_tile = pl.BlockSpec((SSM_TILES, LANES, SSM_WIDE), lambda i: (0, 0, 0))
    coef = pl.BlockSpec((STATE_VREG_ROWS, LANES), lambda i: (0, 0))
    states = pl.BlockSpec((tc * STATE_VREG_ROWS, LANES), back)
    vec = pl.BlockSpec((1, SSM_WIDTH), lambda i: (0, 0))
    tshape = jax.ShapeDtypeStruct((SSM_TILES, LANES, SSM_WIDE), F32)
    cshape = jax.ShapeDtypeStruct((STATE_VREG_ROWS, LANES), F32)
    return pl.pallas_call(
        body, name=name, grid=(n_chunk,),
        in_specs=[rows, rows, states, states, _tile_spec(tiles_cn, 2), _tile_spec(tiles_cn, 3), _tile_spec(tiles_nc, 2),
                  _tile_spec(tiles_nc, 3), coef, coef, vec],
        out_specs=[rows, vec, in_tile, in_tile, in_tile, in_tile, coef, coef],
        out_shape=[jax.ShapeDtypeStruct((n_rows, SSM_WIDTH), MXU_DTYPE), jax.ShapeDtypeStruct((1, SSM_WIDTH), F32),
                   tshape, tshape, tshape, tshape, cshape, cshape],
        scratch_shapes=[pltpu.VMEM((tc * STATE_VREG_ROWS, LANES), F32)] * 2 + [pltpu.VMEM((4, STATE_VREG_ROWS, LANES), F32)],
        compiler_params=_params("arbitrary"),
    )(proj, dy, h_re, h_im, tiles_cn, tiles_cn, tiles_nc, tiles_nc, a_re, a_im, gain)


def _scan_rows(t):
    return pl.ds(pl.multiple_of(t * STATE_VREG_ROWS, 8), STATE_VREG_ROWS)


GATHER_GROUPS = (("w_in",), ("w_glu", "w_att_up", "w_mix_out"), ("w_xq", "w_xkv", "w_xo", "w_ff1", "w_ff2"))
SCATTER_GROUPS = (("w_ff2", "w_ff1"), ("w_xo", "w_xq", "w_xkv", "w_mix_out"), ("w_att_up", "w_glu", "w_in"))


def _local_grads(x, mem, pos_col, target, sm, fetch, send, start_token):
    b_re_t = sm["ssm_b_re"].transpose(2, 0, 1)
    b_im_t = sm["ssm_b_im"].transpose(2, 0, 1)
    logdt = sm["ssm_log_dt"].reshape(SSM_GROUPS, 1)
    c_re, c_im = sm["ssm_c_re"], sm["ssm_c_im"]
    grp = (SSM_GROUPS, SSM_STATE)
    chn = (SSM_GROUP, SSM_GROUPS, SSM_STATE)

    wts = {}
    cos_t, sin_t = _rope_tables(pos_col, after=start_token, name="rope_tables")
    h0, xh0, rs0, h0m = _ln_fwd(x, None, sm["ln_in_g"], sm["ln_in_b"], alpha=1.0, name="ln_in_fwd")
    disc_in = (logdt, sm["ssm_a_re"], sm["ssm_a_im"], b_re_t, b_im_t)
    ab_re, ab_im, bb_re_t, bb_im_t = _whole(_disc, disc_in, [grp, grp, chn, chn], name="ssm_disc")
    a_re_rows, a_im_rows = ab_re.reshape(STATE_VREG_ROWS, LANES), ab_im.reshape(STATE_VREG_ROWS, LANES)
    tiles_cn = _tiles_cn(jnp.stack([bb_re_t.transpose(1, 0, 2), bb_im_t.transpose(1, 0, 2), c_re, -c_im])
                         ).astype(MXU_DTYPE)
    tiles_nc = _tiles_nc(jnp.stack([c_re.transpose(0, 2, 1), -c_im.transpose(0, 2, 1), bb_re_t.transpose(1, 2, 0),
                                    bb_im_t.transpose(1, 2, 0)])).astype(MXU_DTYPE)
    wts.update(fetch(0, [h0m, tiles_cn, tiles_nc]))
    proj = _mm(h0m, wts["w_in"], bias=sm["b_in"], b_shards=True, name="in_proj")

    y, gy, h_re, h_im = _ssm_fwd(proj, tiles_cn, tiles_nc, a_re_rows, a_im_rows, sm["ssm_d"], name="ssm_fwd")

    q, k, v = _qkv_split(proj, cos_t, sin_t, name="qkv_split")
    outs, lses = [], []
    for g, dil in enumerate(DILATIONS):
        o_g, l_g = _dil_fwd(q[g], k[g], v[g], dil, name=f"dil_att_fwd_{dil}")
        outs.append(o_g)
        lses.append(l_g)
    att, lse = _att_merge(outs, lses, name="att_merge")
    wts.update(fetch(1, [att]))
    z = _mm(gy, wts["w_glu"], bias=sm["b_glu"], b_shards=True, name="glu_proj")
    b_att = _mm(att, wts["w_att_up"], b_shards=True, name="att_up")

    mixed = _mix_fwd(proj, z, b_att, name="gate_mix")
    mix_out = _mm(mixed, wts["w_mix_out"], bias=sm["b_mix_out"], name="mix_out")
    h1, xh1, rs1, h1m = _ln_fwd(h0, mix_out, sm["ln1_g"], sm["ln1_b"], alpha=DEEPNORM_ALPHA, name="ln1_fwd")

    wts.update(fetch(2, [h1m]))
    xq = _mm(h1m, wts["w_xq"], out_dtype=MXU_DTYPE, name="xatt_q")
    kv = _mm(mem, wts["w_xkv"], out_dtype=MXU_DTYPE, b_shards=True, name="xatt_kv")
    xo_in = _xatt_fwd(xq, kv, name="xatt_fwd")
    xo = _mm(xo_in, wts["w_xo"], name="xatt_o")
    h2, xh2, rs2, h2m = _ln_fwd(h1, xo, sm["ln2_g"], sm["ln2_b"], alpha=DEEPNORM_ALPHA, name="ln2_fwd")

    pre, act = _mm(h2m, wts["w_ff1"], bias=sm["b_ff1"], b_shards=True, name="ff1",
                   also=(lambda r: jnp.square(jnp.maximum(r, 0.0)), MXU_DTYPE))
    ff = _mm(act, wts["w_ff2"], bias=sm["b_ff2"], name="ff2")

    gw, gs = {}, {}
    dr3, dr3m, gs["ln3_g"], gs["ln3_b"], gs["b_ff2"], loss_row = _ln_loss_bwd(
        h2, ff, target, sm["ln3_g"], sm["ln3_b"], alpha=DEEPNORM_ALPHA, name="ln3_loss")
    wgrad = functools.partial(_mm, ta=True, out_dtype=WIRE_DTYPE, tk=2048)
    gw["w_ff2"] = wgrad(act, dr3m, tk=1024, name="ff2_dw")
    dpre, gs["b_ff1"] = _mm(dr3m, wts["w_ff2"], tb=True, out_dtype=MXU_DTYPE, colsum=True, name="ff2_dx",
                            gate=(pre, lambda p: 2.0 * jnp.maximum(p, 0.0)))
    gw["w_ff1"] = wgrad(h2m, dpre, out_shards=True, name="ff1_dw")
    sent = send(0, gw)
    dh2 = _mm(dpre, wts["w_ff1"], tb=True, b_shards=True, after=sent, name="ff1_dx")

    dr2, dr2m, gs["ln2_g"], gs["ln2_b"], _ = _ln_bwd(dr3, dh2, xh2, rs2, sm["ln2_g"], alpha=DEEPNORM_ALPHA,
                                                     name="ln2_bwd")
    gw["w_xo"] = wgrad(xo_in, dr2m, name="xatt_o_dw")
    dxo_in = _mm(dr2m, wts["w_xo"], tb=True, out_dtype=MXU_DTYPE, name="xatt_o_dx")
    dxq, dkv = _xatt_bwd(xq, kv, dxo_in, name="xatt_bwd")
    gw["w_xq"] = wgrad(h1m, dxq, name="xatt_q_dw")
    gw["w_xkv"] = wgrad(mem, dkv, out_shards=True, name="xatt_kv_dw")
    dh1 = _mm(dxq, wts["w_xq"], tb=True, name="xatt_q_dx")

    dr1, dr1m, gs["ln1_g"], gs["ln1_b"], gs["b_mix_out"] = _ln_bwd(dr2, dh1, xh1, rs1, sm["ln1_g"],
                                                                   alpha=DEEPNORM_ALPHA, name="ln1_bwd")
    gw["w_mix_out"] = wgrad(mixed, dr1m, name="mix_out_dw")
    sent = send(1, gw)
    dmixed = _mm(dr1m, wts["w_mix_out"], tb=True, after=sent, name="mix_out_dx")
    dgs, dga, dz, db_att, s_gs, s_ga, gs["b_glu"] = _mix_bwd(dmixed, proj, z, b_att, name="gate_mix_bwd")

    gw["w_att_up"] = wgrad(att, db_att, out_shards=True, name="att_up_dw")
    datt = _mm(db_att, wts["w_att_up"], tb=True, b_shards=True, name="att_up_dx")
    stats = _att_stats(datt, att, lse, name="att_stats")
    dqkv = [_dil_bwd(q[g], k[g], v[g], datt, stats, dil, name=f"dil_att_bwd_{dil}") for g, dil in enumerate(DILATIONS)]

    gw["w_glu"] = wgrad(gy, dz, out_shards=True, name="glu_dw")
    dgy = _mm(dz, wts["w_glu"], tb=True, b_shards=True, name="glu_dx")
    dy, gs["ssm_d"] = _gelu_bwd(dgy, y, proj, name="gelu_bwd")
    du, s_u, dc_re_t, dc_im_t, dbb_re_t, dbb_im_t, da_re, da_im = _ssm_bwd(
        proj, dy, h_re, h_im, tiles_cn, tiles_nc, a_re_rows, a_im_rows, sm["ssm_d"], name="ssm_bwd")
    gs["ssm_c_re"], gs["ssm_c_im"] = _untile_cn(dc_re_t), -_untile_cn(dc_im_t)
    disc_ct = (da_re.reshape(grp), da_im.reshape(grp), _untile_cn(dbb_re_t).transpose(1, 0, 2),
               _untile_cn(dbb_im_t).transpose(1, 0, 2))
    d_logdt, gs["ssm_a_re"], gs["ssm_a_im"], d_b_re_t, d_b_im_t = _whole(
        _disc_transpose, disc_in + disc_ct, [(SSM_GROUPS, 1), grp, grp, chn, chn], name="ssm_disc_bwd")
    gs["ssm_log_dt"] = d_logdt
    gs["ssm_b_re"], gs["ssm_b_im"] = d_b_re_t.transpose(1, 2, 0), d_b_im_t.transpose(1, 2, 0)

    dproj, s_qkv = _dproj_assemble(du, dqkv, dgs, dga, cos_t, sin_t, name="dproj_assemble")
    gs["b_in"] = jnp.concatenate([s_u, *s_qkv, s_gs, s_ga], axis=1)
    gw["w_in"] = wgrad(h0m, dproj, out_shards=True, name="in_proj_dw")
    sent = send(2, gw)
    dh0 = _mm(dproj, wts["w_in"], tb=True, b_shards=True, after=sent, name="in_proj_dx")
    grad_x, gs["ln_in_g"], gs["ln_in_b"], _ = _ln_bwd(dr1, dh0, xh0, rs0, sm["ln_in_g"], alpha=DEEPNORM_ALPHA,
                                                      operand=False, name="ln_in_bwd")
    return loss_row, grad_x, gs


N_PEER = N_DEV - 1
_IN_HBM = pl.BlockSpec(memory_space=pltpu.HBM)
_IN_SEMAPHORE = pl.BlockSpec(memory_space=pltpu.SEMAPHORE)


def _device_index():
    return 4 * lax.axis_index("x") + 2 * lax.axis_index("y") + lax.axis_index("c")


def _exchange_copies(src_refs, land_refs, send_sems, recv_sems, scatter):
    x, y, c = lax.axis_index("x"), lax.axis_index("y"), lax.axis_index("c")
    me = 4 * x + 2 * y + c
    pairs = []
    for a, (src_ref, land_ref) in enumerate(zip(src_refs, land_refs)):
        for kk in range(1, N_DEV):
            px = (x + (kk >> 2)) % 2
            py = (y + ((kk >> 1) & 1)) % 2
            pc = (c + (kk & 1)) % 2
            peer = 4 * px + 2 * py + pc
            sem = a * N_PEER + kk - 1
            src = src_ref.at[peer] if scatter else src_ref

            def copy(dst, src=src, sem=sem, px=px, py=py, pc=pc):
                return pltpu.make_async_remote_copy(
                    src_ref=src, dst_ref=dst, send_sem=send_sems.at[sem], recv_sem=recv_sems.at[sem],
                    device_id=(px, py, pc), device_id_type=pl.DeviceIdType.MESH)

            pairs.append((functools.partial(copy, land_ref.at[me]), functools.partial(copy, land_ref.at[peer])))
    return pairs


def _exchange_start(srcs, *, scatter, name, after=None):
    n_arr = len(srcs)
    lands = [lax.empty((N_DEV,) + tuple(s.shape[1:] if scatter else s.shape), s.dtype) for s in srcs]
    n_in = 2 * n_arr + (after is not None)

    def body(*refs):
        send_sems, recv_sems = refs[n_in], refs[n_in + 1]
        for sent, _ in _exchange_copies(refs[:n_arr], refs[n_arr:2 * n_arr], send_sems, recv_sems, scatter):
            sent().start()
        refs[-1][...] = jnp.zeros_like(refs[-1])

    through = [pltpu.HBM(t.shape, t.dtype) for t in (*srcs, *lands)]
    res = pl.pallas_call(
        body, name=name,
        out_shape=(pltpu.SemaphoreType.DMA((n_arr * N_PEER,)), pltpu.SemaphoreType.DMA((n_arr * N_PEER,)), *through,
                   jax.ShapeDtypeStruct((8, LANES), F32)),
        in_specs=[_IN_HBM] * (2 * n_arr) + [pl.BlockSpec(memory_space=pl.ANY)] * (after is not None),
        out_specs=(_IN_SEMAPHORE, _IN_SEMAPHORE, *[_IN_HBM] * (2 * n_arr), pl.BlockSpec(memory_space=pltpu.VMEM)),
        input_output_aliases={i: 2 + i for i in range(2 * n_arr)},
        compiler_params=pltpu.CompilerParams(has_side_effects=pltpu.SideEffectType.DATAFLOW_SIDE_EFFECTING),
    )(*[pltpu.with_memory_space_constraint(t, pltpu.HBM) for t in (*srcs, *lands)],
      *([after] if after is not None else []))
    return (res[0], res[1], res[2:2 + n_arr], res[2 + n_arr:2 + 2 * n_arr], scatter), res[-1]


def _exchange_wait(handle, *, after, name):
    send_sems, recv_sems, srcs, lands, scatter = handle
    n_arr = len(srcs)
    after = list(after)

    def body(*refs):
        for sent, received in _exchange_copies(refs[:n_arr], refs[n_arr:2 * n_arr], refs[2 * n_arr], refs[2 * n_arr + 1],
                                               scatter):
            sent().wait_send()
            received().wait_recv()

    res = pl.pallas_call(
        body, name=name, out_shape=tuple(pltpu.HBM(t.shape, t.dtype) for t in (*srcs, *lands)),
        in_specs=[_IN_HBM] * (2 * n_arr) + [_IN_SEMAPHORE, _IN_SEMAPHORE] + [pl.BlockSpec(memory_space=pl.ANY)] * len(after),
        out_specs=tuple([_IN_HBM] * (2 * n_arr)), input_output_aliases={i: i for i in range(2 * n_arr)},
        compiler_params=pltpu.CompilerParams(has_side_effects=pltpu.SideEffectType.DATAFLOW_SIDE_EFFECTING),
    )(*srcs, *lands, send_sems, recv_sems, *after)
    return res[n_arr:]


def _with_own_slot(land, own):
    return lax.dynamic_update_slice_in_dim(land, own[None], _device_index(), axis=0)


def _adamw(g, w, m, v):
    m_new = ADAM_B1 * m + (1.0 - ADAM_B1) * g
    v_new = ADAM_B2 * v + (1.0 - ADAM_B2) * jnp.square(g)
    m_hat = m_new / (1.0 - ADAM_B1 ** ADAM_STEP)
    v_hat = v_new / (1.0 - ADAM_B2 ** ADAM_STEP)
    return g, -ADAM_LR * (m_hat / (jnp.sqrt(v_hat) + ADAM_EPS) + ADAM_WD * w), m_new, v_new


def _reduce_adamw(gstack, w, m, v, *, name, tr=128):
    n_rows, cols = w.shape
    tr = min(tr, n_rows)
    assert n_rows % tr == 0, (name, n_rows, tr)

    def body(g_ref, w_ref, m_ref, v_ref, *out_refs):
        g = g_ref[0].astype(F32)
        for dev in range(1, N_DEV):
            g = g + g_ref[dev].astype(F32)
        for o_ref, val in zip(out_refs, _adamw(g, w_ref[...], m_ref[...], v_ref[...])):
            o_ref[...] = val

    flat = pl.BlockSpec((tr, cols), lambda i: (i, 0))
    shape = jax.ShapeDtypeStruct((n_rows, cols), F32)
    return pl.pallas_call(
        body, name=name, grid=(n_rows // tr,),
        in_specs=[pl.BlockSpec((N_DEV, tr, cols), lambda i: (0, i, 0)), flat, flat, flat],
        out_specs=[flat] * 4, out_shape=[shape] * 4, compiler_params=_params("parallel"),
    )(gstack, w, m, v)


SMALL_FLAT_SSM = ("ssm_b_re", "ssm_b_im", "ssm_c_re", "ssm_c_im")


def _small_view(name, shape):
    size = int(np.prod(shape))
    if name in SMALL_FLAT_SSM:
        return SSM_GROUPS, size // SSM_GROUPS
    if name in ("ssm_a_re", "ssm_a_im"):
        return SSM_GROUPS, SSM_STATE
    return 1, size


def _pack_rows(view):
    return -(-(view[0] * view[1]) // PACK_COLS)


def _pack_small(gs, views):
    parts = []
    for n, view in zip(SMALL, views):
        flat = gs[n].reshape(-1).astype(WIRE_DTYPE)
        parts.append(jnp.pad(flat, (0, _pack_rows(view) * PACK_COLS - flat.shape[0])))
    total = sum(p.shape[0] for p in parts) // PACK_COLS
    parts.append(jnp.zeros(((-total % PACK_ROW_ALIGN) * PACK_COLS,), WIRE_DTYPE))
    return jnp.concatenate(parts).reshape(-1, PACK_COLS)


def _small_pieces(view):
    rows, cols = view
    if cols == PACK_COLS:
        return [(0, rows, 0, 0, 0, cols)]
    if rows == 1 and cols > PACK_COLS:
        return [(kk, 1, 0, 0, kk * PACK_COLS, PACK_COLS) for kk in range(cols // PACK_COLS)]
    if rows == 1:
        return [(0, 1, 0, 0, 0, cols)]
    return [((r * cols) // PACK_COLS, 1, (r * cols) % PACK_COLS, r, 0, cols) for r in range(rows)]


def _adamw_small(stack, views, w, m, v, *, name):
    n = len(SMALL)

    def body(stack_ref, *refs):
        ins, outs = refs[:3 * n], refs[3 * n:]
        first = 0
        for i, view in enumerate(views):
            for prow, nrows, lane, orow, ocol, width in _small_pieces(view):
                src = (slice(first + prow, first + prow + nrows), slice(lane, lane + width))
                dst = (slice(orow, orow + nrows), slice(ocol, ocol + width))
                g = stack_ref[(0,) + src].astype(F32)
                for dev in range(1, N_DEV):
                    g = g + stack_ref[(dev,) + src].astype(F32)
                res = _adamw(g, ins[i][dst], ins[n + i][dst], ins[2 * n + i][dst])
                for kk, val in enumerate(res):
                    outs[kk * n + i][dst] = val
            first += _pack_rows(view)

    res = pl.pallas_call(
        body, name=name, out_shape=[jax.ShapeDtypeStruct(view, F32) for _ in range(4) for view in views],
        compiler_params=pltpu.CompilerParams(vmem_limit_bytes=VMEM_LIMIT_BYTES),
    )(stack, *w, *m, *v)
    return [res[kk * n:(kk + 1) * n] for kk in range(4)]


def kernel(x, mem, positions, ln_in_g, ln_in_b, w_in, b_in, ssm_log_dt, ssm_a_re, ssm_a_im, ssm_b_re, ssm_b_im, ssm_c_re, ssm_c_im, ssm_d, w_glu, b_glu, w_att_up, w_mix_out, b_mix_out, ln1_g, ln1_b, w_xq, w_xkv, w_xo, ln2_g, ln2_b, w_ff1, b_ff1, w_ff2, b_ff2, ln3_g, ln3_b, loss_target, m_ln_in_g, m_ln_in_b, m_w_in, m_b_in, m_ssm_log_dt, m_ssm_a_re, m_ssm_a_im, m_ssm_b_re, m_ssm_b_im, m_ssm_c_re, m_ssm_c_im, m_ssm_d, m_w_glu, m_b_glu, m_w_att_up, m_w_mix_out, m_b_mix_out, m_ln1_g, m_ln1_b, m_w_xq, m_w_xkv, m_w_xo, m_ln2_g, m_ln2_b, m_w_ff1, m_b_ff1, m_w_ff2, m_b_ff2, m_ln3_g, m_ln3_b, v_ln_in_g, v_ln_in_b, v_w_in, v_b_in, v_ssm_log_dt, v_ssm_a_re, v_ssm_a_im, v_ssm_b_re, v_ssm_b_im, v_ssm_c_re, v_ssm_c_im, v_ssm_d, v_w_glu, v_b_glu, v_w_att_up, v_w_mix_out, v_b_mix_out, v_ln1_g, v_ln1_b, v_w_xq, v_w_xkv, v_w_xo, v_ln2_g, v_ln2_b, v_w_ff1, v_b_ff1, v_w_ff2, v_b_ff2, v_ln3_g, v_ln3_b):
    given = dict(locals())
    w_arg = {n: given[n] for n in WEIGHTS}
    m_arg = {n: given["m_" + n] for n in WEIGHTS}
    v_arg = {n: given["v_" + n] for n in WEIGHTS}

    shards = {n: w_arg[n][0].astype(MXU_DTYPE) for n in BIG}
    gathers, token = [], None
    for i, names in enumerate(GATHER_GROUPS):
        handle, token = _exchange_start([shards[n] for n in names], scatter=False, after=token, name=f"gather_start_{i}")
        gathers.append(handle)

    small_views = [_small_view(n, w_arg[n].shape) for n in SMALL]
    small_wmv = [d[n].reshape(view) for d in (w_arg, m_arg, v_arg) for n, view in zip(SMALL, small_views)]
    relaid = [t for t, n in zip(small_wmv, SMALL * 3) if n in SMALL_FLAT_SSM]

    def fetch(i, after):
        lands = _exchange_wait(gathers[i], after=after + (relaid if i == 0 else []), name=f"gather_wait_{i}")
        full = {n: _with_own_slot(land, shards[n]) for n, land in zip(GATHER_GROUPS[i], lands)}
        return {n: t if n in BIG_COL_SHARDED else t.reshape(-1, t.shape[-1]) for n, t in full.items()}

    scatters = {}

    def send(i, gw):
        slots = [gw[n] if n in BIG_COL_SHARDED else gw[n].reshape(N_DEV, -1, gw[n].shape[-1]) for n in SCATTER_GROUPS[i]]
        handle, sent = _exchange_start(slots, scatter=True, name=f"scatter_start_{i}")
        scatters[i] = (handle, slots)
        return sent

    sm = {}
    for n in SMALL:
        t = w_arg[n]
        if n.startswith("ssm_") and n not in ("ssm_d", "ssm_log_dt"):
            sm[n] = t[0]
        else:
            sm[n] = t.reshape(1, -1)

    loss_row, grad_x, gs = _local_grads(x[0], mem[0], positions.reshape(-1, 1), loss_target[0], sm, fetch, send, token)
    loss = lax.psum(loss_row[0, 0], ("x", "y", "c"))
    small = _pack_small(gs, small_views)
    small_handle, _ = _exchange_start([small], scatter=False, name="small_start")

    results = [{}, {}, {}, {}]
    done = grad_x
    for i, names in enumerate(SCATTER_GROUPS):
        handle, slots = scatters[i]
        lands = _exchange_wait(handle, after=[done], name=f"scatter_wait_{i}")
        for n, land, slot in zip(names, lands, slots):
            own = lax.dynamic_index_in_dim(slot, _device_index(), axis=0, keepdims=False)
            res = _reduce_adamw(_with_own_slot(land, own), w_arg[n][0], m_arg[n][0], v_arg[n][0], name="adamw_" + n)
            done = res[0]
            for d, r in zip(results, res):
                d[n] = r[None]
    small_stack = _with_own_slot(_exchange_wait(small_handle, after=[done], name="small_wait")[0], small)
    n_small = len(SMALL)
    res = _adamw_small(small_stack, small_views, small_wmv[:n_small], small_wmv[n_small:2 * n_small],
                       small_wmv[2 * n_small:], name="adamw_small")
    for d, r in zip(results, res):
        d.update({n: t.reshape(w_arg[n].shape) for n, t in zip(SMALL, r)})
    out = [loss, grad_x[None]]
    for d in results:
        out += [d[n] for n in WEIGHTS]
    return tuple(out)
```

```python
import functools

import numpy as np
import jax
import jax.numpy as jnp
from jax import lax
from jax.experimental import pallas as pl
from jax.experimental.pallas import tpu as pltpu

F32 = jnp.float32
MXU_DTYPE = jnp.bfloat16
WIRE_DTYPE = jnp.bfloat16
VMEM_LIMIT_BYTES = 48 * 1024 * 1024
LANES = 128

N_DEV = 8
D_MODEL = 1024
SSM_GROUP = 16
SSM_WIDTH = 768
SSM_GROUPS = SSM_WIDTH // SSM_GROUP
SSM_STATE = 64
SSM_CH = SSM_GROUPS * SSM_STATE
SSM_TILES = SSM_WIDTH // LANES
GROUPS_PER_TILE = LANES // SSM_GROUP
STATE_VREG_ROWS = SSM_CH // LANES
ATT_HEAD_DIM = 64
ATT_HEADS_PER_GROUP = 4
ATT_MERGED = ATT_HEADS_PER_GROUP * ATT_HEAD_DIM
LANE_HALVES = ATT_MERGED // LANES
DILATIONS = (1, 4, 16)
ATT_BLK = 128
ATT_SCALE = ATT_HEAD_DIM ** -0.5
ROT_DIM = ATT_HEAD_DIM // 4
ROPE_THETA = 500000.0
XATT_HEADS = 4
XATT_HEAD_DIM = D_MODEL // XATT_HEADS
XATT_SCALE = XATT_HEAD_DIM ** -0.5
DEEPNORM_ALPHA = 2.0 ** 0.25
LN_EPS = 1e-5
NEG_INF = -1e30
OFF_Q_BLK, OFF_K_BLK, OFF_V_BLK = 3, 6, 9
OFF_GS_BLK, OFF_GA_BLK = 3, 4

ADAM_LR = 0.001
ADAM_B1 = 0.9
ADAM_B2 = 0.999
ADAM_EPS = 1e-08
ADAM_WD = 0.01
ADAM_STEP = 10

BIG = ("w_in", "w_glu", "w_att_up", "w_mix_out", "w_xq", "w_xkv", "w_xo", "w_ff1", "w_ff2")
BIG_COL_SHARDED = ("w_in", "w_glu", "w_att_up", "w_xkv", "w_ff1")
WEIGHTS = ("ln_in_g", "ln_in_b", "w_in", "b_in", "ssm_log_dt", "ssm_a_re", "ssm_a_im", "ssm_b_re", "ssm_b_im",
           "ssm_c_re", "ssm_c_im", "ssm_d", "w_glu", "b_glu", "w_att_up", "w_mix_out", "b_mix_out", "ln1_g", "ln1_b",
           "w_xq", "w_xkv", "w_xo", "ln2_g", "ln2_b", "w_ff1", "b_ff1", "w_ff2", "b_ff2", "ln3_g", "ln3_b")
SMALL = tuple(n for n in WEIGHTS if n not in BIG)
PACK_COLS = 1024
PACK_ROW_ALIGN = 256


def _params(*sem):
    return pltpu.CompilerParams(dimension_semantics=sem, vmem_limit_bytes=VMEM_LIMIT_BYTES)


def _dot(a, b, ca, cb):
    return lax.dot_general(a.astype(MXU_DTYPE), b.astype(MXU_DTYPE), (((ca,), (cb,)), ((), ())),
                           preferred_element_type=F32)


def _fit(dim, pref):
    if dim <= pref:
        return dim
    best = max(t for t in range(LANES, pref + 1, LANES) if dim % t == 0)
    return best


def _mm(a, b, *, name, ta=False, tb=False, bias=None, out_dtype=F32, b_shards=False, out_shards=False, after=None,
        also=None, gate=None, colsum=False, tm=2048, tn=1024, tk=1024):
    m, k = (a.shape[1], a.shape[0]) if ta else a.shape
    order = (lambda f: (lambda j, i, kk: f(i, j, kk))) if colsum else (lambda f: f)
    spec = lambda shape, f: pl.BlockSpec(shape, order(f))
    if b_shards:
        n_sh, rows, n_loc = b.shape
        if tb:
            n, tn, tk = rows, _fit(rows, tn), n_loc
            assert k == n_sh * n_loc, (name, k, b.shape)
            b_spec = spec((1, tn, tk), lambda i, j, kk: (kk, j, 0))
        else:
            n, tn, tk = n_sh * n_loc, n_loc, _fit(k, tk)
            b_spec = spec((1, tk, tn), lambda i, j, kk: (j, kk, 0))
    else:
        n = b.shape[0] if tb else b.shape[1]
        tn = n // N_DEV if out_shards else _fit(n, tn)
        tk = _fit(k, tk)
        b_spec = spec((tn, tk), lambda i, j, kk: (j, kk)) if tb else spec((tk, tn), lambda i, j, kk: (kk, j))
    tm = _fit(m, tm)
    nk = k // tk
    a_spec = spec((tk, tm), lambda i, j, kk: (kk, i)) if ta else spec((tm, tk), lambda i, j, kk: (i, kk))
    tile = spec((tm, tn), lambda i, j, kk: (i, j))
    in_specs, args = [a_spec, b_spec], [a, b]
    if bias is not None:
        in_specs.append(spec((1, tn), lambda i, j, kk: (0, j)))
        args.append(bias)
    if gate is not None:
        in_specs.append(tile)
        args.append(gate[0])
    if after is not None:
        in_specs.append(pl.BlockSpec(memory_space=pl.ANY))
        args.append(after)
    n_in = len(args)
    if out_shards:
        assert n == N_DEV * tn, (name, n, tn)
        out_specs = [spec((1, tm, tn), lambda i, j, kk: (j, i, 0))]
        out_shape = [jax.ShapeDtypeStruct((N_DEV, m, tn), out_dtype)]
    else:
        out_specs = [tile]
        out_shape = [jax.ShapeDtypeStruct((m, n), out_dtype)]
    if also is not None:
        out_specs.append(tile)
        out_shape.append(jax.ShapeDtypeStruct((m, n), also[1]))
    if colsum:
        out_specs.append(spec((1, tn), lambda i, j, kk: (0, j)))
        out_shape.append(jax.ShapeDtypeStruct((1, n), F32))

    def body(*refs):
        a_ref, b_ref = refs[0], refs[1]
        o_ref = refs[n_in]

        def product():
            return _dot(a_ref[...], b_ref[0] if b_shards else b_ref[...], 0 if ta else 1, 1 if tb else 0)

        def finish(r):
            if bias is not None:
                r = r + refs[2][...]
            if gate is not None:
                r = r * gate[1](refs[2 + (bias is not None)][...])
            if out_shards:
                o_ref[0] = r.astype(o_ref.dtype)
            else:
                o_ref[...] = r.astype(o_ref.dtype)
            if also is not None:
                refs[n_in + 1][...] = also[0](r).astype(also[1])
            if colsum:
                s_ref = refs[n_in + 1 + (also is not None)]

                @pl.when(pl.program_id(1) == 0)
                def _():
                    s_ref[...] = jnp.zeros_like(s_ref)

                s_ref[...] += _colsum(r)

        if nk == 1:
            finish(product())
            return
        acc_ref = refs[-1]
        kk = pl.program_id(2)

        @pl.when(kk == 0)
        def _():
            acc_ref[...] = jnp.zeros_like(acc_ref)

        acc_ref[...] += product()

        @pl.when(kk == nk - 1)
        def _():
            finish(acc_ref[...])

    grid = (n // tn, m // tm, nk) if colsum else (m // tm, n // tn, nk)
    res = pl.pallas_call(
        body, name=name, grid=grid, in_specs=in_specs, out_specs=out_specs, out_shape=out_shape,
        scratch_shapes=[pltpu.VMEM((tm, tn), F32)] if nk > 1 else [],
        compiler_params=_params("parallel", "arbitrary" if colsum else "parallel", "arbitrary"),
    )(*args)
    return res[0] if len(res) == 1 else res


def _rowcall(fn, rows, fulls, row_outs, acc_outs=(), *, n_rows, tm, name, after=None):
    n_r, n_f, n_o, n_a = len(rows), len(fulls), len(row_outs), len(acc_outs)
    n_in = n_r + n_f + (after is not None)
    assert n_rows % tm == 0, (name, n_rows, tm)

    def body(*refs):
        res = fn(*[r[...] for r in refs[:n_r + n_f]])
        res = tuple(res) if isinstance(res, (tuple, list)) else (res,)
        o_refs = refs[n_in:n_in + n_o]
        a_refs = refs[n_in + n_o:]
        for o_ref, val in zip(o_refs, res[:n_o]):
            o_ref[...] = val.astype(o_ref.dtype)
        if n_a:
            @pl.when(pl.program_id(0) == 0)
            def _():
                for a_ref in a_refs:
                    a_ref[...] = jnp.zeros_like(a_ref)

            for a_ref, val in zip(a_refs, res[n_o:]):
                a_ref[...] += val

    in_specs = [pl.BlockSpec((tm, w), functools.partial(lambda i, cb: (i, cb), cb=cb)) for _, w, cb in rows]
    in_specs += [pl.BlockSpec(f.shape, functools.partial(lambda i, nd: (0,) * nd, nd=f.ndim)) for f in fulls]
    in_specs += [pl.BlockSpec(memory_space=pl.ANY)] * (after is not None)
    out_specs = [pl.BlockSpec((tm, w), lambda i: (i, 0)) for w, _ in row_outs]
    out_specs += [pl.BlockSpec((1, w), lambda i: (0, 0)) for w in acc_outs]
    out_shape = [jax.ShapeDtypeStruct((n_rows, w), dt) for w, dt in row_outs]
    out_shape += [jax.ShapeDtypeStruct((1, w), F32) for w in acc_outs]
    return pl.pallas_call(
        body, name=name, grid=(n_rows // tm,), in_specs=in_specs, out_specs=out_specs, out_shape=out_shape,
        compiler_params=_params("arbitrary" if n_a else "parallel"),
    )(*[r[0] for r in rows], *fulls, *([after] if after is not None else []))


def _colsum(v):
    return jnp.sum(v, axis=0, keepdims=True)


def _ln_fwd(a, r, g, b, *, alpha, name):
    n_rows, d = a.shape

    def fn(*t):
        xin = t[0] if alpha == 1.0 else alpha * t[0]
        if r is not None:
            xin = xin + t[1]
        gv, bv = t[-2], t[-1]
        mu = jnp.mean(xin, axis=-1, keepdims=True)
        xc = xin - mu
        var = jnp.mean(xc * xc, axis=-1, keepdims=True)
        rstd = lax.rsqrt(var + LN_EPS)
        xh = xc * rstd
        y = xh * gv + bv
        return y, xh, rstd, y

    rows = [(a, d, 0)] + ([(r, d, 0)] if r is not None else [])
    return _rowcall(fn, rows, [g, b], [(d, F32), (d, F32), (1, F32), (d, MXU_DTYPE)], n_rows=n_rows, tm=256, name=name)


def _ln_bwd(dya, dyb, xh, rstd, g, *, alpha, name, operand=True):
    n_rows, d = xh.shape

    def fn(da, db, xhv, rs, gv):
        dy = alpha * da + db
        dyg = dy * gv
        m1 = jnp.mean(dyg, axis=-1, keepdims=True)
        m2 = jnp.mean(dyg * xhv, axis=-1, keepdims=True)
        dx = rs * (dyg - m1 - xhv * m2)
        return (dx,) + ((dx,) if operand else ()) + (_colsum(dy * xhv), _colsum(dy), _colsum(dx))

    rows = [(dya, d, 0), (dyb, d, 0), (xh, d, 0), (rstd, 1, 0)]
    return _rowcall(fn, rows, [g], [(d, F32)] + [(d, MXU_DTYPE)] * operand, [d, d, d], n_rows=n_rows, tm=256, name=name)


def _ln_loss_bwd(a, r, target, g, b, *, alpha, name):
    n_rows, d = a.shape

    def fn(av, rv, tv, gv, bv):
        xin = alpha * av + rv
        mu = jnp.mean(xin, axis=-1, keepdims=True)
        xc = xin - mu
        var = jnp.mean(xc * xc, axis=-1, keepdims=True)
        rs = lax.rsqrt(var + LN_EPS)
        xh = xc * rs
        diff = xh * gv + bv - tv
        part = jnp.sum(jnp.sum(diff * diff, axis=1, keepdims=True), axis=0, keepdims=True) * (0.5 / d)
        dy = diff * (1.0 / d)
        dyg = dy * gv
        m1 = jnp.mean(dyg, axis=-1, keepdims=True)
        m2 = jnp.mean(dyg * xh, axis=-1, keepdims=True)
        dx = rs * (dyg - m1 - xh * m2)
        return dx, dx, _colsum(dy * xh), _colsum(dy), _colsum(dx), jnp.broadcast_to(part, (1, LANES))

    return _rowcall(fn, [(a, d, 0), (r, d, 0), (target, d, 0)], [g, b], [(d, F32), (d, MXU_DTYPE)], [d, d, d, LANES],
                    n_rows=n_rows, tm=256, name=name)


def _rope_lane_constants():
    lane = np.arange(ATT_MERGED)
    in_head = lane % ATT_HEAD_DIM
    sign = np.where(in_head < ROT_DIM // 2, -1.0, np.where(in_head < ROT_DIM, 1.0, 0.0)).astype(np.float32)
    inv_freq = ROPE_THETA ** (-jnp.arange(0, ROT_DIM, 2, dtype=F32) / ROT_DIM)
    return inv_freq[lane % (ROT_DIM // 2)].reshape(1, ATT_MERGED), jnp.asarray(sign).reshape(1, ATT_MERGED)


def _rope_tables(pos_col, *, name, after=None):
    inv_lane, sign = _rope_lane_constants()

    def fn(pos, inv, sg):
        ang = pos.astype(F32) * inv
        return jnp.where(sg != 0.0, jnp.cos(ang), 1.0), sg * jnp.sin(ang)

    return _rowcall(fn, [(pos_col, 1, 0)], [inv_lane, sign], [(ATT_MERGED, F32), (ATT_MERGED, F32)],
                    n_rows=pos_col.shape[0], tm=512, name=name, after=after)


def _rot_partner(t):
    lane = lax.broadcasted_iota(jnp.int32, t.shape, 1)
    width = t.shape[1]
    return jnp.where((lane & (ROT_DIM // 2)) == 0, pltpu.roll(t, width - ROT_DIM // 2, 1), pltpu.roll(t, ROT_DIM // 2, 1))


def _rope(t, cos_t, sin_t):
    return t * cos_t + _rot_partner(t) * sin_t


def _rope_transpose(dt, cos_t, sin_t):
    return dt * cos_t + _rot_partner(dt * sin_t)


def _strided_rows(r, count, stride):
    return pl.ds(r, count) if stride == 1 else pl.ds(r, count, stride=stride)


def _qkv_split(proj, cos_t, sin_t, *, name, tm=512):
    n_rows = proj.shape[0]
    n_g = len(DILATIONS)

    def body(*refs):
        n_src = LANE_HALVES * 3 * n_g
        src, tables, dst = refs[:n_src], refs[n_src:n_src + 2 * LANE_HALVES], refs[n_src + 2 * LANE_HALVES:]
        for kind in range(3):
            for g, dil in enumerate(DILATIONS):
                for half in range(LANE_HALVES):
                    x_ref, o_ref = src[(kind * n_g + g) * LANE_HALVES + half], dst[kind * n_g + g]
                    cos_ref, sin_ref = tables[half], tables[LANE_HALVES + half]
                    for r in range(dil):
                        rows = _strided_rows(r, tm // dil, dil)
                        t = x_ref[rows, :]
                        if kind < 2:
                            t = _rope(t, cos_ref[rows, :], sin_ref[rows, :])
                        lo = r * ATT_MERGED + half * LANES
                        o_ref[:, lo:lo + LANES] = t.astype(o_ref.dtype)

    half_spec = lambda cb: pl.BlockSpec((tm, LANES), functools.partial(lambda i, cb: (i, cb), cb=cb))
    in_specs = [half_spec((off + g) * LANE_HALVES + half)
                for off in (OFF_Q_BLK, OFF_K_BLK, OFF_V_BLK) for g in range(n_g) for half in range(LANE_HALVES)]
    in_specs += [half_spec(half) for _ in range(2) for half in range(LANE_HALVES)]
    out_specs = [pl.BlockSpec((tm // dil, dil * ATT_MERGED), lambda i: (i, 0)) for _ in range(3) for dil in DILATIONS]
    out_shape = [jax.ShapeDtypeStruct((n_rows // dil, dil * ATT_MERGED), MXU_DTYPE) for _ in range(3) for dil in DILATIONS]
    outs = pl.pallas_call(
        body, name=name, grid=(n_rows // tm,), in_specs=in_specs, out_specs=out_specs, out_shape=out_shape,
        compiler_params=_params("parallel"),
    )(*[proj] * (LANE_HALVES * 3 * n_g), *[cos_t] * LANE_HALVES, *[sin_t] * LANE_HALVES)
    return outs[:n_g], outs[n_g:2 * n_g], outs[2 * n_g:]


def _mix(gs, ga, z1, z2, b_att):
    return jax.nn.sigmoid(gs) * (z1 * jax.nn.sigmoid(z2)) + jax.nn.sigmoid(ga) * b_att


def _mix_rows(proj, z, b_att):
    return [(proj, D_MODEL, OFF_GS_BLK), (proj, D_MODEL, OFF_GA_BLK), (z, D_MODEL, 0), (z, D_MODEL, 1), (b_att, D_MODEL, 0)]


def _mix_fwd(proj, z, b_att, *, name):
    return _rowcall(_mix, _mix_rows(proj, z, b_att), [], [(D_MODEL, MXU_DTYPE)],
                    n_rows=proj.shape[0], tm=256, name=name)[0]


def _mix_bwd(dmixed, proj, z, b_att, *, name):
    def fn(dm, gs, ga, z1, z2, ba):
        _, vjp = jax.vjp(_mix, gs, ga, z1, z2, ba)
        dgs, dga, dz1, dz2, dba = vjp(dm)
        dz = jnp.concatenate([dz1, dz2], axis=1)
        return dgs, dga, dz, dba, _colsum(dgs), _colsum(dga), _colsum(dz)

    rows = [(dmixed, D_MODEL, 0)] + _mix_rows(proj, z, b_att)
    widths = [D_MODEL, D_MODEL, 2 * D_MODEL, D_MODEL]
    return _rowcall(fn, rows, [], [(w, MXU_DTYPE) for w in widths], widths[:3], n_rows=proj.shape[0], tm=256, name=name)


def _gelu_bwd(dgy, y, proj, *, name):
    def fn(dg, yv, u):
        _, vjp = jax.vjp(jax.nn.gelu, yv)
        dy = vjp(dg)[0]
        return dy, _colsum(dy * u)

    return _rowcall(fn, [(dgy, SSM_WIDTH, 0), (y, SSM_WIDTH, 0), (proj, SSM_WIDTH, 0)], [], [(SSM_WIDTH, F32)],
                    [SSM_WIDTH], n_rows=y.shape[0], tm=512, name=name)


HEAD_ROWS = ATT_HEADS_PER_GROUP * ATT_BLK


def _head_masks(rows):
    head = lax.broadcasted_iota(jnp.int32, (rows, ATT_MERGED), 1) >> (ATT_HEAD_DIM.bit_length() - 1)
    return [head == h for h in range(ATT_HEADS_PER_GROUP)]


def _stack_heads(t, masks):
    return jnp.concatenate([jnp.where(m, t, jnp.zeros_like(t)) for m in masks], axis=0)


def _unstack_heads(t4, masks):
    blocks = [t4[h * ATT_BLK:(h + 1) * ATT_BLK] for h in range(ATT_HEADS_PER_GROUP)]
    return jnp.where(masks[0], blocks[0], jnp.where(masks[1], blocks[1], jnp.where(masks[2], blocks[2], blocks[3])))


def _head_column(stats, first):
    return jnp.concatenate([stats[:, first + h:first + h + 1] for h in range(ATT_HEADS_PER_GROUP)], axis=0)


def _band_mask(first_key):
    qi = lax.broadcasted_iota(jnp.int32, (HEAD_ROWS, 2 * ATT_BLK), 0) & (ATT_BLK - 1)
    ki = lax.broadcasted_iota(jnp.int32, (HEAD_ROWS, 2 * ATT_BLK), 1)
    steps = qi + ATT_BLK - ki
    return (steps >= 0) & (steps <= ATT_BLK) & (ki >= first_key)


def _dil_fwd(q, k, v, dil, *, name):
    n_blk = q.shape[0] // ATT_BLK
    cur = pl.BlockSpec((ATT_BLK, ATT_MERGED), lambda r, n: (n, r))
    prev = pl.BlockSpec((ATT_BLK, ATT_MERGED), lambda r, n: (jnp.maximum(n - 1, 0), r))

    def body(q_ref, kp_ref, kc_ref, vp_ref, vc_ref, o_ref, l_ref):
        masks = _head_masks(ATT_BLK)
        valid = _band_mask(jnp.where(pl.program_id(1) > 0, 0, ATT_BLK))
        keys = jnp.concatenate([kp_ref[...], kc_ref[...]], axis=0)
        vals = jnp.concatenate([vp_ref[...], vc_ref[...]], axis=0)
        s = jnp.where(valid, _dot(_stack_heads(q_ref[...], masks), keys, 1, 1) * ATT_SCALE, NEG_INF)
        m = jnp.max(s, axis=-1, keepdims=True)
        p = jnp.exp(s - m)
        den = jnp.sum(p, axis=-1, keepdims=True)
        o_ref[...] = _unstack_heads(_dot(p, vals, 1, 0) / den, masks)
        l_ref[...] = _unstack_heads(jnp.broadcast_to(m + jnp.log(den), (HEAD_ROWS, ATT_MERGED)), masks)

    shape = jax.ShapeDtypeStruct(q.shape, F32)
    return pl.pallas_call(
        body, name=name, grid=(dil, n_blk), in_specs=[cur, prev, cur, prev, cur], out_specs=[cur, cur],
        out_shape=[shape, shape], compiler_params=_params("parallel", "parallel"),
    )(q, k, k, v, v)


def _att_merge(outs, lses, *, name, tm=512):
    n_g = len(outs)
    n_rows = outs[0].shape[0] * DILATIONS[0]

    def body(*refs):
        src, (att_ref, lse_ref), tmp = refs[:2 * n_g], refs[2 * n_g:2 * n_g + 2], refs[2 * n_g + 2:]
        vals = []
        for idx, src_ref in enumerate(src):
            dil = DILATIONS[idx % n_g]
            if dil == 1:
                vals.append(src_ref[...])
                continue
            for r in range(dil):
                for half in range(LANE_HALVES):
                    lo = r * ATT_MERGED + half * LANES
                    tmp[LANE_HALVES * idx + half][_strided_rows(r, tm // dil, dil), :] = src_ref[:, lo:lo + LANES]
            vals.append(jnp.concatenate([tmp[LANE_HALVES * idx + half][...] for half in range(LANE_HALVES)], axis=1))
        o, l = vals[:n_g], vals[n_g:]
        m = functools.reduce(jnp.maximum, l)
        e = [jnp.exp(li - m) for li in l]
        z = functools.reduce(jnp.add, e)
        att_ref[...] = functools.reduce(jnp.add, [(ei / z) * oi for ei, oi in zip(e, o)])
        lse_ref[...] = m + jnp.log(z)

    in_specs = [pl.BlockSpec((tm // dil, dil * ATT_MERGED), lambda i: (i, 0)) for _ in range(2) for dil in DILATIONS]
    row = pl.BlockSpec((tm, ATT_MERGED), lambda i: (i, 0))
    shape = jax.ShapeDtypeStruct((n_rows, ATT_MERGED), F32)
    return pl.pallas_call(
        body, name=name, grid=(n_rows // tm,), in_specs=in_specs, out_specs=[row, row], out_shape=[shape, shape],
        scratch_shapes=[pltpu.VMEM((tm, LANES), F32)] * (LANE_HALVES * 2 * n_g), compiler_params=_params("parallel"),
    )(*outs, *lses)


def _att_stats(datt, att, lse, *, name):
    n_rows = datt.shape[0]

    def fn(d, a, l):
        prod = d * a
        lane = lax.broadcasted_iota(jnp.int32, (d.shape[0], LANES), 1)
        out = jnp.zeros((d.shape[0], LANES), F32)
        for h in range(ATT_HEADS_PER_GROUP):
            lo = h * ATT_HEAD_DIM
            out = jnp.where(lane == h, l[:, lo:lo + 1], out)
            delta = jnp.sum(prod[:, lo:lo + ATT_HEAD_DIM], axis=-1, keepdims=True)
            out = jnp.where(lane == ATT_HEADS_PER_GROUP + h, delta, out)
        return out

    rows = [(t, ATT_MERGED, 0) for t in (datt, att, lse)]
    return _rowcall(fn, rows, [], [(LANES, F32)], n_rows=n_rows, tm=512, name=name)[0]


def _dil_bwd(q, k, v, datt, stats, dil, *, name):
    n_rows = datt.shape[0]
    n_blk = n_rows // dil // ATT_BLK
    span = ATT_BLK * dil
    cur = pl.BlockSpec((ATT_BLK, ATT_MERGED), lambda n, r: (n, r))
    prev = pl.BlockSpec((ATT_BLK, ATT_MERGED), lambda n, r: (jnp.maximum(n - 1, 0), r))
    nxt = pl.BlockSpec((ATT_BLK, ATT_MERGED), lambda n, r: (jnp.minimum(n + 1, n_blk - 1), r))
    seq = lambda half, ahead: pl.BlockSpec((span, LANES), lambda n, r: (jnp.minimum(n + ahead, n_blk - 1), half))

    def body(qc_ref, qn_ref, kp_ref, kc_ref, vp_ref, vc_ref, dc0_ref, dc1_ref, dn0_ref, dn1_ref, sc_ref, sn_ref,
             dq0_ref, dq1_ref, dk0_ref, dk1_ref, dv0_ref, dv1_ref):
        n = pl.program_id(0)
        rows = slice(None) if dil == 1 else _strided_rows(pl.program_id(1), ATT_BLK, dil)

        def read(ref0, ref1):
            return jnp.concatenate([ref0[rows, :], ref1[rows, :]], axis=1)

        def write(ref0, ref1, val):
            ref0[rows, :] = val[:, :LANES]
            ref1[rows, :] = val[:, LANES:]

        masks = _head_masks(ATT_BLK)
        valid = _band_mask(jnp.where(n > 0, 0, ATT_BLK))
        qi = lax.broadcasted_iota(jnp.int32, (HEAD_ROWS, ATT_BLK), 0) & (ATT_BLK - 1)
        ki = lax.broadcasted_iota(jnp.int32, (HEAD_ROWS, ATT_BLK), 1)
        valid_next = (ki - qi) >= jnp.where(n < n_blk - 1, 0, ATT_BLK)

        kc, vc = kc_ref[...], vc_ref[...]
        keys = jnp.concatenate([kp_ref[...], kc], axis=0)
        vals = jnp.concatenate([vp_ref[...], vc], axis=0)
        q4 = _stack_heads(qc_ref[...], masks)
        d4 = _stack_heads(read(dc0_ref, dc1_ref).astype(MXU_DTYPE), masks)
        st = sc_ref[rows, :]
        p = jnp.where(valid, jnp.exp(_dot(q4, keys, 1, 1) * ATT_SCALE - _head_column(st, 0)), 0.0)
        ds = p * (_dot(d4, vals, 1, 1) - _head_column(st, ATT_HEADS_PER_GROUP)) * ATT_SCALE
        write(dq0_ref, dq1_ref, _unstack_heads(_dot(ds, keys, 1, 0), masks))

        q4n = _stack_heads(qn_ref[...], masks)
        d4n = _stack_heads(read(dn0_ref, dn1_ref).astype(MXU_DTYPE), masks)
        stn = sn_ref[rows, :]
        p_n = jnp.where(valid_next, jnp.exp(_dot(q4n, kc, 1, 1) * ATT_SCALE - _head_column(stn, 0)), 0.0)
        ds_n = p_n * (_dot(d4n, vc, 1, 1) - _head_column(stn, ATT_HEADS_PER_GROUP)) * ATT_SCALE
        write(dv0_ref, dv1_ref, _dot(p[:, ATT_BLK:], d4, 0, 0) + _dot(p_n, d4n, 0, 0))
        write(dk0_ref, dk1_ref, _dot(ds[:, ATT_BLK:], q4, 0, 0) + _dot(ds_n, q4n, 0, 0))

    shape = jax.ShapeDtypeStruct((n_rows, LANES), F32)
    out = seq(0, 0)
    res = pl.pallas_call(
        body, name=name, grid=(n_blk, dil),
        in_specs=[cur, nxt, prev, cur, prev, cur, seq(0, 0), seq(1, 0), seq(0, 1), seq(1, 1), seq(0, 0), seq(0, 1)],
        out_specs=[out] * 6, out_shape=[shape] * 6, compiler_params=_params("parallel", "arbitrary"),
    )(q, q, k, k, v, v, datt, datt, datt, datt, stats, stats)
    return [(res[2 * i], res[2 * i + 1]) for i in range(3)]


def _dproj_assemble(du, dqkv, dgs, dga, cos_t, sin_t, *, name):
    n_g = len(DILATIONS)

    def fn(*t):
        n_half = LANE_HALVES * 3 * n_g
        du_t, halves, (dgs_t, dga_t, c, s) = t[0], t[1:1 + n_half], t[1 + n_half:]
        parts = [jnp.concatenate(halves[LANE_HALVES * i:LANE_HALVES * (i + 1)], axis=1) for i in range(3 * n_g)]
        for i in range(2 * n_g):
            parts[i] = _rope_transpose(parts[i], c, s)
        cast = [p.astype(MXU_DTYPE) for p in parts]
        return [jnp.concatenate([du_t] + cast + [dgs_t, dga_t], axis=1)] + [_colsum(p) for p in parts]

    rows = [(du, SSM_WIDTH, 0)]
    rows += [(half, LANES, 0) for i in range(3) for g in range(n_g) for half in dqkv[g][i]]
    rows += [(dgs, D_MODEL, 0), (dga, D_MODEL, 0), (cos_t, ATT_MERGED, 0), (sin_t, ATT_MERGED, 0)]
    width = SSM_WIDTH + 3 * n_g * ATT_MERGED + 2 * D_MODEL
    res = _rowcall(fn, rows, [], [(width, MXU_DTYPE)], [ATT_MERGED] * (3 * n_g), n_rows=du.shape[0], tm=256, name=name)
    return res[0], res[1:]


def _xhead(h):
    return slice(h * XATT_HEAD_DIM, (h + 1) * XATT_HEAD_DIM)


def _xatt_probs(qh, kh):
    s = _dot(qh, kh, 1, 1) * XATT_SCALE
    e = jnp.exp(s - jnp.max(s, axis=-1, keepdims=True))
    return e / jnp.sum(e, axis=-1, keepdims=True)


def _xatt_fwd(q, kv, *, name, tm=512):
    n_rows = q.shape[0]
    n_mem = kv.shape[0]

    def body(q_ref, kv_ref, o_ref):
        for h in range(XATT_HEADS):
            sl = _xhead(h)
            p = _xatt_probs(q_ref[:, sl], kv_ref[:, sl])
            o_ref[:, sl] = _dot(p, kv_ref[:, D_MODEL + h * XATT_HEAD_DIM:D_MODEL + (h + 1) * XATT_HEAD_DIM], 1, 0
                                ).astype(o_ref.dtype)

    row = pl.BlockSpec((tm, D_MODEL), lambda i: (i, 0))
    return pl.pallas_call(
        body, name=name, grid=(n_rows // tm,),
        in_specs=[row, pl.BlockSpec((n_mem, 2 * D_MODEL), lambda i: (0, 0))], out_specs=row,
        out_shape=jax.ShapeDtypeStruct((n_rows, D_MODEL), MXU_DTYPE), compiler_params=_params("parallel"),
    )(q, kv)


def _xatt_bwd(q, kv, do, *, name, tm=512):
    n_rows = q.shape[0]
    n_mem = kv.shape[0]

    def body(q_ref, kv_ref, do_ref, dq_ref, dkv_ref):
        @pl.when(pl.program_id(0) == 0)
        def _():
            dkv_ref[...] = jnp.zeros_like(dkv_ref)

        for h in range(XATT_HEADS):
            sl = _xhead(h)
            vsl = slice(D_MODEL + h * XATT_HEAD_DIM, D_MODEL + (h + 1) * XATT_HEAD_DIM)
            qh, kh, doh = q_ref[:, sl], kv_ref[:, sl], do_ref[:, sl]
            p = _xatt_probs(qh, kh)
            dp = _dot(doh, kv_ref[:, vsl], 1, 1)
            ds = p * (dp - jnp.sum(dp * p, axis=-1, keepdims=True)) * XATT_SCALE
            dq_ref[:, sl] = _dot(ds, kh, 1, 0).astype(dq_ref.dtype)
            dkv_ref[:, sl] += _dot(ds, qh, 0, 0)
            dkv_ref[:, vsl] += _dot(p, doh, 0, 0)

    row = pl.BlockSpec((tm, D_MODEL), lambda i: (i, 0))
    full = pl.BlockSpec((n_mem, 2 * D_MODEL), lambda i: (0, 0))
    return pl.pallas_call(
        body, name=name, grid=(n_rows // tm,), in_specs=[row, full, row], out_specs=[row, full],
        out_shape=[jax.ShapeDtypeStruct((n_rows, D_MODEL), MXU_DTYPE), jax.ShapeDtypeStruct((n_mem, 2 * D_MODEL), F32)],
        compiler_params=_params("arbitrary"),
    )(q, kv, do)


def _disc(logdt, a_re, a_im, b_re, b_im):
    dt = jnp.exp(logdt)
    mag = jnp.exp(a_re * dt)
    ab_re = mag * jnp.cos(a_im * dt)
    ab_im = mag * jnp.sin(a_im * dt)
    den = jnp.square(a_re) + jnp.square(a_im)
    nr = ab_re - 1.0
    f_re = (nr * a_re + ab_im * a_im) / den
    f_im = (ab_im * a_re - nr * a_im) / den
    bb_re = f_re[None] * b_re - f_im[None] * b_im
    bb_im = f_re[None] * b_im + f_im[None] * b_re
    return ab_re, ab_im, bb_re, bb_im


def _disc_transpose(logdt, a_re, a_im, b_re, b_im, g_ab_re, g_ab_im, g_bb_re, g_bb_im):
    dt = jnp.exp(logdt)
    mag = jnp.exp(a_re * dt)
    th = a_im * dt
    cs, sn = jnp.cos(th), jnp.sin(th)
    ab_re, ab_im = mag * cs, mag * sn
    den = jnp.square(a_re) + jnp.square(a_im)
    nr = ab_re - 1.0
    f_re = (nr * a_re + ab_im * a_im) / den
    f_im = (ab_im * a_re - nr * a_im) / den
    d_f_re = jnp.sum(g_bb_re * b_re + g_bb_im * b_im, axis=0)
    d_f_im = jnp.sum(g_bb_im * b_re - g_bb_re * b_im, axis=0)
    d_b_re = g_bb_re * f_re[None] + g_bb_im * f_im[None]
    d_b_im = g_bb_im * f_re[None] - g_bb_re * f_im[None]
    d_n_re, d_n_im = d_f_re / den, d_f_im / den
    d_den = -(d_f_re * f_re + d_f_im * f_im) / den
    d_ab_re = g_ab_re + d_n_re * a_re - d_n_im * a_im
    d_ab_im = g_ab_im + d_n_re * a_im + d_n_im * a_re
    d_a_re = d_n_re * nr + d_n_im * ab_im + 2.0 * d_den * a_re
    d_a_im = d_n_re * ab_im - d_n_im * nr + 2.0 * d_den * a_im
    d_mag = d_ab_re * cs + d_ab_im * sn
    d_th = mag * (d_ab_im * cs - d_ab_re * sn)
    d_a_re = d_a_re + d_mag * mag * dt
    d_a_im = d_a_im + d_th * dt
    d_dt = jnp.sum(d_mag * mag * a_re + d_th * a_im, axis=-1, keepdims=True)
    return d_dt * dt, d_a_re, d_a_im, d_b_re, d_b_im


def _whole(fn, args, out_shapes, *, name):
    n_in = len(args)

    def body(*refs):
        res = fn(*[r[...] for r in refs[:n_in]])
        for o_ref, val in zip(refs[n_in:], res):
            o_ref[...] = val

    return pl.pallas_call(body, name=name, out_shape=[jax.ShapeDtypeStruct(s, F32) for s in out_shapes],
                          compiler_params=pltpu.CompilerParams(vmem_limit_bytes=VMEM_LIMIT_BYTES))(*args)


def _tiles_cn(t):
    t = t.reshape(-1, SSM_TILES, GROUPS_PER_TILE, SSM_GROUP, SSM_STATE)
    eye = jnp.eye(GROUPS_PER_TILE, dtype=t.dtype)
    return (t[:, :, :, :, None, :] * eye[:, None, :, None]).reshape(-1, SSM_TILES, LANES, GROUPS_PER_TILE * SSM_STATE)


def _tiles_nc(t):
    t = t.reshape(-1, SSM_TILES, GROUPS_PER_TILE, SSM_STATE, SSM_GROUP)
    eye = jnp.eye(GROUPS_PER_TILE, dtype=t.dtype)
    return (t[:, :, :, :, None, :] * eye[:, None, :, None]).reshape(-1, SSM_TILES, GROUPS_PER_TILE * SSM_STATE, LANES)


def _untile_cn(t):
    t = t.reshape(SSM_TILES, GROUPS_PER_TILE, SSM_GROUP, GROUPS_PER_TILE, SSM_STATE)
    eye = jnp.eye(GROUPS_PER_TILE, dtype=t.dtype)
    return jnp.sum(t * eye[None, :, None, :, None], axis=3).reshape(SSM_GROUPS, SSM_GROUP, SSM_STATE)


SSM_WIDE = GROUPS_PER_TILE * SSM_STATE
LANE_GROUPS_PER_TILE = SSM_WIDE // LANES


def _chan(j):
    return slice(j * LANES, (j + 1) * LANES)


def _time_major_rows(j, q, tc):
    return pl.ds(j * LANE_GROUPS_PER_TILE + q, tc, stride=STATE_VREG_ROWS)


def _to_time_major(x, t_re_ref, t_im_ref, dst_re, dst_im, tc):
    for j in range(SSM_TILES):
        xj = x[:, _chan(j)]
        for t_ref, dst in ((t_re_ref, dst_re), (t_im_ref, dst_im)):
            r = _dot(xj, t_ref[j], 1, 0)
            for q in range(LANE_GROUPS_PER_TILE):
                dst[_time_major_rows(j, q, tc), :] = r[:, q * LANES:(q + 1) * LANES]


def _from_time_major(src, j, tc):
    return jnp.concatenate([src[_time_major_rows(j, q, tc), :] for q in range(LANE_GROUPS_PER_TILE)], axis=1)


def _scan_chunk(w_re, w_im, h_re, h_im, a_re, a_im, start, tc):
    def step(t, carry):
        hr, hi = carry
        rows = _scan_rows(t)
        nr = a_re * hr - a_im * hi + w_re[rows, :]
        ni = a_re * hi + a_im * hr + w_im[rows, :]
        h_re[rows, :] = nr
        h_im[rows, :] = ni
        return nr, ni

    return lax.fori_loop(0, tc, step, start, unroll=8)


SSM_CHUNK = 256


def _tile_spec(stack, k):
    return pl.BlockSpec((pl.Squeezed(),) + tuple(stack.shape[1:]), lambda i: (k, 0, 0, 0))


def _ssm_fwd(proj, tiles_cn, tiles_nc, a_re, a_im, gain, *, name, tc=SSM_CHUNK):
    n_rows = proj.shape[0]
    n_chunk = n_rows // tc

    def body(u_ref, tbr_ref, tbi_ref, tcr_ref, tci_ref, ar_ref, ai_ref, g_ref, y_ref, gy_ref, hr, hi, wr, wi, state):
        @pl.when(pl.program_id(0) == 0)
        def _():
            state[...] = jnp.zeros_like(state)

        u = u_ref[...]
        _to_time_major(u, tbr_ref, tbi_ref, wr, wi, tc)
        state[0], state[1] = _scan_chunk(wr, wi, hr, hi, ar_ref[...], ai_ref[...], (state[0], state[1]), tc)
        for j in range(SSM_TILES):
            yj = (_dot(_from_time_major(hr, j, tc), tcr_ref[j], 1, 0) + _dot(_from_time_major(hi, j, tc), tci_ref[j], 1, 0)
                  + g_ref[:, _chan(j)] * u[:, _chan(j)])
            y_ref[:, _chan(j)] = yj
            gy_ref[:, _chan(j)] = jax.nn.gelu(yj).astype(gy_ref.dtype)

    rows = pl.BlockSpec((tc, SSM_WIDTH), lambda i: (i, 0))
    coef = pl.BlockSpec((STATE_VREG_ROWS, LANES), lambda i: (0, 0))
    states = pl.BlockSpec((tc * STATE_VREG_ROWS, LANES), lambda i: (i, 0))
    sshape = jax.ShapeDtypeStruct((n_rows * STATE_VREG_ROWS, LANES), F32)
    return pl.pallas_call(
        body, name=name, grid=(n_chunk,),
        in_specs=[rows, _tile_spec(tiles_cn, 0), _tile_spec(tiles_cn, 1), _tile_spec(tiles_nc, 0), _tile_spec(tiles_nc, 1),
                  coef, coef, pl.BlockSpec((1, SSM_WIDTH), lambda i: (0, 0))],
        out_specs=[rows, rows, states, states],
        out_shape=[jax.ShapeDtypeStruct((n_rows, SSM_WIDTH), F32), jax.ShapeDtypeStruct((n_rows, SSM_WIDTH), MXU_DTYPE),
                   sshape, sshape],
        scratch_shapes=[pltpu.VMEM((tc * STATE_VREG_ROWS, LANES), F32)] * 2 + [pltpu.VMEM((2, STATE_VREG_ROWS, LANES), F32)],
        compiler_params=_params("arbitrary"),
    )(proj, tiles_cn, tiles_cn, tiles_nc, tiles_nc, a_re, a_im, gain)


def _ssm_bwd(proj, dy, h_re, h_im, tiles_cn, tiles_nc, a_re, a_im, gain, *, name, tc=SSM_CHUNK):
    n_rows = proj.shape[0]
    n_chunk = n_rows // tc

    def body(u_ref, dy_ref, hr, hi, tdr_ref, tdi_ref, tur_ref, tui_ref, ar_ref, ai_ref, g_ref,
             du_ref, su_ref, dcr_ref, dci_ref, dbr_ref, dbi_ref, dar_ref, dai_ref, wr, wi, carry):
        @pl.when(pl.program_id(0) == 0)
        def _():
            carry[...] = jnp.zeros_like(carry)
            for acc_ref in (su_ref, dcr_ref, dci_ref, dbr_ref, dbi_ref):
                acc_ref[...] = jnp.zeros_like(acc_ref)

        a_r, a_i = ar_ref[...], ai_ref[...]
        u, dyv = u_ref[...], dy_ref[...]
        _to_time_major(dyv, tdr_ref, tdi_ref, wr, wi, tc)

        def step(kk, c):
            lam_r, lam_i, dar, dai = c
            rows = _scan_rows(tc - 1 - kk)
            h_r, h_i = hr[rows, :], hi[rows, :]
            dar = dar + lam_r * h_r + lam_i * h_i
            dai = dai + lam_i * h_r - lam_r * h_i
            new_r = wr[rows, :] + a_r * lam_r + a_i * lam_i
            new_i = wi[rows, :] + a_r * lam_i - a_i * lam_r
            wr[rows, :] = new_r
            wi[rows, :] = new_i
            return new_r, new_i, dar, dai

        carry[0], carry[1], carry[2], carry[3] = lax.fori_loop(0, tc, step, (carry[0], carry[1], carry[2], carry[3]),
                                                              unroll=8)
        dar_ref[...] = carry[2]
        dai_ref[...] = carry[3]
        for j in range(SSM_TILES):
            cj = _chan(j)
            lam_r, lam_i = _from_time_major(wr, j, tc), _from_time_major(wi, j, tc)
            dcr_ref[j] += _dot(dyv[:, cj], _from_time_major(hr, j, tc), 0, 0)
            dci_ref[j] += _dot(dyv[:, cj], _from_time_major(hi, j, tc), 0, 0)
            dbr_ref[j] += _dot(u[:, cj], lam_r, 0, 0)
            dbi_ref[j] += _dot(u[:, cj], lam_i, 0, 0)
            duj = _dot(lam_r, tur_ref[j], 1, 0) + _dot(lam_i, tui_ref[j], 1, 0) + g_ref[:, cj] * dyv[:, cj]
            du_ref[:, cj] = duj.astype(du_ref.dtype)
            su_ref[:, cj] += _colsum(duj)

    back = lambda i: (n_chunk - 1 - i, 0)
    rows = pl.BlockSpec((tc, SSM_WIDTH), back)
    in_tile = pl.BlockSpec((SSM_TILES, LANES, SSM_WIDE), lambda i: (0, 0, 0))
    coef = pl.BlockSpec((STATE_VREG_ROWS, LANES), lambda i: (0, 0))
    states = pl.BlockSpec((tc * STATE_VREG_ROWS, LANES), back)
    vec = pl.BlockSpec((1, SSM_WIDTH), lambda i: (0, 0))
    tshape = jax.ShapeDtypeStruct((SSM_TILES, LANES, SSM_WIDE), F32)
    cshape = jax.ShapeDtypeStruct((STATE_VREG_ROWS, LANES), F32)
    return pl.pallas_call(
        body, name=name, grid=(n_chunk,),
        in_specs=[rows, rows, states, states, _tile_spec(tiles_cn, 2), _tile_spec(tiles_cn, 3), _tile_spec(tiles_nc, 2),
                  _tile_spec(tiles_nc, 3), coef, coef, vec],
        out_specs=[rows, vec, in_tile, in_tile, in_tile, in_tile, coef, coef],
        out_shape=[jax.ShapeDtypeStruct((n_rows, SSM_WIDTH), MXU_DTYPE), jax.ShapeDtypeStruct((1, SSM_WIDTH), F32),
                   tshape, tshape, tshape, tshape, cshape, cshape],
        scratch_shapes=[pltpu.VMEM((tc * STATE_VREG_ROWS, LANES), F32)] * 2 + [pltpu.VMEM((4, STATE_VREG_ROWS, LANES), F32)],
        compiler_params=_params("arbitrary"),
    )(proj, dy, h_re, h_im, tiles_cn, tiles_cn, tiles_nc, tiles_nc, a_re, a_im, gain)


def _scan_rows(t):
    return pl.ds(pl.multiple_of(t * STATE_VREG_ROWS, 8), STATE_VREG_ROWS)


GATHER_GROUPS = (("w_in",), ("w_glu", "w_att_up", "w_mix_out"), ("w_xq", "w_xkv", "w_xo", "w_ff1", "w_ff2"))
SCATTER_GROUPS = (("w_ff2", "w_ff1"), ("w_xo", "w_xq", "w_xkv", "w_mix_out"), ("w_att_up", "w_glu", "w_in"))


def _local_grads(x, mem, pos_col, target, sm, fetch, send, start_token):
    b_re_t = sm["ssm_b_re"].transpose(2, 0, 1)
    b_im_t = sm["ssm_b_im"].transpose(2, 0, 1)
    logdt = sm["ssm_log_dt"].reshape(SSM_GROUPS, 1)
    c_re, c_im = sm["ssm_c_re"], sm["ssm_c_im"]
    grp = (SSM_GROUPS, SSM_STATE)
    chn = (SSM_GROUP, SSM_GROUPS, SSM_STATE)

    wts = {}
    cos_t, sin_t = _rope_tables(pos_col, after=start_token, name="rope_tables")
    h0, xh0, rs0, h0m = _ln_fwd(x, None, sm["ln_in_g"], sm["ln_in_b"], alpha=1.0, name="ln_in_fwd")
    disc_in = (logdt, sm["ssm_a_re"], sm["ssm_a_im"], b_re_t, b_im_t)
    ab_re, ab_im, bb_re_t, bb_im_t = _whole(_disc, disc_in, [grp, grp, chn, chn], name="ssm_disc")
    a_re_rows, a_im_rows = ab_re.reshape(STATE_VREG_ROWS, LANES), ab_im.reshape(STATE_VREG_ROWS, LANES)
    tiles_cn = _tiles_cn(jnp.stack([bb_re_t.transpose(1, 0, 2), bb_im_t.transpose(1, 0, 2), c_re, -c_im])
                         ).astype(MXU_DTYPE)
    tiles_nc = _tiles_nc(jnp.stack([c_re.transpose(0, 2, 1), -c_im.transpose(0, 2, 1), bb_re_t.transpose(1, 2, 0),
                                    bb_im_t.transpose(1, 2, 0)])).astype(MXU_DTYPE)
    wts.update(fetch(0, [h0m, tiles_cn, tiles_nc]))
    proj = _mm(h0m, wts["w_in"], bias=sm["b_in"], b_shards=True, name="in_proj")

    y, gy, h_re, h_im = _ssm_fwd(proj, tiles_cn, tiles_nc, a_re_rows, a_im_rows, sm["ssm_d"], name="ssm_fwd")

    q, k, v = _qkv_split(proj, cos_t, sin_t, name="qkv_split")
    outs, lses = [], []
    for g, dil in enumerate(DILATIONS):
        o_g, l_g = _dil_fwd(q[g], k[g], v[g], dil, name=f"dil_att_fwd_{dil}")
        outs.append(o_g)
        lses.append(l_g)
    att, lse = _att_merge(outs, lses, name="att_merge")
    wts.update(fetch(1, [att]))
    z = _mm(gy, wts["w_glu"], bias=sm["b_glu"], b_shards=True, name="glu_proj")
    b_att = _mm(att, wts["w_att_up"], b_shards=True, name="att_up")

    mixed = _mix_fwd(proj, z, b_att, name="gate_mix")
    mix_out = _mm(mixed, wts["w_mix_out"], bias=sm["b_mix_out"], name="mix_out")
    h1, xh1, rs1, h1m = _ln_fwd(h0, mix_out, sm["ln1_g"], sm["ln1_b"], alpha=DEEPNORM_ALPHA, name="ln1_fwd")

    wts.update(fetch(2, [h1m]))
    xq = _mm(h1m, wts["w_xq"], out_dtype=MXU_DTYPE, name="xatt_q")
    kv = _mm(mem, wts["w_xkv"], out_dtype=MXU_DTYPE, b_shards=True, name="xatt_kv")
    xo_in = _xatt_fwd(xq, kv, name="xatt_fwd")
    xo = _mm(xo_in, wts["w_xo"], name="xatt_o")
    h2, xh2, rs2, h2m = _ln_fwd(h1, xo, sm["ln2_g"], sm["ln2_b"], alpha=DEEPNORM_ALPHA, name="ln2_fwd")

    pre, act = _mm(h2m, wts["w_ff1"], bias=sm["b_ff1"], b_shards=True, name="ff1",
                   also=(lambda r: jnp.square(jnp.maximum(r, 0.0)), MXU_DTYPE))
    ff = _mm(act, wts["w_ff2"], bias=sm["b_ff2"], name="ff2")

    gw, gs = {}, {}
    dr3, dr3m, gs["ln3_g"], gs["ln3_b"], gs["b_ff2"], loss_row = _ln_loss_bwd(
        h2, ff, target, sm["ln3_g"], sm["ln3_b"], alpha=DEEPNORM_ALPHA, name="ln3_loss")
    wgrad = functools.partial(_mm, ta=True, out_dtype=WIRE_DTYPE, tk=2048)
    gw["w_ff2"] = wgrad(act, dr3m, tk=1024, name="ff2_dw")
    dpre, gs["b_ff1"] = _mm(dr3m, wts["w_ff2"], tb=True, out_dtype=MXU_DTYPE, colsum=True, name="ff2_dx",
                            gate=(pre, lambda p: 2.0 * jnp.maximum(p, 0.0)))
    gw["w_ff1"] = wgrad(h2m, dpre, out_shards=True, name="ff1_dw")
    sent = send(0, gw)
    dh2 = _mm(dpre, wts["w_ff1"], tb=True, b_shards=True, after=sent, name="ff1_dx")

    dr2, dr2m, gs["ln2_g"], gs["ln2_b"], _ = _ln_bwd(dr3, dh2, xh2, rs2, sm["ln2_g"], alpha=DEEPNORM_ALPHA,
                                                     name="ln2_bwd")
    gw["w_xo"] = wgrad(xo_in, dr2m, name="xatt_o_dw")
    dxo_in = _mm(dr2m, wts["w_xo"], tb=True, out_dtype=MXU_DTYPE, name="xatt_o_dx")
    dxq, dkv = _xatt_bwd(xq, kv, dxo_in, name="xatt_bwd")
    gw["w_xq"] = wgrad(h1m, dxq, name="xatt_q_dw")
    gw["w_xkv"] = wgrad(mem, dkv, out_shards=True, name="xatt_kv_dw")
    dh1 = _mm(dxq, wts["w_xq"], tb=True, name="xatt_q_dx")

    dr1, dr1m, gs["ln1_g"], gs["ln1_b"], gs["b_mix_out"] = _ln_bwd(dr2, dh1, xh1, rs1, sm["ln1_g"],
                                                                   alpha=DEEPNORM_ALPHA, name="ln1_bwd")
    gw["w_mix_out"] = wgrad(mixed, dr1m, name="mix_out_dw")
    sent = send(1, gw)
    dmixed = _mm(dr1m, wts["w_mix_out"], tb=True, after=sent, name="mix_out_dx")
    dgs, dga, dz, db_att, s_gs, s_ga, gs["b_glu"] = _mix_bwd(dmixed, proj, z, b_att, name="gate_mix_bwd")

    gw["w_att_up"] = wgrad(att, db_att, out_shards=True, name="att_up_dw")
    datt = _mm(db_att, wts["w_att_up"], tb=True, b_shards=True, name="att_up_dx")
    stats = _att_stats(datt, att, lse, name="att_stats")
    dqkv = [_dil_bwd(q[g], k[g], v[g], datt, stats, dil, name=f"dil_att_bwd_{dil}") for g, dil in enumerate(DILATIONS)]

    gw["w_glu"] = wgrad(gy, dz, out_shards=True, name="glu_dw")
    dgy = _mm(dz, wts["w_glu"], tb=True, b_shards=True, name="glu_dx")
    dy, gs["ssm_d"] = _gelu_bwd(dgy, y, proj, name="gelu_bwd")
    du, s_u, dc_re_t, dc_im_t, dbb_re_t, dbb_im_t, da_re, da_im = _ssm_bwd(
        proj, dy, h_re, h_im, tiles_cn, tiles_nc, a_re_rows, a_im_rows, sm["ssm_d"], name="ssm_bwd")
    gs["ssm_c_re"], gs["ssm_c_im"] = _untile_cn(dc_re_t), -_untile_cn(dc_im_t)
    disc_ct = (da_re.reshape(grp), da_im.reshape(grp), _untile_cn(dbb_re_t).transpose(1, 0, 2),
               _untile_cn(dbb_im_t).transpose(1, 0, 2))
    d_logdt, gs["ssm_a_re"], gs["ssm_a_im"], d_b_re_t, d_b_im_t = _whole(
        _disc_transpose, disc_in + disc_ct, [(SSM_GROUPS, 1), grp, grp, chn, chn], name="ssm_disc_bwd")
    gs["ssm_log_dt"] = d_logdt
    gs["ssm_b_re"], gs["ssm_b_im"] = d_b_re_t.transpose(1, 2, 0), d_b_im_t.transpose(1, 2, 0)

    dproj, s_qkv = _dproj_assemble(du, dqkv, dgs, dga, cos_t, sin_t, name="dproj_assemble")
    gs["b_in"] = jnp.concatenate([s_u, *s_qkv, s_gs, s_ga], axis=1)
    gw["w_in"] = wgrad(h0m, dproj, out_shards=True, name="in_proj_dw")
    sent = send(2, gw)
    dh0 = _mm(dproj, wts["w_in"], tb=True, b_shards=True, after=sent, name="in_proj_dx")
    grad_x, gs["ln_in_g"], gs["ln_in_b"], _ = _ln_bwd(dr1, dh0, xh0, rs0, sm["ln_in_g"], alpha=DEEPNORM_ALPHA,
                                                      operand=False, name="ln_in_bwd")
    return loss_row, grad_x, gs


N_PEER = N_DEV - 1
_IN_HBM = pl.BlockSpec(memory_space=pltpu.HBM)
_IN_SEMAPHORE = pl.BlockSpec(memory_space=pltpu.SEMAPHORE)


def _device_index():
    return 4 * lax.axis_index("x") + 2 * lax.axis_index("y") + lax.axis_index("c")


def _exchange_copies(src_refs, land_refs, send_sems, recv_sems, scatter):
    x, y, c = lax.axis_index("x"), lax.axis_index("y"), lax.axis_index("c")
    me = 4 * x + 2 * y + c
    pairs = []
    for a, (src_ref, land_ref) in enumerate(zip(src_refs, land_refs)):
        for kk in range(1, N_DEV):
            px = (x + (kk >> 2)) % 2
            py = (y + ((kk >> 1) & 1)) % 2
            pc = (c + (kk & 1)) % 2
            peer = 4 * px + 2 * py + pc
            sem = a * N_PEER + kk - 1
            src = src_ref.at[peer] if scatter else src_ref

            def copy(dst, src=src, sem=sem, px=px, py=py, pc=pc):
                return pltpu.make_async_remote_copy(
                    src_ref=src, dst_ref=dst, send_sem=send_sems.at[sem], recv_sem=recv_sems.at[sem],
                    device_id=(px, py, pc), device_id_type=pl.DeviceIdType.MESH)

            pairs.append((functools.partial(copy, land_ref.at[me]), functools.partial(copy, land_ref.at[peer])))
    return pairs


def _own_copies(src_refs, land_refs, own_sems, scatter):
    me = _device_index()
    return [functools.partial(pltpu.make_async_copy, src_ref.at[me] if scatter else src_ref, land_ref.at[me],
                              own_sems.at[a]) for a, (src_ref, land_ref) in enumerate(zip(src_refs, land_refs))]


def _exchange_start(srcs, *, scatter, name, after=None):
    n_arr = len(srcs)
    lands = [lax.empty((N_DEV,) + tuple(s.shape[1:] if scatter else s.shape), s.dtype) for s in srcs]
    n_in = 2 * n_arr + (after is not None)

    def body(*refs):
        send_sems, recv_sems, own_sems = refs[n_in], refs[n_in + 1], refs[n_in + 2]
        for sent, _ in _exchange_copies(refs[:n_arr], refs[n_arr:2 * n_arr], send_sems, recv_sems, scatter):
            sent().start()
        for own in _own_copies(refs[:n_arr], refs[n_arr:2 * n_arr], own_sems, scatter):
            own().start()
        refs[-1][...] = jnp.zeros_like(refs[-1])

    through = [pltpu.HBM(t.shape, t.dtype) for t in (*srcs, *lands)]
    res = pl.pallas_call(
        body, name=name,
        out_shape=(pltpu.SemaphoreType.DMA((n_arr * N_PEER,)), pltpu.SemaphoreType.DMA((n_arr * N_PEER,)),
                   pltpu.SemaphoreType.DMA((n_arr,)), *through, jax.ShapeDtypeStruct((8, LANES), F32)),
        in_specs=[_IN_HBM] * (2 * n_arr) + [pl.BlockSpec(memory_space=pl.ANY)] * (after is not None),
        out_specs=(_IN_SEMAPHORE, _IN_SEMAPHORE, _IN_SEMAPHORE, *[_IN_HBM] * (2 * n_arr),
                   pl.BlockSpec(memory_space=pltpu.VMEM)),
        input_output_aliases={i: 3 + i for i in range(2 * n_arr)},
        compiler_params=pltpu.CompilerParams(has_side_effects=pltpu.SideEffectType.DATAFLOW_SIDE_EFFECTING),
    )(*[pltpu.with_memory_space_constraint(t, pltpu.HBM) for t in (*srcs, *lands)],
      *([after] if after is not None else []))
    return (res[0], res[1], res[2], res[3:3 + n_arr], res[3 + n_arr:3 + 2 * n_arr], scatter), res[-1]


def _exchange_wait(handle, *, after, name):
    send_sems, recv_sems, own_sems, srcs, lands, scatter = handle
    n_arr = len(srcs)
    after = list(after)

    def body(*refs):
        src_refs, land_refs = refs[:n_arr], refs[n_arr:2 * n_arr]
        for sent, received in _exchange_copies(src_refs, land_refs, refs[2 * n_arr], refs[2 * n_arr + 1], scatter):
            sent().wait_send()
            received().wait_recv()
        for own in _own_copies(src_refs, land_refs, refs[2 * n_arr + 2], scatter):
            own().wait()

    res = pl.pallas_call(
        body, name=name, out_shape=tuple(pltpu.HBM(t.shape, t.dtype) for t in (*srcs, *lands)),
        in_specs=[_IN_HBM] * (2 * n_arr) + [_IN_SEMAPHORE] * 3 + [pl.BlockSpec(memory_space=pl.ANY)] * len(after),
        out_specs=tuple([_IN_HBM] * (2 * n_arr)), input_output_aliases={i: i for i in range(2 * n_arr)},
        compiler_params=pltpu.CompilerParams(has_side_effects=pltpu.SideEffectType.DATAFLOW_SIDE_EFFECTING),
    )(*srcs, *lands, send_sems, recv_sems, own_sems, *after)
    return res[n_arr:]


def _adamw(g, w, m, v):
    m_new = ADAM_B1 * m + (1.0 - ADAM_B1) * g
    v_new = ADAM_B2 * v + (1.0 - ADAM_B2) * jnp.square(g)
    m_hat = m_new / (1.0 - ADAM_B1 ** ADAM_STEP)
    v_hat = v_new / (1.0 - ADAM_B2 ** ADAM_STEP)
    return g, -ADAM_LR * (m_hat / (jnp.sqrt(v_hat) + ADAM_EPS) + ADAM_WD * w), m_new, v_new


def _reduce_adamw(gstack, w, m, v, *, name, tr=128):
    n_rows, cols = w.shape
    tr = min(tr, n_rows)
    assert n_rows % tr == 0, (name, n_rows, tr)

    def body(g_ref, w_ref, m_ref, v_ref, *out_refs):
        g = g_ref[0].astype(F32)
        for dev in range(1, N_DEV):
            g = g + g_ref[dev].astype(F32)
        for o_ref, val in zip(out_refs, _adamw(g, w_ref[...], m_ref[...], v_ref[...])):
            o_ref[...] = val

    flat = pl.BlockSpec((tr, cols), lambda i: (i, 0))
    shape = jax.ShapeDtypeStruct((n_rows, cols), F32)
    return pl.pallas_call(
        body, name=name, grid=(n_rows // tr,),
        in_specs=[pl.BlockSpec((N_DEV, tr, cols), lambda i: (0, i, 0)), flat, flat, flat],
        out_specs=[flat] * 4, out_shape=[shape] * 4, compiler_params=_params("parallel"),
    )(gstack, w, m, v)


SMALL_FLAT_SSM = ("ssm_b_re", "ssm_b_im", "ssm_c_re", "ssm_c_im")


def _small_view(name, shape):
    size = int(np.prod(shape))
    if name in SMALL_FLAT_SSM:
        return SSM_GROUPS, size // SSM_GROUPS
    if name in ("ssm_a_re", "ssm_a_im"):
        return SSM_GROUPS, SSM_STATE
    return 1, size


def _pack_rows(view):
    return -(-(view[0] * view[1]) // PACK_COLS)


def _pack_small(gs, views):
    parts = []
    for n, view in zip(SMALL, views):
        flat = gs[n].reshape(-1).astype(WIRE_DTYPE)
        parts.append(jnp.pad(flat, (0, _pack_rows(view) * PACK_COLS - flat.shape[0])))
    total = sum(p.shape[0] for p in parts) // PACK_COLS
    parts.append(jnp.zeros(((-total % PACK_ROW_ALIGN) * PACK_COLS,), WIRE_DTYPE))
    return jnp.concatenate(parts).reshape(-1, PACK_COLS)


def _small_pieces(view):
    rows, cols = view
    if cols == PACK_COLS:
        return [(0, rows, 0, 0, 0, cols)]
    if rows == 1 and cols > PACK_COLS:
        return [(kk, 1, 0, 0, kk * PACK_COLS, PACK_COLS) for kk in range(cols // PACK_COLS)]
    if rows == 1:
        return [(0, 1, 0, 0, 0, cols)]
    return [((r * cols) // PACK_COLS, 1, (r * cols) % PACK_COLS, r, 0, cols) for r in range(rows)]


def _adamw_small(stack, views, w, m, v, *, name):
    n = len(SMALL)

    def body(stack_ref, *refs):
        ins, outs = refs[:3 * n], refs[3 * n:]
        first = 0
        for i, view in enumerate(views):
            for prow, nrows, lane, orow, ocol, width in _small_pieces(view):
                src = (slice(first + prow, first + prow + nrows), slice(lane, lane + width))
                dst = (slice(orow, orow + nrows), slice(ocol, ocol + width))
                g = stack_ref[(0,) + src].astype(F32)
                for dev in range(1, N_DEV):
                    g = g + stack_ref[(dev,) + src].astype(F32)
                res = _adamw(g, ins[i][dst], ins[n + i][dst], ins[2 * n + i][dst])
                for kk, val in enumerate(res):
                    outs[kk * n + i][dst] = val
            first += _pack_rows(view)

    res = pl.pallas_call(
        body, name=name, out_shape=[jax.ShapeDtypeStruct(view, F32) for _ in range(4) for view in views],
        compiler_params=pltpu.CompilerParams(vmem_limit_bytes=VMEM_LIMIT_BYTES),
    )(stack, *w, *m, *v)
    return [res[kk * n:(kk + 1) * n] for kk in range(4)]


def kernel(x, mem, positions, ln_in_g, ln_in_b, w_in, b_in, ssm_log_dt, ssm_a_re, ssm_a_im, ssm_b_re, ssm_b_im, ssm_c_re, ssm_c_im, ssm_d, w_glu, b_glu, w_att_up, w_mix_out, b_mix_out, ln1_g, ln1_b, w_xq, w_xkv, w_xo, ln2_g, ln2_b, w_ff1, b_ff1, w_ff2, b_ff2, ln3_g, ln3_b, loss_target, m_ln_in_g, m_ln_in_b, m_w_in, m_b_in, m_ssm_log_dt, m_ssm_a_re, m_ssm_a_im, m_ssm_b_re, m_ssm_b_im, m_ssm_c_re, m_ssm_c_im, m_ssm_d, m_w_glu, m_b_glu, m_w_att_up, m_w_mix_out, m_b_mix_out, m_ln1_g, m_ln1_b, m_w_xq, m_w_xkv, m_w_xo, m_ln2_g, m_ln2_b, m_w_ff1, m_b_ff1, m_w_ff2, m_b_ff2, m_ln3_g, m_ln3_b, v_ln_in_g, v_ln_in_b, v_w_in, v_b_in, v_ssm_log_dt, v_ssm_a_re, v_ssm_a_im, v_ssm_b_re, v_ssm_b_im, v_ssm_c_re, v_ssm_c_im, v_ssm_d, v_w_glu, v_b_glu, v_w_att_up, v_w_mix_out, v_b_mix_out, v_ln1_g, v_ln1_b, v_w_xq, v_w_xkv, v_w_xo, v_ln2_g, v_ln2_b, v_w_ff1, v_b_ff1, v_w_ff2, v_b_ff2, v_ln3_g, v_ln3_b):
    given = dict(locals())
    w_arg = {n: given[n] for n in WEIGHTS}
    m_arg = {n: given["m_" + n] for n in WEIGHTS}
    v_arg = {n: given["v_" + n] for n in WEIGHTS}

    shards = {n: w_arg[n][0].astype(MXU_DTYPE) for n in BIG}
    gathers, token = [], None
    for i, names in enumerate(GATHER_GROUPS):
        handle, token = _exchange_start([shards[n] for n in names], scatter=False, after=token, name=f"gather_start_{i}")
        gathers.append(handle)

    small_views = [_small_view(n, w_arg[n].shape) for n in SMALL]
    small_wmv = [d[n].reshape(view) for d in (w_arg, m_arg, v_arg) for n, view in zip(SMALL, small_views)]
    relaid = [t for t, n in zip(small_wmv, SMALL * 3) if n in SMALL_FLAT_SSM]

    def fetch(i, after):
        lands = _exchange_wait(gathers[i], after=after + (relaid if i == 0 else []), name=f"gather_wait_{i}")
        full = dict(zip(GATHER_GROUPS[i], lands))
        return {n: t if n in BIG_COL_SHARDED else t.reshape(-1, t.shape[-1]) for n, t in full.items()}

    scatters = {}

    def send(i, gw):
        slots = [gw[n] if n in BIG_COL_SHARDED else gw[n].reshape(N_DEV, -1, gw[n].shape[-1]) for n in SCATTER_GROUPS[i]]
        handle, sent = _exchange_start(slots, scatter=True, name=f"scatter_start_{i}")
        scatters[i] = (handle, slots)
        return sent

    sm = {}
    for n in SMALL:
        t = w_arg[n]
        if n.startswith("ssm_") and n not in ("ssm_d", "ssm_log_dt"):
            sm[n] = t[0]
        else:
            sm[n] = t.reshape(1, -1)

    loss_row, grad_x, gs = _local_grads(x[0], mem[0], positions.reshape(-1, 1), loss_target[0], sm, fetch, send, token)
    loss = lax.psum(loss_row[0, 0], ("x", "y", "c"))
    small = _pack_small(gs, small_views)
    small_handle, _ = _exchange_start([small], scatter=False, name="small_start")

    results = [{}, {}, {}, {}]
    done = grad_x
    for i, names in enumerate(SCATTER_GROUPS):
        handle, slots = scatters[i]
        lands = _exchange_wait(handle, after=[done], name=f"scatter_wait_{i}")
        for n, land, slot in zip(names, lands, slots):
            res = _reduce_adamw(land, w_arg[n][0], m_arg[n][0], v_arg[n][0], name="adamw_" + n)
            done = res[0]
            for d, r in zip(results, res):
                d[n] = r[None]
    small_stack = _exchange_wait(small_handle, after=[done], name="small_wait")[0]
    n_small = len(SMALL)
    res = _adamw_small(small_stack, small_views, small_wmv[:n_small], small_wmv[n_small:2 * n_small],
                       small_wmv[2 * n_small:], name="adamw_small")
    for d, r in zip(results, res):
        d.update({n: t.reshape(w_arg[n].shape) for n, t in zip(SMALL, r)})
    out = [loss, grad_x[None]]
    for d in results:
        out += [d[n] for n in WEIGHTS]
    return tuple(out)
```

```python
import functools

import numpy as np
import jax
import jax.numpy as jnp
from jax import lax
from jax.experimental import pallas as pl
from jax.experimental.pallas import tpu as pltpu

F32 = jnp.float32
MXU_DTYPE = jnp.bfloat16
WIRE_DTYPE = jnp.bfloat16
VMEM_LIMIT_BYTES = 48 * 1024 * 1024
LANES = 128

N_DEV = 8
D_MODEL = 1024
SSM_GROUP = 16
SSM_WIDTH = 768
SSM_GROUPS = SSM_WIDTH // SSM_GROUP
SSM_STATE = 64
SSM_CH = SSM_GROUPS * SSM_STATE
SSM_TILES = SSM_WIDTH // LANES
GROUPS_PER_TILE = LANES // SSM_GROUP
STATE_VREG_ROWS = SSM_CH // LANES
ATT_HEAD_DIM = 64
ATT_HEADS_PER_GROUP = 4
ATT_MERGED = ATT_HEADS_PER_GROUP * ATT_HEAD_DIM
LANE_HALVES = ATT_MERGED // LANES
DILATIONS = (1, 4, 16)
ATT_BLK = 128
ATT_SCALE = ATT_HEAD_DIM ** -0.5
ROT_DIM = ATT_HEAD_DIM // 4
ROPE_THETA = 500000.0
XATT_HEADS = 4
XATT_HEAD_DIM = D_MODEL // XATT_HEADS
XATT_SCALE = XATT_HEAD_DIM ** -0.5
DEEPNORM_ALPHA = 2.0 ** 0.25
LN_EPS = 1e-5
NEG_INF = -1e30
OFF_Q_BLK, OFF_K_BLK, OFF_V_BLK = 3, 6, 9
OFF_GS_BLK, OFF_GA_BLK = 3, 4

ADAM_LR = 0.001
ADAM_B1 = 0.9
ADAM_B2 = 0.999
ADAM_EPS = 1e-08
ADAM_WD = 0.01
ADAM_STEP = 10

BIG = ("w_in", "w_glu", "w_att_up", "w_mix_out", "w_xq", "w_xkv", "w_xo", "w_ff1", "w_ff2")
BIG_COL_SHARDED = ("w_in", "w_glu", "w_att_up", "w_xkv", "w_ff1")
WEIGHTS = ("ln_in_g", "ln_in_b", "w_in", "b_in", "ssm_log_dt", "ssm_a_re", "ssm_a_im", "ssm_b_re", "ssm_b_im",
           "ssm_c_re", "ssm_c_im", "ssm_d", "w_glu", "b_glu", "w_att_up", "w_mix_out", "b_mix_out", "ln1_g", "ln1_b",
           "w_xq", "w_xkv", "w_xo", "ln2_g", "ln2_b", "w_ff1", "b_ff1", "w_ff2", "b_ff2", "ln3_g", "ln3_b")
SMALL = tuple(n for n in WEIGHTS if n not in BIG)
PACK_COLS = 1024
PACK_ROW_ALIGN = 256


def _params(*sem):
    return pltpu.CompilerParams(dimension_semantics=sem, vmem_limit_bytes=VMEM_LIMIT_BYTES)


def _dot(a, b, ca, cb):
    return lax.dot_general(a.astype(MXU_DTYPE), b.astype(MXU_DTYPE), (((ca,), (cb,)), ((), ())),
                           preferred_element_type=F32)


def _fit(dim, pref):
    if dim <= pref:
        return dim
    best = max(t for t in range(LANES, pref + 1, LANES) if dim % t == 0)
    return best


def _mm(a, b, *, name, ta=False, tb=False, bias=None, out_dtype=F32, b_shards=False, out_shards=False, after=None,
        also=None, gate=None, colsum=False, tm=2048, tn=1024, tk=1024):
    m, k = (a.shape[1], a.shape[0]) if ta else a.shape
    order = (lambda f: (lambda j, i, kk: f(i, j, kk))) if colsum else (lambda f: f)
    spec = lambda shape, f: pl.BlockSpec(shape, order(f))
    if b_shards:
        n_sh, rows, n_loc = b.shape
        if tb:
            n, tn, tk = rows, _fit(rows, tn), n_loc
            assert k == n_sh * n_loc, (name, k, b.shape)
            b_spec = spec((1, tn, tk), lambda i, j, kk: (kk, j, 0))
        else:
            n, tn, tk = n_sh * n_loc, n_loc, _fit(k, tk)
            b_spec = spec((1, tk, tn), lambda i, j, kk: (j, kk, 0))
    else:
        n = b.shape[0] if tb else b.shape[1]
        tn = n // N_DEV if out_shards else _fit(n, tn)
        tk = _fit(k, tk)
        b_spec = spec((tn, tk), lambda i, j, kk: (j, kk)) if tb else spec((tk, tn), lambda i, j, kk: (kk, j))
    tm = _fit(m, tm)
    nk = k // tk
    a_spec = spec((tk, tm), lambda i, j, kk: (kk, i)) if ta else spec((tm, tk), lambda i, j, kk: (i, kk))
    tile = spec((tm, tn), lambda i, j, kk: (i, j))
    in_specs, args = [a_spec, b_spec], [a, b]
    if bias is not None:
        in_specs.append(spec((1, tn), lambda i, j, kk: (0, j)))
        args.append(bias)
    if gate is not None:
        in_specs.append(tile)
        args.append(gate[0])
    if after is not None:
        in_specs.append(pl.BlockSpec(memory_space=pl.ANY))
        args.append(after)
    n_in = len(args)
    if out_shards:
        assert n == N_DEV * tn, (name, n, tn)
        out_specs = [spec((1, tm, tn), lambda i, j, kk: (j, i, 0))]
        out_shape = [jax.ShapeDtypeStruct((N_DEV, m, tn), out_dtype)]
    else:
        out_specs = [tile]
        out_shape = [jax.ShapeDtypeStruct((m, n), out_dtype)]
    if also is not None:
        out_specs.append(tile)
        out_shape.append(jax.ShapeDtypeStruct((m, n), also[1]))
    if colsum:
        out_specs.append(spec((1, tn), lambda i, j, kk: (0, j)))
        out_shape.append(jax.ShapeDtypeStruct((1, n), F32))

    def body(*refs):
        a_ref, b_ref = refs[0], refs[1]
        o_ref = refs[n_in]

        def product():
            return _dot(a_ref[...], b_ref[0] if b_shards else b_ref[...], 0 if ta else 1, 1 if tb else 0)

        def finish(r):
            if bias is not None:
                r = r + refs[2][...]
            if gate is not None:
                r = r * gate[1](refs[2 + (bias is not None)][...])
            if out_shards:
                o_ref[0] = r.astype(o_ref.dtype)
            else:
                o_ref[...] = r.astype(o_ref.dtype)
            if also is not None:
                refs[n_in + 1][...] = also[0](r).astype(also[1])
            if colsum:
                s_ref = refs[n_in + 1 + (also is not None)]

                @pl.when(pl.program_id(1) == 0)
                def _():
                    s_ref[...] = jnp.zeros_like(s_ref)

                s_ref[...] += _colsum(r)

        if nk == 1:
            finish(product())
            return
        acc_ref = refs[-1]
        kk = pl.program_id(2)

        @pl.when(kk == 0)
        def _():
            acc_ref[...] = jnp.zeros_like(acc_ref)

        acc_ref[...] += product()

        @pl.when(kk == nk - 1)
        def _():
            finish(acc_ref[...])

    grid = (n // tn, m // tm, nk) if colsum else (m // tm, n // tn, nk)
    res = pl.pallas_call(
        body, name=name, grid=grid, in_specs=in_specs, out_specs=out_specs, out_shape=out_shape,
        scratch_shapes=[pltpu.VMEM((tm, tn), F32)] if nk > 1 else [],
        compiler_params=_params("parallel", "arbitrary" if colsum else "parallel", "arbitrary"),
    )(*args)
    return res[0] if len(res) == 1 else res


def _rowcall(fn, rows, fulls, row_outs, acc_outs=(), *, n_rows, tm, name, after=None):
    n_r, n_f, n_o, n_a = len(rows), len(fulls), len(row_outs), len(acc_outs)
    n_in = n_r + n_f + (after is not None)
    assert n_rows % tm == 0, (name, n_rows, tm)

    def body(*refs):
        res = fn(*[r[...] for r in refs[:n_r + n_f]])
        res = tuple(res) if isinstance(res, (tuple, list)) else (res,)
        o_refs = refs[n_in:n_in + n_o]
        a_refs = refs[n_in + n_o:]
        for o_ref, val in zip(o_refs, res[:n_o]):
            o_ref[...] = val.astype(o_ref.dtype)
        if n_a:
            @pl.when(pl.program_id(0) == 0)
            def _():
                for a_ref in a_refs:
                    a_ref[...] = jnp.zeros_like(a_ref)

            for a_ref, val in zip(a_refs, res[n_o:]):
                a_ref[...] += val

    in_specs = [pl.BlockSpec((tm, w), functools.partial(lambda i, cb: (i, cb), cb=cb)) for _, w, cb in rows]
    in_specs += [pl.BlockSpec(f.shape, functools.partial(lambda i, nd: (0,) * nd, nd=f.ndim)) for f in fulls]
    in_specs += [pl.BlockSpec(memory_space=pl.ANY)] * (after is not None)
    out_specs = [pl.BlockSpec((tm, w), lambda i: (i, 0)) for w, _ in row_outs]
    out_specs += [pl.BlockSpec((1, w), lambda i: (0, 0)) for w in acc_outs]
    out_shape = [jax.ShapeDtypeStruct((n_rows, w), dt) for w, dt in row_outs]
    out_shape += [jax.ShapeDtypeStruct((1, w), F32) for w in acc_outs]
    return pl.pallas_call(
        body, name=name, grid=(n_rows // tm,), in_specs=in_specs, out_specs=out_specs, out_shape=out_shape,
        compiler_params=_params("arbitrary" if n_a else "parallel"),
    )(*[r[0] for r in rows], *fulls, *([after] if after is not None else []))


def _colsum(v):
    return jnp.sum(v, axis=0, keepdims=True)


def _ln_fwd(a, r, g, b, *, alpha, name):
    n_rows, d = a.shape

    def fn(*t):
        xin = t[0] if alpha == 1.0 else alpha * t[0]
        if r is not None:
            xin = xin + t[1]
        gv, bv = t[-2], t[-1]
        mu = jnp.mean(xin, axis=-1, keepdims=True)
        xc = xin - mu
        var = jnp.mean(xc * xc, axis=-1, keepdims=True)
        rstd = lax.rsqrt(var + LN_EPS)
        xh = xc * rstd
        y = xh * gv + bv
        return y, xh, rstd, y

    rows = [(a, d, 0)] + ([(r, d, 0)] if r is not None else [])
    return _rowcall(fn, rows, [g, b], [(d, F32), (d, F32), (1, F32), (d, MXU_DTYPE)], n_rows=n_rows, tm=256, name=name)


def _ln_bwd(dya, dyb, xh, rstd, g, *, alpha, name, operand=True):
    n_rows, d = xh.shape

    def fn(da, db, xhv, rs, gv):
        dy = alpha * da + db
        dyg = dy * gv
        m1 = jnp.mean(dyg, axis=-1, keepdims=True)
        m2 = jnp.mean(dyg * xhv, axis=-1, keepdims=True)
        dx = rs * (dyg - m1 - xhv * m2)
        return (dx,) + ((dx,) if operand else ()) + (_colsum(dy * xhv), _colsum(dy), _colsum(dx))

    rows = [(dya, d, 0), (dyb, d, 0), (xh, d, 0), (rstd, 1, 0)]
    return _rowcall(fn, rows, [g], [(d, F32)] + [(d, MXU_DTYPE)] * operand, [d, d, d], n_rows=n_rows, tm=256, name=name)


def _ln_loss_bwd(a, r, target, g, b, *, alpha, name):
    n_rows, d = a.shape

    def fn(av, rv, tv, gv, bv):
        xin = alpha * av + rv
        mu = jnp.mean(xin, axis=-1, keepdims=True)
        xc = xin - mu
        var = jnp.mean(xc * xc, axis=-1, keepdims=True)
        rs = lax.rsqrt(var + LN_EPS)
        xh = xc * rs
        diff = xh * gv + bv - tv
        part = jnp.sum(jnp.sum(diff * diff, axis=1, keepdims=True), axis=0, keepdims=True) * (0.5 / d)
        dy = diff * (1.0 / d)
        dyg = dy * gv
        m1 = jnp.mean(dyg, axis=-1, keepdims=True)
        m2 = jnp.mean(dyg * xh, axis=-1, keepdims=True)
        dx = rs * (dyg - m1 - xh * m2)
        return dx, dx, _colsum(dy * xh), _colsum(dy), _colsum(dx), jnp.broadcast_to(part, (1, LANES))

    return _rowcall(fn, [(a, d, 0), (r, d, 0), (target, d, 0)], [g, b], [(d, F32), (d, MXU_DTYPE)], [d, d, d, LANES],
                    n_rows=n_rows, tm=256, name=name)


def _rope_lane_constants():
    lane = np.arange(ATT_MERGED)
    in_head = lane % ATT_HEAD_DIM
    sign = np.where(in_head < ROT_DIM // 2, -1.0, np.where(in_head < ROT_DIM, 1.0, 0.0)).astype(np.float32)
    inv_freq = ROPE_THETA ** (-jnp.arange(0, ROT_DIM, 2, dtype=F32) / ROT_DIM)
    return inv_freq[lane % (ROT_DIM // 2)].reshape(1, ATT_MERGED), jnp.asarray(sign).reshape(1, ATT_MERGED)


def _rope_tables(pos_col, *, name, after=None):
    inv_lane, sign = _rope_lane_constants()

    def fn(pos, inv, sg):
        ang = pos.astype(F32) * inv
        return jnp.where(sg != 0.0, jnp.cos(ang), 1.0), sg * jnp.sin(ang)

    return _rowcall(fn, [(pos_col, 1, 0)], [inv_lane, sign], [(ATT_MERGED, F32), (ATT_MERGED, F32)],
                    n_rows=pos_col.shape[0], tm=512, name=name, after=after)


def _rot_partner(t):
    lane = lax.broadcasted_iota(jnp.int32, t.shape, 1)
    width = t.shape[1]
    return jnp.where((lane & (ROT_DIM // 2)) == 0, pltpu.roll(t, width - ROT_DIM // 2, 1), pltpu.roll(t, ROT_DIM // 2, 1))


def _rope(t, cos_t, sin_t):
    return t * cos_t + _rot_partner(t) * sin_t


def _rope_transpose(dt, cos_t, sin_t):
    return dt * cos_t + _rot_partner(dt * sin_t)


def _strided_rows(r, count, stride):
    return pl.ds(r, count) if stride == 1 else pl.ds(r, count, stride=stride)


def _qkv_split(proj, cos_t, sin_t, *, name, tm=512):
    n_rows = proj.shape[0]
    n_g = len(DILATIONS)

    def body(*refs):
        n_src = LANE_HALVES * 3 * n_g
        src, tables, dst = refs[:n_src], refs[n_src:n_src + 2 * LANE_HALVES], refs[n_src + 2 * LANE_HALVES:]
        for kind in range(3):
            for g, dil in enumerate(DILATIONS):
                for half in range(LANE_HALVES):
                    x_ref, o_ref = src[(kind * n_g + g) * LANE_HALVES + half], dst[kind * n_g + g]
                    cos_ref, sin_ref = tables[half], tables[LANE_HALVES + half]
                    for r in range(dil):
                        rows = _strided_rows(r, tm // dil, dil)
                        t = x_ref[rows, :]
                        if kind < 2:
                            t = _rope(t, cos_ref[rows, :], sin_ref[rows, :])
                        lo = r * ATT_MERGED + half * LANES
                        o_ref[:, lo:lo + LANES] = t.astype(o_ref.dtype)

    half_spec = lambda cb: pl.BlockSpec((tm, LANES), functools.partial(lambda i, cb: (i, cb), cb=cb))
    in_specs = [half_spec((off + g) * LANE_HALVES + half)
                for off in (OFF_Q_BLK, OFF_K_BLK, OFF_V_BLK) for g in range(n_g) for half in range(LANE_HALVES)]
    in_specs += [half_spec(half) for _ in range(2) for half in range(LANE_HALVES)]
    out_specs = [pl.BlockSpec((tm // dil, dil * ATT_MERGED), lambda i: (i, 0)) for _ in range(3) for dil in DILATIONS]
    out_shape = [jax.ShapeDtypeStruct((n_rows // dil, dil * ATT_MERGED), MXU_DTYPE) for _ in range(3) for dil in DILATIONS]
    outs = pl.pallas_call(
        body, name=name, grid=(n_rows // tm,), in_specs=in_specs, out_specs=out_specs, out_shape=out_shape,
        compiler_params=_params("parallel"),
    )(*[proj] * (LANE_HALVES * 3 * n_g), *[cos_t] * LANE_HALVES, *[sin_t] * LANE_HALVES)
    return outs[:n_g], outs[n_g:2 * n_g], outs[2 * n_g:]


def _mix(gs, ga, z1, z2, b_att):
    return jax.nn.sigmoid(gs) * (z1 * jax.nn.sigmoid(z2)) + jax.nn.sigmoid(ga) * b_att


def _mix_rows(proj, z, b_att):
    return [(proj, D_MODEL, OFF_GS_BLK), (proj, D_MODEL, OFF_GA_BLK), (z, D_MODEL, 0), (z, D_MODEL, 1), (b_att, D_MODEL, 0)]


def _mix_fwd(proj, z, b_att, *, name):
    return _rowcall(_mix, _mix_rows(proj, z, b_att), [], [(D_MODEL, MXU_DTYPE)],
                    n_rows=proj.shape[0], tm=256, name=name)[0]


def _mix_bwd(dmixed, proj, z, b_att, *, name):
    def fn(dm, gs, ga, z1, z2, ba):
        _, vjp = jax.vjp(_mix, gs, ga, z1, z2, ba)
        dgs, dga, dz1, dz2, dba = vjp(dm)
        dz = jnp.concatenate([dz1, dz2], axis=1)
        return dgs, dga, dz, dba, _colsum(dgs), _colsum(dga), _colsum(dz)

    rows = [(dmixed, D_MODEL, 0)] + _mix_rows(proj, z, b_att)
    widths = [D_MODEL, D_MODEL, 2 * D_MODEL, D_MODEL]
    return _rowcall(fn, rows, [], [(w, MXU_DTYPE) for w in widths], widths[:3], n_rows=proj.shape[0], tm=256, name=name)


def _gelu_bwd(dgy, y, proj, *, name):
    def fn(dg, yv, u):
        _, vjp = jax.vjp(jax.nn.gelu, yv)
        dy = vjp(dg)[0]
        return dy, _colsum(dy * u)

    return _rowcall(fn, [(dgy, SSM_WIDTH, 0), (y, SSM_WIDTH, 0), (proj, SSM_WIDTH, 0)], [], [(SSM_WIDTH, F32)],
                    [SSM_WIDTH], n_rows=y.shape[0], tm=512, name=name)


HEAD_ROWS = ATT_HEADS_PER_GROUP * ATT_BLK


def _head_masks(rows):
    head = lax.broadcasted_iota(jnp.int32, (rows, ATT_MERGED), 1) >> (ATT_HEAD_DIM.bit_length() - 1)
    return [head == h for h in range(ATT_HEADS_PER_GROUP)]


def _stack_heads(t, masks):
    return jnp.concatenate([jnp.where(m, t, jnp.zeros_like(t)) for m in masks], axis=0)


def _unstack_heads(t4, masks):
    blocks = [t4[h * ATT_BLK:(h + 1) * ATT_BLK] for h in range(ATT_HEADS_PER_GROUP)]
    return jnp.where(masks[0], blocks[0], jnp.where(masks[1], blocks[1], jnp.where(masks[2], blocks[2], blocks[3])))


def _head_column(stats, first):
    return jnp.concatenate([stats[:, first + h:first + h + 1] for h in range(ATT_HEADS_PER_GROUP)], axis=0)


def _band_mask(first_key):
    qi = lax.broadcasted_iota(jnp.int32, (HEAD_ROWS, 2 * ATT_BLK), 0) & (ATT_BLK - 1)
    ki = lax.broadcasted_iota(jnp.int32, (HEAD_ROWS, 2 * ATT_BLK), 1)
    steps = qi + ATT_BLK - ki
    return (steps >= 0) & (steps <= ATT_BLK) & (ki >= first_key)


def _dil_fwd(q, k, v, dil, *, name):
    n_blk = q.shape[0] // ATT_BLK
    cur = pl.BlockSpec((ATT_BLK, ATT_MERGED), lambda r, n: (n, r))
    prev = pl.BlockSpec((ATT_BLK, ATT_MERGED), lambda r, n: (jnp.maximum(n - 1, 0), r))

    def body(q_ref, kp_ref, kc_ref, vp_ref, vc_ref, o_ref, l_ref):
        masks = _head_masks(ATT_BLK)
        valid = _band_mask(jnp.where(pl.program_id(1) > 0, 0, ATT_BLK))
        keys = jnp.concatenate([kp_ref[...], kc_ref[...]], axis=0)
        vals = jnp.concatenate([vp_ref[...], vc_ref[...]], axis=0)
        s = jnp.where(valid, _dot(_stack_heads(q_ref[...], masks), keys, 1, 1) * ATT_SCALE, NEG_INF)
        m = jnp.max(s, axis=-1, keepdims=True)
        p = jnp.exp(s - m)
        den = jnp.sum(p, axis=-1, keepdims=True)
        o_ref[...] = _unstack_heads(_dot(p, vals, 1, 0) / den, masks)
        l_ref[...] = _unstack_heads(jnp.broadcast_to(m + jnp.log(den), (HEAD_ROWS, ATT_MERGED)), masks)

    shape = jax.ShapeDtypeStruct(q.shape, F32)
    return pl.pallas_call(
        body, name=name, grid=(dil, n_blk), in_specs=[cur, prev, cur, prev, cur], out_specs=[cur, cur],
        out_shape=[shape, shape], compiler_params=_params("parallel", "parallel"),
    )(q, k, k, v, v)


def _att_merge(outs, lses, *, name, tm=512):
    n_g = len(outs)
    n_rows = outs[0].shape[0] * DILATIONS[0]

    def body(*refs):
        src, (att_ref, lse_ref), tmp = refs[:2 * n_g], refs[2 * n_g:2 * n_g + 2], refs[2 * n_g + 2:]
        vals = []
        for idx, src_ref in enumerate(src):
            dil = DILATIONS[idx % n_g]
            if dil == 1:
                vals.append(src_ref[...])
                continue
            for r in range(dil):
                for half in range(LANE_HALVES):
                    lo = r * ATT_MERGED + half * LANES
                    tmp[LANE_HALVES * idx + half][_strided_rows(r, tm // dil, dil), :] = src_ref[:, lo:lo + LANES]
            vals.append(jnp.concatenate([tmp[LANE_HALVES * idx + half][...] for half in range(LANE_HALVES)], axis=1))
        o, l = vals[:n_g], vals[n_g:]
        m = functools.reduce(jnp.maximum, l)
        e = [jnp.exp(li - m) for li in l]
        z = functools.reduce(jnp.add, e)
        att_ref[...] = functools.reduce(jnp.add, [(ei / z) * oi for ei, oi in zip(e, o)])
        lse_ref[...] = m + jnp.log(z)

    in_specs = [pl.BlockSpec((tm // dil, dil * ATT_MERGED), lambda i: (i, 0)) for _ in range(2) for dil in DILATIONS]
    row = pl.BlockSpec((tm, ATT_MERGED), lambda i: (i, 0))
    shape = jax.ShapeDtypeStruct((n_rows, ATT_MERGED), F32)
    return pl.pallas_call(
        body, name=name, grid=(n_rows // tm,), in_specs=in_specs, out_specs=[row, row], out_shape=[shape, shape],
        scratch_shapes=[pltpu.VMEM((tm, LANES), F32)] * (LANE_HALVES * 2 * n_g), compiler_params=_params("parallel"),
    )(*outs, *lses)


def _att_stats(datt, att, lse, *, name):
    n_rows = datt.shape[0]

    def fn(d, a, l):
        prod = d * a
        lane = lax.broadcasted_iota(jnp.int32, (d.shape[0], LANES), 1)
        out = jnp.zeros((d.shape[0], LANES), F32)
        for h in range(ATT_HEADS_PER_GROUP):
            lo = h * ATT_HEAD_DIM
            out = jnp.where(lane == h, l[:, lo:lo + 1], out)
            delta = jnp.sum(prod[:, lo:lo + ATT_HEAD_DIM], axis=-1, keepdims=True)
            out = jnp.where(lane == ATT_HEADS_PER_GROUP + h, delta, out)
        return out

    rows = [(t, ATT_MERGED, 0) for t in (datt, att, lse)]
    return _rowcall(fn, rows, [], [(LANES, F32)], n_rows=n_rows, tm=512, name=name)[0]


def _dil_bwd(q, k, v, datt, stats, dil, *, name):
    n_rows = datt.shape[0]
    n_blk = n_rows // dil // ATT_BLK
    span = ATT_BLK * dil
    cur = pl.BlockSpec((ATT_BLK, ATT_MERGED), lambda n, r: (n, r))
    prev = pl.BlockSpec((ATT_BLK, ATT_MERGED), lambda n, r: (jnp.maximum(n - 1, 0), r))
    nxt = pl.BlockSpec((ATT_BLK, ATT_MERGED), lambda n, r: (jnp.minimum(n + 1, n_blk - 1), r))
    seq = lambda half, ahead: pl.BlockSpec((span, LANES), lambda n, r: (jnp.minimum(n + ahead, n_blk - 1), half))

    def body(qc_ref, qn_ref, kp_ref, kc_ref, vp_ref, vc_ref, dc0_ref, dc1_ref, dn0_ref, dn1_ref, sc_ref, sn_ref,
             dq0_ref, dq1_ref, dk0_ref, dk1_ref, dv0_ref, dv1_ref):
        n = pl.program_id(0)
        rows = slice(None) if dil == 1 else _strided_rows(pl.program_id(1), ATT_BLK, dil)

        def read(ref0, ref1):
            return jnp.concatenate([ref0[rows, :], ref1[rows, :]], axis=1)

        def write(ref0, ref1, val):
            ref0[rows, :] = val[:, :LANES]
            ref1[rows, :] = val[:, LANES:]

        masks = _head_masks(ATT_BLK)
        valid = _band_mask(jnp.where(n > 0, 0, ATT_BLK))
        qi = lax.broadcasted_iota(jnp.int32, (HEAD_ROWS, ATT_BLK), 0) & (ATT_BLK - 1)
        ki = lax.broadcasted_iota(jnp.int32, (HEAD_ROWS, ATT_BLK), 1)
        valid_next = (ki - qi) >= jnp.where(n < n_blk - 1, 0, ATT_BLK)

        kc, vc = kc_ref[...], vc_ref[...]
        keys = jnp.concatenate([kp_ref[...], kc], axis=0)
        vals = jnp.concatenate([vp_ref[...], vc], axis=0)
        q4 = _stack_heads(qc_ref[...], masks)
        d4 = _stack_heads(read(dc0_ref, dc1_ref).astype(MXU_DTYPE), masks)
        st = sc_ref[rows, :]
        p = jnp.where(valid, jnp.exp(_dot(q4, keys, 1, 1) * ATT_SCALE - _head_column(st, 0)), 0.0)
        ds = p * (_dot(d4, vals, 1, 1) - _head_column(st, ATT_HEADS_PER_GROUP)) * ATT_SCALE
        write(dq0_ref, dq1_ref, _unstack_heads(_dot(ds, keys, 1, 0), masks))

        q4n = _stack_heads(qn_ref[...], masks)
        d4n = _stack_heads(read(dn0_ref, dn1_ref).astype(MXU_DTYPE), masks)
        stn = sn_ref[rows, :]
        p_n = jnp.where(valid_next, jnp.exp(_dot(q4n, kc, 1, 1) * ATT_SCALE - _head_column(stn, 0)), 0.0)
        ds_n = p_n * (_dot(d4n, vc, 1, 1) - _head_column(stn, ATT_HEADS_PER_GROUP)) * ATT_SCALE
        write(dv0_ref, dv1_ref, _dot(p[:, ATT_BLK:], d4, 0, 0) + _dot(p_n, d4n, 0, 0))
        write(dk0_ref, dk1_ref, _dot(ds[:, ATT_BLK:], q4, 0, 0) + _dot(ds_n, q4n, 0, 0))

    shape = jax.ShapeDtypeStruct((n_rows, LANES), F32)
    out = seq(0, 0)
    res = pl.pallas_call(
        body, name=name, grid=(n_blk, dil),
        in_specs=[cur, nxt, prev, cur, prev, cur, seq(0, 0), seq(1, 0), seq(0, 1), seq(1, 1), seq(0, 0), seq(0, 1)],
        out_specs=[out] * 6, out_shape=[shape] * 6, compiler_params=_params("parallel", "arbitrary"),
    )(q, q, k, k, v, v, datt, datt, datt, datt, stats, stats)
    return [(res[2 * i], res[2 * i + 1]) for i in range(3)]


def _dproj_assemble(du, dqkv, dgs, dga, cos_t, sin_t, *, name):
    n_g = len(DILATIONS)

    def fn(*t):
        n_half = LANE_HALVES * 3 * n_g
        du_t, halves, (dgs_t, dga_t, c, s) = t[0], t[1:1 + n_half], t[1 + n_half:]
        parts = [jnp.concatenate(halves[LANE_HALVES * i:LANE_HALVES * (i + 1)], axis=1) for i in range(3 * n_g)]
        for i in range(2 * n_g):
            parts[i] = _rope_transpose(parts[i], c, s)
        cast = [p.astype(MXU_DTYPE) for p in parts]
        return [jnp.concatenate([du_t] + cast + [dgs_t, dga_t], axis=1)] + [_colsum(p) for p in parts]

    rows = [(du, SSM_WIDTH, 0)]
    rows += [(half, LANES, 0) for i in range(3) for g in range(n_g) for half in dqkv[g][i]]
    rows += [(dgs, D_MODEL, 0), (dga, D_MODEL, 0), (cos_t, ATT_MERGED, 0), (sin_t, ATT_MERGED, 0)]
    width = SSM_WIDTH + 3 * n_g * ATT_MERGED + 2 * D_MODEL
    res = _rowcall(fn, rows, [], [(width, MXU_DTYPE)], [ATT_MERGED] * (3 * n_g), n_rows=du.shape[0], tm=256, name=name)
    return res[0], res[1:]


def _xhead(h):
    return slice(h * XATT_HEAD_DIM, (h + 1) * XATT_HEAD_DIM)


def _xatt_probs(qh, kh):
    s = _dot(qh, kh, 1, 1) * XATT_SCALE
    e = jnp.exp(s - jnp.max(s, axis=-1, keepdims=True))
    return e / jnp.sum(e, axis=-1, keepdims=True)


def _xatt_fwd(q, kv, *, name, tm=512):
    n_rows = q.shape[0]
    n_mem = kv.shape[0]

    def body(q_ref, kv_ref, o_ref):
        for h in range(XATT_HEADS):
            sl = _xhead(h)
            p = _xatt_probs(q_ref[:, sl], kv_ref[:, sl])
            o_ref[:, sl] = _dot(p, kv_ref[:, D_MODEL + h * XATT_HEAD_DIM:D_MODEL + (h + 1) * XATT_HEAD_DIM], 1, 0
                                ).astype(o_ref.dtype)

    row = pl.BlockSpec((tm, D_MODEL), lambda i: (i, 0))
    return pl.pallas_call(
        body, name=name, grid=(n_rows // tm,),
        in_specs=[row, pl.BlockSpec((n_mem, 2 * D_MODEL), lambda i: (0, 0))], out_specs=row,
        out_shape=jax.ShapeDtypeStruct((n_rows, D_MODEL), MXU_DTYPE), compiler_params=_params("parallel"),
    )(q, kv)


def _xatt_bwd(q, kv, do, *, name, tm=512):
    n_rows = q.shape[0]
    n_mem = kv.shape[0]

    def body(q_ref, kv_ref, do_ref, dq_ref, dkv_ref):
        @pl.when(pl.program_id(0) == 0)
        def _():
            dkv_ref[...] = jnp.zeros_like(dkv_ref)

        for h in range(XATT_HEADS):
            sl = _xhead(h)
            vsl = slice(D_MODEL + h * XATT_HEAD_DIM, D_MODEL + (h + 1) * XATT_HEAD_DIM)
            qh, kh, doh = q_ref[:, sl], kv_ref[:, sl], do_ref[:, sl]
            p = _xatt_probs(qh, kh)
            dp = _dot(doh, kv_ref[:, vsl], 1, 1)
            ds = p * (dp - jnp.sum(dp * p, axis=-1, keepdims=True)) * XATT_SCALE
            dq_ref[:, sl] = _dot(ds, kh, 1, 0).astype(dq_ref.dtype)
            dkv_ref[:, sl] += _dot(ds, qh, 0, 0)
            dkv_ref[:, vsl] += _dot(p, doh, 0, 0)

    row = pl.BlockSpec((tm, D_MODEL), lambda i: (i, 0))
    full = pl.BlockSpec((n_mem, 2 * D_MODEL), lambda i: (0, 0))
    return pl.pallas_call(
        body, name=name, grid=(n_rows // tm,), in_specs=[row, full, row], out_specs=[row, full],
        out_shape=[jax.ShapeDtypeStruct((n_rows, D_MODEL), MXU_DTYPE), jax.ShapeDtypeStruct((n_mem, 2 * D_MODEL), F32)],
        compiler_params=_params("arbitrary"),
    )(q, kv, do)


def _disc(logdt, a_re, a_im, b_re, b_im):
    dt = jnp.exp(logdt)
    mag = jnp.exp(a_re * dt)
    ab_re = mag * jnp.cos(a_im * dt)
    ab_im = mag * jnp.sin(a_im * dt)
    den = jnp.square(a_re) + jnp.square(a_im)
    nr = ab_re - 1.0
    f_re = (nr * a_re + ab_im * a_im) / den
    f_im = (ab_im * a_re - nr * a_im) / den
    bb_re = f_re[None] * b_re - f_im[None] * b_im
    bb_im = f_re[None] * b_im + f_im[None] * b_re
    return ab_re, ab_im, bb_re, bb_im


def _disc_transpose(logdt, a_re, a_im, b_re, b_im, g_ab_re, g_ab_im, g_bb_re, g_bb_im):
    dt = jnp.exp(logdt)
    mag = jnp.exp(a_re * dt)
    th = a_im * dt
    cs, sn = jnp.cos(th), jnp.sin(th)
    ab_re, ab_im = mag * cs, mag * sn
    den = jnp.square(a_re) + jnp.square(a_im)
    nr = ab_re - 1.0
    f_re = (nr * a_re + ab_im * a_im) / den
    f_im = (ab_im * a_re - nr * a_im) / den
    d_f_re = jnp.sum(g_bb_re * b_re + g_bb_im * b_im, axis=0)
    d_f_im = jnp.sum(g_bb_im * b_re - g_bb_re * b_im, axis=0)
    d_b_re = g_bb_re * f_re[None] + g_bb_im * f_im[None]
    d_b_im = g_bb_im * f_re[None] - g_bb_re * f_im[None]
    d_n_re, d_n_im = d_f_re / den, d_f_im / den
    d_den = -(d_f_re * f_re + d_f_im * f_im) / den
    d_ab_re = g_ab_re + d_n_re * a_re - d_n_im * a_im
    d_ab_im = g_ab_im + d_n_re * a_im + d_n_im * a_re
    d_a_re = d_n_re * nr + d_n_im * ab_im + 2.0 * d_den * a_re
    d_a_im = d_n_re * ab_im - d_n_im * nr + 2.0 * d_den * a_im
    d_mag = d_ab_re * cs + d_ab_im * sn
    d_th = mag * (d_ab_im * cs - d_ab_re * sn)
    d_a_re = d_a_re + d_mag * mag * dt
    d_a_im = d_a_im + d_th * dt
    d_dt = jnp.sum(d_mag * mag * a_re + d_th * a_im, axis=-1, keepdims=True)
    return d_dt * dt, d_a_re, d_a_im, d_b_re, d_b_im


def _whole(fn, args, out_shapes, *, name):
    n_in = len(args)

    def body(*refs):
        res = fn(*[r[...] for r in refs[:n_in]])
        for o_ref, val in zip(refs[n_in:], res):
            o_ref[...] = val

    return pl.pallas_call(body, name=name, out_shape=[jax.ShapeDtypeStruct(s, F32) for s in out_shapes],
                          compiler_params=pltpu.CompilerParams(vmem_limit_bytes=VMEM_LIMIT_BYTES))(*args)


SSM_WIDE =GROUPS_PER_TILE * SSM_STATE
LANE_GROUPS_PER_TILE = SSM_WIDE // LANES


def _chan(j):
    return slice(j * LANES, (j + 1) * LANES)


def _time_major_rows(j, q, tc):
    return pl.ds(j * LANE_GROUPS_PER_TILE + q, tc, stride=STATE_VREG_ROWS)


def _to_time_major(x, t_re_ref, t_im_ref, dst_re, dst_im, tc):
    for j in range(SSM_TILES):
        xj = x[:, _chan(j)]
        for t_ref, dst in ((t_re_ref, dst_re), (t_im_ref, dst_im)):
            r = _dot(xj, t_ref[j], 1, 0)
            for q in range(LANE_GROUPS_PER_TILE):
                dst[_time_major_rows(j, q, tc), :] = r[:, q * LANES:(q + 1) * LANES]


def _from_time_major(src, j, tc):
    return jnp.concatenate([src[_time_major_rows(j, q, tc), :] for q in range(LANE_GROUPS_PER_TILE)], axis=1)


def _scan_chunk(w_re, w_im, h_re, h_im, a_re, a_im, start, tc):
    def step(t, carry):
        hr, hi = carry
        rows = _scan_rows(t)
        nr = a_re * hr - a_im * hi + w_re[rows, :]
        ni = a_re * hi + a_im * hr + w_im[rows, :]
        h_re[rows, :] = nr
        h_im[rows, :] = ni
        return nr, ni

    return lax.fori_loop(0, tc, step, start, unroll=8)


SSM_CHUNK = 256


def _tile_spec(stack, k):
    return pl.BlockSpec((pl.Squeezed(),) + tuple(stack.shape[1:]), lambda i: (k, 0, 0, 0))


def _expand_block_diagonal(src_ref, dst):
    dst[...] = jnp.zeros_like(dst)
    r, c = src_ref.shape[1:]
    for g in range(SSM_GROUPS):
        j, gl = divmod(g, GROUPS_PER_TILE)
        dst[j, gl * r:(gl + 1) * r, gl * c:(gl + 1) * c] = src_ref[g].astype(dst.dtype)


def _extract_block_diagonal(src, dst_ref):
    r, c = dst_ref.shape[1:]
    for g in range(SSM_GROUPS):
        j, gl = divmod(g, GROUPS_PER_TILE)
        dst_ref[g] = src[j, gl * r:(gl + 1) * r, gl * c:(gl + 1) * c]


def _ssm_fwd(proj, blocks_cn, blocks_nc, a_re, a_im, gain, *, name, tc=SSM_CHUNK):
    n_rows = proj.shape[0]
    n_chunk = n_rows // tc

    def body(u_ref, br_ref, bi_ref, cr_ref, ci_ref, ar_ref, ai_ref, g_ref, y_ref, gy_ref, hr, hi, wr, wi, state,
             tbr_ref, tbi_ref, tcr_ref, tci_ref):
        @pl.when(pl.program_id(0) == 0)
        def _():
            state[...] = jnp.zeros_like(state)
            for src_ref, dst in ((br_ref, tbr_ref), (bi_ref, tbi_ref), (cr_ref, tcr_ref), (ci_ref, tci_ref)):
                _expand_block_diagonal(src_ref, dst)

        u = u_ref[...]
        _to_time_major(u, tbr_ref, tbi_ref, wr, wi, tc)
        state[0], state[1] = _scan_chunk(wr, wi, hr, hi, ar_ref[...], ai_ref[...], (state[0], state[1]), tc)
        for j in range(SSM_TILES):
            yj = (_dot(_from_time_major(hr, j, tc), tcr_ref[j], 1, 0) + _dot(_from_time_major(hi, j, tc), tci_ref[j], 1, 0)
                  + g_ref[:, _chan(j)] * u[:, _chan(j)])
            y_ref[:, _chan(j)] = yj
            gy_ref[:, _chan(j)] = jax.nn.gelu(yj).astype(gy_ref.dtype)

    rows = pl.BlockSpec((tc, SSM_WIDTH), lambda i: (i, 0))
    coef = pl.BlockSpec((STATE_VREG_ROWS, LANES), lambda i: (0, 0))
    states = pl.BlockSpec((tc * STATE_VREG_ROWS, LANES), lambda i: (i, 0))
    sshape = jax.ShapeDtypeStruct((n_rows * STATE_VREG_ROWS, LANES), F32)
    return pl.pallas_call(
        body, name=name, grid=(n_chunk,),
        in_specs=[rows, _tile_spec(blocks_cn, 0), _tile_spec(blocks_cn, 1), _tile_spec(blocks_nc, 0),
                  _tile_spec(blocks_nc, 1), coef, coef, pl.BlockSpec((1, SSM_WIDTH), lambda i: (0, 0))],
        out_specs=[rows, rows, states, states],
        out_shape=[jax.ShapeDtypeStruct((n_rows, SSM_WIDTH), F32), jax.ShapeDtypeStruct((n_rows, SSM_WIDTH), MXU_DTYPE),
                   sshape, sshape],
        scratch_shapes=[pltpu.VMEM((tc * STATE_VREG_ROWS, LANES), F32)] * 2 + [pltpu.VMEM((2, STATE_VREG_ROWS, LANES), F32)]
        + [pltpu.VMEM((SSM_TILES, LANES, SSM_WIDE), MXU_DTYPE)] * 2 + [pltpu.VMEM((SSM_TILES, SSM_WIDE, LANES), MXU_DTYPE)] * 2,
        compiler_params=_params("arbitrary"),
    )(proj, blocks_cn, blocks_cn, blocks_nc, blocks_nc, a_re, a_im, gain)


def _ssm_bwd(proj, dy, h_re, h_im, blocks_cn, blocks_nc, a_re, a_im, gain, *, name, tc=SSM_CHUNK):
    n_rows = proj.shape[0]
    n_chunk = n_rows // tc

    def body(u_ref, dy_ref, hr, hi, cr_ref, ci_ref, br_ref, bi_ref, ar_ref, ai_ref, g_ref,
             du_ref, su_ref, dc_re_ref, dc_im_ref, db_re_ref, db_im_ref, dar_ref, dai_ref, wr, wi, carry,
             tdr_ref, tdi_ref, tur_ref, tui_ref, dcr_ref, dci_ref, dbr_ref, dbi_ref):
        @pl.when(pl.program_id(0) == 0)
        def _():
            carry[...] = jnp.zeros_like(carry)
            for acc_ref in (su_ref, dcr_ref, dci_ref, dbr_ref, dbi_ref):
                acc_ref[...] = jnp.zeros_like(acc_ref)
            for src_ref, dst in ((cr_ref, tdr_ref), (ci_ref, tdi_ref), (br_ref, tur_ref), (bi_ref, tui_ref)):
                _expand_block_diagonal(src_ref, dst)

        a_r, a_i = ar_ref[...], ai_ref[...]
        u, dyv = u_ref[...], dy_ref[...]
        _to_time_major(dyv, tdr_ref, tdi_ref, wr, wi, tc)

        def step(kk, c):
            lam_r, lam_i, dar, dai = c
            rows = _scan_rows(tc - 1 - kk)
            h_r, h_i = hr[rows, :], hi[rows, :]
            dar = dar + lam_r * h_r + lam_i * h_i
            dai = dai + lam_i * h_r - lam_r * h_i
            new_r = wr[rows, :] + a_r * lam_r + a_i * lam_i
            new_i = wi[rows, :] + a_r * lam_i - a_i * lam_r
            wr[rows, :] = new_r
            wi[rows, :] = new_i
            return new_r, new_i, dar, dai

        carry[0], carry[1], carry[2], carry[3] = lax.fori_loop(0, tc, step, (carry[0], carry[1], carry[2], carry[3]),
                                                              unroll=8)
        dar_ref[...] = carry[2]
        dai_ref[...] = carry[3]
        for j in range(SSM_TILES):
            cj = _chan(j)
            lam_r, lam_i = _from_time_major(wr, j, tc), _from_time_major(wi, j, tc)
            dcr_ref[j] += _dot(dyv[:, cj], _from_time_major(hr, j, tc), 0, 0)
            dci_ref[j] += _dot(dyv[:, cj], _from_time_major(hi, j, tc), 0, 0)
            dbr_ref[j] += _dot(u[:, cj], lam_r, 0, 0)
            dbi_ref[j] += _dot(u[:, cj], lam_i, 0, 0)
            duj = _dot(lam_r, tur_ref[j], 1, 0) + _dot(lam_i, tui_ref[j], 1, 0) + g_ref[:, cj] * dyv[:, cj]
            du_ref[:, cj] = duj.astype(du_ref.dtype)
            su_ref[:, cj] += _colsum(duj)

        @pl.when(pl.program_id(0) == n_chunk - 1)
        def _():
            for src, dst_ref in ((dcr_ref, dc_re_ref), (dci_ref, dc_im_ref), (dbr_ref, db_re_ref), (dbi_ref, db_im_ref)):
                _extract_block_diagonal(src, dst_ref)

    back = lambda i: (n_chunk - 1 - i, 0)
    rows = pl.BlockSpec((tc, SSM_WIDTH), back)
    blocks = pl.BlockSpec((SSM_GROUPS, SSM_GROUP, SSM_STATE), lambda i: (0, 0, 0))
    coef = pl.BlockSpec((STATE_VREG_ROWS, LANES), lambda i: (0, 0))
    states = pl.BlockSpec((tc * STATE_VREG_ROWS, LANES), back)
    vec = pl.BlockSpec((1, SSM_WIDTH), lambda i: (0, 0))
    bshape = jax.ShapeDtypeStruct((SSM_GROUPS, SSM_GROUP, SSM_STATE), F32)
    cshape = jax.ShapeDtypeStruct((STATE_VREG_ROWS, LANES), F32)
    return pl.pallas_call(
        body, name=name, grid=(n_chunk,),
        in_specs=[rows, rows, states, states, _tile_spec(blocks_cn, 2), _tile_spec(blocks_cn, 3), _tile_spec(blocks_nc, 2),
                  _tile_spec(blocks_nc, 3), coef, coef, vec],
        out_specs=[rows, vec, blocks, blocks, blocks, blocks, coef, coef],
        out_shape=[jax.ShapeDtypeStruct((n_rows, SSM_WIDTH), MXU_DTYPE), jax.ShapeDtypeStruct((1, SSM_WIDTH), F32),
                   bshape, bshape, bshape, bshape, cshape, cshape],
        scratch_shapes=[pltpu.VMEM((tc * STATE_VREG_ROWS, LANES), F32)] * 2 + [pltpu.VMEM((4, STATE_VREG_ROWS, LANES), F32)]
        + [pltpu.VMEM((SSM_TILES, LANES, SSM_WIDE), MXU_DTYPE)] * 2 + [pltpu.VMEM((SSM_TILES, SSM_WIDE, LANES), MXU_DTYPE)] * 2
        + [pltpu.VMEM((SSM_TILES, LANES, SSM_WIDE), F32)] * 4,
        compiler_params=_params("arbitrary"),
    )(proj, dy, h_re, h_im, blocks_cn, blocks_cn, blocks_nc, blocks_nc, a_re, a_im, gain)


def _scan_rows(t):
    return pl.ds(pl.multiple_of(t * STATE_VREG_ROWS, 8), STATE_VREG_ROWS)


GATHER_GROUPS = (("w_in",), ("w_glu", "w_att_up", "w_mix_out"), ("w_xq", "w_xkv", "w_xo", "w_ff1", "w_ff2"))
SCATTER_GROUPS = (("w_ff2", "w_ff1"), ("w_xo", "w_xq", "w_xkv", "w_mix_out"), ("w_att_up", "w_glu", "w_in"))


def _local_grads(x, mem, pos_col, target, sm, fetch, send, start_token):
    b_re_t = sm["ssm_b_re"].transpose(2, 0, 1)
    b_im_t = sm["ssm_b_im"].transpose(2, 0, 1)
    logdt = sm["ssm_log_dt"].reshape(SSM_GROUPS, 1)
    c_re, c_im = sm["ssm_c_re"], sm["ssm_c_im"]
    grp = (SSM_GROUPS, SSM_STATE)
    chn = (SSM_GROUP, SSM_GROUPS, SSM_STATE)

    wts = {}
    cos_t, sin_t = _rope_tables(pos_col, after=start_token, name="rope_tables")
    h0, xh0, rs0, h0m = _ln_fwd(x, None, sm["ln_in_g"], sm["ln_in_b"], alpha=1.0, name="ln_in_fwd")
    disc_in = (logdt, sm["ssm_a_re"], sm["ssm_a_im"], b_re_t, b_im_t)
    ab_re, ab_im, bb_re_t, bb_im_t = _whole(_disc, disc_in, [grp, grp, chn, chn], name="ssm_disc")
    a_re_rows, a_im_rows = ab_re.reshape(STATE_VREG_ROWS, LANES), ab_im.reshape(STATE_VREG_ROWS, LANES)
    tiles_cn = jnp.stack([bb_re_t.transpose(1, 0, 2), bb_im_t.transpose(1, 0, 2), c_re, -c_im])
    tiles_nc = jnp.stack([c_re.transpose(0, 2, 1), -c_im.transpose(0, 2, 1), bb_re_t.transpose(1, 2, 0),
                          bb_im_t.transpose(1, 2, 0)])
    wts.update(fetch(0, [h0m, tiles_cn, tiles_nc]))
    proj = _mm(h0m, wts["w_in"], bias=sm["b_in"], b_shards=True, name="in_proj")

    y, gy, h_re, h_im = _ssm_fwd(proj, tiles_cn, tiles_nc, a_re_rows, a_im_rows, sm["ssm_d"], name="ssm_fwd")

    q, k, v = _qkv_split(proj, cos_t, sin_t, name="qkv_split")
    outs, lses = [], []
    for g, dil in enumerate(DILATIONS):
        o_g, l_g = _dil_fwd(q[g], k[g], v[g], dil, name=f"dil_att_fwd_{dil}")
        outs.append(o_g)
        lses.append(l_g)
    att, lse = _att_merge(outs, lses, name="att_merge")
    wts.update(fetch(1, [att]))
    z = _mm(gy, wts["w_glu"], bias=sm["b_glu"], b_shards=True, name="glu_proj")
    b_att = _mm(att, wts["w_att_up"], b_shards=True, name="att_up")

    mixed = _mix_fwd(proj, z, b_att, name="gate_mix")
    mix_out = _mm(mixed, wts["w_mix_out"], bias=sm["b_mix_out"], name="mix_out")
    h1, xh1, rs1, h1m = _ln_fwd(h0, mix_out, sm["ln1_g"], sm["ln1_b"], alpha=DEEPNORM_ALPHA, name="ln1_fwd")

    wts.update(fetch(2, [h1m]))
    xq = _mm(h1m, wts["w_xq"], out_dtype=MXU_DTYPE, name="xatt_q")
    kv = _mm(mem, wts["w_xkv"], out_dtype=MXU_DTYPE, b_shards=True, name="xatt_kv")
    xo_in = _xatt_fwd(xq, kv, name="xatt_fwd")
    xo = _mm(xo_in, wts["w_xo"], name="xatt_o")
    h2, xh2, rs2, h2m = _ln_fwd(h1, xo, sm["ln2_g"], sm["ln2_b"], alpha=DEEPNORM_ALPHA, name="ln2_fwd")

    pre, act = _mm(h2m, wts["w_ff1"], bias=sm["b_ff1"], b_shards=True, name="ff1",
                   also=(lambda r: jnp.square(jnp.maximum(r, 0.0)), MXU_DTYPE))
    ff = _mm(act, wts["w_ff2"], bias=sm["b_ff2"], name="ff2")

    gw, gs = {}, {}
    dr3, dr3m, gs["ln3_g"], gs["ln3_b"], gs["b_ff2"], loss_row = _ln_loss_bwd(
        h2, ff, target, sm["ln3_g"], sm["ln3_b"], alpha=DEEPNORM_ALPHA, name="ln3_loss")
    wgrad = functools.partial(_mm, ta=True, out_dtype=WIRE_DTYPE, tk=2048)
    gw["w_ff2"] = wgrad(act, dr3m, tk=1024, name="ff2_dw")
    dpre, gs["b_ff1"] = _mm(dr3m, wts["w_ff2"], tb=True, out_dtype=MXU_DTYPE, colsum=True, name="ff2_dx",
                            gate=(pre, lambda p: 2.0 * jnp.maximum(p, 0.0)))
    gw["w_ff1"] = wgrad(h2m, dpre, out_shards=True, name="ff1_dw")
    sent = send(0, gw)
    dh2 = _mm(dpre, wts["w_ff1"], tb=True, b_shards=True, after=sent, name="ff1_dx")

    dr2, dr2m, gs["ln2_g"], gs["ln2_b"], _ = _ln_bwd(dr3, dh2, xh2, rs2, sm["ln2_g"], alpha=DEEPNORM_ALPHA,
                                                     name="ln2_bwd")
    gw["w_xo"] = wgrad(xo_in, dr2m, name="xatt_o_dw")
    dxo_in = _mm(dr2m, wts["w_xo"], tb=True, out_dtype=MXU_DTYPE, name="xatt_o_dx")
    dxq, dkv = _xatt_bwd(xq, kv, dxo_in, name="xatt_bwd")
    gw["w_xq"] = wgrad(h1m, dxq, name="xatt_q_dw")
    gw["w_xkv"] = wgrad(mem, dkv, out_shards=True, name="xatt_kv_dw")
    dh1 = _mm(dxq, wts["w_xq"], tb=True, name="xatt_q_dx")

    dr1, dr1m, gs["ln1_g"], gs["ln1_b"], gs["b_mix_out"] = _ln_bwd(dr2, dh1, xh1, rs1, sm["ln1_g"],
                                                                   alpha=DEEPNORM_ALPHA, name="ln1_bwd")
    gw["w_mix_out"] = wgrad(mixed, dr1m, name="mix_out_dw")
    sent = send(1, gw)
    dmixed = _mm(dr1m, wts["w_mix_out"], tb=True, after=sent, name="mix_out_dx")
    dgs, dga, dz, db_att, s_gs, s_ga, gs["b_glu"] = _mix_bwd(dmixed, proj, z, b_att, name="gate_mix_bwd")

    gw["w_att_up"] = wgrad(att, db_att, out_shards=True, name="att_up_dw")
    datt = _mm(db_att, wts["w_att_up"], tb=True, b_shards=True, name="att_up_dx")
    stats = _att_stats(datt, att, lse, name="att_stats")
    dqkv = [_dil_bwd(q[g], k[g], v[g], datt, stats, dil, name=f"dil_att_bwd_{dil}") for g, dil in enumerate(DILATIONS)]

    gw["w_glu"] = wgrad(gy, dz, out_shards=True, name="glu_dw")
    dgy = _mm(dz, wts["w_glu"], tb=True, b_shards=True, name="glu_dx")
    dy, gs["ssm_d"] = _gelu_bwd(dgy, y, proj, name="gelu_bwd")
    du, s_u, dc_re_t, dc_im_t, dbb_re_t, dbb_im_t, da_re, da_im = _ssm_bwd(
        proj, dy, h_re, h_im, tiles_cn, tiles_nc, a_re_rows, a_im_rows, sm["ssm_d"], name="ssm_bwd")
    gs["ssm_c_re"], gs["ssm_c_im"] = dc_re_t, -dc_im_t
    disc_ct = (da_re.reshape(grp), da_im.reshape(grp), dbb_re_t.transpose(1, 0, 2), dbb_im_t.transpose(1, 0, 2))
    d_logdt, gs["ssm_a_re"], gs["ssm_a_im"], d_b_re_t, d_b_im_t = _whole(
        _disc_transpose, disc_in + disc_ct, [(SSM_GROUPS, 1), grp, grp, chn, chn], name="ssm_disc_bwd")
    gs["ssm_log_dt"] = d_logdt
    gs["ssm_b_re"], gs["ssm_b_im"] = d_b_re_t.transpose(1, 2, 0), d_b_im_t.transpose(1, 2, 0)

    dproj, s_qkv = _dproj_assemble(du, dqkv, dgs, dga, cos_t, sin_t, name="dproj_assemble")
    gs["b_in"] = jnp.concatenate([s_u, *s_qkv, s_gs, s_ga], axis=1)
    gw["w_in"] = wgrad(h0m, dproj, out_shards=True, name="in_proj_dw")
    sent = send(2, gw)
    dh0 = _mm(dproj, wts["w_in"], tb=True, b_shards=True, after=sent, name="in_proj_dx")
    grad_x, gs["ln_in_g"], gs["ln_in_b"], _ = _ln_bwd(dr1, dh0, xh0, rs0, sm["ln_in_g"], alpha=DEEPNORM_ALPHA,
                                                      operand=False, name="ln_in_bwd")
    return loss_row, grad_x, gs


N_PEER = N_DEV - 1
_IN_HBM = pl.BlockSpec(memory_space=pltpu.HBM)
_IN_SEMAPHORE = pl.BlockSpec(memory_space=pltpu.SEMAPHORE)


def _device_index():
    return 4 * lax.axis_index("x") + 2 * lax.axis_index("y") + lax.axis_index("c")


def _exchange_copies(src_refs, land_refs, send_sems, recv_sems, scatter):
    x, y, c = lax.axis_index("x"), lax.axis_index("y"), lax.axis_index("c")
    me = 4 * x + 2 * y + c
    pairs = []
    for a, (src_ref, land_ref) in enumerate(zip(src_refs, land_refs)):
        for kk in range(1, N_DEV):
            px = (x + (kk >> 2)) % 2
            py = (y + ((kk >> 1) & 1)) % 2
            pc = (c + (kk & 1)) % 2
            peer = 4 * px + 2 * py + pc
            sem = a * N_PEER + kk - 1
            src = src_ref.at[peer] if scatter else src_ref

            def copy(dst, src=src, sem=sem, px=px, py=py, pc=pc):
                return pltpu.make_async_remote_copy(
                    src_ref=src, dst_ref=dst, send_sem=send_sems.at[sem], recv_sem=recv_sems.at[sem],
                    device_id=(px, py, pc), device_id_type=pl.DeviceIdType.MESH)

            pairs.append((functools.partial(copy, land_ref.at[me]), functools.partial(copy, land_ref.at[peer])))
    return pairs


def _own_copies(src_refs, land_refs, own_sems, scatter):
    me = _device_index()
    return [functools.partial(pltpu.make_async_copy, src_ref.at[me] if scatter else src_ref, land_ref.at[me],
                              own_sems.at[a]) for a, (src_ref, land_ref) in enumerate(zip(src_refs, land_refs))]


def _exchange_start(srcs, *, scatter, name, after=None):
    n_arr = len(srcs)
    lands = [lax.empty((N_DEV,) + tuple(s.shape[1:] if scatter else s.shape), s.dtype) for s in srcs]
    n_in = 2 * n_arr + (after is not None)

    def body(*refs):
        send_sems, recv_sems, own_sems = refs[n_in], refs[n_in + 1], refs[n_in + 2]
        for sent, _ in _exchange_copies(refs[:n_arr], refs[n_arr:2 * n_arr], send_sems, recv_sems, scatter):
            sent().start()
        for own in _own_copies(refs[:n_arr], refs[n_arr:2 * n_arr], own_sems, scatter):
            own().start()
        refs[-1][...] = jnp.zeros_like(refs[-1])

    through = [pltpu.HBM(t.shape, t.dtype) for t in (*srcs, *lands)]
    res = pl.pallas_call(
        body, name=name,
        out_shape=(pltpu.SemaphoreType.DMA((n_arr * N_PEER,)), pltpu.SemaphoreType.DMA((n_arr * N_PEER,)),
                   pltpu.SemaphoreType.DMA((n_arr,)), *through, jax.ShapeDtypeStruct((8, LANES), F32)),
        in_specs=[_IN_HBM] * (2 * n_arr) + [pl.BlockSpec(memory_space=pl.ANY)] * (after is not None),
        out_specs=(_IN_SEMAPHORE, _IN_SEMAPHORE, _IN_SEMAPHORE, *[_IN_HBM] * (2 * n_arr),
                   pl.BlockSpec(memory_space=pltpu.VMEM)),
        input_output_aliases={i: 3 + i for i in range(2 * n_arr)},
        compiler_params=pltpu.CompilerParams(has_side_effects=pltpu.SideEffectType.DATAFLOW_SIDE_EFFECTING),
    )(*[pltpu.with_memory_space_constraint(t, pltpu.HBM) for t in (*srcs, *lands)],
      *([after] if after is not None else []))
    return (res[0], res[1], res[2], res[3:3 + n_arr], res[3 + n_arr:3 + 2 * n_arr], scatter), res[-1]


def _exchange_wait(handle, *, after, name):
    send_sems, recv_sems, own_sems, srcs, lands, scatter = handle
    n_arr = len(srcs)
    after = list(after)

    def body(*refs):
        src_refs, land_refs = refs[:n_arr], refs[n_arr:2 * n_arr]
        for sent, received in _exchange_copies(src_refs, land_refs, refs[2 * n_arr], refs[2 * n_arr + 1], scatter):
            sent().wait_send()
            received().wait_recv()
        for own in _own_copies(src_refs, land_refs, refs[2 * n_arr + 2], scatter):
            own().wait()

    res = pl.pallas_call(
        body, name=name, out_shape=tuple(pltpu.HBM(t.shape, t.dtype) for t in (*srcs, *lands)),
        in_specs=[_IN_HBM] * (2 * n_arr) + [_IN_SEMAPHORE] * 3 + [pl.BlockSpec(memory_space=pl.ANY)] * len(after),
        out_specs=tuple([_IN_HBM] * (2 * n_arr)), input_output_aliases={i: i for i in range(2 * n_arr)},
        compiler_params=pltpu.CompilerParams(has_side_effects=pltpu.SideEffectType.DATAFLOW_SIDE_EFFECTING),
    )(*srcs, *lands, send_sems, recv_sems, own_sems, *after)
    return res[n_arr:]


def _adamw(g, w, m, v):
    m_new = ADAM_B1 * m + (1.0 - ADAM_B1) * g
    v_new = ADAM_B2 * v + (1.0 - ADAM_B2) * jnp.square(g)
    m_hat = m_new / (1.0 - ADAM_B1 ** ADAM_STEP)
    v_hat = v_new / (1.0 - ADAM_B2 ** ADAM_STEP)
    return g, -ADAM_LR * (m_hat / (jnp.sqrt(v_hat) + ADAM_EPS) + ADAM_WD * w), m_new, v_new


def _reduce_adamw(gstack, w, m, v, *, name, tr=128):
    n_rows, cols = w.shape
    tr = min(tr, n_rows)
    assert n_rows % tr == 0, (name, n_rows, tr)

    def body(g_ref, w_ref, m_ref, v_ref, *out_refs):
        g = g_ref[0].astype(F32)
        for dev in range(1, N_DEV):
            g = g + g_ref[dev].astype(F32)
        for o_ref, val in zip(out_refs, _adamw(g, w_ref[...], m_ref[...], v_ref[...])):
            o_ref[...] = val

    flat = pl.BlockSpec((tr, cols), lambda i: (i, 0))
    shape = jax.ShapeDtypeStruct((n_rows, cols), F32)
    return pl.pallas_call(
        body, name=name, grid=(n_rows // tr,),
        in_specs=[pl.BlockSpec((N_DEV, tr, cols), lambda i: (0, i, 0)), flat, flat, flat],
        out_specs=[flat] * 4, out_shape=[shape] * 4, compiler_params=_params("parallel"),
    )(gstack, w, m, v)


SMALL_FLAT_SSM = ("ssm_b_re", "ssm_b_im", "ssm_c_re", "ssm_c_im")


def _small_view(name, shape):
    size = int(np.prod(shape))
    if name in SMALL_FLAT_SSM:
        return SSM_GROUPS, size // SSM_GROUPS
    if name in ("ssm_a_re", "ssm_a_im"):
        return SSM_GROUPS, SSM_STATE
    return 1, size


def _pack_rows(view):
    return -(-(view[0] * view[1]) // PACK_COLS)


def _pack_small(gs, views):
    parts = []
    for n, view in zip(SMALL, views):
        flat = gs[n].reshape(-1).astype(WIRE_DTYPE)
        parts.append(jnp.pad(flat, (0, _pack_rows(view) * PACK_COLS - flat.shape[0])))
    total = sum(p.shape[0] for p in parts) // PACK_COLS
    parts.append(jnp.zeros(((-total % PACK_ROW_ALIGN) * PACK_COLS,), WIRE_DTYPE))
    return jnp.concatenate(parts).reshape(-1, PACK_COLS)


def _small_pieces(view):
    rows, cols = view
    if cols == PACK_COLS:
        return [(0, rows, 0, 0, 0, cols)]
    if rows == 1 and cols > PACK_COLS:
        return [(kk, 1, 0, 0, kk * PACK_COLS, PACK_COLS) for kk in range(cols // PACK_COLS)]
    if rows == 1:
        return [(0, 1, 0, 0, 0, cols)]
    return [((r * cols) // PACK_COLS, 1, (r * cols) % PACK_COLS, r, 0, cols) for r in range(rows)]


def _adamw_small(stack, views, w, m, v, *, name):
    n = len(SMALL)

    def body(stack_ref, *refs):
        ins, outs = refs[:3 * n], refs[3 * n:]
        first = 0
        for i, view in enumerate(views):
            for prow, nrows, lane, orow, ocol, width in _small_pieces(view):
                src = (slice(first + prow, first + prow + nrows), slice(lane, lane + width))
                dst = (slice(orow, orow + nrows), slice(ocol, ocol + width))
                g = stack_ref[(0,) + src].astype(F32)
                for dev in range(1, N_DEV):
                    g = g + stack_ref[(dev,) + src].astype(F32)
                res = _adamw(g, ins[i][dst], ins[n + i][dst], ins[2 * n + i][dst])
                for kk, val in enumerate(res):
                    outs[kk * n + i][dst] = val
            first += _pack_rows(view)

    res = pl.pallas_call(
        body, name=name, out_shape=[jax.ShapeDtypeStruct(view, F32) for _ in range(4) for view in views],
        compiler_params=pltpu.CompilerParams(vmem_limit_bytes=VMEM_LIMIT_BYTES),
    )(stack, *w, *m, *v)
    return [res[kk * n:(kk + 1) * n] for kk in range(4)]


def kernel(x, mem, positions, ln_in_g, ln_in_b, w_in, b_in, ssm_log_dt, ssm_a_re, ssm_a_im, ssm_b_re, ssm_b_im, ssm_c_re, ssm_c_im, ssm_d, w_glu, b_glu, w_att_up, w_mix_out, b_mix_out, ln1_g, ln1_b, w_xq, w_xkv, w_xo, ln2_g, ln2_b, w_ff1, b_ff1, w_ff2, b_ff2, ln3_g, ln3_b, loss_target, m_ln_in_g, m_ln_in_b, m_w_in, m_b_in, m_ssm_log_dt, m_ssm_a_re, m_ssm_a_im, m_ssm_b_re, m_ssm_b_im, m_ssm_c_re, m_ssm_c_im, m_ssm_d, m_w_glu, m_b_glu, m_w_att_up, m_w_mix_out, m_b_mix_out, m_ln1_g, m_ln1_b, m_w_xq, m_w_xkv, m_w_xo, m_ln2_g, m_ln2_b, m_w_ff1, m_b_ff1, m_w_ff2, m_b_ff2, m_ln3_g, m_ln3_b, v_ln_in_g, v_ln_in_b, v_w_in, v_b_in, v_ssm_log_dt, v_ssm_a_re, v_ssm_a_im, v_ssm_b_re, v_ssm_b_im, v_ssm_c_re, v_ssm_c_im, v_ssm_d, v_w_glu, v_b_glu, v_w_att_up, v_w_mix_out, v_b_mix_out, v_ln1_g, v_ln1_b, v_w_xq, v_w_xkv, v_w_xo, v_ln2_g, v_ln2_b, v_w_ff1, v_b_ff1, v_w_ff2, v_b_ff2, v_ln3_g, v_ln3_b):
    given = dict(locals())
    w_arg = {n: given[n] for n in WEIGHTS}
    m_arg = {n: given["m_" + n] for n in WEIGHTS}
    v_arg = {n: given["v_" + n] for n in WEIGHTS}

    shards = {n: w_arg[n][0].astype(MXU_DTYPE) for n in BIG}
    gathers, token = [], None
    for i, names in enumerate(GATHER_GROUPS):
        handle, token = _exchange_start([shards[n] for n in names], scatter=False, after=token, name=f"gather_start_{i}")
        gathers.append(handle)

    small_views = [_small_view(n, w_arg[n].shape) for n in SMALL]
    small_wmv = [d[n].reshape(view) for d in (w_arg, m_arg, v_arg) for n, view in zip(SMALL, small_views)]
    relaid = [t for t, n in zip(small_wmv, SMALL * 3) if n in SMALL_FLAT_SSM]

    def fetch(i, after):
        lands = _exchange_wait(gathers[i], after=after + (relaid if i == 0 else []), name=f"gather_wait_{i}")
        full = dict(zip(GATHER_GROUPS[i], lands))
        return {n: t if n in BIG_COL_SHARDED else t.reshape(-1, t.shape[-1]) for n, t in full.items()}

    scatters = {}

    def send(i, gw):
        slots = [gw[n] if n in BIG_COL_SHARDED else gw[n].reshape(N_DEV, -1, gw[n].shape[-1]) for n in SCATTER_GROUPS[i]]
        handle, sent = _exchange_start(slots, scatter=True, name=f"scatter_start_{i}")
        scatters[i] = (handle, slots)
        return sent

    sm = {}
    for n in SMALL:
        t = w_arg[n]
        if n.startswith("ssm_") and n not in ("ssm_d", "ssm_log_dt"):
            sm[n] = t[0]
        else:
            sm[n] = t.reshape(1, -1)

    loss_row, grad_x, gs = _local_grads(x[0], mem[0], positions.reshape(-1, 1), loss_target[0], sm, fetch, send, token)
    loss = lax.psum(loss_row[0, 0], ("x", "y", "c"))
    small = _pack_small(gs, small_views)
    small_handle, _ = _exchange_start([small], scatter=False, name="small_start")

    results = [{}, {}, {}, {}]
    done = grad_x
    for i, names in enumerate(SCATTER_GROUPS):
        handle, slots = scatters[i]
        lands = _exchange_wait(handle, after=[done], name=f"scatter_wait_{i}")
        for n, land, slot in zip(names, lands, slots):
            res = _reduce_adamw(land, w_arg[n][0], m_arg[n][0], v_arg[n][0], name="adamw_" + n)
            done = res[0]
            for d, r in zip(results, res):
                d[n] = r[None]
    small_stack = _exchange_wait(small_handle, after=[done], name="small_wait")[0]
    n_small = len(SMALL)
    res = _adamw_small(small_stack, small_views, small_wmv[:n_small], small_wmv[n_small:2 * n_small],
                       small_wmv[2 * n_small:], name="adamw_small")
    for d, r in zip(results, res):
        d.update({n: t.reshape(w_arg[n].shape) for n, t in zip(SMALL, r)})
    out = [loss, grad_x[None]]
    for d in results:
        out += [d[n] for n in WEIGHTS]
    return tuple(out)
```

```python
import functools

import numpy as np
import jax
import jax.numpy as jnp
from jax import lax
from jax.experimental import pallas as pl
from jax.experimental.pallas import tpu as pltpu

F32 = jnp.float32
MXU_DTYPE = jnp.bfloat16
WIRE_DTYPE = jnp.bfloat16
VMEM_LIMIT_BYTES = 48 * 1024 * 1024
LANES = 128

N_DEV = 8
D_MODEL = 1024
SSM_GROUP = 16
SSM_WIDTH = 768
SSM_GROUPS = SSM_WIDTH // SSM_GROUP
SSM_STATE = 64
SSM_CH = SSM_GROUPS * SSM_STATE
SSM_TILES = SSM_WIDTH // LANES
GROUPS_PER_TILE = LANES // SSM_GROUP
STATE_VREG_ROWS = SSM_CH // LANES
ATT_HEAD_DIM = 64
ATT_HEADS_PER_GROUP = 4
ATT_MERGED = ATT_HEADS_PER_GROUP * ATT_HEAD_DIM
LANE_HALVES = ATT_MERGED // LANES
DILATIONS = (1, 4, 16)
ATT_BLK = 128
ATT_SCALE = ATT_HEAD_DIM ** -0.5
ROT_DIM = ATT_HEAD_DIM // 4
ROPE_THETA = 500000.0
XATT_HEADS = 4
XATT_HEAD_DIM = D_MODEL // XATT_HEADS
XATT_SCALE = XATT_HEAD_DIM ** -0.5
DEEPNORM_ALPHA = 2.0 ** 0.25
LN_EPS = 1e-5
NEG_INF = -1e30
OFF_Q_BLK, OFF_K_BLK, OFF_V_BLK = 3, 6, 9
OFF_GS_BLK, OFF_GA_BLK = 3, 4

ADAM_LR = 0.001
ADAM_B1 = 0.9
ADAM_B2 = 0.999
ADAM_EPS = 1e-08
ADAM_WD = 0.01
ADAM_STEP = 10

BIG = ("w_in", "w_glu", "w_att_up", "w_mix_out", "w_xq", "w_xkv", "w_xo", "w_ff1", "w_ff2")
BIG_COL_SHARDED = ("w_in", "w_glu", "w_att_up", "w_xkv", "w_ff1")
WEIGHTS = ("ln_in_g", "ln_in_b", "w_in", "b_in", "ssm_log_dt", "ssm_a_re", "ssm_a_im", "ssm_b_re", "ssm_b_im",
           "ssm_c_re", "ssm_c_im", "ssm_d", "w_glu", "b_glu", "w_att_up", "w_mix_out", "b_mix_out", "ln1_g", "ln1_b",
           "w_xq", "w_xkv", "w_xo", "ln2_g", "ln2_b", "w_ff1", "b_ff1", "w_ff2", "b_ff2", "ln3_g", "ln3_b")
SMALL = tuple(n for n in WEIGHTS if n not in BIG)
PACK_COLS = 1024
PACK_ROW_ALIGN = 16


def _params(*sem):
    return pltpu.CompilerParams(dimension_semantics=sem, vmem_limit_bytes=VMEM_LIMIT_BYTES)


def _dot(a, b, ca, cb):
    return lax.dot_general(a.astype(MXU_DTYPE), b.astype(MXU_DTYPE), (((ca,), (cb,)), ((), ())),
                           preferred_element_type=F32)


def _fit(dim, pref):
    if dim <= pref:
        return dim
    best = max(t for t in range(LANES, pref + 1, LANES) if dim % t == 0)
    return best


def _mm(a, b, *, name, ta=False, tb=False, bias=None, out_dtype=F32, b_shards=False, out_shards=False, after=None,
        also=None, gate=None, colsum=False, tm=2048, tn=1024, tk=1024):
    m, k = (a.shape[1], a.shape[0]) if ta else a.shape
    order = (lambda f: (lambda j, i, kk: f(i, j, kk))) if colsum else (lambda f: f)
    spec = lambda shape, f: pl.BlockSpec(shape, order(f))
    if b_shards:
        n_sh, rows, n_loc = b.shape
        if tb:
            n, tn, tk = rows, _fit(rows, tn), n_loc
            assert k == n_sh * n_loc, (name, k, b.shape)
            b_spec = spec((1, tn, tk), lambda i, j, kk: (kk, j, 0))
        else:
            n, tn, tk = n_sh * n_loc, n_loc, _fit(k, tk)
            b_spec = spec((1, tk, tn), lambda i, j, kk: (j, kk, 0))
    else:
        n = b.shape[0] if tb else b.shape[1]
        tn = n // N_DEV if out_shards else _fit(n, tn)
        tk = _fit(k, tk)
        b_spec = spec((tn, tk), lambda i, j, kk: (j, kk)) if tb else spec((tk, tn), lambda i, j, kk: (kk, j))
    tm = _fit(m, tm)
    nk = k // tk
    a_spec = spec((tk, tm), lambda i, j, kk: (kk, i)) if ta else spec((tm, tk), lambda i, j, kk: (i, kk))
    tile = spec((tm, tn), lambda i, j, kk: (i, j))
    in_specs, args = [a_spec, b_spec], [a, b]
    if bias is not None:
        in_specs.append(spec((1, tn), lambda i, j, kk: (0, j)))
        args.append(bias)
    if gate is not None:
        in_specs.append(tile)
        args.append(gate[0])
    if after is not None:
        in_specs.append(pl.BlockSpec(memory_space=pl.ANY))
        args.append(after)
    n_in = len(args)
    if out_shards:
        assert n == N_DEV * tn, (name, n, tn)
        out_specs = [spec((1, tm, tn), lambda i, j, kk: (j, i, 0))]
        out_shape = [jax.ShapeDtypeStruct((N_DEV, m, tn), out_dtype)]
    else:
        out_specs = [tile]
        out_shape = [jax.ShapeDtypeStruct((m, n), out_dtype)]
    if also is not None:
        out_specs.append(tile)
        out_shape.append(jax.ShapeDtypeStruct((m, n), also[1]))
    if colsum:
        out_specs.append(spec((1, tn), lambda i, j, kk: (0, j)))
        out_shape.append(jax.ShapeDtypeStruct((1, n), F32))

    def body(*refs):
        a_ref, b_ref = refs[0], refs[1]
        o_ref = refs[n_in]

        def product():
            return _dot(a_ref[...], b_ref[0] if b_shards else b_ref[...], 0 if ta else 1, 1 if tb else 0)

        def finish(r):
            if bias is not None:
                r = r + refs[2][...]
            if gate is not None:
                r = r * gate[1](refs[2 + (bias is not None)][...])
            if out_shards:
                o_ref[0] = r.astype(o_ref.dtype)
            else:
                o_ref[...] = r.astype(o_ref.dtype)
            if also is not None:
                refs[n_in + 1][...] = also[0](r).astype(also[1])
            if colsum:
                s_ref = refs[n_in + 1 + (also is not None)]

                @pl.when(pl.program_id(1) == 0)
                def _():
                    s_ref[...] = jnp.zeros_like(s_ref)

                s_ref[...] += _colsum(r)

        if nk == 1:
            finish(product())
            return
        acc_ref = refs[-1]
        kk = pl.program_id(2)

        @pl.when(kk == 0)
        def _():
            acc_ref[...] = jnp.zeros_like(acc_ref)

        acc_ref[...] += product()

        @pl.when(kk == nk - 1)
        def _():
            finish(acc_ref[...])

    grid = (n // tn, m // tm, nk) if colsum else (m // tm, n // tn, nk)
    res = pl.pallas_call(
        body, name=name, grid=grid, in_specs=in_specs, out_specs=out_specs, out_shape=out_shape,
        scratch_shapes=[pltpu.VMEM((tm, tn), F32)] if nk > 1 else [],
        compiler_params=_params("parallel", "arbitrary" if colsum else "parallel", "arbitrary"),
    )(*args)
    return res[0] if len(res) == 1 else res


def _rowcall(fn, rows, fulls, row_outs, acc_outs=(), *, n_rows, tm, name, after=None):
    n_r, n_f, n_o, n_a = len(rows), len(fulls), len(row_outs), len(acc_outs)
    n_in = n_r + n_f + (after is not None)
    assert n_rows % tm == 0, (name, n_rows, tm)

    def body(*refs):
        res = fn(*[r[...] for r in refs[:n_r + n_f]])
        res = tuple(res) if isinstance(res, (tuple, list)) else (res,)
        o_refs = refs[n_in:n_in + n_o]
        a_refs = refs[n_in + n_o:]
        for o_ref, val in zip(o_refs, res[:n_o]):
            o_ref[...] = val.astype(o_ref.dtype)
        if n_a:
            @pl.when(pl.program_id(0) == 0)
            def _():
                for a_ref in a_refs:
                    a_ref[...] = jnp.zeros_like(a_ref)

            for a_ref, val in zip(a_refs, res[n_o:]):
                a_ref[...] += val

    in_specs = [pl.BlockSpec((tm, w), functools.partial(lambda i, cb: (i, cb), cb=cb)) for _, w, cb in rows]
    in_specs += [pl.BlockSpec(f.shape, functools.partial(lambda i, nd: (0,) * nd, nd=f.ndim)) for f in fulls]
    in_specs += [pl.BlockSpec(memory_space=pl.ANY)] * (after is not None)
    out_specs = [pl.BlockSpec((tm, w), lambda i: (i, 0)) for w, _ in row_outs]
    out_specs += [pl.BlockSpec((1, w), lambda i: (0, 0)) for w in acc_outs]
    out_shape = [jax.ShapeDtypeStruct((n_rows, w), dt) for w, dt in row_outs]
    out_shape += [jax.ShapeDtypeStruct((1, w), F32) for w in acc_outs]
    return pl.pallas_call(
        body, name=name, grid=(n_rows // tm,), in_specs=in_specs, out_specs=out_specs, out_shape=out_shape,
        compiler_params=_params("arbitrary" if n_a else "parallel"),
    )(*[r[0] for r in rows], *fulls, *([after] if after is not None else []))


def _colsum(v):
    return jnp.sum(v, axis=0, keepdims=True)


def _ln_fwd(a, r, g, b, *, alpha, name):
    n_rows, d = a.shape

    def fn(*t):
        xin = t[0] if alpha == 1.0 else alpha * t[0]
        if r is not None:
            xin = xin + t[1]
        gv, bv = t[-2], t[-1]
        mu = jnp.mean(xin, axis=-1, keepdims=True)
        xc = xin - mu
        var = jnp.mean(xc * xc, axis=-1, keepdims=True)
        rstd = lax.rsqrt(var + LN_EPS)
        xh = xc * rstd
        y = xh * gv + bv
        return y, xh, rstd, y

    rows = [(a, d, 0)] + ([(r, d, 0)] if r is not None else [])
    return _rowcall(fn, rows, [g, b], [(d, F32), (d, F32), (1, F32), (d, MXU_DTYPE)], n_rows=n_rows, tm=256, name=name)


def _ln_bwd(dya, dyb, xh, rstd, g, *, alpha, name, operand=True):
    n_rows, d = xh.shape

    def fn(da, db, xhv, rs, gv):
        dy = alpha * da + db
        dyg = dy * gv
        m1 = jnp.mean(dyg, axis=-1, keepdims=True)
        m2 = jnp.mean(dyg * xhv, axis=-1, keepdims=True)
        dx = rs * (dyg - m1 - xhv * m2)
        return (dx,) + ((dx,) if operand else ()) + (_colsum(dy * xhv), _colsum(dy), _colsum(dx))

    rows = [(dya, d, 0), (dyb, d, 0), (xh, d, 0), (rstd, 1, 0)]
    return _rowcall(fn, rows, [g], [(d, F32)] + [(d, MXU_DTYPE)] * operand, [d, d, d], n_rows=n_rows, tm=256, name=name)


def _ln_loss_bwd(a, r, target, g, b, *, alpha, name):
    n_rows, d = a.shape

    def fn(av, rv, tv, gv, bv):
        xin = alpha * av + rv
        mu = jnp.mean(xin, axis=-1, keepdims=True)
        xc = xin - mu
        var = jnp.mean(xc * xc, axis=-1, keepdims=True)
        rs = lax.rsqrt(var + LN_EPS)
        xh = xc * rs
        diff = xh * gv + bv - tv
        part = jnp.sum(jnp.sum(diff * diff, axis=1, keepdims=True), axis=0, keepdims=True) * (0.5 / d)
        dy = diff * (1.0 / d)
        dyg = dy * gv
        m1 = jnp.mean(dyg, axis=-1, keepdims=True)
        m2 = jnp.mean(dyg * xh, axis=-1, keepdims=True)
        dx = rs * (dyg - m1 - xh * m2)
        return dx, dx, _colsum(dy * xh), _colsum(dy), _colsum(dx), jnp.broadcast_to(part, (1, LANES))

    return _rowcall(fn, [(a, d, 0), (r, d, 0), (target, d, 0)], [g, b], [(d, F32), (d, MXU_DTYPE)], [d, d, d, LANES],
                    n_rows=n_rows, tm=256, name=name)


def _rope_lane_constants():
    lane = np.arange(ATT_MERGED)
    in_head = lane % ATT_HEAD_DIM
    sign = np.where(in_head < ROT_DIM // 2, -1.0, np.where(in_head < ROT_DIM, 1.0, 0.0)).astype(np.float32)
    inv_freq = ROPE_THETA ** (-jnp.arange(0, ROT_DIM, 2, dtype=F32) / ROT_DIM)
    return inv_freq[lane % (ROT_DIM // 2)].reshape(1, ATT_MERGED), jnp.asarray(sign).reshape(1, ATT_MERGED)


def _rope_tables(pos_col, *, name, after=None):
    inv_lane, sign = _rope_lane_constants()

    def fn(pos, inv, sg):
        ang = pos.astype(F32) * inv
        return jnp.where(sg != 0.0, jnp.cos(ang), 1.0), sg * jnp.sin(ang)

    return _rowcall(fn, [(pos_col, 1, 0)], [inv_lane, sign], [(ATT_MERGED, F32), (ATT_MERGED, F32)],
                    n_rows=pos_col.shape[0], tm=512, name=name, after=after)


def _rot_partner(t):
    lane = lax.broadcasted_iota(jnp.int32, t.shape, 1)
    width = t.shape[1]
    return jnp.where((lane & (ROT_DIM // 2)) == 0, pltpu.roll(t, width - ROT_DIM // 2, 1), pltpu.roll(t, ROT_DIM // 2, 1))


def _rope(t, cos_t, sin_t):
    return t * cos_t + _rot_partner(t) * sin_t


def _rope_transpose(dt, cos_t, sin_t):
    return dt * cos_t + _rot_partner(dt * sin_t)


def _strided_rows(r, count, stride):
    return pl.ds(r, count) if stride == 1 else pl.ds(r, count, stride=stride)


def _qkv_split(proj, cos_t, sin_t, *, name, tm=512):
    n_rows = proj.shape[0]
    n_g = len(DILATIONS)

    def body(*refs):
        n_src = LANE_HALVES * 3 * n_g
        src, tables, dst = refs[:n_src], refs[n_src:n_src + 2 * LANE_HALVES], refs[n_src + 2 * LANE_HALVES:]
        for kind in range(3):
            for g, dil in enumerate(DILATIONS):
                for half in range(LANE_HALVES):
                    x_ref, o_ref = src[(kind * n_g + g) * LANE_HALVES + half], dst[kind * n_g + g]
                    cos_ref, sin_ref = tables[half], tables[LANE_HALVES + half]
                    for r in range(dil):
                        rows = _strided_rows(r, tm // dil, dil)
                        t = x_ref[rows, :]
                        if kind < 2:
                            t = _rope(t, cos_ref[rows, :], sin_ref[rows, :])
                        lo = r * ATT_MERGED + half * LANES
                        o_ref[:, lo:lo + LANES] = t.astype(o_ref.dtype)

    half_spec = lambda cb: pl.BlockSpec((tm, LANES), functools.partial(lambda i, cb: (i, cb), cb=cb))
    in_specs = [half_spec((off + g) * LANE_HALVES + half)
                for off in (OFF_Q_BLK, OFF_K_BLK, OFF_V_BLK) for g in range(n_g) for half in range(LANE_HALVES)]
    in_specs += [half_spec(half) for _ in range(2) for half in range(LANE_HALVES)]
    out_specs = [pl.BlockSpec((tm // dil, dil * ATT_MERGED), lambda i: (i, 0)) for _ in range(3) for dil in DILATIONS]
    out_shape = [jax.ShapeDtypeStruct((n_rows // dil, dil * ATT_MERGED), MXU_DTYPE) for _ in range(3) for dil in DILATIONS]
    outs = pl.pallas_call(
        body, name=name, grid=(n_rows // tm,), in_specs=in_specs, out_specs=out_specs, out_shape=out_shape,
        compiler_params=_params("parallel"),
    )(*[proj] * (LANE_HALVES * 3 * n_g), *[cos_t] * LANE_HALVES, *[sin_t] * LANE_HALVES)
    return outs[:n_g], outs[n_g:2 * n_g], outs[2 * n_g:]


def _mix(gs, ga, z1, z2, b_att):
    return jax.nn.sigmoid(gs) * (z1 * jax.nn.sigmoid(z2)) + jax.nn.sigmoid(ga) * b_att


def _mix_rows(proj, z, b_att):
    return [(proj, D_MODEL, OFF_GS_BLK), (proj, D_MODEL, OFF_GA_BLK), (z, D_MODEL, 0), (z, D_MODEL, 1), (b_att, D_MODEL, 0)]


def _mix_fwd(proj, z, b_att, *, name):
    return _rowcall(_mix, _mix_rows(proj, z, b_att), [], [(D_MODEL, MXU_DTYPE)],
                    n_rows=proj.shape[0], tm=256, name=name)[0]


def _mix_bwd(dmixed, proj, z, b_att, *, name):
    def fn(dm, gs, ga, z1, z2, ba):
        _, vjp = jax.vjp(_mix, gs, ga, z1, z2, ba)
        dgs, dga, dz1, dz2, dba = vjp(dm)
        dz = jnp.concatenate([dz1, dz2], axis=1)
        return dgs, dga, dz, dba, _colsum(dgs), _colsum(dga), _colsum(dz)

    rows = [(dmixed, D_MODEL, 0)] + _mix_rows(proj, z, b_att)
    widths = [D_MODEL, D_MODEL, 2 * D_MODEL, D_MODEL]
    return _rowcall(fn, rows, [], [(w, MXU_DTYPE) for w in widths], widths[:3], n_rows=proj.shape[0], tm=256, name=name)


def _gelu_bwd(dgy, y, proj, *, name):
    def fn(dg, yv, u):
        _, vjp = jax.vjp(jax.nn.gelu, yv)
        dy = vjp(dg)[0]
        return dy, _colsum(dy * u)

    return _rowcall(fn, [(dgy, SSM_WIDTH, 0), (y, SSM_WIDTH, 0), (proj, SSM_WIDTH, 0)], [], [(SSM_WIDTH, F32)],
                    [SSM_WIDTH], n_rows=y.shape[0], tm=512, name=name)


HEAD_ROWS = ATT_HEADS_PER_GROUP * ATT_BLK


def _head_masks(rows):
    head = lax.broadcasted_iota(jnp.int32, (rows, ATT_MERGED), 1) >> (ATT_HEAD_DIM.bit_length() - 1)
    return [head == h for h in range(ATT_HEADS_PER_GROUP)]


def _stack_heads(t, masks):
    return jnp.concatenate([jnp.where(m, t, jnp.zeros_like(t)) for m in masks], axis=0)


def _unstack_heads(t4, masks):
    blocks = [t4[h * ATT_BLK:(h + 1) * ATT_BLK] for h in range(ATT_HEADS_PER_GROUP)]
    return jnp.where(masks[0], blocks[0], jnp.where(masks[1], blocks[1], jnp.where(masks[2], blocks[2], blocks[3])))


def _head_column(stats, first):
    return jnp.concatenate([stats[:, first + h:first + h + 1] for h in range(ATT_HEADS_PER_GROUP)], axis=0)


def _band_mask(first_key):
    qi = lax.broadcasted_iota(jnp.int32, (HEAD_ROWS, 2 * ATT_BLK), 0) & (ATT_BLK - 1)
    ki = lax.broadcasted_iota(jnp.int32, (HEAD_ROWS, 2 * ATT_BLK), 1)
    steps = qi + ATT_BLK - ki
    return (steps >= 0) & (steps <= ATT_BLK) & (ki >= first_key)


def _dil_fwd(q, k, v, dil, *, name):
    n_blk = q.shape[0] // ATT_BLK
    cur = pl.BlockSpec((ATT_BLK, ATT_MERGED), lambda r, n: (n, r))
    prev = pl.BlockSpec((ATT_BLK, ATT_MERGED), lambda r, n: (jnp.maximum(n - 1, 0), r))

    def body(q_ref, kp_ref, kc_ref, vp_ref, vc_ref, o_ref, l_ref):
        masks = _head_masks(ATT_BLK)
        valid = _band_mask(jnp.where(pl.program_id(1) > 0, 0, ATT_BLK))
        keys = jnp.concatenate([kp_ref[...], kc_ref[...]], axis=0)
        vals = jnp.concatenate([vp_ref[...], vc_ref[...]], axis=0)
        s = jnp.where(valid, _dot(_stack_heads(q_ref[...], masks), keys, 1, 1) * ATT_SCALE, NEG_INF)
        m = jnp.max(s, axis=-1, keepdims=True)
        p = jnp.exp(s - m)
        den = jnp.sum(p, axis=-1, keepdims=True)
        o_ref[...] = _unstack_heads(_dot(p, vals, 1, 0) / den, masks)
        l_ref[...] = _unstack_heads(jnp.broadcast_to(m + jnp.log(den), (HEAD_ROWS, ATT_MERGED)), masks)

    shape = jax.ShapeDtypeStruct(q.shape, F32)
    return pl.pallas_call(
        body, name=name, grid=(dil, n_blk), in_specs=[cur, prev, cur, prev, cur], out_specs=[cur, cur],
        out_shape=[shape, shape], compiler_params=_params("parallel", "parallel"),
    )(q, k, k, v, v)


def _att_merge(outs, lses, *, name, tm=512):
    n_g = len(outs)
    n_rows = outs[0].shape[0] * DILATIONS[0]

    def body(*refs):
        src, (att_ref, lse_ref), tmp = refs[:2 * n_g], refs[2 * n_g:2 * n_g + 2], refs[2 * n_g + 2:]
        vals = []
        for idx, src_ref in enumerate(src):
            dil = DILATIONS[idx % n_g]
            if dil == 1:
                vals.append(src_ref[...])
                continue
            for r in range(dil):
                for half in range(LANE_HALVES):
                    lo = r * ATT_MERGED + half * LANES
                    tmp[LANE_HALVES * idx + half][_strided_rows(r, tm // dil, dil), :] = src_ref[:, lo:lo + LANES]
            vals.append(jnp.concatenate([tmp[LANE_HALVES * idx + half][...] for half in range(LANE_HALVES)], axis=1))
        o, l = vals[:n_g], vals[n_g:]
        m = functools.reduce(jnp.maximum, l)
        e = [jnp.exp(li - m) for li in l]
        z = functools.reduce(jnp.add, e)
        att_ref[...] = functools.reduce(jnp.add, [(ei / z) * oi for ei, oi in zip(e, o)])
        lse_ref[...] = m + jnp.log(z)

    in_specs = [pl.BlockSpec((tm // dil, dil * ATT_MERGED), lambda i: (i, 0)) for _ in range(2) for dil in DILATIONS]
    row = pl.BlockSpec((tm, ATT_MERGED), lambda i: (i, 0))
    shape = jax.ShapeDtypeStruct((n_rows, ATT_MERGED), F32)
    return pl.pallas_call(
        body, name=name, grid=(n_rows // tm,), in_specs=in_specs, out_specs=[row, row], out_shape=[shape, shape],
        scratch_shapes=[pltpu.VMEM((tm, LANES), F32)] * (LANE_HALVES * 2 * n_g), compiler_params=_params("parallel"),
    )(*outs, *lses)


def _att_stats(datt, att, lse, *, name):
    n_rows = datt.shape[0]

    def fn(d, a, l):
        prod = d * a
        lane = lax.broadcasted_iota(jnp.int32, (d.shape[0], LANES), 1)
        out = jnp.zeros((d.shape[0], LANES), F32)
        for h in range(ATT_HEADS_PER_GROUP):
            lo = h * ATT_HEAD_DIM
            out = jnp.where(lane == h, l[:, lo:lo + 1], out)
            delta = jnp.sum(prod[:, lo:lo + ATT_HEAD_DIM], axis=-1, keepdims=True)
            out = jnp.where(lane == ATT_HEADS_PER_GROUP + h, delta, out)
        return out

    rows = [(t, ATT_MERGED, 0) for t in (datt, att, lse)]
    return _rowcall(fn, rows, [], [(LANES, F32)], n_rows=n_rows, tm=512, name=name)[0]


def _dil_bwd(q, k, v, datt, stats, dil, *, name):
    n_rows = datt.shape[0]
    n_blk = n_rows // dil // ATT_BLK
    span = ATT_BLK * dil
    cur = pl.BlockSpec((ATT_BLK, ATT_MERGED), lambda n, r: (n, r))
    prev = pl.BlockSpec((ATT_BLK, ATT_MERGED), lambda n, r: (jnp.maximum(n - 1, 0), r))
    nxt = pl.BlockSpec((ATT_BLK, ATT_MERGED), lambda n, r: (jnp.minimum(n + 1, n_blk - 1), r))
    seq = lambda half, ahead: pl.BlockSpec((span, LANES), lambda n, r: (jnp.minimum(n + ahead, n_blk - 1), half))

    def body(qc_ref, qn_ref, kp_ref, kc_ref, vp_ref, vc_ref, dc0_ref, dc1_ref, dn0_ref, dn1_ref, sc_ref, sn_ref,
             dq0_ref, dq1_ref, dk0_ref, dk1_ref, dv0_ref, dv1_ref):
        n = pl.program_id(0)
        rows = slice(None) if dil == 1 else _strided_rows(pl.program_id(1), ATT_BLK, dil)

        def read(ref0, ref1):
            return jnp.concatenate([ref0[rows, :], ref1[rows, :]], axis=1)

        def write(ref0, ref1, val):
            ref0[rows, :] = val[:, :LANES]
            ref1[rows, :] = val[:, LANES:]

        masks = _head_masks(ATT_BLK)
        valid = _band_mask(jnp.where(n > 0, 0, ATT_BLK))
        qi = lax.broadcasted_iota(jnp.int32, (HEAD_ROWS, ATT_BLK), 0) & (ATT_BLK - 1)
        ki = lax.broadcasted_iota(jnp.int32, (HEAD_ROWS, ATT_BLK), 1)
        valid_next = (ki - qi) >= jnp.where(n < n_blk - 1, 0, ATT_BLK)

        kc, vc = kc_ref[...], vc_ref[...]
        keys = jnp.concatenate([kp_ref[...], kc], axis=0)
        vals = jnp.concatenate([vp_ref[...], vc], axis=0)
        q4 = _stack_heads(qc_ref[...], masks)
        d4 = _stack_heads(read(dc0_ref, dc1_ref).astype(MXU_DTYPE), masks)
        st = sc_ref[rows, :]
        p = jnp.where(valid, jnp.exp(_dot(q4, keys, 1, 1) * ATT_SCALE - _head_column(st, 0)), 0.0)
        ds = p * (_dot(d4, vals, 1, 1) - _head_column(st, ATT_HEADS_PER_GROUP)) * ATT_SCALE
        write(dq0_ref, dq1_ref, _unstack_heads(_dot(ds, keys, 1, 0), masks))

        q4n = _stack_heads(qn_ref[...], masks)
        d4n = _stack_heads(read(dn0_ref, dn1_ref).astype(MXU_DTYPE), masks)
        stn = sn_ref[rows, :]
        p_n = jnp.where(valid_next, jnp.exp(_dot(q4n, kc, 1, 1) * ATT_SCALE - _head_column(stn, 0)), 0.0)
        ds_n = p_n * (_dot(d4n, vc, 1, 1) - _head_column(stn, ATT_HEADS_PER_GROUP)) * ATT_SCALE
        write(dv0_ref, dv1_ref, _dot(p[:, ATT_BLK:], d4, 0, 0) + _dot(p_n, d4n, 0, 0))
        write(dk0_ref, dk1_ref, _dot(ds[:, ATT_BLK:], q4, 0, 0) + _dot(ds_n, q4n, 0, 0))

    shape = jax.ShapeDtypeStruct((n_rows, LANES), F32)
    out = seq(0, 0)
    res = pl.pallas_call(
        body, name=name, grid=(n_blk, dil),
        in_specs=[cur, nxt, prev, cur, prev, cur, seq(0, 0), seq(1, 0), seq(0, 1), seq(1, 1), seq(0, 0), seq(0, 1)],
        out_specs=[out] * 6, out_shape=[shape] * 6, compiler_params=_params("parallel", "arbitrary"),
    )(q, q, k, k, v, v, datt, datt, datt, datt, stats, stats)
    return [(res[2 * i], res[2 * i + 1]) for i in range(3)]


def _dproj_assemble(du, dqkv, dgs, dga, cos_t, sin_t, *, name):
    n_g = len(DILATIONS)

    def fn(*t):
        n_half = LANE_HALVES * 3 * n_g
        du_t, halves, (dgs_t, dga_t, c, s) = t[0], t[1:1 + n_half], t[1 + n_half:]
        parts = [jnp.concatenate(halves[LANE_HALVES * i:LANE_HALVES * (i + 1)], axis=1) for i in range(3 * n_g)]
        for i in range(2 * n_g):
            parts[i] = _rope_transpose(parts[i], c, s)
        cast = [p.astype(MXU_DTYPE) for p in parts]
        return [jnp.concatenate([du_t] + cast + [dgs_t, dga_t], axis=1)] + [_colsum(p) for p in parts]

    rows = [(du, SSM_WIDTH, 0)]
    rows += [(half, LANES, 0) for i in range(3) for g in range(n_g) for half in dqkv[g][i]]
    rows += [(dgs, D_MODEL, 0), (dga, D_MODEL, 0), (cos_t, ATT_MERGED, 0), (sin_t, ATT_MERGED, 0)]
    width = SSM_WIDTH + 3 * n_g * ATT_MERGED + 2 * D_MODEL
    res = _rowcall(fn, rows, [], [(width, MXU_DTYPE)], [ATT_MERGED] * (3 * n_g), n_rows=du.shape[0], tm=256, name=name)
    return res[0], res[1:]


def _xhead(h):
    return slice(h * XATT_HEAD_DIM, (h + 1) * XATT_HEAD_DIM)


def _xatt_probs(qh, kh):
    s = _dot(qh, kh, 1, 1) * XATT_SCALE
    e = jnp.exp(s - jnp.max(s, axis=-1, keepdims=True))
    return e / jnp.sum(e, axis=-1, keepdims=True)


def _xatt_fwd(q, kv, *, name, tm=512):
    n_rows = q.shape[0]
    n_mem = kv.shape[0]

    def body(q_ref, kv_ref, o_ref):
        for h in range(XATT_HEADS):
            sl = _xhead(h)
            p = _xatt_probs(q_ref[:, sl], kv_ref[:, sl])
            o_ref[:, sl] = _dot(p, kv_ref[:, D_MODEL + h * XATT_HEAD_DIM:D_MODEL + (h + 1) * XATT_HEAD_DIM], 1, 0
                                ).astype(o_ref.dtype)

    row = pl.BlockSpec((tm, D_MODEL), lambda i: (i, 0))
    return pl.pallas_call(
        body, name=name, grid=(n_rows // tm,),
        in_specs=[row, pl.BlockSpec((n_mem, 2 * D_MODEL), lambda i: (0, 0))], out_specs=row,
        out_shape=jax.ShapeDtypeStruct((n_rows, D_MODEL), MXU_DTYPE), compiler_params=_params("parallel"),
    )(q, kv)


def _xatt_bwd(q, kv, do, *, name, tm=512):
    n_rows = q.shape[0]
    n_mem = kv.shape[0]

    def body(q_ref, kv_ref, do_ref, dq_ref, dkv_ref):
        @pl.when(pl.program_id(0) == 0)
        def _():
            dkv_ref[...] = jnp.zeros_like(dkv_ref)

        for h in range(XATT_HEADS):
            sl = _xhead(h)
            vsl = slice(D_MODEL + h * XATT_HEAD_DIM, D_MODEL + (h + 1) * XATT_HEAD_DIM)
            qh, kh, doh = q_ref[:, sl], kv_ref[:, sl], do_ref[:, sl]
            p = _xatt_probs(qh, kh)
            dp = _dot(doh, kv_ref[:, vsl], 1, 1)
            ds = p * (dp - jnp.sum(dp * p, axis=-1, keepdims=True)) * XATT_SCALE
            dq_ref[:, sl] = _dot(ds, kh, 1, 0).astype(dq_ref.dtype)
            dkv_ref[:, sl] += _dot(ds, qh, 0, 0)
            dkv_ref[:, vsl] += _dot(p, doh, 0, 0)

    row = pl.BlockSpec((tm, D_MODEL), lambda i: (i, 0))
    full = pl.BlockSpec((n_mem, 2 * D_MODEL), lambda i: (0, 0))
    return pl.pallas_call(
        body, name=name, grid=(n_rows // tm,), in_specs=[row, full, row], out_specs=[row, full],
        out_shape=[jax.ShapeDtypeStruct((n_rows, D_MODEL), MXU_DTYPE), jax.ShapeDtypeStruct((n_mem, 2 * D_MODEL), F32)],
        compiler_params=_params("arbitrary"),
    )(q, kv, do)


def _disc(logdt, a_re, a_im, b_re, b_im):
    dt = jnp.exp(logdt)
    mag = jnp.exp(a_re * dt)
    ab_re = mag * jnp.cos(a_im * dt)
    ab_im = mag * jnp.sin(a_im * dt)
    den = jnp.square(a_re) + jnp.square(a_im)
    nr = ab_re - 1.0
    f_re = (nr * a_re + ab_im * a_im) / den
    f_im = (ab_im * a_re - nr * a_im) / den
    bb_re = f_re[None] * b_re - f_im[None] * b_im
    bb_im = f_re[None] * b_im + f_im[None] * b_re
    return ab_re, ab_im, bb_re, bb_im


def _disc_transpose(logdt, a_re, a_im, b_re, b_im, g_ab_re, g_ab_im, g_bb_re, g_bb_im):
    dt = jnp.exp(logdt)
    mag = jnp.exp(a_re * dt)
    th = a_im * dt
    cs, sn = jnp.cos(th), jnp.sin(th)
    ab_re, ab_im = mag * cs, mag * sn
    den = jnp.square(a_re) + jnp.square(a_im)
    nr = ab_re - 1.0
    f_re = (nr * a_re + ab_im * a_im) / den
    f_im = (ab_im * a_re - nr * a_im) / den
    d_f_re = jnp.sum(g_bb_re * b_re + g_bb_im * b_im, axis=0)
    d_f_im = jnp.sum(g_bb_im * b_re - g_bb_re * b_im, axis=0)
    d_b_re = g_bb_re * f_re[None] + g_bb_im * f_im[None]
    d_b_im = g_bb_im * f_re[None] - g_bb_re * f_im[None]
    d_n_re, d_n_im = d_f_re / den, d_f_im / den
    d_den = -(d_f_re * f_re + d_f_im * f_im) / den
    d_ab_re = g_ab_re + d_n_re * a_re - d_n_im * a_im
    d_ab_im = g_ab_im + d_n_re * a_im + d_n_im * a_re
    d_a_re = d_n_re * nr + d_n_im * ab_im + 2.0 * d_den * a_re
    d_a_im = d_n_re * ab_im - d_n_im * nr + 2.0 * d_den * a_im
    d_mag = d_ab_re * cs + d_ab_im * sn
    d_th = mag * (d_ab_im * cs - d_ab_re * sn)
    d_a_re = d_a_re + d_mag * mag * dt
    d_a_im = d_a_im + d_th * dt
    d_dt = jnp.sum(d_mag * mag * a_re + d_th * a_im, axis=-1, keepdims=True)
    return d_dt * dt, d_a_re, d_a_im, d_b_re, d_b_im


def _whole(fn, args, out_shapes, *, name):
    n_in = len(args)

    def body(*refs):
        res = fn(*[r[...] for r in refs[:n_in]])
        for o_ref, val in zip(refs[n_in:], res):
            o_ref[...] = val

    return pl.pallas_call(body, name=name, out_shape=[jax.ShapeDtypeStruct(s, F32) for s in out_shapes],
                          compiler_params=pltpu.CompilerParams(vmem_limit_bytes=VMEM_LIMIT_BYTES))(*args)


SSM_WIDE =GROUPS_PER_TILE * SSM_STATE
LANE_GROUPS_PER_TILE = SSM_WIDE // LANES


def _chan(j):
    return slice(j * LANES, (j + 1) * LANES)


def _time_major_rows(j, q, tc):
    return pl.ds(j * LANE_GROUPS_PER_TILE + q, tc, stride=STATE_VREG_ROWS)


def _to_time_major(x, t_re_ref, t_im_ref, dst_re, dst_im, tc):
    for j in range(SSM_TILES):
        xj = x[:, _chan(j)]
        for t_ref, dst in ((t_re_ref, dst_re), (t_im_ref, dst_im)):
            r = _dot(xj, t_ref[j], 1, 0)
            for q in range(LANE_GROUPS_PER_TILE):
                dst[_time_major_rows(j, q, tc), :] = r[:, q * LANES:(q + 1) * LANES]


def _from_time_major(src, j, tc):
    return jnp.concatenate([src[_time_major_rows(j, q, tc), :] for q in range(LANE_GROUPS_PER_TILE)], axis=1)


def _scan_chunk(w_re, w_im, h_re, h_im, a_re, a_im, start, tc):
    def step(t, carry):
        hr, hi = carry
        rows = _scan_rows(t)
        nr = a_re * hr - a_im * hi + w_re[rows, :]
        ni = a_re * hi + a_im * hr + w_im[rows, :]
        h_re[rows, :] = nr
        h_im[rows, :] = ni
        return nr, ni

    return lax.fori_loop(0, tc, step, start, unroll=8)


SSM_CHUNK = 256


def _tile_spec(stack, k):
    return pl.BlockSpec((pl.Squeezed(),) + tuple(stack.shape[1:]), lambda i: (k, 0, 0, 0))


def _expand_block_diagonal(src_ref, dst):
    dst[...] = jnp.zeros_like(dst)
    r, c = src_ref.shape[1:]
    for g in range(SSM_GROUPS):
        j, gl = divmod(g, GROUPS_PER_TILE)
        dst[j, gl * r:(gl + 1) * r, gl * c:(gl + 1) * c] = src_ref[g].astype(dst.dtype)


def _extract_block_diagonal(src, dst_ref):
    r, c = dst_ref.shape[1:]
    for g in range(SSM_GROUPS):
        j, gl = divmod(g, GROUPS_PER_TILE)
        dst_ref[g] = src[j, gl * r:(gl + 1) * r, gl * c:(gl + 1) * c]


def _ssm_fwd(proj, blocks_cn, blocks_nc, a_re, a_im, gain, *, name, tc=SSM_CHUNK):
    n_rows = proj.shape[0]
    n_chunk = n_rows // tc

    def body(u_ref, br_ref, bi_ref, cr_ref, ci_ref, ar_ref, ai_ref, g_ref, y_ref, gy_ref, hr, hi, wr, wi, state,
             tbr_ref, tbi_ref, tcr_ref, tci_ref):
        @pl.when(pl.program_id(0) == 0)
        def _():
            state[...] = jnp.zeros_like(state)
            for src_ref, dst in ((br_ref, tbr_ref), (bi_ref, tbi_ref), (cr_ref, tcr_ref), (ci_ref, tci_ref)):
                _expand_block_diagonal(src_ref, dst)

        u = u_ref[...]
        _to_time_major(u, tbr_ref, tbi_ref, wr, wi, tc)
        state[0], state[1] = _scan_chunk(wr, wi, hr, hi, ar_ref[...], ai_ref[...], (state[0], state[1]), tc)
        for j in range(SSM_TILES):
            yj = (_dot(_from_time_major(hr, j, tc), tcr_ref[j], 1, 0) + _dot(_from_time_major(hi, j, tc), tci_ref[j], 1, 0)
                  + g_ref[:, _chan(j)] * u[:, _chan(j)])
            y_ref[:, _chan(j)] = yj
            gy_ref[:, _chan(j)] = jax.nn.gelu(yj).astype(gy_ref.dtype)

    rows = pl.BlockSpec((tc, SSM_WIDTH), lambda i: (i, 0))
    coef = pl.BlockSpec((STATE_VREG_ROWS, LANES), lambda i: (0, 0))
    states = pl.BlockSpec((tc * STATE_VREG_ROWS, LANES), lambda i: (i, 0))
    sshape = jax.ShapeDtypeStruct((n_rows * STATE_VREG_ROWS, LANES), F32)
    return pl.pallas_call(
        body, name=name, grid=(n_chunk,),
        in_specs=[rows, _tile_spec(blocks_cn, 0), _tile_spec(blocks_cn, 1), _tile_spec(blocks_nc, 0),
                  _tile_spec(blocks_nc, 1), coef, coef, pl.BlockSpec((1, SSM_WIDTH), lambda i: (0, 0))],
        out_specs=[rows, rows, states, states],
        out_shape=[jax.ShapeDtypeStruct((n_rows, SSM_WIDTH), F32), jax.ShapeDtypeStruct((n_rows, SSM_WIDTH), MXU_DTYPE),
                   sshape, sshape],
        scratch_shapes=[pltpu.VMEM((tc * STATE_VREG_ROWS, LANES), F32)] * 2 + [pltpu.VMEM((2, STATE_VREG_ROWS, LANES), F32)]
        + [pltpu.VMEM((SSM_TILES, LANES, SSM_WIDE), MXU_DTYPE)] * 2 + [pltpu.VMEM((SSM_TILES, SSM_WIDE, LANES), MXU_DTYPE)] * 2,
        compiler_params=_params("arbitrary"),
    )(proj, blocks_cn, blocks_cn, blocks_nc, blocks_nc, a_re, a_im, gain)


def _ssm_bwd(proj, dy, h_re, h_im, blocks_cn, blocks_nc, a_re, a_im, gain, *, name, tc=SSM_CHUNK):
    n_rows = proj.shape[0]
    n_chunk = n_rows // tc

    def body(u_ref, dy_ref, hr, hi, cr_ref, ci_ref, br_ref, bi_ref, ar_ref, ai_ref, g_ref,
             du_ref, su_ref, dc_re_ref, dc_im_ref, db_re_ref, db_im_ref, dar_ref, dai_ref, wr, wi, carry,
             tdr_ref, tdi_ref, tur_ref, tui_ref, dcr_ref, dci_ref, dbr_ref, dbi_ref):
        @pl.when(pl.program_id(0) == 0)
        def _():
            carry[...] = jnp.zeros_like(carry)
            for acc_ref in (su_ref, dcr_ref, dci_ref, dbr_ref, dbi_ref):
                acc_ref[...] = jnp.zeros_like(acc_ref)
            for src_ref, dst in ((cr_ref, tdr_ref), (ci_ref, tdi_ref), (br_ref, tur_ref), (bi_ref, tui_ref)):
                _expand_block_diagonal(src_ref, dst)

        a_r, a_i = ar_ref[...], ai_ref[...]
        u, dyv = u_ref[...], dy_ref[...]
        _to_time_major(dyv, tdr_ref, tdi_ref, wr, wi, tc)

        def step(kk, c):
            lam_r, lam_i, dar, dai = c
            rows = _scan_rows(tc - 1 - kk)
            h_r, h_i = hr[rows, :], hi[rows, :]
            dar = dar + lam_r * h_r + lam_i * h_i
            dai = dai + lam_i * h_r - lam_r * h_i
            new_r = wr[rows, :] + a_r * lam_r + a_i * lam_i
            new_i = wi[rows, :] + a_r * lam_i - a_i * lam_r
            wr[rows, :] = new_r
            wi[rows, :] = new_i
            return new_r, new_i, dar, dai

        carry[0], carry[1], carry[2], carry[3] = lax.fori_loop(0, tc, step, (carry[0], carry[1], carry[2], carry[3]),
                                                              unroll=8)
        dar_ref[...] = carry[2]
        dai_ref[...] = carry[3]
        for j in range(SSM_TILES):
            cj = _chan(j)
            lam_r, lam_i = _from_time_major(wr, j, tc), _from_time_major(wi, j, tc)
            dcr_ref[j] += _dot(dyv[:, cj], _from_time_major(hr, j, tc), 0, 0)
            dci_ref[j] += _dot(dyv[:, cj], _from_time_major(hi, j, tc), 0, 0)
            dbr_ref[j] += _dot(u[:, cj], lam_r, 0, 0)
            dbi_ref[j] += _dot(u[:, cj], lam_i, 0, 0)
            duj = _dot(lam_r, tur_ref[j], 1, 0) + _dot(lam_i, tui_ref[j], 1, 0) + g_ref[:, cj] * dyv[:, cj]
            du_ref[:, cj] = duj.astype(du_ref.dtype)
            su_ref[:, cj] += _colsum(duj)

        @pl.when(pl.program_id(0) == n_chunk - 1)
        def _():
            for src, dst_ref in ((dcr_ref, dc_re_ref), (dci_ref, dc_im_ref), (dbr_ref, db_re_ref), (dbi_ref, db_im_ref)):
                _extract_block_diagonal(src, dst_ref)

    back = lambda i: (n_chunk - 1 - i, 0)
    rows = pl.BlockSpec((tc, SSM_WIDTH), back)
    blocks = pl.BlockSpec((SSM_GROUPS, SSM_GROUP, SSM_STATE), lambda i: (0, 0, 0))
    coef = pl.BlockSpec((STATE_VREG_ROWS, LANES), lambda i: (0, 0))
    states = pl.BlockSpec((tc * STATE_VREG_ROWS, LANES), back)
    vec = pl.BlockSpec((1, SSM_WIDTH), lambda i: (0, 0))
    bshape = jax.ShapeDtypeStruct((SSM_GROUPS, SSM_GROUP, SSM_STATE), F32)
    cshape = jax.ShapeDtypeStruct((STATE_VREG_ROWS, LANES), F32)
    return pl.pallas_call(
        body, name=name, grid=(n_chunk,),
        in_specs=[rows, rows, states, states, _tile_spec(blocks_cn, 2), _tile_spec(blocks_cn, 3), _tile_spec(blocks_nc, 2),
                  _tile_spec(blocks_nc, 3), coef, coef, vec],
        out_specs=[rows, vec, blocks, blocks, blocks, blocks, coef, coef],
        out_shape=[jax.ShapeDtypeStruct((n_rows, SSM_WIDTH), MXU_DTYPE), jax.ShapeDtypeStruct((1, SSM_WIDTH), F32),
                   bshape, bshape, bshape, bshape, cshape, cshape],
        scratch_shapes=[pltpu.VMEM((tc * STATE_VREG_ROWS, LANES), F32)] * 2 + [pltpu.VMEM((4, STATE_VREG_ROWS, LANES), F32)]
        + [pltpu.VMEM((SSM_TILES, LANES, SSM_WIDE), MXU_DTYPE)] * 2 + [pltpu.VMEM((SSM_TILES, SSM_WIDE, LANES), MXU_DTYPE)] * 2
        + [pltpu.VMEM((SSM_TILES, LANES, SSM_WIDE), F32)] * 4,
        compiler_params=_params("arbitrary"),
    )(proj, dy, h_re, h_im, blocks_cn, blocks_cn, blocks_nc, blocks_nc, a_re, a_im, gain)


def _scan_rows(t):
    return pl.ds(pl.multiple_of(t * STATE_VREG_ROWS, 8), STATE_VREG_ROWS)


GATHER_GROUPS = (("w_in",), ("w_glu", "w_att_up", "w_mix_out"), ("w_xq", "w_xkv", "w_xo", "w_ff1", "w_ff2"))
SCATTER_GROUPS = (("w_ff2", "w_ff1"), ("w_xo", "w_xq", "w_xkv", "w_mix_out"), ("w_att_up", "w_glu"), ("w_in",))


def _local_grads(x, mem, pos_col, target, sm, fetch, send, send_small, start_token):
    b_re_t = sm["ssm_b_re"].transpose(2, 0, 1)
    b_im_t = sm["ssm_b_im"].transpose(2, 0, 1)
    logdt = sm["ssm_log_dt"].reshape(SSM_GROUPS, 1)
    c_re, c_im = sm["ssm_c_re"], sm["ssm_c_im"]
    grp = (SSM_GROUPS, SSM_STATE)
    chn = (SSM_GROUP, SSM_GROUPS, SSM_STATE)

    wts = {}
    cos_t, sin_t = _rope_tables(pos_col, after=start_token, name="rope_tables")
    h0, xh0, rs0, h0m = _ln_fwd(x, None, sm["ln_in_g"], sm["ln_in_b"], alpha=1.0, name="ln_in_fwd")
    disc_in = (logdt, sm["ssm_a_re"], sm["ssm_a_im"], b_re_t, b_im_t)
    ab_re, ab_im, bb_re_t, bb_im_t = _whole(_disc, disc_in, [grp, grp, chn, chn], name="ssm_disc")
    a_re_rows, a_im_rows = ab_re.reshape(STATE_VREG_ROWS, LANES), ab_im.reshape(STATE_VREG_ROWS, LANES)
    tiles_cn = jnp.stack([bb_re_t.transpose(1, 0, 2), bb_im_t.transpose(1, 0, 2), c_re, -c_im])
    tiles_nc = jnp.stack([c_re.transpose(0, 2, 1), -c_im.transpose(0, 2, 1), bb_re_t.transpose(1, 2, 0),
                          bb_im_t.transpose(1, 2, 0)])
    wts.update(fetch(0, [h0m, tiles_cn, tiles_nc]))
    proj = _mm(h0m, wts["w_in"], bias=sm["b_in"], b_shards=True, name="in_proj")

    y, gy, h_re, h_im = _ssm_fwd(proj, tiles_cn, tiles_nc, a_re_rows, a_im_rows, sm["ssm_d"], name="ssm_fwd")

    q, k, v = _qkv_split(proj, cos_t, sin_t, name="qkv_split")
    outs, lses = [], []
    for g, dil in enumerate(DILATIONS):
        o_g, l_g = _dil_fwd(q[g], k[g], v[g], dil, name=f"dil_att_fwd_{dil}")
        outs.append(o_g)
        lses.append(l_g)
    att, lse = _att_merge(outs, lses, name="att_merge")
    wts.update(fetch(1, [att]))
    z = _mm(gy, wts["w_glu"], bias=sm["b_glu"], b_shards=True, name="glu_proj")
    b_att = _mm(att, wts["w_att_up"], b_shards=True, name="att_up")

    mixed = _mix_fwd(proj, z, b_att, name="gate_mix")
    mix_out = _mm(mixed, wts["w_mix_out"], bias=sm["b_mix_out"], name="mix_out")
    h1, xh1, rs1, h1m = _ln_fwd(h0, mix_out, sm["ln1_g"], sm["ln1_b"], alpha=DEEPNORM_ALPHA, name="ln1_fwd")

    wts.update(fetch(2, [h1m]))
    xq = _mm(h1m, wts["w_xq"], out_dtype=MXU_DTYPE, name="xatt_q")
    kv = _mm(mem, wts["w_xkv"], out_dtype=MXU_DTYPE, b_shards=True, name="xatt_kv")
    xo_in = _xatt_fwd(xq, kv, name="xatt_fwd")
    xo = _mm(xo_in, wts["w_xo"], name="xatt_o")
    h2, xh2, rs2, h2m = _ln_fwd(h1, xo, sm["ln2_g"], sm["ln2_b"], alpha=DEEPNORM_ALPHA, name="ln2_fwd")

    pre, act = _mm(h2m, wts["w_ff1"], bias=sm["b_ff1"], b_shards=True, name="ff1",
                   also=(lambda r: jnp.square(jnp.maximum(r, 0.0)), MXU_DTYPE))
    ff = _mm(act, wts["w_ff2"], bias=sm["b_ff2"], name="ff2")

    gw, gs = {}, {}
    dr3, dr3m, gs["ln3_g"], gs["ln3_b"], gs["b_ff2"], loss_row = _ln_loss_bwd(
        h2, ff, target, sm["ln3_g"], sm["ln3_b"], alpha=DEEPNORM_ALPHA, name="ln3_loss")
    wgrad = functools.partial(_mm, ta=True, out_dtype=WIRE_DTYPE, tk=2048)
    gw["w_ff2"] = wgrad(act, dr3m, tk=1024, name="ff2_dw")
    dpre, gs["b_ff1"] = _mm(dr3m, wts["w_ff2"], tb=True, out_dtype=MXU_DTYPE, colsum=True, name="ff2_dx",
                            gate=(pre, lambda p: 2.0 * jnp.maximum(p, 0.0)))
    gw["w_ff1"] = wgrad(h2m, dpre, out_shards=True, name="ff1_dw")
    sent = send(0, gw)
    dh2 = _mm(dpre, wts["w_ff1"], tb=True, b_shards=True, after=sent, name="ff1_dx")

    dr2, dr2m, gs["ln2_g"], gs["ln2_b"], _ = _ln_bwd(dr3, dh2, xh2, rs2, sm["ln2_g"], alpha=DEEPNORM_ALPHA,
                                                     name="ln2_bwd")
    gw["w_xo"] = wgrad(xo_in, dr2m, name="xatt_o_dw")
    dxo_in = _mm(dr2m, wts["w_xo"], tb=True, out_dtype=MXU_DTYPE, name="xatt_o_dx")
    dxq, dkv = _xatt_bwd(xq, kv, dxo_in, name="xatt_bwd")
    gw["w_xq"] = wgrad(h1m, dxq, name="xatt_q_dw")
    gw["w_xkv"] = wgrad(mem, dkv, out_shards=True, name="xatt_kv_dw")
    dh1 = _mm(dxq, wts["w_xq"], tb=True, name="xatt_q_dx")

    dr1, dr1m, gs["ln1_g"], gs["ln1_b"], gs["b_mix_out"] = _ln_bwd(dr2, dh1, xh1, rs1, sm["ln1_g"],
                                                                   alpha=DEEPNORM_ALPHA, name="ln1_bwd")
    gw["w_mix_out"] = wgrad(mixed, dr1m, name="mix_out_dw")
    sent = send(1, gw)
    dmixed = _mm(dr1m, wts["w_mix_out"], tb=True, after=sent, name="mix_out_dx")
    dgs, dga, dz, db_att, s_gs, s_ga, gs["b_glu"] = _mix_bwd(dmixed, proj, z, b_att, name="gate_mix_bwd")

    gw["w_att_up"] = wgrad(att, db_att, out_shards=True, name="att_up_dw")
    gw["w_glu"] = wgrad(gy, dz, out_shards=True, name="glu_dw")
    sent = send(2, gw)
    datt = _mm(db_att, wts["w_att_up"], tb=True, b_shards=True, after=sent, name="att_up_dx")
    stats = _att_stats(datt, att, lse, name="att_stats")
    dqkv = [_dil_bwd(q[g], k[g], v[g], datt, stats, dil, name=f"dil_att_bwd_{dil}") for g, dil in enumerate(DILATIONS)]

    dgy = _mm(dz, wts["w_glu"], tb=True, b_shards=True, name="glu_dx")
    dy, gs["ssm_d"] = _gelu_bwd(dgy, y, proj, name="gelu_bwd")
    du, s_u, dc_re_t, dc_im_t, dbb_re_t, dbb_im_t, da_re, da_im = _ssm_bwd(
        proj, dy, h_re, h_im, tiles_cn, tiles_nc, a_re_rows, a_im_rows, sm["ssm_d"], name="ssm_bwd")
    gs["ssm_c_re"], gs["ssm_c_im"] = dc_re_t, -dc_im_t
    disc_ct = (da_re.reshape(grp), da_im.reshape(grp), dbb_re_t.transpose(1, 0, 2), dbb_im_t.transpose(1, 0, 2))
    d_logdt, gs["ssm_a_re"], gs["ssm_a_im"], d_b_re_t, d_b_im_t = _whole(
        _disc_transpose, disc_in + disc_ct, [(SSM_GROUPS, 1), grp, grp, chn, chn], name="ssm_disc_bwd")
    gs["ssm_log_dt"] = d_logdt
    gs["ssm_b_re"], gs["ssm_b_im"] = d_b_re_t.transpose(1, 2, 0), d_b_im_t.transpose(1, 2, 0)

    dproj, s_qkv = _dproj_assemble(du, dqkv, dgs, dga, cos_t, sin_t, name="dproj_assemble")
    gs["b_in"] = jnp.concatenate([s_u, *s_qkv, s_gs, s_ga], axis=1)
    sent = send_small(gs, SMALL_EARLY)
    gw["w_in"] = wgrad(h0m, dproj, out_shards=True, after=sent, name="in_proj_dw")
    sent = send(3, gw)
    dh0 = _mm(dproj, wts["w_in"], tb=True, b_shards=True, after=sent, name="in_proj_dx")
    grad_x, gs["ln_in_g"], gs["ln_in_b"], _ = _ln_bwd(dr1, dh0, xh0, rs0, sm["ln_in_g"], alpha=DEEPNORM_ALPHA,
                                                      operand=False, name="ln_in_bwd")
    return loss_row, grad_x, gs


N_PEER = N_DEV - 1
_IN_HBM = pl.BlockSpec(memory_space=pltpu.HBM)
_IN_SEMAPHORE = pl.BlockSpec(memory_space=pltpu.SEMAPHORE)


def _device_index():
    return 4 * lax.axis_index("x") + 2 * lax.axis_index("y") + lax.axis_index("c")


def _exchange_copies(src_refs, land_refs, send_sems, recv_sems, scatter):
    x, y, c = lax.axis_index("x"), lax.axis_index("y"), lax.axis_index("c")
    me = 4 * x + 2 * y + c
    pairs = []
    for a, (src_ref, land_ref) in enumerate(zip(src_refs, land_refs)):
        for kk in range(1, N_DEV):
            px = (x + (kk >> 2)) % 2
            py = (y + ((kk >> 1) & 1)) % 2
            pc = (c + (kk & 1)) % 2
            peer = 4 * px + 2 * py + pc
            sem = a * N_PEER + kk - 1
            src = src_ref.at[peer] if scatter else src_ref

            def copy(dst, src=src, sem=sem, px=px, py=py, pc=pc):
                return pltpu.make_async_remote_copy(
                    src_ref=src, dst_ref=dst, send_sem=send_sems.at[sem], recv_sem=recv_sems.at[sem],
                    device_id=(px, py, pc), device_id_type=pl.DeviceIdType.MESH)

            pairs.append((functools.partial(copy, land_ref.at[me]), functools.partial(copy, land_ref.at[peer])))
    return pairs


def _own_copies(src_refs, land_refs, own_sems, scatter):
    me = _device_index()
    return [functools.partial(pltpu.make_async_copy, src_ref.at[me] if scatter else src_ref, land_ref.at[me],
                              own_sems.at[a]) for a, (src_ref, land_ref) in enumerate(zip(src_refs, land_refs))]


def _exchange_start(srcs, *, scatter, name, after=None):
    n_arr = len(srcs)
    lands = [lax.empty((N_DEV,) + tuple(s.shape[1:] if scatter else s.shape), s.dtype) for s in srcs]
    n_in = 2 * n_arr + (after is not None)

    def body(*refs):
        send_sems, recv_sems, own_sems = refs[n_in], refs[n_in + 1], refs[n_in + 2]
        for sent, _ in _exchange_copies(refs[:n_arr], refs[n_arr:2 * n_arr], send_sems, recv_sems, scatter):
            sent().start()
        for own in _own_copies(refs[:n_arr], refs[n_arr:2 * n_arr], own_sems, scatter):
            own().start()
        refs[-1][...] = jnp.zeros_like(refs[-1])

    through = [pltpu.HBM(t.shape, t.dtype) for t in (*srcs, *lands)]
    res = pl.pallas_call(
        body, name=name,
        out_shape=(pltpu.SemaphoreType.DMA((n_arr * N_PEER,)), pltpu.SemaphoreType.DMA((n_arr * N_PEER,)),
                   pltpu.SemaphoreType.DMA((n_arr,)), *through, jax.ShapeDtypeStruct((8, LANES), F32)),
        in_specs=[_IN_HBM] * (2 * n_arr) + [pl.BlockSpec(memory_space=pl.ANY)] * (after is not None),
        out_specs=(_IN_SEMAPHORE, _IN_SEMAPHORE, _IN_SEMAPHORE, *[_IN_HBM] * (2 * n_arr),
                   pl.BlockSpec(memory_space=pltpu.VMEM)),
        input_output_aliases={i: 3 + i for i in range(2 * n_arr)},
        compiler_params=pltpu.CompilerParams(has_side_effects=pltpu.SideEffectType.DATAFLOW_SIDE_EFFECTING),
    )(*[pltpu.with_memory_space_constraint(t, pltpu.HBM) for t in (*srcs, *lands)],
      *([after] if after is not None else []))
    return (res[0], res[1], res[2], res[3:3 + n_arr], res[3 + n_arr:3 + 2 * n_arr], scatter), res[-1]


def _exchange_wait(handle, *, after, name):
    send_sems, recv_sems, own_sems, srcs, lands, scatter = handle
    n_arr = len(srcs)
    after = list(after)

    def body(*refs):
        src_refs, land_refs = refs[:n_arr], refs[n_arr:2 * n_arr]
        for sent, received in _exchange_copies(src_refs, land_refs, refs[2 * n_arr], refs[2 * n_arr + 1], scatter):
            sent().wait_send()
            received().wait_recv()
        for own in _own_copies(src_refs, land_refs, refs[2 * n_arr + 2], scatter):
            own().wait()

    res = pl.pallas_call(
        body, name=name, out_shape=tuple(pltpu.HBM(t.shape, t.dtype) for t in (*srcs, *lands)),
        in_specs=[_IN_HBM] * (2 * n_arr) + [_IN_SEMAPHORE] * 3 + [pl.BlockSpec(memory_space=pl.ANY)] * len(after),
        out_specs=tuple([_IN_HBM] * (2 * n_arr)), input_output_aliases={i: i for i in range(2 * n_arr)},
        compiler_params=pltpu.CompilerParams(has_side_effects=pltpu.SideEffectType.DATAFLOW_SIDE_EFFECTING),
    )(*srcs, *lands, send_sems, recv_sems, own_sems, *after)
    return res[n_arr:]


def _adamw(g, w, m, v):
    m_new = ADAM_B1 * m + (1.0 - ADAM_B1) * g
    v_new = ADAM_B2 * v + (1.0 - ADAM_B2) * jnp.square(g)
    m_hat = m_new / (1.0 - ADAM_B1 ** ADAM_STEP)
    v_hat = v_new / (1.0 - ADAM_B2 ** ADAM_STEP)
    return g, -ADAM_LR * (m_hat / (jnp.sqrt(v_hat) + ADAM_EPS) + ADAM_WD * w), m_new, v_new


def _reduce_adamw(gstack, w, m, v, *, name, tr=128):
    n_rows, cols = w.shape
    tr = min(tr, n_rows)
    assert n_rows % tr == 0, (name, n_rows, tr)

    def body(g_ref, w_ref, m_ref, v_ref, *out_refs):
        g = g_ref[0].astype(F32)
        for dev in range(1, N_DEV):
            g = g + g_ref[dev].astype(F32)
        for o_ref, val in zip(out_refs, _adamw(g, w_ref[...], m_ref[...], v_ref[...])):
            o_ref[...] = val

    flat = pl.BlockSpec((tr, cols), lambda i: (i, 0))
    shape = jax.ShapeDtypeStruct((n_rows, cols), F32)
    return pl.pallas_call(
        body, name=name, grid=(n_rows // tr,),
        in_specs=[pl.BlockSpec((N_DEV, tr, cols), lambda i: (0, i, 0)), flat, flat, flat],
        out_specs=[flat] * 4, out_shape=[shape] * 4, compiler_params=_params("parallel"),
    )(gstack, w, m, v)


SMALL_FLAT_SSM = ("ssm_b_re", "ssm_b_im", "ssm_c_re", "ssm_c_im")


def _small_view(name, shape):
    size = int(np.prod(shape))
    if name in SMALL_FLAT_SSM:
        return SSM_GROUPS, size // SSM_GROUPS
    if name in ("ssm_a_re", "ssm_a_im"):
        return SSM_GROUPS, SSM_STATE
    return 1, size


def _pack_rows(view):
    return -(-(view[0] * view[1]) // PACK_COLS)


SMALL_LATE = ("ln_in_g", "ln_in_b")
SMALL_EARLY = tuple(n for n in SMALL if n not in SMALL_LATE)


def _pack_small(gs, names, views):
    parts = []
    for n in names:
        flat = gs[n].reshape(-1).astype(WIRE_DTYPE)
        parts.append(jnp.pad(flat, (0, _pack_rows(views[n]) * PACK_COLS - flat.shape[0])))
    total = sum(p.shape[0] for p in parts) // PACK_COLS
    parts.append(jnp.zeros(((-total % PACK_ROW_ALIGN) * PACK_COLS,), WIRE_DTYPE))
    return jnp.concatenate(parts).reshape(-1, PACK_COLS)


def _small_pieces(view):
    rows, cols = view
    if cols == PACK_COLS:
        return [(0, rows, 0, 0, 0, cols)]
    if rows == 1 and cols > PACK_COLS:
        return [(kk, 1, 0, 0, kk * PACK_COLS, PACK_COLS) for kk in range(cols // PACK_COLS)]
    if rows == 1:
        return [(0, 1, 0, 0, 0, cols)]
    return [((r * cols) // PACK_COLS, 1, (r * cols) % PACK_COLS, r, 0, cols) for r in range(rows)]


def _adamw_small(stacks, views, w, m, v, *, name):
    n = len(SMALL)
    place, first = {}, [0, 0]
    for k, names in enumerate((SMALL_EARLY, SMALL_LATE)):
        for name_ in names:
            place[name_] = (k, first[k])
            first[k] += _pack_rows(views[name_])

    def body(early_ref, late_ref, *refs):
        ins, outs = refs[:3 * n], refs[3 * n:]
        for i, name_ in enumerate(SMALL):
            stack_ref = (early_ref, late_ref)[place[name_][0]]
            row0 = place[name_][1]
            for prow, nrows, lane, orow, ocol, width in _small_pieces(views[name_]):
                src = (slice(row0 + prow, row0 + prow + nrows), slice(lane, lane + width))
                dst = (slice(orow, orow + nrows), slice(ocol, ocol + width))
                g = stack_ref[(0,) + src].astype(F32)
                for dev in range(1, N_DEV):
                    g = g + stack_ref[(dev,) + src].astype(F32)
                res = _adamw(g, ins[i][dst], ins[n + i][dst], ins[2 * n + i][dst])
                for kk, val in enumerate(res):
                    outs[kk * n + i][dst] = val

    res = pl.pallas_call(
        body, name=name, out_shape=[jax.ShapeDtypeStruct(views[name_], F32) for _ in range(4) for name_ in SMALL],
        compiler_params=pltpu.CompilerParams(vmem_limit_bytes=VMEM_LIMIT_BYTES),
    )(*stacks, *[d[name_] for d in (w, m, v) for name_ in SMALL])
    return [dict(zip(SMALL, res[kk * n:(kk + 1) * n])) for kk in range(4)]


def kernel(x, mem, positions, ln_in_g, ln_in_b, w_in, b_in, ssm_log_dt, ssm_a_re, ssm_a_im, ssm_b_re, ssm_b_im, ssm_c_re, ssm_c_im, ssm_d, w_glu, b_glu, w_att_up, w_mix_out, b_mix_out, ln1_g, ln1_b, w_xq, w_xkv, w_xo, ln2_g, ln2_b, w_ff1, b_ff1, w_ff2, b_ff2, ln3_g, ln3_b, loss_target, m_ln_in_g, m_ln_in_b, m_w_in, m_b_in, m_ssm_log_dt, m_ssm_a_re, m_ssm_a_im, m_ssm_b_re, m_ssm_b_im, m_ssm_c_re, m_ssm_c_im, m_ssm_d, m_w_glu, m_b_glu, m_w_att_up, m_w_mix_out, m_b_mix_out, m_ln1_g, m_ln1_b, m_w_xq, m_w_xkv, m_w_xo, m_ln2_g, m_ln2_b, m_w_ff1, m_b_ff1, m_w_ff2, m_b_ff2, m_ln3_g, m_ln3_b, v_ln_in_g, v_ln_in_b, v_w_in, v_b_in, v_ssm_log_dt, v_ssm_a_re, v_ssm_a_im, v_ssm_b_re, v_ssm_b_im, v_ssm_c_re, v_ssm_c_im, v_ssm_d, v_w_glu, v_b_glu, v_w_att_up, v_w_mix_out, v_b_mix_out, v_ln1_g, v_ln1_b, v_w_xq, v_w_xkv, v_w_xo, v_ln2_g, v_ln2_b, v_w_ff1, v_b_ff1, v_w_ff2, v_b_ff2, v_ln3_g, v_ln3_b):
    given = dict(locals())
    w_arg = {n: given[n] for n in WEIGHTS}
    m_arg = {n: given["m_" + n] for n in WEIGHTS}
    v_arg = {n: given["v_" + n] for n in WEIGHTS}

    shards = {n: w_arg[n][0].astype(MXU_DTYPE) for n in BIG}
    gathers, token = [], None
    for i, names in enumerate(GATHER_GROUPS):
        handle, token = _exchange_start([shards[n] for n in names], scatter=False, after=token, name=f"gather_start_{i}")
        gathers.append(handle)

    small_views = {n: _small_view(n, w_arg[n].shape) for n in SMALL}
    small_w, small_m, small_v = [{n: d[n].reshape(small_views[n]) for n in SMALL} for d in (w_arg, m_arg, v_arg)]
    relaid = [d[n] for d in (small_w, small_m, small_v) for n in SMALL_FLAT_SSM]

    def fetch(i, after):
        lands = _exchange_wait(gathers[i], after=after + (relaid if i == 0 else []), name=f"gather_wait_{i}")
        full = dict(zip(GATHER_GROUPS[i], lands))
        return {n: t if n in BIG_COL_SHARDED else t.reshape(-1, t.shape[-1]) for n, t in full.items()}

    scatters = {}

    def send(i, gw):
        slots = [gw[n] if n in BIG_COL_SHARDED else gw[n].reshape(N_DEV, -1, gw[n].shape[-1]) for n in SCATTER_GROUPS[i]]
        handle, sent = _exchange_start(slots, scatter=True, name=f"scatter_start_{i}")
        scatters[i] = (handle, slots)
        return sent

    sm = {}
    for n in SMALL:
        t = w_arg[n]
        if n.startswith("ssm_") and n not in ("ssm_d", "ssm_log_dt"):
            sm[n] = t[0]
        else:
            sm[n] = t.reshape(1, -1)

    smalls = []

    def send_small(gs, names):
        handle, sent = _exchange_start([_pack_small(gs, names, small_views)], scatter=False,
                                       name=f"small_start_{len(smalls)}")
        smalls.append(handle)
        return sent

    loss_row, grad_x, gs = _local_grads(x[0], mem[0], positions.reshape(-1, 1), loss_target[0], sm, fetch, send,
                                        send_small, token)
    loss = lax.psum(loss_row[0, 0], ("x", "y", "c"))
    send_small(gs, SMALL_LATE)

    results = [{}, {}, {}, {}]
    done = grad_x
    for i, names in enumerate(SCATTER_GROUPS):
        handle, slots = scatters[i]
        lands = _exchange_wait(handle, after=[done], name=f"scatter_wait_{i}")
        for n, land, slot in zip(names, lands, slots):
            res = _reduce_adamw(land, w_arg[n][0], m_arg[n][0], v_arg[n][0], name="adamw_" + n)
            done = res[0]
            for d, r in zip(results, res):
                d[n] = r[None]
    stacks = [_exchange_wait(handle, after=[done], name=f"small_wait_{i}")[0] for i, handle in enumerate(smalls)]
    res = _adamw_small(stacks, small_views, small_w, small_m, small_v, name="adamw_small")
    for d, r in zip(results, res):
        d.update({n: r[n].reshape(w_arg[n].shape) for n in SMALL})
    out = [loss, grad_x[None]]
    for d in results:
        out += [d[n] for n in WEIGHTS]
    return tuple(out)
```

```python
import functools

import numpy as np
import jax
import jax.numpy as jnp
from jax import lax
from jax.experimental import pallas as pl
from jax.experimental.pallas import tpu as pltpu

F32 = jnp.float32
MXU_DTYPE = jnp.bfloat16
WIRE_DTYPE = jnp.bfloat16
VMEM_LIMIT_BYTES = 48 * 1024 * 1024
LANES = 128

N_DEV = 8
D_MODEL = 1024
SSM_GROUP = 16
SSM_WIDTH = 768
SSM_GROUPS = SSM_WIDTH // SSM_GROUP
SSM_STATE = 64
SSM_CH = SSM_GROUPS * SSM_STATE
SSM_TILES = SSM_WIDTH // LANES
GROUPS_PER_TILE = LANES // SSM_GROUP
STATE_VREG_ROWS = SSM_CH // LANES
ATT_HEAD_DIM = 64
ATT_HEADS_PER_GROUP = 4
ATT_MERGED = ATT_HEADS_PER_GROUP * ATT_HEAD_DIM
LANE_HALVES = ATT_MERGED // LANES
DILATIONS = (1, 4, 16)
ATT_BLK = 128
ATT_SCALE = ATT_HEAD_DIM ** -0.5
ROT_DIM = ATT_HEAD_DIM // 4
ROPE_THETA = 500000.0
XATT_HEADS = 4
XATT_HEAD_DIM = D_MODEL // XATT_HEADS
XATT_SCALE = XATT_HEAD_DIM ** -0.5
DEEPNORM_ALPHA = 2.0 ** 0.25
LN_EPS = 1e-5
NEG_INF = -1e30
OFF_Q_BLK, OFF_K_BLK, OFF_V_BLK = 3, 6, 9
OFF_GS_BLK, OFF_GA_BLK = 3, 4

ADAM_LR = 0.001
ADAM_B1 = 0.9
ADAM_B2 = 0.999
ADAM_EPS = 1e-08
ADAM_WD = 0.01
ADAM_STEP = 10

BIG = ("w_in", "w_glu", "w_att_up", "w_mix_out", "w_xq", "w_xkv", "w_xo", "w_ff1", "w_ff2")
BIG_COL_SHARDED = ("w_in", "w_glu", "w_att_up", "w_xkv", "w_ff1")
WEIGHTS = ("ln_in_g", "ln_in_b", "w_in", "b_in", "ssm_log_dt", "ssm_a_re", "ssm_a_im", "ssm_b_re", "ssm_b_im",
           "ssm_c_re", "ssm_c_im", "ssm_d", "w_glu", "b_glu", "w_att_up", "w_mix_out", "b_mix_out", "ln1_g", "ln1_b",
           "w_xq", "w_xkv", "w_xo", "ln2_g", "ln2_b", "w_ff1", "b_ff1", "w_ff2", "b_ff2", "ln3_g", "ln3_b")
SMALL = tuple(n for n in WEIGHTS if n not in BIG)
PACK_COLS = 1024
PACK_ROW_ALIGN = 16


def _params(*sem):
    return pltpu.CompilerParams(dimension_semantics=sem, vmem_limit_bytes=VMEM_LIMIT_BYTES)


def _dot(a, b, ca, cb):
    return lax.dot_general(a.astype(MXU_DTYPE), b.astype(MXU_DTYPE), (((ca,), (cb,)), ((), ())),
                           preferred_element_type=F32)


def _fit(dim, pref):
    if dim <= pref:
        return dim
    best = max(t for t in range(LANES, pref + 1, LANES) if dim % t == 0)
    return best


def _mm(a, b, *, name, ta=False, tb=False, bias=None, out_dtype=F32, b_shards=False, out_shards=False, after=None,
        also=None, gate=None, colsum=False, tm=2048, tn=1024, tk=1024):
    m, k = (a.shape[1], a.shape[0]) if ta else a.shape
    order = (lambda f: (lambda j, i, kk: f(i, j, kk))) if colsum else (lambda f: f)
    spec = lambda shape, f: pl.BlockSpec(shape, order(f))
    if b_shards:
        n_sh, rows, n_loc = b.shape
        if tb:
            n, tn, tk = rows, _fit(rows, tn), n_loc
            assert k == n_sh * n_loc, (name, k, b.shape)
            b_spec = spec((1, tn, tk), lambda i, j, kk: (kk, j, 0))
        else:
            n, tn, tk = n_sh * n_loc, n_loc, _fit(k, tk)
            b_spec = spec((1, tk, tn), lambda i, j, kk: (j, kk, 0))
    else:
        n = b.shape[0] if tb else b.shape[1]
        tn = n // N_DEV if out_shards else _fit(n, tn)
        tk = _fit(k, tk)
        b_spec = spec((tn, tk), lambda i, j, kk: (j, kk)) if tb else spec((tk, tn), lambda i, j, kk: (kk, j))
    tm = _fit(m, tm)
    nk = k // tk
    a_spec = spec((tk, tm), lambda i, j, kk: (kk, i)) if ta else spec((tm, tk), lambda i, j, kk: (i, kk))
    tile = spec((tm, tn), lambda i, j, kk: (i, j))
    in_specs, args = [a_spec, b_spec], [a, b]
    if bias is not None:
        in_specs.append(spec((1, tn), lambda i, j, kk: (0, j)))
        args.append(bias)
    if gate is not None:
        in_specs.append(tile)
        args.append(gate[0])
    if after is not None:
        in_specs.append(pl.BlockSpec(memory_space=pl.ANY))
        args.append(after)
    n_in = len(args)
    if out_shards:
        assert n == N_DEV * tn, (name, n, tn)
        out_specs = [spec((1, tm, tn), lambda i, j, kk: (j, i, 0))]
        out_shape = [jax.ShapeDtypeStruct((N_DEV, m, tn), out_dtype)]
    else:
        out_specs = [tile]
        out_shape = [jax.ShapeDtypeStruct((m, n), out_dtype)]
    if also is not None:
        out_specs.append(tile)
        out_shape.append(jax.ShapeDtypeStruct((m, n), also[1]))
    if colsum:
        out_specs.append(spec((1, tn), lambda i, j, kk: (0, j)))
        out_shape.append(jax.ShapeDtypeStruct((1, n), F32))

    def body(*refs):
        a_ref, b_ref = refs[0], refs[1]
        o_ref = refs[n_in]

        def product():
            return _dot(a_ref[...], b_ref[0] if b_shards else b_ref[...], 0 if ta else 1, 1 if tb else 0)

        def finish(r):
            if bias is not None:
                r = r + refs[2][...]
            if gate is not None:
                r = r * gate[1](refs[2 + (bias is not None)][...])
            if out_shards:
                o_ref[0] = r.astype(o_ref.dtype)
            else:
                o_ref[...] = r.astype(o_ref.dtype)
            if also is not None:
                refs[n_in + 1][...] = also[0](r).astype(also[1])
            if colsum:
                s_ref = refs[n_in + 1 + (also is not None)]

                @pl.when(pl.program_id(1) == 0)
                def _():
                    s_ref[...] = jnp.zeros_like(s_ref)

                s_ref[...] += _colsum(r)

        if nk == 1:
            finish(product())
            return
        acc_ref = refs[-1]
        kk = pl.program_id(2)

        @pl.when(kk == 0)
        def _():
            acc_ref[...] = jnp.zeros_like(acc_ref)

        acc_ref[...] += product()

        @pl.when(kk == nk - 1)
        def _():
            finish(acc_ref[...])

    grid = (n // tn, m // tm, nk) if colsum else (m // tm, n // tn, nk)
    res = pl.pallas_call(
        body, name=name, grid=grid, in_specs=in_specs, out_specs=out_specs, out_shape=out_shape,
        scratch_shapes=[pltpu.VMEM((tm, tn), F32)] if nk > 1 else [],
        compiler_params=_params("parallel", "arbitrary" if colsum else "parallel", "arbitrary"),
    )(*args)
    return res[0] if len(res) == 1 else res


def _mm_shards(a, w, bias, shard_ids, *, name, prev=None, tm=2048):
    m, k = a.shape
    n_sh, _, n_loc = w.shape
    tm = _fit(m, tm)

    def body(ids_ref, a_ref, w_ref, b_ref, *rest):
        rest[-1][...] = _dot(a_ref[...], w_ref[0], 1, 0) + b_ref[...]

    grid_spec = pltpu.PrefetchScalarGridSpec(
        num_scalar_prefetch=1, grid=(m // tm, shard_ids.shape[0]),
        in_specs=[pl.BlockSpec((tm, k), lambda i, j, ids: (i, 0)),
                  pl.BlockSpec((1, k, n_loc), lambda i, j, ids: (ids[j], 0, 0)),
                  pl.BlockSpec((1, n_loc), lambda i, j, ids: (0, ids[j]))]
        + [pl.BlockSpec(memory_space=pl.ANY)] * (prev is not None),
        out_specs=pl.BlockSpec((tm, n_loc), lambda i, j, ids: (i, ids[j])))
    return pl.pallas_call(
        body, name=name, grid_spec=grid_spec, out_shape=jax.ShapeDtypeStruct((m, n_sh * n_loc), F32),
        input_output_aliases={4: 0} if prev is not None else {}, compiler_params=_params("parallel", "arbitrary"),
    )(shard_ids, a, w, bias, *([prev] if prev is not None else []))


def _rowcall(fn, rows, fulls, row_outs, acc_outs=(), *, n_rows, tm, name, after=None):
    n_r, n_f, n_o, n_a = len(rows), len(fulls), len(row_outs), len(acc_outs)
    n_in = n_r + n_f + (after is not None)
    assert n_rows % tm == 0, (name, n_rows, tm)

    def body(*refs):
        res = fn(*[r[...] for r in refs[:n_r + n_f]])
        res = tuple(res) if isinstance(res, (tuple, list)) else (res,)
        o_refs = refs[n_in:n_in + n_o]
        a_refs = refs[n_in + n_o:]
        for o_ref, val in zip(o_refs, res[:n_o]):
            o_ref[...] = val.astype(o_ref.dtype)
        if n_a:
            @pl.when(pl.program_id(0) == 0)
            def _():
                for a_ref in a_refs:
                    a_ref[...] = jnp.zeros_like(a_ref)

            for a_ref, val in zip(a_refs, res[n_o:]):
                a_ref[...] += val

    in_specs = [pl.BlockSpec((tm, w), functools.partial(lambda i, cb: (i, cb), cb=cb)) for _, w, cb in rows]
    in_specs += [pl.BlockSpec(f.shape, functools.partial(lambda i, nd: (0,) * nd, nd=f.ndim)) for f in fulls]
    in_specs += [pl.BlockSpec(memory_space=pl.ANY)] * (after is not None)
    out_specs = [pl.BlockSpec((tm, w), lambda i: (i, 0)) for w, _ in row_outs]
    out_specs += [pl.BlockSpec((1, w), lambda i: (0, 0)) for w in acc_outs]
    out_shape = [jax.ShapeDtypeStruct((n_rows, w), dt) for w, dt in row_outs]
    out_shape += [jax.ShapeDtypeStruct((1, w), F32) for w in acc_outs]
    return pl.pallas_call(
        body, name=name, grid=(n_rows // tm,), in_specs=in_specs, out_specs=out_specs, out_shape=out_shape,
        compiler_params=_params("arbitrary" if n_a else "parallel"),
    )(*[r[0] for r in rows], *fulls, *([after] if after is not None else []))


def _colsum(v):
    return jnp.sum(v, axis=0, keepdims=True)


def _ln_fwd(a, r, g, b, *, alpha, name):
    n_rows, d = a.shape

    def fn(*t):
        xin = t[0] if alpha == 1.0 else alpha * t[0]
        if r is not None:
            xin = xin + t[1]
        gv, bv = t[-2], t[-1]
        mu = jnp.mean(xin, axis=-1, keepdims=True)
        xc = xin - mu
        var = jnp.mean(xc * xc, axis=-1, keepdims=True)
        rstd = lax.rsqrt(var + LN_EPS)
        xh = xc * rstd
        y = xh * gv + bv
        return y, xh, rstd, y

    rows = [(a, d, 0)] + ([(r, d, 0)] if r is not None else [])
    return _rowcall(fn, rows, [g, b], [(d, F32), (d, F32), (1, F32), (d, MXU_DTYPE)], n_rows=n_rows, tm=256, name=name)


def _ln_bwd(dya, dyb, xh, rstd, g, *, alpha, name, operand=True):
    n_rows, d = xh.shape

    def fn(da, db, xhv, rs, gv):
        dy = alpha * da + db
        dyg = dy * gv
        m1 = jnp.mean(dyg, axis=-1, keepdims=True)
        m2 = jnp.mean(dyg * xhv, axis=-1, keepdims=True)
        dx = rs * (dyg - m1 - xhv * m2)
        return (dx,) + ((dx,) if operand else ()) + (_colsum(dy * xhv), _colsum(dy), _colsum(dx))

    rows = [(dya, d, 0), (dyb, d, 0), (xh, d, 0), (rstd, 1, 0)]
    return _rowcall(fn, rows, [g], [(d, F32)] + [(d, MXU_DTYPE)] * operand, [d, d, d], n_rows=n_rows, tm=256, name=name)


def _ln_loss_bwd(a, r, target, g, b, *, alpha, name):
    n_rows, d = a.shape

    def fn(av, rv, tv, gv, bv):
        xin = alpha * av + rv
        mu = jnp.mean(xin, axis=-1, keepdims=True)
        xc = xin - mu
        var = jnp.mean(xc * xc, axis=-1, keepdims=True)
        rs = lax.rsqrt(var + LN_EPS)
        xh = xc * rs
        diff = xh * gv + bv - tv
        part = jnp.sum(jnp.sum(diff * diff, axis=1, keepdims=True), axis=0, keepdims=True) * (0.5 / d)
        dy = diff * (1.0 / d)
        dyg = dy * gv
        m1 = jnp.mean(dyg, axis=-1, keepdims=True)
        m2 = jnp.mean(dyg * xh, axis=-1, keepdims=True)
        dx = rs * (dyg - m1 - xh * m2)
        return dx, dx, _colsum(dy * xh), _colsum(dy), _colsum(dx), jnp.broadcast_to(part, (1, LANES))

    return _rowcall(fn, [(a, d, 0), (r, d, 0), (target, d, 0)], [g, b], [(d, F32), (d, MXU_DTYPE)], [d, d, d, LANES],
                    n_rows=n_rows, tm=256, name=name)


def _rope_lane_constants():
    lane = np.arange(ATT_MERGED)
    in_head = lane % ATT_HEAD_DIM
    sign = np.where(in_head < ROT_DIM // 2, -1.0, np.where(in_head < ROT_DIM, 1.0, 0.0)).astype(np.float32)
    inv_freq = ROPE_THETA ** (-jnp.arange(0, ROT_DIM, 2, dtype=F32) / ROT_DIM)
    return inv_freq[lane % (ROT_DIM // 2)].reshape(1, ATT_MERGED), jnp.asarray(sign).reshape(1, ATT_MERGED)


def _rope_tables(pos_col, *, name, after=None):
    inv_lane, sign = _rope_lane_constants()

    def fn(pos, inv, sg):
        ang = pos.astype(F32) * inv
        return jnp.where(sg != 0.0, jnp.cos(ang), 1.0), sg * jnp.sin(ang)

    return _rowcall(fn, [(pos_col, 1, 0)], [inv_lane, sign], [(ATT_MERGED, F32), (ATT_MERGED, F32)],
                    n_rows=pos_col.shape[0], tm=512, name=name, after=after)


def _rot_partner(t):
    lane = lax.broadcasted_iota(jnp.int32, t.shape, 1)
    width = t.shape[1]
    return jnp.where((lane & (ROT_DIM // 2)) == 0, pltpu.roll(t, width - ROT_DIM // 2, 1), pltpu.roll(t, ROT_DIM // 2, 1))


def _rope(t, cos_t, sin_t):
    return t * cos_t + _rot_partner(t) * sin_t


def _rope_transpose(dt, cos_t, sin_t):
    return dt * cos_t + _rot_partner(dt * sin_t)


def _strided_rows(r, count, stride):
    return pl.ds(r, count) if stride == 1 else pl.ds(r, count, stride=stride)


def _qkv_split(proj, cos_t, sin_t, *, name, tm=512):
    n_rows = proj.shape[0]
    n_g = len(DILATIONS)

    def body(*refs):
        n_src = LANE_HALVES * 3 * n_g
        src, tables, dst = refs[:n_src], refs[n_src:n_src + 2 * LANE_HALVES], refs[n_src + 2 * LANE_HALVES:]
        for kind in range(3):
            for g, dil in enumerate(DILATIONS):
                for half in range(LANE_HALVES):
                    x_ref, o_ref = src[(kind * n_g + g) * LANE_HALVES + half], dst[kind * n_g + g]
                    cos_ref, sin_ref = tables[half], tables[LANE_HALVES + half]
                    for r in range(dil):
                        rows = _strided_rows(r, tm // dil, dil)
                        t = x_ref[rows, :]
                        if kind < 2:
                            t = _rope(t, cos_ref[rows, :], sin_ref[rows, :])
                        lo = r * ATT_MERGED + half * LANES
                        o_ref[:, lo:lo + LANES] = t.astype(o_ref.dtype)

    half_spec = lambda cb: pl.BlockSpec((tm, LANES), functools.partial(lambda i, cb: (i, cb), cb=cb))
    in_specs = [half_spec((off + g) * LANE_HALVES + half)
                for off in (OFF_Q_BLK, OFF_K_BLK, OFF_V_BLK) for g in range(n_g) for half in range(LANE_HALVES)]
    in_specs += [half_spec(half) for _ in range(2) for half in range(LANE_HALVES)]
    out_specs = [pl.BlockSpec((tm // dil, dil * ATT_MERGED), lambda i: (i, 0)) for _ in range(3) for dil in DILATIONS]
    out_shape = [jax.ShapeDtypeStruct((n_rows // dil, dil * ATT_MERGED), MXU_DTYPE) for _ in range(3) for dil in DILATIONS]
    outs = pl.pallas_call(
        body, name=name, grid=(n_rows // tm,), in_specs=in_specs, out_specs=out_specs, out_shape=out_shape,
        compiler_params=_params("parallel"),
    )(*[proj] * (LANE_HALVES * 3 * n_g), *[cos_t] * LANE_HALVES, *[sin_t] * LANE_HALVES)
    return outs[:n_g], outs[n_g:2 * n_g], outs[2 * n_g:]


def _mix(gs, ga, z1, z2, b_att):
    return jax.nn.sigmoid(gs) * (z1 * jax.nn.sigmoid(z2)) + jax.nn.sigmoid(ga) * b_att


def _mix_rows(proj, z, b_att):
    return [(proj, D_MODEL, OFF_GS_BLK), (proj, D_MODEL, OFF_GA_BLK), (z, D_MODEL, 0), (z, D_MODEL, 1), (b_att, D_MODEL, 0)]


def _mix_fwd(proj, z, b_att, *, name):
    return _rowcall(_mix, _mix_rows(proj, z, b_att), [], [(D_MODEL, MXU_DTYPE)],
                    n_rows=proj.shape[0], tm=256, name=name)[0]


def _mix_bwd(dmixed, proj, z, b_att, *, name):
    def fn(dm, gs, ga, z1, z2, ba):
        _, vjp = jax.vjp(_mix, gs, ga, z1, z2, ba)
        dgs, dga, dz1, dz2, dba = vjp(dm)
        dz = jnp.concatenate([dz1, dz2], axis=1)
        return dgs, dga, dz, dba, _colsum(dgs), _colsum(dga), _colsum(dz)

    rows = [(dmixed, D_MODEL, 0)] + _mix_rows(proj, z, b_att)
    widths = [D_MODEL, D_MODEL, 2 * D_MODEL, D_MODEL]
    return _rowcall(fn, rows, [], [(w, MXU_DTYPE) for w in widths], widths[:3], n_rows=proj.shape[0], tm=256, name=name)


def _gelu_bwd(dgy, y, proj, *, name):
    def fn(dg, yv, u):
        _, vjp = jax.vjp(jax.nn.gelu, yv)
        dy = vjp(dg)[0]
        return dy, _colsum(dy * u)

    return _rowcall(fn, [(dgy, SSM_WIDTH, 0), (y, SSM_WIDTH, 0), (proj, SSM_WIDTH, 0)], [], [(SSM_WIDTH, F32)],
                    [SSM_WIDTH], n_rows=y.shape[0], tm=512, name=name)


HEAD_ROWS = ATT_HEADS_PER_GROUP * ATT_BLK


def _head_masks(rows):
    head = lax.broadcasted_iota(jnp.int32, (rows, ATT_MERGED), 1) >> (ATT_HEAD_DIM.bit_length() - 1)
    return [head == h for h in range(ATT_HEADS_PER_GROUP)]


def _stack_heads(t, masks):
    return jnp.concatenate([jnp.where(m, t, jnp.zeros_like(t)) for m in masks], axis=0)


def _unstack_heads(t4, masks):
    blocks = [t4[h * ATT_BLK:(h + 1) * ATT_BLK] for h in range(ATT_HEADS_PER_GROUP)]
    return jnp.where(masks[0], blocks[0], jnp.where(masks[1], blocks[1], jnp.where(masks[2], blocks[2], blocks[3])))


def _head_column(stats, first):
    return jnp.concatenate([stats[:, first + h:first + h + 1] for h in range(ATT_HEADS_PER_GROUP)], axis=0)


def _band_mask(first_key):
    qi = lax.broadcasted_iota(jnp.int32, (HEAD_ROWS, 2 * ATT_BLK), 0) & (ATT_BLK - 1)
    ki = lax.broadcasted_iota(jnp.int32, (HEAD_ROWS, 2 * ATT_BLK), 1)
    steps = qi + ATT_BLK - ki
    return (steps >= 0) & (steps <= ATT_BLK) & (ki >= first_key)


def _dil_fwd(q, k, v, dil, *, name):
    n_blk = q.shape[0] // ATT_BLK
    cur = pl.BlockSpec((ATT_BLK, ATT_MERGED), lambda r, n: (n, r))
    prev = pl.BlockSpec((ATT_BLK, ATT_MERGED), lambda r, n: (jnp.maximum(n - 1, 0), r))

    def body(q_ref, kp_ref, kc_ref, vp_ref, vc_ref, o_ref, l_ref):
        masks = _head_masks(ATT_BLK)
        valid = _band_mask(jnp.where(pl.program_id(1) > 0, 0, ATT_BLK))
        keys = jnp.concatenate([kp_ref[...], kc_ref[...]], axis=0)
        vals = jnp.concatenate([vp_ref[...], vc_ref[...]], axis=0)
        s = jnp.where(valid, _dot(_stack_heads(q_ref[...], masks), keys, 1, 1) * ATT_SCALE, NEG_INF)
        m = jnp.max(s, axis=-1, keepdims=True)
        p = jnp.exp(s - m)
        den = jnp.sum(p, axis=-1, keepdims=True)
        o_ref[...] = _unstack_heads(_dot(p, vals, 1, 0) / den, masks)
        l_ref[...] = _unstack_heads(jnp.broadcast_to(m + jnp.log(den), (HEAD_ROWS, ATT_MERGED)), masks)

    shape = jax.ShapeDtypeStruct(q.shape, F32)
    return pl.pallas_call(
        body, name=name, grid=(dil, n_blk), in_specs=[cur, prev, cur, prev, cur], out_specs=[cur, cur],
        out_shape=[shape, shape], compiler_params=_params("parallel", "parallel"),
    )(q, k, k, v, v)


def _att_merge(outs, lses, *, name, tm=512):
    n_g = len(outs)
    n_rows = outs[0].shape[0] * DILATIONS[0]

    def body(*refs):
        src, (att_ref, lse_ref), tmp = refs[:2 * n_g], refs[2 * n_g:2 * n_g + 2], refs[2 * n_g + 2:]
        vals = []
        for idx, src_ref in enumerate(src):
            dil = DILATIONS[idx % n_g]
            if dil == 1:
                vals.append(src_ref[...])
                continue
            for r in range(dil):
                for half in range(LANE_HALVES):
                    lo = r * ATT_MERGED + half * LANES
                    tmp[LANE_HALVES * idx + half][_strided_rows(r, tm // dil, dil), :] = src_ref[:, lo:lo + LANES]
            vals.append(jnp.concatenate([tmp[LANE_HALVES * idx + half][...] for half in range(LANE_HALVES)], axis=1))
        o, l = vals[:n_g], vals[n_g:]
        m = functools.reduce(jnp.maximum, l)
        e = [jnp.exp(li - m) for li in l]
        z = functools.reduce(jnp.add, e)
        att_ref[...] = functools.reduce(jnp.add, [(ei / z) * oi for ei, oi in zip(e, o)])
        lse_ref[...] = m + jnp.log(z)

    in_specs = [pl.BlockSpec((tm // dil, dil * ATT_MERGED), lambda i: (i, 0)) for _ in range(2) for dil in DILATIONS]
    row = pl.BlockSpec((tm, ATT_MERGED), lambda i: (i, 0))
    shape = jax.ShapeDtypeStruct((n_rows, ATT_MERGED), F32)
    return pl.pallas_call(
        body, name=name, grid=(n_rows // tm,), in_specs=in_specs, out_specs=[row, row], out_shape=[shape, shape],
        scratch_shapes=[pltpu.VMEM((tm, LANES), F32)] * (LANE_HALVES * 2 * n_g), compiler_params=_params("parallel"),
    )(*outs, *lses)


def _att_stats(datt, att, lse, *, name):
    n_rows = datt.shape[0]

    def fn(d, a, l):
        prod = d * a
        lane = lax.broadcasted_iota(jnp.int32, (d.shape[0], LANES), 1)
        out = jnp.zeros((d.shape[0], LANES), F32)
        for h in range(ATT_HEADS_PER_GROUP):
            lo = h * ATT_HEAD_DIM
            out = jnp.where(lane == h, l[:, lo:lo + 1], out)
            delta = jnp.sum(prod[:, lo:lo + ATT_HEAD_DIM], axis=-1, keepdims=True)
            out = jnp.where(lane == ATT_HEADS_PER_GROUP + h, delta, out)
        return out

    rows = [(t, ATT_MERGED, 0) for t in (datt, att, lse)]
    return _rowcall(fn, rows, [], [(LANES, F32)], n_rows=n_rows, tm=512, name=name)[0]


def _dil_bwd(q, k, v, datt, stats, dil, *, name):
    n_rows = datt.shape[0]
    n_blk = n_rows // dil // ATT_BLK
    span = ATT_BLK * dil
    cur = pl.BlockSpec((ATT_BLK, ATT_MERGED), lambda n, r: (n, r))
    prev = pl.BlockSpec((ATT_BLK, ATT_MERGED), lambda n, r: (jnp.maximum(n - 1, 0), r))
    nxt = pl.BlockSpec((ATT_BLK, ATT_MERGED), lambda n, r: (jnp.minimum(n + 1, n_blk - 1), r))
    seq = lambda half, ahead: pl.BlockSpec((span, LANES), lambda n, r: (jnp.minimum(n + ahead, n_blk - 1), half))

    def body(qc_ref, qn_ref, kp_ref, kc_ref, vp_ref, vc_ref, dc0_ref, dc1_ref, dn0_ref, dn1_ref, sc_ref, sn_ref,
             dq0_ref, dq1_ref, dk0_ref, dk1_ref, dv0_ref, dv1_ref):
        n = pl.program_id(0)
        rows = slice(None) if dil == 1 else _strided_rows(pl.program_id(1), ATT_BLK, dil)

        def read(ref0, ref1):
            return jnp.concatenate([ref0[rows, :], ref1[rows, :]], axis=1)

        def write(ref0, ref1, val):
            ref0[rows, :] = val[:, :LANES]
            ref1[rows, :] = val[:, LANES:]

        masks = _head_masks(ATT_BLK)
        valid = _band_mask(jnp.where(n > 0, 0, ATT_BLK))
        qi = lax.broadcasted_iota(jnp.int32, (HEAD_ROWS, ATT_BLK), 0) & (ATT_BLK - 1)
        ki = lax.broadcasted_iota(jnp.int32, (HEAD_ROWS, ATT_BLK), 1)
        valid_next = (ki - qi) >= jnp.where(n < n_blk - 1, 0, ATT_BLK)

        kc, vc = kc_ref[...], vc_ref[...]
        keys = jnp.concatenate([kp_ref[...], kc], axis=0)
        vals = jnp.concatenate([vp_ref[...], vc], axis=0)
        q4 = _stack_heads(qc_ref[...], masks)
        d4 = _stack_heads(read(dc0_ref, dc1_ref).astype(MXU_DTYPE), masks)
        st = sc_ref[rows, :]
        p = jnp.where(valid, jnp.exp(_dot(q4, keys, 1, 1) * ATT_SCALE - _head_column(st, 0)), 0.0)
        ds = p * (_dot(d4, vals, 1, 1) - _head_column(st, ATT_HEADS_PER_GROUP)) * ATT_SCALE
        write(dq0_ref, dq1_ref, _unstack_heads(_dot(ds, keys, 1, 0), masks))

        q4n = _stack_heads(qn_ref[...], masks)
        d4n = _stack_heads(read(dn0_ref, dn1_ref).astype(MXU_DTYPE), masks)
        stn = sn_ref[rows, :]
        p_n = jnp.where(valid_next, jnp.exp(_dot(q4n, kc, 1, 1) * ATT_SCALE - _head_column(stn, 0)), 0.0)
        ds_n = p_n * (_dot(d4n, vc, 1, 1) - _head_column(stn, ATT_HEADS_PER_GROUP)) * ATT_SCALE
        write(dv0_ref, dv1_ref, _dot(p[:, ATT_BLK:], d4, 0, 0) + _dot(p_n, d4n, 0, 0))
        write(dk0_ref, dk1_ref, _dot(ds[:, ATT_BLK:], q4, 0, 0) + _dot(ds_n, q4n, 0, 0))

    shape = jax.ShapeDtypeStruct((n_rows, LANES), F32)
    out = seq(0, 0)
    res = pl.pallas_call(
        body, name=name, grid=(n_blk, dil),
        in_specs=[cur, nxt, prev, cur, prev, cur, seq(0, 0), seq(1, 0), seq(0, 1), seq(1, 1), seq(0, 0), seq(0, 1)],
        out_specs=[out] * 6, out_shape=[shape] * 6, compiler_params=_params("parallel", "arbitrary"),
    )(q, q, k, k, v, v, datt, datt, datt, datt, stats, stats)
    return [(res[2 * i], res[2 * i + 1]) for i in range(3)]


def _dproj_assemble(du, dqkv, dgs, dga, cos_t, sin_t, *, name):
    n_g = len(DILATIONS)

    def fn(*t):
        n_half = LANE_HALVES * 3 * n_g
        du_t, halves, (dgs_t, dga_t, c, s) = t[0], t[1:1 + n_half], t[1 + n_half:]
        parts = [jnp.concatenate(halves[LANE_HALVES * i:LANE_HALVES * (i + 1)], axis=1) for i in range(3 * n_g)]
        for i in range(2 * n_g):
            parts[i] = _rope_transpose(parts[i], c, s)
        cast = [p.astype(MXU_DTYPE) for p in parts]
        return [jnp.concatenate([du_t] + cast + [dgs_t, dga_t], axis=1)] + [_colsum(p) for p in parts]

    rows = [(du, SSM_WIDTH, 0)]
    rows += [(half, LANES, 0) for i in range(3) for g in range(n_g) for half in dqkv[g][i]]
    rows += [(dgs, D_MODEL, 0), (dga, D_MODEL, 0), (cos_t, ATT_MERGED, 0), (sin_t, ATT_MERGED, 0)]
    width = SSM_WIDTH + 3 * n_g * ATT_MERGED + 2 * D_MODEL
    res = _rowcall(fn, rows, [], [(width, MXU_DTYPE)], [ATT_MERGED] * (3 * n_g), n_rows=du.shape[0], tm=256, name=name)
    return res[0], res[1:]


def _xhead(h):
    return slice(h * XATT_HEAD_DIM, (h + 1) * XATT_HEAD_DIM)


def _xatt_probs(qh, kh):
    s = _dot(qh, kh, 1, 1) * XATT_SCALE
    e = jnp.exp(s - jnp.max(s, axis=-1, keepdims=True))
    return e / jnp.sum(e, axis=-1, keepdims=True)


def _xatt_fwd(q, kv, *, name, tm=512):
    n_rows = q.shape[0]
    n_mem = kv.shape[0]

    def body(q_ref, kv_ref, o_ref):
        for h in range(XATT_HEADS):
            sl = _xhead(h)
            p = _xatt_probs(q_ref[:, sl], kv_ref[:, sl])
            o_ref[:, sl] = _dot(p, kv_ref[:, D_MODEL + h * XATT_HEAD_DIM:D_MODEL + (h + 1) * XATT_HEAD_DIM], 1, 0
                                ).astype(o_ref.dtype)

    row = pl.BlockSpec((tm, D_MODEL), lambda i: (i, 0))
    return pl.pallas_call(
        body, name=name, grid=(n_rows // tm,),
        in_specs=[row, pl.BlockSpec((n_mem, 2 * D_MODEL), lambda i: (0, 0))], out_specs=row,
        out_shape=jax.ShapeDtypeStruct((n_rows, D_MODEL), MXU_DTYPE), compiler_params=_params("parallel"),
    )(q, kv)


def _xatt_bwd(q, kv, do, *, name, tm=512):
    n_rows = q.shape[0]
    n_mem = kv.shape[0]

    def body(q_ref, kv_ref, do_ref, dq_ref, dkv_ref):
        @pl.when(pl.program_id(0) == 0)
        def _():
            dkv_ref[...] = jnp.zeros_like(dkv_ref)

        for h in range(XATT_HEADS):
            sl = _xhead(h)
            vsl = slice(D_MODEL + h * XATT_HEAD_DIM, D_MODEL + (h + 1) * XATT_HEAD_DIM)
            qh, kh, doh = q_ref[:, sl], kv_ref[:, sl], do_ref[:, sl]
            p = _xatt_probs(qh, kh)
            dp = _dot(doh, kv_ref[:, vsl], 1, 1)
            ds = p * (dp - jnp.sum(dp * p, axis=-1, keepdims=True)) * XATT_SCALE
            dq_ref[:, sl] = _dot(ds, kh, 1, 0).astype(dq_ref.dtype)
            dkv_ref[:, sl] += _dot(ds, qh, 0, 0)
            dkv_ref[:, vsl] += _dot(p, doh, 0, 0)

    row = pl.BlockSpec((tm, D_MODEL), lambda i: (i, 0))
    full = pl.BlockSpec((n_mem, 2 * D_MODEL), lambda i: (0, 0))
    return pl.pallas_call(
        body, name=name, grid=(n_rows // tm,), in_specs=[row, full, row], out_specs=[row, full],
        out_shape=[jax.ShapeDtypeStruct((n_rows, D_MODEL), MXU_DTYPE), jax.ShapeDtypeStruct((n_mem, 2 * D_MODEL), F32)],
        compiler_params=_params("arbitrary"),
    )(q, kv, do)


def _disc(logdt, a_re, a_im, b_re, b_im):
    dt = jnp.exp(logdt)
    mag = jnp.exp(a_re * dt)
    ab_re = mag * jnp.cos(a_im * dt)
    ab_im = mag * jnp.sin(a_im * dt)
    den = jnp.square(a_re) + jnp.square(a_im)
    nr = ab_re - 1.0
    f_re = (nr * a_re + ab_im * a_im) / den
    f_im = (ab_im * a_re - nr * a_im) / den
    bb_re = f_re[None] * b_re - f_im[None] * b_im
    bb_im = f_re[None] * b_im + f_im[None] * b_re
    return ab_re, ab_im, bb_re, bb_im


def _disc_transpose(logdt, a_re, a_im, b_re, b_im, g_ab_re, g_ab_im, g_bb_re, g_bb_im):
    dt = jnp.exp(logdt)
    mag = jnp.exp(a_re * dt)
    th = a_im * dt
    cs, sn = jnp.cos(th), jnp.sin(th)
    ab_re, ab_im = mag * cs, mag * sn
    den = jnp.square(a_re) + jnp.square(a_im)
    nr = ab_re - 1.0
    f_re = (nr * a_re + ab_im * a_im) / den
    f_im = (ab_im * a_re - nr * a_im) / den
    d_f_re = jnp.sum(g_bb_re * b_re + g_bb_im * b_im, axis=0)
    d_f_im = jnp.sum(g_bb_im * b_re - g_bb_re * b_im, axis=0)
    d_b_re = g_bb_re * f_re[None] + g_bb_im * f_im[None]
    d_b_im = g_bb_im * f_re[None] - g_bb_re * f_im[None]
    d_n_re, d_n_im = d_f_re / den, d_f_im / den
    d_den = -(d_f_re * f_re + d_f_im * f_im) / den
    d_ab_re = g_ab_re + d_n_re * a_re - d_n_im * a_im
    d_ab_im = g_ab_im + d_n_re * a_im + d_n_im * a_re
    d_a_re = d_n_re * nr + d_n_im * ab_im + 2.0 * d_den * a_re
    d_a_im = d_n_re * ab_im - d_n_im * nr + 2.0 * d_den * a_im
    d_mag = d_ab_re * cs + d_ab_im * sn
    d_th = mag * (d_ab_im * cs - d_ab_re * sn)
    d_a_re = d_a_re + d_mag * mag * dt
    d_a_im = d_a_im + d_th * dt
    d_dt = jnp.sum(d_mag * mag * a_re + d_th * a_im, axis=-1, keepdims=True)
    return d_dt * dt, d_a_re, d_a_im, d_b_re, d_b_im


def _whole(fn, args, out_shapes, *, name):
    n_in = len(args)

    def body(*refs):
        res = fn(*[r[...] for r in refs[:n_in]])
        for o_ref, val in zip(refs[n_in:], res):
            o_ref[...] = val

    return pl.pallas_call(body, name=name, out_shape=[jax.ShapeDtypeStruct(s, F32) for s in out_shapes],
                          compiler_params=pltpu.CompilerParams(vmem_limit_bytes=VMEM_LIMIT_BYTES))(*args)


SSM_WIDE =GROUPS_PER_TILE * SSM_STATE
LANE_GROUPS_PER_TILE = SSM_WIDE // LANES


def _chan(j):
    return slice(j * LANES, (j + 1) * LANES)


def _time_major_rows(j, q, tc):
    return pl.ds(j * LANE_GROUPS_PER_TILE + q, tc, stride=STATE_VREG_ROWS)


def _to_time_major(x, t_re_ref, t_im_ref, dst_re, dst_im, tc):
    for j in range(SSM_TILES):
        xj = x[:, _chan(j)]
        for t_ref, dst in ((t_re_ref, dst_re), (t_im_ref, dst_im)):
            r = _dot(xj, t_ref[j], 1, 0)
            for q in range(LANE_GROUPS_PER_TILE):
                dst[_time_major_rows(j, q, tc), :] = r[:, q * LANES:(q + 1) * LANES]


def _from_time_major(src, j, tc):
    return jnp.concatenate([src[_time_major_rows(j, q, tc), :] for q in range(LANE_GROUPS_PER_TILE)], axis=1)


def _scan_chunk(w_re, w_im, h_re, h_im, a_re, a_im, start, tc):
    def step(t, carry):
        hr, hi = carry
        rows = _scan_rows(t)
        nr = a_re * hr - a_im * hi + w_re[rows, :]
        ni = a_re * hi + a_im * hr + w_im[rows, :]
        h_re[rows, :] = nr
        h_im[rows, :] = ni
        return nr, ni

    return lax.fori_loop(0, tc, step, start, unroll=8)


SSM_CHUNK = 256


def _tile_spec(stack, k):
    return pl.BlockSpec((pl.Squeezed(),) + tuple(stack.shape[1:]), lambda i: (k, 0, 0, 0))


def _expand_block_diagonal(src_ref, dst):
    dst[...] = jnp.zeros_like(dst)
    r, c = src_ref.shape[1:]
    for g in range(SSM_GROUPS):
        j, gl = divmod(g, GROUPS_PER_TILE)
        dst[j, gl * r:(gl + 1) * r, gl * c:(gl + 1) * c] = src_ref[g].astype(dst.dtype)


def _extract_block_diagonal(src, dst_ref):
    r, c = dst_ref.shape[1:]
    for g in range(SSM_GROUPS):
        j, gl = divmod(g, GROUPS_PER_TILE)
        dst_ref[g] = src[j, gl * r:(gl + 1) * r, gl * c:(gl + 1) * c]


def _ssm_fwd(proj, blocks_cn, blocks_nc, a_re, a_im, gain, *, name, tc=SSM_CHUNK):
    n_rows = proj.shape[0]
    n_chunk = n_rows // tc

    def body(u_ref, br_ref, bi_ref, cr_ref, ci_ref, ar_ref, ai_ref, g_ref, y_ref, gy_ref, hr, hi, wr, wi, state,
             tbr_ref, tbi_ref, tcr_ref, tci_ref):
        @pl.when(pl.program_id(0) == 0)
        def _():
            state[...] = jnp.zeros_like(state)
            for src_ref, dst in ((br_ref, tbr_ref), (bi_ref, tbi_ref), (cr_ref, tcr_ref), (ci_ref, tci_ref)):
                _expand_block_diagonal(src_ref, dst)

        u = u_ref[...]
        _to_time_major(u, tbr_ref, tbi_ref, wr, wi, tc)
        state[0], state[1] = _scan_chunk(wr, wi, hr, hi, ar_ref[...], ai_ref[...], (state[0], state[1]), tc)
        for j in range(SSM_TILES):
            yj = (_dot(_from_time_major(hr, j, tc), tcr_ref[j], 1, 0) + _dot(_from_time_major(hi, j, tc), tci_ref[j], 1, 0)
                  + g_ref[:, _chan(j)] * u[:, _chan(j)])
            y_ref[:, _chan(j)] = yj
            gy_ref[:, _chan(j)] = jax.nn.gelu(yj).astype(gy_ref.dtype)

    rows = pl.BlockSpec((tc, SSM_WIDTH), lambda i: (i, 0))
    coef = pl.BlockSpec((STATE_VREG_ROWS, LANES), lambda i: (0, 0))
    states = pl.BlockSpec((tc * STATE_VREG_ROWS, LANES), lambda i: (i, 0))
    sshape = jax.ShapeDtypeStruct((n_rows * STATE_VREG_ROWS, LANES), F32)
    return pl.pallas_call(
        body, name=name, grid=(n_chunk,),
        in_specs=[rows, _tile_spec(blocks_cn, 0), _tile_spec(blocks_cn, 1), _tile_spec(blocks_nc, 0),
                  _tile_spec(blocks_nc, 1), coef, coef, pl.BlockSpec((1, SSM_WIDTH), lambda i: (0, 0))],
        out_specs=[rows, rows, states, states],
        out_shape=[jax.ShapeDtypeStruct((n_rows, SSM_WIDTH), F32), jax.ShapeDtypeStruct((n_rows, SSM_WIDTH), MXU_DTYPE),
                   sshape, sshape],
        scratch_shapes=[pltpu.VMEM((tc * STATE_VREG_ROWS, LANES), F32)] * 2 + [pltpu.VMEM((2, STATE_VREG_ROWS, LANES), F32)]
        + [pltpu.VMEM((SSM_TILES, LANES, SSM_WIDE), MXU_DTYPE)] * 2 + [pltpu.VMEM((SSM_TILES, SSM_WIDE, LANES), MXU_DTYPE)] * 2,
        compiler_params=_params("arbitrary"),
    )(proj, blocks_cn, blocks_cn, blocks_nc, blocks_nc, a_re, a_im, gain)


def _ssm_bwd(proj, dy, h_re, h_im, blocks_cn, blocks_nc, a_re, a_im, gain, *, name, tc=SSM_CHUNK):
    n_rows = proj.shape[0]
    n_chunk = n_rows // tc

    def body(u_ref, dy_ref, hr, hi, cr_ref, ci_ref, br_ref, bi_ref, ar_ref, ai_ref, g_ref,
             du_ref, su_ref, dc_re_ref, dc_im_ref, db_re_ref, db_im_ref, dar_ref, dai_ref, wr, wi, carry,
             tdr_ref, tdi_ref, tur_ref, tui_ref, dcr_ref, dci_ref, dbr_ref, dbi_ref):
        @pl.when(pl.program_id(0) == 0)
        def _():
            carry[...] = jnp.zeros_like(carry)
            for acc_ref in (su_ref, dcr_ref, dci_ref, dbr_ref, dbi_ref):
                acc_ref[...] = jnp.zeros_like(acc_ref)
            for src_ref, dst in ((cr_ref, tdr_ref), (ci_ref, tdi_ref), (br_ref, tur_ref), (bi_ref, tui_ref)):
                _expand_block_diagonal(src_ref, dst)

        a_r, a_i = ar_ref[...], ai_ref[...]
        u, dyv = u_ref[...], dy_ref[...]
        _to_time_major(dyv, tdr_ref, tdi_ref, wr, wi, tc)

        def step(kk, c):
            lam_r, lam_i, dar, dai = c
            rows = _scan_rows(tc - 1 - kk)
            h_r, h_i = hr[rows, :], hi[rows, :]
            dar = dar + lam_r * h_r + lam_i * h_i
            dai = dai + lam_i * h_r - lam_r * h_i
            new_r = wr[rows, :] + a_r * lam_r + a_i * lam_i
            new_i = wi[rows, :] + a_r * lam_i - a_i * lam_r
            wr[rows, :] = new_r
            wi[rows, :] = new_i
            return new_r, new_i, dar, dai

        carry[0], carry[1], carry[2], carry[3] = lax.fori_loop(0, tc, step, (carry[0], carry[1], carry[2], carry[3]),
                                                              unroll=8)
        dar_ref[...] = carry[2]
        dai_ref[...] = carry[3]
        for j in range(SSM_TILES):
            cj = _chan(j)
            lam_r, lam_i = _from_time_major(wr, j, tc), _from_time_major(wi, j, tc)
            dcr_ref[j] += _dot(dyv[:, cj], _from_time_major(hr, j, tc), 0, 0)
            dci_ref[j] += _dot(dyv[:, cj], _from_time_major(hi, j, tc), 0, 0)
            dbr_ref[j] += _dot(u[:, cj], lam_r, 0, 0)
            dbi_ref[j] += _dot(u[:, cj], lam_i, 0, 0)
            duj = _dot(lam_r, tur_ref[j], 1, 0) + _dot(lam_i, tui_ref[j], 1, 0) + g_ref[:, cj] * dyv[:, cj]
            du_ref[:, cj] = duj.astype(du_ref.dtype)
            su_ref[:, cj] += _colsum(duj)

        @pl.when(pl.program_id(0) == n_chunk - 1)
        def _():
            for src, dst_ref in ((dcr_ref, dc_re_ref), (dci_ref, dc_im_ref), (dbr_ref, db_re_ref), (dbi_ref, db_im_ref)):
                _extract_block_diagonal(src, dst_ref)

    back = lambda i: (n_chunk - 1 - i, 0)
    rows = pl.BlockSpec((tc, SSM_WIDTH), back)
    blocks = pl.BlockSpec((SSM_GROUPS, SSM_GROUP, SSM_STATE), lambda i: (0, 0, 0))
    coef = pl.BlockSpec((STATE_VREG_ROWS, LANES), lambda i: (0, 0))
    states = pl.BlockSpec((tc * STATE_VREG_ROWS, LANES), back)
    vec = pl.BlockSpec((1, SSM_WIDTH), lambda i: (0, 0))
    bshape = jax.ShapeDtypeStruct((SSM_GROUPS, SSM_GROUP, SSM_STATE), F32)
    cshape = jax.ShapeDtypeStruct((STATE_VREG_ROWS, LANES), F32)
    return pl.pallas_call(
        body, name=name, grid=(n_chunk,),
        in_specs=[rows, rows, states, states, _tile_spec(blocks_cn, 2), _tile_spec(blocks_cn, 3), _tile_spec(blocks_nc, 2),
                  _tile_spec(blocks_nc, 3), coef, coef, vec],
        out_specs=[rows, vec, blocks, blocks, blocks, blocks, coef, coef],
        out_shape=[jax.ShapeDtypeStruct((n_rows, SSM_WIDTH), MXU_DTYPE), jax.ShapeDtypeStruct((1, SSM_WIDTH), F32),
                   bshape, bshape, bshape, bshape, cshape, cshape],
        scratch_shapes=[pltpu.VMEM((tc * STATE_VREG_ROWS, LANES), F32)] * 2 + [pltpu.VMEM((4, STATE_VREG_ROWS, LANES), F32)]
        + [pltpu.VMEM((SSM_TILES, LANES, SSM_WIDE), MXU_DTYPE)] * 2 + [pltpu.VMEM((SSM_TILES, SSM_WIDE, LANES), MXU_DTYPE)] * 2
        + [pltpu.VMEM((SSM_TILES, LANES, SSM_WIDE), F32)] * 4,
        compiler_params=_params("arbitrary"),
    )(proj, dy, h_re, h_im, blocks_cn, blocks_cn, blocks_nc, blocks_nc, a_re, a_im, gain)


def _scan_rows(t):
    return pl.ds(pl.multiple_of(t * STATE_VREG_ROWS, 8), STATE_VREG_ROWS)


GATHER_GROUPS = (("w_glu", "w_att_up", "w_mix_out"), ("w_xq", "w_xkv", "w_xo", "w_ff1", "w_ff2"))
SCATTER_GROUPS = (("w_ff2", "w_ff1"), ("w_xo", "w_xq", "w_xkv", "w_mix_out"), ("w_att_up", "w_glu"), ("w_in",))


def _local_grads(x, mem, pos_col, target, sm, fetch_in, fetch, send, send_small, start_token):
    b_re_t = sm["ssm_b_re"].transpose(2, 0, 1)
    b_im_t = sm["ssm_b_im"].transpose(2, 0, 1)
    logdt = sm["ssm_log_dt"].reshape(SSM_GROUPS, 1)
    c_re, c_im = sm["ssm_c_re"], sm["ssm_c_im"]
    grp = (SSM_GROUPS, SSM_STATE)
    chn = (SSM_GROUP, SSM_GROUPS, SSM_STATE)

    wts = {}
    cos_t, sin_t = _rope_tables(pos_col, after=start_token, name="rope_tables")
    h0, xh0, rs0, h0m = _ln_fwd(x, None, sm["ln_in_g"], sm["ln_in_b"], alpha=1.0, name="ln_in_fwd")
    disc_in = (logdt, sm["ssm_a_re"], sm["ssm_a_im"], b_re_t, b_im_t)
    ab_re, ab_im, bb_re_t, bb_im_t = _whole(_disc, disc_in, [grp, grp, chn, chn], name="ssm_disc")
    a_re_rows, a_im_rows = ab_re.reshape(STATE_VREG_ROWS, LANES), ab_im.reshape(STATE_VREG_ROWS, LANES)
    tiles_cn = jnp.stack([bb_re_t.transpose(1, 0, 2), bb_im_t.transpose(1, 0, 2), c_re, -c_im])
    tiles_nc = jnp.stack([c_re.transpose(0, 2, 1), -c_im.transpose(0, 2, 1), bb_re_t.transpose(1, 2, 0),
                          bb_im_t.transpose(1, 2, 0)])
    w_in_near, near_ids = fetch_in(0, [h0m, tiles_cn, tiles_nc])
    proj = _mm_shards(h0m, w_in_near, sm["b_in"], near_ids, name="in_proj_near")
    wts["w_in"], far_ids = fetch_in(1, [proj])
    proj = _mm_shards(h0m, wts["w_in"], sm["b_in"], far_ids, prev=proj, name="in_proj_far")

    y, gy, h_re, h_im = _ssm_fwd(proj, tiles_cn, tiles_nc, a_re_rows, a_im_rows, sm["ssm_d"], name="ssm_fwd")

    q, k, v = _qkv_split(proj, cos_t, sin_t, name="qkv_split")
    outs, lses = [], []
    for g, dil in enumerate(DILATIONS):
        o_g, l_g = _dil_fwd(q[g], k[g], v[g], dil, name=f"dil_att_fwd_{dil}")
        outs.append(o_g)
        lses.append(l_g)
    att, lse = _att_merge(outs, lses, name="att_merge")
    wts.update(fetch(0, [att]))
    z = _mm(gy, wts["w_glu"], bias=sm["b_glu"], b_shards=True, name="glu_proj")
    b_att = _mm(att, wts["w_att_up"], b_shards=True, name="att_up")

    mixed = _mix_fwd(proj, z, b_att, name="gate_mix")
    mix_out = _mm(mixed, wts["w_mix_out"], bias=sm["b_mix_out"], name="mix_out")
    h1, xh1, rs1, h1m = _ln_fwd(h0, mix_out, sm["ln1_g"], sm["ln1_b"], alpha=DEEPNORM_ALPHA, name="ln1_fwd")

    wts.update(fetch(1, [h1m]))
    xq = _mm(h1m, wts["w_xq"], out_dtype=MXU_DTYPE, name="xatt_q")
    kv = _mm(mem, wts["w_xkv"], out_dtype=MXU_DTYPE, b_shards=True, name="xatt_kv")
    xo_in = _xatt_fwd(xq, kv, name="xatt_fwd")
    xo = _mm(xo_in, wts["w_xo"], name="xatt_o")
    h2, xh2, rs2, h2m = _ln_fwd(h1, xo, sm["ln2_g"], sm["ln2_b"], alpha=DEEPNORM_ALPHA, name="ln2_fwd")

    pre, act = _mm(h2m, wts["w_ff1"], bias=sm["b_ff1"], b_shards=True, name="ff1",
                   also=(lambda r: jnp.square(jnp.maximum(r, 0.0)), MXU_DTYPE))
    ff = _mm(act, wts["w_ff2"], bias=sm["b_ff2"], name="ff2")

    gw, gs = {}, {}
    dr3, dr3m, gs["ln3_g"], gs["ln3_b"], gs["b_ff2"], loss_row = _ln_loss_bwd(
        h2, ff, target, sm["ln3_g"], sm["ln3_b"], alpha=DEEPNORM_ALPHA, name="ln3_loss")
    wgrad = functools.partial(_mm, ta=True, out_dtype=WIRE_DTYPE, tk=2048)
    gw["w_ff2"] = wgrad(act, dr3m, tk=1024, name="ff2_dw")
    dpre, gs["b_ff1"] = _mm(dr3m, wts["w_ff2"], tb=True, out_dtype=MXU_DTYPE, colsum=True, name="ff2_dx",
                            gate=(pre, lambda p: 2.0 * jnp.maximum(p, 0.0)))
    gw["w_ff1"] = wgrad(h2m, dpre, out_shards=True, name="ff1_dw")
    sent = send(0, gw)
    dh2 = _mm(dpre, wts["w_ff1"], tb=True, b_shards=True, after=sent, name="ff1_dx")

    dr2, dr2m, gs["ln2_g"], gs["ln2_b"], _ = _ln_bwd(dr3, dh2, xh2, rs2, sm["ln2_g"], alpha=DEEPNORM_ALPHA,
                                                     name="ln2_bwd")
    gw["w_xo"] = wgrad(xo_in, dr2m, name="xatt_o_dw")
    dxo_in = _mm(dr2m, wts["w_xo"], tb=True, out_dtype=MXU_DTYPE, name="xatt_o_dx")
    dxq, dkv = _xatt_bwd(xq, kv, dxo_in, name="xatt_bwd")
    gw["w_xq"] = wgrad(h1m, dxq, name="xatt_q_dw")
    gw["w_xkv"] = wgrad(mem, dkv, out_shards=True, name="xatt_kv_dw")
    dh1 = _mm(dxq, wts["w_xq"], tb=True, name="xatt_q_dx")

    dr1, dr1m, gs["ln1_g"], gs["ln1_b"], gs["b_mix_out"] = _ln_bwd(dr2, dh1, xh1, rs1, sm["ln1_g"],
                                                                   alpha=DEEPNORM_ALPHA, name="ln1_bwd")
    gw["w_mix_out"] = wgrad(mixed, dr1m, name="mix_out_dw")
    sent = send(1, gw)
    dmixed = _mm(dr1m, wts["w_mix_out"], tb=True, after=sent, name="mix_out_dx")
    dgs, dga, dz, db_att, s_gs, s_ga, gs["b_glu"] = _mix_bwd(dmixed, proj, z, b_att, name="gate_mix_bwd")

    gw["w_att_up"] = wgrad(att, db_att, out_shards=True, name="att_up_dw")
    gw["w_glu"] = wgrad(gy, dz, out_shards=True, name="glu_dw")
    sent = send(2, gw)
    datt = _mm(db_att, wts["w_att_up"], tb=True, b_shards=True, after=sent, name="att_up_dx")
    stats = _att_stats(datt, att, lse, name="att_stats")
    dqkv = [_dil_bwd(q[g], k[g], v[g], datt, stats, dil, name=f"dil_att_bwd_{dil}") for g, dil in enumerate(DILATIONS)]

    dgy = _mm(dz, wts["w_glu"], tb=True, b_shards=True, name="glu_dx")
    dy, gs["ssm_d"] = _gelu_bwd(dgy, y, proj, name="gelu_bwd")
    du, s_u, dc_re_t, dc_im_t, dbb_re_t, dbb_im_t, da_re, da_im = _ssm_bwd(
        proj, dy, h_re, h_im, tiles_cn, tiles_nc, a_re_rows, a_im_rows, sm["ssm_d"], name="ssm_bwd")
    gs["ssm_c_re"], gs["ssm_c_im"] = dc_re_t, -dc_im_t
    disc_ct = (da_re.reshape(grp), da_im.reshape(grp), dbb_re_t.transpose(1, 0, 2), dbb_im_t.transpose(1, 0, 2))
    d_logdt, gs["ssm_a_re"], gs["ssm_a_im"], d_b_re_t, d_b_im_t = _whole(
        _disc_transpose, disc_in + disc_ct, [(SSM_GROUPS, 1), grp, grp, chn, chn], name="ssm_disc_bwd")
    gs["ssm_log_dt"] = d_logdt
    gs["ssm_b_re"], gs["ssm_b_im"] = d_b_re_t.transpose(1, 2, 0), d_b_im_t.transpose(1, 2, 0)

    dproj, s_qkv = _dproj_assemble(du, dqkv, dgs, dga, cos_t, sin_t, name="dproj_assemble")
    gs["b_in"] = jnp.concatenate([s_u, *s_qkv, s_gs, s_ga], axis=1)
    sent = send_small(gs, SMALL_EARLY)
    gw["w_in"] = wgrad(h0m, dproj, out_shards=True, after=sent, name="in_proj_dw")
    sent = send(3, gw)
    dh0 = _mm(dproj, wts["w_in"], tb=True, b_shards=True, after=sent, name="in_proj_dx")
    grad_x, gs["ln_in_g"], gs["ln_in_b"], _ = _ln_bwd(dr1, dh0, xh0, rs0, sm["ln_in_g"], alpha=DEEPNORM_ALPHA,
                                                      operand=False, name="ln_in_bwd")
    return loss_row, grad_x, gs


N_PEER = N_DEV - 1
_IN_HBM = pl.BlockSpec(memory_space=pltpu.HBM)
_IN_SEMAPHORE = pl.BlockSpec(memory_space=pltpu.SEMAPHORE)


def _device_index():
    return 4 * lax.axis_index("x") + 2 * lax.axis_index("y") + lax.axis_index("c")


ALL_PEERS = tuple(range(1, N_DEV))
NEAR_PEERS = (1, 2, 3, 4, 5)
FAR_PEERS = (6, 7)


def _peer_index(kk):
    x, y, c = lax.axis_index("x"), lax.axis_index("y"), lax.axis_index("c")
    return 4 * ((x + (kk >> 2)) % 2) + 2 * ((y + ((kk >> 1) & 1)) % 2) + (c + (kk & 1)) % 2


def _exchange_copies(src_refs, land_refs, send_sems, recv_sems, scatter, peers):
    x, y, c = lax.axis_index("x"), lax.axis_index("y"), lax.axis_index("c")
    me = 4 * x + 2 * y + c
    pairs = []
    for a, (src_ref, land_ref) in enumerate(zip(src_refs, land_refs)):
        for idx, kk in enumerate(peers):
            px = (x + (kk >> 2)) % 2
            py = (y + ((kk >> 1) & 1)) % 2
            pc = (c + (kk & 1)) % 2
            peer = 4 * px + 2 * py + pc
            sem = a * len(peers) + idx
            src = src_ref.at[peer] if scatter else src_ref

            def copy(dst, src=src, sem=sem, px=px, py=py, pc=pc):
                return pltpu.make_async_remote_copy(
                    src_ref=src, dst_ref=dst, send_sem=send_sems.at[sem], recv_sem=recv_sems.at[sem],
                    device_id=(px, py, pc), device_id_type=pl.DeviceIdType.MESH)

            pairs.append((functools.partial(copy, land_ref.at[me]), functools.partial(copy, land_ref.at[peer])))
    return pairs


def _own_copies(src_refs, land_refs, own_sems, scatter):
    me = _device_index()
    return [functools.partial(pltpu.make_async_copy, src_ref.at[me] if scatter else src_ref, land_ref.at[me],
                              own_sems.at[a]) for a, (src_ref, land_ref) in enumerate(zip(src_refs, land_refs))]


def _exchange_start(srcs, *, scatter, name, after=None, peers=ALL_PEERS, lands=None):
    n_arr, n_sem = len(srcs), len(srcs) * len(peers)
    own = lands is None
    if own:
        lands = [lax.empty((N_DEV,) + tuple(s.shape[1:] if scatter else s.shape), s.dtype) for s in srcs]
    n_in = 2 * n_arr + (after is not None)

    def body(*refs):
        send_sems, recv_sems = refs[n_in], refs[n_in + 1]
        for sent, _ in _exchange_copies(refs[:n_arr], refs[n_arr:2 * n_arr], send_sems, recv_sems, scatter, peers):
            sent().start()
        if own:
            for local in _own_copies(refs[:n_arr], refs[n_arr:2 * n_arr], refs[n_in + 2], scatter):
                local().start()
        refs[-1][...] = jnp.zeros_like(refs[-1])

    sems = [pltpu.SemaphoreType.DMA((n_sem,)), pltpu.SemaphoreType.DMA((n_sem,))] + [pltpu.SemaphoreType.DMA((n_arr,))] * own
    through = [pltpu.HBM(t.shape, t.dtype) for t in (*srcs, *lands)]
    res = pl.pallas_call(
        body, name=name, out_shape=(*sems, *through, jax.ShapeDtypeStruct((8, LANES), F32)),
        in_specs=[_IN_HBM] * (2 * n_arr) + [pl.BlockSpec(memory_space=pl.ANY)] * (after is not None),
        out_specs=(*[_IN_SEMAPHORE] * len(sems), *[_IN_HBM] * (2 * n_arr), pl.BlockSpec(memory_space=pltpu.VMEM)),
        input_output_aliases={i: len(sems) + i for i in range(2 * n_arr)},
        compiler_params=pltpu.CompilerParams(has_side_effects=pltpu.SideEffectType.DATAFLOW_SIDE_EFFECTING),
    )(*[pltpu.with_memory_space_constraint(t, pltpu.HBM) for t in (*srcs, *lands)],
      *([after] if after is not None else []))
    first = len(sems)
    handle = dict(sems=res[:first], srcs=res[first:first + n_arr], lands=res[first + n_arr:first + 2 * n_arr],
                  scatter=scatter, peers=peers, own=own)
    return handle, res[-1]


def _exchange_wait(handle, *, after, name, srcs=None, lands=None):
    srcs = handle["srcs"] if srcs is None else srcs
    lands = handle["lands"] if lands is None else lands
    sems, scatter, peers, own = handle["sems"], handle["scatter"], handle["peers"], handle["own"]
    n_arr = len(srcs)
    after = list(after)

    def body(*refs):
        src_refs, land_refs = refs[:n_arr], refs[n_arr:2 * n_arr]
        for sent, received in _exchange_copies(src_refs, land_refs, refs[2 * n_arr], refs[2 * n_arr + 1], scatter, peers):
            sent().wait_send()
            received().wait_recv()
        if own:
            for local in _own_copies(src_refs, land_refs, refs[2 * n_arr + 2], scatter):
                local().wait()

    res = pl.pallas_call(
        body, name=name, out_shape=tuple(pltpu.HBM(t.shape, t.dtype) for t in (*srcs, *lands)),
        in_specs=[_IN_HBM] * (2 * n_arr) + [_IN_SEMAPHORE] * len(sems) + [pl.BlockSpec(memory_space=pl.ANY)] * len(after),
        out_specs=tuple([_IN_HBM] * (2 * n_arr)), input_output_aliases={i: i for i in range(2 * n_arr)},
        compiler_params=pltpu.CompilerParams(has_side_effects=pltpu.SideEffectType.DATAFLOW_SIDE_EFFECTING),
    )(*srcs, *lands, *sems, *after)
    return res[:n_arr], res[n_arr:]


def _adamw(g, w, m, v):
    m_new = ADAM_B1 * m + (1.0 - ADAM_B1) * g
    v_new = ADAM_B2 * v + (1.0 - ADAM_B2) * jnp.square(g)
    m_hat = m_new / (1.0 - ADAM_B1 ** ADAM_STEP)
    v_hat = v_new / (1.0 - ADAM_B2 ** ADAM_STEP)
    return g, -ADAM_LR * (m_hat / (jnp.sqrt(v_hat) + ADAM_EPS) + ADAM_WD * w), m_new, v_new


def _reduce_adamw(gstack, w, m, v, *, name, tr=128):
    n_rows, cols = w.shape
    tr = min(tr, n_rows)
    assert n_rows % tr == 0, (name, n_rows, tr)

    def body(g_ref, w_ref, m_ref, v_ref, *out_refs):
        g = g_ref[0].astype(F32)
        for dev in range(1, N_DEV):
            g = g + g_ref[dev].astype(F32)
        for o_ref, val in zip(out_refs, _adamw(g, w_ref[...], m_ref[...], v_ref[...])):
            o_ref[...] = val

    flat = pl.BlockSpec((tr, cols), lambda i: (i, 0))
    shape = jax.ShapeDtypeStruct((n_rows, cols), F32)
    return pl.pallas_call(
        body, name=name, grid=(n_rows // tr,),
        in_specs=[pl.BlockSpec((N_DEV, tr, cols), lambda i: (0, i, 0)), flat, flat, flat],
        out_specs=[flat] * 4, out_shape=[shape] * 4, compiler_params=_params("parallel"),
    )(gstack, w, m, v)


SMALL_FLAT_SSM = ("ssm_b_re", "ssm_b_im", "ssm_c_re", "ssm_c_im")


def _small_view(name, shape):
    size = int(np.prod(shape))
    if name in SMALL_FLAT_SSM:
        return SSM_GROUPS, size // SSM_GROUPS
    if name in ("ssm_a_re", "ssm_a_im"):
        return SSM_GROUPS, SSM_STATE
    return 1, size


def _pack_rows(view):
    return -(-(view[0] * view[1]) // PACK_COLS)


SMALL_LATE = ("ln_in_g", "ln_in_b")
SMALL_EARLY = tuple(n for n in SMALL if n not in SMALL_LATE)


def _pack_small(gs, names, views):
    parts = []
    for n in names:
        flat = gs[n].reshape(-1).astype(WIRE_DTYPE)
        parts.append(jnp.pad(flat, (0, _pack_rows(views[n]) * PACK_COLS - flat.shape[0])))
    total = sum(p.shape[0] for p in parts) // PACK_COLS
    parts.append(jnp.zeros(((-total % PACK_ROW_ALIGN) * PACK_COLS,), WIRE_DTYPE))
    return jnp.concatenate(parts).reshape(-1, PACK_COLS)


def _small_pieces(view):
    rows, cols = view
    if cols == PACK_COLS:
        return [(0, rows, 0, 0, 0, cols)]
    if rows == 1 and cols > PACK_COLS:
        return [(kk, 1, 0, 0, kk * PACK_COLS, PACK_COLS) for kk in range(cols // PACK_COLS)]
    if rows == 1:
        return [(0, 1, 0, 0, 0, cols)]
    return [((r * cols) // PACK_COLS, 1, (r * cols) % PACK_COLS, r, 0, cols) for r in range(rows)]


def _adamw_small(stacks, views, w, m, v, *, name):
    n = len(SMALL)
    place, first = {}, [0, 0]
    for k, names in enumerate((SMALL_EARLY, SMALL_LATE)):
        for name_ in names:
            place[name_] = (k, first[k])
            first[k] += _pack_rows(views[name_])

    def body(early_ref, late_ref, *refs):
        ins, outs = refs[:3 * n], refs[3 * n:]
        for i, name_ in enumerate(SMALL):
            stack_ref = (early_ref, late_ref)[place[name_][0]]
            row0 = place[name_][1]
            for prow, nrows, lane, orow, ocol, width in _small_pieces(views[name_]):
                src = (slice(row0 + prow, row0 + prow + nrows), slice(lane, lane + width))
                dst = (slice(orow, orow + nrows), slice(ocol, ocol + width))
                g = stack_ref[(0,) + src].astype(F32)
                for dev in range(1, N_DEV):
                    g = g + stack_ref[(dev,) + src].astype(F32)
                res = _adamw(g, ins[i][dst], ins[n + i][dst], ins[2 * n + i][dst])
                for kk, val in enumerate(res):
                    outs[kk * n + i][dst] = val

    res = pl.pallas_call(
        body, name=name, out_shape=[jax.ShapeDtypeStruct(views[name_], F32) for _ in range(4) for name_ in SMALL],
        compiler_params=pltpu.CompilerParams(vmem_limit_bytes=VMEM_LIMIT_BYTES),
    )(*stacks, *[d[name_] for d in (w, m, v) for name_ in SMALL])
    return [dict(zip(SMALL, res[kk * n:(kk + 1) * n])) for kk in range(4)]


def kernel(x, mem, positions, ln_in_g, ln_in_b, w_in, b_in, ssm_log_dt, ssm_a_re, ssm_a_im, ssm_b_re, ssm_b_im, ssm_c_re, ssm_c_im, ssm_d, w_glu, b_glu, w_att_up, w_mix_out, b_mix_out, ln1_g, ln1_b, w_xq, w_xkv, w_xo, ln2_g, ln2_b, w_ff1, b_ff1, w_ff2, b_ff2, ln3_g, ln3_b, loss_target, m_ln_in_g, m_ln_in_b, m_w_in, m_b_in, m_ssm_log_dt, m_ssm_a_re, m_ssm_a_im, m_ssm_b_re, m_ssm_b_im, m_ssm_c_re, m_ssm_c_im, m_ssm_d, m_w_glu, m_b_glu, m_w_att_up, m_w_mix_out, m_b_mix_out, m_ln1_g, m_ln1_b, m_w_xq, m_w_xkv, m_w_xo, m_ln2_g, m_ln2_b, m_w_ff1, m_b_ff1, m_w_ff2, m_b_ff2, m_ln3_g, m_ln3_b, v_ln_in_g, v_ln_in_b, v_w_in, v_b_in, v_ssm_log_dt, v_ssm_a_re, v_ssm_a_im, v_ssm_b_re, v_ssm_b_im, v_ssm_c_re, v_ssm_c_im, v_ssm_d, v_w_glu, v_b_glu, v_w_att_up, v_w_mix_out, v_b_mix_out, v_ln1_g, v_ln1_b, v_w_xq, v_w_xkv, v_w_xo, v_ln2_g, v_ln2_b, v_w_ff1, v_b_ff1, v_w_ff2, v_b_ff2, v_ln3_g, v_ln3_b):
    given = dict(locals())
    w_arg = {n: given[n] for n in WEIGHTS}
    m_arg = {n: given["m_" + n] for n in WEIGHTS}
    v_arg = {n: given["v_" + n] for n in WEIGHTS}

    shards = {n: w_arg[n][0].astype(MXU_DTYPE) for n in BIG}
    in_near, token = _exchange_start([shards["w_in"]], scatter=False, peers=NEAR_PEERS, name="gather_start_in_near")
    in_far, token = _exchange_start(in_near["srcs"], scatter=False, peers=FAR_PEERS, lands=in_near["lands"],
                                    after=token, name="gather_start_in_far")
    w_in_state = [in_far["srcs"], in_far["lands"]]
    gathers = []
    for i, names in enumerate(GATHER_GROUPS):
        handle, token = _exchange_start([shards[n] for n in names], scatter=False, after=token, name=f"gather_start_{i}")
        gathers.append(handle)

    small_views = {n: _small_view(n, w_arg[n].shape) for n in SMALL}
    small_w, small_m, small_v = [{n: d[n].reshape(small_views[n]) for n in SMALL} for d in (w_arg, m_arg, v_arg)]
    relaid = [d[n] for d in (small_w, small_m, small_v) for n in SMALL_FLAT_SSM]

    def fetch_in(part, after):
        handle, peers, tag = ((in_near, (0,) + NEAR_PEERS, "near"), (in_far, FAR_PEERS, "far"))[part]
        w_in_state[:] = _exchange_wait(handle, after=after + (relaid if part == 0 else []), srcs=w_in_state[0],
                                       lands=w_in_state[1], name="gather_wait_in_" + tag)
        return w_in_state[1][0], jnp.stack([_peer_index(kk) for kk in peers]).astype(jnp.int32)

    def fetch(i, after):
        _, lands = _exchange_wait(gathers[i], after=after, name=f"gather_wait_{i}")
        full = dict(zip(GATHER_GROUPS[i], lands))
        return {n: t if n in BIG_COL_SHARDED else t.reshape(-1, t.shape[-1]) for n, t in full.items()}

    scatters = {}

    def send(i, gw):
        slots = [gw[n] if n in BIG_COL_SHARDED else gw[n].reshape(N_DEV, -1, gw[n].shape[-1]) for n in SCATTER_GROUPS[i]]
        handle, sent = _exchange_start(slots, scatter=True, name=f"scatter_start_{i}")
        scatters[i] = (handle, slots)
        return sent

    sm = {}
    for n in SMALL:
        t = w_arg[n]
        if n.startswith("ssm_") and n not in ("ssm_d", "ssm_log_dt"):
            sm[n] = t[0]
        else:
            sm[n] = t.reshape(1, -1)

    smalls = []

    def send_small(gs, names):
        handle, sent = _exchange_start([_pack_small(gs, names, small_views)], scatter=False,
                                       name=f"small_start_{len(smalls)}")
        smalls.append(handle)
        return sent

    loss_row, grad_x, gs = _local_grads(x[0], mem[0], positions.reshape(-1, 1), loss_target[0], sm, fetch_in, fetch,
                                        send, send_small, token)
    loss = lax.psum(loss_row[0, 0], ("x", "y", "c"))
    send_small(gs, SMALL_LATE)

    results = [{}, {}, {}, {}]
    done = grad_x
    for i, names in enumerate(SCATTER_GROUPS):
        handle, slots = scatters[i]
        _, lands = _exchange_wait(handle, after=[done], name=f"scatter_wait_{i}")
        for n, land, slot in zip(names, lands, slots):
            res = _reduce_adamw(land, w_arg[n][0], m_arg[n][0], v_arg[n][0], name="adamw_" + n)
            done = res[0]
            for d, r in zip(results, res):
                d[n] = r[None]
    stacks = [_exchange_wait(handle, after=[done], name=f"small_wait_{i}")[1][0] for i, handle in enumerate(smalls)]
    res = _adamw_small(stacks, small_views, small_w, small_m, small_v, name="adamw_small")
    for d, r in zip(results, res):
        d.update({n: r[n].reshape(w_arg[n].shape) for n in SMALL})
    out = [loss, grad_x[None]]
    for d in results:
        out += [d[n] for n in WEIGHTS]
    return tuple(out)
```

```python
import functools

import numpy as np
import jax
import jax.numpy as jnp
from jax import lax
from jax.experimental import pallas as pl
from jax.experimental.pallas import tpu as pltpu

F32 = jnp.float32
MXU_DTYPE = jnp.bfloat16
WIRE_DTYPE = jnp.bfloat16
VMEM_LIMIT_BYTES = 48 * 1024 * 1024
LANES = 128

N_DEV = 8
D_MODEL = 1024
SSM_GROUP = 16
SSM_WIDTH = 768
SSM_GROUPS = SSM_WIDTH // SSM_GROUP
SSM_STATE = 64
SSM_CH = SSM_GROUPS * SSM_STATE
SSM_TILES = SSM_WIDTH // LANES
GROUPS_PER_TILE = LANES // SSM_GROUP
STATE_VREG_ROWS = SSM_CH // LANES
ATT_HEAD_DIM = 64
ATT_HEADS_PER_GROUP = 4
ATT_MERGED = ATT_HEADS_PER_GROUP * ATT_HEAD_DIM
LANE_HALVES = ATT_MERGED // LANES
DILATIONS = (1, 4, 16)
ATT_BLK = 128
ATT_SCALE = ATT_HEAD_DIM ** -0.5
ROT_DIM = ATT_HEAD_DIM // 4
ROPE_THETA = 500000.0
XATT_HEADS = 4
XATT_HEAD_DIM = D_MODEL // XATT_HEADS
XATT_SCALE = XATT_HEAD_DIM ** -0.5
DEEPNORM_ALPHA = 2.0 ** 0.25
LN_EPS = 1e-5
NEG_INF = -1e30
OFF_Q_BLK, OFF_K_BLK, OFF_V_BLK = 3, 6, 9
OFF_GS_BLK, OFF_GA_BLK = 3, 4

ADAM_LR = 0.001
ADAM_B1 = 0.9
ADAM_B2 = 0.999
ADAM_EPS = 1e-08
ADAM_WD = 0.01
ADAM_STEP = 10

BIG = ("w_in", "w_glu", "w_att_up", "w_mix_out", "w_xq", "w_xkv", "w_xo", "w_ff1", "w_ff2")
BIG_COL_SHARDED = ("w_in", "w_glu", "w_att_up", "w_xkv", "w_ff1")
WEIGHTS = ("ln_in_g", "ln_in_b", "w_in", "b_in", "ssm_log_dt", "ssm_a_re", "ssm_a_im", "ssm_b_re", "ssm_b_im",
           "ssm_c_re", "ssm_c_im", "ssm_d", "w_glu", "b_glu", "w_att_up", "w_mix_out", "b_mix_out", "ln1_g", "ln1_b",
           "w_xq", "w_xkv", "w_xo", "ln2_g", "ln2_b", "w_ff1", "b_ff1", "w_ff2", "b_ff2", "ln3_g", "ln3_b")
SMALL = tuple(n for n in WEIGHTS if n not in BIG)
PACK_COLS = 1024
PACK_ROW_ALIGN = 16


def _params(*sem):
    return pltpu.CompilerParams(dimension_semantics=sem, vmem_limit_bytes=VMEM_LIMIT_BYTES)


def _dot(a, b, ca, cb):
    return lax.dot_general(a.astype(MXU_DTYPE), b.astype(MXU_DTYPE), (((ca,), (cb,)), ((), ())),
                           preferred_element_type=F32)


def _fit(dim, pref):
    if dim <= pref:
        return dim
    best = max(t for t in range(LANES, pref + 1, LANES) if dim % t == 0)
    return best


def _mm(a, b, *, name, ta=False, tb=False, bias=None, out_dtype=F32, b_shards=False, out_shards=False, after=None,
        also=None, gate=None, colsum=False, tm=2048, tn=1024, tk=1024):
    m, k = (a.shape[1], a.shape[0]) if ta else a.shape
    order = (lambda f: (lambda j, i, kk: f(i, j, kk))) if colsum else (lambda f: f)
    spec = lambda shape, f: pl.BlockSpec(shape, order(f))
    if b_shards:
        n_sh, rows, n_loc = b.shape
        if tb:
            n, tn, tk = rows, _fit(rows, tn), n_loc
            assert k == n_sh * n_loc, (name, k, b.shape)
            b_spec = spec((1, tn, tk), lambda i, j, kk: (kk, j, 0))
        else:
            n, tn, tk = n_sh * n_loc, n_loc, _fit(k, tk)
            b_spec = spec((1, tk, tn), lambda i, j, kk: (j, kk, 0))
    else:
        n = b.shape[0] if tb else b.shape[1]
        tn = n // N_DEV if out_shards else _fit(n, tn)
        tk = _fit(k, tk)
        b_spec = spec((tn, tk), lambda i, j, kk: (j, kk)) if tb else spec((tk, tn), lambda i, j, kk: (kk, j))
    tm = _fit(m, tm)
    nk = k // tk
    a_spec = spec((tk, tm), lambda i, j, kk: (kk, i)) if ta else spec((tm, tk), lambda i, j, kk: (i, kk))
    tile = spec((tm, tn), lambda i, j, kk: (i, j))
    in_specs, args = [a_spec, b_spec], [a, b]
    if bias is not None:
        in_specs.append(spec((1, tn), lambda i, j, kk: (0, j)))
        args.append(bias)
    if gate is not None:
        in_specs.append(tile)
        args.append(gate[0])
    if after is not None:
        in_specs.append(pl.BlockSpec(memory_space=pl.ANY))
        args.append(after)
    n_in = len(args)
    if out_shards:
        assert n == N_DEV * tn, (name, n, tn)
        out_specs = [spec((1, tm, tn), lambda i, j, kk: (j, i, 0))]
        out_shape = [jax.ShapeDtypeStruct((N_DEV, m, tn), out_dtype)]
    else:
        out_specs = [tile]
        out_shape = [jax.ShapeDtypeStruct((m, n), out_dtype)]
    if also is not None:
        out_specs.append(tile)
        out_shape.append(jax.ShapeDtypeStruct((m, n), also[1]))
    if colsum:
        out_specs.append(spec((1, tn), lambda i, j, kk: (0, j)))
        out_shape.append(jax.ShapeDtypeStruct((1, n), F32))

    def body(*refs):
        a_ref, b_ref = refs[0], refs[1]
        o_ref = refs[n_in]

        def product():
            return _dot(a_ref[...], b_ref[0] if b_shards else b_ref[...], 0 if ta else 1, 1 if tb else 0)

        def finish(r):
            if bias is not None:
                r = r + refs[2][...]
            if gate is not None:
                r = r * gate[1](refs[2 + (bias is not None)][...])
            if out_shards:
                o_ref[0] = r.astype(o_ref.dtype)
            else:
                o_ref[...] = r.astype(o_ref.dtype)
            if also is not None:
                refs[n_in + 1][...] = also[0](r).astype(also[1])
            if colsum:
                s_ref = refs[n_in + 1 + (also is not None)]

                @pl.when(pl.program_id(1) == 0)
                def _():
                    s_ref[...] = jnp.zeros_like(s_ref)

                s_ref[...] += _colsum(r)

        if nk == 1:
            finish(product())
            return
        acc_ref = refs[-1]
        kk = pl.program_id(2)

        @pl.when(kk == 0)
        def _():
            acc_ref[...] = jnp.zeros_like(acc_ref)

        acc_ref[...] += product()

        @pl.when(kk == nk - 1)
        def _():
            finish(acc_ref[...])

    grid = (n // tn, m // tm, nk) if colsum else (m // tm, n // tn, nk)
    res = pl.pallas_call(
        body, name=name, grid=grid, in_specs=in_specs, out_specs=out_specs, out_shape=out_shape,
        scratch_shapes=[pltpu.VMEM((tm, tn), F32)] if nk > 1 else [],
        compiler_params=_params("parallel", "arbitrary" if colsum else "parallel", "arbitrary"),
    )(*args)
    return res[0] if len(res) == 1 else res


def _mm_shards(a, w, bias, shard_ids, *, name, prev=None, tm=2048):
    m, k = a.shape
    n_sh, _, n_loc = w.shape
    tm = _fit(m, tm)

    def body(ids_ref, a_ref, w_ref, b_ref, *rest):
        rest[-1][...] = _dot(a_ref[...], w_ref[0], 1, 0) + b_ref[...]

    grid_spec = pltpu.PrefetchScalarGridSpec(
        num_scalar_prefetch=1, grid=(m // tm, shard_ids.shape[0]),
        in_specs=[pl.BlockSpec((tm, k), lambda i, j, ids: (i, 0)),
                  pl.BlockSpec((1, k, n_loc), lambda i, j, ids: (ids[j], 0, 0)),
                  pl.BlockSpec((1, n_loc), lambda i, j, ids: (0, ids[j]))]
        + [pl.BlockSpec(memory_space=pl.ANY)] * (prev is not None),
        out_specs=pl.BlockSpec((tm, n_loc), lambda i, j, ids: (i, ids[j])))
    return pl.pallas_call(
        body, name=name, grid_spec=grid_spec, out_shape=jax.ShapeDtypeStruct((m, n_sh * n_loc), F32),
        input_output_aliases={4: 0} if prev is not None else {}, compiler_params=_params("parallel", "arbitrary"),
    )(shard_ids, a, w, bias, *([prev] if prev is not None else []))


def _rowcall(fn, rows, fulls, row_outs, acc_outs=(), *, n_rows, tm, name, after=None):
    n_r, n_f, n_o, n_a = len(rows), len(fulls), len(row_outs), len(acc_outs)
    n_in = n_r + n_f + (after is not None)
    assert n_rows % tm == 0, (name, n_rows, tm)

    def body(*refs):
        res = fn(*[r[...] for r in refs[:n_r + n_f]])
        res = tuple(res) if isinstance(res, (tuple, list)) else (res,)
        o_refs = refs[n_in:n_in + n_o]
        a_refs = refs[n_in + n_o:]
        for o_ref, val in zip(o_refs, res[:n_o]):
            o_ref[...] = val.astype(o_ref.dtype)
        if n_a:
            @pl.when(pl.program_id(0) == 0)
            def _():
                for a_ref in a_refs:
                    a_ref[...] = jnp.zeros_like(a_ref)

            for a_ref, val in zip(a_refs, res[n_o:]):
                a_ref[...] += val

    in_specs = [pl.BlockSpec((tm, w), functools.partial(lambda i, cb: (i, cb), cb=cb)) for _, w, cb in rows]
    in_specs += [pl.BlockSpec(f.shape, functools.partial(lambda i, nd: (0,) * nd, nd=f.ndim)) for f in fulls]
    in_specs += [pl.BlockSpec(memory_space=pl.ANY)] * (after is not None)
    out_specs = [pl.BlockSpec((tm, w), lambda i: (i, 0)) for w, _ in row_outs]
    out_specs += [pl.BlockSpec((1, w), lambda i: (0, 0)) for w in acc_outs]
    out_shape = [jax.ShapeDtypeStruct((n_rows, w), dt) for w, dt in row_outs]
    out_shape += [jax.ShapeDtypeStruct((1, w), F32) for w in acc_outs]
    return pl.pallas_call(
        body, name=name, grid=(n_rows // tm,), in_specs=in_specs, out_specs=out_specs, out_shape=out_shape,
        compiler_params=_params("arbitrary" if n_a else "parallel"),
    )(*[r[0] for r in rows], *fulls, *([after] if after is not None else []))


def _colsum(v):
    return jnp.sum(v, axis=0, keepdims=True)


def _ln_fwd(a, r, g, b, *, alpha, name):
    n_rows, d = a.shape

    def fn(*t):
        xin = t[0] if alpha == 1.0 else alpha * t[0]
        if r is not None:
            xin = xin + t[1]
        gv, bv = t[-2], t[-1]
        mu = jnp.mean(xin, axis=-1, keepdims=True)
        xc = xin - mu
        var = jnp.mean(xc * xc, axis=-1, keepdims=True)
        rstd = lax.rsqrt(var + LN_EPS)
        xh = xc * rstd
        y = xh * gv + bv
        return y, xh, rstd, y

    rows = [(a, d, 0)] + ([(r, d, 0)] if r is not None else [])
    return _rowcall(fn, rows, [g, b], [(d, F32), (d, F32), (1, F32), (d, MXU_DTYPE)], n_rows=n_rows, tm=256, name=name)


def _ln_bwd(dya, dyb, xh, rstd, g, *, alpha, name, operand=True):
    n_rows, d = xh.shape

    def fn(da, db, xhv, rs, gv):
        dy = alpha * da + db
        dyg = dy * gv
        m1 = jnp.mean(dyg, axis=-1, keepdims=True)
        m2 = jnp.mean(dyg * xhv, axis=-1, keepdims=True)
        dx = rs * (dyg - m1 - xhv * m2)
        return (dx,) + ((dx,) if operand else ()) + (_colsum(dy * xhv), _colsum(dy), _colsum(dx))

    rows = [(dya, d, 0), (dyb, d, 0), (xh, d, 0), (rstd, 1, 0)]
    return _rowcall(fn, rows, [g], [(d, F32)] + [(d, MXU_DTYPE)] * operand, [d, d, d], n_rows=n_rows, tm=256, name=name)


def _ln_loss_bwd(a, r, target, g, b, *, alpha, name):
    n_rows, d = a.shape

    def fn(av, rv, tv, gv, bv):
        xin = alpha * av + rv
        mu = jnp.mean(xin, axis=-1, keepdims=True)
        xc = xin - mu
        var = jnp.mean(xc * xc, axis=-1, keepdims=True)
        rs = lax.rsqrt(var + LN_EPS)
        xh = xc * rs
        diff = xh * gv + bv - tv
        part = jnp.sum(jnp.sum(diff * diff, axis=1, keepdims=True), axis=0, keepdims=True) * (0.5 / d)
        dy = diff * (1.0 / d)
        dyg = dy * gv
        m1 = jnp.mean(dyg, axis=-1, keepdims=True)
        m2 = jnp.mean(dyg * xh, axis=-1, keepdims=True)
        dx = rs * (dyg - m1 - xh * m2)
        return dx, dx, _colsum(dy * xh), _colsum(dy), _colsum(dx), jnp.broadcast_to(part, (1, LANES))

    return _rowcall(fn, [(a, d, 0), (r, d, 0), (target, d, 0)], [g, b], [(d, F32), (d, MXU_DTYPE)], [d, d, d, LANES],
                    n_rows=n_rows, tm=256, name=name)


def _rope_lane_constants():
    lane = np.arange(ATT_MERGED)
    in_head = lane % ATT_HEAD_DIM
    sign = np.where(in_head < ROT_DIM // 2, -1.0, np.where(in_head < ROT_DIM, 1.0, 0.0)).astype(np.float32)
    inv_freq = ROPE_THETA ** (-jnp.arange(0, ROT_DIM, 2, dtype=F32) / ROT_DIM)
    return inv_freq[lane % (ROT_DIM // 2)].reshape(1, ATT_MERGED), jnp.asarray(sign).reshape(1, ATT_MERGED)


def _rope_tables(pos_col, *, name, after=None):
    inv_lane, sign = _rope_lane_constants()

    def fn(pos, inv, sg):
        ang = pos.astype(F32) * inv
        return jnp.where(sg != 0.0, jnp.cos(ang), 1.0), sg * jnp.sin(ang)

    return _rowcall(fn, [(pos_col, 1, 0)], [inv_lane, sign], [(ATT_MERGED, F32), (ATT_MERGED, F32)],
                    n_rows=pos_col.shape[0], tm=512, name=name, after=after)


def _rot_partner(t):
    lane = lax.broadcasted_iota(jnp.int32, t.shape, 1)
    width = t.shape[1]
    return jnp.where((lane & (ROT_DIM // 2)) == 0, pltpu.roll(t, width - ROT_DIM // 2, 1), pltpu.roll(t, ROT_DIM // 2, 1))


def _rope(t, cos_t, sin_t):
    return t * cos_t + _rot_partner(t) * sin_t


def _rope_transpose(dt, cos_t, sin_t):
    return dt * cos_t + _rot_partner(dt * sin_t)


def _strided_rows(r, count, stride):
    return pl.ds(r, count) if stride == 1 else pl.ds(r, count, stride=stride)


def _qkv_split(proj, cos_t, sin_t, *, name, tm=512):
    n_rows = proj.shape[0]
    n_g = len(DILATIONS)

    def body(*refs):
        n_src = LANE_HALVES * 3 * n_g
        src, tables, dst = refs[:n_src], refs[n_src:n_src + 2 * LANE_HALVES], refs[n_src + 2 * LANE_HALVES:]
        for kind in range(3):
            for g, dil in enumerate(DILATIONS):
                for half in range(LANE_HALVES):
                    x_ref, o_ref = src[(kind * n_g + g) * LANE_HALVES + half], dst[kind * n_g + g]
                    cos_ref, sin_ref = tables[half], tables[LANE_HALVES + half]
                    for r in range(dil):
                        rows = _strided_rows(r, tm // dil, dil)
                        t = x_ref[rows, :]
                        if kind < 2:
                            t = _rope(t, cos_ref[rows, :], sin_ref[rows, :])
                        lo = r * ATT_MERGED + half * LANES
                        o_ref[:, lo:lo + LANES] = t.astype(o_ref.dtype)

    half_spec = lambda cb: pl.BlockSpec((tm, LANES), functools.partial(lambda i, cb: (i, cb), cb=cb))
    in_specs = [half_spec((off + g) * LANE_HALVES + half)
                for off in (OFF_Q_BLK, OFF_K_BLK, OFF_V_BLK) for g in range(n_g) for half in range(LANE_HALVES)]
    in_specs += [half_spec(half) for _ in range(2) for half in range(LANE_HALVES)]
    out_specs = [pl.BlockSpec((tm // dil, dil * ATT_MERGED), lambda i: (i, 0)) for _ in range(3) for dil in DILATIONS]
    out_shape = [jax.ShapeDtypeStruct((n_rows // dil, dil * ATT_MERGED), MXU_DTYPE) for _ in range(3) for dil in DILATIONS]
    outs = pl.pallas_call(
        body, name=name, grid=(n_rows // tm,), in_specs=in_specs, out_specs=out_specs, out_shape=out_shape,
        compiler_params=_params("parallel"),
    )(*[proj] * (LANE_HALVES * 3 * n_g), *[cos_t] * LANE_HALVES, *[sin_t] * LANE_HALVES)
    return outs[:n_g], outs[n_g:2 * n_g], outs[2 * n_g:]


def _mix(gs, ga, z1, z2, b_att):
    return jax.nn.sigmoid(gs) * (z1 * jax.nn.sigmoid(z2)) + jax.nn.sigmoid(ga) * b_att


def _mix_rows(proj, z, b_att):
    return [(proj, D_MODEL, OFF_GS_BLK), (proj, D_MODEL, OFF_GA_BLK), (z, D_MODEL, 0), (z, D_MODEL, 1), (b_att, D_MODEL, 0)]


def _mix_fwd(proj, z, b_att, *, name):
    return _rowcall(_mix, _mix_rows(proj, z, b_att), [], [(D_MODEL, MXU_DTYPE)],
                    n_rows=proj.shape[0], tm=256, name=name)[0]


def _mix_bwd(dmixed, proj, z, b_att, *, name):
    def fn(dm, gs, ga, z1, z2, ba):
        _, vjp = jax.vjp(_mix, gs, ga, z1, z2, ba)
        dgs, dga, dz1, dz2, dba = vjp(dm)
        dz = jnp.concatenate([dz1, dz2], axis=1)
        return dgs, dga, dz, dba, _colsum(dgs), _colsum(dga), _colsum(dz)

    rows = [(dmixed, D_MODEL, 0)] + _mix_rows(proj, z, b_att)
    widths = [D_MODEL, D_MODEL, 2 * D_MODEL, D_MODEL]
    return _rowcall(fn, rows, [], [(w, MXU_DTYPE) for w in widths], widths[:3], n_rows=proj.shape[0], tm=256, name=name)


def _gelu_bwd(dgy, y, proj, *, name):
    def fn(dg, yv, u):
        _, vjp = jax.vjp(jax.nn.gelu, yv)
        dy = vjp(dg)[0]
        return dy, _colsum(dy * u)

    return _rowcall(fn, [(dgy, SSM_WIDTH, 0), (y, SSM_WIDTH, 0), (proj, SSM_WIDTH, 0)], [], [(SSM_WIDTH, F32)],
                    [SSM_WIDTH], n_rows=y.shape[0], tm=512, name=name)


HEAD_ROWS = ATT_HEADS_PER_GROUP * ATT_BLK


def _head_masks(rows):
    head = lax.broadcasted_iota(jnp.int32, (rows, ATT_MERGED), 1) >> (ATT_HEAD_DIM.bit_length() - 1)
    return [head == h for h in range(ATT_HEADS_PER_GROUP)]


def _stack_heads(t, masks):
    return jnp.concatenate([jnp.where(m, t, jnp.zeros_like(t)) for m in masks], axis=0)


def _unstack_heads(t4, masks):
    blocks = [t4[h * ATT_BLK:(h + 1) * ATT_BLK] for h in range(ATT_HEADS_PER_GROUP)]
    return jnp.where(masks[0], blocks[0], jnp.where(masks[1], blocks[1], jnp.where(masks[2], blocks[2], blocks[3])))


def _head_column(stats, first):
    return jnp.concatenate([stats[:, first + h:first + h + 1] for h in range(ATT_HEADS_PER_GROUP)], axis=0)


def _band_mask(first_key):
    qi = lax.broadcasted_iota(jnp.int32, (HEAD_ROWS, 2 * ATT_BLK), 0) & (ATT_BLK - 1)
    ki = lax.broadcasted_iota(jnp.int32, (HEAD_ROWS, 2 * ATT_BLK), 1)
    steps = qi + ATT_BLK - ki
    return (steps >= 0) & (steps <= ATT_BLK) & (ki >= first_key)


def _dil_fwd(q, k, v, dil, *, name):
    n_blk = q.shape[0] // ATT_BLK
    cur = pl.BlockSpec((ATT_BLK, ATT_MERGED), lambda r, n: (n, r))
    prev = pl.BlockSpec((ATT_BLK, ATT_MERGED), lambda r, n: (jnp.maximum(n - 1, 0), r))

    def body(q_ref, kp_ref, kc_ref, vp_ref, vc_ref, o_ref, l_ref):
        masks = _head_masks(ATT_BLK)
        valid = _band_mask(jnp.where(pl.program_id(1) > 0, 0, ATT_BLK))
        keys = jnp.concatenate([kp_ref[...], kc_ref[...]], axis=0)
        vals = jnp.concatenate([vp_ref[...], vc_ref[...]], axis=0)
        s = jnp.where(valid, _dot(_stack_heads(q_ref[...], masks), keys, 1, 1) * ATT_SCALE, NEG_INF)
        m = jnp.max(s, axis=-1, keepdims=True)
        p = jnp.exp(s - m)
        den = jnp.sum(p, axis=-1, keepdims=True)
        o_ref[...] = _unstack_heads(_dot(p, vals, 1, 0) / den, masks)
        l_ref[...] = _unstack_heads(jnp.broadcast_to(m + jnp.log(den), (HEAD_ROWS, ATT_MERGED)), masks)

    shape = jax.ShapeDtypeStruct(q.shape, F32)
    return pl.pallas_call(
        body, name=name, grid=(dil, n_blk), in_specs=[cur, prev, cur, prev, cur], out_specs=[cur, cur],
        out_shape=[shape, shape], compiler_params=_params("parallel", "parallel"),
    )(q, k, k, v, v)


def _att_merge(outs, lses, *, name, tm=512):
    n_g = len(outs)
    n_rows = outs[0].shape[0] * DILATIONS[0]

    def body(*refs):
        src, (att_ref, lse_ref), tmp = refs[:2 * n_g], refs[2 * n_g:2 * n_g + 2], refs[2 * n_g + 2:]
        vals = []
        for idx, src_ref in enumerate(src):
            dil = DILATIONS[idx % n_g]
            if dil == 1:
                vals.append(src_ref[...])
                continue
            for r in range(dil):
                for half in range(LANE_HALVES):
                    lo = r * ATT_MERGED + half * LANES
                    tmp[LANE_HALVES * idx + half][_strided_rows(r, tm // dil, dil), :] = src_ref[:, lo:lo + LANES]
            vals.append(jnp.concatenate([tmp[LANE_HALVES * idx + half][...] for half in range(LANE_HALVES)], axis=1))
        o, l = vals[:n_g], vals[n_g:]
        m = functools.reduce(jnp.maximum, l)
        e = [jnp.exp(li - m) for li in l]
        z = functools.reduce(jnp.add, e)
        att_ref[...] = functools.reduce(jnp.add, [(ei / z) * oi for ei, oi in zip(e, o)])
        lse_ref[...] = m + jnp.log(z)

    in_specs = [pl.BlockSpec((tm // dil, dil * ATT_MERGED), lambda i: (i, 0)) for _ in range(2) for dil in DILATIONS]
    row = pl.BlockSpec((tm, ATT_MERGED), lambda i: (i, 0))
    shape = jax.ShapeDtypeStruct((n_rows, ATT_MERGED), F32)
    return pl.pallas_call(
        body, name=name, grid=(n_rows // tm,), in_specs=in_specs, out_specs=[row, row], out_shape=[shape, shape],
        scratch_shapes=[pltpu.VMEM((tm, LANES), F32)] * (LANE_HALVES * 2 * n_g), compiler_params=_params("parallel"),
    )(*outs, *lses)


def _att_stats(datt, att, lse, *, name):
    n_rows = datt.shape[0]

    def fn(d, a, l):
        prod = d * a
        lane = lax.broadcasted_iota(jnp.int32, (d.shape[0], LANES), 1)
        out = jnp.zeros((d.shape[0], LANES), F32)
        for h in range(ATT_HEADS_PER_GROUP):
            lo = h * ATT_HEAD_DIM
            out = jnp.where(lane == h, l[:, lo:lo + 1], out)
            delta = jnp.sum(prod[:, lo:lo + ATT_HEAD_DIM], axis=-1, keepdims=True)
            out = jnp.where(lane == ATT_HEADS_PER_GROUP + h, delta, out)
        return out

    rows = [(t, ATT_MERGED, 0) for t in (datt, att, lse)]
    return _rowcall(fn, rows, [], [(LANES, F32)], n_rows=n_rows, tm=512, name=name)[0]


def _dil_bwd(q, k, v, datt, stats, dil, *, name):
    n_rows = datt.shape[0]
    n_blk = n_rows // dil // ATT_BLK
    span = ATT_BLK * dil
    cur = pl.BlockSpec((ATT_BLK, ATT_MERGED), lambda n, r: (n, r))
    prev = pl.BlockSpec((ATT_BLK, ATT_MERGED), lambda n, r: (jnp.maximum(n - 1, 0), r))
    nxt = pl.BlockSpec((ATT_BLK, ATT_MERGED), lambda n, r: (jnp.minimum(n + 1, n_blk - 1), r))
    seq = lambda half, ahead: pl.BlockSpec((span, LANES), lambda n, r: (jnp.minimum(n + ahead, n_blk - 1), half))

    def body(qc_ref, qn_ref, kp_ref, kc_ref, vp_ref, vc_ref, dc0_ref, dc1_ref, dn0_ref, dn1_ref, sc_ref, sn_ref,
             dq0_ref, dq1_ref, dk0_ref, dk1_ref, dv0_ref, dv1_ref):
        n = pl.program_id(0)
        rows = slice(None) if dil == 1 else _strided_rows(pl.program_id(1), ATT_BLK, dil)

        def read(ref0, ref1):
            return jnp.concatenate([ref0[rows, :], ref1[rows, :]], axis=1)

        def write(ref0, ref1, val):
            ref0[rows, :] = val[:, :LANES]
            ref1[rows, :] = val[:, LANES:]

        masks = _head_masks(ATT_BLK)
        valid = _band_mask(jnp.where(n > 0, 0, ATT_BLK))
        qi = lax.broadcasted_iota(jnp.int32, (HEAD_ROWS, ATT_BLK), 0) & (ATT_BLK - 1)
        ki = lax.broadcasted_iota(jnp.int32, (HEAD_ROWS, ATT_BLK), 1)
        valid_next = (ki - qi) >= jnp.where(n < n_blk - 1, 0, ATT_BLK)

        kc, vc = kc_ref[...], vc_ref[...]
        keys = jnp.concatenate([kp_ref[...], kc], axis=0)
        vals = jnp.concatenate([vp_ref[...], vc], axis=0)
        q4 = _stack_heads(qc_ref[...], masks)
        d4 = _stack_heads(read(dc0_ref, dc1_ref).astype(MXU_DTYPE), masks)
        st = sc_ref[rows, :]
        p = jnp.where(valid, jnp.exp(_dot(q4, keys, 1, 1) * ATT_SCALE - _head_column(st, 0)), 0.0)
        ds = p * (_dot(d4, vals, 1, 1) - _head_column(st, ATT_HEADS_PER_GROUP)) * ATT_SCALE
        write(dq0_ref, dq1_ref, _unstack_heads(_dot(ds, keys, 1, 0), masks))

        q4n = _stack_heads(qn_ref[...], masks)
        d4n = _stack_heads(read(dn0_ref, dn1_ref).astype(MXU_DTYPE), masks)
        stn = sn_ref[rows, :]
        p_n = jnp.where(valid_next, jnp.exp(_dot(q4n, kc, 1, 1) * ATT_SCALE - _head_column(stn, 0)), 0.0)
        ds_n = p_n * (_dot(d4n, vc, 1, 1) - _head_column(stn, ATT_HEADS_PER_GROUP)) * ATT_SCALE
        write(dv0_ref, dv1_ref, _dot(p[:, ATT_BLK:], d4, 0, 0) + _dot(p_n, d4n, 0, 0))
        write(dk0_ref, dk1_ref, _dot(ds[:, ATT_BLK:], q4, 0, 0) + _dot(ds_n, q4n, 0, 0))

    shape = jax.ShapeDtypeStruct((n_rows, LANES), F32)
    out = seq(0, 0)
    res = pl.pallas_call(
        body, name=name, grid=(n_blk, dil),
        in_specs=[cur, nxt, prev, cur, prev, cur, seq(0, 0), seq(1, 0), seq(0, 1), seq(1, 1), seq(0, 0), seq(0, 1)],
        out_specs=[out] * 6, out_shape=[shape] * 6, compiler_params=_params("parallel", "arbitrary"),
    )(q, q, k, k, v, v, datt, datt, datt, datt, stats, stats)
    return [(res[2 * i], res[2 * i + 1]) for i in range(3)]


def _dproj_assemble(du, dqkv, dgs, dga, cos_t, sin_t, *, name):
    n_g = len(DILATIONS)

    def fn(*t):
        n_half = LANE_HALVES * 3 * n_g
        du_t, halves, (dgs_t, dga_t, c, s) = t[0], t[1:1 + n_half], t[1 + n_half:]
        parts = [jnp.concatenate(halves[LANE_HALVES * i:LANE_HALVES * (i + 1)], axis=1) for i in range(3 * n_g)]
        for i in range(2 * n_g):
            parts[i] = _rope_transpose(parts[i], c, s)
        cast = [p.astype(MXU_DTYPE) for p in parts]
        return [jnp.concatenate([du_t] + cast + [dgs_t, dga_t], axis=1)] + [_colsum(p) for p in parts]

    rows = [(du, SSM_WIDTH, 0)]
    rows += [(half, LANES, 0) for i in range(3) for g in range(n_g) for half in dqkv[g][i]]
    rows += [(dgs, D_MODEL, 0), (dga, D_MODEL, 0), (cos_t, ATT_MERGED, 0), (sin_t, ATT_MERGED, 0)]
    width = SSM_WIDTH + 3 * n_g * ATT_MERGED + 2 * D_MODEL
    res = _rowcall(fn, rows, [], [(width, MXU_DTYPE)], [ATT_MERGED] * (3 * n_g), n_rows=du.shape[0], tm=256, name=name)
    return res[0], res[1:]


def _xhead(h):
    return slice(h * XATT_HEAD_DIM, (h + 1) * XATT_HEAD_DIM)


def _xatt_probs(qh, kh):
    s = _dot(qh, kh, 1, 1) * XATT_SCALE
    e = jnp.exp(s - jnp.max(s, axis=-1, keepdims=True))
    return e / jnp.sum(e, axis=-1, keepdims=True)


def _xatt_fwd(q, kv, *, name, tm=512):
    n_rows = q.shape[0]
    n_mem = kv.shape[0]

    def body(q_ref, kv_ref, o_ref):
        for h in range(XATT_HEADS):
            sl = _xhead(h)
            p = _xatt_probs(q_ref[:, sl], kv_ref[:, sl])
            o_ref[:, sl] = _dot(p, kv_ref[:, D_MODEL + h * XATT_HEAD_DIM:D_MODEL + (h + 1) * XATT_HEAD_DIM], 1, 0
                                ).astype(o_ref.dtype)

    row = pl.BlockSpec((tm, D_MODEL), lambda i: (i, 0))
    return pl.pallas_call(
        body, name=name, grid=(n_rows // tm,),
        in_specs=[row, pl.BlockSpec((n_mem, 2 * D_MODEL), lambda i: (0, 0))], out_specs=row,
        out_shape=jax.ShapeDtypeStruct((n_rows, D_MODEL), MXU_DTYPE), compiler_params=_params("parallel"),
    )(q, kv)


def _xatt_bwd(q, kv, do, *, name, tm=512):
    n_rows = q.shape[0]
    n_mem = kv.shape[0]

    def body(q_ref, kv_ref, do_ref, dq_ref, dkv_ref):
        @pl.when(pl.program_id(0) == 0)
        def _():
            dkv_ref[...] = jnp.zeros_like(dkv_ref)

        for h in range(XATT_HEADS):
            sl = _xhead(h)
            vsl = slice(D_MODEL + h * XATT_HEAD_DIM, D_MODEL + (h + 1) * XATT_HEAD_DIM)
            qh, kh, doh = q_ref[:, sl], kv_ref[:, sl], do_ref[:, sl]
            p = _xatt_probs(qh, kh)
            dp = _dot(doh, kv_ref[:, vsl], 1, 1)
            ds = p * (dp - jnp.sum(dp * p, axis=-1, keepdims=True)) * XATT_SCALE
            dq_ref[:, sl] = _dot(ds, kh, 1, 0).astype(dq_ref.dtype)
            dkv_ref[:, sl] += _dot(ds, qh, 0, 0)
            dkv_ref[:, vsl] += _dot(p, doh, 0, 0)

    row = pl.BlockSpec((tm, D_MODEL), lambda i: (i, 0))
    full = pl.BlockSpec((n_mem, 2 * D_MODEL), lambda i: (0, 0))
    return pl.pallas_call(
        body, name=name, grid=(n_rows // tm,), in_specs=[row, full, row], out_specs=[row, full],
        out_shape=[jax.ShapeDtypeStruct((n_rows, D_MODEL), MXU_DTYPE), jax.ShapeDtypeStruct((n_mem, 2 * D_MODEL), F32)],
        compiler_params=_params("arbitrary"),
    )(q, kv, do)


def _disc(logdt, a_re, a_im, b_re, b_im):
    dt = jnp.exp(logdt)
    mag = jnp.exp(a_re * dt)
    ab_re = mag * jnp.cos(a_im * dt)
    ab_im = mag * jnp.sin(a_im * dt)
    den = jnp.square(a_re) + jnp.square(a_im)
    nr = ab_re - 1.0
    f_re = (nr * a_re + ab_im * a_im) / den
    f_im = (ab_im * a_re - nr * a_im) / den
    bb_re = f_re[None] * b_re - f_im[None] * b_im
    bb_im = f_re[None] * b_im + f_im[None] * b_re
    return ab_re, ab_im, bb_re, bb_im


def _disc_transpose(logdt, a_re, a_im, b_re, b_im, g_ab_re, g_ab_im, g_bb_re, g_bb_im):
    dt = jnp.exp(logdt)
    mag = jnp.exp(a_re * dt)
    th = a_im * dt
    cs, sn = jnp.cos(th), jnp.sin(th)
    ab_re, ab_im = mag * cs, mag * sn
    den = jnp.square(a_re) + jnp.square(a_im)
    nr = ab_re - 1.0
    f_re = (nr * a_re + ab_im * a_im) / den
    f_im = (ab_im * a_re - nr * a_im) / den
    d_f_re = jnp.sum(g_bb_re * b_re + g_bb_im * b_im, axis=0)
    d_f_im = jnp.sum(g_bb_im * b_re - g_bb_re * b_im, axis=0)
    d_b_re = g_bb_re * f_re[None] + g_bb_im * f_im[None]
    d_b_im = g_bb_im * f_re[None] - g_bb_re * f_im[None]
    d_n_re, d_n_im = d_f_re / den, d_f_im / den
    d_den = -(d_f_re * f_re + d_f_im * f_im) / den
    d_ab_re = g_ab_re + d_n_re * a_re - d_n_im * a_im
    d_ab_im = g_ab_im + d_n_re * a_im + d_n_im * a_re
    d_a_re = d_n_re * nr + d_n_im * ab_im + 2.0 * d_den * a_re
    d_a_im = d_n_re * ab_im - d_n_im * nr + 2.0 * d_den * a_im
    d_mag = d_ab_re * cs + d_ab_im * sn
    d_th = mag * (d_ab_im * cs - d_ab_re * sn)
    d_a_re = d_a_re + d_mag * mag * dt
    d_a_im = d_a_im + d_th * dt
    d_dt = jnp.sum(d_mag * mag * a_re + d_th * a_im, axis=-1, keepdims=True)
    return d_dt * dt, d_a_re, d_a_im, d_b_re, d_b_im


def _whole(fn, args, out_shapes, *, name):
    n_in = len(args)

    def body(*refs):
        res = fn(*[r[...] for r in refs[:n_in]])
        for o_ref, val in zip(refs[n_in:], res):
            o_ref[...] = val

    return pl.pallas_call(body, name=name, out_shape=[jax.ShapeDtypeStruct(s, F32) for s in out_shapes],
                          compiler_params=pltpu.CompilerParams(vmem_limit_bytes=VMEM_LIMIT_BYTES))(*args)


SSM_WIDE =GROUPS_PER_TILE * SSM_STATE
LANE_GROUPS_PER_TILE = SSM_WIDE // LANES


def _chan(j):
    return slice(j * LANES, (j + 1) * LANES)


def _time_major_rows(j, q, tc):
    return pl.ds(j * LANE_GROUPS_PER_TILE + q, tc, stride=STATE_VREG_ROWS)


def _to_time_major(x, t_re_ref, t_im_ref, dst_re, dst_im, tc):
    for j in range(SSM_TILES):
        xj = x[:, _chan(j)]
        for t_ref, dst in ((t_re_ref, dst_re), (t_im_ref, dst_im)):
            r = _dot(xj, t_ref[j], 1, 0)
            for q in range(LANE_GROUPS_PER_TILE):
                dst[_time_major_rows(j, q, tc), :] = r[:, q * LANES:(q + 1) * LANES]


def _from_time_major(src, j, tc):
    return jnp.concatenate([src[_time_major_rows(j, q, tc), :] for q in range(LANE_GROUPS_PER_TILE)], axis=1)


def _scan_chunk(w_re, w_im, h_re, h_im, a_re, a_im, start, tc):
    def step(t, carry):
        hr, hi = carry
        rows = _scan_rows(t)
        nr = a_re * hr - a_im * hi + w_re[rows, :]
        ni = a_re * hi + a_im * hr + w_im[rows, :]
        h_re[rows, :] = nr
        h_im[rows, :] = ni
        return nr, ni

    return lax.fori_loop(0, tc, step, start, unroll=8)


SSM_CHUNK = 256


def _tile_spec(stack, k):
    return pl.BlockSpec((pl.Squeezed(),) + tuple(stack.shape[1:]), lambda i: (k, 0, 0, 0))


def _expand_block_diagonal(src_ref, dst):
    dst[...] = jnp.zeros_like(dst)
    r, c = src_ref.shape[1:]
    for g in range(SSM_GROUPS):
        j, gl = divmod(g, GROUPS_PER_TILE)
        dst[j, gl * r:(gl + 1) * r, gl * c:(gl + 1) * c] = src_ref[g].astype(dst.dtype)


def _extract_block_diagonal(src, dst_ref):
    r, c = dst_ref.shape[1:]
    for g in range(SSM_GROUPS):
        j, gl = divmod(g, GROUPS_PER_TILE)
        dst_ref[g] = src[j, gl * r:(gl + 1) * r, gl * c:(gl + 1) * c]


def _ssm_fwd(proj, blocks_cn, blocks_nc, a_re, a_im, gain, *, name, tc=SSM_CHUNK):
    n_rows = proj.shape[0]
    n_chunk = n_rows // tc

    def body(u_ref, br_ref, bi_ref, cr_ref, ci_ref, ar_ref, ai_ref, g_ref, y_ref, gy_ref, hr, hi, wr, wi, state,
             tbr_ref, tbi_ref, tcr_ref, tci_ref):
        @pl.when(pl.program_id(0) == 0)
        def _():
            state[...] = jnp.zeros_like(state)
            for src_ref, dst in ((br_ref, tbr_ref), (bi_ref, tbi_ref), (cr_ref, tcr_ref), (ci_ref, tci_ref)):
                _expand_block_diagonal(src_ref, dst)

        u = u_ref[...]
        _to_time_major(u, tbr_ref, tbi_ref, wr, wi, tc)
        state[0], state[1] = _scan_chunk(wr, wi, hr, hi, ar_ref[...], ai_ref[...], (state[0], state[1]), tc)
        for j in range(SSM_TILES):
            yj = (_dot(_from_time_major(hr, j, tc), tcr_ref[j], 1, 0) + _dot(_from_time_major(hi, j, tc), tci_ref[j], 1, 0)
                  + g_ref[:, _chan(j)] * u[:, _chan(j)])
            y_ref[:, _chan(j)] = yj
            gy_ref[:, _chan(j)] = jax.nn.gelu(yj).astype(gy_ref.dtype)

    rows = pl.BlockSpec((tc, SSM_WIDTH), lambda i: (i, 0))
    coef = pl.BlockSpec((STATE_VREG_ROWS, LANES), lambda i: (0, 0))
    states = pl.BlockSpec((tc * STATE_VREG_ROWS, LANES), lambda i: (i, 0))
    sshape = jax.ShapeDtypeStruct((n_rows * STATE_VREG_ROWS, LANES), F32)
    return pl.pallas_call(
        body, name=name, grid=(n_chunk,),
        in_specs=[rows, _tile_spec(blocks_cn, 0), _tile_spec(blocks_cn, 1), _tile_spec(blocks_nc, 0),
                  _tile_spec(blocks_nc, 1), coef, coef, pl.BlockSpec((1, SSM_WIDTH), lambda i: (0, 0))],
        out_specs=[rows, rows, states, states],
        out_shape=[jax.ShapeDtypeStruct((n_rows, SSM_WIDTH), F32), jax.ShapeDtypeStruct((n_rows, SSM_WIDTH), MXU_DTYPE),
                   sshape, sshape],
        scratch_shapes=[pltpu.VMEM((tc * STATE_VREG_ROWS, LANES), F32)] * 2 + [pltpu.VMEM((2, STATE_VREG_ROWS, LANES), F32)]
        + [pltpu.VMEM((SSM_TILES, LANES, SSM_WIDE), MXU_DTYPE)] * 2 + [pltpu.VMEM((SSM_TILES, SSM_WIDE, LANES), MXU_DTYPE)] * 2,
        compiler_params=_params("arbitrary"),
    )(proj, blocks_cn, blocks_cn, blocks_nc, blocks_nc, a_re, a_im, gain)


def _ssm_bwd(proj, dy, h_re, h_im, blocks_cn, blocks_nc, a_re, a_im, gain, *, name, tc=SSM_CHUNK):
    n_rows = proj.shape[0]
    n_chunk = n_rows // tc

    def body(u_ref, dy_ref, hr, hi, cr_ref, ci_ref, br_ref, bi_ref, ar_ref, ai_ref, g_ref,
             du_ref, su_ref, dc_re_ref, dc_im_ref, db_re_ref, db_im_ref, dar_ref, dai_ref, wr, wi, carry,
             tdr_ref, tdi_ref, tur_ref, tui_ref, dcr_ref, dci_ref, dbr_ref, dbi_ref):
        @pl.when(pl.program_id(0) == 0)
        def _():
            carry[...] = jnp.zeros_like(carry)
            for acc_ref in (su_ref, dcr_ref, dci_ref, dbr_ref, dbi_ref):
                acc_ref[...] = jnp.zeros_like(acc_ref)
            for src_ref, dst in ((cr_ref, tdr_ref), (ci_ref, tdi_ref), (br_ref, tur_ref), (bi_ref, tui_ref)):
                _expand_block_diagonal(src_ref, dst)

        a_r, a_i = ar_ref[...], ai_ref[...]
        u, dyv = u_ref[...], dy_ref[...]
        _to_time_major(dyv, tdr_ref, tdi_ref, wr, wi, tc)

        def step(kk, c):
            lam_r, lam_i, dar, dai = c
            rows = _scan_rows(tc - 1 - kk)
            h_r, h_i = hr[rows, :], hi[rows, :]
            dar = dar + lam_r * h_r + lam_i * h_i
            dai = dai + lam_i * h_r - lam_r * h_i
            new_r = wr[rows, :] + a_r * lam_r + a_i * lam_i
            new_i = wi[rows, :] + a_r * lam_i - a_i * lam_r
            wr[rows, :] = new_r
            wi[rows, :] = new_i
            return new_r, new_i, dar, dai

        carry[0], carry[1], carry[2], carry[3] = lax.fori_loop(0, tc, step, (carry[0], carry[1], carry[2], carry[3]),
                                                              unroll=8)
        dar_ref[...] = carry[2]
        dai_ref[...] = carry[3]
        for j in range(SSM_TILES):
            cj = _chan(j)
            lam_r, lam_i = _from_time_major(wr, j, tc), _from_time_major(wi, j, tc)
            dcr_ref[j] += _dot(dyv[:, cj], _from_time_major(hr, j, tc), 0, 0)
            dci_ref[j] += _dot(dyv[:, cj], _from_time_major(hi, j, tc), 0, 0)
            dbr_ref[j] += _dot(u[:, cj], lam_r, 0, 0)
            dbi_ref[j] += _dot(u[:, cj], lam_i, 0, 0)
            duj = _dot(lam_r, tur_ref[j], 1, 0) + _dot(lam_i, tui_ref[j], 1, 0) + g_ref[:, cj] * dyv[:, cj]
            du_ref[:, cj] = duj.astype(du_ref.dtype)
            su_ref[:, cj] += _colsum(duj)

        @pl.when(pl.program_id(0) == n_chunk - 1)
        def _():
            for src, dst_ref in ((dcr_ref, dc_re_ref), (dci_ref, dc_im_ref), (dbr_ref, db_re_ref), (dbi_ref, db_im_ref)):
                _extract_block_diagonal(src, dst_ref)

    back = lambda i: (n_chunk - 1 - i, 0)
    rows = pl.BlockSpec((tc, SSM_WIDTH), back)
    blocks = pl.BlockSpec((SSM_GROUPS, SSM_GROUP, SSM_STATE), lambda i: (0, 0, 0))
    coef = pl.BlockSpec((STATE_VREG_ROWS, LANES), lambda i: (0, 0))
    states = pl.BlockSpec((tc * STATE_VREG_ROWS, LANES), back)
    vec = pl.BlockSpec((1, SSM_WIDTH), lambda i: (0, 0))
    bshape = jax.ShapeDtypeStruct((SSM_GROUPS, SSM_GROUP, SSM_STATE), F32)
    cshape = jax.ShapeDtypeStruct((STATE_VREG_ROWS, LANES), F32)
    return pl.pallas_call(
        body, name=name, grid=(n_chunk,),
        in_specs=[rows, rows, states, states, _tile_spec(blocks_cn, 2), _tile_spec(blocks_cn, 3), _tile_spec(blocks_nc, 2),
                  _tile_spec(blocks_nc, 3), coef, coef, vec],
        out_specs=[rows, vec, blocks, blocks, blocks, blocks, coef, coef],
        out_shape=[jax.ShapeDtypeStruct((n_rows, SSM_WIDTH), MXU_DTYPE), jax.ShapeDtypeStruct((1, SSM_WIDTH), F32),
                   bshape, bshape, bshape, bshape, cshape, cshape],
        scratch_shapes=[pltpu.VMEM((tc * STATE_VREG_ROWS, LANES), F32)] * 2 + [pltpu.VMEM((4, STATE_VREG_ROWS, LANES), F32)]
        + [pltpu.VMEM((SSM_TILES, LANES, SSM_WIDE), MXU_DTYPE)] * 2 + [pltpu.VMEM((SSM_TILES, SSM_WIDE, LANES), MXU_DTYPE)] * 2
        + [pltpu.VMEM((SSM_TILES, LANES, SSM_WIDE), F32)] * 4,
        compiler_params=_params("arbitrary"),
    )(proj, dy, h_re, h_im, blocks_cn, blocks_cn, blocks_nc, blocks_nc, a_re, a_im, gain)


def _scan_rows(t):
    return pl.ds(pl.multiple_of(t * STATE_VREG_ROWS, 8), STATE_VREG_ROWS)


GATHER_GROUPS = (("w_glu", "w_att_up", "w_mix_out"), ("w_xq", "w_xkv", "w_xo", "w_ff1", "w_ff2"))
SCATTER_GROUPS = (("w_ff2", "w_ff1"), ("w_xo", "w_xq", "w_xkv", "w_mix_out"), ("w_att_up", "w_glu"), ("w_in",))


def _local_grads(x, mem, pos_col, target, sm, fetch_in, fetch, send, send_small, start_token):
    b_re_t = sm["ssm_b_re"].transpose(2, 0, 1)
    b_im_t = sm["ssm_b_im"].transpose(2, 0, 1)
    logdt = sm["ssm_log_dt"].reshape(SSM_GROUPS, 1)
    c_re, c_im = sm["ssm_c_re"], sm["ssm_c_im"]
    grp = (SSM_GROUPS, SSM_STATE)
    chn = (SSM_GROUP, SSM_GROUPS, SSM_STATE)

    wts = {}
    cos_t, sin_t = _rope_tables(pos_col, after=start_token, name="rope_tables")
    h0, xh0, rs0, h0m = _ln_fwd(x, None, sm["ln_in_g"], sm["ln_in_b"], alpha=1.0, name="ln_in_fwd")
    disc_in = (logdt, sm["ssm_a_re"], sm["ssm_a_im"], b_re_t, b_im_t)
    ab_re, ab_im, bb_re_t, bb_im_t = _whole(_disc, disc_in, [grp, grp, chn, chn], name="ssm_disc")
    a_re_rows, a_im_rows = ab_re.reshape(STATE_VREG_ROWS, LANES), ab_im.reshape(STATE_VREG_ROWS, LANES)
    tiles_cn = jnp.stack([bb_re_t.transpose(1, 0, 2), bb_im_t.transpose(1, 0, 2), c_re, -c_im])
    tiles_nc = jnp.stack([c_re.transpose(0, 2, 1), -c_im.transpose(0, 2, 1), bb_re_t.transpose(1, 2, 0),
                          bb_im_t.transpose(1, 2, 0)])
    w_in_near, near_ids = fetch_in(0, [h0m, tiles_cn, tiles_nc])
    proj = _mm_shards(h0m, w_in_near, sm["b_in"], near_ids, name="in_proj_near")
    wts["w_in"], far_ids = fetch_in(1, [proj])
    proj = _mm_shards(h0m, wts["w_in"], sm["b_in"], far_ids, prev=proj, name="in_proj_far")

    y, gy, h_re, h_im = _ssm_fwd(proj, tiles_cn, tiles_nc, a_re_rows, a_im_rows, sm["ssm_d"], name="ssm_fwd")

    q, k, v = _qkv_split(proj, cos_t, sin_t, name="qkv_split")
    outs, lses = [], []
    for g, dil in enumerate(DILATIONS):
        o_g, l_g = _dil_fwd(q[g], k[g], v[g], dil, name=f"dil_att_fwd_{dil}")
        outs.append(o_g)
        lses.append(l_g)
    att, lse = _att_merge(outs, lses, name="att_merge")
    wts.update(fetch(0, [att]))
    z = _mm(gy, wts["w_glu"], bias=sm["b_glu"], b_shards=True, name="glu_proj")
    b_att = _mm(att, wts["w_att_up"], b_shards=True, name="att_up")

    mixed = _mix_fwd(proj, z, b_att, name="gate_mix")
    mix_out = _mm(mixed, wts["w_mix_out"], bias=sm["b_mix_out"], name="mix_out")
    h1, xh1, rs1, h1m = _ln_fwd(h0, mix_out, sm["ln1_g"], sm["ln1_b"], alpha=DEEPNORM_ALPHA, name="ln1_fwd")

    wts.update(fetch(1, [h1m]))
    xq = _mm(h1m, wts["w_xq"], out_dtype=MXU_DTYPE, name="xatt_q")
    kv = _mm(mem, wts["w_xkv"], out_dtype=MXU_DTYPE, b_shards=True, name="xatt_kv")
    xo_in = _xatt_fwd(xq, kv, name="xatt_fwd")
    xo = _mm(xo_in, wts["w_xo"], name="xatt_o")
    h2, xh2, rs2, h2m = _ln_fwd(h1, xo, sm["ln2_g"], sm["ln2_b"], alpha=DEEPNORM_ALPHA, name="ln2_fwd")

    pre, act = _mm(h2m, wts["w_ff1"], bias=sm["b_ff1"], b_shards=True, name="ff1",
                   also=(lambda r: jnp.square(jnp.maximum(r, 0.0)), MXU_DTYPE))
    ff = _mm(act, wts["w_ff2"], bias=sm["b_ff2"], name="ff2")

    gw, gs = {}, {}
    dr3, dr3m, gs["ln3_g"], gs["ln3_b"], gs["b_ff2"], loss_row = _ln_loss_bwd(
        h2, ff, target, sm["ln3_g"], sm["ln3_b"], alpha=DEEPNORM_ALPHA, name="ln3_loss")
    wgrad = functools.partial(_mm, ta=True, out_dtype=WIRE_DTYPE, tk=2048)
    gw["w_ff2"] = wgrad(act, dr3m, tk=1024, name="ff2_dw")
    dpre, gs["b_ff1"] = _mm(dr3m, wts["w_ff2"], tb=True, out_dtype=MXU_DTYPE, colsum=True, name="ff2_dx",
                            gate=(pre, lambda p: 2.0 * jnp.maximum(p, 0.0)))
    gw["w_ff1"] = wgrad(h2m, dpre, out_shards=True, name="ff1_dw")
    sent = send(0, gw)
    dh2 = _mm(dpre, wts["w_ff1"], tb=True, b_shards=True, after=sent, name="ff1_dx")

    dr2, dr2m, gs["ln2_g"], gs["ln2_b"], _ = _ln_bwd(dr3, dh2, xh2, rs2, sm["ln2_g"], alpha=DEEPNORM_ALPHA,
                                                     name="ln2_bwd")
    gw["w_xo"] = wgrad(xo_in, dr2m, name="xatt_o_dw")
    dxo_in = _mm(dr2m, wts["w_xo"], tb=True, out_dtype=MXU_DTYPE, name="xatt_o_dx")
    dxq, dkv = _xatt_bwd(xq, kv, dxo_in, name="xatt_bwd")
    gw["w_xq"] = wgrad(h1m, dxq, name="xatt_q_dw")
    gw["w_xkv"] = wgrad(mem, dkv, out_shards=True, name="xatt_kv_dw")
    dh1 = _mm(dxq, wts["w_xq"], tb=True, name="xatt_q_dx")

    dr1, dr1m, gs["ln1_g"], gs["ln1_b"], gs["b_mix_out"] = _ln_bwd(dr2, dh1, xh1, rs1, sm["ln1_g"],
                                                                   alpha=DEEPNORM_ALPHA, name="ln1_bwd")
    gw["w_mix_out"] = wgrad(mixed, dr1m, name="mix_out_dw")
    sent = send(1, gw)
    dmixed = _mm(dr1m, wts["w_mix_out"], tb=True, after=sent, name="mix_out_dx")
    dgs, dga, dz, db_att, s_gs, s_ga, gs["b_glu"] = _mix_bwd(dmixed, proj, z, b_att, name="gate_mix_bwd")

    gw["w_att_up"] = wgrad(att, db_att, out_shards=True, name="att_up_dw")
    gw["w_glu"] = wgrad(gy, dz, out_shards=True, name="glu_dw")
    sent = send(2, gw)
    datt = _mm(db_att, wts["w_att_up"], tb=True, b_shards=True, after=sent, name="att_up_dx")
    stats = _att_stats(datt, att, lse, name="att_stats")
    dqkv = [_dil_bwd(q[g], k[g], v[g], datt, stats, dil, name=f"dil_att_bwd_{dil}") for g, dil in enumerate(DILATIONS)]

    dgy = _mm(dz, wts["w_glu"], tb=True, b_shards=True, name="glu_dx")
    dy, gs["ssm_d"] = _gelu_bwd(dgy, y, proj, name="gelu_bwd")
    du, s_u, dc_re_t, dc_im_t, dbb_re_t, dbb_im_t, da_re, da_im = _ssm_bwd(
        proj, dy, h_re, h_im, tiles_cn, tiles_nc, a_re_rows, a_im_rows, sm["ssm_d"], name="ssm_bwd")
    gs["ssm_c_re"], gs["ssm_c_im"] = dc_re_t, -dc_im_t
    disc_ct = (da_re.reshape(grp), da_im.reshape(grp), dbb_re_t.transpose(1, 0, 2), dbb_im_t.transpose(1, 0, 2))
    d_logdt, gs["ssm_a_re"], gs["ssm_a_im"], d_b_re_t, d_b_im_t = _whole(
        _disc_transpose, disc_in + disc_ct, [(SSM_GROUPS, 1), grp, grp, chn, chn], name="ssm_disc_bwd")
    gs["ssm_log_dt"] = d_logdt
    gs["ssm_b_re"], gs["ssm_b_im"] = d_b_re_t.transpose(1, 2, 0), d_b_im_t.transpose(1, 2, 0)

    dproj, s_qkv = _dproj_assemble(du, dqkv, dgs, dga, cos_t, sin_t, name="dproj_assemble")
    gs["b_in"] = jnp.concatenate([s_u, *s_qkv, s_gs, s_ga], axis=1)
    sent = send_small(gs, SMALL_EARLY)
    gw["w_in"] = wgrad(h0m, dproj, out_shards=True, after=sent, name="in_proj_dw")
    sent = send(3, gw)
    dh0 = _mm(dproj, wts["w_in"], tb=True, b_shards=True, after=sent, name="in_proj_dx")
    grad_x, gs["ln_in_g"], gs["ln_in_b"], _ = _ln_bwd(dr1, dh0, xh0, rs0, sm["ln_in_g"], alpha=DEEPNORM_ALPHA,
                                                      operand=False, name="ln_in_bwd")
    return loss_row, grad_x, gs


N_PEER = N_DEV - 1
_IN_HBM = pl.BlockSpec(memory_space=pltpu.HBM)
_IN_SEMAPHORE = pl.BlockSpec(memory_space=pltpu.SEMAPHORE)


def _device_index():
    return 4 * lax.axis_index("x") + 2 * lax.axis_index("y") + lax.axis_index("c")


ALL_PEERS = tuple(range(1, N_DEV))
NEAR_PEERS = (1, 2, 3, 4, 5)
FAR_PEERS = (6, 7)


def _peer_index(kk):
    x, y, c = lax.axis_index("x"), lax.axis_index("y"), lax.axis_index("c")
    return 4 * ((x + (kk >> 2)) % 2) + 2 * ((y + ((kk >> 1) & 1)) % 2) + (c + (kk & 1)) % 2


def _exchange_copies(src_refs, land_refs, send_sems, recv_sems, scatter, peers):
    x, y, c = lax.axis_index("x"), lax.axis_index("y"), lax.axis_index("c")
    me = 4 * x + 2 * y + c
    pairs = []
    for a, (src_ref, land_ref) in enumerate(zip(src_refs, land_refs)):
        for idx, kk in enumerate(peers):
            px = (x + (kk >> 2)) % 2
            py = (y + ((kk >> 1) & 1)) % 2
            pc = (c + (kk & 1)) % 2
            peer = 4 * px + 2 * py + pc
            sem = a * len(peers) + idx
            src = src_ref.at[peer] if scatter else src_ref

            def copy(dst, src=src, sem=sem, px=px, py=py, pc=pc):
                return pltpu.make_async_remote_copy(
                    src_ref=src, dst_ref=dst, send_sem=send_sems.at[sem], recv_sem=recv_sems.at[sem],
                    device_id=(px, py, pc), device_id_type=pl.DeviceIdType.MESH)

            pairs.append((functools.partial(copy, land_ref.at[me]), functools.partial(copy, land_ref.at[peer])))
    return pairs


def _own_copies(src_refs, land_refs, own_sems, scatter):
    me = _device_index()
    return [functools.partial(pltpu.make_async_copy, src_ref.at[me] if scatter else src_ref, land_ref.at[me],
                              own_sems.at[a]) for a, (src_ref, land_ref) in enumerate(zip(src_refs, land_refs))]


def _exchange_start(srcs, *, scatter, name, after=None, peers=ALL_PEERS, lands=None):
    n_arr, n_sem = len(srcs), len(srcs) * len(peers)
    own = lands is None
    if own:
        lands = [lax.empty((N_DEV,) + tuple(s.shape[1:] if scatter else s.shape), s.dtype) for s in srcs]
    n_in = 2 * n_arr + (after is not None)

    def body(*refs):
        send_sems, recv_sems = refs[n_in], refs[n_in + 1]
        for sent, _ in _exchange_copies(refs[:n_arr], refs[n_arr:2 * n_arr], send_sems, recv_sems, scatter, peers):
            sent().start()
        if own:
            for local in _own_copies(refs[:n_arr], refs[n_arr:2 * n_arr], refs[n_in + 2], scatter):
                local().start()
        refs[-1][...] = jnp.zeros_like(refs[-1])

    sems = [pltpu.SemaphoreType.DMA((n_sem,)), pltpu.SemaphoreType.DMA((n_sem,))] + [pltpu.SemaphoreType.DMA((n_arr,))] * own
    through = [pltpu.HBM(t.shape, t.dtype) for t in (*srcs, *lands)]
    res = pl.pallas_call(
        body, name=name, out_shape=(*sems, *through, jax.ShapeDtypeStruct((8, LANES), F32)),
        in_specs=[_IN_HBM] * (2 * n_arr) + [pl.BlockSpec(memory_space=pl.ANY)] * (after is not None),
        out_specs=(*[_IN_SEMAPHORE] * len(sems), *[_IN_HBM] * (2 * n_arr), pl.BlockSpec(memory_space=pltpu.VMEM)),
        input_output_aliases={i: len(sems) + i for i in range(2 * n_arr)},
        compiler_params=pltpu.CompilerParams(has_side_effects=pltpu.SideEffectType.DATAFLOW_SIDE_EFFECTING),
    )(*[pltpu.with_memory_space_constraint(t, pltpu.HBM) for t in (*srcs, *lands)],
      *([after] if after is not None else []))
    first = len(sems)
    handle = dict(sems=res[:first], srcs=res[first:first + n_arr], lands=res[first + n_arr:first + 2 * n_arr],
                  scatter=scatter, peers=peers, own=own)
    return handle, res[-1]


def _exchange_wait(handle, *, after, name, srcs=None, lands=None):
    srcs = handle["srcs"] if srcs is None else srcs
    lands = handle["lands"] if lands is None else lands
    sems, scatter, peers, own = handle["sems"], handle["scatter"], handle["peers"], handle["own"]
    n_arr = len(srcs)
    after = list(after)

    def body(*refs):
        src_refs, land_refs = refs[:n_arr], refs[n_arr:2 * n_arr]
        for sent, received in _exchange_copies(src_refs, land_refs, refs[2 * n_arr], refs[2 * n_arr + 1], scatter, peers):
            sent().wait_send()
            received().wait_recv()
        if own:
            for local in _own_copies(src_refs, land_refs, refs[2 * n_arr + 2], scatter):
                local().wait()

    res = pl.pallas_call(
        body, name=name, out_shape=tuple(pltpu.HBM(t.shape, t.dtype) for t in (*srcs, *lands)),
        in_specs=[_IN_HBM] * (2 * n_arr) + [_IN_SEMAPHORE] * len(sems) + [pl.BlockSpec(memory_space=pl.ANY)] * len(after),
        out_specs=tuple([_IN_HBM] * (2 * n_arr)), input_output_aliases={i: i for i in range(2 * n_arr)},
        compiler_params=pltpu.CompilerParams(has_side_effects=pltpu.SideEffectType.DATAFLOW_SIDE_EFFECTING),
    )(*srcs, *lands, *sems, *after)
    return res[:n_arr], res[n_arr:]


def _adamw(g, w, m, v):
    m_new = ADAM_B1 * m + (1.0 - ADAM_B1) * g
    v_new = ADAM_B2 * v + (1.0 - ADAM_B2) * jnp.square(g)
    m_hat = m_new / (1.0 - ADAM_B1 ** ADAM_STEP)
    v_hat = v_new / (1.0 - ADAM_B2 ** ADAM_STEP)
    return g, -ADAM_LR * (m_hat / (jnp.sqrt(v_hat) + ADAM_EPS) + ADAM_WD * w), m_new, v_new


def _reduce_adamw(gstack, w, m, v, *, name, tr=128):
    n_rows, cols = w.shape
    tr = min(tr, n_rows)
    assert n_rows % tr == 0, (name, n_rows, tr)

    def body(g_ref, w_ref, m_ref, v_ref, *out_refs):
        g = g_ref[0].astype(F32)
        for dev in range(1, N_DEV):
            g = g + g_ref[dev].astype(F32)
        for o_ref, val in zip(out_refs, _adamw(g, w_ref[...], m_ref[...], v_ref[...])):
            o_ref[...] = val

    flat = pl.BlockSpec((tr, cols), lambda i: (i, 0))
    shape = jax.ShapeDtypeStruct((n_rows, cols), F32)
    return pl.pallas_call(
        body, name=name, grid=(n_rows // tr,),
        in_specs=[pl.BlockSpec((N_DEV, tr, cols), lambda i: (0, i, 0)), flat, flat, flat],
        out_specs=[flat] * 4, out_shape=[shape] * 4, compiler_params=_params("parallel"),
    )(gstack, w, m, v)


SMALL_FLAT_SSM = ("ssm_b_re", "ssm_b_im", "ssm_c_re", "ssm_c_im")


def _small_view(name, shape):
    size = int(np.prod(shape))
    if name in SMALL_FLAT_SSM:
        return SSM_GROUPS, size // SSM_GROUPS
    if name in ("ssm_a_re", "ssm_a_im"):
        return SSM_GROUPS, SSM_STATE
    return 1, size


def _pack_rows(view):
    return -(-(view[0] * view[1]) // PACK_COLS)


SMALL_LATE = ("ln_in_g", "ln_in_b")
SMALL_EARLY = tuple(n for n in SMALL if n not in SMALL_LATE)


def _pack_small(gs, names, views):
    parts = []
    for n in names:
        flat = gs[n].reshape(-1).astype(WIRE_DTYPE)
        parts.append(jnp.pad(flat, (0, _pack_rows(views[n]) * PACK_COLS - flat.shape[0])))
    total = sum(p.shape[0] for p in parts) // PACK_COLS
    parts.append(jnp.zeros(((-total % PACK_ROW_ALIGN) * PACK_COLS,), WIRE_DTYPE))
    return jnp.concatenate(parts).reshape(-1, PACK_COLS)


def _small_pieces(view):
    rows, cols = view
    if cols == PACK_COLS:
        return [(0, rows, 0, 0, 0, cols)]
    if rows == 1 and cols > PACK_COLS:
        return [(kk, 1, 0, 0, kk * PACK_COLS, PACK_COLS) for kk in range(cols // PACK_COLS)]
    if rows == 1:
        return [(0, 1, 0, 0, 0, cols)]
    return [((r * cols) // PACK_COLS, 1, (r * cols) % PACK_COLS, r, 0, cols) for r in range(rows)]


def _adamw_small(stacks, views, w, m, v, *, name):
    n = len(SMALL)
    place, first = {}, [0, 0]
    for k, names in enumerate((SMALL_EARLY, SMALL_LATE)):
        for name_ in names:
            place[name_] = (k, first[k])
            first[k] += _pack_rows(views[name_])

    def body(early_ref, late_ref, *refs):
        ins, outs = refs[:3 * n], refs[3 * n:]
        for i, name_ in enumerate(SMALL):
            stack_ref = (early_ref, late_ref)[place[name_][0]]
            row0 = place[name_][1]
            for prow, nrows, lane, orow, ocol, width in _small_pieces(views[name_]):
                src = (slice(row0 + prow, row0 + prow + nrows), slice(lane, lane + width))
                dst = (slice(orow, orow + nrows), slice(ocol, ocol + width))
                g = stack_ref[(0,) + src].astype(F32)
                for dev in range(1, N_DEV):
                    g = g + stack_ref[(dev,) + src].astype(F32)
                res = _adamw(g, ins[i][dst], ins[n + i][dst], ins[2 * n + i][dst])
                for kk, val in enumerate(res):
                    outs[kk * n + i][dst] = val

    res = pl.pallas_call(
        body, name=name, out_shape=[jax.ShapeDtypeStruct(views[name_], F32) for _ in range(4) for name_ in SMALL],
        compiler_params=pltpu.CompilerParams(vmem_limit_bytes=VMEM_LIMIT_BYTES),
    )(*stacks, *[d[name_] for d in (w, m, v) for name_ in SMALL])
    return [dict(zip(SMALL, res[kk * n:(kk + 1) * n])) for kk in range(4)]


def kernel(x, mem, positions, ln_in_g, ln_in_b, w_in, b_in, ssm_log_dt, ssm_a_re, ssm_a_im, ssm_b_re, ssm_b_im, ssm_c_re, ssm_c_im, ssm_d, w_glu, b_glu, w_att_up, w_mix_out, b_mix_out, ln1_g, ln1_b, w_xq, w_xkv, w_xo, ln2_g, ln2_b, w_ff1, b_ff1, w_ff2, b_ff2, ln3_g, ln3_b, loss_target, m_ln_in_g, m_ln_in_b, m_w_in, m_b_in, m_ssm_log_dt, m_ssm_a_re, m_ssm_a_im, m_ssm_b_re, m_ssm_b_im, m_ssm_c_re, m_ssm_c_im, m_ssm_d, m_w_glu, m_b_glu, m_w_att_up, m_w_mix_out, m_b_mix_out, m_ln1_g, m_ln1_b, m_w_xq, m_w_xkv, m_w_xo, m_ln2_g, m_ln2_b, m_w_ff1, m_b_ff1, m_w_ff2, m_b_ff2, m_ln3_g, m_ln3_b, v_ln_in_g, v_ln_in_b, v_w_in, v_b_in, v_ssm_log_dt, v_ssm_a_re, v_ssm_a_im, v_ssm_b_re, v_ssm_b_im, v_ssm_c_re, v_ssm_c_im, v_ssm_d, v_w_glu, v_b_glu, v_w_att_up, v_w_mix_out, v_b_mix_out, v_ln1_g, v_ln1_b, v_w_xq, v_w_xkv, v_w_xo, v_ln2_g, v_ln2_b, v_w_ff1, v_b_ff1, v_w_ff2, v_b_ff2, v_ln3_g, v_ln3_b):
    given = dict(locals())
    w_arg = {n: given[n] for n in WEIGHTS}
    m_arg = {n: given["m_" + n] for n in WEIGHTS}
    v_arg = {n: given["v_" + n] for n in WEIGHTS}

    shards = {n: w_arg[n][0].astype(MXU_DTYPE) for n in BIG}
    in_near, token = _exchange_start([shards["w_in"]], scatter=False, peers=NEAR_PEERS, name="gather_start_in_near")
    in_far, token = _exchange_start(in_near["srcs"], scatter=False, peers=FAR_PEERS, lands=in_near["lands"],
                                    after=token, name="gather_start_in_far")
    w_in_state = [in_far["srcs"], in_far["lands"]]
    gathers = []

    small_views = {n: _small_view(n, w_arg[n].shape) for n in SMALL}
    small_w, small_m, small_v = [{n: d[n].reshape(small_views[n]) for n in SMALL} for d in (w_arg, m_arg, v_arg)]
    relaid = [d[n] for d in (small_w, small_m, small_v) for n in SMALL_FLAT_SSM]

    def fetch_in(part, after):
        handle, peers, tag = ((in_near, (0,) + NEAR_PEERS, "near"), (in_far, FAR_PEERS, "far"))[part]
        if part == 1:
            for i, names in enumerate(GATHER_GROUPS):
                gathers.append(_exchange_start([shards[n] for n in names], scatter=False, after=after[0],
                                               name=f"gather_start_{i}")[0])
        w_in_state[:] = _exchange_wait(handle, after=after + (relaid if part == 0 else []), srcs=w_in_state[0],
                                       lands=w_in_state[1], name="gather_wait_in_" + tag)
        return w_in_state[1][0], jnp.stack([_peer_index(kk) for kk in peers]).astype(jnp.int32)

    def fetch(i, after):
        _, lands = _exchange_wait(gathers[i], after=after, name=f"gather_wait_{i}")
        full = dict(zip(GATHER_GROUPS[i], lands))
        return {n: t if n in BIG_COL_SHARDED else t.reshape(-1, t.shape[-1]) for n, t in full.items()}

    scatters = {}

    def send(i, gw):
        slots = [gw[n] if n in BIG_COL_SHARDED else gw[n].reshape(N_DEV, -1, gw[n].shape[-1]) for n in SCATTER_GROUPS[i]]
        handle, sent = _exchange_start(slots, scatter=True, name=f"scatter_start_{i}")
        scatters[i] = (handle, slots)
        return sent

    sm = {}
    for n in SMALL:
        t = w_arg[n]
        if n.startswith("ssm_") and n not in ("ssm_d", "ssm_log_dt"):
            sm[n] = t[0]
        else:
            sm[n] = t.reshape(1, -1)

    smalls = []

    def send_small(gs, names):
        handle, sent = _exchange_start([_pack_small(gs, names, small_views)], scatter=False,
                                       name=f"small_start_{len(smalls)}")
        smalls.append(handle)
        return sent

    loss_row, grad_x, gs = _local_grads(x[0], mem[0], positions.reshape(-1, 1), loss_target[0], sm, fetch_in, fetch,
                                        send, send_small, token)
    loss = lax.psum(loss_row[0, 0], ("x", "y", "c"))
    send_small(gs, SMALL_LATE)

    results = [{}, {}, {}, {}]
    done = grad_x
    for i, names in enumerate(SCATTER_GROUPS):
        handle, slots = scatters[i]
        _, lands = _exchange_wait(handle, after=[done], name=f"scatter_wait_{i}")
        for n, land, slot in zip(names, lands, slots):
            res = _reduce_adamw(land, w_arg[n][0], m_arg[n][0], v_arg[n][0], name="adamw_" + n)
            done = res[0]
            for d, r in zip(results, res):
                d[n] = r[None]
    stacks = [_exchange_wait(handle, after=[done], name=f"small_wait_{i}")[1][0] for i, handle in enumerate(smalls)]
    res = _adamw_small(stacks, small_views, small_w, small_m, small_v, name="adamw_small")
    for d, r in zip(results, res):
        d.update({n: r[n].reshape(w_arg[n].shape) for n in SMALL})
    out = [loss, grad_x[None]]
    for d in results:
        out += [d[n] for n in WEIGHTS]
    return tuple(out)
```

```python
import functools

import numpy as np
import jax
import jax.numpy as jnp
from jax import lax
from jax.experimental import pallas as pl
from jax.experimental.pallas import tpu as pltpu

F32 = jnp.float32
MXU_DTYPE = jnp.bfloat16
WIRE_DTYPE = jnp.bfloat16
VMEM_LIMIT_BYTES = 48 * 1024 * 1024
LANES = 128

N_DEV = 8
D_MODEL = 1024
SSM_GROUP = 16
SSM_WIDTH = 768
SSM_GROUPS = SSM_WIDTH // SSM_GROUP
SSM_STATE = 64
SSM_CH = SSM_GROUPS * SSM_STATE
SSM_TILES = SSM_WIDTH // LANES
GROUPS_PER_TILE = LANES // SSM_GROUP
STATE_VREG_ROWS = SSM_CH // LANES
ATT_HEAD_DIM = 64
ATT_HEADS_PER_GROUP = 4
ATT_MERGED = ATT_HEADS_PER_GROUP * ATT_HEAD_DIM
LANE_HALVES = ATT_MERGED // LANES
DILATIONS = (1, 4, 16)
ATT_BLK = 128
ATT_SCALE = ATT_HEAD_DIM ** -0.5
ROT_DIM = ATT_HEAD_DIM // 4
ROPE_THETA = 500000.0
XATT_HEADS = 4
XATT_HEAD_DIM = D_MODEL // XATT_HEADS
XATT_SCALE = XATT_HEAD_DIM ** -0.5
DEEPNORM_ALPHA = 2.0 ** 0.25
LN_EPS = 1e-5
NEG_INF = -1e30
OFF_Q_BLK, OFF_K_BLK, OFF_V_BLK = 3, 6, 9
OFF_GS_BLK, OFF_GA_BLK = 3, 4

ADAM_LR = 0.001
ADAM_B1 = 0.9
ADAM_B2 = 0.999
ADAM_EPS = 1e-08
ADAM_WD = 0.01
ADAM_STEP = 10

BIG = ("w_in", "w_glu", "w_att_up", "w_mix_out", "w_xq", "w_xkv", "w_xo", "w_ff1", "w_ff2")
BIG_COL_SHARDED = ("w_in", "w_glu", "w_att_up", "w_xkv", "w_ff1")
WEIGHTS = ("ln_in_g", "ln_in_b", "w_in", "b_in", "ssm_log_dt", "ssm_a_re", "ssm_a_im", "ssm_b_re", "ssm_b_im",
           "ssm_c_re", "ssm_c_im", "ssm_d", "w_glu", "b_glu", "w_att_up", "w_mix_out", "b_mix_out", "ln1_g", "ln1_b",
           "w_xq", "w_xkv", "w_xo", "ln2_g", "ln2_b", "w_ff1", "b_ff1", "w_ff2", "b_ff2", "ln3_g", "ln3_b")
SMALL = tuple(n for n in WEIGHTS if n not in BIG)
PACK_COLS = 1024
PACK_ROW_ALIGN = 16


def _params(*sem):
    return pltpu.CompilerParams(dimension_semantics=sem, vmem_limit_bytes=VMEM_LIMIT_BYTES)


def _dot(a, b, ca, cb):
    return lax.dot_general(a.astype(MXU_DTYPE), b.astype(MXU_DTYPE), (((ca,), (cb,)), ((), ())),
                           preferred_element_type=F32)


def _fit(dim, pref):
    if dim <= pref:
        return dim
    best = max(t for t in range(LANES, pref + 1, LANES) if dim % t == 0)
    return best


def _mm(a, b, *, name, ta=False, tb=False, bias=None, out_dtype=F32, b_shards=False, out_shards=False, after=None,
        also=None, gate=None, colsum=False, tm=2048, tn=1024, tk=1024):
    m, k = (a.shape[1], a.shape[0]) if ta else a.shape
    order = (lambda f: (lambda j, i, kk: f(i, j, kk))) if colsum else (lambda f: f)
    spec = lambda shape, f: pl.BlockSpec(shape, order(f))
    if b_shards:
        n_sh, rows, n_loc = b.shape
        if tb:
            n, tn, tk = rows, _fit(rows, tn), n_loc
            assert k == n_sh * n_loc, (name, k, b.shape)
            b_spec = spec((1, tn, tk), lambda i, j, kk: (kk, j, 0))
        else:
            n, tn, tk = n_sh * n_loc, n_loc, _fit(k, tk)
            b_spec = spec((1, tk, tn), lambda i, j, kk: (j, kk, 0))
    else:
        n = b.shape[0] if tb else b.shape[1]
        tn = n // N_DEV if out_shards else _fit(n, tn)
        tk = _fit(k, tk)
        b_spec = spec((tn, tk), lambda i, j, kk: (j, kk)) if tb else spec((tk, tn), lambda i, j, kk: (kk, j))
    tm = _fit(m, tm)
    nk = k // tk
    a_spec = spec((tk, tm), lambda i, j, kk: (kk, i)) if ta else spec((tm, tk), lambda i, j, kk: (i, kk))
    tile = spec((tm, tn), lambda i, j, kk: (i, j))
    in_specs, args = [a_spec, b_spec], [a, b]
    if bias is not None:
        in_specs.append(spec((1, tn), lambda i, j, kk: (0, j)))
        args.append(bias)
    if gate is not None:
        in_specs.append(tile)
        args.append(gate[0])
    if after is not None:
        in_specs.append(pl.BlockSpec(memory_space=pl.ANY))
        args.append(after)
    n_in = len(args)
    if out_shards:
        assert n == N_DEV * tn, (name, n, tn)
        out_specs = [spec((1, tm, tn), lambda i, j, kk: (j, i, 0))]
        out_shape = [jax.ShapeDtypeStruct((N_DEV, m, tn), out_dtype)]
    else:
        out_specs = [tile]
        out_shape = [jax.ShapeDtypeStruct((m, n), out_dtype)]
    if also is not None:
        out_specs.append(tile)
        out_shape.append(jax.ShapeDtypeStruct((m, n), also[1]))
    if colsum:
        out_specs.append(spec((1, tn), lambda i, j, kk: (0, j)))
        out_shape.append(jax.ShapeDtypeStruct((1, n), F32))

    def body(*refs):
        a_ref, b_ref = refs[0], refs[1]
        o_ref = refs[n_in]

        def product():
            return _dot(a_ref[...], b_ref[0] if b_shards else b_ref[...], 0 if ta else 1, 1 if tb else 0)

        def finish(r):
            if bias is not None:
                r = r + refs[2][...]
            if gate is not None:
                r = r * gate[1](refs[2 + (bias is not None)][...])
            if out_shards:
                o_ref[0] = r.astype(o_ref.dtype)
            else:
                o_ref[...] = r.astype(o_ref.dtype)
            if also is not None:
                refs[n_in + 1][...] = also[0](r).astype(also[1])
            if colsum:
                s_ref = refs[n_in + 1 + (also is not None)]

                @pl.when(pl.program_id(1) == 0)
                def _():
                    s_ref[...] = jnp.zeros_like(s_ref)

                s_ref[...] += _colsum(r)

        if nk == 1:
            finish(product())
            return
        acc_ref = refs[-1]
        kk = pl.program_id(2)

        @pl.when(kk == 0)
        def _():
            acc_ref[...] = jnp.zeros_like(acc_ref)

        acc_ref[...] += product()

        @pl.when(kk == nk - 1)
        def _():
            finish(acc_ref[...])

    grid = (n // tn, m // tm, nk) if colsum else (m // tm, n // tn, nk)
    res = pl.pallas_call(
        body, name=name, grid=grid, in_specs=in_specs, out_specs=out_specs, out_shape=out_shape,
        scratch_shapes=[pltpu.VMEM((tm, tn), F32)] if nk > 1 else [],
        compiler_params=_params("parallel", "arbitrary" if colsum else "parallel", "arbitrary"),
    )(*args)
    return res[0] if len(res) == 1 else res


def _mm_shards(a, w, bias, shard_ids, *, name, prev=None, tm=2048):
    m, k = a.shape
    n_sh, _, n_loc = w.shape
    tm = _fit(m, tm)

    def body(ids_ref, a_ref, w_ref, b_ref, *rest):
        rest[-1][...] = _dot(a_ref[...], w_ref[0], 1, 0) + b_ref[...]

    grid_spec = pltpu.PrefetchScalarGridSpec(
        num_scalar_prefetch=1, grid=(m // tm, shard_ids.shape[0]),
        in_specs=[pl.BlockSpec((tm, k), lambda i, j, ids: (i, 0)),
                  pl.BlockSpec((1, k, n_loc), lambda i, j, ids: (ids[j], 0, 0)),
                  pl.BlockSpec((1, n_loc), lambda i, j, ids: (0, ids[j]))]
        + [pl.BlockSpec(memory_space=pl.ANY)] * (prev is not None),
        out_specs=pl.BlockSpec((tm, n_loc), lambda i, j, ids: (i, ids[j])))
    return pl.pallas_call(
        body, name=name, grid_spec=grid_spec, out_shape=jax.ShapeDtypeStruct((m, n_sh * n_loc), F32),
        input_output_aliases={4: 0} if prev is not None else {}, compiler_params=_params("parallel", "arbitrary"),
    )(shard_ids, a, w, bias, *([prev] if prev is not None else []))


def _rowcall(fn, rows, fulls, row_outs, acc_outs=(), *, n_rows, tm, name, after=None):
    n_r, n_f, n_o, n_a = len(rows), len(fulls), len(row_outs), len(acc_outs)
    n_in = n_r + n_f + (after is not None)
    assert n_rows % tm == 0, (name, n_rows, tm)

    def body(*refs):
        res = fn(*[r[...] for r in refs[:n_r + n_f]])
        res = tuple(res) if isinstance(res, (tuple, list)) else (res,)
        o_refs = refs[n_in:n_in + n_o]
        a_refs = refs[n_in + n_o:]
        for o_ref, val in zip(o_refs, res[:n_o]):
            o_ref[...] = val.astype(o_ref.dtype)
        if n_a:
            @pl.when(pl.program_id(0) == 0)
            def _():
                for a_ref in a_refs:
                    a_ref[...] = jnp.zeros_like(a_ref)

            for a_ref, val in zip(a_refs, res[n_o:]):
                a_ref[...] += val

    in_specs = [pl.BlockSpec((tm, w), functools.partial(lambda i, cb: (i, cb), cb=cb)) for _, w, cb in rows]
    in_specs += [pl.BlockSpec(f.shape, functools.partial(lambda i, nd: (0,) * nd, nd=f.ndim)) for f in fulls]
    in_specs += [pl.BlockSpec(memory_space=pl.ANY)] * (after is not None)
    out_specs = [pl.BlockSpec((tm, w), lambda i: (i, 0)) for w, _ in row_outs]
    out_specs += [pl.BlockSpec((1, w), lambda i: (0, 0)) for w in acc_outs]
    out_shape = [jax.ShapeDtypeStruct((n_rows, w), dt) for w, dt in row_outs]
    out_shape += [jax.ShapeDtypeStruct((1, w), F32) for w in acc_outs]
    return pl.pallas_call(
        body, name=name, grid=(n_rows // tm,), in_specs=in_specs, out_specs=out_specs, out_shape=out_shape,
        compiler_params=_params("arbitrary" if n_a else "parallel"),
    )(*[r[0] for r in rows], *fulls, *([after] if after is not None else []))


def _colsum(v):
    return jnp.sum(v, axis=0, keepdims=True)


def _ln_fwd(a, r, g, b, *, alpha, name):
    n_rows, d = a.shape

    def fn(*t):
        xin = t[0] if alpha == 1.0 else alpha * t[0]
        if r is not None:
            xin = xin + t[1]
        gv, bv = t[-2], t[-1]
        mu = jnp.mean(xin, axis=-1, keepdims=True)
        xc = xin - mu
        var = jnp.mean(xc * xc, axis=-1, keepdims=True)
        rstd = lax.rsqrt(var + LN_EPS)
        xh = xc * rstd
        y = xh * gv + bv
        return y, xh, rstd, y

    rows = [(a, d, 0)] + ([(r, d, 0)] if r is not None else [])
    return _rowcall(fn, rows, [g, b], [(d, F32), (d, F32), (1, F32), (d, MXU_DTYPE)], n_rows=n_rows, tm=256, name=name)


def _ln_bwd(dya, dyb, xh, rstd, g, *, alpha, name, operand=True):
    n_rows, d = xh.shape

    def fn(da, db, xhv, rs, gv):
        dy = alpha * da + db
        dyg = dy * gv
        m1 = jnp.mean(dyg, axis=-1, keepdims=True)
        m2 = jnp.mean(dyg * xhv, axis=-1, keepdims=True)
        dx = rs * (dyg - m1 - xhv * m2)
        return (dx,) + ((dx,) if operand else ()) + (_colsum(dy * xhv), _colsum(dy), _colsum(dx))

    rows = [(dya, d, 0), (dyb, d, 0), (xh, d, 0), (rstd, 1, 0)]
    return _rowcall(fn, rows, [g], [(d, F32)] + [(d, MXU_DTYPE)] * operand, [d, d, d], n_rows=n_rows, tm=256, name=name)


def _ln_loss_bwd(a, r, target, g, b, *, alpha, name):
    n_rows, d = a.shape

    def fn(av, rv, tv, gv, bv):
        xin = alpha * av + rv
        mu = jnp.mean(xin, axis=-1, keepdims=True)
        xc = xin - mu
        var = jnp.mean(xc * xc, axis=-1, keepdims=True)
        rs = lax.rsqrt(var + LN_EPS)
        xh = xc * rs
        diff = xh * gv + bv - tv
        part = jnp.sum(jnp.sum(diff * diff, axis=1, keepdims=True), axis=0, keepdims=True) * (0.5 / d)
        dy = diff * (1.0 / d)
        dyg = dy * gv
        m1 = jnp.mean(dyg, axis=-1, keepdims=True)
        m2 = jnp.mean(dyg * xh, axis=-1, keepdims=True)
        dx = rs * (dyg - m1 - xh * m2)
        return dx, dx, _colsum(dy * xh), _colsum(dy), _colsum(dx), jnp.broadcast_to(part, (1, LANES))

    return _rowcall(fn, [(a, d, 0), (r, d, 0), (target, d, 0)], [g, b], [(d, F32), (d, MXU_DTYPE)], [d, d, d, LANES],
                    n_rows=n_rows, tm=256, name=name)


def _rope_lane_constants():
    lane = np.arange(ATT_MERGED)
    in_head = lane % ATT_HEAD_DIM
    sign = np.where(in_head < ROT_DIM // 2, -1.0, np.where(in_head < ROT_DIM, 1.0, 0.0)).astype(np.float32)
    inv_freq = ROPE_THETA ** (-jnp.arange(0, ROT_DIM, 2, dtype=F32) / ROT_DIM)
    return inv_freq[lane % (ROT_DIM // 2)].reshape(1, ATT_MERGED), jnp.asarray(sign).reshape(1, ATT_MERGED)


def _rope_tables(pos_col, *, name, after=None):
    inv_lane, sign = _rope_lane_constants()

    def fn(pos, inv, sg):
        ang = pos.astype(F32) * inv
        return jnp.where(sg != 0.0, jnp.cos(ang), 1.0), sg * jnp.sin(ang)

    return _rowcall(fn, [(pos_col, 1, 0)], [inv_lane, sign], [(ATT_MERGED, F32), (ATT_MERGED, F32)],
                    n_rows=pos_col.shape[0], tm=512, name=name, after=after)


def _rot_partner(t):
    lane = lax.broadcasted_iota(jnp.int32, t.shape, 1)
    width = t.shape[1]
    return jnp.where((lane & (ROT_DIM // 2)) == 0, pltpu.roll(t, width - ROT_DIM // 2, 1), pltpu.roll(t, ROT_DIM // 2, 1))


def _rope(t, cos_t, sin_t):
    return t * cos_t + _rot_partner(t) * sin_t


def _rope_transpose(dt, cos_t, sin_t):
    return dt * cos_t + _rot_partner(dt * sin_t)


def _strided_rows(r, count, stride):
    return pl.ds(r, count) if stride == 1 else pl.ds(r, count, stride=stride)


def _qkv_split(proj, cos_t, sin_t, *, name, tm=512):
    n_rows = proj.shape[0]
    n_g = len(DILATIONS)

    def body(*refs):
        n_src = LANE_HALVES * 3 * n_g
        src, tables, dst = refs[:n_src], refs[n_src:n_src + 2 * LANE_HALVES], refs[n_src + 2 * LANE_HALVES:]
        for kind in range(3):
            for g, dil in enumerate(DILATIONS):
                for half in range(LANE_HALVES):
                    x_ref, o_ref = src[(kind * n_g + g) * LANE_HALVES + half], dst[kind * n_g + g]
                    cos_ref, sin_ref = tables[half], tables[LANE_HALVES + half]
                    for r in range(dil):
                        rows = _strided_rows(r, tm // dil, dil)
                        t = x_ref[rows, :]
                        if kind < 2:
                            t = _rope(t, cos_ref[rows, :], sin_ref[rows, :])
                        lo = r * ATT_MERGED + half * LANES
                        o_ref[:, lo:lo + LANES] = t.astype(o_ref.dtype)

    half_spec = lambda cb: pl.BlockSpec((tm, LANES), functools.partial(lambda i, cb: (i, cb), cb=cb))
    in_specs = [half_spec((off + g) * LANE_HALVES + half)
                for off in (OFF_Q_BLK, OFF_K_BLK, OFF_V_BLK) for g in range(n_g) for half in range(LANE_HALVES)]
    in_specs += [half_spec(half) for _ in range(2) for half in range(LANE_HALVES)]
    out_specs = [pl.BlockSpec((tm // dil, dil * ATT_MERGED), lambda i: (i, 0)) for _ in range(3) for dil in DILATIONS]
    out_shape = [jax.ShapeDtypeStruct((n_rows // dil, dil * ATT_MERGED), MXU_DTYPE) for _ in range(3) for dil in DILATIONS]
    outs = pl.pallas_call(
        body, name=name, grid=(n_rows // tm,), in_specs=in_specs, out_specs=out_specs, out_shape=out_shape,
        compiler_params=_params("parallel"),
    )(*[proj] * (LANE_HALVES * 3 * n_g), *[cos_t] * LANE_HALVES, *[sin_t] * LANE_HALVES)
    return outs[:n_g], outs[n_g:2 * n_g], outs[2 * n_g:]


def _mix(gs, ga, z1, z2, b_att):
    return jax.nn.sigmoid(gs) * (z1 * jax.nn.sigmoid(z2)) + jax.nn.sigmoid(ga) * b_att


def _mix_rows(proj, z, b_att):
    return [(proj, D_MODEL, OFF_GS_BLK), (proj, D_MODEL, OFF_GA_BLK), (z, D_MODEL, 0), (z, D_MODEL, 1), (b_att, D_MODEL, 0)]


def _mix_fwd(proj, z, b_att, *, name):
    return _rowcall(_mix, _mix_rows(proj, z, b_att), [], [(D_MODEL, MXU_DTYPE)],
                    n_rows=proj.shape[0], tm=256, name=name)[0]


def _mix_bwd(dmixed, proj, z, b_att, *, name):
    def fn(dm, gs, ga, z1, z2, ba):
        _, vjp = jax.vjp(_mix, gs, ga, z1, z2, ba)
        dgs, dga, dz1, dz2, dba = vjp(dm)
        dz = jnp.concatenate([dz1, dz2], axis=1)
        return dgs, dga, dz, dba, _colsum(dgs), _colsum(dga), _colsum(dz)

    rows = [(dmixed, D_MODEL, 0)] + _mix_rows(proj, z, b_att)
    widths = [D_MODEL, D_MODEL, 2 * D_MODEL, D_MODEL]
    return _rowcall(fn, rows, [], [(w, MXU_DTYPE) for w in widths], widths[:3], n_rows=proj.shape[0], tm=256, name=name)


def _gelu_bwd(dgy, y, proj, *, name):
    def fn(dg, yv, u):
        _, vjp = jax.vjp(jax.nn.gelu, yv)
        dy = vjp(dg)[0]
        return dy, _colsum(dy * u)

    return _rowcall(fn, [(dgy, SSM_WIDTH, 0), (y, SSM_WIDTH, 0), (proj, SSM_WIDTH, 0)], [], [(SSM_WIDTH, F32)],
                    [SSM_WIDTH], n_rows=y.shape[0], tm=512, name=name)


HEAD_ROWS = ATT_HEADS_PER_GROUP * ATT_BLK


def _head_masks(rows):
    head = lax.broadcasted_iota(jnp.int32, (rows, ATT_MERGED), 1) >> (ATT_HEAD_DIM.bit_length() - 1)
    return [head == h for h in range(ATT_HEADS_PER_GROUP)]


def _stack_heads(t, masks):
    return jnp.concatenate([jnp.where(m, t, jnp.zeros_like(t)) for m in masks], axis=0)


def _unstack_heads(t4, masks):
    blocks = [t4[h * ATT_BLK:(h + 1) * ATT_BLK] for h in range(ATT_HEADS_PER_GROUP)]
    return jnp.where(masks[0], blocks[0], jnp.where(masks[1], blocks[1], jnp.where(masks[2], blocks[2], blocks[3])))


def _head_column(stats, first):
    return jnp.concatenate([stats[:, first + h:first + h + 1] for h in range(ATT_HEADS_PER_GROUP)], axis=0)


def _band_mask(first_key):
    qi = lax.broadcasted_iota(jnp.int32, (HEAD_ROWS, 2 * ATT_BLK), 0) & (ATT_BLK - 1)
    ki = lax.broadcasted_iota(jnp.int32, (HEAD_ROWS, 2 * ATT_BLK), 1)
    steps = qi + ATT_BLK - ki
    return (steps >= 0) & (steps <= ATT_BLK) & (ki >= first_key)


def _dil_fwd(q, k, v, dil, *, name):
    n_blk = q.shape[0] // ATT_BLK
    cur = pl.BlockSpec((ATT_BLK, ATT_MERGED), lambda r, n: (n, r))
    prev = pl.BlockSpec((ATT_BLK, ATT_MERGED), lambda r, n: (jnp.maximum(n - 1, 0), r))

    def body(q_ref, kp_ref, kc_ref, vp_ref, vc_ref, o_ref, l_ref):
        masks = _head_masks(ATT_BLK)
        valid = _band_mask(jnp.where(pl.program_id(1) > 0, 0, ATT_BLK))
        keys = jnp.concatenate([kp_ref[...], kc_ref[...]], axis=0)
        vals = jnp.concatenate([vp_ref[...], vc_ref[...]], axis=0)
        s = jnp.where(valid, _dot(_stack_heads(q_ref[...], masks), keys, 1, 1) * ATT_SCALE, NEG_INF)
        m = jnp.max(s, axis=-1, keepdims=True)
        p = jnp.exp(s - m)
        den = jnp.sum(p, axis=-1, keepdims=True)
        o_ref[...] = _unstack_heads(_dot(p, vals, 1, 0) / den, masks)
        l_ref[...] = _unstack_heads(jnp.broadcast_to(m + jnp.log(den), (HEAD_ROWS, ATT_MERGED)), masks)

    shape = jax.ShapeDtypeStruct(q.shape, F32)
    return pl.pallas_call(
        body, name=name, grid=(dil, n_blk), in_specs=[cur, prev, cur, prev, cur], out_specs=[cur, cur],
        out_shape=[shape, shape], compiler_params=_params("parallel", "parallel"),
    )(q, k, k, v, v)


def _att_merge(outs, lses, *, name, tm=512):
    n_g = len(outs)
    n_rows = outs[0].shape[0] * DILATIONS[0]

    def body(*refs):
        src, (att_ref, lse_ref), tmp = refs[:2 * n_g], refs[2 * n_g:2 * n_g + 2], refs[2 * n_g + 2:]
        vals = []
        for idx, src_ref in enumerate(src):
            dil = DILATIONS[idx % n_g]
            if dil == 1:
                vals.append(src_ref[...])
                continue
            for r in range(dil):
                for half in range(LANE_HALVES):
                    lo = r * ATT_MERGED + half * LANES
                    tmp[LANE_HALVES * idx + half][_strided_rows(r, tm // dil, dil), :] = src_ref[:, lo:lo + LANES]
            vals.append(jnp.concatenate([tmp[LANE_HALVES * idx + half][...] for half in range(LANE_HALVES)], axis=1))
        o, l = vals[:n_g], vals[n_g:]
        m = functools.reduce(jnp.maximum, l)
        e = [jnp.exp(li - m) for li in l]
        z = functools.reduce(jnp.add, e)
        att_ref[...] = functools.reduce(jnp.add, [(ei / z) * oi for ei, oi in zip(e, o)])
        lse_ref[...] = m + jnp.log(z)

    in_specs = [pl.BlockSpec((tm // dil, dil * ATT_MERGED), lambda i: (i, 0)) for _ in range(2) for dil in DILATIONS]
    row = pl.BlockSpec((tm, ATT_MERGED), lambda i: (i, 0))
    shape = jax.ShapeDtypeStruct((n_rows, ATT_MERGED), F32)
    return pl.pallas_call(
        body, name=name, grid=(n_rows // tm,), in_specs=in_specs, out_specs=[row, row], out_shape=[shape, shape],
        scratch_shapes=[pltpu.VMEM((tm, LANES), F32)] * (LANE_HALVES * 2 * n_g), compiler_params=_params("parallel"),
    )(*outs, *lses)


def _att_stats(datt, att, lse, *, name):
    n_rows = datt.shape[0]

    def fn(d, a, l):
        prod = d * a
        lane = lax.broadcasted_iota(jnp.int32, (d.shape[0], LANES), 1)
        out = jnp.zeros((d.shape[0], LANES), F32)
        for h in range(ATT_HEADS_PER_GROUP):
            lo = h * ATT_HEAD_DIM
            out = jnp.where(lane == h, l[:, lo:lo + 1], out)
            delta = jnp.sum(prod[:, lo:lo + ATT_HEAD_DIM], axis=-1, keepdims=True)
            out = jnp.where(lane == ATT_HEADS_PER_GROUP + h, delta, out)
        return out

    rows = [(t, ATT_MERGED, 0) for t in (datt, att, lse)]
    return _rowcall(fn, rows, [], [(LANES, F32)], n_rows=n_rows, tm=512, name=name)[0]


def _dil_bwd(q, k, v, datt, stats, dil, *, name):
    n_rows = datt.shape[0]
    n_blk = n_rows // dil // ATT_BLK
    span = ATT_BLK * dil
    cur = pl.BlockSpec((ATT_BLK, ATT_MERGED), lambda n, r: (n, r))
    prev = pl.BlockSpec((ATT_BLK, ATT_MERGED), lambda n, r: (jnp.maximum(n - 1, 0), r))
    nxt = pl.BlockSpec((ATT_BLK, ATT_MERGED), lambda n, r: (jnp.minimum(n + 1, n_blk - 1), r))
    seq = lambda half, ahead: pl.BlockSpec((span, LANES), lambda n, r: (jnp.minimum(n + ahead, n_blk - 1), half))

    def body(qc_ref, qn_ref, kp_ref, kc_ref, vp_ref, vc_ref, dc0_ref, dc1_ref, dn0_ref, dn1_ref, sc_ref, sn_ref,
             dq0_ref, dq1_ref, dk0_ref, dk1_ref, dv0_ref, dv1_ref):
        n = pl.program_id(0)
        rows = slice(None) if dil == 1 else _strided_rows(pl.program_id(1), ATT_BLK, dil)

        def read(ref0, ref1):
            return jnp.concatenate([ref0[rows, :], ref1[rows, :]], axis=1)

        def write(ref0, ref1, val):
            ref0[rows, :] = val[:, :LANES]
            ref1[rows, :] = val[:, LANES:]

        masks = _head_masks(ATT_BLK)
        valid = _band_mask(jnp.where(n > 0, 0, ATT_BLK))
        qi = lax.broadcasted_iota(jnp.int32, (HEAD_ROWS, ATT_BLK), 0) & (ATT_BLK - 1)
        ki = lax.broadcasted_iota(jnp.int32, (HEAD_ROWS, ATT_BLK), 1)
        valid_next = (ki - qi) >= jnp.where(n < n_blk - 1, 0, ATT_BLK)

        kc, vc = kc_ref[...], vc_ref[...]
        keys = jnp.concatenate([kp_ref[...], kc], axis=0)
        vals = jnp.concatenate([vp_ref[...], vc], axis=0)
        q4 = _stack_heads(qc_ref[...], masks)
        d4 = _stack_heads(read(dc0_ref, dc1_ref).astype(MXU_DTYPE), masks)
        st = sc_ref[rows, :]
        p = jnp.where(valid, jnp.exp(_dot(q4, keys, 1, 1) * ATT_SCALE - _head_column(st, 0)), 0.0)
        ds = p * (_dot(d4, vals, 1, 1) - _head_column(st, ATT_HEADS_PER_GROUP)) * ATT_SCALE
        write(dq0_ref, dq1_ref, _unstack_heads(_dot(ds, keys, 1, 0), masks))

        q4n = _stack_heads(qn_ref[...], masks)
        d4n = _stack_heads(read(dn0_ref, dn1_ref).astype(MXU_DTYPE), masks)
        stn = sn_ref[rows, :]
        p_n = jnp.where(valid_next, jnp.exp(_dot(q4n, kc, 1, 1) * ATT_SCALE - _head_column(stn, 0)), 0.0)
        ds_n = p_n * (_dot(d4n, vc, 1, 1) - _head_column(stn, ATT_HEADS_PER_GROUP)) * ATT_SCALE
        write(dv0_ref, dv1_ref, _dot(p[:, ATT_BLK:], d4, 0, 0) + _dot(p_n, d4n, 0, 0))
        write(dk0_ref, dk1_ref, _dot(ds[:, ATT_BLK:], q4, 0, 0) + _dot(ds_n, q4n, 0, 0))

    shape = jax.ShapeDtypeStruct((n_rows, LANES), F32)
    out = seq(0, 0)
    res = pl.pallas_call(
        body, name=name, grid=(n_blk, dil),
        in_specs=[cur, nxt, prev, cur, prev, cur, seq(0, 0), seq(1, 0), seq(0, 1), seq(1, 1), seq(0, 0), seq(0, 1)],
        out_specs=[out] * 6, out_shape=[shape] * 6, compiler_params=_params("parallel", "arbitrary"),
    )(q, q, k, k, v, v, datt, datt, datt, datt, stats, stats)
    return [(res[2 * i], res[2 * i + 1]) for i in range(3)]


def _dproj_assemble(du, dqkv, dgs, dga, cos_t, sin_t, *, name):
    n_g = len(DILATIONS)

    def fn(*t):
        n_half = LANE_HALVES * 3 * n_g
        du_t, halves, (dgs_t, dga_t, c, s) = t[0], t[1:1 + n_half], t[1 + n_half:]
        parts = [jnp.concatenate(halves[LANE_HALVES * i:LANE_HALVES * (i + 1)], axis=1) for i in range(3 * n_g)]
        for i in range(2 * n_g):
            parts[i] = _rope_transpose(parts[i], c, s)
        cast = [p.astype(MXU_DTYPE) for p in parts]
        return [jnp.concatenate([du_t] + cast + [dgs_t, dga_t], axis=1)] + [_colsum(p) for p in parts]

    rows = [(du, SSM_WIDTH, 0)]
    rows += [(half, LANES, 0) for i in range(3) for g in range(n_g) for half in dqkv[g][i]]
    rows += [(dgs, D_MODEL, 0), (dga, D_MODEL, 0), (cos_t, ATT_MERGED, 0), (sin_t, ATT_MERGED, 0)]
    width = SSM_WIDTH + 3 * n_g * ATT_MERGED + 2 * D_MODEL
    res = _rowcall(fn, rows, [], [(width, MXU_DTYPE)], [ATT_MERGED] * (3 * n_g), n_rows=du.shape[0], tm=256, name=name)
    return res[0], res[1:]


def _xhead(h):
    return slice(h * XATT_HEAD_DIM, (h + 1) * XATT_HEAD_DIM)


def _xatt_probs(qh, kh):
    s = _dot(qh, kh, 1, 1) * XATT_SCALE
    e = jnp.exp(s - jnp.max(s, axis=-1, keepdims=True))
    return e / jnp.sum(e, axis=-1, keepdims=True)


def _xatt_fwd(q, kv, *, name, tm=512):
    n_rows = q.shape[0]
    n_mem = kv.shape[0]

    def body(q_ref, kv_ref, o_ref):
        for h in range(XATT_HEADS):
            sl = _xhead(h)
            p = _xatt_probs(q_ref[:, sl], kv_ref[:, sl])
            o_ref[:, sl] = _dot(p, kv_ref[:, D_MODEL + h * XATT_HEAD_DIM:D_MODEL + (h + 1) * XATT_HEAD_DIM], 1, 0
                                ).astype(o_ref.dtype)

    row = pl.BlockSpec((tm, D_MODEL), lambda i: (i, 0))
    return pl.pallas_call(
        body, name=name, grid=(n_rows // tm,),
        in_specs=[row, pl.BlockSpec((n_mem, 2 * D_MODEL), lambda i: (0, 0))], out_specs=row,
        out_shape=jax.ShapeDtypeStruct((n_rows, D_MODEL), MXU_DTYPE), compiler_params=_params("parallel"),
    )(q, kv)


def _xatt_bwd(q, kv, do, *, name, tm=512):
    n_rows = q.shape[0]
    n_mem = kv.shape[0]

    def body(q_ref, kv_ref, do_ref, dq_ref, dkv_ref):
        @pl.when(pl.program_id(0) == 0)
        def _():
            dkv_ref[...] = jnp.zeros_like(dkv_ref)

        for h in range(XATT_HEADS):
            sl = _xhead(h)
            vsl = slice(D_MODEL + h * XATT_HEAD_DIM, D_MODEL + (h + 1) * XATT_HEAD_DIM)
            qh, kh, doh = q_ref[:, sl], kv_ref[:, sl], do_ref[:, sl]
            p = _xatt_probs(qh, kh)
            dp = _dot(doh, kv_ref[:, vsl], 1, 1)
            ds = p * (dp - jnp.sum(dp * p, axis=-1, keepdims=True)) * XATT_SCALE
            dq_ref[:, sl] = _dot(ds, kh, 1, 0).astype(dq_ref.dtype)
            dkv_ref[:, sl] += _dot(ds, qh, 0, 0)
            dkv_ref[:, vsl] += _dot(p, doh, 0, 0)

    row = pl.BlockSpec((tm, D_MODEL), lambda i: (i, 0))
    full = pl.BlockSpec((n_mem, 2 * D_MODEL), lambda i: (0, 0))
    return pl.pallas_call(
        body, name=name, grid=(n_rows // tm,), in_specs=[row, full, row], out_specs=[row, full],
        out_shape=[jax.ShapeDtypeStruct((n_rows, D_MODEL), MXU_DTYPE), jax.ShapeDtypeStruct((n_mem, 2 * D_MODEL), F32)],
        compiler_params=_params("arbitrary"),
    )(q, kv, do)


def _disc(logdt, a_re, a_im, b_re, b_im):
    dt = jnp.exp(logdt)
    mag = jnp.exp(a_re * dt)
    ab_re = mag * jnp.cos(a_im * dt)
    ab_im = mag * jnp.sin(a_im * dt)
    den = jnp.square(a_re) + jnp.square(a_im)
    nr = ab_re - 1.0
    f_re = (nr * a_re + ab_im * a_im) / den
    f_im = (ab_im * a_re - nr * a_im) / den
    bb_re = f_re[None] * b_re - f_im[None] * b_im
    bb_im = f_re[None] * b_im + f_im[None] * b_re
    return ab_re, ab_im, bb_re, bb_im


def _disc_transpose(logdt, a_re, a_im, b_re, b_im, g_ab_re, g_ab_im, g_bb_re, g_bb_im):
    dt = jnp.exp(logdt)
    mag = jnp.exp(a_re * dt)
    th = a_im * dt
    cs, sn = jnp.cos(th), jnp.sin(th)
    ab_re, ab_im = mag * cs, mag * sn
    den = jnp.square(a_re) + jnp.square(a_im)
    nr = ab_re - 1.0
    f_re = (nr * a_re + ab_im * a_im) / den
    f_im = (ab_im * a_re - nr * a_im) / den
    d_f_re = jnp.sum(g_bb_re * b_re + g_bb_im * b_im, axis=0)
    d_f_im = jnp.sum(g_bb_im * b_re - g_bb_re * b_im, axis=0)
    d_b_re = g_bb_re * f_re[None] + g_bb_im * f_im[None]
    d_b_im = g_bb_im * f_re[None] - g_bb_re * f_im[None]
    d_n_re, d_n_im = d_f_re / den, d_f_im / den
    d_den = -(d_f_re * f_re + d_f_im * f_im) / den
    d_ab_re = g_ab_re + d_n_re * a_re - d_n_im * a_im
    d_ab_im = g_ab_im + d_n_re * a_im + d_n_im * a_re
    d_a_re = d_n_re * nr + d_n_im * ab_im + 2.0 * d_den * a_re
    d_a_im = d_n_re * ab_im - d_n_im * nr + 2.0 * d_den * a_im
    d_mag = d_ab_re * cs + d_ab_im * sn
    d_th = mag * (d_ab_im * cs - d_ab_re * sn)
    d_a_re = d_a_re + d_mag * mag * dt
    d_a_im = d_a_im + d_th * dt
    d_dt = jnp.sum(d_mag * mag * a_re + d_th * a_im, axis=-1, keepdims=True)
    return d_dt * dt, d_a_re, d_a_im, d_b_re, d_b_im


def _whole(fn, args, out_shapes, *, name):
    n_in = len(args)

    def body(*refs):
        res = fn(*[r[...] for r in refs[:n_in]])
        for o_ref, val in zip(refs[n_in:], res):
            o_ref[...] = val

    return pl.pallas_call(body, name=name, out_shape=[jax.ShapeDtypeStruct(s, F32) for s in out_shapes],
                          compiler_params=pltpu.CompilerParams(vmem_limit_bytes=VMEM_LIMIT_BYTES))(*args)


SSM_WIDE =GROUPS_PER_TILE * SSM_STATE
LANE_GROUPS_PER_TILE = SSM_WIDE // LANES


def _chan(j):
    return slice(j * LANES, (j + 1) * LANES)


def _time_major_rows(j, q, tc):
    return pl.ds(j * LANE_GROUPS_PER_TILE + q, tc, stride=STATE_VREG_ROWS)


def _to_time_major(x, t_re_ref, t_im_ref, dst_re, dst_im, tc):
    for j in range(SSM_TILES):
        xj = x[:, _chan(j)]
        for t_ref, dst in ((t_re_ref, dst_re), (t_im_ref, dst_im)):
            r = _dot(xj, t_ref[j], 1, 0)
            for q in range(LANE_GROUPS_PER_TILE):
                dst[_time_major_rows(j, q, tc), :] = r[:, q * LANES:(q + 1) * LANES]


def _from_time_major(src, j, tc):
    return jnp.concatenate([src[_time_major_rows(j, q, tc), :] for q in range(LANE_GROUPS_PER_TILE)], axis=1)


def _scan_chunk(w_re, w_im, h_re, h_im, a_re, a_im, start, tc):
    def step(t, carry):
        hr, hi = carry
        rows = _scan_rows(t)
        nr = a_re * hr - a_im * hi + w_re[rows, :]
        ni = a_re * hi + a_im * hr + w_im[rows, :]
        h_re[rows, :] = nr
        h_im[rows, :] = ni
        return nr, ni

    return lax.fori_loop(0, tc, step, start, unroll=8)


SSM_CHUNK = 256


def _tile_spec(stack, k):
    return pl.BlockSpec((pl.Squeezed(),) + tuple(stack.shape[1:]), lambda i: (k, 0, 0, 0))


def _expand_block_diagonal(src_ref, dst):
    dst[...] = jnp.zeros_like(dst)
    r, c = src_ref.shape[1:]
    for g in range(SSM_GROUPS):
        j, gl = divmod(g, GROUPS_PER_TILE)
        dst[j, gl * r:(gl + 1) * r, gl * c:(gl + 1) * c] = src_ref[g].astype(dst.dtype)


def _extract_block_diagonal(src, dst_ref):
    r, c = dst_ref.shape[1:]
    for g in range(SSM_GROUPS):
        j, gl = divmod(g, GROUPS_PER_TILE)
        dst_ref[g] = src[j, gl * r:(gl + 1) * r, gl * c:(gl + 1) * c]


def _ssm_fwd(proj, blocks_cn, blocks_nc, a_re, a_im, gain, *, name, tc=SSM_CHUNK):
    n_rows = proj.shape[0]
    n_chunk = n_rows // tc

    def body(u_ref, br_ref, bi_ref, cr_ref, ci_ref, ar_ref, ai_ref, g_ref, y_ref, gy_ref, hr, hi, wr, wi, state,
             tbr_ref, tbi_ref, tcr_ref, tci_ref):
        @pl.when(pl.program_id(0) == 0)
        def _():
            state[...] = jnp.zeros_like(state)
            for src_ref, dst in ((br_ref, tbr_ref), (bi_ref, tbi_ref), (cr_ref, tcr_ref), (ci_ref, tci_ref)):
                _expand_block_diagonal(src_ref, dst)

        u = u_ref[...]
        _to_time_major(u, tbr_ref, tbi_ref, wr, wi, tc)
        state[0], state[1] = _scan_chunk(wr, wi, hr, hi, ar_ref[...], ai_ref[...], (state[0], state[1]), tc)
        for j in range(SSM_TILES):
            yj = (_dot(_from_time_major(hr, j, tc), tcr_ref[j], 1, 0) + _dot(_from_time_major(hi, j, tc), tci_ref[j], 1, 0)
                  + g_ref[:, _chan(j)] * u[:, _chan(j)])
            y_ref[:, _chan(j)] = yj
            gy_ref[:, _chan(j)] = jax.nn.gelu(yj).astype(gy_ref.dtype)

    rows = pl.BlockSpec((tc, SSM_WIDTH), lambda i: (i, 0))
    coef = pl.BlockSpec((STATE_VREG_ROWS, LANES), lambda i: (0, 0))
    states = pl.BlockSpec((tc * STATE_VREG_ROWS, LANES), lambda i: (i, 0))
    sshape = jax.ShapeDtypeStruct((n_rows * STATE_VREG_ROWS, LANES), F32)
    return pl.pallas_call(
        body, name=name, grid=(n_chunk,),
        in_specs=[rows, _tile_spec(blocks_cn, 0), _tile_spec(blocks_cn, 1), _tile_spec(blocks_nc, 0),
                  _tile_spec(blocks_nc, 1), coef, coef, pl.BlockSpec((1, SSM_WIDTH), lambda i: (0, 0))],
        out_specs=[rows, rows, states, states],
        out_shape=[jax.ShapeDtypeStruct((n_rows, SSM_WIDTH), F32), jax.ShapeDtypeStruct((n_rows, SSM_WIDTH), MXU_DTYPE),
                   sshape, sshape],
        scratch_shapes=[pltpu.VMEM((tc * STATE_VREG_ROWS, LANES), F32)] * 2 + [pltpu.VMEM((2, STATE_VREG_ROWS, LANES), F32)]
        + [pltpu.VMEM((SSM_TILES, LANES, SSM_WIDE), MXU_DTYPE)] * 2 + [pltpu.VMEM((SSM_TILES, SSM_WIDE, LANES), MXU_DTYPE)] * 2,
        compiler_params=_params("arbitrary"),
    )(proj, blocks_cn, blocks_cn, blocks_nc, blocks_nc, a_re, a_im, gain)


def _ssm_bwd(proj, dy, h_re, h_im, blocks_cn, blocks_nc, a_re, a_im, gain, *, name, tc=SSM_CHUNK):
    n_rows = proj.shape[0]
    n_chunk = n_rows // tc

    def body(u_ref, dy_ref, hr, hi, cr_ref, ci_ref, br_ref, bi_ref, ar_ref, ai_ref, g_ref,
             du_ref, su_ref, dc_re_ref, dc_im_ref, db_re_ref, db_im_ref, dar_ref, dai_ref, wr, wi, carry,
             tdr_ref, tdi_ref, tur_ref, tui_ref, dcr_ref, dci_ref, dbr_ref, dbi_ref):
        @pl.when(pl.program_id(0) == 0)
        def _():
            carry[...] = jnp.zeros_like(carry)
            for acc_ref in (su_ref, dcr_ref, dci_ref, dbr_ref, dbi_ref):
                acc_ref[...] = jnp.zeros_like(acc_ref)
            for src_ref, dst in ((cr_ref, tdr_ref), (ci_ref, tdi_ref), (br_ref, tur_ref), (bi_ref, tui_ref)):
                _expand_block_diagonal(src_ref, dst)

        a_r, a_i = ar_ref[...], ai_ref[...]
        u, dyv = u_ref[...], dy_ref[...]
        _to_time_major(dyv, tdr_ref, tdi_ref, wr, wi, tc)

        def step(kk, c):
            lam_r, lam_i, dar, dai = c
            rows = _scan_rows(tc - 1 - kk)
            h_r, h_i = hr[rows, :], hi[rows, :]
            dar = dar + lam_r * h_r + lam_i * h_i
            dai = dai + lam_i * h_r - lam_r * h_i
            new_r = wr[rows, :] + a_r * lam_r + a_i * lam_i
            new_i = wi[rows, :] + a_r * lam_i - a_i * lam_r
            wr[rows, :] = new_r
            wi[rows, :] = new_i
            return new_r, new_i, dar, dai

        carry[0], carry[1], carry[2], carry[3] = lax.fori_loop(0, tc, step, (carry[0], carry[1], carry[2], carry[3]),
                                                              unroll=8)
        dar_ref[...] = carry[2]
        dai_ref[...] = carry[3]
        for j in range(SSM_TILES):
            cj = _chan(j)
            lam_r, lam_i = _from_time_major(wr, j, tc), _from_time_major(wi, j, tc)
            dcr_ref[j] += _dot(dyv[:, cj], _from_time_major(hr, j, tc), 0, 0)
            dci_ref[j] += _dot(dyv[:, cj], _from_time_major(hi, j, tc), 0, 0)
            dbr_ref[j] += _dot(u[:, cj], lam_r, 0, 0)
            dbi_ref[j] += _dot(u[:, cj], lam_i, 0, 0)
            duj = _dot(lam_r, tur_ref[j], 1, 0) + _dot(lam_i, tui_ref[j], 1, 0) + g_ref[:, cj] * dyv[:, cj]
            du_ref[:, cj] = duj.astype(du_ref.dtype)
            su_ref[:, cj] += _colsum(duj)

        @pl.when(pl.program_id(0) == n_chunk - 1)
        def _():
            for src, dst_ref in ((dcr_ref, dc_re_ref), (dci_ref, dc_im_ref), (dbr_ref, db_re_ref), (dbi_ref, db_im_ref)):
                _extract_block_diagonal(src, dst_ref)

    back = lambda i: (n_chunk - 1 - i, 0)
    rows = pl.BlockSpec((tc, SSM_WIDTH), back)
    blocks = pl.BlockSpec((SSM_GROUPS, SSM_GROUP, SSM_STATE), lambda i: (0, 0, 0))
    coef = pl.BlockSpec((STATE_VREG_ROWS, LANES), lambda i: (0, 0))
    states = pl.BlockSpec((tc * STATE_VREG_ROWS, LANES), back)
    vec = pl.BlockSpec((1, SSM_WIDTH), lambda i: (0, 0))
    bshape = jax.ShapeDtypeStruct((SSM_GROUPS, SSM_GROUP, SSM_STATE), F32)
    cshape = jax.ShapeDtypeStruct((STATE_VREG_ROWS, LANES), F32)
    return pl.pallas_call(
        body, name=name, grid=(n_chunk,),
        in_specs=[rows, rows, states, states, _tile_spec(blocks_cn, 2), _tile_spec(blocks_cn, 3), _tile_spec(blocks_nc, 2),
                  _tile_spec(blocks_nc, 3), coef, coef, vec],
        out_specs=[rows, vec, blocks, blocks, blocks, blocks, coef, coef],
        out_shape=[jax.ShapeDtypeStruct((n_rows, SSM_WIDTH), MXU_DTYPE), jax.ShapeDtypeStruct((1, SSM_WIDTH), F32),
                   bshape, bshape, bshape, bshape, cshape, cshape],
        scratch_shapes=[pltpu.VMEM((tc * STATE_VREG_ROWS, LANES), F32)] * 2 + [pltpu.VMEM((4, STATE_VREG_ROWS, LANES), F32)]
        + [pltpu.VMEM((SSM_TILES, LANES, SSM_WIDE), MXU_DTYPE)] * 2 + [pltpu.VMEM((SSM_TILES, SSM_WIDE, LANES), MXU_DTYPE)] * 2
        + [pltpu.VMEM((SSM_TILES, LANES, SSM_WIDE), F32)] * 4,
        compiler_params=_params("arbitrary"),
    )(proj, dy, h_re, h_im, blocks_cn, blocks_cn, blocks_nc, blocks_nc, a_re, a_im, gain)


def _scan_rows(t):
    return pl.ds(pl.multiple_of(t * STATE_VREG_ROWS, 8), STATE_VREG_ROWS)


GATHER_GROUPS = (("w_glu", "w_att_up", "w_mix_out"), ("w_xq", "w_xkv", "w_xo", "w_ff1", "w_ff2"))
SCATTER_GROUPS = (("w_ff2", "w_ff1"), ("w_xo", "w_xq", "w_xkv", "w_mix_out"), ("w_att_up", "w_glu"), ("w_in",))


def _local_grads(x, mem, pos_col, target, sm, fetch_in, fetch, send, send_small, start_token):
    b_re_t = sm["ssm_b_re"].transpose(2, 0, 1)
    b_im_t = sm["ssm_b_im"].transpose(2, 0, 1)
    logdt = sm["ssm_log_dt"].reshape(SSM_GROUPS, 1)
    c_re, c_im = sm["ssm_c_re"], sm["ssm_c_im"]
    grp = (SSM_GROUPS, SSM_STATE)
    chn = (SSM_GROUP, SSM_GROUPS, SSM_STATE)

    wts = {}
    cos_t, sin_t = _rope_tables(pos_col, after=start_token, name="rope_tables")
    h0, xh0, rs0, h0m = _ln_fwd(x, None, sm["ln_in_g"], sm["ln_in_b"], alpha=1.0, name="ln_in_fwd")
    disc_in = (logdt, sm["ssm_a_re"], sm["ssm_a_im"], b_re_t, b_im_t)
    ab_re, ab_im, bb_re_t, bb_im_t = _whole(_disc, disc_in, [grp, grp, chn, chn], name="ssm_disc")
    a_re_rows, a_im_rows = ab_re.reshape(STATE_VREG_ROWS, LANES), ab_im.reshape(STATE_VREG_ROWS, LANES)
    tiles_cn = jnp.stack([bb_re_t.transpose(1, 0, 2), bb_im_t.transpose(1, 0, 2), c_re, -c_im])
    tiles_nc = jnp.stack([c_re.transpose(0, 2, 1), -c_im.transpose(0, 2, 1), bb_re_t.transpose(1, 2, 0),
                          bb_im_t.transpose(1, 2, 0)])
    w_in_near, near_ids = fetch_in(0, [h0m, tiles_cn, tiles_nc])
    proj = _mm_shards(h0m, w_in_near, sm["b_in"], near_ids, name="in_proj_near")
    wts["w_in"], far_ids = fetch_in(1, [proj])
    proj = _mm_shards(h0m, wts["w_in"], sm["b_in"], far_ids, prev=proj, name="in_proj_far")

    y, gy, h_re, h_im = _ssm_fwd(proj, tiles_cn, tiles_nc, a_re_rows, a_im_rows, sm["ssm_d"], name="ssm_fwd")

    q, k, v = _qkv_split(proj, cos_t, sin_t, name="qkv_split")
    outs, lses = [], []
    for g, dil in enumerate(DILATIONS):
        o_g, l_g = _dil_fwd(q[g], k[g], v[g], dil, name=f"dil_att_fwd_{dil}")
        outs.append(o_g)
        lses.append(l_g)
    att, lse = _att_merge(outs, lses, name="att_merge")
    wts.update(fetch(0, [att]))
    z = _mm(gy, wts["w_glu"], bias=sm["b_glu"], b_shards=True, name="glu_proj")
    b_att = _mm(att, wts["w_att_up"], b_shards=True, name="att_up")

    mixed = _mix_fwd(proj, z, b_att, name="gate_mix")
    mix_out = _mm(mixed, wts["w_mix_out"], bias=sm["b_mix_out"], name="mix_out")
    h1, xh1, rs1, h1m = _ln_fwd(h0, mix_out, sm["ln1_g"], sm["ln1_b"], alpha=DEEPNORM_ALPHA, name="ln1_fwd")

    wts.update(fetch(1, [h1m]))
    xq = _mm(h1m, wts["w_xq"], out_dtype=MXU_DTYPE, name="xatt_q")
    kv = _mm(mem, wts["w_xkv"], out_dtype=MXU_DTYPE, b_shards=True, name="xatt_kv")
    xo_in = _xatt_fwd(xq, kv, name="xatt_fwd")
    xo = _mm(xo_in, wts["w_xo"], name="xatt_o")
    h2, xh2, rs2, h2m = _ln_fwd(h1, xo, sm["ln2_g"], sm["ln2_b"], alpha=DEEPNORM_ALPHA, name="ln2_fwd")

    pre, act = _mm(h2m, wts["w_ff1"], bias=sm["b_ff1"], b_shards=True, name="ff1",
                   also=(lambda r: jnp.square(jnp.maximum(r, 0.0)), MXU_DTYPE))
    ff = _mm(act, wts["w_ff2"], bias=sm["b_ff2"], name="ff2")

    gw, gs = {}, {}
    dr3, dr3m, gs["ln3_g"], gs["ln3_b"], gs["b_ff2"], loss_row = _ln_loss_bwd(
        h2, ff, target, sm["ln3_g"], sm["ln3_b"], alpha=DEEPNORM_ALPHA, name="ln3_loss")
    wgrad = functools.partial(_mm, ta=True, out_dtype=WIRE_DTYPE, tk=2048)
    gw["w_ff2"] = wgrad(act, dr3m, tk=1024, name="ff2_dw")
    dpre, gs["b_ff1"] = _mm(dr3m, wts["w_ff2"], tb=True, out_dtype=MXU_DTYPE, colsum=True, name="ff2_dx",
                            gate=(pre, lambda p: 2.0 * jnp.maximum(p, 0.0)))
    gw["w_ff1"] = wgrad(h2m, dpre, out_shards=True, name="ff1_dw")
    sent = send(0, gw)
    dh2 = _mm(dpre, wts["w_ff1"], tb=True, b_shards=True, after=sent, name="ff1_dx")

    dr2, dr2m, gs["ln2_g"], gs["ln2_b"], _ = _ln_bwd(dr3, dh2, xh2, rs2, sm["ln2_g"], alpha=DEEPNORM_ALPHA,
                                                     name="ln2_bwd")
    gw["w_xo"] = wgrad(xo_in, dr2m, name="xatt_o_dw")
    dxo_in = _mm(dr2m, wts["w_xo"], tb=True, out_dtype=MXU_DTYPE, name="xatt_o_dx")
    dxq, dkv = _xatt_bwd(xq, kv, dxo_in, name="xatt_bwd")
    gw["w_xq"] = wgrad(h1m, dxq, name="xatt_q_dw")
    gw["w_xkv"] = wgrad(mem, dkv, out_shards=True, name="xatt_kv_dw")
    dh1 = _mm(dxq, wts["w_xq"], tb=True, name="xatt_q_dx")

    dr1, dr1m, gs["ln1_g"], gs["ln1_b"], gs["b_mix_out"] = _ln_bwd(dr2, dh1, xh1, rs1, sm["ln1_g"],
                                                                   alpha=DEEPNORM_ALPHA, name="ln1_bwd")
    gw["w_mix_out"] = wgrad(mixed, dr1m, name="mix_out_dw")
    sent = send(1, gw)
    dmixed = _mm(dr1m, wts["w_mix_out"], tb=True, after=sent, name="mix_out_dx")
    dgs, dga, dz, db_att, s_gs, s_ga, gs["b_glu"] = _mix_bwd(dmixed, proj, z, b_att, name="gate_mix_bwd")

    gw["w_att_up"] = wgrad(att, db_att, out_shards=True, name="att_up_dw")
    gw["w_glu"] = wgrad(gy, dz, out_shards=True, name="glu_dw")
    sent = send(2, gw)
    datt = _mm(db_att, wts["w_att_up"], tb=True, b_shards=True, after=sent, name="att_up_dx")
    stats = _att_stats(datt, att, lse, name="att_stats")
    dqkv = [_dil_bwd(q[g], k[g], v[g], datt, stats, dil, name=f"dil_att_bwd_{dil}") for g, dil in enumerate(DILATIONS)]

    dgy = _mm(dz, wts["w_glu"], tb=True, b_shards=True, name="glu_dx")
    dy, gs["ssm_d"] = _gelu_bwd(dgy, y, proj, name="gelu_bwd")
    du, s_u, dc_re_t, dc_im_t, dbb_re_t, dbb_im_t, da_re, da_im = _ssm_bwd(
        proj, dy, h_re, h_im, tiles_cn, tiles_nc, a_re_rows, a_im_rows, sm["ssm_d"], name="ssm_bwd")
    gs["ssm_c_re"], gs["ssm_c_im"] = dc_re_t, -dc_im_t
    disc_ct = (da_re.reshape(grp), da_im.reshape(grp), dbb_re_t.transpose(1, 0, 2), dbb_im_t.transpose(1, 0, 2))
    d_logdt, gs["ssm_a_re"], gs["ssm_a_im"], d_b_re_t, d_b_im_t = _whole(
        _disc_transpose, disc_in + disc_ct, [(SSM_GROUPS, 1), grp, grp, chn, chn], name="ssm_disc_bwd")
    gs["ssm_log_dt"] = d_logdt
    gs["ssm_b_re"], gs["ssm_b_im"] = d_b_re_t.transpose(1, 2, 0), d_b_im_t.transpose(1, 2, 0)

    dproj, s_qkv = _dproj_assemble(du, dqkv, dgs, dga, cos_t, sin_t, name="dproj_assemble")
    gs["b_in"] = jnp.concatenate([s_u, *s_qkv, s_gs, s_ga], axis=1)
    sent = send_small(gs, SMALL_EARLY)
    gw["w_in"] = wgrad(h0m, dproj, out_shards=True, after=sent, name="in_proj_dw")
    sent = send(3, gw)
    dh0 = _mm(dproj, wts["w_in"], tb=True, b_shards=True, after=sent, name="in_proj_dx")
    grad_x, gs["ln_in_g"], gs["ln_in_b"], _ = _ln_bwd(dr1, dh0, xh0, rs0, sm["ln_in_g"], alpha=DEEPNORM_ALPHA,
                                                      operand=False, name="ln_in_bwd")
    return loss_row, grad_x, gs


N_PEER = N_DEV - 1
_IN_HBM = pl.BlockSpec(memory_space=pltpu.HBM)
_IN_SEMAPHORE = pl.BlockSpec(memory_space=pltpu.SEMAPHORE)


def _device_index():
    return 4 * lax.axis_index("x") + 2 * lax.axis_index("y") + lax.axis_index("c")


ALL_PEERS = tuple(range(1, N_DEV))
NEAR_PEERS = (1, 2, 3, 4, 5)
FAR_PEERS = (6, 7)


def _peer_index(kk):
    x, y, c = lax.axis_index("x"), lax.axis_index("y"), lax.axis_index("c")
    return 4 * ((x + (kk >> 2)) % 2) + 2 * ((y + ((kk >> 1) & 1)) % 2) + (c + (kk & 1)) % 2


def _exchange_copies(src_refs, land_refs, send_sems, recv_sems, scatter, peers):
    x, y, c = lax.axis_index("x"), lax.axis_index("y"), lax.axis_index("c")
    me = 4 * x + 2 * y + c
    pairs = []
    for a, (src_ref, land_ref) in enumerate(zip(src_refs, land_refs)):
        for idx, kk in enumerate(peers):
            px = (x + (kk >> 2)) % 2
            py = (y + ((kk >> 1) & 1)) % 2
            pc = (c + (kk & 1)) % 2
            peer = 4 * px + 2 * py + pc
            sem = a * len(peers) + idx
            src = src_ref.at[peer] if scatter else src_ref

            def copy(dst, src=src, sem=sem, px=px, py=py, pc=pc):
                return pltpu.make_async_remote_copy(
                    src_ref=src, dst_ref=dst, send_sem=send_sems.at[sem], recv_sem=recv_sems.at[sem],
                    device_id=(px, py, pc), device_id_type=pl.DeviceIdType.MESH)

            pairs.append((functools.partial(copy, land_ref.at[me]), functools.partial(copy, land_ref.at[peer])))
    return pairs


def _own_copies(src_refs, land_refs, own_sems, scatter):
    me = _device_index()
    return [functools.partial(pltpu.make_async_copy, src_ref.at[me] if scatter else src_ref, land_ref.at[me],
                              own_sems.at[a]) for a, (src_ref, land_ref) in enumerate(zip(src_refs, land_refs))]


def _exchange_start(srcs, *, scatter, name, after=None, peers=ALL_PEERS, lands=None):
    n_arr, n_sem = len(srcs), len(srcs) * len(peers)
    own = lands is None
    if own:
        lands = [lax.empty((N_DEV,) + tuple(s.shape[1:] if scatter else s.shape), s.dtype) for s in srcs]
    n_in = 2 * n_arr + (after is not None)

    def body(*refs):
        send_sems, recv_sems = refs[n_in], refs[n_in + 1]
        for sent, _ in _exchange_copies(refs[:n_arr], refs[n_arr:2 * n_arr], send_sems, recv_sems, scatter, peers):
            sent().start()
        if own:
            for local in _own_copies(refs[:n_arr], refs[n_arr:2 * n_arr], refs[n_in + 2], scatter):
                local().start()
        refs[-1][...] = jnp.zeros_like(refs[-1])

    sems = [pltpu.SemaphoreType.DMA((n_sem,)), pltpu.SemaphoreType.DMA((n_sem,))] + [pltpu.SemaphoreType.DMA((n_arr,))] * own
    through = [pltpu.HBM(t.shape, t.dtype) for t in (*srcs, *lands)]
    res = pl.pallas_call(
        body, name=name, out_shape=(*sems, *through, jax.ShapeDtypeStruct((8, LANES), F32)),
        in_specs=[_IN_HBM] * (2 * n_arr) + [pl.BlockSpec(memory_space=pl.ANY)] * (after is not None),
        out_specs=(*[_IN_SEMAPHORE] * len(sems), *[_IN_HBM] * (2 * n_arr), pl.BlockSpec(memory_space=pltpu.VMEM)),
        input_output_aliases={i: len(sems) + i for i in range(2 * n_arr)},
        compiler_params=pltpu.CompilerParams(has_side_effects=pltpu.SideEffectType.DATAFLOW_SIDE_EFFECTING),
    )(*[pltpu.with_memory_space_constraint(t, pltpu.HBM) for t in (*srcs, *lands)],
      *([after] if after is not None else []))
    first = len(sems)
    handle = dict(sems=res[:first], srcs=res[first:first + n_arr], lands=res[first + n_arr:first + 2 * n_arr],
                  scatter=scatter, peers=peers, own=own)
    return handle, res[-1]


def _exchange_wait(handle, *, after, name, srcs=None, lands=None):
    srcs = handle["srcs"] if srcs is None else srcs
    lands = handle["lands"] if lands is None else lands
    sems, scatter, peers, own = handle["sems"], handle["scatter"], handle["peers"], handle["own"]
    n_arr = len(srcs)
    after = list(after)

    def body(*refs):
        src_refs, land_refs = refs[:n_arr], refs[n_arr:2 * n_arr]
        for sent, received in _exchange_copies(src_refs, land_refs, refs[2 * n_arr], refs[2 * n_arr + 1], scatter, peers):
            sent().wait_send()
            received().wait_recv()
        if own:
            for local in _own_copies(src_refs, land_refs, refs[2 * n_arr + 2], scatter):
                local().wait()

    res = pl.pallas_call(
        body, name=name, out_shape=tuple(pltpu.HBM(t.shape, t.dtype) for t in (*srcs, *lands)),
        in_specs=[_IN_HBM] * (2 * n_arr) + [_IN_SEMAPHORE] * len(sems) + [pl.BlockSpec(memory_space=pl.ANY)] * len(after),
        out_specs=tuple([_IN_HBM] * (2 * n_arr)), input_output_aliases={i: i for i in range(2 * n_arr)},
        compiler_params=pltpu.CompilerParams(has_side_effects=pltpu.SideEffectType.DATAFLOW_SIDE_EFFECTING),
    )(*srcs, *lands, *sems, *after)
    return res[:n_arr], res[n_arr:]


def _adamw(g, w, m, v):
    m_new = ADAM_B1 * m + (1.0 - ADAM_B1) * g
    v_new = ADAM_B2 * v + (1.0 - ADAM_B2) * jnp.square(g)
    m_hat = m_new / (1.0 - ADAM_B1 ** ADAM_STEP)
    v_hat = v_new / (1.0 - ADAM_B2 ** ADAM_STEP)
    return g, -ADAM_LR * (m_hat / (jnp.sqrt(v_hat) + ADAM_EPS) + ADAM_WD * w), m_new, v_new


def _reduce_adamw(gstack, w, m, v, *, name, tr=128):
    n_rows, cols = w.shape
    tr = min(tr, n_rows)
    assert n_rows % tr == 0, (name, n_rows, tr)

    def body(g_ref, w_ref, m_ref, v_ref, *out_refs):
        g = g_ref[0].astype(F32)
        for dev in range(1, N_DEV):
            g = g + g_ref[dev].astype(F32)
        for o_ref, val in zip(out_refs, _adamw(g, w_ref[...], m_ref[...], v_ref[...])):
            o_ref[...] = val

    flat = pl.BlockSpec((tr, cols), lambda i: (i, 0))
    shape = jax.ShapeDtypeStruct((n_rows, cols), F32)
    return pl.pallas_call(
        body, name=name, grid=(n_rows // tr,),
        in_specs=[pl.BlockSpec((N_DEV, tr, cols), lambda i: (0, i, 0)), flat, flat, flat],
        out_specs=[flat] * 4, out_shape=[shape] * 4, compiler_params=_params("parallel"),
    )(gstack, w, m, v)


SMALL_FLAT_SSM = ("ssm_b_re", "ssm_b_im", "ssm_c_re", "ssm_c_im")


def _small_view(name, shape):
    size = int(np.prod(shape))
    if name in SMALL_FLAT_SSM:
        return SSM_GROUPS, size // SSM_GROUPS
    if name in ("ssm_a_re", "ssm_a_im"):
        return SSM_GROUPS, SSM_STATE
    return 1, size


def _pack_rows(view):
    return -(-(view[0] * view[1]) // PACK_COLS)


SMALL_LATE = ("ln_in_g", "ln_in_b")
SMALL_EARLY = tuple(n for n in SMALL if n not in SMALL_LATE)


def _pack_small(gs, names, views):
    parts = []
    for n in names:
        flat = gs[n].reshape(-1).astype(WIRE_DTYPE)
        parts.append(jnp.pad(flat, (0, _pack_rows(views[n]) * PACK_COLS - flat.shape[0])))
    total = sum(p.shape[0] for p in parts) // PACK_COLS
    parts.append(jnp.zeros(((-total % PACK_ROW_ALIGN) * PACK_COLS,), WIRE_DTYPE))
    return jnp.concatenate(parts).reshape(-1, PACK_COLS)


def _small_pieces(view):
    rows, cols = view
    if cols == PACK_COLS:
        return [(0, rows, 0, 0, 0, cols)]
    if rows == 1 and cols > PACK_COLS:
        return [(kk, 1, 0, 0, kk * PACK_COLS, PACK_COLS) for kk in range(cols // PACK_COLS)]
    if rows == 1:
        return [(0, 1, 0, 0, 0, cols)]
    return [((r * cols) // PACK_COLS, 1, (r * cols) % PACK_COLS, r, 0, cols) for r in range(rows)]


def _adamw_small(stacks, views, w, m, v, *, name):
    n = len(SMALL)
    place, first = {}, [0, 0]
    for k, names in enumerate((SMALL_EARLY, SMALL_LATE)):
        for name_ in names:
            place[name_] = (k, first[k])
            first[k] += _pack_rows(views[name_])

    def body(early_ref, late_ref, *refs):
        ins, outs = refs[:3 * n], refs[3 * n:]
        for i, name_ in enumerate(SMALL):
            stack_ref = (early_ref, late_ref)[place[name_][0]]
            row0 = place[name_][1]
            for prow, nrows, lane, orow, ocol, width in _small_pieces(views[name_]):
                src = (slice(row0 + prow, row0 + prow + nrows), slice(lane, lane + width))
                dst = (slice(orow, orow + nrows), slice(ocol, ocol + width))
                g = stack_ref[(0,) + src].astype(F32)
                for dev in range(1, N_DEV):
                    g = g + stack_ref[(dev,) + src].astype(F32)
                res = _adamw(g, ins[i][dst], ins[n + i][dst], ins[2 * n + i][dst])
                for kk, val in enumerate(res):
                    outs[kk * n + i][dst] = val

    res = pl.pallas_call(
        body, name=name, out_shape=[jax.ShapeDtypeStruct(views[name_], F32) for _ in range(4) for name_ in SMALL],
        compiler_params=pltpu.CompilerParams(vmem_limit_bytes=VMEM_LIMIT_BYTES),
    )(*stacks, *[d[name_] for d in (w, m, v) for name_ in SMALL])
    return [dict(zip(SMALL, res[kk * n:(kk + 1) * n])) for kk in range(4)]


def kernel(x, mem, positions, ln_in_g, ln_in_b, w_in, b_in, ssm_log_dt, ssm_a_re, ssm_a_im, ssm_b_re, ssm_b_im, ssm_c_re, ssm_c_im, ssm_d, w_glu, b_glu, w_att_up, w_mix_out, b_mix_out, ln1_g, ln1_b, w_xq, w_xkv, w_xo, ln2_g, ln2_b, w_ff1, b_ff1, w_ff2, b_ff2, ln3_g, ln3_b, loss_target, m_ln_in_g, m_ln_in_b, m_w_in, m_b_in, m_ssm_log_dt, m_ssm_a_re, m_ssm_a_im, m_ssm_b_re, m_ssm_b_im, m_ssm_c_re, m_ssm_c_im, m_ssm_d, m_w_glu, m_b_glu, m_w_att_up, m_w_mix_out, m_b_mix_out, m_ln1_g, m_ln1_b, m_w_xq, m_w_xkv, m_w_xo, m_ln2_g, m_ln2_b, m_w_ff1, m_b_ff1, m_w_ff2, m_b_ff2, m_ln3_g, m_ln3_b, v_ln_in_g, v_ln_in_b, v_w_in, v_b_in, v_ssm_log_dt, v_ssm_a_re, v_ssm_a_im, v_ssm_b_re, v_ssm_b_im, v_ssm_c_re, v_ssm_c_im, v_ssm_d, v_w_glu, v_b_glu, v_w_att_up, v_w_mix_out, v_b_mix_out, v_ln1_g, v_ln1_b, v_w_xq, v_w_xkv, v_w_xo, v_ln2_g, v_ln2_b, v_w_ff1, v_b_ff1, v_w_ff2, v_b_ff2, v_ln3_g, v_ln3_b):
    given = dict(locals())
    w_arg = {n: given[n] for n in WEIGHTS}
    m_arg = {n: given["m_" + n] for n in WEIGHTS}
    v_arg = {n: given["v_" + n] for n in WEIGHTS}

    in_near, token = _exchange_start([w_arg["w_in"][0].astype(MXU_DTYPE)], scatter=False, peers=NEAR_PEERS,
                                     name="gather_start_in_near")
    in_far, token = _exchange_start(in_near["srcs"], scatter=False, peers=FAR_PEERS, lands=in_near["lands"],
                                    after=token, name="gather_start_in_far")
    w_in_state = [in_far["srcs"], in_far["lands"]]
    token, w_arg, m_arg, v_arg = lax.optimization_barrier((token, w_arg, m_arg, v_arg))
    shards = {n: w_arg[n][0].astype(MXU_DTYPE) for n in BIG if n != "w_in"}
    gathers = []
    for i, names in enumerate(GATHER_GROUPS):
        handle, token = _exchange_start([shards[n] for n in names], scatter=False, after=token, name=f"gather_start_{i}")
        gathers.append(handle)

    small_views = {n: _small_view(n, w_arg[n].shape) for n in SMALL}
    small_w, small_m, small_v = [{n: d[n].reshape(small_views[n]) for n in SMALL} for d in (w_arg, m_arg, v_arg)]
    relaid = [d[n] for d in (small_w, small_m, small_v) for n in SMALL_FLAT_SSM]

    def fetch_in(part, after):
        handle, peers, tag = ((in_near, (0,) + NEAR_PEERS, "near"), (in_far, FAR_PEERS, "far"))[part]
        w_in_state[:] = _exchange_wait(handle, after=after + (relaid if part == 0 else []), srcs=w_in_state[0],
                                       lands=w_in_state[1], name="gather_wait_in_" + tag)
        return w_in_state[1][0], jnp.stack([_peer_index(kk) for kk in peers]).astype(jnp.int32)

    def fetch(i, after):
        _, lands = _exchange_wait(gathers[i], after=after, name=f"gather_wait_{i}")
        full = dict(zip(GATHER_GROUPS[i], lands))
        return {n: t if n in BIG_COL_SHARDED else t.reshape(-1, t.shape[-1]) for n, t in full.items()}

    scatters = {}

    def send(i, gw):
        slots = [gw[n] if n in BIG_COL_SHARDED else gw[n].reshape(N_DEV, -1, gw[n].shape[-1]) for n in SCATTER_GROUPS[i]]
        handle, sent = _exchange_start(slots, scatter=True, name=f"scatter_start_{i}")
        scatters[i] = (handle, slots)
        return sent

    sm = {}
    for n in SMALL:
        t = w_arg[n]
        if n.startswith("ssm_") and n not in ("ssm_d", "ssm_log_dt"):
            sm[n] = t[0]
        else:
            sm[n] = t.reshape(1, -1)

    smalls = []

    def send_small(gs, names):
        handle, sent = _exchange_start([_pack_small(gs, names, small_views)], scatter=False,
                                       name=f"small_start_{len(smalls)}")
        smalls.append(handle)
        return sent

    loss_row, grad_x, gs = _local_grads(x[0], mem[0], positions.reshape(-1, 1), loss_target[0], sm, fetch_in, fetch,
                                        send, send_small, token)
    loss = lax.psum(loss_row[0, 0], ("x", "y", "c"))
    send_small(gs, SMALL_LATE)

    results = [{}, {}, {}, {}]
    done = grad_x
    for i, names in enumerate(SCATTER_GROUPS):
        handle, slots = scatters[i]
        _, lands = _exchange_wait(handle, after=[done], name=f"scatter_wait_{i}")
        for n, land, slot in zip(names, lands, slots):
            res = _reduce_adamw(land, w_arg[n][0], m_arg[n][0], v_arg[n][0], name="adamw_" + n)
            done = res[0]
            for d, r in zip(results, res):
                d[n] = r[None]
    stacks = [_exchange_wait(handle, after=[done], name=f"small_wait_{i}")[1][0] for i, handle in enumerate(smalls)]
    res = _adamw_small(stacks, small_views, small_w, small_m, small_v, name="adamw_small")
    for d, r in zip(results, res):
        d.update({n: r[n].reshape(w_arg[n].shape) for n in SMALL})
    out = [loss, grad_x[None]]
    for d in results:
        out += [d[n] for n in WEIGHTS]
    return tuple(out)
```

```python
import functools

import numpy as np
import jax
import jax.numpy as jnp
from jax import lax
from jax.experimental import pallas as pl
from jax.experimental.pallas import tpu as pltpu

F32 = jnp.float32
MXU_DTYPE = jnp.bfloat16
WIRE_DTYPE = jnp.bfloat16
VMEM_LIMIT_BYTES = 48 * 1024 * 1024
LANES = 128

N_DEV = 8
D_MODEL = 1024
SSM_GROUP = 16
SSM_WIDTH = 768
SSM_GROUPS = SSM_WIDTH // SSM_GROUP
SSM_STATE = 64
SSM_CH = SSM_GROUPS * SSM_STATE
SSM_TILES = SSM_WIDTH // LANES
GROUPS_PER_TILE = LANES // SSM_GROUP
STATE_VREG_ROWS = SSM_CH // LANES
ATT_HEAD_DIM = 64
ATT_HEADS_PER_GROUP = 4
ATT_MERGED = ATT_HEADS_PER_GROUP * ATT_HEAD_DIM
LANE_HALVES = ATT_MERGED // LANES
DILATIONS = (1, 4, 16)
ATT_BLK = 128
ATT_SCALE = ATT_HEAD_DIM ** -0.5
ROT_DIM = ATT_HEAD_DIM // 4
ROPE_THETA = 500000.0
XATT_HEADS = 4
XATT_HEAD_DIM = D_MODEL // XATT_HEADS
XATT_SCALE = XATT_HEAD_DIM ** -0.5
DEEPNORM_ALPHA = 2.0 ** 0.25
LN_EPS = 1e-5
NEG_INF = -1e30
OFF_Q_BLK, OFF_K_BLK, OFF_V_BLK = 3, 6, 9
OFF_GS_BLK, OFF_GA_BLK = 3, 4

ADAM_LR = 0.001
ADAM_B1 = 0.9
ADAM_B2 = 0.999
ADAM_EPS = 1e-08
ADAM_WD = 0.01
ADAM_STEP = 10

BIG = ("w_in", "w_glu", "w_att_up", "w_mix_out", "w_xq", "w_xkv", "w_xo", "w_ff1", "w_ff2")
BIG_COL_SHARDED = ("w_in", "w_glu", "w_att_up", "w_xkv", "w_ff1")
WEIGHTS = ("ln_in_g", "ln_in_b", "w_in", "b_in", "ssm_log_dt", "ssm_a_re", "ssm_a_im", "ssm_b_re", "ssm_b_im",
           "ssm_c_re", "ssm_c_im", "ssm_d", "w_glu", "b_glu", "w_att_up", "w_mix_out", "b_mix_out", "ln1_g", "ln1_b",
           "w_xq", "w_xkv", "w_xo", "ln2_g", "ln2_b", "w_ff1", "b_ff1", "w_ff2", "b_ff2", "ln3_g", "ln3_b")
SMALL = tuple(n for n in WEIGHTS if n not in BIG)
PACK_COLS = 1024
PACK_ROW_ALIGN = 16


def _params(*sem):
    return pltpu.CompilerParams(dimension_semantics=sem, vmem_limit_bytes=VMEM_LIMIT_BYTES)


def _dot(a, b, ca, cb):
    return lax.dot_general(a.astype(MXU_DTYPE), b.astype(MXU_DTYPE), (((ca,), (cb,)), ((), ())),
                           preferred_element_type=F32)


def _fit(dim, pref):
    if dim <= pref:
        return dim
    best = max(t for t in range(LANES, pref + 1, LANES) if dim % t == 0)
    return best


def _mm(a, b, *, name, ta=False, tb=False, bias=None, out_dtype=F32, b_shards=False, out_shards=False, after=None,
        also=None, gate=None, colsum=False, tm=2048, tn=1024, tk=1024):
    m, k = (a.shape[1], a.shape[0]) if ta else a.shape
    order = (lambda f: (lambda j, i, kk: f(i, j, kk))) if colsum else (lambda f: f)
    spec = lambda shape, f: pl.BlockSpec(shape, order(f))
    if b_shards:
        n_sh, rows, n_loc = b.shape
        if tb:
            n, tn, tk = rows, _fit(rows, tn), n_loc
            assert k == n_sh * n_loc, (name, k, b.shape)
            b_spec = spec((1, tn, tk), lambda i, j, kk: (kk, j, 0))
        else:
            n, tn, tk = n_sh * n_loc, n_loc, _fit(k, tk)
            b_spec = spec((1, tk, tn), lambda i, j, kk: (j, kk, 0))
    else:
        n = b.shape[0] if tb else b.shape[1]
        tn = n // N_DEV if out_shards else _fit(n, tn)
        tk = _fit(k, tk)
        b_spec = spec((tn, tk), lambda i, j, kk: (j, kk)) if tb else spec((tk, tn), lambda i, j, kk: (kk, j))
    tm = _fit(m, tm)
    nk = k // tk
    a_spec = spec((tk, tm), lambda i, j, kk: (kk, i)) if ta else spec((tm, tk), lambda i, j, kk: (i, kk))
    tile = spec((tm, tn), lambda i, j, kk: (i, j))
    in_specs, args = [a_spec, b_spec], [a, b]
    if bias is not None:
        in_specs.append(spec((1, tn), lambda i, j, kk: (0, j)))
        args.append(bias)
    if gate is not None:
        in_specs.append(tile)
        args.append(gate[0])
    if after is not None:
        in_specs.append(pl.BlockSpec(memory_space=pl.ANY))
        args.append(after)
    n_in = len(args)
    if out_shards:
        assert n == N_DEV * tn, (name, n, tn)
        out_specs = [spec((1, tm, tn), lambda i, j, kk: (j, i, 0))]
        out_shape = [jax.ShapeDtypeStruct((N_DEV, m, tn), out_dtype)]
    else:
        out_specs = [tile]
        out_shape = [jax.ShapeDtypeStruct((m, n), out_dtype)]
    if also is not None:
        out_specs.append(tile)
        out_shape.append(jax.ShapeDtypeStruct((m, n), also[1]))
    if colsum:
        out_specs.append(spec((1, tn), lambda i, j, kk: (0, j)))
        out_shape.append(jax.ShapeDtypeStruct((1, n), F32))

    def body(*refs):
        a_ref, b_ref = refs[0], refs[1]
        o_ref = refs[n_in]

        def product():
            return _dot(a_ref[...], b_ref[0] if b_shards else b_ref[...], 0 if ta else 1, 1 if tb else 0)

        def finish(r):
            if bias is not None:
                r = r + refs[2][...]
            if gate is not None:
                r = r * gate[1](refs[2 + (bias is not None)][...])
            if out_shards:
                o_ref[0] = r.astype(o_ref.dtype)
            else:
                o_ref[...] = r.astype(o_ref.dtype)
            if also is not None:
                refs[n_in + 1][...] = also[0](r).astype(also[1])
            if colsum:
                s_ref = refs[n_in + 1 + (also is not None)]

                @pl.when(pl.program_id(1) == 0)
                def _():
                    s_ref[...] = jnp.zeros_like(s_ref)

                s_ref[...] += _colsum(r)

        if nk == 1:
            finish(product())
            return
        acc_ref = refs[-1]
        kk = pl.program_id(2)

        @pl.when(kk == 0)
        def _():
            acc_ref[...] = jnp.zeros_like(acc_ref)

        acc_ref[...] += product()

        @pl.when(kk == nk - 1)
        def _():
            finish(acc_ref[...])

    grid = (n // tn, m // tm, nk) if colsum else (m // tm, n // tn, nk)
    res = pl.pallas_call(
        body, name=name, grid=grid, in_specs=in_specs, out_specs=out_specs, out_shape=out_shape,
        scratch_shapes=[pltpu.VMEM((tm, tn), F32)] if nk > 1 else [],
        compiler_params=_params("parallel", "arbitrary" if colsum else "parallel", "arbitrary"),
    )(*args)
    return res[0] if len(res) == 1 else res


def _mm_shards(a, w, bias, shard_ids, *, name, prev=None, tm=2048):
    m, k = a.shape
    n_sh, _, n_loc = w.shape
    tm = _fit(m, tm)

    def body(ids_ref, a_ref, w_ref, b_ref, *rest):
        rest[-1][...] = _dot(a_ref[...], w_ref[0], 1, 0) + b_ref[...]

    grid_spec = pltpu.PrefetchScalarGridSpec(
        num_scalar_prefetch=1, grid=(m // tm, shard_ids.shape[0]),
        in_specs=[pl.BlockSpec((tm, k), lambda i, j, ids: (i, 0)),
                  pl.BlockSpec((1, k, n_loc), lambda i, j, ids: (ids[j], 0, 0)),
                  pl.BlockSpec((1, n_loc), lambda i, j, ids: (0, ids[j]))]
        + [pl.BlockSpec(memory_space=pl.ANY)] * (prev is not None),
        out_specs=pl.BlockSpec((tm, n_loc), lambda i, j, ids: (i, ids[j])))
    return pl.pallas_call(
        body, name=name, grid_spec=grid_spec, out_shape=jax.ShapeDtypeStruct((m, n_sh * n_loc), F32),
        input_output_aliases={4: 0} if prev is not None else {}, compiler_params=_params("parallel", "arbitrary"),
    )(shard_ids, a, w, bias, *([prev] if prev is not None else []))


def _rowcall(fn, rows, fulls, row_outs, acc_outs=(), *, n_rows, tm, name, after=None):
    n_r, n_f, n_o, n_a = len(rows), len(fulls), len(row_outs), len(acc_outs)
    n_in = n_r + n_f + (after is not None)
    assert n_rows % tm == 0, (name, n_rows, tm)

    def body(*refs):
        res = fn(*[r[...] for r in refs[:n_r + n_f]])
        res = tuple(res) if isinstance(res, (tuple, list)) else (res,)
        o_refs = refs[n_in:n_in + n_o]
        a_refs = refs[n_in + n_o:]
        for o_ref, val in zip(o_refs, res[:n_o]):
            o_ref[...] = val.astype(o_ref.dtype)
        if n_a:
            @pl.when(pl.program_id(0) == 0)
            def _():
                for a_ref in a_refs:
                    a_ref[...] = jnp.zeros_like(a_ref)

            for a_ref, val in zip(a_refs, res[n_o:]):
                a_ref[...] += val

    in_specs = [pl.BlockSpec((tm, w), functools.partial(lambda i, cb: (i, cb), cb=cb)) for _, w, cb in rows]
    in_specs += [pl.BlockSpec(f.shape, functools.partial(lambda i, nd: (0,) * nd, nd=f.ndim)) for f in fulls]
    in_specs += [pl.BlockSpec(memory_space=pl.ANY)] * (after is not None)
    out_specs = [pl.BlockSpec((tm, w), lambda i: (i, 0)) for w, _ in row_outs]
    out_specs += [pl.BlockSpec((1, w), lambda i: (0, 0)) for w in acc_outs]
    out_shape = [jax.ShapeDtypeStruct((n_rows, w), dt) for w, dt in row_outs]
    out_shape += [jax.ShapeDtypeStruct((1, w), F32) for w in acc_outs]
    return pl.pallas_call(
        body, name=name, grid=(n_rows // tm,), in_specs=in_specs, out_specs=out_specs, out_shape=out_shape,
        compiler_params=_params("arbitrary" if n_a else "parallel"),
    )(*[r[0] for r in rows], *fulls, *([after] if after is not None else []))


def _colsum(v):
    return jnp.sum(v, axis=0, keepdims=True)


def _ln_fwd(a, r, g, b, *, alpha, name):
    n_rows, d = a.shape

    def fn(*t):
        xin = t[0] if alpha == 1.0 else alpha * t[0]
        if r is not None:
            xin = xin + t[1]
        gv, bv = t[-2], t[-1]
        mu = jnp.mean(xin, axis=-1, keepdims=True)
        xc = xin - mu
        var = jnp.mean(xc * xc, axis=-1, keepdims=True)
        rstd = lax.rsqrt(var + LN_EPS)
        xh = xc * rstd
        y = xh * gv + bv
        return y, xh, rstd, y

    rows = [(a, d, 0)] + ([(r, d, 0)] if r is not None else [])
    return _rowcall(fn, rows, [g, b], [(d, F32), (d, F32), (1, F32), (d, MXU_DTYPE)], n_rows=n_rows, tm=256, name=name)


def _ln_bwd(dya, dyb, xh, rstd, g, *, alpha, name, operand=True):
    n_rows, d = xh.shape

    def fn(da, db, xhv, rs, gv):
        dy = alpha * da + db
        dyg = dy * gv
        m1 = jnp.mean(dyg, axis=-1, keepdims=True)
        m2 = jnp.mean(dyg * xhv, axis=-1, keepdims=True)
        dx = rs * (dyg - m1 - xhv * m2)
        return (dx,) + ((dx,) if operand else ()) + (_colsum(dy * xhv), _colsum(dy), _colsum(dx))

    rows = [(dya, d, 0), (dyb, d, 0), (xh, d, 0), (rstd, 1, 0)]
    return _rowcall(fn, rows, [g], [(d, F32)] + [(d, MXU_DTYPE)] * operand, [d, d, d], n_rows=n_rows, tm=256, name=name)


def _ln_loss_bwd(a, r, target, g, b, *, alpha, name):
    n_rows, d = a.shape

    def fn(av, rv, tv, gv, bv):
        xin = alpha * av + rv
        mu = jnp.mean(xin, axis=-1, keepdims=True)
        xc = xin - mu
        var = jnp.mean(xc * xc, axis=-1, keepdims=True)
        rs = lax.rsqrt(var + LN_EPS)
        xh = xc * rs
        diff = xh * gv + bv - tv
        part = jnp.sum(jnp.sum(diff * diff, axis=1, keepdims=True), axis=0, keepdims=True) * (0.5 / d)
        dy = diff * (1.0 / d)
        dyg = dy * gv
        m1 = jnp.mean(dyg, axis=-1, keepdims=True)
        m2 = jnp.mean(dyg * xh, axis=-1, keepdims=True)
        dx = rs * (dyg - m1 - xh * m2)
        return dx, dx, _colsum(dy * xh), _colsum(dy), _colsum(dx), jnp.broadcast_to(part, (1, LANES))

    return _rowcall(fn, [(a, d, 0), (r, d, 0), (target, d, 0)], [g, b], [(d, F32), (d, MXU_DTYPE)], [d, d, d, LANES],
                    n_rows=n_rows, tm=256, name=name)


def _rope_lane_constants():
    lane = np.arange(ATT_MERGED)
    in_head = lane % ATT_HEAD_DIM
    sign = np.where(in_head < ROT_DIM // 2, -1.0, np.where(in_head < ROT_DIM, 1.0, 0.0)).astype(np.float32)
    inv_freq = ROPE_THETA ** (-jnp.arange(0, ROT_DIM, 2, dtype=F32) / ROT_DIM)
    return inv_freq[lane % (ROT_DIM // 2)].reshape(1, ATT_MERGED), jnp.asarray(sign).reshape(1, ATT_MERGED)


def _rope_tables(pos_col, *, name, after=None):
    inv_lane, sign = _rope_lane_constants()

    def fn(pos, inv, sg):
        ang = pos.astype(F32) * inv
        return jnp.where(sg != 0.0, jnp.cos(ang), 1.0), sg * jnp.sin(ang)

    return _rowcall(fn, [(pos_col, 1, 0)], [inv_lane, sign], [(ATT_MERGED, F32), (ATT_MERGED, F32)],
                    n_rows=pos_col.shape[0], tm=512, name=name, after=after)


def _rot_partner(t):
    lane = lax.broadcasted_iota(jnp.int32, t.shape, 1)
    width = t.shape[1]
    return jnp.where((lane & (ROT_DIM // 2)) == 0, pltpu.roll(t, width - ROT_DIM // 2, 1), pltpu.roll(t, ROT_DIM // 2, 1))


def _rope(t, cos_t, sin_t):
    return t * cos_t + _rot_partner(t) * sin_t


def _rope_transpose(dt, cos_t, sin_t):
    return dt * cos_t + _rot_partner(dt * sin_t)


def _strided_rows(r, count, stride):
    return pl.ds(r, count) if stride == 1 else pl.ds(r, count, stride=stride)


def _qkv_split(proj, cos_t, sin_t, *, name, tm=512):
    n_rows = proj.shape[0]
    n_g = len(DILATIONS)

    def body(*refs):
        n_src = LANE_HALVES * 3 * n_g
        src, tables, dst = refs[:n_src], refs[n_src:n_src + 2 * LANE_HALVES], refs[n_src + 2 * LANE_HALVES:]
        for kind in range(3):
            for g, dil in enumerate(DILATIONS):
                for half in range(LANE_HALVES):
                    x_ref, o_ref = src[(kind * n_g + g) * LANE_HALVES + half], dst[kind * n_g + g]
                    cos_ref, sin_ref = tables[half], tables[LANE_HALVES + half]
                    for r in range(dil):
                        rows = _strided_rows(r, tm // dil, dil)
                        t = x_ref[rows, :]
                        if kind < 2:
                            t = _rope(t, cos_ref[rows, :], sin_ref[rows, :])
                        lo = r * ATT_MERGED + half * LANES
                        o_ref[:, lo:lo + LANES] = t.astype(o_ref.dtype)

    half_spec = lambda cb: pl.BlockSpec((tm, LANES), functools.partial(lambda i, cb: (i, cb), cb=cb))
    in_specs = [half_spec((off + g) * LANE_HALVES + half)
                for off in (OFF_Q_BLK, OFF_K_BLK, OFF_V_BLK) for g in range(n_g) for half in range(LANE_HALVES)]
    in_specs += [half_spec(half) for _ in range(2) for half in range(LANE_HALVES)]
    out_specs = [pl.BlockSpec((tm // dil, dil * ATT_MERGED), lambda i: (i, 0)) for _ in range(3) for dil in DILATIONS]
    out_shape = [jax.ShapeDtypeStruct((n_rows // dil, dil * ATT_MERGED), MXU_DTYPE) for _ in range(3) for dil in DILATIONS]
    outs = pl.pallas_call(
        body, name=name, grid=(n_rows // tm,), in_specs=in_specs, out_specs=out_specs, out_shape=out_shape,
        compiler_params=_params("parallel"),
    )(*[proj] * (LANE_HALVES * 3 * n_g), *[cos_t] * LANE_HALVES, *[sin_t] * LANE_HALVES)
    return outs[:n_g], outs[n_g:2 * n_g], outs[2 * n_g:]


def _mix(gs, ga, z1, z2, b_att):
    return jax.nn.sigmoid(gs) * (z1 * jax.nn.sigmoid(z2)) + jax.nn.sigmoid(ga) * b_att


def _mix_rows(proj, z, b_att):
    return [(proj, D_MODEL, OFF_GS_BLK), (proj, D_MODEL, OFF_GA_BLK), (z, D_MODEL, 0), (z, D_MODEL, 1), (b_att, D_MODEL, 0)]


def _mix_fwd(proj, z, b_att, *, name):
    return _rowcall(_mix, _mix_rows(proj, z, b_att), [], [(D_MODEL, MXU_DTYPE)],
                    n_rows=proj.shape[0], tm=256, name=name)[0]


def _mix_bwd(dmixed, proj, z, b_att, *, name):
    def fn(dm, gs, ga, z1, z2, ba):
        _, vjp = jax.vjp(_mix, gs, ga, z1, z2, ba)
        dgs, dga, dz1, dz2, dba = vjp(dm)
        dz = jnp.concatenate([dz1, dz2], axis=1)
        return dgs, dga, dz, dba, _colsum(dgs), _colsum(dga), _colsum(dz)

    rows = [(dmixed, D_MODEL, 0)] + _mix_rows(proj, z, b_att)
    widths = [D_MODEL, D_MODEL, 2 * D_MODEL, D_MODEL]
    return _rowcall(fn, rows, [], [(w, MXU_DTYPE) for w in widths], widths[:3], n_rows=proj.shape[0], tm=256, name=name)


def _gelu_bwd(dgy, y, proj, *, name):
    def fn(dg, yv, u):
        _, vjp = jax.vjp(jax.nn.gelu, yv)
        dy = vjp(dg)[0]
        return dy, _colsum(dy * u)

    return _rowcall(fn, [(dgy, SSM_WIDTH, 0), (y, SSM_WIDTH, 0), (proj, SSM_WIDTH, 0)], [], [(SSM_WIDTH, F32)],
                    [SSM_WIDTH], n_rows=y.shape[0], tm=512, name=name)


HEAD_ROWS = ATT_HEADS_PER_GROUP * ATT_BLK


def _head_masks(rows):
    head = lax.broadcasted_iota(jnp.int32, (rows, ATT_MERGED), 1) >> (ATT_HEAD_DIM.bit_length() - 1)
    return [head == h for h in range(ATT_HEADS_PER_GROUP)]


def _stack_heads(t, masks):
    return jnp.concatenate([jnp.where(m, t, jnp.zeros_like(t)) for m in masks], axis=0)


def _unstack_heads(t4, masks):
    blocks = [t4[h * ATT_BLK:(h + 1) * ATT_BLK] for h in range(ATT_HEADS_PER_GROUP)]
    return jnp.where(masks[0], blocks[0], jnp.where(masks[1], blocks[1], jnp.where(masks[2], blocks[2], blocks[3])))


def _head_column(stats, first):
    return jnp.concatenate([stats[:, first + h:first + h + 1] for h in range(ATT_HEADS_PER_GROUP)], axis=0)


def _band_mask(first_key):
    qi = lax.broadcasted_iota(jnp.int32, (HEAD_ROWS, 2 * ATT_BLK), 0) & (ATT_BLK - 1)
    ki = lax.broadcasted_iota(jnp.int32, (HEAD_ROWS, 2 * ATT_BLK), 1)
    steps = qi + ATT_BLK - ki
    return (steps >= 0) & (steps <= ATT_BLK) & (ki >= first_key)


def _dil_fwd(q, k, v, dil, *, name):
    n_blk = q.shape[0] // ATT_BLK
    cur = pl.BlockSpec((ATT_BLK, ATT_MERGED), lambda r, n: (n, r))
    prev = pl.BlockSpec((ATT_BLK, ATT_MERGED), lambda r, n: (jnp.maximum(n - 1, 0), r))

    def body(q_ref, kp_ref, kc_ref, vp_ref, vc_ref, o_ref, l_ref):
        masks = _head_masks(ATT_BLK)
        valid = _band_mask(jnp.where(pl.program_id(1) > 0, 0, ATT_BLK))
        keys = jnp.concatenate([kp_ref[...], kc_ref[...]], axis=0)
        vals = jnp.concatenate([vp_ref[...], vc_ref[...]], axis=0)
        s = jnp.where(valid, _dot(_stack_heads(q_ref[...], masks), keys, 1, 1) * ATT_SCALE, NEG_INF)
        m = jnp.max(s, axis=-1, keepdims=True)
        p = jnp.exp(s - m)
        den = jnp.sum(p, axis=-1, keepdims=True)
        o_ref[...] = _unstack_heads(_dot(p, vals, 1, 0) / den, masks)
        l_ref[...] = _unstack_heads(jnp.broadcast_to(m + jnp.log(den), (HEAD_ROWS, ATT_MERGED)), masks)

    shape = jax.ShapeDtypeStruct(q.shape, F32)
    return pl.pallas_call(
        body, name=name, grid=(dil, n_blk), in_specs=[cur, prev, cur, prev, cur], out_specs=[cur, cur],
        out_shape=[shape, shape], compiler_params=_params("parallel", "parallel"),
    )(q, k, k, v, v)


def _att_merge(outs, lses, *, name, tm=512):
    n_g = len(outs)
    n_rows = outs[0].shape[0] * DILATIONS[0]

    def body(*refs):
        src, (att_ref, lse_ref), tmp = refs[:2 * n_g], refs[2 * n_g:2 * n_g + 2], refs[2 * n_g + 2:]
        vals = []
        for idx, src_ref in enumerate(src):
            dil = DILATIONS[idx % n_g]
            if dil == 1:
                vals.append(src_ref[...])
                continue
            for r in range(dil):
                for half in range(LANE_HALVES):
                    lo = r * ATT_MERGED + half * LANES
                    tmp[LANE_HALVES * idx + half][_strided_rows(r, tm // dil, dil), :] = src_ref[:, lo:lo + LANES]
            vals.append(jnp.concatenate([tmp[LANE_HALVES * idx + half][...] for half in range(LANE_HALVES)], axis=1))
        o, l = vals[:n_g], vals[n_g:]
        m = functools.reduce(jnp.maximum, l)
        e = [jnp.exp(li - m) for li in l]
        z = functools.reduce(jnp.add, e)
        att_ref[...] = functools.reduce(jnp.add, [(ei / z) * oi for ei, oi in zip(e, o)])
        lse_ref[...] = m + jnp.log(z)

    in_specs = [pl.BlockSpec((tm // dil, dil * ATT_MERGED), lambda i: (i, 0)) for _ in range(2) for dil in DILATIONS]
    row = pl.BlockSpec((tm, ATT_MERGED), lambda i: (i, 0))
    shape = jax.ShapeDtypeStruct((n_rows, ATT_MERGED), F32)
    return pl.pallas_call(
        body, name=name, grid=(n_rows // tm,), in_specs=in_specs, out_specs=[row, row], out_shape=[shape, shape],
        scratch_shapes=[pltpu.VMEM((tm, LANES), F32)] * (LANE_HALVES * 2 * n_g), compiler_params=_params("parallel"),
    )(*outs, *lses)


def _att_stats(datt, att, lse, *, name):
    n_rows = datt.shape[0]

    def fn(d, a, l):
        prod = d * a
        lane = lax.broadcasted_iota(jnp.int32, (d.shape[0], LANES), 1)
        out = jnp.zeros((d.shape[0], LANES), F32)
        for h in range(ATT_HEADS_PER_GROUP):
            lo = h * ATT_HEAD_DIM
            out = jnp.where(lane == h, l[:, lo:lo + 1], out)
            delta = jnp.sum(prod[:, lo:lo + ATT_HEAD_DIM], axis=-1, keepdims=True)
            out = jnp.where(lane == ATT_HEADS_PER_GROUP + h, delta, out)
        return out

    rows = [(t, ATT_MERGED, 0) for t in (datt, att, lse)]
    return _rowcall(fn, rows, [], [(LANES, F32)], n_rows=n_rows, tm=512, name=name)[0]


def _dil_bwd(q, k, v, datt, stats, dil, *, name):
    n_rows = datt.shape[0]
    n_blk = n_rows // dil // ATT_BLK
    span = ATT_BLK * dil
    cur = pl.BlockSpec((ATT_BLK, ATT_MERGED), lambda n, r: (n, r))
    prev = pl.BlockSpec((ATT_BLK, ATT_MERGED), lambda n, r: (jnp.maximum(n - 1, 0), r))
    nxt = pl.BlockSpec((ATT_BLK, ATT_MERGED), lambda n, r: (jnp.minimum(n + 1, n_blk - 1), r))
    seq = lambda half, ahead: pl.BlockSpec((span, LANES), lambda n, r: (jnp.minimum(n + ahead, n_blk - 1), half))

    def body(qc_ref, qn_ref, kp_ref, kc_ref, vp_ref, vc_ref, dc0_ref, dc1_ref, dn0_ref, dn1_ref, sc_ref, sn_ref,
             dq0_ref, dq1_ref, dk0_ref, dk1_ref, dv0_ref, dv1_ref):
        n = pl.program_id(0)
        rows = slice(None) if dil == 1 else _strided_rows(pl.program_id(1), ATT_BLK, dil)

        def read(ref0, ref1):
            return jnp.concatenate([ref0[rows, :], ref1[rows, :]], axis=1)

        def write(ref0, ref1, val):
            ref0[rows, :] = val[:, :LANES]
            ref1[rows, :] = val[:, LANES:]

        masks = _head_masks(ATT_BLK)
        valid = _band_mask(jnp.where(n > 0, 0, ATT_BLK))
        qi = lax.broadcasted_iota(jnp.int32, (HEAD_ROWS, ATT_BLK), 0) & (ATT_BLK - 1)
        ki = lax.broadcasted_iota(jnp.int32, (HEAD_ROWS, ATT_BLK), 1)
        valid_next = (ki - qi) >= jnp.where(n < n_blk - 1, 0, ATT_BLK)

        kc, vc = kc_ref[...], vc_ref[...]
        keys = jnp.concatenate([kp_ref[...], kc], axis=0)
        vals = jnp.concatenate([vp_ref[...], vc], axis=0)
        q4 = _stack_heads(qc_ref[...], masks)
        d4 = _stack_heads(read(dc0_ref, dc1_ref).astype(MXU_DTYPE), masks)
        st = sc_ref[rows, :]
        p = jnp.where(valid, jnp.exp(_dot(q4, keys, 1, 1) * ATT_SCALE - _head_column(st, 0)), 0.0)
        ds = p * (_dot(d4, vals, 1, 1) - _head_column(st, ATT_HEADS_PER_GROUP)) * ATT_SCALE
        write(dq0_ref, dq1_ref, _unstack_heads(_dot(ds, keys, 1, 0), masks))

        q4n = _stack_heads(qn_ref[...], masks)
        d4n = _stack_heads(read(dn0_ref, dn1_ref).astype(MXU_DTYPE), masks)
        stn = sn_ref[rows, :]
        p_n = jnp.where(valid_next, jnp.exp(_dot(q4n, kc, 1, 1) * ATT_SCALE - _head_column(stn, 0)), 0.0)
        ds_n = p_n * (_dot(d4n, vc, 1, 1) - _head_column(stn, ATT_HEADS_PER_GROUP)) * ATT_SCALE
        write(dv0_ref, dv1_ref, _dot(p[:, ATT_BLK:], d4, 0, 0) + _dot(p_n, d4n, 0, 0))
        write(dk0_ref, dk1_ref, _dot(ds[:, ATT_BLK:], q4, 0, 0) + _dot(ds_n, q4n, 0, 0))

    shape = jax.ShapeDtypeStruct((n_rows, LANES), F32)
    out = seq(0, 0)
    res = pl.pallas_call(
        body, name=name, grid=(n_blk, dil),
        in_specs=[cur, nxt, prev, cur, prev, cur, seq(0, 0), seq(1, 0), seq(0, 1), seq(1, 1), seq(0, 0), seq(0, 1)],
        out_specs=[out] * 6, out_shape=[shape] * 6, compiler_params=_params("parallel", "arbitrary"),
    )(q, q, k, k, v, v, datt, datt, datt, datt, stats, stats)
    return [(res[2 * i], res[2 * i + 1]) for i in range(3)]


def _dproj_assemble(du, dqkv, dgs, dga, cos_t, sin_t, *, name):
    n_g = len(DILATIONS)

    def fn(*t):
        n_half = LANE_HALVES * 3 * n_g
        du_t, halves, (dgs_t, dga_t, c, s) = t[0], t[1:1 + n_half], t[1 + n_half:]
        parts = [jnp.concatenate(halves[LANE_HALVES * i:LANE_HALVES * (i + 1)], axis=1) for i in range(3 * n_g)]
        for i in range(2 * n_g):
            parts[i] = _rope_transpose(parts[i], c, s)
        cast = [p.astype(MXU_DTYPE) for p in parts]
        return [jnp.concatenate([du_t] + cast + [dgs_t, dga_t], axis=1)] + [_colsum(p) for p in parts]

    rows = [(du, SSM_WIDTH, 0)]
    rows += [(half, LANES, 0) for i in range(3) for g in range(n_g) for half in dqkv[g][i]]
    rows += [(dgs, D_MODEL, 0), (dga, D_MODEL, 0), (cos_t, ATT_MERGED, 0), (sin_t, ATT_MERGED, 0)]
    width = SSM_WIDTH + 3 * n_g * ATT_MERGED + 2 * D_MODEL
    res = _rowcall(fn, rows, [], [(width, MXU_DTYPE)], [ATT_MERGED] * (3 * n_g), n_rows=du.shape[0], tm=256, name=name)
    return res[0], res[1:]


def _xhead(h):
    return slice(h * XATT_HEAD_DIM, (h + 1) * XATT_HEAD_DIM)


def _xatt_probs(qh, kh):
    s = _dot(qh, kh, 1, 1) * XATT_SCALE
    e = jnp.exp(s - jnp.max(s, axis=-1, keepdims=True))
    return e / jnp.sum(e, axis=-1, keepdims=True)


def _xatt_fwd(q, kv, *, name, tm=512):
    n_rows = q.shape[0]
    n_mem = kv.shape[0]

    def body(q_ref, kv_ref, o_ref):
        for h in range(XATT_HEADS):
            sl = _xhead(h)
            p = _xatt_probs(q_ref[:, sl], kv_ref[:, sl])
            o_ref[:, sl] = _dot(p, kv_ref[:, D_MODEL + h * XATT_HEAD_DIM:D_MODEL + (h + 1) * XATT_HEAD_DIM], 1, 0
                                ).astype(o_ref.dtype)

    row = pl.BlockSpec((tm, D_MODEL), lambda i: (i, 0))
    return pl.pallas_call(
        body, name=name, grid=(n_rows // tm,),
        in_specs=[row, pl.BlockSpec((n_mem, 2 * D_MODEL), lambda i: (0, 0))], out_specs=row,
        out_shape=jax.ShapeDtypeStruct((n_rows, D_MODEL), MXU_DTYPE), compiler_params=_params("parallel"),
    )(q, kv)


def _xatt_bwd(q, kv, do, *, name, tm=512):
    n_rows = q.shape[0]
    n_mem = kv.shape[0]

    def body(q_ref, kv_ref, do_ref, dq_ref, dkv_ref):
        @pl.when(pl.program_id(0) == 0)
        def _():
            dkv_ref[...] = jnp.zeros_like(dkv_ref)

        for h in range(XATT_HEADS):
            sl = _xhead(h)
            vsl = slice(D_MODEL + h * XATT_HEAD_DIM, D_MODEL + (h + 1) * XATT_HEAD_DIM)
            qh, kh, doh = q_ref[:, sl], kv_ref[:, sl], do_ref[:, sl]
            p = _xatt_probs(qh, kh)
            dp = _dot(doh, kv_ref[:, vsl], 1, 1)
            ds = p * (dp - jnp.sum(dp * p, axis=-1, keepdims=True)) * XATT_SCALE
            dq_ref[:, sl] = _dot(ds, kh, 1, 0).astype(dq_ref.dtype)
            dkv_ref[:, sl] += _dot(ds, qh, 0, 0)
            dkv_ref[:, vsl] += _dot(p, doh, 0, 0)

    row = pl.BlockSpec((tm, D_MODEL), lambda i: (i, 0))
    full = pl.BlockSpec((n_mem, 2 * D_MODEL), lambda i: (0, 0))
    return pl.pallas_call(
        body, name=name, grid=(n_rows // tm,), in_specs=[row, full, row], out_specs=[row, full],
        out_shape=[jax.ShapeDtypeStruct((n_rows, D_MODEL), MXU_DTYPE), jax.ShapeDtypeStruct((n_mem, 2 * D_MODEL), F32)],
        compiler_params=_params("arbitrary"),
    )(q, kv, do)


def _disc(logdt, a_re, a_im, b_re, b_im):
    dt = jnp.exp(logdt)
    mag = jnp.exp(a_re * dt)
    ab_re = mag * jnp.cos(a_im * dt)
    ab_im = mag * jnp.sin(a_im * dt)
    den = jnp.square(a_re) + jnp.square(a_im)
    nr = ab_re - 1.0
    f_re = (nr * a_re + ab_im * a_im) / den
    f_im = (ab_im * a_re - nr * a_im) / den
    bb_re = f_re[None] * b_re - f_im[None] * b_im
    bb_im = f_re[None] * b_im + f_im[None] * b_re
    return ab_re, ab_im, bb_re, bb_im


def _disc_transpose(logdt, a_re, a_im, b_re, b_im, g_ab_re, g_ab_im, g_bb_re, g_bb_im):
    dt = jnp.exp(logdt)
    mag = jnp.exp(a_re * dt)
    th = a_im * dt
    cs, sn = jnp.cos(th), jnp.sin(th)
    ab_re, ab_im = mag * cs, mag * sn
    den = jnp.square(a_re) + jnp.square(a_im)
    nr = ab_re - 1.0
    f_re = (nr * a_re + ab_im * a_im) / den
    f_im = (ab_im * a_re - nr * a_im) / den
    d_f_re = jnp.sum(g_bb_re * b_re + g_bb_im * b_im, axis=0)
    d_f_im = jnp.sum(g_bb_im * b_re - g_bb_re * b_im, axis=0)
    d_b_re = g_bb_re * f_re[None] + g_bb_im * f_im[None]
    d_b_im = g_bb_im * f_re[None] - g_bb_re * f_im[None]
    d_n_re, d_n_im = d_f_re / den, d_f_im / den
    d_den = -(d_f_re * f_re + d_f_im * f_im) / den
    d_ab_re = g_ab_re + d_n_re * a_re - d_n_im * a_im
    d_ab_im = g_ab_im + d_n_re * a_im + d_n_im * a_re
    d_a_re = d_n_re * nr + d_n_im * ab_im + 2.0 * d_den * a_re
    d_a_im = d_n_re * ab_im - d_n_im * nr + 2.0 * d_den * a_im
    d_mag = d_ab_re * cs + d_ab_im * sn
    d_th = mag * (d_ab_im * cs - d_ab_re * sn)
    d_a_re = d_a_re + d_mag * mag * dt
    d_a_im = d_a_im + d_th * dt
    d_dt = jnp.sum(d_mag * mag * a_re + d_th * a_im, axis=-1, keepdims=True)
    return d_dt * dt, d_a_re, d_a_im, d_b_re, d_b_im


def _full_spec(shape):
    return pl.BlockSpec(tuple(shape), functools.partial(lambda i, nd: (0,) * nd, nd=len(shape)))


def _whole(fn, args, out_shapes, *, name):
    n_in = len(args)

    def body(*refs):
        res = fn(*[r[...] for r in refs[:n_in]])
        for o_ref, val in zip(refs[n_in:], res):
            o_ref[...] = val

    return pl.pallas_call(
        body, name=name, grid=(1,), in_specs=[_full_spec(t.shape) for t in args],
        out_specs=[_full_spec(s) for s in out_shapes], out_shape=[jax.ShapeDtypeStruct(s, F32) for s in out_shapes],
        compiler_params=_params("arbitrary"))(*args)


SSM_WIDE =GROUPS_PER_TILE * SSM_STATE
LANE_GROUPS_PER_TILE = SSM_WIDE // LANES


def _chan(j):
    return slice(j * LANES, (j + 1) * LANES)


def _time_major_rows(j, q, tc):
    return pl.ds(j * LANE_GROUPS_PER_TILE + q, tc, stride=STATE_VREG_ROWS)


def _to_time_major(x, t_re_ref, t_im_ref, dst_re, dst_im, tc):
    for j in range(SSM_TILES):
        xj = x[:, _chan(j)]
        for t_ref, dst in ((t_re_ref, dst_re), (t_im_ref, dst_im)):
            r = _dot(xj, t_ref[j], 1, 0)
            for q in range(LANE_GROUPS_PER_TILE):
                dst[_time_major_rows(j, q, tc), :] = r[:, q * LANES:(q + 1) * LANES]


def _from_time_major(src, j, tc):
    return jnp.concatenate([src[_time_major_rows(j, q, tc), :] for q in range(LANE_GROUPS_PER_TILE)], axis=1)


def _scan_chunk(w_re, w_im, h_re, h_im, a_re, a_im, start, tc):
    def step(t, carry):
        hr, hi = carry
        rows = _scan_rows(t)
        nr = a_re * hr - a_im * hi + w_re[rows, :]
        ni = a_re * hi + a_im * hr + w_im[rows, :]
        h_re[rows, :] = nr
        h_im[rows, :] = ni
        return nr, ni

    return lax.fori_loop(0, tc, step, start, unroll=8)


SSM_CHUNK = 256


def _tile_spec(stack, k):
    return pl.BlockSpec((pl.Squeezed(),) + tuple(stack.shape[1:]), lambda i: (k, 0, 0, 0))


def _expand_block_diagonal(src_ref, dst):
    dst[...] = jnp.zeros_like(dst)
    r, c = src_ref.shape[1:]
    for g in range(SSM_GROUPS):
        j, gl = divmod(g, GROUPS_PER_TILE)
        dst[j, gl * r:(gl + 1) * r, gl * c:(gl + 1) * c] = src_ref[g].astype(dst.dtype)


def _extract_block_diagonal(src, dst_ref):
    r, c = dst_ref.shape[1:]
    for g in range(SSM_GROUPS):
        j, gl = divmod(g, GROUPS_PER_TILE)
        dst_ref[g] = src[j, gl * r:(gl + 1) * r, gl * c:(gl + 1) * c]


def _ssm_fwd(proj, blocks_cn, blocks_nc, a_re, a_im, gain, *, name, tc=SSM_CHUNK):
    n_rows = proj.shape[0]
    n_chunk = n_rows // tc

    def body(u_ref, br_ref, bi_ref, cr_ref, ci_ref, ar_ref, ai_ref, g_ref, y_ref, gy_ref, hr, hi, wr, wi, state,
             tbr_ref, tbi_ref, tcr_ref, tci_ref):
        @pl.when(pl.program_id(0) == 0)
        def _():
            state[...] = jnp.zeros_like(state)
            for src_ref, dst in ((br_ref, tbr_ref), (bi_ref, tbi_ref), (cr_ref, tcr_ref), (ci_ref, tci_ref)):
                _expand_block_diagonal(src_ref, dst)

        u = u_ref[...]
        _to_time_major(u, tbr_ref, tbi_ref, wr, wi, tc)
        state[0], state[1] = _scan_chunk(wr, wi, hr, hi, ar_ref[...], ai_ref[...], (state[0], state[1]), tc)
        for j in range(SSM_TILES):
            yj = (_dot(_from_time_major(hr, j, tc), tcr_ref[j], 1, 0) + _dot(_from_time_major(hi, j, tc), tci_ref[j], 1, 0)
                  + g_ref[:, _chan(j)] * u[:, _chan(j)])
            y_ref[:, _chan(j)] = yj
            gy_ref[:, _chan(j)] = jax.nn.gelu(yj).astype(gy_ref.dtype)

    rows = pl.BlockSpec((tc, SSM_WIDTH), lambda i: (i, 0))
    coef = pl.BlockSpec((STATE_VREG_ROWS, LANES), lambda i: (0, 0))
    states = pl.BlockSpec((tc * STATE_VREG_ROWS, LANES), lambda i: (i, 0))
    sshape = jax.ShapeDtypeStruct((n_rows * STATE_VREG_ROWS, LANES), F32)
    return pl.pallas_call(
        body, name=name, grid=(n_chunk,),
        in_specs=[rows, _tile_spec(blocks_cn, 0), _tile_spec(blocks_cn, 1), _tile_spec(blocks_nc, 0),
                  _tile_spec(blocks_nc, 1), coef, coef, pl.BlockSpec((1, SSM_WIDTH), lambda i: (0, 0))],
        out_specs=[rows, rows, states, states],
        out_shape=[jax.ShapeDtypeStruct((n_rows, SSM_WIDTH), F32), jax.ShapeDtypeStruct((n_rows, SSM_WIDTH), MXU_DTYPE),
                   sshape, sshape],
        scratch_shapes=[pltpu.VMEM((tc * STATE_VREG_ROWS, LANES), F32)] * 2 + [pltpu.VMEM((2, STATE_VREG_ROWS, LANES), F32)]
        + [pltpu.VMEM((SSM_TILES, LANES, SSM_WIDE), MXU_DTYPE)] * 2 + [pltpu.VMEM((SSM_TILES, SSM_WIDE, LANES), MXU_DTYPE)] * 2,
        compiler_params=_params("arbitrary"),
    )(proj, blocks_cn, blocks_cn, blocks_nc, blocks_nc, a_re, a_im, gain)


def _ssm_bwd(proj, dy, h_re, h_im, blocks_cn, blocks_nc, a_re, a_im, gain, *, name, tc=SSM_CHUNK):
    n_rows = proj.shape[0]
    n_chunk = n_rows // tc

    def body(u_ref, dy_ref, hr, hi, cr_ref, ci_ref, br_ref, bi_ref, ar_ref, ai_ref, g_ref,
             du_ref, su_ref, dc_re_ref, dc_im_ref, db_re_ref, db_im_ref, dar_ref, dai_ref, wr, wi, carry,
             tdr_ref, tdi_ref, tur_ref, tui_ref, dcr_ref, dci_ref, dbr_ref, dbi_ref):
        @pl.when(pl.program_id(0) == 0)
        def _():
            carry[...] = jnp.zeros_like(carry)
            for acc_ref in (su_ref, dcr_ref, dci_ref, dbr_ref, dbi_ref):
                acc_ref[...] = jnp.zeros_like(acc_ref)
            for src_ref, dst in ((cr_ref, tdr_ref), (ci_ref, tdi_ref), (br_ref, tur_ref), (bi_ref, tui_ref)):
                _expand_block_diagonal(src_ref, dst)

        a_r, a_i = ar_ref[...], ai_ref[...]
        u, dyv = u_ref[...], dy_ref[...]
        _to_time_major(dyv, tdr_ref, tdi_ref, wr, wi, tc)

        def step(kk, c):
            lam_r, lam_i, dar, dai = c
            rows = _scan_rows(tc - 1 - kk)
            h_r, h_i = hr[rows, :], hi[rows, :]
            dar = dar + lam_r * h_r + lam_i * h_i
            dai = dai + lam_i * h_r - lam_r * h_i
            new_r = wr[rows, :] + a_r * lam_r + a_i * lam_i
            new_i = wi[rows, :] + a_r * lam_i - a_i * lam_r
            wr[rows, :] = new_r
            wi[rows, :] = new_i
            return new_r, new_i, dar, dai

        carry[0], carry[1], carry[2], carry[3] = lax.fori_loop(0, tc, step, (carry[0], carry[1], carry[2], carry[3]),
                                                              unroll=8)
        dar_ref[...] = carry[2]
        dai_ref[...] = carry[3]
        for j in range(SSM_TILES):
            cj = _chan(j)
            lam_r, lam_i = _from_time_major(wr, j, tc), _from_time_major(wi, j, tc)
            dcr_ref[j] += _dot(dyv[:, cj], _from_time_major(hr, j, tc), 0, 0)
            dci_ref[j] += _dot(dyv[:, cj], _from_time_major(hi, j, tc), 0, 0)
            dbr_ref[j] += _dot(u[:, cj], lam_r, 0, 0)
            dbi_ref[j] += _dot(u[:, cj], lam_i, 0, 0)
            duj = _dot(lam_r, tur_ref[j], 1, 0) + _dot(lam_i, tui_ref[j], 1, 0) + g_ref[:, cj] * dyv[:, cj]
            du_ref[:, cj] = duj.astype(du_ref.dtype)
            su_ref[:, cj] += _colsum(duj)

        @pl.when(pl.program_id(0) == n_chunk - 1)
        def _():
            for src, dst_ref in ((dcr_ref, dc_re_ref), (dci_ref, dc_im_ref), (dbr_ref, db_re_ref), (dbi_ref, db_im_ref)):
                _extract_block_diagonal(src, dst_ref)

    back = lambda i: (n_chunk - 1 - i, 0)
    rows = pl.BlockSpec((tc, SSM_WIDTH), back)
    blocks = pl.BlockSpec((SSM_GROUPS, SSM_GROUP, SSM_STATE), lambda i: (0, 0, 0))
    coef = pl.BlockSpec((STATE_VREG_ROWS, LANES), lambda i: (0, 0))
    states = pl.BlockSpec((tc * STATE_VREG_ROWS, LANES), back)
    vec = pl.BlockSpec((1, SSM_WIDTH), lambda i: (0, 0))
    bshape = jax.ShapeDtypeStruct((SSM_GROUPS, SSM_GROUP, SSM_STATE), F32)
    cshape = jax.ShapeDtypeStruct((STATE_VREG_ROWS, LANES), F32)
    return pl.pallas_call(
        body, name=name, grid=(n_chunk,),
        in_specs=[rows, rows, states, states, _tile_spec(blocks_cn, 2), _tile_spec(blocks_cn, 3), _tile_spec(blocks_nc, 2),
                  _tile_spec(blocks_nc, 3), coef, coef, vec],
        out_specs=[rows, vec, blocks, blocks, blocks, blocks, coef, coef],
        out_shape=[jax.ShapeDtypeStruct((n_rows, SSM_WIDTH), MXU_DTYPE), jax.ShapeDtypeStruct((1, SSM_WIDTH), F32),
                   bshape, bshape, bshape, bshape, cshape, cshape],
        scratch_shapes=[pltpu.VMEM((tc * STATE_VREG_ROWS, LANES), F32)] * 2 + [pltpu.VMEM((4, STATE_VREG_ROWS, LANES), F32)]
        + [pltpu.VMEM((SSM_TILES, LANES, SSM_WIDE), MXU_DTYPE)] * 2 + [pltpu.VMEM((SSM_TILES, SSM_WIDE, LANES), MXU_DTYPE)] * 2
        + [pltpu.VMEM((SSM_TILES, LANES, SSM_WIDE), F32)] * 4,
        compiler_params=_params("arbitrary"),
    )(proj, dy, h_re, h_im, blocks_cn, blocks_cn, blocks_nc, blocks_nc, a_re, a_im, gain)


def _scan_rows(t):
    return pl.ds(pl.multiple_of(t * STATE_VREG_ROWS, 8), STATE_VREG_ROWS)


GATHER_GROUPS = (("w_glu", "w_att_up", "w_mix_out"), ("w_xq", "w_xkv", "w_xo", "w_ff1", "w_ff2"))
SCATTER_GROUPS = (("w_ff2", "w_ff1"), ("w_xo", "w_xq", "w_xkv", "w_mix_out"), ("w_att_up", "w_glu"), ("w_in",))


def _local_grads(x, mem, pos_col, target, sm, fetch_in, fetch, send, send_small, start_token):
    b_re_t = sm["ssm_b_re"].transpose(2, 0, 1)
    b_im_t = sm["ssm_b_im"].transpose(2, 0, 1)
    logdt = sm["ssm_log_dt"].reshape(SSM_GROUPS, 1)
    c_re, c_im = sm["ssm_c_re"], sm["ssm_c_im"]
    grp = (SSM_GROUPS, SSM_STATE)
    chn = (SSM_GROUP, SSM_GROUPS, SSM_STATE)

    wts = {}
    cos_t, sin_t = _rope_tables(pos_col, after=start_token, name="rope_tables")
    h0, xh0, rs0, h0m = _ln_fwd(x, None, sm["ln_in_g"], sm["ln_in_b"], alpha=1.0, name="ln_in_fwd")
    disc_in = (logdt, sm["ssm_a_re"], sm["ssm_a_im"], b_re_t, b_im_t)
    ab_re, ab_im, bb_re_t, bb_im_t = _whole(_disc, disc_in, [grp, grp, chn, chn], name="ssm_disc")
    a_re_rows, a_im_rows = ab_re.reshape(STATE_VREG_ROWS, LANES), ab_im.reshape(STATE_VREG_ROWS, LANES)
    tiles_cn = jnp.stack([bb_re_t.transpose(1, 0, 2), bb_im_t.transpose(1, 0, 2), c_re, -c_im])
    tiles_nc = jnp.stack([c_re.transpose(0, 2, 1), -c_im.transpose(0, 2, 1), bb_re_t.transpose(1, 2, 0),
                          bb_im_t.transpose(1, 2, 0)])
    w_in_near, near_ids = fetch_in(0, [h0m, tiles_cn, tiles_nc])
    proj = _mm_shards(h0m, w_in_near, sm["b_in"], near_ids, name="in_proj_near")
    wts["w_in"], far_ids = fetch_in(1, [proj])
    proj = _mm_shards(h0m, wts["w_in"], sm["b_in"], far_ids, prev=proj, name="in_proj_far")

    y, gy, h_re, h_im = _ssm_fwd(proj, tiles_cn, tiles_nc, a_re_rows, a_im_rows, sm["ssm_d"], name="ssm_fwd")

    q, k, v = _qkv_split(proj, cos_t, sin_t, name="qkv_split")
    outs, lses = [], []
    for g, dil in enumerate(DILATIONS):
        o_g, l_g = _dil_fwd(q[g], k[g], v[g], dil, name=f"dil_att_fwd_{dil}")
        outs.append(o_g)
        lses.append(l_g)
    att, lse = _att_merge(outs, lses, name="att_merge")
    wts.update(fetch(0, [att]))
    z = _mm(gy, wts["w_glu"], bias=sm["b_glu"], b_shards=True, name="glu_proj")
    b_att = _mm(att, wts["w_att_up"], b_shards=True, name="att_up")

    mixed = _mix_fwd(proj, z, b_att, name="gate_mix")
    mix_out = _mm(mixed, wts["w_mix_out"], bias=sm["b_mix_out"], name="mix_out")
    h1, xh1, rs1, h1m = _ln_fwd(h0, mix_out, sm["ln1_g"], sm["ln1_b"], alpha=DEEPNORM_ALPHA, name="ln1_fwd")

    wts.update(fetch(1, [h1m]))
    xq = _mm(h1m, wts["w_xq"], out_dtype=MXU_DTYPE, name="xatt_q")
    kv = _mm(mem, wts["w_xkv"], out_dtype=MXU_DTYPE, b_shards=True, name="xatt_kv")
    xo_in = _xatt_fwd(xq, kv, name="xatt_fwd")
    xo = _mm(xo_in, wts["w_xo"], name="xatt_o")
    h2, xh2, rs2, h2m = _ln_fwd(h1, xo, sm["ln2_g"], sm["ln2_b"], alpha=DEEPNORM_ALPHA, name="ln2_fwd")

    pre, act = _mm(h2m, wts["w_ff1"], bias=sm["b_ff1"], b_shards=True, name="ff1",
                   also=(lambda r: jnp.square(jnp.maximum(r, 0.0)), MXU_DTYPE))
    ff = _mm(act, wts["w_ff2"], bias=sm["b_ff2"], name="ff2")

    gw, gs = {}, {}
    dr3, dr3m, gs["ln3_g"], gs["ln3_b"], gs["b_ff2"], loss_row = _ln_loss_bwd(
        h2, ff, target, sm["ln3_g"], sm["ln3_b"], alpha=DEEPNORM_ALPHA, name="ln3_loss")
    wgrad = functools.partial(_mm, ta=True, out_dtype=WIRE_DTYPE, tk=2048)
    gw["w_ff2"] = wgrad(act, dr3m, tk=1024, name="ff2_dw")
    dpre, gs["b_ff1"] = _mm(dr3m, wts["w_ff2"], tb=True, out_dtype=MXU_DTYPE, colsum=True, name="ff2_dx",
                            gate=(pre, lambda p: 2.0 * jnp.maximum(p, 0.0)))
    gw["w_ff1"] = wgrad(h2m, dpre, out_shards=True, name="ff1_dw")
    sent = send(0, gw)
    dh2 = _mm(dpre, wts["w_ff1"], tb=True, b_shards=True, after=sent, name="ff1_dx")

    dr2, dr2m, gs["ln2_g"], gs["ln2_b"], _ = _ln_bwd(dr3, dh2, xh2, rs2, sm["ln2_g"], alpha=DEEPNORM_ALPHA,
                                                     name="ln2_bwd")
    gw["w_xo"] = wgrad(xo_in, dr2m, name="xatt_o_dw")
    dxo_in = _mm(dr2m, wts["w_xo"], tb=True, out_dtype=MXU_DTYPE, name="xatt_o_dx")
    dxq, dkv = _xatt_bwd(xq, kv, dxo_in, name="xatt_bwd")
    gw["w_xq"] = wgrad(h1m, dxq, name="xatt_q_dw")
    gw["w_xkv"] = wgrad(mem, dkv, out_shards=True, name="xatt_kv_dw")
    dh1 = _mm(dxq, wts["w_xq"], tb=True, name="xatt_q_dx")

    dr1, dr1m, gs["ln1_g"], gs["ln1_b"], gs["b_mix_out"] = _ln_bwd(dr2, dh1, xh1, rs1, sm["ln1_g"],
                                                                   alpha=DEEPNORM_ALPHA, name="ln1_bwd")
    gw["w_mix_out"] = wgrad(mixed, dr1m, name="mix_out_dw")
    sent = send(1, gw)
    dmixed = _mm(dr1m, wts["w_mix_out"], tb=True, after=sent, name="mix_out_dx")
    dgs, dga, dz, db_att, s_gs, s_ga, gs["b_glu"] = _mix_bwd(dmixed, proj, z, b_att, name="gate_mix_bwd")

    gw["w_att_up"] = wgrad(att, db_att, out_shards=True, name="att_up_dw")
    gw["w_glu"] = wgrad(gy, dz, out_shards=True, name="glu_dw")
    sent = send(2, gw)
    datt = _mm(db_att, wts["w_att_up"], tb=True, b_shards=True, after=sent, name="att_up_dx")
    stats = _att_stats(datt, att, lse, name="att_stats")
    dqkv = [_dil_bwd(q[g], k[g], v[g], datt, stats, dil, name=f"dil_att_bwd_{dil}") for g, dil in enumerate(DILATIONS)]

    dgy = _mm(dz, wts["w_glu"], tb=True, b_shards=True, name="glu_dx")
    dy, gs["ssm_d"] = _gelu_bwd(dgy, y, proj, name="gelu_bwd")
    du, s_u, dc_re_t, dc_im_t, dbb_re_t, dbb_im_t, da_re, da_im = _ssm_bwd(
        proj, dy, h_re, h_im, tiles_cn, tiles_nc, a_re_rows, a_im_rows, sm["ssm_d"], name="ssm_bwd")
    gs["ssm_c_re"], gs["ssm_c_im"] = dc_re_t, -dc_im_t
    disc_ct = (da_re.reshape(grp), da_im.reshape(grp), dbb_re_t.transpose(1, 0, 2), dbb_im_t.transpose(1, 0, 2))
    d_logdt, gs["ssm_a_re"], gs["ssm_a_im"], d_b_re_t, d_b_im_t = _whole(
        _disc_transpose, disc_in + disc_ct, [(SSM_GROUPS, 1), grp, grp, chn, chn], name="ssm_disc_bwd")
    gs["ssm_log_dt"] = d_logdt
    gs["ssm_b_re"], gs["ssm_b_im"] = d_b_re_t.transpose(1, 2, 0), d_b_im_t.transpose(1, 2, 0)

    dproj, s_qkv = _dproj_assemble(du, dqkv, dgs, dga, cos_t, sin_t, name="dproj_assemble")
    gs["b_in"] = jnp.concatenate([s_u, *s_qkv, s_gs, s_ga], axis=1)
    sent = send_small(gs, SMALL_EARLY)
    gw["w_in"] = wgrad(h0m, dproj, out_shards=True, after=sent, name="in_proj_dw")
    sent = send(3, gw)
    dh0 = _mm(dproj, wts["w_in"], tb=True, b_shards=True, after=sent, name="in_proj_dx")
    grad_x, gs["ln_in_g"], gs["ln_in_b"], _ = _ln_bwd(dr1, dh0, xh0, rs0, sm["ln_in_g"], alpha=DEEPNORM_ALPHA,
                                                      operand=False, name="ln_in_bwd")
    return loss_row, grad_x, gs


N_PEER = N_DEV - 1
_IN_HBM = pl.BlockSpec(memory_space=pltpu.HBM)
_IN_SEMAPHORE = pl.BlockSpec(memory_space=pltpu.SEMAPHORE)


def _device_index():
    return 4 * lax.axis_index("x") + 2 * lax.axis_index("y") + lax.axis_index("c")


ALL_PEERS = tuple(range(1, N_DEV))
NEAR_PEERS = (1, 2, 3, 4, 5)
FAR_PEERS = (6, 7)


def _peer_index(kk):
    x, y, c = lax.axis_index("x"), lax.axis_index("y"), lax.axis_index("c")
    return 4 * ((x + (kk >> 2)) % 2) + 2 * ((y + ((kk >> 1) & 1)) % 2) + (c + (kk & 1)) % 2


def _exchange_copies(src_refs, land_refs, send_sems, recv_sems, scatter, peers):
    x, y, c = lax.axis_index("x"), lax.axis_index("y"), lax.axis_index("c")
    me = 4 * x + 2 * y + c
    pairs = []
    for a, (src_ref, land_ref) in enumerate(zip(src_refs, land_refs)):
        for idx, kk in enumerate(peers):
            px = (x + (kk >> 2)) % 2
            py = (y + ((kk >> 1) & 1)) % 2
            pc = (c + (kk & 1)) % 2
            peer = 4 * px + 2 * py + pc
            sem = a * len(peers) + idx
            src = src_ref.at[peer] if scatter else src_ref

            def copy(dst, src=src, sem=sem, px=px, py=py, pc=pc):
                return pltpu.make_async_remote_copy(
                    src_ref=src, dst_ref=dst, send_sem=send_sems.at[sem], recv_sem=recv_sems.at[sem],
                    device_id=(px, py, pc), device_id_type=pl.DeviceIdType.MESH)

            pairs.append((functools.partial(copy, land_ref.at[me]), functools.partial(copy, land_ref.at[peer])))
    return pairs


def _own_copies(src_refs, land_refs, own_sems, scatter):
    me = _device_index()
    return [functools.partial(pltpu.make_async_copy, src_ref.at[me] if scatter else src_ref, land_ref.at[me],
                              own_sems.at[a]) for a, (src_ref, land_ref) in enumerate(zip(src_refs, land_refs))]


def _exchange_start(srcs, *, scatter, name, after=None, peers=ALL_PEERS, lands=None):
    n_arr, n_sem = len(srcs), len(srcs) * len(peers)
    own = lands is None
    if own:
        lands = [lax.empty((N_DEV,) + tuple(s.shape[1:] if scatter else s.shape), s.dtype) for s in srcs]
    n_in = 2 * n_arr + (after is not None)

    def body(*refs):
        send_sems, recv_sems = refs[n_in], refs[n_in + 1]
        for sent, _ in _exchange_copies(refs[:n_arr], refs[n_arr:2 * n_arr], send_sems, recv_sems, scatter, peers):
            sent().start()
        if own:
            for local in _own_copies(refs[:n_arr], refs[n_arr:2 * n_arr], refs[n_in + 2], scatter):
                local().start()
        refs[-1][...] = jnp.zeros_like(refs[-1])

    sems = [pltpu.SemaphoreType.DMA((n_sem,)), pltpu.SemaphoreType.DMA((n_sem,))] + [pltpu.SemaphoreType.DMA((n_arr,))] * own
    through = [pltpu.HBM(t.shape, t.dtype) for t in (*srcs, *lands)]
    res = pl.pallas_call(
        body, name=name, out_shape=(*sems, *through, jax.ShapeDtypeStruct((8, LANES), F32)),
        in_specs=[_IN_HBM] * (2 * n_arr) + [pl.BlockSpec(memory_space=pl.ANY)] * (after is not None),
        out_specs=(*[_IN_SEMAPHORE] * len(sems), *[_IN_HBM] * (2 * n_arr), pl.BlockSpec(memory_space=pltpu.VMEM)),
        input_output_aliases={i: len(sems) + i for i in range(2 * n_arr)},
        compiler_params=pltpu.CompilerParams(has_side_effects=pltpu.SideEffectType.DATAFLOW_SIDE_EFFECTING),
    )(*[pltpu.with_memory_space_constraint(t, pltpu.HBM) for t in (*srcs, *lands)],
      *([after] if after is not None else []))
    first = len(sems)
    handle = dict(sems=res[:first], srcs=res[first:first + n_arr], lands=res[first + n_arr:first + 2 * n_arr],
                  scatter=scatter, peers=peers, own=own)
    return handle, res[-1]


def _exchange_wait(handle, *, after, name, srcs=None, lands=None):
    srcs = handle["srcs"] if srcs is None else srcs
    lands = handle["lands"] if lands is None else lands
    sems, scatter, peers, own = handle["sems"], handle["scatter"], handle["peers"], handle["own"]
    n_arr = len(srcs)
    after = list(after)

    def body(*refs):
        src_refs, land_refs = refs[:n_arr], refs[n_arr:2 * n_arr]
        for sent, received in _exchange_copies(src_refs, land_refs, refs[2 * n_arr], refs[2 * n_arr + 1], scatter, peers):
            sent().wait_send()
            received().wait_recv()
        if own:
            for local in _own_copies(src_refs, land_refs, refs[2 * n_arr + 2], scatter):
                local().wait()

    res = pl.pallas_call(
        body, name=name, out_shape=tuple(pltpu.HBM(t.shape, t.dtype) for t in (*srcs, *lands)),
        in_specs=[_IN_HBM] * (2 * n_arr) + [_IN_SEMAPHORE] * len(sems) + [pl.BlockSpec(memory_space=pl.ANY)] * len(after),
        out_specs=tuple([_IN_HBM] * (2 * n_arr)), input_output_aliases={i: i for i in range(2 * n_arr)},
        compiler_params=pltpu.CompilerParams(has_side_effects=pltpu.SideEffectType.DATAFLOW_SIDE_EFFECTING),
    )(*srcs, *lands, *sems, *after)
    return res[:n_arr], res[n_arr:]


def _adamw(g, w, m, v):
    m_new = ADAM_B1 * m + (1.0 - ADAM_B1) * g
    v_new = ADAM_B2 * v + (1.0 - ADAM_B2) * jnp.square(g)
    m_hat = m_new / (1.0 - ADAM_B1 ** ADAM_STEP)
    v_hat = v_new / (1.0 - ADAM_B2 ** ADAM_STEP)
    return g, -ADAM_LR * (m_hat / (jnp.sqrt(v_hat) + ADAM_EPS) + ADAM_WD * w), m_new, v_new


def _reduce_adamw(gstack, w, m, v, *, name, tr=128):
    n_rows, cols = w.shape
    tr = min(tr, n_rows)
    assert n_rows % tr == 0, (name, n_rows, tr)

    def body(g_ref, w_ref, m_ref, v_ref, *out_refs):
        g = g_ref[0].astype(F32)
        for dev in range(1, N_DEV):
            g = g + g_ref[dev].astype(F32)
        for o_ref, val in zip(out_refs, _adamw(g, w_ref[...], m_ref[...], v_ref[...])):
            o_ref[...] = val

    flat = pl.BlockSpec((tr, cols), lambda i: (i, 0))
    shape = jax.ShapeDtypeStruct((n_rows, cols), F32)
    return pl.pallas_call(
        body, name=name, grid=(n_rows // tr,),
        in_specs=[pl.BlockSpec((N_DEV, tr, cols), lambda i: (0, i, 0)), flat, flat, flat],
        out_specs=[flat] * 4, out_shape=[shape] * 4, compiler_params=_params("parallel"),
    )(gstack, w, m, v)


SMALL_FLAT_SSM = ("ssm_b_re", "ssm_b_im", "ssm_c_re", "ssm_c_im")


def _small_view(name, shape):
    size = int(np.prod(shape))
    if name in SMALL_FLAT_SSM:
        return SSM_GROUPS, size // SSM_GROUPS
    if name in ("ssm_a_re", "ssm_a_im"):
        return SSM_GROUPS, SSM_STATE
    return 1, size


def _pack_rows(view):
    return -(-(view[0] * view[1]) // PACK_COLS)


SMALL_LATE = ("ln_in_g", "ln_in_b")
SMALL_EARLY = tuple(n for n in SMALL if n not in SMALL_LATE)


def _pack_small(gs, names, views):
    parts = []
    for n in names:
        flat = gs[n].reshape(-1).astype(WIRE_DTYPE)
        parts.append(jnp.pad(flat, (0, _pack_rows(views[n]) * PACK_COLS - flat.shape[0])))
    total = sum(p.shape[0] for p in parts) // PACK_COLS
    parts.append(jnp.zeros(((-total % PACK_ROW_ALIGN) * PACK_COLS,), WIRE_DTYPE))
    return jnp.concatenate(parts).reshape(-1, PACK_COLS)


def _small_pieces(view):
    rows, cols = view
    if cols == PACK_COLS:
        return [(0, rows, 0, 0, 0, cols)]
    if rows == 1 and cols > PACK_COLS:
        return [(kk, 1, 0, 0, kk * PACK_COLS, PACK_COLS) for kk in range(cols // PACK_COLS)]
    if rows == 1:
        return [(0, 1, 0, 0, 0, cols)]
    return [((r * cols) // PACK_COLS, 1, (r * cols) % PACK_COLS, r, 0, cols) for r in range(rows)]


def _adamw_small(stacks, views, w, m, v, *, name):
    n = len(SMALL)
    place, first = {}, [0, 0]
    for k, names in enumerate((SMALL_EARLY, SMALL_LATE)):
        for name_ in names:
            place[name_] = (k, first[k])
            first[k] += _pack_rows(views[name_])

    def body(early_ref, late_ref, *refs):
        ins, outs = refs[:3 * n], refs[3 * n:]
        for i, name_ in enumerate(SMALL):
            stack_ref = (early_ref, late_ref)[place[name_][0]]
            row0 = place[name_][1]
            for prow, nrows, lane, orow, ocol, width in _small_pieces(views[name_]):
                src = (slice(row0 + prow, row0 + prow + nrows), slice(lane, lane + width))
                dst = (slice(orow, orow + nrows), slice(ocol, ocol + width))
                g = stack_ref[(0,) + src].astype(F32)
                for dev in range(1, N_DEV):
                    g = g + stack_ref[(dev,) + src].astype(F32)
                res = _adamw(g, ins[i][dst], ins[n + i][dst], ins[2 * n + i][dst])
                for kk, val in enumerate(res):
                    outs[kk * n + i][dst] = val

    args = [*stacks, *[d[name_] for d in (w, m, v) for name_ in SMALL]]
    out_views = [views[name_] for _ in range(4) for name_ in SMALL]
    res = pl.pallas_call(
        body, name=name, grid=(1,), in_specs=[_full_spec(t.shape) for t in args],
        out_specs=[_full_spec(s) for s in out_views], out_shape=[jax.ShapeDtypeStruct(s, F32) for s in out_views],
        compiler_params=_params("arbitrary"),
    )(*args)
    return [dict(zip(SMALL, res[kk * n:(kk + 1) * n])) for kk in range(4)]


def kernel(x, mem, positions, ln_in_g, ln_in_b, w_in, b_in, ssm_log_dt, ssm_a_re, ssm_a_im, ssm_b_re, ssm_b_im, ssm_c_re, ssm_c_im, ssm_d, w_glu, b_glu, w_att_up, w_mix_out, b_mix_out, ln1_g, ln1_b, w_xq, w_xkv, w_xo, ln2_g, ln2_b, w_ff1, b_ff1, w_ff2, b_ff2, ln3_g, ln3_b, loss_target, m_ln_in_g, m_ln_in_b, m_w_in, m_b_in, m_ssm_log_dt, m_ssm_a_re, m_ssm_a_im, m_ssm_b_re, m_ssm_b_im, m_ssm_c_re, m_ssm_c_im, m_ssm_d, m_w_glu, m_b_glu, m_w_att_up, m_w_mix_out, m_b_mix_out, m_ln1_g, m_ln1_b, m_w_xq, m_w_xkv, m_w_xo, m_ln2_g, m_ln2_b, m_w_ff1, m_b_ff1, m_w_ff2, m_b_ff2, m_ln3_g, m_ln3_b, v_ln_in_g, v_ln_in_b, v_w_in, v_b_in, v_ssm_log_dt, v_ssm_a_re, v_ssm_a_im, v_ssm_b_re, v_ssm_b_im, v_ssm_c_re, v_ssm_c_im, v_ssm_d, v_w_glu, v_b_glu, v_w_att_up, v_w_mix_out, v_b_mix_out, v_ln1_g, v_ln1_b, v_w_xq, v_w_xkv, v_w_xo, v_ln2_g, v_ln2_b, v_w_ff1, v_b_ff1, v_w_ff2, v_b_ff2, v_ln3_g, v_ln3_b):
    given = dict(locals())
    w_arg = {n: given[n] for n in WEIGHTS}
    m_arg = {n: given["m_" + n] for n in WEIGHTS}
    v_arg = {n: given["v_" + n] for n in WEIGHTS}

    in_near, token = _exchange_start([w_arg["w_in"][0].astype(MXU_DTYPE)], scatter=False, peers=NEAR_PEERS,
                                     name="gather_start_in_near")
    in_far, token = _exchange_start(in_near["srcs"], scatter=False, peers=FAR_PEERS, lands=in_near["lands"],
                                    after=token, name="gather_start_in_far")
    w_in_state = [in_far["srcs"], in_far["lands"]]
    token, w_arg, m_arg, v_arg = lax.optimization_barrier((token, w_arg, m_arg, v_arg))
    shards = {n: w_arg[n][0].astype(MXU_DTYPE) for n in BIG if n != "w_in"}
    gathers = []
    for i, names in enumerate(GATHER_GROUPS):
        handle, token = _exchange_start([shards[n] for n in names], scatter=False, after=token, name=f"gather_start_{i}")
        gathers.append(handle)

    small_views = {n: _small_view(n, w_arg[n].shape) for n in SMALL}
    small_w, small_m, small_v = [{n: d[n].reshape(small_views[n]) for n in SMALL} for d in (w_arg, m_arg, v_arg)]
    relaid = [d[n] for d in (small_w, small_m, small_v) for n in SMALL_FLAT_SSM]

    def fetch_in(part, after):
        handle, peers, tag = ((in_near, (0,) + NEAR_PEERS, "near"), (in_far, FAR_PEERS, "far"))[part]
        w_in_state[:] = _exchange_wait(handle, after=after + (relaid if part == 0 else []), srcs=w_in_state[0],
                                       lands=w_in_state[1], name="gather_wait_in_" + tag)
        return w_in_state[1][0], jnp.stack([_peer_index(kk) for kk in peers]).astype(jnp.int32)

    def fetch(i, after):
        _, lands = _exchange_wait(gathers[i], after=after, name=f"gather_wait_{i}")
        full = dict(zip(GATHER_GROUPS[i], lands))
        return {n: t if n in BIG_COL_SHARDED else t.reshape(-1, t.shape[-1]) for n, t in full.items()}

    scatters = {}

    def send(i, gw):
        slots = [gw[n] if n in BIG_COL_SHARDED else gw[n].reshape(N_DEV, -1, gw[n].shape[-1]) for n in SCATTER_GROUPS[i]]
        handle, sent = _exchange_start(slots, scatter=True, name=f"scatter_start_{i}")
        scatters[i] = (handle, slots)
        return sent

    sm = {}
    for n in SMALL:
        t = w_arg[n]
        if n.startswith("ssm_") and n not in ("ssm_d", "ssm_log_dt"):
            sm[n] = t[0]
        else:
            sm[n] = t.reshape(1, -1)

    smalls = []

    def send_small(gs, names):
        handle, sent = _exchange_start([_pack_small(gs, names, small_views)], scatter=False,
                                       name=f"small_start_{len(smalls)}")
        smalls.append(handle)
        return sent

    loss_row, grad_x, gs = _local_grads(x[0], mem[0], positions.reshape(-1, 1), loss_target[0], sm, fetch_in, fetch,
                                        send, send_small, token)
    loss = lax.psum(loss_row[0, 0], ("x", "y", "c"))
    send_small(gs, SMALL_LATE)

    results = [{}, {}, {}, {}]
    done = grad_x
    for i, names in enumerate(SCATTER_GROUPS):
        handle, slots = scatters[i]
        _, lands = _exchange_wait(handle, after=[done], name=f"scatter_wait_{i}")
        for n, land, slot in zip(names, lands, slots):
            res = _reduce_adamw(land, w_arg[n][0], m_arg[n][0], v_arg[n][0], name="adamw_" + n)
            done = res[0]
            for d, r in zip(results, res):
                d[n] = r[None]
    stacks = [_exchange_wait(handle, after=[done], name=f"small_wait_{i}")[1][0] for i, handle in enumerate(smalls)]
    res = _adamw_small(stacks, small_views, small_w, small_m, small_v, name="adamw_small")
    for d, r in zip(results, res):
        d.update({n: r[n].reshape(w_arg[n].shape) for n in SMALL})
    out = [loss, grad_x[None]]
    for d in results:
        out += [d[n] for n in WEIGHTS]
    return tuple(out)
```

```python
import functools

import numpy as np
import jax
import jax.numpy as jnp
from jax import lax
from jax.experimental import pallas as pl
from jax.experimental.pallas import tpu as pltpu

F32 = jnp.float32
MXU_DTYPE = jnp.bfloat16
WIRE_DTYPE = jnp.bfloat16
VMEM_LIMIT_BYTES = 48 * 1024 * 1024
LANES = 128

N_DEV = 8
D_MODEL = 1024
SSM_GROUP = 16
SSM_WIDTH = 768
SSM_GROUPS = SSM_WIDTH // SSM_GROUP
SSM_STATE = 64
SSM_CH = SSM_GROUPS * SSM_STATE
SSM_TILES = SSM_WIDTH // LANES
GROUPS_PER_TILE = LANES // SSM_GROUP
STATE_VREG_ROWS = SSM_CH // LANES
ATT_HEAD_DIM = 64
ATT_HEADS_PER_GROUP = 4
ATT_MERGED = ATT_HEADS_PER_GROUP * ATT_HEAD_DIM
LANE_HALVES = ATT_MERGED // LANES
DILATIONS = (1, 4, 16)
ATT_BLK = 128
ATT_SCALE = ATT_HEAD_DIM ** -0.5
ROT_DIM = ATT_HEAD_DIM // 4
ROPE_THETA = 500000.0
XATT_HEADS = 4
XATT_HEAD_DIM = D_MODEL // XATT_HEADS
XATT_SCALE = XATT_HEAD_DIM ** -0.5
DEEPNORM_ALPHA = 2.0 ** 0.25
LN_EPS = 1e-5
NEG_INF = -1e30
OFF_Q_BLK, OFF_K_BLK, OFF_V_BLK = 3, 6, 9
OFF_GS_BLK, OFF_GA_BLK = 3, 4

ADAM_LR = 0.001
ADAM_B1 = 0.9
ADAM_B2 = 0.999
ADAM_EPS = 1e-08
ADAM_WD = 0.01
ADAM_STEP = 10

BIG = ("w_in", "w_glu", "w_att_up", "w_mix_out", "w_xq", "w_xkv", "w_xo", "w_ff1", "w_ff2")
BIG_COL_SHARDED = ("w_in", "w_glu", "w_att_up", "w_xkv", "w_ff1")
WEIGHTS = ("ln_in_g", "ln_in_b", "w_in", "b_in", "ssm_log_dt", "ssm_a_re", "ssm_a_im", "ssm_b_re", "ssm_b_im",
           "ssm_c_re", "ssm_c_im", "ssm_d", "w_glu", "b_glu", "w_att_up", "w_mix_out", "b_mix_out", "ln1_g", "ln1_b",
           "w_xq", "w_xkv", "w_xo", "ln2_g", "ln2_b", "w_ff1", "b_ff1", "w_ff2", "b_ff2", "ln3_g", "ln3_b")
SMALL = tuple(n for n in WEIGHTS if n not in BIG)
PACK_COLS = 1024
PACK_ROW_ALIGN = 16


def _params(*sem):
    return pltpu.CompilerParams(dimension_semantics=sem, vmem_limit_bytes=VMEM_LIMIT_BYTES)


def _in_hbm(arrays):
    return [pltpu.with_memory_space_constraint(t, pltpu.HBM) for t in arrays]


def _dot(a, b, ca, cb):
    return lax.dot_general(a.astype(MXU_DTYPE), b.astype(MXU_DTYPE), (((ca,), (cb,)), ((), ())),
                           preferred_element_type=F32)


def _fit(dim, pref):
    if dim <= pref:
        return dim
    best = max(t for t in range(LANES, pref + 1, LANES) if dim % t == 0)
    return best


def _mm(a, b, *, name, ta=False, tb=False, bias=None, out_dtype=F32, b_shards=False, out_shards=False, after=None,
        also=None, gate=None, colsum=False, tm=2048, tn=1024, tk=1024):
    m, k = (a.shape[1], a.shape[0]) if ta else a.shape
    order = (lambda f: (lambda j, i, kk: f(i, j, kk))) if colsum else (lambda f: f)
    spec = lambda shape, f: pl.BlockSpec(shape, order(f))
    if b_shards:
        n_sh, rows, n_loc = b.shape
        if tb:
            n, tn, tk = rows, _fit(rows, tn), n_loc
            assert k == n_sh * n_loc, (name, k, b.shape)
            b_spec = spec((1, tn, tk), lambda i, j, kk: (kk, j, 0))
        else:
            n, tn, tk = n_sh * n_loc, n_loc, _fit(k, tk)
            b_spec = spec((1, tk, tn), lambda i, j, kk: (j, kk, 0))
    else:
        n = b.shape[0] if tb else b.shape[1]
        tn = n // N_DEV if out_shards else _fit(n, tn)
        tk = _fit(k, tk)
        b_spec = spec((tn, tk), lambda i, j, kk: (j, kk)) if tb else spec((tk, tn), lambda i, j, kk: (kk, j))
    tm = _fit(m, tm)
    nk = k // tk
    a_spec = spec((tk, tm), lambda i, j, kk: (kk, i)) if ta else spec((tm, tk), lambda i, j, kk: (i, kk))
    tile = spec((tm, tn), lambda i, j, kk: (i, j))
    in_specs, args = [a_spec, b_spec], [a, b]
    if bias is not None:
        in_specs.append(spec((1, tn), lambda i, j, kk: (0, j)))
        args.append(bias)
    if gate is not None:
        in_specs.append(tile)
        args.append(gate[0])
    if after is not None:
        in_specs.append(pl.BlockSpec(memory_space=pl.ANY))
        args.append(after)
    n_in = len(args)
    if out_shards:
        assert n == N_DEV * tn, (name, n, tn)
        out_specs = [spec((1, tm, tn), lambda i, j, kk: (j, i, 0))]
        out_shape = [jax.ShapeDtypeStruct((N_DEV, m, tn), out_dtype)]
    else:
        out_specs = [tile]
        out_shape = [jax.ShapeDtypeStruct((m, n), out_dtype)]
    if also is not None:
        out_specs.append(tile)
        out_shape.append(jax.ShapeDtypeStruct((m, n), also[1]))
    if colsum:
        out_specs.append(spec((1, tn), lambda i, j, kk: (0, j)))
        out_shape.append(jax.ShapeDtypeStruct((1, n), F32))

    def body(*refs):
        a_ref, b_ref = refs[0], refs[1]
        o_ref = refs[n_in]

        def product():
            return _dot(a_ref[...], b_ref[0] if b_shards else b_ref[...], 0 if ta else 1, 1 if tb else 0)

        def finish(r):
            if bias is not None:
                r = r + refs[2][...]
            if gate is not None:
                r = r * gate[1](refs[2 + (bias is not None)][...])
            if out_shards:
                o_ref[0] = r.astype(o_ref.dtype)
            else:
                o_ref[...] = r.astype(o_ref.dtype)
            if also is not None:
                refs[n_in + 1][...] = also[0](r).astype(also[1])
            if colsum:
                s_ref = refs[n_in + 1 + (also is not None)]

                @pl.when(pl.program_id(1) == 0)
                def _():
                    s_ref[...] = jnp.zeros_like(s_ref)

                s_ref[...] += _colsum(r)

        if nk == 1:
            finish(product())
            return
        acc_ref = refs[-1]
        kk = pl.program_id(2)

        @pl.when(kk == 0)
        def _():
            acc_ref[...] = jnp.zeros_like(acc_ref)

        acc_ref[...] += product()

        @pl.when(kk == nk - 1)
        def _():
            finish(acc_ref[...])

    grid = (n // tn, m // tm, nk) if colsum else (m // tm, n // tn, nk)
    res = pl.pallas_call(
        body, name=name, grid=grid, in_specs=in_specs, out_specs=out_specs, out_shape=out_shape,
        scratch_shapes=[pltpu.VMEM((tm, tn), F32)] if nk > 1 else [],
        compiler_params=_params("parallel", "arbitrary" if colsum else "parallel", "arbitrary"),
    )(*_in_hbm(args))
    return res[0] if len(res) == 1 else res


def _mm_shards(a, w, bias, shard_ids, *, name, prev=None, tm=2048):
    m, k = a.shape
    n_sh, _, n_loc = w.shape
    tm = _fit(m, tm)

    def body(ids_ref, a_ref, w_ref, b_ref, *rest):
        rest[-1][...] = _dot(a_ref[...], w_ref[0], 1, 0) + b_ref[...]

    grid_spec = pltpu.PrefetchScalarGridSpec(
        num_scalar_prefetch=1, grid=(m // tm, shard_ids.shape[0]),
        in_specs=[pl.BlockSpec((tm, k), lambda i, j, ids: (i, 0)),
                  pl.BlockSpec((1, k, n_loc), lambda i, j, ids: (ids[j], 0, 0)),
                  pl.BlockSpec((1, n_loc), lambda i, j, ids: (0, ids[j]))]
        + [pl.BlockSpec(memory_space=pl.ANY)] * (prev is not None),
        out_specs=pl.BlockSpec((tm, n_loc), lambda i, j, ids: (i, ids[j])))
    return pl.pallas_call(
        body, name=name, grid_spec=grid_spec, out_shape=jax.ShapeDtypeStruct((m, n_sh * n_loc), F32),
        input_output_aliases={4: 0} if prev is not None else {}, compiler_params=_params("parallel", "arbitrary"),
    )(shard_ids, a, w, bias, *([prev] if prev is not None else []))


def _rowcall(fn, rows, fulls, row_outs, acc_outs=(), *, n_rows, tm, name, after=None):
    n_r, n_f, n_o, n_a = len(rows), len(fulls), len(row_outs), len(acc_outs)
    n_in = n_r + n_f + (after is not None)
    assert n_rows % tm == 0, (name, n_rows, tm)

    def body(*refs):
        res = fn(*[r[...] for r in refs[:n_r + n_f]])
        res = tuple(res) if isinstance(res, (tuple, list)) else (res,)
        o_refs = refs[n_in:n_in + n_o]
        a_refs = refs[n_in + n_o:]
        for o_ref, val in zip(o_refs, res[:n_o]):
            o_ref[...] = val.astype(o_ref.dtype)
        if n_a:
            @pl.when(pl.program_id(0) == 0)
            def _():
                for a_ref in a_refs:
                    a_ref[...] = jnp.zeros_like(a_ref)

            for a_ref, val in zip(a_refs, res[n_o:]):
                a_ref[...] += val

    in_specs = [pl.BlockSpec((tm, w), functools.partial(lambda i, cb: (i, cb), cb=cb)) for _, w, cb in rows]
    in_specs += [pl.BlockSpec(f.shape, functools.partial(lambda i, nd: (0,) * nd, nd=f.ndim)) for f in fulls]
    in_specs += [pl.BlockSpec(memory_space=pl.ANY)] * (after is not None)
    out_specs = [pl.BlockSpec((tm, w), lambda i: (i, 0)) for w, _ in row_outs]
    out_specs += [pl.BlockSpec((1, w), lambda i: (0, 0)) for w in acc_outs]
    out_shape = [jax.ShapeDtypeStruct((n_rows, w), dt) for w, dt in row_outs]
    out_shape += [jax.ShapeDtypeStruct((1, w), F32) for w in acc_outs]
    return pl.pallas_call(
        body, name=name, grid=(n_rows // tm,), in_specs=in_specs, out_specs=out_specs, out_shape=out_shape,
        compiler_params=_params("arbitrary" if n_a else "parallel"),
    )(*_in_hbm([r[0] for r in rows]), *_in_hbm(fulls), *([after] if after is not None else []))


def _colsum(v):
    return jnp.sum(v, axis=0, keepdims=True)


def _ln_fwd(a, r, g, b, *, alpha, name):
    n_rows, d = a.shape

    def fn(*t):
        xin = t[0] if alpha == 1.0 else alpha * t[0]
        if r is not None:
            xin = xin + t[1]
        gv, bv = t[-2], t[-1]
        mu = jnp.mean(xin, axis=-1, keepdims=True)
        xc = xin - mu
        var = jnp.mean(xc * xc, axis=-1, keepdims=True)
        rstd = lax.rsqrt(var + LN_EPS)
        xh = xc * rstd
        y = xh * gv + bv
        return y, xh, rstd, y

    rows = [(a, d, 0)] + ([(r, d, 0)] if r is not None else [])
    return _rowcall(fn, rows, [g, b], [(d, F32), (d, F32), (1, F32), (d, MXU_DTYPE)], n_rows=n_rows, tm=256, name=name)


def _ln_bwd(dya, dyb, xh, rstd, g, *, alpha, name, operand=True):
    n_rows, d = xh.shape

    def fn(da, db, xhv, rs, gv):
        dy = alpha * da + db
        dyg = dy * gv
        m1 = jnp.mean(dyg, axis=-1, keepdims=True)
        m2 = jnp.mean(dyg * xhv, axis=-1, keepdims=True)
        dx = rs * (dyg - m1 - xhv * m2)
        return (dx,) + ((dx,) if operand else ()) + (_colsum(dy * xhv), _colsum(dy), _colsum(dx))

    rows = [(dya, d, 0), (dyb, d, 0), (xh, d, 0), (rstd, 1, 0)]
    return _rowcall(fn, rows, [g], [(d, F32)] + [(d, MXU_DTYPE)] * operand, [d, d, d], n_rows=n_rows, tm=256, name=name)


def _ln_loss_bwd(a, r, target, g, b, *, alpha, name):
    n_rows, d = a.shape

    def fn(av, rv, tv, gv, bv):
        xin = alpha * av + rv
        mu = jnp.mean(xin, axis=-1, keepdims=True)
        xc = xin - mu
        var = jnp.mean(xc * xc, axis=-1, keepdims=True)
        rs = lax.rsqrt(var + LN_EPS)
        xh = xc * rs
        diff = xh * gv + bv - tv
        part = jnp.sum(jnp.sum(diff * diff, axis=1, keepdims=True), axis=0, keepdims=True) * (0.5 / d)
        dy = diff * (1.0 / d)
        dyg = dy * gv
        m1 = jnp.mean(dyg, axis=-1, keepdims=True)
        m2 = jnp.mean(dyg * xh, axis=-1, keepdims=True)
        dx = rs * (dyg - m1 - xh * m2)
        return dx, dx, _colsum(dy * xh), _colsum(dy), _colsum(dx), jnp.broadcast_to(part, (1, LANES))

    return _rowcall(fn, [(a, d, 0), (r, d, 0), (target, d, 0)], [g, b], [(d, F32), (d, MXU_DTYPE)], [d, d, d, LANES],
                    n_rows=n_rows, tm=256, name=name)


def _rope_lane_constants():
    lane = np.arange(ATT_MERGED)
    in_head = lane % ATT_HEAD_DIM
    sign = np.where(in_head < ROT_DIM // 2, -1.0, np.where(in_head < ROT_DIM, 1.0, 0.0)).astype(np.float32)
    inv_freq = ROPE_THETA ** (-jnp.arange(0, ROT_DIM, 2, dtype=F32) / ROT_DIM)
    return inv_freq[lane % (ROT_DIM // 2)].reshape(1, ATT_MERGED), jnp.asarray(sign).reshape(1, ATT_MERGED)


def _rope_tables(pos_col, *, name, after=None):
    inv_lane, sign = _rope_lane_constants()

    def fn(pos, inv, sg):
        ang = pos.astype(F32) * inv
        return jnp.where(sg != 0.0, jnp.cos(ang), 1.0), sg * jnp.sin(ang)

    return _rowcall(fn, [(pos_col, 1, 0)], [inv_lane, sign], [(ATT_MERGED, F32), (ATT_MERGED, F32)],
                    n_rows=pos_col.shape[0], tm=512, name=name, after=after)


def _rot_partner(t):
    lane = lax.broadcasted_iota(jnp.int32, t.shape, 1)
    width = t.shape[1]
    return jnp.where((lane & (ROT_DIM // 2)) == 0, pltpu.roll(t, width - ROT_DIM // 2, 1), pltpu.roll(t, ROT_DIM // 2, 1))


def _rope(t, cos_t, sin_t):
    return t * cos_t + _rot_partner(t) * sin_t


def _rope_transpose(dt, cos_t, sin_t):
    return dt * cos_t + _rot_partner(dt * sin_t)


def _strided_rows(r, count, stride):
    return pl.ds(r, count) if stride == 1 else pl.ds(r, count, stride=stride)


def _qkv_split(proj, cos_t, sin_t, *, name, tm=512):
    n_rows = proj.shape[0]
    n_g = len(DILATIONS)

    def body(*refs):
        n_src = LANE_HALVES * 3 * n_g
        src, tables, dst = refs[:n_src], refs[n_src:n_src + 2 * LANE_HALVES], refs[n_src + 2 * LANE_HALVES:]
        for kind in range(3):
            for g, dil in enumerate(DILATIONS):
                for half in range(LANE_HALVES):
                    x_ref, o_ref = src[(kind * n_g + g) * LANE_HALVES + half], dst[kind * n_g + g]
                    cos_ref, sin_ref = tables[half], tables[LANE_HALVES + half]
                    for r in range(dil):
                        rows = _strided_rows(r, tm // dil, dil)
                        t = x_ref[rows, :]
                        if kind < 2:
                            t = _rope(t, cos_ref[rows, :], sin_ref[rows, :])
                        lo = r * ATT_MERGED + half * LANES
                        o_ref[:, lo:lo + LANES] = t.astype(o_ref.dtype)

    half_spec = lambda cb: pl.BlockSpec((tm, LANES), functools.partial(lambda i, cb: (i, cb), cb=cb))
    in_specs = [half_spec((off + g) * LANE_HALVES + half)
                for off in (OFF_Q_BLK, OFF_K_BLK, OFF_V_BLK) for g in range(n_g) for half in range(LANE_HALVES)]
    in_specs += [half_spec(half) for _ in range(2) for half in range(LANE_HALVES)]
    out_specs = [pl.BlockSpec((tm // dil, dil * ATT_MERGED), lambda i: (i, 0)) for _ in range(3) for dil in DILATIONS]
    out_shape = [jax.ShapeDtypeStruct((n_rows // dil, dil * ATT_MERGED), MXU_DTYPE) for _ in range(3) for dil in DILATIONS]
    outs = pl.pallas_call(
        body, name=name, grid=(n_rows // tm,), in_specs=in_specs, out_specs=out_specs, out_shape=out_shape,
        compiler_params=_params("parallel"),
    )(*[proj] * (LANE_HALVES * 3 * n_g), *[cos_t] * LANE_HALVES, *[sin_t] * LANE_HALVES)
    return outs[:n_g], outs[n_g:2 * n_g], outs[2 * n_g:]


def _mix(gs, ga, z1, z2, b_att):
    return jax.nn.sigmoid(gs) * (z1 * jax.nn.sigmoid(z2)) + jax.nn.sigmoid(ga) * b_att


def _mix_rows(proj, z, b_att):
    return [(proj, D_MODEL, OFF_GS_BLK), (proj, D_MODEL, OFF_GA_BLK), (z, D_MODEL, 0), (z, D_MODEL, 1), (b_att, D_MODEL, 0)]


def _mix_fwd(proj, z, b_att, *, name):
    return _rowcall(_mix, _mix_rows(proj, z, b_att), [], [(D_MODEL, MXU_DTYPE)],
                    n_rows=proj.shape[0], tm=256, name=name)[0]


def _mix_bwd(dmixed, proj, z, b_att, *, name):
    def fn(dm, gs, ga, z1, z2, ba):
        _, vjp = jax.vjp(_mix, gs, ga, z1, z2, ba)
        dgs, dga, dz1, dz2, dba = vjp(dm)
        dz = jnp.concatenate([dz1, dz2], axis=1)
        return dgs, dga, dz, dba, _colsum(dgs), _colsum(dga), _colsum(dz)

    rows = [(dmixed, D_MODEL, 0)] + _mix_rows(proj, z, b_att)
    widths = [D_MODEL, D_MODEL, 2 * D_MODEL, D_MODEL]
    return _rowcall(fn, rows, [], [(w, MXU_DTYPE) for w in widths], widths[:3], n_rows=proj.shape[0], tm=256, name=name)


def _gelu_bwd(dgy, y, proj, *, name):
    def fn(dg, yv, u):
        _, vjp = jax.vjp(jax.nn.gelu, yv)
        dy = vjp(dg)[0]
        return dy, _colsum(dy * u)

    return _rowcall(fn, [(dgy, SSM_WIDTH, 0), (y, SSM_WIDTH, 0), (proj, SSM_WIDTH, 0)], [], [(SSM_WIDTH, F32)],
                    [SSM_WIDTH], n_rows=y.shape[0], tm=512, name=name)


HEAD_ROWS = ATT_HEADS_PER_GROUP * ATT_BLK


def _head_masks(rows):
    head = lax.broadcasted_iota(jnp.int32, (rows, ATT_MERGED), 1) >> (ATT_HEAD_DIM.bit_length() - 1)
    return [head == h for h in range(ATT_HEADS_PER_GROUP)]


def _stack_heads(t, masks):
    return jnp.concatenate([jnp.where(m, t, jnp.zeros_like(t)) for m in masks], axis=0)


def _unstack_heads(t4, masks):
    blocks = [t4[h * ATT_BLK:(h + 1) * ATT_BLK] for h in range(ATT_HEADS_PER_GROUP)]
    return jnp.where(masks[0], blocks[0], jnp.where(masks[1], blocks[1], jnp.where(masks[2], blocks[2], blocks[3])))


def _head_column(stats, first):
    return jnp.concatenate([stats[:, first + h:first + h + 1] for h in range(ATT_HEADS_PER_GROUP)], axis=0)


def _band_mask(first_key):
    qi = lax.broadcasted_iota(jnp.int32, (HEAD_ROWS, 2 * ATT_BLK), 0) & (ATT_BLK - 1)
    ki = lax.broadcasted_iota(jnp.int32, (HEAD_ROWS, 2 * ATT_BLK), 1)
    steps = qi + ATT_BLK - ki
    return (steps >= 0) & (steps <= ATT_BLK) & (ki >= first_key)


def _dil_fwd(q, k, v, dil, *, name):
    n_blk = q.shape[0] // ATT_BLK
    cur = pl.BlockSpec((ATT_BLK, ATT_MERGED), lambda r, n: (n, r))
    prev = pl.BlockSpec((ATT_BLK, ATT_MERGED), lambda r, n: (jnp.maximum(n - 1, 0), r))

    def body(q_ref, kp_ref, kc_ref, vp_ref, vc_ref, o_ref, l_ref):
        masks = _head_masks(ATT_BLK)
        valid = _band_mask(jnp.where(pl.program_id(1) > 0, 0, ATT_BLK))
        keys = jnp.concatenate([kp_ref[...], kc_ref[...]], axis=0)
        vals = jnp.concatenate([vp_ref[...], vc_ref[...]], axis=0)
        s = jnp.where(valid, _dot(_stack_heads(q_ref[...], masks), keys, 1, 1) * ATT_SCALE, NEG_INF)
        m = jnp.max(s, axis=-1, keepdims=True)
        p = jnp.exp(s - m)
        den = jnp.sum(p, axis=-1, keepdims=True)
        o_ref[...] = _unstack_heads(_dot(p, vals, 1, 0) / den, masks)
        l_ref[...] = _unstack_heads(jnp.broadcast_to(m + jnp.log(den), (HEAD_ROWS, ATT_MERGED)), masks)

    shape = jax.ShapeDtypeStruct(q.shape, F32)
    return pl.pallas_call(
        body, name=name, grid=(dil, n_blk), in_specs=[cur, prev, cur, prev, cur], out_specs=[cur, cur],
        out_shape=[shape, shape], compiler_params=_params("parallel", "parallel"),
    )(q, k, k, v, v)


def _att_merge(outs, lses, *, name, tm=512):
    n_g = len(outs)
    n_rows = outs[0].shape[0] * DILATIONS[0]

    def body(*refs):
        src, (att_ref, lse_ref), tmp = refs[:2 * n_g], refs[2 * n_g:2 * n_g + 2], refs[2 * n_g + 2:]
        vals = []
        for idx, src_ref in enumerate(src):
            dil = DILATIONS[idx % n_g]
            if dil == 1:
                vals.append(src_ref[...])
                continue
            for r in range(dil):
                for half in range(LANE_HALVES):
                    lo = r * ATT_MERGED + half * LANES
                    tmp[LANE_HALVES * idx + half][_strided_rows(r, tm // dil, dil), :] = src_ref[:, lo:lo + LANES]
            vals.append(jnp.concatenate([tmp[LANE_HALVES * idx + half][...] for half in range(LANE_HALVES)], axis=1))
        o, l = vals[:n_g], vals[n_g:]
        m = functools.reduce(jnp.maximum, l)
        e = [jnp.exp(li - m) for li in l]
        z = functools.reduce(jnp.add, e)
        att_ref[...] = functools.reduce(jnp.add, [(ei / z) * oi for ei, oi in zip(e, o)])
        lse_ref[...] = m + jnp.log(z)

    in_specs = [pl.BlockSpec((tm // dil, dil * ATT_MERGED), lambda i: (i, 0)) for _ in range(2) for dil in DILATIONS]
    row = pl.BlockSpec((tm, ATT_MERGED), lambda i: (i, 0))
    shape = jax.ShapeDtypeStruct((n_rows, ATT_MERGED), F32)
    return pl.pallas_call(
        body, name=name, grid=(n_rows // tm,), in_specs=in_specs, out_specs=[row, row], out_shape=[shape, shape],
        scratch_shapes=[pltpu.VMEM((tm, LANES), F32)] * (LANE_HALVES * 2 * n_g), compiler_params=_params("parallel"),
    )(*outs, *lses)


def _att_stats(datt, att, lse, *, name):
    n_rows = datt.shape[0]

    def fn(d, a, l):
        prod = d * a
        lane = lax.broadcasted_iota(jnp.int32, (d.shape[0], LANES), 1)
        out = jnp.zeros((d.shape[0], LANES), F32)
        for h in range(ATT_HEADS_PER_GROUP):
            lo = h * ATT_HEAD_DIM
            out = jnp.where(lane == h, l[:, lo:lo + 1], out)
            delta = jnp.sum(prod[:, lo:lo + ATT_HEAD_DIM], axis=-1, keepdims=True)
            out = jnp.where(lane == ATT_HEADS_PER_GROUP + h, delta, out)
        return out

    rows = [(t, ATT_MERGED, 0) for t in (datt, att, lse)]
    return _rowcall(fn, rows, [], [(LANES, F32)], n_rows=n_rows, tm=512, name=name)[0]


def _dil_bwd(q, k, v, datt, stats, dil, *, name):
    n_rows = datt.shape[0]
    n_blk = n_rows // dil // ATT_BLK
    span = ATT_BLK * dil
    cur = pl.BlockSpec((ATT_BLK, ATT_MERGED), lambda n, r: (n, r))
    prev = pl.BlockSpec((ATT_BLK, ATT_MERGED), lambda n, r: (jnp.maximum(n - 1, 0), r))
    nxt = pl.BlockSpec((ATT_BLK, ATT_MERGED), lambda n, r: (jnp.minimum(n + 1, n_blk - 1), r))
    seq = lambda half, ahead: pl.BlockSpec((span, LANES), lambda n, r: (jnp.minimum(n + ahead, n_blk - 1), half))

    def body(qc_ref, qn_ref, kp_ref, kc_ref, vp_ref, vc_ref, dc0_ref, dc1_ref, dn0_ref, dn1_ref, sc_ref, sn_ref,
             dq0_ref, dq1_ref, dk0_ref, dk1_ref, dv0_ref, dv1_ref):
        n = pl.program_id(0)
        rows = slice(None) if dil == 1 else _strided_rows(pl.program_id(1), ATT_BLK, dil)

        def read(ref0, ref1):
            return jnp.concatenate([ref0[rows, :], ref1[rows, :]], axis=1)

        def write(ref0, ref1, val):
            ref0[rows, :] = val[:, :LANES]
            ref1[rows, :] = val[:, LANES:]

        masks = _head_masks(ATT_BLK)
        valid = _band_mask(jnp.where(n > 0, 0, ATT_BLK))
        qi = lax.broadcasted_iota(jnp.int32, (HEAD_ROWS, ATT_BLK), 0) & (ATT_BLK - 1)
        ki = lax.broadcasted_iota(jnp.int32, (HEAD_ROWS, ATT_BLK), 1)
        valid_next = (ki - qi) >= jnp.where(n < n_blk - 1, 0, ATT_BLK)

        kc, vc = kc_ref[...], vc_ref[...]
        keys = jnp.concatenate([kp_ref[...], kc], axis=0)
        vals = jnp.concatenate([vp_ref[...], vc], axis=0)
        q4 = _stack_heads(qc_ref[...], masks)
        d4 = _stack_heads(read(dc0_ref, dc1_ref).astype(MXU_DTYPE), masks)
        st = sc_ref[rows, :]
        p = jnp.where(valid, jnp.exp(_dot(q4, keys, 1, 1) * ATT_SCALE - _head_column(st, 0)), 0.0)
        ds = p * (_dot(d4, vals, 1, 1) - _head_column(st, ATT_HEADS_PER_GROUP)) * ATT_SCALE
        write(dq0_ref, dq1_ref, _unstack_heads(_dot(ds, keys, 1, 0), masks))

        q4n = _stack_heads(qn_ref[...], masks)
        d4n = _stack_heads(read(dn0_ref, dn1_ref).astype(MXU_DTYPE), masks)
        stn = sn_ref[rows, :]
        p_n = jnp.where(valid_next, jnp.exp(_dot(q4n, kc, 1, 1) * ATT_SCALE - _head_column(stn, 0)), 0.0)
        ds_n = p_n * (_dot(d4n, vc, 1, 1) - _head_column(stn, ATT_HEADS_PER_GROUP)) * ATT_SCALE
        write(dv0_ref, dv1_ref, _dot(p[:, ATT_BLK:], d4, 0, 0) + _dot(p_n, d4n, 0, 0))
        write(dk0_ref, dk1_ref, _dot(ds[:, ATT_BLK:], q4, 0, 0) + _dot(ds_n, q4n, 0, 0))

    shape = jax.ShapeDtypeStruct((n_rows, LANES), F32)
    out = seq(0, 0)
    res = pl.pallas_call(
        body, name=name, grid=(n_blk, dil),
        in_specs=[cur, nxt, prev, cur, prev, cur, seq(0, 0), seq(1, 0), seq(0, 1), seq(1, 1), seq(0, 0), seq(0, 1)],
        out_specs=[out] * 6, out_shape=[shape] * 6, compiler_params=_params("parallel", "arbitrary"),
    )(q, q, k, k, v, v, datt, datt, datt, datt, stats, stats)
    return [(res[2 * i], res[2 * i + 1]) for i in range(3)]


def _dproj_assemble(du, dqkv, dgs, dga, cos_t, sin_t, *, name):
    n_g = len(DILATIONS)

    def fn(*t):
        n_half = LANE_HALVES * 3 * n_g
        du_t, halves, (dgs_t, dga_t, c, s) = t[0], t[1:1 + n_half], t[1 + n_half:]
        parts = [jnp.concatenate(halves[LANE_HALVES * i:LANE_HALVES * (i + 1)], axis=1) for i in range(3 * n_g)]
        for i in range(2 * n_g):
            parts[i] = _rope_transpose(parts[i], c, s)
        cast = [p.astype(MXU_DTYPE) for p in parts]
        return [jnp.concatenate([du_t] + cast + [dgs_t, dga_t], axis=1)] + [_colsum(p) for p in parts]

    rows = [(du, SSM_WIDTH, 0)]
    rows += [(half, LANES, 0) for i in range(3) for g in range(n_g) for half in dqkv[g][i]]
    rows += [(dgs, D_MODEL, 0), (dga, D_MODEL, 0), (cos_t, ATT_MERGED, 0), (sin_t, ATT_MERGED, 0)]
    width = SSM_WIDTH + 3 * n_g * ATT_MERGED + 2 * D_MODEL
    res = _rowcall(fn, rows, [], [(width, MXU_DTYPE)], [ATT_MERGED] * (3 * n_g), n_rows=du.shape[0], tm=256, name=name)
    return res[0], res[1:]


def _xhead(h):
    return slice(h * XATT_HEAD_DIM, (h + 1) * XATT_HEAD_DIM)


def _xatt_probs(qh, kh):
    s = _dot(qh, kh, 1, 1) * XATT_SCALE
    e = jnp.exp(s - jnp.max(s, axis=-1, keepdims=True))
    return e / jnp.sum(e, axis=-1, keepdims=True)


def _xatt_fwd(q, kv, *, name, tm=512):
    n_rows = q.shape[0]
    n_mem = kv.shape[0]

    def body(q_ref, kv_ref, o_ref):
        for h in range(XATT_HEADS):
            sl = _xhead(h)
            p = _xatt_probs(q_ref[:, sl], kv_ref[:, sl])
            o_ref[:, sl] = _dot(p, kv_ref[:, D_MODEL + h * XATT_HEAD_DIM:D_MODEL + (h + 1) * XATT_HEAD_DIM], 1, 0
                                ).astype(o_ref.dtype)

    row = pl.BlockSpec((tm, D_MODEL), lambda i: (i, 0))
    return pl.pallas_call(
        body, name=name, grid=(n_rows // tm,),
        in_specs=[row, pl.BlockSpec((n_mem, 2 * D_MODEL), lambda i: (0, 0))], out_specs=row,
        out_shape=jax.ShapeDtypeStruct((n_rows, D_MODEL), MXU_DTYPE), compiler_params=_params("parallel"),
    )(q, kv)


def _xatt_bwd(q, kv, do, *, name, tm=512):
    n_rows = q.shape[0]
    n_mem = kv.shape[0]

    def body(q_ref, kv_ref, do_ref, dq_ref, dkv_ref):
        @pl.when(pl.program_id(0) == 0)
        def _():
            dkv_ref[...] = jnp.zeros_like(dkv_ref)

        for h in range(XATT_HEADS):
            sl = _xhead(h)
            vsl = slice(D_MODEL + h * XATT_HEAD_DIM, D_MODEL + (h + 1) * XATT_HEAD_DIM)
            qh, kh, doh = q_ref[:, sl], kv_ref[:, sl], do_ref[:, sl]
            p = _xatt_probs(qh, kh)
            dp = _dot(doh, kv_ref[:, vsl], 1, 1)
            ds = p * (dp - jnp.sum(dp * p, axis=-1, keepdims=True)) * XATT_SCALE
            dq_ref[:, sl] = _dot(ds, kh, 1, 0).astype(dq_ref.dtype)
            dkv_ref[:, sl] += _dot(ds, qh, 0, 0)
            dkv_ref[:, vsl] += _dot(p, doh, 0, 0)

    row = pl.BlockSpec((tm, D_MODEL), lambda i: (i, 0))
    full = pl.BlockSpec((n_mem, 2 * D_MODEL), lambda i: (0, 0))
    return pl.pallas_call(
        body, name=name, grid=(n_rows // tm,), in_specs=[row, full, row], out_specs=[row, full],
        out_shape=[jax.ShapeDtypeStruct((n_rows, D_MODEL), MXU_DTYPE), jax.ShapeDtypeStruct((n_mem, 2 * D_MODEL), F32)],
        compiler_params=_params("arbitrary"),
    )(q, kv, do)


def _disc(logdt, a_re, a_im, b_re, b_im):
    dt = jnp.exp(logdt)
    mag = jnp.exp(a_re * dt)
    ab_re = mag * jnp.cos(a_im * dt)
    ab_im = mag * jnp.sin(a_im * dt)
    den = jnp.square(a_re) + jnp.square(a_im)
    nr = ab_re - 1.0
    f_re = (nr * a_re + ab_im * a_im) / den
    f_im = (ab_im * a_re - nr * a_im) / den
    bb_re = f_re[None] * b_re - f_im[None] * b_im
    bb_im = f_re[None] * b_im + f_im[None] * b_re
    return ab_re, ab_im, bb_re, bb_im


def _disc_transpose(logdt, a_re, a_im, b_re, b_im, g_ab_re, g_ab_im, g_bb_re, g_bb_im):
    dt = jnp.exp(logdt)
    mag = jnp.exp(a_re * dt)
    th = a_im * dt
    cs, sn = jnp.cos(th), jnp.sin(th)
    ab_re, ab_im = mag * cs, mag * sn
    den = jnp.square(a_re) + jnp.square(a_im)
    nr = ab_re - 1.0
    f_re = (nr * a_re + ab_im * a_im) / den
    f_im = (ab_im * a_re - nr * a_im) / den
    d_f_re = jnp.sum(g_bb_re * b_re + g_bb_im * b_im, axis=0)
    d_f_im = jnp.sum(g_bb_im * b_re - g_bb_re * b_im, axis=0)
    d_b_re = g_bb_re * f_re[None] + g_bb_im * f_im[None]
    d_b_im = g_bb_im * f_re[None] - g_bb_re * f_im[None]
    d_n_re, d_n_im = d_f_re / den, d_f_im / den
    d_den = -(d_f_re * f_re + d_f_im * f_im) / den
    d_ab_re = g_ab_re + d_n_re * a_re - d_n_im * a_im
    d_ab_im = g_ab_im + d_n_re * a_im + d_n_im * a_re
    d_a_re = d_n_re * nr + d_n_im * ab_im + 2.0 * d_den * a_re
    d_a_im = d_n_re * ab_im - d_n_im * nr + 2.0 * d_den * a_im
    d_mag = d_ab_re * cs + d_ab_im * sn
    d_th = mag * (d_ab_im * cs - d_ab_re * sn)
    d_a_re = d_a_re + d_mag * mag * dt
    d_a_im = d_a_im + d_th * dt
    d_dt = jnp.sum(d_mag * mag * a_re + d_th * a_im, axis=-1, keepdims=True)
    return d_dt * dt, d_a_re, d_a_im, d_b_re, d_b_im


def _full_spec(shape):
    return pl.BlockSpec(tuple(shape), functools.partial(lambda i, nd: (0,) * nd, nd=len(shape)))


def _whole(fn, args, out_shapes, *, name):
    n_in = len(args)

    def body(*refs):
        res = fn(*[r[...] for r in refs[:n_in]])
        for o_ref, val in zip(refs[n_in:], res):
            o_ref[...] = val

    return pl.pallas_call(
        body, name=name, grid=(1,), in_specs=[_full_spec(t.shape) for t in args],
        out_specs=[_full_spec(s) for s in out_shapes], out_shape=[jax.ShapeDtypeStruct(s, F32) for s in out_shapes],
        compiler_params=_params("arbitrary"))(*args)


SSM_WIDE =GROUPS_PER_TILE * SSM_STATE
LANE_GROUPS_PER_TILE = SSM_WIDE // LANES


def _chan(j):
    return slice(j * LANES, (j + 1) * LANES)


def _time_major_rows(j, q, tc):
    return pl.ds(j * LANE_GROUPS_PER_TILE + q, tc, stride=STATE_VREG_ROWS)


def _to_time_major(x, t_re_ref, t_im_ref, dst_re, dst_im, tc):
    for j in range(SSM_TILES):
        xj = x[:, _chan(j)]
        for t_ref, dst in ((t_re_ref, dst_re), (t_im_ref, dst_im)):
            r = _dot(xj, t_ref[j], 1, 0)
            for q in range(LANE_GROUPS_PER_TILE):
                dst[_time_major_rows(j, q, tc), :] = r[:, q * LANES:(q + 1) * LANES]


def _from_time_major(src, j, tc):
    return jnp.concatenate([src[_time_major_rows(j, q, tc), :] for q in range(LANE_GROUPS_PER_TILE)], axis=1)


def _scan_chunk(w_re, w_im, h_re, h_im, a_re, a_im, start, tc):
    def step(t, carry):
        hr, hi = carry
        rows = _scan_rows(t)
        nr = a_re * hr - a_im * hi + w_re[rows, :]
        ni = a_re * hi + a_im * hr + w_im[rows, :]
        h_re[rows, :] = nr
        h_im[rows, :] = ni
        return nr, ni

    return lax.fori_loop(0, tc, step, start, unroll=8)


SSM_CHUNK = 256


def _tile_spec(stack, k):
    return pl.BlockSpec((pl.Squeezed(),) + tuple(stack.shape[1:]), lambda i: (k, 0, 0, 0))


def _expand_block_diagonal(src_ref, dst):
    dst[...] = jnp.zeros_like(dst)
    r, c = src_ref.shape[1:]
    for g in range(SSM_GROUPS):
        j, gl = divmod(g, GROUPS_PER_TILE)
        dst[j, gl * r:(gl + 1) * r, gl * c:(gl + 1) * c] = src_ref[g].astype(dst.dtype)


def _extract_block_diagonal(src, dst_ref):
    r, c = dst_ref.shape[1:]
    for g in range(SSM_GROUPS):
        j, gl = divmod(g, GROUPS_PER_TILE)
        dst_ref[g] = src[j, gl * r:(gl + 1) * r, gl * c:(gl + 1) * c]


def _ssm_fwd(proj, blocks_cn, blocks_nc, a_re, a_im, gain, *, name, tc=SSM_CHUNK):
    n_rows = proj.shape[0]
    n_chunk = n_rows // tc

    def body(u_ref, br_ref, bi_ref, cr_ref, ci_ref, ar_ref, ai_ref, g_ref, y_ref, gy_ref, hr, hi, wr, wi, state,
             tbr_ref, tbi_ref, tcr_ref, tci_ref):
        @pl.when(pl.program_id(0) == 0)
        def _():
            state[...] = jnp.zeros_like(state)
            for src_ref, dst in ((br_ref, tbr_ref), (bi_ref, tbi_ref), (cr_ref, tcr_ref), (ci_ref, tci_ref)):
                _expand_block_diagonal(src_ref, dst)

        u = u_ref[...]
        _to_time_major(u, tbr_ref, tbi_ref, wr, wi, tc)
        state[0], state[1] = _scan_chunk(wr, wi, hr, hi, ar_ref[...], ai_ref[...], (state[0], state[1]), tc)
        for j in range(SSM_TILES):
            yj = (_dot(_from_time_major(hr, j, tc), tcr_ref[j], 1, 0) + _dot(_from_time_major(hi, j, tc), tci_ref[j], 1, 0)
                  + g_ref[:, _chan(j)] * u[:, _chan(j)])
            y_ref[:, _chan(j)] = yj
            gy_ref[:, _chan(j)] = jax.nn.gelu(yj).astype(gy_ref.dtype)

    rows = pl.BlockSpec((tc, SSM_WIDTH), lambda i: (i, 0))
    coef = pl.BlockSpec((STATE_VREG_ROWS, LANES), lambda i: (0, 0))
    states = pl.BlockSpec((tc * STATE_VREG_ROWS, LANES), lambda i: (i, 0))
    sshape = jax.ShapeDtypeStruct((n_rows * STATE_VREG_ROWS, LANES), F32)
    return pl.pallas_call(
        body, name=name, grid=(n_chunk,),
        in_specs=[rows, _tile_spec(blocks_cn, 0), _tile_spec(blocks_cn, 1), _tile_spec(blocks_nc, 0),
                  _tile_spec(blocks_nc, 1), coef, coef, pl.BlockSpec((1, SSM_WIDTH), lambda i: (0, 0))],
        out_specs=[rows, rows, states, states],
        out_shape=[jax.ShapeDtypeStruct((n_rows, SSM_WIDTH), F32), jax.ShapeDtypeStruct((n_rows, SSM_WIDTH), MXU_DTYPE),
                   sshape, sshape],
        scratch_shapes=[pltpu.VMEM((tc * STATE_VREG_ROWS, LANES), F32)] * 2 + [pltpu.VMEM((2, STATE_VREG_ROWS, LANES), F32)]
        + [pltpu.VMEM((SSM_TILES, LANES, SSM_WIDE), MXU_DTYPE)] * 2 + [pltpu.VMEM((SSM_TILES, SSM_WIDE, LANES), MXU_DTYPE)] * 2,
        compiler_params=_params("arbitrary"),
    )(proj, blocks_cn, blocks_cn, blocks_nc, blocks_nc, a_re, a_im, gain)


def _ssm_bwd(proj, dy, h_re, h_im, blocks_cn, blocks_nc, a_re, a_im, gain, *, name, tc=SSM_CHUNK):
    n_rows = proj.shape[0]
    n_chunk = n_rows // tc

    def body(u_ref, dy_ref, hr, hi, cr_ref, ci_ref, br_ref, bi_ref, ar_ref, ai_ref, g_ref,
             du_ref, su_ref, dc_re_ref, dc_im_ref, db_re_ref, db_im_ref, dar_ref, dai_ref, wr, wi, carry,
             tdr_ref, tdi_ref, tur_ref, tui_ref, dcr_ref, dci_ref, dbr_ref, dbi_ref):
        @pl.when(pl.program_id(0) == 0)
        def _():
            carry[...] = jnp.zeros_like(carry)
            for acc_ref in (su_ref, dcr_ref, dci_ref, dbr_ref, dbi_ref):
                acc_ref[...] = jnp.zeros_like(acc_ref)
            for src_ref, dst in ((cr_ref, tdr_ref), (ci_ref, tdi_ref), (br_ref, tur_ref), (bi_ref, tui_ref)):
                _expand_block_diagonal(src_ref, dst)

        a_r, a_i = ar_ref[...], ai_ref[...]
        u, dyv = u_ref[...], dy_ref[...]
        _to_time_major(dyv, tdr_ref, tdi_ref, wr, wi, tc)

        def step(kk, c):
            lam_r, lam_i, dar, dai = c
            rows = _scan_rows(tc - 1 - kk)
            h_r, h_i = hr[rows, :], hi[rows, :]
            dar = dar + lam_r * h_r + lam_i * h_i
            dai = dai + lam_i * h_r - lam_r * h_i
            new_r = wr[rows, :] + a_r * lam_r + a_i * lam_i
            new_i = wi[rows, :] + a_r * lam_i - a_i * lam_r
            wr[rows, :] = new_r
            wi[rows, :] = new_i
            return new_r, new_i, dar, dai

        carry[0], carry[1], carry[2], carry[3] = lax.fori_loop(0, tc, step, (carry[0], carry[1], carry[2], carry[3]),
                                                              unroll=8)
        dar_ref[...] = carry[2]
        dai_ref[...] = carry[3]
        for j in range(SSM_TILES):
            cj = _chan(j)
            lam_r, lam_i = _from_time_major(wr, j, tc), _from_time_major(wi, j, tc)
            dcr_ref[j] += _dot(dyv[:, cj], _from_time_major(hr, j, tc), 0, 0)
            dci_ref[j] += _dot(dyv[:, cj], _from_time_major(hi, j, tc), 0, 0)
            dbr_ref[j] += _dot(u[:, cj], lam_r, 0, 0)
            dbi_ref[j] += _dot(u[:, cj], lam_i, 0, 0)
            duj = _dot(lam_r, tur_ref[j], 1, 0) + _dot(lam_i, tui_ref[j], 1, 0) + g_ref[:, cj] * dyv[:, cj]
            du_ref[:, cj] = duj.astype(du_ref.dtype)
            su_ref[:, cj] += _colsum(duj)

        @pl.when(pl.program_id(0) == n_chunk - 1)
        def _():
            for src, dst_ref in ((dcr_ref, dc_re_ref), (dci_ref, dc_im_ref), (dbr_ref, db_re_ref), (dbi_ref, db_im_ref)):
                _extract_block_diagonal(src, dst_ref)

    back = lambda i: (n_chunk - 1 - i, 0)
    rows = pl.BlockSpec((tc, SSM_WIDTH), back)
    blocks = pl.BlockSpec((SSM_GROUPS, SSM_GROUP, SSM_STATE), lambda i: (0, 0, 0))
    coef = pl.BlockSpec((STATE_VREG_ROWS, LANES), lambda i: (0, 0))
    states = pl.BlockSpec((tc * STATE_VREG_ROWS, LANES), back)
    vec = pl.BlockSpec((1, SSM_WIDTH), lambda i: (0, 0))
    bshape = jax.ShapeDtypeStruct((SSM_GROUPS, SSM_GROUP, SSM_STATE), F32)
    cshape = jax.ShapeDtypeStruct((STATE_VREG_ROWS, LANES), F32)
    return pl.pallas_call(
        body, name=name, grid=(n_chunk,),
        in_specs=[rows, rows, states, states, _tile_spec(blocks_cn, 2), _tile_spec(blocks_cn, 3), _tile_spec(blocks_nc, 2),
                  _tile_spec(blocks_nc, 3), coef, coef, vec],
        out_specs=[rows, vec, blocks, blocks, blocks, blocks, coef, coef],
        out_shape=[jax.ShapeDtypeStruct((n_rows, SSM_WIDTH), MXU_DTYPE), jax.ShapeDtypeStruct((1, SSM_WIDTH), F32),
                   bshape, bshape, bshape, bshape, cshape, cshape],
        scratch_shapes=[pltpu.VMEM((tc * STATE_VREG_ROWS, LANES), F32)] * 2 + [pltpu.VMEM((4, STATE_VREG_ROWS, LANES), F32)]
        + [pltpu.VMEM((SSM_TILES, LANES, SSM_WIDE), MXU_DTYPE)] * 2 + [pltpu.VMEM((SSM_TILES, SSM_WIDE, LANES), MXU_DTYPE)] * 2
        + [pltpu.VMEM((SSM_TILES, LANES, SSM_WIDE), F32)] * 4,
        compiler_params=_params("arbitrary"),
    )(proj, dy, h_re, h_im, blocks_cn, blocks_cn, blocks_nc, blocks_nc, a_re, a_im, gain)


def _scan_rows(t):
    return pl.ds(pl.multiple_of(t * STATE_VREG_ROWS, 8), STATE_VREG_ROWS)


GATHER_GROUPS = (("w_glu", "w_att_up", "w_mix_out"), ("w_xq", "w_xkv", "w_xo", "w_ff1", "w_ff2"))
SCATTER_GROUPS = (("w_ff2", "w_ff1"), ("w_xo", "w_xq", "w_xkv", "w_mix_out"), ("w_att_up", "w_glu"), ("w_in",))


def _local_grads(x, mem, pos_col, target, sm, fetch_in, fetch, send, send_small, start_token):
    b_re_t = sm["ssm_b_re"].transpose(2, 0, 1)
    b_im_t = sm["ssm_b_im"].transpose(2, 0, 1)
    logdt = sm["ssm_log_dt"].reshape(SSM_GROUPS, 1)
    c_re, c_im = sm["ssm_c_re"], sm["ssm_c_im"]
    grp = (SSM_GROUPS, SSM_STATE)
    chn = (SSM_GROUP, SSM_GROUPS, SSM_STATE)

    wts = {}
    cos_t, sin_t = _rope_tables(pos_col, after=start_token, name="rope_tables")
    h0, xh0, rs0, h0m = _ln_fwd(x, None, sm["ln_in_g"], sm["ln_in_b"], alpha=1.0, name="ln_in_fwd")
    disc_in = (logdt, sm["ssm_a_re"], sm["ssm_a_im"], b_re_t, b_im_t)
    ab_re, ab_im, bb_re_t, bb_im_t = _whole(_disc, disc_in, [grp, grp, chn, chn], name="ssm_disc")
    a_re_rows, a_im_rows = ab_re.reshape(STATE_VREG_ROWS, LANES), ab_im.reshape(STATE_VREG_ROWS, LANES)
    tiles_cn = jnp.stack([bb_re_t.transpose(1, 0, 2), bb_im_t.transpose(1, 0, 2), c_re, -c_im])
    tiles_nc = jnp.stack([c_re.transpose(0, 2, 1), -c_im.transpose(0, 2, 1), bb_re_t.transpose(1, 2, 0),
                          bb_im_t.transpose(1, 2, 0)])
    w_in_near, near_ids = fetch_in(0, [h0m, tiles_cn, tiles_nc])
    proj = _mm_shards(h0m, w_in_near, sm["b_in"], near_ids, name="in_proj_near")
    wts["w_in"], far_ids = fetch_in(1, [proj])
    proj = _mm_shards(h0m, wts["w_in"], sm["b_in"], far_ids, prev=proj, name="in_proj_far")

    y, gy, h_re, h_im = _ssm_fwd(proj, tiles_cn, tiles_nc, a_re_rows, a_im_rows, sm["ssm_d"], name="ssm_fwd")

    q, k, v = _qkv_split(proj, cos_t, sin_t, name="qkv_split")
    outs, lses = [], []
    for g, dil in enumerate(DILATIONS):
        o_g, l_g = _dil_fwd(q[g], k[g], v[g], dil, name=f"dil_att_fwd_{dil}")
        outs.append(o_g)
        lses.append(l_g)
    att, lse = _att_merge(outs, lses, name="att_merge")
    wts.update(fetch(0, [att]))
    z = _mm(gy, wts["w_glu"], bias=sm["b_glu"], b_shards=True, name="glu_proj")
    b_att = _mm(att, wts["w_att_up"], b_shards=True, name="att_up")

    mixed = _mix_fwd(proj, z, b_att, name="gate_mix")
    mix_out = _mm(mixed, wts["w_mix_out"], bias=sm["b_mix_out"], name="mix_out")
    h1, xh1, rs1, h1m = _ln_fwd(h0, mix_out, sm["ln1_g"], sm["ln1_b"], alpha=DEEPNORM_ALPHA, name="ln1_fwd")

    wts.update(fetch(1, [h1m]))
    xq = _mm(h1m, wts["w_xq"], out_dtype=MXU_DTYPE, name="xatt_q")
    kv = _mm(mem, wts["w_xkv"], out_dtype=MXU_DTYPE, b_shards=True, name="xatt_kv")
    xo_in = _xatt_fwd(xq, kv, name="xatt_fwd")
    xo = _mm(xo_in, wts["w_xo"], name="xatt_o")
    h2, xh2, rs2, h2m = _ln_fwd(h1, xo, sm["ln2_g"], sm["ln2_b"], alpha=DEEPNORM_ALPHA, name="ln2_fwd")

    pre, act = _mm(h2m, wts["w_ff1"], bias=sm["b_ff1"], b_shards=True, name="ff1",
                   also=(lambda r: jnp.square(jnp.maximum(r, 0.0)), MXU_DTYPE))
    ff = _mm(act, wts["w_ff2"], bias=sm["b_ff2"], name="ff2")

    gw, gs = {}, {}
    dr3, dr3m, gs["ln3_g"], gs["ln3_b"], gs["b_ff2"], loss_row = _ln_loss_bwd(
        h2, ff, target, sm["ln3_g"], sm["ln3_b"], alpha=DEEPNORM_ALPHA, name="ln3_loss")
    wgrad = functools.partial(_mm, ta=True, out_dtype=WIRE_DTYPE, tk=2048)
    gw["w_ff2"] = wgrad(act, dr3m, tk=1024, name="ff2_dw")
    dpre, gs["b_ff1"] = _mm(dr3m, wts["w_ff2"], tb=True, out_dtype=MXU_DTYPE, colsum=True, name="ff2_dx",
                            gate=(pre, lambda p: 2.0 * jnp.maximum(p, 0.0)))
    gw["w_ff1"] = wgrad(h2m, dpre, out_shards=True, name="ff1_dw")
    sent = send(0, gw)
    dh2 = _mm(dpre, wts["w_ff1"], tb=True, b_shards=True, after=sent, name="ff1_dx")

    dr2, dr2m, gs["ln2_g"], gs["ln2_b"], _ = _ln_bwd(dr3, dh2, xh2, rs2, sm["ln2_g"], alpha=DEEPNORM_ALPHA,
                                                     name="ln2_bwd")
    gw["w_xo"] = wgrad(xo_in, dr2m, name="xatt_o_dw")
    dxo_in = _mm(dr2m, wts["w_xo"], tb=True, out_dtype=MXU_DTYPE, name="xatt_o_dx")
    dxq, dkv = _xatt_bwd(xq, kv, dxo_in, name="xatt_bwd")
    gw["w_xq"] = wgrad(h1m, dxq, name="xatt_q_dw")
    gw["w_xkv"] = wgrad(mem, dkv, out_shards=True, name="xatt_kv_dw")
    dh1 = _mm(dxq, wts["w_xq"], tb=True, name="xatt_q_dx")

    dr1, dr1m, gs["ln1_g"], gs["ln1_b"], gs["b_mix_out"] = _ln_bwd(dr2, dh1, xh1, rs1, sm["ln1_g"],
                                                                   alpha=DEEPNORM_ALPHA, name="ln1_bwd")
    gw["w_mix_out"] = wgrad(mixed, dr1m, name="mix_out_dw")
    sent = send(1, gw)
    dmixed = _mm(dr1m, wts["w_mix_out"], tb=True, after=sent, name="mix_out_dx")
    dgs, dga, dz, db_att, s_gs, s_ga, gs["b_glu"] = _mix_bwd(dmixed, proj, z, b_att, name="gate_mix_bwd")

    gw["w_att_up"] = wgrad(att, db_att, out_shards=True, name="att_up_dw")
    gw["w_glu"] = wgrad(gy, dz, out_shards=True, name="glu_dw")
    sent = send(2, gw)
    datt = _mm(db_att, wts["w_att_up"], tb=True, b_shards=True, after=sent, name="att_up_dx")
    stats = _att_stats(datt, att, lse, name="att_stats")
    dqkv = [_dil_bwd(q[g], k[g], v[g], datt, stats, dil, name=f"dil_att_bwd_{dil}") for g, dil in enumerate(DILATIONS)]

    dgy = _mm(dz, wts["w_glu"], tb=True, b_shards=True, name="glu_dx")
    dy, gs["ssm_d"] = _gelu_bwd(dgy, y, proj, name="gelu_bwd")
    du, s_u, dc_re_t, dc_im_t, dbb_re_t, dbb_im_t, da_re, da_im = _ssm_bwd(
        proj, dy, h_re, h_im, tiles_cn, tiles_nc, a_re_rows, a_im_rows, sm["ssm_d"], name="ssm_bwd")
    gs["ssm_c_re"], gs["ssm_c_im"] = dc_re_t, -dc_im_t
    disc_ct = (da_re.reshape(grp), da_im.reshape(grp), dbb_re_t.transpose(1, 0, 2), dbb_im_t.transpose(1, 0, 2))
    d_logdt, gs["ssm_a_re"], gs["ssm_a_im"], d_b_re_t, d_b_im_t = _whole(
        _disc_transpose, disc_in + disc_ct, [(SSM_GROUPS, 1), grp, grp, chn, chn], name="ssm_disc_bwd")
    gs["ssm_log_dt"] = d_logdt
    gs["ssm_b_re"], gs["ssm_b_im"] = d_b_re_t.transpose(1, 2, 0), d_b_im_t.transpose(1, 2, 0)

    dproj, s_qkv = _dproj_assemble(du, dqkv, dgs, dga, cos_t, sin_t, name="dproj_assemble")
    gs["b_in"] = jnp.concatenate([s_u, *s_qkv, s_gs, s_ga], axis=1)
    sent = send_small(gs, SMALL_EARLY)
    gw["w_in"] = wgrad(h0m, dproj, out_shards=True, after=sent, name="in_proj_dw")
    sent = send(3, gw)
    dh0 = _mm(dproj, wts["w_in"], tb=True, b_shards=True, after=sent, name="in_proj_dx")
    grad_x, gs["ln_in_g"], gs["ln_in_b"], _ = _ln_bwd(dr1, dh0, xh0, rs0, sm["ln_in_g"], alpha=DEEPNORM_ALPHA,
                                                      operand=False, name="ln_in_bwd")
    return loss_row, grad_x, gs


N_PEER = N_DEV - 1
_IN_HBM = pl.BlockSpec(memory_space=pltpu.HBM)
_IN_SEMAPHORE = pl.BlockSpec(memory_space=pltpu.SEMAPHORE)


def _device_index():
    return 4 * lax.axis_index("x") + 2 * lax.axis_index("y") + lax.axis_index("c")


ALL_PEERS = tuple(range(1, N_DEV))
NEAR_PEERS = (1, 2, 3, 4, 5)
FAR_PEERS = (6, 7)


def _peer_index(kk):
    x, y, c = lax.axis_index("x"), lax.axis_index("y"), lax.axis_index("c")
    return 4 * ((x + (kk >> 2)) % 2) + 2 * ((y + ((kk >> 1) & 1)) % 2) + (c + (kk & 1)) % 2


def _exchange_copies(src_refs, land_refs, send_sems, recv_sems, scatter, peers):
    x, y, c = lax.axis_index("x"), lax.axis_index("y"), lax.axis_index("c")
    me = 4 * x + 2 * y + c
    pairs = []
    for a, (src_ref, land_ref) in enumerate(zip(src_refs, land_refs)):
        for idx, kk in enumerate(peers):
            px = (x + (kk >> 2)) % 2
            py = (y + ((kk >> 1) & 1)) % 2
            pc = (c + (kk & 1)) % 2
            peer = 4 * px + 2 * py + pc
            sem = a * len(peers) + idx
            src = src_ref.at[peer] if scatter else src_ref

            def copy(dst, src=src, sem=sem, px=px, py=py, pc=pc):
                return pltpu.make_async_remote_copy(
                    src_ref=src, dst_ref=dst, send_sem=send_sems.at[sem], recv_sem=recv_sems.at[sem],
                    device_id=(px, py, pc), device_id_type=pl.DeviceIdType.MESH)

            pairs.append((functools.partial(copy, land_ref.at[me]), functools.partial(copy, land_ref.at[peer])))
    return pairs


def _own_copies(src_refs, land_refs, own_sems, scatter):
    me = _device_index()
    return [functools.partial(pltpu.make_async_copy, src_ref.at[me] if scatter else src_ref, land_ref.at[me],
                              own_sems.at[a]) for a, (src_ref, land_ref) in enumerate(zip(src_refs, land_refs))]


def _exchange_start(srcs, *, scatter, name, after=None, peers=ALL_PEERS, lands=None):
    n_arr, n_sem = len(srcs), len(srcs) * len(peers)
    own = lands is None
    if own:
        lands = [lax.empty((N_DEV,) + tuple(s.shape[1:] if scatter else s.shape), s.dtype) for s in srcs]
    n_in = 2 * n_arr + (after is not None)

    def body(*refs):
        send_sems, recv_sems = refs[n_in], refs[n_in + 1]
        for sent, _ in _exchange_copies(refs[:n_arr], refs[n_arr:2 * n_arr], send_sems, recv_sems, scatter, peers):
            sent().start()
        if own:
            for local in _own_copies(refs[:n_arr], refs[n_arr:2 * n_arr], refs[n_in + 2], scatter):
                local().start()
        refs[-1][...] = jnp.zeros_like(refs[-1])

    sems = [pltpu.SemaphoreType.DMA((n_sem,)), pltpu.SemaphoreType.DMA((n_sem,))] + [pltpu.SemaphoreType.DMA((n_arr,))] * own
    through = [pltpu.HBM(t.shape, t.dtype) for t in (*srcs, *lands)]
    res = pl.pallas_call(
        body, name=name, out_shape=(*sems, *through, jax.ShapeDtypeStruct((8, LANES), F32)),
        in_specs=[_IN_HBM] * (2 * n_arr) + [pl.BlockSpec(memory_space=pl.ANY)] * (after is not None),
        out_specs=(*[_IN_SEMAPHORE] * len(sems), *[_IN_HBM] * (2 * n_arr), pl.BlockSpec(memory_space=pltpu.VMEM)),
        input_output_aliases={i: len(sems) + i for i in range(2 * n_arr)},
        compiler_params=pltpu.CompilerParams(has_side_effects=pltpu.SideEffectType.DATAFLOW_SIDE_EFFECTING),
    )(*[pltpu.with_memory_space_constraint(t, pltpu.HBM) for t in (*srcs, *lands)],
      *([after] if after is not None else []))
    first = len(sems)
    handle = dict(sems=res[:first], srcs=res[first:first + n_arr], lands=res[first + n_arr:first + 2 * n_arr],
                  scatter=scatter, peers=peers, own=own)
    return handle, res[-1]


def _exchange_wait(handle, *, after, name, srcs=None, lands=None):
    srcs = handle["srcs"] if srcs is None else srcs
    lands = handle["lands"] if lands is None else lands
    sems, scatter, peers, own = handle["sems"], handle["scatter"], handle["peers"], handle["own"]
    n_arr = len(srcs)
    after = list(after)

    def body(*refs):
        src_refs, land_refs = refs[:n_arr], refs[n_arr:2 * n_arr]
        for sent, received in _exchange_copies(src_refs, land_refs, refs[2 * n_arr], refs[2 * n_arr + 1], scatter, peers):
            sent().wait_send()
            received().wait_recv()
        if own:
            for local in _own_copies(src_refs, land_refs, refs[2 * n_arr + 2], scatter):
                local().wait()

    res = pl.pallas_call(
        body, name=name, out_shape=tuple(pltpu.HBM(t.shape, t.dtype) for t in (*srcs, *lands)),
        in_specs=[_IN_HBM] * (2 * n_arr) + [_IN_SEMAPHORE] * len(sems) + [pl.BlockSpec(memory_space=pl.ANY)] * len(after),
        out_specs=tuple([_IN_HBM] * (2 * n_arr)), input_output_aliases={i: i for i in range(2 * n_arr)},
        compiler_params=pltpu.CompilerParams(has_side_effects=pltpu.SideEffectType.DATAFLOW_SIDE_EFFECTING),
    )(*srcs, *lands, *sems, *after)
    return res[:n_arr], res[n_arr:]


def _adamw(g, w, m, v):
    m_new = ADAM_B1 * m + (1.0 - ADAM_B1) * g
    v_new = ADAM_B2 * v + (1.0 - ADAM_B2) * jnp.square(g)
    m_hat = m_new / (1.0 - ADAM_B1 ** ADAM_STEP)
    v_hat = v_new / (1.0 - ADAM_B2 ** ADAM_STEP)
    return g, -ADAM_LR * (m_hat / (jnp.sqrt(v_hat) + ADAM_EPS) + ADAM_WD * w), m_new, v_new


def _reduce_adamw(gstack, w, m, v, *, name, tr=128):
    n_rows, cols = w.shape
    tr = min(tr, n_rows)
    assert n_rows % tr == 0, (name, n_rows, tr)

    def body(g_ref, w_ref, m_ref, v_ref, *out_refs):
        g = g_ref[0].astype(F32)
        for dev in range(1, N_DEV):
            g = g + g_ref[dev].astype(F32)
        for o_ref, val in zip(out_refs, _adamw(g, w_ref[...], m_ref[...], v_ref[...])):
            o_ref[...] = val

    flat = pl.BlockSpec((tr, cols), lambda i: (i, 0))
    shape = jax.ShapeDtypeStruct((n_rows, cols), F32)
    return pl.pallas_call(
        body, name=name, grid=(n_rows // tr,),
        in_specs=[pl.BlockSpec((N_DEV, tr, cols), lambda i: (0, i, 0)), flat, flat, flat],
        out_specs=[flat] * 4, out_shape=[shape] * 4, compiler_params=_params("parallel"),
    )(*_in_hbm([gstack, w, m, v]))


SMALL_FLAT_SSM = ("ssm_b_re", "ssm_b_im", "ssm_c_re", "ssm_c_im")


def _small_view(name, shape):
    size = int(np.prod(shape))
    if name in SMALL_FLAT_SSM:
        return SSM_GROUPS, size // SSM_GROUPS
    if name in ("ssm_a_re", "ssm_a_im"):
        return SSM_GROUPS, SSM_STATE
    return 1, size


def _pack_rows(view):
    return -(-(view[0] * view[1]) // PACK_COLS)


SMALL_LATE = ("ln_in_g", "ln_in_b")
SMALL_EARLY = tuple(n for n in SMALL if n not in SMALL_LATE)


def _pack_small(gs, names, views):
    parts = []
    for n in names:
        flat = gs[n].reshape(-1).astype(WIRE_DTYPE)
        parts.append(jnp.pad(flat, (0, _pack_rows(views[n]) * PACK_COLS - flat.shape[0])))
    total = sum(p.shape[0] for p in parts) // PACK_COLS
    parts.append(jnp.zeros(((-total % PACK_ROW_ALIGN) * PACK_COLS,), WIRE_DTYPE))
    return jnp.concatenate(parts).reshape(-1, PACK_COLS)


def _small_pieces(view):
    rows, cols = view
    if cols == PACK_COLS:
        return [(0, rows, 0, 0, 0, cols)]
    if rows == 1 and cols > PACK_COLS:
        return [(kk, 1, 0, 0, kk * PACK_COLS, PACK_COLS) for kk in range(cols // PACK_COLS)]
    if rows == 1:
        return [(0, 1, 0, 0, 0, cols)]
    return [((r * cols) // PACK_COLS, 1, (r * cols) % PACK_COLS, r, 0, cols) for r in range(rows)]


def _adamw_small(stacks, views, w, m, v, *, name):
    n = len(SMALL)
    place, first = {}, [0, 0]
    for k, names in enumerate((SMALL_EARLY, SMALL_LATE)):
        for name_ in names:
            place[name_] = (k, first[k])
            first[k] += _pack_rows(views[name_])

    def body(early_ref, late_ref, *refs):
        ins, outs = refs[:3 * n], refs[3 * n:]
        for i, name_ in enumerate(SMALL):
            stack_ref = (early_ref, late_ref)[place[name_][0]]
            row0 = place[name_][1]
            for prow, nrows, lane, orow, ocol, width in _small_pieces(views[name_]):
                src = (slice(row0 + prow, row0 + prow + nrows), slice(lane, lane + width))
                dst = (slice(orow, orow + nrows), slice(ocol, ocol + width))
                g = stack_ref[(0,) + src].astype(F32)
                for dev in range(1, N_DEV):
                    g = g + stack_ref[(dev,) + src].astype(F32)
                res = _adamw(g, ins[i][dst], ins[n + i][dst], ins[2 * n + i][dst])
                for kk, val in enumerate(res):
                    outs[kk * n + i][dst] = val

    args = [*stacks, *[d[name_] for d in (w, m, v) for name_ in SMALL]]
    out_views = [views[name_] for _ in range(4) for name_ in SMALL]
    res = pl.pallas_call(
        body, name=name, grid=(1,), in_specs=[_full_spec(t.shape) for t in args],
        out_specs=[_full_spec(s) for s in out_views], out_shape=[jax.ShapeDtypeStruct(s, F32) for s in out_views],
        compiler_params=_params("arbitrary"),
    )(*args)
    return [dict(zip(SMALL, res[kk * n:(kk + 1) * n])) for kk in range(4)]


def kernel(x, mem, positions, ln_in_g, ln_in_b, w_in, b_in, ssm_log_dt, ssm_a_re, ssm_a_im, ssm_b_re, ssm_b_im, ssm_c_re, ssm_c_im, ssm_d, w_glu, b_glu, w_att_up, w_mix_out, b_mix_out, ln1_g, ln1_b, w_xq, w_xkv, w_xo, ln2_g, ln2_b, w_ff1, b_ff1, w_ff2, b_ff2, ln3_g, ln3_b, loss_target, m_ln_in_g, m_ln_in_b, m_w_in, m_b_in, m_ssm_log_dt, m_ssm_a_re, m_ssm_a_im, m_ssm_b_re, m_ssm_b_im, m_ssm_c_re, m_ssm_c_im, m_ssm_d, m_w_glu, m_b_glu, m_w_att_up, m_w_mix_out, m_b_mix_out, m_ln1_g, m_ln1_b, m_w_xq, m_w_xkv, m_w_xo, m_ln2_g, m_ln2_b, m_w_ff1, m_b_ff1, m_w_ff2, m_b_ff2, m_ln3_g, m_ln3_b, v_ln_in_g, v_ln_in_b, v_w_in, v_b_in, v_ssm_log_dt, v_ssm_a_re, v_ssm_a_im, v_ssm_b_re, v_ssm_b_im, v_ssm_c_re, v_ssm_c_im, v_ssm_d, v_w_glu, v_b_glu, v_w_att_up, v_w_mix_out, v_b_mix_out, v_ln1_g, v_ln1_b, v_w_xq, v_w_xkv, v_w_xo, v_ln2_g, v_ln2_b, v_w_ff1, v_b_ff1, v_w_ff2, v_b_ff2, v_ln3_g, v_ln3_b):
    given = dict(locals())
    w_arg = {n: given[n] for n in WEIGHTS}
    m_arg = {n: given["m_" + n] for n in WEIGHTS}
    v_arg = {n: given["v_" + n] for n in WEIGHTS}

    in_near, token = _exchange_start([w_arg["w_in"][0].astype(MXU_DTYPE)], scatter=False, peers=NEAR_PEERS,
                                     name="gather_start_in_near")
    in_far, token = _exchange_start(in_near["srcs"], scatter=False, peers=FAR_PEERS, lands=in_near["lands"],
                                    after=token, name="gather_start_in_far")
    w_in_state = [in_far["srcs"], in_far["lands"]]
    token, w_arg, m_arg, v_arg = lax.optimization_barrier((token, w_arg, m_arg, v_arg))
    shards = {n: w_arg[n][0].astype(MXU_DTYPE) for n in BIG if n != "w_in"}
    gathers = []
    for i, names in enumerate(GATHER_GROUPS):
        handle, token = _exchange_start([shards[n] for n in names], scatter=False, after=token, name=f"gather_start_{i}")
        gathers.append(handle)

    small_views = {n: _small_view(n, w_arg[n].shape) for n in SMALL}
    small_w, small_m, small_v = [{n: d[n].reshape(small_views[n]) for n in SMALL} for d in (w_arg, m_arg, v_arg)]
    relaid = [d[n] for d in (small_w, small_m, small_v) for n in SMALL_FLAT_SSM]

    def fetch_in(part, after):
        handle, peers, tag = ((in_near, (0,) + NEAR_PEERS, "near"), (in_far, FAR_PEERS, "far"))[part]
        w_in_state[:] = _exchange_wait(handle, after=after + (relaid if part == 0 else []), srcs=w_in_state[0],
                                       lands=w_in_state[1], name="gather_wait_in_" + tag)
        return w_in_state[1][0], jnp.stack([_peer_index(kk) for kk in peers]).astype(jnp.int32)

    def fetch(i, after):
        _, lands = _exchange_wait(gathers[i], after=after, name=f"gather_wait_{i}")
        full = dict(zip(GATHER_GROUPS[i], lands))
        return {n: t if n in BIG_COL_SHARDED else t.reshape(-1, t.shape[-1]) for n, t in full.items()}

    scatters = {}

    def send(i, gw):
        slots = [gw[n] if n in BIG_COL_SHARDED else gw[n].reshape(N_DEV, -1, gw[n].shape[-1]) for n in SCATTER_GROUPS[i]]
        handle, sent = _exchange_start(slots, scatter=True, name=f"scatter_start_{i}")
        scatters[i] = (handle, slots)
        return sent

    sm = {}
    for n in SMALL:
        t = w_arg[n]
        if n.startswith("ssm_") and n not in ("ssm_d", "ssm_log_dt"):
            sm[n] = t[0]
        else:
            sm[n] = t.reshape(1, -1)

    smalls = []

    def send_small(gs, names):
        handle, sent = _exchange_start([_pack_small(gs, names, small_views)], scatter=False,
                                       name=f"small_start_{len(smalls)}")
        smalls.append(handle)
        return sent

    loss_row, grad_x, gs = _local_grads(x[0], mem[0], positions.reshape(-1, 1), loss_target[0], sm, fetch_in, fetch,
                                        send, send_small, token)
    loss = lax.psum(loss_row[0, 0], ("x", "y", "c"))
    send_small(gs, SMALL_LATE)

    results = [{}, {}, {}, {}]
    done = grad_x
    for i, names in enumerate(SCATTER_GROUPS):
        handle, slots = scatters[i]
        _, lands = _exchange_wait(handle, after=[done], name=f"scatter_wait_{i}")
        for n, land, slot in zip(names, lands, slots):
            res = _reduce_adamw(land, w_arg[n][0], m_arg[n][0], v_arg[n][0], name="adamw_" + n)
            done = res[0]
            for d, r in zip(results, res):
                d[n] = r[None]
    stacks = [_exchange_wait(handle, after=[done], name=f"small_wait_{i}")[1][0] for i, handle in enumerate(smalls)]
    res = _adamw_small(stacks, small_views, small_w, small_m, small_v, name="adamw_small")
    for d, r in zip(results, res):
        d.update({n: r[n].reshape(w_arg[n].shape) for n in SMALL})
    out = [loss, grad_x[None]]
    for d in results:
        out += [d[n] for n in WEIGHTS]
    return tuple(out)
```

```python
import functools

import numpy as np
import jax
import jax.numpy as jnp
from jax import lax
from jax.experimental import pallas as pl
from jax.experimental.pallas import tpu as pltpu

F32 = jnp.float32
MXU_DTYPE = jnp.bfloat16
WIRE_DTYPE = jnp.bfloat16
VMEM_LIMIT_BYTES = 48 * 1024 * 1024
LANES = 128

N_DEV = 8
D_MODEL = 1024
SSM_GROUP = 16
SSM_WIDTH = 768
SSM_GROUPS = SSM_WIDTH // SSM_GROUP
SSM_STATE = 64
SSM_CH = SSM_GROUPS * SSM_STATE
SSM_TILES = SSM_WIDTH // LANES
GROUPS_PER_TILE = LANES // SSM_GROUP
STATE_VREG_ROWS = SSM_CH // LANES
ATT_HEAD_DIM = 64
ATT_HEADS_PER_GROUP = 4
ATT_MERGED = ATT_HEADS_PER_GROUP * ATT_HEAD_DIM
LANE_HALVES = ATT_MERGED // LANES
DILATIONS = (1, 4, 16)
ATT_BLK = 128
ATT_SCALE = ATT_HEAD_DIM ** -0.5
ROT_DIM = ATT_HEAD_DIM // 4
ROPE_THETA = 500000.0
XATT_HEADS = 4
XATT_HEAD_DIM = D_MODEL // XATT_HEADS
XATT_SCALE = XATT_HEAD_DIM ** -0.5
DEEPNORM_ALPHA = 2.0 ** 0.25
LN_EPS = 1e-5
NEG_INF = -1e30
OFF_Q_BLK, OFF_K_BLK, OFF_V_BLK = 3, 6, 9
OFF_GS_BLK, OFF_GA_BLK = 3, 4

ADAM_LR = 0.001
ADAM_B1 = 0.9
ADAM_B2 = 0.999
ADAM_EPS = 1e-08
ADAM_WD = 0.01
ADAM_STEP = 10

BIG = ("w_in", "w_glu", "w_att_up", "w_mix_out", "w_xq", "w_xkv", "w_xo", "w_ff1", "w_ff2")
BIG_COL_SHARDED = ("w_in", "w_glu", "w_att_up", "w_xkv", "w_ff1")
WEIGHTS = ("ln_in_g", "ln_in_b", "w_in", "b_in", "ssm_log_dt", "ssm_a_re", "ssm_a_im", "ssm_b_re", "ssm_b_im",
           "ssm_c_re", "ssm_c_im", "ssm_d", "w_glu", "b_glu", "w_att_up", "w_mix_out", "b_mix_out", "ln1_g", "ln1_b",
           "w_xq", "w_xkv", "w_xo", "ln2_g", "ln2_b", "w_ff1", "b_ff1", "w_ff2", "b_ff2", "ln3_g", "ln3_b")
SMALL = tuple(n for n in WEIGHTS if n not in BIG)
PACK_COLS = 1024
PACK_ROW_ALIGN = 16


def _params(*sem):
    return pltpu.CompilerParams(dimension_semantics=sem, vmem_limit_bytes=VMEM_LIMIT_BYTES)


def _dot(a, b, ca, cb):
    return lax.dot_general(a.astype(MXU_DTYPE), b.astype(MXU_DTYPE), (((ca,), (cb,)), ((), ())),
                           preferred_element_type=F32)


def _fit(dim, pref):
    if dim <= pref:
        return dim
    best = max(t for t in range(LANES, pref + 1, LANES) if dim % t == 0)
    return best


def _mm(a, b, *, name, ta=False, tb=False, bias=None, out_dtype=F32, b_shards=False, out_shards=False, after=None,
        also=None, gate=None, colsum=False, epilogue=None, tm=2048, tn=1024, tk=1024):
    m, k = (a.shape[1], a.shape[0]) if ta else a.shape
    order = (lambda f: (lambda j, i, kk: f(i, j, kk))) if colsum else (lambda f: f)
    spec = lambda shape, f: pl.BlockSpec(shape, order(f))
    if b_shards:
        n_sh, rows, n_loc = b.shape
        if tb:
            n, tn, tk = rows, _fit(rows, tn), n_loc
            assert k == n_sh * n_loc, (name, k, b.shape)
            b_spec = spec((1, tn, tk), lambda i, j, kk: (kk, j, 0))
        else:
            n, tn, tk = n_sh * n_loc, n_loc, _fit(k, tk)
            b_spec = spec((1, tk, tn), lambda i, j, kk: (j, kk, 0))
    else:
        n = b.shape[0] if tb else b.shape[1]
        tn = n // N_DEV if out_shards else _fit(n, tn)
        tk = _fit(k, tk)
        b_spec = spec((tn, tk), lambda i, j, kk: (j, kk)) if tb else spec((tk, tn), lambda i, j, kk: (kk, j))
    tm = _fit(m, tm)
    nk = k // tk
    a_spec = spec((tk, tm), lambda i, j, kk: (kk, i)) if ta else spec((tm, tk), lambda i, j, kk: (i, kk))
    tile = spec((tm, tn), lambda i, j, kk: (i, j))
    in_specs, args = [a_spec, b_spec], [a, b]
    if bias is not None:
        in_specs.append(spec((1, tn), lambda i, j, kk: (0, j)))
        args.append(bias)
    if gate is not None:
        in_specs.append(tile)
        args.append(gate[0])
    if after is not None:
        in_specs.append(pl.BlockSpec(memory_space=pl.ANY))
        args.append(after)
    if epilogue is not None:
        ep_fn, ep_rows, ep_fulls, ep_row_outs, ep_acc_outs = epilogue
        assert tn == n and not (colsum or also or gate or out_shards), name
        ep_first = len(args)
        in_specs += [spec((tm, t.shape[1]), lambda i, j, kk: (i, 0)) for t in ep_rows]
        in_specs += [pl.BlockSpec(t.shape, functools.partial(lambda i, j, kk, nd: (0,) * nd, nd=t.ndim)) for t in ep_fulls]
        args += [*ep_rows, *ep_fulls]
    n_in = len(args)
    if epilogue is not None:
        out_specs = [spec((tm, w), lambda i, j, kk: (i, 0)) for w, _ in ep_row_outs]
        out_specs += [spec((1, w), lambda i, j, kk: (0, 0)) for w in ep_acc_outs]
        out_shape = [jax.ShapeDtypeStruct((m, w), dt) for w, dt in ep_row_outs]
        out_shape += [jax.ShapeDtypeStruct((1, w), F32) for w in ep_acc_outs]
    elif out_shards:
        assert n == N_DEV * tn, (name, n, tn)
        out_specs = [spec((1, tm, tn), lambda i, j, kk: (j, i, 0))]
        out_shape = [jax.ShapeDtypeStruct((N_DEV, m, tn), out_dtype)]
    else:
        out_specs = [tile]
        out_shape = [jax.ShapeDtypeStruct((m, n), out_dtype)]
    if also is not None:
        out_specs.append(tile)
        out_shape.append(jax.ShapeDtypeStruct((m, n), also[1]))
    if colsum:
        out_specs.append(spec((1, tn), lambda i, j, kk: (0, j)))
        out_shape.append(jax.ShapeDtypeStruct((1, n), F32))

    def body(*refs):
        a_ref, b_ref = refs[0], refs[1]
        o_ref = refs[n_in]
        first_row_tile = pl.program_id(1 if colsum else 0) == 0

        def product():
            return _dot(a_ref[...], b_ref[0] if b_shards else b_ref[...], 0 if ta else 1, 1 if tb else 0)

        def finish(r):
            if bias is not None:
                r = r + refs[2][...]
            if gate is not None:
                r = r * gate[1](refs[2 + (bias is not None)][...])
            if epilogue is not None:
                res = ep_fn(r, *[ref[...] for ref in refs[ep_first:n_in]])
                n_o = len(ep_row_outs)
                for ref, val in zip(refs[n_in:n_in + n_o], res[:n_o]):
                    ref[...] = val.astype(ref.dtype)
                acc_refs = refs[n_in + n_o:n_in + n_o + len(ep_acc_outs)]
                if acc_refs:
                    @pl.when(first_row_tile)
                    def _():
                        for ref in acc_refs:
                            ref[...] = jnp.zeros_like(ref)

                    for ref, val in zip(acc_refs, res[n_o:]):
                        ref[...] += val
                return
            if out_shards:
                o_ref[0] = r.astype(o_ref.dtype)
            else:
                o_ref[...] = r.astype(o_ref.dtype)
            if also is not None:
                refs[n_in + 1][...] = also[0](r).astype(also[1])
            if colsum:
                s_ref = refs[n_in + 1 + (also is not None)]

                @pl.when(first_row_tile)
                def _():
                    s_ref[...] = jnp.zeros_like(s_ref)

                s_ref[...] += _colsum(r)

        if nk == 1:
            finish(product())
            return
        acc_ref = refs[-1]
        kk = pl.program_id(2)

        @pl.when(kk == 0)
        def _():
            acc_ref[...] = jnp.zeros_like(acc_ref)

        acc_ref[...] += product()

        @pl.when(kk == nk - 1)
        def _():
            finish(acc_ref[...])

    grid = (n // tn, m // tm, nk) if colsum else (m // tm, n // tn, nk)
    res = pl.pallas_call(
        body, name=name, grid=grid, in_specs=in_specs, out_specs=out_specs, out_shape=out_shape,
        scratch_shapes=[pltpu.VMEM((tm, tn), F32)] if nk > 1 else [],
        compiler_params=_params("arbitrary" if epilogue is not None else "parallel",
                                "arbitrary" if colsum else "parallel", "arbitrary"),
    )(*args)
    return res[0] if len(res) == 1 else res


def _mm_shards(a, w, bias, shard_ids, *, name, prev=None, tm=2048):
    m, k = a.shape
    n_sh, _, n_loc = w.shape
    tm = _fit(m, tm)

    def body(ids_ref, a_ref, w_ref, b_ref, *rest):
        rest[-1][...] = _dot(a_ref[...], w_ref[0], 1, 0) + b_ref[...]

    grid_spec = pltpu.PrefetchScalarGridSpec(
        num_scalar_prefetch=1, grid=(m // tm, shard_ids.shape[0]),
        in_specs=[pl.BlockSpec((tm, k), lambda i, j, ids: (i, 0)),
                  pl.BlockSpec((1, k, n_loc), lambda i, j, ids: (ids[j], 0, 0)),
                  pl.BlockSpec((1, n_loc), lambda i, j, ids: (0, ids[j]))]
        + [pl.BlockSpec(memory_space=pl.ANY)] * (prev is not None),
        out_specs=pl.BlockSpec((tm, n_loc), lambda i, j, ids: (i, ids[j])))
    return pl.pallas_call(
        body, name=name, grid_spec=grid_spec, out_shape=jax.ShapeDtypeStruct((m, n_sh * n_loc), F32),
        input_output_aliases={4: 0} if prev is not None else {}, compiler_params=_params("parallel", "arbitrary"),
    )(shard_ids, a, w, bias, *([prev] if prev is not None else []))


def _rowcall(fn, rows, fulls, row_outs, acc_outs=(), *, n_rows, tm, name, after=None):
    n_r, n_f, n_o, n_a = len(rows), len(fulls), len(row_outs), len(acc_outs)
    n_in = n_r + n_f + (after is not None)
    assert n_rows % tm == 0, (name, n_rows, tm)

    def body(*refs):
        res = fn(*[r[...] for r in refs[:n_r + n_f]])
        res = tuple(res) if isinstance(res, (tuple, list)) else (res,)
        o_refs = refs[n_in:n_in + n_o]
        a_refs = refs[n_in + n_o:]
        for o_ref, val in zip(o_refs, res[:n_o]):
            o_ref[...] = val.astype(o_ref.dtype)
        if n_a:
            @pl.when(pl.program_id(0) == 0)
            def _():
                for a_ref in a_refs:
                    a_ref[...] = jnp.zeros_like(a_ref)

            for a_ref, val in zip(a_refs, res[n_o:]):
                a_ref[...] += val

    in_specs = [pl.BlockSpec((tm, w), functools.partial(lambda i, cb: (i, cb), cb=cb)) for _, w, cb in rows]
    in_specs += [pl.BlockSpec(f.shape, functools.partial(lambda i, nd: (0,) * nd, nd=f.ndim)) for f in fulls]
    in_specs += [pl.BlockSpec(memory_space=pl.ANY)] * (after is not None)
    out_specs = [pl.BlockSpec((tm, w), lambda i: (i, 0)) for w, _ in row_outs]
    out_specs += [pl.BlockSpec((1, w), lambda i: (0, 0)) for w in acc_outs]
    out_shape = [jax.ShapeDtypeStruct((n_rows, w), dt) for w, dt in row_outs]
    out_shape += [jax.ShapeDtypeStruct((1, w), F32) for w in acc_outs]
    return pl.pallas_call(
        body, name=name, grid=(n_rows // tm,), in_specs=in_specs, out_specs=out_specs, out_shape=out_shape,
        compiler_params=_params("arbitrary" if n_a else "parallel"),
    )(*[r[0] for r in rows], *fulls, *([after] if after is not None else []))


def _colsum(v):
    return jnp.sum(v, axis=0, keepdims=True)


def _ln_fwd(a, r, g, b, *, alpha, name):
    n_rows, d = a.shape

    def fn(*t):
        xin = t[0] if alpha == 1.0 else alpha * t[0]
        if r is not None:
            xin = xin + t[1]
        gv, bv = t[-2], t[-1]
        mu = jnp.mean(xin, axis=-1, keepdims=True)
        xc = xin - mu
        var = jnp.mean(xc * xc, axis=-1, keepdims=True)
        rstd = lax.rsqrt(var + LN_EPS)
        xh = xc * rstd
        y = xh * gv + bv
        return y, xh, rstd, y

    rows = [(a, d, 0)] + ([(r, d, 0)] if r is not None else [])
    return _rowcall(fn, rows, [g, b], [(d, F32), (d, F32), (1, F32), (d, MXU_DTYPE)], n_rows=n_rows, tm=256, name=name)


def _ln_bwd(dya, dyb, xh, rstd, g, *, alpha, name, operand=True):
    n_rows, d = xh.shape

    def fn(da, db, xhv, rs, gv):
        dy = alpha * da + db
        dyg = dy * gv
        m1 = jnp.mean(dyg, axis=-1, keepdims=True)
        m2 = jnp.mean(dyg * xhv, axis=-1, keepdims=True)
        dx = rs * (dyg - m1 - xhv * m2)
        return (dx,) + ((dx,) if operand else ()) + (_colsum(dy * xhv), _colsum(dy), _colsum(dx))

    rows = [(dya, d, 0), (dyb, d, 0), (xh, d, 0), (rstd, 1, 0)]
    return _rowcall(fn, rows, [g], [(d, F32)] + [(d, MXU_DTYPE)] * operand, [d, d, d], n_rows=n_rows, tm=256, name=name)


def _mm_ln_fwd(x, w, bias, a, g, b, *, alpha, name):
    d = a.shape[1]

    def fn(r, av, gv, bv):
        xin = alpha * av + r
        mu = jnp.mean(xin, axis=-1, keepdims=True)
        xc = xin - mu
        var = jnp.mean(xc * xc, axis=-1, keepdims=True)
        rstd = lax.rsqrt(var + LN_EPS)
        xh = xc * rstd
        y = xh * gv + bv
        return y, xh, rstd, y

    return _mm(x, w, bias=bias, name=name, tm=512,
               epilogue=(fn, [a], [g, b], [(d, F32), (d, F32), (1, F32), (d, MXU_DTYPE)], []))


def _mm_ln_bwd(x, w, dya, xh, rstd, g, *, alpha, name, operand=True, **product):
    d = xh.shape[1]

    def fn(r, da, xhv, rs, gv):
        dy = alpha * da + r
        dyg = dy * gv
        m1 = jnp.mean(dyg, axis=-1, keepdims=True)
        m2 = jnp.mean(dyg * xhv, axis=-1, keepdims=True)
        dx = rs * (dyg - m1 - xhv * m2)
        return (dx,) + ((dx,) if operand else ()) + (_colsum(dy * xhv), _colsum(dy), _colsum(dx))

    return _mm(x, w, tb=True, name=name, tm=512, **product,
               epilogue=(fn, [dya, xh, rstd], [g], [(d, F32)] + [(d, MXU_DTYPE)] * operand, [d, d, d]))


def _ln_loss_bwd(a, r, target, g, b, *, alpha, name):
    n_rows, d = a.shape

    def fn(av, rv, tv, gv, bv):
        xin = alpha * av + rv
        mu = jnp.mean(xin, axis=-1, keepdims=True)
        xc = xin - mu
        var = jnp.mean(xc * xc, axis=-1, keepdims=True)
        rs = lax.rsqrt(var + LN_EPS)
        xh = xc * rs
        diff = xh * gv + bv - tv
        part = jnp.sum(jnp.sum(diff * diff, axis=1, keepdims=True), axis=0, keepdims=True) * (0.5 / d)
        dy = diff * (1.0 / d)
        dyg = dy * gv
        m1 = jnp.mean(dyg, axis=-1, keepdims=True)
        m2 = jnp.mean(dyg * xh, axis=-1, keepdims=True)
        dx = rs * (dyg - m1 - xh * m2)
        return dx, dx, _colsum(dy * xh), _colsum(dy), _colsum(dx), jnp.broadcast_to(part, (1, LANES))

    return _rowcall(fn, [(a, d, 0), (r, d, 0), (target, d, 0)], [g, b], [(d, F32), (d, MXU_DTYPE)], [d, d, d, LANES],
                    n_rows=n_rows, tm=256, name=name)


def _rope_lane_constants():
    lane = np.arange(ATT_MERGED)
    in_head = lane % ATT_HEAD_DIM
    sign = np.where(in_head < ROT_DIM // 2, -1.0, np.where(in_head < ROT_DIM, 1.0, 0.0)).astype(np.float32)
    inv_freq = ROPE_THETA ** (-jnp.arange(0, ROT_DIM, 2, dtype=F32) / ROT_DIM)
    return inv_freq[lane % (ROT_DIM // 2)].reshape(1, ATT_MERGED), jnp.asarray(sign).reshape(1, ATT_MERGED)


def _rope_tables(pos_col, *, name, after=None):
    inv_lane, sign = _rope_lane_constants()

    def fn(pos, inv, sg):
        ang = pos.astype(F32) * inv
        return jnp.where(sg != 0.0, jnp.cos(ang), 1.0), sg * jnp.sin(ang)

    return _rowcall(fn, [(pos_col, 1, 0)], [inv_lane, sign], [(ATT_MERGED, F32), (ATT_MERGED, F32)],
                    n_rows=pos_col.shape[0], tm=512, name=name, after=after)


def _rot_partner(t):
    lane = lax.broadcasted_iota(jnp.int32, t.shape, 1)
    width = t.shape[1]
    return jnp.where((lane & (ROT_DIM // 2)) == 0, pltpu.roll(t, width - ROT_DIM // 2, 1), pltpu.roll(t, ROT_DIM // 2, 1))


def _rope(t, cos_t, sin_t):
    return t * cos_t + _rot_partner(t) * sin_t


def _rope_transpose(dt, cos_t, sin_t):
    return dt * cos_t + _rot_partner(dt * sin_t)


def _strided_rows(r, count, stride):
    return pl.ds(r, count) if stride == 1 else pl.ds(r, count, stride=stride)


def _qkv_split(proj, cos_t, sin_t, *, name, tm=512):
    n_rows = proj.shape[0]
    n_g = len(DILATIONS)

    def body(*refs):
        n_src = LANE_HALVES * 3 * n_g
        src, tables, dst = refs[:n_src], refs[n_src:n_src + 2 * LANE_HALVES], refs[n_src + 2 * LANE_HALVES:]
        for kind in range(3):
            for g, dil in enumerate(DILATIONS):
                for half in range(LANE_HALVES):
                    x_ref, o_ref = src[(kind * n_g + g) * LANE_HALVES + half], dst[kind * n_g + g]
                    cos_ref, sin_ref = tables[half], tables[LANE_HALVES + half]
                    for r in range(dil):
                        rows = _strided_rows(r, tm // dil, dil)
                        t = x_ref[rows, :]
                        if kind < 2:
                            t = _rope(t, cos_ref[rows, :], sin_ref[rows, :])
                        lo = r * ATT_MERGED + half * LANES
                        o_ref[:, lo:lo + LANES] = t.astype(o_ref.dtype)

    half_spec = lambda cb: pl.BlockSpec((tm, LANES), functools.partial(lambda i, cb: (i, cb), cb=cb))
    in_specs = [half_spec((off + g) * LANE_HALVES + half)
                for off in (OFF_Q_BLK, OFF_K_BLK, OFF_V_BLK) for g in range(n_g) for half in range(LANE_HALVES)]
    in_specs += [half_spec(half) for _ in range(2) for half in range(LANE_HALVES)]
    out_specs = [pl.BlockSpec((tm // dil, dil * ATT_MERGED), lambda i: (i, 0)) for _ in range(3) for dil in DILATIONS]
    out_shape = [jax.ShapeDtypeStruct((n_rows // dil, dil * ATT_MERGED), MXU_DTYPE) for _ in range(3) for dil in DILATIONS]
    outs = pl.pallas_call(
        body, name=name, grid=(n_rows // tm,), in_specs=in_specs, out_specs=out_specs, out_shape=out_shape,
        compiler_params=_params("parallel"),
    )(*[proj] * (LANE_HALVES * 3 * n_g), *[cos_t] * LANE_HALVES, *[sin_t] * LANE_HALVES)
    return outs[:n_g], outs[n_g:2 * n_g], outs[2 * n_g:]


def _mix(gs, ga, z1, z2, b_att):
    return jax.nn.sigmoid(gs) * (z1 * jax.nn.sigmoid(z2)) + jax.nn.sigmoid(ga) * b_att


def _mix_rows(proj, z, b_att):
    return [(proj, D_MODEL, OFF_GS_BLK), (proj, D_MODEL, OFF_GA_BLK), (z, D_MODEL, 0), (z, D_MODEL, 1), (b_att, D_MODEL, 0)]


def _mix_fwd(proj, z, b_att, *, name):
    return _rowcall(_mix, _mix_rows(proj, z, b_att), [], [(D_MODEL, MXU_DTYPE)],
                    n_rows=proj.shape[0], tm=256, name=name)[0]


def _mix_bwd(dmixed, proj, z, b_att, *, name):
    def fn(dm, gs, ga, z1, z2, ba):
        _, vjp = jax.vjp(_mix, gs, ga, z1, z2, ba)
        dgs, dga, dz1, dz2, dba = vjp(dm)
        dz = jnp.concatenate([dz1, dz2], axis=1)
        return dgs, dga, dz, dba, _colsum(dgs), _colsum(dga), _colsum(dz)

    rows = [(dmixed, D_MODEL, 0)] + _mix_rows(proj, z, b_att)
    widths = [D_MODEL, D_MODEL, 2 * D_MODEL, D_MODEL]
    return _rowcall(fn, rows, [], [(w, MXU_DTYPE) for w in widths], widths[:3], n_rows=proj.shape[0], tm=256, name=name)


def _gelu_bwd(dgy, y, proj, *, name):
    def fn(dg, yv, u):
        _, vjp = jax.vjp(jax.nn.gelu, yv)
        dy = vjp(dg)[0]
        return dy, _colsum(dy * u)

    return _rowcall(fn, [(dgy, SSM_WIDTH, 0), (y, SSM_WIDTH, 0), (proj, SSM_WIDTH, 0)], [], [(SSM_WIDTH, F32)],
                    [SSM_WIDTH], n_rows=y.shape[0], tm=512, name=name)


HEAD_ROWS = ATT_HEADS_PER_GROUP * ATT_BLK


def _head_masks(rows):
    head = lax.broadcasted_iota(jnp.int32, (rows, ATT_MERGED), 1) >> (ATT_HEAD_DIM.bit_length() - 1)
    return [head == h for h in range(ATT_HEADS_PER_GROUP)]


def _stack_heads(t, masks):
    return jnp.concatenate([jnp.where(m, t, jnp.zeros_like(t)) for m in masks], axis=0)


def _unstack_heads(t4, masks):
    blocks = [t4[h * ATT_BLK:(h + 1) * ATT_BLK] for h in range(ATT_HEADS_PER_GROUP)]
    return jnp.where(masks[0], blocks[0], jnp.where(masks[1], blocks[1], jnp.where(masks[2], blocks[2], blocks[3])))


def _head_column(stats, first):
    return jnp.concatenate([stats[:, first + h:first + h + 1] for h in range(ATT_HEADS_PER_GROUP)], axis=0)


def _band_mask(first_key):
    qi = lax.broadcasted_iota(jnp.int32, (HEAD_ROWS, 2 * ATT_BLK), 0) & (ATT_BLK - 1)
    ki = lax.broadcasted_iota(jnp.int32, (HEAD_ROWS, 2 * ATT_BLK), 1)
    steps = qi + ATT_BLK - ki
    return (steps >= 0) & (steps <= ATT_BLK) & (ki >= first_key)


def _dil_fwd(q, k, v, dil, *, name):
    n_blk = q.shape[0] // ATT_BLK
    cur = pl.BlockSpec((ATT_BLK, ATT_MERGED), lambda r, n: (n, r))
    prev = pl.BlockSpec((ATT_BLK, ATT_MERGED), lambda r, n: (jnp.maximum(n - 1, 0), r))

    def body(q_ref, kp_ref, kc_ref, vp_ref, vc_ref, o_ref, l_ref):
        masks = _head_masks(ATT_BLK)
        valid = _band_mask(jnp.where(pl.program_id(1) > 0, 0, ATT_BLK))
        keys = jnp.concatenate([kp_ref[...], kc_ref[...]], axis=0)
        vals = jnp.concatenate([vp_ref[...], vc_ref[...]], axis=0)
        s = jnp.where(valid, _dot(_stack_heads(q_ref[...], masks), keys, 1, 1) * ATT_SCALE, NEG_INF)
        m = jnp.max(s, axis=-1, keepdims=True)
        p = jnp.exp(s - m)
        den = jnp.sum(p, axis=-1, keepdims=True)
        o_ref[...] = _unstack_heads(_dot(p, vals, 1, 0) / den, masks)
        l_ref[...] = _unstack_heads(jnp.broadcast_to(m + jnp.log(den), (HEAD_ROWS, ATT_MERGED)), masks)

    shape = jax.ShapeDtypeStruct(q.shape, F32)
    return pl.pallas_call(
        body, name=name, grid=(dil, n_blk), in_specs=[cur, prev, cur, prev, cur], out_specs=[cur, cur],
        out_shape=[shape, shape], compiler_params=_params("parallel", "parallel"),
    )(q, k, k, v, v)


def _att_merge(outs, lses, *, name, tm=512):
    n_g = len(outs)
    n_rows = outs[0].shape[0] * DILATIONS[0]

    def body(*refs):
        src, (att_ref, lse_ref), tmp = refs[:2 * n_g], refs[2 * n_g:2 * n_g + 2], refs[2 * n_g + 2:]
        vals = []
        for idx, src_ref in enumerate(src):
            dil = DILATIONS[idx % n_g]
            if dil == 1:
                vals.append(src_ref[...])
                continue
            for r in range(dil):
                for half in range(LANE_HALVES):
                    lo = r * ATT_MERGED + half * LANES
                    tmp[LANE_HALVES * idx + half][_strided_rows(r, tm // dil, dil), :] = src_ref[:, lo:lo + LANES]
            vals.append(jnp.concatenate([tmp[LANE_HALVES * idx + half][...] for half in range(LANE_HALVES)], axis=1))
        o, l = vals[:n_g], vals[n_g:]
        m = functools.reduce(jnp.maximum, l)
        e = [jnp.exp(li - m) for li in l]
        z = functools.reduce(jnp.add, e)
        att_ref[...] = functools.reduce(jnp.add, [(ei / z) * oi for ei, oi in zip(e, o)])
        lse_ref[...] = m + jnp.log(z)

    in_specs = [pl.BlockSpec((tm // dil, dil * ATT_MERGED), lambda i: (i, 0)) for _ in range(2) for dil in DILATIONS]
    row = pl.BlockSpec((tm, ATT_MERGED), lambda i: (i, 0))
    shape = jax.ShapeDtypeStruct((n_rows, ATT_MERGED), F32)
    return pl.pallas_call(
        body, name=name, grid=(n_rows // tm,), in_specs=in_specs, out_specs=[row, row], out_shape=[shape, shape],
        scratch_shapes=[pltpu.VMEM((tm, LANES), F32)] * (LANE_HALVES * 2 * n_g), compiler_params=_params("parallel"),
    )(*outs, *lses)


def _att_stats(datt, att, lse, *, name):
    n_rows = datt.shape[0]

    def fn(d, a, l):
        prod = d * a
        lane = lax.broadcasted_iota(jnp.int32, (d.shape[0], LANES), 1)
        out = jnp.zeros((d.shape[0], LANES), F32)
        for h in range(ATT_HEADS_PER_GROUP):
            lo = h * ATT_HEAD_DIM
            out = jnp.where(lane == h, l[:, lo:lo + 1], out)
            delta = jnp.sum(prod[:, lo:lo + ATT_HEAD_DIM], axis=-1, keepdims=True)
            out = jnp.where(lane == ATT_HEADS_PER_GROUP + h, delta, out)
        return out

    rows = [(t, ATT_MERGED, 0) for t in (datt, att, lse)]
    return _rowcall(fn, rows, [], [(LANES, F32)], n_rows=n_rows, tm=512, name=name)[0]


def _dil_bwd(q, k, v, datt, stats, dil, *, name):
    n_rows = datt.shape[0]
    n_blk = n_rows // dil // ATT_BLK
    span = ATT_BLK * dil
    cur = pl.BlockSpec((ATT_BLK, ATT_MERGED), lambda n, r: (n, r))
    prev = pl.BlockSpec((ATT_BLK, ATT_MERGED), lambda n, r: (jnp.maximum(n - 1, 0), r))
    nxt = pl.BlockSpec((ATT_BLK, ATT_MERGED), lambda n, r: (jnp.minimum(n + 1, n_blk - 1), r))
    seq = lambda half, ahead: pl.BlockSpec((span, LANES), lambda n, r: (jnp.minimum(n + ahead, n_blk - 1), half))

    def body(qc_ref, qn_ref, kp_ref, kc_ref, vp_ref, vc_ref, dc0_ref, dc1_ref, dn0_ref, dn1_ref, sc_ref, sn_ref,
             dq0_ref, dq1_ref, dk0_ref, dk1_ref, dv0_ref, dv1_ref):
        n = pl.program_id(0)
        rows = slice(None) if dil == 1 else _strided_rows(pl.program_id(1), ATT_BLK, dil)

        def read(ref0, ref1):
            return jnp.concatenate([ref0[rows, :], ref1[rows, :]], axis=1)

        def write(ref0, ref1, val):
            ref0[rows, :] = val[:, :LANES]
            ref1[rows, :] = val[:, LANES:]

        masks = _head_masks(ATT_BLK)
        valid = _band_mask(jnp.where(n > 0, 0, ATT_BLK))
        qi = lax.broadcasted_iota(jnp.int32, (HEAD_ROWS, ATT_BLK), 0) & (ATT_BLK - 1)
        ki = lax.broadcasted_iota(jnp.int32, (HEAD_ROWS, ATT_BLK), 1)
        valid_next = (ki - qi) >= jnp.where(n < n_blk - 1, 0, ATT_BLK)

        kc, vc = kc_ref[...], vc_ref[...]
        keys = jnp.concatenate([kp_ref[...], kc], axis=0)
        vals = jnp.concatenate([vp_ref[...], vc], axis=0)
        q4 = _stack_heads(qc_ref[...], masks)
        d4 = _stack_heads(read(dc0_ref, dc1_ref).astype(MXU_DTYPE), masks)
        st = sc_ref[rows, :]
        p = jnp.where(valid, jnp.exp(_dot(q4, keys, 1, 1) * ATT_SCALE - _head_column(st, 0)), 0.0)
        ds = p * (_dot(d4, vals, 1, 1) - _head_column(st, ATT_HEADS_PER_GROUP)) * ATT_SCALE
        write(dq0_ref, dq1_ref, _unstack_heads(_dot(ds, keys, 1, 0), masks))

        q4n = _stack_heads(qn_ref[...], masks)
        d4n = _stack_heads(read(dn0_ref, dn1_ref).astype(MXU_DTYPE), masks)
        stn = sn_ref[rows, :]
        p_n = jnp.where(valid_next, jnp.exp(_dot(q4n, kc, 1, 1) * ATT_SCALE - _head_column(stn, 0)), 0.0)
        ds_n = p_n * (_dot(d4n, vc, 1, 1) - _head_column(stn, ATT_HEADS_PER_GROUP)) * ATT_SCALE
        write(dv0_ref, dv1_ref, _dot(p[:, ATT_BLK:], d4, 0, 0) + _dot(p_n, d4n, 0, 0))
        write(dk0_ref, dk1_ref, _dot(ds[:, ATT_BLK:], q4, 0, 0) + _dot(ds_n, q4n, 0, 0))

    shape = jax.ShapeDtypeStruct((n_rows, LANES), F32)
    out = seq(0, 0)
    res = pl.pallas_call(
        body, name=name, grid=(n_blk, dil),
        in_specs=[cur, nxt, prev, cur, prev, cur, seq(0, 0), seq(1, 0), seq(0, 1), seq(1, 1), seq(0, 0), seq(0, 1)],
        out_specs=[out] * 6, out_shape=[shape] * 6, compiler_params=_params("parallel", "arbitrary"),
    )(q, q, k, k, v, v, datt, datt, datt, datt, stats, stats)
    return [(res[2 * i], res[2 * i + 1]) for i in range(3)]


def _dproj_assemble(du, dqkv, dgs, dga, cos_t, sin_t, *, name):
    n_g = len(DILATIONS)

    def fn(*t):
        n_half = LANE_HALVES * 3 * n_g
        du_t, halves, (dgs_t, dga_t, c, s) = t[0], t[1:1 + n_half], t[1 + n_half:]
        parts = [jnp.concatenate(halves[LANE_HALVES * i:LANE_HALVES * (i + 1)], axis=1) for i in range(3 * n_g)]
        for i in range(2 * n_g):
            parts[i] = _rope_transpose(parts[i], c, s)
        cast = [p.astype(MXU_DTYPE) for p in parts]
        return [jnp.concatenate([du_t] + cast + [dgs_t, dga_t], axis=1)] + [_colsum(p) for p in parts]

    rows = [(du, SSM_WIDTH, 0)]
    rows += [(half, LANES, 0) for i in range(3) for g in range(n_g) for half in dqkv[g][i]]
    rows += [(dgs, D_MODEL, 0), (dga, D_MODEL, 0), (cos_t, ATT_MERGED, 0), (sin_t, ATT_MERGED, 0)]
    width = SSM_WIDTH + 3 * n_g * ATT_MERGED + 2 * D_MODEL
    res = _rowcall(fn, rows, [], [(width, MXU_DTYPE)], [ATT_MERGED] * (3 * n_g), n_rows=du.shape[0], tm=256, name=name)
    return res[0], res[1:]


def _xhead(h):
    return slice(h * XATT_HEAD_DIM, (h + 1) * XATT_HEAD_DIM)


def _xatt_probs(qh, kh):
    s = _dot(qh, kh, 1, 1) * XATT_SCALE
    e = jnp.exp(s - jnp.max(s, axis=-1, keepdims=True))
    return e / jnp.sum(e, axis=-1, keepdims=True)


def _xatt_fwd(q, kv, *, name, tm=512):
    n_rows = q.shape[0]
    n_mem = kv.shape[0]

    def body(q_ref, kv_ref, o_ref):
        for h in range(XATT_HEADS):
            sl = _xhead(h)
            p = _xatt_probs(q_ref[:, sl], kv_ref[:, sl])
            o_ref[:, sl] = _dot(p, kv_ref[:, D_MODEL + h * XATT_HEAD_DIM:D_MODEL + (h + 1) * XATT_HEAD_DIM], 1, 0
                                ).astype(o_ref.dtype)

    row = pl.BlockSpec((tm, D_MODEL), lambda i: (i, 0))
    return pl.pallas_call(
        body, name=name, grid=(n_rows // tm,),
        in_specs=[row, pl.BlockSpec((n_mem, 2 * D_MODEL), lambda i: (0, 0))], out_specs=row,
        out_shape=jax.ShapeDtypeStruct((n_rows, D_MODEL), MXU_DTYPE), compiler_params=_params("parallel"),
    )(q, kv)


def _xatt_bwd(q, kv, do, *, name, tm=512):
    n_rows = q.shape[0]
    n_mem = kv.shape[0]

    def body(q_ref, kv_ref, do_ref, dq_ref, dkv_ref):
        @pl.when(pl.program_id(0) == 0)
        def _():
            dkv_ref[...] = jnp.zeros_like(dkv_ref)

        for h in range(XATT_HEADS):
            sl = _xhead(h)
            vsl = slice(D_MODEL + h * XATT_HEAD_DIM, D_MODEL + (h + 1) * XATT_HEAD_DIM)
            qh, kh, doh = q_ref[:, sl], kv_ref[:, sl], do_ref[:, sl]
            p = _xatt_probs(qh, kh)
            dp = _dot(doh, kv_ref[:, vsl], 1, 1)
            ds = p * (dp - jnp.sum(dp * p, axis=-1, keepdims=True)) * XATT_SCALE
            dq_ref[:, sl] = _dot(ds, kh, 1, 0).astype(dq_ref.dtype)
            dkv_ref[:, sl] += _dot(ds, qh, 0, 0)
            dkv_ref[:, vsl] += _dot(p, doh, 0, 0)

    row = pl.BlockSpec((tm, D_MODEL), lambda i: (i, 0))
    full = pl.BlockSpec((n_mem, 2 * D_MODEL), lambda i: (0, 0))
    return pl.pallas_call(
        body, name=name, grid=(n_rows // tm,), in_specs=[row, full, row], out_specs=[row, full],
        out_shape=[jax.ShapeDtypeStruct((n_rows, D_MODEL), MXU_DTYPE), jax.ShapeDtypeStruct((n_mem, 2 * D_MODEL), F32)],
        compiler_params=_params("arbitrary"),
    )(q, kv, do)


def _disc(logdt, a_re, a_im, b_re, b_im):
    dt = jnp.exp(logdt)
    mag = jnp.exp(a_re * dt)
    ab_re = mag * jnp.cos(a_im * dt)
    ab_im = mag * jnp.sin(a_im * dt)
    den = jnp.square(a_re) + jnp.square(a_im)
    nr = ab_re - 1.0
    f_re = (nr * a_re + ab_im * a_im) / den
    f_im = (ab_im * a_re - nr * a_im) / den
    bb_re = f_re[None] * b_re - f_im[None] * b_im
    bb_im = f_re[None] * b_im + f_im[None] * b_re
    return ab_re, ab_im, bb_re, bb_im


def _disc_transpose(logdt, a_re, a_im, b_re, b_im, g_ab_re, g_ab_im, g_bb_re, g_bb_im):
    dt = jnp.exp(logdt)
    mag = jnp.exp(a_re * dt)
    th = a_im * dt
    cs, sn = jnp.cos(th), jnp.sin(th)
    ab_re, ab_im = mag * cs, mag * sn
    den = jnp.square(a_re) + jnp.square(a_im)
    nr = ab_re - 1.0
    f_re = (nr * a_re + ab_im * a_im) / den
    f_im = (ab_im * a_re - nr * a_im) / den
    d_f_re = jnp.sum(g_bb_re * b_re + g_bb_im * b_im, axis=0)
    d_f_im = jnp.sum(g_bb_im * b_re - g_bb_re * b_im, axis=0)
    d_b_re = g_bb_re * f_re[None] + g_bb_im * f_im[None]
    d_b_im = g_bb_im * f_re[None] - g_bb_re * f_im[None]
    d_n_re, d_n_im = d_f_re / den, d_f_im / den
    d_den = -(d_f_re * f_re + d_f_im * f_im) / den
    d_ab_re = g_ab_re + d_n_re * a_re - d_n_im * a_im
    d_ab_im = g_ab_im + d_n_re * a_im + d_n_im * a_re
    d_a_re = d_n_re * nr + d_n_im * ab_im + 2.0 * d_den * a_re
    d_a_im = d_n_re * ab_im - d_n_im * nr + 2.0 * d_den * a_im
    d_mag = d_ab_re * cs + d_ab_im * sn
    d_th = mag * (d_ab_im * cs - d_ab_re * sn)
    d_a_re = d_a_re + d_mag * mag * dt
    d_a_im = d_a_im + d_th * dt
    d_dt = jnp.sum(d_mag * mag * a_re + d_th * a_im, axis=-1, keepdims=True)
    return d_dt * dt, d_a_re, d_a_im, d_b_re, d_b_im


def _full_spec(shape):
    return pl.BlockSpec(tuple(shape), functools.partial(lambda i, nd: (0,) * nd, nd=len(shape)))


def _whole(fn, args, out_shapes, *, name):
    n_in = len(args)

    def body(*refs):
        res = fn(*[r[...] for r in refs[:n_in]])
        for o_ref, val in zip(refs[n_in:], res):
            o_ref[...] = val

    return pl.pallas_call(
        body, name=name, grid=(1,), in_specs=[_full_spec(t.shape) for t in args],
        out_specs=[_full_spec(s) for s in out_shapes], out_shape=[jax.ShapeDtypeStruct(s, F32) for s in out_shapes],
        compiler_params=_params("arbitrary"))(*args)


SSM_WIDE =GROUPS_PER_TILE * SSM_STATE
LANE_GROUPS_PER_TILE = SSM_WIDE // LANES


def _chan(j):
    return slice(j * LANES, (j + 1) * LANES)


def _time_major_rows(j, q, tc):
    return pl.ds(j * LANE_GROUPS_PER_TILE + q, tc, stride=STATE_VREG_ROWS)


def _to_time_major(x, t_re_ref, t_im_ref, dst_re, dst_im, tc):
    for j in range(SSM_TILES):
        xj = x[:, _chan(j)]
        for t_ref, dst in ((t_re_ref, dst_re), (t_im_ref, dst_im)):
            r = _dot(xj, t_ref[j], 1, 0)
            for q in range(LANE_GROUPS_PER_TILE):
                dst[_time_major_rows(j, q, tc), :] = r[:, q * LANES:(q + 1) * LANES]


def _from_time_major(src, j, tc):
    return jnp.concatenate([src[_time_major_rows(j, q, tc), :] for q in range(LANE_GROUPS_PER_TILE)], axis=1)


def _scan_chunk(w_re, w_im, h_re, h_im, a_re, a_im, start, tc):
    def step(t, carry):
        hr, hi = carry
        rows = _scan_rows(t)
        nr = a_re * hr - a_im * hi + w_re[rows, :]
        ni = a_re * hi + a_im * hr + w_im[rows, :]
        h_re[rows, :] = nr
        h_im[rows, :] = ni
        return nr, ni

    return lax.fori_loop(0, tc, step, start, unroll=8)


SSM_CHUNK = 256


def _tile_spec(stack, k):
    return pl.BlockSpec((pl.Squeezed(),) + tuple(stack.shape[1:]), lambda i: (k, 0, 0, 0))


def _expand_block_diagonal(src_ref, dst):
    dst[...] = jnp.zeros_like(dst)
    r, c = src_ref.shape[1:]
    for g in range(SSM_GROUPS):
        j, gl = divmod(g, GROUPS_PER_TILE)
        dst[j, gl * r:(gl + 1) * r, gl * c:(gl + 1) * c] = src_ref[g].astype(dst.dtype)


def _extract_block_diagonal(src, dst_ref):
    r, c = dst_ref.shape[1:]
    for g in range(SSM_GROUPS):
        j, gl = divmod(g, GROUPS_PER_TILE)
        dst_ref[g] = src[j, gl * r:(gl + 1) * r, gl * c:(gl + 1) * c]


def _ssm_fwd(proj, blocks_cn, blocks_nc, a_re, a_im, gain, *, name, tc=SSM_CHUNK):
    n_rows = proj.shape[0]
    n_chunk = n_rows // tc

    def body(u_ref, br_ref, bi_ref, cr_ref, ci_ref, ar_ref, ai_ref, g_ref, y_ref, gy_ref, hr, hi, wr, wi, state,
             tbr_ref, tbi_ref, tcr_ref, tci_ref):
        @pl.when(pl.program_id(0) == 0)
        def _():
            state[...] = jnp.zeros_like(state)
            for src_ref, dst in ((br_ref, tbr_ref), (bi_ref, tbi_ref), (cr_ref, tcr_ref), (ci_ref, tci_ref)):
                _expand_block_diagonal(src_ref, dst)

        u = u_ref[...]
        _to_time_major(u, tbr_ref, tbi_ref, wr, wi, tc)
        state[0], state[1] = _scan_chunk(wr, wi, hr, hi, ar_ref[...], ai_ref[...], (state[0], state[1]), tc)
        for j in range(SSM_TILES):
            yj = (_dot(_from_time_major(hr, j, tc), tcr_ref[j], 1, 0) + _dot(_from_time_major(hi, j, tc), tci_ref[j], 1, 0)
                  + g_ref[:, _chan(j)] * u[:, _chan(j)])
            y_ref[:, _chan(j)] = yj
            gy_ref[:, _chan(j)] = jax.nn.gelu(yj).astype(gy_ref.dtype)

    rows = pl.BlockSpec((tc, SSM_WIDTH), lambda i: (i, 0))
    coef = pl.BlockSpec((STATE_VREG_ROWS, LANES), lambda i: (0, 0))
    states = pl.BlockSpec((tc * STATE_VREG_ROWS, LANES), lambda i: (i, 0))
    sshape = jax.ShapeDtypeStruct((n_rows * STATE_VREG_ROWS, LANES), F32)
    return pl.pallas_call(
        body, name=name, grid=(n_chunk,),
        in_specs=[rows, _tile_spec(blocks_cn, 0), _tile_spec(blocks_cn, 1), _tile_spec(blocks_nc, 0),
                  _tile_spec(blocks_nc, 1), coef, coef, pl.BlockSpec((1, SSM_WIDTH), lambda i: (0, 0))],
        out_specs=[rows, rows, states, states],
        out_shape=[jax.ShapeDtypeStruct((n_rows, SSM_WIDTH), F32), jax.ShapeDtypeStruct((n_rows, SSM_WIDTH), MXU_DTYPE),
                   sshape, sshape],
        scratch_shapes=[pltpu.VMEM((tc * STATE_VREG_ROWS, LANES), F32)] * 2 + [pltpu.VMEM((2, STATE_VREG_ROWS, LANES), F32)]
        + [pltpu.VMEM((SSM_TILES, LANES, SSM_WIDE), MXU_DTYPE)] * 2 + [pltpu.VMEM((SSM_TILES, SSM_WIDE, LANES), MXU_DTYPE)] * 2,
        compiler_params=_params("arbitrary"),
    )(proj, blocks_cn, blocks_cn, blocks_nc, blocks_nc, a_re, a_im, gain)


def _ssm_bwd(proj, dy, h_re, h_im, blocks_cn, blocks_nc, a_re, a_im, gain, *, name, tc=SSM_CHUNK):
    n_rows = proj.shape[0]
    n_chunk = n_rows // tc

    def body(u_ref, dy_ref, hr, hi, cr_ref, ci_ref, br_ref, bi_ref, ar_ref, ai_ref, g_ref,
             du_ref, su_ref, dc_re_ref, dc_im_ref, db_re_ref, db_im_ref, dar_ref, dai_ref, wr, wi, carry,
             tdr_ref, tdi_ref, tur_ref, tui_ref, dcr_ref, dci_ref, dbr_ref, dbi_ref):
        @pl.when(pl.program_id(0) == 0)
        def _():
            carry[...] = jnp.zeros_like(carry)
            for acc_ref in (su_ref, dcr_ref, dci_ref, dbr_ref, dbi_ref):
                acc_ref[...] = jnp.zeros_like(acc_ref)
            for src_ref, dst in ((cr_ref, tdr_ref), (ci_ref, tdi_ref), (br_ref, tur_ref), (bi_ref, tui_ref)):
                _expand_block_diagonal(src_ref, dst)

        a_r, a_i = ar_ref[...], ai_ref[...]
        u, dyv = u_ref[...], dy_ref[...]
        _to_time_major(dyv, tdr_ref, tdi_ref, wr, wi, tc)

        def step(kk, c):
            lam_r, lam_i, dar, dai = c
            rows = _scan_rows(tc - 1 - kk)
            h_r, h_i = hr[rows, :], hi[rows, :]
            dar = dar + lam_r * h_r + lam_i * h_i
            dai = dai + lam_i * h_r - lam_r * h_i
            new_r = wr[rows, :] + a_r * lam_r + a_i * lam_i
            new_i = wi[rows, :] + a_r * lam_i - a_i * lam_r
            wr[rows, :] = new_r
            wi[rows, :] = new_i
            return new_r, new_i, dar, dai

        carry[0], carry[1], carry[2], carry[3] = lax.fori_loop(0, tc, step, (carry[0], carry[1], carry[2], carry[3]),
                                                              unroll=8)
        dar_ref[...] = carry[2]
        dai_ref[...] = carry[3]
        for j in range(SSM_TILES):
            cj = _chan(j)
            lam_r, lam_i = _from_time_major(wr, j, tc), _from_time_major(wi, j, tc)
            dcr_ref[j] += _dot(dyv[:, cj], _from_time_major(hr, j, tc), 0, 0)
            dci_ref[j] += _dot(dyv[:, cj], _from_time_major(hi, j, tc), 0, 0)
            dbr_ref[j] += _dot(u[:, cj], lam_r, 0, 0)
            dbi_ref[j] += _dot(u[:, cj], lam_i, 0, 0)
            duj = _dot(lam_r, tur_ref[j], 1, 0) + _dot(lam_i, tui_ref[j], 1, 0) + g_ref[:, cj] * dyv[:, cj]
            du_ref[:, cj] = duj.astype(du_ref.dtype)
            su_ref[:, cj] += _colsum(duj)

        @pl.when(pl.program_id(0) == n_chunk - 1)
        def _():
            for src, dst_ref in ((dcr_ref, dc_re_ref), (dci_ref, dc_im_ref), (dbr_ref, db_re_ref), (dbi_ref, db_im_ref)):
                _extract_block_diagonal(src, dst_ref)

    back = lambda i: (n_chunk - 1 - i, 0)
    rows = pl.BlockSpec((tc, SSM_WIDTH), back)
    blocks = pl.BlockSpec((SSM_GROUPS, SSM_GROUP, SSM_STATE), lambda i: (0, 0, 0))
    coef = pl.BlockSpec((STATE_VREG_ROWS, LANES), lambda i: (0, 0))
    states = pl.BlockSpec((tc * STATE_VREG_ROWS, LANES), back)
    vec = pl.BlockSpec((1, SSM_WIDTH), lambda i: (0, 0))
    bshape = jax.ShapeDtypeStruct((SSM_GROUPS, SSM_GROUP, SSM_STATE), F32)
    cshape = jax.ShapeDtypeStruct((STATE_VREG_ROWS, LANES), F32)
    return pl.pallas_call(
        body, name=name, grid=(n_chunk,),
        in_specs=[rows, rows, states, states, _tile_spec(blocks_cn, 2), _tile_spec(blocks_cn, 3), _tile_spec(blocks_nc, 2),
                  _tile_spec(blocks_nc, 3), coef, coef, vec],
        out_specs=[rows, vec, blocks, blocks, blocks, blocks, coef, coef],
        out_shape=[jax.ShapeDtypeStruct((n_rows, SSM_WIDTH), MXU_DTYPE), jax.ShapeDtypeStruct((1, SSM_WIDTH), F32),
                   bshape, bshape, bshape, bshape, cshape, cshape],
        scratch_shapes=[pltpu.VMEM((tc * STATE_VREG_ROWS, LANES), F32)] * 2 + [pltpu.VMEM((4, STATE_VREG_ROWS, LANES), F32)]
        + [pltpu.VMEM((SSM_TILES, LANES, SSM_WIDE), MXU_DTYPE)] * 2 + [pltpu.VMEM((SSM_TILES, SSM_WIDE, LANES), MXU_DTYPE)] * 2
        + [pltpu.VMEM((SSM_TILES, LANES, SSM_WIDE), F32)] * 4,
        compiler_params=_params("arbitrary"),
    )(proj, dy, h_re, h_im, blocks_cn, blocks_cn, blocks_nc, blocks_nc, a_re, a_im, gain)


def _scan_rows(t):
    return pl.ds(pl.multiple_of(t * STATE_VREG_ROWS, 8), STATE_VREG_ROWS)


GATHER_GROUPS = (("w_glu", "w_att_up", "w_mix_out"), ("w_xq", "w_xkv", "w_xo", "w_ff1", "w_ff2"))
SCATTER_GROUPS = (("w_ff2", "w_ff1"), ("w_xo", "w_xq", "w_xkv", "w_mix_out"), ("w_att_up", "w_glu"), ("w_in",))


def _local_grads(x, mem, pos_col, target, sm, fetch_in, fetch, send, send_small, start_token):
    b_re_t = sm["ssm_b_re"].transpose(2, 0, 1)
    b_im_t = sm["ssm_b_im"].transpose(2, 0, 1)
    logdt = sm["ssm_log_dt"].reshape(SSM_GROUPS, 1)
    c_re, c_im = sm["ssm_c_re"], sm["ssm_c_im"]
    grp = (SSM_GROUPS, SSM_STATE)
    chn = (SSM_GROUP, SSM_GROUPS, SSM_STATE)

    wts = {}
    cos_t, sin_t = _rope_tables(pos_col, after=start_token, name="rope_tables")
    h0, xh0, rs0, h0m = _ln_fwd(x, None, sm["ln_in_g"], sm["ln_in_b"], alpha=1.0, name="ln_in_fwd")
    disc_in = (logdt, sm["ssm_a_re"], sm["ssm_a_im"], b_re_t, b_im_t)
    ab_re, ab_im, bb_re_t, bb_im_t = _whole(_disc, disc_in, [grp, grp, chn, chn], name="ssm_disc")
    a_re_rows, a_im_rows = ab_re.reshape(STATE_VREG_ROWS, LANES), ab_im.reshape(STATE_VREG_ROWS, LANES)
    tiles_cn = jnp.stack([bb_re_t.transpose(1, 0, 2), bb_im_t.transpose(1, 0, 2), c_re, -c_im])
    tiles_nc = jnp.stack([c_re.transpose(0, 2, 1), -c_im.transpose(0, 2, 1), bb_re_t.transpose(1, 2, 0),
                          bb_im_t.transpose(1, 2, 0)])
    w_in_near, near_ids = fetch_in(0, [h0m, tiles_cn, tiles_nc])
    proj = _mm_shards(h0m, w_in_near, sm["b_in"], near_ids, name="in_proj_near")
    wts["w_in"], far_ids = fetch_in(1, [proj])
    proj = _mm_shards(h0m, wts["w_in"], sm["b_in"], far_ids, prev=proj, name="in_proj_far")

    y, gy, h_re, h_im = _ssm_fwd(proj, tiles_cn, tiles_nc, a_re_rows, a_im_rows, sm["ssm_d"], name="ssm_fwd")

    q, k, v = _qkv_split(proj, cos_t, sin_t, name="qkv_split")
    outs, lses = [], []
    for g, dil in enumerate(DILATIONS):
        o_g, l_g = _dil_fwd(q[g], k[g], v[g], dil, name=f"dil_att_fwd_{dil}")
        outs.append(o_g)
        lses.append(l_g)
    att, lse = _att_merge(outs, lses, name="att_merge")
    wts.update(fetch(0, [att]))
    z = _mm(gy, wts["w_glu"], bias=sm["b_glu"], b_shards=True, name="glu_proj")
    b_att = _mm(att, wts["w_att_up"], b_shards=True, name="att_up")

    mixed = _mix_fwd(proj, z, b_att, name="gate_mix")
    h1, xh1, rs1, h1m = _mm_ln_fwd(mixed, wts["w_mix_out"], sm["b_mix_out"], h0, sm["ln1_g"], sm["ln1_b"],
                                   alpha=DEEPNORM_ALPHA, name="mix_out_ln1")

    wts.update(fetch(1, [h1m]))
    xq = _mm(h1m, wts["w_xq"], out_dtype=MXU_DTYPE, name="xatt_q")
    kv = _mm(mem, wts["w_xkv"], out_dtype=MXU_DTYPE, b_shards=True, name="xatt_kv")
    xo_in = _xatt_fwd(xq, kv, name="xatt_fwd")
    h2, xh2, rs2, h2m = _mm_ln_fwd(xo_in, wts["w_xo"], None, h1, sm["ln2_g"], sm["ln2_b"], alpha=DEEPNORM_ALPHA,
                                   name="xatt_o_ln2")

    pre, act = _mm(h2m, wts["w_ff1"], bias=sm["b_ff1"], b_shards=True, name="ff1",
                   also=(lambda r: jnp.square(jnp.maximum(r, 0.0)), MXU_DTYPE))
    ff = _mm(act, wts["w_ff2"], bias=sm["b_ff2"], name="ff2")

    gw, gs = {}, {}
    dr3, dr3m, gs["ln3_g"], gs["ln3_b"], gs["b_ff2"], loss_row = _ln_loss_bwd(
        h2, ff, target, sm["ln3_g"], sm["ln3_b"], alpha=DEEPNORM_ALPHA, name="ln3_loss")
    wgrad = functools.partial(_mm, ta=True, out_dtype=WIRE_DTYPE, tk=2048)
    gw["w_ff2"] = wgrad(act, dr3m, tk=1024, name="ff2_dw")
    dpre, gs["b_ff1"] = _mm(dr3m, wts["w_ff2"], tb=True, out_dtype=MXU_DTYPE, colsum=True, name="ff2_dx",
                            gate=(pre, lambda p: 2.0 * jnp.maximum(p, 0.0)))
    gw["w_ff1"] = wgrad(h2m, dpre, out_shards=True, name="ff1_dw")
    sent = send(0, gw)
    dr2, dr2m, gs["ln2_g"], gs["ln2_b"], _ = _mm_ln_bwd(
        dpre, wts["w_ff1"], dr3, xh2, rs2, sm["ln2_g"], alpha=DEEPNORM_ALPHA, b_shards=True, after=sent,
        name="ff1_dx_ln2")
    gw["w_xo"] = wgrad(xo_in, dr2m, name="xatt_o_dw")
    dxo_in = _mm(dr2m, wts["w_xo"], tb=True, out_dtype=MXU_DTYPE, name="xatt_o_dx")
    dxq, dkv = _xatt_bwd(xq, kv, dxo_in, name="xatt_bwd")
    gw["w_xq"] = wgrad(h1m, dxq, name="xatt_q_dw")
    gw["w_xkv"] = wgrad(mem, dkv, out_shards=True, name="xatt_kv_dw")
    dr1, dr1m, gs["ln1_g"], gs["ln1_b"], gs["b_mix_out"] = _mm_ln_bwd(
        dxq, wts["w_xq"], dr2, xh1, rs1, sm["ln1_g"], alpha=DEEPNORM_ALPHA, name="xatt_q_dx_ln1")
    gw["w_mix_out"] = wgrad(mixed, dr1m, name="mix_out_dw")
    sent = send(1, gw)
    dmixed = _mm(dr1m, wts["w_mix_out"], tb=True, after=sent, name="mix_out_dx")
    dgs, dga, dz, db_att, s_gs, s_ga, gs["b_glu"] = _mix_bwd(dmixed, proj, z, b_att, name="gate_mix_bwd")

    gw["w_att_up"] = wgrad(att, db_att, out_shards=True, name="att_up_dw")
    gw["w_glu"] = wgrad(gy, dz, out_shards=True, name="glu_dw")
    sent = send(2, gw)
    datt = _mm(db_att, wts["w_att_up"], tb=True, b_shards=True, after=sent, name="att_up_dx")
    stats = _att_stats(datt, att, lse, name="att_stats")
    dqkv = [_dil_bwd(q[g], k[g], v[g], datt, stats, dil, name=f"dil_att_bwd_{dil}") for g, dil in enumerate(DILATIONS)]

    dgy = _mm(dz, wts["w_glu"], tb=True, b_shards=True, name="glu_dx")
    dy, gs["ssm_d"] = _gelu_bwd(dgy, y, proj, name="gelu_bwd")
    du, s_u, dc_re_t, dc_im_t, dbb_re_t, dbb_im_t, da_re, da_im = _ssm_bwd(
        proj, dy, h_re, h_im, tiles_cn, tiles_nc, a_re_rows, a_im_rows, sm["ssm_d"], name="ssm_bwd")
    gs["ssm_c_re"], gs["ssm_c_im"] = dc_re_t, -dc_im_t
    disc_ct = (da_re.reshape(grp), da_im.reshape(grp), dbb_re_t.transpose(1, 0, 2), dbb_im_t.transpose(1, 0, 2))
    d_logdt, gs["ssm_a_re"], gs["ssm_a_im"], d_b_re_t, d_b_im_t = _whole(
        _disc_transpose, disc_in + disc_ct, [(SSM_GROUPS, 1), grp, grp, chn, chn], name="ssm_disc_bwd")
    gs["ssm_log_dt"] = d_logdt
    gs["ssm_b_re"], gs["ssm_b_im"] = d_b_re_t.transpose(1, 2, 0), d_b_im_t.transpose(1, 2, 0)

    dproj, s_qkv = _dproj_assemble(du, dqkv, dgs, dga, cos_t, sin_t, name="dproj_assemble")
    gs["b_in"] = jnp.concatenate([s_u, *s_qkv, s_gs, s_ga], axis=1)
    sent = send_small(gs, SMALL_EARLY)
    gw["w_in"] = wgrad(h0m, dproj, out_shards=True, after=sent, name="in_proj_dw")
    sent = send(3, gw)
    dh0 = _mm(dproj, wts["w_in"], tb=True, b_shards=True, after=sent, name="in_proj_dx")
    grad_x, gs["ln_in_g"], gs["ln_in_b"], _ = _ln_bwd(dr1, dh0, xh0, rs0, sm["ln_in_g"], alpha=DEEPNORM_ALPHA,
                                                      operand=False, name="ln_in_bwd")
    return loss_row, grad_x, gs


N_PEER = N_DEV - 1
_IN_HBM = pl.BlockSpec(memory_space=pltpu.HBM)
_IN_SEMAPHORE = pl.BlockSpec(memory_space=pltpu.SEMAPHORE)


def _device_index():
    return 4 * lax.axis_index("x") + 2 * lax.axis_index("y") + lax.axis_index("c")


ALL_PEERS = tuple(range(1, N_DEV))
NEAR_PEERS = (1, 2, 3, 4, 5)
FAR_PEERS = (6, 7)


def _peer_index(kk):
    x, y, c = lax.axis_index("x"), lax.axis_index("y"), lax.axis_index("c")
    return 4 * ((x + (kk >> 2)) % 2) + 2 * ((y + ((kk >> 1) & 1)) % 2) + (c + (kk & 1)) % 2


def _exchange_copies(src_refs, land_refs, send_sems, recv_sems, scatter, peers):
    x, y, c = lax.axis_index("x"), lax.axis_index("y"), lax.axis_index("c")
    me = 4 * x + 2 * y + c
    pairs = []
    for a, (src_ref, land_ref) in enumerate(zip(src_refs, land_refs)):
        for idx, kk in enumerate(peers):
            px = (x + (kk >> 2)) % 2
            py = (y + ((kk >> 1) & 1)) % 2
            pc = (c + (kk & 1)) % 2
            peer = 4 * px + 2 * py + pc
            sem = a * len(peers) + idx
            src = src_ref.at[peer] if scatter else src_ref

            def copy(dst, src=src, sem=sem, px=px, py=py, pc=pc):
                return pltpu.make_async_remote_copy(
                    src_ref=src, dst_ref=dst, send_sem=send_sems.at[sem], recv_sem=recv_sems.at[sem],
                    device_id=(px, py, pc), device_id_type=pl.DeviceIdType.MESH)

            pairs.append((functools.partial(copy, land_ref.at[me]), functools.partial(copy, land_ref.at[peer])))
    return pairs


def _own_copies(src_refs, land_refs, own_sems, scatter):
    me = _device_index()
    return [functools.partial(pltpu.make_async_copy, src_ref.at[me] if scatter else src_ref, land_ref.at[me],
                              own_sems.at[a]) for a, (src_ref, land_ref) in enumerate(zip(src_refs, land_refs))]


def _exchange_start(srcs, *, scatter, name, after=None, peers=ALL_PEERS, lands=None):
    n_arr, n_sem = len(srcs), len(srcs) * len(peers)
    own = lands is None
    if own:
        lands = [lax.empty((N_DEV,) + tuple(s.shape[1:] if scatter else s.shape), s.dtype) for s in srcs]
    n_in = 2 * n_arr + (after is not None)

    def body(*refs):
        send_sems, recv_sems = refs[n_in], refs[n_in + 1]
        for sent, _ in _exchange_copies(refs[:n_arr], refs[n_arr:2 * n_arr], send_sems, recv_sems, scatter, peers):
            sent().start()
        if own:
            for local in _own_copies(refs[:n_arr], refs[n_arr:2 * n_arr], refs[n_in + 2], scatter):
                local().start()
        refs[-1][...] = jnp.zeros_like(refs[-1])

    sems = [pltpu.SemaphoreType.DMA((n_sem,)), pltpu.SemaphoreType.DMA((n_sem,))] + [pltpu.SemaphoreType.DMA((n_arr,))] * own
    through = [pltpu.HBM(t.shape, t.dtype) for t in (*srcs, *lands)]
    res = pl.pallas_call(
        body, name=name, out_shape=(*sems, *through, jax.ShapeDtypeStruct((8, LANES), F32)),
        in_specs=[_IN_HBM] * (2 * n_arr) + [pl.BlockSpec(memory_space=pl.ANY)] * (after is not None),
        out_specs=(*[_IN_SEMAPHORE] * len(sems), *[_IN_HBM] * (2 * n_arr), pl.BlockSpec(memory_space=pltpu.VMEM)),
        input_output_aliases={i: len(sems) + i for i in range(2 * n_arr)},
        compiler_params=pltpu.CompilerParams(has_side_effects=pltpu.SideEffectType.DATAFLOW_SIDE_EFFECTING),
    )(*[pltpu.with_memory_space_constraint(t, pltpu.HBM) for t in (*srcs, *lands)],
      *([after] if after is not None else []))
    first = len(sems)
    handle = dict(sems=res[:first], srcs=res[first:first + n_arr], lands=res[first + n_arr:first + 2 * n_arr],
                  scatter=scatter, peers=peers, own=own)
    return handle, res[-1]


def _exchange_wait(handle, *, after, name, srcs=None, lands=None):
    srcs = handle["srcs"] if srcs is None else srcs
    lands = handle["lands"] if lands is None else lands
    sems, scatter, peers, own = handle["sems"], handle["scatter"], handle["peers"], handle["own"]
    n_arr = len(srcs)
    after = list(after)

    def body(*refs):
        src_refs, land_refs = refs[:n_arr], refs[n_arr:2 * n_arr]
        for sent, received in _exchange_copies(src_refs, land_refs, refs[2 * n_arr], refs[2 * n_arr + 1], scatter, peers):
            sent().wait_send()
            received().wait_recv()
        if own:
            for local in _own_copies(src_refs, land_refs, refs[2 * n_arr + 2], scatter):
                local().wait()

    res = pl.pallas_call(
        body, name=name, out_shape=tuple(pltpu.HBM(t.shape, t.dtype) for t in (*srcs, *lands)),
        in_specs=[_IN_HBM] * (2 * n_arr) + [_IN_SEMAPHORE] * len(sems) + [pl.BlockSpec(memory_space=pl.ANY)] * len(after),
        out_specs=tuple([_IN_HBM] * (2 * n_arr)), input_output_aliases={i: i for i in range(2 * n_arr)},
        compiler_params=pltpu.CompilerParams(has_side_effects=pltpu.SideEffectType.DATAFLOW_SIDE_EFFECTING),
    )(*srcs, *lands, *sems, *after)
    return res[:n_arr], res[n_arr:]


def _adamw(g, w, m, v):
    m_new = ADAM_B1 * m + (1.0 - ADAM_B1) * g
    v_new = ADAM_B2 * v + (1.0 - ADAM_B2) * jnp.square(g)
    m_hat = m_new / (1.0 - ADAM_B1 ** ADAM_STEP)
    v_hat = v_new / (1.0 - ADAM_B2 ** ADAM_STEP)
    return g, -ADAM_LR * (m_hat / (jnp.sqrt(v_hat) + ADAM_EPS) + ADAM_WD * w), m_new, v_new


def _reduce_adamw(gstack, w, m, v, *, name, tr=128):
    n_rows, cols = w.shape
    tr = min(tr, n_rows)
    assert n_rows % tr == 0, (name, n_rows, tr)

    def body(g_ref, w_ref, m_ref, v_ref, *out_refs):
        g = g_ref[0].astype(F32)
        for dev in range(1, N_DEV):
            g = g + g_ref[dev].astype(F32)
        for o_ref, val in zip(out_refs, _adamw(g, w_ref[...], m_ref[...], v_ref[...])):
            o_ref[...] = val

    flat = pl.BlockSpec((tr, cols), lambda i: (i, 0))
    shape = jax.ShapeDtypeStruct((n_rows, cols), F32)
    return pl.pallas_call(
        body, name=name, grid=(n_rows // tr,),
        in_specs=[pl.BlockSpec((N_DEV, tr, cols), lambda i: (0, i, 0)), flat, flat, flat],
        out_specs=[flat] * 4, out_shape=[shape] * 4, compiler_params=_params("parallel"),
    )(gstack, w, m, v)


SMALL_FLAT_SSM = ("ssm_b_re", "ssm_b_im", "ssm_c_re", "ssm_c_im")


def _small_view(name, shape):
    size = int(np.prod(shape))
    if name in SMALL_FLAT_SSM:
        return SSM_GROUPS, size // SSM_GROUPS
    if name in ("ssm_a_re", "ssm_a_im"):
        return SSM_GROUPS, SSM_STATE
    return 1, size


def _pack_rows(view):
    return -(-(view[0] * view[1]) // PACK_COLS)


SMALL_LATE = ("ln_in_g", "ln_in_b")
SMALL_EARLY = tuple(n for n in SMALL if n not in SMALL_LATE)


def _pack_small(gs, names, views):
    parts = []
    for n in names:
        flat = gs[n].reshape(-1).astype(WIRE_DTYPE)
        parts.append(jnp.pad(flat, (0, _pack_rows(views[n]) * PACK_COLS - flat.shape[0])))
    total = sum(p.shape[0] for p in parts) // PACK_COLS
    parts.append(jnp.zeros(((-total % PACK_ROW_ALIGN) * PACK_COLS,), WIRE_DTYPE))
    return jnp.concatenate(parts).reshape(-1, PACK_COLS)


def _small_pieces(view):
    rows, cols = view
    if cols == PACK_COLS:
        return [(0, rows, 0, 0, 0, cols)]
    if rows == 1 and cols > PACK_COLS:
        return [(kk, 1, 0, 0, kk * PACK_COLS, PACK_COLS) for kk in range(cols // PACK_COLS)]
    if rows == 1:
        return [(0, 1, 0, 0, 0, cols)]
    return [((r * cols) // PACK_COLS, 1, (r * cols) % PACK_COLS, r, 0, cols) for r in range(rows)]


def _adamw_small(stacks, views, w, m, v, *, name):
    n = len(SMALL)
    place, first = {}, [0, 0]
    for k, names in enumerate((SMALL_EARLY, SMALL_LATE)):
        for name_ in names:
            place[name_] = (k, first[k])
            first[k] += _pack_rows(views[name_])

    def body(early_ref, late_ref, *refs):
        ins, outs = refs[:3 * n], refs[3 * n:]
        for i, name_ in enumerate(SMALL):
            stack_ref = (early_ref, late_ref)[place[name_][0]]
            row0 = place[name_][1]
            for prow, nrows, lane, orow, ocol, width in _small_pieces(views[name_]):
                src = (slice(row0 + prow, row0 + prow + nrows), slice(lane, lane + width))
                dst = (slice(orow, orow + nrows), slice(ocol, ocol + width))
                g = stack_ref[(0,) + src].astype(F32)
                for dev in range(1, N_DEV):
                    g = g + stack_ref[(dev,) + src].astype(F32)
                res = _adamw(g, ins[i][dst], ins[n + i][dst], ins[2 * n + i][dst])
                for kk, val in enumerate(res):
                    outs[kk * n + i][dst] = val

    args = [*stacks, *[d[name_] for d in (w, m, v) for name_ in SMALL]]
    out_views = [views[name_] for _ in range(4) for name_ in SMALL]
    res = pl.pallas_call(
        body, name=name, grid=(1,), in_specs=[_full_spec(t.shape) for t in args],
        out_specs=[_full_spec(s) for s in out_views], out_shape=[jax.ShapeDtypeStruct(s, F32) for s in out_views],
        compiler_params=_params("arbitrary"),
    )(*args)
    return [dict(zip(SMALL, res[kk * n:(kk + 1) * n])) for kk in range(4)]


def kernel(x, mem, positions, ln_in_g, ln_in_b, w_in, b_in, ssm_log_dt, ssm_a_re, ssm_a_im, ssm_b_re, ssm_b_im, ssm_c_re, ssm_c_im, ssm_d, w_glu, b_glu, w_att_up, w_mix_out, b_mix_out, ln1_g, ln1_b, w_xq, w_xkv, w_xo, ln2_g, ln2_b, w_ff1, b_ff1, w_ff2, b_ff2, ln3_g, ln3_b, loss_target, m_ln_in_g, m_ln_in_b, m_w_in, m_b_in, m_ssm_log_dt, m_ssm_a_re, m_ssm_a_im, m_ssm_b_re, m_ssm_b_im, m_ssm_c_re, m_ssm_c_im, m_ssm_d, m_w_glu, m_b_glu, m_w_att_up, m_w_mix_out, m_b_mix_out, m_ln1_g, m_ln1_b, m_w_xq, m_w_xkv, m_w_xo, m_ln2_g, m_ln2_b, m_w_ff1, m_b_ff1, m_w_ff2, m_b_ff2, m_ln3_g, m_ln3_b, v_ln_in_g, v_ln_in_b, v_w_in, v_b_in, v_ssm_log_dt, v_ssm_a_re, v_ssm_a_im, v_ssm_b_re, v_ssm_b_im, v_ssm_c_re, v_ssm_c_im, v_ssm_d, v_w_glu, v_b_glu, v_w_att_up, v_w_mix_out, v_b_mix_out, v_ln1_g, v_ln1_b, v_w_xq, v_w_xkv, v_w_xo, v_ln2_g, v_ln2_b, v_w_ff1, v_b_ff1, v_w_ff2, v_b_ff2, v_ln3_g, v_ln3_b):
    given = dict(locals())
    w_arg = {n: given[n] for n in WEIGHTS}
    m_arg = {n: given["m_" + n] for n in WEIGHTS}
    v_arg = {n: given["v_" + n] for n in WEIGHTS}

    in_near, token = _exchange_start([w_arg["w_in"][0].astype(MXU_DTYPE)], scatter=False, peers=NEAR_PEERS,
                                     name="gather_start_in_near")
    in_far, token = _exchange_start(in_near["srcs"], scatter=False, peers=FAR_PEERS, lands=in_near["lands"],
                                    after=token, name="gather_start_in_far")
    w_in_state = [in_far["srcs"], in_far["lands"]]
    token, w_arg, m_arg, v_arg = lax.optimization_barrier((token, w_arg, m_arg, v_arg))
    shards = {n: w_arg[n][0].astype(MXU_DTYPE) for n in BIG if n != "w_in"}
    gathers = []
    for i, names in enumerate(GATHER_GROUPS):
        handle, token = _exchange_start([shards[n] for n in names], scatter=False, after=token, name=f"gather_start_{i}")
        gathers.append(handle)

    small_views = {n: _small_view(n, w_arg[n].shape) for n in SMALL}
    small_w, small_m, small_v = [{n: d[n].reshape(small_views[n]) for n in SMALL} for d in (w_arg, m_arg, v_arg)]
    relaid = [d[n] for d in (small_w, small_m, small_v) for n in SMALL_FLAT_SSM]

    def fetch_in(part, after):
        handle, peers, tag = ((in_near, (0,) + NEAR_PEERS, "near"), (in_far, FAR_PEERS, "far"))[part]
        w_in_state[:] = _exchange_wait(handle, after=after + (relaid if part == 0 else []), srcs=w_in_state[0],
                                       lands=w_in_state[1], name="gather_wait_in_" + tag)
        return w_in_state[1][0], jnp.stack([_peer_index(kk) for kk in peers]).astype(jnp.int32)

    def fetch(i, after):
        _, lands = _exchange_wait(gathers[i], after=after, name=f"gather_wait_{i}")
        full = dict(zip(GATHER_GROUPS[i], lands))
        return {n: t if n in BIG_COL_SHARDED else t.reshape(-1, t.shape[-1]) for n, t in full.items()}

    scatters = {}

    def send(i, gw):
        slots = [gw[n] if n in BIG_COL_SHARDED else gw[n].reshape(N_DEV, -1, gw[n].shape[-1]) for n in SCATTER_GROUPS[i]]
        handle, sent = _exchange_start(slots, scatter=True, name=f"scatter_start_{i}")
        scatters[i] = (handle, slots)
        return sent

    sm = {}
    for n in SMALL:
        t = w_arg[n]
        if n.startswith("ssm_") and n not in ("ssm_d", "ssm_log_dt"):
            sm[n] = t[0]
        else:
            sm[n] = t.reshape(1, -1)

    smalls = []

    def send_small(gs, names):
        handle, sent = _exchange_start([_pack_small(gs, names, small_views)], scatter=False,
                                       name=f"small_start_{len(smalls)}")
        smalls.append(handle)
        return sent

    loss_row, grad_x, gs = _local_grads(x[0], mem[0], positions.reshape(-1, 1), loss_target[0], sm, fetch_in, fetch,
                                        send, send_small, token)
    loss = lax.psum(loss_row[0, 0], ("x", "y", "c"))
    send_small(gs, SMALL_LATE)

    results = [{}, {}, {}, {}]
    done = grad_x
    for i, names in enumerate(SCATTER_GROUPS):
        handle, slots = scatters[i]
        _, lands = _exchange_wait(handle, after=[done], name=f"scatter_wait_{i}")
        for n, land, slot in zip(names, lands, slots):
            res = _reduce_adamw(land, w_arg[n][0], m_arg[n][0], v_arg[n][0], name="adamw_" + n)
            done = res[0]
            for d, r in zip(results, res):
                d[n] = r[None]
    stacks = [_exchange_wait(handle, after=[done], name=f"small_wait_{i}")[1][0] for i, handle in enumerate(smalls)]
    res = _adamw_small(stacks, small_views, small_w, small_m, small_v, name="adamw_small")
    for d, r in zip(results, res):
        d.update({n: r[n].reshape(w_arg[n].shape) for n in SMALL})
    out = [loss, grad_x[None]]
    for d in results:
        out += [d[n] for n in WEIGHTS]
    return tuple(out)
```

```python
import functools

import numpy as np
import jax
import jax.numpy as jnp
from jax import lax
from jax.experimental import pallas as pl
from jax.experimental.pallas import tpu as pltpu

F32 = jnp.float32
MXU_DTYPE = jnp.bfloat16
WIRE_DTYPE = jnp.bfloat16
VMEM_LIMIT_BYTES = 48 * 1024 * 1024
LANES = 128

N_DEV = 8
D_MODEL = 1024
SSM_GROUP = 16
SSM_WIDTH = 768
SSM_GROUPS = SSM_WIDTH // SSM_GROUP
SSM_STATE = 64
SSM_CH = SSM_GROUPS * SSM_STATE
SSM_TILES = SSM_WIDTH // LANES
GROUPS_PER_TILE = LANES // SSM_GROUP
STATE_VREG_ROWS = SSM_CH // LANES
ATT_HEAD_DIM = 64
ATT_HEADS_PER_GROUP = 4
ATT_MERGED = ATT_HEADS_PER_GROUP * ATT_HEAD_DIM
LANE_HALVES = ATT_MERGED // LANES
DILATIONS = (1, 4, 16)
ATT_BLK = 128
ATT_SCALE = ATT_HEAD_DIM ** -0.5
ROT_DIM = ATT_HEAD_DIM // 4
ROPE_THETA = 500000.0
XATT_HEADS = 4
XATT_HEAD_DIM = D_MODEL // XATT_HEADS
XATT_SCALE = XATT_HEAD_DIM ** -0.5
DEEPNORM_ALPHA = 2.0 ** 0.25
LN_EPS = 1e-5
NEG_INF = -1e30
OFF_Q_BLK, OFF_K_BLK, OFF_V_BLK = 3, 6, 9
OFF_GS_BLK, OFF_GA_BLK = 3, 4

ADAM_LR = 0.001
ADAM_B1 = 0.9
ADAM_B2 = 0.999
ADAM_EPS = 1e-08
ADAM_WD = 0.01
ADAM_STEP = 10

BIG = ("w_in", "w_glu", "w_att_up", "w_mix_out", "w_xq", "w_xkv", "w_xo", "w_ff1", "w_ff2")
BIG_COL_SHARDED = ("w_in", "w_glu", "w_att_up", "w_xkv", "w_ff1")
WEIGHTS = ("ln_in_g", "ln_in_b", "w_in", "b_in", "ssm_log_dt", "ssm_a_re", "ssm_a_im", "ssm_b_re", "ssm_b_im",
           "ssm_c_re", "ssm_c_im", "ssm_d", "w_glu", "b_glu", "w_att_up", "w_mix_out", "b_mix_out", "ln1_g", "ln1_b",
           "w_xq", "w_xkv", "w_xo", "ln2_g", "ln2_b", "w_ff1", "b_ff1", "w_ff2", "b_ff2", "ln3_g", "ln3_b")
SMALL = tuple(n for n in WEIGHTS if n not in BIG)
PACK_COLS = 1024
PACK_ROW_ALIGN = 16


def _params(*sem):
    return pltpu.CompilerParams(dimension_semantics=sem, vmem_limit_bytes=VMEM_LIMIT_BYTES)


def _dot(a, b, ca, cb):
    return lax.dot_general(a.astype(MXU_DTYPE), b.astype(MXU_DTYPE), (((ca,), (cb,)), ((), ())),
                           preferred_element_type=F32)


def _fit(dim, pref):
    if dim <= pref:
        return dim
    best = max(t for t in range(LANES, pref + 1, LANES) if dim % t == 0)
    return best


def _mm(a, b, *, name, ta=False, tb=False, bias=None, out_dtype=F32, b_shards=False, out_shards=False, after=None,
        also=None, gate=None, colsum=False, epilogue=None, tm=2048, tn=1024, tk=1024):
    m, k = (a.shape[1], a.shape[0]) if ta else a.shape
    order = (lambda f: (lambda j, i, kk: f(i, j, kk))) if colsum else (lambda f: f)
    spec = lambda shape, f: pl.BlockSpec(shape, order(f))
    if b_shards:
        n_sh, rows, n_loc = b.shape
        if tb:
            n, tn, tk = rows, _fit(rows, tn), n_loc
            assert k == n_sh * n_loc, (name, k, b.shape)
            b_spec = spec((1, tn, tk), lambda i, j, kk: (kk, j, 0))
        else:
            n, tn, tk = n_sh * n_loc, n_loc, _fit(k, tk)
            b_spec = spec((1, tk, tn), lambda i, j, kk: (j, kk, 0))
    else:
        n = b.shape[0] if tb else b.shape[1]
        tn = n // N_DEV if out_shards else _fit(n, tn)
        tk = _fit(k, tk)
        b_spec = spec((tn, tk), lambda i, j, kk: (j, kk)) if tb else spec((tk, tn), lambda i, j, kk: (kk, j))
    tm = _fit(m, tm)
    nk = k // tk
    a_spec = spec((tk, tm), lambda i, j, kk: (kk, i)) if ta else spec((tm, tk), lambda i, j, kk: (i, kk))
    tile = spec((tm, tn), lambda i, j, kk: (i, j))
    in_specs, args = [a_spec, b_spec], [a, b]
    if bias is not None:
        in_specs.append(spec((1, tn), lambda i, j, kk: (0, j)))
        args.append(bias)
    if gate is not None:
        in_specs.append(tile)
        args.append(gate[0])
    if after is not None:
        in_specs.append(pl.BlockSpec(memory_space=pl.ANY))
        args.append(after)
    if epilogue is not None:
        ep_fn, ep_rows, ep_fulls, ep_row_outs, ep_acc_outs = epilogue
        assert tn == n and not (colsum or also or gate or out_shards), name
        ep_first = len(args)
        in_specs += [spec((tm, t.shape[1]), lambda i, j, kk: (i, 0)) for t in ep_rows]
        in_specs += [pl.BlockSpec(t.shape, functools.partial(lambda i, j, kk, nd: (0,) * nd, nd=t.ndim)) for t in ep_fulls]
        args += [*ep_rows, *ep_fulls]
    n_in = len(args)
    if epilogue is not None:
        out_specs = [spec((tm, w), lambda i, j, kk: (i, 0)) for w, _ in ep_row_outs]
        out_specs += [spec((1, w), lambda i, j, kk: (0, 0)) for w in ep_acc_outs]
        out_shape = [jax.ShapeDtypeStruct((m, w), dt) for w, dt in ep_row_outs]
        out_shape += [jax.ShapeDtypeStruct((1, w), F32) for w in ep_acc_outs]
    elif out_shards:
        assert n == N_DEV * tn, (name, n, tn)
        out_specs = [spec((1, tm, tn), lambda i, j, kk: (j, i, 0))]
        out_shape = [jax.ShapeDtypeStruct((N_DEV, m, tn), out_dtype)]
    else:
        out_specs = [tile]
        out_shape = [jax.ShapeDtypeStruct((m, n), out_dtype)]
    if also is not None:
        out_specs.append(tile)
        out_shape.append(jax.ShapeDtypeStruct((m, n), also[1]))
    if colsum:
        out_specs.append(spec((1, tn), lambda i, j, kk: (0, j)))
        out_shape.append(jax.ShapeDtypeStruct((1, n), F32))

    def body(*refs):
        a_ref, b_ref = refs[0], refs[1]
        o_ref = refs[n_in]
        first_row_tile = pl.program_id(1 if colsum else 0) == 0

        def product():
            return _dot(a_ref[...], b_ref[0] if b_shards else b_ref[...], 0 if ta else 1, 1 if tb else 0)

        def finish(r):
            if bias is not None:
                r = r + refs[2][...]
            if gate is not None:
                r = r * gate[1](refs[2 + (bias is not None)][...])
            if epilogue is not None:
                res = ep_fn(r, *[ref[...] for ref in refs[ep_first:n_in]])
                n_o = len(ep_row_outs)
                for ref, val in zip(refs[n_in:n_in + n_o], res[:n_o]):
                    ref[...] = val.astype(ref.dtype)
                acc_refs = refs[n_in + n_o:n_in + n_o + len(ep_acc_outs)]
                if acc_refs:
                    @pl.when(first_row_tile)
                    def _():
                        for ref in acc_refs:
                            ref[...] = jnp.zeros_like(ref)

                    for ref, val in zip(acc_refs, res[n_o:]):
                        ref[...] += val
                return
            if out_shards:
                o_ref[0] = r.astype(o_ref.dtype)
            else:
                o_ref[...] = r.astype(o_ref.dtype)
            if also is not None:
                refs[n_in + 1][...] = also[0](r).astype(also[1])
            if colsum:
                s_ref = refs[n_in + 1 + (also is not None)]

                @pl.when(first_row_tile)
                def _():
                    s_ref[...] = jnp.zeros_like(s_ref)

                s_ref[...] += _colsum(r)

        if nk == 1:
            finish(product())
            return
        acc_ref = refs[-1]
        kk = pl.program_id(2)

        @pl.when(kk == 0)
        def _():
            acc_ref[...] = jnp.zeros_like(acc_ref)

        acc_ref[...] += product()

        @pl.when(kk == nk - 1)
        def _():
            finish(acc_ref[...])

    grid = (n // tn, m // tm, nk) if colsum else (m // tm, n // tn, nk)
    res = pl.pallas_call(
        body, name=name, grid=grid, in_specs=in_specs, out_specs=out_specs, out_shape=out_shape,
        scratch_shapes=[pltpu.VMEM((tm, tn), F32)] if nk > 1 else [],
        compiler_params=_params("arbitrary" if epilogue is not None else "parallel",
                                "arbitrary" if colsum else "parallel", "arbitrary"),
    )(*args)
    return res[0] if len(res) == 1 else res


def _mm_shards(a, w, bias, shard_ids, *, name, prev=None, tm=2048):
    m, k = a.shape
    n_sh, _, n_loc = w.shape
    tm = _fit(m, tm)

    def body(ids_ref, a_ref, w_ref, b_ref, *rest):
        rest[-1][...] = _dot(a_ref[...], w_ref[0], 1, 0) + b_ref[...]

    grid_spec = pltpu.PrefetchScalarGridSpec(
        num_scalar_prefetch=1, grid=(m // tm, shard_ids.shape[0]),
        in_specs=[pl.BlockSpec((tm, k), lambda i, j, ids: (i, 0)),
                  pl.BlockSpec((1, k, n_loc), lambda i, j, ids: (ids[j], 0, 0)),
                  pl.BlockSpec((1, n_loc), lambda i, j, ids: (0, ids[j]))]
        + [pl.BlockSpec(memory_space=pl.ANY)] * (prev is not None),
        out_specs=pl.BlockSpec((tm, n_loc), lambda i, j, ids: (i, ids[j])))
    return pl.pallas_call(
        body, name=name, grid_spec=grid_spec, out_shape=jax.ShapeDtypeStruct((m, n_sh * n_loc), F32),
        input_output_aliases={4: 0} if prev is not None else {}, compiler_params=_params("parallel", "arbitrary"),
    )(shard_ids, a, w, bias, *([prev] if prev is not None else []))


def _rowcall(fn, rows, fulls, row_outs, acc_outs=(), *, n_rows, tm, name, after=None):
    n_r, n_f, n_o, n_a = len(rows), len(fulls), len(row_outs), len(acc_outs)
    n_in = n_r + n_f + (after is not None)
    assert n_rows % tm == 0, (name, n_rows, tm)

    def body(*refs):
        res = fn(*[r[...] for r in refs[:n_r + n_f]])
        res = tuple(res) if isinstance(res, (tuple, list)) else (res,)
        o_refs = refs[n_in:n_in + n_o]
        a_refs = refs[n_in + n_o:]
        for o_ref, val in zip(o_refs, res[:n_o]):
            o_ref[...] = val.astype(o_ref.dtype)
        if n_a:
            @pl.when(pl.program_id(0) == 0)
            def _():
                for a_ref in a_refs:
                    a_ref[...] = jnp.zeros_like(a_ref)

            for a_ref, val in zip(a_refs, res[n_o:]):
                a_ref[...] += val

    in_specs = [pl.BlockSpec((tm, w), functools.partial(lambda i, cb: (i, cb), cb=cb)) for _, w, cb in rows]
    in_specs += [pl.BlockSpec(f.shape, functools.partial(lambda i, nd: (0,) * nd, nd=f.ndim)) for f in fulls]
    in_specs += [pl.BlockSpec(memory_space=pl.ANY)] * (after is not None)
    out_specs = [pl.BlockSpec((tm, w), lambda i: (i, 0)) for w, _ in row_outs]
    out_specs += [pl.BlockSpec((1, w), lambda i: (0, 0)) for w in acc_outs]
    out_shape = [jax.ShapeDtypeStruct((n_rows, w), dt) for w, dt in row_outs]
    out_shape += [jax.ShapeDtypeStruct((1, w), F32) for w in acc_outs]
    return pl.pallas_call(
        body, name=name, grid=(n_rows // tm,), in_specs=in_specs, out_specs=out_specs, out_shape=out_shape,
        compiler_params=_params("arbitrary" if n_a else "parallel"),
    )(*[r[0] for r in rows], *fulls, *([after] if after is not None else []))


def _colsum(v):
    return jnp.sum(v, axis=0, keepdims=True)


def _ln_fwd(a, r, g, b, *, alpha, name):
    n_rows, d = a.shape

    def fn(*t):
        xin = t[0] if alpha == 1.0 else alpha * t[0]
        if r is not None:
            xin = xin + t[1]
        gv, bv = t[-2], t[-1]
        mu = jnp.mean(xin, axis=-1, keepdims=True)
        xc = xin - mu
        var = jnp.mean(xc * xc, axis=-1, keepdims=True)
        rstd = lax.rsqrt(var + LN_EPS)
        xh = xc * rstd
        y = xh * gv + bv
        return y, xh, rstd, y

    rows = [(a, d, 0)] + ([(r, d, 0)] if r is not None else [])
    return _rowcall(fn, rows, [g, b], [(d, F32), (d, F32), (1, F32), (d, MXU_DTYPE)], n_rows=n_rows, tm=256, name=name)


def _ln_bwd(dya, dyb, xh, rstd, g, *, alpha, name, operand=True):
    n_rows, d = xh.shape

    def fn(da, db, xhv, rs, gv):
        dy = alpha * da + db
        dyg = dy * gv
        m1 = jnp.mean(dyg, axis=-1, keepdims=True)
        m2 = jnp.mean(dyg * xhv, axis=-1, keepdims=True)
        dx = rs * (dyg - m1 - xhv * m2)
        return (dx,) + ((dx,) if operand else ()) + (_colsum(dy * xhv), _colsum(dy), _colsum(dx))

    rows = [(dya, d, 0), (dyb, d, 0), (xh, d, 0), (rstd, 1, 0)]
    return _rowcall(fn, rows, [g], [(d, F32)] + [(d, MXU_DTYPE)] * operand, [d, d, d], n_rows=n_rows, tm=256, name=name)


def _mm_ln_fwd(x, w, bias, a, g, b, *, alpha, name):
    d = a.shape[1]

    def fn(r, av, gv, bv):
        xin = alpha * av + r
        mu = jnp.mean(xin, axis=-1, keepdims=True)
        xc = xin - mu
        var = jnp.mean(xc * xc, axis=-1, keepdims=True)
        rstd = lax.rsqrt(var + LN_EPS)
        xh = xc * rstd
        y = xh * gv + bv
        return y, xh, rstd, y

    return _mm(x, w, bias=bias, name=name, tm=512,
               epilogue=(fn, [a], [g, b], [(d, F32), (d, F32), (1, F32), (d, MXU_DTYPE)], []))


def _mm_ln_bwd(x, w, dya, xh, rstd, g, *, alpha, name):
    d = xh.shape[1]

    def fn(r, da, xhv, rs, gv):
        dy = alpha * da + r
        dyg = dy * gv
        m1 = jnp.mean(dyg, axis=-1, keepdims=True)
        m2 = jnp.mean(dyg * xhv, axis=-1, keepdims=True)
        dx = rs * (dyg - m1 - xhv * m2)
        return dx, dx, _colsum(dy * xhv), _colsum(dy), _colsum(dx)

    return _mm(x, w, tb=True, name=name, tm=512,
               epilogue=(fn, [dya, xh, rstd], [g], [(d, F32), (d, MXU_DTYPE)], [d, d, d]))


def _ln_loss_bwd(a, r, target, g, b, *, alpha, name):
    n_rows, d = a.shape

    def fn(av, rv, tv, gv, bv):
        xin = alpha * av + rv
        mu = jnp.mean(xin, axis=-1, keepdims=True)
        xc = xin - mu
        var = jnp.mean(xc * xc, axis=-1, keepdims=True)
        rs = lax.rsqrt(var + LN_EPS)
        xh = xc * rs
        diff = xh * gv + bv - tv
        part = jnp.sum(jnp.sum(diff * diff, axis=1, keepdims=True), axis=0, keepdims=True) * (0.5 / d)
        dy = diff * (1.0 / d)
        dyg = dy * gv
        m1 = jnp.mean(dyg, axis=-1, keepdims=True)
        m2 = jnp.mean(dyg * xh, axis=-1, keepdims=True)
        dx = rs * (dyg - m1 - xh * m2)
        return dx, dx, _colsum(dy * xh), _colsum(dy), _colsum(dx), jnp.broadcast_to(part, (1, LANES))

    return _rowcall(fn, [(a, d, 0), (r, d, 0), (target, d, 0)], [g, b], [(d, F32), (d, MXU_DTYPE)], [d, d, d, LANES],
                    n_rows=n_rows, tm=256, name=name)


def _rope_lane_constants():
    lane = np.arange(ATT_MERGED)
    in_head = lane % ATT_HEAD_DIM
    sign = np.where(in_head < ROT_DIM // 2, -1.0, np.where(in_head < ROT_DIM, 1.0, 0.0)).astype(np.float32)
    inv_freq = ROPE_THETA ** (-jnp.arange(0, ROT_DIM, 2, dtype=F32) / ROT_DIM)
    return inv_freq[lane % (ROT_DIM // 2)].reshape(1, ATT_MERGED), jnp.asarray(sign).reshape(1, ATT_MERGED)


def _rope_tables(pos_col, *, name, after=None):
    inv_lane, sign = _rope_lane_constants()

    def fn(pos, inv, sg):
        ang = pos.astype(F32) * inv
        return jnp.where(sg != 0.0, jnp.cos(ang), 1.0), sg * jnp.sin(ang)

    return _rowcall(fn, [(pos_col, 1, 0)], [inv_lane, sign], [(ATT_MERGED, F32), (ATT_MERGED, F32)],
                    n_rows=pos_col.shape[0], tm=512, name=name, after=after)


def _rot_partner(t):
    lane = lax.broadcasted_iota(jnp.int32, t.shape, 1)
    width = t.shape[1]
    return jnp.where((lane & (ROT_DIM // 2)) == 0, pltpu.roll(t, width - ROT_DIM // 2, 1), pltpu.roll(t, ROT_DIM // 2, 1))


def _rope(t, cos_t, sin_t):
    return t * cos_t + _rot_partner(t) * sin_t


def _rope_transpose(dt, cos_t, sin_t):
    return dt * cos_t + _rot_partner(dt * sin_t)


def _strided_rows(r, count, stride):
    return pl.ds(r, count) if stride == 1 else pl.ds(r, count, stride=stride)


def _qkv_split(proj, cos_t, sin_t, *, name, tm=512):
    n_rows = proj.shape[0]
    n_g = len(DILATIONS)

    def body(*refs):
        n_src = LANE_HALVES * 3 * n_g
        src, tables, dst = refs[:n_src], refs[n_src:n_src + 2 * LANE_HALVES], refs[n_src + 2 * LANE_HALVES:]
        for kind in range(3):
            for g, dil in enumerate(DILATIONS):
                for half in range(LANE_HALVES):
                    x_ref, o_ref = src[(kind * n_g + g) * LANE_HALVES + half], dst[kind * n_g + g]
                    cos_ref, sin_ref = tables[half], tables[LANE_HALVES + half]
                    for r in range(dil):
                        rows = _strided_rows(r, tm // dil, dil)
                        t = x_ref[rows, :]
                        if kind < 2:
                            t = _rope(t, cos_ref[rows, :], sin_ref[rows, :])
                        lo = r * ATT_MERGED + half * LANES
                        o_ref[:, lo:lo + LANES] = t.astype(o_ref.dtype)

    half_spec = lambda cb: pl.BlockSpec((tm, LANES), functools.partial(lambda i, cb: (i, cb), cb=cb))
    in_specs = [half_spec((off + g) * LANE_HALVES + half)
                for off in (OFF_Q_BLK, OFF_K_BLK, OFF_V_BLK) for g in range(n_g) for half in range(LANE_HALVES)]
    in_specs += [half_spec(half) for _ in range(2) for half in range(LANE_HALVES)]
    out_specs = [pl.BlockSpec((tm // dil, dil * ATT_MERGED), lambda i: (i, 0)) for _ in range(3) for dil in DILATIONS]
    out_shape = [jax.ShapeDtypeStruct((n_rows // dil, dil * ATT_MERGED), MXU_DTYPE) for _ in range(3) for dil in DILATIONS]
    outs = pl.pallas_call(
        body, name=name, grid=(n_rows // tm,), in_specs=in_specs, out_specs=out_specs, out_shape=out_shape,
        compiler_params=_params("parallel"),
    )(*[proj] * (LANE_HALVES * 3 * n_g), *[cos_t] * LANE_HALVES, *[sin_t] * LANE_HALVES)
    return outs[:n_g], outs[n_g:2 * n_g], outs[2 * n_g:]


def _mix(gs, ga, z1, z2, b_att):
    return jax.nn.sigmoid(gs) * (z1 * jax.nn.sigmoid(z2)) + jax.nn.sigmoid(ga) * b_att


def _mix_rows(proj, z, b_att):
    return [(proj, D_MODEL, OFF_GS_BLK), (proj, D_MODEL, OFF_GA_BLK), (z, D_MODEL, 0), (z, D_MODEL, 1), (b_att, D_MODEL, 0)]


def _mix_fwd(proj, z, b_att, *, name):
    return _rowcall(_mix, _mix_rows(proj, z, b_att), [], [(D_MODEL, MXU_DTYPE)],
                    n_rows=proj.shape[0], tm=256, name=name)[0]


def _mix_bwd(dmixed, proj, z, b_att, *, name):
    def fn(dm, gs, ga, z1, z2, ba):
        _, vjp = jax.vjp(_mix, gs, ga, z1, z2, ba)
        dgs, dga, dz1, dz2, dba = vjp(dm)
        dz = jnp.concatenate([dz1, dz2], axis=1)
        return dgs, dga, dz, dba, _colsum(dgs), _colsum(dga), _colsum(dz)

    rows = [(dmixed, D_MODEL, 0)] + _mix_rows(proj, z, b_att)
    widths = [D_MODEL, D_MODEL, 2 * D_MODEL, D_MODEL]
    return _rowcall(fn, rows, [], [(w, MXU_DTYPE) for w in widths], widths[:3], n_rows=proj.shape[0], tm=256, name=name)


def _gelu_bwd(dgy, y, proj, *, name):
    def fn(dg, yv, u):
        _, vjp = jax.vjp(jax.nn.gelu, yv)
        dy = vjp(dg)[0]
        return dy, _colsum(dy * u)

    return _rowcall(fn, [(dgy, SSM_WIDTH, 0), (y, SSM_WIDTH, 0), (proj, SSM_WIDTH, 0)], [], [(SSM_WIDTH, F32)],
                    [SSM_WIDTH], n_rows=y.shape[0], tm=512, name=name)


HEAD_ROWS = ATT_HEADS_PER_GROUP * ATT_BLK


def _head_masks(rows):
    head = lax.broadcasted_iota(jnp.int32, (rows, ATT_MERGED), 1) >> (ATT_HEAD_DIM.bit_length() - 1)
    return [head == h for h in range(ATT_HEADS_PER_GROUP)]


def _stack_heads(t, masks):
    return jnp.concatenate([jnp.where(m, t, jnp.zeros_like(t)) for m in masks], axis=0)


def _unstack_heads(t4, masks):
    blocks = [t4[h * ATT_BLK:(h + 1) * ATT_BLK] for h in range(ATT_HEADS_PER_GROUP)]
    return jnp.where(masks[0], blocks[0], jnp.where(masks[1], blocks[1], jnp.where(masks[2], blocks[2], blocks[3])))


def _head_column(stats, first):
    return jnp.concatenate([stats[:, first + h:first + h + 1] for h in range(ATT_HEADS_PER_GROUP)], axis=0)


def _band_mask(first_key):
    qi = lax.broadcasted_iota(jnp.int32, (HEAD_ROWS, 2 * ATT_BLK), 0) & (ATT_BLK - 1)
    ki = lax.broadcasted_iota(jnp.int32, (HEAD_ROWS, 2 * ATT_BLK), 1)
    steps = qi + ATT_BLK - ki
    return (steps >= 0) & (steps <= ATT_BLK) & (ki >= first_key)


def _dil_fwd(q, k, v, dil, *, name):
    n_blk = q.shape[0] // ATT_BLK
    cur = pl.BlockSpec((ATT_BLK, ATT_MERGED), lambda r, n: (n, r))
    prev = pl.BlockSpec((ATT_BLK, ATT_MERGED), lambda r, n: (jnp.maximum(n - 1, 0), r))

    def body(q_ref, kp_ref, kc_ref, vp_ref, vc_ref, o_ref, l_ref):
        masks = _head_masks(ATT_BLK)
        valid = _band_mask(jnp.where(pl.program_id(1) > 0, 0, ATT_BLK))
        keys = jnp.concatenate([kp_ref[...], kc_ref[...]], axis=0)
        vals = jnp.concatenate([vp_ref[...], vc_ref[...]], axis=0)
        s = jnp.where(valid, _dot(_stack_heads(q_ref[...], masks), keys, 1, 1) * ATT_SCALE, NEG_INF)
        m = jnp.max(s, axis=-1, keepdims=True)
        p = jnp.exp(s - m)
        den = jnp.sum(p, axis=-1, keepdims=True)
        o_ref[...] = _unstack_heads(_dot(p, vals, 1, 0) / den, masks)
        l_ref[...] = _unstack_heads(jnp.broadcast_to(m + jnp.log(den), (HEAD_ROWS, ATT_MERGED)), masks)

    shape = jax.ShapeDtypeStruct(q.shape, F32)
    return pl.pallas_call(
        body, name=name, grid=(dil, n_blk), in_specs=[cur, prev, cur, prev, cur], out_specs=[cur, cur],
        out_shape=[shape, shape], compiler_params=_params("parallel", "parallel"),
    )(q, k, k, v, v)


def _att_merge(outs, lses, *, name, tm=512):
    n_g = len(outs)
    n_rows = outs[0].shape[0] * DILATIONS[0]

    def body(*refs):
        src, (att_ref, lse_ref), tmp = refs[:2 * n_g], refs[2 * n_g:2 * n_g + 2], refs[2 * n_g + 2:]
        vals = []
        for idx, src_ref in enumerate(src):
            dil = DILATIONS[idx % n_g]
            if dil == 1:
                vals.append(src_ref[...])
                continue
            for r in range(dil):
                for half in range(LANE_HALVES):
                    lo = r * ATT_MERGED + half * LANES
                    tmp[LANE_HALVES * idx + half][_strided_rows(r, tm // dil, dil), :] = src_ref[:, lo:lo + LANES]
            vals.append(jnp.concatenate([tmp[LANE_HALVES * idx + half][...] for half in range(LANE_HALVES)], axis=1))
        o, l = vals[:n_g], vals[n_g:]
        m = functools.reduce(jnp.maximum, l)
        e = [jnp.exp(li - m) for li in l]
        z = functools.reduce(jnp.add, e)
        att_ref[...] = functools.reduce(jnp.add, [(ei / z) * oi for ei, oi in zip(e, o)])
        lse_ref[...] = m + jnp.log(z)

    in_specs = [pl.BlockSpec((tm // dil, dil * ATT_MERGED), lambda i: (i, 0)) for _ in range(2) for dil in DILATIONS]
    row = pl.BlockSpec((tm, ATT_MERGED), lambda i: (i, 0))
    shape = jax.ShapeDtypeStruct((n_rows, ATT_MERGED), F32)
    return pl.pallas_call(
        body, name=name, grid=(n_rows // tm,), in_specs=in_specs, out_specs=[row, row], out_shape=[shape, shape],
        scratch_shapes=[pltpu.VMEM((tm, LANES), F32)] * (LANE_HALVES * 2 * n_g), compiler_params=_params("parallel"),
    )(*outs, *lses)


def _att_stats(datt, att, lse, *, name):
    n_rows = datt.shape[0]

    def fn(d, a, l):
        prod = d * a
        lane = lax.broadcasted_iota(jnp.int32, (d.shape[0], LANES), 1)
        out = jnp.zeros((d.shape[0], LANES), F32)
        for h in range(ATT_HEADS_PER_GROUP):
            lo = h * ATT_HEAD_DIM
            out = jnp.where(lane == h, l[:, lo:lo + 1], out)
            delta = jnp.sum(prod[:, lo:lo + ATT_HEAD_DIM], axis=-1, keepdims=True)
            out = jnp.where(lane == ATT_HEADS_PER_GROUP + h, delta, out)
        return out

    rows = [(t, ATT_MERGED, 0) for t in (datt, att, lse)]
    return _rowcall(fn, rows, [], [(LANES, F32)], n_rows=n_rows, tm=512, name=name)[0]


def _dil_bwd(q, k, v, datt, stats, dil, *, name):
    n_rows = datt.shape[0]
    n_blk = n_rows // dil // ATT_BLK
    span = ATT_BLK * dil
    cur = pl.BlockSpec((ATT_BLK, ATT_MERGED), lambda n, r: (n, r))
    prev = pl.BlockSpec((ATT_BLK, ATT_MERGED), lambda n, r: (jnp.maximum(n - 1, 0), r))
    nxt = pl.BlockSpec((ATT_BLK, ATT_MERGED), lambda n, r: (jnp.minimum(n + 1, n_blk - 1), r))
    seq = lambda half, ahead: pl.BlockSpec((span, LANES), lambda n, r: (jnp.minimum(n + ahead, n_blk - 1), half))

    def body(qc_ref, qn_ref, kp_ref, kc_ref, vp_ref, vc_ref, dc0_ref, dc1_ref, dn0_ref, dn1_ref, sc_ref, sn_ref,
             dq0_ref, dq1_ref, dk0_ref, dk1_ref, dv0_ref, dv1_ref):
        n = pl.program_id(0)
        rows = slice(None) if dil == 1 else _strided_rows(pl.program_id(1), ATT_BLK, dil)

        def read(ref0, ref1):
            return jnp.concatenate([ref0[rows, :], ref1[rows, :]], axis=1)

        def write(ref0, ref1, val):
            ref0[rows, :] = val[:, :LANES]
            ref1[rows, :] = val[:, LANES:]

        masks = _head_masks(ATT_BLK)
        valid = _band_mask(jnp.where(n > 0, 0, ATT_BLK))
        qi = lax.broadcasted_iota(jnp.int32, (HEAD_ROWS, ATT_BLK), 0) & (ATT_BLK - 1)
        ki = lax.broadcasted_iota(jnp.int32, (HEAD_ROWS, ATT_BLK), 1)
        valid_next = (ki - qi) >= jnp.where(n < n_blk - 1, 0, ATT_BLK)

        kc, vc = kc_ref[...], vc_ref[...]
        keys = jnp.concatenate([kp_ref[...], kc], axis=0)
        vals = jnp.concatenate([vp_ref[...], vc], axis=0)
        q4 = _stack_heads(qc_ref[...], masks)
        d4 = _stack_heads(read(dc0_ref, dc1_ref).astype(MXU_DTYPE), masks)
        st = sc_ref[rows, :]
        p = jnp.where(valid, jnp.exp(_dot(q4, keys, 1, 1) * ATT_SCALE - _head_column(st, 0)), 0.0)
        ds = p * (_dot(d4, vals, 1, 1) - _head_column(st, ATT_HEADS_PER_GROUP)) * ATT_SCALE
        write(dq0_ref, dq1_ref, _unstack_heads(_dot(ds, keys, 1, 0), masks))

        q4n = _stack_heads(qn_ref[...], masks)
        d4n = _stack_heads(read(dn0_ref, dn1_ref).astype(MXU_DTYPE), masks)
        stn = sn_ref[rows, :]
        p_n = jnp.where(valid_next, jnp.exp(_dot(q4n, kc, 1, 1) * ATT_SCALE - _head_column(stn, 0)), 0.0)
        ds_n = p_n * (_dot(d4n, vc, 1, 1) - _head_column(stn, ATT_HEADS_PER_GROUP)) * ATT_SCALE
        write(dv0_ref, dv1_ref, _dot(p[:, ATT_BLK:], d4, 0, 0) + _dot(p_n, d4n, 0, 0))
        write(dk0_ref, dk1_ref, _dot(ds[:, ATT_BLK:], q4, 0, 0) + _dot(ds_n, q4n, 0, 0))

    shape = jax.ShapeDtypeStruct((n_rows, LANES), F32)
    out = seq(0, 0)
    res = pl.pallas_call(
        body, name=name, grid=(n_blk, dil),
        in_specs=[cur, nxt, prev, cur, prev, cur, seq(0, 0), seq(1, 0), seq(0, 1), seq(1, 1), seq(0, 0), seq(0, 1)],
        out_specs=[out] * 6, out_shape=[shape] * 6, compiler_params=_params("parallel", "arbitrary"),
    )(q, q, k, k, v, v, datt, datt, datt, datt, stats, stats)
    return [(res[2 * i], res[2 * i + 1]) for i in range(3)]


def _dproj_assemble(du, dqkv, dgs, dga, cos_t, sin_t, *, name):
    n_g = len(DILATIONS)

    def fn(*t):
        n_half = LANE_HALVES * 3 * n_g
        du_t, halves, (dgs_t, dga_t, c, s) = t[0], t[1:1 + n_half], t[1 + n_half:]
        parts = [jnp.concatenate(halves[LANE_HALVES * i:LANE_HALVES * (i + 1)], axis=1) for i in range(3 * n_g)]
        for i in range(2 * n_g):
            parts[i] = _rope_transpose(parts[i], c, s)
        cast = [p.astype(MXU_DTYPE) for p in parts]
        return [jnp.concatenate([du_t] + cast + [dgs_t, dga_t], axis=1)] + [_colsum(p) for p in parts]

    rows = [(du, SSM_WIDTH, 0)]
    rows += [(half, LANES, 0) for i in range(3) for g in range(n_g) for half in dqkv[g][i]]
    rows += [(dgs, D_MODEL, 0), (dga, D_MODEL, 0), (cos_t, ATT_MERGED, 0), (sin_t, ATT_MERGED, 0)]
    width = SSM_WIDTH + 3 * n_g * ATT_MERGED + 2 * D_MODEL
    res = _rowcall(fn, rows, [], [(width, MXU_DTYPE)], [ATT_MERGED] * (3 * n_g), n_rows=du.shape[0], tm=256, name=name)
    return res[0], res[1:]


def _xhead(h):
    return slice(h * XATT_HEAD_DIM, (h + 1) * XATT_HEAD_DIM)


def _xatt_probs(qh, kh):
    s = _dot(qh, kh, 1, 1) * XATT_SCALE
    e = jnp.exp(s - jnp.max(s, axis=-1, keepdims=True))
    return e / jnp.sum(e, axis=-1, keepdims=True)


def _xatt_fwd(q, kv, *, name, tm=512):
    n_rows = q.shape[0]
    n_mem = kv.shape[0]

    def body(q_ref, kv_ref, o_ref):
        for h in range(XATT_HEADS):
            sl = _xhead(h)
            p = _xatt_probs(q_ref[:, sl], kv_ref[:, sl])
            o_ref[:, sl] = _dot(p, kv_ref[:, D_MODEL + h * XATT_HEAD_DIM:D_MODEL + (h + 1) * XATT_HEAD_DIM], 1, 0
                                ).astype(o_ref.dtype)

    row = pl.BlockSpec((tm, D_MODEL), lambda i: (i, 0))
    return pl.pallas_call(
        body, name=name, grid=(n_rows // tm,),
        in_specs=[row, pl.BlockSpec((n_mem, 2 * D_MODEL), lambda i: (0, 0))], out_specs=row,
        out_shape=jax.ShapeDtypeStruct((n_rows, D_MODEL), MXU_DTYPE), compiler_params=_params("parallel"),
    )(q, kv)


def _xatt_bwd(q, kv, do, *, name, tm=512):
    n_rows = q.shape[0]
    n_mem = kv.shape[0]

    def body(q_ref, kv_ref, do_ref, dq_ref, dkv_ref):
        @pl.when(pl.program_id(0) == 0)
        def _():
            dkv_ref[...] = jnp.zeros_like(dkv_ref)

        for h in range(XATT_HEADS):
            sl = _xhead(h)
            vsl = slice(D_MODEL + h * XATT_HEAD_DIM, D_MODEL + (h + 1) * XATT_HEAD_DIM)
            qh, kh, doh = q_ref[:, sl], kv_ref[:, sl], do_ref[:, sl]
            p = _xatt_probs(qh, kh)
            dp = _dot(doh, kv_ref[:, vsl], 1, 1)
            ds = p * (dp - jnp.sum(dp * p, axis=-1, keepdims=True)) * XATT_SCALE
            dq_ref[:, sl] = _dot(ds, kh, 1, 0).astype(dq_ref.dtype)
            dkv_ref[:, sl] += _dot(ds, qh, 0, 0)
            dkv_ref[:, vsl] += _dot(p, doh, 0, 0)

    row = pl.BlockSpec((tm, D_MODEL), lambda i: (i, 0))
    full = pl.BlockSpec((n_mem, 2 * D_MODEL), lambda i: (0, 0))
    return pl.pallas_call(
        body, name=name, grid=(n_rows // tm,), in_specs=[row, full, row], out_specs=[row, full],
        out_shape=[jax.ShapeDtypeStruct((n_rows, D_MODEL), MXU_DTYPE), jax.ShapeDtypeStruct((n_mem, 2 * D_MODEL), F32)],
        compiler_params=_params("arbitrary"),
    )(q, kv, do)


def _disc(logdt, a_re, a_im, b_re, b_im):
    dt = jnp.exp(logdt)
    mag = jnp.exp(a_re * dt)
    ab_re = mag * jnp.cos(a_im * dt)
    ab_im = mag * jnp.sin(a_im * dt)
    den = jnp.square(a_re) + jnp.square(a_im)
    nr = ab_re - 1.0
    f_re = (nr * a_re + ab_im * a_im) / den
    f_im = (ab_im * a_re - nr * a_im) / den
    bb_re = f_re[None] * b_re - f_im[None] * b_im
    bb_im = f_re[None] * b_im + f_im[None] * b_re
    return ab_re, ab_im, bb_re, bb_im


def _disc_transpose(logdt, a_re, a_im, b_re, b_im, g_ab_re, g_ab_im, g_bb_re, g_bb_im):
    dt = jnp.exp(logdt)
    mag = jnp.exp(a_re * dt)
    th = a_im * dt
    cs, sn = jnp.cos(th), jnp.sin(th)
    ab_re, ab_im = mag * cs, mag * sn
    den = jnp.square(a_re) + jnp.square(a_im)
    nr = ab_re - 1.0
    f_re = (nr * a_re + ab_im * a_im) / den
    f_im = (ab_im * a_re - nr * a_im) / den
    d_f_re = jnp.sum(g_bb_re * b_re + g_bb_im * b_im, axis=0)
    d_f_im = jnp.sum(g_bb_im * b_re - g_bb_re * b_im, axis=0)
    d_b_re = g_bb_re * f_re[None] + g_bb_im * f_im[None]
    d_b_im = g_bb_im * f_re[None] - g_bb_re * f_im[None]
    d_n_re, d_n_im = d_f_re / den, d_f_im / den
    d_den = -(d_f_re * f_re + d_f_im * f_im) / den
    d_ab_re = g_ab_re + d_n_re * a_re - d_n_im * a_im
    d_ab_im = g_ab_im + d_n_re * a_im + d_n_im * a_re
    d_a_re = d_n_re * nr + d_n_im * ab_im + 2.0 * d_den * a_re
    d_a_im = d_n_re * ab_im - d_n_im * nr + 2.0 * d_den * a_im
    d_mag = d_ab_re * cs + d_ab_im * sn
    d_th = mag * (d_ab_im * cs - d_ab_re * sn)
    d_a_re = d_a_re + d_mag * mag * dt
    d_a_im = d_a_im + d_th * dt
    d_dt = jnp.sum(d_mag * mag * a_re + d_th * a_im, axis=-1, keepdims=True)
    return d_dt * dt, d_a_re, d_a_im, d_b_re, d_b_im


def _full_spec(shape):
    return pl.BlockSpec(tuple(shape), functools.partial(lambda i, nd: (0,) * nd, nd=len(shape)))


def _whole(fn, args, out_shapes, *, name):
    n_in = len(args)

    def body(*refs):
        res = fn(*[r[...] for r in refs[:n_in]])
        for o_ref, val in zip(refs[n_in:], res):
            o_ref[...] = val

    return pl.pallas_call(
        body, name=name, grid=(1,), in_specs=[_full_spec(t.shape) for t in args],
        out_specs=[_full_spec(s) for s in out_shapes], out_shape=[jax.ShapeDtypeStruct(s, F32) for s in out_shapes],
        compiler_params=_params("arbitrary"))(*args)


SSM_WIDE =GROUPS_PER_TILE * SSM_STATE
LANE_GROUPS_PER_TILE = SSM_WIDE // LANES


def _chan(j):
    return slice(j * LANES, (j + 1) * LANES)


def _time_major_rows(j, q, tc):
    return pl.ds(j * LANE_GROUPS_PER_TILE + q, tc, stride=STATE_VREG_ROWS)


def _to_time_major(x, t_re_ref, t_im_ref, dst_re, dst_im, tc):
    for j in range(SSM_TILES):
        xj = x[:, _chan(j)]
        for t_ref, dst in ((t_re_ref, dst_re), (t_im_ref, dst_im)):
            r = _dot(xj, t_ref[j], 1, 0)
            for q in range(LANE_GROUPS_PER_TILE):
                dst[_time_major_rows(j, q, tc), :] = r[:, q * LANES:(q + 1) * LANES]


def _from_time_major(src, j, tc):
    return jnp.concatenate([src[_time_major_rows(j, q, tc), :] for q in range(LANE_GROUPS_PER_TILE)], axis=1)


def _scan_chunk(w_re, w_im, h_re, h_im, a_re, a_im, start, tc):
    def step(t, carry):
        hr, hi = carry
        rows = _scan_rows(t)
        nr = a_re * hr - a_im * hi + w_re[rows, :]
        ni = a_re * hi + a_im * hr + w_im[rows, :]
        h_re[rows, :] = nr
        h_im[rows, :] = ni
        return nr, ni

    return lax.fori_loop(0, tc, step, start, unroll=8)


SSM_CHUNK = 256


def _tile_spec(stack, k):
    return pl.BlockSpec((pl.Squeezed(),) + tuple(stack.shape[1:]), lambda i: (k, 0, 0, 0))


def _expand_block_diagonal(src_ref, dst):
    dst[...] = jnp.zeros_like(dst)
    r, c = src_ref.shape[1:]
    for g in range(SSM_GROUPS):
        j, gl = divmod(g, GROUPS_PER_TILE)
        dst[j, gl * r:(gl + 1) * r, gl * c:(gl + 1) * c] = src_ref[g].astype(dst.dtype)


def _extract_block_diagonal(src, dst_ref):
    r, c = dst_ref.shape[1:]
    for g in range(SSM_GROUPS):
        j, gl = divmod(g, GROUPS_PER_TILE)
        dst_ref[g] = src[j, gl * r:(gl + 1) * r, gl * c:(gl + 1) * c]


def _ssm_fwd(proj, blocks_cn, blocks_nc, a_re, a_im, gain, *, name, tc=SSM_CHUNK):
    n_rows = proj.shape[0]
    n_chunk = n_rows // tc

    def body(u_ref, br_ref, bi_ref, cr_ref, ci_ref, ar_ref, ai_ref, g_ref, y_ref, gy_ref, hr, hi, wr, wi, state,
             tbr_ref, tbi_ref, tcr_ref, tci_ref):
        @pl.when(pl.program_id(0) == 0)
        def _():
            state[...] = jnp.zeros_like(state)
            for src_ref, dst in ((br_ref, tbr_ref), (bi_ref, tbi_ref), (cr_ref, tcr_ref), (ci_ref, tci_ref)):
                _expand_block_diagonal(src_ref, dst)

        u = u_ref[...]
        _to_time_major(u, tbr_ref, tbi_ref, wr, wi, tc)
        state[0], state[1] = _scan_chunk(wr, wi, hr, hi, ar_ref[...], ai_ref[...], (state[0], state[1]), tc)
        for j in range(SSM_TILES):
            yj = (_dot(_from_time_major(hr, j, tc), tcr_ref[j], 1, 0) + _dot(_from_time_major(hi, j, tc), tci_ref[j], 1, 0)
                  + g_ref[:, _chan(j)] * u[:, _chan(j)])
            y_ref[:, _chan(j)] = yj
            gy_ref[:, _chan(j)] = jax.nn.gelu(yj).astype(gy_ref.dtype)

    rows = pl.BlockSpec((tc, SSM_WIDTH), lambda i: (i, 0))
    coef = pl.BlockSpec((STATE_VREG_ROWS, LANES), lambda i: (0, 0))
    states = pl.BlockSpec((tc * STATE_VREG_ROWS, LANES), lambda i: (i, 0))
    sshape = jax.ShapeDtypeStruct((n_rows * STATE_VREG_ROWS, LANES), F32)
    return pl.pallas_call(
        body, name=name, grid=(n_chunk,),
        in_specs=[rows, _tile_spec(blocks_cn, 0), _tile_spec(blocks_cn, 1), _tile_spec(blocks_nc, 0),
                  _tile_spec(blocks_nc, 1), coef, coef, pl.BlockSpec((1, SSM_WIDTH), lambda i: (0, 0))],
        out_specs=[rows, rows, states, states],
        out_shape=[jax.ShapeDtypeStruct((n_rows, SSM_WIDTH), F32), jax.ShapeDtypeStruct((n_rows, SSM_WIDTH), MXU_DTYPE),
                   sshape, sshape],
        scratch_shapes=[pltpu.VMEM((tc * STATE_VREG_ROWS, LANES), F32)] * 2 + [pltpu.VMEM((2, STATE_VREG_ROWS, LANES), F32)]
        + [pltpu.VMEM((SSM_TILES, LANES, SSM_WIDE), MXU_DTYPE)] * 2 + [pltpu.VMEM((SSM_TILES, SSM_WIDE, LANES), MXU_DTYPE)] * 2,
        compiler_params=_params("arbitrary"),
    )(proj, blocks_cn, blocks_cn, blocks_nc, blocks_nc, a_re, a_im, gain)


def _ssm_bwd(proj, dy, h_re, h_im, blocks_cn, blocks_nc, a_re, a_im, gain, *, name, tc=SSM_CHUNK):
    n_rows = proj.shape[0]
    n_chunk = n_rows // tc

    def body(u_ref, dy_ref, hr, hi, cr_ref, ci_ref, br_ref, bi_ref, ar_ref, ai_ref, g_ref,
             du_ref, su_ref, dc_re_ref, dc_im_ref, db_re_ref, db_im_ref, dar_ref, dai_ref, wr, wi, carry,
             tdr_ref, tdi_ref, tur_ref, tui_ref, dcr_ref, dci_ref, dbr_ref, dbi_ref):
        @pl.when(pl.program_id(0) == 0)
        def _():
            carry[...] = jnp.zeros_like(carry)
            for acc_ref in (su_ref, dcr_ref, dci_ref, dbr_ref, dbi_ref):
                acc_ref[...] = jnp.zeros_like(acc_ref)
            for src_ref, dst in ((cr_ref, tdr_ref), (ci_ref, tdi_ref), (br_ref, tur_ref), (bi_ref, tui_ref)):
                _expand_block_diagonal(src_ref, dst)

        a_r, a_i = ar_ref[...], ai_ref[...]
        u, dyv = u_ref[...], dy_ref[...]
        _to_time_major(dyv, tdr_ref, tdi_ref, wr, wi, tc)

        def step(kk, c):
            lam_r, lam_i, dar, dai = c
            rows = _scan_rows(tc - 1 - kk)
            h_r, h_i = hr[rows, :], hi[rows, :]
            dar = dar + lam_r * h_r + lam_i * h_i
            dai = dai + lam_i * h_r - lam_r * h_i
            new_r = wr[rows, :] + a_r * lam_r + a_i * lam_i
            new_i = wi[rows, :] + a_r * lam_i - a_i * lam_r
            wr[rows, :] = new_r
            wi[rows, :] = new_i
            return new_r, new_i, dar, dai

        carry[0], carry[1], carry[2], carry[3] = lax.fori_loop(0, tc, step, (carry[0], carry[1], carry[2], carry[3]),
                                                              unroll=8)
        dar_ref[...] = carry[2]
        dai_ref[...] = carry[3]
        for j in range(SSM_TILES):
            cj = _chan(j)
            lam_r, lam_i = _from_time_major(wr, j, tc), _from_time_major(wi, j, tc)
            dcr_ref[j] += _dot(dyv[:, cj], _from_time_major(hr, j, tc), 0, 0)
            dci_ref[j] += _dot(dyv[:, cj], _from_time_major(hi, j, tc), 0, 0)
            dbr_ref[j] += _dot(u[:, cj], lam_r, 0, 0)
            dbi_ref[j] += _dot(u[:, cj], lam_i, 0, 0)
            duj = _dot(lam_r, tur_ref[j], 1, 0) + _dot(lam_i, tui_ref[j], 1, 0) + g_ref[:, cj] * dyv[:, cj]
            du_ref[:, cj] = duj.astype(du_ref.dtype)
            su_ref[:, cj] += _colsum(duj)

        @pl.when(pl.program_id(0) == n_chunk - 1)
        def _():
            for src, dst_ref in ((dcr_ref, dc_re_ref), (dci_ref, dc_im_ref), (dbr_ref, db_re_ref), (dbi_ref, db_im_ref)):
                _extract_block_diagonal(src, dst_ref)

    back = lambda i: (n_chunk - 1 - i, 0)
    rows = pl.BlockSpec((tc, SSM_WIDTH), back)
    blocks = pl.BlockSpec((SSM_GROUPS, SSM_GROUP, SSM_STATE), lambda i: (0, 0, 0))
    coef = pl.BlockSpec((STATE_VREG_ROWS, LANES), lambda i: (0, 0))
    states = pl.BlockSpec((tc * STATE_VREG_ROWS, LANES), back)
    vec = pl.BlockSpec((1, SSM_WIDTH), lambda i: (0, 0))
    bshape = jax.ShapeDtypeStruct((SSM_GROUPS, SSM_GROUP, SSM_STATE), F32)
    cshape = jax.ShapeDtypeStruct((STATE_VREG_ROWS, LANES), F32)
    return pl.pallas_call(
        body, name=name, grid=(n_chunk,),
        in_specs=[rows, rows, states, states, _tile_spec(blocks_cn, 2), _tile_spec(blocks_cn, 3), _tile_spec(blocks_nc, 2),
                  _tile_spec(blocks_nc, 3), coef, coef, vec],
        out_specs=[rows, vec, blocks, blocks, blocks, blocks, coef, coef],
        out_shape=[jax.ShapeDtypeStruct((n_rows, SSM_WIDTH), MXU_DTYPE), jax.ShapeDtypeStruct((1, SSM_WIDTH), F32),
                   bshape, bshape, bshape, bshape, cshape, cshape],
        scratch_shapes=[pltpu.VMEM((tc * STATE_VREG_ROWS, LANES), F32)] * 2 + [pltpu.VMEM((4, STATE_VREG_ROWS, LANES), F32)]
        + [pltpu.VMEM((SSM_TILES, LANES, SSM_WIDE), MXU_DTYPE)] * 2 + [pltpu.VMEM((SSM_TILES, SSM_WIDE, LANES), MXU_DTYPE)] * 2
        + [pltpu.VMEM((SSM_TILES, LANES, SSM_WIDE), F32)] * 4,
        compiler_params=_params("arbitrary"),
    )(proj, dy, h_re, h_im, blocks_cn, blocks_cn, blocks_nc, blocks_nc, a_re, a_im, gain)


def _scan_rows(t):
    return pl.ds(pl.multiple_of(t * STATE_VREG_ROWS, 8), STATE_VREG_ROWS)


GATHER_GROUPS = (("w_glu", "w_att_up", "w_mix_out"), ("w_xq", "w_xkv", "w_xo", "w_ff1", "w_ff2"))
SCATTER_GROUPS = (("w_ff2", "w_ff1"), ("w_xo", "w_xq", "w_xkv", "w_mix_out"), ("w_att_up", "w_glu"), ("w_in",))


def _local_grads(x, mem, pos_col, target, sm, fetch_in, fetch, send, send_small, start_token):
    b_re_t = sm["ssm_b_re"].transpose(2, 0, 1)
    b_im_t = sm["ssm_b_im"].transpose(2, 0, 1)
    logdt = sm["ssm_log_dt"].reshape(SSM_GROUPS, 1)
    c_re, c_im = sm["ssm_c_re"], sm["ssm_c_im"]
    grp = (SSM_GROUPS, SSM_STATE)
    chn = (SSM_GROUP, SSM_GROUPS, SSM_STATE)

    wts = {}
    cos_t, sin_t = _rope_tables(pos_col, after=start_token, name="rope_tables")
    h0, xh0, rs0, h0m = _ln_fwd(x, None, sm["ln_in_g"], sm["ln_in_b"], alpha=1.0, name="ln_in_fwd")
    disc_in = (logdt, sm["ssm_a_re"], sm["ssm_a_im"], b_re_t, b_im_t)
    ab_re, ab_im, bb_re_t, bb_im_t = _whole(_disc, disc_in, [grp, grp, chn, chn], name="ssm_disc")
    a_re_rows, a_im_rows = ab_re.reshape(STATE_VREG_ROWS, LANES), ab_im.reshape(STATE_VREG_ROWS, LANES)
    tiles_cn = jnp.stack([bb_re_t.transpose(1, 0, 2), bb_im_t.transpose(1, 0, 2), c_re, -c_im])
    tiles_nc = jnp.stack([c_re.transpose(0, 2, 1), -c_im.transpose(0, 2, 1), bb_re_t.transpose(1, 2, 0),
                          bb_im_t.transpose(1, 2, 0)])
    w_in_near, near_ids = fetch_in(0, [h0m, tiles_cn, tiles_nc])
    proj = _mm_shards(h0m, w_in_near, sm["b_in"], near_ids, name="in_proj_near")
    wts["w_in"], far_ids = fetch_in(1, [proj])
    proj = _mm_shards(h0m, wts["w_in"], sm["b_in"], far_ids, prev=proj, name="in_proj_far")

    y, gy, h_re, h_im = _ssm_fwd(proj, tiles_cn, tiles_nc, a_re_rows, a_im_rows, sm["ssm_d"], name="ssm_fwd")

    q, k, v = _qkv_split(proj, cos_t, sin_t, name="qkv_split")
    outs, lses = [], []
    for g, dil in enumerate(DILATIONS):
        o_g, l_g = _dil_fwd(q[g], k[g], v[g], dil, name=f"dil_att_fwd_{dil}")
        outs.append(o_g)
        lses.append(l_g)
    att, lse = _att_merge(outs, lses, name="att_merge")
    wts.update(fetch(0, [att]))
    z = _mm(gy, wts["w_glu"], bias=sm["b_glu"], b_shards=True, name="glu_proj")
    b_att = _mm(att, wts["w_att_up"], b_shards=True, name="att_up")

    mixed = _mix_fwd(proj, z, b_att, name="gate_mix")
    h1, xh1, rs1, h1m = _mm_ln_fwd(mixed, wts["w_mix_out"], sm["b_mix_out"], h0, sm["ln1_g"], sm["ln1_b"],
                                   alpha=DEEPNORM_ALPHA, name="mix_out_ln1")

    wts.update(fetch(1, [h1m]))
    xq = _mm(h1m, wts["w_xq"], out_dtype=MXU_DTYPE, name="xatt_q")
    kv = _mm(mem, wts["w_xkv"], out_dtype=MXU_DTYPE, b_shards=True, name="xatt_kv")
    xo_in = _xatt_fwd(xq, kv, name="xatt_fwd")
    h2, xh2, rs2, h2m = _mm_ln_fwd(xo_in, wts["w_xo"], None, h1, sm["ln2_g"], sm["ln2_b"], alpha=DEEPNORM_ALPHA,
                                   name="xatt_o_ln2")

    pre, act = _mm(h2m, wts["w_ff1"], bias=sm["b_ff1"], b_shards=True, name="ff1",
                   also=(lambda r: jnp.square(jnp.maximum(r, 0.0)), MXU_DTYPE))
    ff = _mm(act, wts["w_ff2"], bias=sm["b_ff2"], name="ff2")

    gw, gs = {}, {}
    dr3, dr3m, gs["ln3_g"], gs["ln3_b"], gs["b_ff2"], loss_row = _ln_loss_bwd(
        h2, ff, target, sm["ln3_g"], sm["ln3_b"], alpha=DEEPNORM_ALPHA, name="ln3_loss")
    wgrad = functools.partial(_mm, ta=True, out_dtype=WIRE_DTYPE, tk=2048)
    gw["w_ff2"] = wgrad(act, dr3m, tk=1024, name="ff2_dw")
    dpre, gs["b_ff1"] = _mm(dr3m, wts["w_ff2"], tb=True, out_dtype=MXU_DTYPE, colsum=True, name="ff2_dx",
                            gate=(pre, lambda p: 2.0 * jnp.maximum(p, 0.0)))
    gw["w_ff1"] = wgrad(h2m, dpre, out_shards=True, name="ff1_dw")
    sent = send(0, gw)
    dh2 = _mm(dpre, wts["w_ff1"], tb=True, b_shards=True, after=sent, name="ff1_dx")
    dr2, dr2m, gs["ln2_g"], gs["ln2_b"], _ = _ln_bwd(dr3, dh2, xh2, rs2, sm["ln2_g"], alpha=DEEPNORM_ALPHA,
                                                     name="ln2_bwd")
    gw["w_xo"] = wgrad(xo_in, dr2m, name="xatt_o_dw")
    dxo_in = _mm(dr2m, wts["w_xo"], tb=True, out_dtype=MXU_DTYPE, name="xatt_o_dx")
    dxq, dkv = _xatt_bwd(xq, kv, dxo_in, name="xatt_bwd")
    gw["w_xq"] = wgrad(h1m, dxq, name="xatt_q_dw")
    gw["w_xkv"] = wgrad(mem, dkv, out_shards=True, name="xatt_kv_dw")
    dr1, dr1m, gs["ln1_g"], gs["ln1_b"], gs["b_mix_out"] = _mm_ln_bwd(
        dxq, wts["w_xq"], dr2, xh1, rs1, sm["ln1_g"], alpha=DEEPNORM_ALPHA, name="xatt_q_dx_ln1")
    gw["w_mix_out"] = wgrad(mixed, dr1m, name="mix_out_dw")
    sent = send(1, gw)
    dmixed = _mm(dr1m, wts["w_mix_out"], tb=True, after=sent, name="mix_out_dx")
    dgs, dga, dz, db_att, s_gs, s_ga, gs["b_glu"] = _mix_bwd(dmixed, proj, z, b_att, name="gate_mix_bwd")

    gw["w_att_up"] = wgrad(att, db_att, out_shards=True, name="att_up_dw")
    gw["w_glu"] = wgrad(gy, dz, out_shards=True, name="glu_dw")
    sent = send(2, gw)
    datt = _mm(db_att, wts["w_att_up"], tb=True, b_shards=True, after=sent, name="att_up_dx")
    stats = _att_stats(datt, att, lse, name="att_stats")
    dqkv = [_dil_bwd(q[g], k[g], v[g], datt, stats, dil, name=f"dil_att_bwd_{dil}") for g, dil in enumerate(DILATIONS)]

    dgy = _mm(dz, wts["w_glu"], tb=True, b_shards=True, name="glu_dx")
    dy, gs["ssm_d"] = _gelu_bwd(dgy, y, proj, name="gelu_bwd")
    du, s_u, dc_re_t, dc_im_t, dbb_re_t, dbb_im_t, da_re, da_im = _ssm_bwd(
        proj, dy, h_re, h_im, tiles_cn, tiles_nc, a_re_rows, a_im_rows, sm["ssm_d"], name="ssm_bwd")
    gs["ssm_c_re"], gs["ssm_c_im"] = dc_re_t, -dc_im_t
    disc_ct = (da_re.reshape(grp), da_im.reshape(grp), dbb_re_t.transpose(1, 0, 2), dbb_im_t.transpose(1, 0, 2))
    d_logdt, gs["ssm_a_re"], gs["ssm_a_im"], d_b_re_t, d_b_im_t = _whole(
        _disc_transpose, disc_in + disc_ct, [(SSM_GROUPS, 1), grp, grp, chn, chn], name="ssm_disc_bwd")
    gs["ssm_log_dt"] = d_logdt
    gs["ssm_b_re"], gs["ssm_b_im"] = d_b_re_t.transpose(1, 2, 0), d_b_im_t.transpose(1, 2, 0)

    dproj, s_qkv = _dproj_assemble(du, dqkv, dgs, dga, cos_t, sin_t, name="dproj_assemble")
    gs["b_in"] = jnp.concatenate([s_u, *s_qkv, s_gs, s_ga], axis=1)
    sent = send_small(gs, SMALL_EARLY)
    gw["w_in"] = wgrad(h0m, dproj, out_shards=True, after=sent, name="in_proj_dw")
    sent = send(3, gw)
    dh0 = _mm(dproj, wts["w_in"], tb=True, b_shards=True, after=sent, name="in_proj_dx")
    grad_x, gs["ln_in_g"], gs["ln_in_b"], _ = _ln_bwd(dr1, dh0, xh0, rs0, sm["ln_in_g"], alpha=DEEPNORM_ALPHA,
                                                      operand=False, name="ln_in_bwd")
    return loss_row, grad_x, gs


N_PEER = N_DEV - 1
_IN_HBM = pl.BlockSpec(memory_space=pltpu.HBM)
_IN_SEMAPHORE = pl.BlockSpec(memory_space=pltpu.SEMAPHORE)


def _device_index():
    return 4 * lax.axis_index("x") + 2 * lax.axis_index("y") + lax.axis_index("c")


ALL_PEERS = tuple(range(1, N_DEV))
NEAR_PEERS = (1, 2, 3, 4, 5)
FAR_PEERS = (6, 7)


def _peer_index(kk):
    x, y, c = lax.axis_index("x"), lax.axis_index("y"), lax.axis_index("c")
    return 4 * ((x + (kk >> 2)) % 2) + 2 * ((y + ((kk >> 1) & 1)) % 2) + (c + (kk & 1)) % 2


def _exchange_copies(src_refs, land_refs, send_sems, recv_sems, scatter, peers):
    x, y, c = lax.axis_index("x"), lax.axis_index("y"), lax.axis_index("c")
    me = 4 * x + 2 * y + c
    pairs = []
    for a, (src_ref, land_ref) in enumerate(zip(src_refs, land_refs)):
        for idx, kk in enumerate(peers):
            px = (x + (kk >> 2)) % 2
            py = (y + ((kk >> 1) & 1)) % 2
            pc = (c + (kk & 1)) % 2
            peer = 4 * px + 2 * py + pc
            sem = a * len(peers) + idx
            src = src_ref.at[peer] if scatter else src_ref

            def copy(dst, src=src, sem=sem, px=px, py=py, pc=pc):
                return pltpu.make_async_remote_copy(
                    src_ref=src, dst_ref=dst, send_sem=send_sems.at[sem], recv_sem=recv_sems.at[sem],
                    device_id=(px, py, pc), device_id_type=pl.DeviceIdType.MESH)

            pairs.append((functools.partial(copy, land_ref.at[me]), functools.partial(copy, land_ref.at[peer])))
    return pairs


def _own_copies(src_refs, land_refs, own_sems, scatter):
    me = _device_index()
    return [functools.partial(pltpu.make_async_copy, src_ref.at[me] if scatter else src_ref, land_ref.at[me],
                              own_sems.at[a]) for a, (src_ref, land_ref) in enumerate(zip(src_refs, land_refs))]


def _exchange_start(srcs, *, scatter, name, after=None, peers=ALL_PEERS, lands=None):
    n_arr, n_sem = len(srcs), len(srcs) * len(peers)
    own = lands is None
    if own:
        lands = [lax.empty((N_DEV,) + tuple(s.shape[1:] if scatter else s.shape), s.dtype) for s in srcs]
    n_in = 2 * n_arr + (after is not None)

    def body(*refs):
        send_sems, recv_sems = refs[n_in], refs[n_in + 1]
        for sent, _ in _exchange_copies(refs[:n_arr], refs[n_arr:2 * n_arr], send_sems, recv_sems, scatter, peers):
            sent().start()
        if own:
            for local in _own_copies(refs[:n_arr], refs[n_arr:2 * n_arr], refs[n_in + 2], scatter):
                local().start()
        refs[-1][...] = jnp.zeros_like(refs[-1])

    sems = [pltpu.SemaphoreType.DMA((n_sem,)), pltpu.SemaphoreType.DMA((n_sem,))] + [pltpu.SemaphoreType.DMA((n_arr,))] * own
    through = [pltpu.HBM(t.shape, t.dtype) for t in (*srcs, *lands)]
    res = pl.pallas_call(
        body, name=name, out_shape=(*sems, *through, jax.ShapeDtypeStruct((8, LANES), F32)),
        in_specs=[_IN_HBM] * (2 * n_arr) + [pl.BlockSpec(memory_space=pl.ANY)] * (after is not None),
        out_specs=(*[_IN_SEMAPHORE] * len(sems), *[_IN_HBM] * (2 * n_arr), pl.BlockSpec(memory_space=pltpu.VMEM)),
        input_output_aliases={i: len(sems) + i for i in range(2 * n_arr)},
        compiler_params=pltpu.CompilerParams(has_side_effects=pltpu.SideEffectType.DATAFLOW_SIDE_EFFECTING),
    )(*[pltpu.with_memory_space_constraint(t, pltpu.HBM) for t in (*srcs, *lands)],
      *([after] if after is not None else []))
    first = len(sems)
    handle = dict(sems=res[:first], srcs=res[first:first + n_arr], lands=res[first + n_arr:first + 2 * n_arr],
                  scatter=scatter, peers=peers, own=own)
    return handle, res[-1]


def _exchange_wait(handle, *, after, name, srcs=None, lands=None):
    srcs = handle["srcs"] if srcs is None else srcs
    lands = handle["lands"] if lands is None else lands
    sems, scatter, peers, own = handle["sems"], handle["scatter"], handle["peers"], handle["own"]
    n_arr = len(srcs)
    after = list(after)

    def body(*refs):
        src_refs, land_refs = refs[:n_arr], refs[n_arr:2 * n_arr]
        for sent, received in _exchange_copies(src_refs, land_refs, refs[2 * n_arr], refs[2 * n_arr + 1], scatter, peers):
            sent().wait_send()
            received().wait_recv()
        if own:
            for local in _own_copies(src_refs, land_refs, refs[2 * n_arr + 2], scatter):
                local().wait()

    res = pl.pallas_call(
        body, name=name, out_shape=tuple(pltpu.HBM(t.shape, t.dtype) for t in (*srcs, *lands)),
        in_specs=[_IN_HBM] * (2 * n_arr) + [_IN_SEMAPHORE] * len(sems) + [pl.BlockSpec(memory_space=pl.ANY)] * len(after),
        out_specs=tuple([_IN_HBM] * (2 * n_arr)), input_output_aliases={i: i for i in range(2 * n_arr)},
        compiler_params=pltpu.CompilerParams(has_side_effects=pltpu.SideEffectType.DATAFLOW_SIDE_EFFECTING),
    )(*srcs, *lands, *sems, *after)
    return res[:n_arr], res[n_arr:]


def _adamw(g, w, m, v):
    m_new = ADAM_B1 * m + (1.0 - ADAM_B1) * g
    v_new = ADAM_B2 * v + (1.0 - ADAM_B2) * jnp.square(g)
    m_hat = m_new / (1.0 - ADAM_B1 ** ADAM_STEP)
    v_hat = v_new / (1.0 - ADAM_B2 ** ADAM_STEP)
    return g, -ADAM_LR * (m_hat / (jnp.sqrt(v_hat) + ADAM_EPS) + ADAM_WD * w), m_new, v_new


def _reduce_adamw(gstack, w, m, v, *, name, tr=128):
    n_rows, cols = w.shape
    tr = min(tr, n_rows)
    assert n_rows % tr == 0, (name, n_rows, tr)

    def body(g_ref, w_ref, m_ref, v_ref, *out_refs):
        g = g_ref[0].astype(F32)
        for dev in range(1, N_DEV):
            g = g + g_ref[dev].astype(F32)
        for o_ref, val in zip(out_refs, _adamw(g, w_ref[...], m_ref[...], v_ref[...])):
            o_ref[...] = val

    flat = pl.BlockSpec((tr, cols), lambda i: (i, 0))
    shape = jax.ShapeDtypeStruct((n_rows, cols), F32)
    return pl.pallas_call(
        body, name=name, grid=(n_rows // tr,),
        in_specs=[pl.BlockSpec((N_DEV, tr, cols), lambda i: (0, i, 0)), flat, flat, flat],
        out_specs=[flat] * 4, out_shape=[shape] * 4, compiler_params=_params("parallel"),
    )(gstack, w, m, v)


SMALL_FLAT_SSM = ("ssm_b_re", "ssm_b_im", "ssm_c_re", "ssm_c_im")


def _small_view(name, shape):
    size = int(np.prod(shape))
    if name in SMALL_FLAT_SSM:
        return SSM_GROUPS, size // SSM_GROUPS
    if name in ("ssm_a_re", "ssm_a_im"):
        return SSM_GROUPS, SSM_STATE
    return 1, size


def _pack_rows(view):
    return -(-(view[0] * view[1]) // PACK_COLS)


SMALL_LATE = ("ln_in_g", "ln_in_b")
SMALL_EARLY = tuple(n for n in SMALL if n not in SMALL_LATE)


def _pack_small(gs, names, views):
    parts = []
    for n in names:
        flat = gs[n].reshape(-1).astype(WIRE_DTYPE)
        parts.append(jnp.pad(flat, (0, _pack_rows(views[n]) * PACK_COLS - flat.shape[0])))
    total = sum(p.shape[0] for p in parts) // PACK_COLS
    parts.append(jnp.zeros(((-total % PACK_ROW_ALIGN) * PACK_COLS,), WIRE_DTYPE))
    return jnp.concatenate(parts).reshape(-1, PACK_COLS)


def _small_pieces(view):
    rows, cols = view
    if cols == PACK_COLS:
        return [(0, rows, 0, 0, 0, cols)]
    if rows == 1 and cols > PACK_COLS:
        return [(kk, 1, 0, 0, kk * PACK_COLS, PACK_COLS) for kk in range(cols // PACK_COLS)]
    if rows == 1:
        return [(0, 1, 0, 0, 0, cols)]
    return [((r * cols) // PACK_COLS, 1, (r * cols) % PACK_COLS, r, 0, cols) for r in range(rows)]


def _adamw_small(stacks, views, w, m, v, *, name):
    n = len(SMALL)
    place, first = {}, [0, 0]
    for k, names in enumerate((SMALL_EARLY, SMALL_LATE)):
        for name_ in names:
            place[name_] = (k, first[k])
            first[k] += _pack_rows(views[name_])

    def body(early_ref, late_ref, *refs):
        ins, outs = refs[:3 * n], refs[3 * n:]
        for i, name_ in enumerate(SMALL):
            stack_ref = (early_ref, late_ref)[place[name_][0]]
            row0 = place[name_][1]
            for prow, nrows, lane, orow, ocol, width in _small_pieces(views[name_]):
                src = (slice(row0 + prow, row0 + prow + nrows), slice(lane, lane + width))
                dst = (slice(orow, orow + nrows), slice(ocol, ocol + width))
                g = stack_ref[(0,) + src].astype(F32)
                for dev in range(1, N_DEV):
                    g = g + stack_ref[(dev,) + src].astype(F32)
                res = _adamw(g, ins[i][dst], ins[n + i][dst], ins[2 * n + i][dst])
                for kk, val in enumerate(res):
                    outs[kk * n + i][dst] = val

    args = [*stacks, *[d[name_] for d in (w, m, v) for name_ in SMALL]]
    out_views = [views[name_] for _ in range(4) for name_ in SMALL]
    res = pl.pallas_call(
        body, name=name, grid=(1,), in_specs=[_full_spec(t.shape) for t in args],
        out_specs=[_full_spec(s) for s in out_views], out_shape=[jax.ShapeDtypeStruct(s, F32) for s in out_views],
        compiler_params=_params("arbitrary"),
    )(*args)
    return [dict(zip(SMALL, res[kk * n:(kk + 1) * n])) for kk in range(4)]


def kernel(x, mem, positions, ln_in_g, ln_in_b, w_in, b_in, ssm_log_dt, ssm_a_re, ssm_a_im, ssm_b_re, ssm_b_im, ssm_c_re, ssm_c_im, ssm_d, w_glu, b_glu, w_att_up, w_mix_out, b_mix_out, ln1_g, ln1_b, w_xq, w_xkv, w_xo, ln2_g, ln2_b, w_ff1, b_ff1, w_ff2, b_ff2, ln3_g, ln3_b, loss_target, m_ln_in_g, m_ln_in_b, m_w_in, m_b_in, m_ssm_log_dt, m_ssm_a_re, m_ssm_a_im, m_ssm_b_re, m_ssm_b_im, m_ssm_c_re, m_ssm_c_im, m_ssm_d, m_w_glu, m_b_glu, m_w_att_up, m_w_mix_out, m_b_mix_out, m_ln1_g, m_ln1_b, m_w_xq, m_w_xkv, m_w_xo, m_ln2_g, m_ln2_b, m_w_ff1, m_b_ff1, m_w_ff2, m_b_ff2, m_ln3_g, m_ln3_b, v_ln_in_g, v_ln_in_b, v_w_in, v_b_in, v_ssm_log_dt, v_ssm_a_re, v_ssm_a_im, v_ssm_b_re, v_ssm_b_im, v_ssm_c_re, v_ssm_c_im, v_ssm_d, v_w_glu, v_b_glu, v_w_att_up, v_w_mix_out, v_b_mix_out, v_ln1_g, v_ln1_b, v_w_xq, v_w_xkv, v_w_xo, v_ln2_g, v_ln2_b, v_w_ff1, v_b_ff1, v_w_ff2, v_b_ff2, v_ln3_g, v_ln3_b):
    given = dict(locals())
    w_arg = {n: given[n] for n in WEIGHTS}
    m_arg = {n: given["m_" + n] for n in WEIGHTS}
    v_arg = {n: given["v_" + n] for n in WEIGHTS}

    in_near, token = _exchange_start([w_arg["w_in"][0].astype(MXU_DTYPE)], scatter=False, peers=NEAR_PEERS,
                                     name="gather_start_in_near")
    in_far, token = _exchange_start(in_near["srcs"], scatter=False, peers=FAR_PEERS, lands=in_near["lands"],
                                    after=token, name="gather_start_in_far")
    w_in_state = [in_far["srcs"], in_far["lands"]]
    token, w_arg, m_arg, v_arg = lax.optimization_barrier((token, w_arg, m_arg, v_arg))
    shards = {n: w_arg[n][0].astype(MXU_DTYPE) for n in BIG if n != "w_in"}
    gathers = []
    for i, names in enumerate(GATHER_GROUPS):
        handle, token = _exchange_start([shards[n] for n in names], scatter=False, after=token, name=f"gather_start_{i}")
        gathers.append(handle)

    small_views = {n: _small_view(n, w_arg[n].shape) for n in SMALL}
    small_w, small_m, small_v = [{n: d[n].reshape(small_views[n]) for n in SMALL} for d in (w_arg, m_arg, v_arg)]
    relaid = [d[n] for d in (small_w, small_m, small_v) for n in SMALL_FLAT_SSM]

    def fetch_in(part, after):
        handle, peers, tag = ((in_near, (0,) + NEAR_PEERS, "near"), (in_far, FAR_PEERS, "far"))[part]
        w_in_state[:] = _exchange_wait(handle, after=after + (relaid if part == 0 else []), srcs=w_in_state[0],
                                       lands=w_in_state[1], name="gather_wait_in_" + tag)
        return w_in_state[1][0], jnp.stack([_peer_index(kk) for kk in peers]).astype(jnp.int32)

    def fetch(i, after):
        _, lands = _exchange_wait(gathers[i], after=after, name=f"gather_wait_{i}")
        full = dict(zip(GATHER_GROUPS[i], lands))
        return {n: t if n in BIG_COL_SHARDED else t.reshape(-1, t.shape[-1]) for n, t in full.items()}

    scatters = {}

    def send(i, gw):
        slots = [gw[n] if n in BIG_COL_SHARDED else gw[n].reshape(N_DEV, -1, gw[n].shape[-1]) for n in SCATTER_GROUPS[i]]
        handle, sent = _exchange_start(slots, scatter=True, name=f"scatter_start_{i}")
        scatters[i] = (handle, slots)
        return sent

    sm = {}
    for n in SMALL:
        t = w_arg[n]
        if n.startswith("ssm_") and n not in ("ssm_d", "ssm_log_dt"):
            sm[n] = t[0]
        else:
            sm[n] = t.reshape(1, -1)

    smalls = []

    def send_small(gs, names):
        handle, sent = _exchange_start([_pack_small(gs, names, small_views)], scatter=False,
                                       name=f"small_start_{len(smalls)}")
        smalls.append(handle)
        return sent

    loss_row, grad_x, gs = _local_grads(x[0], mem[0], positions.reshape(-1, 1), loss_target[0], sm, fetch_in, fetch,
                                        send, send_small, token)
    loss = lax.psum(loss_row[0, 0], ("x", "y", "c"))
    send_small(gs, SMALL_LATE)

    results = [{}, {}, {}, {}]
    done = grad_x
    for i, names in enumerate(SCATTER_GROUPS):
        handle, slots = scatters[i]
        _, lands = _exchange_wait(handle, after=[done], name=f"scatter_wait_{i}")
        for n, land, slot in zip(names, lands, slots):
            res = _reduce_adamw(land, w_arg[n][0], m_arg[n][0], v_arg[n][0], name="adamw_" + n)
            done = res[0]
            for d, r in zip(results, res):
                d[n] = r[None]
    stacks = [_exchange_wait(handle, after=[done], name=f"small_wait_{i}")[1][0] for i, handle in enumerate(smalls)]
    res = _adamw_small(stacks, small_views, small_w, small_m, small_v, name="adamw_small")
    for d, r in zip(results, res):
        d.update({n: r[n].reshape(w_arg[n].shape) for n in SMALL})
    out = [loss, grad_x[None]]
    for d in results:
        out += [d[n] for n in WEIGHTS]
    return tuple(out)
```

```python
import functools

import numpy as np
import jax
import jax.numpy as jnp
from jax import lax
from jax.experimental import pallas as pl
from jax.experimental.pallas import tpu as pltpu

F32 = jnp.float32
MXU_DTYPE = jnp.bfloat16
WIRE_DTYPE = jnp.bfloat16
VMEM_LIMIT_BYTES = 48 * 1024 * 1024
LANES = 128

N_DEV = 8
D_MODEL = 1024
SSM_GROUP = 16
SSM_WIDTH = 768
SSM_GROUPS = SSM_WIDTH // SSM_GROUP
SSM_STATE = 64
SSM_CH = SSM_GROUPS * SSM_STATE
SSM_TILES = SSM_WIDTH // LANES
GROUPS_PER_TILE = LANES // SSM_GROUP
STATE_VREG_ROWS = SSM_CH // LANES
ATT_HEAD_DIM = 64
ATT_HEADS_PER_GROUP = 4
ATT_MERGED = ATT_HEADS_PER_GROUP * ATT_HEAD_DIM
LANE_HALVES = ATT_MERGED // LANES
DILATIONS = (1, 4, 16)
ATT_BLK = 128
ATT_SCALE = ATT_HEAD_DIM ** -0.5
ROT_DIM = ATT_HEAD_DIM // 4
ROPE_THETA = 500000.0
XATT_HEADS = 4
XATT_HEAD_DIM = D_MODEL // XATT_HEADS
XATT_SCALE = XATT_HEAD_DIM ** -0.5
DEEPNORM_ALPHA = 2.0 ** 0.25
LN_EPS = 1e-5
NEG_INF = -1e30
OFF_Q_BLK, OFF_K_BLK, OFF_V_BLK = 3, 6, 9
OFF_GS_BLK, OFF_GA_BLK = 3, 4

ADAM_LR = 0.001
ADAM_B1 = 0.9
ADAM_B2 = 0.999
ADAM_EPS = 1e-08
ADAM_WD = 0.01
ADAM_STEP = 10

BIG = ("w_in", "w_glu", "w_att_up", "w_mix_out", "w_xq", "w_xkv", "w_xo", "w_ff1", "w_ff2")
BIG_COL_SHARDED = ("w_in", "w_glu", "w_att_up", "w_xkv", "w_ff1")
WEIGHTS = ("ln_in_g", "ln_in_b", "w_in", "b_in", "ssm_log_dt", "ssm_a_re", "ssm_a_im", "ssm_b_re", "ssm_b_im",
           "ssm_c_re", "ssm_c_im", "ssm_d", "w_glu", "b_glu", "w_att_up", "w_mix_out", "b_mix_out", "ln1_g", "ln1_b",
           "w_xq", "w_xkv", "w_xo", "ln2_g", "ln2_b", "w_ff1", "b_ff1", "w_ff2", "b_ff2", "ln3_g", "ln3_b")
SMALL = tuple(n for n in WEIGHTS if n not in BIG)
PACK_COLS = 1024
PACK_ROW_ALIGN = 16


def _params(*sem):
    return pltpu.CompilerParams(dimension_semantics=sem, vmem_limit_bytes=VMEM_LIMIT_BYTES)


def _dot(a, b, ca, cb):
    return lax.dot_general(a.astype(MXU_DTYPE), b.astype(MXU_DTYPE), (((ca,), (cb,)), ((), ())),
                           preferred_element_type=F32)


def _fit(dim, pref):
    if dim <= pref:
        return dim
    best = max(t for t in range(LANES, pref + 1, LANES) if dim % t == 0)
    return best


def _mm(a, b, *, name, ta=False, tb=False, bias=None, out_dtype=F32, b_shards=False, out_shards=False, after=None,
        also=None, gate=None, colsum=False, epilogue=None, tm=2048, tn=1024, tk=1024):
    m, k = (a.shape[1], a.shape[0]) if ta else a.shape
    order = (lambda f: (lambda j, i, kk: f(i, j, kk))) if colsum else (lambda f: f)
    spec = lambda shape, f: pl.BlockSpec(shape, order(f))
    if b_shards:
        n_sh, rows, n_loc = b.shape
        if tb:
            n, tn, tk = rows, _fit(rows, tn), n_loc
            assert k == n_sh * n_loc, (name, k, b.shape)
            b_spec = spec((1, tn, tk), lambda i, j, kk: (kk, j, 0))
        else:
            n, tn, tk = n_sh * n_loc, n_loc, _fit(k, tk)
            b_spec = spec((1, tk, tn), lambda i, j, kk: (j, kk, 0))
    else:
        n = b.shape[0] if tb else b.shape[1]
        tn = n // N_DEV if out_shards else _fit(n, tn)
        tk = _fit(k, tk)
        b_spec = spec((tn, tk), lambda i, j, kk: (j, kk)) if tb else spec((tk, tn), lambda i, j, kk: (kk, j))
    tm = _fit(m, tm)
    nk = k // tk
    a_spec = spec((tk, tm), lambda i, j, kk: (kk, i)) if ta else spec((tm, tk), lambda i, j, kk: (i, kk))
    tile = spec((tm, tn), lambda i, j, kk: (i, j))
    in_specs, args = [a_spec, b_spec], [a, b]
    if bias is not None:
        in_specs.append(spec((1, tn), lambda i, j, kk: (0, j)))
        args.append(bias)
    if gate is not None:
        in_specs.append(tile)
        args.append(gate[0])
    if after is not None:
        in_specs.append(pl.BlockSpec(memory_space=pl.ANY))
        args.append(after)
    if epilogue is not None:
        ep_fn, ep_rows, ep_fulls, ep_row_outs, ep_acc_outs = epilogue
        assert tn == n and not (colsum or also or gate or out_shards), name
        ep_first = len(args)
        in_specs += [spec((tm, t.shape[1]), lambda i, j, kk: (i, 0)) for t in ep_rows]
        in_specs += [pl.BlockSpec(t.shape, functools.partial(lambda i, j, kk, nd: (0,) * nd, nd=t.ndim)) for t in ep_fulls]
        args += [*ep_rows, *ep_fulls]
    n_in = len(args)
    if epilogue is not None:
        out_specs = [spec((tm, w), lambda i, j, kk: (i, 0)) for w, _ in ep_row_outs]
        out_specs += [spec((1, w), lambda i, j, kk: (0, 0)) for w in ep_acc_outs]
        out_shape = [jax.ShapeDtypeStruct((m, w), dt) for w, dt in ep_row_outs]
        out_shape += [jax.ShapeDtypeStruct((1, w), F32) for w in ep_acc_outs]
    elif out_shards:
        assert n == N_DEV * tn, (name, n, tn)
        out_specs = [spec((1, tm, tn), lambda i, j, kk: (j, i, 0))]
        out_shape = [jax.ShapeDtypeStruct((N_DEV, m, tn), out_dtype)]
    else:
        out_specs = [tile]
        out_shape = [jax.ShapeDtypeStruct((m, n), out_dtype)]
    if also is not None:
        out_specs.append(tile)
        out_shape.append(jax.ShapeDtypeStruct((m, n), also[1]))
    if colsum:
        out_specs.append(spec((1, tn), lambda i, j, kk: (0, j)))
        out_shape.append(jax.ShapeDtypeStruct((1, n), F32))

    def body(*refs):
        a_ref, b_ref = refs[0], refs[1]
        o_ref = refs[n_in]
        first_row_tile = pl.program_id(1 if colsum else 0) == 0

        def product():
            return _dot(a_ref[...], b_ref[0] if b_shards else b_ref[...], 0 if ta else 1, 1 if tb else 0)

        def finish(r):
            if bias is not None:
                r = r + refs[2][...]
            if gate is not None:
                r = r * gate[1](refs[2 + (bias is not None)][...])
            if epilogue is not None:
                res = ep_fn(r, *[ref[...] for ref in refs[ep_first:n_in]])
                n_o = len(ep_row_outs)
                for ref, val in zip(refs[n_in:n_in + n_o], res[:n_o]):
                    ref[...] = val.astype(ref.dtype)
                acc_refs = refs[n_in + n_o:n_in + n_o + len(ep_acc_outs)]
                if acc_refs:
                    @pl.when(first_row_tile)
                    def _():
                        for ref in acc_refs:
                            ref[...] = jnp.zeros_like(ref)

                    for ref, val in zip(acc_refs, res[n_o:]):
                        ref[...] += val
                return
            if out_shards:
                o_ref[0] = r.astype(o_ref.dtype)
            else:
                o_ref[...] = r.astype(o_ref.dtype)
            if also is not None:
                refs[n_in + 1][...] = also[0](r).astype(also[1])
            if colsum:
                s_ref = refs[n_in + 1 + (also is not None)]

                @pl.when(first_row_tile)
                def _():
                    s_ref[...] = jnp.zeros_like(s_ref)

                s_ref[...] += _colsum(r)

        if nk == 1:
            finish(product())
            return
        acc_ref = refs[-1]
        kk = pl.program_id(2)

        @pl.when(kk == 0)
        def _():
            acc_ref[...] = jnp.zeros_like(acc_ref)

        acc_ref[...] += product()

        @pl.when(kk == nk - 1)
        def _():
            finish(acc_ref[...])

    grid = (n // tn, m // tm, nk) if colsum else (m // tm, n // tn, nk)
    res = pl.pallas_call(
        body, name=name, grid=grid, in_specs=in_specs, out_specs=out_specs, out_shape=out_shape,
        scratch_shapes=[pltpu.VMEM((tm, tn), F32)] if nk > 1 else [],
        compiler_params=_params("arbitrary" if epilogue is not None else "parallel",
                                "arbitrary" if colsum else "parallel", "arbitrary"),
    )(*args)
    return res[0] if len(res) == 1 else res


def _mm_shards(a, w, bias, shard_ids, *, name, prev=None, tm=2048):
    m, k = a.shape
    n_sh, _, n_loc = w.shape
    tm = _fit(m, tm)

    def body(ids_ref, a_ref, w_ref, b_ref, *rest):
        rest[-1][...] = _dot(a_ref[...], w_ref[0], 1, 0) + b_ref[...]

    grid_spec = pltpu.PrefetchScalarGridSpec(
        num_scalar_prefetch=1, grid=(m // tm, shard_ids.shape[0]),
        in_specs=[pl.BlockSpec((tm, k), lambda i, j, ids: (i, 0)),
                  pl.BlockSpec((1, k, n_loc), lambda i, j, ids: (ids[j], 0, 0)),
                  pl.BlockSpec((1, n_loc), lambda i, j, ids: (0, ids[j]))]
        + [pl.BlockSpec(memory_space=pl.ANY)] * (prev is not None),
        out_specs=pl.BlockSpec((tm, n_loc), lambda i, j, ids: (i, ids[j])))
    return pl.pallas_call(
        body, name=name, grid_spec=grid_spec, out_shape=jax.ShapeDtypeStruct((m, n_sh * n_loc), F32),
        input_output_aliases={4: 0} if prev is not None else {}, compiler_params=_params("parallel", "arbitrary"),
    )(shard_ids, a, w, bias, *([prev] if prev is not None else []))


def _rowcall(fn, rows, fulls, row_outs, acc_outs=(), *, n_rows, tm, name, after=None):
    n_r, n_f, n_o, n_a = len(rows), len(fulls), len(row_outs), len(acc_outs)
    n_in = n_r + n_f + (after is not None)
    assert n_rows % tm == 0, (name, n_rows, tm)

    def body(*refs):
        res = fn(*[r[...] for r in refs[:n_r + n_f]])
        res = tuple(res) if isinstance(res, (tuple, list)) else (res,)
        o_refs = refs[n_in:n_in + n_o]
        a_refs = refs[n_in + n_o:]
        for o_ref, val in zip(o_refs, res[:n_o]):
            o_ref[...] = val.astype(o_ref.dtype)
        if n_a:
            @pl.when(pl.program_id(0) == 0)
            def _():
                for a_ref in a_refs:
                    a_ref[...] = jnp.zeros_like(a_ref)

            for a_ref, val in zip(a_refs, res[n_o:]):
                a_ref[...] += val

    in_specs = [pl.BlockSpec((tm, w), functools.partial(lambda i, cb: (i, cb), cb=cb)) for _, w, cb in rows]
    in_specs += [pl.BlockSpec(f.shape, functools.partial(lambda i, nd: (0,) * nd, nd=f.ndim)) for f in fulls]
    in_specs += [pl.BlockSpec(memory_space=pl.ANY)] * (after is not None)
    out_specs = [pl.BlockSpec((tm, w), lambda i: (i, 0)) for w, _ in row_outs]
    out_specs += [pl.BlockSpec((1, w), lambda i: (0, 0)) for w in acc_outs]
    out_shape = [jax.ShapeDtypeStruct((n_rows, w), dt) for w, dt in row_outs]
    out_shape += [jax.ShapeDtypeStruct((1, w), F32) for w in acc_outs]
    return pl.pallas_call(
        body, name=name, grid=(n_rows // tm,), in_specs=in_specs, out_specs=out_specs, out_shape=out_shape,
        compiler_params=_params("arbitrary" if n_a else "parallel"),
    )(*[r[0] for r in rows], *fulls, *([after] if after is not None else []))


def _colsum(v):
    return jnp.sum(v, axis=0, keepdims=True)


def _layer_norm(xin, g, b):
    mu = jnp.mean(xin, axis=-1, keepdims=True)
    xc = xin - mu
    var = jnp.mean(xc * xc, axis=-1, keepdims=True)
    rstd = lax.rsqrt(var + LN_EPS)
    xh = xc * rstd
    return xh * g + b, xh, rstd


def _layer_norm_bwd(dy, xh, rstd, g):
    dyg = dy * g
    m1 = jnp.mean(dyg, axis=-1, keepdims=True)
    m2 = jnp.mean(dyg * xh, axis=-1, keepdims=True)
    dx = rstd * (dyg - m1 - xh * m2)
    return dx, _colsum(dy * xh), _colsum(dy), _colsum(dx)


def _ln_fwd(a, g, b, *, name):
    n_rows, d = a.shape

    def fn(av, gv, bv):
        y, xh, rstd = _layer_norm(av, gv, bv)
        return y, xh, rstd, y

    return _rowcall(fn, [(a, d, 0)], [g, b], [(d, F32), (d, F32), (1, F32), (d, MXU_DTYPE)], n_rows=n_rows, tm=256,
                    name=name)


def _ln_bwd(dya, dyb, xh, rstd, g, *, alpha, name, operand=True):
    n_rows, d = xh.shape

    def fn(da, db, xhv, rs, gv):
        dx, *sums = _layer_norm_bwd(alpha * da + db, xhv, rs, gv)
        return (dx,) + ((dx,) if operand else ()) + tuple(sums)

    rows = [(dya, d, 0), (dyb, d, 0), (xh, d, 0), (rstd, 1, 0)]
    return _rowcall(fn, rows, [g], [(d, F32)] + [(d, MXU_DTYPE)] * operand, [d, d, d], n_rows=n_rows, tm=256, name=name)


def _mm_ln_fwd(x, w, bias, a, g, b, *, alpha, name):
    d = a.shape[1]

    def fn(r, av, gv, bv):
        y, xh, rstd = _layer_norm(alpha * av + r, gv, bv)
        return y, xh, rstd, y

    return _mm(x, w, bias=bias, name=name, tm=512,
               epilogue=(fn, [a], [g, b], [(d, F32), (d, F32), (1, F32), (d, MXU_DTYPE)], []))


def _mm_ln_bwd(x, w, dya, xh, rstd, g, *, alpha, name):
    d = xh.shape[1]

    def fn(r, da, xhv, rs, gv):
        dx, *sums = _layer_norm_bwd(alpha * da + r, xhv, rs, gv)
        return (dx, dx, *sums)

    return _mm(x, w, tb=True, name=name, tm=512,
               epilogue=(fn, [dya, xh, rstd], [g], [(d, F32), (d, MXU_DTYPE)], [d, d, d]))


def _ln_loss_bwd(a, r, target, g, b, *, alpha, name):
    n_rows, d = a.shape

    def fn(av, rv, tv, gv, bv):
        y, xh, rs = _layer_norm(alpha * av + rv, gv, bv)
        diff = y - tv
        part = jnp.sum(jnp.sum(diff * diff, axis=1, keepdims=True), axis=0, keepdims=True) * (0.5 / d)
        dx, *sums = _layer_norm_bwd(diff * (1.0 / d), xh, rs, gv)
        return (dx, dx, *sums, jnp.broadcast_to(part, (1, LANES)))

    return _rowcall(fn, [(a, d, 0), (r, d, 0), (target, d, 0)], [g, b], [(d, F32), (d, MXU_DTYPE)], [d, d, d, LANES],
                    n_rows=n_rows, tm=256, name=name)


def _rope_lane_constants():
    lane = np.arange(ATT_MERGED)
    in_head = lane % ATT_HEAD_DIM
    sign = np.where(in_head < ROT_DIM // 2, -1.0, np.where(in_head < ROT_DIM, 1.0, 0.0)).astype(np.float32)
    inv_freq = ROPE_THETA ** (-jnp.arange(0, ROT_DIM, 2, dtype=F32) / ROT_DIM)
    return inv_freq[lane % (ROT_DIM // 2)].reshape(1, ATT_MERGED), jnp.asarray(sign).reshape(1, ATT_MERGED)


def _rope_tables(pos_col, *, name, after=None):
    inv_lane, sign = _rope_lane_constants()

    def fn(pos, inv, sg):
        ang = pos.astype(F32) * inv
        return jnp.where(sg != 0.0, jnp.cos(ang), 1.0), sg * jnp.sin(ang)

    return _rowcall(fn, [(pos_col, 1, 0)], [inv_lane, sign], [(ATT_MERGED, F32), (ATT_MERGED, F32)],
                    n_rows=pos_col.shape[0], tm=512, name=name, after=after)


def _rot_partner(t):
    lane = lax.broadcasted_iota(jnp.int32, t.shape, 1)
    width = t.shape[1]
    return jnp.where((lane & (ROT_DIM // 2)) == 0, pltpu.roll(t, width - ROT_DIM // 2, 1), pltpu.roll(t, ROT_DIM // 2, 1))


def _rope(t, cos_t, sin_t):
    return t * cos_t + _rot_partner(t) * sin_t


def _rope_transpose(dt, cos_t, sin_t):
    return dt * cos_t + _rot_partner(dt * sin_t)


def _strided_rows(r, count, stride):
    return pl.ds(r, count) if stride == 1 else pl.ds(r, count, stride=stride)


def _qkv_split(proj, cos_t, sin_t, *, name, tm=512):
    n_rows = proj.shape[0]
    n_g = len(DILATIONS)

    def body(*refs):
        n_src = LANE_HALVES * 3 * n_g
        src, tables, dst = refs[:n_src], refs[n_src:n_src + 2 * LANE_HALVES], refs[n_src + 2 * LANE_HALVES:]
        for kind in range(3):
            for g, dil in enumerate(DILATIONS):
                for half in range(LANE_HALVES):
                    x_ref, o_ref = src[(kind * n_g + g) * LANE_HALVES + half], dst[kind * n_g + g]
                    cos_ref, sin_ref = tables[half], tables[LANE_HALVES + half]
                    for r in range(dil):
                        rows = _strided_rows(r, tm // dil, dil)
                        t = x_ref[rows, :]
                        if kind < 2:
                            t = _rope(t, cos_ref[rows, :], sin_ref[rows, :])
                        lo = r * ATT_MERGED + half * LANES
                        o_ref[:, lo:lo + LANES] = t.astype(o_ref.dtype)

    half_spec = lambda cb: pl.BlockSpec((tm, LANES), functools.partial(lambda i, cb: (i, cb), cb=cb))
    in_specs = [half_spec((off + g) * LANE_HALVES + half)
                for off in (OFF_Q_BLK, OFF_K_BLK, OFF_V_BLK) for g in range(n_g) for half in range(LANE_HALVES)]
    in_specs += [half_spec(half) for _ in range(2) for half in range(LANE_HALVES)]
    out_specs = [pl.BlockSpec((tm // dil, dil * ATT_MERGED), lambda i: (i, 0)) for _ in range(3) for dil in DILATIONS]
    out_shape = [jax.ShapeDtypeStruct((n_rows // dil, dil * ATT_MERGED), MXU_DTYPE) for _ in range(3) for dil in DILATIONS]
    outs = pl.pallas_call(
        body, name=name, grid=(n_rows // tm,), in_specs=in_specs, out_specs=out_specs, out_shape=out_shape,
        compiler_params=_params("parallel"),
    )(*[proj] * (LANE_HALVES * 3 * n_g), *[cos_t] * LANE_HALVES, *[sin_t] * LANE_HALVES)
    return outs[:n_g], outs[n_g:2 * n_g], outs[2 * n_g:]


def _mix(gs, ga, z1, z2, b_att):
    return jax.nn.sigmoid(gs) * (z1 * jax.nn.sigmoid(z2)) + jax.nn.sigmoid(ga) * b_att


def _mix_rows(proj, z, b_att):
    return [(proj, D_MODEL, OFF_GS_BLK), (proj, D_MODEL, OFF_GA_BLK), (z, D_MODEL, 0), (z, D_MODEL, 1), (b_att, D_MODEL, 0)]


def _mix_out_ln(proj, z, b_att, w, bias, a, g, b, *, alpha, name):
    def fn(gs, ga, z1, z2, ba, av, wv, biasv, gv, bv):
        mixed = _mix(gs, ga, z1, z2, ba)
        y, xh, rstd = _layer_norm(alpha * av + (_dot(mixed, wv, 1, 0) + biasv), gv, bv)
        return mixed, y, xh, rstd, y

    rows = _mix_rows(proj, z, b_att) + [(a, D_MODEL, 0)]
    outs = [(D_MODEL, MXU_DTYPE), (D_MODEL, F32), (D_MODEL, F32), (1, F32), (D_MODEL, MXU_DTYPE)]
    return _rowcall(fn, rows, [w, bias, g, b], outs, n_rows=proj.shape[0], tm=256, name=name)


def _mix_bwd(dmixed, proj, z, b_att, *, name):
    def fn(dm, gs, ga, z1, z2, ba):
        _, vjp = jax.vjp(_mix, gs, ga, z1, z2, ba)
        dgs, dga, dz1, dz2, dba = vjp(dm)
        dz = jnp.concatenate([dz1, dz2], axis=1)
        return dgs, dga, dz, dba, _colsum(dgs), _colsum(dga), _colsum(dz)

    rows = [(dmixed, D_MODEL, 0)] + _mix_rows(proj, z, b_att)
    widths = [D_MODEL, D_MODEL, 2 * D_MODEL, D_MODEL]
    return _rowcall(fn, rows, [], [(w, MXU_DTYPE) for w in widths], widths[:3], n_rows=proj.shape[0], tm=256, name=name)


def _gelu_bwd(dgy, y, proj, *, name):
    def fn(dg, yv, u):
        _, vjp = jax.vjp(jax.nn.gelu, yv)
        dy = vjp(dg)[0]
        return dy, _colsum(dy * u)

    return _rowcall(fn, [(dgy, SSM_WIDTH, 0), (y, SSM_WIDTH, 0), (proj, SSM_WIDTH, 0)], [], [(SSM_WIDTH, F32)],
                    [SSM_WIDTH], n_rows=y.shape[0], tm=512, name=name)


HEAD_ROWS = ATT_HEADS_PER_GROUP * ATT_BLK


def _head_masks(rows):
    head = lax.broadcasted_iota(jnp.int32, (rows, ATT_MERGED), 1) >> (ATT_HEAD_DIM.bit_length() - 1)
    return [head == h for h in range(ATT_HEADS_PER_GROUP)]


def _stack_heads(t, masks):
    return jnp.concatenate([jnp.where(m, t, jnp.zeros_like(t)) for m in masks], axis=0)


def _unstack_heads(t4, masks):
    blocks = [t4[h * ATT_BLK:(h + 1) * ATT_BLK] for h in range(ATT_HEADS_PER_GROUP)]
    return jnp.where(masks[0], blocks[0], jnp.where(masks[1], blocks[1], jnp.where(masks[2], blocks[2], blocks[3])))


def _head_column(stats, first):
    return jnp.concatenate([stats[:, first + h:first + h + 1] for h in range(ATT_HEADS_PER_GROUP)], axis=0)


def _band_mask(first_key):
    qi = lax.broadcasted_iota(jnp.int32, (HEAD_ROWS, 2 * ATT_BLK), 0) & (ATT_BLK - 1)
    ki = lax.broadcasted_iota(jnp.int32, (HEAD_ROWS, 2 * ATT_BLK), 1)
    steps = qi + ATT_BLK - ki
    return (steps >= 0) & (steps <= ATT_BLK) & (ki >= first_key)


def _dil_fwd(q, k, v, dil, *, name):
    n_blk = q.shape[0] // ATT_BLK
    cur = pl.BlockSpec((ATT_BLK, ATT_MERGED), lambda r, n: (n, r))
    prev = pl.BlockSpec((ATT_BLK, ATT_MERGED), lambda r, n: (jnp.maximum(n - 1, 0), r))

    def body(q_ref, kp_ref, kc_ref, vp_ref, vc_ref, o_ref, l_ref):
        masks = _head_masks(ATT_BLK)
        valid = _band_mask(jnp.where(pl.program_id(1) > 0, 0, ATT_BLK))
        keys = jnp.concatenate([kp_ref[...], kc_ref[...]], axis=0)
        vals = jnp.concatenate([vp_ref[...], vc_ref[...]], axis=0)
        s = jnp.where(valid, _dot(_stack_heads(q_ref[...], masks), keys, 1, 1) * ATT_SCALE, NEG_INF)
        m = jnp.max(s, axis=-1, keepdims=True)
        p = jnp.exp(s - m)
        den = jnp.sum(p, axis=-1, keepdims=True)
        o_ref[...] = _unstack_heads(_dot(p, vals, 1, 0) / den, masks)
        l_ref[...] = _unstack_heads(jnp.broadcast_to(m + jnp.log(den), (HEAD_ROWS, ATT_MERGED)), masks)

    shape = jax.ShapeDtypeStruct(q.shape, F32)
    return pl.pallas_call(
        body, name=name, grid=(dil, n_blk), in_specs=[cur, prev, cur, prev, cur], out_specs=[cur, cur],
        out_shape=[shape, shape], compiler_params=_params("parallel", "parallel"),
    )(q, k, k, v, v)


def _att_merge(outs, lses, *, name, tm=512):
    n_g = len(outs)
    n_rows = outs[0].shape[0] * DILATIONS[0]

    def body(*refs):
        src, (att_ref, lse_ref), tmp = refs[:2 * n_g], refs[2 * n_g:2 * n_g + 2], refs[2 * n_g + 2:]
        vals = []
        for idx, src_ref in enumerate(src):
            dil = DILATIONS[idx % n_g]
            if dil == 1:
                vals.append(src_ref[...])
                continue
            for r in range(dil):
                for half in range(LANE_HALVES):
                    lo = r * ATT_MERGED + half * LANES
                    tmp[LANE_HALVES * idx + half][_strided_rows(r, tm // dil, dil), :] = src_ref[:, lo:lo + LANES]
            vals.append(jnp.concatenate([tmp[LANE_HALVES * idx + half][...] for half in range(LANE_HALVES)], axis=1))
        o, l = vals[:n_g], vals[n_g:]
        m = functools.reduce(jnp.maximum, l)
        e = [jnp.exp(li - m) for li in l]
        z = functools.reduce(jnp.add, e)
        att_ref[...] = functools.reduce(jnp.add, [(ei / z) * oi for ei, oi in zip(e, o)])
        lse_ref[...] = m + jnp.log(z)

    in_specs = [pl.BlockSpec((tm // dil, dil * ATT_MERGED), lambda i: (i, 0)) for _ in range(2) for dil in DILATIONS]
    row = pl.BlockSpec((tm, ATT_MERGED), lambda i: (i, 0))
    shape = jax.ShapeDtypeStruct((n_rows, ATT_MERGED), F32)
    return pl.pallas_call(
        body, name=name, grid=(n_rows // tm,), in_specs=in_specs, out_specs=[row, row], out_shape=[shape, shape],
        scratch_shapes=[pltpu.VMEM((tm, LANES), F32)] * (LANE_HALVES * 2 * n_g), compiler_params=_params("parallel"),
    )(*outs, *lses)


def _att_stats(datt, att, lse, *, name):
    n_rows = datt.shape[0]

    def fn(d, a, l):
        prod = d * a
        lane = lax.broadcasted_iota(jnp.int32, (d.shape[0], LANES), 1)
        out = jnp.zeros((d.shape[0], LANES), F32)
        for h in range(ATT_HEADS_PER_GROUP):
            lo = h * ATT_HEAD_DIM
            out = jnp.where(lane == h, l[:, lo:lo + 1], out)
            delta = jnp.sum(prod[:, lo:lo + ATT_HEAD_DIM], axis=-1, keepdims=True)
            out = jnp.where(lane == ATT_HEADS_PER_GROUP + h, delta, out)
        return out

    rows = [(t, ATT_MERGED, 0) for t in (datt, att, lse)]
    return _rowcall(fn, rows, [], [(LANES, F32)], n_rows=n_rows, tm=512, name=name)[0]


def _dil_bwd(q, k, v, datt, stats, dil, *, name):
    n_rows = datt.shape[0]
    n_blk = n_rows // dil // ATT_BLK
    span = ATT_BLK * dil
    cur = pl.BlockSpec((ATT_BLK, ATT_MERGED), lambda n, r: (n, r))
    prev = pl.BlockSpec((ATT_BLK, ATT_MERGED), lambda n, r: (jnp.maximum(n - 1, 0), r))
    nxt = pl.BlockSpec((ATT_BLK, ATT_MERGED), lambda n, r: (jnp.minimum(n + 1, n_blk - 1), r))
    seq = lambda half, ahead: pl.BlockSpec((span, LANES), lambda n, r: (jnp.minimum(n + ahead, n_blk - 1), half))

    def body(qc_ref, qn_ref, kp_ref, kc_ref, vp_ref, vc_ref, dc0_ref, dc1_ref, dn0_ref, dn1_ref, sc_ref, sn_ref,
             dq0_ref, dq1_ref, dk0_ref, dk1_ref, dv0_ref, dv1_ref):
        n = pl.program_id(0)
        rows = slice(None) if dil == 1 else _strided_rows(pl.program_id(1), ATT_BLK, dil)

        def read(ref0, ref1):
            return jnp.concatenate([ref0[rows, :], ref1[rows, :]], axis=1)

        def write(ref0, ref1, val):
            ref0[rows, :] = val[:, :LANES]
            ref1[rows, :] = val[:, LANES:]

        masks = _head_masks(ATT_BLK)
        valid = _band_mask(jnp.where(n > 0, 0, ATT_BLK))
        qi = lax.broadcasted_iota(jnp.int32, (HEAD_ROWS, ATT_BLK), 0) & (ATT_BLK - 1)
        ki = lax.broadcasted_iota(jnp.int32, (HEAD_ROWS, ATT_BLK), 1)
        valid_next = (ki - qi) >= jnp.where(n < n_blk - 1, 0, ATT_BLK)

        kc, vc = kc_ref[...], vc_ref[...]
        keys = jnp.concatenate([kp_ref[...], kc], axis=0)
        vals = jnp.concatenate([vp_ref[...], vc], axis=0)
        q4 = _stack_heads(qc_ref[...], masks)
        d4 = _stack_heads(read(dc0_ref, dc1_ref).astype(MXU_DTYPE), masks)
        st = sc_ref[rows, :]
        p = jnp.where(valid, jnp.exp(_dot(q4, keys, 1, 1) * ATT_SCALE - _head_column(st, 0)), 0.0)
        ds = p * (_dot(d4, vals, 1, 1) - _head_column(st, ATT_HEADS_PER_GROUP)) * ATT_SCALE
        write(dq0_ref, dq1_ref, _unstack_heads(_dot(ds, keys, 1, 0), masks))

        q4n = _stack_heads(qn_ref[...], masks)
        d4n = _stack_heads(read(dn0_ref, dn1_ref).astype(MXU_DTYPE), masks)
        stn = sn_ref[rows, :]
        p_n = jnp.where(valid_next, jnp.exp(_dot(q4n, kc, 1, 1) * ATT_SCALE - _head_column(stn, 0)), 0.0)
        ds_n = p_n * (_dot(d4n, vc, 1, 1) - _head_column(stn, ATT_HEADS_PER_GROUP)) * ATT_SCALE
        write(dv0_ref, dv1_ref, _dot(p[:, ATT_BLK:], d4, 0, 0) + _dot(p_n, d4n, 0, 0))
        write(dk0_ref, dk1_ref, _dot(ds[:, ATT_BLK:], q4, 0, 0) + _dot(ds_n, q4n, 0, 0))

    shape = jax.ShapeDtypeStruct((n_rows, LANES), F32)
    out = seq(0, 0)
    res = pl.pallas_call(
        body, name=name, grid=(n_blk, dil),
        in_specs=[cur, nxt, prev, cur, prev, cur, seq(0, 0), seq(1, 0), seq(0, 1), seq(1, 1), seq(0, 0), seq(0, 1)],
        out_specs=[out] * 6, out_shape=[shape] * 6, compiler_params=_params("parallel", "arbitrary"),
    )(q, q, k, k, v, v, datt, datt, datt, datt, stats, stats)
    return [(res[2 * i], res[2 * i + 1]) for i in range(3)]


def _dproj_assemble(du, dqkv, dgs, dga, cos_t, sin_t, *, name):
    n_g = len(DILATIONS)

    def fn(*t):
        n_half = LANE_HALVES * 3 * n_g
        du_t, halves, (dgs_t, dga_t, c, s) = t[0], t[1:1 + n_half], t[1 + n_half:]
        parts = [jnp.concatenate(halves[LANE_HALVES * i:LANE_HALVES * (i + 1)], axis=1) for i in range(3 * n_g)]
        for i in range(2 * n_g):
            parts[i] = _rope_transpose(parts[i], c, s)
        cast = [p.astype(MXU_DTYPE) for p in parts]
        return [jnp.concatenate([du_t] + cast + [dgs_t, dga_t], axis=1)] + [_colsum(p) for p in parts]

    rows = [(du, SSM_WIDTH, 0)]
    rows += [(half, LANES, 0) for i in range(3) for g in range(n_g) for half in dqkv[g][i]]
    rows += [(dgs, D_MODEL, 0), (dga, D_MODEL, 0), (cos_t, ATT_MERGED, 0), (sin_t, ATT_MERGED, 0)]
    width = SSM_WIDTH + 3 * n_g * ATT_MERGED + 2 * D_MODEL
    res = _rowcall(fn, rows, [], [(width, MXU_DTYPE)], [ATT_MERGED] * (3 * n_g), n_rows=du.shape[0], tm=256, name=name)
    return res[0], res[1:]


def _xhead(h):
    return slice(h * XATT_HEAD_DIM, (h + 1) * XATT_HEAD_DIM)


def _xatt_probs(qh, kh):
    s = _dot(qh, kh, 1, 1) * XATT_SCALE
    e = jnp.exp(s - jnp.max(s, axis=-1, keepdims=True))
    return e / jnp.sum(e, axis=-1, keepdims=True)


def _xatt_fwd(q, kv, *, name, tm=512):
    n_rows = q.shape[0]
    n_mem = kv.shape[0]

    def body(q_ref, kv_ref, o_ref):
        for h in range(XATT_HEADS):
            sl = _xhead(h)
            p = _xatt_probs(q_ref[:, sl], kv_ref[:, sl])
            o_ref[:, sl] = _dot(p, kv_ref[:, D_MODEL + h * XATT_HEAD_DIM:D_MODEL + (h + 1) * XATT_HEAD_DIM], 1, 0
                                ).astype(o_ref.dtype)

    row = pl.BlockSpec((tm, D_MODEL), lambda i: (i, 0))
    return pl.pallas_call(
        body, name=name, grid=(n_rows // tm,),
        in_specs=[row, pl.BlockSpec((n_mem, 2 * D_MODEL), lambda i: (0, 0))], out_specs=row,
        out_shape=jax.ShapeDtypeStruct((n_rows, D_MODEL), MXU_DTYPE), compiler_params=_params("parallel"),
    )(q, kv)


def _xatt_bwd(q, kv, do, *, name, tm=512):
    n_rows = q.shape[0]
    n_mem = kv.shape[0]

    def body(q_ref, kv_ref, do_ref, dq_ref, dkv_ref):
        @pl.when(pl.program_id(0) == 0)
        def _():
            dkv_ref[...] = jnp.zeros_like(dkv_ref)

        for h in range(XATT_HEADS):
            sl = _xhead(h)
            vsl = slice(D_MODEL + h * XATT_HEAD_DIM, D_MODEL + (h + 1) * XATT_HEAD_DIM)
            qh, kh, doh = q_ref[:, sl], kv_ref[:, sl], do_ref[:, sl]
            p = _xatt_probs(qh, kh)
            dp = _dot(doh, kv_ref[:, vsl], 1, 1)
            ds = p * (dp - jnp.sum(dp * p, axis=-1, keepdims=True)) * XATT_SCALE
            dq_ref[:, sl] = _dot(ds, kh, 1, 0).astype(dq_ref.dtype)
            dkv_ref[:, sl] += _dot(ds, qh, 0, 0)
            dkv_ref[:, vsl] += _dot(p, doh, 0, 0)

    row = pl.BlockSpec((tm, D_MODEL), lambda i: (i, 0))
    full = pl.BlockSpec((n_mem, 2 * D_MODEL), lambda i: (0, 0))
    return pl.pallas_call(
        body, name=name, grid=(n_rows // tm,), in_specs=[row, full, row], out_specs=[row, full],
        out_shape=[jax.ShapeDtypeStruct((n_rows, D_MODEL), MXU_DTYPE), jax.ShapeDtypeStruct((n_mem, 2 * D_MODEL), F32)],
        compiler_params=_params("arbitrary"),
    )(q, kv, do)


def _disc(logdt, a_re, a_im, b_re, b_im):
    dt = jnp.exp(logdt)
    mag = jnp.exp(a_re * dt)
    ab_re = mag * jnp.cos(a_im * dt)
    ab_im = mag * jnp.sin(a_im * dt)
    den = jnp.square(a_re) + jnp.square(a_im)
    nr = ab_re - 1.0
    f_re = (nr * a_re + ab_im * a_im) / den
    f_im = (ab_im * a_re - nr * a_im) / den
    bb_re = f_re[None] * b_re - f_im[None] * b_im
    bb_im = f_re[None] * b_im + f_im[None] * b_re
    return ab_re, ab_im, bb_re, bb_im


def _disc_transpose(logdt, a_re, a_im, b_re, b_im, g_ab_re, g_ab_im, g_bb_re, g_bb_im):
    dt = jnp.exp(logdt)
    mag = jnp.exp(a_re * dt)
    th = a_im * dt
    cs, sn = jnp.cos(th), jnp.sin(th)
    ab_re, ab_im = mag * cs, mag * sn
    den = jnp.square(a_re) + jnp.square(a_im)
    nr = ab_re - 1.0
    f_re = (nr * a_re + ab_im * a_im) / den
    f_im = (ab_im * a_re - nr * a_im) / den
    d_f_re = jnp.sum(g_bb_re * b_re + g_bb_im * b_im, axis=0)
    d_f_im = jnp.sum(g_bb_im * b_re - g_bb_re * b_im, axis=0)
    d_b_re = g_bb_re * f_re[None] + g_bb_im * f_im[None]
    d_b_im = g_bb_im * f_re[None] - g_bb_re * f_im[None]
    d_n_re, d_n_im = d_f_re / den, d_f_im / den
    d_den = -(d_f_re * f_re + d_f_im * f_im) / den
    d_ab_re = g_ab_re + d_n_re * a_re - d_n_im * a_im
    d_ab_im = g_ab_im + d_n_re * a_im + d_n_im * a_re
    d_a_re = d_n_re * nr + d_n_im * ab_im + 2.0 * d_den * a_re
    d_a_im = d_n_re * ab_im - d_n_im * nr + 2.0 * d_den * a_im
    d_mag = d_ab_re * cs + d_ab_im * sn
    d_th = mag * (d_ab_im * cs - d_ab_re * sn)
    d_a_re = d_a_re + d_mag * mag * dt
    d_a_im = d_a_im + d_th * dt
    d_dt = jnp.sum(d_mag * mag * a_re + d_th * a_im, axis=-1, keepdims=True)
    return d_dt * dt, d_a_re, d_a_im, d_b_re, d_b_im


def _full_spec(shape):
    return pl.BlockSpec(tuple(shape), functools.partial(lambda i, nd: (0,) * nd, nd=len(shape)))


def _whole(fn, args, out_shapes, *, name):
    n_in = len(args)

    def body(*refs):
        res = fn(*[r[...] for r in refs[:n_in]])
        for o_ref, val in zip(refs[n_in:], res):
            o_ref[...] = val

    return pl.pallas_call(
        body, name=name, grid=(1,), in_specs=[_full_spec(t.shape) for t in args],
        out_specs=[_full_spec(s) for s in out_shapes], out_shape=[jax.ShapeDtypeStruct(s, F32) for s in out_shapes],
        compiler_params=_params("arbitrary"))(*args)


SSM_WIDE =GROUPS_PER_TILE * SSM_STATE
LANE_GROUPS_PER_TILE = SSM_WIDE // LANES


def _chan(j):
    return slice(j * LANES, (j + 1) * LANES)


def _time_major_rows(j, q, tc):
    return pl.ds(j * LANE_GROUPS_PER_TILE + q, tc, stride=STATE_VREG_ROWS)


def _to_time_major(x, t_re_ref, t_im_ref, dst_re, dst_im, tc):
    for j in range(SSM_TILES):
        xj = x[:, _chan(j)]
        for t_ref, dst in ((t_re_ref, dst_re), (t_im_ref, dst_im)):
            r = _dot(xj, t_ref[j], 1, 0)
            for q in range(LANE_GROUPS_PER_TILE):
                dst[_time_major_rows(j, q, tc), :] = r[:, q * LANES:(q + 1) * LANES]


def _from_time_major(src, j, tc):
    return jnp.concatenate([src[_time_major_rows(j, q, tc), :] for q in range(LANE_GROUPS_PER_TILE)], axis=1)


def _scan_chunk(w_re, w_im, h_re, h_im, a_re, a_im, start, tc):
    def step(t, carry):
        hr, hi = carry
        rows = _scan_rows(t)
        nr = a_re * hr - a_im * hi + w_re[rows, :]
        ni = a_re * hi + a_im * hr + w_im[rows, :]
        h_re[rows, :] = nr
        h_im[rows, :] = ni
        return nr, ni

    return lax.fori_loop(0, tc, step, start, unroll=8)


SSM_CHUNK = 256


def _tile_spec(stack, k):
    return pl.BlockSpec((pl.Squeezed(),) + tuple(stack.shape[1:]), lambda i: (k, 0, 0, 0))


def _expand_block_diagonal(src_ref, dst):
    dst[...] = jnp.zeros_like(dst)
    r, c = src_ref.shape[1:]
    for g in range(SSM_GROUPS):
        j, gl = divmod(g, GROUPS_PER_TILE)
        dst[j, gl * r:(gl + 1) * r, gl * c:(gl + 1) * c] = src_ref[g].astype(dst.dtype)


def _extract_block_diagonal(src, dst_ref):
    r, c = dst_ref.shape[1:]
    for g in range(SSM_GROUPS):
        j, gl = divmod(g, GROUPS_PER_TILE)
        dst_ref[g] = src[j, gl * r:(gl + 1) * r, gl * c:(gl + 1) * c]


def _ssm_fwd(proj, blocks_cn, blocks_nc, a_re, a_im, gain, *, name, tc=SSM_CHUNK):
    n_rows = proj.shape[0]
    n_chunk = n_rows // tc

    def body(u_ref, br_ref, bi_ref, cr_ref, ci_ref, ar_ref, ai_ref, g_ref, y_ref, gy_ref, hr, hi, wr, wi, state,
             tbr_ref, tbi_ref, tcr_ref, tci_ref):
        @pl.when(pl.program_id(0) == 0)
        def _():
            state[...] = jnp.zeros_like(state)
            for src_ref, dst in ((br_ref, tbr_ref), (bi_ref, tbi_ref), (cr_ref, tcr_ref), (ci_ref, tci_ref)):
                _expand_block_diagonal(src_ref, dst)

        u = u_ref[...]
        _to_time_major(u, tbr_ref, tbi_ref, wr, wi, tc)
        state[0], state[1] = _scan_chunk(wr, wi, hr, hi, ar_ref[...], ai_ref[...], (state[0], state[1]), tc)
        for j in range(SSM_TILES):
            yj = (_dot(_from_time_major(hr, j, tc), tcr_ref[j], 1, 0) + _dot(_from_time_major(hi, j, tc), tci_ref[j], 1, 0)
                  + g_ref[:, _chan(j)] * u[:, _chan(j)])
            y_ref[:, _chan(j)] = yj
            gy_ref[:, _chan(j)] = jax.nn.gelu(yj).astype(gy_ref.dtype)

    rows = pl.BlockSpec((tc, SSM_WIDTH), lambda i: (i, 0))
    coef = pl.BlockSpec((STATE_VREG_ROWS, LANES), lambda i: (0, 0))
    states = pl.BlockSpec((tc * STATE_VREG_ROWS, LANES), lambda i: (i, 0))
    sshape = jax.ShapeDtypeStruct((n_rows * STATE_VREG_ROWS, LANES), F32)
    return pl.pallas_call(
        body, name=name, grid=(n_chunk,),
        in_specs=[rows, _tile_spec(blocks_cn, 0), _tile_spec(blocks_cn, 1), _tile_spec(blocks_nc, 0),
                  _tile_spec(blocks_nc, 1), coef, coef, pl.BlockSpec((1, SSM_WIDTH), lambda i: (0, 0))],
        out_specs=[rows, rows, states, states],
        out_shape=[jax.ShapeDtypeStruct((n_rows, SSM_WIDTH), F32), jax.ShapeDtypeStruct((n_rows, SSM_WIDTH), MXU_DTYPE),
                   sshape, sshape],
        scratch_shapes=[pltpu.VMEM((tc * STATE_VREG_ROWS, LANES), F32)] * 2 + [pltpu.VMEM((2, STATE_VREG_ROWS, LANES), F32)]
        + [pltpu.VMEM((SSM_TILES, LANES, SSM_WIDE), MXU_DTYPE)] * 2 + [pltpu.VMEM((SSM_TILES, SSM_WIDE, LANES), MXU_DTYPE)] * 2,
        compiler_params=_params("arbitrary"),
    )(proj, blocks_cn, blocks_cn, blocks_nc, blocks_nc, a_re, a_im, gain)


def _ssm_bwd(proj, dy, h_re, h_im, blocks_cn, blocks_nc, a_re, a_im, gain, *, name, tc=SSM_CHUNK):
    n_rows = proj.shape[0]
    n_chunk = n_rows // tc

    def body(u_ref, dy_ref, hr, hi, cr_ref, ci_ref, br_ref, bi_ref, ar_ref, ai_ref, g_ref,
             du_ref, su_ref, dc_re_ref, dc_im_ref, db_re_ref, db_im_ref, dar_ref, dai_ref, wr, wi, carry,
             tdr_ref, tdi_ref, tur_ref, tui_ref, dcr_ref, dci_ref, dbr_ref, dbi_ref):
        @pl.when(pl.program_id(0) == 0)
        def _():
            carry[...] = jnp.zeros_like(carry)
            for acc_ref in (su_ref, dcr_ref, dci_ref, dbr_ref, dbi_ref):
                acc_ref[...] = jnp.zeros_like(acc_ref)
            for src_ref, dst in ((cr_ref, tdr_ref), (ci_ref, tdi_ref), (br_ref, tur_ref), (bi_ref, tui_ref)):
                _expand_block_diagonal(src_ref, dst)

        a_r, a_i = ar_ref[...], ai_ref[...]
        u, dyv = u_ref[...], dy_ref[...]
        _to_time_major(dyv, tdr_ref, tdi_ref, wr, wi, tc)

        def step(kk, c):
            lam_r, lam_i, dar, dai = c
            rows = _scan_rows(tc - 1 - kk)
            h_r, h_i = hr[rows, :], hi[rows, :]
            dar = dar + lam_r * h_r + lam_i * h_i
            dai = dai + lam_i * h_r - lam_r * h_i
            new_r = wr[rows, :] + a_r * lam_r + a_i * lam_i
            new_i = wi[rows, :] + a_r * lam_i - a_i * lam_r
            wr[rows, :] = new_r
            wi[rows, :] = new_i
            return new_r, new_i, dar, dai

        carry[0], carry[1], carry[2], carry[3] = lax.fori_loop(0, tc, step, (carry[0], carry[1], carry[2], carry[3]),
                                                              unroll=8)
        dar_ref[...] = carry[2]
        dai_ref[...] = carry[3]
        for j in range(SSM_TILES):
            cj = _chan(j)
            lam_r, lam_i = _from_time_major(wr, j, tc), _from_time_major(wi, j, tc)
            dcr_ref[j] += _dot(dyv[:, cj], _from_time_major(hr, j, tc), 0, 0)
            dci_ref[j] += _dot(dyv[:, cj], _from_time_major(hi, j, tc), 0, 0)
            dbr_ref[j] += _dot(u[:, cj], lam_r, 0, 0)
            dbi_ref[j] += _dot(u[:, cj], lam_i, 0, 0)
            duj = _dot(lam_r, tur_ref[j], 1, 0) + _dot(lam_i, tui_ref[j], 1, 0) + g_ref[:, cj] * dyv[:, cj]
            du_ref[:, cj] = duj.astype(du_ref.dtype)
            su_ref[:, cj] += _colsum(duj)

        @pl.when(pl.program_id(0) == n_chunk - 1)
        def _():
            for src, dst_ref in ((dcr_ref, dc_re_ref), (dci_ref, dc_im_ref), (dbr_ref, db_re_ref), (dbi_ref, db_im_ref)):
                _extract_block_diagonal(src, dst_ref)

    back = lambda i: (n_chunk - 1 - i, 0)
    rows = pl.BlockSpec((tc, SSM_WIDTH), back)
    blocks = pl.BlockSpec((SSM_GROUPS, SSM_GROUP, SSM_STATE), lambda i: (0, 0, 0))
    coef = pl.BlockSpec((STATE_VREG_ROWS, LANES), lambda i: (0, 0))
    states = pl.BlockSpec((tc * STATE_VREG_ROWS, LANES), back)
    vec = pl.BlockSpec((1, SSM_WIDTH), lambda i: (0, 0))
    bshape = jax.ShapeDtypeStruct((SSM_GROUPS, SSM_GROUP, SSM_STATE), F32)
    cshape = jax.ShapeDtypeStruct((STATE_VREG_ROWS, LANES), F32)
    return pl.pallas_call(
        body, name=name, grid=(n_chunk,),
        in_specs=[rows, rows, states, states, _tile_spec(blocks_cn, 2), _tile_spec(blocks_cn, 3), _tile_spec(blocks_nc, 2),
                  _tile_spec(blocks_nc, 3), coef, coef, vec],
        out_specs=[rows, vec, blocks, blocks, blocks, blocks, coef, coef],
        out_shape=[jax.ShapeDtypeStruct((n_rows, SSM_WIDTH), MXU_DTYPE), jax.ShapeDtypeStruct((1, SSM_WIDTH), F32),
                   bshape, bshape, bshape, bshape, cshape, cshape],
        scratch_shapes=[pltpu.VMEM((tc * STATE_VREG_ROWS, LANES), F32)] * 2 + [pltpu.VMEM((4, STATE_VREG_ROWS, LANES), F32)]
        + [pltpu.VMEM((SSM_TILES, LANES, SSM_WIDE), MXU_DTYPE)] * 2 + [pltpu.VMEM((SSM_TILES, SSM_WIDE, LANES), MXU_DTYPE)] * 2
        + [pltpu.VMEM((SSM_TILES, LANES, SSM_WIDE), F32)] * 4,
        compiler_params=_params("arbitrary"),
    )(proj, dy, h_re, h_im, blocks_cn, blocks_cn, blocks_nc, blocks_nc, a_re, a_im, gain)


def _scan_rows(t):
    return pl.ds(pl.multiple_of(t * STATE_VREG_ROWS, 8), STATE_VREG_ROWS)


GATHER_GROUPS = (("w_glu", "w_att_up", "w_mix_out"), ("w_xq", "w_xkv", "w_xo", "w_ff1", "w_ff2"))
SCATTER_GROUPS = (("w_ff2", "w_ff1"), ("w_xo", "w_xq", "w_xkv", "w_mix_out"), ("w_att_up", "w_glu"), ("w_in",))


def _local_grads(x, mem, pos_col, target, sm, fetch_in, fetch, send, send_small, start_token):
    b_re_t = sm["ssm_b_re"].transpose(2, 0, 1)
    b_im_t = sm["ssm_b_im"].transpose(2, 0, 1)
    logdt = sm["ssm_log_dt"].reshape(SSM_GROUPS, 1)
    c_re, c_im = sm["ssm_c_re"], sm["ssm_c_im"]
    grp = (SSM_GROUPS, SSM_STATE)
    chn = (SSM_GROUP, SSM_GROUPS, SSM_STATE)

    wts = {}
    cos_t, sin_t = _rope_tables(pos_col, after=start_token, name="rope_tables")
    h0, xh0, rs0, h0m = _ln_fwd(x, sm["ln_in_g"], sm["ln_in_b"], name="ln_in_fwd")
    disc_in = (logdt, sm["ssm_a_re"], sm["ssm_a_im"], b_re_t, b_im_t)
    ab_re, ab_im, bb_re_t, bb_im_t = _whole(_disc, disc_in, [grp, grp, chn, chn], name="ssm_disc")
    a_re_rows, a_im_rows = ab_re.reshape(STATE_VREG_ROWS, LANES), ab_im.reshape(STATE_VREG_ROWS, LANES)
    tiles_cn = jnp.stack([bb_re_t.transpose(1, 0, 2), bb_im_t.transpose(1, 0, 2), c_re, -c_im])
    tiles_nc = jnp.stack([c_re.transpose(0, 2, 1), -c_im.transpose(0, 2, 1), bb_re_t.transpose(1, 2, 0),
                          bb_im_t.transpose(1, 2, 0)])
    w_in_near, near_ids = fetch_in(0, [h0m, tiles_cn, tiles_nc])
    proj = _mm_shards(h0m, w_in_near, sm["b_in"], near_ids, name="in_proj_near")
    wts["w_in"], far_ids = fetch_in(1, [proj])
    proj = _mm_shards(h0m, wts["w_in"], sm["b_in"], far_ids, prev=proj, name="in_proj_far")

    y, gy, h_re, h_im = _ssm_fwd(proj, tiles_cn, tiles_nc, a_re_rows, a_im_rows, sm["ssm_d"], name="ssm_fwd")

    q, k, v = _qkv_split(proj, cos_t, sin_t, name="qkv_split")
    outs, lses = [], []
    for g, dil in enumerate(DILATIONS):
        o_g, l_g = _dil_fwd(q[g], k[g], v[g], dil, name=f"dil_att_fwd_{dil}")
        outs.append(o_g)
        lses.append(l_g)
    att, lse = _att_merge(outs, lses, name="att_merge")
    wts.update(fetch(0, [att]))
    z = _mm(gy, wts["w_glu"], bias=sm["b_glu"], b_shards=True, name="glu_proj")
    b_att = _mm(att, wts["w_att_up"], b_shards=True, name="att_up")

    mixed, h1, xh1, rs1, h1m = _mix_out_ln(proj, z, b_att, wts["w_mix_out"], sm["b_mix_out"], h0, sm["ln1_g"],
                                           sm["ln1_b"], alpha=DEEPNORM_ALPHA, name="gate_mix_out_ln1")

    wts.update(fetch(1, [h1m]))
    xq = _mm(h1m, wts["w_xq"], out_dtype=MXU_DTYPE, name="xatt_q")
    kv = _mm(mem, wts["w_xkv"], out_dtype=MXU_DTYPE, b_shards=True, name="xatt_kv")
    xo_in = _xatt_fwd(xq, kv, name="xatt_fwd")
    h2, xh2, rs2, h2m = _mm_ln_fwd(xo_in, wts["w_xo"], None, h1, sm["ln2_g"], sm["ln2_b"], alpha=DEEPNORM_ALPHA,
                                   name="xatt_o_ln2")

    pre, act = _mm(h2m, wts["w_ff1"], bias=sm["b_ff1"], b_shards=True, name="ff1",
                   also=(lambda r: jnp.square(jnp.maximum(r, 0.0)), MXU_DTYPE))
    ff = _mm(act, wts["w_ff2"], bias=sm["b_ff2"], name="ff2")

    gw, gs = {}, {}
    dr3, dr3m, gs["ln3_g"], gs["ln3_b"], gs["b_ff2"], loss_row = _ln_loss_bwd(
        h2, ff, target, sm["ln3_g"], sm["ln3_b"], alpha=DEEPNORM_ALPHA, name="ln3_loss")
    wgrad = functools.partial(_mm, ta=True, out_dtype=WIRE_DTYPE, tk=2048)
    gw["w_ff2"] = wgrad(act, dr3m, tk=1024, name="ff2_dw")
    dpre, gs["b_ff1"] = _mm(dr3m, wts["w_ff2"], tb=True, out_dtype=MXU_DTYPE, colsum=True, name="ff2_dx",
                            gate=(pre, lambda p: 2.0 * jnp.maximum(p, 0.0)))
    gw["w_ff1"] = wgrad(h2m, dpre, out_shards=True, name="ff1_dw")
    sent = send(0, gw)
    dh2 = _mm(dpre, wts["w_ff1"], tb=True, b_shards=True, after=sent, name="ff1_dx")
    dr2, dr2m, gs["ln2_g"], gs["ln2_b"], _ = _ln_bwd(dr3, dh2, xh2, rs2, sm["ln2_g"], alpha=DEEPNORM_ALPHA,
                                                     name="ln2_bwd")
    gw["w_xo"] = wgrad(xo_in, dr2m, name="xatt_o_dw")
    dxo_in = _mm(dr2m, wts["w_xo"], tb=True, out_dtype=MXU_DTYPE, name="xatt_o_dx")
    dxq, dkv = _xatt_bwd(xq, kv, dxo_in, name="xatt_bwd")
    gw["w_xq"] = wgrad(h1m, dxq, name="xatt_q_dw")
    gw["w_xkv"] = wgrad(mem, dkv, out_shards=True, name="xatt_kv_dw")
    dr1, dr1m, gs["ln1_g"], gs["ln1_b"], gs["b_mix_out"] = _mm_ln_bwd(
        dxq, wts["w_xq"], dr2, xh1, rs1, sm["ln1_g"], alpha=DEEPNORM_ALPHA, name="xatt_q_dx_ln1")
    gw["w_mix_out"] = wgrad(mixed, dr1m, name="mix_out_dw")
    sent = send(1, gw)
    dmixed = _mm(dr1m, wts["w_mix_out"], tb=True, after=sent, name="mix_out_dx")
    dgs, dga, dz, db_att, s_gs, s_ga, gs["b_glu"] = _mix_bwd(dmixed, proj, z, b_att, name="gate_mix_bwd")

    gw["w_att_up"] = wgrad(att, db_att, out_shards=True, name="att_up_dw")
    gw["w_glu"] = wgrad(gy, dz, out_shards=True, name="glu_dw")
    sent = send(2, gw)
    datt = _mm(db_att, wts["w_att_up"], tb=True, b_shards=True, after=sent, name="att_up_dx")
    stats = _att_stats(datt, att, lse, name="att_stats")
    dqkv = [_dil_bwd(q[g], k[g], v[g], datt, stats, dil, name=f"dil_att_bwd_{dil}") for g, dil in enumerate(DILATIONS)]

    dgy = _mm(dz, wts["w_glu"], tb=True, b_shards=True, name="glu_dx")
    dy, gs["ssm_d"] = _gelu_bwd(dgy, y, proj, name="gelu_bwd")
    du, s_u, dc_re_t, dc_im_t, dbb_re_t, dbb_im_t, da_re, da_im = _ssm_bwd(
        proj, dy, h_re, h_im, tiles_cn, tiles_nc, a_re_rows, a_im_rows, sm["ssm_d"], name="ssm_bwd")
    gs["ssm_c_re"], gs["ssm_c_im"] = dc_re_t, -dc_im_t
    disc_ct = (da_re.reshape(grp), da_im.reshape(grp), dbb_re_t.transpose(1, 0, 2), dbb_im_t.transpose(1, 0, 2))
    d_logdt, gs["ssm_a_re"], gs["ssm_a_im"], d_b_re_t, d_b_im_t = _whole(
        _disc_transpose, disc_in + disc_ct, [(SSM_GROUPS, 1), grp, grp, chn, chn], name="ssm_disc_bwd")
    gs["ssm_log_dt"] = d_logdt
    gs["ssm_b_re"], gs["ssm_b_im"] = d_b_re_t.transpose(1, 2, 0), d_b_im_t.transpose(1, 2, 0)

    dproj, s_qkv = _dproj_assemble(du, dqkv, dgs, dga, cos_t, sin_t, name="dproj_assemble")
    gs["b_in"] = jnp.concatenate([s_u, *s_qkv, s_gs, s_ga], axis=1)
    sent = send_small(gs, SMALL_EARLY)
    gw["w_in"] = wgrad(h0m, dproj, out_shards=True, after=sent, name="in_proj_dw")
    sent = send(3, gw)
    dh0 = _mm(dproj, wts["w_in"], tb=True, b_shards=True, after=sent, name="in_proj_dx")
    grad_x, gs["ln_in_g"], gs["ln_in_b"], _ = _ln_bwd(dr1, dh0, xh0, rs0, sm["ln_in_g"], alpha=DEEPNORM_ALPHA,
                                                      operand=False, name="ln_in_bwd")
    return loss_row, grad_x, gs


N_PEER = N_DEV - 1
_IN_HBM = pl.BlockSpec(memory_space=pltpu.HBM)
_IN_SEMAPHORE = pl.BlockSpec(memory_space=pltpu.SEMAPHORE)


def _device_index():
    return 4 * lax.axis_index("x") + 2 * lax.axis_index("y") + lax.axis_index("c")


ALL_PEERS = tuple(range(1, N_DEV))
NEAR_PEERS = (1, 2, 3, 4, 5)
FAR_PEERS = (6, 7)


def _peer_index(kk):
    x, y, c = lax.axis_index("x"), lax.axis_index("y"), lax.axis_index("c")
    return 4 * ((x + (kk >> 2)) % 2) + 2 * ((y + ((kk >> 1) & 1)) % 2) + (c + (kk & 1)) % 2


def _exchange_copies(src_refs, land_refs, send_sems, recv_sems, scatter, peers):
    x, y, c = lax.axis_index("x"), lax.axis_index("y"), lax.axis_index("c")
    me = 4 * x + 2 * y + c
    pairs = []
    for a, (src_ref, land_ref) in enumerate(zip(src_refs, land_refs)):
        for idx, kk in enumerate(peers):
            px = (x + (kk >> 2)) % 2
            py = (y + ((kk >> 1) & 1)) % 2
            pc = (c + (kk & 1)) % 2
            peer = 4 * px + 2 * py + pc
            sem = a * len(peers) + idx
            src = src_ref.at[peer] if scatter else src_ref

            def copy(dst, src=src, sem=sem, px=px, py=py, pc=pc):
                return pltpu.make_async_remote_copy(
                    src_ref=src, dst_ref=dst, send_sem=send_sems.at[sem], recv_sem=recv_sems.at[sem],
                    device_id=(px, py, pc), device_id_type=pl.DeviceIdType.MESH)

            pairs.append((functools.partial(copy, land_ref.at[me]), functools.partial(copy, land_ref.at[peer])))
    return pairs


def _own_copies(src_refs, land_refs, own_sems, scatter):
    me = _device_index()
    return [functools.partial(pltpu.make_async_copy, src_ref.at[me] if scatter else src_ref, land_ref.at[me],
                              own_sems.at[a]) for a, (src_ref, land_ref) in enumerate(zip(src_refs, land_refs))]


def _exchange_start(srcs, *, scatter, name, after=None, peers=ALL_PEERS, lands=None):
    n_arr, n_sem = len(srcs), len(srcs) * len(peers)
    own = lands is None
    if own:
        lands = [lax.empty((N_DEV,) + tuple(s.shape[1:] if scatter else s.shape), s.dtype) for s in srcs]
    n_in = 2 * n_arr + (after is not None)

    def body(*refs):
        send_sems, recv_sems = refs[n_in], refs[n_in + 1]
        for sent, _ in _exchange_copies(refs[:n_arr], refs[n_arr:2 * n_arr], send_sems, recv_sems, scatter, peers):
            sent().start()
        if own:
            for local in _own_copies(refs[:n_arr], refs[n_arr:2 * n_arr], refs[n_in + 2], scatter):
                local().start()
        refs[-1][...] = jnp.zeros_like(refs[-1])

    sems = [pltpu.SemaphoreType.DMA((n_sem,)), pltpu.SemaphoreType.DMA((n_sem,))] + [pltpu.SemaphoreType.DMA((n_arr,))] * own
    through = [pltpu.HBM(t.shape, t.dtype) for t in (*srcs, *lands)]
    res = pl.pallas_call(
        body, name=name, out_shape=(*sems, *through, jax.ShapeDtypeStruct((8, LANES), F32)),
        in_specs=[_IN_HBM] * (2 * n_arr) + [pl.BlockSpec(memory_space=pl.ANY)] * (after is not None),
        out_specs=(*[_IN_SEMAPHORE] * len(sems), *[_IN_HBM] * (2 * n_arr), pl.BlockSpec(memory_space=pltpu.VMEM)),
        input_output_aliases={i: len(sems) + i for i in range(2 * n_arr)},
        compiler_params=pltpu.CompilerParams(has_side_effects=pltpu.SideEffectType.DATAFLOW_SIDE_EFFECTING),
    )(*[pltpu.with_memory_space_constraint(t, pltpu.HBM) for t in (*srcs, *lands)],
      *([after] if after is not None else []))
    first = len(sems)
    handle = dict(sems=res[:first], srcs=res[first:first + n_arr], lands=res[first + n_arr:first + 2 * n_arr],
                  scatter=scatter, peers=peers, own=own)
    return handle, res[-1]


def _exchange_wait(handle, *, after, name, srcs=None, lands=None):
    srcs = handle["srcs"] if srcs is None else srcs
    lands = handle["lands"] if lands is None else lands
    sems, scatter, peers, own = handle["sems"], handle["scatter"], handle["peers"], handle["own"]
    n_arr = len(srcs)
    after = list(after)

    def body(*refs):
        src_refs, land_refs = refs[:n_arr], refs[n_arr:2 * n_arr]
        for sent, received in _exchange_copies(src_refs, land_refs, refs[2 * n_arr], refs[2 * n_arr + 1], scatter, peers):
            sent().wait_send()
            received().wait_recv()
        if own:
            for local in _own_copies(src_refs, land_refs, refs[2 * n_arr + 2], scatter):
                local().wait()

    res = pl.pallas_call(
        body, name=name, out_shape=tuple(pltpu.HBM(t.shape, t.dtype) for t in (*srcs, *lands)),
        in_specs=[_IN_HBM] * (2 * n_arr) + [_IN_SEMAPHORE] * len(sems) + [pl.BlockSpec(memory_space=pl.ANY)] * len(after),
        out_specs=tuple([_IN_HBM] * (2 * n_arr)), input_output_aliases={i: i for i in range(2 * n_arr)},
        compiler_params=pltpu.CompilerParams(has_side_effects=pltpu.SideEffectType.DATAFLOW_SIDE_EFFECTING),
    )(*srcs, *lands, *sems, *after)
    return res[:n_arr], res[n_arr:]


def _adamw(g, w, m, v):
    m_new = ADAM_B1 * m + (1.0 - ADAM_B1) * g
    v_new = ADAM_B2 * v + (1.0 - ADAM_B2) * jnp.square(g)
    m_hat = m_new / (1.0 - ADAM_B1 ** ADAM_STEP)
    v_hat = v_new / (1.0 - ADAM_B2 ** ADAM_STEP)
    return g, -ADAM_LR * (m_hat / (jnp.sqrt(v_hat) + ADAM_EPS) + ADAM_WD * w), m_new, v_new


def _reduce_adamw(gstack, w, m, v, *, name, tr=128):
    n_rows, cols = w.shape
    tr = min(tr, n_rows)
    assert n_rows % tr == 0, (name, n_rows, tr)

    def body(g_ref, w_ref, m_ref, v_ref, *out_refs):
        g = g_ref[0].astype(F32)
        for dev in range(1, N_DEV):
            g = g + g_ref[dev].astype(F32)
        for o_ref, val in zip(out_refs, _adamw(g, w_ref[...], m_ref[...], v_ref[...])):
            o_ref[...] = val

    flat = pl.BlockSpec((tr, cols), lambda i: (i, 0))
    shape = jax.ShapeDtypeStruct((n_rows, cols), F32)
    return pl.pallas_call(
        body, name=name, grid=(n_rows // tr,),
        in_specs=[pl.BlockSpec((N_DEV, tr, cols), lambda i: (0, i, 0)), flat, flat, flat],
        out_specs=[flat] * 4, out_shape=[shape] * 4, compiler_params=_params("parallel"),
    )(gstack, w, m, v)


SMALL_FLAT_SSM = ("ssm_b_re", "ssm_b_im", "ssm_c_re", "ssm_c_im")


def _small_view(name, shape):
    size = int(np.prod(shape))
    if name in SMALL_FLAT_SSM:
        return SSM_GROUPS, size // SSM_GROUPS
    if name in ("ssm_a_re", "ssm_a_im"):
        return SSM_GROUPS, SSM_STATE
    return 1, size


def _pack_rows(view):
    return -(-(view[0] * view[1]) // PACK_COLS)


SMALL_LATE = ("ln_in_g", "ln_in_b")
SMALL_EARLY = tuple(n for n in SMALL if n not in SMALL_LATE)


def _pack_small(gs, names, views):
    parts = []
    for n in names:
        flat = gs[n].reshape(-1).astype(WIRE_DTYPE)
        parts.append(jnp.pad(flat, (0, _pack_rows(views[n]) * PACK_COLS - flat.shape[0])))
    total = sum(p.shape[0] for p in parts) // PACK_COLS
    parts.append(jnp.zeros(((-total % PACK_ROW_ALIGN) * PACK_COLS,), WIRE_DTYPE))
    return jnp.concatenate(parts).reshape(-1, PACK_COLS)


def _small_pieces(view):
    rows, cols = view
    if cols == PACK_COLS:
        return [(0, rows, 0, 0, 0, cols)]
    if rows == 1 and cols > PACK_COLS:
        return [(kk, 1, 0, 0, kk * PACK_COLS, PACK_COLS) for kk in range(cols // PACK_COLS)]
    if rows == 1:
        return [(0, 1, 0, 0, 0, cols)]
    return [((r * cols) // PACK_COLS, 1, (r * cols) % PACK_COLS, r, 0, cols) for r in range(rows)]


def _adamw_small(stacks, views, w, m, v, *, name):
    n = len(SMALL)
    place, first = {}, [0, 0]
    for k, names in enumerate((SMALL_EARLY, SMALL_LATE)):
        for name_ in names:
            place[name_] = (k, first[k])
            first[k] += _pack_rows(views[name_])

    def body(early_ref, late_ref, *refs):
        ins, outs = refs[:3 * n], refs[3 * n:]
        for i, name_ in enumerate(SMALL):
            stack_ref = (early_ref, late_ref)[place[name_][0]]
            row0 = place[name_][1]
            for prow, nrows, lane, orow, ocol, width in _small_pieces(views[name_]):
                src = (slice(row0 + prow, row0 + prow + nrows), slice(lane, lane + width))
                dst = (slice(orow, orow + nrows), slice(ocol, ocol + width))
                g = stack_ref[(0,) + src].astype(F32)
                for dev in range(1, N_DEV):
                    g = g + stack_ref[(dev,) + src].astype(F32)
                res = _adamw(g, ins[i][dst], ins[n + i][dst], ins[2 * n + i][dst])
                for kk, val in enumerate(res):
                    outs[kk * n + i][dst] = val

    args = [*stacks, *[d[name_] for d in (w, m, v) for name_ in SMALL]]
    out_views = [views[name_] for _ in range(4) for name_ in SMALL]
    res = pl.pallas_call(
        body, name=name, grid=(1,), in_specs=[_full_spec(t.shape) for t in args],
        out_specs=[_full_spec(s) for s in out_views], out_shape=[jax.ShapeDtypeStruct(s, F32) for s in out_views],
        compiler_params=_params("arbitrary"),
    )(*args)
    return [dict(zip(SMALL, res[kk * n:(kk + 1) * n])) for kk in range(4)]


def kernel(x, mem, positions, ln_in_g, ln_in_b, w_in, b_in, ssm_log_dt, ssm_a_re, ssm_a_im, ssm_b_re, ssm_b_im, ssm_c_re, ssm_c_im, ssm_d, w_glu, b_glu, w_att_up, w_mix_out, b_mix_out, ln1_g, ln1_b, w_xq, w_xkv, w_xo, ln2_g, ln2_b, w_ff1, b_ff1, w_ff2, b_ff2, ln3_g, ln3_b, loss_target, m_ln_in_g, m_ln_in_b, m_w_in, m_b_in, m_ssm_log_dt, m_ssm_a_re, m_ssm_a_im, m_ssm_b_re, m_ssm_b_im, m_ssm_c_re, m_ssm_c_im, m_ssm_d, m_w_glu, m_b_glu, m_w_att_up, m_w_mix_out, m_b_mix_out, m_ln1_g, m_ln1_b, m_w_xq, m_w_xkv, m_w_xo, m_ln2_g, m_ln2_b, m_w_ff1, m_b_ff1, m_w_ff2, m_b_ff2, m_ln3_g, m_ln3_b, v_ln_in_g, v_ln_in_b, v_w_in, v_b_in, v_ssm_log_dt, v_ssm_a_re, v_ssm_a_im, v_ssm_b_re, v_ssm_b_im, v_ssm_c_re, v_ssm_c_im, v_ssm_d, v_w_glu, v_b_glu, v_w_att_up, v_w_mix_out, v_b_mix_out, v_ln1_g, v_ln1_b, v_w_xq, v_w_xkv, v_w_xo, v_ln2_g, v_ln2_b, v_w_ff1, v_b_ff1, v_w_ff2, v_b_ff2, v_ln3_g, v_ln3_b):
    given = dict(locals())
    w_arg = {n: given[n] for n in WEIGHTS}
    m_arg = {n: given["m_" + n] for n in WEIGHTS}
    v_arg = {n: given["v_" + n] for n in WEIGHTS}

    in_near, token = _exchange_start([w_arg["w_in"][0].astype(MXU_DTYPE)], scatter=False, peers=NEAR_PEERS,
                                     name="gather_start_in_near")
    in_far, token = _exchange_start(in_near["srcs"], scatter=False, peers=FAR_PEERS, lands=in_near["lands"],
                                    after=token, name="gather_start_in_far")
    w_in_state = [in_far["srcs"], in_far["lands"]]
    token, w_arg, m_arg, v_arg = lax.optimization_barrier((token, w_arg, m_arg, v_arg))
    shards = {n: w_arg[n][0].astype(MXU_DTYPE) for n in BIG if n != "w_in"}
    gathers = []
    for i, names in enumerate(GATHER_GROUPS):
        handle, token = _exchange_start([shards[n] for n in names], scatter=False, after=token, name=f"gather_start_{i}")
        gathers.append(handle)

    small_views = {n: _small_view(n, w_arg[n].shape) for n in SMALL}
    small_w, small_m, small_v = [{n: d[n].reshape(small_views[n]) for n in SMALL} for d in (w_arg, m_arg, v_arg)]
    relaid = [d[n] for d in (small_w, small_m, small_v) for n in SMALL_FLAT_SSM]

    def fetch_in(part, after):
        handle, peers, tag = ((in_near, (0,) + NEAR_PEERS, "near"), (in_far, FAR_PEERS, "far"))[part]
        w_in_state[:] = _exchange_wait(handle, after=after + (relaid if part == 0 else []), srcs=w_in_state[0],
                                       lands=w_in_state[1], name="gather_wait_in_" + tag)
        return w_in_state[1][0], jnp.stack([_peer_index(kk) for kk in peers]).astype(jnp.int32)

    def fetch(i, after):
        _, lands = _exchange_wait(gathers[i], after=after, name=f"gather_wait_{i}")
        full = dict(zip(GATHER_GROUPS[i], lands))
        return {n: t if n in BIG_COL_SHARDED else t.reshape(-1, t.shape[-1]) for n, t in full.items()}

    scatters = {}

    def send(i, gw):
        slots = [gw[n] if n in BIG_COL_SHARDED else gw[n].reshape(N_DEV, -1, gw[n].shape[-1]) for n in SCATTER_GROUPS[i]]
        handle, sent = _exchange_start(slots, scatter=True, name=f"scatter_start_{i}")
        scatters[i] = (handle, slots)
        return sent

    sm = {}
    for n in SMALL:
        t = w_arg[n]
        if n.startswith("ssm_") and n not in ("ssm_d", "ssm_log_dt"):
            sm[n] = t[0]
        else:
            sm[n] = t.reshape(1, -1)

    smalls = []

    def send_small(gs, names):
        handle, sent = _exchange_start([_pack_small(gs, names, small_views)], scatter=False,
                                       name=f"small_start_{len(smalls)}")
        smalls.append(handle)
        return sent

    loss_row, grad_x, gs = _local_grads(x[0], mem[0], positions.reshape(-1, 1), loss_target[0], sm, fetch_in, fetch,
                                        send, send_small, token)
    loss = lax.psum(loss_row[0, 0], ("x", "y", "c"))
    send_small(gs, SMALL_LATE)

    results = [{}, {}, {}, {}]
    done = grad_x
    for i, names in enumerate(SCATTER_GROUPS):
        handle, slots = scatters[i]
        _, lands = _exchange_wait(handle, after=[done], name=f"scatter_wait_{i}")
        for n, land, slot in zip(names, lands, slots):
            res = _reduce_adamw(land, w_arg[n][0], m_arg[n][0], v_arg[n][0], name="adamw_" + n)
            done = res[0]
            for d, r in zip(results, res):
                d[n] = r[None]
    stacks = [_exchange_wait(handle, after=[done], name=f"small_wait_{i}")[1][0] for i, handle in enumerate(smalls)]
    res = _adamw_small(stacks, small_views, small_w, small_m, small_v, name="adamw_small")
    for d, r in zip(results, res):
        d.update({n: r[n].reshape(w_arg[n].shape) for n in SMALL})
    out = [loss, grad_x[None]]
    for d in results:
        out += [d[n] for n in WEIGHTS]
    return tuple(out)
```

```python
import functools

import numpy as np
import jax
import jax.numpy as jnp
from jax import lax
from jax.experimental import pallas as pl
from jax.experimental.pallas import tpu as pltpu

F32 = jnp.float32
MXU_DTYPE = jnp.bfloat16
WIRE_DTYPE = jnp.bfloat16
VMEM_LIMIT_BYTES = 48 * 1024 * 1024
LANES = 128

N_DEV = 8
D_MODEL = 1024
SSM_GROUP = 16
SSM_WIDTH = 768
SSM_GROUPS = SSM_WIDTH // SSM_GROUP
SSM_STATE = 64
SSM_CH = SSM_GROUPS * SSM_STATE
SSM_TILES = SSM_WIDTH // LANES
GROUPS_PER_TILE = LANES // SSM_GROUP
STATE_VREG_ROWS = SSM_CH // LANES
ATT_HEAD_DIM = 64
ATT_HEADS_PER_GROUP = 4
ATT_MERGED = ATT_HEADS_PER_GROUP * ATT_HEAD_DIM
LANE_HALVES = ATT_MERGED // LANES
DILATIONS = (1, 4, 16)
ATT_BLK = 128
ATT_SCALE = ATT_HEAD_DIM ** -0.5
ROT_DIM = ATT_HEAD_DIM // 4
ROPE_THETA = 500000.0
XATT_HEADS = 4
XATT_HEAD_DIM = D_MODEL // XATT_HEADS
XATT_SCALE = XATT_HEAD_DIM ** -0.5
DEEPNORM_ALPHA = 2.0 ** 0.25
LN_EPS = 1e-5
NEG_INF = -1e30
OFF_Q_BLK, OFF_K_BLK, OFF_V_BLK = 3, 6, 9
OFF_GS_BLK, OFF_GA_BLK = 3, 4

ADAM_LR = 0.001
ADAM_B1 = 0.9
ADAM_B2 = 0.999
ADAM_EPS = 1e-08
ADAM_WD = 0.01
ADAM_STEP = 10

BIG = ("w_in", "w_glu", "w_att_up", "w_mix_out", "w_xq", "w_xkv", "w_xo", "w_ff1", "w_ff2")
BIG_COL_SHARDED = ("w_in", "w_glu", "w_att_up", "w_xkv", "w_ff1")
WEIGHTS = ("ln_in_g", "ln_in_b", "w_in", "b_in", "ssm_log_dt", "ssm_a_re", "ssm_a_im", "ssm_b_re", "ssm_b_im",
           "ssm_c_re", "ssm_c_im", "ssm_d", "w_glu", "b_glu", "w_att_up", "w_mix_out", "b_mix_out", "ln1_g", "ln1_b",
           "w_xq", "w_xkv", "w_xo", "ln2_g", "ln2_b", "w_ff1", "b_ff1", "w_ff2", "b_ff2", "ln3_g", "ln3_b")
SMALL = tuple(n for n in WEIGHTS if n not in BIG)
PACK_COLS = 1024
PACK_ROW_ALIGN = 16


def _params(*sem):
    return pltpu.CompilerParams(dimension_semantics=sem, vmem_limit_bytes=VMEM_LIMIT_BYTES)


def _dot(a, b, ca, cb):
    return lax.dot_general(a.astype(MXU_DTYPE), b.astype(MXU_DTYPE), (((ca,), (cb,)), ((), ())),
                           preferred_element_type=F32)


def _fit(dim, pref):
    if dim <= pref:
        return dim
    best = max(t for t in range(LANES, pref + 1, LANES) if dim % t == 0)
    return best


def _mm(a, b, *, name, ta=False, tb=False, bias=None, out_dtype=F32, b_shards=False, out_shards=False, after=None,
        also=None, gate=None, colsum=False, epilogue=None, tm=2048, tn=1024, tk=1024):
    m, k = (a.shape[1], a.shape[0]) if ta else a.shape
    order = (lambda f: (lambda j, i, kk: f(i, j, kk))) if colsum else (lambda f: f)
    spec = lambda shape, f: pl.BlockSpec(shape, order(f))
    if b_shards:
        n_sh, rows, n_loc = b.shape
        if tb:
            n, tn, tk = rows, _fit(rows, tn), n_loc
            assert k == n_sh * n_loc, (name, k, b.shape)
            b_spec = spec((1, tn, tk), lambda i, j, kk: (kk, j, 0))
        else:
            n, tn, tk = n_sh * n_loc, n_loc, _fit(k, tk)
            b_spec = spec((1, tk, tn), lambda i, j, kk: (j, kk, 0))
    else:
        n = b.shape[0] if tb else b.shape[1]
        tn = n // N_DEV if out_shards else _fit(n, tn)
        tk = _fit(k, tk)
        b_spec = spec((tn, tk), lambda i, j, kk: (j, kk)) if tb else spec((tk, tn), lambda i, j, kk: (kk, j))
    tm = _fit(m, tm)
    nk = k // tk
    a_spec = spec((tk, tm), lambda i, j, kk: (kk, i)) if ta else spec((tm, tk), lambda i, j, kk: (i, kk))
    tile = spec((tm, tn), lambda i, j, kk: (i, j))
    in_specs, args = [a_spec, b_spec], [a, b]
    if bias is not None:
        in_specs.append(spec((1, tn), lambda i, j, kk: (0, j)))
        args.append(bias)
    if gate is not None:
        in_specs.append(tile)
        args.append(gate[0])
    if after is not None:
        in_specs.append(pl.BlockSpec(memory_space=pl.ANY))
        args.append(after)
    if epilogue is not None:
        ep_fn, ep_rows, ep_fulls, ep_row_outs, ep_acc_outs = epilogue
        assert tn == n and not (colsum or also or gate or out_shards), name
        ep_first = len(args)
        in_specs += [spec((tm, t.shape[1]), lambda i, j, kk: (i, 0)) for t in ep_rows]
        in_specs += [pl.BlockSpec(t.shape, functools.partial(lambda i, j, kk, nd: (0,) * nd, nd=t.ndim)) for t in ep_fulls]
        args += [*ep_rows, *ep_fulls]
    n_in = len(args)
    if epilogue is not None:
        out_specs = [spec((tm, w), lambda i, j, kk: (i, 0)) for w, _ in ep_row_outs]
        out_specs += [spec((1, w), lambda i, j, kk: (0, 0)) for w in ep_acc_outs]
        out_shape = [jax.ShapeDtypeStruct((m, w), dt) for w, dt in ep_row_outs]
        out_shape += [jax.ShapeDtypeStruct((1, w), F32) for w in ep_acc_outs]
    elif out_shards:
        assert n == N_DEV * tn, (name, n, tn)
        out_specs = [spec((1, tm, tn), lambda i, j, kk: (j, i, 0))]
        out_shape = [jax.ShapeDtypeStruct((N_DEV, m, tn), out_dtype)]
    else:
        out_specs = [tile]
        out_shape = [jax.ShapeDtypeStruct((m, n), out_dtype)]
    if also is not None:
        out_specs.append(tile)
        out_shape.append(jax.ShapeDtypeStruct((m, n), also[1]))
    if colsum:
        out_specs.append(spec((1, tn), lambda i, j, kk: (0, j)))
        out_shape.append(jax.ShapeDtypeStruct((1, n), F32))

    def body(*refs):
        a_ref, b_ref = refs[0], refs[1]
        o_ref = refs[n_in]
        first_row_tile = pl.program_id(1 if colsum else 0) == 0

        def product():
            return _dot(a_ref[...], b_ref[0] if b_shards else b_ref[...], 0 if ta else 1, 1 if tb else 0)

        def finish(r):
            if bias is not None:
                r = r + refs[2][...]
            if gate is not None:
                r = r * gate[1](refs[2 + (bias is not None)][...])
            if epilogue is not None:
                res = ep_fn(r, *[ref[...] for ref in refs[ep_first:n_in]])
                n_o = len(ep_row_outs)
                for ref, val in zip(refs[n_in:n_in + n_o], res[:n_o]):
                    ref[...] = val.astype(ref.dtype)
                acc_refs = refs[n_in + n_o:n_in + n_o + len(ep_acc_outs)]
                if acc_refs:
                    @pl.when(first_row_tile)
                    def _():
                        for ref in acc_refs:
                            ref[...] = jnp.zeros_like(ref)

                    for ref, val in zip(acc_refs, res[n_o:]):
                        ref[...] += val
                return
            if out_shards:
                o_ref[0] = r.astype(o_ref.dtype)
            else:
                o_ref[...] = r.astype(o_ref.dtype)
            if also is not None:
                refs[n_in + 1][...] = also[0](r).astype(also[1])
            if colsum:
                s_ref = refs[n_in + 1 + (also is not None)]

                @pl.when(first_row_tile)
                def _():
                    s_ref[...] = jnp.zeros_like(s_ref)

                s_ref[...] += _colsum(r)

        if nk == 1:
            finish(product())
            return
        acc_ref = refs[-1]
        kk = pl.program_id(2)

        @pl.when(kk == 0)
        def _():
            acc_ref[...] = jnp.zeros_like(acc_ref)

        acc_ref[...] += product()

        @pl.when(kk == nk - 1)
        def _():
            finish(acc_ref[...])

    grid = (n // tn, m // tm, nk) if colsum else (m // tm, n // tn, nk)
    res = pl.pallas_call(
        body, name=name, grid=grid, in_specs=in_specs, out_specs=out_specs, out_shape=out_shape,
        scratch_shapes=[pltpu.VMEM((tm, tn), F32)] if nk > 1 else [],
        compiler_params=_params("arbitrary" if epilogue is not None else "parallel",
                                "arbitrary" if colsum else "parallel", "arbitrary"),
    )(*args)
    return res[0] if len(res) == 1 else res


def _mm_shards(a, w, bias, shard_ids, *, name, prev=None, tm=2048):
    m, k = a.shape
    n_sh, _, n_loc = w.shape
    tm = _fit(m, tm)

    def body(ids_ref, a_ref, w_ref, b_ref, *rest):
        rest[-1][...] = _dot(a_ref[...], w_ref[0], 1, 0) + b_ref[...]

    grid_spec = pltpu.PrefetchScalarGridSpec(
        num_scalar_prefetch=1, grid=(m // tm, shard_ids.shape[0]),
        in_specs=[pl.BlockSpec((tm, k), lambda i, j, ids: (i, 0)),
                  pl.BlockSpec((1, k, n_loc), lambda i, j, ids: (ids[j], 0, 0)),
                  pl.BlockSpec((1, n_loc), lambda i, j, ids: (0, ids[j]))]
        + [pl.BlockSpec(memory_space=pl.ANY)] * (prev is not None),
        out_specs=pl.BlockSpec((tm, n_loc), lambda i, j, ids: (i, ids[j])))
    return pl.pallas_call(
        body, name=name, grid_spec=grid_spec, out_shape=jax.ShapeDtypeStruct((m, n_sh * n_loc), F32),
        input_output_aliases={4: 0} if prev is not None else {}, compiler_params=_params("parallel", "arbitrary"),
    )(shard_ids, a, w, bias, *([prev] if prev is not None else []))


def _rowcall(fn, rows, fulls, row_outs, acc_outs=(), *, n_rows, tm, name, after=None):
    n_r, n_f, n_o, n_a = len(rows), len(fulls), len(row_outs), len(acc_outs)
    n_in = n_r + n_f + (after is not None)
    assert n_rows % tm == 0, (name, n_rows, tm)

    def body(*refs):
        res = fn(*[r[...] for r in refs[:n_r + n_f]])
        res = tuple(res) if isinstance(res, (tuple, list)) else (res,)
        o_refs = refs[n_in:n_in + n_o]
        a_refs = refs[n_in + n_o:]
        for o_ref, val in zip(o_refs, res[:n_o]):
            o_ref[...] = val.astype(o_ref.dtype)
        if n_a:
            @pl.when(pl.program_id(0) == 0)
            def _():
                for a_ref in a_refs:
                    a_ref[...] = jnp.zeros_like(a_ref)

            for a_ref, val in zip(a_refs, res[n_o:]):
                a_ref[...] += val

    in_specs = [pl.BlockSpec((tm, w), functools.partial(lambda i, cb: (i, cb), cb=cb)) for _, w, cb in rows]
    in_specs += [pl.BlockSpec(f.shape, functools.partial(lambda i, nd: (0,) * nd, nd=f.ndim)) for f in fulls]
    in_specs += [pl.BlockSpec(memory_space=pl.ANY)] * (after is not None)
    out_specs = [pl.BlockSpec((tm, w), lambda i: (i, 0)) for w, _ in row_outs]
    out_specs += [pl.BlockSpec((1, w), lambda i: (0, 0)) for w in acc_outs]
    out_shape = [jax.ShapeDtypeStruct((n_rows, w), dt) for w, dt in row_outs]
    out_shape += [jax.ShapeDtypeStruct((1, w), F32) for w in acc_outs]
    return pl.pallas_call(
        body, name=name, grid=(n_rows // tm,), in_specs=in_specs, out_specs=out_specs, out_shape=out_shape,
        compiler_params=_params("arbitrary" if n_a else "parallel"),
    )(*[r[0] for r in rows], *fulls, *([after] if after is not None else []))


def _colsum(v):
    return jnp.sum(v, axis=0, keepdims=True)


def _layer_norm(xin, g, b):
    mu = jnp.mean(xin, axis=-1, keepdims=True)
    xc = xin - mu
    var = jnp.mean(xc * xc, axis=-1, keepdims=True)
    rstd = lax.rsqrt(var + LN_EPS)
    xh = xc * rstd
    return xh * g + b, xh, rstd


def _layer_norm_bwd(dy, xh, rstd, g):
    dyg = dy * g
    m1 = jnp.mean(dyg, axis=-1, keepdims=True)
    m2 = jnp.mean(dyg * xh, axis=-1, keepdims=True)
    dx = rstd * (dyg - m1 - xh * m2)
    return dx, _colsum(dy * xh), _colsum(dy), _colsum(dx)


def _ln_fwd(a, g, b, *, name):
    n_rows, d = a.shape

    def fn(av, gv, bv):
        y, xh, rstd = _layer_norm(av, gv, bv)
        return y, xh, rstd, y

    return _rowcall(fn, [(a, d, 0)], [g, b], [(d, F32), (d, F32), (1, F32), (d, MXU_DTYPE)], n_rows=n_rows, tm=256,
                    name=name)


def _ln_bwd(dya, dyb, xh, rstd, g, *, alpha, name, operand=True):
    n_rows, d = xh.shape

    def fn(da, db, xhv, rs, gv):
        dx, *sums = _layer_norm_bwd(alpha * da + db, xhv, rs, gv)
        return (dx,) + ((dx,) if operand else ()) + tuple(sums)

    rows = [(dya, d, 0), (dyb, d, 0), (xh, d, 0), (rstd, 1, 0)]
    return _rowcall(fn, rows, [g], [(d, F32)] + [(d, MXU_DTYPE)] * operand, [d, d, d], n_rows=n_rows, tm=256, name=name)


LN_EPILOGUE_ROWS = 1024


def _mm_ln_fwd(x, w, bias, a, g, b, *, alpha, name):
    d = a.shape[1]

    def fn(r, av, gv, bv):
        y, xh, rstd = _layer_norm(alpha * av + r, gv, bv)
        return y, xh, rstd, y

    return _mm(x, w, bias=bias, name=name, tm=LN_EPILOGUE_ROWS,
               epilogue=(fn, [a], [g, b], [(d, F32), (d, F32), (1, F32), (d, MXU_DTYPE)], []))


def _mm_ln_bwd(x, w, dya, xh, rstd, g, *, alpha, name):
    d = xh.shape[1]

    def fn(r, da, xhv, rs, gv):
        dx, *sums = _layer_norm_bwd(alpha * da + r, xhv, rs, gv)
        return (dx, dx, *sums)

    return _mm(x, w, tb=True, name=name, tm=LN_EPILOGUE_ROWS,
               epilogue=(fn, [dya, xh, rstd], [g], [(d, F32), (d, MXU_DTYPE)], [d, d, d]))


def _ln_loss_bwd(a, r, target, g, b, *, alpha, name):
    n_rows, d = a.shape

    def fn(av, rv, tv, gv, bv):
        y, xh, rs = _layer_norm(alpha * av + rv, gv, bv)
        diff = y - tv
        part = jnp.sum(jnp.sum(diff * diff, axis=1, keepdims=True), axis=0, keepdims=True) * (0.5 / d)
        dx, *sums = _layer_norm_bwd(diff * (1.0 / d), xh, rs, gv)
        return (dx, dx, *sums, jnp.broadcast_to(part, (1, LANES)))

    return _rowcall(fn, [(a, d, 0), (r, d, 0), (target, d, 0)], [g, b], [(d, F32), (d, MXU_DTYPE)], [d, d, d, LANES],
                    n_rows=n_rows, tm=256, name=name)


def _rope_lane_constants():
    lane = np.arange(ATT_MERGED)
    in_head = lane % ATT_HEAD_DIM
    sign = np.where(in_head < ROT_DIM // 2, -1.0, np.where(in_head < ROT_DIM, 1.0, 0.0)).astype(np.float32)
    inv_freq = ROPE_THETA ** (-jnp.arange(0, ROT_DIM, 2, dtype=F32) / ROT_DIM)
    return inv_freq[lane % (ROT_DIM // 2)].reshape(1, ATT_MERGED), jnp.asarray(sign).reshape(1, ATT_MERGED)


def _rope_tables(pos_col, *, name, after=None):
    inv_lane, sign = _rope_lane_constants()

    def fn(pos, inv, sg):
        ang = pos.astype(F32) * inv
        return jnp.where(sg != 0.0, jnp.cos(ang), 1.0), sg * jnp.sin(ang)

    return _rowcall(fn, [(pos_col, 1, 0)], [inv_lane, sign], [(ATT_MERGED, F32), (ATT_MERGED, F32)],
                    n_rows=pos_col.shape[0], tm=512, name=name, after=after)


def _rot_partner(t):
    lane = lax.broadcasted_iota(jnp.int32, t.shape, 1)
    width = t.shape[1]
    return jnp.where((lane & (ROT_DIM // 2)) == 0, pltpu.roll(t, width - ROT_DIM // 2, 1), pltpu.roll(t, ROT_DIM // 2, 1))


def _rope(t, cos_t, sin_t):
    return t * cos_t + _rot_partner(t) * sin_t


def _rope_transpose(dt, cos_t, sin_t):
    return dt * cos_t + _rot_partner(dt * sin_t)


def _strided_rows(r, count, stride):
    return pl.ds(r, count) if stride == 1 else pl.ds(r, count, stride=stride)


def _qkv_split(proj, cos_t, sin_t, *, name, tm=512):
    n_rows = proj.shape[0]
    n_g = len(DILATIONS)

    def body(*refs):
        n_src = LANE_HALVES * 3 * n_g
        src, tables, dst = refs[:n_src], refs[n_src:n_src + 2 * LANE_HALVES], refs[n_src + 2 * LANE_HALVES:]
        for kind in range(3):
            for g, dil in enumerate(DILATIONS):
                for half in range(LANE_HALVES):
                    x_ref, o_ref = src[(kind * n_g + g) * LANE_HALVES + half], dst[kind * n_g + g]
                    cos_ref, sin_ref = tables[half], tables[LANE_HALVES + half]
                    for r in range(dil):
                        rows = _strided_rows(r, tm // dil, dil)
                        t = x_ref[rows, :]
                        if kind < 2:
                            t = _rope(t, cos_ref[rows, :], sin_ref[rows, :])
                        lo = r * ATT_MERGED + half * LANES
                        o_ref[:, lo:lo + LANES] = t.astype(o_ref.dtype)

    half_spec = lambda cb: pl.BlockSpec((tm, LANES), functools.partial(lambda i, cb: (i, cb), cb=cb))
    in_specs = [half_spec((off + g) * LANE_HALVES + half)
                for off in (OFF_Q_BLK, OFF_K_BLK, OFF_V_BLK) for g in range(n_g) for half in range(LANE_HALVES)]
    in_specs += [half_spec(half) for _ in range(2) for half in range(LANE_HALVES)]
    out_specs = [pl.BlockSpec((tm // dil, dil * ATT_MERGED), lambda i: (i, 0)) for _ in range(3) for dil in DILATIONS]
    out_shape = [jax.ShapeDtypeStruct((n_rows // dil, dil * ATT_MERGED), MXU_DTYPE) for _ in range(3) for dil in DILATIONS]
    outs = pl.pallas_call(
        body, name=name, grid=(n_rows // tm,), in_specs=in_specs, out_specs=out_specs, out_shape=out_shape,
        compiler_params=_params("parallel"),
    )(*[proj] * (LANE_HALVES * 3 * n_g), *[cos_t] * LANE_HALVES, *[sin_t] * LANE_HALVES)
    return outs[:n_g], outs[n_g:2 * n_g], outs[2 * n_g:]


def _mix(gs, ga, z1, z2, b_att):
    return jax.nn.sigmoid(gs) * (z1 * jax.nn.sigmoid(z2)) + jax.nn.sigmoid(ga) * b_att


def _mix_rows(proj, z, b_att):
    return [(proj, D_MODEL, OFF_GS_BLK), (proj, D_MODEL, OFF_GA_BLK), (z, D_MODEL, 0), (z, D_MODEL, 1), (b_att, D_MODEL, 0)]


def _mix_out_ln(proj, z, b_att, w, bias, a, g, b, *, alpha, name):
    def fn(gs, ga, z1, z2, ba, av, wv, biasv, gv, bv):
        mixed = _mix(gs, ga, z1, z2, ba)
        y, xh, rstd = _layer_norm(alpha * av + (_dot(mixed, wv, 1, 0) + biasv), gv, bv)
        return mixed, y, xh, rstd, y

    rows = _mix_rows(proj, z, b_att) + [(a, D_MODEL, 0)]
    outs = [(D_MODEL, MXU_DTYPE), (D_MODEL, F32), (D_MODEL, F32), (1, F32), (D_MODEL, MXU_DTYPE)]
    return _rowcall(fn, rows, [w, bias, g, b], outs, n_rows=proj.shape[0], tm=256, name=name)


def _mix_bwd(dmixed, proj, z, b_att, *, name):
    def fn(dm, gs, ga, z1, z2, ba):
        _, vjp = jax.vjp(_mix, gs, ga, z1, z2, ba)
        dgs, dga, dz1, dz2, dba = vjp(dm)
        dz = jnp.concatenate([dz1, dz2], axis=1)
        return dgs, dga, dz, dba, _colsum(dgs), _colsum(dga), _colsum(dz)

    rows = [(dmixed, D_MODEL, 0)] + _mix_rows(proj, z, b_att)
    widths = [D_MODEL, D_MODEL, 2 * D_MODEL, D_MODEL]
    return _rowcall(fn, rows, [], [(w, MXU_DTYPE) for w in widths], widths[:3], n_rows=proj.shape[0], tm=256, name=name)


def _gelu_bwd(dgy, y, proj, *, name):
    def fn(dg, yv, u):
        _, vjp = jax.vjp(jax.nn.gelu, yv)
        dy = vjp(dg)[0]
        return dy, _colsum(dy * u)

    return _rowcall(fn, [(dgy, SSM_WIDTH, 0), (y, SSM_WIDTH, 0), (proj, SSM_WIDTH, 0)], [], [(SSM_WIDTH, F32)],
                    [SSM_WIDTH], n_rows=y.shape[0], tm=512, name=name)


HEAD_ROWS = ATT_HEADS_PER_GROUP * ATT_BLK


def _head_masks(rows):
    head = lax.broadcasted_iota(jnp.int32, (rows, ATT_MERGED), 1) >> (ATT_HEAD_DIM.bit_length() - 1)
    return [head == h for h in range(ATT_HEADS_PER_GROUP)]


def _stack_heads(t, masks):
    return jnp.concatenate([jnp.where(m, t, jnp.zeros_like(t)) for m in masks], axis=0)


def _unstack_heads(t4, masks):
    blocks = [t4[h * ATT_BLK:(h + 1) * ATT_BLK] for h in range(ATT_HEADS_PER_GROUP)]
    return jnp.where(masks[0], blocks[0], jnp.where(masks[1], blocks[1], jnp.where(masks[2], blocks[2], blocks[3])))


def _head_column(stats, first):
    return jnp.concatenate([stats[:, first + h:first + h + 1] for h in range(ATT_HEADS_PER_GROUP)], axis=0)


def _band_mask(first_key):
    qi = lax.broadcasted_iota(jnp.int32, (HEAD_ROWS, 2 * ATT_BLK), 0) & (ATT_BLK - 1)
    ki = lax.broadcasted_iota(jnp.int32, (HEAD_ROWS, 2 * ATT_BLK), 1)
    steps = qi + ATT_BLK - ki
    return (steps >= 0) & (steps <= ATT_BLK) & (ki >= first_key)


def _dil_fwd(q, k, v, dil, *, name):
    n_blk = q.shape[0] // ATT_BLK
    cur = pl.BlockSpec((ATT_BLK, ATT_MERGED), lambda r, n: (n, r))
    prev = pl.BlockSpec((ATT_BLK, ATT_MERGED), lambda r, n: (jnp.maximum(n - 1, 0), r))

    def body(q_ref, kp_ref, kc_ref, vp_ref, vc_ref, o_ref, l_ref):
        masks = _head_masks(ATT_BLK)
        valid = _band_mask(jnp.where(pl.program_id(1) > 0, 0, ATT_BLK))
        keys = jnp.concatenate([kp_ref[...], kc_ref[...]], axis=0)
        vals = jnp.concatenate([vp_ref[...], vc_ref[...]], axis=0)
        s = jnp.where(valid, _dot(_stack_heads(q_ref[...], masks), keys, 1, 1) * ATT_SCALE, NEG_INF)
        m = jnp.max(s, axis=-1, keepdims=True)
        p = jnp.exp(s - m)
        den = jnp.sum(p, axis=-1, keepdims=True)
        o_ref[...] = _unstack_heads(_dot(p, vals, 1, 0) / den, masks)
        l_ref[...] = _unstack_heads(jnp.broadcast_to(m + jnp.log(den), (HEAD_ROWS, ATT_MERGED)), masks)

    shape = jax.ShapeDtypeStruct(q.shape, F32)
    return pl.pallas_call(
        body, name=name, grid=(dil, n_blk), in_specs=[cur, prev, cur, prev, cur], out_specs=[cur, cur],
        out_shape=[shape, shape], compiler_params=_params("parallel", "parallel"),
    )(q, k, k, v, v)


def _att_merge(outs, lses, *, name, tm=512):
    n_g = len(outs)
    n_rows = outs[0].shape[0] * DILATIONS[0]

    def body(*refs):
        src, (att_ref, lse_ref), tmp = refs[:2 * n_g], refs[2 * n_g:2 * n_g + 2], refs[2 * n_g + 2:]
        vals = []
        for idx, src_ref in enumerate(src):
            dil = DILATIONS[idx % n_g]
            if dil == 1:
                vals.append(src_ref[...])
                continue
            for r in range(dil):
                for half in range(LANE_HALVES):
                    lo = r * ATT_MERGED + half * LANES
                    tmp[LANE_HALVES * idx + half][_strided_rows(r, tm // dil, dil), :] = src_ref[:, lo:lo + LANES]
            vals.append(jnp.concatenate([tmp[LANE_HALVES * idx + half][...] for half in range(LANE_HALVES)], axis=1))
        o, l = vals[:n_g], vals[n_g:]
        m = functools.reduce(jnp.maximum, l)
        e = [jnp.exp(li - m) for li in l]
        z = functools.reduce(jnp.add, e)
        att_ref[...] = functools.reduce(jnp.add, [(ei / z) * oi for ei, oi in zip(e, o)])
        lse_ref[...] = m + jnp.log(z)

    in_specs = [pl.BlockSpec((tm // dil, dil * ATT_MERGED), lambda i: (i, 0)) for _ in range(2) for dil in DILATIONS]
    row = pl.BlockSpec((tm, ATT_MERGED), lambda i: (i, 0))
    shape = jax.ShapeDtypeStruct((n_rows, ATT_MERGED), F32)
    return pl.pallas_call(
        body, name=name, grid=(n_rows // tm,), in_specs=in_specs, out_specs=[row, row], out_shape=[shape, shape],
        scratch_shapes=[pltpu.VMEM((tm, LANES), F32)] * (LANE_HALVES * 2 * n_g), compiler_params=_params("parallel"),
    )(*outs, *lses)


def _att_stats(datt, att, lse, *, name):
    n_rows = datt.shape[0]

    def fn(d, a, l):
        prod = d * a
        lane = lax.broadcasted_iota(jnp.int32, (d.shape[0], LANES), 1)
        out = jnp.zeros((d.shape[0], LANES), F32)
        for h in range(ATT_HEADS_PER_GROUP):
            lo = h * ATT_HEAD_DIM
            out = jnp.where(lane == h, l[:, lo:lo + 1], out)
            delta = jnp.sum(prod[:, lo:lo + ATT_HEAD_DIM], axis=-1, keepdims=True)
            out = jnp.where(lane == ATT_HEADS_PER_GROUP + h, delta, out)
        return out

    rows = [(t, ATT_MERGED, 0) for t in (datt, att, lse)]
    return _rowcall(fn, rows, [], [(LANES, F32)], n_rows=n_rows, tm=512, name=name)[0]


def _dil_bwd(q, k, v, datt, stats, dil, *, name):
    n_rows = datt.shape[0]
    n_blk = n_rows // dil // ATT_BLK
    span = ATT_BLK * dil
    cur = pl.BlockSpec((ATT_BLK, ATT_MERGED), lambda n, r: (n, r))
    prev = pl.BlockSpec((ATT_BLK, ATT_MERGED), lambda n, r: (jnp.maximum(n - 1, 0), r))
    nxt = pl.BlockSpec((ATT_BLK, ATT_MERGED), lambda n, r: (jnp.minimum(n + 1, n_blk - 1), r))
    seq = lambda half, ahead: pl.BlockSpec((span, LANES), lambda n, r: (jnp.minimum(n + ahead, n_blk - 1), half))

    def body(qc_ref, qn_ref, kp_ref, kc_ref, vp_ref, vc_ref, dc0_ref, dc1_ref, dn0_ref, dn1_ref, sc_ref, sn_ref,
             dq0_ref, dq1_ref, dk0_ref, dk1_ref, dv0_ref, dv1_ref):
        n = pl.program_id(0)
        rows = slice(None) if dil == 1 else _strided_rows(pl.program_id(1), ATT_BLK, dil)

        def read(ref0, ref1):
            return jnp.concatenate([ref0[rows, :], ref1[rows, :]], axis=1)

        def write(ref0, ref1, val):
            ref0[rows, :] = val[:, :LANES]
            ref1[rows, :] = val[:, LANES:]

        masks = _head_masks(ATT_BLK)
        valid = _band_mask(jnp.where(n > 0, 0, ATT_BLK))
        qi = lax.broadcasted_iota(jnp.int32, (HEAD_ROWS, ATT_BLK), 0) & (ATT_BLK - 1)
        ki = lax.broadcasted_iota(jnp.int32, (HEAD_ROWS, ATT_BLK), 1)
        valid_next = (ki - qi) >= jnp.where(n < n_blk - 1, 0, ATT_BLK)

        kc, vc = kc_ref[...], vc_ref[...]
        keys = jnp.concatenate([kp_ref[...], kc], axis=0)
        vals = jnp.concatenate([vp_ref[...], vc], axis=0)
        q4 = _stack_heads(qc_ref[...], masks)
        d4 = _stack_heads(read(dc0_ref, dc1_ref).astype(MXU_DTYPE), masks)
        st = sc_ref[rows, :]
        p = jnp.where(valid, jnp.exp(_dot(q4, keys, 1, 1) * ATT_SCALE - _head_column(st, 0)), 0.0)
        ds = p * (_dot(d4, vals, 1, 1) - _head_column(st, ATT_HEADS_PER_GROUP)) * ATT_SCALE
        write(dq0_ref, dq1_ref, _unstack_heads(_dot(ds, keys, 1, 0), masks))

        q4n = _stack_heads(qn_ref[...], masks)
        d4n = _stack_heads(read(dn0_ref, dn1_ref).astype(MXU_DTYPE), masks)
        stn = sn_ref[rows, :]
        p_n = jnp.where(valid_next, jnp.exp(_dot(q4n, kc, 1, 1) * ATT_SCALE - _head_column(stn, 0)), 0.0)
        ds_n = p_n * (_dot(d4n, vc, 1, 1) - _head_column(stn, ATT_HEADS_PER_GROUP)) * ATT_SCALE
        write(dv0_ref, dv1_ref, _dot(p[:, ATT_BLK:], d4, 0, 0) + _dot(p_n, d4n, 0, 0))
        write(dk0_ref, dk1_ref, _dot(ds[:, ATT_BLK:], q4, 0, 0) + _dot(ds_n, q4n, 0, 0))

    shape = jax.ShapeDtypeStruct((n_rows, LANES), F32)
    out = seq(0, 0)
    res = pl.pallas_call(
        body, name=name, grid=(n_blk, dil),
        in_specs=[cur, nxt, prev, cur, prev, cur, seq(0, 0), seq(1, 0), seq(0, 1), seq(1, 1), seq(0, 0), seq(0, 1)],
        out_specs=[out] * 6, out_shape=[shape] * 6, compiler_params=_params("parallel", "arbitrary"),
    )(q, q, k, k, v, v, datt, datt, datt, datt, stats, stats)
    return [(res[2 * i], res[2 * i + 1]) for i in range(3)]


def _dproj_assemble(du, dqkv, dgs, dga, cos_t, sin_t, *, name):
    n_g = len(DILATIONS)

    def fn(*t):
        n_half = LANE_HALVES * 3 * n_g
        du_t, halves, (dgs_t, dga_t, c, s) = t[0], t[1:1 + n_half], t[1 + n_half:]
        parts = [jnp.concatenate(halves[LANE_HALVES * i:LANE_HALVES * (i + 1)], axis=1) for i in range(3 * n_g)]
        for i in range(2 * n_g):
            parts[i] = _rope_transpose(parts[i], c, s)
        cast = [p.astype(MXU_DTYPE) for p in parts]
        return [jnp.concatenate([du_t] + cast + [dgs_t, dga_t], axis=1)] + [_colsum(p) for p in parts]

    rows = [(du, SSM_WIDTH, 0)]
    rows += [(half, LANES, 0) for i in range(3) for g in range(n_g) for half in dqkv[g][i]]
    rows += [(dgs, D_MODEL, 0), (dga, D_MODEL, 0), (cos_t, ATT_MERGED, 0), (sin_t, ATT_MERGED, 0)]
    width = SSM_WIDTH + 3 * n_g * ATT_MERGED + 2 * D_MODEL
    res = _rowcall(fn, rows, [], [(width, MXU_DTYPE)], [ATT_MERGED] * (3 * n_g), n_rows=du.shape[0], tm=256, name=name)
    return res[0], res[1:]


def _xhead(h):
    return slice(h * XATT_HEAD_DIM, (h + 1) * XATT_HEAD_DIM)


def _xatt_probs(qh, kh):
    s = _dot(qh, kh, 1, 1) * XATT_SCALE
    e = jnp.exp(s - jnp.max(s, axis=-1, keepdims=True))
    return e / jnp.sum(e, axis=-1, keepdims=True)


def _xatt_fwd(q, kv, *, name, tm=512):
    n_rows = q.shape[0]
    n_mem = kv.shape[0]

    def body(q_ref, kv_ref, o_ref):
        for h in range(XATT_HEADS):
            sl = _xhead(h)
            p = _xatt_probs(q_ref[:, sl], kv_ref[:, sl])
            o_ref[:, sl] = _dot(p, kv_ref[:, D_MODEL + h * XATT_HEAD_DIM:D_MODEL + (h + 1) * XATT_HEAD_DIM], 1, 0
                                ).astype(o_ref.dtype)

    row = pl.BlockSpec((tm, D_MODEL), lambda i: (i, 0))
    return pl.pallas_call(
        body, name=name, grid=(n_rows // tm,),
        in_specs=[row, pl.BlockSpec((n_mem, 2 * D_MODEL), lambda i: (0, 0))], out_specs=row,
        out_shape=jax.ShapeDtypeStruct((n_rows, D_MODEL), MXU_DTYPE), compiler_params=_params("parallel"),
    )(q, kv)


def _xatt_bwd(q, kv, do, *, name, tm=512):
    n_rows = q.shape[0]
    n_mem = kv.shape[0]

    def body(q_ref, kv_ref, do_ref, dq_ref, dkv_ref):
        @pl.when(pl.program_id(0) == 0)
        def _():
            dkv_ref[...] = jnp.zeros_like(dkv_ref)

        for h in range(XATT_HEADS):
            sl = _xhead(h)
            vsl = slice(D_MODEL + h * XATT_HEAD_DIM, D_MODEL + (h + 1) * XATT_HEAD_DIM)
            qh, kh, doh = q_ref[:, sl], kv_ref[:, sl], do_ref[:, sl]
            p = _xatt_probs(qh, kh)
            dp = _dot(doh, kv_ref[:, vsl], 1, 1)
            ds = p * (dp - jnp.sum(dp * p, axis=-1, keepdims=True)) * XATT_SCALE
            dq_ref[:, sl] = _dot(ds, kh, 1, 0).astype(dq_ref.dtype)
            dkv_ref[:, sl] += _dot(ds, qh, 0, 0)
            dkv_ref[:, vsl] += _dot(p, doh, 0, 0)

    row = pl.BlockSpec((tm, D_MODEL), lambda i: (i, 0))
    full = pl.BlockSpec((n_mem, 2 * D_MODEL), lambda i: (0, 0))
    return pl.pallas_call(
        body, name=name, grid=(n_rows // tm,), in_specs=[row, full, row], out_specs=[row, full],
        out_shape=[jax.ShapeDtypeStruct((n_rows, D_MODEL), MXU_DTYPE), jax.ShapeDtypeStruct((n_mem, 2 * D_MODEL), F32)],
        compiler_params=_params("arbitrary"),
    )(q, kv, do)


def _disc(logdt, a_re, a_im, b_re, b_im):
    dt = jnp.exp(logdt)
    mag = jnp.exp(a_re * dt)
    ab_re = mag * jnp.cos(a_im * dt)
    ab_im = mag * jnp.sin(a_im * dt)
    den = jnp.square(a_re) + jnp.square(a_im)
    nr = ab_re - 1.0
    f_re = (nr * a_re + ab_im * a_im) / den
    f_im = (ab_im * a_re - nr * a_im) / den
    bb_re = f_re[None] * b_re - f_im[None] * b_im
    bb_im = f_re[None] * b_im + f_im[None] * b_re
    return ab_re, ab_im, bb_re, bb_im


def _disc_transpose(logdt, a_re, a_im, b_re, b_im, g_ab_re, g_ab_im, g_bb_re, g_bb_im):
    dt = jnp.exp(logdt)
    mag = jnp.exp(a_re * dt)
    th = a_im * dt
    cs, sn = jnp.cos(th), jnp.sin(th)
    ab_re, ab_im = mag * cs, mag * sn
    den = jnp.square(a_re) + jnp.square(a_im)
    nr = ab_re - 1.0
    f_re = (nr * a_re + ab_im * a_im) / den
    f_im = (ab_im * a_re - nr * a_im) / den
    d_f_re = jnp.sum(g_bb_re * b_re + g_bb_im * b_im, axis=0)
    d_f_im = jnp.sum(g_bb_im * b_re - g_bb_re * b_im, axis=0)
    d_b_re = g_bb_re * f_re[None] + g_bb_im * f_im[None]
    d_b_im = g_bb_im * f_re[None] - g_bb_re * f_im[None]
    d_n_re, d_n_im = d_f_re / den, d_f_im / den
    d_den = -(d_f_re * f_re + d_f_im * f_im) / den
    d_ab_re = g_ab_re + d_n_re * a_re - d_n_im * a_im
    d_ab_im = g_ab_im + d_n_re * a_im + d_n_im * a_re
    d_a_re = d_n_re * nr + d_n_im * ab_im + 2.0 * d_den * a_re
    d_a_im = d_n_re * ab_im - d_n_im * nr + 2.0 * d_den * a_im
    d_mag = d_ab_re * cs + d_ab_im * sn
    d_th = mag * (d_ab_im * cs - d_ab_re * sn)
    d_a_re = d_a_re + d_mag * mag * dt
    d_a_im = d_a_im + d_th * dt
    d_dt = jnp.sum(d_mag * mag * a_re + d_th * a_im, axis=-1, keepdims=True)
    return d_dt * dt, d_a_re, d_a_im, d_b_re, d_b_im


def _full_spec(shape):
    return pl.BlockSpec(tuple(shape), functools.partial(lambda i, nd: (0,) * nd, nd=len(shape)))


def _whole(fn, args, out_shapes, *, name):
    n_in = len(args)

    def body(*refs):
        res = fn(*[r[...] for r in refs[:n_in]])
        for o_ref, val in zip(refs[n_in:], res):
            o_ref[...] = val

    return pl.pallas_call(
        body, name=name, grid=(1,), in_specs=[_full_spec(t.shape) for t in args],
        out_specs=[_full_spec(s) for s in out_shapes], out_shape=[jax.ShapeDtypeStruct(s, F32) for s in out_shapes],
        compiler_params=_params("arbitrary"))(*args)


SSM_WIDE = GROUPS_PER_TILE * SSM_STATE
LANE_GROUPS_PER_TILE = SSM_WIDE // LANES


def _chan(j):
    return slice(j * LANES, (j + 1) * LANES)


def _time_major_rows(j, q, tc):
    return pl.ds(j * LANE_GROUPS_PER_TILE + q, tc, stride=STATE_VREG_ROWS)


def _to_time_major(x, t_re_ref, t_im_ref, dst_re, dst_im, tc):
    for j in range(SSM_TILES):
        xj = x[:, _chan(j)]
        for t_ref, dst in ((t_re_ref, dst_re), (t_im_ref, dst_im)):
            r = _dot(xj, t_ref[j], 1, 0)
            for q in range(LANE_GROUPS_PER_TILE):
                dst[_time_major_rows(j, q, tc), :] = r[:, q * LANES:(q + 1) * LANES]


def _from_time_major(src, j, tc):
    return jnp.concatenate([src[_time_major_rows(j, q, tc), :] for q in range(LANE_GROUPS_PER_TILE)], axis=1)


def _scan_chunk(w_re, w_im, h_re, h_im, a_re, a_im, start, tc):
    def step(t, carry):
        hr, hi = carry
        rows = _scan_rows(t)
        nr = a_re * hr - a_im * hi + w_re[rows, :]
        ni = a_re * hi + a_im * hr + w_im[rows, :]
        h_re[rows, :] = nr
        h_im[rows, :] = ni
        return nr, ni

    return lax.fori_loop(0, tc, step, start, unroll=8)


SSM_CHUNK = 256


def _tile_spec(stack, k):
    return pl.BlockSpec((pl.Squeezed(),) + tuple(stack.shape[1:]), lambda i: (k, 0, 0, 0))


def _expand_block_diagonal(src_ref, dst):
    dst[...] = jnp.zeros_like(dst)
    r, c = src_ref.shape[1:]
    for g in range(SSM_GROUPS):
        j, gl = divmod(g, GROUPS_PER_TILE)
        dst[j, gl * r:(gl + 1) * r, gl * c:(gl + 1) * c] = src_ref[g].astype(dst.dtype)


def _extract_block_diagonal(src, dst_ref):
    r, c = dst_ref.shape[1:]
    for g in range(SSM_GROUPS):
        j, gl = divmod(g, GROUPS_PER_TILE)
        dst_ref[g] = src[j, gl * r:(gl + 1) * r, gl * c:(gl + 1) * c]


def _ssm_fwd(proj, blocks_cn, blocks_nc, a_re, a_im, gain, *, name, tc=SSM_CHUNK):
    n_rows = proj.shape[0]
    n_chunk = n_rows // tc

    def body(u_ref, br_ref, bi_ref, cr_ref, ci_ref, ar_ref, ai_ref, g_ref, y_ref, gy_ref, hr, hi, wr, wi, state,
             tbr_ref, tbi_ref, tcr_ref, tci_ref):
        @pl.when(pl.program_id(0) == 0)
        def _():
            state[...] = jnp.zeros_like(state)
            for src_ref, dst in ((br_ref, tbr_ref), (bi_ref, tbi_ref), (cr_ref, tcr_ref), (ci_ref, tci_ref)):
                _expand_block_diagonal(src_ref, dst)

        u = u_ref[...]
        _to_time_major(u, tbr_ref, tbi_ref, wr, wi, tc)
        state[0], state[1] = _scan_chunk(wr, wi, hr, hi, ar_ref[...], ai_ref[...], (state[0], state[1]), tc)
        for j in range(SSM_TILES):
            yj = (_dot(_from_time_major(hr, j, tc), tcr_ref[j], 1, 0) + _dot(_from_time_major(hi, j, tc), tci_ref[j], 1, 0)
                  + g_ref[:, _chan(j)] * u[:, _chan(j)])
            y_ref[:, _chan(j)] = yj
            gy_ref[:, _chan(j)] = jax.nn.gelu(yj).astype(gy_ref.dtype)

    rows = pl.BlockSpec((tc, SSM_WIDTH), lambda i: (i, 0))
    coef = pl.BlockSpec((STATE_VREG_ROWS, LANES), lambda i: (0, 0))
    states = pl.BlockSpec((tc * STATE_VREG_ROWS, LANES), lambda i: (i, 0))
    sshape = jax.ShapeDtypeStruct((n_rows * STATE_VREG_ROWS, LANES), F32)
    return pl.pallas_call(
        body, name=name, grid=(n_chunk,),
        in_specs=[rows, _tile_spec(blocks_cn, 0), _tile_spec(blocks_cn, 1), _tile_spec(blocks_nc, 0),
                  _tile_spec(blocks_nc, 1), coef, coef, pl.BlockSpec((1, SSM_WIDTH), lambda i: (0, 0))],
        out_specs=[rows, rows, states, states],
        out_shape=[jax.ShapeDtypeStruct((n_rows, SSM_WIDTH), F32), jax.ShapeDtypeStruct((n_rows, SSM_WIDTH), MXU_DTYPE),
                   sshape, sshape],
        scratch_shapes=[pltpu.VMEM((tc * STATE_VREG_ROWS, LANES), F32)] * 2 + [pltpu.VMEM((2, STATE_VREG_ROWS, LANES), F32)]
        + [pltpu.VMEM((SSM_TILES, LANES, SSM_WIDE), MXU_DTYPE)] * 2 + [pltpu.VMEM((SSM_TILES, SSM_WIDE, LANES), MXU_DTYPE)] * 2,
        compiler_params=_params("arbitrary"),
    )(proj, blocks_cn, blocks_cn, blocks_nc, blocks_nc, a_re, a_im, gain)


def _ssm_bwd(proj, dy, h_re, h_im, blocks_cn, blocks_nc, a_re, a_im, gain, *, name, tc=SSM_CHUNK):
    n_rows = proj.shape[0]
    n_chunk = n_rows // tc

    def body(u_ref, dy_ref, hr, hi, cr_ref, ci_ref, br_ref, bi_ref, ar_ref, ai_ref, g_ref,
             du_ref, su_ref, dc_re_ref, dc_im_ref, db_re_ref, db_im_ref, dar_ref, dai_ref, wr, wi, carry,
             tdr_ref, tdi_ref, tur_ref, tui_ref, dcr_ref, dci_ref, dbr_ref, dbi_ref):
        @pl.when(pl.program_id(0) == 0)
        def _():
            carry[...] = jnp.zeros_like(carry)
            for acc_ref in (su_ref, dcr_ref, dci_ref, dbr_ref, dbi_ref):
                acc_ref[...] = jnp.zeros_like(acc_ref)
            for src_ref, dst in ((cr_ref, tdr_ref), (ci_ref, tdi_ref), (br_ref, tur_ref), (bi_ref, tui_ref)):
                _expand_block_diagonal(src_ref, dst)

        a_r, a_i = ar_ref[...], ai_ref[...]
        u, dyv = u_ref[...], dy_ref[...]
        _to_time_major(dyv, tdr_ref, tdi_ref, wr, wi, tc)

        def step(kk, c):
            lam_r, lam_i, dar, dai = c
            rows = _scan_rows(tc - 1 - kk)
            h_r, h_i = hr[rows, :], hi[rows, :]
            dar = dar + lam_r * h_r + lam_i * h_i
            dai = dai + lam_i * h_r - lam_r * h_i
            new_r = wr[rows, :] + a_r * lam_r + a_i * lam_i
            new_i = wi[rows, :] + a_r * lam_i - a_i * lam_r
            wr[rows, :] = new_r
            wi[rows, :] = new_i
            return new_r, new_i, dar, dai

        carry[0], carry[1], carry[2], carry[3] = lax.fori_loop(0, tc, step, (carry[0], carry[1], carry[2], carry[3]),
                                                              unroll=8)
        dar_ref[...] = carry[2]
        dai_ref[...] = carry[3]
        for j in range(SSM_TILES):
            cj = _chan(j)
            lam_r, lam_i = _from_time_major(wr, j, tc), _from_time_major(wi, j, tc)
            dcr_ref[j] += _dot(dyv[:, cj], _from_time_major(hr, j, tc), 0, 0)
            dci_ref[j] += _dot(dyv[:, cj], _from_time_major(hi, j, tc), 0, 0)
            dbr_ref[j] += _dot(u[:, cj], lam_r, 0, 0)
            dbi_ref[j] += _dot(u[:, cj], lam_i, 0, 0)
            duj = _dot(lam_r, tur_ref[j], 1, 0) + _dot(lam_i, tui_ref[j], 1, 0) + g_ref[:, cj] * dyv[:, cj]
            du_ref[:, cj] = duj.astype(du_ref.dtype)
            su_ref[:, cj] += _colsum(duj)

        @pl.when(pl.program_id(0) == n_chunk - 1)
        def _():
            for src, dst_ref in ((dcr_ref, dc_re_ref), (dci_ref, dc_im_ref), (dbr_ref, db_re_ref), (dbi_ref, db_im_ref)):
                _extract_block_diagonal(src, dst_ref)

    back = lambda i: (n_chunk - 1 - i, 0)
    rows = pl.BlockSpec((tc, SSM_WIDTH), back)
    blocks = pl.BlockSpec((SSM_GROUPS, SSM_GROUP, SSM_STATE), lambda i: (0, 0, 0))
    coef = pl.BlockSpec((STATE_VREG_ROWS, LANES), lambda i: (0, 0))
    states = pl.BlockSpec((tc * STATE_VREG_ROWS, LANES), back)
    vec = pl.BlockSpec((1, SSM_WIDTH), lambda i: (0, 0))
    bshape = jax.ShapeDtypeStruct((SSM_GROUPS, SSM_GROUP, SSM_STATE), F32)
    cshape = jax.ShapeDtypeStruct((STATE_VREG_ROWS, LANES), F32)
    return pl.pallas_call(
        body, name=name, grid=(n_chunk,),
        in_specs=[rows, rows, states, states, _tile_spec(blocks_cn, 2), _tile_spec(blocks_cn, 3), _tile_spec(blocks_nc, 2),
                  _tile_spec(blocks_nc, 3), coef, coef, vec],
        out_specs=[rows, vec, blocks, blocks, blocks, blocks, coef, coef],
        out_shape=[jax.ShapeDtypeStruct((n_rows, SSM_WIDTH), MXU_DTYPE), jax.ShapeDtypeStruct((1, SSM_WIDTH), F32),
                   bshape, bshape, bshape, bshape, cshape, cshape],
        scratch_shapes=[pltpu.VMEM((tc * STATE_VREG_ROWS, LANES), F32)] * 2 + [pltpu.VMEM((4, STATE_VREG_ROWS, LANES), F32)]
        + [pltpu.VMEM((SSM_TILES, LANES, SSM_WIDE), MXU_DTYPE)] * 2 + [pltpu.VMEM((SSM_TILES, SSM_WIDE, LANES), MXU_DTYPE)] * 2
        + [pltpu.VMEM((SSM_TILES, LANES, SSM_WIDE), F32)] * 4,
        compiler_params=_params("arbitrary"),
    )(proj, dy, h_re, h_im, blocks_cn, blocks_cn, blocks_nc, blocks_nc, a_re, a_im, gain)


def _scan_rows(t):
    return pl.ds(pl.multiple_of(t * STATE_VREG_ROWS, 8), STATE_VREG_ROWS)


GATHER_GROUPS = (("w_glu", "w_att_up", "w_mix_out"), ("w_xq", "w_xkv", "w_xo", "w_ff1", "w_ff2"))
SCATTER_GROUPS = (("w_ff2", "w_ff1"), ("w_xo", "w_xq", "w_xkv", "w_mix_out"), ("w_att_up", "w_glu"), ("w_in",))


def _local_grads(x, mem, pos_col, target, sm, fetch_in, fetch, send, send_small, start_token):
    b_re_t = sm["ssm_b_re"].transpose(2, 0, 1)
    b_im_t = sm["ssm_b_im"].transpose(2, 0, 1)
    logdt = sm["ssm_log_dt"].reshape(SSM_GROUPS, 1)
    c_re, c_im = sm["ssm_c_re"], sm["ssm_c_im"]
    grp = (SSM_GROUPS, SSM_STATE)
    chn = (SSM_GROUP, SSM_GROUPS, SSM_STATE)

    wts = {}
    cos_t, sin_t = _rope_tables(pos_col, after=start_token, name="rope_tables")
    h0, xh0, rs0, h0m = _ln_fwd(x, sm["ln_in_g"], sm["ln_in_b"], name="ln_in_fwd")
    disc_in = (logdt, sm["ssm_a_re"], sm["ssm_a_im"], b_re_t, b_im_t)
    ab_re, ab_im, bb_re_t, bb_im_t = _whole(_disc, disc_in, [grp, grp, chn, chn], name="ssm_disc")
    a_re_rows, a_im_rows = ab_re.reshape(STATE_VREG_ROWS, LANES), ab_im.reshape(STATE_VREG_ROWS, LANES)
    tiles_cn = jnp.stack([bb_re_t.transpose(1, 0, 2), bb_im_t.transpose(1, 0, 2), c_re, -c_im])
    tiles_nc = jnp.stack([c_re.transpose(0, 2, 1), -c_im.transpose(0, 2, 1), bb_re_t.transpose(1, 2, 0),
                          bb_im_t.transpose(1, 2, 0)])
    w_in_near, near_ids = fetch_in(0, [h0m, tiles_cn, tiles_nc])
    proj = _mm_shards(h0m, w_in_near, sm["b_in"], near_ids, name="in_proj_near")
    wts["w_in"], far_ids = fetch_in(1, [proj])
    proj = _mm_shards(h0m, wts["w_in"], sm["b_in"], far_ids, prev=proj, name="in_proj_far")

    y, gy, h_re, h_im = _ssm_fwd(proj, tiles_cn, tiles_nc, a_re_rows, a_im_rows, sm["ssm_d"], name="ssm_fwd")

    q, k, v = _qkv_split(proj, cos_t, sin_t, name="qkv_split")
    outs, lses = [], []
    for g, dil in enumerate(DILATIONS):
        o_g, l_g = _dil_fwd(q[g], k[g], v[g], dil, name=f"dil_att_fwd_{dil}")
        outs.append(o_g)
        lses.append(l_g)
    att, lse = _att_merge(outs, lses, name="att_merge")
    wts.update(fetch(0, [att]))
    z = _mm(gy, wts["w_glu"], bias=sm["b_glu"], b_shards=True, name="glu_proj")
    b_att = _mm(att, wts["w_att_up"], b_shards=True, name="att_up")

    mixed, h1, xh1, rs1, h1m = _mix_out_ln(proj, z, b_att, wts["w_mix_out"], sm["b_mix_out"], h0, sm["ln1_g"],
                                           sm["ln1_b"], alpha=DEEPNORM_ALPHA, name="gate_mix_out_ln1")

    wts.update(fetch(1, [h1m]))
    xq = _mm(h1m, wts["w_xq"], out_dtype=MXU_DTYPE, name="xatt_q")
    kv = _mm(mem, wts["w_xkv"], out_dtype=MXU_DTYPE, b_shards=True, name="xatt_kv")
    xo_in = _xatt_fwd(xq, kv, name="xatt_fwd")
    h2, xh2, rs2, h2m = _mm_ln_fwd(xo_in, wts["w_xo"], None, h1, sm["ln2_g"], sm["ln2_b"], alpha=DEEPNORM_ALPHA,
                                   name="xatt_o_ln2")

    pre, act = _mm(h2m, wts["w_ff1"], bias=sm["b_ff1"], b_shards=True, name="ff1",
                   also=(lambda r: jnp.square(jnp.maximum(r, 0.0)), MXU_DTYPE))
    ff = _mm(act, wts["w_ff2"], bias=sm["b_ff2"], name="ff2")

    gw, gs = {}, {}
    dr3, dr3m, gs["ln3_g"], gs["ln3_b"], gs["b_ff2"], loss_row = _ln_loss_bwd(
        h2, ff, target, sm["ln3_g"], sm["ln3_b"], alpha=DEEPNORM_ALPHA, name="ln3_loss")
    wgrad = functools.partial(_mm, ta=True, out_dtype=WIRE_DTYPE, tk=2048)
    gw["w_ff2"] = wgrad(act, dr3m, tk=1024, name="ff2_dw")
    dpre, gs["b_ff1"] = _mm(dr3m, wts["w_ff2"], tb=True, out_dtype=MXU_DTYPE, colsum=True, name="ff2_dx",
                            gate=(pre, lambda p: 2.0 * jnp.maximum(p, 0.0)))
    gw["w_ff1"] = wgrad(h2m, dpre, out_shards=True, name="ff1_dw")
    sent = send(0, gw)
    dh2 = _mm(dpre, wts["w_ff1"], tb=True, b_shards=True, after=sent, name="ff1_dx")
    dr2, dr2m, gs["ln2_g"], gs["ln2_b"], _ = _ln_bwd(dr3, dh2, xh2, rs2, sm["ln2_g"], alpha=DEEPNORM_ALPHA,
                                                     name="ln2_bwd")
    gw["w_xo"] = wgrad(xo_in, dr2m, name="xatt_o_dw")
    dxo_in = _mm(dr2m, wts["w_xo"], tb=True, out_dtype=MXU_DTYPE, name="xatt_o_dx")
    dxq, dkv = _xatt_bwd(xq, kv, dxo_in, name="xatt_bwd")
    gw["w_xq"] = wgrad(h1m, dxq, name="xatt_q_dw")
    gw["w_xkv"] = wgrad(mem, dkv, out_shards=True, name="xatt_kv_dw")
    dr1, dr1m, gs["ln1_g"], gs["ln1_b"], gs["b_mix_out"] = _mm_ln_bwd(
        dxq, wts["w_xq"], dr2, xh1, rs1, sm["ln1_g"], alpha=DEEPNORM_ALPHA, name="xatt_q_dx_ln1")
    gw["w_mix_out"] = wgrad(mixed, dr1m, name="mix_out_dw")
    sent = send(1, gw)
    dmixed = _mm(dr1m, wts["w_mix_out"], tb=True, after=sent, name="mix_out_dx")
    dgs, dga, dz, db_att, s_gs, s_ga, gs["b_glu"] = _mix_bwd(dmixed, proj, z, b_att, name="gate_mix_bwd")

    gw["w_att_up"] = wgrad(att, db_att, out_shards=True, name="att_up_dw")
    gw["w_glu"] = wgrad(gy, dz, out_shards=True, name="glu_dw")
    sent = send(2, gw)
    datt = _mm(db_att, wts["w_att_up"], tb=True, b_shards=True, after=sent, name="att_up_dx")
    stats = _att_stats(datt, att, lse, name="att_stats")
    dqkv = [_dil_bwd(q[g], k[g], v[g], datt, stats, dil, name=f"dil_att_bwd_{dil}") for g, dil in enumerate(DILATIONS)]

    dgy = _mm(dz, wts["w_glu"], tb=True, b_shards=True, name="glu_dx")
    dy, gs["ssm_d"] = _gelu_bwd(dgy, y, proj, name="gelu_bwd")
    du, s_u, dc_re_t, dc_im_t, dbb_re_t, dbb_im_t, da_re, da_im = _ssm_bwd(
        proj, dy, h_re, h_im, tiles_cn, tiles_nc, a_re_rows, a_im_rows, sm["ssm_d"], name="ssm_bwd")
    gs["ssm_c_re"], gs["ssm_c_im"] = dc_re_t, -dc_im_t
    disc_ct = (da_re.reshape(grp), da_im.reshape(grp), dbb_re_t.transpose(1, 0, 2), dbb_im_t.transpose(1, 0, 2))
    d_logdt, gs["ssm_a_re"], gs["ssm_a_im"], d_b_re_t, d_b_im_t = _whole(
        _disc_transpose, disc_in + disc_ct, [(SSM_GROUPS, 1), grp, grp, chn, chn], name="ssm_disc_bwd")
    gs["ssm_log_dt"] = d_logdt
    gs["ssm_b_re"], gs["ssm_b_im"] = d_b_re_t.transpose(1, 2, 0), d_b_im_t.transpose(1, 2, 0)

    dproj, s_qkv = _dproj_assemble(du, dqkv, dgs, dga, cos_t, sin_t, name="dproj_assemble")
    gs["b_in"] = jnp.concatenate([s_u, *s_qkv, s_gs, s_ga], axis=1)
    sent = send_small(gs, SMALL_EARLY)
    gw["w_in"] = wgrad(h0m, dproj, out_shards=True, after=sent, name="in_proj_dw")
    sent = send(3, gw)
    dh0 = _mm(dproj, wts["w_in"], tb=True, b_shards=True, after=sent, name="in_proj_dx")
    grad_x, gs["ln_in_g"], gs["ln_in_b"], _ = _ln_bwd(dr1, dh0, xh0, rs0, sm["ln_in_g"], alpha=DEEPNORM_ALPHA,
                                                      operand=False, name="ln_in_bwd")
    return loss_row, grad_x, gs


_IN_HBM = pl.BlockSpec(memory_space=pltpu.HBM)
_IN_SEMAPHORE = pl.BlockSpec(memory_space=pltpu.SEMAPHORE)


def _device_index():
    return 4 * lax.axis_index("x") + 2 * lax.axis_index("y") + lax.axis_index("c")


ALL_PEERS = tuple(range(1, N_DEV))
NEAR_PEERS = (1, 2, 3, 4, 5)
FAR_PEERS = (6, 7)


def _peer_index(kk):
    x, y, c = lax.axis_index("x"), lax.axis_index("y"), lax.axis_index("c")
    return 4 * ((x + (kk >> 2)) % 2) + 2 * ((y + ((kk >> 1) & 1)) % 2) + (c + (kk & 1)) % 2


def _exchange_copies(src_refs, land_refs, send_sems, recv_sems, scatter, peers):
    x, y, c = lax.axis_index("x"), lax.axis_index("y"), lax.axis_index("c")
    me = 4 * x + 2 * y + c
    pairs = []
    for a, (src_ref, land_ref) in enumerate(zip(src_refs, land_refs)):
        for idx, kk in enumerate(peers):
            px = (x + (kk >> 2)) % 2
            py = (y + ((kk >> 1) & 1)) % 2
            pc = (c + (kk & 1)) % 2
            peer = 4 * px + 2 * py + pc
            sem = a * len(peers) + idx
            src = src_ref.at[peer] if scatter else src_ref

            def copy(dst, src=src, sem=sem, px=px, py=py, pc=pc):
                return pltpu.make_async_remote_copy(
                    src_ref=src, dst_ref=dst, send_sem=send_sems.at[sem], recv_sem=recv_sems.at[sem],
                    device_id=(px, py, pc), device_id_type=pl.DeviceIdType.MESH)

            pairs.append((functools.partial(copy, land_ref.at[me]), functools.partial(copy, land_ref.at[peer])))
    return pairs


def _own_copies(src_refs, land_refs, own_sems, scatter):
    me = _device_index()
    return [functools.partial(pltpu.make_async_copy, src_ref.at[me] if scatter else src_ref, land_ref.at[me],
                              own_sems.at[a]) for a, (src_ref, land_ref) in enumerate(zip(src_refs, land_refs))]


def _exchange_start(srcs, *, scatter, name, after=None, peers=ALL_PEERS, lands=None):
    n_arr, n_sem = len(srcs), len(srcs) * len(peers)
    own = lands is None
    if own:
        lands = [lax.empty((N_DEV,) + tuple(s.shape[1:] if scatter else s.shape), s.dtype) for s in srcs]
    n_in = 2 * n_arr + (after is not None)

    def body(*refs):
        send_sems, recv_sems = refs[n_in], refs[n_in + 1]
        for sent, _ in _exchange_copies(refs[:n_arr], refs[n_arr:2 * n_arr], send_sems, recv_sems, scatter, peers):
            sent().start()
        if own:
            for local in _own_copies(refs[:n_arr], refs[n_arr:2 * n_arr], refs[n_in + 2], scatter):
                local().start()
        refs[-1][...] = jnp.zeros_like(refs[-1])

    sems = [pltpu.SemaphoreType.DMA((n_sem,)), pltpu.SemaphoreType.DMA((n_sem,))] + [pltpu.SemaphoreType.DMA((n_arr,))] * own
    through = [pltpu.HBM(t.shape, t.dtype) for t in (*srcs, *lands)]
    res = pl.pallas_call(
        body, name=name, out_shape=(*sems, *through, jax.ShapeDtypeStruct((8, LANES), F32)),
        in_specs=[_IN_HBM] * (2 * n_arr) + [pl.BlockSpec(memory_space=pl.ANY)] * (after is not None),
        out_specs=(*[_IN_SEMAPHORE] * len(sems), *[_IN_HBM] * (2 * n_arr), pl.BlockSpec(memory_space=pltpu.VMEM)),
        input_output_aliases={i: len(sems) + i for i in range(2 * n_arr)},
        compiler_params=pltpu.CompilerParams(has_side_effects=pltpu.SideEffectType.DATAFLOW_SIDE_EFFECTING),
    )(*[pltpu.with_memory_space_constraint(t, pltpu.HBM) for t in (*srcs, *lands)],
      *([after] if after is not None else []))
    first = len(sems)
    handle = dict(sems=res[:first], srcs=res[first:first + n_arr], lands=res[first + n_arr:first + 2 * n_arr],
                  scatter=scatter, peers=peers, own=own)
    return handle, res[-1]


def _exchange_wait(handle, *, after, name, srcs=None, lands=None):
    srcs = handle["srcs"] if srcs is None else srcs
    lands = handle["lands"] if lands is None else lands
    sems, scatter, peers, own = handle["sems"], handle["scatter"], handle["peers"], handle["own"]
    n_arr = len(srcs)
    after = list(after)

    def body(*refs):
        src_refs, land_refs = refs[:n_arr], refs[n_arr:2 * n_arr]
        for sent, received in _exchange_copies(src_refs, land_refs, refs[2 * n_arr], refs[2 * n_arr + 1], scatter, peers):
            sent().wait_send()
            received().wait_recv()
        if own:
            for local in _own_copies(src_refs, land_refs, refs[2 * n_arr + 2], scatter):
                local().wait()

    res = pl.pallas_call(
        body, name=name, out_shape=tuple(pltpu.HBM(t.shape, t.dtype) for t in (*srcs, *lands)),
        in_specs=[_IN_HBM] * (2 * n_arr) + [_IN_SEMAPHORE] * len(sems) + [pl.BlockSpec(memory_space=pl.ANY)] * len(after),
        out_specs=tuple([_IN_HBM] * (2 * n_arr)), input_output_aliases={i: i for i in range(2 * n_arr)},
        compiler_params=pltpu.CompilerParams(has_side_effects=pltpu.SideEffectType.DATAFLOW_SIDE_EFFECTING),
    )(*srcs, *lands, *sems, *after)
    return res[:n_arr], res[n_arr:]


def _adamw(g, w, m, v):
    m_new = ADAM_B1 * m + (1.0 - ADAM_B1) * g
    v_new = ADAM_B2 * v + (1.0 - ADAM_B2) * jnp.square(g)
    m_hat = m_new / (1.0 - ADAM_B1 ** ADAM_STEP)
    v_hat = v_new / (1.0 - ADAM_B2 ** ADAM_STEP)
    return g, -ADAM_LR * (m_hat / (jnp.sqrt(v_hat) + ADAM_EPS) + ADAM_WD * w), m_new, v_new


def _reduce_adamw(gstack, w, m, v, *, name, tr=128):
    n_rows, cols = w.shape
    tr = min(tr, n_rows)
    assert n_rows % tr == 0, (name, n_rows, tr)

    def body(g_ref, w_ref, m_ref, v_ref, *out_refs):
        g = g_ref[0].astype(F32)
        for dev in range(1, N_DEV):
            g = g + g_ref[dev].astype(F32)
        for o_ref, val in zip(out_refs, _adamw(g, w_ref[...], m_ref[...], v_ref[...])):
            o_ref[...] = val

    flat = pl.BlockSpec((tr, cols), lambda i: (i, 0))
    shape = jax.ShapeDtypeStruct((n_rows, cols), F32)
    return pl.pallas_call(
        body, name=name, grid=(n_rows // tr,),
        in_specs=[pl.BlockSpec((N_DEV, tr, cols), lambda i: (0, i, 0)), flat, flat, flat],
        out_specs=[flat] * 4, out_shape=[shape] * 4, compiler_params=_params("parallel"),
    )(gstack, w, m, v)


SMALL_FLAT_SSM = ("ssm_b_re", "ssm_b_im", "ssm_c_re", "ssm_c_im")


def _small_view(name, shape):
    size = int(np.prod(shape))
    if name in SMALL_FLAT_SSM:
        return SSM_GROUPS, size // SSM_GROUPS
    if name in ("ssm_a_re", "ssm_a_im"):
        return SSM_GROUPS, SSM_STATE
    return 1, size


def _pack_rows(view):
    return -(-(view[0] * view[1]) // PACK_COLS)


SMALL_LATE = ("ln_in_g", "ln_in_b")
SMALL_EARLY = tuple(n for n in SMALL if n not in SMALL_LATE)


def _pack_small(gs, names, views):
    parts = []
    for n in names:
        flat = gs[n].reshape(-1).astype(WIRE_DTYPE)
        parts.append(jnp.pad(flat, (0, _pack_rows(views[n]) * PACK_COLS - flat.shape[0])))
    total = sum(p.shape[0] for p in parts) // PACK_COLS
    parts.append(jnp.zeros(((-total % PACK_ROW_ALIGN) * PACK_COLS,), WIRE_DTYPE))
    return jnp.concatenate(parts).reshape(-1, PACK_COLS)


def _small_pieces(view):
    rows, cols = view
    if cols == PACK_COLS:
        return [(0, rows, 0, 0, 0, cols)]
    if rows == 1 and cols > PACK_COLS:
        return [(kk, 1, 0, 0, kk * PACK_COLS, PACK_COLS) for kk in range(cols // PACK_COLS)]
    if rows == 1:
        return [(0, 1, 0, 0, 0, cols)]
    return [((r * cols) // PACK_COLS, 1, (r * cols) % PACK_COLS, r, 0, cols) for r in range(rows)]


def _adamw_small(stacks, views, w, m, v, *, name):
    n = len(SMALL)
    place, first = {}, [0, 0]
    for k, names in enumerate((SMALL_EARLY, SMALL_LATE)):
        for name_ in names:
            place[name_] = (k, first[k])
            first[k] += _pack_rows(views[name_])

    def body(early_ref, late_ref, *refs):
        ins, outs = refs[:3 * n], refs[3 * n:]
        for i, name_ in enumerate(SMALL):
            stack_ref = (early_ref, late_ref)[place[name_][0]]
            row0 = place[name_][1]
            for prow, nrows, lane, orow, ocol, width in _small_pieces(views[name_]):
                src = (slice(row0 + prow, row0 + prow + nrows), slice(lane, lane + width))
                dst = (slice(orow, orow + nrows), slice(ocol, ocol + width))
                g = stack_ref[(0,) + src].astype(F32)
                for dev in range(1, N_DEV):
                    g = g + stack_ref[(dev,) + src].astype(F32)
                res = _adamw(g, ins[i][dst], ins[n + i][dst], ins[2 * n + i][dst])
                for kk, val in enumerate(res):
                    outs[kk * n + i][dst] = val

    args = [*stacks, *[d[name_] for d in (w, m, v) for name_ in SMALL]]
    out_views = [views[name_] for _ in range(4) for name_ in SMALL]
    res = pl.pallas_call(
        body, name=name, grid=(1,), in_specs=[_full_spec(t.shape) for t in args],
        out_specs=[_full_spec(s) for s in out_views], out_shape=[jax.ShapeDtypeStruct(s, F32) for s in out_views],
        compiler_params=_params("arbitrary"),
    )(*args)
    return [dict(zip(SMALL, res[kk * n:(kk + 1) * n])) for kk in range(4)]


def kernel(x, mem, positions, ln_in_g, ln_in_b, w_in, b_in, ssm_log_dt, ssm_a_re, ssm_a_im, ssm_b_re, ssm_b_im, ssm_c_re, ssm_c_im, ssm_d, w_glu, b_glu, w_att_up, w_mix_out, b_mix_out, ln1_g, ln1_b, w_xq, w_xkv, w_xo, ln2_g, ln2_b, w_ff1, b_ff1, w_ff2, b_ff2, ln3_g, ln3_b, loss_target, m_ln_in_g, m_ln_in_b, m_w_in, m_b_in, m_ssm_log_dt, m_ssm_a_re, m_ssm_a_im, m_ssm_b_re, m_ssm_b_im, m_ssm_c_re, m_ssm_c_im, m_ssm_d, m_w_glu, m_b_glu, m_w_att_up, m_w_mix_out, m_b_mix_out, m_ln1_g, m_ln1_b, m_w_xq, m_w_xkv, m_w_xo, m_ln2_g, m_ln2_b, m_w_ff1, m_b_ff1, m_w_ff2, m_b_ff2, m_ln3_g, m_ln3_b, v_ln_in_g, v_ln_in_b, v_w_in, v_b_in, v_ssm_log_dt, v_ssm_a_re, v_ssm_a_im, v_ssm_b_re, v_ssm_b_im, v_ssm_c_re, v_ssm_c_im, v_ssm_d, v_w_glu, v_b_glu, v_w_att_up, v_w_mix_out, v_b_mix_out, v_ln1_g, v_ln1_b, v_w_xq, v_w_xkv, v_w_xo, v_ln2_g, v_ln2_b, v_w_ff1, v_b_ff1, v_w_ff2, v_b_ff2, v_ln3_g, v_ln3_b):
    given = dict(locals())
    w_arg = {n: given[n] for n in WEIGHTS}
    m_arg = {n: given["m_" + n] for n in WEIGHTS}
    v_arg = {n: given["v_" + n] for n in WEIGHTS}

    in_near, token = _exchange_start([w_arg["w_in"][0].astype(MXU_DTYPE)], scatter=False, peers=NEAR_PEERS,
                                     name="gather_start_in_near")
    in_far, token = _exchange_start(in_near["srcs"], scatter=False, peers=FAR_PEERS, lands=in_near["lands"],
                                    after=token, name="gather_start_in_far")
    w_in_state = [in_far["srcs"], in_far["lands"]]
    token, w_arg, m_arg, v_arg = lax.optimization_barrier((token, w_arg, m_arg, v_arg))
    shards = {n: w_arg[n][0].astype(MXU_DTYPE) for n in BIG if n != "w_in"}
    gathers = []
    for i, names in enumerate(GATHER_GROUPS):
        handle, token = _exchange_start([shards[n] for n in names], scatter=False, after=token, name=f"gather_start_{i}")
        gathers.append(handle)

    small_views = {n: _small_view(n, w_arg[n].shape) for n in SMALL}
    small_w, small_m, small_v = [{n: d[n].reshape(small_views[n]) for n in SMALL} for d in (w_arg, m_arg, v_arg)]
    relaid = [d[n] for d in (small_w, small_m, small_v) for n in SMALL_FLAT_SSM]

    def fetch_in(part, after):
        handle, peers, tag = ((in_near, (0,) + NEAR_PEERS, "near"), (in_far, FAR_PEERS, "far"))[part]
        w_in_state[:] = _exchange_wait(handle, after=after + (relaid if part == 0 else []), srcs=w_in_state[0],
                                       lands=w_in_state[1], name="gather_wait_in_" + tag)
        return w_in_state[1][0], jnp.stack([_peer_index(kk) for kk in peers]).astype(jnp.int32)

    def fetch(i, after):
        _, lands = _exchange_wait(gathers[i], after=after, name=f"gather_wait_{i}")
        full = dict(zip(GATHER_GROUPS[i], lands))
        return {n: t if n in BIG_COL_SHARDED else t.reshape(-1, t.shape[-1]) for n, t in full.items()}

    scatters = {}

    def send(i, gw):
        slots = [gw[n] if n in BIG_COL_SHARDED else gw[n].reshape(N_DEV, -1, gw[n].shape[-1]) for n in SCATTER_GROUPS[i]]
        handle, sent = _exchange_start(slots, scatter=True, name=f"scatter_start_{i}")
        scatters[i] = (handle, slots)
        return sent

    sm = {}
    for n in SMALL:
        t = w_arg[n]
        if n.startswith("ssm_") and n not in ("ssm_d", "ssm_log_dt"):
            sm[n] = t[0]
        else:
            sm[n] = t.reshape(1, -1)

    smalls = []

    def send_small(gs, names):
        handle, sent = _exchange_start([_pack_small(gs, names, small_views)], scatter=False,
                                       name=f"small_start_{len(smalls)}")
        smalls.append(handle)
        return sent

    loss_row, grad_x, gs = _local_grads(x[0], mem[0], positions.reshape(-1, 1), loss_target[0], sm, fetch_in, fetch,
                                        send, send_small, token)
    loss = lax.psum(loss_row[0, 0], ("x", "y", "c"))
    send_small(gs, SMALL_LATE)

    results = [{}, {}, {}, {}]
    done = grad_x
    for i, names in enumerate(SCATTER_GROUPS):
        handle, slots = scatters[i]
        _, lands = _exchange_wait(handle, after=[done], name=f"scatter_wait_{i}")
        for n, land, slot in zip(names, lands, slots):
            res = _reduce_adamw(land, w_arg[n][0], m_arg[n][0], v_arg[n][0], name="adamw_" + n)
            done = res[0]
            for d, r in zip(results, res):
                d[n] = r[None]
    stacks = [_exchange_wait(handle, after=[done], name=f"small_wait_{i}")[1][0] for i, handle in enumerate(smalls)]
    res = _adamw_small(stacks, small_views, small_w, small_m, small_v, name="adamw_small")
    for d, r in zip(results, res):
        d.update({n: r[n].reshape(w_arg[n].shape) for n in SMALL})
    out = [loss, grad_x[None]]
    for d in results:
        out += [d[n] for n in WEIGHTS]
    return tuple(out)
```

```python
import functools

import numpy as np
import jax
import jax.numpy as jnp
from jax import lax
from jax.experimental import pallas as pl
from jax.experimental.pallas import tpu as pltpu

F32 = jnp.float32
MXU_DTYPE = jnp.bfloat16
WIRE_DTYPE = jnp.bfloat16
VMEM_LIMIT_BYTES = 48 * 1024 * 1024
LANES = 128

N_DEV = 8
D_MODEL = 1024
SSM_GROUP = 16
SSM_WIDTH = 768
SSM_GROUPS = SSM_WIDTH // SSM_GROUP
SSM_STATE = 64
SSM_CH = SSM_GROUPS * SSM_STATE
SSM_TILES = SSM_WIDTH // LANES
GROUPS_PER_TILE = LANES // SSM_GROUP
STATE_VREG_ROWS = SSM_CH // LANES
ATT_HEAD_DIM = 64
ATT_HEADS_PER_GROUP = 4
ATT_MERGED = ATT_HEADS_PER_GROUP * ATT_HEAD_DIM
LANE_HALVES = ATT_MERGED // LANES
DILATIONS = (1, 4, 16)
ATT_BLK = 128
ATT_SCALE = ATT_HEAD_DIM ** -0.5
ROT_DIM = ATT_HEAD_DIM // 4
ROPE_THETA = 500000.0
XATT_HEADS = 4
XATT_HEAD_DIM = D_MODEL // XATT_HEADS
XATT_SCALE = XATT_HEAD_DIM ** -0.5
DEEPNORM_ALPHA = 2.0 ** 0.25
LN_EPS = 1e-5
NEG_INF = -1e30
OFF_Q_BLK, OFF_K_BLK, OFF_V_BLK = 3, 6, 9
OFF_GS_BLK, OFF_GA_BLK = 3, 4

ADAM_LR = 0.001
ADAM_B1 = 0.9
ADAM_B2 = 0.999
ADAM_EPS = 1e-08
ADAM_WD = 0.01
ADAM_STEP = 10

BIG = ("w_in", "w_glu", "w_att_up", "w_mix_out", "w_xq", "w_xkv", "w_xo", "w_ff1", "w_ff2")
BIG_COL_SHARDED = ("w_in", "w_glu", "w_att_up", "w_xkv", "w_ff1")
WEIGHTS = ("ln_in_g", "ln_in_b", "w_in", "b_in", "ssm_log_dt", "ssm_a_re", "ssm_a_im", "ssm_b_re", "ssm_b_im",
           "ssm_c_re", "ssm_c_im", "ssm_d", "w_glu", "b_glu", "w_att_up", "w_mix_out", "b_mix_out", "ln1_g", "ln1_b",
           "w_xq", "w_xkv", "w_xo", "ln2_g", "ln2_b", "w_ff1", "b_ff1", "w_ff2", "b_ff2", "ln3_g", "ln3_b")
SMALL = tuple(n for n in WEIGHTS if n not in BIG)
PACK_COLS = 1024
PACK_ROW_ALIGN = 16


def _params(*sem):
    return pltpu.CompilerParams(dimension_semantics=sem, vmem_limit_bytes=VMEM_LIMIT_BYTES)


def _dot(a, b, ca, cb):
    return lax.dot_general(a.astype(MXU_DTYPE), b.astype(MXU_DTYPE), (((ca,), (cb,)), ((), ())),
                           preferred_element_type=F32)


def _fit(dim, pref):
    if dim <= pref:
        return dim
    best = max(t for t in range(LANES, pref + 1, LANES) if dim % t == 0)
    return best


def _mm(a, b, *, name, ta=False, tb=False, bias=None, out_dtype=F32, b_shards=False, out_shards=False, after=None,
        also=None, gate=None, colsum=False, epilogue=None, tm=2048, tn=1024, tk=1024):
    m, k = (a.shape[1], a.shape[0]) if ta else a.shape
    order = (lambda f: (lambda j, i, kk: f(i, j, kk))) if colsum else (lambda f: f)
    spec = lambda shape, f: pl.BlockSpec(shape, order(f))
    if b_shards:
        n_sh, rows, n_loc = b.shape
        if tb:
            n, tn, tk = rows, _fit(rows, tn), n_loc
            assert k == n_sh * n_loc, (name, k, b.shape)
            b_spec = spec((1, tn, tk), lambda i, j, kk: (kk, j, 0))
        else:
            n, tn, tk = n_sh * n_loc, n_loc, _fit(k, tk)
            b_spec = spec((1, tk, tn), lambda i, j, kk: (j, kk, 0))
    else:
        n = b.shape[0] if tb else b.shape[1]
        tn = n // N_DEV if out_shards else _fit(n, tn)
        tk = _fit(k, tk)
        b_spec = spec((tn, tk), lambda i, j, kk: (j, kk)) if tb else spec((tk, tn), lambda i, j, kk: (kk, j))
    tm = _fit(m, tm)
    nk = k // tk
    a_spec = spec((tk, tm), lambda i, j, kk: (kk, i)) if ta else spec((tm, tk), lambda i, j, kk: (i, kk))
    tile = spec((tm, tn), lambda i, j, kk: (i, j))
    in_specs, args = [a_spec, b_spec], [a, b]
    if bias is not None:
        in_specs.append(spec((1, tn), lambda i, j, kk: (0, j)))
        args.append(bias)
    if gate is not None:
        in_specs.append(tile)
        args.append(gate[0])
    if after is not None:
        in_specs.append(pl.BlockSpec(memory_space=pl.ANY))
        args.append(after)
    if epilogue is not None:
        ep_fn, ep_rows, ep_fulls, ep_row_outs, ep_acc_outs = epilogue
        assert tn == n and not (colsum or also or gate or out_shards), name
        ep_first = len(args)
        in_specs += [spec((tm, t.shape[1]), lambda i, j, kk: (i, 0)) for t in ep_rows]
        in_specs += [pl.BlockSpec(t.shape, functools.partial(lambda i, j, kk, nd: (0,) * nd, nd=t.ndim)) for t in ep_fulls]
        args += [*ep_rows, *ep_fulls]
    n_in = len(args)
    if epilogue is not None:
        out_specs = [spec((tm, w), lambda i, j, kk: (i, 0)) for w, _ in ep_row_outs]
        out_specs += [spec((1, w), lambda i, j, kk: (0, 0)) for w in ep_acc_outs]
        out_shape = [jax.ShapeDtypeStruct((m, w), dt) for w, dt in ep_row_outs]
        out_shape += [jax.ShapeDtypeStruct((1, w), F32) for w in ep_acc_outs]
    elif out_shards:
        assert n == N_DEV * tn, (name, n, tn)
        out_specs = [spec((1, tm, tn), lambda i, j, kk: (j, i, 0))]
        out_shape = [jax.ShapeDtypeStruct((N_DEV, m, tn), out_dtype)]
    else:
        out_specs = [tile]
        out_shape = [jax.ShapeDtypeStruct((m, n), out_dtype)]
    if also is not None:
        out_specs.append(tile)
        out_shape.append(jax.ShapeDtypeStruct((m, n), also[1]))
    if colsum:
        out_specs.append(spec((1, tn), lambda i, j, kk: (0, j)))
        out_shape.append(jax.ShapeDtypeStruct((1, n), F32))

    def body(*refs):
        a_ref, b_ref = refs[0], refs[1]
        o_ref = refs[n_in]
        first_row_tile = pl.program_id(1 if colsum else 0) == 0

        def product():
            return _dot(a_ref[...], b_ref[0] if b_shards else b_ref[...], 0 if ta else 1, 1 if tb else 0)

        def finish(r):
            if bias is not None:
                r = r + refs[2][...]
            if gate is not None:
                r = r * gate[1](refs[2 + (bias is not None)][...])
            if epilogue is not None:
                res = ep_fn(r, *[ref[...] for ref in refs[ep_first:n_in]])
                n_o = len(ep_row_outs)
                for ref, val in zip(refs[n_in:n_in + n_o], res[:n_o]):
                    ref[...] = val.astype(ref.dtype)
                acc_refs = refs[n_in + n_o:n_in + n_o + len(ep_acc_outs)]
                if acc_refs:
                    @pl.when(first_row_tile)
                    def _():
                        for ref in acc_refs:
                            ref[...] = jnp.zeros_like(ref)

                    for ref, val in zip(acc_refs, res[n_o:]):
                        ref[...] += val
                return
            if out_shards:
                o_ref[0] = r.astype(o_ref.dtype)
            else:
                o_ref[...] = r.astype(o_ref.dtype)
            if also is not None:
                refs[n_in + 1][...] = also[0](r).astype(also[1])
            if colsum:
                s_ref = refs[n_in + 1 + (also is not None)]

                @pl.when(first_row_tile)
                def _():
                    s_ref[...] = jnp.zeros_like(s_ref)

                s_ref[...] += _colsum(r)

        if nk == 1:
            finish(product())
            return
        acc_ref = refs[-1]
        kk = pl.program_id(2)

        @pl.when(kk == 0)
        def _():
            acc_ref[...] = jnp.zeros_like(acc_ref)

        acc_ref[...] += product()

        @pl.when(kk == nk - 1)
        def _():
            finish(acc_ref[...])

    grid = (n // tn, m // tm, nk) if colsum else (m // tm, n // tn, nk)
    res = pl.pallas_call(
        body, name=name, grid=grid, in_specs=in_specs, out_specs=out_specs, out_shape=out_shape,
        scratch_shapes=[pltpu.VMEM((tm, tn), F32)] if nk > 1 else [],
        compiler_params=_params("arbitrary" if epilogue is not None else "parallel",
                                "arbitrary" if colsum else "parallel", "arbitrary"),
    )(*args)
    return res[0] if len(res) == 1 else res


def _mm_shards(a, w, bias, shard_ids, *, name, prev=None, tm=2048):
    m, k = a.shape
    n_sh, _, n_loc = w.shape
    tm = _fit(m, tm)

    def body(ids_ref, a_ref, w_ref, b_ref, *rest):
        rest[-1][...] = _dot(a_ref[...], w_ref[0], 1, 0) + b_ref[...]

    grid_spec = pltpu.PrefetchScalarGridSpec(
        num_scalar_prefetch=1, grid=(m // tm, shard_ids.shape[0]),
        in_specs=[pl.BlockSpec((tm, k), lambda i, j, ids: (i, 0)),
                  pl.BlockSpec((1, k, n_loc), lambda i, j, ids: (ids[j], 0, 0)),
                  pl.BlockSpec((1, n_loc), lambda i, j, ids: (0, ids[j]))]
        + [pl.BlockSpec(memory_space=pl.ANY)] * (prev is not None),
        out_specs=pl.BlockSpec((tm, n_loc), lambda i, j, ids: (i, ids[j])))
    return pl.pallas_call(
        body, name=name, grid_spec=grid_spec, out_shape=jax.ShapeDtypeStruct((m, n_sh * n_loc), F32),
        input_output_aliases={4: 0} if prev is not None else {}, compiler_params=_params("parallel", "arbitrary"),
    )(shard_ids, a, w, bias, *([prev] if prev is not None else []))


def _rowcall(fn, rows, fulls, row_outs, acc_outs=(), *, n_rows, tm, name, after=None):
    n_r, n_f, n_o, n_a = len(rows), len(fulls), len(row_outs), len(acc_outs)
    n_in = n_r + n_f + (after is not None)
    assert n_rows % tm == 0, (name, n_rows, tm)

    def body(*refs):
        res = fn(*[r[...] for r in refs[:n_r + n_f]])
        res = tuple(res) if isinstance(res, (tuple, list)) else (res,)
        o_refs = refs[n_in:n_in + n_o]
        a_refs = refs[n_in + n_o:]
        for o_ref, val in zip(o_refs, res[:n_o]):
            o_ref[...] = val.astype(o_ref.dtype)
        if n_a:
            @pl.when(pl.program_id(0) == 0)
            def _():
                for a_ref in a_refs:
                    a_ref[...] = jnp.zeros_like(a_ref)

            for a_ref, val in zip(a_refs, res[n_o:]):
                a_ref[...] += val

    in_specs = [pl.BlockSpec((tm, w), functools.partial(lambda i, cb: (i, cb), cb=cb)) for _, w, cb in rows]
    in_specs += [pl.BlockSpec(f.shape, functools.partial(lambda i, nd: (0,) * nd, nd=f.ndim)) for f in fulls]
    in_specs += [pl.BlockSpec(memory_space=pl.ANY)] * (after is not None)
    out_specs = [pl.BlockSpec((tm, w), lambda i: (i, 0)) for w, _ in row_outs]
    out_specs += [pl.BlockSpec((1, w), lambda i: (0, 0)) for w in acc_outs]
    out_shape = [jax.ShapeDtypeStruct((n_rows, w), dt) for w, dt in row_outs]
    out_shape += [jax.ShapeDtypeStruct((1, w), F32) for w in acc_outs]
    return pl.pallas_call(
        body, name=name, grid=(n_rows // tm,), in_specs=in_specs, out_specs=out_specs, out_shape=out_shape,
        compiler_params=_params("arbitrary" if n_a else "parallel"),
    )(*[r[0] for r in rows], *fulls, *([after] if after is not None else []))


def _colsum(v):
    return jnp.sum(v, axis=0, keepdims=True)


def _layer_norm(xin, g, b):
    mu = jnp.mean(xin, axis=-1, keepdims=True)
    xc = xin - mu
    var = jnp.mean(xc * xc, axis=-1, keepdims=True)
    rstd = lax.rsqrt(var + LN_EPS)
    xh = xc * rstd
    return xh * g + b, xh, rstd


def _layer_norm_bwd(dy, xh, rstd, g):
    dyg = dy * g
    m1 = jnp.mean(dyg, axis=-1, keepdims=True)
    m2 = jnp.mean(dyg * xh, axis=-1, keepdims=True)
    dx = rstd * (dyg - m1 - xh * m2)
    return dx, _colsum(dy * xh), _colsum(dy), _colsum(dx)


def _ln_fwd(a, g, b, *, name):
    n_rows, d = a.shape

    def fn(av, gv, bv):
        y, xh, rstd = _layer_norm(av, gv, bv)
        return y, xh, rstd, y

    return _rowcall(fn, [(a, d, 0)], [g, b], [(d, F32), (d, F32), (1, F32), (d, MXU_DTYPE)], n_rows=n_rows, tm=256,
                    name=name)


def _ln_bwd(dya, dyb, xh, rstd, g, *, alpha, name, operand=True):
    n_rows, d = xh.shape

    def fn(da, db, xhv, rs, gv):
        dx, *sums = _layer_norm_bwd(alpha * da + db, xhv, rs, gv)
        return (dx,) + ((dx,) if operand else ()) + tuple(sums)

    rows = [(dya, d, 0), (dyb, d, 0), (xh, d, 0), (rstd, 1, 0)]
    return _rowcall(fn, rows, [g], [(d, F32)] + [(d, MXU_DTYPE)] * operand, [d, d, d], n_rows=n_rows, tm=256, name=name)


LN_EPILOGUE_ROWS = 1024


def _mm_ln_fwd(x, w, bias, a, g, b, *, alpha, name):
    d = a.shape[1]

    def fn(r, av, gv, bv):
        y, xh, rstd = _layer_norm(alpha * av + r, gv, bv)
        return y, xh, rstd, y

    return _mm(x, w, bias=bias, name=name, tm=LN_EPILOGUE_ROWS,
               epilogue=(fn, [a], [g, b], [(d, F32), (d, F32), (1, F32), (d, MXU_DTYPE)], []))


def _mm_ln_bwd(x, w, dya, xh, rstd, g, *, alpha, name):
    d = xh.shape[1]

    def fn(r, da, xhv, rs, gv):
        dx, *sums = _layer_norm_bwd(alpha * da + r, xhv, rs, gv)
        return (dx, dx, *sums)

    return _mm(x, w, tb=True, name=name, tm=LN_EPILOGUE_ROWS,
               epilogue=(fn, [dya, xh, rstd], [g], [(d, F32), (d, MXU_DTYPE)], [d, d, d]))


def _ln_loss_bwd(a, r, target, g, b, *, alpha, name):
    n_rows, d = a.shape

    def fn(av, rv, tv, gv, bv):
        y, xh, rs = _layer_norm(alpha * av + rv, gv, bv)
        diff = y - tv
        part = jnp.sum(jnp.sum(diff * diff, axis=1, keepdims=True), axis=0, keepdims=True) * (0.5 / d)
        dx, *sums = _layer_norm_bwd(diff * (1.0 / d), xh, rs, gv)
        return (dx, dx, *sums, jnp.broadcast_to(part, (1, LANES)))

    return _rowcall(fn, [(a, d, 0), (r, d, 0), (target, d, 0)], [g, b], [(d, F32), (d, MXU_DTYPE)], [d, d, d, LANES],
                    n_rows=n_rows, tm=256, name=name)


def _rope_lane_constants():
    lane = np.arange(ATT_MERGED)
    in_head = lane % ATT_HEAD_DIM
    sign = np.where(in_head < ROT_DIM // 2, -1.0, np.where(in_head < ROT_DIM, 1.0, 0.0)).astype(np.float32)
    inv_freq = ROPE_THETA ** (-jnp.arange(0, ROT_DIM, 2, dtype=F32) / ROT_DIM)
    return inv_freq[lane % (ROT_DIM // 2)].reshape(1, ATT_MERGED), jnp.asarray(sign).reshape(1, ATT_MERGED)


def _rope_tables(pos_col, *, name, after=None):
    inv_lane, sign = _rope_lane_constants()

    def fn(pos, inv, sg):
        ang = pos.astype(F32) * inv
        return jnp.where(sg != 0.0, jnp.cos(ang), 1.0), sg * jnp.sin(ang)

    return _rowcall(fn, [(pos_col, 1, 0)], [inv_lane, sign], [(ATT_MERGED, F32), (ATT_MERGED, F32)],
                    n_rows=pos_col.shape[0], tm=512, name=name, after=after)


def _rot_partner(t):
    lane = lax.broadcasted_iota(jnp.int32, t.shape, 1)
    width = t.shape[1]
    return jnp.where((lane & (ROT_DIM // 2)) == 0, pltpu.roll(t, width - ROT_DIM // 2, 1), pltpu.roll(t, ROT_DIM // 2, 1))


def _rope(t, cos_t, sin_t):
    return t * cos_t + _rot_partner(t) * sin_t


def _rope_transpose(dt, cos_t, sin_t):
    return dt * cos_t + _rot_partner(dt * sin_t)


def _strided_rows(r, count, stride):
    return pl.ds(r, count) if stride == 1 else pl.ds(r, count, stride=stride)


def _qkv_split(proj, cos_t, sin_t, *, name, tm=512):
    n_rows = proj.shape[0]
    n_g = len(DILATIONS)

    def body(*refs):
        n_src = LANE_HALVES * 3 * n_g
        src, tables, dst = refs[:n_src], refs[n_src:n_src + 2 * LANE_HALVES], refs[n_src + 2 * LANE_HALVES:]
        for kind in range(3):
            for g, dil in enumerate(DILATIONS):
                for half in range(LANE_HALVES):
                    x_ref, o_ref = src[(kind * n_g + g) * LANE_HALVES + half], dst[kind * n_g + g]
                    cos_ref, sin_ref = tables[half], tables[LANE_HALVES + half]
                    for r in range(dil):
                        rows = _strided_rows(r, tm // dil, dil)
                        t = x_ref[rows, :]
                        if kind < 2:
                            t = _rope(t, cos_ref[rows, :], sin_ref[rows, :])
                        lo = r * ATT_MERGED + half * LANES
                        o_ref[:, lo:lo + LANES] = t.astype(o_ref.dtype)

    half_spec = lambda cb: pl.BlockSpec((tm, LANES), functools.partial(lambda i, cb: (i, cb), cb=cb))
    in_specs = [half_spec((off + g) * LANE_HALVES + half)
                for off in (OFF_Q_BLK, OFF_K_BLK, OFF_V_BLK) for g in range(n_g) for half in range(LANE_HALVES)]
    in_specs += [half_spec(half) for _ in range(2) for half in range(LANE_HALVES)]
    out_specs = [pl.BlockSpec((tm // dil, dil * ATT_MERGED), lambda i: (i, 0)) for _ in range(3) for dil in DILATIONS]
    out_shape = [jax.ShapeDtypeStruct((n_rows // dil, dil * ATT_MERGED), MXU_DTYPE) for _ in range(3) for dil in DILATIONS]
    outs = pl.pallas_call(
        body, name=name, grid=(n_rows // tm,), in_specs=in_specs, out_specs=out_specs, out_shape=out_shape,
        compiler_params=_params("parallel"),
    )(*[proj] * (LANE_HALVES * 3 * n_g), *[cos_t] * LANE_HALVES, *[sin_t] * LANE_HALVES)
    return outs[:n_g], outs[n_g:2 * n_g], outs[2 * n_g:]


def _mix(gs, ga, z1, z2, b_att):
    return jax.nn.sigmoid(gs) * (z1 * jax.nn.sigmoid(z2)) + jax.nn.sigmoid(ga) * b_att


def _mix_rows(proj, z, b_att):
    return [(proj, D_MODEL, OFF_GS_BLK), (proj, D_MODEL, OFF_GA_BLK), (z, D_MODEL, 0), (z, D_MODEL, 1), (b_att, D_MODEL, 0)]


def _mix_out_ln(proj, z, b_att, w, bias, a, g, b, *, alpha, name):
    def fn(gs, ga, z1, z2, ba, av, wv, biasv, gv, bv):
        mixed = _mix(gs, ga, z1, z2, ba)
        y, xh, rstd = _layer_norm(alpha * av + (_dot(mixed, wv, 1, 0) + biasv), gv, bv)
        return mixed, y, xh, rstd, y

    rows = _mix_rows(proj, z, b_att) + [(a, D_MODEL, 0)]
    outs = [(D_MODEL, MXU_DTYPE), (D_MODEL, F32), (D_MODEL, F32), (1, F32), (D_MODEL, MXU_DTYPE)]
    return _rowcall(fn, rows, [w, bias, g, b], outs, n_rows=proj.shape[0], tm=256, name=name)


def _mix_bwd(dmixed, proj, z, b_att, *, name):
    def fn(dm, gs, ga, z1, z2, ba):
        _, vjp = jax.vjp(_mix, gs, ga, z1, z2, ba)
        dgs, dga, dz1, dz2, dba = vjp(dm)
        dz = jnp.concatenate([dz1, dz2], axis=1)
        return dgs, dga, dz, dba, _colsum(dgs), _colsum(dga), _colsum(dz)

    rows = [(dmixed, D_MODEL, 0)] + _mix_rows(proj, z, b_att)
    widths = [D_MODEL, D_MODEL, 2 * D_MODEL, D_MODEL]
    return _rowcall(fn, rows, [], [(w, MXU_DTYPE) for w in widths], widths[:3], n_rows=proj.shape[0], tm=256, name=name)


def _gelu_bwd(dgy, y, proj, *, name):
    def fn(dg, yv, u):
        _, vjp = jax.vjp(jax.nn.gelu, yv)
        dy = vjp(dg)[0]
        return dy, _colsum(dy * u)

    return _rowcall(fn, [(dgy, SSM_WIDTH, 0), (y, SSM_WIDTH, 0), (proj, SSM_WIDTH, 0)], [], [(SSM_WIDTH, F32)],
                    [SSM_WIDTH], n_rows=y.shape[0], tm=512, name=name)


HEAD_ROWS = ATT_HEADS_PER_GROUP * ATT_BLK


def _head_masks(rows):
    head = lax.broadcasted_iota(jnp.int32, (rows, ATT_MERGED), 1) >> (ATT_HEAD_DIM.bit_length() - 1)
    return [head == h for h in range(ATT_HEADS_PER_GROUP)]


def _stack_heads(t, masks):
    return jnp.concatenate([jnp.where(m, t, jnp.zeros_like(t)) for m in masks], axis=0)


def _unstack_heads(t4, masks):
    blocks = [t4[h * ATT_BLK:(h + 1) * ATT_BLK] for h in range(ATT_HEADS_PER_GROUP)]
    return jnp.where(masks[0], blocks[0], jnp.where(masks[1], blocks[1], jnp.where(masks[2], blocks[2], blocks[3])))


def _head_column(stats, first):
    return jnp.concatenate([stats[:, first + h:first + h + 1] for h in range(ATT_HEADS_PER_GROUP)], axis=0)


def _band_mask(first_key):
    qi = lax.broadcasted_iota(jnp.int32, (HEAD_ROWS, 2 * ATT_BLK), 0) & (ATT_BLK - 1)
    ki = lax.broadcasted_iota(jnp.int32, (HEAD_ROWS, 2 * ATT_BLK), 1)
    steps = qi + ATT_BLK - ki
    return (steps >= 0) & (steps <= ATT_BLK) & (ki >= first_key)


def _dil_fwd(q, k, v, dil, *, name):
    n_blk = q.shape[0] // ATT_BLK
    cur = pl.BlockSpec((ATT_BLK, ATT_MERGED), lambda r, n: (n, r))
    prev = pl.BlockSpec((ATT_BLK, ATT_MERGED), lambda r, n: (jnp.maximum(n - 1, 0), r))

    def body(q_ref, kp_ref, kc_ref, vp_ref, vc_ref, o_ref, l_ref):
        masks = _head_masks(ATT_BLK)
        valid = _band_mask(jnp.where(pl.program_id(1) > 0, 0, ATT_BLK))
        keys = jnp.concatenate([kp_ref[...], kc_ref[...]], axis=0)
        vals = jnp.concatenate([vp_ref[...], vc_ref[...]], axis=0)
        s = jnp.where(valid, _dot(_stack_heads(q_ref[...], masks), keys, 1, 1) * ATT_SCALE, NEG_INF)
        m = jnp.max(s, axis=-1, keepdims=True)
        p = jnp.exp(s - m)
        den = jnp.sum(p, axis=-1, keepdims=True)
        o_ref[...] = _unstack_heads(_dot(p, vals, 1, 0) / den, masks)
        l_ref[...] = _unstack_heads(jnp.broadcast_to(m + jnp.log(den), (HEAD_ROWS, ATT_MERGED)), masks)

    shape = jax.ShapeDtypeStruct(q.shape, F32)
    return pl.pallas_call(
        body, name=name, grid=(dil, n_blk), in_specs=[cur, prev, cur, prev, cur], out_specs=[cur, cur],
        out_shape=[shape, shape], compiler_params=_params("parallel", "parallel"),
    )(q, k, k, v, v)


def _att_merge(outs, lses, *, name, tm=512):
    n_g = len(outs)
    n_rows = outs[0].shape[0] * DILATIONS[0]

    def body(*refs):
        src, (att_ref, lse_ref), tmp = refs[:2 * n_g], refs[2 * n_g:2 * n_g + 2], refs[2 * n_g + 2:]
        vals = []
        for idx, src_ref in enumerate(src):
            dil = DILATIONS[idx % n_g]
            if dil == 1:
                vals.append(src_ref[...])
                continue
            for r in range(dil):
                for half in range(LANE_HALVES):
                    lo = r * ATT_MERGED + half * LANES
                    tmp[LANE_HALVES * idx + half][_strided_rows(r, tm // dil, dil), :] = src_ref[:, lo:lo + LANES]
            vals.append(jnp.concatenate([tmp[LANE_HALVES * idx + half][...] for half in range(LANE_HALVES)], axis=1))
        o, l = vals[:n_g], vals[n_g:]
        m = functools.reduce(jnp.maximum, l)
        e = [jnp.exp(li - m) for li in l]
        z = functools.reduce(jnp.add, e)
        att_ref[...] = functools.reduce(jnp.add, [(ei / z) * oi for ei, oi in zip(e, o)])
        lse_ref[...] = m + jnp.log(z)

    in_specs = [pl.BlockSpec((tm // dil, dil * ATT_MERGED), lambda i: (i, 0)) for _ in range(2) for dil in DILATIONS]
    row = pl.BlockSpec((tm, ATT_MERGED), lambda i: (i, 0))
    shape = jax.ShapeDtypeStruct((n_rows, ATT_MERGED), F32)
    return pl.pallas_call(
        body, name=name, grid=(n_rows // tm,), in_specs=in_specs, out_specs=[row, row], out_shape=[shape, shape],
        scratch_shapes=[pltpu.VMEM((tm, LANES), F32)] * (LANE_HALVES * 2 * n_g), compiler_params=_params("parallel"),
    )(*outs, *lses)


def _att_stats(datt, att, lse, *, name):
    n_rows = datt.shape[0]

    def fn(d, a, l):
        prod = d * a
        lane = lax.broadcasted_iota(jnp.int32, (d.shape[0], LANES), 1)
        out = jnp.zeros((d.shape[0], LANES), F32)
        for h in range(ATT_HEADS_PER_GROUP):
            lo = h * ATT_HEAD_DIM
            out = jnp.where(lane == h, l[:, lo:lo + 1], out)
            delta = jnp.sum(prod[:, lo:lo + ATT_HEAD_DIM], axis=-1, keepdims=True)
            out = jnp.where(lane == ATT_HEADS_PER_GROUP + h, delta, out)
        return out

    rows = [(t, ATT_MERGED, 0) for t in (datt, att, lse)]
    return _rowcall(fn, rows, [], [(LANES, F32)], n_rows=n_rows, tm=512, name=name)[0]


def _dil_bwd(q, k, v, datt, stats, dil, *, name):
    n_rows = datt.shape[0]
    n_blk = n_rows // dil // ATT_BLK
    span = ATT_BLK * dil
    cur = pl.BlockSpec((ATT_BLK, ATT_MERGED), lambda n, r: (n, r))
    prev = pl.BlockSpec((ATT_BLK, ATT_MERGED), lambda n, r: (jnp.maximum(n - 1, 0), r))
    nxt = pl.BlockSpec((ATT_BLK, ATT_MERGED), lambda n, r: (jnp.minimum(n + 1, n_blk - 1), r))
    seq = lambda half, ahead: pl.BlockSpec((span, LANES), lambda n, r: (jnp.minimum(n + ahead, n_blk - 1), half))

    def body(qc_ref, qn_ref, kp_ref, kc_ref, vp_ref, vc_ref, dc0_ref, dc1_ref, dn0_ref, dn1_ref, sc_ref, sn_ref,
             dq0_ref, dq1_ref, dk0_ref, dk1_ref, dv0_ref, dv1_ref):
        n = pl.program_id(0)
        rows = slice(None) if dil == 1 else _strided_rows(pl.program_id(1), ATT_BLK, dil)

        def read(ref0, ref1):
            return jnp.concatenate([ref0[rows, :], ref1[rows, :]], axis=1)

        def write(ref0, ref1, val):
            ref0[rows, :] = val[:, :LANES]
            ref1[rows, :] = val[:, LANES:]

        masks = _head_masks(ATT_BLK)
        valid = _band_mask(jnp.where(n > 0, 0, ATT_BLK))
        qi = lax.broadcasted_iota(jnp.int32, (HEAD_ROWS, ATT_BLK), 0) & (ATT_BLK - 1)
        ki = lax.broadcasted_iota(jnp.int32, (HEAD_ROWS, ATT_BLK), 1)
        valid_next = (ki - qi) >= jnp.where(n < n_blk - 1, 0, ATT_BLK)

        kc, vc = kc_ref[...], vc_ref[...]
        keys = jnp.concatenate([kp_ref[...], kc], axis=0)
        vals = jnp.concatenate([vp_ref[...], vc], axis=0)
        q4 = _stack_heads(qc_ref[...], masks)
        d4 = _stack_heads(read(dc0_ref, dc1_ref).astype(MXU_DTYPE), masks)
        st = sc_ref[rows, :]
        p = jnp.where(valid, jnp.exp(_dot(q4, keys, 1, 1) * ATT_SCALE - _head_column(st, 0)), 0.0)
        ds = p * (_dot(d4, vals, 1, 1) - _head_column(st, ATT_HEADS_PER_GROUP)) * ATT_SCALE
        write(dq0_ref, dq1_ref, _unstack_heads(_dot(ds, keys, 1, 0), masks))

        q4n = _stack_heads(qn_ref[...], masks)
        d4n = _stack_heads(read(dn0_ref, dn1_ref).astype(MXU_DTYPE), masks)
        stn = sn_ref[rows, :]
        p_n = jnp.where(valid_next, jnp.exp(_dot(q4n, kc, 1, 1) * ATT_SCALE - _head_column(stn, 0)), 0.0)
        ds_n = p_n * (_dot(d4n, vc, 1, 1) - _head_column(stn, ATT_HEADS_PER_GROUP)) * ATT_SCALE
        write(dv0_ref, dv1_ref, _dot(p[:, ATT_BLK:], d4, 0, 0) + _dot(p_n, d4n, 0, 0))
        write(dk0_ref, dk1_ref, _dot(ds[:, ATT_BLK:], q4, 0, 0) + _dot(ds_n, q4n, 0, 0))

    shape = jax.ShapeDtypeStruct((n_rows, LANES), F32)
    out = seq(0, 0)
    res = pl.pallas_call(
        body, name=name, grid=(n_blk, dil),
        in_specs=[cur, nxt, prev, cur, prev, cur, seq(0, 0), seq(1, 0), seq(0, 1), seq(1, 1), seq(0, 0), seq(0, 1)],
        out_specs=[out] * 6, out_shape=[shape] * 6, compiler_params=_params("parallel", "arbitrary"),
    )(q, q, k, k, v, v, datt, datt, datt, datt, stats, stats)
    return [(res[2 * i], res[2 * i + 1]) for i in range(3)]


def _dproj_assemble(du, dqkv, dgs, dga, cos_t, sin_t, *, name):
    n_g = len(DILATIONS)

    def fn(*t):
        n_half = LANE_HALVES * 3 * n_g
        du_t, halves, (dgs_t, dga_t, c, s) = t[0], t[1:1 + n_half], t[1 + n_half:]
        parts = [jnp.concatenate(halves[LANE_HALVES * i:LANE_HALVES * (i + 1)], axis=1) for i in range(3 * n_g)]
        for i in range(2 * n_g):
            parts[i] = _rope_transpose(parts[i], c, s)
        cast = [p.astype(MXU_DTYPE) for p in parts]
        return [jnp.concatenate([du_t] + cast + [dgs_t, dga_t], axis=1)] + [_colsum(p) for p in parts]

    rows = [(du, SSM_WIDTH, 0)]
    rows += [(half, LANES, 0) for i in range(3) for g in range(n_g) for half in dqkv[g][i]]
    rows += [(dgs, D_MODEL, 0), (dga, D_MODEL, 0), (cos_t, ATT_MERGED, 0), (sin_t, ATT_MERGED, 0)]
    width = SSM_WIDTH + 3 * n_g * ATT_MERGED + 2 * D_MODEL
    res = _rowcall(fn, rows, [], [(width, MXU_DTYPE)], [ATT_MERGED] * (3 * n_g), n_rows=du.shape[0], tm=256, name=name)
    return res[0], res[1:]


def _xhead(h):
    return slice(h * XATT_HEAD_DIM, (h + 1) * XATT_HEAD_DIM)


def _xatt_probs(qh, kh):
    s = _dot(qh, kh, 1, 1) * XATT_SCALE
    e = jnp.exp(s - jnp.max(s, axis=-1, keepdims=True))
    return e / jnp.sum(e, axis=-1, keepdims=True)


def _xatt_fwd(q, kv, *, name, tm=512):
    n_rows = q.shape[0]
    n_mem = kv.shape[0]

    def body(q_ref, kv_ref, o_ref):
        for h in range(XATT_HEADS):
            sl = _xhead(h)
            p = _xatt_probs(q_ref[:, sl], kv_ref[:, sl])
            o_ref[:, sl] = _dot(p, kv_ref[:, D_MODEL + h * XATT_HEAD_DIM:D_MODEL + (h + 1) * XATT_HEAD_DIM], 1, 0
                                ).astype(o_ref.dtype)

    row = pl.BlockSpec((tm, D_MODEL), lambda i: (i, 0))
    return pl.pallas_call(
        body, name=name, grid=(n_rows // tm,),
        in_specs=[row, pl.BlockSpec((n_mem, 2 * D_MODEL), lambda i: (0, 0))], out_specs=row,
        out_shape=jax.ShapeDtypeStruct((n_rows, D_MODEL), MXU_DTYPE), compiler_params=_params("parallel"),
    )(q, kv)


def _xatt_bwd(q, kv, do, *, name, tm=512):
    n_rows = q.shape[0]
    n_mem = kv.shape[0]

    def body(q_ref, kv_ref, do_ref, dq_ref, dkv_ref):
        @pl.when(pl.program_id(0) == 0)
        def _():
            dkv_ref[...] = jnp.zeros_like(dkv_ref)

        for h in range(XATT_HEADS):
            sl = _xhead(h)
            vsl = slice(D_MODEL + h * XATT_HEAD_DIM, D_MODEL + (h + 1) * XATT_HEAD_DIM)
            qh, kh, doh = q_ref[:, sl], kv_ref[:, sl], do_ref[:, sl]
            p = _xatt_probs(qh, kh)
            dp = _dot(doh, kv_ref[:, vsl], 1, 1)
            ds = p * (dp - jnp.sum(dp * p, axis=-1, keepdims=True)) * XATT_SCALE
            dq_ref[:, sl] = _dot(ds, kh, 1, 0).astype(dq_ref.dtype)
            dkv_ref[:, sl] += _dot(ds, qh, 0, 0)
            dkv_ref[:, vsl] += _dot(p, doh, 0, 0)

    row = pl.BlockSpec((tm, D_MODEL), lambda i: (i, 0))
    full = pl.BlockSpec((n_mem, 2 * D_MODEL), lambda i: (0, 0))
    return pl.pallas_call(
        body, name=name, grid=(n_rows // tm,), in_specs=[row, full, row], out_specs=[row, full],
        out_shape=[jax.ShapeDtypeStruct((n_rows, D_MODEL), MXU_DTYPE), jax.ShapeDtypeStruct((n_mem, 2 * D_MODEL), F32)],
        compiler_params=_params("arbitrary"),
    )(q, kv, do)


def _disc(logdt, a_re, a_im, b_re, b_im):
    dt = jnp.exp(logdt)
    mag = jnp.exp(a_re * dt)
    ab_re = mag * jnp.cos(a_im * dt)
    ab_im = mag * jnp.sin(a_im * dt)
    den = jnp.square(a_re) + jnp.square(a_im)
    nr = ab_re - 1.0
    f_re = (nr * a_re + ab_im * a_im) / den
    f_im = (ab_im * a_re - nr * a_im) / den
    bb_re = f_re[None] * b_re - f_im[None] * b_im
    bb_im = f_re[None] * b_im + f_im[None] * b_re
    return ab_re, ab_im, bb_re, bb_im


def _disc_transpose(logdt, a_re, a_im, b_re, b_im, g_ab_re, g_ab_im, g_bb_re, g_bb_im):
    dt = jnp.exp(logdt)
    mag = jnp.exp(a_re * dt)
    th = a_im * dt
    cs, sn = jnp.cos(th), jnp.sin(th)
    ab_re, ab_im = mag * cs, mag * sn
    den = jnp.square(a_re) + jnp.square(a_im)
    nr = ab_re - 1.0
    f_re = (nr * a_re + ab_im * a_im) / den
    f_im = (ab_im * a_re - nr * a_im) / den
    d_f_re = jnp.sum(g_bb_re * b_re + g_bb_im * b_im, axis=0)
    d_f_im = jnp.sum(g_bb_im * b_re - g_bb_re * b_im, axis=0)
    d_b_re = g_bb_re * f_re[None] + g_bb_im * f_im[None]
    d_b_im = g_bb_im * f_re[None] - g_bb_re * f_im[None]
    d_n_re, d_n_im = d_f_re / den, d_f_im / den
    d_den = -(d_f_re * f_re + d_f_im * f_im) / den
    d_ab_re = g_ab_re + d_n_re * a_re - d_n_im * a_im
    d_ab_im = g_ab_im + d_n_re * a_im + d_n_im * a_re
    d_a_re = d_n_re * nr + d_n_im * ab_im + 2.0 * d_den * a_re
    d_a_im = d_n_re * ab_im - d_n_im * nr + 2.0 * d_den * a_im
    d_mag = d_ab_re * cs + d_ab_im * sn
    d_th = mag * (d_ab_im * cs - d_ab_re * sn)
    d_a_re = d_a_re + d_mag * mag * dt
    d_a_im = d_a_im + d_th * dt
    d_dt = jnp.sum(d_mag * mag * a_re + d_th * a_im, axis=-1, keepdims=True)
    return d_dt * dt, d_a_re, d_a_im, d_b_re, d_b_im


def _full_spec(shape):
    return pl.BlockSpec(tuple(shape), functools.partial(lambda i, nd: (0,) * nd, nd=len(shape)))


def _whole(fn, args, out_shapes, *, name):
    n_in = len(args)

    def body(*refs):
        res = fn(*[r[...] for r in refs[:n_in]])
        for o_ref, val in zip(refs[n_in:], res):
            o_ref[...] = val

    return pl.pallas_call(
        body, name=name, grid=(1,), in_specs=[_full_spec(t.shape) for t in args],
        out_specs=[_full_spec(s) for s in out_shapes], out_shape=[jax.ShapeDtypeStruct(s, F32) for s in out_shapes],
        compiler_params=_params("arbitrary"))(*args)


SSM_WIDE = GROUPS_PER_TILE * SSM_STATE
LANE_GROUPS_PER_TILE = SSM_WIDE // LANES


def _chan(j):
    return slice(j * LANES, (j + 1) * LANES)


def _time_major_rows(j, q, tc):
    return pl.ds(j * LANE_GROUPS_PER_TILE + q, tc, stride=STATE_VREG_ROWS)


def _to_time_major(x, t_re_ref, t_im_ref, dst_re, dst_im, tc):
    for j in range(SSM_TILES):
        xj = x[:, _chan(j)]
        for t_ref, dst in ((t_re_ref, dst_re), (t_im_ref, dst_im)):
            r = _dot(xj, t_ref[j], 1, 0)
            for q in range(LANE_GROUPS_PER_TILE):
                dst[_time_major_rows(j, q, tc), :] = r[:, q * LANES:(q + 1) * LANES]


def _from_time_major(src, j, tc):
    return jnp.concatenate([src[_time_major_rows(j, q, tc), :] for q in range(LANE_GROUPS_PER_TILE)], axis=1)


def _scan_chunk(w_re, w_im, h_re, h_im, a_re, a_im, start, tc):
    def step(t, carry):
        hr, hi = carry
        rows = _scan_rows(t)
        nr = a_re * hr - a_im * hi + w_re[rows, :]
        ni = a_re * hi + a_im * hr + w_im[rows, :]
        h_re[rows, :] = nr
        h_im[rows, :] = ni
        return nr, ni

    return lax.fori_loop(0, tc, step, start, unroll=8)


SSM_CHUNK = 256


def _tile_spec(stack, k):
    return pl.BlockSpec((pl.Squeezed(),) + tuple(stack.shape[1:]), lambda i: (k, 0, 0, 0))


def _expand_block_diagonal(src_ref, dst):
    dst[...] = jnp.zeros_like(dst)
    r, c = src_ref.shape[1:]
    for g in range(SSM_GROUPS):
        j, gl = divmod(g, GROUPS_PER_TILE)
        dst[j, gl * r:(gl + 1) * r, gl * c:(gl + 1) * c] = src_ref[g].astype(dst.dtype)


def _extract_block_diagonal(src, dst_ref):
    r, c = dst_ref.shape[1:]
    for g in range(SSM_GROUPS):
        j, gl = divmod(g, GROUPS_PER_TILE)
        dst_ref[g] = src[j, gl * r:(gl + 1) * r, gl * c:(gl + 1) * c]


def _ssm_fwd(proj, blocks_cn, blocks_nc, a_re, a_im, gain, *, name, tc=SSM_CHUNK):
    n_rows = proj.shape[0]
    n_chunk = n_rows // tc

    def body(u_ref, br_ref, bi_ref, cr_ref, ci_ref, ar_ref, ai_ref, g_ref, y_ref, gy_ref, hr, hi, wr, wi, state,
             tbr_ref, tbi_ref, tcr_ref, tci_ref):
        @pl.when(pl.program_id(0) == 0)
        def _():
            state[...] = jnp.zeros_like(state)
            for src_ref, dst in ((br_ref, tbr_ref), (bi_ref, tbi_ref), (cr_ref, tcr_ref), (ci_ref, tci_ref)):
                _expand_block_diagonal(src_ref, dst)

        u = u_ref[...]
        _to_time_major(u, tbr_ref, tbi_ref, wr, wi, tc)
        state[0], state[1] = _scan_chunk(wr, wi, hr, hi, ar_ref[...], ai_ref[...], (state[0], state[1]), tc)
        for j in range(SSM_TILES):
            yj = (_dot(_from_time_major(hr, j, tc), tcr_ref[j], 1, 0) + _dot(_from_time_major(hi, j, tc), tci_ref[j], 1, 0)
                  + g_ref[:, _chan(j)] * u[:, _chan(j)])
            y_ref[:, _chan(j)] = yj
            gy_ref[:, _chan(j)] = jax.nn.gelu(yj).astype(gy_ref.dtype)

    rows = pl.BlockSpec((tc, SSM_WIDTH), lambda i: (i, 0))
    coef = pl.BlockSpec((STATE_VREG_ROWS, LANES), lambda i: (0, 0))
    states = pl.BlockSpec((tc * STATE_VREG_ROWS, LANES), lambda i: (i, 0))
    sshape = jax.ShapeDtypeStruct((n_rows * STATE_VREG_ROWS, LANES), F32)
    return pl.pallas_call(
        body, name=name, grid=(n_chunk,),
        in_specs=[rows, _tile_spec(blocks_cn, 0), _tile_spec(blocks_cn, 1), _tile_spec(blocks_nc, 0),
                  _tile_spec(blocks_nc, 1), coef, coef, pl.BlockSpec((1, SSM_WIDTH), lambda i: (0, 0))],
        out_specs=[rows, rows, states, states],
        out_shape=[jax.ShapeDtypeStruct((n_rows, SSM_WIDTH), F32), jax.ShapeDtypeStruct((n_rows, SSM_WIDTH), MXU_DTYPE),
                   sshape, sshape],
        scratch_shapes=[pltpu.VMEM((tc * STATE_VREG_ROWS, LANES), F32)] * 2 + [pltpu.VMEM((2, STATE_VREG_ROWS, LANES), F32)]
        + [pltpu.VMEM((SSM_TILES, LANES, SSM_WIDE), MXU_DTYPE)] * 2 + [pltpu.VMEM((SSM_TILES, SSM_WIDE, LANES), MXU_DTYPE)] * 2,
        compiler_params=_params("arbitrary"),
    )(proj, blocks_cn, blocks_cn, blocks_nc, blocks_nc, a_re, a_im, gain)


def _ssm_bwd(proj, dy, h_re, h_im, blocks_cn, blocks_nc, a_re, a_im, gain, *, name, tc=SSM_CHUNK):
    n_rows = proj.shape[0]
    n_chunk = n_rows // tc

    def body(u_ref, dy_ref, hr, hi, cr_ref, ci_ref, br_ref, bi_ref, ar_ref, ai_ref, g_ref,
             du_ref, su_ref, dc_re_ref, dc_im_ref, db_re_ref, db_im_ref, dar_ref, dai_ref, wr, wi, carry,
             tdr_ref, tdi_ref, tur_ref, tui_ref, dcr_ref, dci_ref, dbr_ref, dbi_ref):
        @pl.when(pl.program_id(0) == 0)
        def _():
            carry[...] = jnp.zeros_like(carry)
            for acc_ref in (su_ref, dcr_ref, dci_ref, dbr_ref, dbi_ref):
                acc_ref[...] = jnp.zeros_like(acc_ref)
            for src_ref, dst in ((cr_ref, tdr_ref), (ci_ref, tdi_ref), (br_ref, tur_ref), (bi_ref, tui_ref)):
                _expand_block_diagonal(src_ref, dst)

        a_r, a_i = ar_ref[...], ai_ref[...]
        u, dyv = u_ref[...], dy_ref[...]
        _to_time_major(dyv, tdr_ref, tdi_ref, wr, wi, tc)

        def step(kk, c):
            lam_r, lam_i, dar, dai = c
            rows = _scan_rows(tc - 1 - kk)
            h_r, h_i = hr[rows, :], hi[rows, :]
            dar = dar + lam_r * h_r + lam_i * h_i
            dai = dai + lam_i * h_r - lam_r * h_i
            new_r = wr[rows, :] + a_r * lam_r + a_i * lam_i
            new_i = wi[rows, :] + a_r * lam_i - a_i * lam_r
            wr[rows, :] = new_r
            wi[rows, :] = new_i
            return new_r, new_i, dar, dai

        carry[0], carry[1], carry[2], carry[3] = lax.fori_loop(0, tc, step, (carry[0], carry[1], carry[2], carry[3]),
                                                              unroll=8)
        dar_ref[...] = carry[2]
        dai_ref[...] = carry[3]
        for j in range(SSM_TILES):
            cj = _chan(j)
            lam_r, lam_i = _from_time_major(wr, j, tc), _from_time_major(wi, j, tc)
            dcr_ref[j] += _dot(dyv[:, cj], _from_time_major(hr, j, tc), 0, 0)
            dci_ref[j] += _dot(dyv[:, cj], _from_time_major(hi, j, tc), 0, 0)
            dbr_ref[j] += _dot(u[:, cj], lam_r, 0, 0)
            dbi_ref[j] += _dot(u[:, cj], lam_i, 0, 0)
            duj = _dot(lam_r, tur_ref[j], 1, 0) + _dot(lam_i, tui_ref[j], 1, 0) + g_ref[:, cj] * dyv[:, cj]
            du_ref[:, cj] = duj.astype(du_ref.dtype)
            su_ref[:, cj] += _colsum(duj)

        @pl.when(pl.program_id(0) == n_chunk - 1)
        def _():
            for src, dst_ref in ((dcr_ref, dc_re_ref), (dci_ref, dc_im_ref), (dbr_ref, db_re_ref), (dbi_ref, db_im_ref)):
                _extract_block_diagonal(src, dst_ref)

    back = lambda i: (n_chunk - 1 - i, 0)
    rows = pl.BlockSpec((tc, SSM_WIDTH), back)
    blocks = pl.BlockSpec((SSM_GROUPS, SSM_GROUP, SSM_STATE), lambda i: (0, 0, 0))
    coef = pl.BlockSpec((STATE_VREG_ROWS, LANES), lambda i: (0, 0))
    states = pl.BlockSpec((tc * STATE_VREG_ROWS, LANES), back)
    vec = pl.BlockSpec((1, SSM_WIDTH), lambda i: (0, 0))
    bshape = jax.ShapeDtypeStruct((SSM_GROUPS, SSM_GROUP, SSM_STATE), F32)
    cshape = jax.ShapeDtypeStruct((STATE_VREG_ROWS, LANES), F32)
    return pl.pallas_call(
        body, name=name, grid=(n_chunk,),
        in_specs=[rows, rows, states, states, _tile_spec(blocks_cn, 2), _tile_spec(blocks_cn, 3), _tile_spec(blocks_nc, 2),
                  _tile_spec(blocks_nc, 3), coef, coef, vec],
        out_specs=[rows, vec, blocks, blocks, blocks, blocks, coef, coef],
        out_shape=[jax.ShapeDtypeStruct((n_rows, SSM_WIDTH), MXU_DTYPE), jax.ShapeDtypeStruct((1, SSM_WIDTH), F32),
                   bshape, bshape, bshape, bshape, cshape, cshape],
        scratch_shapes=[pltpu.VMEM((tc * STATE_VREG_ROWS, LANES), F32)] * 2 + [pltpu.VMEM((4, STATE_VREG_ROWS, LANES), F32)]
        + [pltpu.VMEM((SSM_TILES, LANES, SSM_WIDE), MXU_DTYPE)] * 2 + [pltpu.VMEM((SSM_TILES, SSM_WIDE, LANES), MXU_DTYPE)] * 2
        + [pltpu.VMEM((SSM_TILES, LANES, SSM_WIDE), F32)] * 4,
        compiler_params=_params("arbitrary"),
    )(proj, dy, h_re, h_im, blocks_cn, blocks_cn, blocks_nc, blocks_nc, a_re, a_im, gain)


def _scan_rows(t):
    return pl.ds(pl.multiple_of(t * STATE_VREG_ROWS, 8), STATE_VREG_ROWS)


GATHER_GROUPS = (("w_glu", "w_att_up", "w_mix_out"), ("w_xq", "w_xkv", "w_xo", "w_ff1", "w_ff2"))
SCATTER_GROUPS = (("w_ff2", "w_ff1"), ("w_xo", "w_xq", "w_xkv", "w_mix_out"), ("w_att_up", "w_glu"), ("w_in",))


def _local_grads(x, mem, pos_col, target, sm, fetch_in, fetch, send, send_small, start_token):
    b_re_t = sm["ssm_b_re"].transpose(2, 0, 1)
    b_im_t = sm["ssm_b_im"].transpose(2, 0, 1)
    logdt = sm["ssm_log_dt"].reshape(SSM_GROUPS, 1)
    c_re, c_im = sm["ssm_c_re"], sm["ssm_c_im"]
    grp = (SSM_GROUPS, SSM_STATE)
    chn = (SSM_GROUP, SSM_GROUPS, SSM_STATE)

    wts = {}
    cos_t, sin_t = _rope_tables(pos_col, after=start_token, name="rope_tables")
    h0, xh0, rs0, h0m = _ln_fwd(x, sm["ln_in_g"], sm["ln_in_b"], name="ln_in_fwd")
    disc_in = (logdt, sm["ssm_a_re"], sm["ssm_a_im"], b_re_t, b_im_t)
    ab_re, ab_im, bb_re_t, bb_im_t = _whole(_disc, disc_in, [grp, grp, chn, chn], name="ssm_disc")
    a_re_rows, a_im_rows = ab_re.reshape(STATE_VREG_ROWS, LANES), ab_im.reshape(STATE_VREG_ROWS, LANES)
    tiles_cn = jnp.stack([bb_re_t.transpose(1, 0, 2), bb_im_t.transpose(1, 0, 2), c_re, -c_im])
    tiles_nc = jnp.stack([c_re.transpose(0, 2, 1), -c_im.transpose(0, 2, 1), bb_re_t.transpose(1, 2, 0),
                          bb_im_t.transpose(1, 2, 0)])
    w_in_near, near_ids = fetch_in(0, [h0m, tiles_cn, tiles_nc])
    proj = _mm_shards(h0m, w_in_near, sm["b_in"], near_ids, name="in_proj_near")
    wts["w_in"], far_ids = fetch_in(1, [proj])
    proj = _mm_shards(h0m, wts["w_in"], sm["b_in"], far_ids, prev=proj, name="in_proj_far")

    y, gy, h_re, h_im = _ssm_fwd(proj, tiles_cn, tiles_nc, a_re_rows, a_im_rows, sm["ssm_d"], name="ssm_fwd")

    q, k, v = _qkv_split(proj, cos_t, sin_t, name="qkv_split")
    outs, lses = [], []
    for g, dil in enumerate(DILATIONS):
        o_g, l_g = _dil_fwd(q[g], k[g], v[g], dil, name=f"dil_att_fwd_{dil}")
        outs.append(o_g)
        lses.append(l_g)
    att, lse = _att_merge(outs, lses, name="att_merge")
    wts.update(fetch(0, [att]))
    z = _mm(gy, wts["w_glu"], bias=sm["b_glu"], b_shards=True, name="glu_proj")
    b_att = _mm(att, wts["w_att_up"], b_shards=True, name="att_up")

    mixed, h1, xh1, rs1, h1m = _mix_out_ln(proj, z, b_att, wts["w_mix_out"], sm["b_mix_out"], h0, sm["ln1_g"],
                                           sm["ln1_b"], alpha=DEEPNORM_ALPHA, name="gate_mix_out_ln1")

    wts.update(fetch(1, [h1m]))
    xq = _mm(h1m, wts["w_xq"], out_dtype=MXU_DTYPE, name="xatt_q")
    kv = _mm(mem, wts["w_xkv"], out_dtype=MXU_DTYPE, b_shards=True, name="xatt_kv")
    xo_in = _xatt_fwd(xq, kv, name="xatt_fwd")
    h2, xh2, rs2, h2m = _mm_ln_fwd(xo_in, wts["w_xo"], None, h1, sm["ln2_g"], sm["ln2_b"], alpha=DEEPNORM_ALPHA,
                                   name="xatt_o_ln2")

    pre, act = _mm(h2m, wts["w_ff1"], bias=sm["b_ff1"], b_shards=True, name="ff1",
                   also=(lambda r: jnp.square(jnp.maximum(r, 0.0)), MXU_DTYPE))
    ff = _mm(act, wts["w_ff2"], bias=sm["b_ff2"], name="ff2")

    gw, gs = {}, {}
    dr3, dr3m, gs["ln3_g"], gs["ln3_b"], gs["b_ff2"], loss_row = _ln_loss_bwd(
        h2, ff, target, sm["ln3_g"], sm["ln3_b"], alpha=DEEPNORM_ALPHA, name="ln3_loss")
    wgrad = functools.partial(_mm, ta=True, out_dtype=WIRE_DTYPE, tk=2048)
    gw["w_ff2"] = wgrad(act, dr3m, tk=1024, name="ff2_dw")
    dpre, gs["b_ff1"] = _mm(dr3m, wts["w_ff2"], tb=True, out_dtype=MXU_DTYPE, colsum=True, name="ff2_dx",
                            gate=(pre, lambda p: 2.0 * jnp.maximum(p, 0.0)))
    gw["w_ff1"] = wgrad(h2m, dpre, out_shards=True, name="ff1_dw")
    sent = send(0, gw)
    dh2 = _mm(dpre, wts["w_ff1"], tb=True, b_shards=True, after=sent, name="ff1_dx")
    dr2, dr2m, gs["ln2_g"], gs["ln2_b"], _ = _ln_bwd(dr3, dh2, xh2, rs2, sm["ln2_g"], alpha=DEEPNORM_ALPHA,
                                                     name="ln2_bwd")
    gw["w_xo"] = wgrad(xo_in, dr2m, name="xatt_o_dw")
    dxo_in = _mm(dr2m, wts["w_xo"], tb=True, out_dtype=MXU_DTYPE, name="xatt_o_dx")
    dxq, dkv = _xatt_bwd(xq, kv, dxo_in, name="xatt_bwd")
    gw["w_xq"] = wgrad(h1m, dxq, name="xatt_q_dw")
    gw["w_xkv"] = wgrad(mem, dkv, out_shards=True, name="xatt_kv_dw")
    dr1, dr1m, gs["ln1_g"], gs["ln1_b"], gs["b_mix_out"] = _mm_ln_bwd(
        dxq, wts["w_xq"], dr2, xh1, rs1, sm["ln1_g"], alpha=DEEPNORM_ALPHA, name="xatt_q_dx_ln1")
    gw["w_mix_out"] = wgrad(mixed, dr1m, name="mix_out_dw")
    sent = send(1, gw)
    dmixed = _mm(dr1m, wts["w_mix_out"], tb=True, after=sent, name="mix_out_dx")
    dgs, dga, dz, db_att, s_gs, s_ga, gs["b_glu"] = _mix_bwd(dmixed, proj, z, b_att, name="gate_mix_bwd")

    gw["w_att_up"] = wgrad(att, db_att, out_shards=True, name="att_up_dw")
    gw["w_glu"] = wgrad(gy, dz, out_shards=True, name="glu_dw")
    sent = send(2, gw)
    datt = _mm(db_att, wts["w_att_up"], tb=True, b_shards=True, after=sent, name="att_up_dx")
    stats = _att_stats(datt, att, lse, name="att_stats")
    dqkv = [_dil_bwd(q[g], k[g], v[g], datt, stats, dil, name=f"dil_att_bwd_{dil}") for g, dil in enumerate(DILATIONS)]

    dgy = _mm(dz, wts["w_glu"], tb=True, b_shards=True, name="glu_dx")
    dy, gs["ssm_d"] = _gelu_bwd(dgy, y, proj, name="gelu_bwd")
    du, s_u, dc_re_t, dc_im_t, dbb_re_t, dbb_im_t, da_re, da_im = _ssm_bwd(
        proj, dy, h_re, h_im, tiles_cn, tiles_nc, a_re_rows, a_im_rows, sm["ssm_d"], name="ssm_bwd")
    gs["ssm_c_re"], gs["ssm_c_im"] = dc_re_t, -dc_im_t
    disc_ct = (da_re.reshape(grp), da_im.reshape(grp), dbb_re_t.transpose(1, 0, 2), dbb_im_t.transpose(1, 0, 2))
    d_logdt, gs["ssm_a_re"], gs["ssm_a_im"], d_b_re_t, d_b_im_t = _whole(
        _disc_transpose, disc_in + disc_ct, [(SSM_GROUPS, 1), grp, grp, chn, chn], name="ssm_disc_bwd")
    gs["ssm_log_dt"] = d_logdt
    gs["ssm_b_re"], gs["ssm_b_im"] = d_b_re_t.transpose(1, 2, 0), d_b_im_t.transpose(1, 2, 0)

    dproj, s_qkv = _dproj_assemble(du, dqkv, dgs, dga, cos_t, sin_t, name="dproj_assemble")
    gs["b_in"] = jnp.concatenate([s_u, *s_qkv, s_gs, s_ga], axis=1)
    sent = send_small(gs, SMALL_EARLY)
    gw["w_in"] = wgrad(h0m, dproj, out_shards=True, after=sent, name="in_proj_dw")
    sent = send(3, gw)
    dh0 = _mm(dproj, wts["w_in"], tb=True, b_shards=True, after=sent, name="in_proj_dx")
    grad_x, gs["ln_in_g"], gs["ln_in_b"], _ = _ln_bwd(dr1, dh0, xh0, rs0, sm["ln_in_g"], alpha=DEEPNORM_ALPHA,
                                                      operand=False, name="ln_in_bwd")
    return loss_row, grad_x, gs


_IN_HBM = pl.BlockSpec(memory_space=pltpu.HBM)
_IN_SEMAPHORE = pl.BlockSpec(memory_space=pltpu.SEMAPHORE)


def _device_index():
    return 4 * lax.axis_index("x") + 2 * lax.axis_index("y") + lax.axis_index("c")


ALL_PEERS = tuple(range(1, N_DEV))
NEAR_PEERS = (1, 2, 3, 4, 5)
FAR_PEERS = (6, 7)


def _peer_index(kk):
    x, y, c = lax.axis_index("x"), lax.axis_index("y"), lax.axis_index("c")
    return 4 * ((x + (kk >> 2)) % 2) + 2 * ((y + ((kk >> 1) & 1)) % 2) + (c + (kk & 1)) % 2


def _exchange_copies(src_refs, land_refs, send_sems, recv_sems, scatter, peers):
    x, y, c = lax.axis_index("x"), lax.axis_index("y"), lax.axis_index("c")
    me = 4 * x + 2 * y + c
    pairs = []
    for a, (src_ref, land_ref) in enumerate(zip(src_refs, land_refs)):
        for idx, kk in enumerate(peers):
            px = (x + (kk >> 2)) % 2
            py = (y + ((kk >> 1) & 1)) % 2
            pc = (c + (kk & 1)) % 2
            peer = 4 * px + 2 * py + pc
            sem = a * len(peers) + idx
            src = src_ref.at[peer] if scatter else src_ref

            def copy(dst, src=src, sem=sem, px=px, py=py, pc=pc):
                return pltpu.make_async_remote_copy(
                    src_ref=src, dst_ref=dst, send_sem=send_sems.at[sem], recv_sem=recv_sems.at[sem],
                    device_id=(px, py, pc), device_id_type=pl.DeviceIdType.MESH)

            pairs.append((functools.partial(copy, land_ref.at[me]), functools.partial(copy, land_ref.at[peer])))
    return pairs


def _own_copies(src_refs, land_refs, own_sems, scatter):
    me = _device_index()
    return [functools.partial(pltpu.make_async_copy, src_ref.at[me] if scatter else src_ref, land_ref.at[me],
                              own_sems.at[a]) for a, (src_ref, land_ref) in enumerate(zip(src_refs, land_refs))]


def _exchange_start(srcs, *, scatter, name, after=None, peers=ALL_PEERS, lands=None):
    n_arr, n_sem = len(srcs), len(srcs) * len(peers)
    own = lands is None
    if own:
        lands = [lax.empty((N_DEV,) + tuple(s.shape[1:] if scatter else s.shape), s.dtype) for s in srcs]
    n_in = 2 * n_arr + (after is not None)

    def body(*refs):
        send_sems, recv_sems = refs[n_in], refs[n_in + 1]
        for sent, _ in _exchange_copies(refs[:n_arr], refs[n_arr:2 * n_arr], send_sems, recv_sems, scatter, peers):
            sent().start()
        if own:
            for local in _own_copies(refs[:n_arr], refs[n_arr:2 * n_arr], refs[n_in + 2], scatter):
                local().start()
        refs[-1][...] = jnp.zeros_like(refs[-1])

    sems = [pltpu.SemaphoreType.DMA((n_sem,)), pltpu.SemaphoreType.DMA((n_sem,))] + [pltpu.SemaphoreType.DMA((n_arr,))] * own
    through = [pltpu.HBM(t.shape, t.dtype) for t in (*srcs, *lands)]
    res = pl.pallas_call(
        body, name=name, out_shape=(*sems, *through, jax.ShapeDtypeStruct((8, LANES), F32)),
        in_specs=[_IN_HBM] * (2 * n_arr) + [pl.BlockSpec(memory_space=pl.ANY)] * (after is not None),
        out_specs=(*[_IN_SEMAPHORE] * len(sems), *[_IN_HBM] * (2 * n_arr), pl.BlockSpec(memory_space=pltpu.VMEM)),
        input_output_aliases={i: len(sems) + i for i in range(2 * n_arr)},
        compiler_params=pltpu.CompilerParams(has_side_effects=pltpu.SideEffectType.DATAFLOW_SIDE_EFFECTING),
    )(*[pltpu.with_memory_space_constraint(t, pltpu.HBM) for t in (*srcs, *lands)],
      *([after] if after is not None else []))
    first = len(sems)
    handle = dict(sems=res[:first], srcs=res[first:first + n_arr], lands=res[first + n_arr:first + 2 * n_arr],
                  scatter=scatter, peers=peers, own=own)
    return handle, res[-1]


def _exchange_wait(handle, *, after, name, srcs=None, lands=None):
    srcs = handle["srcs"] if srcs is None else srcs
    lands = handle["lands"] if lands is None else lands
    sems, scatter, peers, own = handle["sems"], handle["scatter"], handle["peers"], handle["own"]
    n_arr = len(srcs)
    after = list(after)

    def body(*refs):
        src_refs, land_refs = refs[:n_arr], refs[n_arr:2 * n_arr]
        for sent, received in _exchange_copies(src_refs, land_refs, refs[2 * n_arr], refs[2 * n_arr + 1], scatter, peers):
            sent().wait_send()
            received().wait_recv()
        if own:
            for local in _own_copies(src_refs, land_refs, refs[2 * n_arr + 2], scatter):
                local().wait()

    res = pl.pallas_call(
        body, name=name, out_shape=tuple(pltpu.HBM(t.shape, t.dtype) for t in (*srcs, *lands)),
        in_specs=[_IN_HBM] * (2 * n_arr) + [_IN_SEMAPHORE] * len(sems) + [pl.BlockSpec(memory_space=pl.ANY)] * len(after),
        out_specs=tuple([_IN_HBM] * (2 * n_arr)), input_output_aliases={i: i for i in range(2 * n_arr)},
        compiler_params=pltpu.CompilerParams(has_side_effects=pltpu.SideEffectType.DATAFLOW_SIDE_EFFECTING),
    )(*srcs, *lands, *sems, *after)
    return res[:n_arr], res[n_arr:]


def _adamw(g, w, m, v):
    m_new = ADAM_B1 * m + (1.0 - ADAM_B1) * g
    v_new = ADAM_B2 * v + (1.0 - ADAM_B2) * jnp.square(g)
    m_hat = m_new / (1.0 - ADAM_B1 ** ADAM_STEP)
    v_hat = v_new / (1.0 - ADAM_B2 ** ADAM_STEP)
    return g, -ADAM_LR * (m_hat / (jnp.sqrt(v_hat) + ADAM_EPS) + ADAM_WD * w), m_new, v_new


def _reduce_adamw(gstack, w, m, v, *, name, tr=128):
    n_rows, cols = w.shape
    tr = min(tr, n_rows)
    assert n_rows % tr == 0, (name, n_rows, tr)

    def body(g_ref, w_ref, m_ref, v_ref, *out_refs):
        g = g_ref[0].astype(F32)
        for dev in range(1, N_DEV):
            g = g + g_ref[dev].astype(F32)
        for o_ref, val in zip(out_refs, _adamw(g, w_ref[...], m_ref[...], v_ref[...])):
            o_ref[...] = val

    flat = pl.BlockSpec((tr, cols), lambda i: (i, 0))
    shape = jax.ShapeDtypeStruct((n_rows, cols), F32)
    return pl.pallas_call(
        body, name=name, grid=(n_rows // tr,),
        in_specs=[pl.BlockSpec((N_DEV, tr, cols), lambda i: (0, i, 0)), flat, flat, flat],
        out_specs=[flat] * 4, out_shape=[shape] * 4, compiler_params=_params("parallel"),
    )(gstack, w, m, v)


SMALL_FLAT_SSM = ("ssm_b_re", "ssm_b_im", "ssm_c_re", "ssm_c_im")


def _small_view(name, shape):
    size = int(np.prod(shape))
    if name in SMALL_FLAT_SSM:
        return SSM_GROUPS, size // SSM_GROUPS
    if name in ("ssm_a_re", "ssm_a_im"):
        return SSM_GROUPS, SSM_STATE
    return 1, size


def _pack_rows(view):
    return -(-(view[0] * view[1]) // PACK_COLS)


SMALL_LATE = ("ln_in_g", "ln_in_b")
SMALL_EARLY = tuple(n for n in SMALL if n not in SMALL_LATE)


def _pack_small(gs, names, views):
    parts = []
    for n in names:
        flat = gs[n].reshape(-1).astype(WIRE_DTYPE)
        parts.append(jnp.pad(flat, (0, _pack_rows(views[n]) * PACK_COLS - flat.shape[0])))
    total = sum(p.shape[0] for p in parts) // PACK_COLS
    parts.append(jnp.zeros(((-total % PACK_ROW_ALIGN) * PACK_COLS,), WIRE_DTYPE))
    return jnp.concatenate(parts).reshape(-1, PACK_COLS)


def _small_pieces(view):
    rows, cols = view
    if cols == PACK_COLS:
        return [(0, rows, 0, 0, 0, cols)]
    if rows == 1 and cols > PACK_COLS:
        return [(kk, 1, 0, 0, kk * PACK_COLS, PACK_COLS) for kk in range(cols // PACK_COLS)]
    if rows == 1:
        return [(0, 1, 0, 0, 0, cols)]
    return [((r * cols) // PACK_COLS, 1, (r * cols) % PACK_COLS, r, 0, cols) for r in range(rows)]


def _adamw_small(stacks, views, w, m, v, *, name):
    n = len(SMALL)
    place, first = {}, [0, 0]
    for k, names in enumerate((SMALL_EARLY, SMALL_LATE)):
        for name_ in names:
            place[name_] = (k, first[k])
            first[k] += _pack_rows(views[name_])

    def body(early_ref, late_ref, *refs):
        ins, outs = refs[:3 * n], refs[3 * n:]
        for i, name_ in enumerate(SMALL):
            stack_ref = (early_ref, late_ref)[place[name_][0]]
            row0 = place[name_][1]
            for prow, nrows, lane, orow, ocol, width in _small_pieces(views[name_]):
                src = (slice(row0 + prow, row0 + prow + nrows), slice(lane, lane + width))
                dst = (slice(orow, orow + nrows), slice(ocol, ocol + width))
                g = stack_ref[(0,) + src].astype(F32)
                for dev in range(1, N_DEV):
                    g = g + stack_ref[(dev,) + src].astype(F32)
                res = _adamw(g, ins[i][dst], ins[n + i][dst], ins[2 * n + i][dst])
                for kk, val in enumerate(res):
                    outs[kk * n + i][dst] = val

    args = [*stacks, *[d[name_] for d in (w, m, v) for name_ in SMALL]]
    out_views = [views[name_] for _ in range(4) for name_ in SMALL]
    res = pl.pallas_call(
        body, name=name, grid=(1,), in_specs=[_full_spec(t.shape) for t in args],
        out_specs=[_full_spec(s) for s in out_views], out_shape=[jax.ShapeDtypeStruct(s, F32) for s in out_views],
        compiler_params=_params("arbitrary"),
    )(*args)
    return [dict(zip(SMALL, res[kk * n:(kk + 1) * n])) for kk in range(4)]


def kernel(x, mem, positions, ln_in_g, ln_in_b, w_in, b_in, ssm_log_dt, ssm_a_re, ssm_a_im, ssm_b_re, ssm_b_im, ssm_c_re, ssm_c_im, ssm_d, w_glu, b_glu, w_att_up, w_mix_out, b_mix_out, ln1_g, ln1_b, w_xq, w_xkv, w_xo, ln2_g, ln2_b, w_ff1, b_ff1, w_ff2, b_ff2, ln3_g, ln3_b, loss_target, m_ln_in_g, m_ln_in_b, m_w_in, m_b_in, m_ssm_log_dt, m_ssm_a_re, m_ssm_a_im, m_ssm_b_re, m_ssm_b_im, m_ssm_c_re, m_ssm_c_im, m_ssm_d, m_w_glu, m_b_glu, m_w_att_up, m_w_mix_out, m_b_mix_out, m_ln1_g, m_ln1_b, m_w_xq, m_w_xkv, m_w_xo, m_ln2_g, m_ln2_b, m_w_ff1, m_b_ff1, m_w_ff2, m_b_ff2, m_ln3_g, m_ln3_b, v_ln_in_g, v_ln_in_b, v_w_in, v_b_in, v_ssm_log_dt, v_ssm_a_re, v_ssm_a_im, v_ssm_b_re, v_ssm_b_im, v_ssm_c_re, v_ssm_c_im, v_ssm_d, v_w_glu, v_b_glu, v_w_att_up, v_w_mix_out, v_b_mix_out, v_ln1_g, v_ln1_b, v_w_xq, v_w_xkv, v_w_xo, v_ln2_g, v_ln2_b, v_w_ff1, v_b_ff1, v_w_ff2, v_b_ff2, v_ln3_g, v_ln3_b):
    given = dict(locals())
    w_arg = {n: given[n] for n in WEIGHTS}
    m_arg = {n: given["m_" + n] for n in WEIGHTS}
    v_arg = {n: given["v_" + n] for n in WEIGHTS}

    in_near, token = _exchange_start([w_arg["w_in"][0].astype(MXU_DTYPE)], scatter=False, peers=NEAR_PEERS,
                                     name="gather_start_in_near")
    w_in_state = [in_near["srcs"], in_near["lands"]]
    token, w_arg, m_arg, v_arg = lax.optimization_barrier((token, w_arg, m_arg, v_arg))
    shards = {n: w_arg[n][0].astype(MXU_DTYPE) for n in BIG if n != "w_in"}
    later = {}

    def start_group(i, after):
        later[i], sent = _exchange_start([shards[n] for n in GATHER_GROUPS[i]], scatter=False, after=after,
                                         name=f"gather_start_{i}")
        return sent

    small_views = {n: _small_view(n, w_arg[n].shape) for n in SMALL}
    small_w, small_m, small_v = [{n: d[n].reshape(small_views[n]) for n in SMALL} for d in (w_arg, m_arg, v_arg)]
    relaid = [d[n] for d in (small_w, small_m, small_v) for n in SMALL_FLAT_SSM]

    def fetch_in(part, after):
        if part == 0:
            w_in_state[:] = _exchange_wait(in_near, after=after + relaid, name="gather_wait_in_near")
            later["far"], sent = _exchange_start(w_in_state[0], scatter=False, peers=FAR_PEERS, lands=w_in_state[1],
                                                 name="gather_start_in_far")
            w_in_state[:] = [later["far"]["srcs"], later["far"]["lands"]]
            start_group(0, sent)
            return w_in_state[1][0], jnp.stack([_peer_index(kk) for kk in (0,) + NEAR_PEERS]).astype(jnp.int32)
        w_in_state[:] = _exchange_wait(later["far"], after=after, name="gather_wait_in_far")
        start_group(1, w_in_state[1][0])
        return w_in_state[1][0], jnp.stack([_peer_index(kk) for kk in FAR_PEERS]).astype(jnp.int32)

    def fetch(i, after):
        _, lands = _exchange_wait(later[i], after=after, name=f"gather_wait_{i}")
        full = dict(zip(GATHER_GROUPS[i], lands))
        return {n: t if n in BIG_COL_SHARDED else t.reshape(-1, t.shape[-1]) for n, t in full.items()}

    scatters = {}

    def send(i, gw):
        slots = [gw[n] if n in BIG_COL_SHARDED else gw[n].reshape(N_DEV, -1, gw[n].shape[-1]) for n in SCATTER_GROUPS[i]]
        handle, sent = _exchange_start(slots, scatter=True, name=f"scatter_start_{i}")
        scatters[i] = (handle, slots)
        return sent

    sm = {}
    for n in SMALL:
        t = w_arg[n]
        if n.startswith("ssm_") and n not in ("ssm_d", "ssm_log_dt"):
            sm[n] = t[0]
        else:
            sm[n] = t.reshape(1, -1)

    smalls = []

    def send_small(gs, names):
        handle, sent = _exchange_start([_pack_small(gs, names, small_views)], scatter=False,
                                       name=f"small_start_{len(smalls)}")
        smalls.append(handle)
        return sent

    loss_row, grad_x, gs = _local_grads(x[0], mem[0], positions.reshape(-1, 1), loss_target[0], sm, fetch_in, fetch,
                                        send, send_small, token)
    loss = lax.psum(loss_row[0, 0], ("x", "y", "c"))
    send_small(gs, SMALL_LATE)

    results = [{}, {}, {}, {}]
    done = grad_x
    for i, names in enumerate(SCATTER_GROUPS):
        handle, slots = scatters[i]
        _, lands = _exchange_wait(handle, after=[done], name=f"scatter_wait_{i}")
        for n, land, slot in zip(names, lands, slots):
            res = _reduce_adamw(land, w_arg[n][0], m_arg[n][0], v_arg[n][0], name="adamw_" + n)
            done = res[0]
            for d, r in zip(results, res):
                d[n] = r[None]
    stacks = [_exchange_wait(handle, after=[done], name=f"small_wait_{i}")[1][0] for i, handle in enumerate(smalls)]
    res = _adamw_small(stacks, small_views, small_w, small_m, small_v, name="adamw_small")
    for d, r in zip(results, res):
        d.update({n: r[n].reshape(w_arg[n].shape) for n in SMALL})
    out = [loss, grad_x[None]]
    for d in results:
        out += [d[n] for n in WEIGHTS]
    return tuple(out)
```

```python
import functools

import numpy as np
import jax
import jax.numpy as jnp
from jax import lax
from jax.experimental import pallas as pl
from jax.experimental.pallas import tpu as pltpu

F32 = jnp.float32
MXU_DTYPE = jnp.bfloat16
WIRE_DTYPE = jnp.bfloat16
VMEM_LIMIT_BYTES = 48 * 1024 * 1024
LANES = 128

N_DEV = 8
D_MODEL = 1024
SSM_GROUP = 16
SSM_WIDTH = 768
SSM_GROUPS = SSM_WIDTH // SSM_GROUP
SSM_STATE = 64
SSM_CH = SSM_GROUPS * SSM_STATE
SSM_TILES = SSM_WIDTH // LANES
GROUPS_PER_TILE = LANES // SSM_GROUP
STATE_VREG_ROWS = SSM_CH // LANES
ATT_HEAD_DIM = 64
ATT_HEADS_PER_GROUP = 4
ATT_MERGED = ATT_HEADS_PER_GROUP * ATT_HEAD_DIM
LANE_HALVES = ATT_MERGED // LANES
DILATIONS = (1, 4, 16)
ATT_BLK = 128
ATT_SCALE = ATT_HEAD_DIM ** -0.5
ROT_DIM = ATT_HEAD_DIM // 4
ROPE_THETA = 500000.0
XATT_HEADS = 4
XATT_HEAD_DIM = D_MODEL // XATT_HEADS
XATT_SCALE = XATT_HEAD_DIM ** -0.5
DEEPNORM_ALPHA = 2.0 ** 0.25
LN_EPS = 1e-5
NEG_INF = -1e30
OFF_Q_BLK, OFF_K_BLK, OFF_V_BLK = 3, 6, 9
OFF_GS_BLK, OFF_GA_BLK = 3, 4

ADAM_LR = 0.001
ADAM_B1 = 0.9
ADAM_B2 = 0.999
ADAM_EPS = 1e-08
ADAM_WD = 0.01
ADAM_STEP = 10

BIG = ("w_in", "w_glu", "w_att_up", "w_mix_out", "w_xq", "w_xkv", "w_xo", "w_ff1", "w_ff2")
BIG_COL_SHARDED = ("w_in", "w_glu", "w_att_up", "w_xkv", "w_ff1")
WEIGHTS = ("ln_in_g", "ln_in_b", "w_in", "b_in", "ssm_log_dt", "ssm_a_re", "ssm_a_im", "ssm_b_re", "ssm_b_im",
           "ssm_c_re", "ssm_c_im", "ssm_d", "w_glu", "b_glu", "w_att_up", "w_mix_out", "b_mix_out", "ln1_g", "ln1_b",
           "w_xq", "w_xkv", "w_xo", "ln2_g", "ln2_b", "w_ff1", "b_ff1", "w_ff2", "b_ff2", "ln3_g", "ln3_b")
SMALL = tuple(n for n in WEIGHTS if n not in BIG)
PACK_COLS = 1024
PACK_ROW_ALIGN = 16


def _params(*sem):
    return pltpu.CompilerParams(dimension_semantics=sem, vmem_limit_bytes=VMEM_LIMIT_BYTES)


def _dot(a, b, ca, cb):
    return lax.dot_general(a.astype(MXU_DTYPE), b.astype(MXU_DTYPE), (((ca,), (cb,)), ((), ())),
                           preferred_element_type=F32)


def _fit(dim, pref):
    if dim <= pref:
        return dim
    best = max(t for t in range(LANES, pref + 1, LANES) if dim % t == 0)
    return best


def _mm(a, b, *, name, ta=False, tb=False, bias=None, out_dtype=F32, b_shards=False, out_shards=False, after=None,
        also=None, gate=None, colsum=False, epilogue=None, tm=2048, tn=1024, tk=1024):
    m, k = (a.shape[1], a.shape[0]) if ta else a.shape
    order = (lambda f: (lambda j, i, kk: f(i, j, kk))) if colsum else (lambda f: f)
    spec = lambda shape, f: pl.BlockSpec(shape, order(f))
    if b_shards:
        n_sh, rows, n_loc = b.shape
        if tb:
            n, tn, tk = rows, _fit(rows, tn), n_loc
            assert k == n_sh * n_loc, (name, k, b.shape)
            b_spec = spec((1, tn, tk), lambda i, j, kk: (kk, j, 0))
        else:
            n, tn, tk = n_sh * n_loc, n_loc, _fit(k, tk)
            b_spec = spec((1, tk, tn), lambda i, j, kk: (j, kk, 0))
    else:
        n = b.shape[0] if tb else b.shape[1]
        tn = n // N_DEV if out_shards else _fit(n, tn)
        tk = _fit(k, tk)
        b_spec = spec((tn, tk), lambda i, j, kk: (j, kk)) if tb else spec((tk, tn), lambda i, j, kk: (kk, j))
    tm = _fit(m, tm)
    nk = k // tk
    a_spec = spec((tk, tm), lambda i, j, kk: (kk, i)) if ta else spec((tm, tk), lambda i, j, kk: (i, kk))
    tile = spec((tm, tn), lambda i, j, kk: (i, j))
    in_specs, args = [a_spec, b_spec], [a, b]
    if bias is not None:
        in_specs.append(spec((1, tn), lambda i, j, kk: (0, j)))
        args.append(bias)
    if gate is not None:
        in_specs.append(tile)
        args.append(gate[0])
    if after is not None:
        in_specs.append(pl.BlockSpec(memory_space=pl.ANY))
        args.append(after)
    if epilogue is not None:
        ep_fn, ep_rows, ep_fulls, ep_row_outs, ep_acc_outs = epilogue
        assert tn == n and not (colsum or also or gate or out_shards), name
        ep_first = len(args)
        in_specs += [spec((tm, t.shape[1]), lambda i, j, kk: (i, 0)) for t in ep_rows]
        in_specs += [pl.BlockSpec(t.shape, functools.partial(lambda i, j, kk, nd: (0,) * nd, nd=t.ndim)) for t in ep_fulls]
        args += [*ep_rows, *ep_fulls]
    n_in = len(args)
    if epilogue is not None:
        out_specs = [spec((tm, w), lambda i, j, kk: (i, 0)) for w, _ in ep_row_outs]
        out_specs += [spec((1, w), lambda i, j, kk: (0, 0)) for w in ep_acc_outs]
        out_shape = [jax.ShapeDtypeStruct((m, w), dt) for w, dt in ep_row_outs]
        out_shape += [jax.ShapeDtypeStruct((1, w), F32) for w in ep_acc_outs]
    elif out_shards:
        assert n == N_DEV * tn, (name, n, tn)
        out_specs = [spec((1, tm, tn), lambda i, j, kk: (j, i, 0))]
        out_shape = [jax.ShapeDtypeStruct((N_DEV, m, tn), out_dtype)]
    else:
        out_specs = [tile]
        out_shape = [jax.ShapeDtypeStruct((m, n), out_dtype)]
    if also is not None:
        out_specs.append(tile)
        out_shape.append(jax.ShapeDtypeStruct((m, n), also[1]))
    if colsum:
        out_specs.append(spec((1, tn), lambda i, j, kk: (0, j)))
        out_shape.append(jax.ShapeDtypeStruct((1, n), F32))

    def body(*refs):
        a_ref, b_ref = refs[0], refs[1]
        o_ref = refs[n_in]
        first_row_tile = pl.program_id(1 if colsum else 0) == 0

        def product():
            return _dot(a_ref[...], b_ref[0] if b_shards else b_ref[...], 0 if ta else 1, 1 if tb else 0)

        def finish(r):
            if bias is not None:
                r = r + refs[2][...]
            if gate is not None:
                r = r * gate[1](refs[2 + (bias is not None)][...])
            if epilogue is not None:
                res = ep_fn(r, *[ref[...] for ref in refs[ep_first:n_in]])
                n_o = len(ep_row_outs)
                for ref, val in zip(refs[n_in:n_in + n_o], res[:n_o]):
                    ref[...] = val.astype(ref.dtype)
                acc_refs = refs[n_in + n_o:n_in + n_o + len(ep_acc_outs)]
                if acc_refs:
                    @pl.when(first_row_tile)
                    def _():
                        for ref in acc_refs:
                            ref[...] = jnp.zeros_like(ref)

                    for ref, val in zip(acc_refs, res[n_o:]):
                        ref[...] += val
                return
            if out_shards:
                o_ref[0] = r.astype(o_ref.dtype)
            else:
                o_ref[...] = r.astype(o_ref.dtype)
            if also is not None:
                refs[n_in + 1][...] = also[0](r).astype(also[1])
            if colsum:
                s_ref = refs[n_in + 1 + (also is not None)]

                @pl.when(first_row_tile)
                def _():
                    s_ref[...] = jnp.zeros_like(s_ref)

                s_ref[...] += _colsum(r)

        if nk == 1:
            finish(product())
            return
        acc_ref = refs[-1]
        kk = pl.program_id(2)

        @pl.when(kk == 0)
        def _():
            acc_ref[...] = jnp.zeros_like(acc_ref)

        acc_ref[...] += product()

        @pl.when(kk == nk - 1)
        def _():
            finish(acc_ref[...])

    grid = (n // tn, m // tm, nk) if colsum else (m // tm, n // tn, nk)
    res = pl.pallas_call(
        body, name=name, grid=grid, in_specs=in_specs, out_specs=out_specs, out_shape=out_shape,
        scratch_shapes=[pltpu.VMEM((tm, tn), F32)] if nk > 1 else [],
        compiler_params=_params("arbitrary" if epilogue is not None else "parallel",
                                "arbitrary" if colsum else "parallel", "arbitrary"),
    )(*args)
    return res[0] if len(res) == 1 else res


def _mm_shards(a, w, bias, shard_ids, *, name, prev=None, tm=2048):
    m, k = a.shape
    n_sh, _, n_loc = w.shape
    tm = _fit(m, tm)

    def body(ids_ref, a_ref, w_ref, b_ref, *rest):
        rest[-1][...] = _dot(a_ref[...], w_ref[0], 1, 0) + b_ref[...]

    grid_spec = pltpu.PrefetchScalarGridSpec(
        num_scalar_prefetch=1, grid=(m // tm, shard_ids.shape[0]),
        in_specs=[pl.BlockSpec((tm, k), lambda i, j, ids: (i, 0)),
                  pl.BlockSpec((1, k, n_loc), lambda i, j, ids: (ids[j], 0, 0)),
                  pl.BlockSpec((1, n_loc), lambda i, j, ids: (0, ids[j]))]
        + [pl.BlockSpec(memory_space=pl.ANY)] * (prev is not None),
        out_specs=pl.BlockSpec((tm, n_loc), lambda i, j, ids: (i, ids[j])))
    return pl.pallas_call(
        body, name=name, grid_spec=grid_spec, out_shape=jax.ShapeDtypeStruct((m, n_sh * n_loc), F32),
        input_output_aliases={4: 0} if prev is not None else {}, compiler_params=_params("parallel", "arbitrary"),
    )(shard_ids, a, w, bias, *([prev] if prev is not None else []))


ROW_TILE = 512


def _rowcall(fn, rows, fulls, row_outs, acc_outs=(), *, n_rows, tm, name, after=None):
    n_r, n_f, n_o, n_a = len(rows), len(fulls), len(row_outs), len(acc_outs)
    n_in = n_r + n_f + (after is not None)
    assert n_rows % tm == 0, (name, n_rows, tm)

    def body(*refs):
        res = fn(*[r[...] for r in refs[:n_r + n_f]])
        res = tuple(res) if isinstance(res, (tuple, list)) else (res,)
        o_refs = refs[n_in:n_in + n_o]
        a_refs = refs[n_in + n_o:]
        for o_ref, val in zip(o_refs, res[:n_o]):
            o_ref[...] = val.astype(o_ref.dtype)
        if n_a:
            @pl.when(pl.program_id(0) == 0)
            def _():
                for a_ref in a_refs:
                    a_ref[...] = jnp.zeros_like(a_ref)

            for a_ref, val in zip(a_refs, res[n_o:]):
                a_ref[...] += val

    in_specs = [pl.BlockSpec((tm, w), functools.partial(lambda i, cb: (i, cb), cb=cb)) for _, w, cb in rows]
    in_specs += [pl.BlockSpec(f.shape, functools.partial(lambda i, nd: (0,) * nd, nd=f.ndim)) for f in fulls]
    in_specs += [pl.BlockSpec(memory_space=pl.ANY)] * (after is not None)
    out_specs = [pl.BlockSpec((tm, w), lambda i: (i, 0)) for w, _ in row_outs]
    out_specs += [pl.BlockSpec((1, w), lambda i: (0, 0)) for w in acc_outs]
    out_shape = [jax.ShapeDtypeStruct((n_rows, w), dt) for w, dt in row_outs]
    out_shape += [jax.ShapeDtypeStruct((1, w), F32) for w in acc_outs]
    return pl.pallas_call(
        body, name=name, grid=(n_rows // tm,), in_specs=in_specs, out_specs=out_specs, out_shape=out_shape,
        compiler_params=_params("arbitrary" if n_a else "parallel"),
    )(*[r[0] for r in rows], *fulls, *([after] if after is not None else []))


def _colsum(v):
    return jnp.sum(v, axis=0, keepdims=True)


def _layer_norm(xin, g, b):
    mu = jnp.mean(xin, axis=-1, keepdims=True)
    xc = xin - mu
    var = jnp.mean(xc * xc, axis=-1, keepdims=True)
    rstd = lax.rsqrt(var + LN_EPS)
    xh = xc * rstd
    return xh * g + b, xh, rstd


def _layer_norm_bwd(dy, xh, rstd, g):
    dyg = dy * g
    m1 = jnp.mean(dyg, axis=-1, keepdims=True)
    m2 = jnp.mean(dyg * xh, axis=-1, keepdims=True)
    dx = rstd * (dyg - m1 - xh * m2)
    return dx, _colsum(dy * xh), _colsum(dy), _colsum(dx)


def _ln_fwd(a, g, b, *, name):
    n_rows, d = a.shape

    def fn(av, gv, bv):
        y, xh, rstd = _layer_norm(av, gv, bv)
        return y, xh, rstd, y

    return _rowcall(fn, [(a, d, 0)], [g, b], [(d, F32), (d, F32), (1, F32), (d, MXU_DTYPE)], n_rows=n_rows, tm=ROW_TILE,
                    name=name)


def _ln_bwd(dya, dyb, xh, rstd, g, *, alpha, name, operand=True):
    n_rows, d = xh.shape

    def fn(da, db, xhv, rs, gv):
        dx, *sums = _layer_norm_bwd(alpha * da + db, xhv, rs, gv)
        return (dx,) + ((dx,) if operand else ()) + tuple(sums)

    rows = [(dya, d, 0), (dyb, d, 0), (xh, d, 0), (rstd, 1, 0)]
    return _rowcall(fn, rows, [g], [(d, F32)] + [(d, MXU_DTYPE)] * operand, [d, d, d], n_rows=n_rows, tm=ROW_TILE, name=name)


LN_EPILOGUE_ROWS = 1024


def _mm_ln_fwd(x, w, bias, a, g, b, *, alpha, name):
    d = a.shape[1]

    def fn(r, av, gv, bv):
        y, xh, rstd = _layer_norm(alpha * av + r, gv, bv)
        return y, xh, rstd, y

    return _mm(x, w, bias=bias, name=name, tm=LN_EPILOGUE_ROWS,
               epilogue=(fn, [a], [g, b], [(d, F32), (d, F32), (1, F32), (d, MXU_DTYPE)], []))


def _mm_ln_bwd(x, w, dya, xh, rstd, g, *, alpha, name):
    d = xh.shape[1]

    def fn(r, da, xhv, rs, gv):
        dx, *sums = _layer_norm_bwd(alpha * da + r, xhv, rs, gv)
        return (dx, dx, *sums)

    return _mm(x, w, tb=True, name=name, tm=LN_EPILOGUE_ROWS,
               epilogue=(fn, [dya, xh, rstd], [g], [(d, F32), (d, MXU_DTYPE)], [d, d, d]))


def _ln_loss_bwd(a, r, target, g, b, *, alpha, name):
    n_rows, d = a.shape

    def fn(av, rv, tv, gv, bv):
        y, xh, rs = _layer_norm(alpha * av + rv, gv, bv)
        diff = y - tv
        part = jnp.sum(jnp.sum(diff * diff, axis=1, keepdims=True), axis=0, keepdims=True) * (0.5 / d)
        dx, *sums = _layer_norm_bwd(diff * (1.0 / d), xh, rs, gv)
        return (dx, dx, *sums, jnp.broadcast_to(part, (1, LANES)))

    return _rowcall(fn, [(a, d, 0), (r, d, 0), (target, d, 0)], [g, b], [(d, F32), (d, MXU_DTYPE)], [d, d, d, LANES],
                    n_rows=n_rows, tm=ROW_TILE, name=name)


def _rope_lane_constants():
    lane = np.arange(ATT_MERGED)
    in_head = lane % ATT_HEAD_DIM
    sign = np.where(in_head < ROT_DIM // 2, -1.0, np.where(in_head < ROT_DIM, 1.0, 0.0)).astype(np.float32)
    inv_freq = ROPE_THETA ** (-jnp.arange(0, ROT_DIM, 2, dtype=F32) / ROT_DIM)
    return inv_freq[lane % (ROT_DIM // 2)].reshape(1, ATT_MERGED), jnp.asarray(sign).reshape(1, ATT_MERGED)


def _rope_tables(pos_col, *, name, after=None):
    inv_lane, sign = _rope_lane_constants()

    def fn(pos, inv, sg):
        ang = pos.astype(F32) * inv
        return jnp.where(sg != 0.0, jnp.cos(ang), 1.0), sg * jnp.sin(ang)

    return _rowcall(fn, [(pos_col, 1, 0)], [inv_lane, sign], [(ATT_MERGED, F32), (ATT_MERGED, F32)],
                    n_rows=pos_col.shape[0], tm=512, name=name, after=after)


def _rot_partner(t):
    lane = lax.broadcasted_iota(jnp.int32, t.shape, 1)
    width = t.shape[1]
    return jnp.where((lane & (ROT_DIM // 2)) == 0, pltpu.roll(t, width - ROT_DIM // 2, 1), pltpu.roll(t, ROT_DIM // 2, 1))


def _rope(t, cos_t, sin_t):
    return t * cos_t + _rot_partner(t) * sin_t


def _rope_transpose(dt, cos_t, sin_t):
    return dt * cos_t + _rot_partner(dt * sin_t)


def _strided_rows(r, count, stride):
    return pl.ds(r, count) if stride == 1 else pl.ds(r, count, stride=stride)


def _qkv_split(proj, cos_t, sin_t, *, name, tm=512):
    n_rows = proj.shape[0]
    n_g = len(DILATIONS)

    def body(*refs):
        n_src = LANE_HALVES * 3 * n_g
        src, tables, dst = refs[:n_src], refs[n_src:n_src + 2 * LANE_HALVES], refs[n_src + 2 * LANE_HALVES:]
        for kind in range(3):
            for g, dil in enumerate(DILATIONS):
                for half in range(LANE_HALVES):
                    x_ref, o_ref = src[(kind * n_g + g) * LANE_HALVES + half], dst[kind * n_g + g]
                    cos_ref, sin_ref = tables[half], tables[LANE_HALVES + half]
                    for r in range(dil):
                        rows = _strided_rows(r, tm // dil, dil)
                        t = x_ref[rows, :]
                        if kind < 2:
                            t = _rope(t, cos_ref[rows, :], sin_ref[rows, :])
                        lo = r * ATT_MERGED + half * LANES
                        o_ref[:, lo:lo + LANES] = t.astype(o_ref.dtype)

    half_spec = lambda cb: pl.BlockSpec((tm, LANES), functools.partial(lambda i, cb: (i, cb), cb=cb))
    in_specs = [half_spec((off + g) * LANE_HALVES + half)
                for off in (OFF_Q_BLK, OFF_K_BLK, OFF_V_BLK) for g in range(n_g) for half in range(LANE_HALVES)]
    in_specs += [half_spec(half) for _ in range(2) for half in range(LANE_HALVES)]
    out_specs = [pl.BlockSpec((tm // dil, dil * ATT_MERGED), lambda i: (i, 0)) for _ in range(3) for dil in DILATIONS]
    out_shape = [jax.ShapeDtypeStruct((n_rows // dil, dil * ATT_MERGED), MXU_DTYPE) for _ in range(3) for dil in DILATIONS]
    outs = pl.pallas_call(
        body, name=name, grid=(n_rows // tm,), in_specs=in_specs, out_specs=out_specs, out_shape=out_shape,
        compiler_params=_params("parallel"),
    )(*[proj] * (LANE_HALVES * 3 * n_g), *[cos_t] * LANE_HALVES, *[sin_t] * LANE_HALVES)
    return outs[:n_g], outs[n_g:2 * n_g], outs[2 * n_g:]


def _mix(gs, ga, z1, z2, b_att):
    return jax.nn.sigmoid(gs) * (z1 * jax.nn.sigmoid(z2)) + jax.nn.sigmoid(ga) * b_att


def _mix_rows(proj, z, b_att):
    return [(proj, D_MODEL, OFF_GS_BLK), (proj, D_MODEL, OFF_GA_BLK), (z, D_MODEL, 0), (z, D_MODEL, 1), (b_att, D_MODEL, 0)]


def _mix_out_ln(proj, z, b_att, w, bias, a, g, b, *, alpha, name):
    def fn(gs, ga, z1, z2, ba, av, wv, biasv, gv, bv):
        mixed = _mix(gs, ga, z1, z2, ba)
        y, xh, rstd = _layer_norm(alpha * av + (_dot(mixed, wv, 1, 0) + biasv), gv, bv)
        return mixed, y, xh, rstd, y

    rows = _mix_rows(proj, z, b_att) + [(a, D_MODEL, 0)]
    outs = [(D_MODEL, MXU_DTYPE), (D_MODEL, F32), (D_MODEL, F32), (1, F32), (D_MODEL, MXU_DTYPE)]
    return _rowcall(fn, rows, [w, bias, g, b], outs, n_rows=proj.shape[0], tm=ROW_TILE // 2, name=name)


def _mix_bwd(dmixed, proj, z, b_att, *, name):
    def fn(dm, gs, ga, z1, z2, ba):
        _, vjp = jax.vjp(_mix, gs, ga, z1, z2, ba)
        dgs, dga, dz1, dz2, dba = vjp(dm)
        dz = jnp.concatenate([dz1, dz2], axis=1)
        return dgs, dga, dz, dba, _colsum(dgs), _colsum(dga), _colsum(dz)

    rows = [(dmixed, D_MODEL, 0)] + _mix_rows(proj, z, b_att)
    widths = [D_MODEL, D_MODEL, 2 * D_MODEL, D_MODEL]
    return _rowcall(fn, rows, [], [(w, MXU_DTYPE) for w in widths], widths[:3], n_rows=proj.shape[0], tm=ROW_TILE, name=name)


def _gelu_bwd(dgy, y, proj, *, name):
    def fn(dg, yv, u):
        _, vjp = jax.vjp(jax.nn.gelu, yv)
        dy = vjp(dg)[0]
        return dy, _colsum(dy * u)

    return _rowcall(fn, [(dgy, SSM_WIDTH, 0), (y, SSM_WIDTH, 0), (proj, SSM_WIDTH, 0)], [], [(SSM_WIDTH, F32)],
                    [SSM_WIDTH], n_rows=y.shape[0], tm=512, name=name)


HEAD_ROWS = ATT_HEADS_PER_GROUP * ATT_BLK


def _head_masks(rows):
    head = lax.broadcasted_iota(jnp.int32, (rows, ATT_MERGED), 1) >> (ATT_HEAD_DIM.bit_length() - 1)
    return [head == h for h in range(ATT_HEADS_PER_GROUP)]


def _stack_heads(t, masks):
    return jnp.concatenate([jnp.where(m, t, jnp.zeros_like(t)) for m in masks], axis=0)


def _unstack_heads(t4, masks):
    blocks = [t4[h * ATT_BLK:(h + 1) * ATT_BLK] for h in range(ATT_HEADS_PER_GROUP)]
    return jnp.where(masks[0], blocks[0], jnp.where(masks[1], blocks[1], jnp.where(masks[2], blocks[2], blocks[3])))


def _head_column(stats, first):
    return jnp.concatenate([stats[:, first + h:first + h + 1] for h in range(ATT_HEADS_PER_GROUP)], axis=0)


def _band_mask(first_key):
    qi = lax.broadcasted_iota(jnp.int32, (HEAD_ROWS, 2 * ATT_BLK), 0) & (ATT_BLK - 1)
    ki = lax.broadcasted_iota(jnp.int32, (HEAD_ROWS, 2 * ATT_BLK), 1)
    steps = qi + ATT_BLK - ki
    return (steps >= 0) & (steps <= ATT_BLK) & (ki >= first_key)


def _dil_fwd(q, k, v, dil, *, name):
    n_blk = q.shape[0] // ATT_BLK
    cur = pl.BlockSpec((ATT_BLK, ATT_MERGED), lambda r, n: (n, r))
    prev = pl.BlockSpec((ATT_BLK, ATT_MERGED), lambda r, n: (jnp.maximum(n - 1, 0), r))

    def body(q_ref, kp_ref, kc_ref, vp_ref, vc_ref, o_ref, l_ref):
        masks = _head_masks(ATT_BLK)
        valid = _band_mask(jnp.where(pl.program_id(1) > 0, 0, ATT_BLK))
        keys = jnp.concatenate([kp_ref[...], kc_ref[...]], axis=0)
        vals = jnp.concatenate([vp_ref[...], vc_ref[...]], axis=0)
        s = jnp.where(valid, _dot(_stack_heads(q_ref[...], masks), keys, 1, 1) * ATT_SCALE, NEG_INF)
        m = jnp.max(s, axis=-1, keepdims=True)
        p = jnp.exp(s - m)
        den = jnp.sum(p, axis=-1, keepdims=True)
        o_ref[...] = _unstack_heads(_dot(p, vals, 1, 0) / den, masks)
        l_ref[...] = _unstack_heads(jnp.broadcast_to(m + jnp.log(den), (HEAD_ROWS, ATT_MERGED)), masks)

    shape = jax.ShapeDtypeStruct(q.shape, F32)
    return pl.pallas_call(
        body, name=name, grid=(dil, n_blk), in_specs=[cur, prev, cur, prev, cur], out_specs=[cur, cur],
        out_shape=[shape, shape], compiler_params=_params("parallel", "parallel"),
    )(q, k, k, v, v)


def _att_merge(outs, lses, *, name, tm=512):
    n_g = len(outs)
    n_rows = outs[0].shape[0] * DILATIONS[0]

    def body(*refs):
        src, (att_ref, lse_ref), tmp = refs[:2 * n_g], refs[2 * n_g:2 * n_g + 2], refs[2 * n_g + 2:]
        vals = []
        for idx, src_ref in enumerate(src):
            dil = DILATIONS[idx % n_g]
            if dil == 1:
                vals.append(src_ref[...])
                continue
            for r in range(dil):
                for half in range(LANE_HALVES):
                    lo = r * ATT_MERGED + half * LANES
                    tmp[LANE_HALVES * idx + half][_strided_rows(r, tm // dil, dil), :] = src_ref[:, lo:lo + LANES]
            vals.append(jnp.concatenate([tmp[LANE_HALVES * idx + half][...] for half in range(LANE_HALVES)], axis=1))
        o, l = vals[:n_g], vals[n_g:]
        m = functools.reduce(jnp.maximum, l)
        e = [jnp.exp(li - m) for li in l]
        z = functools.reduce(jnp.add, e)
        att_ref[...] = functools.reduce(jnp.add, [(ei / z) * oi for ei, oi in zip(e, o)])
        lse_ref[...] = m + jnp.log(z)

    in_specs = [pl.BlockSpec((tm // dil, dil * ATT_MERGED), lambda i: (i, 0)) for _ in range(2) for dil in DILATIONS]
    row = pl.BlockSpec((tm, ATT_MERGED), lambda i: (i, 0))
    shape = jax.ShapeDtypeStruct((n_rows, ATT_MERGED), F32)
    return pl.pallas_call(
        body, name=name, grid=(n_rows // tm,), in_specs=in_specs, out_specs=[row, row], out_shape=[shape, shape],
        scratch_shapes=[pltpu.VMEM((tm, LANES), F32)] * (LANE_HALVES * 2 * n_g), compiler_params=_params("parallel"),
    )(*outs, *lses)


def _att_stats(datt, att, lse, *, name):
    n_rows = datt.shape[0]

    def fn(d, a, l):
        prod = d * a
        lane = lax.broadcasted_iota(jnp.int32, (d.shape[0], LANES), 1)
        out = jnp.zeros((d.shape[0], LANES), F32)
        for h in range(ATT_HEADS_PER_GROUP):
            lo = h * ATT_HEAD_DIM
            out = jnp.where(lane == h, l[:, lo:lo + 1], out)
            delta = jnp.sum(prod[:, lo:lo + ATT_HEAD_DIM], axis=-1, keepdims=True)
            out = jnp.where(lane == ATT_HEADS_PER_GROUP + h, delta, out)
        return out

    rows = [(t, ATT_MERGED, 0) for t in (datt, att, lse)]
    return _rowcall(fn, rows, [], [(LANES, F32)], n_rows=n_rows, tm=512, name=name)[0]


def _dil_bwd(q, k, v, datt, stats, dil, *, name):
    n_rows = datt.shape[0]
    n_blk = n_rows // dil // ATT_BLK
    span = ATT_BLK * dil
    cur = pl.BlockSpec((ATT_BLK, ATT_MERGED), lambda n, r: (n, r))
    prev = pl.BlockSpec((ATT_BLK, ATT_MERGED), lambda n, r: (jnp.maximum(n - 1, 0), r))
    nxt = pl.BlockSpec((ATT_BLK, ATT_MERGED), lambda n, r: (jnp.minimum(n + 1, n_blk - 1), r))
    seq = lambda half, ahead: pl.BlockSpec((span, LANES), lambda n, r: (jnp.minimum(n + ahead, n_blk - 1), half))

    def body(qc_ref, qn_ref, kp_ref, kc_ref, vp_ref, vc_ref, dc0_ref, dc1_ref, dn0_ref, dn1_ref, sc_ref, sn_ref,
             dq0_ref, dq1_ref, dk0_ref, dk1_ref, dv0_ref, dv1_ref):
        n = pl.program_id(0)
        rows = slice(None) if dil == 1 else _strided_rows(pl.program_id(1), ATT_BLK, dil)

        def read(ref0, ref1):
            return jnp.concatenate([ref0[rows, :], ref1[rows, :]], axis=1)

        def write(ref0, ref1, val):
            ref0[rows, :] = val[:, :LANES]
            ref1[rows, :] = val[:, LANES:]

        masks = _head_masks(ATT_BLK)
        valid = _band_mask(jnp.where(n > 0, 0, ATT_BLK))
        qi = lax.broadcasted_iota(jnp.int32, (HEAD_ROWS, ATT_BLK), 0) & (ATT_BLK - 1)
        ki = lax.broadcasted_iota(jnp.int32, (HEAD_ROWS, ATT_BLK), 1)
        valid_next = (ki - qi) >= jnp.where(n < n_blk - 1, 0, ATT_BLK)

        kc, vc = kc_ref[...], vc_ref[...]
        keys = jnp.concatenate([kp_ref[...], kc], axis=0)
        vals = jnp.concatenate([vp_ref[...], vc], axis=0)
        q4 = _stack_heads(qc_ref[...], masks)
        d4 = _stack_heads(read(dc0_ref, dc1_ref).astype(MXU_DTYPE), masks)
        st = sc_ref[rows, :]
        p = jnp.where(valid, jnp.exp(_dot(q4, keys, 1, 1) * ATT_SCALE - _head_column(st, 0)), 0.0)
        ds = p * (_dot(d4, vals, 1, 1) - _head_column(st, ATT_HEADS_PER_GROUP)) * ATT_SCALE
        write(dq0_ref, dq1_ref, _unstack_heads(_dot(ds, keys, 1, 0), masks))

        q4n = _stack_heads(qn_ref[...], masks)
        d4n = _stack_heads(read(dn0_ref, dn1_ref).astype(MXU_DTYPE), masks)
        stn = sn_ref[rows, :]
        p_n = jnp.where(valid_next, jnp.exp(_dot(q4n, kc, 1, 1) * ATT_SCALE - _head_column(stn, 0)), 0.0)
        ds_n = p_n * (_dot(d4n, vc, 1, 1) - _head_column(stn, ATT_HEADS_PER_GROUP)) * ATT_SCALE
        write(dv0_ref, dv1_ref, _dot(p[:, ATT_BLK:], d4, 0, 0) + _dot(p_n, d4n, 0, 0))
        write(dk0_ref, dk1_ref, _dot(ds[:, ATT_BLK:], q4, 0, 0) + _dot(ds_n, q4n, 0, 0))

    shape = jax.ShapeDtypeStruct((n_rows, LANES), F32)
    out = seq(0, 0)
    res = pl.pallas_call(
        body, name=name, grid=(n_blk, dil),
        in_specs=[cur, nxt, prev, cur, prev, cur, seq(0, 0), seq(1, 0), seq(0, 1), seq(1, 1), seq(0, 0), seq(0, 1)],
        out_specs=[out] * 6, out_shape=[shape] * 6, compiler_params=_params("parallel", "arbitrary"),
    )(q, q, k, k, v, v, datt, datt, datt, datt, stats, stats)
    return [(res[2 * i], res[2 * i + 1]) for i in range(3)]


def _dproj_assemble(du, dqkv, dgs, dga, cos_t, sin_t, *, name):
    n_g = len(DILATIONS)

    def fn(*t):
        n_half = LANE_HALVES * 3 * n_g
        du_t, halves, (dgs_t, dga_t, c, s) = t[0], t[1:1 + n_half], t[1 + n_half:]
        parts = [jnp.concatenate(halves[LANE_HALVES * i:LANE_HALVES * (i + 1)], axis=1) for i in range(3 * n_g)]
        for i in range(2 * n_g):
            parts[i] = _rope_transpose(parts[i], c, s)
        cast = [p.astype(MXU_DTYPE) for p in parts]
        return [jnp.concatenate([du_t] + cast + [dgs_t, dga_t], axis=1)] + [_colsum(p) for p in parts]

    rows = [(du, SSM_WIDTH, 0)]
    rows += [(half, LANES, 0) for i in range(3) for g in range(n_g) for half in dqkv[g][i]]
    rows += [(dgs, D_MODEL, 0), (dga, D_MODEL, 0), (cos_t, ATT_MERGED, 0), (sin_t, ATT_MERGED, 0)]
    width = SSM_WIDTH + 3 * n_g * ATT_MERGED + 2 * D_MODEL
    res = _rowcall(fn, rows, [], [(width, MXU_DTYPE)], [ATT_MERGED] * (3 * n_g), n_rows=du.shape[0], tm=ROW_TILE, name=name)
    return res[0], res[1:]


def _xhead(h):
    return slice(h * XATT_HEAD_DIM, (h + 1) * XATT_HEAD_DIM)


def _xatt_probs(qh, kh):
    s = _dot(qh, kh, 1, 1) * XATT_SCALE
    e = jnp.exp(s - jnp.max(s, axis=-1, keepdims=True))
    return e / jnp.sum(e, axis=-1, keepdims=True)


def _xatt_fwd(q, kv, *, name, tm=512):
    n_rows = q.shape[0]
    n_mem = kv.shape[0]

    def body(q_ref, kv_ref, o_ref):
        for h in range(XATT_HEADS):
            sl = _xhead(h)
            p = _xatt_probs(q_ref[:, sl], kv_ref[:, sl])
            o_ref[:, sl] = _dot(p, kv_ref[:, D_MODEL + h * XATT_HEAD_DIM:D_MODEL + (h + 1) * XATT_HEAD_DIM], 1, 0
                                ).astype(o_ref.dtype)

    row = pl.BlockSpec((tm, D_MODEL), lambda i: (i, 0))
    return pl.pallas_call(
        body, name=name, grid=(n_rows // tm,),
        in_specs=[row, pl.BlockSpec((n_mem, 2 * D_MODEL), lambda i: (0, 0))], out_specs=row,
        out_shape=jax.ShapeDtypeStruct((n_rows, D_MODEL), MXU_DTYPE), compiler_params=_params("parallel"),
    )(q, kv)


def _xatt_bwd(q, kv, do, *, name, tm=512):
    n_rows = q.shape[0]
    n_mem = kv.shape[0]

    def body(q_ref, kv_ref, do_ref, dq_ref, dkv_ref):
        @pl.when(pl.program_id(0) == 0)
        def _():
            dkv_ref[...] = jnp.zeros_like(dkv_ref)

        for h in range(XATT_HEADS):
            sl = _xhead(h)
            vsl = slice(D_MODEL + h * XATT_HEAD_DIM, D_MODEL + (h + 1) * XATT_HEAD_DIM)
            qh, kh, doh = q_ref[:, sl], kv_ref[:, sl], do_ref[:, sl]
            p = _xatt_probs(qh, kh)
            dp = _dot(doh, kv_ref[:, vsl], 1, 1)
            ds = p * (dp - jnp.sum(dp * p, axis=-1, keepdims=True)) * XATT_SCALE
            dq_ref[:, sl] = _dot(ds, kh, 1, 0).astype(dq_ref.dtype)
            dkv_ref[:, sl] += _dot(ds, qh, 0, 0)
            dkv_ref[:, vsl] += _dot(p, doh, 0, 0)

    row = pl.BlockSpec((tm, D_MODEL), lambda i: (i, 0))
    full = pl.BlockSpec((n_mem, 2 * D_MODEL), lambda i: (0, 0))
    return pl.pallas_call(
        body, name=name, grid=(n_rows // tm,), in_specs=[row, full, row], out_specs=[row, full],
        out_shape=[jax.ShapeDtypeStruct((n_rows, D_MODEL), MXU_DTYPE), jax.ShapeDtypeStruct((n_mem, 2 * D_MODEL), F32)],
        compiler_params=_params("arbitrary"),
    )(q, kv, do)


def _disc(logdt, a_re, a_im, b_re, b_im):
    dt = jnp.exp(logdt)
    mag = jnp.exp(a_re * dt)
    ab_re = mag * jnp.cos(a_im * dt)
    ab_im = mag * jnp.sin(a_im * dt)
    den = jnp.square(a_re) + jnp.square(a_im)
    nr = ab_re - 1.0
    f_re = (nr * a_re + ab_im * a_im) / den
    f_im = (ab_im * a_re - nr * a_im) / den
    bb_re = f_re[None] * b_re - f_im[None] * b_im
    bb_im = f_re[None] * b_im + f_im[None] * b_re
    return ab_re, ab_im, bb_re, bb_im


def _disc_transpose(logdt, a_re, a_im, b_re, b_im, g_ab_re, g_ab_im, g_bb_re, g_bb_im):
    dt = jnp.exp(logdt)
    mag = jnp.exp(a_re * dt)
    th = a_im * dt
    cs, sn = jnp.cos(th), jnp.sin(th)
    ab_re, ab_im = mag * cs, mag * sn
    den = jnp.square(a_re) + jnp.square(a_im)
    nr = ab_re - 1.0
    f_re = (nr * a_re + ab_im * a_im) / den
    f_im = (ab_im * a_re - nr * a_im) / den
    d_f_re = jnp.sum(g_bb_re * b_re + g_bb_im * b_im, axis=0)
    d_f_im = jnp.sum(g_bb_im * b_re - g_bb_re * b_im, axis=0)
    d_b_re = g_bb_re * f_re[None] + g_bb_im * f_im[None]
    d_b_im = g_bb_im * f_re[None] - g_bb_re * f_im[None]
    d_n_re, d_n_im = d_f_re / den, d_f_im / den
    d_den = -(d_f_re * f_re + d_f_im * f_im) / den
    d_ab_re = g_ab_re + d_n_re * a_re - d_n_im * a_im
    d_ab_im = g_ab_im + d_n_re * a_im + d_n_im * a_re
    d_a_re = d_n_re * nr + d_n_im * ab_im + 2.0 * d_den * a_re
    d_a_im = d_n_re * ab_im - d_n_im * nr + 2.0 * d_den * a_im
    d_mag = d_ab_re * cs + d_ab_im * sn
    d_th = mag * (d_ab_im * cs - d_ab_re * sn)
    d_a_re = d_a_re + d_mag * mag * dt
    d_a_im = d_a_im + d_th * dt
    d_dt = jnp.sum(d_mag * mag * a_re + d_th * a_im, axis=-1, keepdims=True)
    return d_dt * dt, d_a_re, d_a_im, d_b_re, d_b_im


def _full_spec(shape):
    return pl.BlockSpec(tuple(shape), functools.partial(lambda i, nd: (0,) * nd, nd=len(shape)))


def _whole(fn, args, out_shapes, *, name):
    n_in = len(args)

    def body(*refs):
        res = fn(*[r[...] for r in refs[:n_in]])
        for o_ref, val in zip(refs[n_in:], res):
            o_ref[...] = val

    return pl.pallas_call(
        body, name=name, grid=(1,), in_specs=[_full_spec(t.shape) for t in args],
        out_specs=[_full_spec(s) for s in out_shapes], out_shape=[jax.ShapeDtypeStruct(s, F32) for s in out_shapes],
        compiler_params=_params("arbitrary"))(*args)


SSM_WIDE = GROUPS_PER_TILE * SSM_STATE
LANE_GROUPS_PER_TILE = SSM_WIDE // LANES


def _chan(j):
    return slice(j * LANES, (j + 1) * LANES)


def _time_major_rows(j, q, tc):
    return pl.ds(j * LANE_GROUPS_PER_TILE + q, tc, stride=STATE_VREG_ROWS)


def _to_time_major(x, t_re_ref, t_im_ref, dst_re, dst_im, tc):
    for j in range(SSM_TILES):
        xj = x[:, _chan(j)]
        for t_ref, dst in ((t_re_ref, dst_re), (t_im_ref, dst_im)):
            r = _dot(xj, t_ref[j], 1, 0)
            for q in range(LANE_GROUPS_PER_TILE):
                dst[_time_major_rows(j, q, tc), :] = r[:, q * LANES:(q + 1) * LANES]


def _from_time_major(src, j, tc):
    return jnp.concatenate([src[_time_major_rows(j, q, tc), :] for q in range(LANE_GROUPS_PER_TILE)], axis=1)


def _scan_chunk(w_re, w_im, h_re, h_im, a_re, a_im, start, tc):
    def step(t, carry):
        hr, hi = carry
        rows = _scan_rows(t)
        nr = a_re * hr - a_im * hi + w_re[rows, :]
        ni = a_re * hi + a_im * hr + w_im[rows, :]
        h_re[rows, :] = nr
        h_im[rows, :] = ni
        return nr, ni

    return lax.fori_loop(0, tc, step, start, unroll=8)


SSM_CHUNK = 256


def _tile_spec(stack, k):
    return pl.BlockSpec((pl.Squeezed(),) + tuple(stack.shape[1:]), lambda i: (k, 0, 0, 0))


def _expand_block_diagonal(src_ref, dst):
    dst[...] = jnp.zeros_like(dst)
    r, c = src_ref.shape[1:]
    for g in range(SSM_GROUPS):
        j, gl = divmod(g, GROUPS_PER_TILE)
        dst[j, gl * r:(gl + 1) * r, gl * c:(gl + 1) * c] = src_ref[g].astype(dst.dtype)


def _extract_block_diagonal(src, dst_ref):
    r, c = dst_ref.shape[1:]
    for g in range(SSM_GROUPS):
        j, gl = divmod(g, GROUPS_PER_TILE)
        dst_ref[g] = src[j, gl * r:(gl + 1) * r, gl * c:(gl + 1) * c]


def _ssm_fwd(proj, blocks_cn, blocks_nc, a_re, a_im, gain, *, name, tc=SSM_CHUNK):
    n_rows = proj.shape[0]
    n_chunk = n_rows // tc

    def body(u_ref, br_ref, bi_ref, cr_ref, ci_ref, ar_ref, ai_ref, g_ref, y_ref, gy_ref, hr, hi, wr, wi, state,
             tbr_ref, tbi_ref, tcr_ref, tci_ref):
        @pl.when(pl.program_id(0) == 0)
        def _():
            state[...] = jnp.zeros_like(state)
            for src_ref, dst in ((br_ref, tbr_ref), (bi_ref, tbi_ref), (cr_ref, tcr_ref), (ci_ref, tci_ref)):
                _expand_block_diagonal(src_ref, dst)

        u = u_ref[...]
        _to_time_major(u, tbr_ref, tbi_ref, wr, wi, tc)
        state[0], state[1] = _scan_chunk(wr, wi, hr, hi, ar_ref[...], ai_ref[...], (state[0], state[1]), tc)
        for j in range(SSM_TILES):
            yj = (_dot(_from_time_major(hr, j, tc), tcr_ref[j], 1, 0) + _dot(_from_time_major(hi, j, tc), tci_ref[j], 1, 0)
                  + g_ref[:, _chan(j)] * u[:, _chan(j)])
            y_ref[:, _chan(j)] = yj
            gy_ref[:, _chan(j)] = jax.nn.gelu(yj).astype(gy_ref.dtype)

    rows = pl.BlockSpec((tc, SSM_WIDTH), lambda i: (i, 0))
    coef = pl.BlockSpec((STATE_VREG_ROWS, LANES), lambda i: (0, 0))
    states = pl.BlockSpec((tc * STATE_VREG_ROWS, LANES), lambda i: (i, 0))
    sshape = jax.ShapeDtypeStruct((n_rows * STATE_VREG_ROWS, LANES), F32)
    return pl.pallas_call(
        body, name=name, grid=(n_chunk,),
        in_specs=[rows, _tile_spec(blocks_cn, 0), _tile_spec(blocks_cn, 1), _tile_spec(blocks_nc, 0),
                  _tile_spec(blocks_nc, 1), coef, coef, pl.BlockSpec((1, SSM_WIDTH), lambda i: (0, 0))],
        out_specs=[rows, rows, states, states],
        out_shape=[jax.ShapeDtypeStruct((n_rows, SSM_WIDTH), F32), jax.ShapeDtypeStruct((n_rows, SSM_WIDTH), MXU_DTYPE),
                   sshape, sshape],
        scratch_shapes=[pltpu.VMEM((tc * STATE_VREG_ROWS, LANES), F32)] * 2 + [pltpu.VMEM((2, STATE_VREG_ROWS, LANES), F32)]
        + [pltpu.VMEM((SSM_TILES, LANES, SSM_WIDE), MXU_DTYPE)] * 2 + [pltpu.VMEM((SSM_TILES, SSM_WIDE, LANES), MXU_DTYPE)] * 2,
        compiler_params=_params("arbitrary"),
    )(proj, blocks_cn, blocks_cn, blocks_nc, blocks_nc, a_re, a_im, gain)


def _ssm_bwd(proj, dy, h_re, h_im, blocks_cn, blocks_nc, a_re, a_im, gain, *, name, tc=SSM_CHUNK):
    n_rows = proj.shape[0]
    n_chunk = n_rows // tc

    def body(u_ref, dy_ref, hr, hi, cr_ref, ci_ref, br_ref, bi_ref, ar_ref, ai_ref, g_ref,
             du_ref, su_ref, dc_re_ref, dc_im_ref, db_re_ref, db_im_ref, dar_ref, dai_ref, wr, wi, carry,
             tdr_ref, tdi_ref, tur_ref, tui_ref, dcr_ref, dci_ref, dbr_ref, dbi_ref):
        @pl.when(pl.program_id(0) == 0)
        def _():
            carry[...] = jnp.zeros_like(carry)
            for acc_ref in (su_ref, dcr_ref, dci_ref, dbr_ref, dbi_ref):
                acc_ref[...] = jnp.zeros_like(acc_ref)
            for src_ref, dst in ((cr_ref, tdr_ref), (ci_ref, tdi_ref), (br_ref, tur_ref), (bi_ref, tui_ref)):
                _expand_block_diagonal(src_ref, dst)

        a_r, a_i = ar_ref[...], ai_ref[...]
        u, dyv = u_ref[...], dy_ref[...]
        _to_time_major(dyv, tdr_ref, tdi_ref, wr, wi, tc)

        def step(kk, c):
            lam_r, lam_i, dar, dai = c
            rows = _scan_rows(tc - 1 - kk)
            h_r, h_i = hr[rows, :], hi[rows, :]
            dar = dar + lam_r * h_r + lam_i * h_i
            dai = dai + lam_i * h_r - lam_r * h_i
            new_r = wr[rows, :] + a_r * lam_r + a_i * lam_i
            new_i = wi[rows, :] + a_r * lam_i - a_i * lam_r
            wr[rows, :] = new_r
            wi[rows, :] = new_i
            return new_r, new_i, dar, dai

        carry[0], carry[1], carry[2], carry[3] = lax.fori_loop(0, tc, step, (carry[0], carry[1], carry[2], carry[3]),
                                                              unroll=8)
        dar_ref[...] = carry[2]
        dai_ref[...] = carry[3]
        for j in range(SSM_TILES):
            cj = _chan(j)
            lam_r, lam_i = _from_time_major(wr, j, tc), _from_time_major(wi, j, tc)
            dcr_ref[j] += _dot(dyv[:, cj], _from_time_major(hr, j, tc), 0, 0)
            dci_ref[j] += _dot(dyv[:, cj], _from_time_major(hi, j, tc), 0, 0)
            dbr_ref[j] += _dot(u[:, cj], lam_r, 0, 0)
            dbi_ref[j] += _dot(u[:, cj], lam_i, 0, 0)
            duj = _dot(lam_r, tur_ref[j], 1, 0) + _dot(lam_i, tui_ref[j], 1, 0) + g_ref[:, cj] * dyv[:, cj]
            du_ref[:, cj] = duj.astype(du_ref.dtype)
            su_ref[:, cj] += _colsum(duj)

        @pl.when(pl.program_id(0) == n_chunk - 1)
        def _():
            for src, dst_ref in ((dcr_ref, dc_re_ref), (dci_ref, dc_im_ref), (dbr_ref, db_re_ref), (dbi_ref, db_im_ref)):
                _extract_block_diagonal(src, dst_ref)

    back = lambda i: (n_chunk - 1 - i, 0)
    rows = pl.BlockSpec((tc, SSM_WIDTH), back)
    blocks = pl.BlockSpec((SSM_GROUPS, SSM_GROUP, SSM_STATE), lambda i: (0, 0, 0))
    coef = pl.BlockSpec((STATE_VREG_ROWS, LANES), lambda i: (0, 0))
    states = pl.BlockSpec((tc * STATE_VREG_ROWS, LANES), back)
    vec = pl.BlockSpec((1, SSM_WIDTH), lambda i: (0, 0))
    bshape = jax.ShapeDtypeStruct((SSM_GROUPS, SSM_GROUP, SSM_STATE), F32)
    cshape = jax.ShapeDtypeStruct((STATE_VREG_ROWS, LANES), F32)
    return pl.pallas_call(
        body, name=name, grid=(n_chunk,),
        in_specs=[rows, rows, states, states, _tile_spec(blocks_cn, 2), _tile_spec(blocks_cn, 3), _tile_spec(blocks_nc, 2),
                  _tile_spec(blocks_nc, 3), coef, coef, vec],
        out_specs=[rows, vec, blocks, blocks, blocks, blocks, coef, coef],
        out_shape=[jax.ShapeDtypeStruct((n_rows, SSM_WIDTH), MXU_DTYPE), jax.ShapeDtypeStruct((1, SSM_WIDTH), F32),
                   bshape, bshape, bshape, bshape, cshape, cshape],
        scratch_shapes=[pltpu.VMEM((tc * STATE_VREG_ROWS, LANES), F32)] * 2 + [pltpu.VMEM((4, STATE_VREG_ROWS, LANES), F32)]
        + [pltpu.VMEM((SSM_TILES, LANES, SSM_WIDE), MXU_DTYPE)] * 2 + [pltpu.VMEM((SSM_TILES, SSM_WIDE, LANES), MXU_DTYPE)] * 2
        + [pltpu.VMEM((SSM_TILES, LANES, SSM_WIDE), F32)] * 4,
        compiler_params=_params("arbitrary"),
    )(proj, dy, h_re, h_im, blocks_cn, blocks_cn, blocks_nc, blocks_nc, a_re, a_im, gain)


def _scan_rows(t):
    return pl.ds(pl.multiple_of(t * STATE_VREG_ROWS, 8), STATE_VREG_ROWS)


GATHER_GROUPS = (("w_glu", "w_att_up", "w_mix_out"), ("w_xq", "w_xkv", "w_xo", "w_ff1", "w_ff2"))
SCATTER_GROUPS = (("w_ff2", "w_ff1"), ("w_xo", "w_xq", "w_xkv", "w_mix_out"), ("w_att_up", "w_glu"), ("w_in",))


def _local_grads(x, mem, pos_col, target, sm, fetch_in, fetch, send, send_small, start_token):
    b_re_t = sm["ssm_b_re"].transpose(2, 0, 1)
    b_im_t = sm["ssm_b_im"].transpose(2, 0, 1)
    logdt = sm["ssm_log_dt"].reshape(SSM_GROUPS, 1)
    c_re, c_im = sm["ssm_c_re"], sm["ssm_c_im"]
    grp = (SSM_GROUPS, SSM_STATE)
    chn = (SSM_GROUP, SSM_GROUPS, SSM_STATE)

    wts = {}
    cos_t, sin_t = _rope_tables(pos_col, after=start_token, name="rope_tables")
    h0, xh0, rs0, h0m = _ln_fwd(x, sm["ln_in_g"], sm["ln_in_b"], name="ln_in_fwd")
    disc_in = (logdt, sm["ssm_a_re"], sm["ssm_a_im"], b_re_t, b_im_t)
    ab_re, ab_im, bb_re_t, bb_im_t = _whole(_disc, disc_in, [grp, grp, chn, chn], name="ssm_disc")
    a_re_rows, a_im_rows = ab_re.reshape(STATE_VREG_ROWS, LANES), ab_im.reshape(STATE_VREG_ROWS, LANES)
    tiles_cn = jnp.stack([bb_re_t.transpose(1, 0, 2), bb_im_t.transpose(1, 0, 2), c_re, -c_im])
    tiles_nc = jnp.stack([c_re.transpose(0, 2, 1), -c_im.transpose(0, 2, 1), bb_re_t.transpose(1, 2, 0),
                          bb_im_t.transpose(1, 2, 0)])
    w_in_near, near_ids = fetch_in(0, [h0m, tiles_cn, tiles_nc])
    proj = _mm_shards(h0m, w_in_near, sm["b_in"], near_ids, name="in_proj_near")
    wts["w_in"], far_ids = fetch_in(1, [proj])
    proj = _mm_shards(h0m, wts["w_in"], sm["b_in"], far_ids, prev=proj, name="in_proj_far")

    y, gy, h_re, h_im = _ssm_fwd(proj, tiles_cn, tiles_nc, a_re_rows, a_im_rows, sm["ssm_d"], name="ssm_fwd")

    q, k, v = _qkv_split(proj, cos_t, sin_t, name="qkv_split")
    outs, lses = [], []
    for g, dil in enumerate(DILATIONS):
        o_g, l_g = _dil_fwd(q[g], k[g], v[g], dil, name=f"dil_att_fwd_{dil}")
        outs.append(o_g)
        lses.append(l_g)
    att, lse = _att_merge(outs, lses, name="att_merge")
    wts.update(fetch(0, [att]))
    z = _mm(gy, wts["w_glu"], bias=sm["b_glu"], b_shards=True, name="glu_proj")
    b_att = _mm(att, wts["w_att_up"], b_shards=True, name="att_up")

    mixed, h1, xh1, rs1, h1m = _mix_out_ln(proj, z, b_att, wts["w_mix_out"], sm["b_mix_out"], h0, sm["ln1_g"],
                                           sm["ln1_b"], alpha=DEEPNORM_ALPHA, name="gate_mix_out_ln1")

    wts.update(fetch(1, [h1m]))
    xq = _mm(h1m, wts["w_xq"], out_dtype=MXU_DTYPE, name="xatt_q")
    kv = _mm(mem, wts["w_xkv"], out_dtype=MXU_DTYPE, b_shards=True, name="xatt_kv")
    xo_in = _xatt_fwd(xq, kv, name="xatt_fwd")
    h2, xh2, rs2, h2m = _mm_ln_fwd(xo_in, wts["w_xo"], None, h1, sm["ln2_g"], sm["ln2_b"], alpha=DEEPNORM_ALPHA,
                                   name="xatt_o_ln2")

    pre, act = _mm(h2m, wts["w_ff1"], bias=sm["b_ff1"], b_shards=True, name="ff1",
                   also=(lambda r: jnp.square(jnp.maximum(r, 0.0)), MXU_DTYPE))
    ff = _mm(act, wts["w_ff2"], bias=sm["b_ff2"], name="ff2")

    gw, gs = {}, {}
    dr3, dr3m, gs["ln3_g"], gs["ln3_b"], gs["b_ff2"], loss_row = _ln_loss_bwd(
        h2, ff, target, sm["ln3_g"], sm["ln3_b"], alpha=DEEPNORM_ALPHA, name="ln3_loss")
    wgrad = functools.partial(_mm, ta=True, out_dtype=WIRE_DTYPE, tk=2048)
    gw["w_ff2"] = wgrad(act, dr3m, tk=1024, name="ff2_dw")
    dpre, gs["b_ff1"] = _mm(dr3m, wts["w_ff2"], tb=True, out_dtype=MXU_DTYPE, colsum=True, name="ff2_dx",
                            gate=(pre, lambda p: 2.0 * jnp.maximum(p, 0.0)))
    gw["w_ff1"] = wgrad(h2m, dpre, out_shards=True, name="ff1_dw")
    sent = send(0, gw)
    dh2 = _mm(dpre, wts["w_ff1"], tb=True, b_shards=True, after=sent, name="ff1_dx")
    dr2, dr2m, gs["ln2_g"], gs["ln2_b"], _ = _ln_bwd(dr3, dh2, xh2, rs2, sm["ln2_g"], alpha=DEEPNORM_ALPHA,
                                                     name="ln2_bwd")
    gw["w_xo"] = wgrad(xo_in, dr2m, name="xatt_o_dw")
    dxo_in = _mm(dr2m, wts["w_xo"], tb=True, out_dtype=MXU_DTYPE, name="xatt_o_dx")
    dxq, dkv = _xatt_bwd(xq, kv, dxo_in, name="xatt_bwd")
    gw["w_xq"] = wgrad(h1m, dxq, name="xatt_q_dw")
    gw["w_xkv"] = wgrad(mem, dkv, out_shards=True, name="xatt_kv_dw")
    dr1, dr1m, gs["ln1_g"], gs["ln1_b"], gs["b_mix_out"] = _mm_ln_bwd(
        dxq, wts["w_xq"], dr2, xh1, rs1, sm["ln1_g"], alpha=DEEPNORM_ALPHA, name="xatt_q_dx_ln1")
    gw["w_mix_out"] = wgrad(mixed, dr1m, name="mix_out_dw")
    sent = send(1, gw)
    dmixed = _mm(dr1m, wts["w_mix_out"], tb=True, after=sent, name="mix_out_dx")
    dgs, dga, dz, db_att, s_gs, s_ga, gs["b_glu"] = _mix_bwd(dmixed, proj, z, b_att, name="gate_mix_bwd")

    gw["w_att_up"] = wgrad(att, db_att, out_shards=True, name="att_up_dw")
    gw["w_glu"] = wgrad(gy, dz, out_shards=True, name="glu_dw")
    sent = send(2, gw)
    datt = _mm(db_att, wts["w_att_up"], tb=True, b_shards=True, after=sent, name="att_up_dx")
    stats = _att_stats(datt, att, lse, name="att_stats")
    dqkv = [_dil_bwd(q[g], k[g], v[g], datt, stats, dil, name=f"dil_att_bwd_{dil}") for g, dil in enumerate(DILATIONS)]

    dgy = _mm(dz, wts["w_glu"], tb=True, b_shards=True, name="glu_dx")
    dy, gs["ssm_d"] = _gelu_bwd(dgy, y, proj, name="gelu_bwd")
    du, s_u, dc_re_t, dc_im_t, dbb_re_t, dbb_im_t, da_re, da_im = _ssm_bwd(
        proj, dy, h_re, h_im, tiles_cn, tiles_nc, a_re_rows, a_im_rows, sm["ssm_d"], name="ssm_bwd")
    gs["ssm_c_re"], gs["ssm_c_im"] = dc_re_t, -dc_im_t
    disc_ct = (da_re.reshape(grp), da_im.reshape(grp), dbb_re_t.transpose(1, 0, 2), dbb_im_t.transpose(1, 0, 2))
    d_logdt, gs["ssm_a_re"], gs["ssm_a_im"], d_b_re_t, d_b_im_t = _whole(
        _disc_transpose, disc_in + disc_ct, [(SSM_GROUPS, 1), grp, grp, chn, chn], name="ssm_disc_bwd")
    gs["ssm_log_dt"] = d_logdt
    gs["ssm_b_re"], gs["ssm_b_im"] = d_b_re_t.transpose(1, 2, 0), d_b_im_t.transpose(1, 2, 0)

    dproj, s_qkv = _dproj_assemble(du, dqkv, dgs, dga, cos_t, sin_t, name="dproj_assemble")
    gs["b_in"] = jnp.concatenate([s_u, *s_qkv, s_gs, s_ga], axis=1)
    sent = send_small(gs, SMALL_EARLY)
    gw["w_in"] = wgrad(h0m, dproj, out_shards=True, after=sent, name="in_proj_dw")
    sent = send(3, gw)
    dh0 = _mm(dproj, wts["w_in"], tb=True, b_shards=True, after=sent, name="in_proj_dx")
    grad_x, gs["ln_in_g"], gs["ln_in_b"], _ = _ln_bwd(dr1, dh0, xh0, rs0, sm["ln_in_g"], alpha=DEEPNORM_ALPHA,
                                                      operand=False, name="ln_in_bwd")
    return loss_row, grad_x, gs


_IN_HBM = pl.BlockSpec(memory_space=pltpu.HBM)
_IN_SEMAPHORE = pl.BlockSpec(memory_space=pltpu.SEMAPHORE)


def _device_index():
    return 4 * lax.axis_index("x") + 2 * lax.axis_index("y") + lax.axis_index("c")


ALL_PEERS = tuple(range(1, N_DEV))
NEAR_PEERS = (1, 2, 3, 4, 5)
FAR_PEERS = (6, 7)


def _peer_index(kk):
    x, y, c = lax.axis_index("x"), lax.axis_index("y"), lax.axis_index("c")
    return 4 * ((x + (kk >> 2)) % 2) + 2 * ((y + ((kk >> 1) & 1)) % 2) + (c + (kk & 1)) % 2


def _exchange_copies(src_refs, land_refs, send_sems, recv_sems, scatter, peers):
    x, y, c = lax.axis_index("x"), lax.axis_index("y"), lax.axis_index("c")
    me = 4 * x + 2 * y + c
    pairs = []
    for a, (src_ref, land_ref) in enumerate(zip(src_refs, land_refs)):
        for idx, kk in enumerate(peers):
            px = (x + (kk >> 2)) % 2
            py = (y + ((kk >> 1) & 1)) % 2
            pc = (c + (kk & 1)) % 2
            peer = 4 * px + 2 * py + pc
            sem = a * len(peers) + idx
            src = src_ref.at[peer] if scatter else src_ref

            def copy(dst, src=src, sem=sem, px=px, py=py, pc=pc):
                return pltpu.make_async_remote_copy(
                    src_ref=src, dst_ref=dst, send_sem=send_sems.at[sem], recv_sem=recv_sems.at[sem],
                    device_id=(px, py, pc), device_id_type=pl.DeviceIdType.MESH)

            pairs.append((functools.partial(copy, land_ref.at[me]), functools.partial(copy, land_ref.at[peer])))
    return pairs


def _own_copies(src_refs, land_refs, own_sems, scatter):
    me = _device_index()
    return [functools.partial(pltpu.make_async_copy, src_ref.at[me] if scatter else src_ref, land_ref.at[me],
                              own_sems.at[a]) for a, (src_ref, land_ref) in enumerate(zip(src_refs, land_refs))]


def _exchange_start(srcs, *, scatter, name, after=None, peers=ALL_PEERS, lands=None):
    n_arr, n_sem = len(srcs), len(srcs) * len(peers)
    own = lands is None
    if own:
        lands = [lax.empty((N_DEV,) + tuple(s.shape[1:] if scatter else s.shape), s.dtype) for s in srcs]
    n_in = 2 * n_arr + (after is not None)

    def body(*refs):
        send_sems, recv_sems = refs[n_in], refs[n_in + 1]
        for sent, _ in _exchange_copies(refs[:n_arr], refs[n_arr:2 * n_arr], send_sems, recv_sems, scatter, peers):
            sent().start()
        if own:
            for local in _own_copies(refs[:n_arr], refs[n_arr:2 * n_arr], refs[n_in + 2], scatter):
                local().start()
        refs[-1][...] = jnp.zeros_like(refs[-1])

    sems = [pltpu.SemaphoreType.DMA((n_sem,)), pltpu.SemaphoreType.DMA((n_sem,))] + [pltpu.SemaphoreType.DMA((n_arr,))] * own
    through = [pltpu.HBM(t.shape, t.dtype) for t in (*srcs, *lands)]
    res = pl.pallas_call(
        body, name=name, out_shape=(*sems, *through, jax.ShapeDtypeStruct((8, LANES), F32)),
        in_specs=[_IN_HBM] * (2 * n_arr) + [pl.BlockSpec(memory_space=pl.ANY)] * (after is not None),
        out_specs=(*[_IN_SEMAPHORE] * len(sems), *[_IN_HBM] * (2 * n_arr), pl.BlockSpec(memory_space=pltpu.VMEM)),
        input_output_aliases={i: len(sems) + i for i in range(2 * n_arr)},
        compiler_params=pltpu.CompilerParams(has_side_effects=pltpu.SideEffectType.DATAFLOW_SIDE_EFFECTING),
    )(*[pltpu.with_memory_space_constraint(t, pltpu.HBM) for t in (*srcs, *lands)],
      *([after] if after is not None else []))
    first = len(sems)
    handle = dict(sems=res[:first], srcs=res[first:first + n_arr], lands=res[first + n_arr:first + 2 * n_arr],
                  scatter=scatter, peers=peers, own=own)
    return handle, res[-1]


def _exchange_wait(handle, *, after, name, srcs=None, lands=None):
    srcs = handle["srcs"] if srcs is None else srcs
    lands = handle["lands"] if lands is None else lands
    sems, scatter, peers, own = handle["sems"], handle["scatter"], handle["peers"], handle["own"]
    n_arr = len(srcs)
    after = list(after)

    def body(*refs):
        src_refs, land_refs = refs[:n_arr], refs[n_arr:2 * n_arr]
        for sent, received in _exchange_copies(src_refs, land_refs, refs[2 * n_arr], refs[2 * n_arr + 1], scatter, peers):
            sent().wait_send()
            received().wait_recv()
        if own:
            for local in _own_copies(src_refs, land_refs, refs[2 * n_arr + 2], scatter):
                local().wait()

    res = pl.pallas_call(
        body, name=name, out_shape=tuple(pltpu.HBM(t.shape, t.dtype) for t in (*srcs, *lands)),
        in_specs=[_IN_HBM] * (2 * n_arr) + [_IN_SEMAPHORE] * len(sems) + [pl.BlockSpec(memory_space=pl.ANY)] * len(after),
        out_specs=tuple([_IN_HBM] * (2 * n_arr)), input_output_aliases={i: i for i in range(2 * n_arr)},
        compiler_params=pltpu.CompilerParams(has_side_effects=pltpu.SideEffectType.DATAFLOW_SIDE_EFFECTING),
    )(*srcs, *lands, *sems, *after)
    return res[:n_arr], res[n_arr:]


def _adamw(g, w, m, v):
    m_new = ADAM_B1 * m + (1.0 - ADAM_B1) * g
    v_new = ADAM_B2 * v + (1.0 - ADAM_B2) * jnp.square(g)
    m_hat = m_new / (1.0 - ADAM_B1 ** ADAM_STEP)
    v_hat = v_new / (1.0 - ADAM_B2 ** ADAM_STEP)
    return g, -ADAM_LR * (m_hat / (jnp.sqrt(v_hat) + ADAM_EPS) + ADAM_WD * w), m_new, v_new


def _reduce_adamw(gstack, w, m, v, *, name, tr=128):
    n_rows, cols = w.shape
    tr = min(tr, n_rows)
    assert n_rows % tr == 0, (name, n_rows, tr)

    def body(g_ref, w_ref, m_ref, v_ref, *out_refs):
        g = g_ref[0].astype(F32)
        for dev in range(1, N_DEV):
            g = g + g_ref[dev].astype(F32)
        for o_ref, val in zip(out_refs, _adamw(g, w_ref[...], m_ref[...], v_ref[...])):
            o_ref[...] = val

    flat = pl.BlockSpec((tr, cols), lambda i: (i, 0))
    shape = jax.ShapeDtypeStruct((n_rows, cols), F32)
    return pl.pallas_call(
        body, name=name, grid=(n_rows // tr,),
        in_specs=[pl.BlockSpec((N_DEV, tr, cols), lambda i: (0, i, 0)), flat, flat, flat],
        out_specs=[flat] * 4, out_shape=[shape] * 4, compiler_params=_params("parallel"),
    )(gstack, w, m, v)


SMALL_FLAT_SSM = ("ssm_b_re", "ssm_b_im", "ssm_c_re", "ssm_c_im")


def _small_view(name, shape):
    size = int(np.prod(shape))
    if name in SMALL_FLAT_SSM:
        return SSM_GROUPS, size // SSM_GROUPS
    if name in ("ssm_a_re", "ssm_a_im"):
        return SSM_GROUPS, SSM_STATE
    return 1, size


def _pack_rows(view):
    return -(-(view[0] * view[1]) // PACK_COLS)


SMALL_LATE = ("ln_in_g", "ln_in_b")
SMALL_EARLY = tuple(n for n in SMALL if n not in SMALL_LATE)


def _pack_small(gs, names, views):
    parts = []
    for n in names:
        flat = gs[n].reshape(-1).astype(WIRE_DTYPE)
        parts.append(jnp.pad(flat, (0, _pack_rows(views[n]) * PACK_COLS - flat.shape[0])))
    total = sum(p.shape[0] for p in parts) // PACK_COLS
    parts.append(jnp.zeros(((-total % PACK_ROW_ALIGN) * PACK_COLS,), WIRE_DTYPE))
    return jnp.concatenate(parts).reshape(-1, PACK_COLS)


def _small_pieces(view):
    rows, cols = view
    if cols == PACK_COLS:
        return [(0, rows, 0, 0, 0, cols)]
    if rows == 1 and cols > PACK_COLS:
        return [(kk, 1, 0, 0, kk * PACK_COLS, PACK_COLS) for kk in range(cols // PACK_COLS)]
    if rows == 1:
        return [(0, 1, 0, 0, 0, cols)]
    return [((r * cols) // PACK_COLS, 1, (r * cols) % PACK_COLS, r, 0, cols) for r in range(rows)]


def _adamw_small(stacks, views, w, m, v, *, name):
    n = len(SMALL)
    place, first = {}, [0, 0]
    for k, names in enumerate((SMALL_EARLY, SMALL_LATE)):
        for name_ in names:
            place[name_] = (k, first[k])
            first[k] += _pack_rows(views[name_])

    def body(early_ref, late_ref, *refs):
        ins, outs = refs[:3 * n], refs[3 * n:]
        for i, name_ in enumerate(SMALL):
            stack_ref = (early_ref, late_ref)[place[name_][0]]
            row0 = place[name_][1]
            for prow, nrows, lane, orow, ocol, width in _small_pieces(views[name_]):
                src = (slice(row0 + prow, row0 + prow + nrows), slice(lane, lane + width))
                dst = (slice(orow, orow + nrows), slice(ocol, ocol + width))
                g = stack_ref[(0,) + src].astype(F32)
                for dev in range(1, N_DEV):
                    g = g + stack_ref[(dev,) + src].astype(F32)
                res = _adamw(g, ins[i][dst], ins[n + i][dst], ins[2 * n + i][dst])
                for kk, val in enumerate(res):
                    outs[kk * n + i][dst] = val

    args = [*stacks, *[d[name_] for d in (w, m, v) for name_ in SMALL]]
    out_views = [views[name_] for _ in range(4) for name_ in SMALL]
    res = pl.pallas_call(
        body, name=name, grid=(1,), in_specs=[_full_spec(t.shape) for t in args],
        out_specs=[_full_spec(s) for s in out_views], out_shape=[jax.ShapeDtypeStruct(s, F32) for s in out_views],
        compiler_params=_params("arbitrary"),
    )(*args)
    return [dict(zip(SMALL, res[kk * n:(kk + 1) * n])) for kk in range(4)]


def kernel(x, mem, positions, ln_in_g, ln_in_b, w_in, b_in, ssm_log_dt, ssm_a_re, ssm_a_im, ssm_b_re, ssm_b_im, ssm_c_re, ssm_c_im, ssm_d, w_glu, b_glu, w_att_up, w_mix_out, b_mix_out, ln1_g, ln1_b, w_xq, w_xkv, w_xo, ln2_g, ln2_b, w_ff1, b_ff1, w_ff2, b_ff2, ln3_g, ln3_b, loss_target, m_ln_in_g, m_ln_in_b, m_w_in, m_b_in, m_ssm_log_dt, m_ssm_a_re, m_ssm_a_im, m_ssm_b_re, m_ssm_b_im, m_ssm_c_re, m_ssm_c_im, m_ssm_d, m_w_glu, m_b_glu, m_w_att_up, m_w_mix_out, m_b_mix_out, m_ln1_g, m_ln1_b, m_w_xq, m_w_xkv, m_w_xo, m_ln2_g, m_ln2_b, m_w_ff1, m_b_ff1, m_w_ff2, m_b_ff2, m_ln3_g, m_ln3_b, v_ln_in_g, v_ln_in_b, v_w_in, v_b_in, v_ssm_log_dt, v_ssm_a_re, v_ssm_a_im, v_ssm_b_re, v_ssm_b_im, v_ssm_c_re, v_ssm_c_im, v_ssm_d, v_w_glu, v_b_glu, v_w_att_up, v_w_mix_out, v_b_mix_out, v_ln1_g, v_ln1_b, v_w_xq, v_w_xkv, v_w_xo, v_ln2_g, v_ln2_b, v_w_ff1, v_b_ff1, v_w_ff2, v_b_ff2, v_ln3_g, v_ln3_b):
    given = dict(locals())
    w_arg = {n: given[n] for n in WEIGHTS}
    m_arg = {n: given["m_" + n] for n in WEIGHTS}
    v_arg = {n: given["v_" + n] for n in WEIGHTS}

    in_near, token = _exchange_start([w_arg["w_in"][0].astype(MXU_DTYPE)], scatter=False, peers=NEAR_PEERS,
                                     name="gather_start_in_near")
    in_far, token = _exchange_start(in_near["srcs"], scatter=False, peers=FAR_PEERS, lands=in_near["lands"],
                                    after=token, name="gather_start_in_far")
    w_in_state = [in_far["srcs"], in_far["lands"]]
    token, w_arg, m_arg, v_arg = lax.optimization_barrier((token, w_arg, m_arg, v_arg))
    shards = {n: w_arg[n][0].astype(MXU_DTYPE) for n in BIG if n != "w_in"}
    gathers = []
    for i, names in enumerate(GATHER_GROUPS):
        handle, token = _exchange_start([shards[n] for n in names], scatter=False, after=token, name=f"gather_start_{i}")
        gathers.append(handle)

    small_views = {n: _small_view(n, w_arg[n].shape) for n in SMALL}
    small_w, small_m, small_v = [{n: d[n].reshape(small_views[n]) for n in SMALL} for d in (w_arg, m_arg, v_arg)]
    relaid = [d[n] for d in (small_w, small_m, small_v) for n in SMALL_FLAT_SSM]

    def fetch_in(part, after):
        handle, peers, tag = ((in_near, (0,) + NEAR_PEERS, "near"), (in_far, FAR_PEERS, "far"))[part]
        w_in_state[:] = _exchange_wait(handle, after=after + (relaid if part == 0 else []), srcs=w_in_state[0],
                                       lands=w_in_state[1], name="gather_wait_in_" + tag)
        return w_in_state[1][0], jnp.stack([_peer_index(kk) for kk in peers]).astype(jnp.int32)

    def fetch(i, after):
        _, lands = _exchange_wait(gathers[i], after=after, name=f"gather_wait_{i}")
        full = dict(zip(GATHER_GROUPS[i], lands))
        return {n: t if n in BIG_COL_SHARDED else t.reshape(-1, t.shape[-1]) for n, t in full.items()}

    scatters = {}

    def send(i, gw):
        slots = [gw[n] if n in BIG_COL_SHARDED else gw[n].reshape(N_DEV, -1, gw[n].shape[-1]) for n in SCATTER_GROUPS[i]]
        handle, sent = _exchange_start(slots, scatter=True, name=f"scatter_start_{i}")
        scatters[i] = (handle, slots)
        return sent

    sm = {}
    for n in SMALL:
        t = w_arg[n]
        if n.startswith("ssm_") and n not in ("ssm_d", "ssm_log_dt"):
            sm[n] = t[0]
        else:
            sm[n] = t.reshape(1, -1)

    smalls = []

    def send_small(gs, names):
        handle, sent = _exchange_start([_pack_small(gs, names, small_views)], scatter=False,
                                       name=f"small_start_{len(smalls)}")
        smalls.append(handle)
        return sent

    loss_row, grad_x, gs = _local_grads(x[0], mem[0], positions.reshape(-1, 1), loss_target[0], sm, fetch_in, fetch,
                                        send, send_small, token)
    loss = lax.psum(loss_row[0, 0], ("x", "y", "c"))
    send_small(gs, SMALL_LATE)

    results = [{}, {}, {}, {}]
    done = grad_x
    for i, names in enumerate(SCATTER_GROUPS):
        handle, slots = scatters[i]
        _, lands = _exchange_wait(handle, after=[done], name=f"scatter_wait_{i}")
        for n, land, slot in zip(names, lands, slots):
            res = _reduce_adamw(land, w_arg[n][0], m_arg[n][0], v_arg[n][0], name="adamw_" + n)
            done = res[0]
            for d, r in zip(results, res):
                d[n] = r[None]
    stacks = [_exchange_wait(handle, after=[done], name=f"small_wait_{i}")[1][0] for i, handle in enumerate(smalls)]
    res = _adamw_small(stacks, small_views, small_w, small_m, small_v, name="adamw_small")
    for d, r in zip(results, res):
        d.update({n: r[n].reshape(w_arg[n].shape) for n in SMALL})
    out = [loss, grad_x[None]]
    for d in results:
        out += [d[n] for n in WEIGHTS]
    return tuple(out)
```

```python
import functools

import numpy as np
import jax
import jax.numpy as jnp
from jax import lax
from jax.experimental import pallas as pl
from jax.experimental.pallas import tpu as pltpu

F32 = jnp.float32
MXU_DTYPE = jnp.bfloat16
WIRE_DTYPE = jnp.bfloat16
VMEM_LIMIT_BYTES = 48 * 1024 * 1024
LANES = 128

N_DEV = 8
D_MODEL = 1024
SSM_GROUP = 16
SSM_WIDTH = 768
SSM_GROUPS = SSM_WIDTH // SSM_GROUP
SSM_STATE = 64
SSM_CH = SSM_GROUPS * SSM_STATE
SSM_TILES = SSM_WIDTH // LANES
GROUPS_PER_TILE = LANES // SSM_GROUP
STATE_VREG_ROWS = SSM_CH // LANES
ATT_HEAD_DIM = 64
ATT_HEADS_PER_GROUP = 4
ATT_MERGED = ATT_HEADS_PER_GROUP * ATT_HEAD_DIM
LANE_HALVES = ATT_MERGED // LANES
DILATIONS = (1, 4, 16)
ATT_BLK = 128
ATT_SCALE = ATT_HEAD_DIM ** -0.5
ROT_DIM = ATT_HEAD_DIM // 4
ROPE_THETA = 500000.0
XATT_HEADS = 4
XATT_HEAD_DIM = D_MODEL // XATT_HEADS
XATT_SCALE = XATT_HEAD_DIM ** -0.5
DEEPNORM_ALPHA = 2.0 ** 0.25
LN_EPS = 1e-5
NEG_INF = -1e30
OFF_Q_BLK, OFF_K_BLK, OFF_V_BLK = 3, 6, 9
OFF_GS_BLK, OFF_GA_BLK = 3, 4

ADAM_LR = 0.001
ADAM_B1 = 0.9
ADAM_B2 = 0.999
ADAM_EPS = 1e-08
ADAM_WD = 0.01
ADAM_STEP = 10

BIG = ("w_in", "w_glu", "w_att_up", "w_mix_out", "w_xq", "w_xkv", "w_xo", "w_ff1", "w_ff2")
BIG_COL_SHARDED = ("w_in", "w_glu", "w_att_up", "w_xkv", "w_ff1")
WEIGHTS = ("ln_in_g", "ln_in_b", "w_in", "b_in", "ssm_log_dt", "ssm_a_re", "ssm_a_im", "ssm_b_re", "ssm_b_im",
           "ssm_c_re", "ssm_c_im", "ssm_d", "w_glu", "b_glu", "w_att_up", "w_mix_out", "b_mix_out", "ln1_g", "ln1_b",
           "w_xq", "w_xkv", "w_xo", "ln2_g", "ln2_b", "w_ff1", "b_ff1", "w_ff2", "b_ff2", "ln3_g", "ln3_b")
SMALL = tuple(n for n in WEIGHTS if n not in BIG)
PACK_COLS = 1024
PACK_ROW_ALIGN = 16


def _params(*sem):
    return pltpu.CompilerParams(dimension_semantics=sem, vmem_limit_bytes=VMEM_LIMIT_BYTES)


def _dot(a, b, ca, cb):
    return lax.dot_general(a.astype(MXU_DTYPE), b.astype(MXU_DTYPE), (((ca,), (cb,)), ((), ())),
                           preferred_element_type=F32)


def _fit(dim, pref):
    if dim <= pref:
        return dim
    best = max(t for t in range(LANES, pref + 1, LANES) if dim % t == 0)
    return best


def _mm(a, b, *, name, ta=False, tb=False, bias=None, out_dtype=F32, b_shards=False, out_shards=False, after=None,
        also=None, gate=None, colsum=False, epilogue=None, tm=2048, tn=1024, tk=1024):
    m, k = (a.shape[1], a.shape[0]) if ta else a.shape
    order = (lambda f: (lambda j, i, kk: f(i, j, kk))) if colsum else (lambda f: f)
    spec = lambda shape, f: pl.BlockSpec(shape, order(f))
    if b_shards:
        n_sh, rows, n_loc = b.shape
        if tb:
            n, tn, tk = rows, _fit(rows, tn), n_loc
            assert k == n_sh * n_loc, (name, k, b.shape)
            b_spec = spec((1, tn, tk), lambda i, j, kk: (kk, j, 0))
        else:
            n, tn, tk = n_sh * n_loc, n_loc, _fit(k, tk)
            b_spec = spec((1, tk, tn), lambda i, j, kk: (j, kk, 0))
    else:
        n = b.shape[0] if tb else b.shape[1]
        tn = n // N_DEV if out_shards else _fit(n, tn)
        tk = _fit(k, tk)
        b_spec = spec((tn, tk), lambda i, j, kk: (j, kk)) if tb else spec((tk, tn), lambda i, j, kk: (kk, j))
    tm = _fit(m, tm)
    nk = k // tk
    a_spec = spec((tk, tm), lambda i, j, kk: (kk, i)) if ta else spec((tm, tk), lambda i, j, kk: (i, kk))
    tile = spec((tm, tn), lambda i, j, kk: (i, j))
    in_specs, args = [a_spec, b_spec], [a, b]
    if bias is not None:
        in_specs.append(spec((1, tn), lambda i, j, kk: (0, j)))
        args.append(bias)
    if gate is not None:
        in_specs.append(tile)
        args.append(gate[0])
    if after is not None:
        in_specs.append(pl.BlockSpec(memory_space=pl.ANY))
        args.append(after)
    if epilogue is not None:
        ep_fn, ep_rows, ep_fulls, ep_row_outs, ep_acc_outs = epilogue
        assert tn == n and not (colsum or also or gate or out_shards), name
        ep_first = len(args)
        in_specs += [spec((tm, t.shape[1]), lambda i, j, kk: (i, 0)) for t in ep_rows]
        in_specs += [pl.BlockSpec(t.shape, functools.partial(lambda i, j, kk, nd: (0,) * nd, nd=t.ndim)) for t in ep_fulls]
        args += [*ep_rows, *ep_fulls]
    n_in = len(args)
    if epilogue is not None:
        out_specs = [spec((tm, w), lambda i, j, kk: (i, 0)) for w, _ in ep_row_outs]
        out_specs += [spec((1, w), lambda i, j, kk: (0, 0)) for w in ep_acc_outs]
        out_shape = [jax.ShapeDtypeStruct((m, w), dt) for w, dt in ep_row_outs]
        out_shape += [jax.ShapeDtypeStruct((1, w), F32) for w in ep_acc_outs]
    elif out_shards:
        assert n == N_DEV * tn, (name, n, tn)
        out_specs = [spec((1, tm, tn), lambda i, j, kk: (j, i, 0))]
        out_shape = [jax.ShapeDtypeStruct((N_DEV, m, tn), out_dtype)]
    else:
        out_specs = [tile]
        out_shape = [jax.ShapeDtypeStruct((m, n), out_dtype)]
    if also is not None:
        out_specs.append(tile)
        out_shape.append(jax.ShapeDtypeStruct((m, n), also[1]))
    if colsum:
        out_specs.append(spec((1, tn), lambda i, j, kk: (0, j)))
        out_shape.append(jax.ShapeDtypeStruct((1, n), F32))

    def body(*refs):
        a_ref, b_ref = refs[0], refs[1]
        o_ref = refs[n_in]
        first_row_tile = pl.program_id(1 if colsum else 0) == 0

        def product():
            return _dot(a_ref[...], b_ref[0] if b_shards else b_ref[...], 0 if ta else 1, 1 if tb else 0)

        def finish(r):
            if bias is not None:
                r = r + refs[2][...]
            if gate is not None:
                r = r * gate[1](refs[2 + (bias is not None)][...])
            if epilogue is not None:
                res = ep_fn(r, *[ref[...] for ref in refs[ep_first:n_in]])
                n_o = len(ep_row_outs)
                for ref, val in zip(refs[n_in:n_in + n_o], res[:n_o]):
                    ref[...] = val.astype(ref.dtype)
                acc_refs = refs[n_in + n_o:n_in + n_o + len(ep_acc_outs)]
                if acc_refs:
                    @pl.when(first_row_tile)
                    def _():
                        for ref in acc_refs:
                            ref[...] = jnp.zeros_like(ref)

                    for ref, val in zip(acc_refs, res[n_o:]):
                        ref[...] += val
                return
            if out_shards:
                o_ref[0] = r.astype(o_ref.dtype)
            else:
                o_ref[...] = r.astype(o_ref.dtype)
            if also is not None:
                refs[n_in + 1][...] = also[0](r).astype(also[1])
            if colsum:
                s_ref = refs[n_in + 1 + (also is not None)]

                @pl.when(first_row_tile)
                def _():
                    s_ref[...] = jnp.zeros_like(s_ref)

                s_ref[...] += _colsum(r)

        if nk == 1:
            finish(product())
            return
        acc_ref = refs[-1]
        kk = pl.program_id(2)

        @pl.when(kk == 0)
        def _():
            acc_ref[...] = product()

        if nk > 2:
            @pl.when((kk > 0) & (kk < nk - 1))
            def _():
                acc_ref[...] += product()

        @pl.when(kk == nk - 1)
        def _():
            finish(acc_ref[...] + product())

    grid = (n // tn, m // tm, nk) if colsum else (m // tm, n // tn, nk)
    res = pl.pallas_call(
        body, name=name, grid=grid, in_specs=in_specs, out_specs=out_specs, out_shape=out_shape,
        scratch_shapes=[pltpu.VMEM((tm, tn), F32)] if nk > 1 else [],
        compiler_params=_params("arbitrary" if epilogue is not None else "parallel",
                                "arbitrary" if colsum else "parallel", "arbitrary"),
    )(*args)
    return res[0] if len(res) == 1 else res


def _mm_shards(a, w, bias, shard_ids, *, name, prev=None, tm=2048):
    m, k = a.shape
    n_sh, _, n_loc = w.shape
    tm = _fit(m, tm)

    def body(ids_ref, a_ref, w_ref, b_ref, *rest):
        rest[-1][...] = _dot(a_ref[...], w_ref[0], 1, 0) + b_ref[...]

    grid_spec = pltpu.PrefetchScalarGridSpec(
        num_scalar_prefetch=1, grid=(m // tm, shard_ids.shape[0]),
        in_specs=[pl.BlockSpec((tm, k), lambda i, j, ids: (i, 0)),
                  pl.BlockSpec((1, k, n_loc), lambda i, j, ids: (ids[j], 0, 0)),
                  pl.BlockSpec((1, n_loc), lambda i, j, ids: (0, ids[j]))]
        + [pl.BlockSpec(memory_space=pl.ANY)] * (prev is not None),
        out_specs=pl.BlockSpec((tm, n_loc), lambda i, j, ids: (i, ids[j])))
    return pl.pallas_call(
        body, name=name, grid_spec=grid_spec, out_shape=jax.ShapeDtypeStruct((m, n_sh * n_loc), F32),
        input_output_aliases={4: 0} if prev is not None else {}, compiler_params=_params("parallel", "arbitrary"),
    )(shard_ids, a, w, bias, *([prev] if prev is not None else []))


ROW_TILE = 512


def _rowcall(fn, rows, fulls, row_outs, acc_outs=(), *, n_rows, tm, name, after=None):
    n_r, n_f, n_o, n_a = len(rows), len(fulls), len(row_outs), len(acc_outs)
    n_in = n_r + n_f + (after is not None)
    assert n_rows % tm == 0, (name, n_rows, tm)

    def body(*refs):
        res = fn(*[r[...] for r in refs[:n_r + n_f]])
        res = tuple(res) if isinstance(res, (tuple, list)) else (res,)
        o_refs = refs[n_in:n_in + n_o]
        a_refs = refs[n_in + n_o:]
        for o_ref, val in zip(o_refs, res[:n_o]):
            o_ref[...] = val.astype(o_ref.dtype)
        if n_a:
            @pl.when(pl.program_id(0) == 0)
            def _():
                for a_ref in a_refs:
                    a_ref[...] = jnp.zeros_like(a_ref)

            for a_ref, val in zip(a_refs, res[n_o:]):
                a_ref[...] += val

    in_specs = [pl.BlockSpec((tm, w), functools.partial(lambda i, cb: (i, cb), cb=cb)) for _, w, cb in rows]
    in_specs += [pl.BlockSpec(f.shape, functools.partial(lambda i, nd: (0,) * nd, nd=f.ndim)) for f in fulls]
    in_specs += [pl.BlockSpec(memory_space=pl.ANY)] * (after is not None)
    out_specs = [pl.BlockSpec((tm, w), lambda i: (i, 0)) for w, _ in row_outs]
    out_specs += [pl.BlockSpec((1, w), lambda i: (0, 0)) for w in acc_outs]
    out_shape = [jax.ShapeDtypeStruct((n_rows, w), dt) for w, dt in row_outs]
    out_shape += [jax.ShapeDtypeStruct((1, w), F32) for w in acc_outs]
    return pl.pallas_call(
        body, name=name, grid=(n_rows // tm,), in_specs=in_specs, out_specs=out_specs, out_shape=out_shape,
        compiler_params=_params("arbitrary" if n_a else "parallel"),
    )(*[r[0] for r in rows], *fulls, *([after] if after is not None else []))


def _colsum(v):
    return jnp.sum(v, axis=0, keepdims=True)


def _layer_norm(xin, g, b):
    mu = jnp.mean(xin, axis=-1, keepdims=True)
    xc = xin - mu
    var = jnp.mean(xc * xc, axis=-1, keepdims=True)
    rstd = lax.rsqrt(var + LN_EPS)
    xh = xc * rstd
    return xh * g + b, xh, rstd


def _layer_norm_bwd(dy, xh, rstd, g):
    dyg = dy * g
    m1 = jnp.mean(dyg, axis=-1, keepdims=True)
    m2 = jnp.mean(dyg * xh, axis=-1, keepdims=True)
    dx = rstd * (dyg - m1 - xh * m2)
    return dx, _colsum(dy * xh), _colsum(dy), _colsum(dx)


def _ln_fwd(a, g, b, *, name):
    n_rows, d = a.shape

    def fn(av, gv, bv):
        y, xh, rstd = _layer_norm(av, gv, bv)
        return y, xh, rstd, y

    return _rowcall(fn, [(a, d, 0)], [g, b], [(d, F32), (d, F32), (1, F32), (d, MXU_DTYPE)], n_rows=n_rows, tm=ROW_TILE,
                    name=name)


def _ln_bwd(dya, dyb, xh, rstd, g, *, alpha, name, operand=True):
    n_rows, d = xh.shape

    def fn(da, db, xhv, rs, gv):
        dx, *sums = _layer_norm_bwd(alpha * da + db, xhv, rs, gv)
        return (dx,) + ((dx,) if operand else ()) + tuple(sums)

    rows = [(dya, d, 0), (dyb, d, 0), (xh, d, 0), (rstd, 1, 0)]
    return _rowcall(fn, rows, [g], [(d, F32)] + [(d, MXU_DTYPE)] * operand, [d, d, d], n_rows=n_rows, tm=ROW_TILE, name=name)


LN_EPILOGUE_ROWS = 1024


def _mm_ln_fwd(x, w, bias, a, g, b, *, alpha, name):
    d = a.shape[1]

    def fn(r, av, gv, bv):
        y, xh, rstd = _layer_norm(alpha * av + r, gv, bv)
        return y, xh, rstd, y

    return _mm(x, w, bias=bias, name=name, tm=LN_EPILOGUE_ROWS,
               epilogue=(fn, [a], [g, b], [(d, F32), (d, F32), (1, F32), (d, MXU_DTYPE)], []))


def _mm_ln_bwd(x, w, dya, xh, rstd, g, *, alpha, name):
    d = xh.shape[1]

    def fn(r, da, xhv, rs, gv):
        dx, *sums = _layer_norm_bwd(alpha * da + r, xhv, rs, gv)
        return (dx, dx, *sums)

    return _mm(x, w, tb=True, name=name, tm=LN_EPILOGUE_ROWS,
               epilogue=(fn, [dya, xh, rstd], [g], [(d, F32), (d, MXU_DTYPE)], [d, d, d]))


def _ln_loss_bwd(a, r, target, g, b, *, alpha, name):
    n_rows, d = a.shape

    def fn(av, rv, tv, gv, bv):
        y, xh, rs = _layer_norm(alpha * av + rv, gv, bv)
        diff = y - tv
        part = jnp.sum(jnp.sum(diff * diff, axis=1, keepdims=True), axis=0, keepdims=True) * (0.5 / d)
        dx, *sums = _layer_norm_bwd(diff * (1.0 / d), xh, rs, gv)
        return (dx, dx, *sums, jnp.broadcast_to(part, (1, LANES)))

    return _rowcall(fn, [(a, d, 0), (r, d, 0), (target, d, 0)], [g, b], [(d, F32), (d, MXU_DTYPE)], [d, d, d, LANES],
                    n_rows=n_rows, tm=ROW_TILE, name=name)


def _rope_lane_constants():
    lane = np.arange(ATT_MERGED)
    in_head = lane % ATT_HEAD_DIM
    sign = np.where(in_head < ROT_DIM // 2, -1.0, np.where(in_head < ROT_DIM, 1.0, 0.0)).astype(np.float32)
    inv_freq = ROPE_THETA ** (-jnp.arange(0, ROT_DIM, 2, dtype=F32) / ROT_DIM)
    return inv_freq[lane % (ROT_DIM // 2)].reshape(1, ATT_MERGED), jnp.asarray(sign).reshape(1, ATT_MERGED)


def _rope_tables(pos_col, *, name, after=None):
    inv_lane, sign = _rope_lane_constants()

    def fn(pos, inv, sg):
        ang = pos.astype(F32) * inv
        return jnp.where(sg != 0.0, jnp.cos(ang), 1.0), sg * jnp.sin(ang)

    return _rowcall(fn, [(pos_col, 1, 0)], [inv_lane, sign], [(ATT_MERGED, F32), (ATT_MERGED, F32)],
                    n_rows=pos_col.shape[0], tm=512, name=name, after=after)


def _rot_partner(t):
    lane = lax.broadcasted_iota(jnp.int32, t.shape, 1)
    width = t.shape[1]
    return jnp.where((lane & (ROT_DIM // 2)) == 0, pltpu.roll(t, width - ROT_DIM // 2, 1), pltpu.roll(t, ROT_DIM // 2, 1))


def _rope(t, cos_t, sin_t):
    return t * cos_t + _rot_partner(t) * sin_t


def _rope_transpose(dt, cos_t, sin_t):
    return dt * cos_t + _rot_partner(dt * sin_t)


def _strided_rows(r, count, stride):
    return pl.ds(r, count) if stride == 1 else pl.ds(r, count, stride=stride)


def _qkv_split(proj, cos_t, sin_t, *, name, tm=512):
    n_rows = proj.shape[0]
    n_g = len(DILATIONS)

    def body(*refs):
        n_src = LANE_HALVES * 3 * n_g
        src, tables, dst = refs[:n_src], refs[n_src:n_src + 2 * LANE_HALVES], refs[n_src + 2 * LANE_HALVES:]
        for kind in range(3):
            for g, dil in enumerate(DILATIONS):
                for half in range(LANE_HALVES):
                    x_ref, o_ref = src[(kind * n_g + g) * LANE_HALVES + half], dst[kind * n_g + g]
                    cos_ref, sin_ref = tables[half], tables[LANE_HALVES + half]
                    for r in range(dil):
                        rows = _strided_rows(r, tm // dil, dil)
                        t = x_ref[rows, :]
                        if kind < 2:
                            t = _rope(t, cos_ref[rows, :], sin_ref[rows, :])
                        lo = r * ATT_MERGED + half * LANES
                        o_ref[:, lo:lo + LANES] = t.astype(o_ref.dtype)

    half_spec = lambda cb: pl.BlockSpec((tm, LANES), functools.partial(lambda i, cb: (i, cb), cb=cb))
    in_specs = [half_spec((off + g) * LANE_HALVES + half)
                for off in (OFF_Q_BLK, OFF_K_BLK, OFF_V_BLK) for g in range(n_g) for half in range(LANE_HALVES)]
    in_specs += [half_spec(half) for _ in range(2) for half in range(LANE_HALVES)]
    out_specs = [pl.BlockSpec((tm // dil, dil * ATT_MERGED), lambda i: (i, 0)) for _ in range(3) for dil in DILATIONS]
    out_shape = [jax.ShapeDtypeStruct((n_rows // dil, dil * ATT_MERGED), MXU_DTYPE) for _ in range(3) for dil in DILATIONS]
    outs = pl.pallas_call(
        body, name=name, grid=(n_rows // tm,), in_specs=in_specs, out_specs=out_specs, out_shape=out_shape,
        compiler_params=_params("parallel"),
    )(*[proj] * (LANE_HALVES * 3 * n_g), *[cos_t] * LANE_HALVES, *[sin_t] * LANE_HALVES)
    return outs[:n_g], outs[n_g:2 * n_g], outs[2 * n_g:]


def _mix(gs, ga, z1, z2, b_att):
    return jax.nn.sigmoid(gs) * (z1 * jax.nn.sigmoid(z2)) + jax.nn.sigmoid(ga) * b_att


def _mix_rows(proj, z, b_att):
    return [(proj, D_MODEL, OFF_GS_BLK), (proj, D_MODEL, OFF_GA_BLK), (z, D_MODEL, 0), (z, D_MODEL, 1), (b_att, D_MODEL, 0)]


def _mix_out_ln(proj, z, b_att, w, bias, a, g, b, *, alpha, name):
    def fn(gs, ga, z1, z2, ba, av, wv, biasv, gv, bv):
        mixed = _mix(gs, ga, z1, z2, ba)
        y, xh, rstd = _layer_norm(alpha * av + (_dot(mixed, wv, 1, 0) + biasv), gv, bv)
        return mixed, y, xh, rstd, y

    rows = _mix_rows(proj, z, b_att) + [(a, D_MODEL, 0)]
    outs = [(D_MODEL, MXU_DTYPE), (D_MODEL, F32), (D_MODEL, F32), (1, F32), (D_MODEL, MXU_DTYPE)]
    return _rowcall(fn, rows, [w, bias, g, b], outs, n_rows=proj.shape[0], tm=ROW_TILE // 2, name=name)


def _mix_bwd(dmixed, proj, z, b_att, *, name):
    def fn(dm, gs, ga, z1, z2, ba):
        _, vjp = jax.vjp(_mix, gs, ga, z1, z2, ba)
        dgs, dga, dz1, dz2, dba = vjp(dm)
        dz = jnp.concatenate([dz1, dz2], axis=1)
        return dgs, dga, dz, dba, _colsum(dgs), _colsum(dga), _colsum(dz)

    rows = [(dmixed, D_MODEL, 0)] + _mix_rows(proj, z, b_att)
    widths = [D_MODEL, D_MODEL, 2 * D_MODEL, D_MODEL]
    return _rowcall(fn, rows, [], [(w, MXU_DTYPE) for w in widths], widths[:3], n_rows=proj.shape[0], tm=ROW_TILE, name=name)


def _gelu_bwd(dgy, y, proj, *, name):
    def fn(dg, yv, u):
        _, vjp = jax.vjp(jax.nn.gelu, yv)
        dy = vjp(dg)[0]
        return dy, _colsum(dy * u)

    return _rowcall(fn, [(dgy, SSM_WIDTH, 0), (y, SSM_WIDTH, 0), (proj, SSM_WIDTH, 0)], [], [(SSM_WIDTH, F32)],
                    [SSM_WIDTH], n_rows=y.shape[0], tm=512, name=name)


HEAD_ROWS = ATT_HEADS_PER_GROUP * ATT_BLK


def _head_masks(rows):
    head = lax.broadcasted_iota(jnp.int32, (rows, ATT_MERGED), 1) >> (ATT_HEAD_DIM.bit_length() - 1)
    return [head == h for h in range(ATT_HEADS_PER_GROUP)]


def _stack_heads(t, masks):
    return jnp.concatenate([jnp.where(m, t, jnp.zeros_like(t)) for m in masks], axis=0)


def _unstack_heads(t4, masks):
    blocks = [t4[h * ATT_BLK:(h + 1) * ATT_BLK] for h in range(ATT_HEADS_PER_GROUP)]
    return jnp.where(masks[0], blocks[0], jnp.where(masks[1], blocks[1], jnp.where(masks[2], blocks[2], blocks[3])))


def _head_column(stats, first):
    return jnp.concatenate([stats[:, first + h:first + h + 1] for h in range(ATT_HEADS_PER_GROUP)], axis=0)


def _band_mask(first_key):
    qi = lax.broadcasted_iota(jnp.int32, (HEAD_ROWS, 2 * ATT_BLK), 0) & (ATT_BLK - 1)
    ki = lax.broadcasted_iota(jnp.int32, (HEAD_ROWS, 2 * ATT_BLK), 1)
    steps = qi + ATT_BLK - ki
    return (steps >= 0) & (steps <= ATT_BLK) & (ki >= first_key)


def _dil_fwd(q, k, v, dil, *, name):
    n_blk = q.shape[0] // ATT_BLK
    cur = pl.BlockSpec((ATT_BLK, ATT_MERGED), lambda r, n: (n, r))
    prev = pl.BlockSpec((ATT_BLK, ATT_MERGED), lambda r, n: (jnp.maximum(n - 1, 0), r))

    def body(q_ref, kp_ref, kc_ref, vp_ref, vc_ref, o_ref, l_ref):
        masks = _head_masks(ATT_BLK)
        valid = _band_mask(jnp.where(pl.program_id(1) > 0, 0, ATT_BLK))
        keys = jnp.concatenate([kp_ref[...], kc_ref[...]], axis=0)
        vals = jnp.concatenate([vp_ref[...], vc_ref[...]], axis=0)
        s = jnp.where(valid, _dot(_stack_heads(q_ref[...], masks), keys, 1, 1) * ATT_SCALE, NEG_INF)
        m = jnp.max(s, axis=-1, keepdims=True)
        p = jnp.exp(s - m)
        den = jnp.sum(p, axis=-1, keepdims=True)
        o_ref[...] = _unstack_heads(_dot(p, vals, 1, 0) / den, masks)
        l_ref[...] = _unstack_heads(jnp.broadcast_to(m + jnp.log(den), (HEAD_ROWS, ATT_MERGED)), masks)

    shape = jax.ShapeDtypeStruct(q.shape, F32)
    return pl.pallas_call(
        body, name=name, grid=(dil, n_blk), in_specs=[cur, prev, cur, prev, cur], out_specs=[cur, cur],
        out_shape=[shape, shape], compiler_params=_params("parallel", "parallel"),
    )(q, k, k, v, v)


def _att_merge(outs, lses, *, name, tm=512):
    n_g = len(outs)
    n_rows = outs[0].shape[0] * DILATIONS[0]

    def body(*refs):
        src, (att_ref, lse_ref), tmp = refs[:2 * n_g], refs[2 * n_g:2 * n_g + 2], refs[2 * n_g + 2:]
        vals = []
        for idx, src_ref in enumerate(src):
            dil = DILATIONS[idx % n_g]
            if dil == 1:
                vals.append(src_ref[...])
                continue
            for r in range(dil):
                for half in range(LANE_HALVES):
                    lo = r * ATT_MERGED + half * LANES
                    tmp[LANE_HALVES * idx + half][_strided_rows(r, tm // dil, dil), :] = src_ref[:, lo:lo + LANES]
            vals.append(jnp.concatenate([tmp[LANE_HALVES * idx + half][...] for half in range(LANE_HALVES)], axis=1))
        o, l = vals[:n_g], vals[n_g:]
        m = functools.reduce(jnp.maximum, l)
        e = [jnp.exp(li - m) for li in l]
        z = functools.reduce(jnp.add, e)
        att_ref[...] = functools.reduce(jnp.add, [(ei / z) * oi for ei, oi in zip(e, o)])
        lse_ref[...] = m + jnp.log(z)

    in_specs = [pl.BlockSpec((tm // dil, dil * ATT_MERGED), lambda i: (i, 0)) for _ in range(2) for dil in DILATIONS]
    row = pl.BlockSpec((tm, ATT_MERGED), lambda i: (i, 0))
    shape = jax.ShapeDtypeStruct((n_rows, ATT_MERGED), F32)
    return pl.pallas_call(
        body, name=name, grid=(n_rows // tm,), in_specs=in_specs, out_specs=[row, row], out_shape=[shape, shape],
        scratch_shapes=[pltpu.VMEM((tm, LANES), F32)] * (LANE_HALVES * 2 * n_g), compiler_params=_params("parallel"),
    )(*outs, *lses)


def _att_stats(datt, att, lse, *, name):
    n_rows = datt.shape[0]

    def fn(d, a, l):
        prod = d * a
        lane = lax.broadcasted_iota(jnp.int32, (d.shape[0], LANES), 1)
        out = jnp.zeros((d.shape[0], LANES), F32)
        for h in range(ATT_HEADS_PER_GROUP):
            lo = h * ATT_HEAD_DIM
            out = jnp.where(lane == h, l[:, lo:lo + 1], out)
            delta = jnp.sum(prod[:, lo:lo + ATT_HEAD_DIM], axis=-1, keepdims=True)
            out = jnp.where(lane == ATT_HEADS_PER_GROUP + h, delta, out)
        return out

    rows = [(t, ATT_MERGED, 0) for t in (datt, att, lse)]
    return _rowcall(fn, rows, [], [(LANES, F32)], n_rows=n_rows, tm=512, name=name)[0]


def _dil_bwd(q, k, v, datt, stats, dil, *, name):
    n_rows = datt.shape[0]
    n_blk = n_rows // dil // ATT_BLK
    span = ATT_BLK * dil
    cur = pl.BlockSpec((ATT_BLK, ATT_MERGED), lambda n, r: (n, r))
    prev = pl.BlockSpec((ATT_BLK, ATT_MERGED), lambda n, r: (jnp.maximum(n - 1, 0), r))
    nxt = pl.BlockSpec((ATT_BLK, ATT_MERGED), lambda n, r: (jnp.minimum(n + 1, n_blk - 1), r))
    seq = lambda half, ahead: pl.BlockSpec((span, LANES), lambda n, r: (jnp.minimum(n + ahead, n_blk - 1), half))

    def body(qc_ref, qn_ref, kp_ref, kc_ref, vp_ref, vc_ref, dc0_ref, dc1_ref, dn0_ref, dn1_ref, sc_ref, sn_ref,
             dq0_ref, dq1_ref, dk0_ref, dk1_ref, dv0_ref, dv1_ref):
        n = pl.program_id(0)
        rows = slice(None) if dil == 1 else _strided_rows(pl.program_id(1), ATT_BLK, dil)

        def read(ref0, ref1):
            return jnp.concatenate([ref0[rows, :], ref1[rows, :]], axis=1)

        def write(ref0, ref1, val):
            ref0[rows, :] = val[:, :LANES]
            ref1[rows, :] = val[:, LANES:]

        masks = _head_masks(ATT_BLK)
        valid = _band_mask(jnp.where(n > 0, 0, ATT_BLK))
        qi = lax.broadcasted_iota(jnp.int32, (HEAD_ROWS, ATT_BLK), 0) & (ATT_BLK - 1)
        ki = lax.broadcasted_iota(jnp.int32, (HEAD_ROWS, ATT_BLK), 1)
        valid_next = (ki - qi) >= jnp.where(n < n_blk - 1, 0, ATT_BLK)

        kc, vc = kc_ref[...], vc_ref[...]
        keys = jnp.concatenate([kp_ref[...], kc], axis=0)
        vals = jnp.concatenate([vp_ref[...], vc], axis=0)
        q4 = _stack_heads(qc_ref[...], masks)
        d4 = _stack_heads(read(dc0_ref, dc1_ref).astype(MXU_DTYPE), masks)
        st = sc_ref[rows, :]
        p = jnp.where(valid, jnp.exp(_dot(q4, keys, 1, 1) * ATT_SCALE - _head_column(st, 0)), 0.0)
        ds = p * (_dot(d4, vals, 1, 1) - _head_column(st, ATT_HEADS_PER_GROUP)) * ATT_SCALE
        write(dq0_ref, dq1_ref, _unstack_heads(_dot(ds, keys, 1, 0), masks))

        q4n = _stack_heads(qn_ref[...], masks)
        d4n = _stack_heads(read(dn0_ref, dn1_ref).astype(MXU_DTYPE), masks)
        stn = sn_ref[rows, :]
        p_n = jnp.where(valid_next, jnp.exp(_dot(q4n, kc, 1, 1) * ATT_SCALE - _head_column(stn, 0)), 0.0)
        ds_n = p_n * (_dot(d4n, vc, 1, 1) - _head_column(stn, ATT_HEADS_PER_GROUP)) * ATT_SCALE
        write(dv0_ref, dv1_ref, _dot(p[:, ATT_BLK:], d4, 0, 0) + _dot(p_n, d4n, 0, 0))
        write(dk0_ref, dk1_ref, _dot(ds[:, ATT_BLK:], q4, 0, 0) + _dot(ds_n, q4n, 0, 0))

    shape = jax.ShapeDtypeStruct((n_rows, LANES), F32)
    out = seq(0, 0)
    res = pl.pallas_call(
        body, name=name, grid=(n_blk, dil),
        in_specs=[cur, nxt, prev, cur, prev, cur, seq(0, 0), seq(1, 0), seq(0, 1), seq(1, 1), seq(0, 0), seq(0, 1)],
        out_specs=[out] * 6, out_shape=[shape] * 6, compiler_params=_params("parallel", "arbitrary"),
    )(q, q, k, k, v, v, datt, datt, datt, datt, stats, stats)
    return [(res[2 * i], res[2 * i + 1]) for i in range(3)]


def _dproj_assemble(du, dqkv, dgs, dga, cos_t, sin_t, *, name):
    n_g = len(DILATIONS)

    def fn(*t):
        n_half = LANE_HALVES * 3 * n_g
        du_t, halves, (dgs_t, dga_t, c, s) = t[0], t[1:1 + n_half], t[1 + n_half:]
        parts = [jnp.concatenate(halves[LANE_HALVES * i:LANE_HALVES * (i + 1)], axis=1) for i in range(3 * n_g)]
        for i in range(2 * n_g):
            parts[i] = _rope_transpose(parts[i], c, s)
        cast = [p.astype(MXU_DTYPE) for p in parts]
        return [jnp.concatenate([du_t] + cast + [dgs_t, dga_t], axis=1)] + [_colsum(p) for p in parts]

    rows = [(du, SSM_WIDTH, 0)]
    rows += [(half, LANES, 0) for i in range(3) for g in range(n_g) for half in dqkv[g][i]]
    rows += [(dgs, D_MODEL, 0), (dga, D_MODEL, 0), (cos_t, ATT_MERGED, 0), (sin_t, ATT_MERGED, 0)]
    width = SSM_WIDTH + 3 * n_g * ATT_MERGED + 2 * D_MODEL
    res = _rowcall(fn, rows, [], [(width, MXU_DTYPE)], [ATT_MERGED] * (3 * n_g), n_rows=du.shape[0], tm=ROW_TILE, name=name)
    return res[0], res[1:]


def _xhead(h):
    return slice(h * XATT_HEAD_DIM, (h + 1) * XATT_HEAD_DIM)


def _xatt_probs(qh, kh):
    s = _dot(qh, kh, 1, 1) * XATT_SCALE
    e = jnp.exp(s - jnp.max(s, axis=-1, keepdims=True))
    return e / jnp.sum(e, axis=-1, keepdims=True)


def _xatt_fwd(q, kv, *, name, tm=512):
    n_rows = q.shape[0]
    n_mem = kv.shape[0]

    def body(q_ref, kv_ref, o_ref):
        for h in range(XATT_HEADS):
            sl = _xhead(h)
            p = _xatt_probs(q_ref[:, sl], kv_ref[:, sl])
            o_ref[:, sl] = _dot(p, kv_ref[:, D_MODEL + h * XATT_HEAD_DIM:D_MODEL + (h + 1) * XATT_HEAD_DIM], 1, 0
                                ).astype(o_ref.dtype)

    row = pl.BlockSpec((tm, D_MODEL), lambda i: (i, 0))
    return pl.pallas_call(
        body, name=name, grid=(n_rows // tm,),
        in_specs=[row, pl.BlockSpec((n_mem, 2 * D_MODEL), lambda i: (0, 0))], out_specs=row,
        out_shape=jax.ShapeDtypeStruct((n_rows, D_MODEL), MXU_DTYPE), compiler_params=_params("parallel"),
    )(q, kv)


def _xatt_bwd(q, kv, do, *, name, tm=512):
    n_rows = q.shape[0]
    n_mem = kv.shape[0]

    def body(q_ref, kv_ref, do_ref, dq_ref, dkv_ref):
        @pl.when(pl.program_id(0) == 0)
        def _():
            dkv_ref[...] = jnp.zeros_like(dkv_ref)

        for h in range(XATT_HEADS):
            sl = _xhead(h)
            vsl = slice(D_MODEL + h * XATT_HEAD_DIM, D_MODEL + (h + 1) * XATT_HEAD_DIM)
            qh, kh, doh = q_ref[:, sl], kv_ref[:, sl], do_ref[:, sl]
            p = _xatt_probs(qh, kh)
            dp = _dot(doh, kv_ref[:, vsl], 1, 1)
            ds = p * (dp - jnp.sum(dp * p, axis=-1, keepdims=True)) * XATT_SCALE
            dq_ref[:, sl] = _dot(ds, kh, 1, 0).astype(dq_ref.dtype)
            dkv_ref[:, sl] += _dot(ds, qh, 0, 0)
            dkv_ref[:, vsl] += _dot(p, doh, 0, 0)

    row = pl.BlockSpec((tm, D_MODEL), lambda i: (i, 0))
    full = pl.BlockSpec((n_mem, 2 * D_MODEL), lambda i: (0, 0))
    return pl.pallas_call(
        body, name=name, grid=(n_rows // tm,), in_specs=[row, full, row], out_specs=[row, full],
        out_shape=[jax.ShapeDtypeStruct((n_rows, D_MODEL), MXU_DTYPE), jax.ShapeDtypeStruct((n_mem, 2 * D_MODEL), F32)],
        compiler_params=_params("arbitrary"),
    )(q, kv, do)


def _disc(logdt, a_re, a_im, b_re, b_im):
    dt = jnp.exp(logdt)
    mag = jnp.exp(a_re * dt)
    ab_re = mag * jnp.cos(a_im * dt)
    ab_im = mag * jnp.sin(a_im * dt)
    den = jnp.square(a_re) + jnp.square(a_im)
    nr = ab_re - 1.0
    f_re = (nr * a_re + ab_im * a_im) / den
    f_im = (ab_im * a_re - nr * a_im) / den
    bb_re = f_re[None] * b_re - f_im[None] * b_im
    bb_im = f_re[None] * b_im + f_im[None] * b_re
    return ab_re, ab_im, bb_re, bb_im


def _disc_transpose(logdt, a_re, a_im, b_re, b_im, g_ab_re, g_ab_im, g_bb_re, g_bb_im):
    dt = jnp.exp(logdt)
    mag = jnp.exp(a_re * dt)
    th = a_im * dt
    cs, sn = jnp.cos(th), jnp.sin(th)
    ab_re, ab_im = mag * cs, mag * sn
    den = jnp.square(a_re) + jnp.square(a_im)
    nr = ab_re - 1.0
    f_re = (nr * a_re + ab_im * a_im) / den
    f_im = (ab_im * a_re - nr * a_im) / den
    d_f_re = jnp.sum(g_bb_re * b_re + g_bb_im * b_im, axis=0)
    d_f_im = jnp.sum(g_bb_im * b_re - g_bb_re * b_im, axis=0)
    d_b_re = g_bb_re * f_re[None] + g_bb_im * f_im[None]
    d_b_im = g_bb_im * f_re[None] - g_bb_re * f_im[None]
    d_n_re, d_n_im = d_f_re / den, d_f_im / den
    d_den = -(d_f_re * f_re + d_f_im * f_im) / den
    d_ab_re = g_ab_re + d_n_re * a_re - d_n_im * a_im
    d_ab_im = g_ab_im + d_n_re * a_im + d_n_im * a_re
    d_a_re = d_n_re * nr + d_n_im * ab_im + 2.0 * d_den * a_re
    d_a_im = d_n_re * ab_im - d_n_im * nr + 2.0 * d_den * a_im
    d_mag = d_ab_re * cs + d_ab_im * sn
    d_th = mag * (d_ab_im * cs - d_ab_re * sn)
    d_a_re = d_a_re + d_mag * mag * dt
    d_a_im = d_a_im + d_th * dt
    d_dt = jnp.sum(d_mag * mag * a_re + d_th * a_im, axis=-1, keepdims=True)
    return d_dt * dt, d_a_re, d_a_im, d_b_re, d_b_im


def _full_spec(shape):
    return pl.BlockSpec(tuple(shape), functools.partial(lambda i, nd: (0,) * nd, nd=len(shape)))


def _whole(fn, args, out_shapes, *, name):
    n_in = len(args)

    def body(*refs):
        res = fn(*[r[...] for r in refs[:n_in]])
        for o_ref, val in zip(refs[n_in:], res):
            o_ref[...] = val

    return pl.pallas_call(
        body, name=name, grid=(1,), in_specs=[_full_spec(t.shape) for t in args],
        out_specs=[_full_spec(s) for s in out_shapes], out_shape=[jax.ShapeDtypeStruct(s, F32) for s in out_shapes],
        compiler_params=_params("arbitrary"))(*args)


SSM_WIDE = GROUPS_PER_TILE * SSM_STATE
LANE_GROUPS_PER_TILE = SSM_WIDE // LANES


def _chan(j):
    return slice(j * LANES, (j + 1) * LANES)


def _time_major_rows(j, q, tc):
    return pl.ds(j * LANE_GROUPS_PER_TILE + q, tc, stride=STATE_VREG_ROWS)


def _to_time_major(x, t_re_ref, t_im_ref, dst_re, dst_im, tc):
    for j in range(SSM_TILES):
        xj = x[:, _chan(j)]
        for t_ref, dst in ((t_re_ref, dst_re), (t_im_ref, dst_im)):
            r = _dot(xj, t_ref[j], 1, 0)
            for q in range(LANE_GROUPS_PER_TILE):
                dst[_time_major_rows(j, q, tc), :] = r[:, q * LANES:(q + 1) * LANES]


def _from_time_major(src, j, tc):
    return jnp.concatenate([src[_time_major_rows(j, q, tc), :] for q in range(LANE_GROUPS_PER_TILE)], axis=1)


def _scan_chunk(w_re, w_im, h_re, h_im, a_re, a_im, start, tc):
    def step(t, carry):
        hr, hi = carry
        rows = _scan_rows(t)
        nr = a_re * hr - a_im * hi + w_re[rows, :]
        ni = a_re * hi + a_im * hr + w_im[rows, :]
        h_re[rows, :] = nr
        h_im[rows, :] = ni
        return nr, ni

    return lax.fori_loop(0, tc, step, start, unroll=8)


SSM_CHUNK = 256


def _tile_spec(stack, k):
    return pl.BlockSpec((pl.Squeezed(),) + tuple(stack.shape[1:]), lambda i: (k, 0, 0, 0))


def _expand_block_diagonal(src_ref, dst):
    dst[...] = jnp.zeros_like(dst)
    r, c = src_ref.shape[1:]
    for g in range(SSM_GROUPS):
        j, gl = divmod(g, GROUPS_PER_TILE)
        dst[j, gl * r:(gl + 1) * r, gl * c:(gl + 1) * c] = src_ref[g].astype(dst.dtype)


def _extract_block_diagonal(src, dst_ref):
    r, c = dst_ref.shape[1:]
    for g in range(SSM_GROUPS):
        j, gl = divmod(g, GROUPS_PER_TILE)
        dst_ref[g] = src[j, gl * r:(gl + 1) * r, gl * c:(gl + 1) * c]


def _ssm_fwd(proj, blocks_cn, blocks_nc, a_re, a_im, gain, *, name, tc=SSM_CHUNK):
    n_rows = proj.shape[0]
    n_chunk = n_rows // tc

    def body(u_ref, br_ref, bi_ref, cr_ref, ci_ref, ar_ref, ai_ref, g_ref, y_ref, gy_ref, hr, hi, wr, wi, state,
             tbr_ref, tbi_ref, tcr_ref, tci_ref):
        @pl.when(pl.program_id(0) == 0)
        def _():
            state[...] = jnp.zeros_like(state)
            for src_ref, dst in ((br_ref, tbr_ref), (bi_ref, tbi_ref), (cr_ref, tcr_ref), (ci_ref, tci_ref)):
                _expand_block_diagonal(src_ref, dst)

        u = u_ref[...]
        _to_time_major(u, tbr_ref, tbi_ref, wr, wi, tc)
        state[0], state[1] = _scan_chunk(wr, wi, hr, hi, ar_ref[...], ai_ref[...], (state[0], state[1]), tc)
        for j in range(SSM_TILES):
            yj = (_dot(_from_time_major(hr, j, tc), tcr_ref[j], 1, 0) + _dot(_from_time_major(hi, j, tc), tci_ref[j], 1, 0)
                  + g_ref[:, _chan(j)] * u[:, _chan(j)])
            y_ref[:, _chan(j)] = yj
            gy_ref[:, _chan(j)] = jax.nn.gelu(yj).astype(gy_ref.dtype)

    rows = pl.BlockSpec((tc, SSM_WIDTH), lambda i: (i, 0))
    coef = pl.BlockSpec((STATE_VREG_ROWS, LANES), lambda i: (0, 0))
    states = pl.BlockSpec((tc * STATE_VREG_ROWS, LANES), lambda i: (i, 0))
    sshape = jax.ShapeDtypeStruct((n_rows * STATE_VREG_ROWS, LANES), F32)
    return pl.pallas_call(
        body, name=name, grid=(n_chunk,),
        in_specs=[rows, _tile_spec(blocks_cn, 0), _tile_spec(blocks_cn, 1), _tile_spec(blocks_nc, 0),
                  _tile_spec(blocks_nc, 1), coef, coef, pl.BlockSpec((1, SSM_WIDTH), lambda i: (0, 0))],
        out_specs=[rows, rows, states, states],
        out_shape=[jax.ShapeDtypeStruct((n_rows, SSM_WIDTH), F32), jax.ShapeDtypeStruct((n_rows, SSM_WIDTH), MXU_DTYPE),
                   sshape, sshape],
        scratch_shapes=[pltpu.VMEM((tc * STATE_VREG_ROWS, LANES), F32)] * 2 + [pltpu.VMEM((2, STATE_VREG_ROWS, LANES), F32)]
        + [pltpu.VMEM((SSM_TILES, LANES, SSM_WIDE), MXU_DTYPE)] * 2 + [pltpu.VMEM((SSM_TILES, SSM_WIDE, LANES), MXU_DTYPE)] * 2,
        compiler_params=_params("arbitrary"),
    )(proj, blocks_cn, blocks_cn, blocks_nc, blocks_nc, a_re, a_im, gain)


def _ssm_bwd(proj, dy, h_re, h_im, blocks_cn, blocks_nc, a_re, a_im, gain, *, name, tc=SSM_CHUNK):
    n_rows = proj.shape[0]
    n_chunk = n_rows // tc

    def body(u_ref, dy_ref, hr, hi, cr_ref, ci_ref, br_ref, bi_ref, ar_ref, ai_ref, g_ref,
             du_ref, su_ref, dc_re_ref, dc_im_ref, db_re_ref, db_im_ref, dar_ref, dai_ref, wr, wi, carry,
             tdr_ref, tdi_ref, tur_ref, tui_ref, dcr_ref, dci_ref, dbr_ref, dbi_ref):
        @pl.when(pl.program_id(0) == 0)
        def _():
            carry[...] = jnp.zeros_like(carry)
            for acc_ref in (su_ref, dcr_ref, dci_ref, dbr_ref, dbi_ref):
                acc_ref[...] = jnp.zeros_like(acc_ref)
            for src_ref, dst in ((cr_ref, tdr_ref), (ci_ref, tdi_ref), (br_ref, tur_ref), (bi_ref, tui_ref)):
                _expand_block_diagonal(src_ref, dst)

        a_r, a_i = ar_ref[...], ai_ref[...]
        u, dyv = u_ref[...], dy_ref[...]
        _to_time_major(dyv, tdr_ref, tdi_ref, wr, wi, tc)

        def step(kk, c):
            lam_r, lam_i, dar, dai = c
            rows = _scan_rows(tc - 1 - kk)
            h_r, h_i = hr[rows, :], hi[rows, :]
            dar = dar + lam_r * h_r + lam_i * h_i
            dai = dai + lam_i * h_r - lam_r * h_i
            new_r = wr[rows, :] + a_r * lam_r + a_i * lam_i
            new_i = wi[rows, :] + a_r * lam_i - a_i * lam_r
            wr[rows, :] = new_r
            wi[rows, :] = new_i
            return new_r, new_i, dar, dai

        carry[0], carry[1], carry[2], carry[3] = lax.fori_loop(0, tc, step, (carry[0], carry[1], carry[2], carry[3]),
                                                              unroll=8)
        dar_ref[...] = carry[2]
        dai_ref[...] = carry[3]
        for j in range(SSM_TILES):
            cj = _chan(j)
            lam_r, lam_i = _from_time_major(wr, j, tc), _from_time_major(wi, j, tc)
            dcr_ref[j] += _dot(dyv[:, cj], _from_time_major(hr, j, tc), 0, 0)
            dci_ref[j] += _dot(dyv[:, cj], _from_time_major(hi, j, tc), 0, 0)
            dbr_ref[j] += _dot(u[:, cj], lam_r, 0, 0)
            dbi_ref[j] += _dot(u[:, cj], lam_i, 0, 0)
            duj = _dot(lam_r, tur_ref[j], 1, 0) + _dot(lam_i, tui_ref[j], 1, 0) + g_ref[:, cj] * dyv[:, cj]
            du_ref[:, cj] = duj.astype(du_ref.dtype)
            su_ref[:, cj] += _colsum(duj)

        @pl.when(pl.program_id(0) == n_chunk - 1)
        def _():
            for src, dst_ref in ((dcr_ref, dc_re_ref), (dci_ref, dc_im_ref), (dbr_ref, db_re_ref), (dbi_ref, db_im_ref)):
                _extract_block_diagonal(src, dst_ref)

    back = lambda i: (n_chunk - 1 - i, 0)
    rows = pl.BlockSpec((tc, SSM_WIDTH), back)
    blocks = pl.BlockSpec((SSM_GROUPS, SSM_GROUP, SSM_STATE), lambda i: (0, 0, 0))
    coef = pl.BlockSpec((STATE_VREG_ROWS, LANES), lambda i: (0, 0))
    states = pl.BlockSpec((tc * STATE_VREG_ROWS, LANES), back)
    vec = pl.BlockSpec((1, SSM_WIDTH), lambda i: (0, 0))
    bshape = jax.ShapeDtypeStruct((SSM_GROUPS, SSM_GROUP, SSM_STATE), F32)
    cshape = jax.ShapeDtypeStruct((STATE_VREG_ROWS, LANES), F32)
    return pl.pallas_call(
        body, name=name, grid=(n_chunk,),
        in_specs=[rows, rows, states, states, _tile_spec(blocks_cn, 2), _tile_spec(blocks_cn, 3), _tile_spec(blocks_nc, 2),
                  _tile_spec(blocks_nc, 3), coef, coef, vec],
        out_specs=[rows, vec, blocks, blocks, blocks, blocks, coef, coef],
        out_shape=[jax.ShapeDtypeStruct((n_rows, SSM_WIDTH), MXU_DTYPE), jax.ShapeDtypeStruct((1, SSM_WIDTH), F32),
                   bshape, bshape, bshape, bshape, cshape, cshape],
        scratch_shapes=[pltpu.VMEM((tc * STATE_VREG_ROWS, LANES), F32)] * 2 + [pltpu.VMEM((4, STATE_VREG_ROWS, LANES), F32)]
        + [pltpu.VMEM((SSM_TILES, LANES, SSM_WIDE), MXU_DTYPE)] * 2 + [pltpu.VMEM((SSM_TILES, SSM_WIDE, LANES), MXU_DTYPE)] * 2
        + [pltpu.VMEM((SSM_TILES, LANES, SSM_WIDE), F32)] * 4,
        compiler_params=_params("arbitrary"),
    )(proj, dy, h_re, h_im, blocks_cn, blocks_cn, blocks_nc, blocks_nc, a_re, a_im, gain)


def _scan_rows(t):
    return pl.ds(pl.multiple_of(t * STATE_VREG_ROWS, 8), STATE_VREG_ROWS)


GATHER_GROUPS = (("w_glu", "w_att_up", "w_mix_out"), ("w_xq", "w_xkv", "w_xo", "w_ff1", "w_ff2"))
SCATTER_GROUPS = (("w_ff2", "w_ff1"), ("w_xo", "w_xq", "w_xkv", "w_mix_out"), ("w_att_up", "w_glu"), ("w_in",))


def _local_grads(x, mem, pos_col, target, sm, fetch_in, fetch, send, send_small, start_token):
    b_re_t = sm["ssm_b_re"].transpose(2, 0, 1)
    b_im_t = sm["ssm_b_im"].transpose(2, 0, 1)
    logdt = sm["ssm_log_dt"].reshape(SSM_GROUPS, 1)
    c_re, c_im = sm["ssm_c_re"], sm["ssm_c_im"]
    grp = (SSM_GROUPS, SSM_STATE)
    chn = (SSM_GROUP, SSM_GROUPS, SSM_STATE)

    wts = {}
    cos_t, sin_t = _rope_tables(pos_col, after=start_token, name="rope_tables")
    h0, xh0, rs0, h0m = _ln_fwd(x, sm["ln_in_g"], sm["ln_in_b"], name="ln_in_fwd")
    disc_in = (logdt, sm["ssm_a_re"], sm["ssm_a_im"], b_re_t, b_im_t)
    ab_re, ab_im, bb_re_t, bb_im_t = _whole(_disc, disc_in, [grp, grp, chn, chn], name="ssm_disc")
    a_re_rows, a_im_rows = ab_re.reshape(STATE_VREG_ROWS, LANES), ab_im.reshape(STATE_VREG_ROWS, LANES)
    tiles_cn = jnp.stack([bb_re_t.transpose(1, 0, 2), bb_im_t.transpose(1, 0, 2), c_re, -c_im])
    tiles_nc = jnp.stack([c_re.transpose(0, 2, 1), -c_im.transpose(0, 2, 1), bb_re_t.transpose(1, 2, 0),
                          bb_im_t.transpose(1, 2, 0)])
    w_in_near, near_ids = fetch_in(0, [h0m, tiles_cn, tiles_nc])
    proj = _mm_shards(h0m, w_in_near, sm["b_in"], near_ids, name="in_proj_near")
    wts["w_in"], far_ids = fetch_in(1, [proj])
    proj = _mm_shards(h0m, wts["w_in"], sm["b_in"], far_ids, prev=proj, name="in_proj_far")

    y, gy, h_re, h_im = _ssm_fwd(proj, tiles_cn, tiles_nc, a_re_rows, a_im_rows, sm["ssm_d"], name="ssm_fwd")

    q, k, v = _qkv_split(proj, cos_t, sin_t, name="qkv_split")
    outs, lses = [], []
    for g, dil in enumerate(DILATIONS):
        o_g, l_g = _dil_fwd(q[g], k[g], v[g], dil, name=f"dil_att_fwd_{dil}")
        outs.append(o_g)
        lses.append(l_g)
    att, lse = _att_merge(outs, lses, name="att_merge")
    wts.update(fetch(0, [att]))
    z = _mm(gy, wts["w_glu"], bias=sm["b_glu"], b_shards=True, name="glu_proj")
    b_att = _mm(att, wts["w_att_up"], b_shards=True, name="att_up")

    mixed, h1, xh1, rs1, h1m = _mix_out_ln(proj, z, b_att, wts["w_mix_out"], sm["b_mix_out"], h0, sm["ln1_g"],
                                           sm["ln1_b"], alpha=DEEPNORM_ALPHA, name="gate_mix_out_ln1")

    wts.update(fetch(1, [h1m]))
    xq = _mm(h1m, wts["w_xq"], out_dtype=MXU_DTYPE, name="xatt_q")
    kv = _mm(mem, wts["w_xkv"], out_dtype=MXU_DTYPE, b_shards=True, name="xatt_kv")
    xo_in = _xatt_fwd(xq, kv, name="xatt_fwd")
    h2, xh2, rs2, h2m = _mm_ln_fwd(xo_in, wts["w_xo"], None, h1, sm["ln2_g"], sm["ln2_b"], alpha=DEEPNORM_ALPHA,
                                   name="xatt_o_ln2")

    pre, act = _mm(h2m, wts["w_ff1"], bias=sm["b_ff1"], b_shards=True, name="ff1",
                   also=(lambda r: jnp.square(jnp.maximum(r, 0.0)), MXU_DTYPE))
    ff = _mm(act, wts["w_ff2"], bias=sm["b_ff2"], name="ff2")

    gw, gs = {}, {}
    dr3, dr3m, gs["ln3_g"], gs["ln3_b"], gs["b_ff2"], loss_row = _ln_loss_bwd(
        h2, ff, target, sm["ln3_g"], sm["ln3_b"], alpha=DEEPNORM_ALPHA, name="ln3_loss")
    wgrad = functools.partial(_mm, ta=True, out_dtype=WIRE_DTYPE, tk=2048)
    gw["w_ff2"] = wgrad(act, dr3m, tk=1024, name="ff2_dw")
    dpre, gs["b_ff1"] = _mm(dr3m, wts["w_ff2"], tb=True, out_dtype=MXU_DTYPE, colsum=True, name="ff2_dx",
                            gate=(pre, lambda p: 2.0 * jnp.maximum(p, 0.0)))
    gw["w_ff1"] = wgrad(h2m, dpre, out_shards=True, name="ff1_dw")
    sent = send(0, gw)
    dh2 = _mm(dpre, wts["w_ff1"], tb=True, b_shards=True, after=sent, name="ff1_dx")
    dr2, dr2m, gs["ln2_g"], gs["ln2_b"], _ = _ln_bwd(dr3, dh2, xh2, rs2, sm["ln2_g"], alpha=DEEPNORM_ALPHA,
                                                     name="ln2_bwd")
    gw["w_xo"] = wgrad(xo_in, dr2m, name="xatt_o_dw")
    dxo_in = _mm(dr2m, wts["w_xo"], tb=True, out_dtype=MXU_DTYPE, name="xatt_o_dx")
    dxq, dkv = _xatt_bwd(xq, kv, dxo_in, name="xatt_bwd")
    gw["w_xq"] = wgrad(h1m, dxq, name="xatt_q_dw")
    gw["w_xkv"] = wgrad(mem, dkv, out_shards=True, name="xatt_kv_dw")
    dr1, dr1m, gs["ln1_g"], gs["ln1_b"], gs["b_mix_out"] = _mm_ln_bwd(
        dxq, wts["w_xq"], dr2, xh1, rs1, sm["ln1_g"], alpha=DEEPNORM_ALPHA, name="xatt_q_dx_ln1")
    gw["w_mix_out"] = wgrad(mixed, dr1m, name="mix_out_dw")
    sent = send(1, gw)
    dmixed = _mm(dr1m, wts["w_mix_out"], tb=True, after=sent, name="mix_out_dx")
    dgs, dga, dz, db_att, s_gs, s_ga, gs["b_glu"] = _mix_bwd(dmixed, proj, z, b_att, name="gate_mix_bwd")

    gw["w_att_up"] = wgrad(att, db_att, out_shards=True, name="att_up_dw")
    gw["w_glu"] = wgrad(gy, dz, out_shards=True, name="glu_dw")
    sent = send(2, gw)
    datt = _mm(db_att, wts["w_att_up"], tb=True, b_shards=True, after=sent, name="att_up_dx")
    stats = _att_stats(datt, att, lse, name="att_stats")
    dqkv = [_dil_bwd(q[g], k[g], v[g], datt, stats, dil, name=f"dil_att_bwd_{dil}") for g, dil in enumerate(DILATIONS)]

    dgy = _mm(dz, wts["w_glu"], tb=True, b_shards=True, name="glu_dx")
    dy, gs["ssm_d"] = _gelu_bwd(dgy, y, proj, name="gelu_bwd")
    du, s_u, dc_re_t, dc_im_t, dbb_re_t, dbb_im_t, da_re, da_im = _ssm_bwd(
        proj, dy, h_re, h_im, tiles_cn, tiles_nc, a_re_rows, a_im_rows, sm["ssm_d"], name="ssm_bwd")
    gs["ssm_c_re"], gs["ssm_c_im"] = dc_re_t, -dc_im_t
    disc_ct = (da_re.reshape(grp), da_im.reshape(grp), dbb_re_t.transpose(1, 0, 2), dbb_im_t.transpose(1, 0, 2))
    d_logdt, gs["ssm_a_re"], gs["ssm_a_im"], d_b_re_t, d_b_im_t = _whole(
        _disc_transpose, disc_in + disc_ct, [(SSM_GROUPS, 1), grp, grp, chn, chn], name="ssm_disc_bwd")
    gs["ssm_log_dt"] = d_logdt
    gs["ssm_b_re"], gs["ssm_b_im"] = d_b_re_t.transpose(1, 2, 0), d_b_im_t.transpose(1, 2, 0)

    dproj, s_qkv = _dproj_assemble(du, dqkv, dgs, dga, cos_t, sin_t, name="dproj_assemble")
    gs["b_in"] = jnp.concatenate([s_u, *s_qkv, s_gs, s_ga], axis=1)
    sent = send_small(gs, SMALL_EARLY)
    gw["w_in"] = wgrad(h0m, dproj, out_shards=True, after=sent, name="in_proj_dw")
    sent = send(3, gw)
    dh0 = _mm(dproj, wts["w_in"], tb=True, b_shards=True, after=sent, name="in_proj_dx")
    grad_x, gs["ln_in_g"], gs["ln_in_b"], _ = _ln_bwd(dr1, dh0, xh0, rs0, sm["ln_in_g"], alpha=DEEPNORM_ALPHA,
                                                      operand=False, name="ln_in_bwd")
    return loss_row, grad_x, gs


_IN_HBM = pl.BlockSpec(memory_space=pltpu.HBM)
_IN_SEMAPHORE = pl.BlockSpec(memory_space=pltpu.SEMAPHORE)


def _device_index():
    return 4 * lax.axis_index("x") + 2 * lax.axis_index("y") + lax.axis_index("c")


ALL_PEERS = tuple(range(1, N_DEV))
NEAR_PEERS = (1, 2, 3, 4, 5)
FAR_PEERS = (6, 7)


def _peer_index(kk):
    x, y, c = lax.axis_index("x"), lax.axis_index("y"), lax.axis_index("c")
    return 4 * ((x + (kk >> 2)) % 2) + 2 * ((y + ((kk >> 1) & 1)) % 2) + (c + (kk & 1)) % 2


def _exchange_copies(src_refs, land_refs, send_sems, recv_sems, scatter, peers):
    x, y, c = lax.axis_index("x"), lax.axis_index("y"), lax.axis_index("c")
    me = 4 * x + 2 * y + c
    pairs = []
    for a, (src_ref, land_ref) in enumerate(zip(src_refs, land_refs)):
        for idx, kk in enumerate(peers):
            px = (x + (kk >> 2)) % 2
            py = (y + ((kk >> 1) & 1)) % 2
            pc = (c + (kk & 1)) % 2
            peer = 4 * px + 2 * py + pc
            sem = a * len(peers) + idx
            src = src_ref.at[peer] if scatter else src_ref

            def copy(dst, src=src, sem=sem, px=px, py=py, pc=pc):
                return pltpu.make_async_remote_copy(
                    src_ref=src, dst_ref=dst, send_sem=send_sems.at[sem], recv_sem=recv_sems.at[sem],
                    device_id=(px, py, pc), device_id_type=pl.DeviceIdType.MESH)

            pairs.append((functools.partial(copy, land_ref.at[me]), functools.partial(copy, land_ref.at[peer])))
    return pairs


def _own_copies(src_refs, land_refs, own_sems, scatter):
    me = _device_index()
    return [functools.partial(pltpu.make_async_copy, src_ref.at[me] if scatter else src_ref, land_ref.at[me],
                              own_sems.at[a]) for a, (src_ref, land_ref) in enumerate(zip(src_refs, land_refs))]


def _exchange_start(srcs, *, scatter, name, after=None, peers=ALL_PEERS, lands=None):
    n_arr, n_sem = len(srcs), len(srcs) * len(peers)
    own = lands is None
    if own:
        lands = [lax.empty((N_DEV,) + tuple(s.shape[1:] if scatter else s.shape), s.dtype) for s in srcs]
    n_in = 2 * n_arr + (after is not None)

    def body(*refs):
        send_sems, recv_sems = refs[n_in], refs[n_in + 1]
        for sent, _ in _exchange_copies(refs[:n_arr], refs[n_arr:2 * n_arr], send_sems, recv_sems, scatter, peers):
            sent().start()
        if own:
            for local in _own_copies(refs[:n_arr], refs[n_arr:2 * n_arr], refs[n_in + 2], scatter):
                local().start()
        refs[-1][...] = jnp.zeros_like(refs[-1])

    sems = [pltpu.SemaphoreType.DMA((n_sem,)), pltpu.SemaphoreType.DMA((n_sem,))] + [pltpu.SemaphoreType.DMA((n_arr,))] * own
    through = [pltpu.HBM(t.shape, t.dtype) for t in (*srcs, *lands)]
    res = pl.pallas_call(
        body, name=name, out_shape=(*sems, *through, jax.ShapeDtypeStruct((8, LANES), F32)),
        in_specs=[_IN_HBM] * (2 * n_arr) + [pl.BlockSpec(memory_space=pl.ANY)] * (after is not None),
        out_specs=(*[_IN_SEMAPHORE] * len(sems), *[_IN_HBM] * (2 * n_arr), pl.BlockSpec(memory_space=pltpu.VMEM)),
        input_output_aliases={i: len(sems) + i for i in range(2 * n_arr)},
        compiler_params=pltpu.CompilerParams(has_side_effects=pltpu.SideEffectType.DATAFLOW_SIDE_EFFECTING),
    )(*[pltpu.with_memory_space_constraint(t, pltpu.HBM) for t in (*srcs, *lands)],
      *([after] if after is not None else []))
    first = len(sems)
    handle = dict(sems=res[:first], srcs=res[first:first + n_arr], lands=res[first + n_arr:first + 2 * n_arr],
                  scatter=scatter, peers=peers, own=own)
    return handle, res[-1]


def _exchange_wait(handle, *, after, name, srcs=None, lands=None):
    srcs = handle["srcs"] if srcs is None else srcs
    lands = handle["lands"] if lands is None else lands
    sems, scatter, peers, own = handle["sems"], handle["scatter"], handle["peers"], handle["own"]
    n_arr = len(srcs)
    after = list(after)

    def body(*refs):
        src_refs, land_refs = refs[:n_arr], refs[n_arr:2 * n_arr]
        for sent, received in _exchange_copies(src_refs, land_refs, refs[2 * n_arr], refs[2 * n_arr + 1], scatter, peers):
            sent().wait_send()
            received().wait_recv()
        if own:
            for local in _own_copies(src_refs, land_refs, refs[2 * n_arr + 2], scatter):
                local().wait()

    res = pl.pallas_call(
        body, name=name, out_shape=tuple(pltpu.HBM(t.shape, t.dtype) for t in (*srcs, *lands)),
        in_specs=[_IN_HBM] * (2 * n_arr) + [_IN_SEMAPHORE] * len(sems) + [pl.BlockSpec(memory_space=pl.ANY)] * len(after),
        out_specs=tuple([_IN_HBM] * (2 * n_arr)), input_output_aliases={i: i for i in range(2 * n_arr)},
        compiler_params=pltpu.CompilerParams(has_side_effects=pltpu.SideEffectType.DATAFLOW_SIDE_EFFECTING),
    )(*srcs, *lands, *sems, *after)
    return res[:n_arr], res[n_arr:]


def _adamw(g, w, m, v):
    m_new = ADAM_B1 * m + (1.0 - ADAM_B1) * g
    v_new = ADAM_B2 * v + (1.0 - ADAM_B2) * jnp.square(g)
    m_hat = m_new / (1.0 - ADAM_B1 ** ADAM_STEP)
    v_hat = v_new / (1.0 - ADAM_B2 ** ADAM_STEP)
    return g, -ADAM_LR * (m_hat / (jnp.sqrt(v_hat) + ADAM_EPS) + ADAM_WD * w), m_new, v_new


def _reduce_adamw(gstack, w, m, v, *, name, tr=128):
    n_rows, cols = w.shape
    tr = min(tr, n_rows)
    assert n_rows % tr == 0, (name, n_rows, tr)

    def body(g_ref, w_ref, m_ref, v_ref, *out_refs):
        g = g_ref[0].astype(F32)
        for dev in range(1, N_DEV):
            g = g + g_ref[dev].astype(F32)
        for o_ref, val in zip(out_refs, _adamw(g, w_ref[...], m_ref[...], v_ref[...])):
            o_ref[...] = val

    flat = pl.BlockSpec((tr, cols), lambda i: (i, 0))
    shape = jax.ShapeDtypeStruct((n_rows, cols), F32)
    return pl.pallas_call(
        body, name=name, grid=(n_rows // tr,),
        in_specs=[pl.BlockSpec((N_DEV, tr, cols), lambda i: (0, i, 0)), flat, flat, flat],
        out_specs=[flat] * 4, out_shape=[shape] * 4, compiler_params=_params("parallel"),
    )(gstack, w, m, v)


SMALL_FLAT_SSM = ("ssm_b_re", "ssm_b_im", "ssm_c_re", "ssm_c_im")


def _small_view(name, shape):
    size = int(np.prod(shape))
    if name in SMALL_FLAT_SSM:
        return SSM_GROUPS, size // SSM_GROUPS
    if name in ("ssm_a_re", "ssm_a_im"):
        return SSM_GROUPS, SSM_STATE
    return 1, size


def _pack_rows(view):
    return -(-(view[0] * view[1]) // PACK_COLS)


SMALL_LATE = ("ln_in_g", "ln_in_b")
SMALL_EARLY = tuple(n for n in SMALL if n not in SMALL_LATE)


def _pack_small(gs, names, views):
    parts = []
    for n in names:
        flat = gs[n].reshape(-1).astype(WIRE_DTYPE)
        parts.append(jnp.pad(flat, (0, _pack_rows(views[n]) * PACK_COLS - flat.shape[0])))
    total = sum(p.shape[0] for p in parts) // PACK_COLS
    parts.append(jnp.zeros(((-total % PACK_ROW_ALIGN) * PACK_COLS,), WIRE_DTYPE))
    return jnp.concatenate(parts).reshape(-1, PACK_COLS)


def _small_pieces(view):
    rows, cols = view
    if cols == PACK_COLS:
        return [(0, rows, 0, 0, 0, cols)]
    if rows == 1 and cols > PACK_COLS:
        return [(kk, 1, 0, 0, kk * PACK_COLS, PACK_COLS) for kk in range(cols // PACK_COLS)]
    if rows == 1:
        return [(0, 1, 0, 0, 0, cols)]
    return [((r * cols) // PACK_COLS, 1, (r * cols) % PACK_COLS, r, 0, cols) for r in range(rows)]


def _adamw_small(stacks, views, w, m, v, *, name):
    n = len(SMALL)
    place, first = {}, [0, 0]
    for k, names in enumerate((SMALL_EARLY, SMALL_LATE)):
        for name_ in names:
            place[name_] = (k, first[k])
            first[k] += _pack_rows(views[name_])

    def body(early_ref, late_ref, *refs):
        ins, outs = refs[:3 * n], refs[3 * n:]
        for i, name_ in enumerate(SMALL):
            stack_ref = (early_ref, late_ref)[place[name_][0]]
            row0 = place[name_][1]
            for prow, nrows, lane, orow, ocol, width in _small_pieces(views[name_]):
                src = (slice(row0 + prow, row0 + prow + nrows), slice(lane, lane + width))
                dst = (slice(orow, orow + nrows), slice(ocol, ocol + width))
                g = stack_ref[(0,) + src].astype(F32)
                for dev in range(1, N_DEV):
                    g = g + stack_ref[(dev,) + src].astype(F32)
                res = _adamw(g, ins[i][dst], ins[n + i][dst], ins[2 * n + i][dst])
                for kk, val in enumerate(res):
                    outs[kk * n + i][dst] = val

    args = [*stacks, *[d[name_] for d in (w, m, v) for name_ in SMALL]]
    out_views = [views[name_] for _ in range(4) for name_ in SMALL]
    res = pl.pallas_call(
        body, name=name, grid=(1,), in_specs=[_full_spec(t.shape) for t in args],
        out_specs=[_full_spec(s) for s in out_views], out_shape=[jax.ShapeDtypeStruct(s, F32) for s in out_views],
        compiler_params=_params("arbitrary"),
    )(*args)
    return [dict(zip(SMALL, res[kk * n:(kk + 1) * n])) for kk in range(4)]


def kernel(x, mem, positions, ln_in_g, ln_in_b, w_in, b_in, ssm_log_dt, ssm_a_re, ssm_a_im, ssm_b_re, ssm_b_im, ssm_c_re, ssm_c_im, ssm_d, w_glu, b_glu, w_att_up, w_mix_out, b_mix_out, ln1_g, ln1_b, w_xq, w_xkv, w_xo, ln2_g, ln2_b, w_ff1, b_ff1, w_ff2, b_ff2, ln3_g, ln3_b, loss_target, m_ln_in_g, m_ln_in_b, m_w_in, m_b_in, m_ssm_log_dt, m_ssm_a_re, m_ssm_a_im, m_ssm_b_re, m_ssm_b_im, m_ssm_c_re, m_ssm_c_im, m_ssm_d, m_w_glu, m_b_glu, m_w_att_up, m_w_mix_out, m_b_mix_out, m_ln1_g, m_ln1_b, m_w_xq, m_w_xkv, m_w_xo, m_ln2_g, m_ln2_b, m_w_ff1, m_b_ff1, m_w_ff2, m_b_ff2, m_ln3_g, m_ln3_b, v_ln_in_g, v_ln_in_b, v_w_in, v_b_in, v_ssm_log_dt, v_ssm_a_re, v_ssm_a_im, v_ssm_b_re, v_ssm_b_im, v_ssm_c_re, v_ssm_c_im, v_ssm_d, v_w_glu, v_b_glu, v_w_att_up, v_w_mix_out, v_b_mix_out, v_ln1_g, v_ln1_b, v_w_xq, v_w_xkv, v_w_xo, v_ln2_g, v_ln2_b, v_w_ff1, v_b_ff1, v_w_ff2, v_b_ff2, v_ln3_g, v_ln3_b):
    given = dict(locals())
    w_arg = {n: given[n] for n in WEIGHTS}
    m_arg = {n: given["m_" + n] for n in WEIGHTS}
    v_arg = {n: given["v_" + n] for n in WEIGHTS}

    in_near, token = _exchange_start([w_arg["w_in"][0].astype(MXU_DTYPE)], scatter=False, peers=NEAR_PEERS,
                                     name="gather_start_in_near")
    in_far, token = _exchange_start(in_near["srcs"], scatter=False, peers=FAR_PEERS, lands=in_near["lands"],
                                    after=token, name="gather_start_in_far")
    w_in_state = [in_far["srcs"], in_far["lands"]]
    token, w_arg, m_arg, v_arg = lax.optimization_barrier((token, w_arg, m_arg, v_arg))
    shards = {n: w_arg[n][0].astype(MXU_DTYPE) for n in BIG if n != "w_in"}
    gathers = []
    for i, names in enumerate(GATHER_GROUPS):
        handle, token = _exchange_start([shards[n] for n in names], scatter=False, after=token, name=f"gather_start_{i}")
        gathers.append(handle)

    small_views = {n: _small_view(n, w_arg[n].shape) for n in SMALL}
    small_w, small_m, small_v = [{n: d[n].reshape(small_views[n]) for n in SMALL} for d in (w_arg, m_arg, v_arg)]
    relaid = [d[n] for d in (small_w, small_m, small_v) for n in SMALL_FLAT_SSM]

    def fetch_in(part, after):
        handle, peers, tag = ((in_near, (0,) + NEAR_PEERS, "near"), (in_far, FAR_PEERS, "far"))[part]
        w_in_state[:] = _exchange_wait(handle, after=after + (relaid if part == 0 else []), srcs=w_in_state[0],
                                       lands=w_in_state[1], name="gather_wait_in_" + tag)
        return w_in_state[1][0], jnp.stack([_peer_index(kk) for kk in peers]).astype(jnp.int32)

    def fetch(i, after):
        _, lands = _exchange_wait(gathers[i], after=after, name=f"gather_wait_{i}")
        full = dict(zip(GATHER_GROUPS[i], lands))
        return {n: t if n in BIG_COL_SHARDED else t.reshape(-1, t.shape[-1]) for n, t in full.items()}

    scatters = {}

    def send(i, gw):
        slots = [gw[n] if n in BIG_COL_SHARDED else gw[n].reshape(N_DEV, -1, gw[n].shape[-1]) for n in SCATTER_GROUPS[i]]
        handle, sent = _exchange_start(slots, scatter=True, name=f"scatter_start_{i}")
        scatters[i] = (handle, slots)
        return sent

    sm = {}
    for n in SMALL:
        t = w_arg[n]
        if n.startswith("ssm_") and n not in ("ssm_d", "ssm_log_dt"):
            sm[n] = t[0]
        else:
            sm[n] = t.reshape(1, -1)

    smalls = []

    def send_small(gs, names):
        handle, sent = _exchange_start([_pack_small(gs, names, small_views)], scatter=False,
                                       name=f"small_start_{len(smalls)}")
        smalls.append(handle)
        return sent

    loss_row, grad_x, gs = _local_grads(x[0], mem[0], positions.reshape(-1, 1), loss_target[0], sm, fetch_in, fetch,
                                        send, send_small, token)
    loss = lax.psum(loss_row[0, 0], ("x", "y", "c"))
    send_small(gs, SMALL_LATE)

    results = [{}, {}, {}, {}]
    done = grad_x
    for i, names in enumerate(SCATTER_GROUPS):
        handle, slots = scatters[i]
        _, lands = _exchange_wait(handle, after=[done], name=f"scatter_wait_{i}")
        for n, land, slot in zip(names, lands, slots):
            res = _reduce_adamw(land, w_arg[n][0], m_arg[n][0], v_arg[n][0], name="adamw_" + n)
            done = res[0]
            for d, r in zip(results, res):
                d[n] = r[None]
    stacks = [_exchange_wait(handle, after=[done], name=f"small_wait_{i}")[1][0] for i, handle in enumerate(smalls)]
    res = _adamw_small(stacks, small_views, small_w, small_m, small_v, name="adamw_small")
    for d, r in zip(results, res):
        d.update({n: r[n].reshape(w_arg[n].shape) for n in SMALL})
    out = [loss, grad_x[None]]
    for d in results:
        out += [d[n] for n in WEIGHTS]
    return tuple(out)
```

```python
import functools

import numpy as np
import jax
import jax.numpy as jnp
from jax import lax
from jax.experimental import pallas as pl
from jax.experimental.pallas import tpu as pltpu

F32 = jnp.float32
MXU_DTYPE = jnp.bfloat16
WIRE_DTYPE = jnp.bfloat16
VMEM_LIMIT_BYTES = 48 * 1024 * 1024
LANES = 128

N_DEV = 8
D_MODEL = 1024
SSM_GROUP = 16
SSM_WIDTH = 768
SSM_GROUPS = SSM_WIDTH // SSM_GROUP
SSM_STATE = 64
SSM_CH = SSM_GROUPS * SSM_STATE
SSM_TILES = SSM_WIDTH // LANES
GROUPS_PER_TILE = LANES // SSM_GROUP
STATE_VREG_ROWS = SSM_CH // LANES
ATT_HEAD_DIM = 64
ATT_HEADS_PER_GROUP = 4
ATT_MERGED = ATT_HEADS_PER_GROUP * ATT_HEAD_DIM
LANE_HALVES = ATT_MERGED // LANES
DILATIONS = (1, 4, 16)
ATT_BLK = 128
ATT_SCALE = ATT_HEAD_DIM ** -0.5
ROT_DIM = ATT_HEAD_DIM // 4
ROPE_THETA = 500000.0
XATT_HEADS = 4
XATT_HEAD_DIM = D_MODEL // XATT_HEADS
XATT_SCALE = XATT_HEAD_DIM ** -0.5
DEEPNORM_ALPHA = 2.0 ** 0.25
LN_EPS = 1e-5
NEG_INF = -1e30
OFF_Q_BLK, OFF_K_BLK, OFF_V_BLK = 3, 6, 9
OFF_GS_BLK, OFF_GA_BLK = 3, 4

ADAM_LR = 0.001
ADAM_B1 = 0.9
ADAM_B2 = 0.999
ADAM_EPS = 1e-08
ADAM_WD = 0.01
ADAM_STEP = 10

BIG = ("w_in", "w_glu", "w_att_up", "w_mix_out", "w_xq", "w_xkv", "w_xo", "w_ff1", "w_ff2")
BIG_COL_SHARDED = ("w_in", "w_glu", "w_att_up", "w_xkv", "w_ff1")
WEIGHTS = ("ln_in_g", "ln_in_b", "w_in", "b_in", "ssm_log_dt", "ssm_a_re", "ssm_a_im", "ssm_b_re", "ssm_b_im",
           "ssm_c_re", "ssm_c_im", "ssm_d", "w_glu", "b_glu", "w_att_up", "w_mix_out", "b_mix_out", "ln1_g", "ln1_b",
           "w_xq", "w_xkv", "w_xo", "ln2_g", "ln2_b", "w_ff1", "b_ff1", "w_ff2", "b_ff2", "ln3_g", "ln3_b")
SMALL = tuple(n for n in WEIGHTS if n not in BIG)
PACK_COLS = 1024
PACK_ROW_ALIGN = 16


def _params(*sem):
    return pltpu.CompilerParams(dimension_semantics=sem, vmem_limit_bytes=VMEM_LIMIT_BYTES)


def _dot(a, b, ca, cb):
    return lax.dot_general(a.astype(MXU_DTYPE), b.astype(MXU_DTYPE), (((ca,), (cb,)), ((), ())),
                           preferred_element_type=F32)


def _fit(dim, pref):
    if dim <= pref:
        return dim
    best = max(t for t in range(LANES, pref + 1, LANES) if dim % t == 0)
    return best


def _mm(a, b, *, name, ta=False, tb=False, bias=None, out_dtype=F32, b_shards=False, out_shards=False, after=None,
        also=None, gate=None, colsum=False, epilogue=None, tm=2048, tn=1024, tk=1024):
    m, k = (a.shape[1], a.shape[0]) if ta else a.shape
    order = (lambda f: (lambda j, i, kk: f(i, j, kk))) if colsum else (lambda f: f)
    spec = lambda shape, f: pl.BlockSpec(shape, order(f))
    if b_shards:
        n_sh, rows, n_loc = b.shape
        if tb:
            n, tn, tk = rows, _fit(rows, tn), n_loc
            assert k == n_sh * n_loc, (name, k, b.shape)
            b_spec = spec((1, tn, tk), lambda i, j, kk: (kk, j, 0))
        else:
            n, tn, tk = n_sh * n_loc, n_loc, _fit(k, tk)
            b_spec = spec((1, tk, tn), lambda i, j, kk: (j, kk, 0))
    else:
        n = b.shape[0] if tb else b.shape[1]
        tn = n // N_DEV if out_shards else _fit(n, tn)
        tk = _fit(k, tk)
        b_spec = spec((tn, tk), lambda i, j, kk: (j, kk)) if tb else spec((tk, tn), lambda i, j, kk: (kk, j))
    tm = _fit(m, tm)
    nk = k // tk
    a_spec = spec((tk, tm), lambda i, j, kk: (kk, i)) if ta else spec((tm, tk), lambda i, j, kk: (i, kk))
    tile = spec((tm, tn), lambda i, j, kk: (i, j))
    in_specs, args = [a_spec, b_spec], [a, b]
    if bias is not None:
        in_specs.append(spec((1, tn), lambda i, j, kk: (0, j)))
        args.append(bias)
    if gate is not None:
        in_specs.append(tile)
        args.append(gate[0])
    if after is not None:
        in_specs.append(pl.BlockSpec(memory_space=pl.ANY))
        args.append(after)
    if epilogue is not None:
        ep_fn, ep_rows, ep_fulls, ep_row_outs, ep_acc_outs = epilogue
        assert tn == n and not (colsum or also or gate or out_shards), name
        ep_first = len(args)
        in_specs += [spec((tm, t.shape[1]), lambda i, j, kk: (i, 0)) for t in ep_rows]
        in_specs += [pl.BlockSpec(t.shape, functools.partial(lambda i, j, kk, nd: (0,) * nd, nd=t.ndim)) for t in ep_fulls]
        args += [*ep_rows, *ep_fulls]
    n_in = len(args)
    if epilogue is not None:
        out_specs = [spec((tm, w), lambda i, j, kk: (i, 0)) for w, _ in ep_row_outs]
        out_specs += [spec((1, w), lambda i, j, kk: (0, 0)) for w in ep_acc_outs]
        out_shape = [jax.ShapeDtypeStruct((m, w), dt) for w, dt in ep_row_outs]
        out_shape += [jax.ShapeDtypeStruct((1, w), F32) for w in ep_acc_outs]
    elif out_shards:
        assert n == N_DEV * tn, (name, n, tn)
        out_specs = [spec((1, tm, tn), lambda i, j, kk: (j, i, 0))]
        out_shape = [jax.ShapeDtypeStruct((N_DEV, m, tn), out_dtype)]
    else:
        out_specs = [tile]
        out_shape = [jax.ShapeDtypeStruct((m, n), out_dtype)]
    if also is not None:
        out_specs.append(tile)
        out_shape.append(jax.ShapeDtypeStruct((m, n), also[1]))
    if colsum:
        out_specs.append(spec((1, tn), lambda i, j, kk: (0, j)))
        out_shape.append(jax.ShapeDtypeStruct((1, n), F32))

    def body(*refs):
        a_ref, b_ref = refs[0], refs[1]
        o_ref = refs[n_in]
        first_row_tile = pl.program_id(1 if colsum else 0) == 0

        def product():
            return _dot(a_ref[...], b_ref[0] if b_shards else b_ref[...], 0 if ta else 1, 1 if tb else 0)

        def finish(r):
            if bias is not None:
                r = r + refs[2][...]
            if gate is not None:
                r = r * gate[1](refs[2 + (bias is not None)][...])
            if epilogue is not None:
                res = ep_fn(r, *[ref[...] for ref in refs[ep_first:n_in]])
                n_o = len(ep_row_outs)
                for ref, val in zip(refs[n_in:n_in + n_o], res[:n_o]):
                    ref[...] = val.astype(ref.dtype)
                acc_refs = refs[n_in + n_o:n_in + n_o + len(ep_acc_outs)]
                if acc_refs:
                    @pl.when(first_row_tile)
                    def _():
                        for ref in acc_refs:
                            ref[...] = jnp.zeros_like(ref)

                    for ref, val in zip(acc_refs, res[n_o:]):
                        ref[...] += val
                return
            if out_shards:
                o_ref[0] = r.astype(o_ref.dtype)
            else:
                o_ref[...] = r.astype(o_ref.dtype)
            if also is not None:
                refs[n_in + 1][...] = also[0](r).astype(also[1])
            if colsum:
                s_ref = refs[n_in + 1 + (also is not None)]

                @pl.when(first_row_tile)
                def _():
                    s_ref[...] = jnp.zeros_like(s_ref)

                s_ref[...] += _colsum(r)

        if nk == 1:
            finish(product())
            return
        acc_ref = refs[-1]
        kk = pl.program_id(2)

        @pl.when(kk == 0)
        def _():
            acc_ref[...] = product()

        if nk > 2:
            @pl.when((kk > 0) & (kk < nk - 1))
            def _():
                acc_ref[...] += product()

        @pl.when(kk == nk - 1)
        def _():
            finish(acc_ref[...] + product())

    grid = (n // tn, m // tm, nk) if colsum else (m // tm, n // tn, nk)
    res = pl.pallas_call(
        body, name=name, grid=grid, in_specs=in_specs, out_specs=out_specs, out_shape=out_shape,
        scratch_shapes=[pltpu.VMEM((tm, tn), F32)] if nk > 1 else [],
        compiler_params=_params("arbitrary" if epilogue is not None else "parallel",
                                "arbitrary" if colsum else "parallel", "arbitrary"),
    )(*args)
    return res[0] if len(res) == 1 else res


def _mm_shards(a, w, bias, shard_ids, *, name, prev=None, tm=2048):
    m, k = a.shape
    n_sh, _, n_loc = w.shape
    tm = _fit(m, tm)

    def body(ids_ref, a_ref, w_ref, b_ref, *rest):
        rest[-1][...] = _dot(a_ref[...], w_ref[0], 1, 0) + b_ref[...]

    grid_spec = pltpu.PrefetchScalarGridSpec(
        num_scalar_prefetch=1, grid=(m // tm, shard_ids.shape[0]),
        in_specs=[pl.BlockSpec((tm, k), lambda i, j, ids: (i, 0)),
                  pl.BlockSpec((1, k, n_loc), lambda i, j, ids: (ids[j], 0, 0)),
                  pl.BlockSpec((1, n_loc), lambda i, j, ids: (0, ids[j]))]
        + [pl.BlockSpec(memory_space=pl.ANY)] * (prev is not None),
        out_specs=pl.BlockSpec((tm, n_loc), lambda i, j, ids: (i, ids[j])))
    return pl.pallas_call(
        body, name=name, grid_spec=grid_spec, out_shape=jax.ShapeDtypeStruct((m, n_sh * n_loc), F32),
        input_output_aliases={4: 0} if prev is not None else {}, compiler_params=_params("parallel", "arbitrary"),
    )(shard_ids, a, w, bias, *([prev] if prev is not None else []))


ROW_TILE = 512


def _rowcall(fn, rows, fulls, row_outs, acc_outs=(), *, n_rows, tm, name, after=None):
    n_r, n_f, n_o, n_a = len(rows), len(fulls), len(row_outs), len(acc_outs)
    n_in = n_r + n_f + (after is not None)
    assert n_rows % tm == 0, (name, n_rows, tm)

    def body(*refs):
        res = fn(*[r[...] for r in refs[:n_r + n_f]])
        res = tuple(res) if isinstance(res, (tuple, list)) else (res,)
        o_refs = refs[n_in:n_in + n_o]
        a_refs = refs[n_in + n_o:]
        for o_ref, val in zip(o_refs, res[:n_o]):
            o_ref[...] = val.astype(o_ref.dtype)
        if n_a:
            @pl.when(pl.program_id(0) == 0)
            def _():
                for a_ref in a_refs:
                    a_ref[...] = jnp.zeros_like(a_ref)

            for a_ref, val in zip(a_refs, res[n_o:]):
                a_ref[...] += val

    in_specs = [pl.BlockSpec((tm, w), functools.partial(lambda i, cb: (i, cb), cb=cb)) for _, w, cb in rows]
    in_specs += [pl.BlockSpec(f.shape, functools.partial(lambda i, nd: (0,) * nd, nd=f.ndim)) for f in fulls]
    in_specs += [pl.BlockSpec(memory_space=pl.ANY)] * (after is not None)
    out_specs = [pl.BlockSpec((tm, w), lambda i: (i, 0)) for w, _ in row_outs]
    out_specs += [pl.BlockSpec((1, w), lambda i: (0, 0)) for w in acc_outs]
    out_shape = [jax.ShapeDtypeStruct((n_rows, w), dt) for w, dt in row_outs]
    out_shape += [jax.ShapeDtypeStruct((1, w), F32) for w in acc_outs]
    return pl.pallas_call(
        body, name=name, grid=(n_rows // tm,), in_specs=in_specs, out_specs=out_specs, out_shape=out_shape,
        compiler_params=_params("arbitrary" if n_a else "parallel"),
    )(*[r[0] for r in rows], *fulls, *([after] if after is not None else []))


def _colsum(v):
    return jnp.sum(v, axis=0, keepdims=True)


def _layer_norm(xin, g, b):
    mu = jnp.mean(xin, axis=-1, keepdims=True)
    xc = xin - mu
    var = jnp.mean(xc * xc, axis=-1, keepdims=True)
    rstd = lax.rsqrt(var + LN_EPS)
    xh = xc * rstd
    return xh * g + b, xh, rstd


def _layer_norm_bwd(dy, xh, rstd, g):
    dyg = dy * g
    m1 = jnp.mean(dyg, axis=-1, keepdims=True)
    m2 = jnp.mean(dyg * xh, axis=-1, keepdims=True)
    dx = rstd * (dyg - m1 - xh * m2)
    return dx, _colsum(dy * xh), _colsum(dy), _colsum(dx)


def _ln_fwd(a, g, b, *, name):
    n_rows, d = a.shape

    def fn(av, gv, bv):
        y, xh, rstd = _layer_norm(av, gv, bv)
        return y, xh, rstd, y

    return _rowcall(fn, [(a, d, 0)], [g, b], [(d, F32), (d, F32), (1, F32), (d, MXU_DTYPE)], n_rows=n_rows, tm=ROW_TILE,
                    name=name)


def _ln_bwd(dya, dyb, xh, rstd, g, *, alpha, name, operand=True):
    n_rows, d = xh.shape

    def fn(da, db, xhv, rs, gv):
        dx, *sums = _layer_norm_bwd(alpha * da + db, xhv, rs, gv)
        return (dx,) + ((dx,) if operand else ()) + tuple(sums)

    rows = [(dya, d, 0), (dyb, d, 0), (xh, d, 0), (rstd, 1, 0)]
    return _rowcall(fn, rows, [g], [(d, F32)] + [(d, MXU_DTYPE)] * operand, [d, d, d], n_rows=n_rows, tm=ROW_TILE, name=name)


LN_EPILOGUE_ROWS = 1024


def _mm_ln_fwd(x, w, bias, a, g, b, *, alpha, name):
    d = a.shape[1]

    def fn(r, av, gv, bv):
        y, xh, rstd = _layer_norm(alpha * av + r, gv, bv)
        return y, xh, rstd, y

    return _mm(x, w, bias=bias, name=name, tm=LN_EPILOGUE_ROWS,
               epilogue=(fn, [a], [g, b], [(d, F32), (d, F32), (1, F32), (d, MXU_DTYPE)], []))


def _mm_ln_bwd(x, w, dya, xh, rstd, g, *, alpha, name):
    d = xh.shape[1]

    def fn(r, da, xhv, rs, gv):
        dx, *sums = _layer_norm_bwd(alpha * da + r, xhv, rs, gv)
        return (dx, dx, *sums)

    return _mm(x, w, tb=True, name=name, tm=LN_EPILOGUE_ROWS,
               epilogue=(fn, [dya, xh, rstd], [g], [(d, F32), (d, MXU_DTYPE)], [d, d, d]))


def _ln_loss_bwd(a, r, target, g, b, *, alpha, name):
    n_rows, d = a.shape

    def fn(av, rv, tv, gv, bv):
        y, xh, rs = _layer_norm(alpha * av + rv, gv, bv)
        diff = y - tv
        part = jnp.sum(jnp.sum(diff * diff, axis=1, keepdims=True), axis=0, keepdims=True) * (0.5 / d)
        dx, *sums = _layer_norm_bwd(diff * (1.0 / d), xh, rs, gv)
        return (dx, dx, *sums, jnp.broadcast_to(part, (1, LANES)))

    return _rowcall(fn, [(a, d, 0), (r, d, 0), (target, d, 0)], [g, b], [(d, F32), (d, MXU_DTYPE)], [d, d, d, LANES],
                    n_rows=n_rows, tm=ROW_TILE, name=name)


def _rope_lane_constants():
    lane = np.arange(ATT_MERGED)
    in_head = lane % ATT_HEAD_DIM
    sign = np.where(in_head < ROT_DIM // 2, -1.0, np.where(in_head < ROT_DIM, 1.0, 0.0)).astype(np.float32)
    inv_freq = ROPE_THETA ** (-jnp.arange(0, ROT_DIM, 2, dtype=F32) / ROT_DIM)
    return inv_freq[lane % (ROT_DIM // 2)].reshape(1, ATT_MERGED), jnp.asarray(sign).reshape(1, ATT_MERGED)


def _rope_tables(pos_col, *, name, after=None):
    inv_lane, sign = _rope_lane_constants()

    def fn(pos, inv, sg):
        ang = pos.astype(F32) * inv
        return jnp.where(sg != 0.0, jnp.cos(ang), 1.0), sg * jnp.sin(ang)

    return _rowcall(fn, [(pos_col, 1, 0)], [inv_lane, sign], [(ATT_MERGED, F32), (ATT_MERGED, F32)],
                    n_rows=pos_col.shape[0], tm=512, name=name, after=after)


def _rot_partner(t):
    lane = lax.broadcasted_iota(jnp.int32, t.shape, 1)
    width = t.shape[1]
    return jnp.where((lane & (ROT_DIM // 2)) == 0, pltpu.roll(t, width - ROT_DIM // 2, 1), pltpu.roll(t, ROT_DIM // 2, 1))


def _rope(t, cos_t, sin_t):
    return t * cos_t + _rot_partner(t) * sin_t


def _rope_transpose(dt, cos_t, sin_t):
    return dt * cos_t + _rot_partner(dt * sin_t)


def _strided_rows(r, count, stride):
    return pl.ds(r, count) if stride == 1 else pl.ds(r, count, stride=stride)


def _qkv_split(proj, cos_t, sin_t, *, name, tm=512):
    n_rows = proj.shape[0]
    n_g = len(DILATIONS)

    def body(*refs):
        n_src = LANE_HALVES * 3 * n_g
        src, tables, dst = refs[:n_src], refs[n_src:n_src + 2 * LANE_HALVES], refs[n_src + 2 * LANE_HALVES:]
        for kind in range(3):
            for g, dil in enumerate(DILATIONS):
                for half in range(LANE_HALVES):
                    x_ref, o_ref = src[(kind * n_g + g) * LANE_HALVES + half], dst[kind * n_g + g]
                    cos_ref, sin_ref = tables[half], tables[LANE_HALVES + half]
                    for r in range(dil):
                        rows = _strided_rows(r, tm // dil, dil)
                        t = x_ref[rows, :]
                        if kind < 2:
                            t = _rope(t, cos_ref[rows, :], sin_ref[rows, :])
                        lo = r * ATT_MERGED + half * LANES
                        o_ref[:, lo:lo + LANES] = t.astype(o_ref.dtype)

    half_spec = lambda cb: pl.BlockSpec((tm, LANES), functools.partial(lambda i, cb: (i, cb), cb=cb))
    in_specs = [half_spec((off + g) * LANE_HALVES + half)
                for off in (OFF_Q_BLK, OFF_K_BLK, OFF_V_BLK) for g in range(n_g) for half in range(LANE_HALVES)]
    in_specs += [half_spec(half) for _ in range(2) for half in range(LANE_HALVES)]
    out_specs = [pl.BlockSpec((tm // dil, dil * ATT_MERGED), lambda i: (i, 0)) for _ in range(3) for dil in DILATIONS]
    out_shape = [jax.ShapeDtypeStruct((n_rows // dil, dil * ATT_MERGED), MXU_DTYPE) for _ in range(3) for dil in DILATIONS]
    outs = pl.pallas_call(
        body, name=name, grid=(n_rows // tm,), in_specs=in_specs, out_specs=out_specs, out_shape=out_shape,
        compiler_params=_params("parallel"),
    )(*[proj] * (LANE_HALVES * 3 * n_g), *[cos_t] * LANE_HALVES, *[sin_t] * LANE_HALVES)
    return outs[:n_g], outs[n_g:2 * n_g], outs[2 * n_g:]


def _mix(gs, ga, z1, z2, b_att):
    return jax.nn.sigmoid(gs) * (z1 * jax.nn.sigmoid(z2)) + jax.nn.sigmoid(ga) * b_att


def _mix_rows(proj, z, b_att):
    return [(proj, D_MODEL, OFF_GS_BLK), (proj, D_MODEL, OFF_GA_BLK), (z, D_MODEL, 0), (z, D_MODEL, 1), (b_att, D_MODEL, 0)]


def _mix_out_ln(proj, z, b_att, w, bias, a, g, b, *, alpha, name):
    def fn(gs, ga, z1, z2, ba, av, wv, biasv, gv, bv):
        mixed = _mix(gs, ga, z1, z2, ba)
        y, xh, rstd = _layer_norm(alpha * av + (_dot(mixed, wv, 1, 0) + biasv), gv, bv)
        return mixed, y, xh, rstd, y

    rows = _mix_rows(proj, z, b_att) + [(a, D_MODEL, 0)]
    outs = [(D_MODEL, MXU_DTYPE), (D_MODEL, F32), (D_MODEL, F32), (1, F32), (D_MODEL, MXU_DTYPE)]
    return _rowcall(fn, rows, [w, bias, g, b], outs, n_rows=proj.shape[0], tm=ROW_TILE // 2, name=name)


def _mix_bwd(dmixed, proj, z, b_att, *, name):
    def fn(dm, gs, ga, z1, z2, ba):
        _, vjp = jax.vjp(_mix, gs, ga, z1, z2, ba)
        dgs, dga, dz1, dz2, dba = vjp(dm)
        dz = jnp.concatenate([dz1, dz2], axis=1)
        return dgs, dga, dz, dba, _colsum(dgs), _colsum(dga), _colsum(dz)

    rows = [(dmixed, D_MODEL, 0)] + _mix_rows(proj, z, b_att)
    widths = [D_MODEL, D_MODEL, 2 * D_MODEL, D_MODEL]
    return _rowcall(fn, rows, [], [(w, MXU_DTYPE) for w in widths], widths[:3], n_rows=proj.shape[0], tm=ROW_TILE, name=name)


def _gelu_bwd(dgy, y, proj, *, name):
    def fn(dg, yv, u):
        _, vjp = jax.vjp(jax.nn.gelu, yv)
        dy = vjp(dg)[0]
        return dy, _colsum(dy * u)

    return _rowcall(fn, [(dgy, SSM_WIDTH, 0), (y, SSM_WIDTH, 0), (proj, SSM_WIDTH, 0)], [], [(SSM_WIDTH, F32)],
                    [SSM_WIDTH], n_rows=y.shape[0], tm=512, name=name)


HEAD_ROWS = ATT_HEADS_PER_GROUP * ATT_BLK


def _head_masks(rows):
    head = lax.broadcasted_iota(jnp.int32, (rows, ATT_MERGED), 1) >> (ATT_HEAD_DIM.bit_length() - 1)
    return [head == h for h in range(ATT_HEADS_PER_GROUP)]


def _stack_heads(t, masks):
    return jnp.concatenate([jnp.where(m, t, jnp.zeros_like(t)) for m in masks], axis=0)


def _unstack_heads(t4, masks):
    blocks = [t4[h * ATT_BLK:(h + 1) * ATT_BLK] for h in range(ATT_HEADS_PER_GROUP)]
    return jnp.where(masks[0], blocks[0], jnp.where(masks[1], blocks[1], jnp.where(masks[2], blocks[2], blocks[3])))


def _head_column(stats, first):
    return jnp.concatenate([stats[:, first + h:first + h + 1] for h in range(ATT_HEADS_PER_GROUP)], axis=0)


def _band_mask(first_key):
    qi = lax.broadcasted_iota(jnp.int32, (HEAD_ROWS, 2 * ATT_BLK), 0) & (ATT_BLK - 1)
    ki = lax.broadcasted_iota(jnp.int32, (HEAD_ROWS, 2 * ATT_BLK), 1)
    steps = qi + ATT_BLK - ki
    return (steps >= 0) & (steps <= ATT_BLK) & (ki >= first_key)


def _dil_fwd(q, k, v, dil, *, name):
    n_blk = q.shape[0] // ATT_BLK
    cur = pl.BlockSpec((ATT_BLK, ATT_MERGED), lambda r, n: (n, r))
    prev = pl.BlockSpec((ATT_BLK, ATT_MERGED), lambda r, n: (jnp.maximum(n - 1, 0), r))

    def body(q_ref, kp_ref, kc_ref, vp_ref, vc_ref, o_ref, l_ref):
        masks = _head_masks(ATT_BLK)
        valid = _band_mask(jnp.where(pl.program_id(1) > 0, 0, ATT_BLK))
        keys = jnp.concatenate([kp_ref[...], kc_ref[...]], axis=0)
        vals = jnp.concatenate([vp_ref[...], vc_ref[...]], axis=0)
        s = jnp.where(valid, _dot(_stack_heads(q_ref[...], masks), keys, 1, 1) * ATT_SCALE, NEG_INF)
        m = jnp.max(s, axis=-1, keepdims=True)
        p = jnp.exp(s - m)
        den = jnp.sum(p, axis=-1, keepdims=True)
        o_ref[...] = _unstack_heads(_dot(p, vals, 1, 0) / den, masks)
        l_ref[...] = _unstack_heads(jnp.broadcast_to(m + jnp.log(den), (HEAD_ROWS, ATT_MERGED)), masks)

    shape = jax.ShapeDtypeStruct(q.shape, F32)
    return pl.pallas_call(
        body, name=name, grid=(dil, n_blk), in_specs=[cur, prev, cur, prev, cur], out_specs=[cur, cur],
        out_shape=[shape, shape], compiler_params=_params("parallel", "parallel"),
    )(q, k, k, v, v)


def _att_merge(outs, lses, *, name, tm=512):
    n_g = len(outs)
    n_rows = outs[0].shape[0] * DILATIONS[0]

    def body(*refs):
        src, (att_ref, lse_ref), tmp = refs[:2 * n_g], refs[2 * n_g:2 * n_g + 2], refs[2 * n_g + 2:]
        vals = []
        for idx, src_ref in enumerate(src):
            dil = DILATIONS[idx % n_g]
            if dil == 1:
                vals.append(src_ref[...])
                continue
            for r in range(dil):
                for half in range(LANE_HALVES):
                    lo = r * ATT_MERGED + half * LANES
                    tmp[LANE_HALVES * idx + half][_strided_rows(r, tm // dil, dil), :] = src_ref[:, lo:lo + LANES]
            vals.append(jnp.concatenate([tmp[LANE_HALVES * idx + half][...] for half in range(LANE_HALVES)], axis=1))
        o, l = vals[:n_g], vals[n_g:]
        m = functools.reduce(jnp.maximum, l)
        e = [jnp.exp(li - m) for li in l]
        z = functools.reduce(jnp.add, e)
        att_ref[...] = functools.reduce(jnp.add, [(ei / z) * oi for ei, oi in zip(e, o)])
        lse_ref[...] = m + jnp.log(z)

    in_specs = [pl.BlockSpec((tm // dil, dil * ATT_MERGED), lambda i: (i, 0)) for _ in range(2) for dil in DILATIONS]
    row = pl.BlockSpec((tm, ATT_MERGED), lambda i: (i, 0))
    shape = jax.ShapeDtypeStruct((n_rows, ATT_MERGED), F32)
    return pl.pallas_call(
        body, name=name, grid=(n_rows // tm,), in_specs=in_specs, out_specs=[row, row], out_shape=[shape, shape],
        scratch_shapes=[pltpu.VMEM((tm, LANES), F32)] * (LANE_HALVES * 2 * n_g), compiler_params=_params("parallel"),
    )(*outs, *lses)


def _att_stats(datt, att, lse, *, name):
    n_rows = datt.shape[0]

    def fn(d, a, l):
        prod = d * a
        lane = lax.broadcasted_iota(jnp.int32, (d.shape[0], LANES), 1)
        out = jnp.zeros((d.shape[0], LANES), F32)
        for h in range(ATT_HEADS_PER_GROUP):
            lo = h * ATT_HEAD_DIM
            out = jnp.where(lane == h, l[:, lo:lo + 1], out)
            delta = jnp.sum(prod[:, lo:lo + ATT_HEAD_DIM], axis=-1, keepdims=True)
            out = jnp.where(lane == ATT_HEADS_PER_GROUP + h, delta, out)
        return out

    rows = [(t, ATT_MERGED, 0) for t in (datt, att, lse)]
    return _rowcall(fn, rows, [], [(LANES, F32)], n_rows=n_rows, tm=512, name=name)[0]


def _dil_bwd(q, k, v, datt, stats, dil, *, name):
    n_rows = datt.shape[0]
    n_blk = n_rows // dil // ATT_BLK
    span = ATT_BLK * dil
    cur = pl.BlockSpec((ATT_BLK, ATT_MERGED), lambda n, r: (n, r))
    prev = pl.BlockSpec((ATT_BLK, ATT_MERGED), lambda n, r: (jnp.maximum(n - 1, 0), r))
    nxt = pl.BlockSpec((ATT_BLK, ATT_MERGED), lambda n, r: (jnp.minimum(n + 1, n_blk - 1), r))
    seq = lambda half, ahead: pl.BlockSpec((span, LANES), lambda n, r: (jnp.minimum(n + ahead, n_blk - 1), half))

    def body(qc_ref, qn_ref, kp_ref, kc_ref, vp_ref, vc_ref, dc0_ref, dc1_ref, dn0_ref, dn1_ref, sc_ref, sn_ref,
             dq0_ref, dq1_ref, dk0_ref, dk1_ref, dv0_ref, dv1_ref):
        n = pl.program_id(0)
        rows = slice(None) if dil == 1 else _strided_rows(pl.program_id(1), ATT_BLK, dil)

        def read(ref0, ref1):
            return jnp.concatenate([ref0[rows, :], ref1[rows, :]], axis=1)

        def write(ref0, ref1, val):
            ref0[rows, :] = val[:, :LANES]
            ref1[rows, :] = val[:, LANES:]

        masks = _head_masks(ATT_BLK)
        valid = _band_mask(jnp.where(n > 0, 0, ATT_BLK))
        qi = lax.broadcasted_iota(jnp.int32, (HEAD_ROWS, ATT_BLK), 0) & (ATT_BLK - 1)
        ki = lax.broadcasted_iota(jnp.int32, (HEAD_ROWS, ATT_BLK), 1)
        valid_next = (ki - qi) >= jnp.where(n < n_blk - 1, 0, ATT_BLK)

        kc, vc = kc_ref[...], vc_ref[...]
        keys = jnp.concatenate([kp_ref[...], kc], axis=0)
        vals = jnp.concatenate([vp_ref[...], vc], axis=0)
        q4 = _stack_heads(qc_ref[...], masks)
        d4 = _stack_heads(read(dc0_ref, dc1_ref).astype(MXU_DTYPE), masks)
        st = sc_ref[rows, :]
        p = jnp.where(valid, jnp.exp(_dot(q4, keys, 1, 1) * ATT_SCALE - _head_column(st, 0)), 0.0)
        ds = p * (_dot(d4, vals, 1, 1) - _head_column(st, ATT_HEADS_PER_GROUP)) * ATT_SCALE
        write(dq0_ref, dq1_ref, _unstack_heads(_dot(ds, keys, 1, 0), masks))

        q4n = _stack_heads(qn_ref[...], masks)
        d4n = _stack_heads(read(dn0_ref, dn1_ref).astype(MXU_DTYPE), masks)
        stn = sn_ref[rows, :]
        p_n = jnp.where(valid_next, jnp.exp(_dot(q4n, kc, 1, 1) * ATT_SCALE - _head_column(stn, 0)), 0.0)
        ds_n = p_n * (_dot(d4n, vc, 1, 1) - _head_column(stn, ATT_HEADS_PER_GROUP)) * ATT_SCALE
        write(dv0_ref, dv1_ref, _dot(p[:, ATT_BLK:], d4, 0, 0) + _dot(p_n, d4n, 0, 0))
        write(dk0_ref, dk1_ref, _dot(ds[:, ATT_BLK:], q4, 0, 0) + _dot(ds_n, q4n, 0, 0))

    shape = jax.ShapeDtypeStruct((n_rows, LANES), F32)
    out = seq(0, 0)
    res = pl.pallas_call(
        body, name=name, grid=(n_blk, dil),
        in_specs=[cur, nxt, prev, cur, prev, cur, seq(0, 0), seq(1, 0), seq(0, 1), seq(1, 1), seq(0, 0), seq(0, 1)],
        out_specs=[out] * 6, out_shape=[shape] * 6, compiler_params=_params("parallel", "arbitrary"),
    )(q, q, k, k, v, v, datt, datt, datt, datt, stats, stats)
    return [(res[2 * i], res[2 * i + 1]) for i in range(3)]


def _dproj_assemble(du, dqkv, dgs, dga, cos_t, sin_t, *, name):
    n_g = len(DILATIONS)

    def fn(*t):
        n_half = LANE_HALVES * 3 * n_g
        du_t, halves, (dgs_t, dga_t, c, s) = t[0], t[1:1 + n_half], t[1 + n_half:]
        parts = [jnp.concatenate(halves[LANE_HALVES * i:LANE_HALVES * (i + 1)], axis=1) for i in range(3 * n_g)]
        for i in range(2 * n_g):
            parts[i] = _rope_transpose(parts[i], c, s)
        cast = [p.astype(MXU_DTYPE) for p in parts]
        return [jnp.concatenate([du_t] + cast + [dgs_t, dga_t], axis=1)] + [_colsum(p) for p in parts]

    rows = [(du, SSM_WIDTH, 0)]
    rows += [(half, LANES, 0) for i in range(3) for g in range(n_g) for half in dqkv[g][i]]
    rows += [(dgs, D_MODEL, 0), (dga, D_MODEL, 0), (cos_t, ATT_MERGED, 0), (sin_t, ATT_MERGED, 0)]
    width = SSM_WIDTH + 3 * n_g * ATT_MERGED + 2 * D_MODEL
    res = _rowcall(fn, rows, [], [(width, MXU_DTYPE)], [ATT_MERGED] * (3 * n_g), n_rows=du.shape[0], tm=ROW_TILE, name=name)
    return res[0], res[1:]


def _xhead(h):
    return slice(h * XATT_HEAD_DIM, (h + 1) * XATT_HEAD_DIM)


def _xatt_probs(qh, kh):
    s = _dot(qh, kh, 1, 1) * XATT_SCALE
    e = jnp.exp(s - jnp.max(s, axis=-1, keepdims=True))
    return e / jnp.sum(e, axis=-1, keepdims=True)


def _xatt_fwd(q, kv, *, name, tm=512):
    n_rows = q.shape[0]
    n_mem = kv.shape[0]

    def body(q_ref, kv_ref, o_ref):
        for h in range(XATT_HEADS):
            sl = _xhead(h)
            p = _xatt_probs(q_ref[:, sl], kv_ref[:, sl])
            o_ref[:, sl] = _dot(p, kv_ref[:, D_MODEL + h * XATT_HEAD_DIM:D_MODEL + (h + 1) * XATT_HEAD_DIM], 1, 0
                                ).astype(o_ref.dtype)

    row = pl.BlockSpec((tm, D_MODEL), lambda i: (i, 0))
    return pl.pallas_call(
        body, name=name, grid=(n_rows // tm,),
        in_specs=[row, pl.BlockSpec((n_mem, 2 * D_MODEL), lambda i: (0, 0))], out_specs=row,
        out_shape=jax.ShapeDtypeStruct((n_rows, D_MODEL), MXU_DTYPE), compiler_params=_params("parallel"),
    )(q, kv)


def _xatt_bwd(q, kv, do, *, name, tm=512):
    n_rows = q.shape[0]
    n_mem = kv.shape[0]

    def body(q_ref, kv_ref, do_ref, dq_ref, dkv_ref):
        @pl.when(pl.program_id(0) == 0)
        def _():
            dkv_ref[...] = jnp.zeros_like(dkv_ref)

        for h in range(XATT_HEADS):
            sl = _xhead(h)
            vsl = slice(D_MODEL + h * XATT_HEAD_DIM, D_MODEL + (h + 1) * XATT_HEAD_DIM)
            qh, kh, doh = q_ref[:, sl], kv_ref[:, sl], do_ref[:, sl]
            p = _xatt_probs(qh, kh)
            dp = _dot(doh, kv_ref[:, vsl], 1, 1)
            ds = p * (dp - jnp.sum(dp * p, axis=-1, keepdims=True)) * XATT_SCALE
            dq_ref[:, sl] = _dot(ds, kh, 1, 0).astype(dq_ref.dtype)
            dkv_ref[:, sl] += _dot(ds, qh, 0, 0)
            dkv_ref[:, vsl] += _dot(p, doh, 0, 0)

    row = pl.BlockSpec((tm, D_MODEL), lambda i: (i, 0))
    full = pl.BlockSpec((n_mem, 2 * D_MODEL), lambda i: (0, 0))
    return pl.pallas_call(
        body, name=name, grid=(n_rows // tm,), in_specs=[row, full, row], out_specs=[row, full],
        out_shape=[jax.ShapeDtypeStruct((n_rows, D_MODEL), MXU_DTYPE), jax.ShapeDtypeStruct((n_mem, 2 * D_MODEL), F32)],
        compiler_params=_params("arbitrary"),
    )(q, kv, do)


def _disc(logdt, a_re, a_im, b_re, b_im):
    dt = jnp.exp(logdt)
    mag = jnp.exp(a_re * dt)
    ab_re = mag * jnp.cos(a_im * dt)
    ab_im = mag * jnp.sin(a_im * dt)
    den = jnp.square(a_re) + jnp.square(a_im)
    nr = ab_re - 1.0
    f_re = (nr * a_re + ab_im * a_im) / den
    f_im = (ab_im * a_re - nr * a_im) / den
    bb_re = f_re[None] * b_re - f_im[None] * b_im
    bb_im = f_re[None] * b_im + f_im[None] * b_re
    return ab_re, ab_im, bb_re, bb_im


def _disc_transpose(logdt, a_re, a_im, b_re, b_im, g_ab_re, g_ab_im, g_bb_re, g_bb_im):
    dt = jnp.exp(logdt)
    mag = jnp.exp(a_re * dt)
    th = a_im * dt
    cs, sn = jnp.cos(th), jnp.sin(th)
    ab_re, ab_im = mag * cs, mag * sn
    den = jnp.square(a_re) + jnp.square(a_im)
    nr = ab_re - 1.0
    f_re = (nr * a_re + ab_im * a_im) / den
    f_im = (ab_im * a_re - nr * a_im) / den
    d_f_re = jnp.sum(g_bb_re * b_re + g_bb_im * b_im, axis=0)
    d_f_im = jnp.sum(g_bb_im * b_re - g_bb_re * b_im, axis=0)
    d_b_re = g_bb_re * f_re[None] + g_bb_im * f_im[None]
    d_b_im = g_bb_im * f_re[None] - g_bb_re * f_im[None]
    d_n_re, d_n_im = d_f_re / den, d_f_im / den
    d_den = -(d_f_re * f_re + d_f_im * f_im) / den
    d_ab_re = g_ab_re + d_n_re * a_re - d_n_im * a_im
    d_ab_im = g_ab_im + d_n_re * a_im + d_n_im * a_re
    d_a_re = d_n_re * nr + d_n_im * ab_im + 2.0 * d_den * a_re
    d_a_im = d_n_re * ab_im - d_n_im * nr + 2.0 * d_den * a_im
    d_mag = d_ab_re * cs + d_ab_im * sn
    d_th = mag * (d_ab_im * cs - d_ab_re * sn)
    d_a_re = d_a_re + d_mag * mag * dt
    d_a_im = d_a_im + d_th * dt
    d_dt = jnp.sum(d_mag * mag * a_re + d_th * a_im, axis=-1, keepdims=True)
    return d_dt * dt, d_a_re, d_a_im, d_b_re, d_b_im


def _full_spec(shape):
    return pl.BlockSpec(tuple(shape), functools.partial(lambda i, nd: (0,) * nd, nd=len(shape)))


def _whole(fn, args, out_shapes, *, name):
    n_in = len(args)

    def body(*refs):
        res = fn(*[r[...] for r in refs[:n_in]])
        for o_ref, val in zip(refs[n_in:], res):
            o_ref[...] = val

    return pl.pallas_call(
        body, name=name, grid=(1,), in_specs=[_full_spec(t.shape) for t in args],
        out_specs=[_full_spec(s) for s in out_shapes], out_shape=[jax.ShapeDtypeStruct(s, F32) for s in out_shapes],
        compiler_params=_params("arbitrary"))(*args)


SSM_WIDE = GROUPS_PER_TILE * SSM_STATE
LANE_GROUPS_PER_TILE = SSM_WIDE // LANES


def _chan(j):
    return slice(j * LANES, (j + 1) * LANES)


def _time_major_rows(j, q, tc):
    return pl.ds(j * LANE_GROUPS_PER_TILE + q, tc, stride=STATE_VREG_ROWS)


def _to_time_major(x, t_re_ref, t_im_ref, dst_re, dst_im, tc):
    for j in range(SSM_TILES):
        xj = x[:, _chan(j)]
        for t_ref, dst in ((t_re_ref, dst_re), (t_im_ref, dst_im)):
            r = _dot(xj, t_ref[j], 1, 0)
            for q in range(LANE_GROUPS_PER_TILE):
                dst[_time_major_rows(j, q, tc), :] = r[:, q * LANES:(q + 1) * LANES]


def _from_time_major(src, j, tc):
    return jnp.concatenate([src[_time_major_rows(j, q, tc), :] for q in range(LANE_GROUPS_PER_TILE)], axis=1)


def _scan_chunk(w_re, w_im, h_re, h_im, a_re, a_im, start, tc):
    def step(t, carry):
        hr, hi = carry
        rows = _scan_rows(t)
        nr = a_re * hr - a_im * hi + w_re[rows, :]
        ni = a_re * hi + a_im * hr + w_im[rows, :]
        h_re[rows, :] = nr
        h_im[rows, :] = ni
        return nr, ni

    return lax.fori_loop(0, tc, step, start, unroll=8)


SSM_CHUNK = 256


def _tile_spec(stack, k):
    return pl.BlockSpec((pl.Squeezed(),) + tuple(stack.shape[1:]), lambda i: (k, 0, 0, 0))


def _expand_block_diagonal(src_ref, dst):
    dst[...] = jnp.zeros_like(dst)
    r, c = src_ref.shape[1:]
    for g in range(SSM_GROUPS):
        j, gl = divmod(g, GROUPS_PER_TILE)
        dst[j, gl * r:(gl + 1) * r, gl * c:(gl + 1) * c] = src_ref[g].astype(dst.dtype)


def _extract_block_diagonal(src, dst_ref):
    r, c = dst_ref.shape[1:]
    for g in range(SSM_GROUPS):
        j, gl = divmod(g, GROUPS_PER_TILE)
        dst_ref[g] = src[j, gl * r:(gl + 1) * r, gl * c:(gl + 1) * c]


def _ssm_fwd(proj, blocks_cn, blocks_nc, a_re, a_im, gain, *, name, tc=SSM_CHUNK):
    n_rows = proj.shape[0]
    n_chunk = n_rows // tc

    def body(u_ref, br_ref, bi_ref, cr_ref, ci_ref, ar_ref, ai_ref, g_ref, y_ref, gy_ref, hr, hi, wr, wi, state,
             tbr_ref, tbi_ref, tcr_ref, tci_ref):
        @pl.when(pl.program_id(0) == 0)
        def _():
            state[...] = jnp.zeros_like(state)
            for src_ref, dst in ((br_ref, tbr_ref), (bi_ref, tbi_ref), (cr_ref, tcr_ref), (ci_ref, tci_ref)):
                _expand_block_diagonal(src_ref, dst)

        u = u_ref[...]
        _to_time_major(u, tbr_ref, tbi_ref, wr, wi, tc)
        state[0], state[1] = _scan_chunk(wr, wi, hr, hi, ar_ref[...], ai_ref[...], (state[0], state[1]), tc)
        for j in range(SSM_TILES):
            yj = (_dot(_from_time_major(hr, j, tc), tcr_ref[j], 1, 0) + _dot(_from_time_major(hi, j, tc), tci_ref[j], 1, 0)
                  + g_ref[:, _chan(j)] * u[:, _chan(j)])
            y_ref[:, _chan(j)] = yj
            gy_ref[:, _chan(j)] = jax.nn.gelu(yj).astype(gy_ref.dtype)

    rows = pl.BlockSpec((tc, SSM_WIDTH), lambda i: (i, 0))
    coef = pl.BlockSpec((STATE_VREG_ROWS, LANES), lambda i: (0, 0))
    states = pl.BlockSpec((tc * STATE_VREG_ROWS, LANES), lambda i: (i, 0))
    sshape = jax.ShapeDtypeStruct((n_rows * STATE_VREG_ROWS, LANES), F32)
    return pl.pallas_call(
        body, name=name, grid=(n_chunk,),
        in_specs=[rows, _tile_spec(blocks_cn, 0), _tile_spec(blocks_cn, 1), _tile_spec(blocks_nc, 0),
                  _tile_spec(blocks_nc, 1), coef, coef, pl.BlockSpec((1, SSM_WIDTH), lambda i: (0, 0))],
        out_specs=[rows, rows, states, states],
        out_shape=[jax.ShapeDtypeStruct((n_rows, SSM_WIDTH), F32), jax.ShapeDtypeStruct((n_rows, SSM_WIDTH), MXU_DTYPE),
                   sshape, sshape],
        scratch_shapes=[pltpu.VMEM((tc * STATE_VREG_ROWS, LANES), F32)] * 2 + [pltpu.VMEM((2, STATE_VREG_ROWS, LANES), F32)]
        + [pltpu.VMEM((SSM_TILES, LANES, SSM_WIDE), MXU_DTYPE)] * 2 + [pltpu.VMEM((SSM_TILES, SSM_WIDE, LANES), MXU_DTYPE)] * 2,
        compiler_params=_params("arbitrary"),
    )(proj, blocks_cn, blocks_cn, blocks_nc, blocks_nc, a_re, a_im, gain)


def _ssm_bwd(proj, dy, h_re, h_im, blocks_cn, blocks_nc, a_re, a_im, gain, *, name, tc=SSM_CHUNK):
    n_rows = proj.shape[0]
    n_chunk = n_rows // tc

    def body(u_ref, dy_ref, hr, hi, cr_ref, ci_ref, br_ref, bi_ref, ar_ref, ai_ref, g_ref,
             du_ref, su_ref, dc_re_ref, dc_im_ref, db_re_ref, db_im_ref, dar_ref, dai_ref, wr, wi, carry,
             tdr_ref, tdi_ref, tur_ref, tui_ref, dcr_ref, dci_ref, dbr_ref, dbi_ref):
        @pl.when(pl.program_id(0) == 0)
        def _():
            carry[...] = jnp.zeros_like(carry)
            for acc_ref in (su_ref, dcr_ref, dci_ref, dbr_ref, dbi_ref):
                acc_ref[...] = jnp.zeros_like(acc_ref)
            for src_ref, dst in ((cr_ref, tdr_ref), (ci_ref, tdi_ref), (br_ref, tur_ref), (bi_ref, tui_ref)):
                _expand_block_diagonal(src_ref, dst)

        a_r, a_i = ar_ref[...], ai_ref[...]
        u, dyv = u_ref[...], dy_ref[...]
        _to_time_major(dyv, tdr_ref, tdi_ref, wr, wi, tc)

        def step(kk, c):
            lam_r, lam_i, dar, dai = c
            rows = _scan_rows(tc - 1 - kk)
            h_r, h_i = hr[rows, :], hi[rows, :]
            dar = dar + lam_r * h_r + lam_i * h_i
            dai = dai + lam_i * h_r - lam_r * h_i
            new_r = wr[rows, :] + a_r * lam_r + a_i * lam_i
            new_i = wi[rows, :] + a_r * lam_i - a_i * lam_r
            wr[rows, :] = new_r
            wi[rows, :] = new_i
            return new_r, new_i, dar, dai

        carry[0], carry[1], carry[2], carry[3] = lax.fori_loop(0, tc, step, (carry[0], carry[1], carry[2], carry[3]),
                                                              unroll=8)
        dar_ref[...] = carry[2]
        dai_ref[...] = carry[3]
        for j in range(SSM_TILES):
            cj = _chan(j)
            lam_r, lam_i = _from_time_major(wr, j, tc), _from_time_major(wi, j, tc)
            dcr_ref[j] += _dot(dyv[:, cj], _from_time_major(hr, j, tc), 0, 0)
            dci_ref[j] += _dot(dyv[:, cj], _from_time_major(hi, j, tc), 0, 0)
            dbr_ref[j] += _dot(u[:, cj], lam_r, 0, 0)
            dbi_ref[j] += _dot(u[:, cj], lam_i, 0, 0)
            duj = _dot(lam_r, tur_ref[j], 1, 0) + _dot(lam_i, tui_ref[j], 1, 0) + g_ref[:, cj] * dyv[:, cj]
            du_ref[:, cj] = duj.astype(du_ref.dtype)
            su_ref[:, cj] += _colsum(duj)

        @pl.when(pl.program_id(0) == n_chunk - 1)
        def _():
            for src, dst_ref in ((dcr_ref, dc_re_ref), (dci_ref, dc_im_ref), (dbr_ref, db_re_ref), (dbi_ref, db_im_ref)):
                _extract_block_diagonal(src, dst_ref)

    back = lambda i: (n_chunk - 1 - i, 0)
    rows = pl.BlockSpec((tc, SSM_WIDTH), back)
    blocks = pl.BlockSpec((SSM_GROUPS, SSM_GROUP, SSM_STATE), lambda i: (0, 0, 0))
    coef = pl.BlockSpec((STATE_VREG_ROWS, LANES), lambda i: (0, 0))
    states = pl.BlockSpec((tc * STATE_VREG_ROWS, LANES), back)
    vec = pl.BlockSpec((1, SSM_WIDTH), lambda i: (0, 0))
    bshape = jax.ShapeDtypeStruct((SSM_GROUPS, SSM_GROUP, SSM_STATE), F32)
    cshape = jax.ShapeDtypeStruct((STATE_VREG_ROWS, LANES), F32)
    return pl.pallas_call(
        body, name=name, grid=(n_chunk,),
        in_specs=[rows, rows, states, states, _tile_spec(blocks_cn, 2), _tile_spec(blocks_cn, 3), _tile_spec(blocks_nc, 2),
                  _tile_spec(blocks_nc, 3), coef, coef, vec],
        out_specs=[rows, vec, blocks, blocks, blocks, blocks, coef, coef],
        out_shape=[jax.ShapeDtypeStruct((n_rows, SSM_WIDTH), MXU_DTYPE), jax.ShapeDtypeStruct((1, SSM_WIDTH), F32),
                   bshape, bshape, bshape, bshape, cshape, cshape],
        scratch_shapes=[pltpu.VMEM((tc * STATE_VREG_ROWS, LANES), F32)] * 2 + [pltpu.VMEM((4, STATE_VREG_ROWS, LANES), F32)]
        + [pltpu.VMEM((SSM_TILES, LANES, SSM_WIDE), MXU_DTYPE)] * 2 + [pltpu.VMEM((SSM_TILES, SSM_WIDE, LANES), MXU_DTYPE)] * 2
        + [pltpu.VMEM((SSM_TILES, LANES, SSM_WIDE), F32)] * 4,
        compiler_params=_params("arbitrary"),
    )(proj, dy, h_re, h_im, blocks_cn, blocks_cn, blocks_nc, blocks_nc, a_re, a_im, gain)


def _scan_rows(t):
    return pl.ds(pl.multiple_of(t * STATE_VREG_ROWS, 8), STATE_VREG_ROWS)


GATHER_GROUPS = (("w_glu", "w_att_up", "w_mix_out"), ("w_xq", "w_xkv", "w_xo", "w_ff1", "w_ff2"))
SCATTER_GROUPS = (("w_ff2", "w_ff1"), ("w_xo", "w_xq", "w_xkv", "w_mix_out"), ("w_att_up", "w_glu"), ("w_in",))


def _local_grads(x, mem, pos_col, target, sm, fetch_in, fetch, send, send_small, start_token):
    b_re_t = sm["ssm_b_re"].transpose(2, 0, 1)
    b_im_t = sm["ssm_b_im"].transpose(2, 0, 1)
    logdt = sm["ssm_log_dt"].reshape(SSM_GROUPS, 1)
    c_re, c_im = sm["ssm_c_re"], sm["ssm_c_im"]
    grp = (SSM_GROUPS, SSM_STATE)
    chn = (SSM_GROUP, SSM_GROUPS, SSM_STATE)

    wts = {}
    cos_t, sin_t = _rope_tables(pos_col, after=start_token, name="rope_tables")
    h0, xh0, rs0, h0m = _ln_fwd(x, sm["ln_in_g"], sm["ln_in_b"], name="ln_in_fwd")
    disc_in = (logdt, sm["ssm_a_re"], sm["ssm_a_im"], b_re_t, b_im_t)
    ab_re, ab_im, bb_re_t, bb_im_t = _whole(_disc, disc_in, [grp, grp, chn, chn], name="ssm_disc")
    a_re_rows, a_im_rows = ab_re.reshape(STATE_VREG_ROWS, LANES), ab_im.reshape(STATE_VREG_ROWS, LANES)
    tiles_cn = jnp.stack([bb_re_t.transpose(1, 0, 2), bb_im_t.transpose(1, 0, 2), c_re, -c_im])
    tiles_nc = jnp.stack([c_re.transpose(0, 2, 1), -c_im.transpose(0, 2, 1), bb_re_t.transpose(1, 2, 0),
                          bb_im_t.transpose(1, 2, 0)])
    w_in_near, near_ids = fetch_in(0, [h0m, tiles_cn, tiles_nc])
    proj = _mm_shards(h0m, w_in_near, sm["b_in"], near_ids, name="in_proj_near")
    wts["w_in"], far_ids = fetch_in(1, [proj])
    proj = _mm_shards(h0m, wts["w_in"], sm["b_in"], far_ids, prev=proj, name="in_proj_far")

    y, gy, h_re, h_im = _ssm_fwd(proj, tiles_cn, tiles_nc, a_re_rows, a_im_rows, sm["ssm_d"], name="ssm_fwd")

    q, k, v = _qkv_split(proj, cos_t, sin_t, name="qkv_split")
    outs, lses = [], []
    for g, dil in enumerate(DILATIONS):
        o_g, l_g = _dil_fwd(q[g], k[g], v[g], dil, name=f"dil_att_fwd_{dil}")
        outs.append(o_g)
        lses.append(l_g)
    att, lse = _att_merge(outs, lses, name="att_merge")
    wts.update(fetch(0, [att]))
    z = _mm(gy, wts["w_glu"], bias=sm["b_glu"], b_shards=True, name="glu_proj")
    b_att = _mm(att, wts["w_att_up"], b_shards=True, name="att_up")

    mixed, h1, xh1, rs1, h1m = _mix_out_ln(proj, z, b_att, wts["w_mix_out"], sm["b_mix_out"], h0, sm["ln1_g"],
                                           sm["ln1_b"], alpha=DEEPNORM_ALPHA, name="gate_mix_out_ln1")

    wts.update(fetch(1, [h1m]))
    xq = _mm(h1m, wts["w_xq"], out_dtype=MXU_DTYPE, name="xatt_q")
    kv = _mm(mem, wts["w_xkv"], out_dtype=MXU_DTYPE, b_shards=True, name="xatt_kv")
    xo_in = _xatt_fwd(xq, kv, name="xatt_fwd")
    h2, xh2, rs2, h2m = _mm_ln_fwd(xo_in, wts["w_xo"], None, h1, sm["ln2_g"], sm["ln2_b"], alpha=DEEPNORM_ALPHA,
                                   name="xatt_o_ln2")

    pre, act = _mm(h2m, wts["w_ff1"], bias=sm["b_ff1"], b_shards=True, name="ff1",
                   also=(lambda r: jnp.square(jnp.maximum(r, 0.0)), MXU_DTYPE))
    ff = _mm(act, wts["w_ff2"], bias=sm["b_ff2"], name="ff2")

    gw, gs = {}, {}
    dr3, dr3m, gs["ln3_g"], gs["ln3_b"], gs["b_ff2"], loss_row = _ln_loss_bwd(
        h2, ff, target, sm["ln3_g"], sm["ln3_b"], alpha=DEEPNORM_ALPHA, name="ln3_loss")
    wgrad = functools.partial(_mm, ta=True, out_dtype=WIRE_DTYPE, tk=2048)
    wgrad_cols = functools.partial(wgrad, out_shards=True, tk=x.shape[0])
    gw["w_ff2"] = wgrad(act, dr3m, tk=1024, name="ff2_dw")
    dpre, gs["b_ff1"] = _mm(dr3m, wts["w_ff2"], tb=True, out_dtype=MXU_DTYPE, colsum=True, name="ff2_dx",
                            gate=(pre, lambda p: 2.0 * jnp.maximum(p, 0.0)))
    gw["w_ff1"] = wgrad_cols(h2m, dpre, name="ff1_dw")
    sent = send(0, gw)
    dh2 = _mm(dpre, wts["w_ff1"], tb=True, b_shards=True, after=sent, name="ff1_dx")
    dr2, dr2m, gs["ln2_g"], gs["ln2_b"], _ = _ln_bwd(dr3, dh2, xh2, rs2, sm["ln2_g"], alpha=DEEPNORM_ALPHA,
                                                     name="ln2_bwd")
    gw["w_xo"] = wgrad(xo_in, dr2m, name="xatt_o_dw")
    dxo_in = _mm(dr2m, wts["w_xo"], tb=True, out_dtype=MXU_DTYPE, name="xatt_o_dx")
    dxq, dkv = _xatt_bwd(xq, kv, dxo_in, name="xatt_bwd")
    gw["w_xq"] = wgrad(h1m, dxq, name="xatt_q_dw")
    gw["w_xkv"] = wgrad_cols(mem, dkv, name="xatt_kv_dw")
    dr1, dr1m, gs["ln1_g"], gs["ln1_b"], gs["b_mix_out"] = _mm_ln_bwd(
        dxq, wts["w_xq"], dr2, xh1, rs1, sm["ln1_g"], alpha=DEEPNORM_ALPHA, name="xatt_q_dx_ln1")
    gw["w_mix_out"] = wgrad(mixed, dr1m, name="mix_out_dw")
    sent = send(1, gw)
    dmixed = _mm(dr1m, wts["w_mix_out"], tb=True, after=sent, name="mix_out_dx")
    dgs, dga, dz, db_att, s_gs, s_ga, gs["b_glu"] = _mix_bwd(dmixed, proj, z, b_att, name="gate_mix_bwd")

    gw["w_att_up"] = wgrad_cols(att, db_att, name="att_up_dw")
    gw["w_glu"] = wgrad_cols(gy, dz, name="glu_dw")
    sent = send(2, gw)
    datt = _mm(db_att, wts["w_att_up"], tb=True, b_shards=True, after=sent, name="att_up_dx")
    stats = _att_stats(datt, att, lse, name="att_stats")
    dqkv = [_dil_bwd(q[g], k[g], v[g], datt, stats, dil, name=f"dil_att_bwd_{dil}") for g, dil in enumerate(DILATIONS)]

    dgy = _mm(dz, wts["w_glu"], tb=True, b_shards=True, name="glu_dx")
    dy, gs["ssm_d"] = _gelu_bwd(dgy, y, proj, name="gelu_bwd")
    du, s_u, dc_re_t, dc_im_t, dbb_re_t, dbb_im_t, da_re, da_im = _ssm_bwd(
        proj, dy, h_re, h_im, tiles_cn, tiles_nc, a_re_rows, a_im_rows, sm["ssm_d"], name="ssm_bwd")
    gs["ssm_c_re"], gs["ssm_c_im"] = dc_re_t, -dc_im_t
    disc_ct = (da_re.reshape(grp), da_im.reshape(grp), dbb_re_t.transpose(1, 0, 2), dbb_im_t.transpose(1, 0, 2))
    d_logdt, gs["ssm_a_re"], gs["ssm_a_im"], d_b_re_t, d_b_im_t = _whole(
        _disc_transpose, disc_in + disc_ct, [(SSM_GROUPS, 1), grp, grp, chn, chn], name="ssm_disc_bwd")
    gs["ssm_log_dt"] = d_logdt
    gs["ssm_b_re"], gs["ssm_b_im"] = d_b_re_t.transpose(1, 2, 0), d_b_im_t.transpose(1, 2, 0)

    dproj, s_qkv = _dproj_assemble(du, dqkv, dgs, dga, cos_t, sin_t, name="dproj_assemble")
    gs["b_in"] = jnp.concatenate([s_u, *s_qkv, s_gs, s_ga], axis=1)
    sent = send_small(gs, SMALL_EARLY)
    gw["w_in"] = wgrad_cols(h0m, dproj, after=sent, name="in_proj_dw")
    sent = send(3, gw)
    dh0 = _mm(dproj, wts["w_in"], tb=True, b_shards=True, after=sent, name="in_proj_dx")
    grad_x, gs["ln_in_g"], gs["ln_in_b"], _ = _ln_bwd(dr1, dh0, xh0, rs0, sm["ln_in_g"], alpha=DEEPNORM_ALPHA,
                                                      operand=False, name="ln_in_bwd")
    return loss_row, grad_x, gs


_IN_HBM = pl.BlockSpec(memory_space=pltpu.HBM)
_IN_SEMAPHORE = pl.BlockSpec(memory_space=pltpu.SEMAPHORE)


def _device_index():
    return 4 * lax.axis_index("x") + 2 * lax.axis_index("y") + lax.axis_index("c")


ALL_PEERS = tuple(range(1, N_DEV))
NEAR_PEERS = (1, 2, 3, 4, 5)
FAR_PEERS = (6, 7)


def _peer_index(kk):
    x, y, c = lax.axis_index("x"), lax.axis_index("y"), lax.axis_index("c")
    return 4 * ((x + (kk >> 2)) % 2) + 2 * ((y + ((kk >> 1) & 1)) % 2) + (c + (kk & 1)) % 2


def _exchange_copies(src_refs, land_refs, send_sems, recv_sems, scatter, peers):
    x, y, c = lax.axis_index("x"), lax.axis_index("y"), lax.axis_index("c")
    me = 4 * x + 2 * y + c
    pairs = []
    for a, (src_ref, land_ref) in enumerate(zip(src_refs, land_refs)):
        for idx, kk in enumerate(peers):
            px = (x + (kk >> 2)) % 2
            py = (y + ((kk >> 1) & 1)) % 2
            pc = (c + (kk & 1)) % 2
            peer = 4 * px + 2 * py + pc
            sem = a * len(peers) + idx
            src = src_ref.at[peer] if scatter else src_ref

            def copy(dst, src=src, sem=sem, px=px, py=py, pc=pc):
                return pltpu.make_async_remote_copy(
                    src_ref=src, dst_ref=dst, send_sem=send_sems.at[sem], recv_sem=recv_sems.at[sem],
                    device_id=(px, py, pc), device_id_type=pl.DeviceIdType.MESH)

            pairs.append((functools.partial(copy, land_ref.at[me]), functools.partial(copy, land_ref.at[peer])))
    return pairs


def _own_copies(src_refs, land_refs, own_sems, scatter):
    me = _device_index()
    return [functools.partial(pltpu.make_async_copy, src_ref.at[me] if scatter else src_ref, land_ref.at[me],
                              own_sems.at[a]) for a, (src_ref, land_ref) in enumerate(zip(src_refs, land_refs))]


def _exchange_start(srcs, *, scatter, name, after=None, peers=ALL_PEERS, lands=None):
    n_arr, n_sem = len(srcs), len(srcs) * len(peers)
    own = lands is None
    if own:
        lands = [lax.empty((N_DEV,) + tuple(s.shape[1:] if scatter else s.shape), s.dtype) for s in srcs]
    n_in = 2 * n_arr + (after is not None)

    def body(*refs):
        send_sems, recv_sems = refs[n_in], refs[n_in + 1]
        for sent, _ in _exchange_copies(refs[:n_arr], refs[n_arr:2 * n_arr], send_sems, recv_sems, scatter, peers):
            sent().start()
        if own:
            for local in _own_copies(refs[:n_arr], refs[n_arr:2 * n_arr], refs[n_in + 2], scatter):
                local().start()
        refs[-1][...] = jnp.zeros_like(refs[-1])

    sems = [pltpu.SemaphoreType.DMA((n_sem,)), pltpu.SemaphoreType.DMA((n_sem,))] + [pltpu.SemaphoreType.DMA((n_arr,))] * own
    through = [pltpu.HBM(t.shape, t.dtype) for t in (*srcs, *lands)]
    res = pl.pallas_call(
        body, name=name, out_shape=(*sems, *through, jax.ShapeDtypeStruct((8, LANES), F32)),
        in_specs=[_IN_HBM] * (2 * n_arr) + [pl.BlockSpec(memory_space=pl.ANY)] * (after is not None),
        out_specs=(*[_IN_SEMAPHORE] * len(sems), *[_IN_HBM] * (2 * n_arr), pl.BlockSpec(memory_space=pltpu.VMEM)),
        input_output_aliases={i: len(sems) + i for i in range(2 * n_arr)},
        compiler_params=pltpu.CompilerParams(has_side_effects=pltpu.SideEffectType.DATAFLOW_SIDE_EFFECTING),
    )(*[pltpu.with_memory_space_constraint(t, pltpu.HBM) for t in (*srcs, *lands)],
      *([after] if after is not None else []))
    first = len(sems)
    handle = dict(sems=res[:first], srcs=res[first:first + n_arr], lands=res[first + n_arr:first + 2 * n_arr],
                  scatter=scatter, peers=peers, own=own)
    return handle, res[-1]


def _exchange_wait(handle, *, after, name, srcs=None, lands=None):
    srcs = handle["srcs"] if srcs is None else srcs
    lands = handle["lands"] if lands is None else lands
    sems, scatter, peers, own = handle["sems"], handle["scatter"], handle["peers"], handle["own"]
    n_arr = len(srcs)
    after = list(after)

    def body(*refs):
        src_refs, land_refs = refs[:n_arr], refs[n_arr:2 * n_arr]
        for sent, received in _exchange_copies(src_refs, land_refs, refs[2 * n_arr], refs[2 * n_arr + 1], scatter, peers):
            sent().wait_send()
            received().wait_recv()
        if own:
            for local in _own_copies(src_refs, land_refs, refs[2 * n_arr + 2], scatter):
                local().wait()

    res = pl.pallas_call(
        body, name=name, out_shape=tuple(pltpu.HBM(t.shape, t.dtype) for t in (*srcs, *lands)),
        in_specs=[_IN_HBM] * (2 * n_arr) + [_IN_SEMAPHORE] * len(sems) + [pl.BlockSpec(memory_space=pl.ANY)] * len(after),
        out_specs=tuple([_IN_HBM] * (2 * n_arr)), input_output_aliases={i: i for i in range(2 * n_arr)},
        compiler_params=pltpu.CompilerParams(has_side_effects=pltpu.SideEffectType.DATAFLOW_SIDE_EFFECTING),
    )(*srcs, *lands, *sems, *after)
    return res[:n_arr], res[n_arr:]


def _adamw(g, w, m, v):
    m_new = ADAM_B1 * m + (1.0 - ADAM_B1) * g
    v_new = ADAM_B2 * v + (1.0 - ADAM_B2) * jnp.square(g)
    m_hat = m_new / (1.0 - ADAM_B1 ** ADAM_STEP)
    v_hat = v_new / (1.0 - ADAM_B2 ** ADAM_STEP)
    return g, -ADAM_LR * (m_hat / (jnp.sqrt(v_hat) + ADAM_EPS) + ADAM_WD * w), m_new, v_new


def _reduce_adamw(gstack, w, m, v, *, name, tr=128):
    n_rows, cols = w.shape
    tr = min(tr, n_rows)
    assert n_rows % tr == 0, (name, n_rows, tr)

    def body(g_ref, w_ref, m_ref, v_ref, *out_refs):
        g = g_ref[0].astype(F32)
        for dev in range(1, N_DEV):
            g = g + g_ref[dev].astype(F32)
        for o_ref, val in zip(out_refs, _adamw(g, w_ref[...], m_ref[...], v_ref[...])):
            o_ref[...] = val

    flat = pl.BlockSpec((tr, cols), lambda i: (i, 0))
    shape = jax.ShapeDtypeStruct((n_rows, cols), F32)
    return pl.pallas_call(
        body, name=name, grid=(n_rows // tr,),
        in_specs=[pl.BlockSpec((N_DEV, tr, cols), lambda i: (0, i, 0)), flat, flat, flat],
        out_specs=[flat] * 4, out_shape=[shape] * 4, compiler_params=_params("parallel"),
    )(gstack, w, m, v)


SMALL_FLAT_SSM = ("ssm_b_re", "ssm_b_im", "ssm_c_re", "ssm_c_im")


def _small_view(name, shape):
    size = int(np.prod(shape))
    if name in SMALL_FLAT_SSM:
        return SSM_GROUPS, size // SSM_GROUPS
    if name in ("ssm_a_re", "ssm_a_im"):
        return SSM_GROUPS, SSM_STATE
    return 1, size


def _pack_rows(view):
    return -(-(view[0] * view[1]) // PACK_COLS)


SMALL_LATE = ("ln_in_g", "ln_in_b")
SMALL_EARLY = tuple(n for n in SMALL if n not in SMALL_LATE)


def _pack_small(gs, names, views):
    parts = []
    for n in names:
        flat = gs[n].reshape(-1).astype(WIRE_DTYPE)
        parts.append(jnp.pad(flat, (0, _pack_rows(views[n]) * PACK_COLS - flat.shape[0])))
    total = sum(p.shape[0] for p in parts) // PACK_COLS
    parts.append(jnp.zeros(((-total % PACK_ROW_ALIGN) * PACK_COLS,), WIRE_DTYPE))
    return jnp.concatenate(parts).reshape(-1, PACK_COLS)


def _small_pieces(view):
    rows, cols = view
    if cols == PACK_COLS:
        return [(0, rows, 0, 0, 0, cols)]
    if rows == 1 and cols > PACK_COLS:
        return [(kk, 1, 0, 0, kk * PACK_COLS, PACK_COLS) for kk in range(cols // PACK_COLS)]
    if rows == 1:
        return [(0, 1, 0, 0, 0, cols)]
    return [((r * cols) // PACK_COLS, 1, (r * cols) % PACK_COLS, r, 0, cols) for r in range(rows)]


def _adamw_small(stacks, views, w, m, v, *, name):
    n = len(SMALL)
    place, first = {}, [0, 0]
    for k, names in enumerate((SMALL_EARLY, SMALL_LATE)):
        for name_ in names:
            place[name_] = (k, first[k])
            first[k] += _pack_rows(views[name_])

    def body(early_ref, late_ref, *refs):
        ins, outs = refs[:3 * n], refs[3 * n:]
        for i, name_ in enumerate(SMALL):
            stack_ref = (early_ref, late_ref)[place[name_][0]]
            row0 = place[name_][1]
            for prow, nrows, lane, orow, ocol, width in _small_pieces(views[name_]):
                src = (slice(row0 + prow, row0 + prow + nrows), slice(lane, lane + width))
                dst = (slice(orow, orow + nrows), slice(ocol, ocol + width))
                g = stack_ref[(0,) + src].astype(F32)
                for dev in range(1, N_DEV):
                    g = g + stack_ref[(dev,) + src].astype(F32)
                res = _adamw(g, ins[i][dst], ins[n + i][dst], ins[2 * n + i][dst])
                for kk, val in enumerate(res):
                    outs[kk * n + i][dst] = val

    args = [*stacks, *[d[name_] for d in (w, m, v) for name_ in SMALL]]
    out_views = [views[name_] for _ in range(4) for name_ in SMALL]
    res = pl.pallas_call(
        body, name=name, grid=(1,), in_specs=[_full_spec(t.shape) for t in args],
        out_specs=[_full_spec(s) for s in out_views], out_shape=[jax.ShapeDtypeStruct(s, F32) for s in out_views],
        compiler_params=_params("arbitrary"),
    )(*args)
    return [dict(zip(SMALL, res[kk * n:(kk + 1) * n])) for kk in range(4)]


def kernel(x, mem, positions, ln_in_g, ln_in_b, w_in, b_in, ssm_log_dt, ssm_a_re, ssm_a_im, ssm_b_re, ssm_b_im, ssm_c_re, ssm_c_im, ssm_d, w_glu, b_glu, w_att_up, w_mix_out, b_mix_out, ln1_g, ln1_b, w_xq, w_xkv, w_xo, ln2_g, ln2_b, w_ff1, b_ff1, w_ff2, b_ff2, ln3_g, ln3_b, loss_target, m_ln_in_g, m_ln_in_b, m_w_in, m_b_in, m_ssm_log_dt, m_ssm_a_re, m_ssm_a_im, m_ssm_b_re, m_ssm_b_im, m_ssm_c_re, m_ssm_c_im, m_ssm_d, m_w_glu, m_b_glu, m_w_att_up, m_w_mix_out, m_b_mix_out, m_ln1_g, m_ln1_b, m_w_xq, m_w_xkv, m_w_xo, m_ln2_g, m_ln2_b, m_w_ff1, m_b_ff1, m_w_ff2, m_b_ff2, m_ln3_g, m_ln3_b, v_ln_in_g, v_ln_in_b, v_w_in, v_b_in, v_ssm_log_dt, v_ssm_a_re, v_ssm_a_im, v_ssm_b_re, v_ssm_b_im, v_ssm_c_re, v_ssm_c_im, v_ssm_d, v_w_glu, v_b_glu, v_w_att_up, v_w_mix_out, v_b_mix_out, v_ln1_g, v_ln1_b, v_w_xq, v_w_xkv, v_w_xo, v_ln2_g, v_ln2_b, v_w_ff1, v_b_ff1, v_w_ff2, v_b_ff2, v_ln3_g, v_ln3_b):
    given = dict(locals())
    w_arg = {n: given[n] for n in WEIGHTS}
    m_arg = {n: given["m_" + n] for n in WEIGHTS}
    v_arg = {n: given["v_" + n] for n in WEIGHTS}

    in_near, token = _exchange_start([w_arg["w_in"][0].astype(MXU_DTYPE)], scatter=False, peers=NEAR_PEERS,
                                     name="gather_start_in_near")
    in_far, token = _exchange_start(in_near["srcs"], scatter=False, peers=FAR_PEERS, lands=in_near["lands"],
                                    after=token, name="gather_start_in_far")
    w_in_state = [in_far["srcs"], in_far["lands"]]
    token, w_arg, m_arg, v_arg = lax.optimization_barrier((token, w_arg, m_arg, v_arg))
    shards = {n: w_arg[n][0].astype(MXU_DTYPE) for n in BIG if n != "w_in"}
    gathers = []
    for i, names in enumerate(GATHER_GROUPS):
        handle, token = _exchange_start([shards[n] for n in names], scatter=False, after=token, name=f"gather_start_{i}")
        gathers.append(handle)

    small_views = {n: _small_view(n, w_arg[n].shape) for n in SMALL}
    small_w, small_m, small_v = [{n: d[n].reshape(small_views[n]) for n in SMALL} for d in (w_arg, m_arg, v_arg)]
    relaid = [d[n] for d in (small_w, small_m, small_v) for n in SMALL_FLAT_SSM]

    def fetch_in(part, after):
        handle, peers, tag = ((in_near, (0,) + NEAR_PEERS, "near"), (in_far, FAR_PEERS, "far"))[part]
        w_in_state[:] = _exchange_wait(handle, after=after + (relaid if part == 0 else []), srcs=w_in_state[0],
                                       lands=w_in_state[1], name="gather_wait_in_" + tag)
        return w_in_state[1][0], jnp.stack([_peer_index(kk) for kk in peers]).astype(jnp.int32)

    def fetch(i, after):
        _, lands = _exchange_wait(gathers[i], after=after, name=f"gather_wait_{i}")
        full = dict(zip(GATHER_GROUPS[i], lands))
        return {n: t if n in BIG_COL_SHARDED else t.reshape(-1, t.shape[-1]) for n, t in full.items()}

    scatters = {}

    def send(i, gw):
        slots = [gw[n] if n in BIG_COL_SHARDED else gw[n].reshape(N_DEV, -1, gw[n].shape[-1]) for n in SCATTER_GROUPS[i]]
        handle, sent = _exchange_start(slots, scatter=True, name=f"scatter_start_{i}")
        scatters[i] = (handle, slots)
        return sent

    sm = {}
    for n in SMALL:
        t = w_arg[n]
        if n.startswith("ssm_") and n not in ("ssm_d", "ssm_log_dt"):
            sm[n] = t[0]
        else:
            sm[n] = t.reshape(1, -1)

    smalls = []

    def send_small(gs, names):
        handle, sent = _exchange_start([_pack_small(gs, names, small_views)], scatter=False,
                                       name=f"small_start_{len(smalls)}")
        smalls.append(handle)
        return sent

    loss_row, grad_x, gs = _local_grads(x[0], mem[0], positions.reshape(-1, 1), loss_target[0], sm, fetch_in, fetch,
                                        send, send_small, token)
    loss = lax.psum(loss_row[0, 0], ("x", "y", "c"))
    send_small(gs, SMALL_LATE)

    results = [{}, {}, {}, {}]
    done = grad_x
    for i, names in enumerate(SCATTER_GROUPS):
        handle, slots = scatters[i]
        _, lands = _exchange_wait(handle, after=[done], name=f"scatter_wait_{i}")
        for n, land, slot in zip(names, lands, slots):
            res = _reduce_adamw(land, w_arg[n][0], m_arg[n][0], v_arg[n][0], name="adamw_" + n)
            done = res[0]
            for d, r in zip(results, res):
                d[n] = r[None]
    stacks = [_exchange_wait(handle, after=[done], name=f"small_wait_{i}")[1][0] for i, handle in enumerate(smalls)]
    res = _adamw_small(stacks, small_views, small_w, small_m, small_v, name="adamw_small")
    for d, r in zip(results, res):
        d.update({n: r[n].reshape(w_arg[n].shape) for n in SMALL})
    out = [loss, grad_x[None]]
    for d in results:
        out += [d[n] for n in WEIGHTS]
    return tuple(out)
```

```python
import functools

import numpy as np
import jax
import jax.numpy as jnp
from jax import lax
from jax.experimental import pallas as pl
from jax.experimental.pallas import tpu as pltpu

F32 = jnp.float32
MXU_DTYPE = jnp.bfloat16
WIRE_DTYPE = jnp.bfloat16
VMEM_LIMIT_BYTES = 48 * 1024 * 1024
LANES = 128

N_DEV = 8
D_MODEL = 1024
SSM_GROUP = 16
SSM_WIDTH = 768
SSM_GROUPS = SSM_WIDTH // SSM_GROUP
SSM_STATE = 64
SSM_CH = SSM_GROUPS * SSM_STATE
SSM_TILES = SSM_WIDTH // LANES
GROUPS_PER_TILE = LANES // SSM_GROUP
STATE_VREG_ROWS = SSM_CH // LANES
ATT_HEAD_DIM = 64
ATT_HEADS_PER_GROUP = 4
ATT_MERGED = ATT_HEADS_PER_GROUP * ATT_HEAD_DIM
LANE_HALVES = ATT_MERGED // LANES
DILATIONS = (1, 4, 16)
ATT_BLK = 128
ATT_SCALE = ATT_HEAD_DIM ** -0.5
ROT_DIM = ATT_HEAD_DIM // 4
ROPE_THETA = 500000.0
XATT_HEADS = 4
XATT_HEAD_DIM = D_MODEL // XATT_HEADS
XATT_SCALE = XATT_HEAD_DIM ** -0.5
DEEPNORM_ALPHA = 2.0 ** 0.25
LN_EPS = 1e-5
NEG_INF = -1e30
OFF_Q_BLK, OFF_K_BLK, OFF_V_BLK = 3, 6, 9
OFF_GS_BLK, OFF_GA_BLK = 3, 4

ADAM_LR = 0.001
ADAM_B1 = 0.9
ADAM_B2 = 0.999
ADAM_EPS = 1e-08
ADAM_WD = 0.01
ADAM_STEP = 10

BIG = ("w_in", "w_glu", "w_att_up", "w_mix_out", "w_xq", "w_xkv", "w_xo", "w_ff1", "w_ff2")
BIG_COL_SHARDED = ("w_in", "w_glu", "w_att_up", "w_xkv", "w_ff1")
WEIGHTS = ("ln_in_g", "ln_in_b", "w_in", "b_in", "ssm_log_dt", "ssm_a_re", "ssm_a_im", "ssm_b_re", "ssm_b_im",
           "ssm_c_re", "ssm_c_im", "ssm_d", "w_glu", "b_glu", "w_att_up", "w_mix_out", "b_mix_out", "ln1_g", "ln1_b",
           "w_xq", "w_xkv", "w_xo", "ln2_g", "ln2_b", "w_ff1", "b_ff1", "w_ff2", "b_ff2", "ln3_g", "ln3_b")
SMALL = tuple(n for n in WEIGHTS if n not in BIG)
PACK_COLS = 1024
PACK_ROW_ALIGN = 16


def _params(*sem):
    return pltpu.CompilerParams(dimension_semantics=sem, vmem_limit_bytes=VMEM_LIMIT_BYTES)


def _dot(a, b, ca, cb):
    return lax.dot_general(a.astype(MXU_DTYPE), b.astype(MXU_DTYPE), (((ca,), (cb,)), ((), ())),
                           preferred_element_type=F32)


def _fit(dim, pref):
    if dim <= pref:
        return dim
    best = max(t for t in range(LANES, pref + 1, LANES) if dim % t == 0)
    return best


def _mm(a, b, *, name, ta=False, tb=False, bias=None, out_dtype=F32, b_shards=False, out_shards=False, after=None,
        also=None, gate=None, colsum=False, epilogue=None, tm=2048, tn=1024, tk=1024):
    m, k = (a.shape[1], a.shape[0]) if ta else a.shape
    order = (lambda f: (lambda j, i, kk: f(i, j, kk))) if colsum else (lambda f: f)
    spec = lambda shape, f: pl.BlockSpec(shape, order(f))
    if b_shards:
        n_sh, rows, n_loc = b.shape
        if tb:
            n, tn, tk = rows, _fit(rows, tn), n_loc
            assert k == n_sh * n_loc, (name, k, b.shape)
            b_spec = spec((1, tn, tk), lambda i, j, kk: (kk, j, 0))
        else:
            n, tn, tk = n_sh * n_loc, n_loc, _fit(k, tk)
            b_spec = spec((1, tk, tn), lambda i, j, kk: (j, kk, 0))
    else:
        n = b.shape[0] if tb else b.shape[1]
        tn = n // N_DEV * int(out_shards) if out_shards else _fit(n, tn)
        tk = _fit(k, tk)
        b_spec = spec((tn, tk), lambda i, j, kk: (j, kk)) if tb else spec((tk, tn), lambda i, j, kk: (kk, j))
    tm = _fit(m, tm)
    nk = k // tk
    a_spec = spec((tk, tm), lambda i, j, kk: (kk, i)) if ta else spec((tm, tk), lambda i, j, kk: (i, kk))
    tile = spec((tm, tn), lambda i, j, kk: (i, j))
    in_specs, args = [a_spec, b_spec], [a, b]
    if bias is not None:
        in_specs.append(spec((1, tn), lambda i, j, kk: (0, j)))
        args.append(bias)
    if gate is not None:
        in_specs.append(tile)
        args.append(gate[0])
    if after is not None:
        in_specs.append(pl.BlockSpec(memory_space=pl.ANY))
        args.append(after)
    if epilogue is not None:
        ep_fn, ep_rows, ep_fulls, ep_row_outs, ep_acc_outs = epilogue
        assert tn == n and not (colsum or also or gate or out_shards), name
        ep_first = len(args)
        in_specs += [spec((tm, t.shape[1]), lambda i, j, kk: (i, 0)) for t in ep_rows]
        in_specs += [pl.BlockSpec(t.shape, functools.partial(lambda i, j, kk, nd: (0,) * nd, nd=t.ndim)) for t in ep_fulls]
        args += [*ep_rows, *ep_fulls]
    n_in = len(args)
    if epilogue is not None:
        out_specs = [spec((tm, w), lambda i, j, kk: (i, 0)) for w, _ in ep_row_outs]
        out_specs += [spec((1, w), lambda i, j, kk: (0, 0)) for w in ep_acc_outs]
        out_shape = [jax.ShapeDtypeStruct((m, w), dt) for w, dt in ep_row_outs]
        out_shape += [jax.ShapeDtypeStruct((1, w), F32) for w in ep_acc_outs]
    elif out_shards:
        assert n % N_DEV == 0 and N_DEV % int(out_shards) == 0, (name, n, out_shards)
        out_specs = [spec((int(out_shards), tm, n // N_DEV), lambda i, j, kk: (j, i, 0))]
        out_shape = [jax.ShapeDtypeStruct((N_DEV, m, n // N_DEV), out_dtype)]
    else:
        out_specs = [tile]
        out_shape = [jax.ShapeDtypeStruct((m, n), out_dtype)]
    if also is not None:
        out_specs.append(tile)
        out_shape.append(jax.ShapeDtypeStruct((m, n), also[1]))
    if colsum:
        out_specs.append(spec((1, tn), lambda i, j, kk: (0, j)))
        out_shape.append(jax.ShapeDtypeStruct((1, n), F32))

    def body(*refs):
        a_ref, b_ref = refs[0], refs[1]
        o_ref = refs[n_in]
        first_row_tile = pl.program_id(1 if colsum else 0) == 0

        def product():
            return _dot(a_ref[...], b_ref[0] if b_shards else b_ref[...], 0 if ta else 1, 1 if tb else 0)

        def finish(r):
            if bias is not None:
                r = r + refs[2][...]
            if gate is not None:
                r = r * gate[1](refs[2 + (bias is not None)][...])
            if epilogue is not None:
                res = ep_fn(r, *[ref[...] for ref in refs[ep_first:n_in]])
                n_o = len(ep_row_outs)
                for ref, val in zip(refs[n_in:n_in + n_o], res[:n_o]):
                    ref[...] = val.astype(ref.dtype)
                acc_refs = refs[n_in + n_o:n_in + n_o + len(ep_acc_outs)]
                if acc_refs:
                    @pl.when(first_row_tile)
                    def _():
                        for ref in acc_refs:
                            ref[...] = jnp.zeros_like(ref)

                    for ref, val in zip(acc_refs, res[n_o:]):
                        ref[...] += val
                return
            if out_shards:
                for s in range(int(out_shards)):
                    o_ref[s] = r[:, s * (n // N_DEV):(s + 1) * (n // N_DEV)].astype(o_ref.dtype)
            else:
                o_ref[...] = r.astype(o_ref.dtype)
            if also is not None:
                refs[n_in + 1][...] = also[0](r).astype(also[1])
            if colsum:
                s_ref = refs[n_in + 1 + (also is not None)]

                @pl.when(first_row_tile)
                def _():
                    s_ref[...] = jnp.zeros_like(s_ref)

                s_ref[...] += _colsum(r)

        if nk == 1:
            finish(product())
            return
        acc_ref = refs[-1]
        kk = pl.program_id(2)

        @pl.when(kk == 0)
        def _():
            acc_ref[...] = product()

        if nk > 2:
            @pl.when((kk > 0) & (kk < nk - 1))
            def _():
                acc_ref[...] += product()

        @pl.when(kk == nk - 1)
        def _():
            finish(acc_ref[...] + product())

    grid = (n // tn, m // tm, nk) if colsum else (m // tm, n // tn, nk)
    res = pl.pallas_call(
        body, name=name, grid=grid, in_specs=in_specs, out_specs=out_specs, out_shape=out_shape,
        scratch_shapes=[pltpu.VMEM((tm, tn), F32)] if nk > 1 else [],
        compiler_params=_params("arbitrary" if epilogue is not None else "parallel",
                                "arbitrary" if colsum else "parallel", "arbitrary"),
    )(*args)
    return res[0] if len(res) == 1 else res


def _mm_shards(a, w, bias, shard_ids, *, name, prev=None, tm=2048):
    m, k = a.shape
    n_sh, _, n_loc = w.shape
    tm = _fit(m, tm)

    def body(ids_ref, a_ref, w_ref, b_ref, *rest):
        rest[-1][...] = _dot(a_ref[...], w_ref[0], 1, 0) + b_ref[...]

    grid_spec = pltpu.PrefetchScalarGridSpec(
        num_scalar_prefetch=1, grid=(m // tm, shard_ids.shape[0]),
        in_specs=[pl.BlockSpec((tm, k), lambda i, j, ids: (i, 0)),
                  pl.BlockSpec((1, k, n_loc), lambda i, j, ids: (ids[j], 0, 0)),
                  pl.BlockSpec((1, n_loc), lambda i, j, ids: (0, ids[j]))]
        + [pl.BlockSpec(memory_space=pl.ANY)] * (prev is not None),
        out_specs=pl.BlockSpec((tm, n_loc), lambda i, j, ids: (i, ids[j])))
    return pl.pallas_call(
        body, name=name, grid_spec=grid_spec, out_shape=jax.ShapeDtypeStruct((m, n_sh * n_loc), F32),
        input_output_aliases={4: 0} if prev is not None else {}, compiler_params=_params("parallel", "arbitrary"),
    )(shard_ids, a, w, bias, *([prev] if prev is not None else []))


ROW_TILE = 512


def _rowcall(fn, rows, fulls, row_outs, acc_outs=(), *, n_rows, tm, name, after=None):
    n_r, n_f, n_o, n_a = len(rows), len(fulls), len(row_outs), len(acc_outs)
    n_in = n_r + n_f + (after is not None)
    assert n_rows % tm == 0, (name, n_rows, tm)

    def body(*refs):
        res = fn(*[r[...] for r in refs[:n_r + n_f]])
        res = tuple(res) if isinstance(res, (tuple, list)) else (res,)
        o_refs = refs[n_in:n_in + n_o]
        a_refs = refs[n_in + n_o:]
        for o_ref, val in zip(o_refs, res[:n_o]):
            o_ref[...] = val.astype(o_ref.dtype)
        if n_a:
            @pl.when(pl.program_id(0) == 0)
            def _():
                for a_ref in a_refs:
                    a_ref[...] = jnp.zeros_like(a_ref)

            for a_ref, val in zip(a_refs, res[n_o:]):
                a_ref[...] += val

    in_specs = [pl.BlockSpec((tm, w), functools.partial(lambda i, cb: (i, cb), cb=cb)) for _, w, cb in rows]
    in_specs += [pl.BlockSpec(f.shape, functools.partial(lambda i, nd: (0,) * nd, nd=f.ndim)) for f in fulls]
    in_specs += [pl.BlockSpec(memory_space=pl.ANY)] * (after is not None)
    out_specs = [pl.BlockSpec((tm, w), lambda i: (i, 0)) for w, _ in row_outs]
    out_specs += [pl.BlockSpec((1, w), lambda i: (0, 0)) for w in acc_outs]
    out_shape = [jax.ShapeDtypeStruct((n_rows, w), dt) for w, dt in row_outs]
    out_shape += [jax.ShapeDtypeStruct((1, w), F32) for w in acc_outs]
    return pl.pallas_call(
        body, name=name, grid=(n_rows // tm,), in_specs=in_specs, out_specs=out_specs, out_shape=out_shape,
        compiler_params=_params("arbitrary" if n_a else "parallel"),
    )(*[r[0] for r in rows], *fulls, *([after] if after is not None else []))


def _colsum(v):
    return jnp.sum(v, axis=0, keepdims=True)


def _layer_norm(xin, g, b):
    mu = jnp.mean(xin, axis=-1, keepdims=True)
    xc = xin - mu
    var = jnp.mean(xc * xc, axis=-1, keepdims=True)
    rstd = lax.rsqrt(var + LN_EPS)
    xh = xc * rstd
    return xh * g + b, xh, rstd


def _layer_norm_bwd(dy, xh, rstd, g):
    dyg = dy * g
    m1 = jnp.mean(dyg, axis=-1, keepdims=True)
    m2 = jnp.mean(dyg * xh, axis=-1, keepdims=True)
    dx = rstd * (dyg - m1 - xh * m2)
    return dx, _colsum(dy * xh), _colsum(dy), _colsum(dx)


def _ln_fwd(a, g, b, *, name):
    n_rows, d = a.shape

    def fn(av, gv, bv):
        y, xh, rstd = _layer_norm(av, gv, bv)
        return y, xh, rstd, y

    return _rowcall(fn, [(a, d, 0)], [g, b], [(d, F32), (d, F32), (1, F32), (d, MXU_DTYPE)], n_rows=n_rows, tm=ROW_TILE,
                    name=name)


def _ln_bwd(dya, dyb, xh, rstd, g, *, alpha, name, operand=True):
    n_rows, d = xh.shape

    def fn(da, db, xhv, rs, gv):
        dx, *sums = _layer_norm_bwd(alpha * da + db, xhv, rs, gv)
        return (dx,) + ((dx,) if operand else ()) + tuple(sums)

    rows = [(dya, d, 0), (dyb, d, 0), (xh, d, 0), (rstd, 1, 0)]
    return _rowcall(fn, rows, [g], [(d, F32)] + [(d, MXU_DTYPE)] * operand, [d, d, d], n_rows=n_rows, tm=ROW_TILE, name=name)


LN_EPILOGUE_ROWS = 1024


def _mm_ln_fwd(x, w, bias, a, g, b, *, alpha, name):
    d = a.shape[1]

    def fn(r, av, gv, bv):
        y, xh, rstd = _layer_norm(alpha * av + r, gv, bv)
        return y, xh, rstd, y

    return _mm(x, w, bias=bias, name=name, tm=LN_EPILOGUE_ROWS,
               epilogue=(fn, [a], [g, b], [(d, F32), (d, F32), (1, F32), (d, MXU_DTYPE)], []))


def _mm_ln_bwd(x, w, dya, xh, rstd, g, *, alpha, name):
    d = xh.shape[1]

    def fn(r, da, xhv, rs, gv):
        dx, *sums = _layer_norm_bwd(alpha * da + r, xhv, rs, gv)
        return (dx, dx, *sums)

    return _mm(x, w, tb=True, name=name, tm=LN_EPILOGUE_ROWS,
               epilogue=(fn, [dya, xh, rstd], [g], [(d, F32), (d, MXU_DTYPE)], [d, d, d]))


def _ln_loss_bwd(a, r, target, g, b, *, alpha, name):
    n_rows, d = a.shape

    def fn(av, rv, tv, gv, bv):
        y, xh, rs = _layer_norm(alpha * av + rv, gv, bv)
        diff = y - tv
        part = jnp.sum(jnp.sum(diff * diff, axis=1, keepdims=True), axis=0, keepdims=True) * (0.5 / d)
        dx, *sums = _layer_norm_bwd(diff * (1.0 / d), xh, rs, gv)
        return (dx, dx, *sums, jnp.broadcast_to(part, (1, LANES)))

    return _rowcall(fn, [(a, d, 0), (r, d, 0), (target, d, 0)], [g, b], [(d, F32), (d, MXU_DTYPE)], [d, d, d, LANES],
                    n_rows=n_rows, tm=ROW_TILE, name=name)


def _rope_lane_constants():
    lane = np.arange(ATT_MERGED)
    in_head = lane % ATT_HEAD_DIM
    sign = np.where(in_head < ROT_DIM // 2, -1.0, np.where(in_head < ROT_DIM, 1.0, 0.0)).astype(np.float32)
    inv_freq = ROPE_THETA ** (-jnp.arange(0, ROT_DIM, 2, dtype=F32) / ROT_DIM)
    return inv_freq[lane % (ROT_DIM // 2)].reshape(1, ATT_MERGED), jnp.asarray(sign).reshape(1, ATT_MERGED)


def _rope_tables(pos_col, *, name, after=None):
    inv_lane, sign = _rope_lane_constants()

    def fn(pos, inv, sg):
        ang = pos.astype(F32) * inv
        return jnp.where(sg != 0.0, jnp.cos(ang), 1.0), sg * jnp.sin(ang)

    return _rowcall(fn, [(pos_col, 1, 0)], [inv_lane, sign], [(ATT_MERGED, F32), (ATT_MERGED, F32)],
                    n_rows=pos_col.shape[0], tm=512, name=name, after=after)


def _rot_partner(t):
    lane = lax.broadcasted_iota(jnp.int32, t.shape, 1)
    width = t.shape[1]
    return jnp.where((lane & (ROT_DIM // 2)) == 0, pltpu.roll(t, width - ROT_DIM // 2, 1), pltpu.roll(t, ROT_DIM // 2, 1))


def _rope(t, cos_t, sin_t):
    return t * cos_t + _rot_partner(t) * sin_t


def _rope_transpose(dt, cos_t, sin_t):
    return dt * cos_t + _rot_partner(dt * sin_t)


def _strided_rows(r, count, stride):
    return pl.ds(r, count) if stride == 1 else pl.ds(r, count, stride=stride)


def _qkv_split(proj, cos_t, sin_t, *, name, tm=512):
    n_rows = proj.shape[0]
    n_g = len(DILATIONS)

    def body(*refs):
        n_src = LANE_HALVES * 3 * n_g
        src, tables, dst = refs[:n_src], refs[n_src:n_src + 2 * LANE_HALVES], refs[n_src + 2 * LANE_HALVES:]
        for kind in range(3):
            for g, dil in enumerate(DILATIONS):
                for half in range(LANE_HALVES):
                    x_ref, o_ref = src[(kind * n_g + g) * LANE_HALVES + half], dst[kind * n_g + g]
                    cos_ref, sin_ref = tables[half], tables[LANE_HALVES + half]
                    for r in range(dil):
                        rows = _strided_rows(r, tm // dil, dil)
                        t = x_ref[rows, :]
                        if kind < 2:
                            t = _rope(t, cos_ref[rows, :], sin_ref[rows, :])
                        lo = r * ATT_MERGED + half * LANES
                        o_ref[:, lo:lo + LANES] = t.astype(o_ref.dtype)

    half_spec = lambda cb: pl.BlockSpec((tm, LANES), functools.partial(lambda i, cb: (i, cb), cb=cb))
    in_specs = [half_spec((off + g) * LANE_HALVES + half)
                for off in (OFF_Q_BLK, OFF_K_BLK, OFF_V_BLK) for g in range(n_g) for half in range(LANE_HALVES)]
    in_specs += [half_spec(half) for _ in range(2) for half in range(LANE_HALVES)]
    out_specs = [pl.BlockSpec((tm // dil, dil * ATT_MERGED), lambda i: (i, 0)) for _ in range(3) for dil in DILATIONS]
    out_shape = [jax.ShapeDtypeStruct((n_rows // dil, dil * ATT_MERGED), MXU_DTYPE) for _ in range(3) for dil in DILATIONS]
    outs = pl.pallas_call(
        body, name=name, grid=(n_rows // tm,), in_specs=in_specs, out_specs=out_specs, out_shape=out_shape,
        compiler_params=_params("parallel"),
    )(*[proj] * (LANE_HALVES * 3 * n_g), *[cos_t] * LANE_HALVES, *[sin_t] * LANE_HALVES)
    return outs[:n_g], outs[n_g:2 * n_g], outs[2 * n_g:]


def _mix(gs, ga, z1, z2, b_att):
    return jax.nn.sigmoid(gs) * (z1 * jax.nn.sigmoid(z2)) + jax.nn.sigmoid(ga) * b_att


def _mix_rows(proj, z, b_att):
    return [(proj, D_MODEL, OFF_GS_BLK), (proj, D_MODEL, OFF_GA_BLK), (z, D_MODEL, 0), (z, D_MODEL, 1), (b_att, D_MODEL, 0)]


def _mix_out_ln(proj, z, b_att, w, bias, a, g, b, *, alpha, name):
    def fn(gs, ga, z1, z2, ba, av, wv, biasv, gv, bv):
        mixed = _mix(gs, ga, z1, z2, ba)
        y, xh, rstd = _layer_norm(alpha * av + (_dot(mixed, wv, 1, 0) + biasv), gv, bv)
        return mixed, y, xh, rstd, y

    rows = _mix_rows(proj, z, b_att) + [(a, D_MODEL, 0)]
    outs = [(D_MODEL, MXU_DTYPE), (D_MODEL, F32), (D_MODEL, F32), (1, F32), (D_MODEL, MXU_DTYPE)]
    return _rowcall(fn, rows, [w, bias, g, b], outs, n_rows=proj.shape[0], tm=ROW_TILE // 2, name=name)


def _mix_bwd(dmixed, proj, z, b_att, *, name):
    def fn(dm, gs, ga, z1, z2, ba):
        _, vjp = jax.vjp(_mix, gs, ga, z1, z2, ba)
        dgs, dga, dz1, dz2, dba = vjp(dm)
        dz = jnp.concatenate([dz1, dz2], axis=1)
        return dgs, dga, dz, dba, _colsum(dgs), _colsum(dga), _colsum(dz)

    rows = [(dmixed, D_MODEL, 0)] + _mix_rows(proj, z, b_att)
    widths = [D_MODEL, D_MODEL, 2 * D_MODEL, D_MODEL]
    return _rowcall(fn, rows, [], [(w, MXU_DTYPE) for w in widths], widths[:3], n_rows=proj.shape[0], tm=ROW_TILE, name=name)


def _gelu_bwd(dgy, y, proj, *, name):
    def fn(dg, yv, u):
        _, vjp = jax.vjp(jax.nn.gelu, yv)
        dy = vjp(dg)[0]
        return dy, _colsum(dy * u)

    return _rowcall(fn, [(dgy, SSM_WIDTH, 0), (y, SSM_WIDTH, 0), (proj, SSM_WIDTH, 0)], [], [(SSM_WIDTH, F32)],
                    [SSM_WIDTH], n_rows=y.shape[0], tm=512, name=name)


HEAD_ROWS = ATT_HEADS_PER_GROUP * ATT_BLK


def _head_masks(rows):
    head = lax.broadcasted_iota(jnp.int32, (rows, ATT_MERGED), 1) >> (ATT_HEAD_DIM.bit_length() - 1)
    return [head == h for h in range(ATT_HEADS_PER_GROUP)]


def _stack_heads(t, masks):
    return jnp.concatenate([jnp.where(m, t, jnp.zeros_like(t)) for m in masks], axis=0)


def _unstack_heads(t4, masks):
    blocks = [t4[h * ATT_BLK:(h + 1) * ATT_BLK] for h in range(ATT_HEADS_PER_GROUP)]
    return jnp.where(masks[0], blocks[0], jnp.where(masks[1], blocks[1], jnp.where(masks[2], blocks[2], blocks[3])))


def _head_column(stats, first):
    return jnp.concatenate([stats[:, first + h:first + h + 1] for h in range(ATT_HEADS_PER_GROUP)], axis=0)


def _band_mask(first_key):
    qi = lax.broadcasted_iota(jnp.int32, (HEAD_ROWS, 2 * ATT_BLK), 0) & (ATT_BLK - 1)
    ki = lax.broadcasted_iota(jnp.int32, (HEAD_ROWS, 2 * ATT_BLK), 1)
    steps = qi + ATT_BLK - ki
    return (steps >= 0) & (steps <= ATT_BLK) & (ki >= first_key)


def _dil_fwd(q, k, v, dil, *, name):
    n_blk = q.shape[0] // ATT_BLK
    cur = pl.BlockSpec((ATT_BLK, ATT_MERGED), lambda r, n: (n, r))
    prev = pl.BlockSpec((ATT_BLK, ATT_MERGED), lambda r, n: (jnp.maximum(n - 1, 0), r))

    def body(q_ref, kp_ref, kc_ref, vp_ref, vc_ref, o_ref, l_ref):
        masks = _head_masks(ATT_BLK)
        valid = _band_mask(jnp.where(pl.program_id(1) > 0, 0, ATT_BLK))
        keys = jnp.concatenate([kp_ref[...], kc_ref[...]], axis=0)
        vals = jnp.concatenate([vp_ref[...], vc_ref[...]], axis=0)
        s = jnp.where(valid, _dot(_stack_heads(q_ref[...], masks), keys, 1, 1) * ATT_SCALE, NEG_INF)
        m = jnp.max(s, axis=-1, keepdims=True)
        p = jnp.exp(s - m)
        den = jnp.sum(p, axis=-1, keepdims=True)
        o_ref[...] = _unstack_heads(_dot(p, vals, 1, 0) / den, masks)
        l_ref[...] = _unstack_heads(jnp.broadcast_to(m + jnp.log(den), (HEAD_ROWS, ATT_MERGED)), masks)

    shape = jax.ShapeDtypeStruct(q.shape, F32)
    return pl.pallas_call(
        body, name=name, grid=(dil, n_blk), in_specs=[cur, prev, cur, prev, cur], out_specs=[cur, cur],
        out_shape=[shape, shape], compiler_params=_params("parallel", "parallel"),
    )(q, k, k, v, v)


def _att_merge(outs, lses, *, name, tm=512):
    n_g = len(outs)
    n_rows = outs[0].shape[0] * DILATIONS[0]

    def body(*refs):
        src, (att_ref, lse_ref), tmp = refs[:2 * n_g], refs[2 * n_g:2 * n_g + 2], refs[2 * n_g + 2:]
        vals = []
        for idx, src_ref in enumerate(src):
            dil = DILATIONS[idx % n_g]
            if dil == 1:
                vals.append(src_ref[...])
                continue
            for r in range(dil):
                for half in range(LANE_HALVES):
                    lo = r * ATT_MERGED + half * LANES
                    tmp[LANE_HALVES * idx + half][_strided_rows(r, tm // dil, dil), :] = src_ref[:, lo:lo + LANES]
            vals.append(jnp.concatenate([tmp[LANE_HALVES * idx + half][...] for half in range(LANE_HALVES)], axis=1))
        o, l = vals[:n_g], vals[n_g:]
        m = functools.reduce(jnp.maximum, l)
        e = [jnp.exp(li - m) for li in l]
        z = functools.reduce(jnp.add, e)
        att_ref[...] = functools.reduce(jnp.add, [(ei / z) * oi for ei, oi in zip(e, o)])
        lse_ref[...] = m + jnp.log(z)

    in_specs = [pl.BlockSpec((tm // dil, dil * ATT_MERGED), lambda i: (i, 0)) for _ in range(2) for dil in DILATIONS]
    row = pl.BlockSpec((tm, ATT_MERGED), lambda i: (i, 0))
    shape = jax.ShapeDtypeStruct((n_rows, ATT_MERGED), F32)
    return pl.pallas_call(
        body, name=name, grid=(n_rows // tm,), in_specs=in_specs, out_specs=[row, row], out_shape=[shape, shape],
        scratch_shapes=[pltpu.VMEM((tm, LANES), F32)] * (LANE_HALVES * 2 * n_g), compiler_params=_params("parallel"),
    )(*outs, *lses)


def _att_stats(datt, att, lse, *, name):
    n_rows = datt.shape[0]

    def fn(d, a, l):
        prod = d * a
        lane = lax.broadcasted_iota(jnp.int32, (d.shape[0], LANES), 1)
        out = jnp.zeros((d.shape[0], LANES), F32)
        for h in range(ATT_HEADS_PER_GROUP):
            lo = h * ATT_HEAD_DIM
            out = jnp.where(lane == h, l[:, lo:lo + 1], out)
            delta = jnp.sum(prod[:, lo:lo + ATT_HEAD_DIM], axis=-1, keepdims=True)
            out = jnp.where(lane == ATT_HEADS_PER_GROUP + h, delta, out)
        return out

    rows = [(t, ATT_MERGED, 0) for t in (datt, att, lse)]
    return _rowcall(fn, rows, [], [(LANES, F32)], n_rows=n_rows, tm=512, name=name)[0]


def _dil_bwd(q, k, v, datt, stats, dil, *, name):
    n_rows = datt.shape[0]
    n_blk = n_rows // dil // ATT_BLK
    span = ATT_BLK * dil
    cur = pl.BlockSpec((ATT_BLK, ATT_MERGED), lambda n, r: (n, r))
    prev = pl.BlockSpec((ATT_BLK, ATT_MERGED), lambda n, r: (jnp.maximum(n - 1, 0), r))
    nxt = pl.BlockSpec((ATT_BLK, ATT_MERGED), lambda n, r: (jnp.minimum(n + 1, n_blk - 1), r))
    seq = lambda half, ahead: pl.BlockSpec((span, LANES), lambda n, r: (jnp.minimum(n + ahead, n_blk - 1), half))

    def body(qc_ref, qn_ref, kp_ref, kc_ref, vp_ref, vc_ref, dc0_ref, dc1_ref, dn0_ref, dn1_ref, sc_ref, sn_ref,
             dq0_ref, dq1_ref, dk0_ref, dk1_ref, dv0_ref, dv1_ref):
        n = pl.program_id(0)
        rows = slice(None) if dil == 1 else _strided_rows(pl.program_id(1), ATT_BLK, dil)

        def read(ref0, ref1):
            return jnp.concatenate([ref0[rows, :], ref1[rows, :]], axis=1)

        def write(ref0, ref1, val):
            ref0[rows, :] = val[:, :LANES]
            ref1[rows, :] = val[:, LANES:]

        masks = _head_masks(ATT_BLK)
        valid = _band_mask(jnp.where(n > 0, 0, ATT_BLK))
        qi = lax.broadcasted_iota(jnp.int32, (HEAD_ROWS, ATT_BLK), 0) & (ATT_BLK - 1)
        ki = lax.broadcasted_iota(jnp.int32, (HEAD_ROWS, ATT_BLK), 1)
        valid_next = (ki - qi) >= jnp.where(n < n_blk - 1, 0, ATT_BLK)

        kc, vc = kc_ref[...], vc_ref[...]
        keys = jnp.concatenate([kp_ref[...], kc], axis=0)
        vals = jnp.concatenate([vp_ref[...], vc], axis=0)
        q4 = _stack_heads(qc_ref[...], masks)
        d4 = _stack_heads(read(dc0_ref, dc1_ref).astype(MXU_DTYPE), masks)
        st = sc_ref[rows, :]
        p = jnp.where(valid, jnp.exp(_dot(q4, keys, 1, 1) * ATT_SCALE - _head_column(st, 0)), 0.0)
        ds = p * (_dot(d4, vals, 1, 1) - _head_column(st, ATT_HEADS_PER_GROUP)) * ATT_SCALE
        write(dq0_ref, dq1_ref, _unstack_heads(_dot(ds, keys, 1, 0), masks))

        q4n = _stack_heads(qn_ref[...], masks)
        d4n = _stack_heads(read(dn0_ref, dn1_ref).astype(MXU_DTYPE), masks)
        stn = sn_ref[rows, :]
        p_n = jnp.where(valid_next, jnp.exp(_dot(q4n, kc, 1, 1) * ATT_SCALE - _head_column(stn, 0)), 0.0)
        ds_n = p_n * (_dot(d4n, vc, 1, 1) - _head_column(stn, ATT_HEADS_PER_GROUP)) * ATT_SCALE
        write(dv0_ref, dv1_ref, _dot(p[:, ATT_BLK:], d4, 0, 0) + _dot(p_n, d4n, 0, 0))
        write(dk0_ref, dk1_ref, _dot(ds[:, ATT_BLK:], q4, 0, 0) + _dot(ds_n, q4n, 0, 0))

    shape = jax.ShapeDtypeStruct((n_rows, LANES), F32)
    out = seq(0, 0)
    res = pl.pallas_call(
        body, name=name, grid=(n_blk, dil),
        in_specs=[cur, nxt, prev, cur, prev, cur, seq(0, 0), seq(1, 0), seq(0, 1), seq(1, 1), seq(0, 0), seq(0, 1)],
        out_specs=[out] * 6, out_shape=[shape] * 6, compiler_params=_params("parallel", "arbitrary"),
    )(q, q, k, k, v, v, datt, datt, datt, datt, stats, stats)
    return [(res[2 * i], res[2 * i + 1]) for i in range(3)]


def _dproj_assemble(du, dqkv, dgs, dga, cos_t, sin_t, *, name):
    n_g = len(DILATIONS)

    def fn(*t):
        n_half = LANE_HALVES * 3 * n_g
        du_t, halves, (dgs_t, dga_t, c, s) = t[0], t[1:1 + n_half], t[1 + n_half:]
        parts = [jnp.concatenate(halves[LANE_HALVES * i:LANE_HALVES * (i + 1)], axis=1) for i in range(3 * n_g)]
        for i in range(2 * n_g):
            parts[i] = _rope_transpose(parts[i], c, s)
        cast = [p.astype(MXU_DTYPE) for p in parts]
        return [jnp.concatenate([du_t] + cast + [dgs_t, dga_t], axis=1)] + [_colsum(p) for p in parts]

    rows = [(du, SSM_WIDTH, 0)]
    rows += [(half, LANES, 0) for i in range(3) for g in range(n_g) for half in dqkv[g][i]]
    rows += [(dgs, D_MODEL, 0), (dga, D_MODEL, 0), (cos_t, ATT_MERGED, 0), (sin_t, ATT_MERGED, 0)]
    width = SSM_WIDTH + 3 * n_g * ATT_MERGED + 2 * D_MODEL
    res = _rowcall(fn, rows, [], [(width, MXU_DTYPE)], [ATT_MERGED] * (3 * n_g), n_rows=du.shape[0], tm=ROW_TILE, name=name)
    return res[0], res[1:]


def _xhead(h):
    return slice(h * XATT_HEAD_DIM, (h + 1) * XATT_HEAD_DIM)


def _xatt_probs(qh, kh):
    s = _dot(qh, kh, 1, 1) * XATT_SCALE
    e = jnp.exp(s - jnp.max(s, axis=-1, keepdims=True))
    return e / jnp.sum(e, axis=-1, keepdims=True)


def _xatt_fwd(q, kv, *, name, tm=512):
    n_rows = q.shape[0]
    n_mem = kv.shape[0]

    def body(q_ref, kv_ref, o_ref):
        for h in range(XATT_HEADS):
            sl = _xhead(h)
            p = _xatt_probs(q_ref[:, sl], kv_ref[:, sl])
            o_ref[:, sl] = _dot(p, kv_ref[:, D_MODEL + h * XATT_HEAD_DIM:D_MODEL + (h + 1) * XATT_HEAD_DIM], 1, 0
                                ).astype(o_ref.dtype)

    row = pl.BlockSpec((tm, D_MODEL), lambda i: (i, 0))
    return pl.pallas_call(
        body, name=name, grid=(n_rows // tm,),
        in_specs=[row, pl.BlockSpec((n_mem, 2 * D_MODEL), lambda i: (0, 0))], out_specs=row,
        out_shape=jax.ShapeDtypeStruct((n_rows, D_MODEL), MXU_DTYPE), compiler_params=_params("parallel"),
    )(q, kv)


def _xatt_bwd(q, kv, do, *, name, tm=512):
    n_rows = q.shape[0]
    n_mem = kv.shape[0]

    def body(q_ref, kv_ref, do_ref, dq_ref, dkv_ref):
        @pl.when(pl.program_id(0) == 0)
        def _():
            dkv_ref[...] = jnp.zeros_like(dkv_ref)

        for h in range(XATT_HEADS):
            sl = _xhead(h)
            vsl = slice(D_MODEL + h * XATT_HEAD_DIM, D_MODEL + (h + 1) * XATT_HEAD_DIM)
            qh, kh, doh = q_ref[:, sl], kv_ref[:, sl], do_ref[:, sl]
            p = _xatt_probs(qh, kh)
            dp = _dot(doh, kv_ref[:, vsl], 1, 1)
            ds = p * (dp - jnp.sum(dp * p, axis=-1, keepdims=True)) * XATT_SCALE
            dq_ref[:, sl] = _dot(ds, kh, 1, 0).astype(dq_ref.dtype)
            dkv_ref[:, sl] += _dot(ds, qh, 0, 0)
            dkv_ref[:, vsl] += _dot(p, doh, 0, 0)

    row = pl.BlockSpec((tm, D_MODEL), lambda i: (i, 0))
    full = pl.BlockSpec((n_mem, 2 * D_MODEL), lambda i: (0, 0))
    return pl.pallas_call(
        body, name=name, grid=(n_rows // tm,), in_specs=[row, full, row], out_specs=[row, full],
        out_shape=[jax.ShapeDtypeStruct((n_rows, D_MODEL), MXU_DTYPE), jax.ShapeDtypeStruct((n_mem, 2 * D_MODEL), F32)],
        compiler_params=_params("arbitrary"),
    )(q, kv, do)


def _disc(logdt, a_re, a_im, b_re, b_im):
    dt = jnp.exp(logdt)
    mag = jnp.exp(a_re * dt)
    ab_re = mag * jnp.cos(a_im * dt)
    ab_im = mag * jnp.sin(a_im * dt)
    den = jnp.square(a_re) + jnp.square(a_im)
    nr = ab_re - 1.0
    f_re = (nr * a_re + ab_im * a_im) / den
    f_im = (ab_im * a_re - nr * a_im) / den
    bb_re = f_re[None] * b_re - f_im[None] * b_im
    bb_im = f_re[None] * b_im + f_im[None] * b_re
    return ab_re, ab_im, bb_re, bb_im


def _disc_transpose(logdt, a_re, a_im, b_re, b_im, g_ab_re, g_ab_im, g_bb_re, g_bb_im):
    dt = jnp.exp(logdt)
    mag = jnp.exp(a_re * dt)
    th = a_im * dt
    cs, sn = jnp.cos(th), jnp.sin(th)
    ab_re, ab_im = mag * cs, mag * sn
    den = jnp.square(a_re) + jnp.square(a_im)
    nr = ab_re - 1.0
    f_re = (nr * a_re + ab_im * a_im) / den
    f_im = (ab_im * a_re - nr * a_im) / den
    d_f_re = jnp.sum(g_bb_re * b_re + g_bb_im * b_im, axis=0)
    d_f_im = jnp.sum(g_bb_im * b_re - g_bb_re * b_im, axis=0)
    d_b_re = g_bb_re * f_re[None] + g_bb_im * f_im[None]
    d_b_im = g_bb_im * f_re[None] - g_bb_re * f_im[None]
    d_n_re, d_n_im = d_f_re / den, d_f_im / den
    d_den = -(d_f_re * f_re + d_f_im * f_im) / den
    d_ab_re = g_ab_re + d_n_re * a_re - d_n_im * a_im
    d_ab_im = g_ab_im + d_n_re * a_im + d_n_im * a_re
    d_a_re = d_n_re * nr + d_n_im * ab_im + 2.0 * d_den * a_re
    d_a_im = d_n_re * ab_im - d_n_im * nr + 2.0 * d_den * a_im
    d_mag = d_ab_re * cs + d_ab_im * sn
    d_th = mag * (d_ab_im * cs - d_ab_re * sn)
    d_a_re = d_a_re + d_mag * mag * dt
    d_a_im = d_a_im + d_th * dt
    d_dt = jnp.sum(d_mag * mag * a_re + d_th * a_im, axis=-1, keepdims=True)
    return d_dt * dt, d_a_re, d_a_im, d_b_re, d_b_im


def _full_spec(shape):
    return pl.BlockSpec(tuple(shape), functools.partial(lambda i, nd: (0,) * nd, nd=len(shape)))


def _whole(fn, args, out_shapes, *, name):
    n_in = len(args)

    def body(*refs):
        res = fn(*[r[...] for r in refs[:n_in]])
        for o_ref, val in zip(refs[n_in:], res):
            o_ref[...] = val

    return pl.pallas_call(
        body, name=name, grid=(1,), in_specs=[_full_spec(t.shape) for t in args],
        out_specs=[_full_spec(s) for s in out_shapes], out_shape=[jax.ShapeDtypeStruct(s, F32) for s in out_shapes],
        compiler_params=_params("arbitrary"))(*args)


SSM_WIDE = GROUPS_PER_TILE * SSM_STATE
LANE_GROUPS_PER_TILE = SSM_WIDE // LANES


def _chan(j):
    return slice(j * LANES, (j + 1) * LANES)


def _time_major_rows(j, q, tc):
    return pl.ds(j * LANE_GROUPS_PER_TILE + q, tc, stride=STATE_VREG_ROWS)


def _to_time_major(x, t_re_ref, t_im_ref, dst_re, dst_im, tc):
    for j in range(SSM_TILES):
        xj = x[:, _chan(j)]
        for t_ref, dst in ((t_re_ref, dst_re), (t_im_ref, dst_im)):
            r = _dot(xj, t_ref[j], 1, 0)
            for q in range(LANE_GROUPS_PER_TILE):
                dst[_time_major_rows(j, q, tc), :] = r[:, q * LANES:(q + 1) * LANES]


def _from_time_major(src, j, tc):
    return jnp.concatenate([src[_time_major_rows(j, q, tc), :] for q in range(LANE_GROUPS_PER_TILE)], axis=1)


def _scan_chunk(w_re, w_im, h_re, h_im, a_re, a_im, start, tc):
    def step(t, carry):
        hr, hi = carry
        rows = _scan_rows(t)
        nr = a_re * hr - a_im * hi + w_re[rows, :]
        ni = a_re * hi + a_im * hr + w_im[rows, :]
        h_re[rows, :] = nr
        h_im[rows, :] = ni
        return nr, ni

    return lax.fori_loop(0, tc, step, start, unroll=8)


SSM_CHUNK = 256


def _tile_spec(stack, k):
    return pl.BlockSpec((pl.Squeezed(),) + tuple(stack.shape[1:]), lambda i: (k, 0, 0, 0))


def _expand_block_diagonal(src_ref, dst):
    dst[...] = jnp.zeros_like(dst)
    r, c = src_ref.shape[1:]
    for g in range(SSM_GROUPS):
        j, gl = divmod(g, GROUPS_PER_TILE)
        dst[j, gl * r:(gl + 1) * r, gl * c:(gl + 1) * c] = src_ref[g].astype(dst.dtype)


def _extract_block_diagonal(src, dst_ref):
    r, c = dst_ref.shape[1:]
    for g in range(SSM_GROUPS):
        j, gl = divmod(g, GROUPS_PER_TILE)
        dst_ref[g] = src[j, gl * r:(gl + 1) * r, gl * c:(gl + 1) * c]


def _ssm_fwd(proj, blocks_cn, blocks_nc, a_re, a_im, gain, *, name, tc=SSM_CHUNK):
    n_rows = proj.shape[0]
    n_chunk = n_rows // tc

    def body(u_ref, br_ref, bi_ref, cr_ref, ci_ref, ar_ref, ai_ref, g_ref, y_ref, gy_ref, hr, hi, wr, wi, state,
             tbr_ref, tbi_ref, tcr_ref, tci_ref):
        @pl.when(pl.program_id(0) == 0)
        def _():
            state[...] = jnp.zeros_like(state)
            for src_ref, dst in ((br_ref, tbr_ref), (bi_ref, tbi_ref), (cr_ref, tcr_ref), (ci_ref, tci_ref)):
                _expand_block_diagonal(src_ref, dst)

        u = u_ref[...]
        _to_time_major(u, tbr_ref, tbi_ref, wr, wi, tc)
        state[0], state[1] = _scan_chunk(wr, wi, hr, hi, ar_ref[...], ai_ref[...], (state[0], state[1]), tc)
        for j in range(SSM_TILES):
            yj = (_dot(_from_time_major(hr, j, tc), tcr_ref[j], 1, 0) + _dot(_from_time_major(hi, j, tc), tci_ref[j], 1, 0)
                  + g_ref[:, _chan(j)] * u[:, _chan(j)])
            y_ref[:, _chan(j)] = yj
            gy_ref[:, _chan(j)] = jax.nn.gelu(yj).astype(gy_ref.dtype)

    rows = pl.BlockSpec((tc, SSM_WIDTH), lambda i: (i, 0))
    coef = pl.BlockSpec((STATE_VREG_ROWS, LANES), lambda i: (0, 0))
    states = pl.BlockSpec((tc * STATE_VREG_ROWS, LANES), lambda i: (i, 0))
    sshape = jax.ShapeDtypeStruct((n_rows * STATE_VREG_ROWS, LANES), F32)
    return pl.pallas_call(
        body, name=name, grid=(n_chunk,),
        in_specs=[rows, _tile_spec(blocks_cn, 0), _tile_spec(blocks_cn, 1), _tile_spec(blocks_nc, 0),
                  _tile_spec(blocks_nc, 1), coef, coef, pl.BlockSpec((1, SSM_WIDTH), lambda i: (0, 0))],
        out_specs=[rows, rows, states, states],
        out_shape=[jax.ShapeDtypeStruct((n_rows, SSM_WIDTH), F32), jax.ShapeDtypeStruct((n_rows, SSM_WIDTH), MXU_DTYPE),
                   sshape, sshape],
        scratch_shapes=[pltpu.VMEM((tc * STATE_VREG_ROWS, LANES), F32)] * 2 + [pltpu.VMEM((2, STATE_VREG_ROWS, LANES), F32)]
        + [pltpu.VMEM((SSM_TILES, LANES, SSM_WIDE), MXU_DTYPE)] * 2 + [pltpu.VMEM((SSM_TILES, SSM_WIDE, LANES), MXU_DTYPE)] * 2,
        compiler_params=_params("arbitrary"),
    )(proj, blocks_cn, blocks_cn, blocks_nc, blocks_nc, a_re, a_im, gain)


def _ssm_bwd(proj, dy, h_re, h_im, blocks_cn, blocks_nc, a_re, a_im, gain, *, name, tc=SSM_CHUNK):
    n_rows = proj.shape[0]
    n_chunk = n_rows // tc

    def body(u_ref, dy_ref, hr, hi, cr_ref, ci_ref, br_ref, bi_ref, ar_ref, ai_ref, g_ref,
             du_ref, su_ref, dc_re_ref, dc_im_ref, db_re_ref, db_im_ref, dar_ref, dai_ref, wr, wi, carry,
             tdr_ref, tdi_ref, tur_ref, tui_ref, dcr_ref, dci_ref, dbr_ref, dbi_ref):
        @pl.when(pl.program_id(0) == 0)
        def _():
            carry[...] = jnp.zeros_like(carry)
            for acc_ref in (su_ref, dcr_ref, dci_ref, dbr_ref, dbi_ref):
                acc_ref[...] = jnp.zeros_like(acc_ref)
            for src_ref, dst in ((cr_ref, tdr_ref), (ci_ref, tdi_ref), (br_ref, tur_ref), (bi_ref, tui_ref)):
                _expand_block_diagonal(src_ref, dst)

        a_r, a_i = ar_ref[...], ai_ref[...]
        u, dyv = u_ref[...], dy_ref[...]
        _to_time_major(dyv, tdr_ref, tdi_ref, wr, wi, tc)

        def step(kk, c):
            lam_r, lam_i, dar, dai = c
            rows = _scan_rows(tc - 1 - kk)
            h_r, h_i = hr[rows, :], hi[rows, :]
            dar = dar + lam_r * h_r + lam_i * h_i
            dai = dai + lam_i * h_r - lam_r * h_i
            new_r = wr[rows, :] + a_r * lam_r + a_i * lam_i
            new_i = wi[rows, :] + a_r * lam_i - a_i * lam_r
            wr[rows, :] = new_r
            wi[rows, :] = new_i
            return new_r, new_i, dar, dai

        carry[0], carry[1], carry[2], carry[3] = lax.fori_loop(0, tc, step, (carry[0], carry[1], carry[2], carry[3]),
                                                              unroll=8)
        dar_ref[...] = carry[2]
        dai_ref[...] = carry[3]
        for j in range(SSM_TILES):
            cj = _chan(j)
            lam_r, lam_i = _from_time_major(wr, j, tc), _from_time_major(wi, j, tc)
            dcr_ref[j] += _dot(dyv[:, cj], _from_time_major(hr, j, tc), 0, 0)
            dci_ref[j] += _dot(dyv[:, cj], _from_time_major(hi, j, tc), 0, 0)
            dbr_ref[j] += _dot(u[:, cj], lam_r, 0, 0)
            dbi_ref[j] += _dot(u[:, cj], lam_i, 0, 0)
            duj = _dot(lam_r, tur_ref[j], 1, 0) + _dot(lam_i, tui_ref[j], 1, 0) + g_ref[:, cj] * dyv[:, cj]
            du_ref[:, cj] = duj.astype(du_ref.dtype)
            su_ref[:, cj] += _colsum(duj)

        @pl.when(pl.program_id(0) == n_chunk - 1)
        def _():
            for src, dst_ref in ((dcr_ref, dc_re_ref), (dci_ref, dc_im_ref), (dbr_ref, db_re_ref), (dbi_ref, db_im_ref)):
                _extract_block_diagonal(src, dst_ref)

    back = lambda i: (n_chunk - 1 - i, 0)
    rows = pl.BlockSpec((tc, SSM_WIDTH), back)
    blocks = pl.BlockSpec((SSM_GROUPS, SSM_GROUP, SSM_STATE), lambda i: (0, 0, 0))
    coef = pl.BlockSpec((STATE_VREG_ROWS, LANES), lambda i: (0, 0))
    states = pl.BlockSpec((tc * STATE_VREG_ROWS, LANES), back)
    vec = pl.BlockSpec((1, SSM_WIDTH), lambda i: (0, 0))
    bshape = jax.ShapeDtypeStruct((SSM_GROUPS, SSM_GROUP, SSM_STATE), F32)
    cshape = jax.ShapeDtypeStruct((STATE_VREG_ROWS, LANES), F32)
    return pl.pallas_call(
        body, name=name, grid=(n_chunk,),
        in_specs=[rows, rows, states, states, _tile_spec(blocks_cn, 2), _tile_spec(blocks_cn, 3), _tile_spec(blocks_nc, 2),
                  _tile_spec(blocks_nc, 3), coef, coef, vec],
        out_specs=[rows, vec, blocks, blocks, blocks, blocks, coef, coef],
        out_shape=[jax.ShapeDtypeStruct((n_rows, SSM_WIDTH), MXU_DTYPE), jax.ShapeDtypeStruct((1, SSM_WIDTH), F32),
                   bshape, bshape, bshape, bshape, cshape, cshape],
        scratch_shapes=[pltpu.VMEM((tc * STATE_VREG_ROWS, LANES), F32)] * 2 + [pltpu.VMEM((4, STATE_VREG_ROWS, LANES), F32)]
        + [pltpu.VMEM((SSM_TILES, LANES, SSM_WIDE), MXU_DTYPE)] * 2 + [pltpu.VMEM((SSM_TILES, SSM_WIDE, LANES), MXU_DTYPE)] * 2
        + [pltpu.VMEM((SSM_TILES, LANES, SSM_WIDE), F32)] * 4,
        compiler_params=_params("arbitrary"),
    )(proj, dy, h_re, h_im, blocks_cn, blocks_cn, blocks_nc, blocks_nc, a_re, a_im, gain)


def _scan_rows(t):
    return pl.ds(pl.multiple_of(t * STATE_VREG_ROWS, 8), STATE_VREG_ROWS)


GATHER_GROUPS = (("w_glu", "w_att_up", "w_mix_out"), ("w_xq", "w_xkv", "w_xo", "w_ff1", "w_ff2"))
SCATTER_GROUPS = (("w_ff2", "w_ff1"), ("w_xo", "w_xq", "w_xkv", "w_mix_out"), ("w_att_up", "w_glu"), ("w_in",))


def _local_grads(x, mem, pos_col, target, sm, fetch_in, fetch, send, send_small, start_token):
    b_re_t = sm["ssm_b_re"].transpose(2, 0, 1)
    b_im_t = sm["ssm_b_im"].transpose(2, 0, 1)
    logdt = sm["ssm_log_dt"].reshape(SSM_GROUPS, 1)
    c_re, c_im = sm["ssm_c_re"], sm["ssm_c_im"]
    grp = (SSM_GROUPS, SSM_STATE)
    chn = (SSM_GROUP, SSM_GROUPS, SSM_STATE)

    wts = {}
    cos_t, sin_t = _rope_tables(pos_col, after=start_token, name="rope_tables")
    h0, xh0, rs0, h0m = _ln_fwd(x, sm["ln_in_g"], sm["ln_in_b"], name="ln_in_fwd")
    disc_in = (logdt, sm["ssm_a_re"], sm["ssm_a_im"], b_re_t, b_im_t)
    ab_re, ab_im, bb_re_t, bb_im_t = _whole(_disc, disc_in, [grp, grp, chn, chn], name="ssm_disc")
    a_re_rows, a_im_rows = ab_re.reshape(STATE_VREG_ROWS, LANES), ab_im.reshape(STATE_VREG_ROWS, LANES)
    tiles_cn = jnp.stack([bb_re_t.transpose(1, 0, 2), bb_im_t.transpose(1, 0, 2), c_re, -c_im])
    tiles_nc = jnp.stack([c_re.transpose(0, 2, 1), -c_im.transpose(0, 2, 1), bb_re_t.transpose(1, 2, 0),
                          bb_im_t.transpose(1, 2, 0)])
    w_in_near, near_ids = fetch_in(0, [h0m, tiles_cn, tiles_nc])
    proj = _mm_shards(h0m, w_in_near, sm["b_in"], near_ids, name="in_proj_near")
    wts["w_in"], far_ids = fetch_in(1, [proj])
    proj = _mm_shards(h0m, wts["w_in"], sm["b_in"], far_ids, prev=proj, name="in_proj_far")

    y, gy, h_re, h_im = _ssm_fwd(proj, tiles_cn, tiles_nc, a_re_rows, a_im_rows, sm["ssm_d"], name="ssm_fwd")

    q, k, v = _qkv_split(proj, cos_t, sin_t, name="qkv_split")
    outs, lses = [], []
    for g, dil in enumerate(DILATIONS):
        o_g, l_g = _dil_fwd(q[g], k[g], v[g], dil, name=f"dil_att_fwd_{dil}")
        outs.append(o_g)
        lses.append(l_g)
    att, lse = _att_merge(outs, lses, name="att_merge")
    wts.update(fetch(0, [att]))
    z = _mm(gy, wts["w_glu"], bias=sm["b_glu"], b_shards=True, name="glu_proj")
    b_att = _mm(att, wts["w_att_up"], b_shards=True, name="att_up")

    mixed, h1, xh1, rs1, h1m = _mix_out_ln(proj, z, b_att, wts["w_mix_out"], sm["b_mix_out"], h0, sm["ln1_g"],
                                           sm["ln1_b"], alpha=DEEPNORM_ALPHA, name="gate_mix_out_ln1")

    wts.update(fetch(1, [h1m]))
    xq = _mm(h1m, wts["w_xq"], out_dtype=MXU_DTYPE, name="xatt_q")
    kv = _mm(mem, wts["w_xkv"], out_dtype=MXU_DTYPE, b_shards=True, name="xatt_kv")
    xo_in = _xatt_fwd(xq, kv, name="xatt_fwd")
    h2, xh2, rs2, h2m = _mm_ln_fwd(xo_in, wts["w_xo"], None, h1, sm["ln2_g"], sm["ln2_b"], alpha=DEEPNORM_ALPHA,
                                   name="xatt_o_ln2")

    pre, act = _mm(h2m, wts["w_ff1"], bias=sm["b_ff1"], b_shards=True, name="ff1",
                   also=(lambda r: jnp.square(jnp.maximum(r, 0.0)), MXU_DTYPE))
    ff = _mm(act, wts["w_ff2"], bias=sm["b_ff2"], name="ff2")

    gw, gs = {}, {}
    dr3, dr3m, gs["ln3_g"], gs["ln3_b"], gs["b_ff2"], loss_row = _ln_loss_bwd(
        h2, ff, target, sm["ln3_g"], sm["ln3_b"], alpha=DEEPNORM_ALPHA, name="ln3_loss")
    wgrad = functools.partial(_mm, ta=True, out_dtype=WIRE_DTYPE, tk=2048)
    wgrad_cols = functools.partial(wgrad, out_shards=True, tk=x.shape[0])
    gw["w_ff2"] = wgrad(act, dr3m, tk=1024, name="ff2_dw")
    dpre, gs["b_ff1"] = _mm(dr3m, wts["w_ff2"], tb=True, out_dtype=MXU_DTYPE, colsum=True, name="ff2_dx",
                            gate=(pre, lambda p: 2.0 * jnp.maximum(p, 0.0)))
    gw["w_ff1"] = wgrad_cols(h2m, dpre, name="ff1_dw")
    sent = send(0, gw)
    dh2 = _mm(dpre, wts["w_ff1"], tb=True, b_shards=True, after=sent, name="ff1_dx")
    dr2, dr2m, gs["ln2_g"], gs["ln2_b"], _ = _ln_bwd(dr3, dh2, xh2, rs2, sm["ln2_g"], alpha=DEEPNORM_ALPHA,
                                                     name="ln2_bwd")
    gw["w_xo"] = wgrad(xo_in, dr2m, name="xatt_o_dw")
    dxo_in = _mm(dr2m, wts["w_xo"], tb=True, out_dtype=MXU_DTYPE, name="xatt_o_dx")
    dxq, dkv = _xatt_bwd(xq, kv, dxo_in, name="xatt_bwd")
    gw["w_xq"] = wgrad(h1m, dxq, name="xatt_q_dw")
    gw["w_xkv"] = wgrad_cols(mem, dkv, name="xatt_kv_dw")
    dr1, dr1m, gs["ln1_g"], gs["ln1_b"], gs["b_mix_out"] = _mm_ln_bwd(
        dxq, wts["w_xq"], dr2, xh1, rs1, sm["ln1_g"], alpha=DEEPNORM_ALPHA, name="xatt_q_dx_ln1")
    gw["w_mix_out"] = wgrad(mixed, dr1m, name="mix_out_dw")
    sent = send(1, gw)
    dmixed = _mm(dr1m, wts["w_mix_out"], tb=True, after=sent, name="mix_out_dx")
    dgs, dga, dz, db_att, s_gs, s_ga, gs["b_glu"] = _mix_bwd(dmixed, proj, z, b_att, name="gate_mix_bwd")

    gw["w_att_up"] = wgrad_cols(att, db_att, name="att_up_dw")
    gw["w_glu"] = wgrad_cols(gy, dz, name="glu_dw")
    sent = send(2, gw)
    datt = _mm(db_att, wts["w_att_up"], tb=True, b_shards=True, after=sent, name="att_up_dx")
    stats = _att_stats(datt, att, lse, name="att_stats")
    dqkv = [_dil_bwd(q[g], k[g], v[g], datt, stats, dil, name=f"dil_att_bwd_{dil}") for g, dil in enumerate(DILATIONS)]

    dgy = _mm(dz, wts["w_glu"], tb=True, b_shards=True, name="glu_dx")
    dy, gs["ssm_d"] = _gelu_bwd(dgy, y, proj, name="gelu_bwd")
    du, s_u, dc_re_t, dc_im_t, dbb_re_t, dbb_im_t, da_re, da_im = _ssm_bwd(
        proj, dy, h_re, h_im, tiles_cn, tiles_nc, a_re_rows, a_im_rows, sm["ssm_d"], name="ssm_bwd")
    gs["ssm_c_re"], gs["ssm_c_im"] = dc_re_t, -dc_im_t
    disc_ct = (da_re.reshape(grp), da_im.reshape(grp), dbb_re_t.transpose(1, 0, 2), dbb_im_t.transpose(1, 0, 2))
    d_logdt, gs["ssm_a_re"], gs["ssm_a_im"], d_b_re_t, d_b_im_t = _whole(
        _disc_transpose, disc_in + disc_ct, [(SSM_GROUPS, 1), grp, grp, chn, chn], name="ssm_disc_bwd")
    gs["ssm_log_dt"] = d_logdt
    gs["ssm_b_re"], gs["ssm_b_im"] = d_b_re_t.transpose(1, 2, 0), d_b_im_t.transpose(1, 2, 0)

    dproj, s_qkv = _dproj_assemble(du, dqkv, dgs, dga, cos_t, sin_t, name="dproj_assemble")
    gs["b_in"] = jnp.concatenate([s_u, *s_qkv, s_gs, s_ga], axis=1)
    sent = send_small(gs, SMALL_EARLY)
    gw["w_in"] = wgrad_cols(h0m, dproj, out_shards=2, after=sent, name="in_proj_dw")
    sent = send(3, gw)
    dh0 = _mm(dproj, wts["w_in"], tb=True, b_shards=True, after=sent, name="in_proj_dx")
    grad_x, gs["ln_in_g"], gs["ln_in_b"], _ = _ln_bwd(dr1, dh0, xh0, rs0, sm["ln_in_g"], alpha=DEEPNORM_ALPHA,
                                                      operand=False, name="ln_in_bwd")
    return loss_row, grad_x, gs


_IN_HBM = pl.BlockSpec(memory_space=pltpu.HBM)
_IN_SEMAPHORE = pl.BlockSpec(memory_space=pltpu.SEMAPHORE)


def _device_index():
    return 4 * lax.axis_index("x") + 2 * lax.axis_index("y") + lax.axis_index("c")


ALL_PEERS = tuple(range(1, N_DEV))
NEAR_PEERS = (1, 2, 3, 4, 5)
FAR_PEERS = (6, 7)


def _peer_index(kk):
    x, y, c = lax.axis_index("x"), lax.axis_index("y"), lax.axis_index("c")
    return 4 * ((x + (kk >> 2)) % 2) + 2 * ((y + ((kk >> 1) & 1)) % 2) + (c + (kk & 1)) % 2


def _exchange_copies(src_refs, land_refs, send_sems, recv_sems, scatter, peers):
    x, y, c = lax.axis_index("x"), lax.axis_index("y"), lax.axis_index("c")
    me = 4 * x + 2 * y + c
    pairs = []
    for a, (src_ref, land_ref) in enumerate(zip(src_refs, land_refs)):
        for idx, kk in enumerate(peers):
            px = (x + (kk >> 2)) % 2
            py = (y + ((kk >> 1) & 1)) % 2
            pc = (c + (kk & 1)) % 2
            peer = 4 * px + 2 * py + pc
            sem = a * len(peers) + idx
            src = src_ref.at[peer] if scatter else src_ref

            def copy(dst, src=src, sem=sem, px=px, py=py, pc=pc):
                return pltpu.make_async_remote_copy(
                    src_ref=src, dst_ref=dst, send_sem=send_sems.at[sem], recv_sem=recv_sems.at[sem],
                    device_id=(px, py, pc), device_id_type=pl.DeviceIdType.MESH)

            pairs.append((functools.partial(copy, land_ref.at[me]), functools.partial(copy, land_ref.at[peer])))
    return pairs


def _own_copies(src_refs, land_refs, own_sems, scatter):
    me = _device_index()
    return [functools.partial(pltpu.make_async_copy, src_ref.at[me] if scatter else src_ref, land_ref.at[me],
                              own_sems.at[a]) for a, (src_ref, land_ref) in enumerate(zip(src_refs, land_refs))]


def _exchange_start(srcs, *, scatter, name, after=None, peers=ALL_PEERS, lands=None):
    n_arr, n_sem = len(srcs), len(srcs) * len(peers)
    own = lands is None
    if own:
        lands = [lax.empty((N_DEV,) + tuple(s.shape[1:] if scatter else s.shape), s.dtype) for s in srcs]
    n_in = 2 * n_arr + (after is not None)

    def body(*refs):
        send_sems, recv_sems = refs[n_in], refs[n_in + 1]
        for sent, _ in _exchange_copies(refs[:n_arr], refs[n_arr:2 * n_arr], send_sems, recv_sems, scatter, peers):
            sent().start()
        if own:
            for local in _own_copies(refs[:n_arr], refs[n_arr:2 * n_arr], refs[n_in + 2], scatter):
                local().start()
        refs[-1][...] = jnp.zeros_like(refs[-1])

    sems = [pltpu.SemaphoreType.DMA((n_sem,)), pltpu.SemaphoreType.DMA((n_sem,))] + [pltpu.SemaphoreType.DMA((n_arr,))] * own
    through = [pltpu.HBM(t.shape, t.dtype) for t in (*srcs, *lands)]
    res = pl.pallas_call(
        body, name=name, out_shape=(*sems, *through, jax.ShapeDtypeStruct((8, LANES), F32)),
        in_specs=[_IN_HBM] * (2 * n_arr) + [pl.BlockSpec(memory_space=pl.ANY)] * (after is not None),
        out_specs=(*[_IN_SEMAPHORE] * len(sems), *[_IN_HBM] * (2 * n_arr), pl.BlockSpec(memory_space=pltpu.VMEM)),
        input_output_aliases={i: len(sems) + i for i in range(2 * n_arr)},
        compiler_params=pltpu.CompilerParams(has_side_effects=pltpu.SideEffectType.DATAFLOW_SIDE_EFFECTING),
    )(*[pltpu.with_memory_space_constraint(t, pltpu.HBM) for t in (*srcs, *lands)],
      *([after] if after is not None else []))
    first = len(sems)
    handle = dict(sems=res[:first], srcs=res[first:first + n_arr], lands=res[first + n_arr:first + 2 * n_arr],
                  scatter=scatter, peers=peers, own=own)
    return handle, res[-1]


def _exchange_wait(handle, *, after, name, srcs=None, lands=None):
    srcs = handle["srcs"] if srcs is None else srcs
    lands = handle["lands"] if lands is None else lands
    sems, scatter, peers, own = handle["sems"], handle["scatter"], handle["peers"], handle["own"]
    n_arr = len(srcs)
    after = list(after)

    def body(*refs):
        src_refs, land_refs = refs[:n_arr], refs[n_arr:2 * n_arr]
        for sent, received in _exchange_copies(src_refs, land_refs, refs[2 * n_arr], refs[2 * n_arr + 1], scatter, peers):
            sent().wait_send()
            received().wait_recv()
        if own:
            for local in _own_copies(src_refs, land_refs, refs[2 * n_arr + 2], scatter):
                local().wait()

    res = pl.pallas_call(
        body, name=name, out_shape=tuple(pltpu.HBM(t.shape, t.dtype) for t in (*srcs, *lands)),
        in_specs=[_IN_HBM] * (2 * n_arr) + [_IN_SEMAPHORE] * len(sems) + [pl.BlockSpec(memory_space=pl.ANY)] * len(after),
        out_specs=tuple([_IN_HBM] * (2 * n_arr)), input_output_aliases={i: i for i in range(2 * n_arr)},
        compiler_params=pltpu.CompilerParams(has_side_effects=pltpu.SideEffectType.DATAFLOW_SIDE_EFFECTING),
    )(*srcs, *lands, *sems, *after)
    return res[:n_arr], res[n_arr:]


def _adamw(g, w, m, v):
    m_new = ADAM_B1 * m + (1.0 - ADAM_B1) * g
    v_new = ADAM_B2 * v + (1.0 - ADAM_B2) * jnp.square(g)
    m_hat = m_new / (1.0 - ADAM_B1 ** ADAM_STEP)
    v_hat = v_new / (1.0 - ADAM_B2 ** ADAM_STEP)
    return g, -ADAM_LR * (m_hat / (jnp.sqrt(v_hat) + ADAM_EPS) + ADAM_WD * w), m_new, v_new


def _reduce_adamw(gstack, w, m, v, *, name, tr=128):
    n_rows, cols = w.shape
    tr = min(tr, n_rows)
    assert n_rows % tr == 0, (name, n_rows, tr)

    def body(g_ref, w_ref, m_ref, v_ref, *out_refs):
        g = g_ref[0].astype(F32)
        for dev in range(1, N_DEV):
            g = g + g_ref[dev].astype(F32)
        for o_ref, val in zip(out_refs, _adamw(g, w_ref[...], m_ref[...], v_ref[...])):
            o_ref[...] = val

    flat = pl.BlockSpec((tr, cols), lambda i: (i, 0))
    shape = jax.ShapeDtypeStruct((n_rows, cols), F32)
    return pl.pallas_call(
        body, name=name, grid=(n_rows // tr,),
        in_specs=[pl.BlockSpec((N_DEV, tr, cols), lambda i: (0, i, 0)), flat, flat, flat],
        out_specs=[flat] * 4, out_shape=[shape] * 4, compiler_params=_params("parallel"),
    )(gstack, w, m, v)


SMALL_FLAT_SSM = ("ssm_b_re", "ssm_b_im", "ssm_c_re", "ssm_c_im")


def _small_view(name, shape):
    size = int(np.prod(shape))
    if name in SMALL_FLAT_SSM:
        return SSM_GROUPS, size // SSM_GROUPS
    if name in ("ssm_a_re", "ssm_a_im"):
        return SSM_GROUPS, SSM_STATE
    return 1, size


def _pack_rows(view):
    return -(-(view[0] * view[1]) // PACK_COLS)


SMALL_LATE = ("ln_in_g", "ln_in_b")
SMALL_EARLY = tuple(n for n in SMALL if n not in SMALL_LATE)


def _pack_small(gs, names, views):
    parts = []
    for n in names:
        flat = gs[n].reshape(-1).astype(WIRE_DTYPE)
        parts.append(jnp.pad(flat, (0, _pack_rows(views[n]) * PACK_COLS - flat.shape[0])))
    total = sum(p.shape[0] for p in parts) // PACK_COLS
    parts.append(jnp.zeros(((-total % PACK_ROW_ALIGN) * PACK_COLS,), WIRE_DTYPE))
    return jnp.concatenate(parts).reshape(-1, PACK_COLS)


def _small_pieces(view):
    rows, cols = view
    if cols == PACK_COLS:
        return [(0, rows, 0, 0, 0, cols)]
    if rows == 1 and cols > PACK_COLS:
        return [(kk, 1, 0, 0, kk * PACK_COLS, PACK_COLS) for kk in range(cols // PACK_COLS)]
    if rows == 1:
        return [(0, 1, 0, 0, 0, cols)]
    return [((r * cols) // PACK_COLS, 1, (r * cols) % PACK_COLS, r, 0, cols) for r in range(rows)]


def _adamw_small(stacks, views, w, m, v, *, name):
    n = len(SMALL)
    place, first = {}, [0, 0]
    for k, names in enumerate((SMALL_EARLY, SMALL_LATE)):
        for name_ in names:
            place[name_] = (k, first[k])
            first[k] += _pack_rows(views[name_])

    def body(early_ref, late_ref, *refs):
        ins, outs = refs[:3 * n], refs[3 * n:]
        for i, name_ in enumerate(SMALL):
            stack_ref = (early_ref, late_ref)[place[name_][0]]
            row0 = place[name_][1]
            for prow, nrows, lane, orow, ocol, width in _small_pieces(views[name_]):
                src = (slice(row0 + prow, row0 + prow + nrows), slice(lane, lane + width))
                dst = (slice(orow, orow + nrows), slice(ocol, ocol + width))
                g = stack_ref[(0,) + src].astype(F32)
                for dev in range(1, N_DEV):
                    g = g + stack_ref[(dev,) + src].astype(F32)
                res = _adamw(g, ins[i][dst], ins[n + i][dst], ins[2 * n + i][dst])
                for kk, val in enumerate(res):
                    outs[kk * n + i][dst] = val

    args = [*stacks, *[d[name_] for d in (w, m, v) for name_ in SMALL]]
    out_views = [views[name_] for _ in range(4) for name_ in SMALL]
    res = pl.pallas_call(
        body, name=name, grid=(1,), in_specs=[_full_spec(t.shape) for t in args],
        out_specs=[_full_spec(s) for s in out_views], out_shape=[jax.ShapeDtypeStruct(s, F32) for s in out_views],
        compiler_params=_params("arbitrary"),
    )(*args)
    return [dict(zip(SMALL, res[kk * n:(kk + 1) * n])) for kk in range(4)]


def kernel(x, mem, positions, ln_in_g, ln_in_b, w_in, b_in, ssm_log_dt, ssm_a_re, ssm_a_im, ssm_b_re, ssm_b_im, ssm_c_re, ssm_c_im, ssm_d, w_glu, b_glu, w_att_up, w_mix_out, b_mix_out, ln1_g, ln1_b, w_xq, w_xkv, w_xo, ln2_g, ln2_b, w_ff1, b_ff1, w_ff2, b_ff2, ln3_g, ln3_b, loss_target, m_ln_in_g, m_ln_in_b, m_w_in, m_b_in, m_ssm_log_dt, m_ssm_a_re, m_ssm_a_im, m_ssm_b_re, m_ssm_b_im, m_ssm_c_re, m_ssm_c_im, m_ssm_d, m_w_glu, m_b_glu, m_w_att_up, m_w_mix_out, m_b_mix_out, m_ln1_g, m_ln1_b, m_w_xq, m_w_xkv, m_w_xo, m_ln2_g, m_ln2_b, m_w_ff1, m_b_ff1, m_w_ff2, m_b_ff2, m_ln3_g, m_ln3_b, v_ln_in_g, v_ln_in_b, v_w_in, v_b_in, v_ssm_log_dt, v_ssm_a_re, v_ssm_a_im, v_ssm_b_re, v_ssm_b_im, v_ssm_c_re, v_ssm_c_im, v_ssm_d, v_w_glu, v_b_glu, v_w_att_up, v_w_mix_out, v_b_mix_out, v_ln1_g, v_ln1_b, v_w_xq, v_w_xkv, v_w_xo, v_ln2_g, v_ln2_b, v_w_ff1, v_b_ff1, v_w_ff2, v_b_ff2, v_ln3_g, v_ln3_b):
    given = dict(locals())
    w_arg = {n: given[n] for n in WEIGHTS}
    m_arg = {n: given["m_" + n] for n in WEIGHTS}
    v_arg = {n: given["v_" + n] for n in WEIGHTS}

    in_near, token = _exchange_start([w_arg["w_in"][0].astype(MXU_DTYPE)], scatter=False, peers=NEAR_PEERS,
                                     name="gather_start_in_near")
    in_far, token = _exchange_start(in_near["srcs"], scatter=False, peers=FAR_PEERS, lands=in_near["lands"],
                                    after=token, name="gather_start_in_far")
    w_in_state = [in_far["srcs"], in_far["lands"]]
    token, w_arg, m_arg, v_arg = lax.optimization_barrier((token, w_arg, m_arg, v_arg))
    shards = {n: w_arg[n][0].astype(MXU_DTYPE) for n in BIG if n != "w_in"}
    gathers = []
    for i, names in enumerate(GATHER_GROUPS):
        handle, token = _exchange_start([shards[n] for n in names], scatter=False, after=token, name=f"gather_start_{i}")
        gathers.append(handle)

    small_views = {n: _small_view(n, w_arg[n].shape) for n in SMALL}
    small_w, small_m, small_v = [{n: d[n].reshape(small_views[n]) for n in SMALL} for d in (w_arg, m_arg, v_arg)]
    relaid = [d[n] for d in (small_w, small_m, small_v) for n in SMALL_FLAT_SSM]

    def fetch_in(part, after):
        handle, peers, tag = ((in_near, (0,) + NEAR_PEERS, "near"), (in_far, FAR_PEERS, "far"))[part]
        w_in_state[:] = _exchange_wait(handle, after=after + (relaid if part == 0 else []), srcs=w_in_state[0],
                                       lands=w_in_state[1], name="gather_wait_in_" + tag)
        return w_in_state[1][0], jnp.stack([_peer_index(kk) for kk in peers]).astype(jnp.int32)

    def fetch(i, after):
        _, lands = _exchange_wait(gathers[i], after=after, name=f"gather_wait_{i}")
        full = dict(zip(GATHER_GROUPS[i], lands))
        return {n: t if n in BIG_COL_SHARDED else t.reshape(-1, t.shape[-1]) for n, t in full.items()}

    scatters = {}

    def send(i, gw):
        slots = [gw[n] if n in BIG_COL_SHARDED else gw[n].reshape(N_DEV, -1, gw[n].shape[-1]) for n in SCATTER_GROUPS[i]]
        handle, sent = _exchange_start(slots, scatter=True, name=f"scatter_start_{i}")
        scatters[i] = (handle, slots)
        return sent

    sm = {}
    for n in SMALL:
        t = w_arg[n]
        if n.startswith("ssm_") and n not in ("ssm_d", "ssm_log_dt"):
            sm[n] = t[0]
        else:
            sm[n] = t.reshape(1, -1)

    smalls = []

    def send_small(gs, names):
        handle, sent = _exchange_start([_pack_small(gs, names, small_views)], scatter=False,
                                       name=f"small_start_{len(smalls)}")
        smalls.append(handle)
        return sent

    loss_row, grad_x, gs = _local_grads(x[0], mem[0], positions.reshape(-1, 1), loss_target[0], sm, fetch_in, fetch,
                                        send, send_small, token)
    loss = lax.psum(loss_row[0, 0], ("x", "y", "c"))
    send_small(gs, SMALL_LATE)

    results = [{}, {}, {}, {}]
    done = grad_x
    for i, names in enumerate(SCATTER_GROUPS):
        handle, slots = scatters[i]
        _, lands = _exchange_wait(handle, after=[done], name=f"scatter_wait_{i}")
        for n, land, slot in zip(names, lands, slots):
            res = _reduce_adamw(land, w_arg[n][0], m_arg[n][0], v_arg[n][0], name="adamw_" + n)
            done = res[0]
            for d, r in zip(results, res):
                d[n] = r[None]
    stacks = [_exchange_wait(handle, after=[done], name=f"small_wait_{i}")[1][0] for i, handle in enumerate(smalls)]
    res = _adamw_small(stacks, small_views, small_w, small_m, small_v, name="adamw_small")
    for d, r in zip(results, res):
        d.update({n: r[n].reshape(w_arg[n].shape) for n in SMALL})
    out = [loss, grad_x[None]]
    for d in results:
        out += [d[n] for n in WEIGHTS]
    return tuple(out)
```

```python
import functools

import numpy as np
import jax
import jax.numpy as jnp
from jax import lax
from jax.experimental import pallas as pl
from jax.experimental.pallas import tpu as pltpu

F32 = jnp.float32
MXU_DTYPE = jnp.bfloat16
WIRE_DTYPE = jnp.bfloat16
VMEM_LIMIT_BYTES = 48 * 1024 * 1024
LANES = 128

N_DEV = 8
D_MODEL = 1024
SSM_GROUP = 16
SSM_WIDTH = 768
SSM_GROUPS = SSM_WIDTH // SSM_GROUP
SSM_STATE = 64
SSM_CH = SSM_GROUPS * SSM_STATE
SSM_TILES = SSM_WIDTH // LANES
GROUPS_PER_TILE = LANES // SSM_GROUP
STATE_VREG_ROWS = SSM_CH // LANES
ATT_HEAD_DIM = 64
ATT_HEADS_PER_GROUP = 4
ATT_MERGED = ATT_HEADS_PER_GROUP * ATT_HEAD_DIM
LANE_HALVES = ATT_MERGED // LANES
DILATIONS = (1, 4, 16)
ATT_BLK = 128
ATT_SCALE = ATT_HEAD_DIM ** -0.5
ROT_DIM = ATT_HEAD_DIM // 4
ROPE_THETA = 500000.0
XATT_HEADS = 4
XATT_HEAD_DIM = D_MODEL // XATT_HEADS
XATT_SCALE = XATT_HEAD_DIM ** -0.5
DEEPNORM_ALPHA = 2.0 ** 0.25
LN_EPS = 1e-5
NEG_INF = -1e30
OFF_Q_BLK, OFF_K_BLK, OFF_V_BLK = 3, 6, 9
OFF_GS_BLK, OFF_GA_BLK = 3, 4

ADAM_LR = 0.001
ADAM_B1 = 0.9
ADAM_B2 = 0.999
ADAM_EPS = 1e-08
ADAM_WD = 0.01
ADAM_STEP = 10

BIG = ("w_in", "w_glu", "w_att_up", "w_mix_out", "w_xq", "w_xkv", "w_xo", "w_ff1", "w_ff2")
BIG_COL_SHARDED = ("w_in", "w_glu", "w_att_up", "w_xkv", "w_ff1")
WEIGHTS = ("ln_in_g", "ln_in_b", "w_in", "b_in", "ssm_log_dt", "ssm_a_re", "ssm_a_im", "ssm_b_re", "ssm_b_im",
           "ssm_c_re", "ssm_c_im", "ssm_d", "w_glu", "b_glu", "w_att_up", "w_mix_out", "b_mix_out", "ln1_g", "ln1_b",
           "w_xq", "w_xkv", "w_xo", "ln2_g", "ln2_b", "w_ff1", "b_ff1", "w_ff2", "b_ff2", "ln3_g", "ln3_b")
SMALL = tuple(n for n in WEIGHTS if n not in BIG)
PACK_COLS = 1024
PACK_ROW_ALIGN = 16


def _params(*sem):
    return pltpu.CompilerParams(dimension_semantics=sem, vmem_limit_bytes=VMEM_LIMIT_BYTES)


def _dot(a, b, ca, cb):
    return lax.dot_general(a.astype(MXU_DTYPE), b.astype(MXU_DTYPE), (((ca,), (cb,)), ((), ())),
                           preferred_element_type=F32)


def _fit(dim, pref):
    if dim <= pref:
        return dim
    best = max(t for t in range(LANES, pref + 1, LANES) if dim % t == 0)
    return best


def _mm(a, b, *, name, ta=False, tb=False, bias=None, out_dtype=F32, b_shards=False, out_shards=False, after=None,
        also=None, gate=None, colsum=False, epilogue=None, k_shards=2, tm=2048, tn=1024, tk=1024):
    m, k = (a.shape[1], a.shape[0]) if ta else a.shape
    order = (lambda f: (lambda j, i, kk: f(i, j, kk))) if colsum else (lambda f: f)
    spec = lambda shape, f: pl.BlockSpec(shape, order(f))
    if b_shards:
        n_sh, rows, n_loc = b.shape
        if tb:
            n, tn, tk = rows, _fit(rows, tn), n_loc * k_shards
            assert k == n_sh * n_loc and n_sh % k_shards == 0, (name, k, b.shape)
            b_spec = spec((k_shards, tn, n_loc), lambda i, j, kk: (kk, j, 0))
        else:
            n, tn, tk = n_sh * n_loc, n_loc, _fit(k, tk)
            b_spec = spec((1, tk, tn), lambda i, j, kk: (j, kk, 0))
    else:
        n = b.shape[0] if tb else b.shape[1]
        tn = n // N_DEV * int(out_shards) if out_shards else _fit(n, tn)
        tk = _fit(k, tk)
        b_spec = spec((tn, tk), lambda i, j, kk: (j, kk)) if tb else spec((tk, tn), lambda i, j, kk: (kk, j))
    tm = _fit(m, tm)
    nk = k // tk
    a_spec = spec((tk, tm), lambda i, j, kk: (kk, i)) if ta else spec((tm, tk), lambda i, j, kk: (i, kk))
    tile = spec((tm, tn), lambda i, j, kk: (i, j))
    in_specs, args = [a_spec, b_spec], [a, b]
    if bias is not None:
        in_specs.append(spec((1, tn), lambda i, j, kk: (0, j)))
        args.append(bias)
    if gate is not None:
        in_specs.append(tile)
        args.append(gate[0])
    if after is not None:
        in_specs.append(pl.BlockSpec(memory_space=pl.ANY))
        args.append(after)
    if epilogue is not None:
        ep_fn, ep_rows, ep_fulls, ep_row_outs, ep_acc_outs = epilogue
        assert tn == n and not (colsum or also or gate or out_shards), name
        ep_first = len(args)
        in_specs += [spec((tm, t.shape[1]), lambda i, j, kk: (i, 0)) for t in ep_rows]
        in_specs += [pl.BlockSpec(t.shape, functools.partial(lambda i, j, kk, nd: (0,) * nd, nd=t.ndim)) for t in ep_fulls]
        args += [*ep_rows, *ep_fulls]
    n_in = len(args)
    if epilogue is not None:
        out_specs = [spec((tm, w), lambda i, j, kk: (i, 0)) for w, _ in ep_row_outs]
        out_specs += [spec((1, w), lambda i, j, kk: (0, 0)) for w in ep_acc_outs]
        out_shape = [jax.ShapeDtypeStruct((m, w), dt) for w, dt in ep_row_outs]
        out_shape += [jax.ShapeDtypeStruct((1, w), F32) for w in ep_acc_outs]
    elif out_shards:
        assert n % N_DEV == 0 and N_DEV % int(out_shards) == 0, (name, n, out_shards)
        out_specs = [spec((int(out_shards), tm, n // N_DEV), lambda i, j, kk: (j, i, 0))]
        out_shape = [jax.ShapeDtypeStruct((N_DEV, m, n // N_DEV), out_dtype)]
    else:
        out_specs = [tile]
        out_shape = [jax.ShapeDtypeStruct((m, n), out_dtype)]
    if also is not None:
        out_specs.append(tile)
        out_shape.append(jax.ShapeDtypeStruct((m, n), also[1]))
    if colsum:
        out_specs.append(spec((1, tn), lambda i, j, kk: (0, j)))
        out_shape.append(jax.ShapeDtypeStruct((1, n), F32))

    def body(*refs):
        a_ref, b_ref = refs[0], refs[1]
        o_ref = refs[n_in]
        first_row_tile = pl.program_id(1 if colsum else 0) == 0

        def product():
            if b_shards and tb:
                bv = jnp.concatenate([b_ref[s] for s in range(k_shards)], axis=1)
            else:
                bv = b_ref[0] if b_shards else b_ref[...]
            return _dot(a_ref[...], bv, 0 if ta else 1, 1 if tb else 0)

        def finish(r):
            if bias is not None:
                r = r + refs[2][...]
            if gate is not None:
                r = r * gate[1](refs[2 + (bias is not None)][...])
            if epilogue is not None:
                res = ep_fn(r, *[ref[...] for ref in refs[ep_first:n_in]])
                n_o = len(ep_row_outs)
                for ref, val in zip(refs[n_in:n_in + n_o], res[:n_o]):
                    ref[...] = val.astype(ref.dtype)
                acc_refs = refs[n_in + n_o:n_in + n_o + len(ep_acc_outs)]
                if acc_refs:
                    @pl.when(first_row_tile)
                    def _():
                        for ref in acc_refs:
                            ref[...] = jnp.zeros_like(ref)

                    for ref, val in zip(acc_refs, res[n_o:]):
                        ref[...] += val
                return
            if out_shards:
                for s in range(int(out_shards)):
                    o_ref[s] = r[:, s * (n // N_DEV):(s + 1) * (n // N_DEV)].astype(o_ref.dtype)
            else:
                o_ref[...] = r.astype(o_ref.dtype)
            if also is not None:
                refs[n_in + 1][...] = also[0](r).astype(also[1])
            if colsum:
                s_ref = refs[n_in + 1 + (also is not None)]

                @pl.when(first_row_tile)
                def _():
                    s_ref[...] = jnp.zeros_like(s_ref)

                s_ref[...] += _colsum(r)

        if nk == 1:
            finish(product())
            return
        acc_ref = refs[-1]
        kk = pl.program_id(2)

        @pl.when(kk == 0)
        def _():
            acc_ref[...] = product()

        if nk > 2:
            @pl.when((kk > 0) & (kk < nk - 1))
            def _():
                acc_ref[...] += product()

        @pl.when(kk == nk - 1)
        def _():
            finish(acc_ref[...] + product())

    grid = (n // tn, m // tm, nk) if colsum else (m // tm, n // tn, nk)
    res = pl.pallas_call(
        body, name=name, grid=grid, in_specs=in_specs, out_specs=out_specs, out_shape=out_shape,
        scratch_shapes=[pltpu.VMEM((tm, tn), F32)] if nk > 1 else [],
        compiler_params=_params("arbitrary" if epilogue is not None else "parallel",
                                "arbitrary" if colsum else "parallel", "arbitrary"),
    )(*args)
    return res[0] if len(res) == 1 else res


def _mm_shards(a, w, bias, shard_ids, *, name, prev=None, tm=2048):
    m, k = a.shape
    n_sh, _, n_loc = w.shape
    tm = _fit(m, tm)

    def body(ids_ref, a_ref, w_ref, b_ref, *rest):
        rest[-1][...] = _dot(a_ref[...], w_ref[0], 1, 0) + b_ref[...]

    grid_spec = pltpu.PrefetchScalarGridSpec(
        num_scalar_prefetch=1, grid=(m // tm, shard_ids.shape[0]),
        in_specs=[pl.BlockSpec((tm, k), lambda i, j, ids: (i, 0)),
                  pl.BlockSpec((1, k, n_loc), lambda i, j, ids: (ids[j], 0, 0)),
                  pl.BlockSpec((1, n_loc), lambda i, j, ids: (0, ids[j]))]
        + [pl.BlockSpec(memory_space=pl.ANY)] * (prev is not None),
        out_specs=pl.BlockSpec((tm, n_loc), lambda i, j, ids: (i, ids[j])))
    return pl.pallas_call(
        body, name=name, grid_spec=grid_spec, out_shape=jax.ShapeDtypeStruct((m, n_sh * n_loc), F32),
        input_output_aliases={4: 0} if prev is not None else {}, compiler_params=_params("parallel", "arbitrary"),
    )(shard_ids, a, w, bias, *([prev] if prev is not None else []))


ROW_TILE = 512


def _rowcall(fn, rows, fulls, row_outs, acc_outs=(), *, n_rows, tm, name, after=None):
    n_r, n_f, n_o, n_a = len(rows), len(fulls), len(row_outs), len(acc_outs)
    n_in = n_r + n_f + (after is not None)
    assert n_rows % tm == 0, (name, n_rows, tm)

    def body(*refs):
        res = fn(*[r[...] for r in refs[:n_r + n_f]])
        res = tuple(res) if isinstance(res, (tuple, list)) else (res,)
        o_refs = refs[n_in:n_in + n_o]
        a_refs = refs[n_in + n_o:]
        for o_ref, val in zip(o_refs, res[:n_o]):
            o_ref[...] = val.astype(o_ref.dtype)
        if n_a:
            @pl.when(pl.program_id(0) == 0)
            def _():
                for a_ref in a_refs:
                    a_ref[...] = jnp.zeros_like(a_ref)

            for a_ref, val in zip(a_refs, res[n_o:]):
                a_ref[...] += val

    in_specs = [pl.BlockSpec((tm, w), functools.partial(lambda i, cb: (i, cb), cb=cb)) for _, w, cb in rows]
    in_specs += [pl.BlockSpec(f.shape, functools.partial(lambda i, nd: (0,) * nd, nd=f.ndim)) for f in fulls]
    in_specs += [pl.BlockSpec(memory_space=pl.ANY)] * (after is not None)
    out_specs = [pl.BlockSpec((tm, w), lambda i: (i, 0)) for w, _ in row_outs]
    out_specs += [pl.BlockSpec((1, w), lambda i: (0, 0)) for w in acc_outs]
    out_shape = [jax.ShapeDtypeStruct((n_rows, w), dt) for w, dt in row_outs]
    out_shape += [jax.ShapeDtypeStruct((1, w), F32) for w in acc_outs]
    return pl.pallas_call(
        body, name=name, grid=(n_rows // tm,), in_specs=in_specs, out_specs=out_specs, out_shape=out_shape,
        compiler_params=_params("arbitrary" if n_a else "parallel"),
    )(*[r[0] for r in rows], *fulls, *([after] if after is not None else []))


def _colsum(v):
    return jnp.sum(v, axis=0, keepdims=True)


def _layer_norm(xin, g, b):
    mu = jnp.mean(xin, axis=-1, keepdims=True)
    xc = xin - mu
    var = jnp.mean(xc * xc, axis=-1, keepdims=True)
    rstd = lax.rsqrt(var + LN_EPS)
    xh = xc * rstd
    return xh * g + b, xh, rstd


def _layer_norm_bwd(dy, xh, rstd, g):
    dyg = dy * g
    m1 = jnp.mean(dyg, axis=-1, keepdims=True)
    m2 = jnp.mean(dyg * xh, axis=-1, keepdims=True)
    dx = rstd * (dyg - m1 - xh * m2)
    return dx, _colsum(dy * xh), _colsum(dy), _colsum(dx)


def _ln_fwd(a, g, b, *, name):
    n_rows, d = a.shape

    def fn(av, gv, bv):
        y, xh, rstd = _layer_norm(av, gv, bv)
        return y, xh, rstd, y

    return _rowcall(fn, [(a, d, 0)], [g, b], [(d, F32), (d, F32), (1, F32), (d, MXU_DTYPE)], n_rows=n_rows, tm=ROW_TILE,
                    name=name)


def _ln_bwd(dya, dyb, xh, rstd, g, *, alpha, name, operand=True):
    n_rows, d = xh.shape

    def fn(da, db, xhv, rs, gv):
        dx, *sums = _layer_norm_bwd(alpha * da + db, xhv, rs, gv)
        return (dx,) + ((dx,) if operand else ()) + tuple(sums)

    rows = [(dya, d, 0), (dyb, d, 0), (xh, d, 0), (rstd, 1, 0)]
    return _rowcall(fn, rows, [g], [(d, F32)] + [(d, MXU_DTYPE)] * operand, [d, d, d], n_rows=n_rows, tm=ROW_TILE, name=name)


LN_EPILOGUE_ROWS = 1024


def _mm_ln_fwd(x, w, bias, a, g, b, *, alpha, name):
    d = a.shape[1]

    def fn(r, av, gv, bv):
        y, xh, rstd = _layer_norm(alpha * av + r, gv, bv)
        return y, xh, rstd, y

    return _mm(x, w, bias=bias, name=name, tm=LN_EPILOGUE_ROWS,
               epilogue=(fn, [a], [g, b], [(d, F32), (d, F32), (1, F32), (d, MXU_DTYPE)], []))


def _mm_ln_bwd(x, w, dya, xh, rstd, g, *, alpha, name):
    d = xh.shape[1]

    def fn(r, da, xhv, rs, gv):
        dx, *sums = _layer_norm_bwd(alpha * da + r, xhv, rs, gv)
        return (dx, dx, *sums)

    return _mm(x, w, tb=True, name=name, tm=LN_EPILOGUE_ROWS,
               epilogue=(fn, [dya, xh, rstd], [g], [(d, F32), (d, MXU_DTYPE)], [d, d, d]))


def _ln_loss_bwd(a, r, target, g, b, *, alpha, name):
    n_rows, d = a.shape

    def fn(av, rv, tv, gv, bv):
        y, xh, rs = _layer_norm(alpha * av + rv, gv, bv)
        diff = y - tv
        part = jnp.sum(jnp.sum(diff * diff, axis=1, keepdims=True), axis=0, keepdims=True) * (0.5 / d)
        dx, *sums = _layer_norm_bwd(diff * (1.0 / d), xh, rs, gv)
        return (dx, dx, *sums, jnp.broadcast_to(part, (1, LANES)))

    return _rowcall(fn, [(a, d, 0), (r, d, 0), (target, d, 0)], [g, b], [(d, F32), (d, MXU_DTYPE)], [d, d, d, LANES],
                    n_rows=n_rows, tm=ROW_TILE, name=name)


def _rope_lane_constants():
    lane = np.arange(ATT_MERGED)
    in_head = lane % ATT_HEAD_DIM
    sign = np.where(in_head < ROT_DIM // 2, -1.0, np.where(in_head < ROT_DIM, 1.0, 0.0)).astype(np.float32)
    inv_freq = ROPE_THETA ** (-jnp.arange(0, ROT_DIM, 2, dtype=F32) / ROT_DIM)
    return inv_freq[lane % (ROT_DIM // 2)].reshape(1, ATT_MERGED), jnp.asarray(sign).reshape(1, ATT_MERGED)


def _rope_tables(pos_col, *, name, after=None):
    inv_lane, sign = _rope_lane_constants()

    def fn(pos, inv, sg):
        ang = pos.astype(F32) * inv
        return jnp.where(sg != 0.0, jnp.cos(ang), 1.0), sg * jnp.sin(ang)

    return _rowcall(fn, [(pos_col, 1, 0)], [inv_lane, sign], [(ATT_MERGED, F32), (ATT_MERGED, F32)],
                    n_rows=pos_col.shape[0], tm=512, name=name, after=after)


def _rot_partner(t):
    lane = lax.broadcasted_iota(jnp.int32, t.shape, 1)
    width = t.shape[1]
    return jnp.where((lane & (ROT_DIM // 2)) == 0, pltpu.roll(t, width - ROT_DIM // 2, 1), pltpu.roll(t, ROT_DIM // 2, 1))


def _rope(t, cos_t, sin_t):
    return t * cos_t + _rot_partner(t) * sin_t


def _rope_transpose(dt, cos_t, sin_t):
    return dt * cos_t + _rot_partner(dt * sin_t)


def _strided_rows(r, count, stride):
    return pl.ds(r, count) if stride == 1 else pl.ds(r, count, stride=stride)


def _qkv_split(proj, cos_t, sin_t, *, name, tm=512):
    n_rows = proj.shape[0]
    n_g = len(DILATIONS)

    def body(*refs):
        n_src = LANE_HALVES * 3 * n_g
        src, tables, dst = refs[:n_src], refs[n_src:n_src + 2 * LANE_HALVES], refs[n_src + 2 * LANE_HALVES:]
        for kind in range(3):
            for g, dil in enumerate(DILATIONS):
                for half in range(LANE_HALVES):
                    x_ref, o_ref = src[(kind * n_g + g) * LANE_HALVES + half], dst[kind * n_g + g]
                    cos_ref, sin_ref = tables[half], tables[LANE_HALVES + half]
                    for r in range(dil):
                        rows = _strided_rows(r, tm // dil, dil)
                        t = x_ref[rows, :]
                        if kind < 2:
                            t = _rope(t, cos_ref[rows, :], sin_ref[rows, :])
                        lo = r * ATT_MERGED + half * LANES
                        o_ref[:, lo:lo + LANES] = t.astype(o_ref.dtype)

    half_spec = lambda cb: pl.BlockSpec((tm, LANES), functools.partial(lambda i, cb: (i, cb), cb=cb))
    in_specs = [half_spec((off + g) * LANE_HALVES + half)
                for off in (OFF_Q_BLK, OFF_K_BLK, OFF_V_BLK) for g in range(n_g) for half in range(LANE_HALVES)]
    in_specs += [half_spec(half) for _ in range(2) for half in range(LANE_HALVES)]
    out_specs = [pl.BlockSpec((tm // dil, dil * ATT_MERGED), lambda i: (i, 0)) for _ in range(3) for dil in DILATIONS]
    out_shape = [jax.ShapeDtypeStruct((n_rows // dil, dil * ATT_MERGED), MXU_DTYPE) for _ in range(3) for dil in DILATIONS]
    outs = pl.pallas_call(
        body, name=name, grid=(n_rows // tm,), in_specs=in_specs, out_specs=out_specs, out_shape=out_shape,
        compiler_params=_params("parallel"),
    )(*[proj] * (LANE_HALVES * 3 * n_g), *[cos_t] * LANE_HALVES, *[sin_t] * LANE_HALVES)
    return outs[:n_g], outs[n_g:2 * n_g], outs[2 * n_g:]


def _mix(gs, ga, z1, z2, b_att):
    return jax.nn.sigmoid(gs) * (z1 * jax.nn.sigmoid(z2)) + jax.nn.sigmoid(ga) * b_att


def _mix_rows(proj, z, b_att):
    return [(proj, D_MODEL, OFF_GS_BLK), (proj, D_MODEL, OFF_GA_BLK), (z, D_MODEL, 0), (z, D_MODEL, 1), (b_att, D_MODEL, 0)]


def _mix_out_ln(proj, z, b_att, w, bias, a, g, b, *, alpha, name):
    def fn(gs, ga, z1, z2, ba, av, wv, biasv, gv, bv):
        mixed = _mix(gs, ga, z1, z2, ba)
        y, xh, rstd = _layer_norm(alpha * av + (_dot(mixed, wv, 1, 0) + biasv), gv, bv)
        return mixed, y, xh, rstd, y

    rows = _mix_rows(proj, z, b_att) + [(a, D_MODEL, 0)]
    outs = [(D_MODEL, MXU_DTYPE), (D_MODEL, F32), (D_MODEL, F32), (1, F32), (D_MODEL, MXU_DTYPE)]
    return _rowcall(fn, rows, [w, bias, g, b], outs, n_rows=proj.shape[0], tm=ROW_TILE // 2, name=name)


def _mix_bwd(dmixed, proj, z, b_att, *, name):
    def fn(dm, gs, ga, z1, z2, ba):
        _, vjp = jax.vjp(_mix, gs, ga, z1, z2, ba)
        dgs, dga, dz1, dz2, dba = vjp(dm)
        dz = jnp.concatenate([dz1, dz2], axis=1)
        return dgs, dga, dz, dba, _colsum(dgs), _colsum(dga), _colsum(dz)

    rows = [(dmixed, D_MODEL, 0)] + _mix_rows(proj, z, b_att)
    widths = [D_MODEL, D_MODEL, 2 * D_MODEL, D_MODEL]
    return _rowcall(fn, rows, [], [(w, MXU_DTYPE) for w in widths], widths[:3], n_rows=proj.shape[0], tm=ROW_TILE, name=name)


def _gelu_bwd(dgy, y, proj, *, name):
    def fn(dg, yv, u):
        _, vjp = jax.vjp(jax.nn.gelu, yv)
        dy = vjp(dg)[0]
        return dy, _colsum(dy * u)

    return _rowcall(fn, [(dgy, SSM_WIDTH, 0), (y, SSM_WIDTH, 0), (proj, SSM_WIDTH, 0)], [], [(SSM_WIDTH, F32)],
                    [SSM_WIDTH], n_rows=y.shape[0], tm=512, name=name)


HEAD_ROWS = ATT_HEADS_PER_GROUP * ATT_BLK


def _head_masks(rows):
    head = lax.broadcasted_iota(jnp.int32, (rows, ATT_MERGED), 1) >> (ATT_HEAD_DIM.bit_length() - 1)
    return [head == h for h in range(ATT_HEADS_PER_GROUP)]


def _stack_heads(t, masks):
    return jnp.concatenate([jnp.where(m, t, jnp.zeros_like(t)) for m in masks], axis=0)


def _unstack_heads(t4, masks):
    blocks = [t4[h * ATT_BLK:(h + 1) * ATT_BLK] for h in range(ATT_HEADS_PER_GROUP)]
    return jnp.where(masks[0], blocks[0], jnp.where(masks[1], blocks[1], jnp.where(masks[2], blocks[2], blocks[3])))


def _head_column(stats, first):
    return jnp.concatenate([stats[:, first + h:first + h + 1] for h in range(ATT_HEADS_PER_GROUP)], axis=0)


def _band_mask(first_key):
    qi = lax.broadcasted_iota(jnp.int32, (HEAD_ROWS, 2 * ATT_BLK), 0) & (ATT_BLK - 1)
    ki = lax.broadcasted_iota(jnp.int32, (HEAD_ROWS, 2 * ATT_BLK), 1)
    steps = qi + ATT_BLK - ki
    return (steps >= 0) & (steps <= ATT_BLK) & (ki >= first_key)


def _dil_fwd(q, k, v, dil, *, name):
    n_blk = q.shape[0] // ATT_BLK
    cur = pl.BlockSpec((ATT_BLK, ATT_MERGED), lambda r, n: (n, r))
    prev = pl.BlockSpec((ATT_BLK, ATT_MERGED), lambda r, n: (jnp.maximum(n - 1, 0), r))

    def body(q_ref, kp_ref, kc_ref, vp_ref, vc_ref, o_ref, l_ref):
        masks = _head_masks(ATT_BLK)
        valid = _band_mask(jnp.where(pl.program_id(1) > 0, 0, ATT_BLK))
        keys = jnp.concatenate([kp_ref[...], kc_ref[...]], axis=0)
        vals = jnp.concatenate([vp_ref[...], vc_ref[...]], axis=0)
        s = jnp.where(valid, _dot(_stack_heads(q_ref[...], masks), keys, 1, 1) * ATT_SCALE, NEG_INF)
        m = jnp.max(s, axis=-1, keepdims=True)
        p = jnp.exp(s - m)
        den = jnp.sum(p, axis=-1, keepdims=True)
        o_ref[...] = _unstack_heads(_dot(p, vals, 1, 0) / den, masks)
        l_ref[...] = _unstack_heads(jnp.broadcast_to(m + jnp.log(den), (HEAD_ROWS, ATT_MERGED)), masks)

    shape = jax.ShapeDtypeStruct(q.shape, F32)
    return pl.pallas_call(
        body, name=name, grid=(dil, n_blk), in_specs=[cur, prev, cur, prev, cur], out_specs=[cur, cur],
        out_shape=[shape, shape], compiler_params=_params("parallel", "parallel"),
    )(q, k, k, v, v)


def _att_merge(outs, lses, *, name, tm=512):
    n_g = len(outs)
    n_rows = outs[0].shape[0] * DILATIONS[0]

    def body(*refs):
        src, (att_ref, lse_ref), tmp = refs[:2 * n_g], refs[2 * n_g:2 * n_g + 2], refs[2 * n_g + 2:]
        vals = []
        for idx, src_ref in enumerate(src):
            dil = DILATIONS[idx % n_g]
            if dil == 1:
                vals.append(src_ref[...])
                continue
            for r in range(dil):
                for half in range(LANE_HALVES):
                    lo = r * ATT_MERGED + half * LANES
                    tmp[LANE_HALVES * idx + half][_strided_rows(r, tm // dil, dil), :] = src_ref[:, lo:lo + LANES]
            vals.append(jnp.concatenate([tmp[LANE_HALVES * idx + half][...] for half in range(LANE_HALVES)], axis=1))
        o, l = vals[:n_g], vals[n_g:]
        m = functools.reduce(jnp.maximum, l)
        e = [jnp.exp(li - m) for li in l]
        z = functools.reduce(jnp.add, e)
        att_ref[...] = functools.reduce(jnp.add, [(ei / z) * oi for ei, oi in zip(e, o)])
        lse_ref[...] = m + jnp.log(z)

    in_specs = [pl.BlockSpec((tm // dil, dil * ATT_MERGED), lambda i: (i, 0)) for _ in range(2) for dil in DILATIONS]
    row = pl.BlockSpec((tm, ATT_MERGED), lambda i: (i, 0))
    shape = jax.ShapeDtypeStruct((n_rows, ATT_MERGED), F32)
    return pl.pallas_call(
        body, name=name, grid=(n_rows // tm,), in_specs=in_specs, out_specs=[row, row], out_shape=[shape, shape],
        scratch_shapes=[pltpu.VMEM((tm, LANES), F32)] * (LANE_HALVES * 2 * n_g), compiler_params=_params("parallel"),
    )(*outs, *lses)


def _att_stats(datt, att, lse, *, name):
    n_rows = datt.shape[0]

    def fn(d, a, l):
        prod = d * a
        lane = lax.broadcasted_iota(jnp.int32, (d.shape[0], LANES), 1)
        out = jnp.zeros((d.shape[0], LANES), F32)
        for h in range(ATT_HEADS_PER_GROUP):
            lo = h * ATT_HEAD_DIM
            out = jnp.where(lane == h, l[:, lo:lo + 1], out)
            delta = jnp.sum(prod[:, lo:lo + ATT_HEAD_DIM], axis=-1, keepdims=True)
            out = jnp.where(lane == ATT_HEADS_PER_GROUP + h, delta, out)
        return out

    rows = [(t, ATT_MERGED, 0) for t in (datt, att, lse)]
    return _rowcall(fn, rows, [], [(LANES, F32)], n_rows=n_rows, tm=512, name=name)[0]


def _dil_bwd(q, k, v, datt, stats, dil, *, name):
    n_rows = datt.shape[0]
    n_blk = n_rows // dil // ATT_BLK
    span = ATT_BLK * dil
    cur = pl.BlockSpec((ATT_BLK, ATT_MERGED), lambda n, r: (n, r))
    prev = pl.BlockSpec((ATT_BLK, ATT_MERGED), lambda n, r: (jnp.maximum(n - 1, 0), r))
    nxt = pl.BlockSpec((ATT_BLK, ATT_MERGED), lambda n, r: (jnp.minimum(n + 1, n_blk - 1), r))
    seq = lambda half, ahead: pl.BlockSpec((span, LANES), lambda n, r: (jnp.minimum(n + ahead, n_blk - 1), half))

    def body(qc_ref, qn_ref, kp_ref, kc_ref, vp_ref, vc_ref, dc0_ref, dc1_ref, dn0_ref, dn1_ref, sc_ref, sn_ref,
             dq0_ref, dq1_ref, dk0_ref, dk1_ref, dv0_ref, dv1_ref):
        n = pl.program_id(0)
        rows = slice(None) if dil == 1 else _strided_rows(pl.program_id(1), ATT_BLK, dil)

        def read(ref0, ref1):
            return jnp.concatenate([ref0[rows, :], ref1[rows, :]], axis=1)

        def write(ref0, ref1, val):
            ref0[rows, :] = val[:, :LANES]
            ref1[rows, :] = val[:, LANES:]

        masks = _head_masks(ATT_BLK)
        valid = _band_mask(jnp.where(n > 0, 0, ATT_BLK))
        qi = lax.broadcasted_iota(jnp.int32, (HEAD_ROWS, ATT_BLK), 0) & (ATT_BLK - 1)
        ki = lax.broadcasted_iota(jnp.int32, (HEAD_ROWS, ATT_BLK), 1)
        valid_next = (ki - qi) >= jnp.where(n < n_blk - 1, 0, ATT_BLK)

        kc, vc = kc_ref[...], vc_ref[...]
        keys = jnp.concatenate([kp_ref[...], kc], axis=0)
        vals = jnp.concatenate([vp_ref[...], vc], axis=0)
        q4 = _stack_heads(qc_ref[...], masks)
        d4 = _stack_heads(read(dc0_ref, dc1_ref).astype(MXU_DTYPE), masks)
        st = sc_ref[rows, :]
        p = jnp.where(valid, jnp.exp(_dot(q4, keys, 1, 1) * ATT_SCALE - _head_column(st, 0)), 0.0)
        ds = p * (_dot(d4, vals, 1, 1) - _head_column(st, ATT_HEADS_PER_GROUP)) * ATT_SCALE
        write(dq0_ref, dq1_ref, _unstack_heads(_dot(ds, keys, 1, 0), masks))

        q4n = _stack_heads(qn_ref[...], masks)
        d4n = _stack_heads(read(dn0_ref, dn1_ref).astype(MXU_DTYPE), masks)
        stn = sn_ref[rows, :]
        p_n = jnp.where(valid_next, jnp.exp(_dot(q4n, kc, 1, 1) * ATT_SCALE - _head_column(stn, 0)), 0.0)
        ds_n = p_n * (_dot(d4n, vc, 1, 1) - _head_column(stn, ATT_HEADS_PER_GROUP)) * ATT_SCALE
        write(dv0_ref, dv1_ref, _dot(p[:, ATT_BLK:], d4, 0, 0) + _dot(p_n, d4n, 0, 0))
        write(dk0_ref, dk1_ref, _dot(ds[:, ATT_BLK:], q4, 0, 0) + _dot(ds_n, q4n, 0, 0))

    shape = jax.ShapeDtypeStruct((n_rows, LANES), F32)
    out = seq(0, 0)
    res = pl.pallas_call(
        body, name=name, grid=(n_blk, dil),
        in_specs=[cur, nxt, prev, cur, prev, cur, seq(0, 0), seq(1, 0), seq(0, 1), seq(1, 1), seq(0, 0), seq(0, 1)],
        out_specs=[out] * 6, out_shape=[shape] * 6, compiler_params=_params("parallel", "arbitrary"),
    )(q, q, k, k, v, v, datt, datt, datt, datt, stats, stats)
    return [(res[2 * i], res[2 * i + 1]) for i in range(3)]


def _dproj_assemble(du, dqkv, dgs, dga, cos_t, sin_t, *, name):
    n_g = len(DILATIONS)

    def fn(*t):
        n_half = LANE_HALVES * 3 * n_g
        du_t, halves, (dgs_t, dga_t, c, s) = t[0], t[1:1 + n_half], t[1 + n_half:]
        parts = [jnp.concatenate(halves[LANE_HALVES * i:LANE_HALVES * (i + 1)], axis=1) for i in range(3 * n_g)]
        for i in range(2 * n_g):
            parts[i] = _rope_transpose(parts[i], c, s)
        cast = [p.astype(MXU_DTYPE) for p in parts]
        return [jnp.concatenate([du_t] + cast + [dgs_t, dga_t], axis=1)] + [_colsum(p) for p in parts]

    rows = [(du, SSM_WIDTH, 0)]
    rows += [(half, LANES, 0) for i in range(3) for g in range(n_g) for half in dqkv[g][i]]
    rows += [(dgs, D_MODEL, 0), (dga, D_MODEL, 0), (cos_t, ATT_MERGED, 0), (sin_t, ATT_MERGED, 0)]
    width = SSM_WIDTH + 3 * n_g * ATT_MERGED + 2 * D_MODEL
    res = _rowcall(fn, rows, [], [(width, MXU_DTYPE)], [ATT_MERGED] * (3 * n_g), n_rows=du.shape[0], tm=ROW_TILE, name=name)
    return res[0], res[1:]


def _xhead(h):
    return slice(h * XATT_HEAD_DIM, (h + 1) * XATT_HEAD_DIM)


def _xatt_probs(qh, kh):
    s = _dot(qh, kh, 1, 1) * XATT_SCALE
    e = jnp.exp(s - jnp.max(s, axis=-1, keepdims=True))
    return e / jnp.sum(e, axis=-1, keepdims=True)


def _xatt_fwd(q, kv, *, name, tm=512):
    n_rows = q.shape[0]
    n_mem = kv.shape[0]

    def body(q_ref, kv_ref, o_ref):
        for h in range(XATT_HEADS):
            sl = _xhead(h)
            p = _xatt_probs(q_ref[:, sl], kv_ref[:, sl])
            o_ref[:, sl] = _dot(p, kv_ref[:, D_MODEL + h * XATT_HEAD_DIM:D_MODEL + (h + 1) * XATT_HEAD_DIM], 1, 0
                                ).astype(o_ref.dtype)

    row = pl.BlockSpec((tm, D_MODEL), lambda i: (i, 0))
    return pl.pallas_call(
        body, name=name, grid=(n_rows // tm,),
        in_specs=[row, pl.BlockSpec((n_mem, 2 * D_MODEL), lambda i: (0, 0))], out_specs=row,
        out_shape=jax.ShapeDtypeStruct((n_rows, D_MODEL), MXU_DTYPE), compiler_params=_params("parallel"),
    )(q, kv)


def _xatt_bwd(q, kv, do, *, name, tm=512):
    n_rows = q.shape[0]
    n_mem = kv.shape[0]

    def body(q_ref, kv_ref, do_ref, dq_ref, dkv_ref):
        @pl.when(pl.program_id(0) == 0)
        def _():
            dkv_ref[...] = jnp.zeros_like(dkv_ref)

        for h in range(XATT_HEADS):
            sl = _xhead(h)
            vsl = slice(D_MODEL + h * XATT_HEAD_DIM, D_MODEL + (h + 1) * XATT_HEAD_DIM)
            qh, kh, doh = q_ref[:, sl], kv_ref[:, sl], do_ref[:, sl]
            p = _xatt_probs(qh, kh)
            dp = _dot(doh, kv_ref[:, vsl], 1, 1)
            ds = p * (dp - jnp.sum(dp * p, axis=-1, keepdims=True)) * XATT_SCALE
            dq_ref[:, sl] = _dot(ds, kh, 1, 0).astype(dq_ref.dtype)
            dkv_ref[:, sl] += _dot(ds, qh, 0, 0)
            dkv_ref[:, vsl] += _dot(p, doh, 0, 0)

    row = pl.BlockSpec((tm, D_MODEL), lambda i: (i, 0))
    full = pl.BlockSpec((n_mem, 2 * D_MODEL), lambda i: (0, 0))
    return pl.pallas_call(
        body, name=name, grid=(n_rows // tm,), in_specs=[row, full, row], out_specs=[row, full],
        out_shape=[jax.ShapeDtypeStruct((n_rows, D_MODEL), MXU_DTYPE), jax.ShapeDtypeStruct((n_mem, 2 * D_MODEL), F32)],
        compiler_params=_params("arbitrary"),
    )(q, kv, do)


def _disc(logdt, a_re, a_im, b_re, b_im):
    dt = jnp.exp(logdt)
    mag = jnp.exp(a_re * dt)
    ab_re = mag * jnp.cos(a_im * dt)
    ab_im = mag * jnp.sin(a_im * dt)
    den = jnp.square(a_re) + jnp.square(a_im)
    nr = ab_re - 1.0
    f_re = (nr * a_re + ab_im * a_im) / den
    f_im = (ab_im * a_re - nr * a_im) / den
    bb_re = f_re[None] * b_re - f_im[None] * b_im
    bb_im = f_re[None] * b_im + f_im[None] * b_re
    return ab_re, ab_im, bb_re, bb_im


def _disc_transpose(logdt, a_re, a_im, b_re, b_im, g_ab_re, g_ab_im, g_bb_re, g_bb_im):
    dt = jnp.exp(logdt)
    mag = jnp.exp(a_re * dt)
    th = a_im * dt
    cs, sn = jnp.cos(th), jnp.sin(th)
    ab_re, ab_im = mag * cs, mag * sn
    den = jnp.square(a_re) + jnp.square(a_im)
    nr = ab_re - 1.0
    f_re = (nr * a_re + ab_im * a_im) / den
    f_im = (ab_im * a_re - nr * a_im) / den
    d_f_re = jnp.sum(g_bb_re * b_re + g_bb_im * b_im, axis=0)
    d_f_im = jnp.sum(g_bb_im * b_re - g_bb_re * b_im, axis=0)
    d_b_re = g_bb_re * f_re[None] + g_bb_im * f_im[None]
    d_b_im = g_bb_im * f_re[None] - g_bb_re * f_im[None]
    d_n_re, d_n_im = d_f_re / den, d_f_im / den
    d_den = -(d_f_re * f_re + d_f_im * f_im) / den
    d_ab_re = g_ab_re + d_n_re * a_re - d_n_im * a_im
    d_ab_im = g_ab_im + d_n_re * a_im + d_n_im * a_re
    d_a_re = d_n_re * nr + d_n_im * ab_im + 2.0 * d_den * a_re
    d_a_im = d_n_re * ab_im - d_n_im * nr + 2.0 * d_den * a_im
    d_mag = d_ab_re * cs + d_ab_im * sn
    d_th = mag * (d_ab_im * cs - d_ab_re * sn)
    d_a_re = d_a_re + d_mag * mag * dt
    d_a_im = d_a_im + d_th * dt
    d_dt = jnp.sum(d_mag * mag * a_re + d_th * a_im, axis=-1, keepdims=True)
    return d_dt * dt, d_a_re, d_a_im, d_b_re, d_b_im


def _full_spec(shape):
    return pl.BlockSpec(tuple(shape), functools.partial(lambda i, nd: (0,) * nd, nd=len(shape)))


def _whole(fn, args, out_shapes, *, name):
    n_in = len(args)

    def body(*refs):
        res = fn(*[r[...] for r in refs[:n_in]])
        for o_ref, val in zip(refs[n_in:], res):
            o_ref[...] = val

    return pl.pallas_call(
        body, name=name, grid=(1,), in_specs=[_full_spec(t.shape) for t in args],
        out_specs=[_full_spec(s) for s in out_shapes], out_shape=[jax.ShapeDtypeStruct(s, F32) for s in out_shapes],
        compiler_params=_params("arbitrary"))(*args)


SSM_WIDE = GROUPS_PER_TILE * SSM_STATE
LANE_GROUPS_PER_TILE = SSM_WIDE // LANES


def _chan(j):
    return slice(j * LANES, (j + 1) * LANES)


def _time_major_rows(j, q, tc):
    return pl.ds(j * LANE_GROUPS_PER_TILE + q, tc, stride=STATE_VREG_ROWS)


def _to_time_major(x, t_re_ref, t_im_ref, dst_re, dst_im, tc):
    for j in range(SSM_TILES):
        xj = x[:, _chan(j)]
        for t_ref, dst in ((t_re_ref, dst_re), (t_im_ref, dst_im)):
            r = _dot(xj, t_ref[j], 1, 0)
            for q in range(LANE_GROUPS_PER_TILE):
                dst[_time_major_rows(j, q, tc), :] = r[:, q * LANES:(q + 1) * LANES]


def _from_time_major(src, j, tc):
    return jnp.concatenate([src[_time_major_rows(j, q, tc), :] for q in range(LANE_GROUPS_PER_TILE)], axis=1)


def _scan_chunk(w_re, w_im, h_re, h_im, a_re, a_im, start, tc):
    def step(t, carry):
        hr, hi = carry
        rows = _scan_rows(t)
        nr = a_re * hr - a_im * hi + w_re[rows, :]
        ni = a_re * hi + a_im * hr + w_im[rows, :]
        h_re[rows, :] = nr
        h_im[rows, :] = ni
        return nr, ni

    return lax.fori_loop(0, tc, step, start, unroll=8)


SSM_CHUNK = 256


def _tile_spec(stack, k):
    return pl.BlockSpec((pl.Squeezed(),) + tuple(stack.shape[1:]), lambda i: (k, 0, 0, 0))


def _expand_block_diagonal(src_ref, dst):
    dst[...] = jnp.zeros_like(dst)
    r, c = src_ref.shape[1:]
    for g in range(SSM_GROUPS):
        j, gl = divmod(g, GROUPS_PER_TILE)
        dst[j, gl * r:(gl + 1) * r, gl * c:(gl + 1) * c] = src_ref[g].astype(dst.dtype)


def _extract_block_diagonal(src, dst_ref):
    r, c = dst_ref.shape[1:]
    for g in range(SSM_GROUPS):
        j, gl = divmod(g, GROUPS_PER_TILE)
        dst_ref[g] = src[j, gl * r:(gl + 1) * r, gl * c:(gl + 1) * c]


def _ssm_fwd(proj, blocks_cn, blocks_nc, a_re, a_im, gain, *, name, tc=SSM_CHUNK):
    n_rows = proj.shape[0]
    n_chunk = n_rows // tc

    def body(u_ref, br_ref, bi_ref, cr_ref, ci_ref, ar_ref, ai_ref, g_ref, y_ref, gy_ref, hr, hi, wr, wi, state,
             tbr_ref, tbi_ref, tcr_ref, tci_ref):
        @pl.when(pl.program_id(0) == 0)
        def _():
            state[...] = jnp.zeros_like(state)
            for src_ref, dst in ((br_ref, tbr_ref), (bi_ref, tbi_ref), (cr_ref, tcr_ref), (ci_ref, tci_ref)):
                _expand_block_diagonal(src_ref, dst)

        u = u_ref[...]
        _to_time_major(u, tbr_ref, tbi_ref, wr, wi, tc)
        state[0], state[1] = _scan_chunk(wr, wi, hr, hi, ar_ref[...], ai_ref[...], (state[0], state[1]), tc)
        for j in range(SSM_TILES):
            yj = (_dot(_from_time_major(hr, j, tc), tcr_ref[j], 1, 0) + _dot(_from_time_major(hi, j, tc), tci_ref[j], 1, 0)
                  + g_ref[:, _chan(j)] * u[:, _chan(j)])
            y_ref[:, _chan(j)] = yj
            gy_ref[:, _chan(j)] = jax.nn.gelu(yj).astype(gy_ref.dtype)

    rows = pl.BlockSpec((tc, SSM_WIDTH), lambda i: (i, 0))
    coef = pl.BlockSpec((STATE_VREG_ROWS, LANES), lambda i: (0, 0))
    states = pl.BlockSpec((tc * STATE_VREG_ROWS, LANES), lambda i: (i, 0))
    sshape = jax.ShapeDtypeStruct((n_rows * STATE_VREG_ROWS, LANES), F32)
    return pl.pallas_call(
        body, name=name, grid=(n_chunk,),
        in_specs=[rows, _tile_spec(blocks_cn, 0), _tile_spec(blocks_cn, 1), _tile_spec(blocks_nc, 0),
                  _tile_spec(blocks_nc, 1), coef, coef, pl.BlockSpec((1, SSM_WIDTH), lambda i: (0, 0))],
        out_specs=[rows, rows, states, states],
        out_shape=[jax.ShapeDtypeStruct((n_rows, SSM_WIDTH), F32), jax.ShapeDtypeStruct((n_rows, SSM_WIDTH), MXU_DTYPE),
                   sshape, sshape],
        scratch_shapes=[pltpu.VMEM((tc * STATE_VREG_ROWS, LANES), F32)] * 2 + [pltpu.VMEM((2, STATE_VREG_ROWS, LANES), F32)]
        + [pltpu.VMEM((SSM_TILES, LANES, SSM_WIDE), MXU_DTYPE)] * 2 + [pltpu.VMEM((SSM_TILES, SSM_WIDE, LANES), MXU_DTYPE)] * 2,
        compiler_params=_params("arbitrary"),
    )(proj, blocks_cn, blocks_cn, blocks_nc, blocks_nc, a_re, a_im, gain)


def _ssm_bwd(proj, dy, h_re, h_im, blocks_cn, blocks_nc, a_re, a_im, gain, *, name, tc=SSM_CHUNK):
    n_rows = proj.shape[0]
    n_chunk = n_rows // tc

    def body(u_ref, dy_ref, hr, hi, cr_ref, ci_ref, br_ref, bi_ref, ar_ref, ai_ref, g_ref,
             du_ref, su_ref, dc_re_ref, dc_im_ref, db_re_ref, db_im_ref, dar_ref, dai_ref, wr, wi, carry,
             tdr_ref, tdi_ref, tur_ref, tui_ref, dcr_ref, dci_ref, dbr_ref, dbi_ref):
        @pl.when(pl.program_id(0) == 0)
        def _():
            carry[...] = jnp.zeros_like(carry)
            for acc_ref in (su_ref, dcr_ref, dci_ref, dbr_ref, dbi_ref):
                acc_ref[...] = jnp.zeros_like(acc_ref)
            for src_ref, dst in ((cr_ref, tdr_ref), (ci_ref, tdi_ref), (br_ref, tur_ref), (bi_ref, tui_ref)):
                _expand_block_diagonal(src_ref, dst)

        a_r, a_i = ar_ref[...], ai_ref[...]
        u, dyv = u_ref[...], dy_ref[...]
        _to_time_major(dyv, tdr_ref, tdi_ref, wr, wi, tc)

        def step(kk, c):
            lam_r, lam_i, dar, dai = c
            rows = _scan_rows(tc - 1 - kk)
            h_r, h_i = hr[rows, :], hi[rows, :]
            dar = dar + lam_r * h_r + lam_i * h_i
            dai = dai + lam_i * h_r - lam_r * h_i
            new_r = wr[rows, :] + a_r * lam_r + a_i * lam_i
            new_i = wi[rows, :] + a_r * lam_i - a_i * lam_r
            wr[rows, :] = new_r
            wi[rows, :] = new_i
            return new_r, new_i, dar, dai

        carry[0], carry[1], carry[2], carry[3] = lax.fori_loop(0, tc, step, (carry[0], carry[1], carry[2], carry[3]),
                                                              unroll=8)
        dar_ref[...] = carry[2]
        dai_ref[...] = carry[3]
        for j in range(SSM_TILES):
            cj = _chan(j)
            lam_r, lam_i = _from_time_major(wr, j, tc), _from_time_major(wi, j, tc)
            dcr_ref[j] += _dot(dyv[:, cj], _from_time_major(hr, j, tc), 0, 0)
            dci_ref[j] += _dot(dyv[:, cj], _from_time_major(hi, j, tc), 0, 0)
            dbr_ref[j] += _dot(u[:, cj], lam_r, 0, 0)
            dbi_ref[j] += _dot(u[:, cj], lam_i, 0, 0)
            duj = _dot(lam_r, tur_ref[j], 1, 0) + _dot(lam_i, tui_ref[j], 1, 0) + g_ref[:, cj] * dyv[:, cj]
            du_ref[:, cj] = duj.astype(du_ref.dtype)
            su_ref[:, cj] += _colsum(duj)

        @pl.when(pl.program_id(0) == n_chunk - 1)
        def _():
            for src, dst_ref in ((dcr_ref, dc_re_ref), (dci_ref, dc_im_ref), (dbr_ref, db_re_ref), (dbi_ref, db_im_ref)):
                _extract_block_diagonal(src, dst_ref)

    back = lambda i: (n_chunk - 1 - i, 0)
    rows = pl.BlockSpec((tc, SSM_WIDTH), back)
    blocks = pl.BlockSpec((SSM_GROUPS, SSM_GROUP, SSM_STATE), lambda i: (0, 0, 0))
    coef = pl.BlockSpec((STATE_VREG_ROWS, LANES), lambda i: (0, 0))
    states = pl.BlockSpec((tc * STATE_VREG_ROWS, LANES), back)
    vec = pl.BlockSpec((1, SSM_WIDTH), lambda i: (0, 0))
    bshape = jax.ShapeDtypeStruct((SSM_GROUPS, SSM_GROUP, SSM_STATE), F32)
    cshape = jax.ShapeDtypeStruct((STATE_VREG_ROWS, LANES), F32)
    return pl.pallas_call(
        body, name=name, grid=(n_chunk,),
        in_specs=[rows, rows, states, states, _tile_spec(blocks_cn, 2), _tile_spec(blocks_cn, 3), _tile_spec(blocks_nc, 2),
                  _tile_spec(blocks_nc, 3), coef, coef, vec],
        out_specs=[rows, vec, blocks, blocks, blocks, blocks, coef, coef],
        out_shape=[jax.ShapeDtypeStruct((n_rows, SSM_WIDTH), MXU_DTYPE), jax.ShapeDtypeStruct((1, SSM_WIDTH), F32),
                   bshape, bshape, bshape, bshape, cshape, cshape],
        scratch_shapes=[pltpu.VMEM((tc * STATE_VREG_ROWS, LANES), F32)] * 2 + [pltpu.VMEM((4, STATE_VREG_ROWS, LANES), F32)]
        + [pltpu.VMEM((SSM_TILES, LANES, SSM_WIDE), MXU_DTYPE)] * 2 + [pltpu.VMEM((SSM_TILES, SSM_WIDE, LANES), MXU_DTYPE)] * 2
        + [pltpu.VMEM((SSM_TILES, LANES, SSM_WIDE), F32)] * 4,
        compiler_params=_params("arbitrary"),
    )(proj, dy, h_re, h_im, blocks_cn, blocks_cn, blocks_nc, blocks_nc, a_re, a_im, gain)


def _scan_rows(t):
    return pl.ds(pl.multiple_of(t * STATE_VREG_ROWS, 8), STATE_VREG_ROWS)


GATHER_GROUPS = (("w_glu", "w_att_up", "w_mix_out"), ("w_xq", "w_xkv", "w_xo", "w_ff1", "w_ff2"))
SCATTER_GROUPS = (("w_ff2", "w_ff1"), ("w_xo", "w_xq", "w_xkv", "w_mix_out"), ("w_att_up", "w_glu"), ("w_in",))


def _local_grads(x, mem, pos_col, target, sm, fetch_in, fetch, send, send_small, start_token):
    b_re_t = sm["ssm_b_re"].transpose(2, 0, 1)
    b_im_t = sm["ssm_b_im"].transpose(2, 0, 1)
    logdt = sm["ssm_log_dt"].reshape(SSM_GROUPS, 1)
    c_re, c_im = sm["ssm_c_re"], sm["ssm_c_im"]
    grp = (SSM_GROUPS, SSM_STATE)
    chn = (SSM_GROUP, SSM_GROUPS, SSM_STATE)

    wts = {}
    cos_t, sin_t = _rope_tables(pos_col, after=start_token, name="rope_tables")
    h0, xh0, rs0, h0m = _ln_fwd(x, sm["ln_in_g"], sm["ln_in_b"], name="ln_in_fwd")
    disc_in = (logdt, sm["ssm_a_re"], sm["ssm_a_im"], b_re_t, b_im_t)
    ab_re, ab_im, bb_re_t, bb_im_t = _whole(_disc, disc_in, [grp, grp, chn, chn], name="ssm_disc")
    a_re_rows, a_im_rows = ab_re.reshape(STATE_VREG_ROWS, LANES), ab_im.reshape(STATE_VREG_ROWS, LANES)
    tiles_cn = jnp.stack([bb_re_t.transpose(1, 0, 2), bb_im_t.transpose(1, 0, 2), c_re, -c_im])
    tiles_nc = jnp.stack([c_re.transpose(0, 2, 1), -c_im.transpose(0, 2, 1), bb_re_t.transpose(1, 2, 0),
                          bb_im_t.transpose(1, 2, 0)])
    w_in_near, near_ids = fetch_in(0, [h0m, tiles_cn, tiles_nc])
    proj = _mm_shards(h0m, w_in_near, sm["b_in"], near_ids, name="in_proj_near")
    wts["w_in"], far_ids = fetch_in(1, [proj])
    proj = _mm_shards(h0m, wts["w_in"], sm["b_in"], far_ids, prev=proj, name="in_proj_far")

    y, gy, h_re, h_im = _ssm_fwd(proj, tiles_cn, tiles_nc, a_re_rows, a_im_rows, sm["ssm_d"], name="ssm_fwd")

    q, k, v = _qkv_split(proj, cos_t, sin_t, name="qkv_split")
    outs, lses = [], []
    for g, dil in enumerate(DILATIONS):
        o_g, l_g = _dil_fwd(q[g], k[g], v[g], dil, name=f"dil_att_fwd_{dil}")
        outs.append(o_g)
        lses.append(l_g)
    att, lse = _att_merge(outs, lses, name="att_merge")
    wts.update(fetch(0, [att]))
    z = _mm(gy, wts["w_glu"], bias=sm["b_glu"], b_shards=True, name="glu_proj")
    b_att = _mm(att, wts["w_att_up"], b_shards=True, name="att_up")

    mixed, h1, xh1, rs1, h1m = _mix_out_ln(proj, z, b_att, wts["w_mix_out"], sm["b_mix_out"], h0, sm["ln1_g"],
                                           sm["ln1_b"], alpha=DEEPNORM_ALPHA, name="gate_mix_out_ln1")

    wts.update(fetch(1, [h1m]))
    xq = _mm(h1m, wts["w_xq"], out_dtype=MXU_DTYPE, name="xatt_q")
    kv = _mm(mem, wts["w_xkv"], out_dtype=MXU_DTYPE, b_shards=True, name="xatt_kv")
    xo_in = _xatt_fwd(xq, kv, name="xatt_fwd")
    h2, xh2, rs2, h2m = _mm_ln_fwd(xo_in, wts["w_xo"], None, h1, sm["ln2_g"], sm["ln2_b"], alpha=DEEPNORM_ALPHA,
                                   name="xatt_o_ln2")

    pre, act = _mm(h2m, wts["w_ff1"], bias=sm["b_ff1"], b_shards=True, name="ff1",
                   also=(lambda r: jnp.square(jnp.maximum(r, 0.0)), MXU_DTYPE))
    ff = _mm(act, wts["w_ff2"], bias=sm["b_ff2"], name="ff2")

    gw, gs = {}, {}
    dr3, dr3m, gs["ln3_g"], gs["ln3_b"], gs["b_ff2"], loss_row = _ln_loss_bwd(
        h2, ff, target, sm["ln3_g"], sm["ln3_b"], alpha=DEEPNORM_ALPHA, name="ln3_loss")
    wgrad = functools.partial(_mm, ta=True, out_dtype=WIRE_DTYPE, tk=2048)
    wgrad_cols = functools.partial(wgrad, out_shards=True, tk=x.shape[0])
    gw["w_ff2"] = wgrad(act, dr3m, tk=1024, name="ff2_dw")
    dpre, gs["b_ff1"] = _mm(dr3m, wts["w_ff2"], tb=True, out_dtype=MXU_DTYPE, colsum=True, name="ff2_dx",
                            gate=(pre, lambda p: 2.0 * jnp.maximum(p, 0.0)))
    gw["w_ff1"] = wgrad_cols(h2m, dpre, name="ff1_dw")
    sent = send(0, gw)
    dh2 = _mm(dpre, wts["w_ff1"], tb=True, b_shards=True, after=sent, name="ff1_dx")
    dr2, dr2m, gs["ln2_g"], gs["ln2_b"], _ = _ln_bwd(dr3, dh2, xh2, rs2, sm["ln2_g"], alpha=DEEPNORM_ALPHA,
                                                     name="ln2_bwd")
    gw["w_xo"] = wgrad(xo_in, dr2m, name="xatt_o_dw")
    dxo_in = _mm(dr2m, wts["w_xo"], tb=True, out_dtype=MXU_DTYPE, name="xatt_o_dx")
    dxq, dkv = _xatt_bwd(xq, kv, dxo_in, name="xatt_bwd")
    gw["w_xq"] = wgrad(h1m, dxq, name="xatt_q_dw")
    gw["w_xkv"] = wgrad_cols(mem, dkv, name="xatt_kv_dw")
    dr1, dr1m, gs["ln1_g"], gs["ln1_b"], gs["b_mix_out"] = _mm_ln_bwd(
        dxq, wts["w_xq"], dr2, xh1, rs1, sm["ln1_g"], alpha=DEEPNORM_ALPHA, name="xatt_q_dx_ln1")
    gw["w_mix_out"] = wgrad(mixed, dr1m, name="mix_out_dw")
    sent = send(1, gw)
    dmixed = _mm(dr1m, wts["w_mix_out"], tb=True, after=sent, name="mix_out_dx")
    dgs, dga, dz, db_att, s_gs, s_ga, gs["b_glu"] = _mix_bwd(dmixed, proj, z, b_att, name="gate_mix_bwd")

    gw["w_att_up"] = wgrad_cols(att, db_att, name="att_up_dw")
    gw["w_glu"] = wgrad_cols(gy, dz, name="glu_dw")
    sent = send(2, gw)
    datt = _mm(db_att, wts["w_att_up"], tb=True, b_shards=True, after=sent, name="att_up_dx")
    stats = _att_stats(datt, att, lse, name="att_stats")
    dqkv = [_dil_bwd(q[g], k[g], v[g], datt, stats, dil, name=f"dil_att_bwd_{dil}") for g, dil in enumerate(DILATIONS)]

    dgy = _mm(dz, wts["w_glu"], tb=True, b_shards=True, name="glu_dx")
    dy, gs["ssm_d"] = _gelu_bwd(dgy, y, proj, name="gelu_bwd")
    du, s_u, dc_re_t, dc_im_t, dbb_re_t, dbb_im_t, da_re, da_im = _ssm_bwd(
        proj, dy, h_re, h_im, tiles_cn, tiles_nc, a_re_rows, a_im_rows, sm["ssm_d"], name="ssm_bwd")
    gs["ssm_c_re"], gs["ssm_c_im"] = dc_re_t, -dc_im_t
    disc_ct = (da_re.reshape(grp), da_im.reshape(grp), dbb_re_t.transpose(1, 0, 2), dbb_im_t.transpose(1, 0, 2))
    d_logdt, gs["ssm_a_re"], gs["ssm_a_im"], d_b_re_t, d_b_im_t = _whole(
        _disc_transpose, disc_in + disc_ct, [(SSM_GROUPS, 1), grp, grp, chn, chn], name="ssm_disc_bwd")
    gs["ssm_log_dt"] = d_logdt
    gs["ssm_b_re"], gs["ssm_b_im"] = d_b_re_t.transpose(1, 2, 0), d_b_im_t.transpose(1, 2, 0)

    dproj, s_qkv = _dproj_assemble(du, dqkv, dgs, dga, cos_t, sin_t, name="dproj_assemble")
    gs["b_in"] = jnp.concatenate([s_u, *s_qkv, s_gs, s_ga], axis=1)
    sent = send_small(gs, SMALL_EARLY)
    gw["w_in"] = wgrad_cols(h0m, dproj, out_shards=2, after=sent, name="in_proj_dw")
    sent = send(3, gw)
    dh0 = _mm(dproj, wts["w_in"], tb=True, b_shards=True, after=sent, name="in_proj_dx")
    grad_x, gs["ln_in_g"], gs["ln_in_b"], _ = _ln_bwd(dr1, dh0, xh0, rs0, sm["ln_in_g"], alpha=DEEPNORM_ALPHA,
                                                      operand=False, name="ln_in_bwd")
    return loss_row, grad_x, gs


_IN_HBM = pl.BlockSpec(memory_space=pltpu.HBM)
_IN_SEMAPHORE = pl.BlockSpec(memory_space=pltpu.SEMAPHORE)


def _device_index():
    return 4 * lax.axis_index("x") + 2 * lax.axis_index("y") + lax.axis_index("c")


ALL_PEERS = tuple(range(1, N_DEV))
NEAR_PEERS = (1, 2, 3, 4, 5)
FAR_PEERS = (6, 7)


def _peer_index(kk):
    x, y, c = lax.axis_index("x"), lax.axis_index("y"), lax.axis_index("c")
    return 4 * ((x + (kk >> 2)) % 2) + 2 * ((y + ((kk >> 1) & 1)) % 2) + (c + (kk & 1)) % 2


def _exchange_copies(src_refs, land_refs, send_sems, recv_sems, scatter, peers):
    x, y, c = lax.axis_index("x"), lax.axis_index("y"), lax.axis_index("c")
    me = 4 * x + 2 * y + c
    pairs = []
    for a, (src_ref, land_ref) in enumerate(zip(src_refs, land_refs)):
        for idx, kk in enumerate(peers):
            px = (x + (kk >> 2)) % 2
            py = (y + ((kk >> 1) & 1)) % 2
            pc = (c + (kk & 1)) % 2
            peer = 4 * px + 2 * py + pc
            sem = a * len(peers) + idx
            src = src_ref.at[peer] if scatter else src_ref

            def copy(dst, src=src, sem=sem, px=px, py=py, pc=pc):
                return pltpu.make_async_remote_copy(
                    src_ref=src, dst_ref=dst, send_sem=send_sems.at[sem], recv_sem=recv_sems.at[sem],
                    device_id=(px, py, pc), device_id_type=pl.DeviceIdType.MESH)

            pairs.append((functools.partial(copy, land_ref.at[me]), functools.partial(copy, land_ref.at[peer])))
    return pairs


def _own_copies(src_refs, land_refs, own_sems, scatter):
    me = _device_index()
    return [functools.partial(pltpu.make_async_copy, src_ref.at[me] if scatter else src_ref, land_ref.at[me],
                              own_sems.at[a]) for a, (src_ref, land_ref) in enumerate(zip(src_refs, land_refs))]


def _exchange_start(srcs, *, scatter, name, after=None, peers=ALL_PEERS, lands=None):
    n_arr, n_sem = len(srcs), len(srcs) * len(peers)
    own = lands is None
    if own:
        lands = [lax.empty((N_DEV,) + tuple(s.shape[1:] if scatter else s.shape), s.dtype) for s in srcs]
    n_in = 2 * n_arr + (after is not None)

    def body(*refs):
        send_sems, recv_sems = refs[n_in], refs[n_in + 1]
        for sent, _ in _exchange_copies(refs[:n_arr], refs[n_arr:2 * n_arr], send_sems, recv_sems, scatter, peers):
            sent().start()
        if own:
            for local in _own_copies(refs[:n_arr], refs[n_arr:2 * n_arr], refs[n_in + 2], scatter):
                local().start()
        refs[-1][...] = jnp.zeros_like(refs[-1])

    sems = [pltpu.SemaphoreType.DMA((n_sem,)), pltpu.SemaphoreType.DMA((n_sem,))] + [pltpu.SemaphoreType.DMA((n_arr,))] * own
    through = [pltpu.HBM(t.shape, t.dtype) for t in (*srcs, *lands)]
    res = pl.pallas_call(
        body, name=name, out_shape=(*sems, *through, jax.ShapeDtypeStruct((8, LANES), F32)),
        in_specs=[_IN_HBM] * (2 * n_arr) + [pl.BlockSpec(memory_space=pl.ANY)] * (after is not None),
        out_specs=(*[_IN_SEMAPHORE] * len(sems), *[_IN_HBM] * (2 * n_arr), pl.BlockSpec(memory_space=pltpu.VMEM)),
        input_output_aliases={i: len(sems) + i for i in range(2 * n_arr)},
        compiler_params=pltpu.CompilerParams(has_side_effects=pltpu.SideEffectType.DATAFLOW_SIDE_EFFECTING),
    )(*[pltpu.with_memory_space_constraint(t, pltpu.HBM) for t in (*srcs, *lands)],
      *([after] if after is not None else []))
    first = len(sems)
    handle = dict(sems=res[:first], srcs=res[first:first + n_arr], lands=res[first + n_arr:first + 2 * n_arr],
                  scatter=scatter, peers=peers, own=own)
    return handle, res[-1]


def _exchange_wait(handle, *, after, name, srcs=None, lands=None):
    srcs = handle["srcs"] if srcs is None else srcs
    lands = handle["lands"] if lands is None else lands
    sems, scatter, peers, own = handle["sems"], handle["scatter"], handle["peers"], handle["own"]
    n_arr = len(srcs)
    after = list(after)

    def body(*refs):
        src_refs, land_refs = refs[:n_arr], refs[n_arr:2 * n_arr]
        for sent, received in _exchange_copies(src_refs, land_refs, refs[2 * n_arr], refs[2 * n_arr + 1], scatter, peers):
            sent().wait_send()
            received().wait_recv()
        if own:
            for local in _own_copies(src_refs, land_refs, refs[2 * n_arr + 2], scatter):
                local().wait()

    res = pl.pallas_call(
        body, name=name, out_shape=tuple(pltpu.HBM(t.shape, t.dtype) for t in (*srcs, *lands)),
        in_specs=[_IN_HBM] * (2 * n_arr) + [_IN_SEMAPHORE] * len(sems) + [pl.BlockSpec(memory_space=pl.ANY)] * len(after),
        out_specs=tuple([_IN_HBM] * (2 * n_arr)), input_output_aliases={i: i for i in range(2 * n_arr)},
        compiler_params=pltpu.CompilerParams(has_side_effects=pltpu.SideEffectType.DATAFLOW_SIDE_EFFECTING),
    )(*srcs, *lands, *sems, *after)
    return res[:n_arr], res[n_arr:]


def _adamw(g, w, m, v):
    m_new = ADAM_B1 * m + (1.0 - ADAM_B1) * g
    v_new = ADAM_B2 * v + (1.0 - ADAM_B2) * jnp.square(g)
    m_hat = m_new / (1.0 - ADAM_B1 ** ADAM_STEP)
    v_hat = v_new / (1.0 - ADAM_B2 ** ADAM_STEP)
    return g, -ADAM_LR * (m_hat / (jnp.sqrt(v_hat) + ADAM_EPS) + ADAM_WD * w), m_new, v_new


def _reduce_adamw(gstack, w, m, v, *, name, tr=128):
    n_rows, cols = w.shape
    tr = min(tr, n_rows)
    assert n_rows % tr == 0, (name, n_rows, tr)

    def body(g_ref, w_ref, m_ref, v_ref, *out_refs):
        g = g_ref[0].astype(F32)
        for dev in range(1, N_DEV):
            g = g + g_ref[dev].astype(F32)
        for o_ref, val in zip(out_refs, _adamw(g, w_ref[...], m_ref[...], v_ref[...])):
            o_ref[...] = val

    flat = pl.BlockSpec((tr, cols), lambda i: (i, 0))
    shape = jax.ShapeDtypeStruct((n_rows, cols), F32)
    return pl.pallas_call(
        body, name=name, grid=(n_rows // tr,),
        in_specs=[pl.BlockSpec((N_DEV, tr, cols), lambda i: (0, i, 0)), flat, flat, flat],
        out_specs=[flat] * 4, out_shape=[shape] * 4, compiler_params=_params("parallel"),
    )(gstack, w, m, v)


SMALL_FLAT_SSM = ("ssm_b_re", "ssm_b_im", "ssm_c_re", "ssm_c_im")


def _small_view(name, shape):
    size = int(np.prod(shape))
    if name in SMALL_FLAT_SSM:
        return SSM_GROUPS, size // SSM_GROUPS
    if name in ("ssm_a_re", "ssm_a_im"):
        return SSM_GROUPS, SSM_STATE
    return 1, size


def _pack_rows(view):
    return -(-(view[0] * view[1]) // PACK_COLS)


SMALL_LATE = ("ln_in_g", "ln_in_b")
SMALL_EARLY = tuple(n for n in SMALL if n not in SMALL_LATE)


def _pack_small(gs, names, views):
    parts = []
    for n in names:
        flat = gs[n].reshape(-1).astype(WIRE_DTYPE)
        parts.append(jnp.pad(flat, (0, _pack_rows(views[n]) * PACK_COLS - flat.shape[0])))
    total = sum(p.shape[0] for p in parts) // PACK_COLS
    parts.append(jnp.zeros(((-total % PACK_ROW_ALIGN) * PACK_COLS,), WIRE_DTYPE))
    return jnp.concatenate(parts).reshape(-1, PACK_COLS)


def _small_pieces(view):
    rows, cols = view
    if cols == PACK_COLS:
        return [(0, rows, 0, 0, 0, cols)]
    if rows == 1 and cols > PACK_COLS:
        return [(kk, 1, 0, 0, kk * PACK_COLS, PACK_COLS) for kk in range(cols // PACK_COLS)]
    if rows == 1:
        return [(0, 1, 0, 0, 0, cols)]
    return [((r * cols) // PACK_COLS, 1, (r * cols) % PACK_COLS, r, 0, cols) for r in range(rows)]


def _adamw_small(stacks, views, w, m, v, *, name):
    n = len(SMALL)
    place, first = {}, [0, 0]
    for k, names in enumerate((SMALL_EARLY, SMALL_LATE)):
        for name_ in names:
            place[name_] = (k, first[k])
            first[k] += _pack_rows(views[name_])

    def body(early_ref, late_ref, *refs):
        ins, outs = refs[:3 * n], refs[3 * n:]
        for i, name_ in enumerate(SMALL):
            stack_ref = (early_ref, late_ref)[place[name_][0]]
            row0 = place[name_][1]
            for prow, nrows, lane, orow, ocol, width in _small_pieces(views[name_]):
                src = (slice(row0 + prow, row0 + prow + nrows), slice(lane, lane + width))
                dst = (slice(orow, orow + nrows), slice(ocol, ocol + width))
                g = stack_ref[(0,) + src].astype(F32)
                for dev in range(1, N_DEV):
                    g = g + stack_ref[(dev,) + src].astype(F32)
                res = _adamw(g, ins[i][dst], ins[n + i][dst], ins[2 * n + i][dst])
                for kk, val in enumerate(res):
                    outs[kk * n + i][dst] = val

    args = [*stacks, *[d[name_] for d in (w, m, v) for name_ in SMALL]]
    out_views = [views[name_] for _ in range(4) for name_ in SMALL]
    res = pl.pallas_call(
        body, name=name, grid=(1,), in_specs=[_full_spec(t.shape) for t in args],
        out_specs=[_full_spec(s) for s in out_views], out_shape=[jax.ShapeDtypeStruct(s, F32) for s in out_views],
        compiler_params=_params("arbitrary"),
    )(*args)
    return [dict(zip(SMALL, res[kk * n:(kk + 1) * n])) for kk in range(4)]


def kernel(x, mem, positions, ln_in_g, ln_in_b, w_in, b_in, ssm_log_dt, ssm_a_re, ssm_a_im, ssm_b_re, ssm_b_im, ssm_c_re, ssm_c_im, ssm_d, w_glu, b_glu, w_att_up, w_mix_out, b_mix_out, ln1_g, ln1_b, w_xq, w_xkv, w_xo, ln2_g, ln2_b, w_ff1, b_ff1, w_ff2, b_ff2, ln3_g, ln3_b, loss_target, m_ln_in_g, m_ln_in_b, m_w_in, m_b_in, m_ssm_log_dt, m_ssm_a_re, m_ssm_a_im, m_ssm_b_re, m_ssm_b_im, m_ssm_c_re, m_ssm_c_im, m_ssm_d, m_w_glu, m_b_glu, m_w_att_up, m_w_mix_out, m_b_mix_out, m_ln1_g, m_ln1_b, m_w_xq, m_w_xkv, m_w_xo, m_ln2_g, m_ln2_b, m_w_ff1, m_b_ff1, m_w_ff2, m_b_ff2, m_ln3_g, m_ln3_b, v_ln_in_g, v_ln_in_b, v_w_in, v_b_in, v_ssm_log_dt, v_ssm_a_re, v_ssm_a_im, v_ssm_b_re, v_ssm_b_im, v_ssm_c_re, v_ssm_c_im, v_ssm_d, v_w_glu, v_b_glu, v_w_att_up, v_w_mix_out, v_b_mix_out, v_ln1_g, v_ln1_b, v_w_xq, v_w_xkv, v_w_xo, v_ln2_g, v_ln2_b, v_w_ff1, v_b_ff1, v_w_ff2, v_b_ff2, v_ln3_g, v_ln3_b):
    given = dict(locals())
    w_arg = {n: given[n] for n in WEIGHTS}
    m_arg = {n: given["m_" + n] for n in WEIGHTS}
    v_arg = {n: given["v_" + n] for n in WEIGHTS}

    in_near, token = _exchange_start([w_arg["w_in"][0].astype(MXU_DTYPE)], scatter=False, peers=NEAR_PEERS,
                                     name="gather_start_in_near")
    in_far, token = _exchange_start(in_near["srcs"], scatter=False, peers=FAR_PEERS, lands=in_near["lands"],
                                    after=token, name="gather_start_in_far")
    w_in_state = [in_far["srcs"], in_far["lands"]]
    token, w_arg, m_arg, v_arg = lax.optimization_barrier((token, w_arg, m_arg, v_arg))
    shards = {n: w_arg[n][0].astype(MXU_DTYPE) for n in BIG if n != "w_in"}
    gathers = []
    for i, names in enumerate(GATHER_GROUPS):
        handle, token = _exchange_start([shards[n] for n in names], scatter=False, after=token, name=f"gather_start_{i}")
        gathers.append(handle)

    small_views = {n: _small_view(n, w_arg[n].shape) for n in SMALL}
    small_w, small_m, small_v = [{n: d[n].reshape(small_views[n]) for n in SMALL} for d in (w_arg, m_arg, v_arg)]
    relaid = [d[n] for d in (small_w, small_m, small_v) for n in SMALL_FLAT_SSM]

    def fetch_in(part, after):
        handle, peers, tag = ((in_near, (0,) + NEAR_PEERS, "near"), (in_far, FAR_PEERS, "far"))[part]
        w_in_state[:] = _exchange_wait(handle, after=after + (relaid if part == 0 else []), srcs=w_in_state[0],
                                       lands=w_in_state[1], name="gather_wait_in_" + tag)
        return w_in_state[1][0], jnp.stack([_peer_index(kk) for kk in peers]).astype(jnp.int32)

    def fetch(i, after):
        _, lands = _exchange_wait(gathers[i], after=after, name=f"gather_wait_{i}")
        full = dict(zip(GATHER_GROUPS[i], lands))
        return {n: t if n in BIG_COL_SHARDED else t.reshape(-1, t.shape[-1]) for n, t in full.items()}

    scatters = {}

    def send(i, gw):
        slots = [gw[n] if n in BIG_COL_SHARDED else gw[n].reshape(N_DEV, -1, gw[n].shape[-1]) for n in SCATTER_GROUPS[i]]
        handle, sent = _exchange_start(slots, scatter=True, name=f"scatter_start_{i}")
        scatters[i] = (handle, slots)
        return sent

    sm = {}
    for n in SMALL:
        t = w_arg[n]
        if n.startswith("ssm_") and n not in ("ssm_d", "ssm_log_dt"):
            sm[n] = t[0]
        else:
            sm[n] = t.reshape(1, -1)

    smalls = []

    def send_small(gs, names):
        handle, sent = _exchange_start([_pack_small(gs, names, small_views)], scatter=False,
                                       name=f"small_start_{len(smalls)}")
        smalls.append(handle)
        return sent

    loss_row, grad_x, gs = _local_grads(x[0], mem[0], positions.reshape(-1, 1), loss_target[0], sm, fetch_in, fetch,
                                        send, send_small, token)
    loss = lax.psum(loss_row[0, 0], ("x", "y", "c"))
    send_small(gs, SMALL_LATE)

    results = [{}, {}, {}, {}]
    done = grad_x
    for i, names in enumerate(SCATTER_GROUPS):
        handle, slots = scatters[i]
        _, lands = _exchange_wait(handle, after=[done], name=f"scatter_wait_{i}")
        for n, land, slot in zip(names, lands, slots):
            res = _reduce_adamw(land, w_arg[n][0], m_arg[n][0], v_arg[n][0], name="adamw_" + n)
            done = res[0]
            for d, r in zip(results, res):
                d[n] = r[None]
    stacks = [_exchange_wait(handle, after=[done], name=f"small_wait_{i}")[1][0] for i, handle in enumerate(smalls)]
    res = _adamw_small(stacks, small_views, small_w, small_m, small_v, name="adamw_small")
    for d, r in zip(results, res):
        d.update({n: r[n].reshape(w_arg[n].shape) for n in SMALL})
    out = [loss, grad_x[None]]
    for d in results:
        out += [d[n] for n in WEIGHTS]
    return tuple(out)
```
